```python
import jax, jax.numpy as jnp
from jax import lax
import numpy as np

D_MODEL = 2048
BATCH = 8
SEQ = 4096
DEPTH = 1

MIX_WIDTH = D_MODEL
CONV_WIDTH = MIX_WIDTH // 2
CONV_GROUPS = 8
CONV_K = 3
DN_HEADS = 8
DN_HEAD_DIM = 128
DN_WIDTH = DN_HEADS * DN_HEAD_DIM
DN_CONV_K = 4
CHUNK = 64
D_FF = 5632
FFN_CONV_K = 3
PLE_DIM = 256
EPS = 1e-6
IN_COLS = 3 * CONV_WIDTH + 4 * DN_WIDTH + 2 * DN_HEADS

kernel_name = "hybrid_shortconv_gated_deltanet_convffn_ple"


def rmsnorm(x, g):
    xf = x.astype(jnp.float32)
    y = xf * lax.rsqrt(jnp.mean(xf * xf, axis=-1, keepdims=True) + EPS) * g.astype(jnp.float32)
    return y.astype(x.dtype)


def causal_dwconv(x, w):
    K = w.shape[0]
    S = x.shape[1]
    xp = jnp.pad(x, ((0, 0), (K - 1, 0), (0, 0)))
    y = xp[:, 0:S] * w[0]
    for j in range(1, K):
        y = y + xp[:, j:j + S] * w[j]
    return y


def l2norm(x):
    return x * lax.rsqrt(jnp.sum(x * x, axis=-1, keepdims=True) + EPS)


def chunk_gated_delta(q, k, v, g, beta):
    B, H, S, dk = q.shape
    dv = v.shape[-1]
    N = S // CHUNK
    q = q * (dk ** -0.5)
    qc = q.reshape(B, H, N, CHUNK, dk)
    kc = k.reshape(B, H, N, CHUNK, dk)
    vc = v.reshape(B, H, N, CHUNK, dv)
    bc = beta.reshape(B, H, N, CHUNK)
    gcum = jnp.cumsum(g.reshape(B, H, N, CHUNK), axis=-1)
    idx = jnp.arange(CHUNK)
    causal = idx[:, None] >= idx[None, :]
    strict = idx[:, None] > idx[None, :]
    diff = gcum[..., :, None] - gcum[..., None, :]
    decay = jnp.exp(jnp.where(causal, diff, -jnp.inf))
    kk = jnp.einsum('bhncd,bhnmd->bhncm', kc, kc)
    L = jnp.where(strict, kk * decay * bc[..., :, None], 0.0)
    A = L + jnp.eye(CHUNK, dtype=jnp.float32)
    rhs = jnp.concatenate([vc * bc[..., None],
                           kc * (bc * jnp.exp(gcum))[..., None]], axis=-1)
    sol = lax.linalg.triangular_solve(A, rhs, left_side=True, lower=True)
    u = sol[..., :dv]
    w = sol[..., dv:]
    qk = jnp.einsum('bhncd,bhnmd->bhncm', qc, kc) * decay
    q_dec = qc * jnp.exp(gcum)[..., None]
    k_dec = kc * jnp.exp(gcum[..., -1:] - gcum)[..., None]
    g_last = jnp.exp(gcum[..., -1])

    def step(state, inp):
        u_n, w_n, qk_n, qd_n, kd_n, gl_n = inp
        v_new = u_n - jnp.einsum('bhcd,bhde->bhce', w_n, state)
        o = (jnp.einsum('bhcd,bhde->bhce', qd_n, state)
             + jnp.einsum('bhcm,bhme->bhce', qk_n, v_new))
        state = state * gl_n[..., None, None] + jnp.einsum('bhcd,bhce->bhde', kd_n, v_new)
        return state, o

    to_front = lambda t: jnp.moveaxis(t, 2, 0)
    xs = (to_front(u), to_front(w), to_front(qk), to_front(q_dec), to_front(k_dec),
          jnp.moveaxis(g_last, 2, 0))
    s0 = jnp.zeros((B, H, dk, dv), jnp.float32)
    _, o = lax.scan(step, s0, xs)
    return jnp.moveaxis(o, 0, 2).reshape(B, H, S, dv)


def hybrid_layer(x, p_i, norm_mix_g, w_in, conv_a_w, conv_qkv_w, a_log, dt_bias, dn_norm_g,
                 w_out, norm_ffn_g, w_up, conv_ffn_w, w_down, norm_ple_g, w_ple_gate, w_ple_proj):
    Bsz, S, _ = x.shape
    h = rmsnorm(x, norm_mix_g)
    proj = h @ w_in
    s1 = CONV_WIDTH
    s2 = 2 * CONV_WIDTH
    s3 = 3 * CONV_WIDTH
    s4 = s3 + 3 * DN_WIDTH
    s5 = s4 + DN_WIDTH
    s6 = s5 + DN_HEADS
    a_x, a_b, a_c, qkv, z, a_dec, b_beta = jnp.split(proj, [s1, s2, s3, s4, s5, s6], axis=-1)

    y_a = a_b * causal_dwconv(a_c * a_x, conv_a_w)

    qkv = jax.nn.silu(causal_dwconv(qkv, conv_qkv_w)).astype(jnp.float32)
    q, k, v = jnp.split(qkv, 3, axis=-1)
    q = l2norm(q.reshape(Bsz, S, DN_HEADS, DN_HEAD_DIM))
    k = l2norm(k.reshape(Bsz, S, DN_HEADS, DN_HEAD_DIM))
    v = v.reshape(Bsz, S, DN_HEADS, DN_HEAD_DIM)
    g = -jnp.exp(a_log.astype(jnp.float32)) * jax.nn.softplus(
        a_dec.astype(jnp.float32) + dt_bias.astype(jnp.float32))
    beta = jax.nn.sigmoid(b_beta.astype(jnp.float32))
    tr = lambda t: jnp.swapaxes(t, 1, 2)
    o = chunk_gated_delta(tr(q), tr(k), tr(v), tr(g), tr(beta))
    o = tr(o)
    zf = z.astype(jnp.float32).reshape(Bsz, S, DN_HEADS, DN_HEAD_DIM)
    o = (o * lax.rsqrt(jnp.mean(o * o, axis=-1, keepdims=True) + EPS)
         * dn_norm_g.astype(jnp.float32) * jax.nn.silu(zf))
    y_b = o.reshape(Bsz, S, DN_WIDTH).astype(x.dtype)

    x = x + jnp.concatenate([y_a, y_b], axis=-1) @ w_out

    h = rmsnorm(x, norm_ffn_g)
    up = causal_dwconv(h @ w_up, conv_ffn_w)
    gate, val = jnp.split(up, 2, axis=-1)
    x = x + (jax.nn.silu(gate) * val) @ w_down

    ple_gate = jax.nn.sigmoid(rmsnorm(x, norm_ple_g) @ w_ple_gate)
    x = x + ple_gate * (p_i @ w_ple_proj)
    return x


def _fwd_setup_inputs(seed: int = 0) -> dict:
    key = jax.random.key(seed)
    ks = jax.random.split(key, 20)
    f32 = jnp.float32
    nrm = lambda k, shape, scale: jax.random.normal(k, shape, f32) * scale
    gain = lambda k, shape: 1.0 + 0.02 * jax.random.normal(k, shape, f32)
    return {
        "x": jax.random.normal(ks[0], (BATCH, SEQ, D_MODEL), f32),
        "p": jax.random.normal(ks[1], (DEPTH, BATCH, SEQ, PLE_DIM), f32),
        "norm_mix_g": gain(ks[2], (DEPTH, D_MODEL)),
        "w_in": nrm(ks[3], (DEPTH, D_MODEL, IN_COLS), D_MODEL ** -0.5),
        "conv_a_w": nrm(ks[4], (DEPTH, CONV_K, CONV_WIDTH), CONV_K ** -0.5),
        "conv_qkv_w": nrm(ks[5], (DEPTH, DN_CONV_K, 3 * DN_WIDTH), DN_CONV_K ** -0.5),
        "a_log": jnp.log(jax.random.uniform(ks[6], (DEPTH, DN_HEADS), f32, 1.0, 16.0)),
        "dt_bias": 0.1 * jax.random.normal(ks[7], (DEPTH, DN_HEADS), f32),
        "dn_norm_g": gain(ks[8], (DEPTH, DN_HEAD_DIM)),
        "w_out": nrm(ks[9], (DEPTH, MIX_WIDTH, D_MODEL), MIX_WIDTH ** -0.5),
        "norm_ffn_g": gain(ks[10], (DEPTH, D_MODEL)),
        "w_up": nrm(ks[11], (DEPTH, D_MODEL, 2 * D_FF), D_MODEL ** -0.5),
        "conv_ffn_w": nrm(ks[12], (DEPTH, FFN_CONV_K, 2 * D_FF), FFN_CONV_K ** -0.5),
        "w_down": nrm(ks[13], (DEPTH, D_FF, D_MODEL), D_FF ** -0.5),
        "norm_ple_g": gain(ks[14], (DEPTH, D_MODEL)),
        "w_ple_gate": nrm(ks[15], (DEPTH, D_MODEL, D_MODEL), D_MODEL ** -0.5),
        "w_ple_proj": nrm(ks[16], (DEPTH, PLE_DIM, D_MODEL), PLE_DIM ** -0.5),
        "final_norm_g": gain(ks[17], (D_MODEL,)),
    }


def _fwd_reference(x, p, norm_mix_g, w_in, conv_a_w, conv_qkv_w, a_log, dt_bias, dn_norm_g,
              w_out, norm_ffn_g, w_up, conv_ffn_w, w_down, norm_ple_g, w_ple_gate,
              w_ple_proj, final_norm_g):
    for i in range(DEPTH):
        x = hybrid_layer(x, p[i], norm_mix_g[i], w_in[i], conv_a_w[i], conv_qkv_w[i],
                         a_log[i], dt_bias[i], dn_norm_g[i], w_out[i], norm_ffn_g[i],
                         w_up[i], conv_ffn_w[i], w_down[i], norm_ple_g[i], w_ple_gate[i],
                         w_ple_proj[i])
    return rmsnorm(x, final_norm_g)


import jax as _jax
import jax.numpy as _jnp

TWIN_FORMAT = 'train_step'
FWD_PARAMS = ['x', 'p', 'norm_mix_g', 'w_in', 'conv_a_w', 'conv_qkv_w', 'a_log', 'dt_bias', 'dn_norm_g', 'w_out', 'norm_ffn_g', 'w_up', 'conv_ffn_w', 'w_down', 'norm_ple_g', 'w_ple_gate', 'w_ple_proj', 'final_norm_g']
TWIN_WEIGHTS = ['norm_mix_g', 'w_in', 'conv_a_w', 'conv_qkv_w', 'a_log', 'dt_bias', 'dn_norm_g', 'w_out', 'norm_ffn_g', 'w_up', 'conv_ffn_w', 'w_down', 'norm_ple_g', 'w_ple_gate', 'w_ple_proj', 'final_norm_g']
TWIN_DIFF_INPUT = 'x'
TWIN_INPUTS = ['x', 'p', 'norm_mix_g', 'w_in', 'conv_a_w', 'conv_qkv_w', 'a_log', 'dt_bias', 'dn_norm_g', 'w_out', 'norm_ffn_g', 'w_up', 'conv_ffn_w', 'w_down', 'norm_ple_g', 'w_ple_gate', 'w_ple_proj', 'final_norm_g', 'loss_target', 'm_norm_mix_g', 'm_w_in', 'm_conv_a_w', 'm_conv_qkv_w', 'm_a_log', 'm_dt_bias', 'm_dn_norm_g', 'm_w_out', 'm_norm_ffn_g', 'm_w_up', 'm_conv_ffn_w', 'm_w_down', 'm_norm_ple_g', 'm_w_ple_gate', 'm_w_ple_proj', 'm_final_norm_g', 'v_norm_mix_g', 'v_w_in', 'v_conv_a_w', 'v_conv_qkv_w', 'v_a_log', 'v_dt_bias', 'v_dn_norm_g', 'v_w_out', 'v_norm_ffn_g', 'v_w_up', 'v_conv_ffn_w', 'v_w_down', 'v_norm_ple_g', 'v_w_ple_gate', 'v_w_ple_proj', 'v_final_norm_g']
TWIN_OUTPUTS = ['loss', 'grad_x', 'grad_norm_mix_g', 'grad_w_in', 'grad_conv_a_w', 'grad_conv_qkv_w', 'grad_a_log', 'grad_dt_bias', 'grad_dn_norm_g', 'grad_w_out', 'grad_norm_ffn_g', 'grad_w_up', 'grad_conv_ffn_w', 'grad_w_down', 'grad_norm_ple_g', 'grad_w_ple_gate', 'grad_w_ple_proj', 'grad_final_norm_g', 'delta_norm_mix_g', 'delta_w_in', 'delta_conv_a_w', 'delta_conv_qkv_w', 'delta_a_log', 'delta_dt_bias', 'delta_dn_norm_g', 'delta_w_out', 'delta_norm_ffn_g', 'delta_w_up', 'delta_conv_ffn_w', 'delta_w_down', 'delta_norm_ple_g', 'delta_w_ple_gate', 'delta_w_ple_proj', 'delta_final_norm_g', 'new_m_norm_mix_g', 'new_m_w_in', 'new_m_conv_a_w', 'new_m_conv_qkv_w', 'new_m_a_log', 'new_m_dt_bias', 'new_m_dn_norm_g', 'new_m_w_out', 'new_m_norm_ffn_g', 'new_m_w_up', 'new_m_conv_ffn_w', 'new_m_w_down', 'new_m_norm_ple_g', 'new_m_w_ple_gate', 'new_m_w_ple_proj', 'new_m_final_norm_g', 'new_v_norm_mix_g', 'new_v_w_in', 'new_v_conv_a_w', 'new_v_conv_qkv_w', 'new_v_a_log', 'new_v_dt_bias', 'new_v_dn_norm_g', 'new_v_w_out', 'new_v_norm_ffn_g', 'new_v_w_up', 'new_v_conv_ffn_w', 'new_v_w_down', 'new_v_norm_ple_g', 'new_v_w_ple_gate', 'new_v_w_ple_proj', 'new_v_final_norm_g']
TWIN_LEAF_KINDS = {'loss': 'loss', 'grad_x': 'grad_x', 'grad_norm_mix_g': 'grad_w', 'grad_w_in': 'grad_w', 'grad_conv_a_w': 'grad_w', 'grad_conv_qkv_w': 'grad_w', 'grad_a_log': 'grad_w', 'grad_dt_bias': 'grad_w', 'grad_dn_norm_g': 'grad_w', 'grad_w_out': 'grad_w', 'grad_norm_ffn_g': 'grad_w', 'grad_w_up': 'grad_w', 'grad_conv_ffn_w': 'grad_w', 'grad_w_down': 'grad_w', 'grad_norm_ple_g': 'grad_w', 'grad_w_ple_gate': 'grad_w', 'grad_w_ple_proj': 'grad_w', 'grad_final_norm_g': 'grad_w', 'delta_norm_mix_g': 'delta_w', 'delta_w_in': 'delta_w', 'delta_conv_a_w': 'delta_w', 'delta_conv_qkv_w': 'delta_w', 'delta_a_log': 'delta_w', 'delta_dt_bias': 'delta_w', 'delta_dn_norm_g': 'delta_w', 'delta_w_out': 'delta_w', 'delta_norm_ffn_g': 'delta_w', 'delta_w_up': 'delta_w', 'delta_conv_ffn_w': 'delta_w', 'delta_w_down': 'delta_w', 'delta_norm_ple_g': 'delta_w', 'delta_w_ple_gate': 'delta_w', 'delta_w_ple_proj': 'delta_w', 'delta_final_norm_g': 'delta_w', 'new_m_norm_mix_g': 'new_m', 'new_m_w_in': 'new_m', 'new_m_conv_a_w': 'new_m', 'new_m_conv_qkv_w': 'new_m', 'new_m_a_log': 'new_m', 'new_m_dt_bias': 'new_m', 'new_m_dn_norm_g': 'new_m', 'new_m_w_out': 'new_m', 'new_m_norm_ffn_g': 'new_m', 'new_m_w_up': 'new_m', 'new_m_conv_ffn_w': 'new_m', 'new_m_w_down': 'new_m', 'new_m_norm_ple_g': 'new_m', 'new_m_w_ple_gate': 'new_m', 'new_m_w_ple_proj': 'new_m', 'new_m_final_norm_g': 'new_m', 'new_v_norm_mix_g': 'new_v', 'new_v_w_in': 'new_v', 'new_v_conv_a_w': 'new_v', 'new_v_conv_qkv_w': 'new_v', 'new_v_a_log': 'new_v', 'new_v_dt_bias': 'new_v', 'new_v_dn_norm_g': 'new_v', 'new_v_w_out': 'new_v', 'new_v_norm_ffn_g': 'new_v', 'new_v_w_up': 'new_v', 'new_v_conv_ffn_w': 'new_v', 'new_v_w_down': 'new_v', 'new_v_norm_ple_g': 'new_v', 'new_v_w_ple_gate': 'new_v', 'new_v_w_ple_proj': 'new_v', 'new_v_final_norm_g': 'new_v'}


def _forward(args):
    return _fwd_reference(*[args[k] for k in FWD_PARAMS])


def _output_shape():
    def fwd():
        inp = _fwd_setup_inputs(0)
        return _fwd_reference(*[inp[k] for k in FWD_PARAMS])
    out = _jax.eval_shape(fwd)
    return out.shape, out.dtype

N_MICROBATCH = 1
ADAM_LR = 0.001
ADAM_B1 = 0.9
ADAM_B2 = 0.999
ADAM_EPS = 1e-08
ADAM_WD = 0.01
ADAM_STEP = 10
PER_EXAMPLE_BATCH_AXIS = {'x': 0, 'p': 1, 'loss_target': 0}
SHARED_INPUTS = []
_WEIGHT_DTYPES = {'norm_mix_g': _jnp.float32, 'w_in': _jnp.float32, 'conv_a_w': _jnp.float32, 'conv_qkv_w': _jnp.float32, 'a_log': _jnp.float32, 'dt_bias': _jnp.float32, 'dn_norm_g': _jnp.float32, 'w_out': _jnp.float32, 'norm_ffn_g': _jnp.float32, 'w_up': _jnp.float32, 'conv_ffn_w': _jnp.float32, 'w_down': _jnp.float32, 'norm_ple_g': _jnp.float32, 'w_ple_gate': _jnp.float32, 'w_ple_proj': _jnp.float32, 'final_norm_g': _jnp.float32}
MOMENT_SCALE = {'norm_mix_g': 9.960312e-02, 'w_in': 5.363591e-02, 'conv_a_w': 7.230514e-02, 'conv_qkv_w': 3.229897e-02, 'a_log': 5.005526e-02, 'dt_bias': 4.143356e-02, 'dn_norm_g': 1.075189e-01, 'w_out': 5.717252e-02, 'norm_ffn_g': 5.332497e-02, 'w_up': 2.236457e-02, 'conv_ffn_w': 2.240560e-02, 'w_down': 3.642891e-02, 'norm_ple_g': 1.233636e-02, 'w_ple_gate': 1.237010e-02, 'w_ple_proj': 3.180796e-02, 'final_norm_g': 1.599043e+01}


def _to_microbatches(a, axis):
    t = _jnp.moveaxis(a, axis, 0)
    t = t.reshape((N_MICROBATCH, t.shape[0] // N_MICROBATCH) + t.shape[1:])
    return _jnp.moveaxis(t, 1, axis + 1)


def setup_inputs(seed: int = 0) -> dict:
    inp = _fwd_setup_inputs(seed)
    key = _jax.random.fold_in(_jax.random.key(seed), 7919)
    shape, _ = _output_shape()
    out = dict(inp)
    out["loss_target"] = _jax.random.normal(_jax.random.fold_in(key, 0), shape, _jnp.float32)
    for i, name in enumerate(TWIN_WEIGHTS):
        w = inp[name].astype(_jnp.float32)
        if MOMENT_SCALE is None:
            s = _jnp.sqrt(_jnp.mean(_jnp.square(w)) + 1e-30)
        else:
            s = MOMENT_SCALE[name]
        km, kv = _jax.random.split(_jax.random.fold_in(key, i + 1))
        out[name] = w
        out["m_" + name] = s * _jax.random.normal(km, w.shape, _jnp.float32)
        out["v_" + name] = (s * s) * _jax.random.uniform(kv, w.shape, _jnp.float32, 0.5, 1.5)
    if N_MICROBATCH > 1:
        for name, axis in PER_EXAMPLE_BATCH_AXIS.items():
            out[name] = _to_microbatches(out[name], axis)
    return {'x': out['x'], 'p': out['p'], 'norm_mix_g': out['norm_mix_g'], 'w_in': out['w_in'], 'conv_a_w': out['conv_a_w'], 'conv_qkv_w': out['conv_qkv_w'], 'a_log': out['a_log'], 'dt_bias': out['dt_bias'], 'dn_norm_g': out['dn_norm_g'], 'w_out': out['w_out'], 'norm_ffn_g': out['norm_ffn_g'], 'w_up': out['w_up'], 'conv_ffn_w': out['conv_ffn_w'], 'w_down': out['w_down'], 'norm_ple_g': out['norm_ple_g'], 'w_ple_gate': out['w_ple_gate'], 'w_ple_proj': out['w_ple_proj'], 'final_norm_g': out['final_norm_g'], 'loss_target': out['loss_target'], 'm_norm_mix_g': out['m_norm_mix_g'], 'm_w_in': out['m_w_in'], 'm_conv_a_w': out['m_conv_a_w'], 'm_conv_qkv_w': out['m_conv_qkv_w'], 'm_a_log': out['m_a_log'], 'm_dt_bias': out['m_dt_bias'], 'm_dn_norm_g': out['m_dn_norm_g'], 'm_w_out': out['m_w_out'], 'm_norm_ffn_g': out['m_norm_ffn_g'], 'm_w_up': out['m_w_up'], 'm_conv_ffn_w': out['m_conv_ffn_w'], 'm_w_down': out['m_w_down'], 'm_norm_ple_g': out['m_norm_ple_g'], 'm_w_ple_gate': out['m_w_ple_gate'], 'm_w_ple_proj': out['m_w_ple_proj'], 'm_final_norm_g': out['m_final_norm_g'], 'v_norm_mix_g': out['v_norm_mix_g'], 'v_w_in': out['v_w_in'], 'v_conv_a_w': out['v_conv_a_w'], 'v_conv_qkv_w': out['v_conv_qkv_w'], 'v_a_log': out['v_a_log'], 'v_dt_bias': out['v_dt_bias'], 'v_dn_norm_g': out['v_dn_norm_g'], 'v_w_out': out['v_w_out'], 'v_norm_ffn_g': out['v_norm_ffn_g'], 'v_w_up': out['v_w_up'], 'v_conv_ffn_w': out['v_conv_ffn_w'], 'v_w_down': out['v_w_down'], 'v_norm_ple_g': out['v_norm_ple_g'], 'v_w_ple_gate': out['v_w_ple_gate'], 'v_w_ple_proj': out['v_w_ple_proj'], 'v_final_norm_g': out['v_final_norm_g']}


def _loss(weights, diff, rest, loss_target):
    with _jax.named_scope("forward"):
        args = {**rest, TWIN_DIFF_INPUT: diff, **{k: w.astype(_WEIGHT_DTYPES[k]) for k, w in weights.items()}}
        y = _forward(args)
    with _jax.named_scope("loss_head"):
        err = _jnp.square(y.astype(_jnp.float32) - loss_target)
        return 0.5 * _jnp.sum(_jnp.mean(err, axis=-1)) if err.ndim else 0.5 * err


def _adamw(w, g, m, v):
    m = ADAM_B1 * m + (1.0 - ADAM_B1) * g
    v = ADAM_B2 * v + (1.0 - ADAM_B2) * _jnp.square(g)
    m_hat = m / (1.0 - ADAM_B1 ** ADAM_STEP)
    v_hat = v / (1.0 - ADAM_B2 ** ADAM_STEP)
    delta = -ADAM_LR * (m_hat / (_jnp.sqrt(v_hat) + ADAM_EPS) + ADAM_WD * w)
    return delta, m, v


def reference(x, p, norm_mix_g, w_in, conv_a_w, conv_qkv_w, a_log, dt_bias, dn_norm_g, w_out, norm_ffn_g, w_up, conv_ffn_w, w_down, norm_ple_g, w_ple_gate, w_ple_proj, final_norm_g, loss_target, m_norm_mix_g, m_w_in, m_conv_a_w, m_conv_qkv_w, m_a_log, m_dt_bias, m_dn_norm_g, m_w_out, m_norm_ffn_g, m_w_up, m_conv_ffn_w, m_w_down, m_norm_ple_g, m_w_ple_gate, m_w_ple_proj, m_final_norm_g, v_norm_mix_g, v_w_in, v_conv_a_w, v_conv_qkv_w, v_a_log, v_dt_bias, v_dn_norm_g, v_w_out, v_norm_ffn_g, v_w_up, v_conv_ffn_w, v_w_down, v_norm_ple_g, v_w_ple_gate, v_w_ple_proj, v_final_norm_g):
    given = dict(x=x, p=p, norm_mix_g=norm_mix_g, w_in=w_in, conv_a_w=conv_a_w, conv_qkv_w=conv_qkv_w, a_log=a_log, dt_bias=dt_bias, dn_norm_g=dn_norm_g, w_out=w_out, norm_ffn_g=norm_ffn_g, w_up=w_up, conv_ffn_w=conv_ffn_w, w_down=w_down, norm_ple_g=norm_ple_g, w_ple_gate=w_ple_gate, w_ple_proj=w_ple_proj, final_norm_g=final_norm_g, loss_target=loss_target, m_norm_mix_g=m_norm_mix_g, m_w_in=m_w_in, m_conv_a_w=m_conv_a_w, m_conv_qkv_w=m_conv_qkv_w, m_a_log=m_a_log, m_dt_bias=m_dt_bias, m_dn_norm_g=m_dn_norm_g, m_w_out=m_w_out, m_norm_ffn_g=m_norm_ffn_g, m_w_up=m_w_up, m_conv_ffn_w=m_conv_ffn_w, m_w_down=m_w_down, m_norm_ple_g=m_norm_ple_g, m_w_ple_gate=m_w_ple_gate, m_w_ple_proj=m_w_ple_proj, m_final_norm_g=m_final_norm_g, v_norm_mix_g=v_norm_mix_g, v_w_in=v_w_in, v_conv_a_w=v_conv_a_w, v_conv_qkv_w=v_conv_qkv_w, v_a_log=v_a_log, v_dt_bias=v_dt_bias, v_dn_norm_g=v_dn_norm_g, v_w_out=v_w_out, v_norm_ffn_g=v_norm_ffn_g, v_w_up=v_w_up, v_conv_ffn_w=v_conv_ffn_w, v_w_down=v_w_down, v_norm_ple_g=v_norm_ple_g, v_w_ple_gate=v_w_ple_gate, v_w_ple_proj=v_w_ple_proj, v_final_norm_g=v_final_norm_g)
    weights = {n: given[n] for n in TWIN_WEIGHTS}
    shared = {n: given[n] for n in SHARED_INPUTS}
    per_example = {n: given[n] for n in ['x', 'p']}
    grad_fn = _jax.value_and_grad(_loss, argnums=(0, 1))

    def one_microbatch(ex, loss_target):
        ex = dict(ex)
        diff = ex.pop(TWIN_DIFF_INPUT)
        return grad_fn(weights, diff, {**shared, **ex}, loss_target)

    if N_MICROBATCH == 1:
        loss, (grad_w, grad_x) = one_microbatch(per_example, given["loss_target"])
    else:
        def body(carry, xs):
            loss_sum, grad_sum = carry
            l_k, (gw_k, gx_k) = one_microbatch(xs[0], xs[1])
            with _jax.named_scope("update"):
                return (loss_sum + l_k, _jax.tree.map(_jnp.add, grad_sum, gw_k)), gx_k

        init = (_jnp.zeros((), _jnp.float32), _jax.tree.map(_jnp.zeros_like, weights))
        (loss, grad_w), grad_x = _jax.lax.scan(body, init, (per_example, given["loss_target"]))
    with _jax.named_scope("update"):
        delta_w, new_m, new_v = {}, {}, {}
        for n in TWIN_WEIGHTS:
            delta_w[n], new_m[n], new_v[n] = _adamw(weights[n], grad_w[n], given["m_" + n], given["v_" + n])
    return (loss, grad_x, *[grad_w[n] for n in TWIN_WEIGHTS], *[delta_w[n] for n in TWIN_WEIGHTS],
            *[new_m[n] for n in TWIN_WEIGHTS], *[new_v[n] for n in TWIN_WEIGHTS])
```

```python
import functools

import jax
import jax.numpy as jnp
from jax import lax
from jax.experimental import pallas as pl
from jax.experimental.pallas import tpu as pltpu

F32 = jnp.float32
BF16 = jnp.bfloat16
HI = lax.Precision.HIGHEST

EPS = 1e-6
CHUNK = 64
HEAD = 128
LANE = 128
N_DEV = 8
AB_PAD = 512

ADAM_LR = 0.001
ADAM_B1 = 0.9
ADAM_B2 = 0.999
ADAM_EPS = 1e-08
ADAM_WD = 0.01
ADAM_STEP = 10

MESH = pl.DeviceIdType.MESH


def _tile(dim, target, align=LANE):
    if dim <= target:
        return dim
    t = (target // align) * align
    while t > align and dim % t:
        t -= align
    assert dim % t == 0, (dim, target)
    return t


def _params(sem, vmem_mb=48):
    return pltpu.CompilerParams(dimension_semantics=sem, vmem_limit_bytes=vmem_mb << 20)


_DN = {"nn": (((1,), (0,)), ((), ())), "nt": (((1,), (1,)), ((), ())), "tn": (((0,), (0,)), ((), ()))}


def _matmul(a, b, mode, *, name, out_dtypes=(F32,), epilogue=None, extras=(), tm=1024, tn=1024, tk=512):
    if mode == "nn":
        (M, K), (K2, N) = a.shape, b.shape
    elif mode == "nt":
        (M, K), (N, K2) = a.shape, b.shape
    else:
        (K, M), (K2, N) = a.shape, b.shape
    assert K == K2, (name, a.shape, b.shape)
    tm, tn, tk = _tile(M, tm), _tile(N, tn), _tile(K, tk)
    nk = K // tk
    n_ex, n_out = len(extras), len(out_dtypes)
    dn = _DN[mode]

    def body(a_ref, b_ref, *rest):
        ex_refs, out_refs, acc = rest[:n_ex], rest[n_ex:n_ex + n_out], rest[-1]
        k = pl.program_id(2)

        @pl.when(k == 0)
        def _():
            acc[...] = jnp.zeros_like(acc)

        acc[...] += lax.dot_general(a_ref[...].astype(BF16), b_ref[...].astype(BF16), dn,
                                    preferred_element_type=F32)

        @pl.when(k == nk - 1)
        def _():
            res = acc[...]
            outs = (res,) if epilogue is None else epilogue(res, *[e[...] for e in ex_refs])
            for o_ref, val in zip(out_refs, outs):
                o_ref[...] = val.astype(o_ref.dtype)

    a_spec = pl.BlockSpec((tk, tm), lambda i, j, k: (k, i)) if mode == "tn" else pl.BlockSpec((tm, tk), lambda i, j, k: (i, k))
    b_spec = pl.BlockSpec((tn, tk), lambda i, j, k: (j, k)) if mode == "nt" else pl.BlockSpec((tk, tn), lambda i, j, k: (k, j))
    mn_spec = pl.BlockSpec((tm, tn), lambda i, j, k: (i, j))
    outs = pl.pallas_call(
        body, name=name, grid=(M // tm, N // tn, nk),
        in_specs=[a_spec, b_spec] + [mn_spec] * n_ex,
        out_specs=[mn_spec] * n_out,
        out_shape=[jax.ShapeDtypeStruct((M, N), dt) for dt in out_dtypes],
        scratch_shapes=[pltpu.VMEM((tm, tn), F32)],
        compiler_params=_params(("parallel", "parallel", "arbitrary"), 56),
    )(a, b, *extras)
    return outs[0] if n_out == 1 else outs


def _rms_fwd(x, g, *, name):
    T, D = x.shape
    tr = _tile(T, 256, 8)

    def body(x_ref, g_ref, h_ref):
        xv = x_ref[...]
        r = lax.rsqrt(jnp.mean(xv * xv, axis=-1, keepdims=True) + EPS)
        h_ref[...] = (xv * r * g_ref[...]).astype(h_ref.dtype)

    return pl.pallas_call(
        body, name=name, grid=(T // tr,),
        in_specs=[pl.BlockSpec((tr, D), lambda i: (i, 0)), pl.BlockSpec((1, D), lambda i: (0, 0))],
        out_specs=pl.BlockSpec((tr, D), lambda i: (i, 0)),
        out_shape=jax.ShapeDtypeStruct((T, D), BF16),
        compiler_params=_params(("parallel",)),
    )(x, g)


def _rms_bwd(x, g, dh, dres, *, name):
    T, D = x.shape
    tr = _tile(T, 256, 8)

    def body(x_ref, g_ref, dh_ref, dres_ref, dx_ref, dg_ref):
        xv = x_ref[...]
        r = lax.rsqrt(jnp.mean(xv * xv, axis=-1, keepdims=True) + EPS)
        xh = xv * r
        dh = dh_ref[...]

        @pl.when(pl.program_id(0) == 0)
        def _():
            dg_ref[...] = jnp.zeros_like(dg_ref)

        dg_ref[...] += jnp.sum(dh * xh, axis=0, keepdims=True)
        dxh = dh * g_ref[...]
        dx_ref[...] = dres_ref[...] + r * (dxh - xh * jnp.mean(dxh * xh, axis=-1, keepdims=True))

    row = pl.BlockSpec((tr, D), lambda i: (i, 0))
    vec = pl.BlockSpec((1, D), lambda i: (0, 0))
    return pl.pallas_call(
        body, name=name, grid=(T // tr,),
        in_specs=[row, vec, row, row], out_specs=[row, vec],
        out_shape=[jax.ShapeDtypeStruct((T, D), F32), jax.ShapeDtypeStruct((1, D), F32)],
        compiler_params=_params(("arbitrary",)),
    )(x, g, dh, dres)


def _final_loss(x, g, tgt, *, name):
    T, D = x.shape
    tr = _tile(T, 256, 8)

    def body(x_ref, g_ref, t_ref, dx_ref, dg_ref, loss_ref):
        xv = x_ref[...]
        r = lax.rsqrt(jnp.mean(xv * xv, axis=-1, keepdims=True) + EPS)
        xh = xv * r
        gv = g_ref[...]
        err = xh * gv - t_ref[...]

        @pl.when(pl.program_id(0) == 0)
        def _():
            dg_ref[...] = jnp.zeros_like(dg_ref)
            loss_ref[...] = jnp.zeros_like(loss_ref)

        part = 0.5 * jnp.sum(jnp.mean(err * err, axis=-1, keepdims=True), axis=0, keepdims=True)
        loss_ref[...] += jnp.broadcast_to(part, loss_ref.shape)
        dy = err * (1.0 / D)
        dg_ref[...] += jnp.sum(dy * xh, axis=0, keepdims=True)
        dxh = dy * gv
        dx_ref[...] = r * (dxh - xh * jnp.mean(dxh * xh, axis=-1, keepdims=True))

    row = pl.BlockSpec((tr, D), lambda i: (i, 0))
    vec = pl.BlockSpec((1, D), lambda i: (0, 0))
    return pl.pallas_call(
        body, name=name, grid=(T // tr,),
        in_specs=[row, vec, row], out_specs=[row, vec, pl.BlockSpec((1, LANE), lambda i: (0, 0))],
        out_shape=[jax.ShapeDtypeStruct((T, D), F32), jax.ShapeDtypeStruct((1, D), F32),
                   jax.ShapeDtypeStruct((1, LANE), F32)],
        compiler_params=_params(("arbitrary",)),
    )(x, g, tgt)


def _ple_bwd(dx3, pp, sg, *, name):
    T, D = dx3.shape
    tr = _tile(T, 256, 8)

    def body(dx_ref, pp_ref, sg_ref, dpg_ref, dpp_ref):
        dx, s = dx_ref[...], sg_ref[...]
        dpg_ref[...] = (dx * pp_ref[...] * s * (1.0 - s)).astype(dpg_ref.dtype)
        dpp_ref[...] = (dx * s).astype(dpp_ref.dtype)

    row = pl.BlockSpec((tr, D), lambda i: (i, 0))
    return pl.pallas_call(
        body, name=name, grid=(T // tr,), in_specs=[row, row, row], out_specs=[row, row],
        out_shape=[jax.ShapeDtypeStruct((T, D), BF16)] * 2, compiler_params=_params(("parallel",)),
    )(dx3, pp, sg)


RC = 64


def _ext(ref, r0, T, before, after):
    parts = []
    if before:
        p0 = pl.multiple_of(jnp.maximum(r0 - 8, 0), 8)
        parts.append(jnp.where(r0 > 0, ref[pl.ds(p0, 8), :], 0.0))
    parts.append(ref[pl.ds(r0, RC), :])
    if after:
        n0 = pl.multiple_of(jnp.minimum(r0 + RC, T - 8), 8)
        parts.append(jnp.where(r0 + RC < T, ref[pl.ds(n0, 8), :], 0.0))
    return parts[0] if len(parts) == 1 else jnp.concatenate(parts, axis=0)


def _down(xx, s):
    return (xx if s == 0 else pltpu.roll(xx, s, 0))[8:, :]


def _up(xx, s, rows):
    return (xx if s == 0 else pltpu.roll(xx, xx.shape[0] - s, 0))[:rows, :]


def _conv_down(xx, w_ref, K):
    y = None
    for j in range(K):
        t = _down(xx, K - 1 - j) * w_ref[j:j + 1, :]
        y = t if y is None else y + t
    return y


def _fold8(x):
    return jnp.sum(x.reshape(x.shape[0] // 8, 8, x.shape[1]), axis=0)


def _silu(x):
    return x * jax.nn.sigmoid(x)


def _dsilu(x):
    s = jax.nn.sigmoid(x)
    return s * (1.0 + x * (1.0 - s))


def _col_specs(T, offs):
    return [pl.BlockSpec((T, LANE), functools.partial(lambda o, j: (0, o + j), o)) for o in offs]


def _group_a_fwd(proj, conv_w, CW, *, name):
    T = proj.shape[0]
    nb = CW // LANE
    K = conv_w.shape[0]

    def body(ax_ref, ab_ref, ac_ref, w_ref, y_ref):
        def step(i, carry):
            r0 = pl.multiple_of(i * RC, RC)
            m = _ext(ac_ref, r0, T, True, False) * _ext(ax_ref, r0, T, True, False)
            y_ref[pl.ds(r0, RC), :] = (ab_ref[pl.ds(r0, RC), :] * _conv_down(m, w_ref, K)).astype(y_ref.dtype)
            return carry
        lax.fori_loop(0, T // RC, step, 0)

    return pl.pallas_call(
        body, name=name, grid=(nb,),
        in_specs=_col_specs(T, (0, nb, 2 * nb)) + [pl.BlockSpec((K, LANE), lambda j: (0, j))],
        out_specs=pl.BlockSpec((T, LANE), lambda j: (0, j)),
        out_shape=jax.ShapeDtypeStruct((T, CW), BF16), compiler_params=_params(("parallel",)),
    )(proj, proj, proj, conv_w)


def _group_a_bwd(proj, conv_w, dycat, CW, *, name):
    T = proj.shape[0]
    nb = CW // LANE
    K = conv_w.shape[0]

    def body(ax_ref, ab_ref, ac_ref, w_ref, dy_ref, dax_ref, dab_ref, dac_ref, dw_ref):
        def step(i, accs):
            r0 = pl.multiple_of(i * RC, RC)
            ax3 = _ext(ax_ref, r0, T, True, True)
            ac3 = _ext(ac_ref, r0, T, True, True)
            m3 = ax3 * ac3
            c = _conv_down(m3[:RC + 8], w_ref, K)
            dy = dy_ref[pl.ds(r0, RC), :]
            dab_ref[pl.ds(r0, RC), :] = (dy * c).astype(dab_ref.dtype)
            dc2 = _ext(dy_ref, r0, T, False, True) * _ext(ab_ref, r0, T, False, True)
            dm = None
            new = []
            for j in range(K):
                s = K - 1 - j
                t = _up(dc2, s, RC) * w_ref[j:j + 1, :]
                dm = t if dm is None else dm + t
                new.append(accs[j] + _fold8(dc2[:RC] * _down(m3[:RC + 8], s)))
            dax_ref[pl.ds(r0, RC), :] = (dm * ac3[8:RC + 8]).astype(dax_ref.dtype)
            dac_ref[pl.ds(r0, RC), :] = (dm * ax3[8:RC + 8]).astype(dac_ref.dtype)
            return tuple(new)

        accs = lax.fori_loop(0, T // RC, step, tuple(jnp.zeros((8, LANE), F32) for _ in range(K)))
        for j in range(K):
            dw_ref[j:j + 1, :] = jnp.sum(accs[j], axis=0, keepdims=True)

    col = pl.BlockSpec((T, LANE), lambda j: (0, j))
    wsp = pl.BlockSpec((K, LANE), lambda j: (0, j))
    return pl.pallas_call(
        body, name=name, grid=(nb,),
        in_specs=_col_specs(T, (0, nb, 2 * nb)) + [wsp, col],
        out_specs=[col, col, col, wsp],
        out_shape=[jax.ShapeDtypeStruct((T, CW), BF16)] * 3 + [jax.ShapeDtypeStruct((K, CW), F32)],
        compiler_params=_params(("parallel",)),
    )(proj, proj, proj, conv_w, dycat)


def _qkv_fwd(proj, conv_w, off, H, *, name):
    T = proj.shape[0]
    nb = 3 * H
    K = conv_w.shape[0]

    def body(x_ref, w_ref, y_ref):
        j = pl.program_id(0)
        is_qk = j < 2 * H
        scale = jnp.where(j < H, HEAD ** -0.5, 1.0).astype(F32)

        def step(i, carry):
            r0 = pl.multiple_of(i * RC, RC)
            s = _silu(_conv_down(_ext(x_ref, r0, T, True, False), w_ref, K))
            r = lax.rsqrt(jnp.sum(s * s, axis=-1, keepdims=True) + EPS) * scale
            y_ref[pl.ds(r0, RC), :] = s * jnp.where(is_qk, r, 1.0)
            return carry
        lax.fori_loop(0, T // RC, step, 0)

    return pl.pallas_call(
        body, name=name, grid=(nb,),
        in_specs=_col_specs(T, (off,)) + [pl.BlockSpec((K, LANE), lambda j: (0, j))],
        out_specs=pl.BlockSpec((T, LANE), lambda j: (0, j)),
        out_shape=jax.ShapeDtypeStruct((T, nb * LANE), F32), compiler_params=_params(("parallel",)),
    )(proj, conv_w)


def _qkv_bwd(proj, conv_w, dq, dk, dv, off, H, *, name):
    T = proj.shape[0]
    nb = 3 * H
    K = conv_w.shape[0]

    def body(x_ref, w_ref, dq_ref, dk_ref, dv_ref, dx_ref, dw_ref):
        j = pl.program_id(0)
        is_qk = j < 2 * H
        scale = jnp.where(j < H, HEAD ** -0.5, 1.0).astype(F32)

        def step(i, accs):
            r0 = pl.multiple_of(i * RC, RC)
            x3 = _ext(x_ref, r0, T, True, True)
            c2 = _conv_down(x3, w_ref, K)
            s2 = _silu(c2)
            dn2 = jnp.where(j < H, _ext(dq_ref, r0, T, False, True),
                            jnp.where(is_qk, _ext(dk_ref, r0, T, False, True), _ext(dv_ref, r0, T, False, True)))
            r = lax.rsqrt(jnp.sum(s2 * s2, axis=-1, keepdims=True) + EPS)
            nh = s2 * r
            dnp = dn2 * scale
            ds_qk = r * (dnp - nh * jnp.sum(dnp * nh, axis=-1, keepdims=True))
            ds2 = jnp.where(is_qk, ds_qk, dn2)
            dc2 = ds2 * _dsilu(c2)
            dx = None
            new = []
            for jj in range(K):
                s = K - 1 - jj
                t = _up(dc2, s, RC) * w_ref[jj:jj + 1, :]
                dx = t if dx is None else dx + t
                new.append(accs[jj] + _fold8(dc2[:RC] * _down(x3[:RC + 8], s)))
            dx_ref[pl.ds(r0, RC), :] = dx.astype(dx_ref.dtype)
            return tuple(new)

        accs = lax.fori_loop(0, T // RC, step, tuple(jnp.zeros((8, LANE), F32) for _ in range(K)))
        for jj in range(K):
            dw_ref[jj:jj + 1, :] = jnp.sum(accs[jj], axis=0, keepdims=True)

    col = pl.BlockSpec((T, LANE), lambda j: (0, j))
    wsp = pl.BlockSpec((K, LANE), lambda j: (0, j))
    return pl.pallas_call(
        body, name=name, grid=(nb,),
        in_specs=_col_specs(T, (off,)) + [wsp] + [
            pl.BlockSpec((T, LANE), functools.partial(lambda o, j: (0, jnp.clip(j - o, 0, H - 1)), o)) for o in (0, H, 2 * H)],
        out_specs=[col, wsp],
        out_shape=[jax.ShapeDtypeStruct((T, nb * LANE), BF16), jax.ShapeDtypeStruct((K, nb * LANE), F32)],
        compiler_params=_params(("parallel",)),
    )(proj, conv_w, dq, dk, dv)


def _softplus(x):
    return jnp.maximum(x, 0.0) + jnp.log(1.0 + jnp.exp(-jnp.abs(x)))


def _gates_fwd(proj, alog, dtb, off, H, *, name):
    T = proj.shape[0]
    tr = _tile(T, 512, 8)

    def body(ab_ref, al_ref, dt_ref, gb_ref):
        ab = ab_ref[...]
        lane = lax.broadcasted_iota(jnp.int32, ab.shape, 1)
        g = -jnp.exp(al_ref[...]) * _softplus(ab + dt_ref[...])
        gb_ref[...] = jnp.where(lane < H, g, jnp.where(lane < 2 * H, jax.nn.sigmoid(ab), 0.0))

    vec = pl.BlockSpec((1, LANE), lambda i: (0, 0))
    return pl.pallas_call(
        body, name=name, grid=(T // tr,),
        in_specs=[pl.BlockSpec((tr, LANE), lambda i: (i, off)), vec, vec],
        out_specs=pl.BlockSpec((tr, LANE), lambda i: (i, 0)),
        out_shape=jax.ShapeDtypeStruct((T, LANE), F32), compiler_params=_params(("parallel",)),
    )(proj, alog, dtb)


def _gates_bwd(proj, alog, dtb, dgb, off, H, *, name):
    T = proj.shape[0]
    tr = _tile(T, 512, 8)

    def body(ab_ref, al_ref, dt_ref, d_ref, dab_ref, dal_ref, ddt_ref):
        ab, d = ab_ref[...], d_ref[...]
        lane = lax.broadcasted_iota(jnp.int32, ab.shape, 1)
        z = ab + dt_ref[...]
        A = -jnp.exp(al_ref[...])
        da = d * A * jax.nn.sigmoid(z)
        beta = jax.nn.sigmoid(ab)
        db = d * beta * (1.0 - beta)
        is_g = lane < H
        dab_ref[...] = jnp.where(is_g, da, jnp.where(lane < 2 * H, db, 0.0)).astype(dab_ref.dtype)

        @pl.when(pl.program_id(0) == 0)
        def _():
            dal_ref[...] = jnp.zeros_like(dal_ref)
            ddt_ref[...] = jnp.zeros_like(ddt_ref)

        dal_ref[...] += jnp.sum(jnp.where(is_g, d * A * _softplus(z), 0.0), axis=0, keepdims=True)
        ddt_ref[...] += jnp.sum(jnp.where(is_g, da, 0.0), axis=0, keepdims=True)

    vec = pl.BlockSpec((1, LANE), lambda i: (0, 0))
    row = pl.BlockSpec((tr, LANE), lambda i: (i, 0))
    return pl.pallas_call(
        body, name=name, grid=(T // tr,),
        in_specs=[pl.BlockSpec((tr, LANE), lambda i: (i, off)), vec, vec, row],
        out_specs=[row, vec, vec],
        out_shape=[jax.ShapeDtypeStruct((T, LANE), BF16), jax.ShapeDtypeStruct((1, LANE), F32),
                   jax.ShapeDtypeStruct((1, LANE), F32)],
        compiler_params=_params(("arbitrary",)),
    )(proj, alog, dtb, dgb)


def _gated_norm_fwd(o, proj, gn, zoff, *, name):
    T, W = o.shape
    tr = _tile(T, 512, 8)

    def body(o_ref, z_ref, g_ref, y_ref):
        ov = o_ref[...]
        r = lax.rsqrt(jnp.mean(ov * ov, axis=-1, keepdims=True) + EPS)
        y_ref[...] = (ov * r * g_ref[...] * _silu(z_ref[...])).astype(y_ref.dtype)

    blk = pl.BlockSpec((tr, LANE), lambda i, j: (i, j))
    return pl.pallas_call(
        body, name=name, grid=(T // tr, W // LANE),
        in_specs=[blk, pl.BlockSpec((tr, LANE), lambda i, j: (i, zoff + j)), pl.BlockSpec((1, LANE), lambda i, j: (0, 0))],
        out_specs=blk, out_shape=jax.ShapeDtypeStruct((T, W), BF16), compiler_params=_params(("parallel", "parallel")),
    )(o, proj, gn)


def _gated_norm_bwd(o, proj, gn, dycat, zoff, yoff, *, name):
    T, W = o.shape
    tr = _tile(T, 512, 8)

    def body(o_ref, z_ref, g_ref, dy_ref, do_ref, dz_ref, dg_ref):
        ov, zv, gv, dy = o_ref[...], z_ref[...], g_ref[...], dy_ref[...]
        r = lax.rsqrt(jnp.mean(ov * ov, axis=-1, keepdims=True) + EPS)
        nh = ov * r
        s = _silu(zv)

        @pl.when((pl.program_id(0) == 0) & (pl.program_id(1) == 0))
        def _():
            dg_ref[...] = jnp.zeros_like(dg_ref)

        dg_ref[...] += jnp.sum(dy * nh * s, axis=0, keepdims=True)
        dz_ref[...] = (dy * nh * gv * _dsilu(zv)).astype(dz_ref.dtype)
        dn = dy * gv * s
        do_ref[...] = r * (dn - nh * jnp.mean(dn * nh, axis=-1, keepdims=True))

    blk = pl.BlockSpec((tr, LANE), lambda i, j: (i, j))
    vec = pl.BlockSpec((1, LANE), lambda i, j: (0, 0))
    return pl.pallas_call(
        body, name=name, grid=(T // tr, W // LANE),
        in_specs=[blk, pl.BlockSpec((tr, LANE), lambda i, j: (i, zoff + j)), vec,
                  pl.BlockSpec((tr, LANE), lambda i, j: (i, yoff + j))],
        out_specs=[blk, blk, vec],
        out_shape=[jax.ShapeDtypeStruct((T, W), F32), jax.ShapeDtypeStruct((T, W), BF16),
                   jax.ShapeDtypeStruct((1, LANE), F32)],
        compiler_params=_params(("arbitrary", "arbitrary")),
    )(o, proj, gn, dycat)


def _ffn_act_fwd(up_pre, conv_w, *, name):
    T, F2 = up_pre.shape
    nb = F2 // 2 // LANE
    K = conv_w.shape[0]

    def body(g_ref, v_ref, wg_ref, wv_ref, y_ref):
        def step(i, carry):
            r0 = pl.multiple_of(i * RC, RC)
            gate = _conv_down(_ext(g_ref, r0, T, True, False), wg_ref, K)
            val = _conv_down(_ext(v_ref, r0, T, True, False), wv_ref, K)
            y_ref[pl.ds(r0, RC), :] = (_silu(gate) * val).astype(y_ref.dtype)
            return carry
        lax.fori_loop(0, T // RC, step, 0)

    return pl.pallas_call(
        body, name=name, grid=(nb,),
        in_specs=_col_specs(T, (0, nb)) + [pl.BlockSpec((K, LANE), lambda j: (0, j)),
                                           pl.BlockSpec((K, LANE), lambda j: (0, nb + j))],
        out_specs=pl.BlockSpec((T, LANE), lambda j: (0, j)),
        out_shape=jax.ShapeDtypeStruct((T, F2 // 2), BF16), compiler_params=_params(("parallel",)),
    )(up_pre, up_pre, conv_w, conv_w)


def _ffn_act_bwd(up_pre, conv_w, dact, *, name):
    T, F2 = up_pre.shape
    nb = F2 // 2 // LANE
    K = conv_w.shape[0]

    def body(g_ref, v_ref, wg_ref, wv_ref, da_ref, dg_ref, dv_ref, dwg_ref, dwv_ref):
        def step(i, accs):
            r0 = pl.multiple_of(i * RC, RC)
            g3 = _ext(g_ref, r0, T, True, True)
            v3 = _ext(v_ref, r0, T, True, True)
            gate2 = _conv_down(g3, wg_ref, K)
            val2 = _conv_down(v3, wv_ref, K)
            da2 = _ext(da_ref, r0, T, False, True)
            dgate2 = da2 * val2 * _dsilu(gate2)
            dval2 = da2 * _silu(gate2)
            dgp, dvp, new = None, None, []
            for j in range(K):
                s = K - 1 - j
                tg = _up(dgate2, s, RC) * wg_ref[j:j + 1, :]
                tv = _up(dval2, s, RC) * wv_ref[j:j + 1, :]
                dgp = tg if dgp is None else dgp + tg
                dvp = tv if dvp is None else dvp + tv
                new.append(accs[2 * j] + _fold8(dgate2[:RC] * _down(g3[:RC + 8], s)))
                new.append(accs[2 * j + 1] + _fold8(dval2[:RC] * _down(v3[:RC + 8], s)))
            dg_ref[pl.ds(r0, RC), :] = dgp.astype(dg_ref.dtype)
            dv_ref[pl.ds(r0, RC), :] = dvp.astype(dv_ref.dtype)
            return tuple(new)

        accs = lax.fori_loop(0, T // RC, step, tuple(jnp.zeros((8, LANE), F32) for _ in range(2 * K)))
        for j in range(K):
            dwg_ref[j:j + 1, :] = jnp.sum(accs[2 * j], axis=0, keepdims=True)
            dwv_ref[j:j + 1, :] = jnp.sum(accs[2 * j + 1], axis=0, keepdims=True)

    col = pl.BlockSpec((T, LANE), lambda j: (0, j))
    wsp = pl.BlockSpec((K, LANE), lambda j: (0, j))
    return pl.pallas_call(
        body, name=name, grid=(nb,),
        in_specs=_col_specs(T, (0, nb)) + [wsp, pl.BlockSpec((K, LANE), lambda j: (0, nb + j)), col],
        out_specs=[col, col, wsp, wsp],
        out_shape=[jax.ShapeDtypeStruct((T, F2 // 2), BF16)] * 2 + [jax.ShapeDtypeStruct((K, F2 // 2), F32)] * 2,
        compiler_params=_params(("parallel",)),
    )(up_pre, up_pre, conv_w, conv_w, dact)


CPB = 8


def _dot(a, b):
    return jnp.dot(a, b, precision=HI, preferred_element_type=F32)


def _dot_nt(a, b):
    return lax.dot_general(a, b, _DN["nt"], precision=HI, preferred_element_type=F32)


def _dot_tn(a, b):
    return lax.dot_general(a, b, _DN["tn"], precision=HI, preferred_element_type=F32)


def _tri(strict=False, upper=False):
    r = lax.broadcasted_iota(jnp.int32, (CHUNK, CHUNK), 0)
    c = lax.broadcasted_iota(jnp.int32, (CHUNK, CHUNK), 1)
    if upper:
        return c >= r
    return (r > c) if strict else (r >= c)


def _chunk_decay(gb):
    gam = _dot(_tri().astype(F32), gb)
    diff = gam[:, :CHUNK] - gam.T[:CHUNK, :]
    D = jnp.exp(jnp.where(_tri(), diff, -1e30))
    return gam, D


def _delta_specs(T, H, cpb):
    rows = cpb * CHUNK
    col = lambda o: pl.BlockSpec((rows, LANE), functools.partial(lambda o, h, n: (n, o + h), o))
    bc = pl.BlockSpec((1, rows, LANE), lambda h, n: (h, n, 0))
    sq = pl.BlockSpec((1, cpb, CHUNK, CHUNK), lambda h, n: (h, n, 0, 0))
    vec = pl.BlockSpec((1, cpb, LANE), lambda h, n: (h, n, 0))
    return col, bc, sq, vec


def _delta_prep_fwd(qkv, gB, bB, H, *, name):
    T = qkv.shape[0]
    N = T // CHUNK
    cpb = _tile(N, CPB, 8)
    col, bc, sq, vec = _delta_specs(T, H, cpb)

    def body(q_ref, k_ref, v_ref, g_ref, b_ref, u_ref, w_ref, qd_ref, kd_ref, qk_ref, ti_ref, gl_ref):
        eye = (lax.broadcasted_iota(jnp.int32, (CHUNK, CHUNK), 0) == lax.broadcasted_iota(jnp.int32, (CHUNK, CHUNK), 1)).astype(F32)

        def step(c, carry):
            r0 = pl.multiple_of(c * CHUNK, CHUNK)
            rows = pl.ds(r0, CHUNK)
            q, k, v = q_ref[rows, :], k_ref[rows, :], v_ref[rows, :]
            bb = b_ref[0, rows, :]
            gam, D = _chunk_decay(g_ref[0, rows, :])
            e = jnp.exp(gam)
            L = jnp.where(_tri(strict=True), _dot_nt(k, k) * D, 0.0) * bb[:, :CHUNK]
            X = -L
            R = eye + X
            for _ in range(5):
                X = _dot(X, X)
                R = R + _dot(R, X)
            u_ref[rows, :] = _dot(R, bb * v)
            w_ref[rows, :] = _dot(R, bb * e * k)
            qd_ref[rows, :] = e * q
            glast = gam[CHUNK - 1:CHUNK, :]
            kd_ref[rows, :] = jnp.exp(glast - gam) * k
            qk_ref[0, c] = _dot_nt(q, k) * D
            ti_ref[0, c] = R
            gl_ref[0, pl.ds(c, 1), :] = jnp.exp(glast)
            return carry
        lax.fori_loop(0, cpb, step, 0)

    full = jax.ShapeDtypeStruct((T, H * LANE), F32)
    sqs = jax.ShapeDtypeStruct((H, N, CHUNK, CHUNK), F32)
    return pl.pallas_call(
        body, name=name, grid=(H, N // cpb),
        in_specs=[col(0), col(H), col(2 * H), bc, bc],
        out_specs=[col(0)] * 4 + [sq, sq, vec],
        out_shape=[full] * 4 + [sqs, sqs, jax.ShapeDtypeStruct((H, N, LANE), F32)],
        compiler_params=_params(("parallel", "parallel")),
    )(qkv, qkv, qkv, gB, bB)


def _delta_scan_fwd(u, w, qd, kd, qk, gl, H, *, name):
    T = u.shape[0]
    N = T // CHUNK
    cpb = _tile(N, CPB, 8)
    col, bc, sq, vec = _delta_specs(T, H, cpb)
    st = pl.BlockSpec((1, cpb, HEAD, HEAD), lambda h, n: (h, n, 0, 0))

    def body(u_ref, w_ref, qd_ref, kd_ref, qk_ref, gl_ref, o_ref, vn_ref, ss_ref, s_scr):
        @pl.when(pl.program_id(1) == 0)
        def _():
            s_scr[...] = jnp.zeros_like(s_scr)

        def step(c, S):
            rows = pl.ds(pl.multiple_of(c * CHUNK, CHUNK), CHUNK)
            ss_ref[0, c] = S
            vn = u_ref[rows, :] - _dot(w_ref[rows, :], S)
            o_ref[rows, :] = _dot(qd_ref[rows, :], S) + _dot(qk_ref[0, c], vn)
            vn_ref[rows, :] = vn
            return S * gl_ref[0, pl.ds(c, 1), :] + _dot_tn(kd_ref[rows, :], vn)
        s_scr[...] = lax.fori_loop(0, cpb, step, s_scr[...])

    full = jax.ShapeDtypeStruct((T, H * LANE), F32)
    return pl.pallas_call(
        body, name=name, grid=(H, N // cpb),
        in_specs=[col(0)] * 4 + [sq, vec],
        out_specs=[col(0), col(0), st],
        out_shape=[full, full, jax.ShapeDtypeStruct((H, N, HEAD, HEAD), F32)],
        scratch_shapes=[pltpu.VMEM((HEAD, HEAD), F32)],
        compiler_params=_params(("parallel", "arbitrary")),
    )(u, w, qd, kd, qk, gl)


def _delta_scan_bwd(do, w, qd, kd, vn, qk, gl, ss, H, *, name):
    T = do.shape[0]
    N = T // CHUNK
    cpb = _tile(N, CPB, 8)
    nbk = N // cpb
    rows_b = cpb * CHUNK
    col = lambda: pl.BlockSpec((rows_b, LANE), lambda h, n: (nbk - 1 - n, h))
    sq = pl.BlockSpec((1, cpb, CHUNK, CHUNK), lambda h, n: (h, nbk - 1 - n, 0, 0))
    vec = pl.BlockSpec((1, cpb, LANE), lambda h, n: (h, nbk - 1 - n, 0))
    st = pl.BlockSpec((1, cpb, HEAD, HEAD), lambda h, n: (h, nbk - 1 - n, 0, 0))

    def body(do_ref, w_ref, qd_ref, kd_ref, vn_ref, qk_ref, gl_ref, ss_ref,
             du_ref, dw_ref, dqd_ref, dkd_ref, dqk_ref, dgl_ref, ds_scr):
        @pl.when(pl.program_id(1) == 0)
        def _():
            ds_scr[...] = jnp.zeros_like(ds_scr)

        def step(i, dS):
            c = cpb - 1 - i
            rows = pl.ds(pl.multiple_of(c * CHUNK, CHUNK), CHUNK)
            S, dov, vnv = ss_ref[0, c], do_ref[rows, :], vn_ref[rows, :]
            dvn = _dot_tn(qk_ref[0, c], dov) + _dot(kd_ref[rows, :], dS)
            du_ref[rows, :] = dvn
            dw_ref[rows, :] = -_dot_nt(dvn, S)
            dqd_ref[rows, :] = _dot_nt(dov, S)
            dkd_ref[rows, :] = _dot_nt(vnv, dS)
            dqk_ref[0, c] = _dot_nt(dov, vnv)
            dgl = jnp.sum(jnp.sum(dS * S, axis=1, keepdims=True), axis=0, keepdims=True)
            dgl_ref[0, pl.ds(c, 1), :] = jnp.broadcast_to(dgl, (1, LANE))
            return (_dot_tn(qd_ref[rows, :], dov) + dS * gl_ref[0, pl.ds(c, 1), :]
                    - _dot_tn(w_ref[rows, :], dvn))
        ds_scr[...] = lax.fori_loop(0, cpb, step, ds_scr[...])

    full = jax.ShapeDtypeStruct((T, H * LANE), F32)
    return pl.pallas_call(
        body, name=name, grid=(H, nbk),
        in_specs=[col()] * 5 + [sq, vec, st],
        out_specs=[col()] * 4 + [sq, vec],
        out_shape=[full] * 4 + [jax.ShapeDtypeStruct((H, N, CHUNK, CHUNK), F32), jax.ShapeDtypeStruct((H, N, LANE), F32)],
        scratch_shapes=[pltpu.VMEM((HEAD, HEAD), F32)],
        compiler_params=_params(("parallel", "arbitrary")),
    )(do, w, qd, kd, vn, qk, gl, ss)


def _delta_prep_bwd(qkv, gB, bB, ti, u, w, qk, du, dw, dqd, dkd, dqk, dgl, H, *, name):
    T = qkv.shape[0]
    N = T // CHUNK
    cpb = _tile(N, CPB, 8)
    col, bc, sq, vec = _delta_specs(T, H, cpb)

    def body(q_ref, k_ref, v_ref, g_ref, b_ref, ti_ref, u_ref, w_ref, qk_ref,
             du_ref, dw_ref, dqd_ref, dkd_ref, dqk_ref, dgl_ref,
             dq_ref, dk_ref, dv_ref, dg_ref, db_ref):
        ones = jnp.ones((CHUNK, LANE), F32)
        lsum = lambda x: jnp.sum(x, axis=-1, keepdims=True)

        def step(c, carry):
            r0 = pl.multiple_of(c * CHUNK, CHUNK)
            rows = pl.ds(r0, CHUNK)
            q, k, v = q_ref[rows, :], k_ref[rows, :], v_ref[rows, :]
            bb = b_ref[0, rows, :]
            gam, D = _chunk_decay(g_ref[0, rows, :])
            e = jnp.exp(gam)
            glast = gam[CHUNK - 1:CHUNK, :]
            eL = jnp.exp(glast - gam)
            gl = jnp.exp(glast)
            Ti, uv, wv, QK = ti_ref[0, c], u_ref[rows, :], w_ref[rows, :], qk_ref[0, c]
            duv, dwv, dqd_v, dkd_v, dqk_v = du_ref[rows, :], dw_ref[rows, :], dqd_ref[rows, :], dkd_ref[rows, :], dqk_ref[0, c]
            KKD = jnp.where(_tri(strict=True), _dot_nt(k, k) * D, 0.0)
            rw = bb * e * k
            dru = _dot_tn(Ti, duv)
            drw = _dot_tn(Ti, dwv)
            dL = jnp.where(_tri(strict=True), -(_dot_nt(dru, uv) + _dot_nt(drw, wv)), 0.0)
            Mm = dL * bb[:, :CHUNK]
            dKK = Mm * D
            dQK = dqk_v * D
            P = Mm * KKD + dqk_v * QK
            dq_ref[rows, :] = _dot(dQK, k) + e * dqd_v
            dk_ref[rows, :] = (_dot_tn(dQK, q) + _dot(dKK, k) + _dot_tn(dKK, k) + bb * e * drw + eL * dkd_v)
            dv_ref[rows, :] = bb * dru
            db = _dot(dL * KKD, ones) + lsum(dru * v) + lsum(drw * e * k)
            kdv = eL * k
            dgam = (_dot(P, ones) - _dot_tn(P, ones) + lsum(drw * rw) + lsum(dqd_v * e * q) - lsum(dkd_v * kdv))
            xlast = jnp.sum(lsum(dkd_v * kdv), axis=0, keepdims=True) + gl * dgl_ref[0, pl.ds(c, 1), :]
            dg_ref[0, rows, :] = _dot(_tri(upper=True).astype(F32), dgam) + xlast
            db_ref[0, rows, :] = db
            return carry
        lax.fori_loop(0, cpb, step, 0)

    full = jax.ShapeDtypeStruct((T, H * LANE), F32)
    bcs = jax.ShapeDtypeStruct((H, T, LANE), F32)
    return pl.pallas_call(
        body, name=name, grid=(H, N // cpb),
        in_specs=[col(0), col(H), col(2 * H), bc, bc, sq, col(0), col(0), sq, col(0), col(0), col(0), col(0), sq, vec],
        out_specs=[col(0), col(0), col(0), bc, bc],
        out_shape=[full, full, full, bcs, bcs],
        compiler_params=_params(("parallel", "parallel")),
    )(qkv, qkv, qkv, gB, bB, ti, u, w, qk, du, dw, dqd, dkd, dqk, dgl)


def _adam(parts, w, m, v, *, name):
    P, R, C = parts.shape
    tr = _tile(R, 256, 8)

    def body(p_ref, w_ref, m_ref, v_ref, g_ref, d_ref, nm_ref, nv_ref):
        g = p_ref[0].astype(F32)
        for i in range(1, P):
            g = g + p_ref[i].astype(F32)
        mn = ADAM_B1 * m_ref[...] + (1.0 - ADAM_B1) * g
        vn = ADAM_B2 * v_ref[...] + (1.0 - ADAM_B2) * (g * g)
        m_hat = mn / (1.0 - ADAM_B1 ** ADAM_STEP)
        v_hat = vn / (1.0 - ADAM_B2 ** ADAM_STEP)
        g_ref[...] = g
        d_ref[...] = -ADAM_LR * (m_hat / (jnp.sqrt(v_hat) + ADAM_EPS) + ADAM_WD * w_ref[...])
        nm_ref[...] = mn
        nv_ref[...] = vn

    blk = pl.BlockSpec((tr, C), lambda i: (i, 0))
    return pl.pallas_call(
        body, name=name, grid=(R // tr,),
        in_specs=[pl.BlockSpec((P, tr, C), lambda i: (0, i, 0)), blk, blk, blk],
        out_specs=[blk] * 4, out_shape=[jax.ShapeDtypeStruct((R, C), F32)] * 4,
        compiler_params=_params(("parallel",)),
    )(parts, w, m, v)


def _mesh_pos():
    return lax.axis_index("x"), lax.axis_index("y"), lax.axis_index("c")


def _peer(k):
    x, y, c = _mesh_pos()
    px, py, pc = x ^ ((k >> 2) & 1), y ^ ((k >> 1) & 1), c ^ (k & 1)
    return (px, py, pc), 4 * px + 2 * py + pc


def _exchange(arrays, scatter, *, name):
    n = len(arrays)
    blocks = [a.shape[1:] if scatter else a.shape for a in arrays]

    def body(*refs):
        srcs, dsts = refs[:n], refs[n:2 * n]
        send_sems, recv_sems, local_sems = refs[2 * n:]
        x, y, c = _mesh_pos()
        me = 4 * x + 2 * y + c
        local, sends = [], []
        for a in range(n):
            cp = pltpu.make_async_copy(srcs[a].at[me] if scatter else srcs[a], dsts[a].at[me], local_sems.at[a])
            cp.start()
            local.append(cp)
            for k in range(1, N_DEV):
                dev, idx = _peer(k)
                cp = pltpu.make_async_remote_copy(
                    src_ref=srcs[a].at[idx] if scatter else srcs[a], dst_ref=dsts[a].at[me],
                    send_sem=send_sems.at[a * N_DEV + k], recv_sem=recv_sems.at[a * N_DEV + k],
                    device_id=dev, device_id_type=MESH)
                cp.start()
                sends.append(cp)
        for a in range(n):
            for k in range(1, N_DEV):
                dev, idx = _peer(k)
                pltpu.make_async_remote_copy(
                    src_ref=srcs[a].at[idx] if scatter else srcs[a], dst_ref=dsts[a].at[idx],
                    send_sem=send_sems.at[a * N_DEV + k], recv_sem=recv_sems.at[a * N_DEV + k],
                    device_id=dev, device_id_type=MESH).wait_recv()
        for cp in sends:
            cp.wait_send()
        for cp in local:
            cp.wait()

    anyspec = pl.BlockSpec(memory_space=pl.ANY)
    return pl.pallas_call(
        body, name=name, in_specs=[anyspec] * n, out_specs=[anyspec] * n,
        out_shape=[jax.ShapeDtypeStruct((N_DEV,) + tuple(b), a.dtype) for a, b in zip(arrays, blocks)],
        scratch_shapes=[pltpu.SemaphoreType.DMA((n * N_DEV,)), pltpu.SemaphoreType.DMA((n * N_DEV,)),
                        pltpu.SemaphoreType.DMA((n,))],
    )(*arrays)


def _local_step(x, p, tgt, W, S):
    T, D = x.shape
    CW = DNW = D // 2
    H = DNW // HEAD
    nA, nD = CW // LANE, DNW // LANE
    qkv_off, z_off, ab_off = 3 * nA, 3 * nA + 3 * nD, 3 * nA + 4 * nD
    alog = jnp.pad(S["a_log"], ((0, 0), (0, LANE - H)))
    dtb = jnp.pad(S["dt_bias"], ((0, 0), (0, LANE - H)))
    add = lambda acc, r: (acc + r,)

    h1 = _rms_fwd(x, S["g_mix"], name="rms1_fwd")
    proj = _matmul(h1, W["w_in"], "nn", name="mm_in")
    y_a = _group_a_fwd(proj, S["conv_a"], CW, name="group_a_fwd")
    qkv = _qkv_fwd(proj, S["conv_qkv"], qkv_off, H, name="qkv_fwd")
    gb = _gates_fwd(proj, alog, dtb, ab_off, H, name="gates_fwd")
    bcast = lambda cols: jnp.broadcast_to(cols.T[:, :, None], (H, T, LANE))
    gB, bB = bcast(gb[:, :H]), bcast(gb[:, H:2 * H])
    u, w, qd, kd, qk, ti, gl = _delta_prep_fwd(qkv, gB, bB, H, name="delta_prep_fwd")
    o, vn, ss = _delta_scan_fwd(u, w, qd, kd, qk, gl, H, name="delta_scan_fwd")
    y_b = _gated_norm_fwd(o, proj, S["dn_g"], z_off, name="gated_norm_fwd")
    ycat = jnp.concatenate([y_a, y_b], axis=1)
    x1 = _matmul(ycat, W["w_out"], "nn", name="mm_out", epilogue=add, extras=(x,))
    h2 = _rms_fwd(x1, S["g_ffn"], name="rms2_fwd")
    up_pre = _matmul(h2, W["w_up"], "nn", name="mm_up")
    act = _ffn_act_fwd(up_pre, S["conv_ffn"], name="ffn_act_fwd")
    x2 = _matmul(act, W["w_down"], "nn", name="mm_down", epilogue=add, extras=(x1,))
    h3 = _rms_fwd(x2, S["g_ple"], name="rms3_fwd")
    pp = _matmul(p, W["w_pp"], "nn", name="mm_pp")

    def ple_epi(acc, x2r, ppr):
        s = jax.nn.sigmoid(acc)
        return x2r + s * ppr, s

    x3, sg = _matmul(h3, W["w_pg"], "nn", name="mm_pg", out_dtypes=(F32, F32), epilogue=ple_epi, extras=(x2, pp))
    dx3, dg_final, loss = _final_loss(x3, S["g_final"], tgt, name="final_loss")

    G = {"g_final": dg_final}
    dpg, dpp = _ple_bwd(dx3, pp, sg, name="ple_bwd")
    G["w_pp"] = _matmul(p, dpp, "tn", name="mm_dwpp", out_dtypes=(BF16,))
    G["w_pg"] = _matmul(h3, dpg, "tn", name="mm_dwpg", out_dtypes=(BF16,))
    dh3 = _matmul(dpg, W["w_pg"], "nt", name="mm_dh3")
    dx2, G["g_ple"] = _rms_bwd(x2, S["g_ple"], dh3, dx3, name="rms3_bwd")
    dact = _matmul(dx2, W["w_down"], "nt", name="mm_dact")
    G["w_down"] = _matmul(act, dx2, "tn", name="mm_dwdown", out_dtypes=(BF16,))
    dup_g, dup_v, dcf_g, dcf_v = _ffn_act_bwd(up_pre, S["conv_ffn"], dact, name="ffn_act_bwd")
    G["conv_ffn"] = jnp.concatenate([dcf_g, dcf_v], axis=1)
    dup = jnp.concatenate([dup_g, dup_v], axis=1)
    dh2 = _matmul(dup, W["w_up"], "nt", name="mm_dh2")
    G["w_up"] = _matmul(h2, dup, "tn", name="mm_dwup", out_dtypes=(BF16,))
    dx1, G["g_ffn"] = _rms_bwd(x1, S["g_ffn"], dh2, dx2, name="rms2_bwd")
    dycat = _matmul(dx1, W["w_out"], "nt", name="mm_dycat")
    G["w_out"] = _matmul(ycat, dx1, "tn", name="mm_dwout", out_dtypes=(BF16,))
    do, dz, G["dn_g"] = _gated_norm_bwd(o, proj, S["dn_g"], dycat, z_off, nA, name="gated_norm_bwd")
    du, dw, dqd, dkd, dqk, dgl = _delta_scan_bwd(do, w, qd, kd, vn, qk, gl, ss, H, name="delta_scan_bwd")
    dq, dk, dv, dgB, dbB = _delta_prep_bwd(qkv, gB, bB, ti, u, w, qk, du, dw, dqd, dkd, dqk, dgl, H,
                                           name="delta_prep_bwd")
    dgb = jnp.pad(jnp.concatenate([dgB[:, :, 0].T, dbB[:, :, 0].T], axis=1), ((0, 0), (0, LANE - 2 * H)))
    dab, dal, ddt = _gates_bwd(proj, alog, dtb, dgb, ab_off, H, name="gates_bwd")
    G["a_log"], G["dt_bias"] = dal[:, :H], ddt[:, :H]
    dqkv, G["conv_qkv"] = _qkv_bwd(proj, S["conv_qkv"], dq, dk, dv, qkv_off, H, name="qkv_bwd")
    dax, dab_, dac, G["conv_a"] = _group_a_bwd(proj, S["conv_a"], dycat, CW, name="group_a_bwd")
    in_p = W["w_in"].shape[1]
    dproj = jnp.concatenate([dax, dab_, dac, dqkv, dz, dab, jnp.zeros((T, in_p - (ab_off + 1) * LANE), BF16)], axis=1)
    dh1 = _matmul(dproj, W["w_in"], "nt", name="mm_dh1")
    G["w_in"] = _matmul(h1, dproj, "tn", name="mm_dwin", out_dtypes=(BF16,))
    grad_x, G["g_mix"] = _rms_bwd(x, S["g_mix"], dh1, dx1, name="rms1_bwd")
    return loss, grad_x, G


def _pad_cols(a, n):
    return jnp.pad(a, ((0, 0), (0, n - a.shape[1])))


def _col_sharded(landed):
    _, R, C = landed.shape
    return jnp.transpose(landed, (1, 0, 2)).reshape(R, N_DEV * C)


def _col_parts(full):
    R, C8 = full.shape
    return jnp.transpose(full.reshape(R, N_DEV, C8 // N_DEV), (1, 0, 2))


def kernel(x, p, norm_mix_g, w_in, conv_a_w, conv_qkv_w, a_log, dt_bias, dn_norm_g, w_out, norm_ffn_g, w_up, conv_ffn_w, w_down, norm_ple_g, w_ple_gate, w_ple_proj, final_norm_g, loss_target, m_norm_mix_g, m_w_in, m_conv_a_w, m_conv_qkv_w, m_a_log, m_dt_bias, m_dn_norm_g, m_w_out, m_norm_ffn_g, m_w_up, m_conv_ffn_w, m_w_down, m_norm_ple_g, m_w_ple_gate, m_w_ple_proj, m_final_norm_g, v_norm_mix_g, v_w_in, v_conv_a_w, v_conv_qkv_w, v_a_log, v_dt_bias, v_dn_norm_g, v_w_out, v_norm_ffn_g, v_w_up, v_conv_ffn_w, v_w_down, v_norm_ple_g, v_w_ple_gate, v_w_ple_proj, v_final_norm_g):
    T, D = x.shape[1], x.shape[2]
    xd, _, cd = _mesh_pos()
    me = 4 * xd + 2 * lax.axis_index("y") + cd

    conv_sh = [conv_a_w[0], conv_qkv_w[0], conv_ffn_w[0]]
    conv_n = [c.size for c in conv_sh]
    pack_rows = -(-sum(conv_n) // LANE)
    conv_pack = jnp.pad(jnp.concatenate([c.reshape(-1) for c in conv_sh]), (0, pack_rows * LANE - sum(conv_n))).reshape(pack_rows, LANE)
    mats = [w_in[0], w_out[0], w_up[0], w_down[0], w_ple_gate[0], w_ple_proj[0]]
    landed = _exchange([m_.astype(BF16) for m_ in mats] + [conv_pack], False, name="gather_weights")
    l_in, l_out, l_up, l_down, l_pg, l_pp, l_conv = landed
    in_cols = N_DEV * w_in.shape[2]
    in_p = (in_cols // LANE) * LANE + AB_PAD
    W = {
        "w_in": _pad_cols(_col_sharded(l_in), in_p),
        "w_out": l_out.reshape(-1, D),
        "w_up": _col_sharded(l_up),
        "w_down": l_down.reshape(-1, D),
        "w_pg": l_pg.reshape(-1, D),
        "w_pp": _col_sharded(l_pp),
    }
    flat = l_conv.reshape(N_DEV, pack_rows * LANE)
    conv_full, o_ = [], 0
    for c, n_ in zip(conv_sh, conv_n):
        conv_full.append(_col_sharded(flat[:, o_:o_ + n_].reshape((N_DEV,) + c.shape)))
        o_ += n_
    S = {
        "g_mix": norm_mix_g, "conv_a": conv_full[0], "conv_qkv": conv_full[1], "a_log": a_log, "dt_bias": dt_bias,
        "dn_g": dn_norm_g, "g_ffn": norm_ffn_g, "conv_ffn": conv_full[2], "g_ple": norm_ple_g,
        "g_final": final_norm_g.reshape(1, D),
    }

    loss_v, grad_x, G = _local_step(x[0], p[0, 0], loss_target[0], W, S)
    loss = lax.psum(loss_v[0, 0], ("x", "y", "c"))

    big = [_col_parts(G["w_in"][:, :in_cols]), G["w_out"].reshape(N_DEV, -1, D), _col_parts(G["w_up"]),
           G["w_down"].reshape(N_DEV, -1, D), G["w_pg"].reshape(N_DEV, -1, D), _col_parts(G["w_pp"])]
    big_l = _exchange(big, True, name="scatter_grads")
    small_names = ["g_mix", "g_ffn", "g_ple", "g_final", "dn_g", "a_log", "dt_bias", "conv_a", "conv_qkv", "conv_ffn"]
    small_rows, pieces = [], []
    for nm in small_names:
        g_ = G[nm].reshape(-1)
        r_ = -(-g_.size // LANE)
        small_rows.append(r_)
        pieces.append(jnp.pad(g_, (0, r_ * LANE - g_.size)).reshape(r_, LANE))
    tot = sum(small_rows)
    pieces.append(jnp.zeros((-(-tot // 8) * 8 - tot, LANE), F32))
    (small_l,) = _exchange([jnp.concatenate(pieces, axis=0)], False, name="gather_small_grads")

    def small_parts(nm):
        i = small_names.index(nm)
        r0 = sum(small_rows[:i])
        shp = G[nm].shape
        return small_l[:, r0:r0 + small_rows[i], :].reshape(N_DEV, -1)[:, :G[nm].size].reshape((N_DEV,) + shp)

    def conv_parts(nm, shard):
        full = small_parts(nm)
        C = shard.shape[-1]
        return lax.dynamic_slice_in_dim(full, me * C, C, axis=2)

    def adam(parts, w_, m_, v_, nm):
        shp = w_.shape
        w2, m2, v2 = (t.reshape(parts.shape[1:]) for t in (w_, m_, v_))
        return tuple(t.reshape(shp) for t in _adam(parts, w2, m2, v2, name="adam_" + nm))

    res = [
        adam(small_parts("g_mix"), norm_mix_g, m_norm_mix_g, v_norm_mix_g, "norm_mix_g"),
        adam(big_l[0], w_in, m_w_in, v_w_in, "w_in"),
        adam(conv_parts("conv_a", conv_a_w), conv_a_w, m_conv_a_w, v_conv_a_w, "conv_a_w"),
        adam(conv_parts("conv_qkv", conv_qkv_w), conv_qkv_w, m_conv_qkv_w, v_conv_qkv_w, "conv_qkv_w"),
        adam(small_parts("a_log"), a_log, m_a_log, v_a_log, "a_log"),
        adam(small_parts("dt_bias"), dt_bias, m_dt_bias, v_dt_bias, "dt_bias"),
        adam(small_parts("dn_g"), dn_norm_g, m_dn_norm_g, v_dn_norm_g, "dn_norm_g"),
        adam(big_l[1], w_out, m_w_out, v_w_out, "w_out"),
        adam(small_parts("g_ffn"), norm_ffn_g, m_norm_ffn_g, v_norm_ffn_g, "norm_ffn_g"),
        adam(big_l[2], w_up, m_w_up, v_w_up, "w_up"),
        adam(conv_parts("conv_ffn", conv_ffn_w), conv_ffn_w, m_conv_ffn_w, v_conv_ffn_w, "conv_ffn_w"),
        adam(big_l[3], w_down, m_w_down, v_w_down, "w_down"),
        adam(small_parts("g_ple"), norm_ple_g, m_norm_ple_g, v_norm_ple_g, "norm_ple_g"),
        adam(big_l[4], w_ple_gate, m_w_ple_gate, v_w_ple_gate, "w_ple_gate"),
        adam(big_l[5], w_ple_proj, m_w_ple_proj, v_w_ple_proj, "w_ple_proj"),
        adam(small_parts("g_final"), final_norm_g.reshape(1, D), m_final_norm_g.reshape(1, D),
             v_final_norm_g.reshape(1, D), "final_norm_g"),
    ]
    res[-1] = tuple(t.reshape(D) for t in res[-1])
    grads, deltas, new_m, new_v = zip(*res)
    return (loss, grad_x[None], *grads, *deltas, *new_m, *new_v)
```

```python
import functools

import jax
import jax.numpy as jnp
from jax import lax
from jax.experimental import pallas as pl
from jax.experimental.pallas import tpu as pltpu

F32 = jnp.float32
BF16 = jnp.bfloat16
HI = lax.Precision.HIGHEST

EPS = 1e-6
CHUNK = 64
HEAD = 128
LANE = 128
N_DEV = 8
AB_PAD = 512

ADAM_LR = 0.001
ADAM_B1 = 0.9
ADAM_B2 = 0.999
ADAM_EPS = 1e-08
ADAM_WD = 0.01
ADAM_STEP = 10

MESH = pl.DeviceIdType.MESH


def _tile(dim, target, align=LANE):
    if dim <= target:
        return dim
    t = (target // align) * align
    while t > align and dim % t:
        t -= align
    assert dim % t == 0, (dim, target)
    return t


def _params(sem, vmem_mb=48):
    return pltpu.CompilerParams(dimension_semantics=sem, vmem_limit_bytes=vmem_mb << 20)


_DN = {"nn": (((1,), (0,)), ((), ())), "nt": (((1,), (1,)), ((), ())), "tn": (((0,), (0,)), ((), ()))}


def _matmul(a, b, mode, *, name, out_dtypes=(F32,), epilogue=None, extras=(), after=None, tm=1024, tn=1024, tk=512):
    if mode == "nn":
        (M, K), (K2, N) = a.shape, b.shape
    elif mode == "nt":
        (M, K), (N, K2) = a.shape, b.shape
    else:
        (K, M), (K2, N) = a.shape, b.shape
    assert K == K2, (name, a.shape, b.shape)
    tm, tn, tk = _tile(M, tm), _tile(N, tn), _tile(K, tk)
    nk = K // tk
    n_ex, n_out = len(extras), len(out_dtypes)
    dn = _DN[mode]

    n_tok = 0 if after is None else 1

    def body(a_ref, b_ref, *rest):
        rest = rest[n_tok:]
        ex_refs, out_refs, acc = rest[:n_ex], rest[n_ex:n_ex + n_out], rest[-1]
        k = pl.program_id(2)

        @pl.when(k == 0)
        def _():
            acc[...] = jnp.zeros_like(acc)

        acc[...] += lax.dot_general(a_ref[...].astype(BF16), b_ref[...].astype(BF16), dn,
                                    preferred_element_type=F32)

        @pl.when(k == nk - 1)
        def _():
            res = acc[...]
            outs = (res,) if epilogue is None else epilogue(res, *[e[...] for e in ex_refs])
            for o_ref, val in zip(out_refs, outs):
                o_ref[...] = val.astype(o_ref.dtype)

    a_spec = pl.BlockSpec((tk, tm), lambda i, j, k: (k, i)) if mode == "tn" else pl.BlockSpec((tm, tk), lambda i, j, k: (i, k))
    b_spec = pl.BlockSpec((tn, tk), lambda i, j, k: (j, k)) if mode == "nt" else pl.BlockSpec((tk, tn), lambda i, j, k: (k, j))
    mn_spec = pl.BlockSpec((tm, tn), lambda i, j, k: (i, j))
    outs = pl.pallas_call(
        body, name=name, grid=(M // tm, N // tn, nk),
        in_specs=[a_spec, b_spec] + [pl.BlockSpec((8, LANE), lambda i, j, k: (0, 0))] * n_tok + [mn_spec] * n_ex,
        out_specs=[mn_spec] * n_out,
        out_shape=[jax.ShapeDtypeStruct((M, N), dt) for dt in out_dtypes],
        scratch_shapes=[pltpu.VMEM((tm, tn), F32)],
        compiler_params=_params(("parallel", "parallel", "arbitrary"), 56),
    )(a, b, *([] if after is None else [after]), *extras)
    return outs[0] if n_out == 1 else outs


def _rms_fwd(x, g, *, name):
    T, D = x.shape
    tr = _tile(T, 256, 8)

    def body(x_ref, g_ref, h_ref):
        xv = x_ref[...]
        r = lax.rsqrt(jnp.mean(xv * xv, axis=-1, keepdims=True) + EPS)
        h_ref[...] = (xv * r * g_ref[...]).astype(h_ref.dtype)

    return pl.pallas_call(
        body, name=name, grid=(T // tr,),
        in_specs=[pl.BlockSpec((tr, D), lambda i: (i, 0)), pl.BlockSpec((1, D), lambda i: (0, 0))],
        out_specs=pl.BlockSpec((tr, D), lambda i: (i, 0)),
        out_shape=jax.ShapeDtypeStruct((T, D), BF16),
        compiler_params=_params(("parallel",)),
    )(x, g)


def _rms_bwd(x, g, dh, dres, *, name):
    T, D = x.shape
    tr = _tile(T, 256, 8)

    def body(x_ref, g_ref, dh_ref, dres_ref, dx_ref, dg_ref):
        xv = x_ref[...]
        r = lax.rsqrt(jnp.mean(xv * xv, axis=-1, keepdims=True) + EPS)
        xh = xv * r
        dh = dh_ref[...]

        @pl.when(pl.program_id(0) == 0)
        def _():
            dg_ref[...] = jnp.zeros_like(dg_ref)

        dg_ref[...] += jnp.sum(dh * xh, axis=0, keepdims=True)
        dxh = dh * g_ref[...]
        dx_ref[...] = dres_ref[...] + r * (dxh - xh * jnp.mean(dxh * xh, axis=-1, keepdims=True))

    row = pl.BlockSpec((tr, D), lambda i: (i, 0))
    vec = pl.BlockSpec((1, D), lambda i: (0, 0))
    return pl.pallas_call(
        body, name=name, grid=(T // tr,),
        in_specs=[row, vec, row, row], out_specs=[row, vec],
        out_shape=[jax.ShapeDtypeStruct((T, D), F32), jax.ShapeDtypeStruct((1, D), F32)],
        compiler_params=_params(("arbitrary",)),
    )(x, g, dh, dres)


def _final_loss(x, g, tgt, *, name):
    T, D = x.shape
    tr = _tile(T, 256, 8)

    def body(x_ref, g_ref, t_ref, dx_ref, dg_ref, loss_ref):
        xv = x_ref[...]
        r = lax.rsqrt(jnp.mean(xv * xv, axis=-1, keepdims=True) + EPS)
        xh = xv * r
        gv = g_ref[...]
        err = xh * gv - t_ref[...]

        @pl.when(pl.program_id(0) == 0)
        def _():
            dg_ref[...] = jnp.zeros_like(dg_ref)
            loss_ref[...] = jnp.zeros_like(loss_ref)

        part = 0.5 * jnp.sum(jnp.mean(err * err, axis=-1, keepdims=True), axis=0, keepdims=True)
        loss_ref[...] += jnp.broadcast_to(part, loss_ref.shape)
        dy = err * (1.0 / D)
        dg_ref[...] += jnp.sum(dy * xh, axis=0, keepdims=True)
        dxh = dy * gv
        dx_ref[...] = r * (dxh - xh * jnp.mean(dxh * xh, axis=-1, keepdims=True))

    row = pl.BlockSpec((tr, D), lambda i: (i, 0))
    vec = pl.BlockSpec((1, D), lambda i: (0, 0))
    return pl.pallas_call(
        body, name=name, grid=(T // tr,),
        in_specs=[row, vec, row], out_specs=[row, vec, pl.BlockSpec((1, LANE), lambda i: (0, 0))],
        out_shape=[jax.ShapeDtypeStruct((T, D), F32), jax.ShapeDtypeStruct((1, D), F32),
                   jax.ShapeDtypeStruct((1, LANE), F32)],
        compiler_params=_params(("arbitrary",)),
    )(x, g, tgt)


def _ple_bwd(dx3, pp, sg, *, name):
    T, D = dx3.shape
    tr = _tile(T, 256, 8)

    def body(dx_ref, pp_ref, sg_ref, dpg_ref, dpp_ref):
        dx, s = dx_ref[...], sg_ref[...]
        dpg_ref[...] = (dx * pp_ref[...] * s * (1.0 - s)).astype(dpg_ref.dtype)
        dpp_ref[...] = (dx * s).astype(dpp_ref.dtype)

    row = pl.BlockSpec((tr, D), lambda i: (i, 0))
    return pl.pallas_call(
        body, name=name, grid=(T // tr,), in_specs=[row, row, row], out_specs=[row, row],
        out_shape=[jax.ShapeDtypeStruct((T, D), BF16)] * 2, compiler_params=_params(("parallel",)),
    )(dx3, pp, sg)


RC = 64


def _ext(ref, r0, T, before, after):
    parts = []
    if before:
        p0 = pl.multiple_of(jnp.maximum(r0 - 8, 0), 8)
        parts.append(jnp.where(r0 > 0, ref[pl.ds(p0, 8), :], 0.0))
    parts.append(ref[pl.ds(r0, RC), :])
    if after:
        n0 = pl.multiple_of(jnp.minimum(r0 + RC, T - 8), 8)
        parts.append(jnp.where(r0 + RC < T, ref[pl.ds(n0, 8), :], 0.0))
    return parts[0] if len(parts) == 1 else jnp.concatenate(parts, axis=0)


def _down(xx, s):
    return (xx if s == 0 else pltpu.roll(xx, s, 0))[8:, :]


def _up(xx, s, rows):
    return (xx if s == 0 else pltpu.roll(xx, xx.shape[0] - s, 0))[:rows, :]


def _conv_down(xx, w_ref, K):
    y = None
    for j in range(K):
        t = _down(xx, K - 1 - j) * w_ref[j:j + 1, :]
        y = t if y is None else y + t
    return y


def _fold8(x):
    return jnp.sum(x.reshape(x.shape[0] // 8, 8, x.shape[1]), axis=0)


def _silu(x):
    return x * jax.nn.sigmoid(x)


def _dsilu(x):
    s = jax.nn.sigmoid(x)
    return s * (1.0 + x * (1.0 - s))


def _col_specs(T, offs):
    return [pl.BlockSpec((T, LANE), functools.partial(lambda o, j: (0, o + j), o)) for o in offs]


def _group_a_fwd(proj, conv_w, CW, *, name):
    T = proj.shape[0]
    nb = CW // LANE
    K = conv_w.shape[0]

    def body(ax_ref, ab_ref, ac_ref, w_ref, y_ref):
        def step(i, carry):
            r0 = pl.multiple_of(i * RC, RC)
            m = _ext(ac_ref, r0, T, True, False) * _ext(ax_ref, r0, T, True, False)
            y_ref[pl.ds(r0, RC), :] = (ab_ref[pl.ds(r0, RC), :] * _conv_down(m, w_ref, K)).astype(y_ref.dtype)
            return carry
        lax.fori_loop(0, T // RC, step, 0)

    return pl.pallas_call(
        body, name=name, grid=(nb,),
        in_specs=_col_specs(T, (0, nb, 2 * nb)) + [pl.BlockSpec((K, LANE), lambda j: (0, j))],
        out_specs=pl.BlockSpec((T, LANE), lambda j: (0, j)),
        out_shape=jax.ShapeDtypeStruct((T, CW), BF16), compiler_params=_params(("parallel",)),
    )(proj, proj, proj, conv_w)


def _group_a_bwd(proj, conv_w, dycat, CW, *, name):
    T = proj.shape[0]
    nb = CW // LANE
    K = conv_w.shape[0]

    def body(ax_ref, ab_ref, ac_ref, w_ref, dy_ref, dax_ref, dab_ref, dac_ref, dw_ref):
        def step(i, accs):
            r0 = pl.multiple_of(i * RC, RC)
            ax3 = _ext(ax_ref, r0, T, True, True)
            ac3 = _ext(ac_ref, r0, T, True, True)
            m3 = ax3 * ac3
            c = _conv_down(m3[:RC + 8], w_ref, K)
            dy = dy_ref[pl.ds(r0, RC), :]
            dab_ref[pl.ds(r0, RC), :] = (dy * c).astype(dab_ref.dtype)
            dc2 = _ext(dy_ref, r0, T, False, True) * _ext(ab_ref, r0, T, False, True)
            dm = None
            new = []
            for j in range(K):
                s = K - 1 - j
                t = _up(dc2, s, RC) * w_ref[j:j + 1, :]
                dm = t if dm is None else dm + t
                new.append(accs[j] + _fold8(dc2[:RC] * _down(m3[:RC + 8], s)))
            dax_ref[pl.ds(r0, RC), :] = (dm * ac3[8:RC + 8]).astype(dax_ref.dtype)
            dac_ref[pl.ds(r0, RC), :] = (dm * ax3[8:RC + 8]).astype(dac_ref.dtype)
            return tuple(new)

        accs = lax.fori_loop(0, T // RC, step, tuple(jnp.zeros((8, LANE), F32) for _ in range(K)))
        for j in range(K):
            dw_ref[j:j + 1, :] = jnp.sum(accs[j], axis=0, keepdims=True)

    col = pl.BlockSpec((T, LANE), lambda j: (0, j))
    wsp = pl.BlockSpec((K, LANE), lambda j: (0, j))
    return pl.pallas_call(
        body, name=name, grid=(nb,),
        in_specs=_col_specs(T, (0, nb, 2 * nb)) + [wsp, col],
        out_specs=[col, col, col, wsp],
        out_shape=[jax.ShapeDtypeStruct((T, CW), BF16)] * 3 + [jax.ShapeDtypeStruct((K, CW), F32)],
        compiler_params=_params(("parallel",)),
    )(proj, proj, proj, conv_w, dycat)


def _qkv_fwd(proj, conv_w, off, H, *, name):
    T = proj.shape[0]
    nb = 3 * H
    K = conv_w.shape[0]

    def body(x_ref, w_ref, y_ref):
        j = pl.program_id(0)
        is_qk = j < 2 * H
        scale = jnp.where(j < H, HEAD ** -0.5, 1.0).astype(F32)

        def step(i, carry):
            r0 = pl.multiple_of(i * RC, RC)
            s = _silu(_conv_down(_ext(x_ref, r0, T, True, False), w_ref, K))
            r = lax.rsqrt(jnp.sum(s * s, axis=-1, keepdims=True) + EPS) * scale
            y_ref[pl.ds(r0, RC), :] = s * jnp.where(is_qk, r, 1.0)
            return carry
        lax.fori_loop(0, T // RC, step, 0)

    return pl.pallas_call(
        body, name=name, grid=(nb,),
        in_specs=_col_specs(T, (off,)) + [pl.BlockSpec((K, LANE), lambda j: (0, j))],
        out_specs=pl.BlockSpec((T, LANE), lambda j: (0, j)),
        out_shape=jax.ShapeDtypeStruct((T, nb * LANE), F32), compiler_params=_params(("parallel",)),
    )(proj, conv_w)


def _qkv_bwd(proj, conv_w, dq, dk, dv, off, H, *, name):
    T = proj.shape[0]
    nb = 3 * H
    K = conv_w.shape[0]

    def body(x_ref, w_ref, dq_ref, dk_ref, dv_ref, dx_ref, dw_ref):
        j = pl.program_id(0)
        is_qk = j < 2 * H
        scale = jnp.where(j < H, HEAD ** -0.5, 1.0).astype(F32)

        def step(i, accs):
            r0 = pl.multiple_of(i * RC, RC)
            x3 = _ext(x_ref, r0, T, True, True)
            c2 = _conv_down(x3, w_ref, K)
            s2 = _silu(c2)
            dn2 = jnp.where(j < H, _ext(dq_ref, r0, T, False, True),
                            jnp.where(is_qk, _ext(dk_ref, r0, T, False, True), _ext(dv_ref, r0, T, False, True)))
            r = lax.rsqrt(jnp.sum(s2 * s2, axis=-1, keepdims=True) + EPS)
            nh = s2 * r
            dnp = dn2 * scale
            ds_qk = r * (dnp - nh * jnp.sum(dnp * nh, axis=-1, keepdims=True))
            ds2 = jnp.where(is_qk, ds_qk, dn2)
            dc2 = ds2 * _dsilu(c2)
            dx = None
            new = []
            for jj in range(K):
                s = K - 1 - jj
                t = _up(dc2, s, RC) * w_ref[jj:jj + 1, :]
                dx = t if dx is None else dx + t
                new.append(accs[jj] + _fold8(dc2[:RC] * _down(x3[:RC + 8], s)))
            dx_ref[pl.ds(r0, RC), :] = dx.astype(dx_ref.dtype)
            return tuple(new)

        accs = lax.fori_loop(0, T // RC, step, tuple(jnp.zeros((8, LANE), F32) for _ in range(K)))
        for jj in range(K):
            dw_ref[jj:jj + 1, :] = jnp.sum(accs[jj], axis=0, keepdims=True)

    col = pl.BlockSpec((T, LANE), lambda j: (0, j))
    wsp = pl.BlockSpec((K, LANE), lambda j: (0, j))
    return pl.pallas_call(
        body, name=name, grid=(nb,),
        in_specs=_col_specs(T, (off,)) + [wsp] + [
            pl.BlockSpec((T, LANE), functools.partial(lambda o, j: (0, jnp.clip(j - o, 0, H - 1)), o)) for o in (0, H, 2 * H)],
        out_specs=[col, wsp],
        out_shape=[jax.ShapeDtypeStruct((T, nb * LANE), BF16), jax.ShapeDtypeStruct((K, nb * LANE), F32)],
        compiler_params=_params(("parallel",)),
    )(proj, conv_w, dq, dk, dv)


def _softplus(x):
    return jnp.maximum(x, 0.0) + jnp.log(1.0 + jnp.exp(-jnp.abs(x)))


def _gates_fwd(proj, alog, dtb, off, H, *, name):
    T = proj.shape[0]
    tr = _tile(T, 512, 8)

    def body(ab_ref, al_ref, dt_ref, gb_ref):
        ab = ab_ref[...]
        lane = lax.broadcasted_iota(jnp.int32, ab.shape, 1)
        g = -jnp.exp(al_ref[...]) * _softplus(ab + dt_ref[...])
        gb_ref[...] = jnp.where(lane < H, g, jnp.where(lane < 2 * H, jax.nn.sigmoid(ab), 0.0))

    vec = pl.BlockSpec((1, LANE), lambda i: (0, 0))
    return pl.pallas_call(
        body, name=name, grid=(T // tr,),
        in_specs=[pl.BlockSpec((tr, LANE), lambda i: (i, off)), vec, vec],
        out_specs=pl.BlockSpec((tr, LANE), lambda i: (i, 0)),
        out_shape=jax.ShapeDtypeStruct((T, LANE), F32), compiler_params=_params(("parallel",)),
    )(proj, alog, dtb)


def _gates_bwd(proj, alog, dtb, dgb, off, H, *, name):
    T = proj.shape[0]
    tr = _tile(T, 512, 8)

    def body(ab_ref, al_ref, dt_ref, d_ref, dab_ref, dal_ref, ddt_ref):
        ab, d = ab_ref[...], d_ref[...]
        lane = lax.broadcasted_iota(jnp.int32, ab.shape, 1)
        z = ab + dt_ref[...]
        A = -jnp.exp(al_ref[...])
        da = d * A * jax.nn.sigmoid(z)
        beta = jax.nn.sigmoid(ab)
        db = d * beta * (1.0 - beta)
        is_g = lane < H
        dab_ref[...] = jnp.where(is_g, da, jnp.where(lane < 2 * H, db, 0.0)).astype(dab_ref.dtype)

        @pl.when(pl.program_id(0) == 0)
        def _():
            dal_ref[...] = jnp.zeros_like(dal_ref)
            ddt_ref[...] = jnp.zeros_like(ddt_ref)

        dal_ref[...] += jnp.sum(jnp.where(is_g, d * A * _softplus(z), 0.0), axis=0, keepdims=True)
        ddt_ref[...] += jnp.sum(jnp.where(is_g, da, 0.0), axis=0, keepdims=True)

    vec = pl.BlockSpec((1, LANE), lambda i: (0, 0))
    row = pl.BlockSpec((tr, LANE), lambda i: (i, 0))
    return pl.pallas_call(
        body, name=name, grid=(T // tr,),
        in_specs=[pl.BlockSpec((tr, LANE), lambda i: (i, off)), vec, vec, row],
        out_specs=[row, vec, vec],
        out_shape=[jax.ShapeDtypeStruct((T, LANE), BF16), jax.ShapeDtypeStruct((1, LANE), F32),
                   jax.ShapeDtypeStruct((1, LANE), F32)],
        compiler_params=_params(("arbitrary",)),
    )(proj, alog, dtb, dgb)


def _gated_norm_fwd(o, proj, gn, zoff, *, name):
    T, W = o.shape
    tr = _tile(T, 512, 8)

    def body(o_ref, z_ref, g_ref, y_ref):
        ov = o_ref[...]
        r = lax.rsqrt(jnp.mean(ov * ov, axis=-1, keepdims=True) + EPS)
        y_ref[...] = (ov * r * g_ref[...] * _silu(z_ref[...])).astype(y_ref.dtype)

    blk = pl.BlockSpec((tr, LANE), lambda i, j: (i, j))
    return pl.pallas_call(
        body, name=name, grid=(T // tr, W // LANE),
        in_specs=[blk, pl.BlockSpec((tr, LANE), lambda i, j: (i, zoff + j)), pl.BlockSpec((1, LANE), lambda i, j: (0, 0))],
        out_specs=blk, out_shape=jax.ShapeDtypeStruct((T, W), BF16), compiler_params=_params(("parallel", "parallel")),
    )(o, proj, gn)


def _gated_norm_bwd(o, proj, gn, dycat, zoff, yoff, *, name):
    T, W = o.shape
    tr = _tile(T, 512, 8)

    def body(o_ref, z_ref, g_ref, dy_ref, do_ref, dz_ref, dg_ref):
        ov, zv, gv, dy = o_ref[...], z_ref[...], g_ref[...], dy_ref[...]
        r = lax.rsqrt(jnp.mean(ov * ov, axis=-1, keepdims=True) + EPS)
        nh = ov * r
        s = _silu(zv)

        @pl.when((pl.program_id(0) == 0) & (pl.program_id(1) == 0))
        def _():
            dg_ref[...] = jnp.zeros_like(dg_ref)

        dg_ref[...] += jnp.sum(dy * nh * s, axis=0, keepdims=True)
        dz_ref[...] = (dy * nh * gv * _dsilu(zv)).astype(dz_ref.dtype)
        dn = dy * gv * s
        do_ref[...] = r * (dn - nh * jnp.mean(dn * nh, axis=-1, keepdims=True))

    blk = pl.BlockSpec((tr, LANE), lambda i, j: (i, j))
    vec = pl.BlockSpec((1, LANE), lambda i, j: (0, 0))
    return pl.pallas_call(
        body, name=name, grid=(T // tr, W // LANE),
        in_specs=[blk, pl.BlockSpec((tr, LANE), lambda i, j: (i, zoff + j)), vec,
                  pl.BlockSpec((tr, LANE), lambda i, j: (i, yoff + j))],
        out_specs=[blk, blk, vec],
        out_shape=[jax.ShapeDtypeStruct((T, W), F32), jax.ShapeDtypeStruct((T, W), BF16),
                   jax.ShapeDtypeStruct((1, LANE), F32)],
        compiler_params=_params(("arbitrary", "arbitrary")),
    )(o, proj, gn, dycat)


def _ffn_act_fwd(up_pre, conv_w, *, name):
    T, F2 = up_pre.shape
    nb = F2 // 2 // LANE
    K = conv_w.shape[0]

    def body(g_ref, v_ref, wg_ref, wv_ref, y_ref):
        def step(i, carry):
            r0 = pl.multiple_of(i * RC, RC)
            gate = _conv_down(_ext(g_ref, r0, T, True, False), wg_ref, K)
            val = _conv_down(_ext(v_ref, r0, T, True, False), wv_ref, K)
            y_ref[pl.ds(r0, RC), :] = (_silu(gate) * val).astype(y_ref.dtype)
            return carry
        lax.fori_loop(0, T // RC, step, 0)

    return pl.pallas_call(
        body, name=name, grid=(nb,),
        in_specs=_col_specs(T, (0, nb)) + [pl.BlockSpec((K, LANE), lambda j: (0, j)),
                                           pl.BlockSpec((K, LANE), lambda j: (0, nb + j))],
        out_specs=pl.BlockSpec((T, LANE), lambda j: (0, j)),
        out_shape=jax.ShapeDtypeStruct((T, F2 // 2), BF16), compiler_params=_params(("parallel",)),
    )(up_pre, up_pre, conv_w, conv_w)


def _ffn_act_bwd(up_pre, conv_w, dact, *, name):
    T, F2 = up_pre.shape
    nb = F2 // 2 // LANE
    K = conv_w.shape[0]

    def body(g_ref, v_ref, wg_ref, wv_ref, da_ref, dg_ref, dv_ref, dwg_ref, dwv_ref):
        def step(i, accs):
            r0 = pl.multiple_of(i * RC, RC)
            g3 = _ext(g_ref, r0, T, True, True)
            v3 = _ext(v_ref, r0, T, True, True)
            gate2 = _conv_down(g3, wg_ref, K)
            val2 = _conv_down(v3, wv_ref, K)
            da2 = _ext(da_ref, r0, T, False, True)
            dgate2 = da2 * val2 * _dsilu(gate2)
            dval2 = da2 * _silu(gate2)
            dgp, dvp, new = None, None, []
            for j in range(K):
                s = K - 1 - j
                tg = _up(dgate2, s, RC) * wg_ref[j:j + 1, :]
                tv = _up(dval2, s, RC) * wv_ref[j:j + 1, :]
                dgp = tg if dgp is None else dgp + tg
                dvp = tv if dvp is None else dvp + tv
                new.append(accs[2 * j] + _fold8(dgate2[:RC] * _down(g3[:RC + 8], s)))
                new.append(accs[2 * j + 1] + _fold8(dval2[:RC] * _down(v3[:RC + 8], s)))
            dg_ref[pl.ds(r0, RC), :] = dgp.astype(dg_ref.dtype)
            dv_ref[pl.ds(r0, RC), :] = dvp.astype(dv_ref.dtype)
            return tuple(new)

        accs = lax.fori_loop(0, T // RC, step, tuple(jnp.zeros((8, LANE), F32) for _ in range(2 * K)))
        for j in range(K):
            dwg_ref[j:j + 1, :] = jnp.sum(accs[2 * j], axis=0, keepdims=True)
            dwv_ref[j:j + 1, :] = jnp.sum(accs[2 * j + 1], axis=0, keepdims=True)

    col = pl.BlockSpec((T, LANE), lambda j: (0, j))
    wsp = pl.BlockSpec((K, LANE), lambda j: (0, j))
    return pl.pallas_call(
        body, name=name, grid=(nb,),
        in_specs=_col_specs(T, (0, nb)) + [wsp, pl.BlockSpec((K, LANE), lambda j: (0, nb + j)), col],
        out_specs=[col, col, wsp, wsp],
        out_shape=[jax.ShapeDtypeStruct((T, F2 // 2), BF16)] * 2 + [jax.ShapeDtypeStruct((K, F2 // 2), F32)] * 2,
        compiler_params=_params(("parallel",)),
    )(up_pre, up_pre, conv_w, conv_w, dact)


CPB = 8


def _dot(a, b):
    return jnp.dot(a, b, precision=HI, preferred_element_type=F32)


def _dot_nt(a, b):
    return lax.dot_general(a, b, _DN["nt"], precision=HI, preferred_element_type=F32)


def _dot_tn(a, b):
    return lax.dot_general(a, b, _DN["tn"], precision=HI, preferred_element_type=F32)


def _tri(strict=False, upper=False):
    r = lax.broadcasted_iota(jnp.int32, (CHUNK, CHUNK), 0)
    c = lax.broadcasted_iota(jnp.int32, (CHUNK, CHUNK), 1)
    if upper:
        return c >= r
    return (r > c) if strict else (r >= c)


def _chunk_decay(gb):
    gam = _dot(_tri().astype(F32), gb)
    diff = gam[:, :CHUNK] - gam.T[:CHUNK, :]
    D = jnp.exp(jnp.where(_tri(), diff, -1e30))
    return gam, D


def _delta_specs(T, H, cpb):
    rows = cpb * CHUNK
    col = lambda o: pl.BlockSpec((rows, LANE), functools.partial(lambda o, h, n: (n, o + h), o))
    bc = pl.BlockSpec((1, rows, LANE), lambda h, n: (h, n, 0))
    sq = pl.BlockSpec((1, cpb, CHUNK, CHUNK), lambda h, n: (h, n, 0, 0))
    vec = pl.BlockSpec((1, cpb, LANE), lambda h, n: (h, n, 0))
    return col, bc, sq, vec


def _delta_prep_fwd(qkv, gB, bB, H, *, name):
    T = qkv.shape[0]
    N = T // CHUNK
    cpb = _tile(N, CPB, 8)
    col, bc, sq, vec = _delta_specs(T, H, cpb)

    def body(q_ref, k_ref, v_ref, g_ref, b_ref, u_ref, w_ref, qd_ref, kd_ref, qk_ref, ti_ref, gl_ref):
        eye = (lax.broadcasted_iota(jnp.int32, (CHUNK, CHUNK), 0) == lax.broadcasted_iota(jnp.int32, (CHUNK, CHUNK), 1)).astype(F32)

        def step(c, carry):
            r0 = pl.multiple_of(c * CHUNK, CHUNK)
            rows = pl.ds(r0, CHUNK)
            q, k, v = q_ref[rows, :], k_ref[rows, :], v_ref[rows, :]
            bb = b_ref[0, rows, :]
            gam, D = _chunk_decay(g_ref[0, rows, :])
            e = jnp.exp(gam)
            L = jnp.where(_tri(strict=True), _dot_nt(k, k) * D, 0.0) * bb[:, :CHUNK]
            X = -L
            R = eye + X
            for _ in range(5):
                X = _dot(X, X)
                R = R + _dot(R, X)
            u_ref[rows, :] = _dot(R, bb * v)
            w_ref[rows, :] = _dot(R, bb * e * k)
            qd_ref[rows, :] = e * q
            glast = gam[CHUNK - 1:CHUNK, :]
            kd_ref[rows, :] = jnp.exp(glast - gam) * k
            qk_ref[0, c] = _dot_nt(q, k) * D
            ti_ref[0, c] = R
            gl_ref[0, pl.ds(c, 1), :] = jnp.exp(glast)
            return carry
        lax.fori_loop(0, cpb, step, 0)

    full = jax.ShapeDtypeStruct((T, H * LANE), F32)
    sqs = jax.ShapeDtypeStruct((H, N, CHUNK, CHUNK), F32)
    return pl.pallas_call(
        body, name=name, grid=(H, N // cpb),
        in_specs=[col(0), col(H), col(2 * H), bc, bc],
        out_specs=[col(0)] * 4 + [sq, sq, vec],
        out_shape=[full] * 4 + [sqs, sqs, jax.ShapeDtypeStruct((H, N, LANE), F32)],
        compiler_params=_params(("parallel", "parallel")),
    )(qkv, qkv, qkv, gB, bB)


def _delta_scan_fwd(u, w, qd, kd, qk, gl, H, *, name):
    T = u.shape[0]
    N = T // CHUNK
    cpb = _tile(N, CPB, 8)
    col, bc, sq, vec = _delta_specs(T, H, cpb)
    st = pl.BlockSpec((1, cpb, HEAD, HEAD), lambda h, n: (h, n, 0, 0))

    def body(u_ref, w_ref, qd_ref, kd_ref, qk_ref, gl_ref, o_ref, vn_ref, ss_ref, s_scr):
        @pl.when(pl.program_id(1) == 0)
        def _():
            s_scr[...] = jnp.zeros_like(s_scr)

        def step(c, S):
            rows = pl.ds(pl.multiple_of(c * CHUNK, CHUNK), CHUNK)
            ss_ref[0, c] = S
            vn = u_ref[rows, :] - _dot(w_ref[rows, :], S)
            o_ref[rows, :] = _dot(qd_ref[rows, :], S) + _dot(qk_ref[0, c], vn)
            vn_ref[rows, :] = vn
            return S * gl_ref[0, pl.ds(c, 1), :] + _dot_tn(kd_ref[rows, :], vn)
        s_scr[...] = lax.fori_loop(0, cpb, step, s_scr[...])

    full = jax.ShapeDtypeStruct((T, H * LANE), F32)
    return pl.pallas_call(
        body, name=name, grid=(H, N // cpb),
        in_specs=[col(0)] * 4 + [sq, vec],
        out_specs=[col(0), col(0), st],
        out_shape=[full, full, jax.ShapeDtypeStruct((H, N, HEAD, HEAD), F32)],
        scratch_shapes=[pltpu.VMEM((HEAD, HEAD), F32)],
        compiler_params=_params(("parallel", "arbitrary")),
    )(u, w, qd, kd, qk, gl)


def _delta_scan_bwd(do, w, qd, kd, vn, qk, gl, ss, H, *, name):
    T = do.shape[0]
    N = T // CHUNK
    cpb = _tile(N, CPB, 8)
    nbk = N // cpb
    rows_b = cpb * CHUNK
    col = lambda: pl.BlockSpec((rows_b, LANE), lambda h, n: (nbk - 1 - n, h))
    sq = pl.BlockSpec((1, cpb, CHUNK, CHUNK), lambda h, n: (h, nbk - 1 - n, 0, 0))
    vec = pl.BlockSpec((1, cpb, LANE), lambda h, n: (h, nbk - 1 - n, 0))
    st = pl.BlockSpec((1, cpb, HEAD, HEAD), lambda h, n: (h, nbk - 1 - n, 0, 0))

    def body(do_ref, w_ref, qd_ref, kd_ref, vn_ref, qk_ref, gl_ref, ss_ref,
             du_ref, dw_ref, dqd_ref, dkd_ref, dqk_ref, dgl_ref, ds_scr):
        @pl.when(pl.program_id(1) == 0)
        def _():
            ds_scr[...] = jnp.zeros_like(ds_scr)

        def step(i, dS):
            c = cpb - 1 - i
            rows = pl.ds(pl.multiple_of(c * CHUNK, CHUNK), CHUNK)
            S, dov, vnv = ss_ref[0, c], do_ref[rows, :], vn_ref[rows, :]
            dvn = _dot_tn(qk_ref[0, c], dov) + _dot(kd_ref[rows, :], dS)
            du_ref[rows, :] = dvn
            dw_ref[rows, :] = -_dot_nt(dvn, S)
            dqd_ref[rows, :] = _dot_nt(dov, S)
            dkd_ref[rows, :] = _dot_nt(vnv, dS)
            dqk_ref[0, c] = _dot_nt(dov, vnv)
            dgl = jnp.sum(jnp.sum(dS * S, axis=1, keepdims=True), axis=0, keepdims=True)
            dgl_ref[0, pl.ds(c, 1), :] = jnp.broadcast_to(dgl, (1, LANE))
            return (_dot_tn(qd_ref[rows, :], dov) + dS * gl_ref[0, pl.ds(c, 1), :]
                    - _dot_tn(w_ref[rows, :], dvn))
        ds_scr[...] = lax.fori_loop(0, cpb, step, ds_scr[...])

    full = jax.ShapeDtypeStruct((T, H * LANE), F32)
    return pl.pallas_call(
        body, name=name, grid=(H, nbk),
        in_specs=[col()] * 5 + [sq, vec, st],
        out_specs=[col()] * 4 + [sq, vec],
        out_shape=[full] * 4 + [jax.ShapeDtypeStruct((H, N, CHUNK, CHUNK), F32), jax.ShapeDtypeStruct((H, N, LANE), F32)],
        scratch_shapes=[pltpu.VMEM((HEAD, HEAD), F32)],
        compiler_params=_params(("parallel", "arbitrary")),
    )(do, w, qd, kd, vn, qk, gl, ss)


def _delta_prep_bwd(qkv, gB, bB, ti, u, w, qk, du, dw, dqd, dkd, dqk, dgl, H, *, name):
    T = qkv.shape[0]
    N = T // CHUNK
    cpb = _tile(N, CPB, 8)
    col, bc, sq, vec = _delta_specs(T, H, cpb)

    def body(q_ref, k_ref, v_ref, g_ref, b_ref, ti_ref, u_ref, w_ref, qk_ref,
             du_ref, dw_ref, dqd_ref, dkd_ref, dqk_ref, dgl_ref,
             dq_ref, dk_ref, dv_ref, dg_ref, db_ref):
        ones = jnp.ones((CHUNK, LANE), F32)
        lsum = lambda x: jnp.sum(x, axis=-1, keepdims=True)

        def step(c, carry):
            r0 = pl.multiple_of(c * CHUNK, CHUNK)
            rows = pl.ds(r0, CHUNK)
            q, k, v = q_ref[rows, :], k_ref[rows, :], v_ref[rows, :]
            bb = b_ref[0, rows, :]
            gam, D = _chunk_decay(g_ref[0, rows, :])
            e = jnp.exp(gam)
            glast = gam[CHUNK - 1:CHUNK, :]
            eL = jnp.exp(glast - gam)
            gl = jnp.exp(glast)
            Ti, uv, wv, QK = ti_ref[0, c], u_ref[rows, :], w_ref[rows, :], qk_ref[0, c]
            duv, dwv, dqd_v, dkd_v, dqk_v = du_ref[rows, :], dw_ref[rows, :], dqd_ref[rows, :], dkd_ref[rows, :], dqk_ref[0, c]
            KKD = jnp.where(_tri(strict=True), _dot_nt(k, k) * D, 0.0)
            rw = bb * e * k
            dru = _dot_tn(Ti, duv)
            drw = _dot_tn(Ti, dwv)
            dL = jnp.where(_tri(strict=True), -(_dot_nt(dru, uv) + _dot_nt(drw, wv)), 0.0)
            Mm = dL * bb[:, :CHUNK]
            dKK = Mm * D
            dQK = dqk_v * D
            P = Mm * KKD + dqk_v * QK
            dq_ref[rows, :] = _dot(dQK, k) + e * dqd_v
            dk_ref[rows, :] = (_dot_tn(dQK, q) + _dot(dKK, k) + _dot_tn(dKK, k) + bb * e * drw + eL * dkd_v)
            dv_ref[rows, :] = bb * dru
            db = _dot(dL * KKD, ones) + lsum(dru * v) + lsum(drw * e * k)
            kdv = eL * k
            dgam = (_dot(P, ones) - _dot_tn(P, ones) + lsum(drw * rw) + lsum(dqd_v * e * q) - lsum(dkd_v * kdv))
            xlast = jnp.sum(lsum(dkd_v * kdv), axis=0, keepdims=True) + gl * dgl_ref[0, pl.ds(c, 1), :]
            dg_ref[0, rows, :] = _dot(_tri(upper=True).astype(F32), dgam) + xlast
            db_ref[0, rows, :] = db
            return carry
        lax.fori_loop(0, cpb, step, 0)

    full = jax.ShapeDtypeStruct((T, H * LANE), F32)
    bcs = jax.ShapeDtypeStruct((H, T, LANE), F32)
    return pl.pallas_call(
        body, name=name, grid=(H, N // cpb),
        in_specs=[col(0), col(H), col(2 * H), bc, bc, sq, col(0), col(0), sq, col(0), col(0), col(0), col(0), sq, vec],
        out_specs=[col(0), col(0), col(0), bc, bc],
        out_shape=[full, full, full, bcs, bcs],
        compiler_params=_params(("parallel", "parallel")),
    )(qkv, qkv, qkv, gB, bB, ti, u, w, qk, du, dw, dqd, dkd, dqk, dgl)


def _adam(parts, w, m, v, *, name):
    P, R, C = parts.shape
    tr = _tile(R, 256, 8)

    def body(p_ref, w_ref, m_ref, v_ref, g_ref, d_ref, nm_ref, nv_ref):
        g = p_ref[0].astype(F32)
        for i in range(1, P):
            g = g + p_ref[i].astype(F32)
        mn = ADAM_B1 * m_ref[...] + (1.0 - ADAM_B1) * g
        vn = ADAM_B2 * v_ref[...] + (1.0 - ADAM_B2) * (g * g)
        m_hat = mn / (1.0 - ADAM_B1 ** ADAM_STEP)
        v_hat = vn / (1.0 - ADAM_B2 ** ADAM_STEP)
        g_ref[...] = g
        d_ref[...] = -ADAM_LR * (m_hat / (jnp.sqrt(v_hat) + ADAM_EPS) + ADAM_WD * w_ref[...])
        nm_ref[...] = mn
        nv_ref[...] = vn

    blk = pl.BlockSpec((tr, C), lambda i: (i, 0))
    return pl.pallas_call(
        body, name=name, grid=(R // tr,),
        in_specs=[pl.BlockSpec((P, tr, C), lambda i: (0, i, 0)), blk, blk, blk],
        out_specs=[blk] * 4, out_shape=[jax.ShapeDtypeStruct((R, C), F32)] * 4,
        compiler_params=_params(("parallel",)),
    )(parts, w, m, v)


def _mesh_pos():
    return lax.axis_index("x"), lax.axis_index("y"), lax.axis_index("c")


def _peer(k):
    x, y, c = _mesh_pos()
    px, py, pc = x ^ ((k >> 2) & 1), y ^ ((k >> 1) & 1), c ^ (k & 1)
    return (px, py, pc), 4 * px + 2 * py + pc


def _exchange(arrays, scatter, *, name):
    n = len(arrays)
    blocks = [a.shape[1:] if scatter else a.shape for a in arrays]

    def body(*refs):
        srcs, dsts = refs[:n], refs[n:2 * n]
        send_sems, recv_sems, local_sems = refs[2 * n:]
        x, y, c = _mesh_pos()
        me = 4 * x + 2 * y + c
        local, sends = [], []
        for a in range(n):
            cp = pltpu.make_async_copy(srcs[a].at[me] if scatter else srcs[a], dsts[a].at[me], local_sems.at[a])
            cp.start()
            local.append(cp)
            for k in range(1, N_DEV):
                dev, idx = _peer(k)
                cp = pltpu.make_async_remote_copy(
                    src_ref=srcs[a].at[idx] if scatter else srcs[a], dst_ref=dsts[a].at[me],
                    send_sem=send_sems.at[a * N_DEV + k], recv_sem=recv_sems.at[a * N_DEV + k],
                    device_id=dev, device_id_type=MESH)
                cp.start()
                sends.append(cp)
        for a in range(n):
            for k in range(1, N_DEV):
                dev, idx = _peer(k)
                pltpu.make_async_remote_copy(
                    src_ref=srcs[a].at[idx] if scatter else srcs[a], dst_ref=dsts[a].at[idx],
                    send_sem=send_sems.at[a * N_DEV + k], recv_sem=recv_sems.at[a * N_DEV + k],
                    device_id=dev, device_id_type=MESH).wait_recv()
        for cp in sends:
            cp.wait_send()
        for cp in local:
            cp.wait()

    anyspec = pl.BlockSpec(memory_space=pl.ANY)
    return pl.pallas_call(
        body, name=name, in_specs=[anyspec] * n, out_specs=[anyspec] * n,
        out_shape=[jax.ShapeDtypeStruct((N_DEV,) + tuple(b), a.dtype) for a, b in zip(arrays, blocks)],
        scratch_shapes=[pltpu.SemaphoreType.DMA((n * N_DEV,)), pltpu.SemaphoreType.DMA((n * N_DEV,)),
                        pltpu.SemaphoreType.DMA((n,))],
    )(*arrays)


_ANY = pl.BlockSpec(memory_space=pl.ANY)
_SEM = pl.BlockSpec(memory_space=pltpu.SEMAPHORE)
_EFFECT = pltpu.SideEffectType.DATAFLOW_SIDE_EFFECTING


def _in_hbm(a):
    return pltpu.with_memory_space_constraint(a, pltpu.HBM)


def _place_own(arrays, scatter, *, name):
    n = len(arrays)
    blocks = [a.shape[1:] if scatter else a.shape for a in arrays]

    def body(*refs):
        srcs, dsts, sems = refs[:n], refs[n:2 * n], refs[2 * n]
        x, y, c = _mesh_pos()
        me = 4 * x + 2 * y + c
        cps = [pltpu.make_async_copy(srcs[a].at[me] if scatter else srcs[a], dsts[a].at[me], sems.at[a]) for a in range(n)]
        for cp in cps:
            cp.start()
        for cp in cps:
            cp.wait()

    return pl.pallas_call(
        body, name=name, in_specs=[_ANY] * n, out_specs=[_ANY] * n,
        out_shape=[jax.ShapeDtypeStruct((N_DEV,) + tuple(b), a.dtype) for a, b in zip(arrays, blocks)],
        scratch_shapes=[pltpu.SemaphoreType.DMA((n,))],
    )(*arrays)


def _split_copy(src, land, send, recv, k, me, scatter, landed):
    dev, idx = _peer(k)
    return pltpu.make_async_remote_copy(
        src_ref=src.at[idx] if scatter else src, dst_ref=land.at[idx if landed else me],
        send_sem=send.at[k], recv_sem=recv.at[k], device_id=dev, device_id_type=MESH)


def _split_start(srcs, lands, scatter, *, name):
    n = len(srcs)

    def body(*refs):
        src, land, send, recv, token = refs[:n], refs[n:2 * n], refs[2 * n:3 * n], refs[3 * n:4 * n], refs[-1]
        x, y, c = _mesh_pos()
        me = 4 * x + 2 * y + c
        for a in range(n):
            for k in range(1, N_DEV):
                _split_copy(src[a], land[a], send[a], recv[a], k, me, scatter, False).start()
        token[...] = jnp.zeros_like(token)

    outs = pl.pallas_call(
        body, name=name,
        out_shape=[pltpu.SemaphoreType.DMA((N_DEV,))] * (2 * n) + [pltpu.HBM(t.shape, t.dtype) for t in list(srcs) + list(lands)]
        + [jax.ShapeDtypeStruct((8, LANE), F32)],
        in_specs=[_ANY] * (2 * n), out_specs=[_SEM] * (2 * n) + [_ANY] * (2 * n) + [pl.BlockSpec(memory_space=pltpu.VMEM)],
        input_output_aliases={i: 2 * n + i for i in range(2 * n)},
        compiler_params=pltpu.CompilerParams(has_side_effects=_EFFECT),
    )(*[_in_hbm(t) for t in list(srcs) + list(lands)])
    handles = [(outs[a], outs[n + a], outs[2 * n + a], outs[3 * n + a]) for a in range(n)]
    return handles, outs[-1]


def _split_wait(handle, after, scatter, *, name):
    send, recv, src_thru, land_thru = handle

    def body(src_ref, land_ref, send_ref, recv_ref, after_ref, src_out, land_out):
        x, y, c = _mesh_pos()
        me = 4 * x + 2 * y + c
        for k in range(1, N_DEV):
            cp = _split_copy(src_ref, land_ref, send_ref, recv_ref, k, me, scatter, True)
            cp.wait_send()
            cp.wait_recv()

    return pl.pallas_call(
        body, name=name,
        out_shape=(pltpu.HBM(src_thru.shape, src_thru.dtype), pltpu.HBM(land_thru.shape, land_thru.dtype)),
        in_specs=(_ANY, _ANY, _SEM, _SEM, _ANY), out_specs=(_ANY, _ANY), input_output_aliases={0: 0, 1: 1},
        compiler_params=pltpu.CompilerParams(has_side_effects=_EFFECT),
    )(src_thru, land_thru, send, recv, after)[1]


def _local_step(x, p, tgt, S, wt, conv, emit):
    T, D = x.shape
    CW = DNW = D // 2
    H = DNW // HEAD
    nA, nD = CW // LANE, DNW // LANE
    qkv_off, z_off, ab_off = 3 * nA, 3 * nA + 3 * nD, 3 * nA + 4 * nD
    alog = jnp.pad(S["a_log"], ((0, 0), (0, LANE - H)))
    dtb = jnp.pad(S["dt_bias"], ((0, 0), (0, LANE - H)))
    add = lambda acc, r: (acc + r,)

    h1 = _rms_fwd(x, S["g_mix"], name="rms1_fwd")
    w_in, cv = wt("w_in", h1), conv(h1)
    proj = _matmul(h1, w_in, "nn", name="mm_in")
    y_a = _group_a_fwd(proj, cv["conv_a"], CW, name="group_a_fwd")
    qkv = _qkv_fwd(proj, cv["conv_qkv"], qkv_off, H, name="qkv_fwd")
    gb = _gates_fwd(proj, alog, dtb, ab_off, H, name="gates_fwd")
    bcast = lambda cols: jnp.broadcast_to(cols.T[:, :, None], (H, T, LANE))
    gB, bB = bcast(gb[:, :H]), bcast(gb[:, H:2 * H])
    u, w, qd, kd, qk, ti, gl = _delta_prep_fwd(qkv, gB, bB, H, name="delta_prep_fwd")
    o, vn, ss = _delta_scan_fwd(u, w, qd, kd, qk, gl, H, name="delta_scan_fwd")
    y_b = _gated_norm_fwd(o, proj, S["dn_g"], z_off, name="gated_norm_fwd")
    ycat = jnp.concatenate([y_a, y_b], axis=1)
    w_out = wt("w_out", ycat)
    x1 = _matmul(ycat, w_out, "nn", name="mm_out", epilogue=add, extras=(x,))
    h2 = _rms_fwd(x1, S["g_ffn"], name="rms2_fwd")
    w_up = wt("w_up", h2)
    up_pre = _matmul(h2, w_up, "nn", name="mm_up")
    act = _ffn_act_fwd(up_pre, cv["conv_ffn"], name="ffn_act_fwd")
    w_down = wt("w_down", act)
    x2 = _matmul(act, w_down, "nn", name="mm_down", epilogue=add, extras=(x1,))
    h3 = _rms_fwd(x2, S["g_ple"], name="rms3_fwd")
    w_pp, w_pg = wt("w_pp", h3), wt("w_pg", h3)
    pp = _matmul(p, w_pp, "nn", name="mm_pp")

    def ple_epi(acc, x2r, ppr):
        s = jax.nn.sigmoid(acc)
        return x2r + s * ppr, s

    x3, sg = _matmul(h3, w_pg, "nn", name="mm_pg", out_dtypes=(F32, F32), epilogue=ple_epi, extras=(x2, pp))
    dx3, dg_final, loss = _final_loss(x3, S["g_final"], tgt, name="final_loss")

    G = {"g_final": dg_final}
    dpg, dpp = _ple_bwd(dx3, pp, sg, name="ple_bwd")
    tok = emit({"w_pp": _matmul(p, dpp, "tn", name="mm_dwpp", out_dtypes=(BF16,)),
                "w_pg": _matmul(h3, dpg, "tn", name="mm_dwpg", out_dtypes=(BF16,))})
    dh3 = _matmul(dpg, w_pg, "nt", name="mm_dh3", after=tok)
    dx2, G["g_ple"] = _rms_bwd(x2, S["g_ple"], dh3, dx3, name="rms3_bwd")
    tok = emit({"w_down": _matmul(act, dx2, "tn", name="mm_dwdown", out_dtypes=(BF16,))})
    dact = _matmul(dx2, w_down, "nt", name="mm_dact", after=tok)
    dup_g, dup_v, dcf_g, dcf_v = _ffn_act_bwd(up_pre, cv["conv_ffn"], dact, name="ffn_act_bwd")
    G["conv_ffn"] = jnp.concatenate([dcf_g, dcf_v], axis=1)
    dup = jnp.concatenate([dup_g, dup_v], axis=1)
    tok = emit({"w_up": _matmul(h2, dup, "tn", name="mm_dwup", out_dtypes=(BF16,))})
    dh2 = _matmul(dup, w_up, "nt", name="mm_dh2", after=tok)
    dx1, G["g_ffn"] = _rms_bwd(x1, S["g_ffn"], dh2, dx2, name="rms2_bwd")
    tok = emit({"w_out": _matmul(ycat, dx1, "tn", name="mm_dwout", out_dtypes=(BF16,))})
    dycat = _matmul(dx1, w_out, "nt", name="mm_dycat", after=tok)
    do, dz, G["dn_g"] = _gated_norm_bwd(o, proj, S["dn_g"], dycat, z_off, nA, name="gated_norm_bwd")
    du, dw, dqd, dkd, dqk, dgl = _delta_scan_bwd(do, w, qd, kd, vn, qk, gl, ss, H, name="delta_scan_bwd")
    dq, dk, dv, dgB, dbB = _delta_prep_bwd(qkv, gB, bB, ti, u, w, qk, du, dw, dqd, dkd, dqk, dgl, H,
                                           name="delta_prep_bwd")
    dgb = jnp.pad(jnp.concatenate([dgB[:, :, 0].T, dbB[:, :, 0].T], axis=1), ((0, 0), (0, LANE - 2 * H)))
    dab, dal, ddt = _gates_bwd(proj, alog, dtb, dgb, ab_off, H, name="gates_bwd")
    G["a_log"], G["dt_bias"] = dal[:, :H], ddt[:, :H]
    dqkv, G["conv_qkv"] = _qkv_bwd(proj, cv["conv_qkv"], dq, dk, dv, qkv_off, H, name="qkv_bwd")
    dax, dab_, dac, G["conv_a"] = _group_a_bwd(proj, cv["conv_a"], dycat, CW, name="group_a_bwd")
    in_p = w_in.shape[1]
    dproj = jnp.concatenate([dax, dab_, dac, dqkv, dz, dab, jnp.zeros((T, in_p - (ab_off + 1) * LANE), BF16)], axis=1)
    tok = emit({"w_in": _matmul(h1, dproj, "tn", name="mm_dwin", out_dtypes=(BF16,))})
    dh1 = _matmul(dproj, w_in, "nt", name="mm_dh1", after=tok)
    grad_x, G["g_mix"] = _rms_bwd(x, S["g_mix"], dh1, dx1, name="rms1_bwd")
    return loss, grad_x, G


def _pad_cols(a, n):
    return jnp.pad(a, ((0, 0), (0, n - a.shape[1])))


def _col_sharded(landed):
    _, R, C = landed.shape
    return jnp.transpose(landed, (1, 0, 2)).reshape(R, N_DEV * C)


def _col_parts(full):
    R, C8 = full.shape
    return jnp.transpose(full.reshape(R, N_DEV, C8 // N_DEV), (1, 0, 2))


def kernel(x, p, norm_mix_g, w_in, conv_a_w, conv_qkv_w, a_log, dt_bias, dn_norm_g, w_out, norm_ffn_g, w_up, conv_ffn_w, w_down, norm_ple_g, w_ple_gate, w_ple_proj, final_norm_g, loss_target, m_norm_mix_g, m_w_in, m_conv_a_w, m_conv_qkv_w, m_a_log, m_dt_bias, m_dn_norm_g, m_w_out, m_norm_ffn_g, m_w_up, m_conv_ffn_w, m_w_down, m_norm_ple_g, m_w_ple_gate, m_w_ple_proj, m_final_norm_g, v_norm_mix_g, v_w_in, v_conv_a_w, v_conv_qkv_w, v_a_log, v_dt_bias, v_dn_norm_g, v_w_out, v_norm_ffn_g, v_w_up, v_conv_ffn_w, v_w_down, v_norm_ple_g, v_w_ple_gate, v_w_ple_proj, v_final_norm_g):
    T, D = x.shape[1], x.shape[2]
    xd, _, cd = _mesh_pos()
    me = 4 * xd + 2 * lax.axis_index("y") + cd

    conv_sh = [conv_a_w[0], conv_qkv_w[0], conv_ffn_w[0]]
    conv_n = [c.size for c in conv_sh]
    pack_rows = -(-sum(conv_n) // LANE)
    conv_pack = jnp.pad(jnp.concatenate([c.reshape(-1) for c in conv_sh]), (0, pack_rows * LANE - sum(conv_n))).reshape(pack_rows, LANE)
    names = ["w_in", "conv", "w_out", "w_up", "w_down", "w_pg", "w_pp"]
    shards = [w_in[0].astype(BF16), conv_pack, w_out[0].astype(BF16), w_up[0].astype(BF16), w_down[0].astype(BF16),
              w_ple_gate[0].astype(BF16), w_ple_proj[0].astype(BF16)]
    handles, tok0 = _split_start(shards, _place_own(shards, False, name="place_weights"), False, name="gather_start")
    handle = dict(zip(names, handles))
    in_cols = N_DEV * w_in.shape[2]
    in_p = (in_cols // LANE) * LANE + AB_PAD
    col_sharded = {"w_in", "w_up", "w_pp"}

    def wt(name, after):
        landed = _split_wait(handle[name], after, False, name="gather_wait_" + name)
        full = _col_sharded(landed) if name in col_sharded else landed.reshape(-1, D)
        return _pad_cols(full, in_p) if name == "w_in" else full

    def conv(after):
        flat = _split_wait(handle["conv"], after, False, name="gather_wait_conv").reshape(N_DEV, pack_rows * LANE)
        out, o_ = {}, 0
        for nm, c, n_ in zip(("conv_a", "conv_qkv", "conv_ffn"), conv_sh, conv_n):
            out[nm] = _col_sharded(flat[:, o_:o_ + n_].reshape((N_DEV,) + c.shape))
            o_ += n_
        return out

    pending = {}

    def emit(grads):
        parts = [_col_parts(g[:, :in_cols] if nm == "w_in" else g) if nm in col_sharded else g.reshape(N_DEV, -1, D)
                 for nm, g in grads.items()]
        hs, tok = _split_start(parts, _place_own(parts, True, name="place_d" + "_".join(grads)), True,
                               name="scatter_start_" + "_".join(grads))
        pending.update(zip(grads, hs))
        return tok

    S = {
        "g_mix": norm_mix_g + tok0[0, 0], "a_log": a_log, "dt_bias": dt_bias, "dn_g": dn_norm_g, "g_ffn": norm_ffn_g,
        "g_ple": norm_ple_g, "g_final": final_norm_g.reshape(1, D),
    }

    loss_v, grad_x, G = _local_step(x[0], p[0, 0], loss_target[0], S, wt, conv, emit)
    loss = lax.psum(loss_v[0, 0], ("x", "y", "c"))

    small_names = ["g_mix", "g_ffn", "g_ple", "g_final", "dn_g", "a_log", "dt_bias", "conv_a", "conv_qkv", "conv_ffn"]
    small_rows, pieces = [], []
    for nm in small_names:
        g_ = G[nm].reshape(-1)
        r_ = -(-g_.size // (8 * LANE)) * 8
        small_rows.append(r_)
        pieces.append(jnp.pad(g_, (0, r_ * LANE - g_.size)).reshape(r_, LANE))
    (small_l,) = _exchange([jnp.concatenate(pieces, axis=0)], False, name="gather_small_grads")
    landed = {nm: _split_wait(h_, grad_x, True, name="scatter_wait_" + nm) for nm, h_ in pending.items()}
    big_l = [landed[nm] for nm in ("w_in", "w_out", "w_up", "w_down", "w_pg", "w_pp")]

    def small_parts(nm):
        i = small_names.index(nm)
        r0 = sum(small_rows[:i])
        shp = G[nm].shape
        return small_l[:, r0:r0 + small_rows[i], :].reshape(N_DEV, -1)[:, :G[nm].size].reshape((N_DEV,) + shp)

    def conv_parts(nm, shard):
        full = small_parts(nm)
        C = shard.shape[-1]
        return lax.dynamic_slice_in_dim(full, me * C, C, axis=2)

    def adam(parts, w_, m_, v_, nm):
        shp = w_.shape
        w2, m2, v2 = (t.reshape(parts.shape[1:]) for t in (w_, m_, v_))
        return tuple(t.reshape(shp) for t in _adam(parts, w2, m2, v2, name="adam_" + nm))

    res = [
        adam(small_parts("g_mix"), norm_mix_g, m_norm_mix_g, v_norm_mix_g, "norm_mix_g"),
        adam(big_l[0], w_in, m_w_in, v_w_in, "w_in"),
        adam(conv_parts("conv_a", conv_a_w), conv_a_w, m_conv_a_w, v_conv_a_w, "conv_a_w"),
        adam(conv_parts("conv_qkv", conv_qkv_w), conv_qkv_w, m_conv_qkv_w, v_conv_qkv_w, "conv_qkv_w"),
        adam(small_parts("a_log"), a_log, m_a_log, v_a_log, "a_log"),
        adam(small_parts("dt_bias"), dt_bias, m_dt_bias, v_dt_bias, "dt_bias"),
        adam(small_parts("dn_g"), dn_norm_g, m_dn_norm_g, v_dn_norm_g, "dn_norm_g"),
        adam(big_l[1], w_out, m_w_out, v_w_out, "w_out"),
        adam(small_parts("g_ffn"), norm_ffn_g, m_norm_ffn_g, v_norm_ffn_g, "norm_ffn_g"),
        adam(big_l[2], w_up, m_w_up, v_w_up, "w_up"),
        adam(conv_parts("conv_ffn", conv_ffn_w), conv_ffn_w, m_conv_ffn_w, v_conv_ffn_w, "conv_ffn_w"),
        adam(big_l[3], w_down, m_w_down, v_w_down, "w_down"),
        adam(small_parts("g_ple"), norm_ple_g, m_norm_ple_g, v_norm_ple_g, "norm_ple_g"),
        adam(big_l[4], w_ple_gate, m_w_ple_gate, v_w_ple_gate, "w_ple_gate"),
        adam(big_l[5], w_ple_proj, m_w_ple_proj, v_w_ple_proj, "w_ple_proj"),
        adam(small_parts("g_final"), final_norm_g.reshape(1, D), m_final_norm_g.reshape(1, D),
             v_final_norm_g.reshape(1, D), "final_norm_g"),
    ]
    res[-1] = tuple(t.reshape(D) for t in res[-1])
    grads, deltas, new_m, new_v = zip(*res)
    return (loss, grad_x[None], *grads, *deltas, *new_m, *new_v)
```

```python
import functools

import jax
import jax.numpy as jnp
from jax import lax
from jax.experimental import pallas as pl
from jax.experimental.pallas import tpu as pltpu

F32 = jnp.float32
BF16 = jnp.bfloat16
HI = lax.Precision.HIGHEST

EPS = 1e-6
CHUNK = 64
HEAD = 128
LANE = 128
N_DEV = 8
AB_PAD = 512

ADAM_LR = 0.001
ADAM_B1 = 0.9
ADAM_B2 = 0.999
ADAM_EPS = 1e-08
ADAM_WD = 0.01
ADAM_STEP = 10

MESH = pl.DeviceIdType.MESH


def _tile(dim, target, align=LANE):
    if dim <= target:
        return dim
    t = (target // align) * align
    while t > align and dim % t:
        t -= align
    assert dim % t == 0, (dim, target)
    return t


def _params(sem, vmem_mb=48):
    return pltpu.CompilerParams(dimension_semantics=sem, vmem_limit_bytes=vmem_mb << 20)


_DN = {"nn": (((1,), (0,)), ((), ())), "nt": (((1,), (1,)), ((), ())), "tn": (((0,), (0,)), ((), ()))}


def _matmul(a, b, mode, *, name, out_dtypes=(F32,), epilogue=None, extras=(), after=None, tm=1024, tn=1024, tk=512):
    if mode == "nn":
        (M, K), (K2, N) = a.shape, b.shape
    elif mode == "nt":
        (M, K), (N, K2) = a.shape, b.shape
    else:
        (K, M), (K2, N) = a.shape, b.shape
    assert K == K2, (name, a.shape, b.shape)
    tm, tn, tk = _tile(M, tm), _tile(N, tn), _tile(K, tk)
    nk = K // tk
    n_ex, n_out = len(extras), len(out_dtypes)
    dn = _DN[mode]

    n_tok = 0 if after is None else 1

    def body(a_ref, b_ref, *rest):
        rest = rest[n_tok:]
        ex_refs, out_refs, acc = rest[:n_ex], rest[n_ex:n_ex + n_out], rest[-1]
        k = pl.program_id(2)

        @pl.when(k == 0)
        def _():
            acc[...] = jnp.zeros_like(acc)

        acc[...] += lax.dot_general(a_ref[...].astype(BF16), b_ref[...].astype(BF16), dn,
                                    preferred_element_type=F32)

        @pl.when(k == nk - 1)
        def _():
            res = acc[...]
            outs = (res,) if epilogue is None else epilogue(res, *[e[...] for e in ex_refs])
            for o_ref, val in zip(out_refs, outs):
                o_ref[...] = val.astype(o_ref.dtype)

    a_spec = pl.BlockSpec((tk, tm), lambda i, j, k: (k, i)) if mode == "tn" else pl.BlockSpec((tm, tk), lambda i, j, k: (i, k))
    b_spec = pl.BlockSpec((tn, tk), lambda i, j, k: (j, k)) if mode == "nt" else pl.BlockSpec((tk, tn), lambda i, j, k: (k, j))
    mn_spec = pl.BlockSpec((tm, tn), lambda i, j, k: (i, j))
    outs = pl.pallas_call(
        body, name=name, grid=(M // tm, N // tn, nk),
        in_specs=[a_spec, b_spec] + [pl.BlockSpec((8, LANE), lambda i, j, k: (0, 0))] * n_tok + [mn_spec] * n_ex,
        out_specs=[mn_spec] * n_out,
        out_shape=[jax.ShapeDtypeStruct((M, N), dt) for dt in out_dtypes],
        scratch_shapes=[pltpu.VMEM((tm, tn), F32)],
        compiler_params=_params(("parallel", "parallel", "arbitrary"), 56),
    )(a, b, *([] if after is None else [after]), *extras)
    return outs[0] if n_out == 1 else outs


def _rms_fwd(x, g, *, name):
    T, D = x.shape
    tr = _tile(T, 256, 8)

    def body(x_ref, g_ref, h_ref):
        xv = x_ref[...]
        r = lax.rsqrt(jnp.mean(xv * xv, axis=-1, keepdims=True) + EPS)
        h_ref[...] = (xv * r * g_ref[...]).astype(h_ref.dtype)

    return pl.pallas_call(
        body, name=name, grid=(T // tr,),
        in_specs=[pl.BlockSpec((tr, D), lambda i: (i, 0)), pl.BlockSpec((1, D), lambda i: (0, 0))],
        out_specs=pl.BlockSpec((tr, D), lambda i: (i, 0)),
        out_shape=jax.ShapeDtypeStruct((T, D), BF16),
        compiler_params=_params(("parallel",)),
    )(x, g)


def _rms_bwd(x, g, dh, dres, *, name):
    T, D = x.shape
    tr = _tile(T, 256, 8)

    def body(x_ref, g_ref, dh_ref, dres_ref, dx_ref, dg_ref):
        xv = x_ref[...]
        r = lax.rsqrt(jnp.mean(xv * xv, axis=-1, keepdims=True) + EPS)
        xh = xv * r
        dh = dh_ref[...]

        @pl.when(pl.program_id(0) == 0)
        def _():
            dg_ref[...] = jnp.zeros_like(dg_ref)

        dg_ref[...] += jnp.sum(dh * xh, axis=0, keepdims=True)
        dxh = dh * g_ref[...]
        dx_ref[...] = dres_ref[...] + r * (dxh - xh * jnp.mean(dxh * xh, axis=-1, keepdims=True))

    row = pl.BlockSpec((tr, D), lambda i: (i, 0))
    vec = pl.BlockSpec((1, D), lambda i: (0, 0))
    return pl.pallas_call(
        body, name=name, grid=(T // tr,),
        in_specs=[row, vec, row, row], out_specs=[row, vec],
        out_shape=[jax.ShapeDtypeStruct((T, D), F32), jax.ShapeDtypeStruct((1, D), F32)],
        compiler_params=_params(("arbitrary",)),
    )(x, g, dh, dres)


def _final_loss(x, g, tgt, *, name):
    T, D = x.shape
    tr = _tile(T, 256, 8)

    def body(x_ref, g_ref, t_ref, dx_ref, dg_ref, loss_ref):
        xv = x_ref[...]
        r = lax.rsqrt(jnp.mean(xv * xv, axis=-1, keepdims=True) + EPS)
        xh = xv * r
        gv = g_ref[...]
        err = xh * gv - t_ref[...]

        @pl.when(pl.program_id(0) == 0)
        def _():
            dg_ref[...] = jnp.zeros_like(dg_ref)
            loss_ref[...] = jnp.zeros_like(loss_ref)

        part = 0.5 * jnp.sum(jnp.mean(err * err, axis=-1, keepdims=True), axis=0, keepdims=True)
        loss_ref[...] += jnp.broadcast_to(part, loss_ref.shape)
        dy = err * (1.0 / D)
        dg_ref[...] += jnp.sum(dy * xh, axis=0, keepdims=True)
        dxh = dy * gv
        dx_ref[...] = r * (dxh - xh * jnp.mean(dxh * xh, axis=-1, keepdims=True))

    row = pl.BlockSpec((tr, D), lambda i: (i, 0))
    vec = pl.BlockSpec((1, D), lambda i: (0, 0))
    return pl.pallas_call(
        body, name=name, grid=(T // tr,),
        in_specs=[row, vec, row], out_specs=[row, vec, pl.BlockSpec((1, LANE), lambda i: (0, 0))],
        out_shape=[jax.ShapeDtypeStruct((T, D), F32), jax.ShapeDtypeStruct((1, D), F32),
                   jax.ShapeDtypeStruct((1, LANE), F32)],
        compiler_params=_params(("arbitrary",)),
    )(x, g, tgt)


def _ple_bwd(dx3, pp, sg, *, name):
    T, D = dx3.shape
    tr = _tile(T, 256, 8)

    def body(dx_ref, pp_ref, sg_ref, dpg_ref, dpp_ref):
        dx, s = dx_ref[...], sg_ref[...]
        dpg_ref[...] = (dx * pp_ref[...] * s * (1.0 - s)).astype(dpg_ref.dtype)
        dpp_ref[...] = (dx * s).astype(dpp_ref.dtype)

    row = pl.BlockSpec((tr, D), lambda i: (i, 0))
    return pl.pallas_call(
        body, name=name, grid=(T // tr,), in_specs=[row, row, row], out_specs=[row, row],
        out_shape=[jax.ShapeDtypeStruct((T, D), BF16)] * 2, compiler_params=_params(("parallel",)),
    )(dx3, pp, sg)


RC = 64


def _ext(ref, r0, T, before, after):
    parts = []
    if before:
        p0 = pl.multiple_of(jnp.maximum(r0 - 8, 0), 8)
        parts.append(jnp.where(r0 > 0, ref[pl.ds(p0, 8), :], 0.0))
    parts.append(ref[pl.ds(r0, RC), :])
    if after:
        n0 = pl.multiple_of(jnp.minimum(r0 + RC, T - 8), 8)
        parts.append(jnp.where(r0 + RC < T, ref[pl.ds(n0, 8), :], 0.0))
    return parts[0] if len(parts) == 1 else jnp.concatenate(parts, axis=0)


def _down(xx, s):
    return (xx if s == 0 else pltpu.roll(xx, s, 0))[8:, :]


def _up(xx, s, rows):
    return (xx if s == 0 else pltpu.roll(xx, xx.shape[0] - s, 0))[:rows, :]


def _conv_down(xx, w_ref, K):
    y = None
    for j in range(K):
        t = _down(xx, K - 1 - j) * w_ref[j:j + 1, :]
        y = t if y is None else y + t
    return y


def _fold8(x):
    return jnp.sum(x.reshape(x.shape[0] // 8, 8, x.shape[1]), axis=0)


def _silu(x):
    return x * jax.nn.sigmoid(x)


def _dsilu(x):
    s = jax.nn.sigmoid(x)
    return s * (1.0 + x * (1.0 - s))


def _col_specs(T, offs):
    return [pl.BlockSpec((T, LANE), functools.partial(lambda o, j: (0, o + j), o)) for o in offs]


def _group_a_fwd(proj, conv_w, CW, *, name):
    T = proj.shape[0]
    nb = CW // LANE
    K = conv_w.shape[0]

    def body(ax_ref, ab_ref, ac_ref, w_ref, y_ref):
        def step(i, carry):
            r0 = pl.multiple_of(i * RC, RC)
            m = _ext(ac_ref, r0, T, True, False) * _ext(ax_ref, r0, T, True, False)
            y_ref[pl.ds(r0, RC), :] = (ab_ref[pl.ds(r0, RC), :] * _conv_down(m, w_ref, K)).astype(y_ref.dtype)
            return carry
        lax.fori_loop(0, T // RC, step, 0)

    return pl.pallas_call(
        body, name=name, grid=(nb,),
        in_specs=_col_specs(T, (0, nb, 2 * nb)) + [pl.BlockSpec((K, LANE), lambda j: (0, j))],
        out_specs=pl.BlockSpec((T, LANE), lambda j: (0, j)),
        out_shape=jax.ShapeDtypeStruct((T, CW), BF16), compiler_params=_params(("parallel",)),
    )(proj, proj, proj, conv_w)


def _group_a_bwd(proj, conv_w, dycat, CW, *, name):
    T = proj.shape[0]
    nb = CW // LANE
    K = conv_w.shape[0]

    def body(ax_ref, ab_ref, ac_ref, w_ref, dy_ref, dax_ref, dab_ref, dac_ref, dw_ref):
        def step(i, accs):
            r0 = pl.multiple_of(i * RC, RC)
            ax3 = _ext(ax_ref, r0, T, True, True)
            ac3 = _ext(ac_ref, r0, T, True, True)
            m3 = ax3 * ac3
            c = _conv_down(m3[:RC + 8], w_ref, K)
            dy = dy_ref[pl.ds(r0, RC), :]
            dab_ref[pl.ds(r0, RC), :] = (dy * c).astype(dab_ref.dtype)
            dc2 = _ext(dy_ref, r0, T, False, True) * _ext(ab_ref, r0, T, False, True)
            dm = None
            new = []
            for j in range(K):
                s = K - 1 - j
                t = _up(dc2, s, RC) * w_ref[j:j + 1, :]
                dm = t if dm is None else dm + t
                new.append(accs[j] + _fold8(dc2[:RC] * _down(m3[:RC + 8], s)))
            dax_ref[pl.ds(r0, RC), :] = (dm * ac3[8:RC + 8]).astype(dax_ref.dtype)
            dac_ref[pl.ds(r0, RC), :] = (dm * ax3[8:RC + 8]).astype(dac_ref.dtype)
            return tuple(new)

        accs = lax.fori_loop(0, T // RC, step, tuple(jnp.zeros((8, LANE), F32) for _ in range(K)))
        for j in range(K):
            dw_ref[j:j + 1, :] = jnp.sum(accs[j], axis=0, keepdims=True)

    col = pl.BlockSpec((T, LANE), lambda j: (0, j))
    wsp = pl.BlockSpec((K, LANE), lambda j: (0, j))
    return pl.pallas_call(
        body, name=name, grid=(nb,),
        in_specs=_col_specs(T, (0, nb, 2 * nb)) + [wsp, col],
        out_specs=[col, col, col, wsp],
        out_shape=[jax.ShapeDtypeStruct((T, CW), BF16)] * 3 + [jax.ShapeDtypeStruct((K, CW), F32)],
        compiler_params=_params(("parallel",)),
    )(proj, proj, proj, conv_w, dycat)


def _qkv_fwd(proj, conv_w, off, H, *, name):
    T = proj.shape[0]
    nb = 3 * H
    K = conv_w.shape[0]

    def body(x_ref, w_ref, y_ref):
        j = pl.program_id(0)
        is_qk = j < 2 * H
        scale = jnp.where(j < H, HEAD ** -0.5, 1.0).astype(F32)

        def step(i, carry):
            r0 = pl.multiple_of(i * RC, RC)
            s = _silu(_conv_down(_ext(x_ref, r0, T, True, False), w_ref, K))
            r = lax.rsqrt(jnp.sum(s * s, axis=-1, keepdims=True) + EPS) * scale
            y_ref[pl.ds(r0, RC), :] = s * jnp.where(is_qk, r, 1.0)
            return carry
        lax.fori_loop(0, T // RC, step, 0)

    return pl.pallas_call(
        body, name=name, grid=(nb,),
        in_specs=_col_specs(T, (off,)) + [pl.BlockSpec((K, LANE), lambda j: (0, j))],
        out_specs=pl.BlockSpec((T, LANE), lambda j: (0, j)),
        out_shape=jax.ShapeDtypeStruct((T, nb * LANE), F32), compiler_params=_params(("parallel",)),
    )(proj, conv_w)


def _qkv_bwd(proj, conv_w, dq, dk, dv, off, H, *, name):
    T = proj.shape[0]
    nb = 3 * H
    K = conv_w.shape[0]

    def body(x_ref, w_ref, dq_ref, dk_ref, dv_ref, dx_ref, dw_ref):
        j = pl.program_id(0)
        is_qk = j < 2 * H
        scale = jnp.where(j < H, HEAD ** -0.5, 1.0).astype(F32)

        def step(i, accs):
            r0 = pl.multiple_of(i * RC, RC)
            x3 = _ext(x_ref, r0, T, True, True)
            c2 = _conv_down(x3, w_ref, K)
            s2 = _silu(c2)
            dn2 = jnp.where(j < H, _ext(dq_ref, r0, T, False, True),
                            jnp.where(is_qk, _ext(dk_ref, r0, T, False, True), _ext(dv_ref, r0, T, False, True)))
            r = lax.rsqrt(jnp.sum(s2 * s2, axis=-1, keepdims=True) + EPS)
            nh = s2 * r
            dnp = dn2 * scale
            ds_qk = r * (dnp - nh * jnp.sum(dnp * nh, axis=-1, keepdims=True))
            ds2 = jnp.where(is_qk, ds_qk, dn2)
            dc2 = ds2 * _dsilu(c2)
            dx = None
            new = []
            for jj in range(K):
                s = K - 1 - jj
                t = _up(dc2, s, RC) * w_ref[jj:jj + 1, :]
                dx = t if dx is None else dx + t
                new.append(accs[jj] + _fold8(dc2[:RC] * _down(x3[:RC + 8], s)))
            dx_ref[pl.ds(r0, RC), :] = dx.astype(dx_ref.dtype)
            return tuple(new)

        accs = lax.fori_loop(0, T // RC, step, tuple(jnp.zeros((8, LANE), F32) for _ in range(K)))
        for jj in range(K):
            dw_ref[jj:jj + 1, :] = jnp.sum(accs[jj], axis=0, keepdims=True)

    col = pl.BlockSpec((T, LANE), lambda j: (0, j))
    wsp = pl.BlockSpec((K, LANE), lambda j: (0, j))
    return pl.pallas_call(
        body, name=name, grid=(nb,),
        in_specs=_col_specs(T, (off,)) + [wsp] + [
            pl.BlockSpec((T, LANE), functools.partial(lambda o, j: (0, jnp.clip(j - o, 0, H - 1)), o)) for o in (0, H, 2 * H)],
        out_specs=[col, wsp],
        out_shape=[jax.ShapeDtypeStruct((T, nb * LANE), BF16), jax.ShapeDtypeStruct((K, nb * LANE), F32)],
        compiler_params=_params(("parallel",)),
    )(proj, conv_w, dq, dk, dv)


def _softplus(x):
    return jnp.maximum(x, 0.0) + jnp.log(1.0 + jnp.exp(-jnp.abs(x)))


def _gates_fwd(proj, alog, dtb, off, H, *, name):
    T = proj.shape[0]
    tr = _tile(T, 512, 8)

    def body(ab_ref, al_ref, dt_ref, gb_ref):
        ab = ab_ref[...]
        lane = lax.broadcasted_iota(jnp.int32, ab.shape, 1)
        g = -jnp.exp(al_ref[...]) * _softplus(ab + dt_ref[...])
        gb_ref[...] = jnp.where(lane < H, g, jnp.where(lane < 2 * H, jax.nn.sigmoid(ab), 0.0))

    vec = pl.BlockSpec((1, LANE), lambda i: (0, 0))
    return pl.pallas_call(
        body, name=name, grid=(T // tr,),
        in_specs=[pl.BlockSpec((tr, LANE), lambda i: (i, off)), vec, vec],
        out_specs=pl.BlockSpec((tr, LANE), lambda i: (i, 0)),
        out_shape=jax.ShapeDtypeStruct((T, LANE), F32), compiler_params=_params(("parallel",)),
    )(proj, alog, dtb)


def _gates_bwd(proj, alog, dtb, dgb, off, H, *, name):
    T = proj.shape[0]
    tr = _tile(T, 512, 8)

    def body(ab_ref, al_ref, dt_ref, d_ref, dab_ref, dal_ref, ddt_ref):
        ab, d = ab_ref[...], d_ref[...]
        lane = lax.broadcasted_iota(jnp.int32, ab.shape, 1)
        z = ab + dt_ref[...]
        A = -jnp.exp(al_ref[...])
        da = d * A * jax.nn.sigmoid(z)
        beta = jax.nn.sigmoid(ab)
        db = d * beta * (1.0 - beta)
        is_g = lane < H
        dab_ref[...] = jnp.where(is_g, da, jnp.where(lane < 2 * H, db, 0.0)).astype(dab_ref.dtype)

        @pl.when(pl.program_id(0) == 0)
        def _():
            dal_ref[...] = jnp.zeros_like(dal_ref)
            ddt_ref[...] = jnp.zeros_like(ddt_ref)

        dal_ref[...] += jnp.sum(jnp.where(is_g, d * A * _softplus(z), 0.0), axis=0, keepdims=True)
        ddt_ref[...] += jnp.sum(jnp.where(is_g, da, 0.0), axis=0, keepdims=True)

    vec = pl.BlockSpec((1, LANE), lambda i: (0, 0))
    row = pl.BlockSpec((tr, LANE), lambda i: (i, 0))
    return pl.pallas_call(
        body, name=name, grid=(T // tr,),
        in_specs=[pl.BlockSpec((tr, LANE), lambda i: (i, off)), vec, vec, row],
        out_specs=[row, vec, vec],
        out_shape=[jax.ShapeDtypeStruct((T, LANE), BF16), jax.ShapeDtypeStruct((1, LANE), F32),
                   jax.ShapeDtypeStruct((1, LANE), F32)],
        compiler_params=_params(("arbitrary",)),
    )(proj, alog, dtb, dgb)


def _gated_norm_fwd(o, proj, gn, zoff, *, name):
    T, W = o.shape
    tr = _tile(T, 512, 8)

    def body(o_ref, z_ref, g_ref, y_ref):
        ov = o_ref[...]
        r = lax.rsqrt(jnp.mean(ov * ov, axis=-1, keepdims=True) + EPS)
        y_ref[...] = (ov * r * g_ref[...] * _silu(z_ref[...])).astype(y_ref.dtype)

    blk = pl.BlockSpec((tr, LANE), lambda i, j: (i, j))
    return pl.pallas_call(
        body, name=name, grid=(T // tr, W // LANE),
        in_specs=[blk, pl.BlockSpec((tr, LANE), lambda i, j: (i, zoff + j)), pl.BlockSpec((1, LANE), lambda i, j: (0, 0))],
        out_specs=blk, out_shape=jax.ShapeDtypeStruct((T, W), BF16), compiler_params=_params(("parallel", "parallel")),
    )(o, proj, gn)


def _gated_norm_bwd(o, proj, gn, dycat, zoff, yoff, *, name):
    T, W = o.shape
    tr = _tile(T, 512, 8)

    def body(o_ref, z_ref, g_ref, dy_ref, do_ref, dz_ref, dg_ref):
        ov, zv, gv, dy = o_ref[...], z_ref[...], g_ref[...], dy_ref[...]
        r = lax.rsqrt(jnp.mean(ov * ov, axis=-1, keepdims=True) + EPS)
        nh = ov * r
        s = _silu(zv)

        @pl.when((pl.program_id(0) == 0) & (pl.program_id(1) == 0))
        def _():
            dg_ref[...] = jnp.zeros_like(dg_ref)

        dg_ref[...] += jnp.sum(dy * nh * s, axis=0, keepdims=True)
        dz_ref[...] = (dy * nh * gv * _dsilu(zv)).astype(dz_ref.dtype)
        dn = dy * gv * s
        do_ref[...] = r * (dn - nh * jnp.mean(dn * nh, axis=-1, keepdims=True))

    blk = pl.BlockSpec((tr, LANE), lambda i, j: (i, j))
    vec = pl.BlockSpec((1, LANE), lambda i, j: (0, 0))
    return pl.pallas_call(
        body, name=name, grid=(T // tr, W // LANE),
        in_specs=[blk, pl.BlockSpec((tr, LANE), lambda i, j: (i, zoff + j)), vec,
                  pl.BlockSpec((tr, LANE), lambda i, j: (i, yoff + j))],
        out_specs=[blk, blk, vec],
        out_shape=[jax.ShapeDtypeStruct((T, W), F32), jax.ShapeDtypeStruct((T, W), BF16),
                   jax.ShapeDtypeStruct((1, LANE), F32)],
        compiler_params=_params(("arbitrary", "arbitrary")),
    )(o, proj, gn, dycat)


def _ffn_act_fwd(up_pre, conv_w, *, name):
    T, F2 = up_pre.shape
    nb = F2 // 2 // LANE
    K = conv_w.shape[0]

    def body(g_ref, v_ref, wg_ref, wv_ref, y_ref):
        def step(i, carry):
            r0 = pl.multiple_of(i * RC, RC)
            gate = _conv_down(_ext(g_ref, r0, T, True, False), wg_ref, K)
            val = _conv_down(_ext(v_ref, r0, T, True, False), wv_ref, K)
            y_ref[pl.ds(r0, RC), :] = (_silu(gate) * val).astype(y_ref.dtype)
            return carry
        lax.fori_loop(0, T // RC, step, 0)

    return pl.pallas_call(
        body, name=name, grid=(nb,),
        in_specs=_col_specs(T, (0, nb)) + [pl.BlockSpec((K, LANE), lambda j: (0, j)),
                                           pl.BlockSpec((K, LANE), lambda j: (0, nb + j))],
        out_specs=pl.BlockSpec((T, LANE), lambda j: (0, j)),
        out_shape=jax.ShapeDtypeStruct((T, F2 // 2), BF16), compiler_params=_params(("parallel",)),
    )(up_pre, up_pre, conv_w, conv_w)


def _ffn_act_bwd(up_pre, conv_w, dact, *, name):
    T, F2 = up_pre.shape
    nb = F2 // 2 // LANE
    K = conv_w.shape[0]

    def body(g_ref, v_ref, wg_ref, wv_ref, da_ref, dg_ref, dv_ref, dwg_ref, dwv_ref):
        def step(i, accs):
            r0 = pl.multiple_of(i * RC, RC)
            g3 = _ext(g_ref, r0, T, True, True)
            v3 = _ext(v_ref, r0, T, True, True)
            gate2 = _conv_down(g3, wg_ref, K)
            val2 = _conv_down(v3, wv_ref, K)
            da2 = _ext(da_ref, r0, T, False, True)
            dgate2 = da2 * val2 * _dsilu(gate2)
            dval2 = da2 * _silu(gate2)
            dgp, dvp, new = None, None, []
            for j in range(K):
                s = K - 1 - j
                tg = _up(dgate2, s, RC) * wg_ref[j:j + 1, :]
                tv = _up(dval2, s, RC) * wv_ref[j:j + 1, :]
                dgp = tg if dgp is None else dgp + tg
                dvp = tv if dvp is None else dvp + tv
                new.append(accs[2 * j] + _fold8(dgate2[:RC] * _down(g3[:RC + 8], s)))
                new.append(accs[2 * j + 1] + _fold8(dval2[:RC] * _down(v3[:RC + 8], s)))
            dg_ref[pl.ds(r0, RC), :] = dgp.astype(dg_ref.dtype)
            dv_ref[pl.ds(r0, RC), :] = dvp.astype(dv_ref.dtype)
            return tuple(new)

        accs = lax.fori_loop(0, T // RC, step, tuple(jnp.zeros((8, LANE), F32) for _ in range(2 * K)))
        for j in range(K):
            dwg_ref[j:j + 1, :] = jnp.sum(accs[2 * j], axis=0, keepdims=True)
            dwv_ref[j:j + 1, :] = jnp.sum(accs[2 * j + 1], axis=0, keepdims=True)

    col = pl.BlockSpec((T, LANE), lambda j: (0, j))
    wsp = pl.BlockSpec((K, LANE), lambda j: (0, j))
    return pl.pallas_call(
        body, name=name, grid=(nb,),
        in_specs=_col_specs(T, (0, nb)) + [wsp, pl.BlockSpec((K, LANE), lambda j: (0, nb + j)), col],
        out_specs=[col, col, wsp, wsp],
        out_shape=[jax.ShapeDtypeStruct((T, F2 // 2), BF16)] * 2 + [jax.ShapeDtypeStruct((K, F2 // 2), F32)] * 2,
        compiler_params=_params(("parallel",)),
    )(up_pre, up_pre, conv_w, conv_w, dact)


CPB = 8
PREP_UNROLL = 4


def _dot(a, b):
    return jnp.dot(a, b, precision=HI, preferred_element_type=F32)


def _dot_nt(a, b):
    return lax.dot_general(a, b, _DN["nt"], precision=HI, preferred_element_type=F32)


def _dot_tn(a, b):
    return lax.dot_general(a, b, _DN["tn"], precision=HI, preferred_element_type=F32)


def _tri(strict=False, upper=False):
    r = lax.broadcasted_iota(jnp.int32, (CHUNK, CHUNK), 0)
    c = lax.broadcasted_iota(jnp.int32, (CHUNK, CHUNK), 1)
    if upper:
        return c >= r
    return (r > c) if strict else (r >= c)


def _chunk_decay(gb):
    gam = _dot(_tri().astype(F32), gb)
    diff = gam[:, :CHUNK] - gam.T[:CHUNK, :]
    D = jnp.exp(jnp.where(_tri(), diff, -1e30))
    return gam, D


def _delta_specs(T, H, cpb):
    rows = cpb * CHUNK
    col = lambda o: pl.BlockSpec((rows, LANE), functools.partial(lambda o, h, n: (n, o + h), o))
    bc = pl.BlockSpec((1, rows, LANE), lambda h, n: (h, n, 0))
    sq = pl.BlockSpec((1, cpb, CHUNK, CHUNK), lambda h, n: (h, n, 0, 0))
    vec = pl.BlockSpec((1, cpb, LANE), lambda h, n: (h, n, 0))
    return col, bc, sq, vec


def _delta_prep_fwd(qkv, gB, bB, H, *, name):
    T = qkv.shape[0]
    N = T // CHUNK
    cpb = _tile(N, CPB, 8)
    col, bc, sq, vec = _delta_specs(T, H, cpb)

    def body(q_ref, k_ref, v_ref, g_ref, b_ref, u_ref, w_ref, qd_ref, kd_ref, qk_ref, ti_ref, gl_ref):
        eye = (lax.broadcasted_iota(jnp.int32, (CHUNK, CHUNK), 0) == lax.broadcasted_iota(jnp.int32, (CHUNK, CHUNK), 1)).astype(F32)

        def step(c, carry):
            r0 = pl.multiple_of(c * CHUNK, CHUNK)
            rows = pl.ds(r0, CHUNK)
            q, k, v = q_ref[rows, :], k_ref[rows, :], v_ref[rows, :]
            bb = b_ref[0, rows, :]
            gam, D = _chunk_decay(g_ref[0, rows, :])
            e = jnp.exp(gam)
            L = jnp.where(_tri(strict=True), _dot_nt(k, k) * D, 0.0) * bb[:, :CHUNK]
            X = -L
            R = eye + X
            for _ in range(5):
                X = _dot(X, X)
                R = R + _dot(R, X)
            u_ref[rows, :] = _dot(R, bb * v)
            w_ref[rows, :] = _dot(R, bb * e * k)
            qd_ref[rows, :] = e * q
            glast = gam[CHUNK - 1:CHUNK, :]
            kd_ref[rows, :] = jnp.exp(glast - gam) * k
            qk_ref[0, c] = _dot_nt(q, k) * D
            ti_ref[0, c] = R
            gl_ref[0, pl.ds(c, 1), :] = jnp.exp(glast)
            return carry
        lax.fori_loop(0, cpb, step, 0, unroll=min(PREP_UNROLL, cpb))

    full = jax.ShapeDtypeStruct((T, H * LANE), F32)
    sqs = jax.ShapeDtypeStruct((H, N, CHUNK, CHUNK), F32)
    return pl.pallas_call(
        body, name=name, grid=(H, N // cpb),
        in_specs=[col(0), col(H), col(2 * H), bc, bc],
        out_specs=[col(0)] * 4 + [sq, sq, vec],
        out_shape=[full] * 4 + [sqs, sqs, jax.ShapeDtypeStruct((H, N, LANE), F32)],
        compiler_params=_params(("parallel", "parallel")),
    )(qkv, qkv, qkv, gB, bB)


HPB = 4


def _scan_specs(H, N, cpb, hb, rev):
    nbk = N // cpb
    blk = (lambda n: nbk - 1 - n) if rev else (lambda n: n)
    col = pl.BlockSpec((cpb * CHUNK, hb * LANE), lambda h, n: (blk(n), h))
    sq = pl.BlockSpec((hb, cpb, CHUNK, CHUNK), lambda h, n: (h, blk(n), 0, 0))
    vec = pl.BlockSpec((hb, cpb, LANE), lambda h, n: (h, blk(n), 0))
    st = pl.BlockSpec((hb, cpb, HEAD, HEAD), lambda h, n: (h, blk(n), 0, 0))
    return col, sq, vec, st


def _delta_scan_fwd(u, w, qd, kd, qk, gl, H, *, name):
    T = u.shape[0]
    N = T // CHUNK
    cpb = _tile(N, CPB, 8)
    hb = min(HPB, H)
    col, sq, vec, st = _scan_specs(H, N, cpb, hb, False)

    def body(u_ref, w_ref, qd_ref, kd_ref, qk_ref, gl_ref, o_ref, vn_ref, ss_ref, s_scr):
        @pl.when(pl.program_id(1) == 0)
        def _():
            s_scr[...] = jnp.zeros_like(s_scr)

        def step(c, states):
            rows = pl.ds(pl.multiple_of(c * CHUNK, CHUNK), CHUNK)
            new = []
            for j, S in enumerate(states):
                ln = slice(j * LANE, (j + 1) * LANE)
                ss_ref[j, c] = S
                vn = u_ref[rows, ln] - _dot(w_ref[rows, ln], S)
                o_ref[rows, ln] = _dot(qd_ref[rows, ln], S) + _dot(qk_ref[j, c], vn)
                vn_ref[rows, ln] = vn
                new.append(S * gl_ref[j, pl.ds(c, 1), :] + _dot_tn(kd_ref[rows, ln], vn))
            return tuple(new)
        out = lax.fori_loop(0, cpb, step, tuple(s_scr[j] for j in range(hb)))
        for j in range(hb):
            s_scr[j] = out[j]

    full = jax.ShapeDtypeStruct((T, H * LANE), F32)
    return pl.pallas_call(
        body, name=name, grid=(H // hb, N // cpb),
        in_specs=[col] * 4 + [sq, vec],
        out_specs=[col, col, st],
        out_shape=[full, full, jax.ShapeDtypeStruct((H, N, HEAD, HEAD), F32)],
        scratch_shapes=[pltpu.VMEM((hb, HEAD, HEAD), F32)],
        compiler_params=_params(("parallel", "arbitrary")),
    )(u, w, qd, kd, qk, gl)


def _delta_scan_bwd(do, w, qd, kd, vn, qk, gl, ss, H, *, name):
    T = do.shape[0]
    N = T // CHUNK
    cpb = _tile(N, CPB, 8)
    hb = min(HPB, H)
    col, sq, vec, st = _scan_specs(H, N, cpb, hb, True)

    def body(do_ref, w_ref, qd_ref, kd_ref, vn_ref, qk_ref, gl_ref, ss_ref,
             du_ref, dw_ref, dqd_ref, dkd_ref, dqk_ref, dgl_ref, ds_scr):
        @pl.when(pl.program_id(1) == 0)
        def _():
            ds_scr[...] = jnp.zeros_like(ds_scr)

        def step(i, dstates):
            c = cpb - 1 - i
            rows = pl.ds(pl.multiple_of(c * CHUNK, CHUNK), CHUNK)
            new = []
            for j, dS in enumerate(dstates):
                ln = slice(j * LANE, (j + 1) * LANE)
                S, dov, vnv = ss_ref[j, c], do_ref[rows, ln], vn_ref[rows, ln]
                dvn = _dot_tn(qk_ref[j, c], dov) + _dot(kd_ref[rows, ln], dS)
                du_ref[rows, ln] = dvn
                dw_ref[rows, ln] = -_dot_nt(dvn, S)
                dqd_ref[rows, ln] = _dot_nt(dov, S)
                dkd_ref[rows, ln] = _dot_nt(vnv, dS)
                dqk_ref[j, c] = _dot_nt(dov, vnv)
                dgl = jnp.sum(jnp.sum(dS * S, axis=1, keepdims=True), axis=0, keepdims=True)
                dgl_ref[j, pl.ds(c, 1), :] = jnp.broadcast_to(dgl, (1, LANE))
                new.append(_dot_tn(qd_ref[rows, ln], dov) + dS * gl_ref[j, pl.ds(c, 1), :]
                           - _dot_tn(w_ref[rows, ln], dvn))
            return tuple(new)
        out = lax.fori_loop(0, cpb, step, tuple(ds_scr[j] for j in range(hb)))
        for j in range(hb):
            ds_scr[j] = out[j]

    full = jax.ShapeDtypeStruct((T, H * LANE), F32)
    return pl.pallas_call(
        body, name=name, grid=(H // hb, N // cpb),
        in_specs=[col] * 5 + [sq, vec, st],
        out_specs=[col] * 4 + [sq, vec],
        out_shape=[full] * 4 + [jax.ShapeDtypeStruct((H, N, CHUNK, CHUNK), F32), jax.ShapeDtypeStruct((H, N, LANE), F32)],
        scratch_shapes=[pltpu.VMEM((hb, HEAD, HEAD), F32)],
        compiler_params=_params(("parallel", "arbitrary")),
    )(do, w, qd, kd, vn, qk, gl, ss)


def _delta_prep_bwd(qkv, gB, bB, ti, u, w, qk, du, dw, dqd, dkd, dqk, dgl, H, *, name):
    T = qkv.shape[0]
    N = T // CHUNK
    cpb = _tile(N, CPB, 8)
    col, bc, sq, vec = _delta_specs(T, H, cpb)

    def body(q_ref, k_ref, v_ref, g_ref, b_ref, ti_ref, u_ref, w_ref, qk_ref,
             du_ref, dw_ref, dqd_ref, dkd_ref, dqk_ref, dgl_ref,
             dq_ref, dk_ref, dv_ref, dg_ref, db_ref):
        ones = jnp.ones((CHUNK, LANE), F32)
        lsum = lambda x: jnp.sum(x, axis=-1, keepdims=True)

        def step(c, carry):
            r0 = pl.multiple_of(c * CHUNK, CHUNK)
            rows = pl.ds(r0, CHUNK)
            q, k, v = q_ref[rows, :], k_ref[rows, :], v_ref[rows, :]
            bb = b_ref[0, rows, :]
            gam, D = _chunk_decay(g_ref[0, rows, :])
            e = jnp.exp(gam)
            glast = gam[CHUNK - 1:CHUNK, :]
            eL = jnp.exp(glast - gam)
            gl = jnp.exp(glast)
            Ti, uv, wv, QK = ti_ref[0, c], u_ref[rows, :], w_ref[rows, :], qk_ref[0, c]
            duv, dwv, dqd_v, dkd_v, dqk_v = du_ref[rows, :], dw_ref[rows, :], dqd_ref[rows, :], dkd_ref[rows, :], dqk_ref[0, c]
            KKD = jnp.where(_tri(strict=True), _dot_nt(k, k) * D, 0.0)
            rw = bb * e * k
            dru = _dot_tn(Ti, duv)
            drw = _dot_tn(Ti, dwv)
            dL = jnp.where(_tri(strict=True), -(_dot_nt(dru, uv) + _dot_nt(drw, wv)), 0.0)
            Mm = dL * bb[:, :CHUNK]
            dKK = Mm * D
            dQK = dqk_v * D
            P = Mm * KKD + dqk_v * QK
            dq_ref[rows, :] = _dot(dQK, k) + e * dqd_v
            dk_ref[rows, :] = (_dot_tn(dQK, q) + _dot(dKK, k) + _dot_tn(dKK, k) + bb * e * drw + eL * dkd_v)
            dv_ref[rows, :] = bb * dru
            db = _dot(dL * KKD, ones) + lsum(dru * v) + lsum(drw * e * k)
            kdv = eL * k
            dgam = (_dot(P, ones) - _dot_tn(P, ones) + lsum(drw * rw) + lsum(dqd_v * e * q) - lsum(dkd_v * kdv))
            xlast = jnp.sum(lsum(dkd_v * kdv), axis=0, keepdims=True) + gl * dgl_ref[0, pl.ds(c, 1), :]
            dg_ref[0, rows, :] = _dot(_tri(upper=True).astype(F32), dgam) + xlast
            db_ref[0, rows, :] = db
            return carry
        lax.fori_loop(0, cpb, step, 0, unroll=min(PREP_UNROLL, cpb))

    full = jax.ShapeDtypeStruct((T, H * LANE), F32)
    bcs = jax.ShapeDtypeStruct((H, T, LANE), F32)
    return pl.pallas_call(
        body, name=name, grid=(H, N // cpb),
        in_specs=[col(0), col(H), col(2 * H), bc, bc, sq, col(0), col(0), sq, col(0), col(0), col(0), col(0), sq, vec],
        out_specs=[col(0), col(0), col(0), bc, bc],
        out_shape=[full, full, full, bcs, bcs],
        compiler_params=_params(("parallel", "parallel")),
    )(qkv, qkv, qkv, gB, bB, ti, u, w, qk, du, dw, dqd, dkd, dqk, dgl)


def _adam(parts, w, m, v, *, name, own=None, me=None):
    P, R, C = parts.shape
    tr = _tile(R, 256, 8)
    n_own = 0 if own is None else 2

    def body(*refs):
        p_ref, w_ref, m_ref, v_ref, g_ref, d_ref, nm_ref, nv_ref = refs[n_own:]
        g = None
        for i in range(P):
            t = p_ref[i].astype(F32)
            if n_own:
                t = jnp.where(refs[0][0] == i, refs[1][...].astype(F32), t)
            g = t if g is None else g + t
        mn = ADAM_B1 * m_ref[...] + (1.0 - ADAM_B1) * g
        vn = ADAM_B2 * v_ref[...] + (1.0 - ADAM_B2) * (g * g)
        m_hat = mn / (1.0 - ADAM_B1 ** ADAM_STEP)
        v_hat = vn / (1.0 - ADAM_B2 ** ADAM_STEP)
        g_ref[...] = g
        d_ref[...] = -ADAM_LR * (m_hat / (jnp.sqrt(v_hat) + ADAM_EPS) + ADAM_WD * w_ref[...])
        nm_ref[...] = mn
        nv_ref[...] = vn

    blk = pl.BlockSpec((tr, C), lambda i: (i, 0))
    return pl.pallas_call(
        body, name=name, grid=(R // tr,),
        in_specs=[pl.BlockSpec(memory_space=pltpu.SMEM), blk][:n_own] + [pl.BlockSpec((P, tr, C), lambda i: (0, i, 0)), blk, blk, blk],
        out_specs=[blk] * 4, out_shape=[jax.ShapeDtypeStruct((R, C), F32)] * 4,
        compiler_params=_params(("parallel",)),
    )(*([me, own] if n_own else []), parts, w, m, v)


def _mesh_pos():
    return lax.axis_index("x"), lax.axis_index("y"), lax.axis_index("c")


def _peer(k):
    x, y, c = _mesh_pos()
    px, py, pc = x ^ ((k >> 2) & 1), y ^ ((k >> 1) & 1), c ^ (k & 1)
    return (px, py, pc), 4 * px + 2 * py + pc


def _exchange(arrays, scatter, *, name):
    n = len(arrays)
    blocks = [a.shape[1:] if scatter else a.shape for a in arrays]

    def body(*refs):
        srcs, dsts = refs[:n], refs[n:2 * n]
        send_sems, recv_sems, local_sems = refs[2 * n:]
        x, y, c = _mesh_pos()
        me = 4 * x + 2 * y + c
        local, sends = [], []
        for a in range(n):
            cp = pltpu.make_async_copy(srcs[a].at[me] if scatter else srcs[a], dsts[a].at[me], local_sems.at[a])
            cp.start()
            local.append(cp)
            for k in range(1, N_DEV):
                dev, idx = _peer(k)
                cp = pltpu.make_async_remote_copy(
                    src_ref=srcs[a].at[idx] if scatter else srcs[a], dst_ref=dsts[a].at[me],
                    send_sem=send_sems.at[a * N_DEV + k], recv_sem=recv_sems.at[a * N_DEV + k],
                    device_id=dev, device_id_type=MESH)
                cp.start()
                sends.append(cp)
        for a in range(n):
            for k in range(1, N_DEV):
                dev, idx = _peer(k)
                pltpu.make_async_remote_copy(
                    src_ref=srcs[a].at[idx] if scatter else srcs[a], dst_ref=dsts[a].at[idx],
                    send_sem=send_sems.at[a * N_DEV + k], recv_sem=recv_sems.at[a * N_DEV + k],
                    device_id=dev, device_id_type=MESH).wait_recv()
        for cp in sends:
            cp.wait_send()
        for cp in local:
            cp.wait()

    anyspec = pl.BlockSpec(memory_space=pl.ANY)
    return pl.pallas_call(
        body, name=name, in_specs=[anyspec] * n, out_specs=[anyspec] * n,
        out_shape=[jax.ShapeDtypeStruct((N_DEV,) + tuple(b), a.dtype) for a, b in zip(arrays, blocks)],
        scratch_shapes=[pltpu.SemaphoreType.DMA((n * N_DEV,)), pltpu.SemaphoreType.DMA((n * N_DEV,)),
                        pltpu.SemaphoreType.DMA((n,))],
    )(*arrays)


_ANY = pl.BlockSpec(memory_space=pl.ANY)
_SEM = pl.BlockSpec(memory_space=pltpu.SEMAPHORE)
_EFFECT = pltpu.SideEffectType.DATAFLOW_SIDE_EFFECTING


def _in_hbm(a):
    return pltpu.with_memory_space_constraint(a, pltpu.HBM)


def _split_copy(src, land, send, recv, k, me, scatter, landed):
    dev, idx = _peer(k)
    return pltpu.make_async_remote_copy(
        src_ref=src.at[idx] if scatter else src, dst_ref=land.at[idx if landed else me],
        send_sem=send.at[k], recv_sem=recv.at[k], device_id=dev, device_id_type=MESH)


def _split_start(srcs, lands, scatter, *, name):
    n = len(srcs)

    def body(*refs):
        src, land, send, recv, token = refs[:n], refs[n:2 * n], refs[2 * n:3 * n], refs[3 * n:4 * n], refs[-1]
        x, y, c = _mesh_pos()
        me = 4 * x + 2 * y + c
        for a in range(n):
            for k in range(1, N_DEV):
                _split_copy(src[a], land[a], send[a], recv[a], k, me, scatter, False).start()
        token[...] = jnp.zeros_like(token)

    outs = pl.pallas_call(
        body, name=name,
        out_shape=[pltpu.SemaphoreType.DMA((N_DEV,))] * (2 * n) + [pltpu.HBM(t.shape, t.dtype) for t in list(srcs) + list(lands)]
        + [jax.ShapeDtypeStruct((8, LANE), F32)],
        in_specs=[_ANY] * (2 * n), out_specs=[_SEM] * (2 * n) + [_ANY] * (2 * n) + [pl.BlockSpec(memory_space=pltpu.VMEM)],
        input_output_aliases={i: 2 * n + i for i in range(2 * n)},
        compiler_params=pltpu.CompilerParams(has_side_effects=_EFFECT),
    )(*[_in_hbm(t) for t in list(srcs) + list(lands)])
    handles = [(outs[a], outs[n + a], outs[2 * n + a], outs[3 * n + a]) for a in range(n)]
    return handles, outs[-1]


def _split_wait(handle, after, scatter, *, name):
    send, recv, src_thru, land_thru = handle

    def body(src_ref, land_ref, send_ref, recv_ref, after_ref, src_out, land_out):
        x, y, c = _mesh_pos()
        me = 4 * x + 2 * y + c
        for k in range(1, N_DEV):
            cp = _split_copy(src_ref, land_ref, send_ref, recv_ref, k, me, scatter, True)
            cp.wait_send()
            cp.wait_recv()

    return pl.pallas_call(
        body, name=name,
        out_shape=(pltpu.HBM(src_thru.shape, src_thru.dtype), pltpu.HBM(land_thru.shape, land_thru.dtype)),
        in_specs=(_ANY, _ANY, _SEM, _SEM, _ANY), out_specs=(_ANY, _ANY), input_output_aliases={0: 0, 1: 1},
        compiler_params=pltpu.CompilerParams(has_side_effects=_EFFECT),
    )(src_thru, land_thru, send, recv, after)[1]


def _local_step(x, p, tgt, S, wt, conv, emit):
    T, D = x.shape
    CW = DNW = D // 2
    H = DNW // HEAD
    nA, nD = CW // LANE, DNW // LANE
    qkv_off, z_off, ab_off = 3 * nA, 3 * nA + 3 * nD, 3 * nA + 4 * nD
    alog = jnp.pad(S["a_log"], ((0, 0), (0, LANE - H)))
    dtb = jnp.pad(S["dt_bias"], ((0, 0), (0, LANE - H)))
    add = lambda acc, r: (acc + r,)

    h1 = _rms_fwd(x, S["g_mix"], name="rms1_fwd")
    w_in, cv = wt("w_in", h1), conv(h1)
    proj = _matmul(h1, w_in, "nn", name="mm_in")
    y_a = _group_a_fwd(proj, cv["conv_a"], CW, name="group_a_fwd")
    qkv = _qkv_fwd(proj, cv["conv_qkv"], qkv_off, H, name="qkv_fwd")
    gb = _gates_fwd(proj, alog, dtb, ab_off, H, name="gates_fwd")
    bcast = lambda cols: jnp.broadcast_to(cols.T[:, :, None], (H, T, LANE))
    gB, bB = bcast(gb[:, :H]), bcast(gb[:, H:2 * H])
    u, w, qd, kd, qk, ti, gl = _delta_prep_fwd(qkv, gB, bB, H, name="delta_prep_fwd")
    o, vn, ss = _delta_scan_fwd(u, w, qd, kd, qk, gl, H, name="delta_scan_fwd")
    y_b = _gated_norm_fwd(o, proj, S["dn_g"], z_off, name="gated_norm_fwd")
    ycat = jnp.concatenate([y_a, y_b], axis=1)
    w_out = wt("w_out", ycat)
    x1 = _matmul(ycat, w_out, "nn", name="mm_out", epilogue=add, extras=(x,))
    h2 = _rms_fwd(x1, S["g_ffn"], name="rms2_fwd")
    w_up = wt("w_up", h2)
    up_pre = _matmul(h2, w_up, "nn", name="mm_up")
    act = _ffn_act_fwd(up_pre, cv["conv_ffn"], name="ffn_act_fwd")
    w_down = wt("w_down", act)
    x2 = _matmul(act, w_down, "nn", name="mm_down", epilogue=add, extras=(x1,))
    h3 = _rms_fwd(x2, S["g_ple"], name="rms3_fwd")
    w_pp, w_pg = wt("w_pp", h3), wt("w_pg", h3)
    pp = _matmul(p, w_pp, "nn", name="mm_pp")

    def ple_epi(acc, x2r, ppr):
        s = jax.nn.sigmoid(acc)
        return x2r + s * ppr, s

    x3, sg = _matmul(h3, w_pg, "nn", name="mm_pg", out_dtypes=(F32, F32), epilogue=ple_epi, extras=(x2, pp))
    dx3, dg_final, loss = _final_loss(x3, S["g_final"], tgt, name="final_loss")

    G = {"g_final": dg_final}
    dpg, dpp = _ple_bwd(dx3, pp, sg, name="ple_bwd")
    tok = emit({"w_pp": _matmul(p, dpp, "tn", name="mm_dwpp", out_dtypes=(BF16,)),
                "w_pg": _matmul(h3, dpg, "tn", name="mm_dwpg", out_dtypes=(BF16,))})
    dh3 = _matmul(dpg, w_pg, "nt", name="mm_dh3", after=tok)
    dx2, G["g_ple"] = _rms_bwd(x2, S["g_ple"], dh3, dx3, name="rms3_bwd")
    tok = emit({"w_down": _matmul(act, dx2, "tn", name="mm_dwdown", out_dtypes=(BF16,))})
    dact = _matmul(dx2, w_down, "nt", name="mm_dact", after=tok)
    dup_g, dup_v, dcf_g, dcf_v = _ffn_act_bwd(up_pre, cv["conv_ffn"], dact, name="ffn_act_bwd")
    G["conv_ffn"] = jnp.concatenate([dcf_g, dcf_v], axis=1)
    dup = jnp.concatenate([dup_g, dup_v], axis=1)
    tok = emit({"w_up": _matmul(h2, dup, "tn", name="mm_dwup", out_dtypes=(BF16,))})
    dh2 = _matmul(dup, w_up, "nt", name="mm_dh2", after=tok)
    dx1, G["g_ffn"] = _rms_bwd(x1, S["g_ffn"], dh2, dx2, name="rms2_bwd")
    tok = emit({"w_out": _matmul(ycat, dx1, "tn", name="mm_dwout", out_dtypes=(BF16,))})
    dycat = _matmul(dx1, w_out, "nt", name="mm_dycat", after=tok)
    do, dz, G["dn_g"] = _gated_norm_bwd(o, proj, S["dn_g"], dycat, z_off, nA, name="gated_norm_bwd")
    du, dw, dqd, dkd, dqk, dgl = _delta_scan_bwd(do, w, qd, kd, vn, qk, gl, ss, H, name="delta_scan_bwd")
    dq, dk, dv, dgB, dbB = _delta_prep_bwd(qkv, gB, bB, ti, u, w, qk, du, dw, dqd, dkd, dqk, dgl, H,
                                           name="delta_prep_bwd")
    dgb = jnp.pad(jnp.concatenate([dgB[:, :, 0].T, dbB[:, :, 0].T], axis=1), ((0, 0), (0, LANE - 2 * H)))
    dab, dal, ddt = _gates_bwd(proj, alog, dtb, dgb, ab_off, H, name="gates_bwd")
    G["a_log"], G["dt_bias"] = dal[:, :H], ddt[:, :H]
    dqkv, G["conv_qkv"] = _qkv_bwd(proj, cv["conv_qkv"], dq, dk, dv, qkv_off, H, name="qkv_bwd")
    dax, dab_, dac, G["conv_a"] = _group_a_bwd(proj, cv["conv_a"], dycat, CW, name="group_a_bwd")
    in_p = w_in.shape[1]
    dproj = jnp.concatenate([dax, dab_, dac, dqkv, dz, dab, jnp.zeros((T, in_p - (ab_off + 1) * LANE), BF16)], axis=1)
    tok = emit({"w_in": _matmul(h1, dproj, "tn", name="mm_dwin", out_dtypes=(BF16,))})
    dh1 = _matmul(dproj, w_in, "nt", name="mm_dh1", after=tok)
    grad_x, G["g_mix"] = _rms_bwd(x, S["g_mix"], dh1, dx1, name="rms1_bwd")
    return loss, grad_x, G


def _pad_cols(a, n):
    return jnp.pad(a, ((0, 0), (0, n - a.shape[1])))


def _col_sharded(landed):
    _, R, C = landed.shape
    return jnp.transpose(landed, (1, 0, 2)).reshape(R, N_DEV * C)


def _col_parts(full):
    R, C8 = full.shape
    return jnp.transpose(full.reshape(R, N_DEV, C8 // N_DEV), (1, 0, 2))


def kernel(x, p, norm_mix_g, w_in, conv_a_w, conv_qkv_w, a_log, dt_bias, dn_norm_g, w_out, norm_ffn_g, w_up, conv_ffn_w, w_down, norm_ple_g, w_ple_gate, w_ple_proj, final_norm_g, loss_target, m_norm_mix_g, m_w_in, m_conv_a_w, m_conv_qkv_w, m_a_log, m_dt_bias, m_dn_norm_g, m_w_out, m_norm_ffn_g, m_w_up, m_conv_ffn_w, m_w_down, m_norm_ple_g, m_w_ple_gate, m_w_ple_proj, m_final_norm_g, v_norm_mix_g, v_w_in, v_conv_a_w, v_conv_qkv_w, v_a_log, v_dt_bias, v_dn_norm_g, v_w_out, v_norm_ffn_g, v_w_up, v_conv_ffn_w, v_w_down, v_norm_ple_g, v_w_ple_gate, v_w_ple_proj, v_final_norm_g):
    T, D = x.shape[1], x.shape[2]
    xd, _, cd = _mesh_pos()
    me = 4 * xd + 2 * lax.axis_index("y") + cd

    conv_sh = [conv_a_w[0], conv_qkv_w[0], conv_ffn_w[0]]
    conv_n = [c.size for c in conv_sh]
    pack_rows = -(-sum(conv_n) // LANE)
    conv_pack = jnp.pad(jnp.concatenate([c.reshape(-1) for c in conv_sh]), (0, pack_rows * LANE - sum(conv_n))).reshape(pack_rows, LANE)
    names = ["w_in", "conv", "w_out", "w_up", "w_down", "w_pg", "w_pp"]
    shards = [w_in[0].astype(BF16), conv_pack, w_out[0].astype(BF16), w_up[0].astype(BF16), w_down[0].astype(BF16),
              w_ple_gate[0].astype(BF16), w_ple_proj[0].astype(BF16)]
    empty_slots = lambda blocks: [lax.empty((N_DEV,) + tuple(b.shape), b.dtype) for b in blocks]
    handles, tok0 = _split_start(shards, empty_slots(shards), False, name="gather_start")
    handle = dict(zip(names, handles))
    own = dict(zip(names, shards))
    in_cols = N_DEV * w_in.shape[2]
    in_p = (in_cols // LANE) * LANE + AB_PAD
    col_sharded = {"w_in", "w_up", "w_pp"}

    def gathered(name, after):
        landed = _split_wait(handle[name], after, False, name="gather_wait_" + name)
        return lax.dynamic_update_index_in_dim(landed, own[name], me, 0)

    def wt(name, after):
        landed = gathered(name, after)
        full = _col_sharded(landed) if name in col_sharded else landed.reshape(-1, D)
        return _pad_cols(full, in_p) if name == "w_in" else full

    def conv(after):
        flat = gathered("conv", after).reshape(N_DEV, pack_rows * LANE)
        out, o_ = {}, 0
        for nm, c, n_ in zip(("conv_a", "conv_qkv", "conv_ffn"), conv_sh, conv_n):
            out[nm] = _col_sharded(flat[:, o_:o_ + n_].reshape((N_DEV,) + c.shape))
            o_ += n_
        return out

    pending, mine = {}, {}

    def emit(grads):
        parts = [_col_parts(g[:, :in_cols] if nm == "w_in" else g) if nm in col_sharded else g.reshape(N_DEV, -1, D)
                 for nm, g in grads.items()]
        hs, tok = _split_start(parts, empty_slots([q[0] for q in parts]), True, name="scatter_start_" + "_".join(grads))
        pending.update(zip(grads, hs))
        mine.update({nm: lax.dynamic_index_in_dim(q, me, 0, keepdims=False) for nm, q in zip(grads, parts)})
        return tok

    S = {
        "g_mix": norm_mix_g + tok0[0, 0], "a_log": a_log, "dt_bias": dt_bias, "dn_g": dn_norm_g, "g_ffn": norm_ffn_g,
        "g_ple": norm_ple_g, "g_final": final_norm_g.reshape(1, D),
    }

    loss_v, grad_x, G = _local_step(x[0], p[0, 0], loss_target[0], S, wt, conv, emit)
    loss = lax.psum(loss_v[0, 0], ("x", "y", "c"))

    small_names = ["g_mix", "g_ffn", "g_ple", "g_final", "dn_g", "a_log", "dt_bias", "conv_a", "conv_qkv", "conv_ffn"]
    small_rows, pieces = [], []
    for nm in small_names:
        g_ = G[nm].reshape(-1)
        r_ = -(-g_.size // (8 * LANE)) * 8
        small_rows.append(r_)
        pieces.append(jnp.pad(g_, (0, r_ * LANE - g_.size)).reshape(r_, LANE))
    (small_l,) = _exchange([jnp.concatenate(pieces, axis=0)], False, name="gather_small_grads")
    landed = {nm: _split_wait(h_, grad_x, True, name="scatter_wait_" + nm) for nm, h_ in pending.items()}
    big_l = [landed[nm] for nm in ("w_in", "w_out", "w_up", "w_down", "w_pg", "w_pp")]

    def small_parts(nm):
        i = small_names.index(nm)
        r0 = sum(small_rows[:i])
        shp = G[nm].shape
        return small_l[:, r0:r0 + small_rows[i], :].reshape(N_DEV, -1)[:, :G[nm].size].reshape((N_DEV,) + shp)

    def conv_parts(nm, shard):
        full = small_parts(nm)
        C = shard.shape[-1]
        return lax.dynamic_slice_in_dim(full, me * C, C, axis=2)

    def adam(parts, w_, m_, v_, nm, own_=None):
        shp = w_.shape
        w2, m2, v2 = (t.reshape(parts.shape[1:]) for t in (w_, m_, v_))
        kw = {} if own_ is None else {"own": own_, "me": me.astype(jnp.int32).reshape(1)}
        return tuple(t.reshape(shp) for t in _adam(parts, w2, m2, v2, name="adam_" + nm, **kw))

    res = [
        adam(small_parts("g_mix"), norm_mix_g, m_norm_mix_g, v_norm_mix_g, "norm_mix_g"),
        adam(big_l[0], w_in, m_w_in, v_w_in, "w_in", mine["w_in"]),
        adam(conv_parts("conv_a", conv_a_w), conv_a_w, m_conv_a_w, v_conv_a_w, "conv_a_w"),
        adam(conv_parts("conv_qkv", conv_qkv_w), conv_qkv_w, m_conv_qkv_w, v_conv_qkv_w, "conv_qkv_w"),
        adam(small_parts("a_log"), a_log, m_a_log, v_a_log, "a_log"),
        adam(small_parts("dt_bias"), dt_bias, m_dt_bias, v_dt_bias, "dt_bias"),
        adam(small_parts("dn_g"), dn_norm_g, m_dn_norm_g, v_dn_norm_g, "dn_norm_g"),
        adam(big_l[1], w_out, m_w_out, v_w_out, "w_out", mine["w_out"]),
        adam(small_parts("g_ffn"), norm_ffn_g, m_norm_ffn_g, v_norm_ffn_g, "norm_ffn_g"),
        adam(big_l[2], w_up, m_w_up, v_w_up, "w_up", mine["w_up"]),
        adam(conv_parts("conv_ffn", conv_ffn_w), conv_ffn_w, m_conv_ffn_w, v_conv_ffn_w, "conv_ffn_w"),
        adam(big_l[3], w_down, m_w_down, v_w_down, "w_down", mine["w_down"]),
        adam(small_parts("g_ple"), norm_ple_g, m_norm_ple_g, v_norm_ple_g, "norm_ple_g"),
        adam(big_l[4], w_ple_gate, m_w_ple_gate, v_w_ple_gate, "w_ple_gate", mine["w_pg"]),
        adam(big_l[5], w_ple_proj, m_w_ple_proj, v_w_ple_proj, "w_ple_proj", mine["w_pp"]),
        adam(small_parts("g_final"), final_norm_g.reshape(1, D), m_final_norm_g.reshape(1, D),
             v_final_norm_g.reshape(1, D), "final_norm_g"),
    ]
    res[-1] = tuple(t.reshape(D) for t in res[-1])
    grads, deltas, new_m, new_v = zip(*res)
    return (loss, grad_x[None], *grads, *deltas, *new_m, *new_v)
```

```python
import functools

import jax
import jax.numpy as jnp
from jax import lax
from jax.experimental import pallas as pl
from jax.experimental.pallas import tpu as pltpu

F32 = jnp.float32
BF16 = jnp.bfloat16

EPS = 1e-6
CHUNK = 64
HEAD = 128
LANE = 128
N_DEV = 8
AB_PAD = 512

ADAM_LR = 0.001
ADAM_B1 = 0.9
ADAM_B2 = 0.999
ADAM_EPS = 1e-08
ADAM_WD = 0.01
ADAM_STEP = 10

MESH = pl.DeviceIdType.MESH


def _tile(dim, target, align=LANE):
    if dim <= target:
        return dim
    t = (target // align) * align
    while t > align and dim % t:
        t -= align
    assert dim % t == 0, (dim, target)
    return t


def _params(sem, vmem_mb=48):
    return pltpu.CompilerParams(dimension_semantics=sem, vmem_limit_bytes=vmem_mb << 20)


_DN = {"nn": (((1,), (0,)), ((), ())), "nt": (((1,), (1,)), ((), ())), "tn": (((0,), (0,)), ((), ()))}


def _matmul(a, b, mode, *, name, out_dtypes=(F32,), epilogue=None, extras=(), after=None, tm=1024, tn=1024, tk=2048):
    if mode == "nn":
        (M, K), (K2, N) = a.shape, b.shape
    elif mode == "nt":
        (M, K), (N, K2) = a.shape, b.shape
    else:
        (K, M), (K2, N) = a.shape, b.shape
    assert K == K2, (name, a.shape, b.shape)
    tm, tn, tk = _tile(M, tm), _tile(N, tn), _tile(K, tk)
    nk = K // tk
    n_ex, n_out = len(extras), len(out_dtypes)
    dn = _DN[mode]

    n_tok = 0 if after is None else 1

    def body(a_ref, b_ref, *rest):
        rest = rest[n_tok:]
        ex_refs, out_refs = rest[:n_ex], rest[n_ex:n_ex + n_out]
        part = lax.dot_general(a_ref[...].astype(BF16), b_ref[...].astype(BF16), dn, preferred_element_type=F32)

        def finish(res):
            outs = (res,) if epilogue is None else epilogue(res, *[e[...] for e in ex_refs])
            for o_ref, val in zip(out_refs, outs):
                o_ref[...] = val.astype(o_ref.dtype)

        if nk == 1:
            finish(part)
            return
        acc, k = rest[-1], pl.program_id(2)

        @pl.when(k == 0)
        def _():
            acc[...] = part

        @pl.when(k > 0)
        def _():
            acc[...] += part

        @pl.when(k == nk - 1)
        def _():
            finish(acc[...])

    a_spec = pl.BlockSpec((tk, tm), lambda i, j, k: (k, i)) if mode == "tn" else pl.BlockSpec((tm, tk), lambda i, j, k: (i, k))
    b_spec = pl.BlockSpec((tn, tk), lambda i, j, k: (j, k)) if mode == "nt" else pl.BlockSpec((tk, tn), lambda i, j, k: (k, j))
    mn_spec = pl.BlockSpec((tm, tn), lambda i, j, k: (i, j))
    outs = pl.pallas_call(
        body, name=name, grid=(M // tm, N // tn, nk),
        in_specs=[a_spec, b_spec] + [pl.BlockSpec((8, LANE), lambda i, j, k: (0, 0))] * n_tok + [mn_spec] * n_ex,
        out_specs=[mn_spec] * n_out,
        out_shape=[jax.ShapeDtypeStruct((M, N), dt) for dt in out_dtypes],
        scratch_shapes=[pltpu.VMEM((tm, tn), F32)] if nk > 1 else [],
        compiler_params=_params(("parallel", "parallel", "arbitrary"), 56),
    )(a, b, *([] if after is None else [after]), *extras)
    return outs[0] if n_out == 1 else outs


def _rms_fwd(x, g, *, name):
    T, D = x.shape
    tr = _tile(T, 256, 8)

    def body(x_ref, g_ref, h_ref):
        xv = x_ref[...]
        r = lax.rsqrt(jnp.mean(xv * xv, axis=-1, keepdims=True) + EPS)
        h_ref[...] = (xv * r * g_ref[...]).astype(h_ref.dtype)

    return pl.pallas_call(
        body, name=name, grid=(T // tr,),
        in_specs=[pl.BlockSpec((tr, D), lambda i: (i, 0)), pl.BlockSpec((1, D), lambda i: (0, 0))],
        out_specs=pl.BlockSpec((tr, D), lambda i: (i, 0)),
        out_shape=jax.ShapeDtypeStruct((T, D), BF16),
        compiler_params=_params(("parallel",)),
    )(x, g)


def _rms_bwd(x, g, dh, dres, *, name):
    T, D = x.shape
    tr = _tile(T, 256, 8)

    def body(x_ref, g_ref, dh_ref, dres_ref, dx_ref, dxb_ref, dg_ref):
        xv = x_ref[...]
        r = lax.rsqrt(jnp.mean(xv * xv, axis=-1, keepdims=True) + EPS)
        xh = xv * r
        dh = dh_ref[...]

        @pl.when(pl.program_id(0) == 0)
        def _():
            dg_ref[...] = jnp.zeros_like(dg_ref)

        dg_ref[...] += jnp.sum(dh * xh, axis=0, keepdims=True)
        dxh = dh * g_ref[...]
        dx = dres_ref[...] + r * (dxh - xh * jnp.mean(dxh * xh, axis=-1, keepdims=True))
        dx_ref[...] = dx
        dxb_ref[...] = dx.astype(dxb_ref.dtype)

    row = pl.BlockSpec((tr, D), lambda i: (i, 0))
    vec = pl.BlockSpec((1, D), lambda i: (0, 0))
    return pl.pallas_call(
        body, name=name, grid=(T // tr,),
        in_specs=[row, vec, row, row], out_specs=[row, row, vec],
        out_shape=[jax.ShapeDtypeStruct((T, D), F32), jax.ShapeDtypeStruct((T, D), BF16), jax.ShapeDtypeStruct((1, D), F32)],
        compiler_params=_params(("arbitrary",)),
    )(x, g, dh, dres)


def _final_loss(x, g, tgt, *, name):
    T, D = x.shape
    tr = _tile(T, 256, 8)

    def body(x_ref, g_ref, t_ref, dx_ref, dg_ref, loss_ref):
        xv = x_ref[...]
        r = lax.rsqrt(jnp.mean(xv * xv, axis=-1, keepdims=True) + EPS)
        xh = xv * r
        gv = g_ref[...]
        err = xh * gv - t_ref[...]

        @pl.when(pl.program_id(0) == 0)
        def _():
            dg_ref[...] = jnp.zeros_like(dg_ref)
            loss_ref[...] = jnp.zeros_like(loss_ref)

        part = 0.5 * jnp.sum(jnp.mean(err * err, axis=-1, keepdims=True), axis=0, keepdims=True)
        loss_ref[...] += jnp.broadcast_to(part, loss_ref.shape)
        dy = err * (1.0 / D)
        dg_ref[...] += jnp.sum(dy * xh, axis=0, keepdims=True)
        dxh = dy * gv
        dx_ref[...] = r * (dxh - xh * jnp.mean(dxh * xh, axis=-1, keepdims=True))

    row = pl.BlockSpec((tr, D), lambda i: (i, 0))
    vec = pl.BlockSpec((1, D), lambda i: (0, 0))
    return pl.pallas_call(
        body, name=name, grid=(T // tr,),
        in_specs=[row, vec, row], out_specs=[row, vec, pl.BlockSpec((1, LANE), lambda i: (0, 0))],
        out_shape=[jax.ShapeDtypeStruct((T, D), F32), jax.ShapeDtypeStruct((1, D), F32),
                   jax.ShapeDtypeStruct((1, LANE), F32)],
        compiler_params=_params(("arbitrary",)),
    )(x, g, tgt)


def _ple_bwd(dx3, pp, sg, *, name):
    T, D = dx3.shape
    tr = _tile(T, 256, 8)

    def body(dx_ref, pp_ref, sg_ref, dpg_ref, dpp_ref):
        dx, s = dx_ref[...], sg_ref[...]
        dpg_ref[...] = (dx * pp_ref[...] * s * (1.0 - s)).astype(dpg_ref.dtype)
        dpp_ref[...] = (dx * s).astype(dpp_ref.dtype)

    row = pl.BlockSpec((tr, D), lambda i: (i, 0))
    return pl.pallas_call(
        body, name=name, grid=(T // tr,), in_specs=[row, row, row], out_specs=[row, row],
        out_shape=[jax.ShapeDtypeStruct((T, D), BF16)] * 2, compiler_params=_params(("parallel",)),
    )(dx3, pp, sg)


RC = 64


def _ext(ref, r0, T, before, after):
    parts = []
    if before:
        p0 = pl.multiple_of(jnp.maximum(r0 - 8, 0), 8)
        parts.append(jnp.where(r0 > 0, ref[pl.ds(p0, 8), :], 0.0))
    parts.append(ref[pl.ds(r0, RC), :])
    if after:
        n0 = pl.multiple_of(jnp.minimum(r0 + RC, T - 8), 8)
        parts.append(jnp.where(r0 + RC < T, ref[pl.ds(n0, 8), :], 0.0))
    return parts[0] if len(parts) == 1 else jnp.concatenate(parts, axis=0)


def _down(xx, s):
    return (xx if s == 0 else pltpu.roll(xx, s, 0))[8:, :]


def _up(xx, s, rows):
    return (xx if s == 0 else pltpu.roll(xx, xx.shape[0] - s, 0))[:rows, :]


def _conv_down(xx, w_ref, K):
    y = None
    for j in range(K):
        t = _down(xx, K - 1 - j) * w_ref[j:j + 1, :]
        y = t if y is None else y + t
    return y


def _fold8(x):
    return jnp.sum(x.reshape(x.shape[0] // 8, 8, x.shape[1]), axis=0)


def _silu(x):
    return x * jax.nn.sigmoid(x)


def _dsilu(x):
    s = jax.nn.sigmoid(x)
    return s * (1.0 + x * (1.0 - s))


def _col_specs(T, offs):
    return [pl.BlockSpec((T, LANE), functools.partial(lambda o, j: (0, o + j), o)) for o in offs]


def _group_a_fwd(proj, conv_w, CW, *, name):
    T = proj.shape[0]
    nb = CW // LANE
    K = conv_w.shape[0]

    def body(ax_ref, ab_ref, ac_ref, w_ref, y_ref):
        def step(i, carry):
            r0 = pl.multiple_of(i * RC, RC)
            m = _ext(ac_ref, r0, T, True, False) * _ext(ax_ref, r0, T, True, False)
            y_ref[pl.ds(r0, RC), :] = (ab_ref[pl.ds(r0, RC), :] * _conv_down(m, w_ref, K)).astype(y_ref.dtype)
            return carry
        lax.fori_loop(0, T // RC, step, 0)

    return pl.pallas_call(
        body, name=name, grid=(nb,),
        in_specs=_col_specs(T, (0, nb, 2 * nb)) + [pl.BlockSpec((K, LANE), lambda j: (0, j))],
        out_specs=pl.BlockSpec((T, LANE), lambda j: (0, j)),
        out_shape=jax.ShapeDtypeStruct((T, CW), BF16), compiler_params=_params(("parallel",)),
    )(proj, proj, proj, conv_w)


def _group_a_bwd(proj, conv_w, dycat, CW, *, name):
    T = proj.shape[0]
    nb = CW // LANE
    K = conv_w.shape[0]

    def body(ax_ref, ab_ref, ac_ref, w_ref, dy_ref, dax_ref, dab_ref, dac_ref, dw_ref):
        def step(i, accs):
            r0 = pl.multiple_of(i * RC, RC)
            ax3 = _ext(ax_ref, r0, T, True, True)
            ac3 = _ext(ac_ref, r0, T, True, True)
            m3 = ax3 * ac3
            c = _conv_down(m3[:RC + 8], w_ref, K)
            dy = dy_ref[pl.ds(r0, RC), :]
            dab_ref[pl.ds(r0, RC), :] = (dy * c).astype(dab_ref.dtype)
            dc2 = _ext(dy_ref, r0, T, False, True) * _ext(ab_ref, r0, T, False, True)
            dm = None
            new = []
            for j in range(K):
                s = K - 1 - j
                t = _up(dc2, s, RC) * w_ref[j:j + 1, :]
                dm = t if dm is None else dm + t
                new.append(accs[j] + _fold8(dc2[:RC] * _down(m3[:RC + 8], s)))
            dax_ref[pl.ds(r0, RC), :] = (dm * ac3[8:RC + 8]).astype(dax_ref.dtype)
            dac_ref[pl.ds(r0, RC), :] = (dm * ax3[8:RC + 8]).astype(dac_ref.dtype)
            return tuple(new)

        accs = lax.fori_loop(0, T // RC, step, tuple(jnp.zeros((8, LANE), F32) for _ in range(K)))
        for j in range(K):
            dw_ref[j:j + 1, :] = jnp.sum(accs[j], axis=0, keepdims=True)

    col = pl.BlockSpec((T, LANE), lambda j: (0, j))
    wsp = pl.BlockSpec((K, LANE), lambda j: (0, j))
    return pl.pallas_call(
        body, name=name, grid=(nb,),
        in_specs=_col_specs(T, (0, nb, 2 * nb)) + [wsp, col],
        out_specs=[col, col, col, wsp],
        out_shape=[jax.ShapeDtypeStruct((T, CW), BF16)] * 3 + [jax.ShapeDtypeStruct((K, CW), F32)],
        compiler_params=_params(("parallel",)),
    )(proj, proj, proj, conv_w, dycat)


def _qkv_fwd(proj, conv_w, off, H, *, name):
    T = proj.shape[0]
    nb = 3 * H
    K = conv_w.shape[0]

    def body(x_ref, w_ref, y_ref):
        j = pl.program_id(0)
        is_qk = j < 2 * H
        scale = jnp.where(j < H, HEAD ** -0.5, 1.0).astype(F32)

        def step(i, carry):
            r0 = pl.multiple_of(i * RC, RC)
            s = _silu(_conv_down(_ext(x_ref, r0, T, True, False), w_ref, K))
            r = lax.rsqrt(jnp.sum(s * s, axis=-1, keepdims=True) + EPS) * scale
            y_ref[pl.ds(r0, RC), :] = s * jnp.where(is_qk, r, 1.0)
            return carry
        lax.fori_loop(0, T // RC, step, 0)

    return pl.pallas_call(
        body, name=name, grid=(nb,),
        in_specs=_col_specs(T, (off,)) + [pl.BlockSpec((K, LANE), lambda j: (0, j))],
        out_specs=pl.BlockSpec((T, LANE), lambda j: (0, j)),
        out_shape=jax.ShapeDtypeStruct((T, nb * LANE), F32), compiler_params=_params(("parallel",)),
    )(proj, conv_w)


def _qkv_bwd(proj, conv_w, dq, dk, dv, off, H, *, name):
    T = proj.shape[0]
    nb = 3 * H
    K = conv_w.shape[0]

    def body(x_ref, w_ref, dq_ref, dk_ref, dv_ref, dx_ref, dw_ref):
        j = pl.program_id(0)
        is_qk = j < 2 * H
        scale = jnp.where(j < H, HEAD ** -0.5, 1.0).astype(F32)

        def step(i, accs):
            r0 = pl.multiple_of(i * RC, RC)
            x3 = _ext(x_ref, r0, T, True, True)
            c2 = _conv_down(x3, w_ref, K)
            s2 = _silu(c2)
            dn2 = jnp.where(j < H, _ext(dq_ref, r0, T, False, True),
                            jnp.where(is_qk, _ext(dk_ref, r0, T, False, True), _ext(dv_ref, r0, T, False, True)))
            r = lax.rsqrt(jnp.sum(s2 * s2, axis=-1, keepdims=True) + EPS)
            nh = s2 * r
            dnp = dn2 * scale
            ds_qk = r * (dnp - nh * jnp.sum(dnp * nh, axis=-1, keepdims=True))
            ds2 = jnp.where(is_qk, ds_qk, dn2)
            dc2 = ds2 * _dsilu(c2)
            dx = None
            new = []
            for jj in range(K):
                s = K - 1 - jj
                t = _up(dc2, s, RC) * w_ref[jj:jj + 1, :]
                dx = t if dx is None else dx + t
                new.append(accs[jj] + _fold8(dc2[:RC] * _down(x3[:RC + 8], s)))
            dx_ref[pl.ds(r0, RC), :] = dx.astype(dx_ref.dtype)
            return tuple(new)

        accs = lax.fori_loop(0, T // RC, step, tuple(jnp.zeros((8, LANE), F32) for _ in range(K)))
        for jj in range(K):
            dw_ref[jj:jj + 1, :] = jnp.sum(accs[jj], axis=0, keepdims=True)

    col = pl.BlockSpec((T, LANE), lambda j: (0, j))
    wsp = pl.BlockSpec((K, LANE), lambda j: (0, j))
    return pl.pallas_call(
        body, name=name, grid=(nb,),
        in_specs=_col_specs(T, (off,)) + [wsp] + [
            pl.BlockSpec((T, LANE), functools.partial(lambda o, j: (0, jnp.clip(j - o, 0, H - 1)), o)) for o in (0, H, 2 * H)],
        out_specs=[col, wsp],
        out_shape=[jax.ShapeDtypeStruct((T, nb * LANE), BF16), jax.ShapeDtypeStruct((K, nb * LANE), F32)],
        compiler_params=_params(("parallel",)),
    )(proj, conv_w, dq, dk, dv)


def _softplus(x):
    return jnp.maximum(x, 0.0) + jnp.log(1.0 + jnp.exp(-jnp.abs(x)))


def _gates_fwd(proj, alog, dtb, off, H, *, name):
    T = proj.shape[0]
    tr = _tile(T, 512, CHUNK)

    def body(ab_ref, al_ref, dt_ref, gb_ref, gam_ref):
        ab = ab_ref[...]
        lane = lax.broadcasted_iota(jnp.int32, ab.shape, 1)
        g = -jnp.exp(al_ref[...]) * _softplus(ab + dt_ref[...])
        gb = jnp.where(lane < H, g, jnp.where(lane < 2 * H, jax.nn.sigmoid(ab), 0.0))
        gb_ref[...] = gb
        tril = _tri().astype(F32)
        for c in range(tr // CHUNK):
            rows = slice(c * CHUNK, (c + 1) * CHUNK)
            gam_ref[rows, :] = _mm(tril, gb[rows, :], precision=lax.Precision.HIGHEST)

    vec = pl.BlockSpec((1, LANE), lambda i: (0, 0))
    row = pl.BlockSpec((tr, LANE), lambda i: (i, 0))
    return pl.pallas_call(
        body, name=name, grid=(T // tr,),
        in_specs=[pl.BlockSpec((tr, LANE), lambda i: (i, off)), vec, vec],
        out_specs=[row, row],
        out_shape=[jax.ShapeDtypeStruct((T, LANE), F32)] * 2, compiler_params=_params(("parallel",)),
    )(proj, alog, dtb)


def _gates_bwd(proj, alog, dtb, dgb, off, H, *, name):
    T = proj.shape[0]
    tr = _tile(T, 512, CHUNK)

    def body(ab_ref, al_ref, dt_ref, d_ref, dab_ref, dal_ref, ddt_ref):
        ab, d = ab_ref[...], d_ref[...]
        lane = lax.broadcasted_iota(jnp.int32, ab.shape, 1)
        is_g = lane < H
        triu = _tri(upper=True).astype(F32)
        dg = jnp.concatenate([_mm(triu, d[c * CHUNK:(c + 1) * CHUNK, :], precision=lax.Precision.HIGHEST)
                              for c in range(tr // CHUNK)], axis=0)
        z = ab + dt_ref[...]
        A = -jnp.exp(al_ref[...])
        da = dg * A * jax.nn.sigmoid(z)
        beta = jax.nn.sigmoid(ab)
        db = d * beta * (1.0 - beta)
        dab_ref[...] = jnp.where(is_g, da, jnp.where(lane < 2 * H, db, 0.0)).astype(dab_ref.dtype)

        @pl.when(pl.program_id(0) == 0)
        def _():
            dal_ref[...] = jnp.zeros_like(dal_ref)
            ddt_ref[...] = jnp.zeros_like(ddt_ref)

        dal_ref[...] += jnp.sum(jnp.where(is_g, dg * A * _softplus(z), 0.0), axis=0, keepdims=True)
        ddt_ref[...] += jnp.sum(jnp.where(is_g, da, 0.0), axis=0, keepdims=True)

    vec = pl.BlockSpec((1, LANE), lambda i: (0, 0))
    row = pl.BlockSpec((tr, LANE), lambda i: (i, 0))
    return pl.pallas_call(
        body, name=name, grid=(T // tr,),
        in_specs=[pl.BlockSpec((tr, LANE), lambda i: (i, off)), vec, vec, row],
        out_specs=[row, vec, vec],
        out_shape=[jax.ShapeDtypeStruct((T, LANE), BF16), jax.ShapeDtypeStruct((1, LANE), F32),
                   jax.ShapeDtypeStruct((1, LANE), F32)],
        compiler_params=_params(("arbitrary",)),
    )(proj, alog, dtb, dgb)


def _gated_norm_fwd(o, proj, gn, zoff, *, name):
    T, W = o.shape
    tr = _tile(T, 512, 8)

    def body(o_ref, z_ref, g_ref, y_ref):
        ov = o_ref[...]
        r = lax.rsqrt(jnp.mean(ov * ov, axis=-1, keepdims=True) + EPS)
        y_ref[...] = (ov * r * g_ref[...] * _silu(z_ref[...])).astype(y_ref.dtype)

    blk = pl.BlockSpec((tr, LANE), lambda i, j: (i, j))
    return pl.pallas_call(
        body, name=name, grid=(T // tr, W // LANE),
        in_specs=[blk, pl.BlockSpec((tr, LANE), lambda i, j: (i, zoff + j)), pl.BlockSpec((1, LANE), lambda i, j: (0, 0))],
        out_specs=blk, out_shape=jax.ShapeDtypeStruct((T, W), BF16), compiler_params=_params(("parallel", "parallel")),
    )(o, proj, gn)


def _gated_norm_bwd(o, proj, gn, dycat, zoff, yoff, *, name):
    T, W = o.shape
    tr = _tile(T, 512, 8)

    def body(o_ref, z_ref, g_ref, dy_ref, do_ref, dz_ref, dg_ref):
        ov, zv, gv, dy = o_ref[...], z_ref[...], g_ref[...], dy_ref[...]
        r = lax.rsqrt(jnp.mean(ov * ov, axis=-1, keepdims=True) + EPS)
        nh = ov * r
        s = _silu(zv)

        @pl.when((pl.program_id(0) == 0) & (pl.program_id(1) == 0))
        def _():
            dg_ref[...] = jnp.zeros_like(dg_ref)

        dg_ref[...] += jnp.sum(dy * nh * s, axis=0, keepdims=True)
        dz_ref[...] = (dy * nh * gv * _dsilu(zv)).astype(dz_ref.dtype)
        dn = dy * gv * s
        do_ref[...] = r * (dn - nh * jnp.mean(dn * nh, axis=-1, keepdims=True))

    blk = pl.BlockSpec((tr, LANE), lambda i, j: (i, j))
    vec = pl.BlockSpec((1, LANE), lambda i, j: (0, 0))
    return pl.pallas_call(
        body, name=name, grid=(T // tr, W // LANE),
        in_specs=[blk, pl.BlockSpec((tr, LANE), lambda i, j: (i, zoff + j)), vec,
                  pl.BlockSpec((tr, LANE), lambda i, j: (i, yoff + j))],
        out_specs=[blk, blk, vec],
        out_shape=[jax.ShapeDtypeStruct((T, W), F32), jax.ShapeDtypeStruct((T, W), BF16),
                   jax.ShapeDtypeStruct((1, LANE), F32)],
        compiler_params=_params(("arbitrary", "arbitrary")),
    )(o, proj, gn, dycat)


def _ffn_act_fwd(up_pre, conv_w, *, name):
    T, F2 = up_pre.shape
    nb = F2 // 2 // LANE
    K = conv_w.shape[0]

    def body(g_ref, v_ref, wg_ref, wv_ref, y_ref):
        def step(i, carry):
            r0 = pl.multiple_of(i * RC, RC)
            gate = _conv_down(_ext(g_ref, r0, T, True, False), wg_ref, K)
            val = _conv_down(_ext(v_ref, r0, T, True, False), wv_ref, K)
            y_ref[pl.ds(r0, RC), :] = (_silu(gate) * val).astype(y_ref.dtype)
            return carry
        lax.fori_loop(0, T // RC, step, 0)

    return pl.pallas_call(
        body, name=name, grid=(nb,),
        in_specs=_col_specs(T, (0, nb)) + [pl.BlockSpec((K, LANE), lambda j: (0, j)),
                                           pl.BlockSpec((K, LANE), lambda j: (0, nb + j))],
        out_specs=pl.BlockSpec((T, LANE), lambda j: (0, j)),
        out_shape=jax.ShapeDtypeStruct((T, F2 // 2), BF16), compiler_params=_params(("parallel",)),
    )(up_pre, up_pre, conv_w, conv_w)


def _ffn_act_bwd(up_pre, conv_w, dact, *, name):
    T, F2 = up_pre.shape
    nb = F2 // 2 // LANE
    K = conv_w.shape[0]

    def body(g_ref, v_ref, wg_ref, wv_ref, da_ref, dg_ref, dv_ref, dwg_ref, dwv_ref):
        def step(i, accs):
            r0 = pl.multiple_of(i * RC, RC)
            g3 = _ext(g_ref, r0, T, True, True)
            v3 = _ext(v_ref, r0, T, True, True)
            gate2 = _conv_down(g3, wg_ref, K)
            val2 = _conv_down(v3, wv_ref, K)
            da2 = _ext(da_ref, r0, T, False, True)
            dgate2 = da2 * val2 * _dsilu(gate2)
            dval2 = da2 * _silu(gate2)
            dgp, dvp, new = None, None, []
            for j in range(K):
                s = K - 1 - j
                tg = _up(dgate2, s, RC) * wg_ref[j:j + 1, :]
                tv = _up(dval2, s, RC) * wv_ref[j:j + 1, :]
                dgp = tg if dgp is None else dgp + tg
                dvp = tv if dvp is None else dvp + tv
                new.append(accs[2 * j] + _fold8(dgate2[:RC] * _down(g3[:RC + 8], s)))
                new.append(accs[2 * j + 1] + _fold8(dval2[:RC] * _down(v3[:RC + 8], s)))
            dg_ref[pl.ds(r0, RC), :] = dgp.astype(dg_ref.dtype)
            dv_ref[pl.ds(r0, RC), :] = dvp.astype(dv_ref.dtype)
            return tuple(new)

        accs = lax.fori_loop(0, T // RC, step, tuple(jnp.zeros((8, LANE), F32) for _ in range(2 * K)))
        for j in range(K):
            dwg_ref[j:j + 1, :] = jnp.sum(accs[2 * j], axis=0, keepdims=True)
            dwv_ref[j:j + 1, :] = jnp.sum(accs[2 * j + 1], axis=0, keepdims=True)

    col = pl.BlockSpec((T, LANE), lambda j: (0, j))
    wsp = pl.BlockSpec((K, LANE), lambda j: (0, j))
    return pl.pallas_call(
        body, name=name, grid=(nb,),
        in_specs=_col_specs(T, (0, nb)) + [wsp, pl.BlockSpec((K, LANE), lambda j: (0, nb + j)), col],
        out_specs=[col, col, wsp, wsp],
        out_shape=[jax.ShapeDtypeStruct((T, F2 // 2), BF16)] * 2 + [jax.ShapeDtypeStruct((K, F2 // 2), F32)] * 2,
        compiler_params=_params(("parallel",)),
    )(up_pre, up_pre, conv_w, conv_w, dact)


CPB = 8
CPB_SCAN = 4
GRP = 8
HP = lax.Precision.HIGH


def _tri(strict=False, upper=False):
    r = lax.broadcasted_iota(jnp.int32, (CHUNK, CHUNK), 0)
    c = lax.broadcasted_iota(jnp.int32, (CHUNK, CHUNK), 1)
    if upper:
        return c >= r
    return (r > c) if strict else (r >= c)


def _mm(a, b, dn="nn", precision=None):
    precision = HP if precision is None else precision
    return lax.dot_general(a, b, _DN[dn], precision=precision, preferred_element_type=F32)


def _each(f, *cols):
    return [f(*xs) for xs in zip(*cols)]


def _decay(gam):
    return jnp.exp(jnp.where(_tri(), gam[:, :CHUNK] - gam.T[:CHUNK, :], -1e30))


def _delta_specs(T, H, cpb):
    rows = cpb * CHUNK
    col = lambda o: pl.BlockSpec((rows, LANE), functools.partial(lambda o, h, n: (n, o + h), o))
    bc = pl.BlockSpec((1, rows, LANE), lambda h, n: (h, n, 0))
    sq = pl.BlockSpec((1, cpb, CHUNK, CHUNK), lambda h, n: (h, n, 0, 0))
    vec = pl.BlockSpec((1, cpb, 1, LANE), lambda h, n: (h, n, 0, 0))
    return col, bc, sq, vec


def _delta_prep_fwd(qkv, gamB, bB, H, *, name):
    T = qkv.shape[0]
    N = T // CHUNK
    cpb = _tile(N, CPB, 8)
    grp = min(GRP, cpb)
    col, bc, sq, vec = _delta_specs(T, H, cpb)

    def body(q_ref, k_ref, v_ref, g_ref, b_ref, u_ref, w_ref, qd_ref, kd_ref, qk_ref, ti_ref, gl_ref):
        eye = (lax.broadcasted_iota(jnp.int32, (CHUNK, CHUNK), 0) == lax.broadcasted_iota(jnp.int32, (CHUNK, CHUNK), 1)).astype(F32)
        strict = _tri(strict=True)
        for c0 in range(0, cpb, grp):
            cs = list(range(c0, c0 + grp))
            rows = [slice(c * CHUNK, (c + 1) * CHUNK) for c in cs]
            q, k, v = ([r_[r, :] for r in rows] for r_ in (q_ref, k_ref, v_ref))
            bb = [b_ref[0, r, :] for r in rows]
            gam = [g_ref[0, r, :] for r in rows]
            D = _each(_decay, gam)
            e = _each(jnp.exp, gam)
            kk = _each(lambda k_: _mm(k_, k_, "nt"), k)
            X = _each(lambda kk_, D_, b_: -(jnp.where(strict, kk_ * D_, 0.0) * b_[:, :CHUNK]), kk, D, bb)
            R = _each(lambda x: eye + x, X)
            for _ in range(5):
                X = _each(lambda x: _mm(x, x), X)
                R = _each(lambda r, x: r + _mm(r, x), R, X)
            u = _each(lambda r, b_, v_: _mm(r, b_ * v_), R, bb, v)
            w = _each(lambda r, b_, e_, k_: _mm(r, b_ * e_ * k_), R, bb, e, k)
            qk = _each(lambda q_, k_, D_: _mm(q_, k_, "nt") * D_, q, k, D)
            for i, c in enumerate(cs):
                glast = gam[i][CHUNK - 1:CHUNK, :]
                u_ref[rows[i], :] = u[i]
                w_ref[rows[i], :] = w[i]
                qd_ref[rows[i], :] = e[i] * q[i]
                kd_ref[rows[i], :] = jnp.exp(glast - gam[i]) * k[i]
                qk_ref[0, c] = qk[i]
                ti_ref[0, c] = R[i]
                gl_ref[0, c] = jnp.exp(glast)

    full = jax.ShapeDtypeStruct((T, H * LANE), F32)
    sqs = jax.ShapeDtypeStruct((H, N, CHUNK, CHUNK), F32)
    return pl.pallas_call(
        body, name=name, grid=(H, N // cpb),
        in_specs=[col(0), col(H), col(2 * H), bc, bc],
        out_specs=[col(0)] * 4 + [sq, sq, vec],
        out_shape=[full] * 4 + [sqs, sqs, jax.ShapeDtypeStruct((H, N, 1, LANE), F32)],
        compiler_params=_params(("parallel", "parallel")),
    )(qkv, qkv, qkv, gamB, bB)


def _scan_specs(H, N, cpb, hb, rev):
    nbk = N // cpb
    blk = (lambda n: nbk - 1 - n) if rev else (lambda n: n)
    col = pl.BlockSpec((cpb * CHUNK, hb * LANE), lambda h, n: (blk(n), h))
    sq = pl.BlockSpec((hb, cpb, CHUNK, CHUNK), lambda h, n: (h, blk(n), 0, 0))
    vec = pl.BlockSpec((hb, cpb, 1, LANE), lambda h, n: (h, blk(n), 0, 0))
    st = pl.BlockSpec((hb, cpb, HEAD, HEAD), lambda h, n: (h, blk(n), 0, 0))
    return col, sq, vec, st


def _delta_scan_fwd(u, w, qd, kd, qk, gl, H, *, name):
    T = u.shape[0]
    N = T // CHUNK
    cpb = _tile(N, CPB_SCAN, 4)
    hb = min(GRP, H)
    col, sq, vec, st = _scan_specs(H, N, cpb, hb, False)
    lanes = [slice(j * LANE, (j + 1) * LANE) for j in range(hb)]
    heads = list(range(hb))

    def body(u_ref, w_ref, qd_ref, kd_ref, qk_ref, gl_ref, o_ref, vn_ref, ss_ref, s_scr):
        @pl.when(pl.program_id(1) == 0)
        def _():
            s_scr[...] = jnp.zeros_like(s_scr)

        def step(c, states):
            rows = pl.ds(pl.multiple_of(c * CHUNK, CHUNK), CHUNK)
            S = list(states)
            for j in heads:
                ss_ref[j, c] = S[j]
            wS = _each(lambda ln, s: _mm(w_ref[rows, ln], s), lanes, S)
            qS = _each(lambda ln, s: _mm(qd_ref[rows, ln], s), lanes, S)
            vn = _each(lambda ln, ws: u_ref[rows, ln] - ws, lanes, wS)
            o = _each(lambda j, qs, vn_: qs + _mm(qk_ref[j, c], vn_), heads, qS, vn)
            new = _each(lambda j, ln, s, vn_: s * gl_ref[j, c] + _mm(kd_ref[rows, ln], vn_, "tn"),
                        heads, lanes, S, vn)
            for j in heads:
                o_ref[rows, lanes[j]] = o[j]
                vn_ref[rows, lanes[j]] = vn[j]
            return tuple(new)
        out = lax.fori_loop(0, cpb, step, tuple(s_scr[j] for j in heads))
        for j in heads:
            s_scr[j] = out[j]

    full = jax.ShapeDtypeStruct((T, H * LANE), F32)
    return pl.pallas_call(
        body, name=name, grid=(H // hb, N // cpb),
        in_specs=[col] * 4 + [sq, vec],
        out_specs=[col, col, st],
        out_shape=[full, full, jax.ShapeDtypeStruct((H, N, HEAD, HEAD), F32)],
        scratch_shapes=[pltpu.VMEM((hb, HEAD, HEAD), F32)],
        compiler_params=_params(("parallel", "arbitrary")),
    )(u, w, qd, kd, qk, gl)


def _delta_scan_bwd(do, w, qd, kd, vn, qk, gl, ss, H, *, name):
    T = do.shape[0]
    N = T // CHUNK
    cpb = _tile(N, CPB_SCAN, 4)
    hb = min(GRP, H)
    col, sq, vec, st = _scan_specs(H, N, cpb, hb, True)
    lanes = [slice(j * LANE, (j + 1) * LANE) for j in range(hb)]
    heads = list(range(hb))

    def body(do_ref, w_ref, qd_ref, kd_ref, vn_ref, qk_ref, gl_ref, ss_ref,
             du_ref, dw_ref, dqd_ref, dkd_ref, dqk_ref, dgl_ref, ds_scr):
        @pl.when(pl.program_id(1) == 0)
        def _():
            ds_scr[...] = jnp.zeros_like(ds_scr)

        def step(i, dstates):
            c = cpb - 1 - i
            rows = pl.ds(pl.multiple_of(c * CHUNK, CHUNK), CHUNK)
            dS = list(dstates)
            S = [ss_ref[j, c] for j in heads]
            dov = [do_ref[rows, ln] for ln in lanes]
            vnv = [vn_ref[rows, ln] for ln in lanes]
            a1 = _each(lambda j, d_: _mm(qk_ref[j, c], d_, "tn"), heads, dov)
            a2 = _each(lambda ln, ds: _mm(kd_ref[rows, ln], ds), lanes, dS)
            dvn = _each(lambda x, y: x + y, a1, a2)
            dqd = _each(lambda d_, s: _mm(d_, s, "nt"), dov, S)
            dkd = _each(lambda v_, ds: _mm(v_, ds, "nt"), vnv, dS)
            dqk = _each(lambda d_, v_: _mm(d_, v_, "nt"), dov, vnv)
            dw = _each(lambda dv_, s: -_mm(dv_, s, "nt"), dvn, S)
            b1 = _each(lambda ln, d_: _mm(qd_ref[rows, ln], d_, "tn"), lanes, dov)
            b2 = _each(lambda ln, dv_: _mm(w_ref[rows, ln], dv_, "tn"), lanes, dvn)
            new = _each(lambda j, x, y, ds: x + ds * gl_ref[j, c] - y, heads, b1, b2, dS)
            for j in heads:
                du_ref[rows, lanes[j]] = dvn[j]
                dw_ref[rows, lanes[j]] = dw[j]
                dqd_ref[rows, lanes[j]] = dqd[j]
                dkd_ref[rows, lanes[j]] = dkd[j]
                dqk_ref[j, c] = dqk[j]
                dgl = jnp.sum(jnp.sum(dS[j] * S[j], axis=1, keepdims=True), axis=0, keepdims=True)
                dgl_ref[j, c] = jnp.broadcast_to(dgl, (1, LANE))
            return tuple(new)
        out = lax.fori_loop(0, cpb, step, tuple(ds_scr[j] for j in heads))
        for j in heads:
            ds_scr[j] = out[j]

    full = jax.ShapeDtypeStruct((T, H * LANE), F32)
    return pl.pallas_call(
        body, name=name, grid=(H // hb, N // cpb),
        in_specs=[col] * 5 + [sq, vec, st],
        out_specs=[col] * 4 + [sq, vec],
        out_shape=[full] * 4 + [jax.ShapeDtypeStruct((H, N, CHUNK, CHUNK), F32), jax.ShapeDtypeStruct((H, N, 1, LANE), F32)],
        scratch_shapes=[pltpu.VMEM((hb, HEAD, HEAD), F32)],
        compiler_params=_params(("parallel", "arbitrary")),
    )(do, w, qd, kd, vn, qk, gl, ss)


def _delta_prep_bwd(qkv, gamB, bB, ti, u, w, qk, du, dw, dqd, dkd, dqk, dgl, H, *, name):
    T = qkv.shape[0]
    N = T // CHUNK
    cpb = _tile(N, CPB, 8)
    grp = min(GRP, cpb)
    col, bc, sq, vec = _delta_specs(T, H, cpb)

    def body(q_ref, k_ref, v_ref, g_ref, b_ref, ti_ref, u_ref, w_ref, qk_ref,
             du_ref, dw_ref, dqd_ref, dkd_ref, dqk_ref, dgl_ref,
             dq_ref, dk_ref, dv_ref, dg_ref, db_ref):
        ones = jnp.ones((CHUNK, LANE), F32)
        strict = _tri(strict=True)
        last = lax.broadcasted_iota(jnp.int32, (CHUNK, LANE), 0) == CHUNK - 1
        lsum = lambda x: jnp.sum(x, axis=-1, keepdims=True)
        for c0 in range(0, cpb, grp):
            cs = list(range(c0, c0 + grp))
            rows = [slice(c * CHUNK, (c + 1) * CHUNK) for c in cs]
            ld = lambda r_: [r_[r, :] for r in rows]
            q, k, v, uv, wv, duv, dwv, dqd_v, dkd_v = (ld(r_) for r_ in (q_ref, k_ref, v_ref, u_ref, w_ref, du_ref, dw_ref, dqd_ref, dkd_ref))
            bb = [b_ref[0, r, :] for r in rows]
            gam = [g_ref[0, r, :] for r in rows]
            Ti = [ti_ref[0, c] for c in cs]
            QK = [qk_ref[0, c] for c in cs]
            dqk_v = [dqk_ref[0, c] for c in cs]
            D = _each(_decay, gam)
            e = _each(jnp.exp, gam)
            glast = [g_[CHUNK - 1:CHUNK, :] for g_ in gam]
            eL = _each(lambda gl_, g_: jnp.exp(gl_ - g_), glast, gam)
            kk = _each(lambda k_: _mm(k_, k_, "nt"), k)
            KKD = _each(lambda kk_, D_: jnp.where(strict, kk_ * D_, 0.0), kk, D)
            dru = _each(lambda t, d_: _mm(t, d_, "tn"), Ti, duv)
            drw = _each(lambda t, d_: _mm(t, d_, "tn"), Ti, dwv)
            l1 = _each(lambda a, b: _mm(a, b, "nt"), dru, uv)
            l2 = _each(lambda a, b: _mm(a, b, "nt"), drw, wv)
            dL = _each(lambda a, b: jnp.where(strict, -(a + b), 0.0), l1, l2)
            Mm = _each(lambda dl, b_: dl * b_[:, :CHUNK], dL, bb)
            dKK = _each(lambda m_, D_: m_ * D_, Mm, D)
            dQK = _each(lambda a, D_: a * D_, dqk_v, D)
            P = _each(lambda m_, kkd, a, qk_: m_ * kkd + a * qk_, Mm, KKD, dqk_v, QK)
            q1 = _each(lambda a, k_: _mm(a, k_), dQK, k)
            k1 = _each(lambda a, q_: _mm(a, q_, "tn"), dQK, q)
            k2 = _each(lambda a, k_: _mm(a, k_), dKK, k)
            k3 = _each(lambda a, k_: _mm(a, k_, "tn"), dKK, k)
            s1 = _each(lambda dl, kkd: _mm(dl * kkd, ones), dL, KKD)
            p1 = _each(lambda p_: _mm(p_, ones), P)
            p2 = _each(lambda p_: _mm(p_, ones, "tn"), P)
            for i, c in enumerate(cs):
                r = rows[i]
                bek = bb[i] * e[i]
                kdv = eL[i] * k[i]
                dq_ref[r, :] = q1[i] + e[i] * dqd_v[i]
                dk_ref[r, :] = k1[i] + k2[i] + k3[i] + bek * drw[i] + eL[i] * dkd_v[i]
                dv_ref[r, :] = bb[i] * dru[i]
                db_ref[0, r, :] = s1[i] + lsum(dru[i] * v[i]) + lsum(drw[i] * e[i] * k[i])
                dgam = (p1[i] - p2[i] + lsum(drw[i] * bek * k[i]) + lsum(dqd_v[i] * e[i] * q[i])
                        - lsum(dkd_v[i] * kdv))
                xlast = jnp.sum(lsum(dkd_v[i] * kdv), axis=0, keepdims=True) + jnp.exp(glast[i]) * dgl_ref[0, c]
                dg_ref[0, r, :] = dgam + jnp.where(last, xlast, 0.0)

    full = jax.ShapeDtypeStruct((T, H * LANE), F32)
    bcs = jax.ShapeDtypeStruct((H, T, LANE), F32)
    return pl.pallas_call(
        body, name=name, grid=(H, N // cpb),
        in_specs=[col(0), col(H), col(2 * H), bc, bc, sq, col(0), col(0), sq, col(0), col(0), col(0), col(0), sq, vec],
        out_specs=[col(0), col(0), col(0), bc, bc],
        out_shape=[full, full, full, bcs, bcs],
        compiler_params=_params(("parallel", "parallel")),
    )(qkv, qkv, qkv, gamB, bB, ti, u, w, qk, du, dw, dqd, dkd, dqk, dgl)


def _adam(parts, w, m, v, *, name, own=None, me=None):
    P, R, C = parts.shape
    tr = _tile(R, 256, 8)
    n_own = 0 if own is None else 2

    def body(*refs):
        p_ref, w_ref, m_ref, v_ref, g_ref, d_ref, nm_ref, nv_ref = refs[n_own:]
        g = None
        for i in range(P):
            t = p_ref[i].astype(F32)
            if n_own:
                t = jnp.where(refs[0][0] == i, refs[1][...].astype(F32), t)
            g = t if g is None else g + t
        mn = ADAM_B1 * m_ref[...] + (1.0 - ADAM_B1) * g
        vn = ADAM_B2 * v_ref[...] + (1.0 - ADAM_B2) * (g * g)
        m_hat = mn / (1.0 - ADAM_B1 ** ADAM_STEP)
        v_hat = vn / (1.0 - ADAM_B2 ** ADAM_STEP)
        g_ref[...] = g
        d_ref[...] = -ADAM_LR * (m_hat / (jnp.sqrt(v_hat) + ADAM_EPS) + ADAM_WD * w_ref[...])
        nm_ref[...] = mn
        nv_ref[...] = vn

    blk = pl.BlockSpec((tr, C), lambda i: (i, 0))
    return pl.pallas_call(
        body, name=name, grid=(R // tr,),
        in_specs=[pl.BlockSpec(memory_space=pltpu.SMEM), blk][:n_own] + [pl.BlockSpec((P, tr, C), lambda i: (0, i, 0)), blk, blk, blk],
        out_specs=[blk] * 4, out_shape=[jax.ShapeDtypeStruct((R, C), F32)] * 4,
        compiler_params=_params(("parallel",)),
    )(*([me, own] if n_own else []), parts, w, m, v)


def _mesh_pos():
    return lax.axis_index("x"), lax.axis_index("y"), lax.axis_index("c")


def _peer(k):
    x, y, c = _mesh_pos()
    px, py, pc = x ^ ((k >> 2) & 1), y ^ ((k >> 1) & 1), c ^ (k & 1)
    return (px, py, pc), 4 * px + 2 * py + pc


def _exchange(arrays, scatter, *, name):
    n = len(arrays)
    blocks = [a.shape[1:] if scatter else a.shape for a in arrays]

    def body(*refs):
        srcs, dsts = refs[:n], refs[n:2 * n]
        send_sems, recv_sems, local_sems = refs[2 * n:]
        x, y, c = _mesh_pos()
        me = 4 * x + 2 * y + c
        local, sends = [], []
        for a in range(n):
            cp = pltpu.make_async_copy(srcs[a].at[me] if scatter else srcs[a], dsts[a].at[me], local_sems.at[a])
            cp.start()
            local.append(cp)
            for k in range(1, N_DEV):
                dev, idx = _peer(k)
                cp = pltpu.make_async_remote_copy(
                    src_ref=srcs[a].at[idx] if scatter else srcs[a], dst_ref=dsts[a].at[me],
                    send_sem=send_sems.at[a * N_DEV + k], recv_sem=recv_sems.at[a * N_DEV + k],
                    device_id=dev, device_id_type=MESH)
                cp.start()
                sends.append(cp)
        for a in range(n):
            for k in range(1, N_DEV):
                dev, idx = _peer(k)
                pltpu.make_async_remote_copy(
                    src_ref=srcs[a].at[idx] if scatter else srcs[a], dst_ref=dsts[a].at[idx],
                    send_sem=send_sems.at[a * N_DEV + k], recv_sem=recv_sems.at[a * N_DEV + k],
                    device_id=dev, device_id_type=MESH).wait_recv()
        for cp in sends:
            cp.wait_send()
        for cp in local:
            cp.wait()

    anyspec = pl.BlockSpec(memory_space=pl.ANY)
    return pl.pallas_call(
        body, name=name, in_specs=[anyspec] * n, out_specs=[anyspec] * n,
        out_shape=[jax.ShapeDtypeStruct((N_DEV,) + tuple(b), a.dtype) for a, b in zip(arrays, blocks)],
        scratch_shapes=[pltpu.SemaphoreType.DMA((n * N_DEV,)), pltpu.SemaphoreType.DMA((n * N_DEV,)),
                        pltpu.SemaphoreType.DMA((n,))],
    )(*arrays)


_ANY = pl.BlockSpec(memory_space=pl.ANY)
_SEM = pl.BlockSpec(memory_space=pltpu.SEMAPHORE)
_EFFECT = pltpu.SideEffectType.DATAFLOW_SIDE_EFFECTING


def _in_hbm(a):
    return pltpu.with_memory_space_constraint(a, pltpu.HBM)


def _split_copy(src, land, send, recv, k, me, scatter, landed):
    dev, idx = _peer(k)
    return pltpu.make_async_remote_copy(
        src_ref=src.at[idx] if scatter else src, dst_ref=land.at[idx if landed else me],
        send_sem=send.at[k], recv_sem=recv.at[k], device_id=dev, device_id_type=MESH)


def _split_start(srcs, lands, scatter, *, name):
    n = len(srcs)

    def body(*refs):
        src, land, send, recv, token = refs[:n], refs[n:2 * n], refs[2 * n:3 * n], refs[3 * n:4 * n], refs[-1]
        x, y, c = _mesh_pos()
        me = 4 * x + 2 * y + c
        for a in range(n):
            for k in range(1, N_DEV):
                _split_copy(src[a], land[a], send[a], recv[a], k, me, scatter, False).start()
        token[...] = jnp.zeros_like(token)

    outs = pl.pallas_call(
        body, name=name,
        out_shape=[pltpu.SemaphoreType.DMA((N_DEV,))] * (2 * n) + [pltpu.HBM(t.shape, t.dtype) for t in list(srcs) + list(lands)]
        + [jax.ShapeDtypeStruct((8, LANE), F32)],
        in_specs=[_ANY] * (2 * n), out_specs=[_SEM] * (2 * n) + [_ANY] * (2 * n) + [pl.BlockSpec(memory_space=pltpu.VMEM)],
        input_output_aliases={i: 2 * n + i for i in range(2 * n)},
        compiler_params=pltpu.CompilerParams(has_side_effects=_EFFECT),
    )(*[_in_hbm(t) for t in list(srcs) + list(lands)])
    handles = [(outs[a], outs[n + a], outs[2 * n + a], outs[3 * n + a]) for a in range(n)]
    return handles, outs[-1]


def _split_wait(handle, after, scatter, *, name):
    send, recv, src_thru, land_thru = handle

    def body(src_ref, land_ref, send_ref, recv_ref, after_ref, src_out, land_out):
        x, y, c = _mesh_pos()
        me = 4 * x + 2 * y + c
        for k in range(1, N_DEV):
            cp = _split_copy(src_ref, land_ref, send_ref, recv_ref, k, me, scatter, True)
            cp.wait_send()
            cp.wait_recv()

    return pl.pallas_call(
        body, name=name,
        out_shape=(pltpu.HBM(src_thru.shape, src_thru.dtype), pltpu.HBM(land_thru.shape, land_thru.dtype)),
        in_specs=(_ANY, _ANY, _SEM, _SEM, _ANY), out_specs=(_ANY, _ANY), input_output_aliases={0: 0, 1: 1},
        compiler_params=pltpu.CompilerParams(has_side_effects=_EFFECT),
    )(src_thru, land_thru, send, recv, after)[1]


def _local_step(x, p, tgt, S, wt, conv, emit):
    T, D = x.shape
    CW = DNW = D // 2
    H = DNW // HEAD
    nA, nD = CW // LANE, DNW // LANE
    qkv_off, z_off, ab_off = 3 * nA, 3 * nA + 3 * nD, 3 * nA + 4 * nD
    alog = jnp.pad(S["a_log"], ((0, 0), (0, LANE - H)))
    dtb = jnp.pad(S["dt_bias"], ((0, 0), (0, LANE - H)))
    add = lambda acc, r: (acc + r,)

    h1 = _rms_fwd(x, S["g_mix"], name="rms1_fwd")
    w_in, cv = wt("w_in", h1), conv(h1)
    proj = _matmul(h1, w_in, "nn", name="mm_in")
    y_a = _group_a_fwd(proj, cv["conv_a"], CW, name="group_a_fwd")
    qkv = _qkv_fwd(proj, cv["conv_qkv"], qkv_off, H, name="qkv_fwd")
    gb, gamc = _gates_fwd(proj, alog, dtb, ab_off, H, name="gates_fwd")
    bcast = lambda cols: jnp.broadcast_to(cols.T[:, :, None], (H, T, LANE))
    gamB, bB = bcast(gamc[:, :H]), bcast(gb[:, H:2 * H])
    u, w, qd, kd, qk, ti, gl = _delta_prep_fwd(qkv, gamB, bB, H, name="delta_prep_fwd")
    o, vn, ss = _delta_scan_fwd(u, w, qd, kd, qk, gl, H, name="delta_scan_fwd")
    y_b = _gated_norm_fwd(o, proj, S["dn_g"], z_off, name="gated_norm_fwd")
    ycat = jnp.concatenate([y_a, y_b], axis=1)
    w_out = wt("w_out", ycat)
    x1 = _matmul(ycat, w_out, "nn", name="mm_out", epilogue=add, extras=(x,))
    h2 = _rms_fwd(x1, S["g_ffn"], name="rms2_fwd")
    w_up = wt("w_up", h2)
    up_pre = _matmul(h2, w_up, "nn", name="mm_up")
    act = _ffn_act_fwd(up_pre, cv["conv_ffn"], name="ffn_act_fwd")
    w_down = wt("w_down", act)
    x2 = _matmul(act, w_down, "nn", name="mm_down", epilogue=add, extras=(x1,))
    h3 = _rms_fwd(x2, S["g_ple"], name="rms3_fwd")
    w_pp, w_pg = wt("w_pp", h3), wt("w_pg", h3)
    pp = _matmul(p, w_pp, "nn", name="mm_pp")

    def ple_epi(acc, x2r, ppr):
        s = jax.nn.sigmoid(acc)
        return x2r + s * ppr, s

    x3, sg = _matmul(h3, w_pg, "nn", name="mm_pg", out_dtypes=(F32, F32), epilogue=ple_epi, extras=(x2, pp), tm=512)
    dx3, dg_final, loss = _final_loss(x3, S["g_final"], tgt, name="final_loss")

    G = {"g_final": dg_final}
    dpg, dpp = _ple_bwd(dx3, pp, sg, name="ple_bwd")
    tok = emit({"w_pp": _matmul(p, dpp, "tn", name="mm_dwpp", out_dtypes=(BF16,)),
                "w_pg": _matmul(h3, dpg, "tn", name="mm_dwpg", out_dtypes=(BF16,))})
    dh3 = _matmul(dpg, w_pg, "nt", name="mm_dh3", after=tok)
    dx2, dx2b, G["g_ple"] = _rms_bwd(x2, S["g_ple"], dh3, dx3, name="rms3_bwd")
    tok = emit({"w_down": _matmul(act, dx2b, "tn", name="mm_dwdown", out_dtypes=(BF16,))})
    dact = _matmul(dx2b, w_down, "nt", name="mm_dact", after=tok)
    dup_g, dup_v, dcf_g, dcf_v = _ffn_act_bwd(up_pre, cv["conv_ffn"], dact, name="ffn_act_bwd")
    G["conv_ffn"] = jnp.concatenate([dcf_g, dcf_v], axis=1)
    dup = jnp.concatenate([dup_g, dup_v], axis=1)
    tok = emit({"w_up": _matmul(h2, dup, "tn", name="mm_dwup", out_dtypes=(BF16,))})
    dh2 = _matmul(dup, w_up, "nt", name="mm_dh2", after=tok)
    dx1, dx1b, G["g_ffn"] = _rms_bwd(x1, S["g_ffn"], dh2, dx2, name="rms2_bwd")
    tok = emit({"w_out": _matmul(ycat, dx1b, "tn", name="mm_dwout", out_dtypes=(BF16,))})
    dycat = _matmul(dx1b, w_out, "nt", name="mm_dycat", after=tok)
    do, dz, G["dn_g"] = _gated_norm_bwd(o, proj, S["dn_g"], dycat, z_off, nA, name="gated_norm_bwd")
    du, dw, dqd, dkd, dqk, dgl = _delta_scan_bwd(do, w, qd, kd, vn, qk, gl, ss, H, name="delta_scan_bwd")
    dq, dk, dv, dgB, dbB = _delta_prep_bwd(qkv, gamB, bB, ti, u, w, qk, du, dw, dqd, dkd, dqk, dgl, H,
                                           name="delta_prep_bwd")
    dgb = jnp.pad(jnp.concatenate([dgB[:, :, 0].T, dbB[:, :, 0].T], axis=1), ((0, 0), (0, LANE - 2 * H)))
    dab, dal, ddt = _gates_bwd(proj, alog, dtb, dgb, ab_off, H, name="gates_bwd")
    G["a_log"], G["dt_bias"] = dal[:, :H], ddt[:, :H]
    dqkv, G["conv_qkv"] = _qkv_bwd(proj, cv["conv_qkv"], dq, dk, dv, qkv_off, H, name="qkv_bwd")
    dax, dab_, dac, G["conv_a"] = _group_a_bwd(proj, cv["conv_a"], dycat, CW, name="group_a_bwd")
    in_p = w_in.shape[1]
    dproj = jnp.concatenate([dax, dab_, dac, dqkv, dz, dab, jnp.zeros((T, in_p - (ab_off + 1) * LANE), BF16)], axis=1)
    tok = emit({"w_in": _matmul(h1, dproj, "tn", name="mm_dwin", out_dtypes=(BF16,))})
    dh1 = _matmul(dproj, w_in, "nt", name="mm_dh1", after=tok)
    grad_x, _, G["g_mix"] = _rms_bwd(x, S["g_mix"], dh1, dx1, name="rms1_bwd")
    return loss, grad_x, G


def _pad_cols(a, n):
    return jnp.pad(a, ((0, 0), (0, n - a.shape[1])))


def _col_sharded(landed):
    _, R, C = landed.shape
    return jnp.transpose(landed, (1, 0, 2)).reshape(R, N_DEV * C)


def _col_parts(full):
    R, C8 = full.shape
    return jnp.transpose(full.reshape(R, N_DEV, C8 // N_DEV), (1, 0, 2))


def kernel(x, p, norm_mix_g, w_in, conv_a_w, conv_qkv_w, a_log, dt_bias, dn_norm_g, w_out, norm_ffn_g, w_up, conv_ffn_w, w_down, norm_ple_g, w_ple_gate, w_ple_proj, final_norm_g, loss_target, m_norm_mix_g, m_w_in, m_conv_a_w, m_conv_qkv_w, m_a_log, m_dt_bias, m_dn_norm_g, m_w_out, m_norm_ffn_g, m_w_up, m_conv_ffn_w, m_w_down, m_norm_ple_g, m_w_ple_gate, m_w_ple_proj, m_final_norm_g, v_norm_mix_g, v_w_in, v_conv_a_w, v_conv_qkv_w, v_a_log, v_dt_bias, v_dn_norm_g, v_w_out, v_norm_ffn_g, v_w_up, v_conv_ffn_w, v_w_down, v_norm_ple_g, v_w_ple_gate, v_w_ple_proj, v_final_norm_g):
    T, D = x.shape[1], x.shape[2]
    xd, _, cd = _mesh_pos()
    me = 4 * xd + 2 * lax.axis_index("y") + cd

    conv_sh = [conv_a_w[0], conv_qkv_w[0], conv_ffn_w[0]]
    conv_n = [c.size for c in conv_sh]
    pack_rows = -(-sum(conv_n) // LANE)
    conv_pack = jnp.pad(jnp.concatenate([c.reshape(-1) for c in conv_sh]), (0, pack_rows * LANE - sum(conv_n))).reshape(pack_rows, LANE)
    names = ["w_in", "conv", "w_out", "w_up", "w_down", "w_pg", "w_pp"]
    shards = [w_in[0].astype(BF16), conv_pack, w_out[0].astype(BF16), w_up[0].astype(BF16), w_down[0].astype(BF16),
              w_ple_gate[0].astype(BF16), w_ple_proj[0].astype(BF16)]
    empty_slots = lambda blocks: [lax.empty((N_DEV,) + tuple(b.shape), b.dtype) for b in blocks]
    handles, tok0 = _split_start(shards, empty_slots(shards), False, name="gather_start")
    handle = dict(zip(names, handles))
    own = dict(zip(names, shards))
    in_cols = N_DEV * w_in.shape[2]
    in_p = (in_cols // LANE) * LANE + AB_PAD
    col_sharded = {"w_in", "w_up", "w_pp"}

    def gathered(name, after):
        landed = _split_wait(handle[name], after, False, name="gather_wait_" + name)
        return lax.dynamic_update_index_in_dim(landed, own[name], me, 0)

    def wt(name, after):
        landed = gathered(name, after)
        full = _col_sharded(landed) if name in col_sharded else landed.reshape(-1, D)
        return _pad_cols(full, in_p) if name == "w_in" else full

    def conv(after):
        flat = gathered("conv", after).reshape(N_DEV, pack_rows * LANE)
        out, o_ = {}, 0
        for nm, c, n_ in zip(("conv_a", "conv_qkv", "conv_ffn"), conv_sh, conv_n):
            out[nm] = _col_sharded(flat[:, o_:o_ + n_].reshape((N_DEV,) + c.shape))
            o_ += n_
        return out

    pending, mine = {}, {}

    def emit(grads):
        parts = [_col_parts(g[:, :in_cols] if nm == "w_in" else g) if nm in col_sharded else g.reshape(N_DEV, -1, D)
                 for nm, g in grads.items()]
        hs, tok = _split_start(parts, empty_slots([q[0] for q in parts]), True, name="scatter_start_" + "_".join(grads))
        pending.update(zip(grads, hs))
        mine.update({nm: lax.dynamic_index_in_dim(q, me, 0, keepdims=False) for nm, q in zip(grads, parts)})
        return tok

    S = {
        "g_mix": norm_mix_g + tok0[0, 0], "a_log": a_log, "dt_bias": dt_bias, "dn_g": dn_norm_g, "g_ffn": norm_ffn_g,
        "g_ple": norm_ple_g, "g_final": final_norm_g.reshape(1, D),
    }

    loss_v, grad_x, G = _local_step(x[0], p[0, 0], loss_target[0], S, wt, conv, emit)
    loss = lax.psum(loss_v[0, 0], ("x", "y", "c"))

    small_names = ["g_mix", "g_ffn", "g_ple", "g_final", "dn_g", "a_log", "dt_bias", "conv_a", "conv_qkv", "conv_ffn"]
    small_rows, pieces = [], []
    for nm in small_names:
        g_ = G[nm].reshape(-1)
        r_ = -(-g_.size // (8 * LANE)) * 8
        small_rows.append(r_)
        pieces.append(jnp.pad(g_, (0, r_ * LANE - g_.size)).reshape(r_, LANE))
    (small_l,) = _exchange([jnp.concatenate(pieces, axis=0)], False, name="gather_small_grads")
    landed = {nm: _split_wait(h_, grad_x, True, name="scatter_wait_" + nm) for nm, h_ in pending.items()}
    big_l = [landed[nm] for nm in ("w_in", "w_out", "w_up", "w_down", "w_pg", "w_pp")]

    def small_parts(nm):
        i = small_names.index(nm)
        r0 = sum(small_rows[:i])
        shp = G[nm].shape
        return small_l[:, r0:r0 + small_rows[i], :].reshape(N_DEV, -1)[:, :G[nm].size].reshape((N_DEV,) + shp)

    def conv_parts(nm, shard):
        full = small_parts(nm)
        C = shard.shape[-1]
        return lax.dynamic_slice_in_dim(full, me * C, C, axis=2)

    def adam(parts, w_, m_, v_, nm, own_=None):
        shp = w_.shape
        w2, m2, v2 = (t.reshape(parts.shape[1:]) for t in (w_, m_, v_))
        kw = {} if own_ is None else {"own": own_, "me": me.astype(jnp.int32).reshape(1)}
        return tuple(t.reshape(shp) for t in _adam(parts, w2, m2, v2, name="adam_" + nm, **kw))

    res = [
        adam(small_parts("g_mix"), norm_mix_g, m_norm_mix_g, v_norm_mix_g, "norm_mix_g"),
        adam(big_l[0], w_in, m_w_in, v_w_in, "w_in", mine["w_in"]),
        adam(conv_parts("conv_a", conv_a_w), conv_a_w, m_conv_a_w, v_conv_a_w, "conv_a_w"),
        adam(conv_parts("conv_qkv", conv_qkv_w), conv_qkv_w, m_conv_qkv_w, v_conv_qkv_w, "conv_qkv_w"),
        adam(small_parts("a_log"), a_log, m_a_log, v_a_log, "a_log"),
        adam(small_parts("dt_bias"), dt_bias, m_dt_bias, v_dt_bias, "dt_bias"),
        adam(small_parts("dn_g"), dn_norm_g, m_dn_norm_g, v_dn_norm_g, "dn_norm_g"),
        adam(big_l[1], w_out, m_w_out, v_w_out, "w_out", mine["w_out"]),
        adam(small_parts("g_ffn"), norm_ffn_g, m_norm_ffn_g, v_norm_ffn_g, "norm_ffn_g"),
        adam(big_l[2], w_up, m_w_up, v_w_up, "w_up", mine["w_up"]),
        adam(conv_parts("conv_ffn", conv_ffn_w), conv_ffn_w, m_conv_ffn_w, v_conv_ffn_w, "conv_ffn_w"),
        adam(big_l[3], w_down, m_w_down, v_w_down, "w_down", mine["w_down"]),
        adam(small_parts("g_ple"), norm_ple_g, m_norm_ple_g, v_norm_ple_g, "norm_ple_g"),
        adam(big_l[4], w_ple_gate, m_w_ple_gate, v_w_ple_gate, "w_ple_gate", mine["w_pg"]),
        adam(big_l[5], w_ple_proj, m_w_ple_proj, v_w_ple_proj, "w_ple_proj", mine["w_pp"]),
        adam(small_parts("g_final"), final_norm_g.reshape(1, D), m_final_norm_g.reshape(1, D),
             v_final_norm_g.reshape(1, D), "final_norm_g"),
    ]
    res[-1] = tuple(t.reshape(D) for t in res[-1])
    grads, deltas, new_m, new_v = zip(*res)
    return (loss, grad_x[None], *grads, *deltas, *new_m, *new_v)
```

```python
import functools

import jax
import jax.numpy as jnp
from jax import lax
from jax.experimental import pallas as pl
from jax.experimental.pallas import tpu as pltpu

F32 = jnp.float32
BF16 = jnp.bfloat16

EPS = 1e-6
CHUNK = 64
HEAD = 128
LANE = 128
N_DEV = 8
AB_PAD = 512

ADAM_LR = 0.001
ADAM_B1 = 0.9
ADAM_B2 = 0.999
ADAM_EPS = 1e-08
ADAM_WD = 0.01
ADAM_STEP = 10

MESH = pl.DeviceIdType.MESH


def _tile(dim, target, align=LANE):
    if dim <= target:
        return dim
    t = (target // align) * align
    while t > align and dim % t:
        t -= align
    assert dim % t == 0, (dim, target)
    return t


def _params(sem, vmem_mb=48):
    return pltpu.CompilerParams(dimension_semantics=sem, vmem_limit_bytes=vmem_mb << 20)


_DN = {"nn": (((1,), (0,)), ((), ())), "nt": (((1,), (1,)), ((), ())), "tn": (((0,), (0,)), ((), ()))}
SHARD_TILE = 1408


def _matmul(a, b, mode, *, name, out_dtypes=(F32,), epilogue=None, extras=(), after=None, b_shards=False,
            out_shards=False, tm=1024, tn=1024, tk=2048):
    shard_w = b.shape[2] if b_shards else None
    if b_shards:
        b_rows, b_cols = b.shape[1], N_DEV * shard_w
    else:
        b_rows, b_cols = b.shape
    if mode == "nn":
        (M, K), (K2, N) = a.shape, (b_rows, b_cols)
    elif mode == "nt":
        (M, K), (N, K2) = a.shape, (b_rows, b_cols)
    else:
        (K, M), (K2, N) = a.shape, (b_rows, b_cols)
    assert K == K2, (name, a.shape, b.shape)
    tm = _tile(M, tm)
    tn = _tile(shard_w if (b_shards and mode == "nn") else N // N_DEV if out_shards else N, tn)
    tk = _tile(shard_w if (b_shards and mode == "nt") else K, tk)
    nk = K // tk
    n_ex, n_out = len(extras), len(out_dtypes)
    dn = _DN[mode]

    n_tok = 0 if after is None else 1

    def body(a_ref, b_ref, *rest):
        rest = rest[n_tok:]
        ex_refs, out_refs = rest[:n_ex], rest[n_ex:n_ex + n_out]
        part = lax.dot_general(a_ref[...].astype(BF16), b_ref[...].astype(BF16), dn, preferred_element_type=F32)

        def finish(res):
            outs = (res,) if epilogue is None else epilogue(res, *[e[...] for e in ex_refs])
            for o_ref, val in zip(out_refs, outs):
                o_ref[...] = val.astype(o_ref.dtype)

        if nk == 1:
            finish(part)
            return
        acc, k = rest[-1], pl.program_id(2)

        @pl.when(k == 0)
        def _():
            acc[...] = part

        @pl.when(k > 0)
        def _():
            acc[...] += part

        @pl.when(k == nk - 1)
        def _():
            finish(acc[...])

    a_spec = pl.BlockSpec((tk, tm), lambda i, j, k: (k, i)) if mode == "tn" else pl.BlockSpec((tm, tk), lambda i, j, k: (i, k))
    if b_shards and mode == "nn":
        per = shard_w // tn
        b_spec = pl.BlockSpec((None, tk, tn), lambda i, j, k: (lax.div(j, per), k, lax.rem(j, per)))
    elif b_shards:
        per = shard_w // tk
        b_spec = pl.BlockSpec((None, tn, tk), lambda i, j, k: (lax.div(k, per), j, lax.rem(k, per)))
    else:
        b_spec = pl.BlockSpec((tn, tk), lambda i, j, k: (j, k)) if mode == "nt" else pl.BlockSpec((tk, tn), lambda i, j, k: (k, j))
    mn_spec = pl.BlockSpec((tm, tn), lambda i, j, k: (i, j))
    if out_shards:
        assert not extras
        per_o = (N // N_DEV) // tn
        out_spec = pl.BlockSpec((None, tm, tn), lambda i, j, k: (lax.div(j, per_o), i, lax.rem(j, per_o)))
        out_dims = (N_DEV, M, N // N_DEV)
    else:
        out_spec, out_dims = mn_spec, (M, N)
    outs = pl.pallas_call(
        body, name=name, grid=(M // tm, N // tn, nk),
        in_specs=[a_spec, b_spec] + [pl.BlockSpec((8, LANE), lambda i, j, k: (0, 0))] * n_tok + [mn_spec] * n_ex,
        out_specs=[out_spec] * n_out,
        out_shape=[jax.ShapeDtypeStruct(out_dims, dt) for dt in out_dtypes],
        scratch_shapes=[pltpu.VMEM((tm, tn), F32)] if nk > 1 else [],
        compiler_params=_params(("parallel", "parallel", "arbitrary"), 56),
    )(a, b, *([] if after is None else [after]), *extras)
    return outs[0] if n_out == 1 else outs


def _rms_fwd(x, g, *, name):
    T, D = x.shape
    tr = _tile(T, 256, 8)

    def body(x_ref, g_ref, h_ref):
        xv = x_ref[...]
        r = lax.rsqrt(jnp.mean(xv * xv, axis=-1, keepdims=True) + EPS)
        h_ref[...] = (xv * r * g_ref[...]).astype(h_ref.dtype)

    return pl.pallas_call(
        body, name=name, grid=(T // tr,),
        in_specs=[pl.BlockSpec((tr, D), lambda i: (i, 0)), pl.BlockSpec((1, D), lambda i: (0, 0))],
        out_specs=pl.BlockSpec((tr, D), lambda i: (i, 0)),
        out_shape=jax.ShapeDtypeStruct((T, D), BF16),
        compiler_params=_params(("parallel",)),
    )(x, g)


def _rms_bwd(x, g, dh, dres, *, name):
    T, D = x.shape
    tr = _tile(T, 256, 8)

    def body(x_ref, g_ref, dh_ref, dres_ref, dx_ref, dxb_ref, dg_ref):
        xv = x_ref[...]
        r = lax.rsqrt(jnp.mean(xv * xv, axis=-1, keepdims=True) + EPS)
        xh = xv * r
        dh = dh_ref[...]

        @pl.when(pl.program_id(0) == 0)
        def _():
            dg_ref[...] = jnp.zeros_like(dg_ref)

        dg_ref[...] += jnp.sum(dh * xh, axis=0, keepdims=True)
        dxh = dh * g_ref[...]
        dx = dres_ref[...] + r * (dxh - xh * jnp.mean(dxh * xh, axis=-1, keepdims=True))
        dx_ref[...] = dx
        dxb_ref[...] = dx.astype(dxb_ref.dtype)

    row = pl.BlockSpec((tr, D), lambda i: (i, 0))
    vec = pl.BlockSpec((1, D), lambda i: (0, 0))
    return pl.pallas_call(
        body, name=name, grid=(T // tr,),
        in_specs=[row, vec, row, row], out_specs=[row, row, vec],
        out_shape=[jax.ShapeDtypeStruct((T, D), F32), jax.ShapeDtypeStruct((T, D), BF16), jax.ShapeDtypeStruct((1, D), F32)],
        compiler_params=_params(("arbitrary",)),
    )(x, g, dh, dres)


def _final_loss(x, g, tgt, *, name):
    T, D = x.shape
    tr = _tile(T, 256, 8)

    def body(x_ref, g_ref, t_ref, dx_ref, dg_ref, loss_ref):
        xv = x_ref[...]
        r = lax.rsqrt(jnp.mean(xv * xv, axis=-1, keepdims=True) + EPS)
        xh = xv * r
        gv = g_ref[...]
        err = xh * gv - t_ref[...]

        @pl.when(pl.program_id(0) == 0)
        def _():
            dg_ref[...] = jnp.zeros_like(dg_ref)
            loss_ref[...] = jnp.zeros_like(loss_ref)

        part = 0.5 * jnp.sum(jnp.mean(err * err, axis=-1, keepdims=True), axis=0, keepdims=True)
        loss_ref[...] += jnp.broadcast_to(part, loss_ref.shape)
        dy = err * (1.0 / D)
        dg_ref[...] += jnp.sum(dy * xh, axis=0, keepdims=True)
        dxh = dy * gv
        dx_ref[...] = r * (dxh - xh * jnp.mean(dxh * xh, axis=-1, keepdims=True))

    row = pl.BlockSpec((tr, D), lambda i: (i, 0))
    vec = pl.BlockSpec((1, D), lambda i: (0, 0))
    return pl.pallas_call(
        body, name=name, grid=(T // tr,),
        in_specs=[row, vec, row], out_specs=[row, vec, pl.BlockSpec((1, LANE), lambda i: (0, 0))],
        out_shape=[jax.ShapeDtypeStruct((T, D), F32), jax.ShapeDtypeStruct((1, D), F32),
                   jax.ShapeDtypeStruct((1, LANE), F32)],
        compiler_params=_params(("arbitrary",)),
    )(x, g, tgt)


def _ple_bwd(dx3, pp, sg, *, name):
    T, D = dx3.shape
    tr = _tile(T, 256, 8)

    def body(dx_ref, pp_ref, sg_ref, dpg_ref, dpp_ref):
        dx, s = dx_ref[...], sg_ref[...]
        dpg_ref[...] = (dx * pp_ref[...] * s * (1.0 - s)).astype(dpg_ref.dtype)
        dpp_ref[...] = (dx * s).astype(dpp_ref.dtype)

    row = pl.BlockSpec((tr, D), lambda i: (i, 0))
    return pl.pallas_call(
        body, name=name, grid=(T // tr,), in_specs=[row, row, row], out_specs=[row, row],
        out_shape=[jax.ShapeDtypeStruct((T, D), BF16)] * 2, compiler_params=_params(("parallel",)),
    )(dx3, pp, sg)


ROWS_QKV_FWD, ROWS_QKV_BWD, ROWS_FFN_FWD, ROWS_FFN_BWD, ROWS_GROUP_A = 512, 256, 256, 128, 256


def _ext(ref, r0, T, before, after, RC):
    parts = []
    if before:
        p0 = pl.multiple_of(jnp.maximum(r0 - 8, 0), 8)
        parts.append(jnp.where(r0 > 0, ref[pl.ds(p0, 8), :], 0.0))
    parts.append(ref[pl.ds(r0, RC), :])
    if after:
        n0 = pl.multiple_of(jnp.minimum(r0 + RC, T - 8), 8)
        parts.append(jnp.where(r0 + RC < T, ref[pl.ds(n0, 8), :], 0.0))
    return parts[0] if len(parts) == 1 else jnp.concatenate(parts, axis=0)


def _down(xx, s):
    return (xx if s == 0 else pltpu.roll(xx, s, 0))[8:, :]


def _up(xx, s, rows):
    return (xx if s == 0 else pltpu.roll(xx, xx.shape[0] - s, 0))[:rows, :]


def _conv_down(xx, w_ref, K):
    y = None
    for j in range(K):
        t = _down(xx, K - 1 - j) * w_ref[j:j + 1, :]
        y = t if y is None else y + t
    return y


def _fold8(x):
    return jnp.sum(x.reshape(x.shape[0] // 8, 8, x.shape[1]), axis=0)


def _silu(x):
    return x * jax.nn.sigmoid(x)


def _dsilu(x):
    s = jax.nn.sigmoid(x)
    return s * (1.0 + x * (1.0 - s))


def _col_specs(T, offs):
    return [pl.BlockSpec((T, LANE), functools.partial(lambda o, j: (0, o + j), o)) for o in offs]


def _group_a_fwd(proj, conv_w, CW, *, name):
    T = proj.shape[0]
    RC = _tile(T, ROWS_GROUP_A, 8)
    nb = CW // LANE
    K = conv_w.shape[0]

    def body(ax_ref, ab_ref, ac_ref, w_ref, y_ref):
        def step(i, carry):
            r0 = pl.multiple_of(i * RC, RC)
            m = _ext(ac_ref, r0, T, True, False, RC) * _ext(ax_ref, r0, T, True, False, RC)
            y_ref[pl.ds(r0, RC), :] = (ab_ref[pl.ds(r0, RC), :] * _conv_down(m, w_ref, K)).astype(y_ref.dtype)
            return carry
        lax.fori_loop(0, T // RC, step, 0)

    return pl.pallas_call(
        body, name=name, grid=(nb,),
        in_specs=_col_specs(T, (0, nb, 2 * nb)) + [pl.BlockSpec((K, LANE), lambda j: (0, j))],
        out_specs=pl.BlockSpec((T, LANE), lambda j: (0, j)),
        out_shape=jax.ShapeDtypeStruct((T, CW), BF16), compiler_params=_params(("parallel",)),
    )(proj, proj, proj, conv_w)


def _group_a_bwd(proj, conv_w, dycat, CW, *, name):
    T = proj.shape[0]
    RC = _tile(T, ROWS_GROUP_A, 8)
    nb = CW // LANE
    K = conv_w.shape[0]

    def body(ax_ref, ab_ref, ac_ref, w_ref, dy_ref, dax_ref, dab_ref, dac_ref, dw_ref):
        def step(i, accs):
            r0 = pl.multiple_of(i * RC, RC)
            ax3 = _ext(ax_ref, r0, T, True, True, RC)
            ac3 = _ext(ac_ref, r0, T, True, True, RC)
            m3 = ax3 * ac3
            c = _conv_down(m3[:RC + 8], w_ref, K)
            dy = dy_ref[pl.ds(r0, RC), :]
            dab_ref[pl.ds(r0, RC), :] = (dy * c).astype(dab_ref.dtype)
            dc2 = _ext(dy_ref, r0, T, False, True, RC) * _ext(ab_ref, r0, T, False, True, RC)
            dm = None
            new = []
            for j in range(K):
                s = K - 1 - j
                t = _up(dc2, s, RC) * w_ref[j:j + 1, :]
                dm = t if dm is None else dm + t
                new.append(accs[j] + _fold8(dc2[:RC] * _down(m3[:RC + 8], s)))
            dax_ref[pl.ds(r0, RC), :] = (dm * ac3[8:RC + 8]).astype(dax_ref.dtype)
            dac_ref[pl.ds(r0, RC), :] = (dm * ax3[8:RC + 8]).astype(dac_ref.dtype)
            return tuple(new)

        accs = lax.fori_loop(0, T // RC, step, tuple(jnp.zeros((8, LANE), F32) for _ in range(K)))
        for j in range(K):
            dw_ref[j:j + 1, :] = jnp.sum(accs[j], axis=0, keepdims=True)

    col = pl.BlockSpec((T, LANE), lambda j: (0, j))
    wsp = pl.BlockSpec((K, LANE), lambda j: (0, j))
    return pl.pallas_call(
        body, name=name, grid=(nb,),
        in_specs=_col_specs(T, (0, nb, 2 * nb)) + [wsp, col],
        out_specs=[col, col, col, wsp],
        out_shape=[jax.ShapeDtypeStruct((T, CW), BF16)] * 3 + [jax.ShapeDtypeStruct((K, CW), F32)],
        compiler_params=_params(("parallel",)),
    )(proj, proj, proj, conv_w, dycat)


def _qkv_fwd(proj, conv_w, off, H, *, name):
    T = proj.shape[0]
    RC = _tile(T, ROWS_QKV_FWD, 8)
    nb = 3 * H
    K = conv_w.shape[0]

    def body(x_ref, w_ref, y_ref):
        j = pl.program_id(0)
        is_qk = j < 2 * H
        scale = jnp.where(j < H, HEAD ** -0.5, 1.0).astype(F32)

        def step(i, carry):
            r0 = pl.multiple_of(i * RC, RC)
            s = _silu(_conv_down(_ext(x_ref, r0, T, True, False, RC), w_ref, K))
            r = lax.rsqrt(jnp.sum(s * s, axis=-1, keepdims=True) + EPS) * scale
            y_ref[pl.ds(r0, RC), :] = s * jnp.where(is_qk, r, 1.0)
            return carry
        lax.fori_loop(0, T // RC, step, 0)

    return pl.pallas_call(
        body, name=name, grid=(nb,),
        in_specs=_col_specs(T, (off,)) + [pl.BlockSpec((K, LANE), lambda j: (0, j))],
        out_specs=pl.BlockSpec((T, LANE), lambda j: (0, j)),
        out_shape=jax.ShapeDtypeStruct((T, nb * LANE), F32), compiler_params=_params(("parallel",)),
    )(proj, conv_w)


def _qkv_bwd(proj, conv_w, dq, dk, dv, off, H, *, name):
    T = proj.shape[0]
    RC = _tile(T, ROWS_QKV_BWD, 8)
    nb = 3 * H
    K = conv_w.shape[0]

    def body(x_ref, w_ref, dq_ref, dk_ref, dv_ref, dx_ref, dw_ref):
        j = pl.program_id(0)
        is_qk = j < 2 * H
        scale = jnp.where(j < H, HEAD ** -0.5, 1.0).astype(F32)

        def step(i, accs):
            r0 = pl.multiple_of(i * RC, RC)
            x3 = _ext(x_ref, r0, T, True, True, RC)
            c2 = _conv_down(x3, w_ref, K)
            s2 = _silu(c2)
            dn2 = jnp.where(j < H, _ext(dq_ref, r0, T, False, True, RC),
                            jnp.where(is_qk, _ext(dk_ref, r0, T, False, True, RC), _ext(dv_ref, r0, T, False, True, RC)))
            r = lax.rsqrt(jnp.sum(s2 * s2, axis=-1, keepdims=True) + EPS)
            nh = s2 * r
            dnp = dn2 * scale
            ds_qk = r * (dnp - nh * jnp.sum(dnp * nh, axis=-1, keepdims=True))
            ds2 = jnp.where(is_qk, ds_qk, dn2)
            dc2 = ds2 * _dsilu(c2)
            dx = None
            new = []
            for jj in range(K):
                s = K - 1 - jj
                t = _up(dc2, s, RC) * w_ref[jj:jj + 1, :]
                dx = t if dx is None else dx + t
                new.append(accs[jj] + _fold8(dc2[:RC] * _down(x3[:RC + 8], s)))
            dx_ref[pl.ds(r0, RC), :] = dx.astype(dx_ref.dtype)
            return tuple(new)

        accs = lax.fori_loop(0, T // RC, step, tuple(jnp.zeros((8, LANE), F32) for _ in range(K)))
        for jj in range(K):
            dw_ref[jj:jj + 1, :] = jnp.sum(accs[jj], axis=0, keepdims=True)

    col = pl.BlockSpec((T, LANE), lambda j: (0, j))
    wsp = pl.BlockSpec((K, LANE), lambda j: (0, j))
    return pl.pallas_call(
        body, name=name, grid=(nb,),
        in_specs=_col_specs(T, (off,)) + [wsp] + [
            pl.BlockSpec((T, LANE), functools.partial(lambda o, j: (0, jnp.clip(j - o, 0, H - 1)), o)) for o in (0, H, 2 * H)],
        out_specs=[col, wsp],
        out_shape=[jax.ShapeDtypeStruct((T, nb * LANE), BF16), jax.ShapeDtypeStruct((K, nb * LANE), F32)],
        compiler_params=_params(("parallel",)),
    )(proj, conv_w, dq, dk, dv)


def _softplus(x):
    return jnp.maximum(x, 0.0) + jnp.log(1.0 + jnp.exp(-jnp.abs(x)))


def _gates_fwd(proj, alog, dtb, off, H, *, name):
    T = proj.shape[0]
    tr = _tile(T, 512, CHUNK)

    def body(ab_ref, al_ref, dt_ref, gb_ref, gam_ref):
        ab = ab_ref[...]
        lane = lax.broadcasted_iota(jnp.int32, ab.shape, 1)
        g = -jnp.exp(al_ref[...]) * _softplus(ab + dt_ref[...])
        gb = jnp.where(lane < H, g, jnp.where(lane < 2 * H, jax.nn.sigmoid(ab), 0.0))
        gb_ref[...] = gb
        tril = _tri().astype(F32)
        for c in range(tr // CHUNK):
            rows = slice(c * CHUNK, (c + 1) * CHUNK)
            gam_ref[rows, :] = _mm(tril, gb[rows, :], precision=lax.Precision.HIGHEST)

    vec = pl.BlockSpec((1, LANE), lambda i: (0, 0))
    row = pl.BlockSpec((tr, LANE), lambda i: (i, 0))
    return pl.pallas_call(
        body, name=name, grid=(T // tr,),
        in_specs=[pl.BlockSpec((tr, LANE), lambda i: (i, off)), vec, vec],
        out_specs=[row, row],
        out_shape=[jax.ShapeDtypeStruct((T, LANE), F32)] * 2, compiler_params=_params(("parallel",)),
    )(proj, alog, dtb)


def _gates_bwd(proj, alog, dtb, dgb, off, H, *, name):
    T = proj.shape[0]
    tr = _tile(T, 512, CHUNK)

    def body(ab_ref, al_ref, dt_ref, d_ref, dab_ref, dal_ref, ddt_ref):
        ab, d = ab_ref[...], d_ref[...]
        lane = lax.broadcasted_iota(jnp.int32, ab.shape, 1)
        is_g = lane < H
        triu = _tri(upper=True).astype(F32)
        dg = jnp.concatenate([_mm(triu, d[c * CHUNK:(c + 1) * CHUNK, :], precision=lax.Precision.HIGHEST)
                              for c in range(tr // CHUNK)], axis=0)
        z = ab + dt_ref[...]
        A = -jnp.exp(al_ref[...])
        da = dg * A * jax.nn.sigmoid(z)
        beta = jax.nn.sigmoid(ab)
        db = d * beta * (1.0 - beta)
        dab_ref[...] = jnp.where(is_g, da, jnp.where(lane < 2 * H, db, 0.0)).astype(dab_ref.dtype)

        @pl.when(pl.program_id(0) == 0)
        def _():
            dal_ref[...] = jnp.zeros_like(dal_ref)
            ddt_ref[...] = jnp.zeros_like(ddt_ref)

        dal_ref[...] += jnp.sum(jnp.where(is_g, dg * A * _softplus(z), 0.0), axis=0, keepdims=True)
        ddt_ref[...] += jnp.sum(jnp.where(is_g, da, 0.0), axis=0, keepdims=True)

    vec = pl.BlockSpec((1, LANE), lambda i: (0, 0))
    row = pl.BlockSpec((tr, LANE), lambda i: (i, 0))
    return pl.pallas_call(
        body, name=name, grid=(T // tr,),
        in_specs=[pl.BlockSpec((tr, LANE), lambda i: (i, off)), vec, vec, row],
        out_specs=[row, vec, vec],
        out_shape=[jax.ShapeDtypeStruct((T, LANE), BF16), jax.ShapeDtypeStruct((1, LANE), F32),
                   jax.ShapeDtypeStruct((1, LANE), F32)],
        compiler_params=_params(("arbitrary",)),
    )(proj, alog, dtb, dgb)


def _gated_norm_fwd(o, proj, gn, zoff, *, name):
    T, W = o.shape
    tr = _tile(T, 512, 8)

    def body(o_ref, z_ref, g_ref, y_ref):
        ov = o_ref[...]
        r = lax.rsqrt(jnp.mean(ov * ov, axis=-1, keepdims=True) + EPS)
        y_ref[...] = (ov * r * g_ref[...] * _silu(z_ref[...])).astype(y_ref.dtype)

    blk = pl.BlockSpec((tr, LANE), lambda i, j: (i, j))
    return pl.pallas_call(
        body, name=name, grid=(T // tr, W // LANE),
        in_specs=[blk, pl.BlockSpec((tr, LANE), lambda i, j: (i, zoff + j)), pl.BlockSpec((1, LANE), lambda i, j: (0, 0))],
        out_specs=blk, out_shape=jax.ShapeDtypeStruct((T, W), BF16), compiler_params=_params(("parallel", "parallel")),
    )(o, proj, gn)


def _gated_norm_bwd(o, proj, gn, dycat, zoff, yoff, *, name):
    T, W = o.shape
    tr = _tile(T, 512, 8)

    def body(o_ref, z_ref, g_ref, dy_ref, do_ref, dz_ref, dg_ref):
        ov, zv, gv, dy = o_ref[...], z_ref[...], g_ref[...], dy_ref[...]
        r = lax.rsqrt(jnp.mean(ov * ov, axis=-1, keepdims=True) + EPS)
        nh = ov * r
        s = _silu(zv)

        @pl.when((pl.program_id(0) == 0) & (pl.program_id(1) == 0))
        def _():
            dg_ref[...] = jnp.zeros_like(dg_ref)

        dg_ref[...] += jnp.sum(dy * nh * s, axis=0, keepdims=True)
        dz_ref[...] = (dy * nh * gv * _dsilu(zv)).astype(dz_ref.dtype)
        dn = dy * gv * s
        do_ref[...] = r * (dn - nh * jnp.mean(dn * nh, axis=-1, keepdims=True))

    blk = pl.BlockSpec((tr, LANE), lambda i, j: (i, j))
    vec = pl.BlockSpec((1, LANE), lambda i, j: (0, 0))
    return pl.pallas_call(
        body, name=name, grid=(T // tr, W // LANE),
        in_specs=[blk, pl.BlockSpec((tr, LANE), lambda i, j: (i, zoff + j)), vec,
                  pl.BlockSpec((tr, LANE), lambda i, j: (i, yoff + j))],
        out_specs=[blk, blk, vec],
        out_shape=[jax.ShapeDtypeStruct((T, W), F32), jax.ShapeDtypeStruct((T, W), BF16),
                   jax.ShapeDtypeStruct((1, LANE), F32)],
        compiler_params=_params(("arbitrary", "arbitrary")),
    )(o, proj, gn, dycat)


def _ffn_act_fwd(up_pre, conv_w, *, name):
    T, F2 = up_pre.shape
    RC = _tile(T, ROWS_FFN_FWD, 8)
    nb = F2 // 2 // LANE
    K = conv_w.shape[0]

    def body(g_ref, v_ref, wg_ref, wv_ref, y_ref):
        def step(i, carry):
            r0 = pl.multiple_of(i * RC, RC)
            gate = _conv_down(_ext(g_ref, r0, T, True, False, RC), wg_ref, K)
            val = _conv_down(_ext(v_ref, r0, T, True, False, RC), wv_ref, K)
            y_ref[pl.ds(r0, RC), :] = (_silu(gate) * val).astype(y_ref.dtype)
            return carry
        lax.fori_loop(0, T // RC, step, 0)

    return pl.pallas_call(
        body, name=name, grid=(nb,),
        in_specs=_col_specs(T, (0, nb)) + [pl.BlockSpec((K, LANE), lambda j: (0, j)),
                                           pl.BlockSpec((K, LANE), lambda j: (0, nb + j))],
        out_specs=pl.BlockSpec((T, LANE), lambda j: (0, j)),
        out_shape=jax.ShapeDtypeStruct((T, F2 // 2), BF16), compiler_params=_params(("parallel",)),
    )(up_pre, up_pre, conv_w, conv_w)


def _ffn_act_bwd(up_pre, conv_w, dact, *, name):
    T, F2 = up_pre.shape
    RC = _tile(T, ROWS_FFN_BWD, 8)
    nb = F2 // 2 // LANE
    K = conv_w.shape[0]

    def body(g_ref, v_ref, wg_ref, wv_ref, da_ref, dg_ref, dv_ref, dwg_ref, dwv_ref):
        def step(i, accs):
            r0 = pl.multiple_of(i * RC, RC)
            g3 = _ext(g_ref, r0, T, True, True, RC)
            v3 = _ext(v_ref, r0, T, True, True, RC)
            gate2 = _conv_down(g3, wg_ref, K)
            val2 = _conv_down(v3, wv_ref, K)
            da2 = _ext(da_ref, r0, T, False, True, RC)
            dgate2 = da2 * val2 * _dsilu(gate2)
            dval2 = da2 * _silu(gate2)
            dgp, dvp, new = None, None, []
            for j in range(K):
                s = K - 1 - j
                tg = _up(dgate2, s, RC) * wg_ref[j:j + 1, :]
                tv = _up(dval2, s, RC) * wv_ref[j:j + 1, :]
                dgp = tg if dgp is None else dgp + tg
                dvp = tv if dvp is None else dvp + tv
                new.append(accs[2 * j] + _fold8(dgate2[:RC] * _down(g3[:RC + 8], s)))
                new.append(accs[2 * j + 1] + _fold8(dval2[:RC] * _down(v3[:RC + 8], s)))
            dg_ref[pl.ds(r0, RC), :] = dgp.astype(dg_ref.dtype)
            dv_ref[pl.ds(r0, RC), :] = dvp.astype(dv_ref.dtype)
            return tuple(new)

        accs = lax.fori_loop(0, T // RC, step, tuple(jnp.zeros((8, LANE), F32) for _ in range(2 * K)))
        for j in range(K):
            dwg_ref[j:j + 1, :] = jnp.sum(accs[2 * j], axis=0, keepdims=True)
            dwv_ref[j:j + 1, :] = jnp.sum(accs[2 * j + 1], axis=0, keepdims=True)

    col = pl.BlockSpec((T, LANE), lambda j: (0, j))
    wsp = pl.BlockSpec((K, LANE), lambda j: (0, j))
    return pl.pallas_call(
        body, name=name, grid=(nb,),
        in_specs=_col_specs(T, (0, nb)) + [wsp, pl.BlockSpec((K, LANE), lambda j: (0, nb + j)), col],
        out_specs=[col, col, wsp, wsp],
        out_shape=[jax.ShapeDtypeStruct((T, F2 // 2), BF16)] * 2 + [jax.ShapeDtypeStruct((K, F2 // 2), F32)] * 2,
        compiler_params=_params(("parallel",)),
    )(up_pre, up_pre, conv_w, conv_w, dact)


CPB = 8
CPB_SCAN = 4
GRP = 8
HP = lax.Precision.HIGH


def _tri(strict=False, upper=False):
    r = lax.broadcasted_iota(jnp.int32, (CHUNK, CHUNK), 0)
    c = lax.broadcasted_iota(jnp.int32, (CHUNK, CHUNK), 1)
    if upper:
        return c >= r
    return (r > c) if strict else (r >= c)


def _mm(a, b, dn="nn", precision=None):
    precision = HP if precision is None else precision
    return lax.dot_general(a, b, _DN[dn], precision=precision, preferred_element_type=F32)


def _each(f, *cols):
    return [f(*xs) for xs in zip(*cols)]


def _decay(gam):
    return jnp.exp(jnp.where(_tri(), gam[:, :CHUNK] - gam.T[:CHUNK, :], -1e30))


def _delta_specs(T, H, cpb):
    rows = cpb * CHUNK
    col = lambda o: pl.BlockSpec((rows, LANE), functools.partial(lambda o, h, n: (n, o + h), o))
    bc = pl.BlockSpec((1, rows, LANE), lambda h, n: (h, n, 0))
    sq = pl.BlockSpec((1, cpb, CHUNK, CHUNK), lambda h, n: (h, n, 0, 0))
    vec = pl.BlockSpec((1, cpb, 1, LANE), lambda h, n: (h, n, 0, 0))
    return col, bc, sq, vec


def _delta_prep_fwd(qkv, gamB, bB, H, *, name):
    T = qkv.shape[0]
    N = T // CHUNK
    cpb = _tile(N, CPB, 8)
    grp = min(GRP, cpb)
    col, bc, sq, vec = _delta_specs(T, H, cpb)

    def body(q_ref, k_ref, v_ref, g_ref, b_ref, u_ref, w_ref, qd_ref, kd_ref, qk_ref, ti_ref, gl_ref):
        eye = (lax.broadcasted_iota(jnp.int32, (CHUNK, CHUNK), 0) == lax.broadcasted_iota(jnp.int32, (CHUNK, CHUNK), 1)).astype(F32)
        strict = _tri(strict=True)
        for c0 in range(0, cpb, grp):
            cs = list(range(c0, c0 + grp))
            rows = [slice(c * CHUNK, (c + 1) * CHUNK) for c in cs]
            q, k, v = ([r_[r, :] for r in rows] for r_ in (q_ref, k_ref, v_ref))
            bb = [b_ref[0, r, :] for r in rows]
            gam = [g_ref[0, r, :] for r in rows]
            D = _each(_decay, gam)
            e = _each(jnp.exp, gam)
            kk = _each(lambda k_: _mm(k_, k_, "nt"), k)
            X = _each(lambda kk_, D_, b_: -(jnp.where(strict, kk_ * D_, 0.0) * b_[:, :CHUNK]), kk, D, bb)
            R = _each(lambda x: eye + x, X)
            for _ in range(5):
                X = _each(lambda x: _mm(x, x), X)
                R = _each(lambda r, x: r + _mm(r, x), R, X)
            u = _each(lambda r, b_, v_: _mm(r, b_ * v_), R, bb, v)
            w = _each(lambda r, b_, e_, k_: _mm(r, b_ * e_ * k_), R, bb, e, k)
            qk = _each(lambda q_, k_, D_: _mm(q_, k_, "nt") * D_, q, k, D)
            for i, c in enumerate(cs):
                glast = gam[i][CHUNK - 1:CHUNK, :]
                u_ref[rows[i], :] = u[i]
                w_ref[rows[i], :] = w[i]
                qd_ref[rows[i], :] = e[i] * q[i]
                kd_ref[rows[i], :] = jnp.exp(glast - gam[i]) * k[i]
                qk_ref[0, c] = qk[i]
                ti_ref[0, c] = R[i]
                gl_ref[0, c] = jnp.exp(glast)

    full = jax.ShapeDtypeStruct((T, H * LANE), F32)
    sqs = jax.ShapeDtypeStruct((H, N, CHUNK, CHUNK), F32)
    return pl.pallas_call(
        body, name=name, grid=(H, N // cpb),
        in_specs=[col(0), col(H), col(2 * H), bc, bc],
        out_specs=[col(0)] * 4 + [sq, sq, vec],
        out_shape=[full] * 4 + [sqs, sqs, jax.ShapeDtypeStruct((H, N, 1, LANE), F32)],
        compiler_params=_params(("parallel", "parallel")),
    )(qkv, qkv, qkv, gamB, bB)


def _scan_specs(H, N, cpb, hb, rev):
    nbk = N // cpb
    blk = (lambda n: nbk - 1 - n) if rev else (lambda n: n)
    col = pl.BlockSpec((cpb * CHUNK, hb * LANE), lambda h, n: (blk(n), h))
    sq = pl.BlockSpec((hb, cpb, CHUNK, CHUNK), lambda h, n: (h, blk(n), 0, 0))
    vec = pl.BlockSpec((hb, cpb, 1, LANE), lambda h, n: (h, blk(n), 0, 0))
    st = pl.BlockSpec((hb, cpb, HEAD, HEAD), lambda h, n: (h, blk(n), 0, 0))
    return col, sq, vec, st


def _delta_scan_fwd(u, w, qd, kd, qk, gl, H, *, name):
    T = u.shape[0]
    N = T // CHUNK
    cpb = _tile(N, CPB_SCAN, 4)
    hb = min(GRP, H)
    col, sq, vec, st = _scan_specs(H, N, cpb, hb, False)
    lanes = [slice(j * LANE, (j + 1) * LANE) for j in range(hb)]
    heads = list(range(hb))

    def body(u_ref, w_ref, qd_ref, kd_ref, qk_ref, gl_ref, o_ref, vn_ref, ss_ref, s_scr):
        @pl.when(pl.program_id(1) == 0)
        def _():
            s_scr[...] = jnp.zeros_like(s_scr)

        def step(c, states):
            rows = pl.ds(pl.multiple_of(c * CHUNK, CHUNK), CHUNK)
            S = list(states)
            for j in heads:
                ss_ref[j, c] = S[j]
            wS = _each(lambda ln, s: _mm(w_ref[rows, ln], s), lanes, S)
            qS = _each(lambda ln, s: _mm(qd_ref[rows, ln], s), lanes, S)
            vn = _each(lambda ln, ws: u_ref[rows, ln] - ws, lanes, wS)
            o = _each(lambda j, qs, vn_: qs + _mm(qk_ref[j, c], vn_), heads, qS, vn)
            new = _each(lambda j, ln, s, vn_: s * gl_ref[j, c] + _mm(kd_ref[rows, ln], vn_, "tn"),
                        heads, lanes, S, vn)
            for j in heads:
                o_ref[rows, lanes[j]] = o[j]
                vn_ref[rows, lanes[j]] = vn[j]
            return tuple(new)
        out = lax.fori_loop(0, cpb, step, tuple(s_scr[j] for j in heads))
        for j in heads:
            s_scr[j] = out[j]

    full = jax.ShapeDtypeStruct((T, H * LANE), F32)
    return pl.pallas_call(
        body, name=name, grid=(H // hb, N // cpb),
        in_specs=[col] * 4 + [sq, vec],
        out_specs=[col, col, st],
        out_shape=[full, full, jax.ShapeDtypeStruct((H, N, HEAD, HEAD), F32)],
        scratch_shapes=[pltpu.VMEM((hb, HEAD, HEAD), F32)],
        compiler_params=_params(("parallel", "arbitrary")),
    )(u, w, qd, kd, qk, gl)


def _delta_scan_bwd(do, w, qd, kd, vn, qk, gl, ss, H, *, name):
    T = do.shape[0]
    N = T // CHUNK
    cpb = _tile(N, CPB_SCAN, 4)
    hb = min(GRP, H)
    col, sq, vec, st = _scan_specs(H, N, cpb, hb, True)
    lanes = [slice(j * LANE, (j + 1) * LANE) for j in range(hb)]
    heads = list(range(hb))

    def body(do_ref, w_ref, qd_ref, kd_ref, vn_ref, qk_ref, gl_ref, ss_ref,
             du_ref, dw_ref, dqd_ref, dkd_ref, dqk_ref, dgl_ref, ds_scr):
        @pl.when(pl.program_id(1) == 0)
        def _():
            ds_scr[...] = jnp.zeros_like(ds_scr)

        def step(i, dstates):
            c = cpb - 1 - i
            rows = pl.ds(pl.multiple_of(c * CHUNK, CHUNK), CHUNK)
            dS = list(dstates)
            S = [ss_ref[j, c] for j in heads]
            dov = [do_ref[rows, ln] for ln in lanes]
            vnv = [vn_ref[rows, ln] for ln in lanes]
            a1 = _each(lambda j, d_: _mm(qk_ref[j, c], d_, "tn"), heads, dov)
            a2 = _each(lambda ln, ds: _mm(kd_ref[rows, ln], ds), lanes, dS)
            dvn = _each(lambda x, y: x + y, a1, a2)
            dqd = _each(lambda d_, s: _mm(d_, s, "nt"), dov, S)
            dkd = _each(lambda v_, ds: _mm(v_, ds, "nt"), vnv, dS)
            dqk = _each(lambda d_, v_: _mm(d_, v_, "nt"), dov, vnv)
            dw = _each(lambda dv_, s: -_mm(dv_, s, "nt"), dvn, S)
            b1 = _each(lambda ln, d_: _mm(qd_ref[rows, ln], d_, "tn"), lanes, dov)
            b2 = _each(lambda ln, dv_: _mm(w_ref[rows, ln], dv_, "tn"), lanes, dvn)
            new = _each(lambda j, x, y, ds: x + ds * gl_ref[j, c] - y, heads, b1, b2, dS)
            for j in heads:
                du_ref[rows, lanes[j]] = dvn[j]
                dw_ref[rows, lanes[j]] = dw[j]
                dqd_ref[rows, lanes[j]] = dqd[j]
                dkd_ref[rows, lanes[j]] = dkd[j]
                dqk_ref[j, c] = dqk[j]
                dgl = jnp.sum(jnp.sum(dS[j] * S[j], axis=1, keepdims=True), axis=0, keepdims=True)
                dgl_ref[j, c] = jnp.broadcast_to(dgl, (1, LANE))
            return tuple(new)
        out = lax.fori_loop(0, cpb, step, tuple(ds_scr[j] for j in heads))
        for j in heads:
            ds_scr[j] = out[j]

    full = jax.ShapeDtypeStruct((T, H * LANE), F32)
    return pl.pallas_call(
        body, name=name, grid=(H // hb, N // cpb),
        in_specs=[col] * 5 + [sq, vec, st],
        out_specs=[col] * 4 + [sq, vec],
        out_shape=[full] * 4 + [jax.ShapeDtypeStruct((H, N, CHUNK, CHUNK), F32), jax.ShapeDtypeStruct((H, N, 1, LANE), F32)],
        scratch_shapes=[pltpu.VMEM((hb, HEAD, HEAD), F32)],
        compiler_params=_params(("parallel", "arbitrary")),
    )(do, w, qd, kd, vn, qk, gl, ss)


def _delta_prep_bwd(qkv, gamB, bB, ti, u, w, qk, du, dw, dqd, dkd, dqk, dgl, H, *, name):
    T = qkv.shape[0]
    N = T // CHUNK
    cpb = _tile(N, CPB, 8)
    grp = min(GRP, cpb)
    col, bc, sq, vec = _delta_specs(T, H, cpb)

    def body(q_ref, k_ref, v_ref, g_ref, b_ref, ti_ref, u_ref, w_ref, qk_ref,
             du_ref, dw_ref, dqd_ref, dkd_ref, dqk_ref, dgl_ref,
             dq_ref, dk_ref, dv_ref, dg_ref, db_ref):
        ones = jnp.ones((CHUNK, LANE), F32)
        strict = _tri(strict=True)
        last = lax.broadcasted_iota(jnp.int32, (CHUNK, LANE), 0) == CHUNK - 1
        lsum = lambda x: jnp.sum(x, axis=-1, keepdims=True)
        for c0 in range(0, cpb, grp):
            cs = list(range(c0, c0 + grp))
            rows = [slice(c * CHUNK, (c + 1) * CHUNK) for c in cs]
            ld = lambda r_: [r_[r, :] for r in rows]
            q, k, v, uv, wv, duv, dwv, dqd_v, dkd_v = (ld(r_) for r_ in (q_ref, k_ref, v_ref, u_ref, w_ref, du_ref, dw_ref, dqd_ref, dkd_ref))
            bb = [b_ref[0, r, :] for r in rows]
            gam = [g_ref[0, r, :] for r in rows]
            Ti = [ti_ref[0, c] for c in cs]
            QK = [qk_ref[0, c] for c in cs]
            dqk_v = [dqk_ref[0, c] for c in cs]
            D = _each(_decay, gam)
            e = _each(jnp.exp, gam)
            glast = [g_[CHUNK - 1:CHUNK, :] for g_ in gam]
            eL = _each(lambda gl_, g_: jnp.exp(gl_ - g_), glast, gam)
            kk = _each(lambda k_: _mm(k_, k_, "nt"), k)
            KKD = _each(lambda kk_, D_: jnp.where(strict, kk_ * D_, 0.0), kk, D)
            dru = _each(lambda t, d_: _mm(t, d_, "tn"), Ti, duv)
            drw = _each(lambda t, d_: _mm(t, d_, "tn"), Ti, dwv)
            l1 = _each(lambda a, b: _mm(a, b, "nt"), dru, uv)
            l2 = _each(lambda a, b: _mm(a, b, "nt"), drw, wv)
            dL = _each(lambda a, b: jnp.where(strict, -(a + b), 0.0), l1, l2)
            Mm = _each(lambda dl, b_: dl * b_[:, :CHUNK], dL, bb)
            dKK = _each(lambda m_, D_: m_ * D_, Mm, D)
            dQK = _each(lambda a, D_: a * D_, dqk_v, D)
            P = _each(lambda m_, kkd, a, qk_: m_ * kkd + a * qk_, Mm, KKD, dqk_v, QK)
            q1 = _each(lambda a, k_: _mm(a, k_), dQK, k)
            k1 = _each(lambda a, q_: _mm(a, q_, "tn"), dQK, q)
            k2 = _each(lambda a, k_: _mm(a, k_), dKK, k)
            k3 = _each(lambda a, k_: _mm(a, k_, "tn"), dKK, k)
            s1 = _each(lambda dl, kkd: _mm(dl * kkd, ones), dL, KKD)
            p1 = _each(lambda p_: _mm(p_, ones), P)
            p2 = _each(lambda p_: _mm(p_, ones, "tn"), P)
            for i, c in enumerate(cs):
                r = rows[i]
                bek = bb[i] * e[i]
                kdv = eL[i] * k[i]
                dq_ref[r, :] = q1[i] + e[i] * dqd_v[i]
                dk_ref[r, :] = k1[i] + k2[i] + k3[i] + bek * drw[i] + eL[i] * dkd_v[i]
                dv_ref[r, :] = bb[i] * dru[i]
                db_ref[0, r, :] = s1[i] + lsum(dru[i] * v[i]) + lsum(drw[i] * e[i] * k[i])
                dgam = (p1[i] - p2[i] + lsum(drw[i] * bek * k[i]) + lsum(dqd_v[i] * e[i] * q[i])
                        - lsum(dkd_v[i] * kdv))
                xlast = jnp.sum(lsum(dkd_v[i] * kdv), axis=0, keepdims=True) + jnp.exp(glast[i]) * dgl_ref[0, c]
                dg_ref[0, r, :] = dgam + jnp.where(last, xlast, 0.0)

    full = jax.ShapeDtypeStruct((T, H * LANE), F32)
    bcs = jax.ShapeDtypeStruct((H, T, LANE), F32)
    return pl.pallas_call(
        body, name=name, grid=(H, N // cpb),
        in_specs=[col(0), col(H), col(2 * H), bc, bc, sq, col(0), col(0), sq, col(0), col(0), col(0), col(0), sq, vec],
        out_specs=[col(0), col(0), col(0), bc, bc],
        out_shape=[full, full, full, bcs, bcs],
        compiler_params=_params(("parallel", "parallel")),
    )(qkv, qkv, qkv, gamB, bB, ti, u, w, qk, du, dw, dqd, dkd, dqk, dgl)


def _adam(parts, w, m, v, *, name, own=None, me=None):
    P, R, C = parts.shape
    tr = _tile(R, 256, 8)
    n_own = 0 if own is None else 2

    def body(*refs):
        p_ref, w_ref, m_ref, v_ref, g_ref, d_ref, nm_ref, nv_ref = refs[n_own:]
        g = None
        for i in range(P):
            t = p_ref[i].astype(F32)
            if n_own:
                t = jnp.where(refs[0][0] == i, refs[1][...].astype(F32), t)
            g = t if g is None else g + t
        mn = ADAM_B1 * m_ref[...] + (1.0 - ADAM_B1) * g
        vn = ADAM_B2 * v_ref[...] + (1.0 - ADAM_B2) * (g * g)
        m_hat = mn / (1.0 - ADAM_B1 ** ADAM_STEP)
        v_hat = vn / (1.0 - ADAM_B2 ** ADAM_STEP)
        g_ref[...] = g
        d_ref[...] = -ADAM_LR * (m_hat / (jnp.sqrt(v_hat) + ADAM_EPS) + ADAM_WD * w_ref[...])
        nm_ref[...] = mn
        nv_ref[...] = vn

    blk = pl.BlockSpec((tr, C), lambda i: (i, 0))
    return pl.pallas_call(
        body, name=name, grid=(R // tr,),
        in_specs=[pl.BlockSpec(memory_space=pltpu.SMEM), blk][:n_own] + [pl.BlockSpec((P, tr, C), lambda i: (0, i, 0)), blk, blk, blk],
        out_specs=[blk] * 4, out_shape=[jax.ShapeDtypeStruct((R, C), F32)] * 4,
        compiler_params=_params(("parallel",)),
    )(*([me, own] if n_own else []), parts, w, m, v)


def _mesh_pos():
    return lax.axis_index("x"), lax.axis_index("y"), lax.axis_index("c")


def _peer(k):
    x, y, c = _mesh_pos()
    px, py, pc = x ^ ((k >> 2) & 1), y ^ ((k >> 1) & 1), c ^ (k & 1)
    return (px, py, pc), 4 * px + 2 * py + pc


def _exchange(arrays, scatter, *, name):
    n = len(arrays)
    blocks = [a.shape[1:] if scatter else a.shape for a in arrays]

    def body(*refs):
        srcs, dsts = refs[:n], refs[n:2 * n]
        send_sems, recv_sems, local_sems = refs[2 * n:]
        x, y, c = _mesh_pos()
        me = 4 * x + 2 * y + c
        local, sends = [], []
        for a in range(n):
            cp = pltpu.make_async_copy(srcs[a].at[me] if scatter else srcs[a], dsts[a].at[me], local_sems.at[a])
            cp.start()
            local.append(cp)
            for k in range(1, N_DEV):
                dev, idx = _peer(k)
                cp = pltpu.make_async_remote_copy(
                    src_ref=srcs[a].at[idx] if scatter else srcs[a], dst_ref=dsts[a].at[me],
                    send_sem=send_sems.at[a * N_DEV + k], recv_sem=recv_sems.at[a * N_DEV + k],
                    device_id=dev, device_id_type=MESH)
                cp.start()
                sends.append(cp)
        for a in range(n):
            for k in range(1, N_DEV):
                dev, idx = _peer(k)
                pltpu.make_async_remote_copy(
                    src_ref=srcs[a].at[idx] if scatter else srcs[a], dst_ref=dsts[a].at[idx],
                    send_sem=send_sems.at[a * N_DEV + k], recv_sem=recv_sems.at[a * N_DEV + k],
                    device_id=dev, device_id_type=MESH).wait_recv()
        for cp in sends:
            cp.wait_send()
        for cp in local:
            cp.wait()

    anyspec = pl.BlockSpec(memory_space=pl.ANY)
    return pl.pallas_call(
        body, name=name, in_specs=[anyspec] * n, out_specs=[anyspec] * n,
        out_shape=[jax.ShapeDtypeStruct((N_DEV,) + tuple(b), a.dtype) for a, b in zip(arrays, blocks)],
        scratch_shapes=[pltpu.SemaphoreType.DMA((n * N_DEV,)), pltpu.SemaphoreType.DMA((n * N_DEV,)),
                        pltpu.SemaphoreType.DMA((n,))],
    )(*arrays)


_ANY = pl.BlockSpec(memory_space=pl.ANY)
_SEM = pl.BlockSpec(memory_space=pltpu.SEMAPHORE)
_EFFECT = pltpu.SideEffectType.DATAFLOW_SIDE_EFFECTING


def _in_hbm(a):
    return pltpu.with_memory_space_constraint(a, pltpu.HBM)


def _split_copy(src, land, send, recv, k, me, scatter, landed):
    dev, idx = _peer(k)
    return pltpu.make_async_remote_copy(
        src_ref=src.at[idx] if scatter else src, dst_ref=land.at[idx if landed else me],
        send_sem=send.at[k], recv_sem=recv.at[k], device_id=dev, device_id_type=MESH)


def _split_start(srcs, lands, scatter, *, name):
    n = len(srcs)

    def body(*refs):
        src, land, send, recv, token = refs[:n], refs[n:2 * n], refs[2 * n:3 * n], refs[3 * n:4 * n], refs[-1]
        x, y, c = _mesh_pos()
        me = 4 * x + 2 * y + c
        for a in range(n):
            for k in range(1, N_DEV):
                _split_copy(src[a], land[a], send[a], recv[a], k, me, scatter, False).start()
        token[...] = jnp.zeros_like(token)

    outs = pl.pallas_call(
        body, name=name,
        out_shape=[pltpu.SemaphoreType.DMA((N_DEV,))] * (2 * n) + [pltpu.HBM(t.shape, t.dtype) for t in list(srcs) + list(lands)]
        + [jax.ShapeDtypeStruct((8, LANE), F32)],
        in_specs=[_ANY] * (2 * n), out_specs=[_SEM] * (2 * n) + [_ANY] * (2 * n) + [pl.BlockSpec(memory_space=pltpu.VMEM)],
        input_output_aliases={i: 2 * n + i for i in range(2 * n)},
        compiler_params=pltpu.CompilerParams(has_side_effects=_EFFECT),
    )(*[_in_hbm(t) for t in list(srcs) + list(lands)])
    handles = [(outs[a], outs[n + a], outs[2 * n + a], outs[3 * n + a]) for a in range(n)]
    return handles, outs[-1]


def _split_wait(handle, after, scatter, *, name):
    send, recv, src_thru, land_thru = handle

    def body(src_ref, land_ref, send_ref, recv_ref, after_ref, src_out, land_out):
        x, y, c = _mesh_pos()
        me = 4 * x + 2 * y + c
        for k in range(1, N_DEV):
            cp = _split_copy(src_ref, land_ref, send_ref, recv_ref, k, me, scatter, True)
            cp.wait_send()
            cp.wait_recv()

    return pl.pallas_call(
        body, name=name,
        out_shape=(pltpu.HBM(src_thru.shape, src_thru.dtype), pltpu.HBM(land_thru.shape, land_thru.dtype)),
        in_specs=(_ANY, _ANY, _SEM, _SEM, _ANY), out_specs=(_ANY, _ANY), input_output_aliases={0: 0, 1: 1},
        compiler_params=pltpu.CompilerParams(has_side_effects=_EFFECT),
    )(src_thru, land_thru, send, recv, after)[1]


def _local_step(x, p, tgt, S, wt, conv, emit):
    T, D = x.shape
    CW = DNW = D // 2
    H = DNW // HEAD
    nA, nD = CW // LANE, DNW // LANE
    qkv_off, z_off, ab_off = 3 * nA, 3 * nA + 3 * nD, 3 * nA + 4 * nD
    alog = jnp.pad(S["a_log"], ((0, 0), (0, LANE - H)))
    dtb = jnp.pad(S["dt_bias"], ((0, 0), (0, LANE - H)))
    add = lambda acc, r: (acc + r,)

    h1 = _rms_fwd(x, S["g_mix"], name="rms1_fwd")
    w_in, cv = wt("w_in", h1), conv(h1)
    proj = _matmul(h1, w_in, "nn", name="mm_in")
    y_a = _group_a_fwd(proj, cv["conv_a"], CW, name="group_a_fwd")
    qkv = _qkv_fwd(proj, cv["conv_qkv"], qkv_off, H, name="qkv_fwd")
    gb, gamc = _gates_fwd(proj, alog, dtb, ab_off, H, name="gates_fwd")
    bcast = lambda cols: jnp.broadcast_to(cols.T[:, :, None], (H, T, LANE))
    gamB, bB = bcast(gamc[:, :H]), bcast(gb[:, H:2 * H])
    u, w, qd, kd, qk, ti, gl = _delta_prep_fwd(qkv, gamB, bB, H, name="delta_prep_fwd")
    o, vn, ss = _delta_scan_fwd(u, w, qd, kd, qk, gl, H, name="delta_scan_fwd")
    y_b = _gated_norm_fwd(o, proj, S["dn_g"], z_off, name="gated_norm_fwd")
    ycat = jnp.concatenate([y_a, y_b], axis=1)
    w_out = wt("w_out", ycat)
    x1 = _matmul(ycat, w_out, "nn", name="mm_out", epilogue=add, extras=(x,))
    h2 = _rms_fwd(x1, S["g_ffn"], name="rms2_fwd")
    w_up = wt("w_up", h2)
    up_pre = _matmul(h2, w_up, "nn", name="mm_up", b_shards=True, tn=SHARD_TILE)
    act = _ffn_act_fwd(up_pre, cv["conv_ffn"], name="ffn_act_fwd")
    w_down = wt("w_down", act)
    x2 = _matmul(act, w_down, "nn", name="mm_down", epilogue=add, extras=(x1,))
    h3 = _rms_fwd(x2, S["g_ple"], name="rms3_fwd")
    w_pp, w_pg = wt("w_pp", h3), wt("w_pg", h3)
    pp = _matmul(p, w_pp, "nn", name="mm_pp", b_shards=True)

    def ple_epi(acc, x2r, ppr):
        s = jax.nn.sigmoid(acc)
        return x2r + s * ppr, s

    x3, sg = _matmul(h3, w_pg, "nn", name="mm_pg", out_dtypes=(F32, F32), epilogue=ple_epi, extras=(x2, pp), tm=512)
    dx3, dg_final, loss = _final_loss(x3, S["g_final"], tgt, name="final_loss")

    G = {"g_final": dg_final}
    dpg, dpp = _ple_bwd(dx3, pp, sg, name="ple_bwd")
    tok = emit({"w_pp": _matmul(p, dpp, "tn", name="mm_dwpp", out_dtypes=(BF16,), out_shards=True),
                "w_pg": _matmul(h3, dpg, "tn", name="mm_dwpg", out_dtypes=(BF16,))})
    dh3 = _matmul(dpg, w_pg, "nt", name="mm_dh3", after=tok)
    dx2, dx2b, G["g_ple"] = _rms_bwd(x2, S["g_ple"], dh3, dx3, name="rms3_bwd")
    tok = emit({"w_down": _matmul(act, dx2b, "tn", name="mm_dwdown", out_dtypes=(BF16,))})
    dact = _matmul(dx2b, w_down, "nt", name="mm_dact", after=tok)
    dup_g, dup_v, dcf_g, dcf_v = _ffn_act_bwd(up_pre, cv["conv_ffn"], dact, name="ffn_act_bwd")
    G["conv_ffn"] = jnp.concatenate([dcf_g, dcf_v], axis=1)
    dup = jnp.concatenate([dup_g, dup_v], axis=1)
    tok = emit({"w_up": _matmul(h2, dup, "tn", name="mm_dwup", out_dtypes=(BF16,), out_shards=True, tn=SHARD_TILE)})
    dh2 = _matmul(dup, w_up, "nt", name="mm_dh2", after=tok, b_shards=True, tk=SHARD_TILE)
    dx1, dx1b, G["g_ffn"] = _rms_bwd(x1, S["g_ffn"], dh2, dx2, name="rms2_bwd")
    tok = emit({"w_out": _matmul(ycat, dx1b, "tn", name="mm_dwout", out_dtypes=(BF16,))})
    dycat = _matmul(dx1b, w_out, "nt", name="mm_dycat", after=tok)
    do, dz, G["dn_g"] = _gated_norm_bwd(o, proj, S["dn_g"], dycat, z_off, nA, name="gated_norm_bwd")
    du, dw, dqd, dkd, dqk, dgl = _delta_scan_bwd(do, w, qd, kd, vn, qk, gl, ss, H, name="delta_scan_bwd")
    dq, dk, dv, dgB, dbB = _delta_prep_bwd(qkv, gamB, bB, ti, u, w, qk, du, dw, dqd, dkd, dqk, dgl, H,
                                           name="delta_prep_bwd")
    dgb = jnp.pad(jnp.concatenate([dgB[:, :, 0].T, dbB[:, :, 0].T], axis=1), ((0, 0), (0, LANE - 2 * H)))
    dab, dal, ddt = _gates_bwd(proj, alog, dtb, dgb, ab_off, H, name="gates_bwd")
    G["a_log"], G["dt_bias"] = dal[:, :H], ddt[:, :H]
    dqkv, G["conv_qkv"] = _qkv_bwd(proj, cv["conv_qkv"], dq, dk, dv, qkv_off, H, name="qkv_bwd")
    dax, dab_, dac, G["conv_a"] = _group_a_bwd(proj, cv["conv_a"], dycat, CW, name="group_a_bwd")
    in_p = w_in.shape[1]
    dproj = jnp.concatenate([dax, dab_, dac, dqkv, dz, dab, jnp.zeros((T, in_p - (ab_off + 1) * LANE), BF16)], axis=1)
    tok = emit({"w_in": _matmul(h1, dproj, "tn", name="mm_dwin", out_dtypes=(BF16,))})
    dh1 = _matmul(dproj, w_in, "nt", name="mm_dh1", after=tok)
    grad_x, _, G["g_mix"] = _rms_bwd(x, S["g_mix"], dh1, dx1, name="rms1_bwd")
    return loss, grad_x, G


def _pad_cols(a, n):
    return jnp.pad(a, ((0, 0), (0, n - a.shape[1])))


def _col_sharded(landed):
    _, R, C = landed.shape
    return jnp.transpose(landed, (1, 0, 2)).reshape(R, N_DEV * C)


def _col_parts(full):
    R, C8 = full.shape
    return jnp.transpose(full.reshape(R, N_DEV, C8 // N_DEV), (1, 0, 2))


def kernel(x, p, norm_mix_g, w_in, conv_a_w, conv_qkv_w, a_log, dt_bias, dn_norm_g, w_out, norm_ffn_g, w_up, conv_ffn_w, w_down, norm_ple_g, w_ple_gate, w_ple_proj, final_norm_g, loss_target, m_norm_mix_g, m_w_in, m_conv_a_w, m_conv_qkv_w, m_a_log, m_dt_bias, m_dn_norm_g, m_w_out, m_norm_ffn_g, m_w_up, m_conv_ffn_w, m_w_down, m_norm_ple_g, m_w_ple_gate, m_w_ple_proj, m_final_norm_g, v_norm_mix_g, v_w_in, v_conv_a_w, v_conv_qkv_w, v_a_log, v_dt_bias, v_dn_norm_g, v_w_out, v_norm_ffn_g, v_w_up, v_conv_ffn_w, v_w_down, v_norm_ple_g, v_w_ple_gate, v_w_ple_proj, v_final_norm_g):
    T, D = x.shape[1], x.shape[2]
    xd, _, cd = _mesh_pos()
    me = 4 * xd + 2 * lax.axis_index("y") + cd

    conv_sh = [conv_a_w[0], conv_qkv_w[0], conv_ffn_w[0]]
    conv_n = [c.size for c in conv_sh]
    pack_rows = -(-sum(conv_n) // LANE)
    conv_pack = jnp.pad(jnp.concatenate([c.reshape(-1) for c in conv_sh]), (0, pack_rows * LANE - sum(conv_n))).reshape(pack_rows, LANE)
    names = ["w_in", "conv", "w_out", "w_up", "w_down", "w_pg", "w_pp"]
    shards = [w_in[0].astype(BF16), conv_pack, w_out[0].astype(BF16), w_up[0].astype(BF16), w_down[0].astype(BF16),
              w_ple_gate[0].astype(BF16), w_ple_proj[0].astype(BF16)]
    empty_slots = lambda blocks: [lax.empty((N_DEV,) + tuple(b.shape), b.dtype) for b in blocks]
    handles, tok0 = _split_start(shards, empty_slots(shards), False, name="gather_start")
    handle = dict(zip(names, handles))
    own = dict(zip(names, shards))
    in_cols = N_DEV * w_in.shape[2]
    in_p = (in_cols // LANE) * LANE + AB_PAD
    in_place = {"w_up", "w_pp"}

    def gathered(name, after):
        landed = _split_wait(handle[name], after, False, name="gather_wait_" + name)
        return lax.dynamic_update_index_in_dim(landed, own[name], me, 0)

    def wt(name, after):
        landed = gathered(name, after)
        if name in in_place:
            return landed
        return _pad_cols(_col_sharded(landed), in_p) if name == "w_in" else landed.reshape(-1, D)

    def conv(after):
        flat = gathered("conv", after).reshape(N_DEV, pack_rows * LANE)
        out, o_ = {}, 0
        for nm, c, n_ in zip(("conv_a", "conv_qkv", "conv_ffn"), conv_sh, conv_n):
            out[nm] = _col_sharded(flat[:, o_:o_ + n_].reshape((N_DEV,) + c.shape))
            o_ += n_
        return out

    pending, mine = {}, {}

    def emit(grads):
        parts = [g if nm in in_place else _col_parts(g[:, :in_cols]) if nm == "w_in" else g.reshape(N_DEV, -1, D)
                 for nm, g in grads.items()]
        hs, tok = _split_start(parts, empty_slots([q[0] for q in parts]), True, name="scatter_start_" + "_".join(grads))
        pending.update(zip(grads, hs))
        mine.update({nm: lax.dynamic_index_in_dim(q, me, 0, keepdims=False) for nm, q in zip(grads, parts)})
        return tok

    S = {
        "g_mix": norm_mix_g + tok0[0, 0], "a_log": a_log, "dt_bias": dt_bias, "dn_g": dn_norm_g, "g_ffn": norm_ffn_g,
        "g_ple": norm_ple_g, "g_final": final_norm_g.reshape(1, D),
    }

    loss_v, grad_x, G = _local_step(x[0], p[0, 0], loss_target[0], S, wt, conv, emit)
    loss = lax.psum(loss_v[0, 0], ("x", "y", "c"))

    small_names = ["g_mix", "g_ffn", "g_ple", "g_final", "dn_g", "a_log", "dt_bias", "conv_a", "conv_qkv", "conv_ffn"]
    small_rows, pieces = [], []
    for nm in small_names:
        g_ = G[nm].reshape(-1)
        r_ = -(-g_.size // (8 * LANE)) * 8
        small_rows.append(r_)
        pieces.append(jnp.pad(g_, (0, r_ * LANE - g_.size)).reshape(r_, LANE))
    (small_l,) = _exchange([jnp.concatenate(pieces, axis=0)], False, name="gather_small_grads")
    landed = {nm: _split_wait(h_, grad_x, True, name="scatter_wait_" + nm) for nm, h_ in pending.items()}
    big_l = [landed[nm] for nm in ("w_in", "w_out", "w_up", "w_down", "w_pg", "w_pp")]

    def small_parts(nm):
        i = small_names.index(nm)
        r0 = sum(small_rows[:i])
        shp = G[nm].shape
        return small_l[:, r0:r0 + small_rows[i], :].reshape(N_DEV, -1)[:, :G[nm].size].reshape((N_DEV,) + shp)

    def conv_parts(nm, shard):
        full = small_parts(nm)
        C = shard.shape[-1]
        return lax.dynamic_slice_in_dim(full, me * C, C, axis=2)

    def adam(parts, w_, m_, v_, nm, own_=None):
        shp = w_.shape
        w2, m2, v2 = (t.reshape(parts.shape[1:]) for t in (w_, m_, v_))
        kw = {} if own_ is None else {"own": own_, "me": me.astype(jnp.int32).reshape(1)}
        return tuple(t.reshape(shp) for t in _adam(parts, w2, m2, v2, name="adam_" + nm, **kw))

    res = [
        adam(small_parts("g_mix"), norm_mix_g, m_norm_mix_g, v_norm_mix_g, "norm_mix_g"),
        adam(big_l[0], w_in, m_w_in, v_w_in, "w_in", mine["w_in"]),
        adam(conv_parts("conv_a", conv_a_w), conv_a_w, m_conv_a_w, v_conv_a_w, "conv_a_w"),
        adam(conv_parts("conv_qkv", conv_qkv_w), conv_qkv_w, m_conv_qkv_w, v_conv_qkv_w, "conv_qkv_w"),
        adam(small_parts("a_log"), a_log, m_a_log, v_a_log, "a_log"),
        adam(small_parts("dt_bias"), dt_bias, m_dt_bias, v_dt_bias, "dt_bias"),
        adam(small_parts("dn_g"), dn_norm_g, m_dn_norm_g, v_dn_norm_g, "dn_norm_g"),
        adam(big_l[1], w_out, m_w_out, v_w_out, "w_out", mine["w_out"]),
        adam(small_parts("g_ffn"), norm_ffn_g, m_norm_ffn_g, v_norm_ffn_g, "norm_ffn_g"),
        adam(big_l[2], w_up, m_w_up, v_w_up, "w_up", mine["w_up"]),
        adam(conv_parts("conv_ffn", conv_ffn_w), conv_ffn_w, m_conv_ffn_w, v_conv_ffn_w, "conv_ffn_w"),
        adam(big_l[3], w_down, m_w_down, v_w_down, "w_down", mine["w_down"]),
        adam(small_parts("g_ple"), norm_ple_g, m_norm_ple_g, v_norm_ple_g, "norm_ple_g"),
        adam(big_l[4], w_ple_gate, m_w_ple_gate, v_w_ple_gate, "w_ple_gate", mine["w_pg"]),
        adam(big_l[5], w_ple_proj, m_w_ple_proj, v_w_ple_proj, "w_ple_proj", mine["w_pp"]),
        adam(small_parts("g_final"), final_norm_g.reshape(1, D), m_final_norm_g.reshape(1, D),
             v_final_norm_g.reshape(1, D), "final_norm_g"),
    ]
    res[-1] = tuple(t.reshape(D) for t in res[-1])
    grads, deltas, new_m, new_v = zip(*res)
    return (loss, grad_x[None], *grads, *deltas, *new_m, *new_v)
```

```python
import functools

import jax
import jax.numpy as jnp
from jax import lax
from jax.experimental import pallas as pl
from jax.experimental.pallas import tpu as pltpu

F32 = jnp.float32
BF16 = jnp.bfloat16

EPS = 1e-6
CHUNK = 64
HEAD = 128
LANE = 128
N_DEV = 8
AB_PAD = 512

ADAM_LR = 0.001
ADAM_B1 = 0.9
ADAM_B2 = 0.999
ADAM_EPS = 1e-08
ADAM_WD = 0.01
ADAM_STEP = 10

MESH = pl.DeviceIdType.MESH


def _tile(dim, target, align=LANE):
    if dim <= target:
        return dim
    t = (target // align) * align
    while t > align and dim % t:
        t -= align
    assert dim % t == 0, (dim, target)
    return t


def _params(sem, vmem_mb=48):
    return pltpu.CompilerParams(dimension_semantics=sem, vmem_limit_bytes=vmem_mb << 20)


_DN = {"nn": (((1,), (0,)), ((), ())), "nt": (((1,), (1,)), ((), ())), "tn": (((0,), (0,)), ((), ()))}
SHARD_TILE = 1408


def _matmul(a, b, mode, *, name, out_dtypes=(F32,), epilogue=None, extras=(), vec_extras=(), n_vec=0, after=None,
            b_shards=False, out_shards=False, tm=1024, tn=1024, tk=2048):
    shard_w = b.shape[2] if b_shards else None
    if b_shards:
        b_rows, b_cols = b.shape[1], N_DEV * shard_w
    else:
        b_rows, b_cols = b.shape
    if mode == "nn":
        (M, K), (K2, N) = a.shape, (b_rows, b_cols)
    elif mode == "nt":
        (M, K), (N, K2) = a.shape, (b_rows, b_cols)
    else:
        (K, M), (K2, N) = a.shape, (b_rows, b_cols)
    assert K == K2, (name, a.shape, b.shape)
    tm = _tile(M, tm)
    tn = _tile(shard_w if (b_shards and mode == "nn") else N // N_DEV if out_shards else N, tn)
    tk = _tile(shard_w if (b_shards and mode == "nt") else K, tk)
    nk = K // tk
    n_ex, n_out = len(extras) + len(vec_extras), len(out_dtypes)
    assert n_vec == 0 or tn == N, (name, tn, N)
    dn = _DN[mode]

    n_tok = 0 if after is None else 1

    def body(a_ref, b_ref, *rest):
        rest = rest[n_tok:]
        ex_refs, out_refs, vec_refs = rest[:n_ex], rest[n_ex:n_ex + n_out], rest[n_ex + n_out:n_ex + n_out + n_vec]
        part = lax.dot_general(a_ref[...].astype(BF16), b_ref[...].astype(BF16), dn, preferred_element_type=F32)
        first_rows = pl.program_id(0) == 0

        def finish(res):
            outs = (res,) if epilogue is None else epilogue(res, *[e[...] for e in ex_refs])
            for o_ref, val in zip(out_refs, outs[:n_out]):
                o_ref[...] = val.astype(o_ref.dtype)
            for v_ref, val in zip(vec_refs, outs[n_out:]):
                @pl.when(first_rows)
                def _(v_ref=v_ref, val=val):
                    v_ref[...] = val

                @pl.when(jnp.logical_not(first_rows))
                def _(v_ref=v_ref, val=val):
                    v_ref[...] += val

        if nk == 1:
            finish(part)
            return
        acc, k = rest[-1], pl.program_id(2)

        @pl.when(k == 0)
        def _():
            acc[...] = part

        @pl.when(k > 0)
        def _():
            acc[...] += part

        @pl.when(k == nk - 1)
        def _():
            finish(acc[...])

    a_spec = pl.BlockSpec((tk, tm), lambda i, j, k: (k, i)) if mode == "tn" else pl.BlockSpec((tm, tk), lambda i, j, k: (i, k))
    if b_shards and mode == "nn":
        per = shard_w // tn
        b_spec = pl.BlockSpec((None, tk, tn), lambda i, j, k: (lax.div(j, per), k, lax.rem(j, per)))
    elif b_shards:
        per = shard_w // tk
        b_spec = pl.BlockSpec((None, tn, tk), lambda i, j, k: (lax.div(k, per), j, lax.rem(k, per)))
    else:
        b_spec = pl.BlockSpec((tn, tk), lambda i, j, k: (j, k)) if mode == "nt" else pl.BlockSpec((tk, tn), lambda i, j, k: (k, j))
    mn_spec = pl.BlockSpec((tm, tn), lambda i, j, k: (i, j))
    vec_spec = pl.BlockSpec((1, tn), lambda i, j, k: (0, j))
    if out_shards:
        assert not extras
        per_o = (N // N_DEV) // tn
        out_spec = pl.BlockSpec((None, tm, tn), lambda i, j, k: (lax.div(j, per_o), i, lax.rem(j, per_o)))
        out_dims = (N_DEV, M, N // N_DEV)
    else:
        out_spec, out_dims = mn_spec, (M, N)
    outs = pl.pallas_call(
        body, name=name, grid=(M // tm, N // tn, nk),
        in_specs=[a_spec, b_spec] + [pl.BlockSpec((8, LANE), lambda i, j, k: (0, 0))] * n_tok
        + [mn_spec] * len(extras) + [vec_spec] * len(vec_extras),
        out_specs=[out_spec] * n_out + [vec_spec] * n_vec,
        out_shape=[jax.ShapeDtypeStruct(out_dims, dt) for dt in out_dtypes] + [jax.ShapeDtypeStruct((1, N), F32)] * n_vec,
        scratch_shapes=[pltpu.VMEM((tm, tn), F32)] if nk > 1 else [],
        compiler_params=_params(("arbitrary" if n_vec else "parallel", "parallel", "arbitrary"), 56),
    )(a, b, *([] if after is None else [after]), *extras, *vec_extras)
    return outs[0] if n_out + n_vec == 1 else outs


def _rms_fwd(x, g, *, name):
    T, D = x.shape
    tr = _tile(T, 256, 8)

    def body(x_ref, g_ref, h_ref):
        xv = x_ref[...]
        r = lax.rsqrt(jnp.mean(xv * xv, axis=-1, keepdims=True) + EPS)
        h_ref[...] = (xv * r * g_ref[...]).astype(h_ref.dtype)

    return pl.pallas_call(
        body, name=name, grid=(T // tr,),
        in_specs=[pl.BlockSpec((tr, D), lambda i: (i, 0)), pl.BlockSpec((1, D), lambda i: (0, 0))],
        out_specs=pl.BlockSpec((tr, D), lambda i: (i, 0)),
        out_shape=jax.ShapeDtypeStruct((T, D), BF16),
        compiler_params=_params(("parallel",)),
    )(x, g)


ROW_TILE = 256


def _epi_residual_rms(acc, res, g):
    xn = acc + res
    r = lax.rsqrt(jnp.mean(xn * xn, axis=-1, keepdims=True) + EPS)
    return xn, xn * r * g


def _epi_rms_bwd(n_copies):
    def epi(dh, x, dres, g):
        r = lax.rsqrt(jnp.mean(x * x, axis=-1, keepdims=True) + EPS)
        xh = x * r
        dxh = dh * g
        dx = dres + r * (dxh - xh * jnp.mean(dxh * xh, axis=-1, keepdims=True))
        return (dx,) * n_copies + (jnp.sum(dh * xh, axis=0, keepdims=True),)
    return epi


def _final_loss(x, g, tgt, *, name):
    T, D = x.shape
    tr = _tile(T, 256, 8)

    def body(x_ref, g_ref, t_ref, dx_ref, dg_ref, loss_ref):
        xv = x_ref[...]
        r = lax.rsqrt(jnp.mean(xv * xv, axis=-1, keepdims=True) + EPS)
        xh = xv * r
        gv = g_ref[...]
        err = xh * gv - t_ref[...]

        @pl.when(pl.program_id(0) == 0)
        def _():
            dg_ref[...] = jnp.zeros_like(dg_ref)
            loss_ref[...] = jnp.zeros_like(loss_ref)

        part = 0.5 * jnp.sum(jnp.mean(err * err, axis=-1, keepdims=True), axis=0, keepdims=True)
        loss_ref[...] += jnp.broadcast_to(part, loss_ref.shape)
        dy = err * (1.0 / D)
        dg_ref[...] += jnp.sum(dy * xh, axis=0, keepdims=True)
        dxh = dy * gv
        dx_ref[...] = r * (dxh - xh * jnp.mean(dxh * xh, axis=-1, keepdims=True))

    row = pl.BlockSpec((tr, D), lambda i: (i, 0))
    vec = pl.BlockSpec((1, D), lambda i: (0, 0))
    return pl.pallas_call(
        body, name=name, grid=(T // tr,),
        in_specs=[row, vec, row], out_specs=[row, vec, pl.BlockSpec((1, LANE), lambda i: (0, 0))],
        out_shape=[jax.ShapeDtypeStruct((T, D), F32), jax.ShapeDtypeStruct((1, D), F32),
                   jax.ShapeDtypeStruct((1, LANE), F32)],
        compiler_params=_params(("arbitrary",)),
    )(x, g, tgt)


def _ple_bwd(dx3, pp, sg, *, name):
    T, D = dx3.shape
    tr = _tile(T, 256, 8)

    def body(dx_ref, pp_ref, sg_ref, dpg_ref, dpp_ref):
        dx, s = dx_ref[...], sg_ref[...]
        dpg_ref[...] = (dx * pp_ref[...] * s * (1.0 - s)).astype(dpg_ref.dtype)
        dpp_ref[...] = (dx * s).astype(dpp_ref.dtype)

    row = pl.BlockSpec((tr, D), lambda i: (i, 0))
    return pl.pallas_call(
        body, name=name, grid=(T // tr,), in_specs=[row, row, row], out_specs=[row, row],
        out_shape=[jax.ShapeDtypeStruct((T, D), BF16)] * 2, compiler_params=_params(("parallel",)),
    )(dx3, pp, sg)


ROWS_QKV_FWD, ROWS_QKV_BWD, ROWS_FFN_FWD, ROWS_FFN_BWD, ROWS_GROUP_A = 512, 256, 256, 128, 256


def _ext(ref, r0, T, before, after, RC):
    parts = []
    if before:
        p0 = pl.multiple_of(jnp.maximum(r0 - 8, 0), 8)
        parts.append(jnp.where(r0 > 0, ref[pl.ds(p0, 8), :], 0.0))
    parts.append(ref[pl.ds(r0, RC), :])
    if after:
        n0 = pl.multiple_of(jnp.minimum(r0 + RC, T - 8), 8)
        parts.append(jnp.where(r0 + RC < T, ref[pl.ds(n0, 8), :], 0.0))
    return parts[0] if len(parts) == 1 else jnp.concatenate(parts, axis=0)


def _down(xx, s):
    return (xx if s == 0 else pltpu.roll(xx, s, 0))[8:, :]


def _up(xx, s, rows):
    return (xx if s == 0 else pltpu.roll(xx, xx.shape[0] - s, 0))[:rows, :]


def _conv_down(xx, w_ref, K):
    y = None
    for j in range(K):
        t = _down(xx, K - 1 - j) * w_ref[j:j + 1, :]
        y = t if y is None else y + t
    return y


def _fold8(x):
    return jnp.sum(x.reshape(x.shape[0] // 8, 8, x.shape[1]), axis=0)


def _silu(x):
    return x * jax.nn.sigmoid(x)


def _dsilu(x):
    s = jax.nn.sigmoid(x)
    return s * (1.0 + x * (1.0 - s))


def _col_specs(T, offs):
    return [pl.BlockSpec((T, LANE), functools.partial(lambda o, j: (0, o + j), o)) for o in offs]


def _group_a_fwd(proj, conv_w, CW, *, name):
    T = proj.shape[0]
    RC = _tile(T, ROWS_GROUP_A, 8)
    nb = CW // LANE
    K = conv_w.shape[0]

    def body(ax_ref, ab_ref, ac_ref, w_ref, y_ref):
        def step(i, carry):
            r0 = pl.multiple_of(i * RC, RC)
            m = _ext(ac_ref, r0, T, True, False, RC) * _ext(ax_ref, r0, T, True, False, RC)
            y_ref[pl.ds(r0, RC), :] = (ab_ref[pl.ds(r0, RC), :] * _conv_down(m, w_ref, K)).astype(y_ref.dtype)
            return carry
        lax.fori_loop(0, T // RC, step, 0)

    return pl.pallas_call(
        body, name=name, grid=(nb,),
        in_specs=_col_specs(T, (0, nb, 2 * nb)) + [pl.BlockSpec((K, LANE), lambda j: (0, j))],
        out_specs=pl.BlockSpec((T, LANE), lambda j: (0, j)),
        out_shape=jax.ShapeDtypeStruct((T, CW), BF16), compiler_params=_params(("parallel",)),
    )(proj, proj, proj, conv_w)


def _group_a_bwd(proj, conv_w, dycat, CW, *, name):
    T = proj.shape[0]
    RC = _tile(T, ROWS_GROUP_A, 8)
    nb = CW // LANE
    K = conv_w.shape[0]

    def body(ax_ref, ab_ref, ac_ref, w_ref, dy_ref, dax_ref, dab_ref, dac_ref, dw_ref):
        def step(i, accs):
            r0 = pl.multiple_of(i * RC, RC)
            ax3 = _ext(ax_ref, r0, T, True, True, RC)
            ac3 = _ext(ac_ref, r0, T, True, True, RC)
            m3 = ax3 * ac3
            c = _conv_down(m3[:RC + 8], w_ref, K)
            dy = dy_ref[pl.ds(r0, RC), :]
            dab_ref[pl.ds(r0, RC), :] = (dy * c).astype(dab_ref.dtype)
            dc2 = _ext(dy_ref, r0, T, False, True, RC) * _ext(ab_ref, r0, T, False, True, RC)
            dm = None
            new = []
            for j in range(K):
                s = K - 1 - j
                t = _up(dc2, s, RC) * w_ref[j:j + 1, :]
                dm = t if dm is None else dm + t
                new.append(accs[j] + _fold8(dc2[:RC] * _down(m3[:RC + 8], s)))
            dax_ref[pl.ds(r0, RC), :] = (dm * ac3[8:RC + 8]).astype(dax_ref.dtype)
            dac_ref[pl.ds(r0, RC), :] = (dm * ax3[8:RC + 8]).astype(dac_ref.dtype)
            return tuple(new)

        accs = lax.fori_loop(0, T // RC, step, tuple(jnp.zeros((8, LANE), F32) for _ in range(K)))
        for j in range(K):
            dw_ref[j:j + 1, :] = jnp.sum(accs[j], axis=0, keepdims=True)

    col = pl.BlockSpec((T, LANE), lambda j: (0, j))
    wsp = pl.BlockSpec((K, LANE), lambda j: (0, j))
    return pl.pallas_call(
        body, name=name, grid=(nb,),
        in_specs=_col_specs(T, (0, nb, 2 * nb)) + [wsp, col],
        out_specs=[col, col, col, wsp],
        out_shape=[jax.ShapeDtypeStruct((T, CW), BF16)] * 3 + [jax.ShapeDtypeStruct((K, CW), F32)],
        compiler_params=_params(("parallel",)),
    )(proj, proj, proj, conv_w, dycat)


def _qkv_fwd(proj, conv_w, off, H, *, name):
    T = proj.shape[0]
    RC = _tile(T, ROWS_QKV_FWD, 8)
    nb = 3 * H
    K = conv_w.shape[0]

    def body(x_ref, w_ref, y_ref):
        j = pl.program_id(0)
        is_qk = j < 2 * H
        scale = jnp.where(j < H, HEAD ** -0.5, 1.0).astype(F32)

        def step(i, carry):
            r0 = pl.multiple_of(i * RC, RC)
            s = _silu(_conv_down(_ext(x_ref, r0, T, True, False, RC), w_ref, K))
            r = lax.rsqrt(jnp.sum(s * s, axis=-1, keepdims=True) + EPS) * scale
            y_ref[pl.ds(r0, RC), :] = s * jnp.where(is_qk, r, 1.0)
            return carry
        lax.fori_loop(0, T // RC, step, 0)

    return pl.pallas_call(
        body, name=name, grid=(nb,),
        in_specs=_col_specs(T, (off,)) + [pl.BlockSpec((K, LANE), lambda j: (0, j))],
        out_specs=pl.BlockSpec((T, LANE), lambda j: (0, j)),
        out_shape=jax.ShapeDtypeStruct((T, nb * LANE), F32), compiler_params=_params(("parallel",)),
    )(proj, conv_w)


def _qkv_bwd(proj, conv_w, dq, dk, dv, off, H, *, name):
    T = proj.shape[0]
    RC = _tile(T, ROWS_QKV_BWD, 8)
    nb = 3 * H
    K = conv_w.shape[0]

    def body(x_ref, w_ref, dq_ref, dk_ref, dv_ref, dx_ref, dw_ref):
        j = pl.program_id(0)
        is_qk = j < 2 * H
        scale = jnp.where(j < H, HEAD ** -0.5, 1.0).astype(F32)

        def step(i, accs):
            r0 = pl.multiple_of(i * RC, RC)
            x3 = _ext(x_ref, r0, T, True, True, RC)
            c2 = _conv_down(x3, w_ref, K)
            s2 = _silu(c2)
            dn2 = jnp.where(j < H, _ext(dq_ref, r0, T, False, True, RC),
                            jnp.where(is_qk, _ext(dk_ref, r0, T, False, True, RC), _ext(dv_ref, r0, T, False, True, RC)))
            r = lax.rsqrt(jnp.sum(s2 * s2, axis=-1, keepdims=True) + EPS)
            nh = s2 * r
            dnp = dn2 * scale
            ds_qk = r * (dnp - nh * jnp.sum(dnp * nh, axis=-1, keepdims=True))
            ds2 = jnp.where(is_qk, ds_qk, dn2)
            dc2 = ds2 * _dsilu(c2)
            dx = None
            new = []
            for jj in range(K):
                s = K - 1 - jj
                t = _up(dc2, s, RC) * w_ref[jj:jj + 1, :]
                dx = t if dx is None else dx + t
                new.append(accs[jj] + _fold8(dc2[:RC] * _down(x3[:RC + 8], s)))
            dx_ref[pl.ds(r0, RC), :] = dx.astype(dx_ref.dtype)
            return tuple(new)

        accs = lax.fori_loop(0, T // RC, step, tuple(jnp.zeros((8, LANE), F32) for _ in range(K)))
        for jj in range(K):
            dw_ref[jj:jj + 1, :] = jnp.sum(accs[jj], axis=0, keepdims=True)

    col = pl.BlockSpec((T, LANE), lambda j: (0, j))
    wsp = pl.BlockSpec((K, LANE), lambda j: (0, j))
    return pl.pallas_call(
        body, name=name, grid=(nb,),
        in_specs=_col_specs(T, (off,)) + [wsp] + [
            pl.BlockSpec((T, LANE), functools.partial(lambda o, j: (0, jnp.clip(j - o, 0, H - 1)), o)) for o in (0, H, 2 * H)],
        out_specs=[col, wsp],
        out_shape=[jax.ShapeDtypeStruct((T, nb * LANE), BF16), jax.ShapeDtypeStruct((K, nb * LANE), F32)],
        compiler_params=_params(("parallel",)),
    )(proj, conv_w, dq, dk, dv)


def _softplus(x):
    return jnp.maximum(x, 0.0) + jnp.log(1.0 + jnp.exp(-jnp.abs(x)))


def _gates_fwd(proj, alog, dtb, off, H, *, name):
    T = proj.shape[0]
    tr = _tile(T, 512, CHUNK)

    def body(ab_ref, al_ref, dt_ref, gb_ref, gam_ref):
        ab = ab_ref[...]
        lane = lax.broadcasted_iota(jnp.int32, ab.shape, 1)
        g = -jnp.exp(al_ref[...]) * _softplus(ab + dt_ref[...])
        gb = jnp.where(lane < H, g, jnp.where(lane < 2 * H, jax.nn.sigmoid(ab), 0.0))
        gb_ref[...] = gb
        tril = _tri().astype(F32)
        for c in range(tr // CHUNK):
            rows = slice(c * CHUNK, (c + 1) * CHUNK)
            gam_ref[rows, :] = _mm(tril, gb[rows, :], precision=lax.Precision.HIGHEST)

    vec = pl.BlockSpec((1, LANE), lambda i: (0, 0))
    row = pl.BlockSpec((tr, LANE), lambda i: (i, 0))
    return pl.pallas_call(
        body, name=name, grid=(T // tr,),
        in_specs=[pl.BlockSpec((tr, LANE), lambda i: (i, off)), vec, vec],
        out_specs=[row, row],
        out_shape=[jax.ShapeDtypeStruct((T, LANE), F32)] * 2, compiler_params=_params(("parallel",)),
    )(proj, alog, dtb)


def _gates_bwd(proj, alog, dtb, dgb, off, H, *, name):
    T = proj.shape[0]
    tr = _tile(T, 512, CHUNK)

    def body(ab_ref, al_ref, dt_ref, d_ref, dab_ref, dal_ref, ddt_ref):
        ab, d = ab_ref[...], d_ref[...]
        lane = lax.broadcasted_iota(jnp.int32, ab.shape, 1)
        is_g = lane < H
        triu = _tri(upper=True).astype(F32)
        dg = jnp.concatenate([_mm(triu, d[c * CHUNK:(c + 1) * CHUNK, :], precision=lax.Precision.HIGHEST)
                              for c in range(tr // CHUNK)], axis=0)
        z = ab + dt_ref[...]
        A = -jnp.exp(al_ref[...])
        da = dg * A * jax.nn.sigmoid(z)
        beta = jax.nn.sigmoid(ab)
        db = d * beta * (1.0 - beta)
        dab_ref[...] = jnp.where(is_g, da, jnp.where(lane < 2 * H, db, 0.0)).astype(dab_ref.dtype)

        @pl.when(pl.program_id(0) == 0)
        def _():
            dal_ref[...] = jnp.zeros_like(dal_ref)
            ddt_ref[...] = jnp.zeros_like(ddt_ref)

        dal_ref[...] += jnp.sum(jnp.where(is_g, dg * A * _softplus(z), 0.0), axis=0, keepdims=True)
        ddt_ref[...] += jnp.sum(jnp.where(is_g, da, 0.0), axis=0, keepdims=True)

    vec = pl.BlockSpec((1, LANE), lambda i: (0, 0))
    row = pl.BlockSpec((tr, LANE), lambda i: (i, 0))
    return pl.pallas_call(
        body, name=name, grid=(T // tr,),
        in_specs=[pl.BlockSpec((tr, LANE), lambda i: (i, off)), vec, vec, row],
        out_specs=[row, vec, vec],
        out_shape=[jax.ShapeDtypeStruct((T, LANE), BF16), jax.ShapeDtypeStruct((1, LANE), F32),
                   jax.ShapeDtypeStruct((1, LANE), F32)],
        compiler_params=_params(("arbitrary",)),
    )(proj, alog, dtb, dgb)


def _gated_norm_fwd(o, proj, gn, zoff, *, name):
    T, W = o.shape
    tr = _tile(T, 512, 8)

    def body(o_ref, z_ref, g_ref, y_ref):
        ov = o_ref[...]
        r = lax.rsqrt(jnp.mean(ov * ov, axis=-1, keepdims=True) + EPS)
        y_ref[...] = (ov * r * g_ref[...] * _silu(z_ref[...])).astype(y_ref.dtype)

    blk = pl.BlockSpec((tr, LANE), lambda i, j: (i, j))
    return pl.pallas_call(
        body, name=name, grid=(T // tr, W // LANE),
        in_specs=[blk, pl.BlockSpec((tr, LANE), lambda i, j: (i, zoff + j)), pl.BlockSpec((1, LANE), lambda i, j: (0, 0))],
        out_specs=blk, out_shape=jax.ShapeDtypeStruct((T, W), BF16), compiler_params=_params(("parallel", "parallel")),
    )(o, proj, gn)


def _gated_norm_bwd(o, proj, gn, dycat, zoff, yoff, *, name):
    T, W = o.shape
    tr = _tile(T, 512, 8)

    def body(o_ref, z_ref, g_ref, dy_ref, do_ref, dz_ref, dg_ref):
        ov, zv, gv, dy = o_ref[...], z_ref[...], g_ref[...], dy_ref[...]
        r = lax.rsqrt(jnp.mean(ov * ov, axis=-1, keepdims=True) + EPS)
        nh = ov * r
        s = _silu(zv)

        @pl.when((pl.program_id(0) == 0) & (pl.program_id(1) == 0))
        def _():
            dg_ref[...] = jnp.zeros_like(dg_ref)

        dg_ref[...] += jnp.sum(dy * nh * s, axis=0, keepdims=True)
        dz_ref[...] = (dy * nh * gv * _dsilu(zv)).astype(dz_ref.dtype)
        dn = dy * gv * s
        do_ref[...] = r * (dn - nh * jnp.mean(dn * nh, axis=-1, keepdims=True))

    blk = pl.BlockSpec((tr, LANE), lambda i, j: (i, j))
    vec = pl.BlockSpec((1, LANE), lambda i, j: (0, 0))
    return pl.pallas_call(
        body, name=name, grid=(T // tr, W // LANE),
        in_specs=[blk, pl.BlockSpec((tr, LANE), lambda i, j: (i, zoff + j)), vec,
                  pl.BlockSpec((tr, LANE), lambda i, j: (i, yoff + j))],
        out_specs=[blk, blk, vec],
        out_shape=[jax.ShapeDtypeStruct((T, W), F32), jax.ShapeDtypeStruct((T, W), BF16),
                   jax.ShapeDtypeStruct((1, LANE), F32)],
        compiler_params=_params(("arbitrary", "arbitrary")),
    )(o, proj, gn, dycat)


def _ffn_act_fwd(up_pre, conv_w, *, name):
    T, F2 = up_pre.shape
    RC = _tile(T, ROWS_FFN_FWD, 8)
    nb = F2 // 2 // LANE
    K = conv_w.shape[0]

    def body(g_ref, v_ref, wg_ref, wv_ref, y_ref):
        def step(i, carry):
            r0 = pl.multiple_of(i * RC, RC)
            gate = _conv_down(_ext(g_ref, r0, T, True, False, RC), wg_ref, K)
            val = _conv_down(_ext(v_ref, r0, T, True, False, RC), wv_ref, K)
            y_ref[pl.ds(r0, RC), :] = (_silu(gate) * val).astype(y_ref.dtype)
            return carry
        lax.fori_loop(0, T // RC, step, 0)

    return pl.pallas_call(
        body, name=name, grid=(nb,),
        in_specs=_col_specs(T, (0, nb)) + [pl.BlockSpec((K, LANE), lambda j: (0, j)),
                                           pl.BlockSpec((K, LANE), lambda j: (0, nb + j))],
        out_specs=pl.BlockSpec((T, LANE), lambda j: (0, j)),
        out_shape=jax.ShapeDtypeStruct((T, F2 // 2), BF16), compiler_params=_params(("parallel",)),
    )(up_pre, up_pre, conv_w, conv_w)


def _ffn_act_bwd(up_pre, conv_w, dact, *, name):
    T, F2 = up_pre.shape
    RC = _tile(T, ROWS_FFN_BWD, 8)
    nb = F2 // 2 // LANE
    K = conv_w.shape[0]

    def body(g_ref, v_ref, wg_ref, wv_ref, da_ref, dg_ref, dv_ref, dwg_ref, dwv_ref):
        def step(i, accs):
            r0 = pl.multiple_of(i * RC, RC)
            g3 = _ext(g_ref, r0, T, True, True, RC)
            v3 = _ext(v_ref, r0, T, True, True, RC)
            gate2 = _conv_down(g3, wg_ref, K)
            val2 = _conv_down(v3, wv_ref, K)
            da2 = _ext(da_ref, r0, T, False, True, RC)
            dgate2 = da2 * val2 * _dsilu(gate2)
            dval2 = da2 * _silu(gate2)
            dgp, dvp, new = None, None, []
            for j in range(K):
                s = K - 1 - j
                tg = _up(dgate2, s, RC) * wg_ref[j:j + 1, :]
                tv = _up(dval2, s, RC) * wv_ref[j:j + 1, :]
                dgp = tg if dgp is None else dgp + tg
                dvp = tv if dvp is None else dvp + tv
                new.append(accs[2 * j] + _fold8(dgate2[:RC] * _down(g3[:RC + 8], s)))
                new.append(accs[2 * j + 1] + _fold8(dval2[:RC] * _down(v3[:RC + 8], s)))
            dg_ref[pl.ds(r0, RC), :] = dgp.astype(dg_ref.dtype)
            dv_ref[pl.ds(r0, RC), :] = dvp.astype(dv_ref.dtype)
            return tuple(new)

        accs = lax.fori_loop(0, T // RC, step, tuple(jnp.zeros((8, LANE), F32) for _ in range(2 * K)))
        for j in range(K):
            dwg_ref[j:j + 1, :] = jnp.sum(accs[2 * j], axis=0, keepdims=True)
            dwv_ref[j:j + 1, :] = jnp.sum(accs[2 * j + 1], axis=0, keepdims=True)

    col = pl.BlockSpec((T, LANE), lambda j: (0, j))
    wsp = pl.BlockSpec((K, LANE), lambda j: (0, j))
    return pl.pallas_call(
        body, name=name, grid=(nb,),
        in_specs=_col_specs(T, (0, nb)) + [wsp, pl.BlockSpec((K, LANE), lambda j: (0, nb + j)), col],
        out_specs=[col, col, wsp, wsp],
        out_shape=[jax.ShapeDtypeStruct((T, F2 // 2), BF16)] * 2 + [jax.ShapeDtypeStruct((K, F2 // 2), F32)] * 2,
        compiler_params=_params(("parallel",)),
    )(up_pre, up_pre, conv_w, conv_w, dact)


CPB = 8
CPB_SCAN = 4
GRP = 8
HP = lax.Precision.HIGH


def _tri(strict=False, upper=False):
    r = lax.broadcasted_iota(jnp.int32, (CHUNK, CHUNK), 0)
    c = lax.broadcasted_iota(jnp.int32, (CHUNK, CHUNK), 1)
    if upper:
        return c >= r
    return (r > c) if strict else (r >= c)


def _mm(a, b, dn="nn", precision=None):
    precision = HP if precision is None else precision
    return lax.dot_general(a, b, _DN[dn], precision=precision, preferred_element_type=F32)


def _mm16(a, b, dn="nn"):
    return lax.dot_general(a.astype(BF16), b.astype(BF16), _DN[dn], preferred_element_type=F32)


def _each(f, *cols):
    return [f(*xs) for xs in zip(*cols)]


def _decay(gam):
    return jnp.exp(jnp.where(_tri(), gam[:, :CHUNK] - gam.T[:CHUNK, :], -1e30))


def _delta_specs(T, H, cpb):
    rows = cpb * CHUNK
    col = lambda o: pl.BlockSpec((rows, LANE), functools.partial(lambda o, h, n: (n, o + h), o))
    bc = pl.BlockSpec((1, rows, LANE), lambda h, n: (h, n, 0))
    sq = pl.BlockSpec((1, cpb, CHUNK, CHUNK), lambda h, n: (h, n, 0, 0))
    vec = pl.BlockSpec((1, cpb, 1, LANE), lambda h, n: (h, n, 0, 0))
    return col, bc, sq, vec


def _delta_prep_fwd(qkv, gamB, bB, H, *, name):
    T = qkv.shape[0]
    N = T // CHUNK
    cpb = _tile(N, CPB, 8)
    grp = min(GRP, cpb)
    col, bc, sq, vec = _delta_specs(T, H, cpb)

    def body(q_ref, k_ref, v_ref, g_ref, b_ref, u_ref, w_ref, qd_ref, kd_ref, qk_ref, ti_ref, gl_ref):
        eye = (lax.broadcasted_iota(jnp.int32, (CHUNK, CHUNK), 0) == lax.broadcasted_iota(jnp.int32, (CHUNK, CHUNK), 1)).astype(F32)
        strict = _tri(strict=True)
        for c0 in range(0, cpb, grp):
            cs = list(range(c0, c0 + grp))
            rows = [slice(c * CHUNK, (c + 1) * CHUNK) for c in cs]
            q, k, v = ([r_[r, :] for r in rows] for r_ in (q_ref, k_ref, v_ref))
            bb = [b_ref[0, r, :] for r in rows]
            gam = [g_ref[0, r, :] for r in rows]
            D = _each(_decay, gam)
            e = _each(jnp.exp, gam)
            kk = _each(lambda k_: _mm(k_, k_, "nt"), k)
            X = _each(lambda kk_, D_, b_: -(jnp.where(strict, kk_ * D_, 0.0) * b_[:, :CHUNK]), kk, D, bb)
            R = _each(lambda x: eye + x, X)
            for _ in range(5):
                X = _each(lambda x: _mm(x, x), X)
                R = _each(lambda r, x: r + _mm(r, x), R, X)
            u = _each(lambda r, b_, v_: _mm(r, b_ * v_), R, bb, v)
            w = _each(lambda r, b_, e_, k_: _mm(r, b_ * e_ * k_), R, bb, e, k)
            qk = _each(lambda q_, k_, D_: _mm(q_, k_, "nt") * D_, q, k, D)
            for i, c in enumerate(cs):
                glast = gam[i][CHUNK - 1:CHUNK, :]
                u_ref[rows[i], :] = u[i]
                w_ref[rows[i], :] = w[i]
                qd_ref[rows[i], :] = e[i] * q[i]
                kd_ref[rows[i], :] = jnp.exp(glast - gam[i]) * k[i]
                qk_ref[0, c] = qk[i]
                ti_ref[0, c] = R[i]
                gl_ref[0, c] = jnp.exp(glast)

    full = jax.ShapeDtypeStruct((T, H * LANE), F32)
    sqs = jax.ShapeDtypeStruct((H, N, CHUNK, CHUNK), F32)
    return pl.pallas_call(
        body, name=name, grid=(H, N // cpb),
        in_specs=[col(0), col(H), col(2 * H), bc, bc],
        out_specs=[col(0)] * 4 + [sq, sq, vec],
        out_shape=[full] * 4 + [sqs, sqs, jax.ShapeDtypeStruct((H, N, 1, LANE), F32)],
        compiler_params=_params(("parallel", "parallel")),
    )(qkv, qkv, qkv, gamB, bB)


def _scan_specs(H, N, cpb, hb, rev):
    nbk = N // cpb
    blk = (lambda n: nbk - 1 - n) if rev else (lambda n: n)
    col = pl.BlockSpec((cpb * CHUNK, hb * LANE), lambda h, n: (blk(n), h))
    sq = pl.BlockSpec((hb, cpb, CHUNK, CHUNK), lambda h, n: (h, blk(n), 0, 0))
    vec = pl.BlockSpec((hb, cpb, 1, LANE), lambda h, n: (h, blk(n), 0, 0))
    st = pl.BlockSpec((hb, cpb, HEAD, HEAD), lambda h, n: (h, blk(n), 0, 0))
    return col, sq, vec, st


def _delta_scan_fwd(u, w, qd, kd, qk, gl, H, *, name):
    T = u.shape[0]
    N = T // CHUNK
    cpb = _tile(N, CPB_SCAN, 4)
    hb = min(GRP, H)
    col, sq, vec, st = _scan_specs(H, N, cpb, hb, False)
    lanes = [slice(j * LANE, (j + 1) * LANE) for j in range(hb)]
    heads = list(range(hb))

    def body(u_ref, w_ref, qd_ref, kd_ref, qk_ref, gl_ref, o_ref, vn_ref, ss_ref, s_scr):
        @pl.when(pl.program_id(1) == 0)
        def _():
            s_scr[...] = jnp.zeros_like(s_scr)

        def step(c, states):
            rows = pl.ds(pl.multiple_of(c * CHUNK, CHUNK), CHUNK)
            S = list(states)
            for j in heads:
                ss_ref[j, c] = S[j]
            wS = _each(lambda ln, s: _mm16(w_ref[rows, ln], s), lanes, S)
            qS = _each(lambda ln, s: _mm16(qd_ref[rows, ln], s), lanes, S)
            vn = _each(lambda ln, ws: u_ref[rows, ln] - ws, lanes, wS)
            o = _each(lambda j, qs, vn_: qs + _mm16(qk_ref[j, c], vn_), heads, qS, vn)
            new = _each(lambda j, ln, s, vn_: s * gl_ref[j, c] + _mm16(kd_ref[rows, ln], vn_, "tn"),
                        heads, lanes, S, vn)
            for j in heads:
                o_ref[rows, lanes[j]] = o[j]
                vn_ref[rows, lanes[j]] = vn[j]
            return tuple(new)
        out = lax.fori_loop(0, cpb, step, tuple(s_scr[j] for j in heads))
        for j in heads:
            s_scr[j] = out[j]

    full = jax.ShapeDtypeStruct((T, H * LANE), F32)
    return pl.pallas_call(
        body, name=name, grid=(H // hb, N // cpb),
        in_specs=[col] * 4 + [sq, vec],
        out_specs=[col, col, st],
        out_shape=[full, full, jax.ShapeDtypeStruct((H, N, HEAD, HEAD), F32)],
        scratch_shapes=[pltpu.VMEM((hb, HEAD, HEAD), F32)],
        compiler_params=_params(("parallel", "arbitrary")),
    )(u, w, qd, kd, qk, gl)


def _delta_scan_bwd(do, w, qd, kd, vn, qk, gl, ss, H, *, name):
    T = do.shape[0]
    N = T // CHUNK
    cpb = _tile(N, CPB_SCAN, 4)
    hb = min(GRP, H)
    col, sq, vec, st = _scan_specs(H, N, cpb, hb, True)
    lanes = [slice(j * LANE, (j + 1) * LANE) for j in range(hb)]
    heads = list(range(hb))

    def body(do_ref, w_ref, qd_ref, kd_ref, vn_ref, qk_ref, gl_ref, ss_ref,
             du_ref, dw_ref, dqd_ref, dkd_ref, dqk_ref, dgl_ref, ds_scr):
        @pl.when(pl.program_id(1) == 0)
        def _():
            ds_scr[...] = jnp.zeros_like(ds_scr)

        def step(i, dstates):
            c = cpb - 1 - i
            rows = pl.ds(pl.multiple_of(c * CHUNK, CHUNK), CHUNK)
            dS = list(dstates)
            S = [ss_ref[j, c] for j in heads]
            dov = [do_ref[rows, ln] for ln in lanes]
            vnv = [vn_ref[rows, ln] for ln in lanes]
            a1 = _each(lambda j, d_: _mm16(qk_ref[j, c], d_, "tn"), heads, dov)
            a2 = _each(lambda ln, ds: _mm16(kd_ref[rows, ln], ds), lanes, dS)
            dvn = _each(lambda x, y: x + y, a1, a2)
            dqd = _each(lambda d_, s: _mm16(d_, s, "nt"), dov, S)
            dkd = _each(lambda v_, ds: _mm16(v_, ds, "nt"), vnv, dS)
            dqk = _each(lambda d_, v_: _mm16(d_, v_, "nt"), dov, vnv)
            dw = _each(lambda dv_, s: -_mm16(dv_, s, "nt"), dvn, S)
            b1 = _each(lambda ln, d_: _mm16(qd_ref[rows, ln], d_, "tn"), lanes, dov)
            b2 = _each(lambda ln, dv_: _mm16(w_ref[rows, ln], dv_, "tn"), lanes, dvn)
            new = _each(lambda j, x, y, ds: x + ds * gl_ref[j, c] - y, heads, b1, b2, dS)
            for j in heads:
                du_ref[rows, lanes[j]] = dvn[j]
                dw_ref[rows, lanes[j]] = dw[j]
                dqd_ref[rows, lanes[j]] = dqd[j]
                dkd_ref[rows, lanes[j]] = dkd[j]
                dqk_ref[j, c] = dqk[j]
                dgl = jnp.sum(jnp.sum(dS[j] * S[j], axis=1, keepdims=True), axis=0, keepdims=True)
                dgl_ref[j, c] = jnp.broadcast_to(dgl, (1, LANE))
            return tuple(new)
        out = lax.fori_loop(0, cpb, step, tuple(ds_scr[j] for j in heads))
        for j in heads:
            ds_scr[j] = out[j]

    full = jax.ShapeDtypeStruct((T, H * LANE), F32)
    return pl.pallas_call(
        body, name=name, grid=(H // hb, N // cpb),
        in_specs=[col] * 5 + [sq, vec, st],
        out_specs=[col] * 4 + [sq, vec],
        out_shape=[full] * 4 + [jax.ShapeDtypeStruct((H, N, CHUNK, CHUNK), F32), jax.ShapeDtypeStruct((H, N, 1, LANE), F32)],
        scratch_shapes=[pltpu.VMEM((hb, HEAD, HEAD), F32)],
        compiler_params=_params(("parallel", "arbitrary")),
    )(do, w, qd, kd, vn, qk, gl, ss)


def _delta_prep_bwd(qkv, gamB, bB, ti, u, w, qk, du, dw, dqd, dkd, dqk, dgl, H, *, name):
    T = qkv.shape[0]
    N = T // CHUNK
    cpb = _tile(N, CPB, 8)
    grp = min(GRP, cpb)
    col, bc, sq, vec = _delta_specs(T, H, cpb)

    def body(q_ref, k_ref, v_ref, g_ref, b_ref, ti_ref, u_ref, w_ref, qk_ref,
             du_ref, dw_ref, dqd_ref, dkd_ref, dqk_ref, dgl_ref,
             dq_ref, dk_ref, dv_ref, dg_ref, db_ref):
        ones = jnp.ones((CHUNK, LANE), F32)
        strict = _tri(strict=True)
        last = lax.broadcasted_iota(jnp.int32, (CHUNK, LANE), 0) == CHUNK - 1
        lsum = lambda x: jnp.sum(x, axis=-1, keepdims=True)
        for c0 in range(0, cpb, grp):
            cs = list(range(c0, c0 + grp))
            rows = [slice(c * CHUNK, (c + 1) * CHUNK) for c in cs]
            ld = lambda r_: [r_[r, :] for r in rows]
            q, k, v, uv, wv, duv, dwv, dqd_v, dkd_v = (ld(r_) for r_ in (q_ref, k_ref, v_ref, u_ref, w_ref, du_ref, dw_ref, dqd_ref, dkd_ref))
            bb = [b_ref[0, r, :] for r in rows]
            gam = [g_ref[0, r, :] for r in rows]
            Ti = [ti_ref[0, c] for c in cs]
            QK = [qk_ref[0, c] for c in cs]
            dqk_v = [dqk_ref[0, c] for c in cs]
            D = _each(_decay, gam)
            e = _each(jnp.exp, gam)
            glast = [g_[CHUNK - 1:CHUNK, :] for g_ in gam]
            eL = _each(lambda gl_, g_: jnp.exp(gl_ - g_), glast, gam)
            kk = _each(lambda k_: _mm(k_, k_, "nt"), k)
            KKD = _each(lambda kk_, D_: jnp.where(strict, kk_ * D_, 0.0), kk, D)
            dru = _each(lambda t, d_: _mm(t, d_, "tn"), Ti, duv)
            drw = _each(lambda t, d_: _mm(t, d_, "tn"), Ti, dwv)
            l1 = _each(lambda a, b: _mm(a, b, "nt"), dru, uv)
            l2 = _each(lambda a, b: _mm(a, b, "nt"), drw, wv)
            dL = _each(lambda a, b: jnp.where(strict, -(a + b), 0.0), l1, l2)
            Mm = _each(lambda dl, b_: dl * b_[:, :CHUNK], dL, bb)
            dKK = _each(lambda m_, D_: m_ * D_, Mm, D)
            dQK = _each(lambda a, D_: a * D_, dqk_v, D)
            P = _each(lambda m_, kkd, a, qk_: m_ * kkd + a * qk_, Mm, KKD, dqk_v, QK)
            q1 = _each(lambda a, k_: _mm(a, k_), dQK, k)
            k1 = _each(lambda a, q_: _mm(a, q_, "tn"), dQK, q)
            k2 = _each(lambda a, k_: _mm(a, k_), dKK, k)
            k3 = _each(lambda a, k_: _mm(a, k_, "tn"), dKK, k)
            s1 = _each(lambda dl, kkd: _mm(dl * kkd, ones), dL, KKD)
            p1 = _each(lambda p_: _mm(p_, ones), P)
            p2 = _each(lambda p_: _mm(p_, ones, "tn"), P)
            for i, c in enumerate(cs):
                r = rows[i]
                bek = bb[i] * e[i]
                kdv = eL[i] * k[i]
                dq_ref[r, :] = q1[i] + e[i] * dqd_v[i]
                dk_ref[r, :] = k1[i] + k2[i] + k3[i] + bek * drw[i] + eL[i] * dkd_v[i]
                dv_ref[r, :] = bb[i] * dru[i]
                db_ref[0, r, :] = s1[i] + lsum(dru[i] * v[i]) + lsum(drw[i] * e[i] * k[i])
                dgam = (p1[i] - p2[i] + lsum(drw[i] * bek * k[i]) + lsum(dqd_v[i] * e[i] * q[i])
                        - lsum(dkd_v[i] * kdv))
                xlast = jnp.sum(lsum(dkd_v[i] * kdv), axis=0, keepdims=True) + jnp.exp(glast[i]) * dgl_ref[0, c]
                dg_ref[0, r, :] = dgam + jnp.where(last, xlast, 0.0)

    full = jax.ShapeDtypeStruct((T, H * LANE), F32)
    bcs = jax.ShapeDtypeStruct((H, T, LANE), F32)
    return pl.pallas_call(
        body, name=name, grid=(H, N // cpb),
        in_specs=[col(0), col(H), col(2 * H), bc, bc, sq, col(0), col(0), sq, col(0), col(0), col(0), col(0), sq, vec],
        out_specs=[col(0), col(0), col(0), bc, bc],
        out_shape=[full, full, full, bcs, bcs],
        compiler_params=_params(("parallel", "parallel")),
    )(qkv, qkv, qkv, gamB, bB, ti, u, w, qk, du, dw, dqd, dkd, dqk, dgl)


def _adam(parts, w, m, v, *, name, own=None, me=None):
    P, R, C = parts.shape
    tr = _tile(R, 256, 8)
    n_own = 0 if own is None else 2

    def body(*refs):
        p_ref, w_ref, m_ref, v_ref, g_ref, d_ref, nm_ref, nv_ref = refs[n_own:]
        g = None
        for i in range(P):
            t = p_ref[i].astype(F32)
            if n_own:
                t = jnp.where(refs[0][0] == i, refs[1][...].astype(F32), t)
            g = t if g is None else g + t
        mn = ADAM_B1 * m_ref[...] + (1.0 - ADAM_B1) * g
        vn = ADAM_B2 * v_ref[...] + (1.0 - ADAM_B2) * (g * g)
        m_hat = mn / (1.0 - ADAM_B1 ** ADAM_STEP)
        v_hat = vn / (1.0 - ADAM_B2 ** ADAM_STEP)
        g_ref[...] = g
        d_ref[...] = -ADAM_LR * (m_hat / (jnp.sqrt(v_hat) + ADAM_EPS) + ADAM_WD * w_ref[...])
        nm_ref[...] = mn
        nv_ref[...] = vn

    blk = pl.BlockSpec((tr, C), lambda i: (i, 0))
    return pl.pallas_call(
        body, name=name, grid=(R // tr,),
        in_specs=[pl.BlockSpec(memory_space=pltpu.SMEM), blk][:n_own] + [pl.BlockSpec((P, tr, C), lambda i: (0, i, 0)), blk, blk, blk],
        out_specs=[blk] * 4, out_shape=[jax.ShapeDtypeStruct((R, C), F32)] * 4,
        compiler_params=_params(("parallel",)),
    )(*([me, own] if n_own else []), parts, w, m, v)


def _mesh_pos():
    return lax.axis_index("x"), lax.axis_index("y"), lax.axis_index("c")


def _peer(k):
    x, y, c = _mesh_pos()
    px, py, pc = x ^ ((k >> 2) & 1), y ^ ((k >> 1) & 1), c ^ (k & 1)
    return (px, py, pc), 4 * px + 2 * py + pc


def _exchange(arrays, scatter, *, name):
    n = len(arrays)
    blocks = [a.shape[1:] if scatter else a.shape for a in arrays]

    def body(*refs):
        srcs, dsts = refs[:n], refs[n:2 * n]
        send_sems, recv_sems, local_sems = refs[2 * n:]
        x, y, c = _mesh_pos()
        me = 4 * x + 2 * y + c
        local, sends = [], []
        for a in range(n):
            cp = pltpu.make_async_copy(srcs[a].at[me] if scatter else srcs[a], dsts[a].at[me], local_sems.at[a])
            cp.start()
            local.append(cp)
            for k in range(1, N_DEV):
                dev, idx = _peer(k)
                cp = pltpu.make_async_remote_copy(
                    src_ref=srcs[a].at[idx] if scatter else srcs[a], dst_ref=dsts[a].at[me],
                    send_sem=send_sems.at[a * N_DEV + k], recv_sem=recv_sems.at[a * N_DEV + k],
                    device_id=dev, device_id_type=MESH)
                cp.start()
                sends.append(cp)
        for a in range(n):
            for k in range(1, N_DEV):
                dev, idx = _peer(k)
                pltpu.make_async_remote_copy(
                    src_ref=srcs[a].at[idx] if scatter else srcs[a], dst_ref=dsts[a].at[idx],
                    send_sem=send_sems.at[a * N_DEV + k], recv_sem=recv_sems.at[a * N_DEV + k],
                    device_id=dev, device_id_type=MESH).wait_recv()
        for cp in sends:
            cp.wait_send()
        for cp in local:
            cp.wait()

    anyspec = pl.BlockSpec(memory_space=pl.ANY)
    return pl.pallas_call(
        body, name=name, in_specs=[anyspec] * n, out_specs=[anyspec] * n,
        out_shape=[jax.ShapeDtypeStruct((N_DEV,) + tuple(b), a.dtype) for a, b in zip(arrays, blocks)],
        scratch_shapes=[pltpu.SemaphoreType.DMA((n * N_DEV,)), pltpu.SemaphoreType.DMA((n * N_DEV,)),
                        pltpu.SemaphoreType.DMA((n,))],
    )(*arrays)


_ANY = pl.BlockSpec(memory_space=pl.ANY)
_SEM = pl.BlockSpec(memory_space=pltpu.SEMAPHORE)
_EFFECT = pltpu.SideEffectType.DATAFLOW_SIDE_EFFECTING


def _in_hbm(a):
    return pltpu.with_memory_space_constraint(a, pltpu.HBM)


def _split_copy(src, land, send, recv, k, me, scatter, landed):
    dev, idx = _peer(k)
    return pltpu.make_async_remote_copy(
        src_ref=src.at[idx] if scatter else src, dst_ref=land.at[idx if landed else me],
        send_sem=send.at[k], recv_sem=recv.at[k], device_id=dev, device_id_type=MESH)


def _split_start(srcs, lands, scatter, *, name):
    n = len(srcs)

    def body(*refs):
        src, land, send, recv, token = refs[:n], refs[n:2 * n], refs[2 * n:3 * n], refs[3 * n:4 * n], refs[-1]
        x, y, c = _mesh_pos()
        me = 4 * x + 2 * y + c
        for a in range(n):
            for k in range(1, N_DEV):
                _split_copy(src[a], land[a], send[a], recv[a], k, me, scatter, False).start()
        token[...] = jnp.zeros_like(token)

    outs = pl.pallas_call(
        body, name=name,
        out_shape=[pltpu.SemaphoreType.DMA((N_DEV,))] * (2 * n) + [pltpu.HBM(t.shape, t.dtype) for t in list(srcs) + list(lands)]
        + [jax.ShapeDtypeStruct((8, LANE), F32)],
        in_specs=[_ANY] * (2 * n), out_specs=[_SEM] * (2 * n) + [_ANY] * (2 * n) + [pl.BlockSpec(memory_space=pltpu.VMEM)],
        input_output_aliases={i: 2 * n + i for i in range(2 * n)},
        compiler_params=pltpu.CompilerParams(has_side_effects=_EFFECT),
    )(*[_in_hbm(t) for t in list(srcs) + list(lands)])
    handles = [(outs[a], outs[n + a], outs[2 * n + a], outs[3 * n + a]) for a in range(n)]
    return handles, outs[-1]


def _split_wait(handle, after, scatter, *, name):
    send, recv, src_thru, land_thru = handle

    def body(src_ref, land_ref, send_ref, recv_ref, after_ref, src_out, land_out):
        x, y, c = _mesh_pos()
        me = 4 * x + 2 * y + c
        for k in range(1, N_DEV):
            cp = _split_copy(src_ref, land_ref, send_ref, recv_ref, k, me, scatter, True)
            cp.wait_send()
            cp.wait_recv()

    return pl.pallas_call(
        body, name=name,
        out_shape=(pltpu.HBM(src_thru.shape, src_thru.dtype), pltpu.HBM(land_thru.shape, land_thru.dtype)),
        in_specs=(_ANY, _ANY, _SEM, _SEM, _ANY), out_specs=(_ANY, _ANY), input_output_aliases={0: 0, 1: 1},
        compiler_params=pltpu.CompilerParams(has_side_effects=_EFFECT),
    )(src_thru, land_thru, send, recv, after)[1]


def _local_step(x, p, tgt, S, wt, conv, emit):
    T, D = x.shape
    CW = DNW = D // 2
    H = DNW // HEAD
    nA, nD = CW // LANE, DNW // LANE
    qkv_off, z_off, ab_off = 3 * nA, 3 * nA + 3 * nD, 3 * nA + 4 * nD
    alog = jnp.pad(S["a_log"], ((0, 0), (0, LANE - H)))
    dtb = jnp.pad(S["dt_bias"], ((0, 0), (0, LANE - H)))

    h1 = _rms_fwd(x, S["g_mix"], name="rms1_fwd")
    w_in, cv = wt("w_in", h1), conv(h1)
    proj = _matmul(h1, w_in, "nn", name="mm_in")
    y_a = _group_a_fwd(proj, cv["conv_a"], CW, name="group_a_fwd")
    qkv = _qkv_fwd(proj, cv["conv_qkv"], qkv_off, H, name="qkv_fwd")
    gb, gamc = _gates_fwd(proj, alog, dtb, ab_off, H, name="gates_fwd")
    bcast = lambda cols: jnp.broadcast_to(cols.T[:, :, None], (H, T, LANE))
    gamB, bB = bcast(gamc[:, :H]), bcast(gb[:, H:2 * H])
    u, w, qd, kd, qk, ti, gl = _delta_prep_fwd(qkv, gamB, bB, H, name="delta_prep_fwd")
    o, vn, ss = _delta_scan_fwd(u, w, qd, kd, qk, gl, H, name="delta_scan_fwd")
    y_b = _gated_norm_fwd(o, proj, S["dn_g"], z_off, name="gated_norm_fwd")
    ycat = jnp.concatenate([y_a, y_b], axis=1)
    w_out = wt("w_out", ycat)
    rows = dict(tm=ROW_TILE, tn=D)
    x1, h2 = _matmul(ycat, w_out, "nn", name="mm_out", out_dtypes=(F32, BF16), epilogue=_epi_residual_rms,
                     extras=(x,), vec_extras=(S["g_ffn"],), **rows)
    w_up = wt("w_up", h2)
    up_pre = _matmul(h2, w_up, "nn", name="mm_up", b_shards=True, tn=SHARD_TILE)
    act = _ffn_act_fwd(up_pre, cv["conv_ffn"], name="ffn_act_fwd")
    w_down = wt("w_down", act)
    x2, h3 = _matmul(act, w_down, "nn", name="mm_down", out_dtypes=(F32, BF16), epilogue=_epi_residual_rms,
                     extras=(x1,), vec_extras=(S["g_ple"],), **rows)
    w_pp, w_pg = wt("w_pp", h3), wt("w_pg", h3)
    pp = _matmul(p, w_pp, "nn", name="mm_pp", b_shards=True)

    def ple_epi(acc, x2r, ppr):
        s = jax.nn.sigmoid(acc)
        return x2r + s * ppr, s

    x3, sg = _matmul(h3, w_pg, "nn", name="mm_pg", out_dtypes=(F32, F32), epilogue=ple_epi, extras=(x2, pp), tm=512)
    dx3, dg_final, loss = _final_loss(x3, S["g_final"], tgt, name="final_loss")

    G = {"g_final": dg_final}
    dpg, dpp = _ple_bwd(dx3, pp, sg, name="ple_bwd")
    tok = emit({"w_pp": _matmul(p, dpp, "tn", name="mm_dwpp", out_dtypes=(BF16,), out_shards=True),
                "w_pg": _matmul(h3, dpg, "tn", name="mm_dwpg", out_dtypes=(BF16,))})
    bwd = dict(out_dtypes=(F32, BF16), epilogue=_epi_rms_bwd(2), n_vec=1, **rows)
    dx2, dx2b, G["g_ple"] = _matmul(dpg, w_pg, "nt", name="mm_dh3", after=tok, extras=(x2, dx3),
                                    vec_extras=(S["g_ple"],), **bwd)
    tok = emit({"w_down": _matmul(act, dx2b, "tn", name="mm_dwdown", out_dtypes=(BF16,))})
    dact = _matmul(dx2b, w_down, "nt", name="mm_dact", after=tok)
    dup_g, dup_v, dcf_g, dcf_v = _ffn_act_bwd(up_pre, cv["conv_ffn"], dact, name="ffn_act_bwd")
    G["conv_ffn"] = jnp.concatenate([dcf_g, dcf_v], axis=1)
    dup = jnp.concatenate([dup_g, dup_v], axis=1)
    tok = emit({"w_up": _matmul(h2, dup, "tn", name="mm_dwup", out_dtypes=(BF16,), out_shards=True, tn=SHARD_TILE)})
    dx1, dx1b, G["g_ffn"] = _matmul(dup, w_up, "nt", name="mm_dh2", after=tok, b_shards=True, tk=SHARD_TILE,
                                    extras=(x1, dx2), vec_extras=(S["g_ffn"],), **bwd)
    tok = emit({"w_out": _matmul(ycat, dx1b, "tn", name="mm_dwout", out_dtypes=(BF16,))})
    dycat = _matmul(dx1b, w_out, "nt", name="mm_dycat", after=tok)
    do, dz, G["dn_g"] = _gated_norm_bwd(o, proj, S["dn_g"], dycat, z_off, nA, name="gated_norm_bwd")
    du, dw, dqd, dkd, dqk, dgl = _delta_scan_bwd(do, w, qd, kd, vn, qk, gl, ss, H, name="delta_scan_bwd")
    dq, dk, dv, dgB, dbB = _delta_prep_bwd(qkv, gamB, bB, ti, u, w, qk, du, dw, dqd, dkd, dqk, dgl, H,
                                           name="delta_prep_bwd")
    dgb = jnp.pad(jnp.concatenate([dgB[:, :, 0].T, dbB[:, :, 0].T], axis=1), ((0, 0), (0, LANE - 2 * H)))
    dab, dal, ddt = _gates_bwd(proj, alog, dtb, dgb, ab_off, H, name="gates_bwd")
    G["a_log"], G["dt_bias"] = dal[:, :H], ddt[:, :H]
    dqkv, G["conv_qkv"] = _qkv_bwd(proj, cv["conv_qkv"], dq, dk, dv, qkv_off, H, name="qkv_bwd")
    dax, dab_, dac, G["conv_a"] = _group_a_bwd(proj, cv["conv_a"], dycat, CW, name="group_a_bwd")
    in_p = w_in.shape[1]
    dproj = jnp.concatenate([dax, dab_, dac, dqkv, dz, dab, jnp.zeros((T, in_p - (ab_off + 1) * LANE), BF16)], axis=1)
    tok = emit({"w_in": _matmul(h1, dproj, "tn", name="mm_dwin", out_dtypes=(BF16,))})
    grad_x, G["g_mix"] = _matmul(dproj, w_in, "nt", name="mm_dh1", after=tok, extras=(x, dx1), vec_extras=(S["g_mix"],),
                                 out_dtypes=(F32,), epilogue=_epi_rms_bwd(1), n_vec=1, **rows)
    return loss, grad_x, G


def _pad_cols(a, n):
    return jnp.pad(a, ((0, 0), (0, n - a.shape[1])))


def _col_sharded(landed):
    _, R, C = landed.shape
    return jnp.transpose(landed, (1, 0, 2)).reshape(R, N_DEV * C)


def _col_parts(full):
    R, C8 = full.shape
    return jnp.transpose(full.reshape(R, N_DEV, C8 // N_DEV), (1, 0, 2))


def kernel(x, p, norm_mix_g, w_in, conv_a_w, conv_qkv_w, a_log, dt_bias, dn_norm_g, w_out, norm_ffn_g, w_up, conv_ffn_w, w_down, norm_ple_g, w_ple_gate, w_ple_proj, final_norm_g, loss_target, m_norm_mix_g, m_w_in, m_conv_a_w, m_conv_qkv_w, m_a_log, m_dt_bias, m_dn_norm_g, m_w_out, m_norm_ffn_g, m_w_up, m_conv_ffn_w, m_w_down, m_norm_ple_g, m_w_ple_gate, m_w_ple_proj, m_final_norm_g, v_norm_mix_g, v_w_in, v_conv_a_w, v_conv_qkv_w, v_a_log, v_dt_bias, v_dn_norm_g, v_w_out, v_norm_ffn_g, v_w_up, v_conv_ffn_w, v_w_down, v_norm_ple_g, v_w_ple_gate, v_w_ple_proj, v_final_norm_g):
    T, D = x.shape[1], x.shape[2]
    xd, _, cd = _mesh_pos()
    me = 4 * xd + 2 * lax.axis_index("y") + cd

    conv_sh = [conv_a_w[0], conv_qkv_w[0], conv_ffn_w[0]]
    conv_n = [c.size for c in conv_sh]
    pack_rows = -(-sum(conv_n) // LANE)
    conv_pack = jnp.pad(jnp.concatenate([c.reshape(-1) for c in conv_sh]), (0, pack_rows * LANE - sum(conv_n))).reshape(pack_rows, LANE)
    names = ["w_in", "conv", "w_out", "w_up", "w_down", "w_pg", "w_pp"]
    shards = [w_in[0].astype(BF16), conv_pack, w_out[0].astype(BF16), w_up[0].astype(BF16), w_down[0].astype(BF16),
              w_ple_gate[0].astype(BF16), w_ple_proj[0].astype(BF16)]
    empty_slots = lambda blocks: [lax.empty((N_DEV,) + tuple(b.shape), b.dtype) for b in blocks]
    handles, tok0 = _split_start(shards, empty_slots(shards), False, name="gather_start")
    handle = dict(zip(names, handles))
    own = dict(zip(names, shards))
    in_cols = N_DEV * w_in.shape[2]
    in_p = (in_cols // LANE) * LANE + AB_PAD
    in_place = {"w_up", "w_pp"}

    def gathered(name, after):
        landed = _split_wait(handle[name], after, False, name="gather_wait_" + name)
        return lax.dynamic_update_index_in_dim(landed, own[name], me, 0)

    def wt(name, after):
        landed = gathered(name, after)
        if name in in_place:
            return landed
        return _pad_cols(_col_sharded(landed), in_p) if name == "w_in" else landed.reshape(-1, D)

    def conv(after):
        flat = gathered("conv", after).reshape(N_DEV, pack_rows * LANE)
        out, o_ = {}, 0
        for nm, c, n_ in zip(("conv_a", "conv_qkv", "conv_ffn"), conv_sh, conv_n):
            out[nm] = _col_sharded(flat[:, o_:o_ + n_].reshape((N_DEV,) + c.shape))
            o_ += n_
        return out

    pending, mine = {}, {}

    def emit(grads):
        parts = [g if nm in in_place else _col_parts(g[:, :in_cols]) if nm == "w_in" else g.reshape(N_DEV, -1, D)
                 for nm, g in grads.items()]
        hs, tok = _split_start(parts, empty_slots([q[0] for q in parts]), True, name="scatter_start_" + "_".join(grads))
        pending.update(zip(grads, hs))
        mine.update({nm: lax.dynamic_index_in_dim(q, me, 0, keepdims=False) for nm, q in zip(grads, parts)})
        return tok

    S = {
        "g_mix": norm_mix_g + tok0[0, 0], "a_log": a_log, "dt_bias": dt_bias, "dn_g": dn_norm_g, "g_ffn": norm_ffn_g,
        "g_ple": norm_ple_g, "g_final": final_norm_g.reshape(1, D),
    }

    loss_v, grad_x, G = _local_step(x[0], p[0, 0], loss_target[0], S, wt, conv, emit)
    loss = lax.psum(loss_v[0, 0], ("x", "y", "c"))

    small_names = ["g_mix", "g_ffn", "g_ple", "g_final", "dn_g", "a_log", "dt_bias", "conv_a", "conv_qkv", "conv_ffn"]
    small_rows, pieces = [], []
    for nm in small_names:
        g_ = G[nm].reshape(-1)
        r_ = -(-g_.size // (8 * LANE)) * 8
        small_rows.append(r_)
        pieces.append(jnp.pad(g_, (0, r_ * LANE - g_.size)).reshape(r_, LANE))
    (small_l,) = _exchange([jnp.concatenate(pieces, axis=0)], False, name="gather_small_grads")
    landed = {nm: _split_wait(h_, grad_x, True, name="scatter_wait_" + nm) for nm, h_ in pending.items()}
    big_l = [landed[nm] for nm in ("w_in", "w_out", "w_up", "w_down", "w_pg", "w_pp")]

    def small_parts(nm):
        i = small_names.index(nm)
        r0 = sum(small_rows[:i])
        shp = G[nm].shape
        return small_l[:, r0:r0 + small_rows[i], :].reshape(N_DEV, -1)[:, :G[nm].size].reshape((N_DEV,) + shp)

    def conv_parts(nm, shard):
        full = small_parts(nm)
        C = shard.shape[-1]
        return lax.dynamic_slice_in_dim(full, me * C, C, axis=2)

    def adam(parts, w_, m_, v_, nm, own_=None):
        shp = w_.shape
        w2, m2, v2 = (t.reshape(parts.shape[1:]) for t in (w_, m_, v_))
        kw = {} if own_ is None else {"own": own_, "me": me.astype(jnp.int32).reshape(1)}
        return tuple(t.reshape(shp) for t in _adam(parts, w2, m2, v2, name="adam_" + nm, **kw))

    res = [
        adam(small_parts("g_mix"), norm_mix_g, m_norm_mix_g, v_norm_mix_g, "norm_mix_g"),
        adam(big_l[0], w_in, m_w_in, v_w_in, "w_in", mine["w_in"]),
        adam(conv_parts("conv_a", conv_a_w), conv_a_w, m_conv_a_w, v_conv_a_w, "conv_a_w"),
        adam(conv_parts("conv_qkv", conv_qkv_w), conv_qkv_w, m_conv_qkv_w, v_conv_qkv_w, "conv_qkv_w"),
        adam(small_parts("a_log"), a_log, m_a_log, v_a_log, "a_log"),
        adam(small_parts("dt_bias"), dt_bias, m_dt_bias, v_dt_bias, "dt_bias"),
        adam(small_parts("dn_g"), dn_norm_g, m_dn_norm_g, v_dn_norm_g, "dn_norm_g"),
        adam(big_l[1], w_out, m_w_out, v_w_out, "w_out", mine["w_out"]),
        adam(small_parts("g_ffn"), norm_ffn_g, m_norm_ffn_g, v_norm_ffn_g, "norm_ffn_g"),
        adam(big_l[2], w_up, m_w_up, v_w_up, "w_up", mine["w_up"]),
        adam(conv_parts("conv_ffn", conv_ffn_w), conv_ffn_w, m_conv_ffn_w, v_conv_ffn_w, "conv_ffn_w"),
        adam(big_l[3], w_down, m_w_down, v_w_down, "w_down", mine["w_down"]),
        adam(small_parts("g_ple"), norm_ple_g, m_norm_ple_g, v_norm_ple_g, "norm_ple_g"),
        adam(big_l[4], w_ple_gate, m_w_ple_gate, v_w_ple_gate, "w_ple_gate", mine["w_pg"]),
        adam(big_l[5], w_ple_proj, m_w_ple_proj, v_w_ple_proj, "w_ple_proj", mine["w_pp"]),
        adam(small_parts("g_final"), final_norm_g.reshape(1, D), m_final_norm_g.reshape(1, D),
             v_final_norm_g.reshape(1, D), "final_norm_g"),
    ]
    res[-1] = tuple(t.reshape(D) for t in res[-1])
    grads, deltas, new_m, new_v = zip(*res)
    return (loss, grad_x[None], *grads, *deltas, *new_m, *new_v)
```

```python
import functools

import jax
import jax.numpy as jnp
from jax import lax
from jax.experimental import pallas as pl
from jax.experimental.pallas import tpu as pltpu

F32 = jnp.float32
BF16 = jnp.bfloat16

EPS = 1e-6
CHUNK = 64
HEAD = 128
LANE = 128
N_DEV = 8
AB_PAD = 512

ADAM_LR = 0.001
ADAM_B1 = 0.9
ADAM_B2 = 0.999
ADAM_EPS = 1e-08
ADAM_WD = 0.01
ADAM_STEP = 10

MESH = pl.DeviceIdType.MESH


def _tile(dim, target, align=LANE):
    if dim <= target:
        return dim
    t = (target // align) * align
    while t > align and dim % t:
        t -= align
    assert dim % t == 0, (dim, target)
    return t


def _params(sem, vmem_mb=48):
    return pltpu.CompilerParams(dimension_semantics=sem, vmem_limit_bytes=vmem_mb << 20)


_DN = {"nn": (((1,), (0,)), ((), ())), "nt": (((1,), (1,)), ((), ())), "tn": (((0,), (0,)), ((), ()))}
SHARD_TILE = 1408


def _matmul(a, b, mode, *, name, out_dtypes=(F32,), epilogue=None, extras=(), vec_extras=(), n_vec=0, after=None,
            b_shards=False, out_shards=False, tm=1024, tn=1024, tk=2048):
    shard_w = b.shape[2] if b_shards else None
    if b_shards:
        b_rows, b_cols = b.shape[1], N_DEV * shard_w
    else:
        b_rows, b_cols = b.shape
    if mode == "nn":
        (M, K), (K2, N) = a.shape, (b_rows, b_cols)
    elif mode == "nt":
        (M, K), (N, K2) = a.shape, (b_rows, b_cols)
    else:
        (K, M), (K2, N) = a.shape, (b_rows, b_cols)
    assert K == K2, (name, a.shape, b.shape)
    tm = _tile(M, tm)
    tn = _tile(shard_w if (b_shards and mode == "nn") else N // N_DEV if out_shards else N, tn)
    tk = _tile(shard_w if (b_shards and mode == "nt") else K, tk)
    nk = K // tk
    n_ex, n_out = len(extras) + len(vec_extras), len(out_dtypes)
    assert n_vec == 0 or tn == N, (name, tn, N)
    dn = _DN[mode]

    n_tok = 0 if after is None else 1

    def body(a_ref, b_ref, *rest):
        rest = rest[n_tok:]
        ex_refs, out_refs, vec_refs = rest[:n_ex], rest[n_ex:n_ex + n_out], rest[n_ex + n_out:n_ex + n_out + n_vec]
        part = lax.dot_general(a_ref[...].astype(BF16), b_ref[...].astype(BF16), dn, preferred_element_type=F32)
        first_rows = pl.program_id(0) == 0

        def finish(res):
            outs = (res,) if epilogue is None else epilogue(res, *[e[...] for e in ex_refs])
            for o_ref, val in zip(out_refs, outs[:n_out]):
                o_ref[...] = val.astype(o_ref.dtype)
            for v_ref, val in zip(vec_refs, outs[n_out:]):
                @pl.when(first_rows)
                def _(v_ref=v_ref, val=val):
                    v_ref[...] = val

                @pl.when(jnp.logical_not(first_rows))
                def _(v_ref=v_ref, val=val):
                    v_ref[...] += val

        if nk == 1:
            finish(part)
            return
        acc, k = rest[-1], pl.program_id(2)

        @pl.when(k == 0)
        def _():
            acc[...] = part

        @pl.when(k > 0)
        def _():
            acc[...] += part

        @pl.when(k == nk - 1)
        def _():
            finish(acc[...])

    a_spec = pl.BlockSpec((tk, tm), lambda i, j, k: (k, i)) if mode == "tn" else pl.BlockSpec((tm, tk), lambda i, j, k: (i, k))
    if b_shards and mode == "nn":
        per = shard_w // tn
        b_spec = pl.BlockSpec((None, tk, tn), lambda i, j, k: (lax.div(j, per), k, lax.rem(j, per)))
    elif b_shards:
        per = shard_w // tk
        b_spec = pl.BlockSpec((None, tn, tk), lambda i, j, k: (lax.div(k, per), j, lax.rem(k, per)))
    else:
        b_spec = pl.BlockSpec((tn, tk), lambda i, j, k: (j, k)) if mode == "nt" else pl.BlockSpec((tk, tn), lambda i, j, k: (k, j))
    mn_spec = pl.BlockSpec((tm, tn), lambda i, j, k: (i, j))
    vec_spec = pl.BlockSpec((1, tn), lambda i, j, k: (0, j))
    if out_shards:
        assert not extras
        per_o = (N // N_DEV) // tn
        out_spec = pl.BlockSpec((None, tm, tn), lambda i, j, k: (lax.div(j, per_o), i, lax.rem(j, per_o)))
        out_dims = (N_DEV, M, N // N_DEV)
    else:
        out_spec, out_dims = mn_spec, (M, N)
    outs = pl.pallas_call(
        body, name=name, grid=(M // tm, N // tn, nk),
        in_specs=[a_spec, b_spec] + [pl.BlockSpec((8, LANE), lambda i, j, k: (0, 0))] * n_tok
        + [mn_spec] * len(extras) + [vec_spec] * len(vec_extras),
        out_specs=[out_spec] * n_out + [vec_spec] * n_vec,
        out_shape=[jax.ShapeDtypeStruct(out_dims, dt) for dt in out_dtypes] + [jax.ShapeDtypeStruct((1, N), F32)] * n_vec,
        scratch_shapes=[pltpu.VMEM((tm, tn), F32)] if nk > 1 else [],
        compiler_params=_params(("arbitrary" if n_vec else "parallel", "parallel", "arbitrary"), 56),
    )(a, b, *([] if after is None else [after]), *extras, *vec_extras)
    return outs[0] if n_out + n_vec == 1 else outs


def _rms_fwd(x, g, *, name):
    T, D = x.shape
    tr = _tile(T, 256, 8)

    def body(x_ref, g_ref, h_ref):
        xv = x_ref[...]
        r = lax.rsqrt(jnp.mean(xv * xv, axis=-1, keepdims=True) + EPS)
        h_ref[...] = (xv * r * g_ref[...]).astype(h_ref.dtype)

    return pl.pallas_call(
        body, name=name, grid=(T // tr,),
        in_specs=[pl.BlockSpec((tr, D), lambda i: (i, 0)), pl.BlockSpec((1, D), lambda i: (0, 0))],
        out_specs=pl.BlockSpec((tr, D), lambda i: (i, 0)),
        out_shape=jax.ShapeDtypeStruct((T, D), BF16),
        compiler_params=_params(("parallel",)),
    )(x, g)


def _rms_bwd(x, g, dh, dres, *, name):
    T, D = x.shape
    tr = _tile(T, 256, 8)
    epi = _epi_rms_bwd(2)

    def body(x_ref, g_ref, dh_ref, dres_ref, dx_ref, dxb_ref, dg_ref):
        dx, _, dgp = epi(dh_ref[...], x_ref[...], dres_ref[...], g_ref[...])

        @pl.when(pl.program_id(0) == 0)
        def _():
            dg_ref[...] = jnp.zeros_like(dg_ref)

        dg_ref[...] += dgp
        dx_ref[...] = dx
        dxb_ref[...] = dx.astype(dxb_ref.dtype)

    row = pl.BlockSpec((tr, D), lambda i: (i, 0))
    vec = pl.BlockSpec((1, D), lambda i: (0, 0))
    return pl.pallas_call(
        body, name=name, grid=(T // tr,),
        in_specs=[row, vec, row, row], out_specs=[row, row, vec],
        out_shape=[jax.ShapeDtypeStruct((T, D), F32), jax.ShapeDtypeStruct((T, D), BF16), jax.ShapeDtypeStruct((1, D), F32)],
        compiler_params=_params(("arbitrary",)),
    )(x, g, dh, dres)


ROW_TILE = 256


def _epi_residual_rms(acc, res, g):
    xn = acc + res
    r = lax.rsqrt(jnp.mean(xn * xn, axis=-1, keepdims=True) + EPS)
    return xn, xn * r * g


def _epi_rms_bwd(n_copies):
    def epi(dh, x, dres, g):
        r = lax.rsqrt(jnp.mean(x * x, axis=-1, keepdims=True) + EPS)
        xh = x * r
        dxh = dh * g
        dx = dres + r * (dxh - xh * jnp.mean(dxh * xh, axis=-1, keepdims=True))
        return (dx,) * n_copies + (jnp.sum(dh * xh, axis=0, keepdims=True),)
    return epi


def _final_loss(x, g, tgt, *, name):
    T, D = x.shape
    tr = _tile(T, 256, 8)

    def body(x_ref, g_ref, t_ref, dx_ref, dg_ref, loss_ref):
        xv = x_ref[...]
        r = lax.rsqrt(jnp.mean(xv * xv, axis=-1, keepdims=True) + EPS)
        xh = xv * r
        gv = g_ref[...]
        err = xh * gv - t_ref[...]

        @pl.when(pl.program_id(0) == 0)
        def _():
            dg_ref[...] = jnp.zeros_like(dg_ref)
            loss_ref[...] = jnp.zeros_like(loss_ref)

        part = 0.5 * jnp.sum(jnp.mean(err * err, axis=-1, keepdims=True), axis=0, keepdims=True)
        loss_ref[...] += jnp.broadcast_to(part, loss_ref.shape)
        dy = err * (1.0 / D)
        dg_ref[...] += jnp.sum(dy * xh, axis=0, keepdims=True)
        dxh = dy * gv
        dx_ref[...] = r * (dxh - xh * jnp.mean(dxh * xh, axis=-1, keepdims=True))

    row = pl.BlockSpec((tr, D), lambda i: (i, 0))
    vec = pl.BlockSpec((1, D), lambda i: (0, 0))
    return pl.pallas_call(
        body, name=name, grid=(T // tr,),
        in_specs=[row, vec, row], out_specs=[row, vec, pl.BlockSpec((1, LANE), lambda i: (0, 0))],
        out_shape=[jax.ShapeDtypeStruct((T, D), F32), jax.ShapeDtypeStruct((1, D), F32),
                   jax.ShapeDtypeStruct((1, LANE), F32)],
        compiler_params=_params(("arbitrary",)),
    )(x, g, tgt)


def _ple_bwd(dx3, pp, sg, *, name):
    T, D = dx3.shape
    tr = _tile(T, 256, 8)

    def body(dx_ref, pp_ref, sg_ref, dpg_ref, dpp_ref):
        dx, s = dx_ref[...], sg_ref[...]
        dpg_ref[...] = (dx * pp_ref[...] * s * (1.0 - s)).astype(dpg_ref.dtype)
        dpp_ref[...] = (dx * s).astype(dpp_ref.dtype)

    row = pl.BlockSpec((tr, D), lambda i: (i, 0))
    return pl.pallas_call(
        body, name=name, grid=(T // tr,), in_specs=[row, row, row], out_specs=[row, row],
        out_shape=[jax.ShapeDtypeStruct((T, D), BF16)] * 2, compiler_params=_params(("parallel",)),
    )(dx3, pp, sg)


ROWS_QKV_FWD, ROWS_QKV_BWD, ROWS_FFN_FWD, ROWS_FFN_BWD, ROWS_GROUP_A = 512, 256, 256, 128, 256


def _ext(ref, r0, T, before, after, RC):
    parts = []
    if before:
        p0 = pl.multiple_of(jnp.maximum(r0 - 8, 0), 8)
        parts.append(jnp.where(r0 > 0, ref[pl.ds(p0, 8), :], 0.0))
    parts.append(ref[pl.ds(r0, RC), :])
    if after:
        n0 = pl.multiple_of(jnp.minimum(r0 + RC, T - 8), 8)
        parts.append(jnp.where(r0 + RC < T, ref[pl.ds(n0, 8), :], 0.0))
    return parts[0] if len(parts) == 1 else jnp.concatenate(parts, axis=0)


def _down(xx, s):
    return (xx if s == 0 else pltpu.roll(xx, s, 0))[8:, :]


def _up(xx, s, rows):
    return (xx if s == 0 else pltpu.roll(xx, xx.shape[0] - s, 0))[:rows, :]


def _conv_down(xx, w_ref, K):
    y = None
    for j in range(K):
        t = _down(xx, K - 1 - j) * w_ref[j:j + 1, :]
        y = t if y is None else y + t
    return y


def _fold8(x):
    return jnp.sum(x.reshape(x.shape[0] // 8, 8, x.shape[1]), axis=0)


def _silu(x):
    return x * jax.nn.sigmoid(x)


def _dsilu(x):
    s = jax.nn.sigmoid(x)
    return s * (1.0 + x * (1.0 - s))


def _col_specs(T, offs):
    return [pl.BlockSpec((T, LANE), functools.partial(lambda o, j: (0, o + j), o)) for o in offs]


def _group_a_fwd(proj, conv_w, CW, *, name):
    T = proj.shape[0]
    RC = _tile(T, ROWS_GROUP_A, 8)
    nb = CW // LANE
    K = conv_w.shape[0]

    def body(ax_ref, ab_ref, ac_ref, w_ref, y_ref):
        def step(i, carry):
            r0 = pl.multiple_of(i * RC, RC)
            m = _ext(ac_ref, r0, T, True, False, RC) * _ext(ax_ref, r0, T, True, False, RC)
            y_ref[pl.ds(r0, RC), :] = (ab_ref[pl.ds(r0, RC), :] * _conv_down(m, w_ref, K)).astype(y_ref.dtype)
            return carry
        lax.fori_loop(0, T // RC, step, 0)

    return pl.pallas_call(
        body, name=name, grid=(nb,),
        in_specs=_col_specs(T, (0, nb, 2 * nb)) + [pl.BlockSpec((K, LANE), lambda j: (0, j))],
        out_specs=pl.BlockSpec((T, LANE), lambda j: (0, j)),
        out_shape=jax.ShapeDtypeStruct((T, CW), BF16), compiler_params=_params(("parallel",)),
    )(proj, proj, proj, conv_w)


def _group_a_bwd(proj, conv_w, dycat, CW, *, name):
    T = proj.shape[0]
    RC = _tile(T, ROWS_GROUP_A, 8)
    nb = CW // LANE
    K = conv_w.shape[0]

    def body(ax_ref, ab_ref, ac_ref, w_ref, dy_ref, dax_ref, dab_ref, dac_ref, dw_ref):
        def step(i, accs):
            r0 = pl.multiple_of(i * RC, RC)
            ax3 = _ext(ax_ref, r0, T, True, True, RC)
            ac3 = _ext(ac_ref, r0, T, True, True, RC)
            m3 = ax3 * ac3
            c = _conv_down(m3[:RC + 8], w_ref, K)
            dy = dy_ref[pl.ds(r0, RC), :]
            dab_ref[pl.ds(r0, RC), :] = (dy * c).astype(dab_ref.dtype)
            dc2 = _ext(dy_ref, r0, T, False, True, RC) * _ext(ab_ref, r0, T, False, True, RC)
            dm = None
            new = []
            for j in range(K):
                s = K - 1 - j
                t = _up(dc2, s, RC) * w_ref[j:j + 1, :]
                dm = t if dm is None else dm + t
                new.append(accs[j] + _fold8(dc2[:RC] * _down(m3[:RC + 8], s)))
            dax_ref[pl.ds(r0, RC), :] = (dm * ac3[8:RC + 8]).astype(dax_ref.dtype)
            dac_ref[pl.ds(r0, RC), :] = (dm * ax3[8:RC + 8]).astype(dac_ref.dtype)
            return tuple(new)

        accs = lax.fori_loop(0, T // RC, step, tuple(jnp.zeros((8, LANE), F32) for _ in range(K)))
        for j in range(K):
            dw_ref[j:j + 1, :] = jnp.sum(accs[j], axis=0, keepdims=True)

    col = pl.BlockSpec((T, LANE), lambda j: (0, j))
    wsp = pl.BlockSpec((K, LANE), lambda j: (0, j))
    return pl.pallas_call(
        body, name=name, grid=(nb,),
        in_specs=_col_specs(T, (0, nb, 2 * nb)) + [wsp, col],
        out_specs=[col, col, col, wsp],
        out_shape=[jax.ShapeDtypeStruct((T, CW), BF16)] * 3 + [jax.ShapeDtypeStruct((K, CW), F32)],
        compiler_params=_params(("parallel",)),
    )(proj, proj, proj, conv_w, dycat)


def _qkv_fwd(proj, conv_w, off, H, *, name):
    T = proj.shape[0]
    RC = _tile(T, ROWS_QKV_FWD, 8)
    nb = 3 * H
    K = conv_w.shape[0]

    def body(x_ref, w_ref, y_ref):
        j = pl.program_id(0)
        is_qk = j < 2 * H
        scale = jnp.where(j < H, HEAD ** -0.5, 1.0).astype(F32)

        def step(i, carry):
            r0 = pl.multiple_of(i * RC, RC)
            s = _silu(_conv_down(_ext(x_ref, r0, T, True, False, RC), w_ref, K))
            r = lax.rsqrt(jnp.sum(s * s, axis=-1, keepdims=True) + EPS) * scale
            y_ref[pl.ds(r0, RC), :] = s * jnp.where(is_qk, r, 1.0)
            return carry
        lax.fori_loop(0, T // RC, step, 0)

    return pl.pallas_call(
        body, name=name, grid=(nb,),
        in_specs=_col_specs(T, (off,)) + [pl.BlockSpec((K, LANE), lambda j: (0, j))],
        out_specs=pl.BlockSpec((T, LANE), lambda j: (0, j)),
        out_shape=jax.ShapeDtypeStruct((T, nb * LANE), F32), compiler_params=_params(("parallel",)),
    )(proj, conv_w)


def _qkv_bwd(proj, conv_w, dq, dk, dv, off, H, *, name):
    T = proj.shape[0]
    RC = _tile(T, ROWS_QKV_BWD, 8)
    nb = 3 * H
    K = conv_w.shape[0]

    def body(x_ref, w_ref, dq_ref, dk_ref, dv_ref, dx_ref, dw_ref):
        j = pl.program_id(0)
        is_qk = j < 2 * H
        scale = jnp.where(j < H, HEAD ** -0.5, 1.0).astype(F32)

        def step(i, accs):
            r0 = pl.multiple_of(i * RC, RC)
            x3 = _ext(x_ref, r0, T, True, True, RC)
            c2 = _conv_down(x3, w_ref, K)
            s2 = _silu(c2)
            dn2 = jnp.where(j < H, _ext(dq_ref, r0, T, False, True, RC),
                            jnp.where(is_qk, _ext(dk_ref, r0, T, False, True, RC), _ext(dv_ref, r0, T, False, True, RC)))
            r = lax.rsqrt(jnp.sum(s2 * s2, axis=-1, keepdims=True) + EPS)
            nh = s2 * r
            dnp = dn2 * scale
            ds_qk = r * (dnp - nh * jnp.sum(dnp * nh, axis=-1, keepdims=True))
            ds2 = jnp.where(is_qk, ds_qk, dn2)
            dc2 = ds2 * _dsilu(c2)
            dx = None
            new = []
            for jj in range(K):
                s = K - 1 - jj
                t = _up(dc2, s, RC) * w_ref[jj:jj + 1, :]
                dx = t if dx is None else dx + t
                new.append(accs[jj] + _fold8(dc2[:RC] * _down(x3[:RC + 8], s)))
            dx_ref[pl.ds(r0, RC), :] = dx.astype(dx_ref.dtype)
            return tuple(new)

        accs = lax.fori_loop(0, T // RC, step, tuple(jnp.zeros((8, LANE), F32) for _ in range(K)))
        for jj in range(K):
            dw_ref[jj:jj + 1, :] = jnp.sum(accs[jj], axis=0, keepdims=True)

    col = pl.BlockSpec((T, LANE), lambda j: (0, j))
    wsp = pl.BlockSpec((K, LANE), lambda j: (0, j))
    return pl.pallas_call(
        body, name=name, grid=(nb,),
        in_specs=_col_specs(T, (off,)) + [wsp] + [
            pl.BlockSpec((T, LANE), functools.partial(lambda o, j: (0, jnp.clip(j - o, 0, H - 1)), o)) for o in (0, H, 2 * H)],
        out_specs=[col, wsp],
        out_shape=[jax.ShapeDtypeStruct((T, nb * LANE), BF16), jax.ShapeDtypeStruct((K, nb * LANE), F32)],
        compiler_params=_params(("parallel",)),
    )(proj, conv_w, dq, dk, dv)


def _softplus(x):
    return jnp.maximum(x, 0.0) + jnp.log(1.0 + jnp.exp(-jnp.abs(x)))


def _gates_fwd(proj, alog, dtb, off, H, *, name):
    T = proj.shape[0]
    tr = _tile(T, 512, CHUNK)

    def body(ab_ref, al_ref, dt_ref, gb_ref, gam_ref):
        ab = ab_ref[...]
        lane = lax.broadcasted_iota(jnp.int32, ab.shape, 1)
        g = -jnp.exp(al_ref[...]) * _softplus(ab + dt_ref[...])
        gb = jnp.where(lane < H, g, jnp.where(lane < 2 * H, jax.nn.sigmoid(ab), 0.0))
        gb_ref[...] = gb
        tril = _tri().astype(F32)
        for c in range(tr // CHUNK):
            rows = slice(c * CHUNK, (c + 1) * CHUNK)
            gam_ref[rows, :] = _mm(tril, gb[rows, :], precision=lax.Precision.HIGHEST)

    vec = pl.BlockSpec((1, LANE), lambda i: (0, 0))
    row = pl.BlockSpec((tr, LANE), lambda i: (i, 0))
    return pl.pallas_call(
        body, name=name, grid=(T // tr,),
        in_specs=[pl.BlockSpec((tr, LANE), lambda i: (i, off)), vec, vec],
        out_specs=[row, row],
        out_shape=[jax.ShapeDtypeStruct((T, LANE), F32)] * 2, compiler_params=_params(("parallel",)),
    )(proj, alog, dtb)


def _gates_bwd(proj, alog, dtb, dgb, off, H, *, name):
    T = proj.shape[0]
    tr = _tile(T, 512, CHUNK)

    def body(ab_ref, al_ref, dt_ref, d_ref, dab_ref, dal_ref, ddt_ref):
        ab, d = ab_ref[...], d_ref[...]
        lane = lax.broadcasted_iota(jnp.int32, ab.shape, 1)
        is_g = lane < H
        triu = _tri(upper=True).astype(F32)
        dg = jnp.concatenate([_mm(triu, d[c * CHUNK:(c + 1) * CHUNK, :], precision=lax.Precision.HIGHEST)
                              for c in range(tr // CHUNK)], axis=0)
        z = ab + dt_ref[...]
        A = -jnp.exp(al_ref[...])
        da = dg * A * jax.nn.sigmoid(z)
        beta = jax.nn.sigmoid(ab)
        db = d * beta * (1.0 - beta)
        dab_ref[...] = jnp.where(is_g, da, jnp.where(lane < 2 * H, db, 0.0)).astype(dab_ref.dtype)

        @pl.when(pl.program_id(0) == 0)
        def _():
            dal_ref[...] = jnp.zeros_like(dal_ref)
            ddt_ref[...] = jnp.zeros_like(ddt_ref)

        dal_ref[...] += jnp.sum(jnp.where(is_g, dg * A * _softplus(z), 0.0), axis=0, keepdims=True)
        ddt_ref[...] += jnp.sum(jnp.where(is_g, da, 0.0), axis=0, keepdims=True)

    vec = pl.BlockSpec((1, LANE), lambda i: (0, 0))
    row = pl.BlockSpec((tr, LANE), lambda i: (i, 0))
    return pl.pallas_call(
        body, name=name, grid=(T // tr,),
        in_specs=[pl.BlockSpec((tr, LANE), lambda i: (i, off)), vec, vec, row],
        out_specs=[row, vec, vec],
        out_shape=[jax.ShapeDtypeStruct((T, LANE), BF16), jax.ShapeDtypeStruct((1, LANE), F32),
                   jax.ShapeDtypeStruct((1, LANE), F32)],
        compiler_params=_params(("arbitrary",)),
    )(proj, alog, dtb, dgb)


def _gated_norm_fwd(o, proj, gn, zoff, *, name):
    T, W = o.shape
    tr = _tile(T, 512, 8)

    def body(o_ref, z_ref, g_ref, y_ref):
        ov = o_ref[...]
        r = lax.rsqrt(jnp.mean(ov * ov, axis=-1, keepdims=True) + EPS)
        y_ref[...] = (ov * r * g_ref[...] * _silu(z_ref[...])).astype(y_ref.dtype)

    blk = pl.BlockSpec((tr, LANE), lambda i, j: (i, j))
    return pl.pallas_call(
        body, name=name, grid=(T // tr, W // LANE),
        in_specs=[blk, pl.BlockSpec((tr, LANE), lambda i, j: (i, zoff + j)), pl.BlockSpec((1, LANE), lambda i, j: (0, 0))],
        out_specs=blk, out_shape=jax.ShapeDtypeStruct((T, W), BF16), compiler_params=_params(("parallel", "parallel")),
    )(o, proj, gn)


def _gated_norm_bwd(o, proj, gn, dycat, zoff, yoff, *, name):
    T, W = o.shape
    tr = _tile(T, 512, 8)

    def body(o_ref, z_ref, g_ref, dy_ref, do_ref, dz_ref, dg_ref):
        ov, zv, gv, dy = o_ref[...], z_ref[...], g_ref[...], dy_ref[...]
        r = lax.rsqrt(jnp.mean(ov * ov, axis=-1, keepdims=True) + EPS)
        nh = ov * r
        s = _silu(zv)

        @pl.when((pl.program_id(0) == 0) & (pl.program_id(1) == 0))
        def _():
            dg_ref[...] = jnp.zeros_like(dg_ref)

        dg_ref[...] += jnp.sum(dy * nh * s, axis=0, keepdims=True)
        dz_ref[...] = (dy * nh * gv * _dsilu(zv)).astype(dz_ref.dtype)
        dn = dy * gv * s
        do_ref[...] = r * (dn - nh * jnp.mean(dn * nh, axis=-1, keepdims=True))

    blk = pl.BlockSpec((tr, LANE), lambda i, j: (i, j))
    vec = pl.BlockSpec((1, LANE), lambda i, j: (0, 0))
    return pl.pallas_call(
        body, name=name, grid=(T // tr, W // LANE),
        in_specs=[blk, pl.BlockSpec((tr, LANE), lambda i, j: (i, zoff + j)), vec,
                  pl.BlockSpec((tr, LANE), lambda i, j: (i, yoff + j))],
        out_specs=[blk, blk, vec],
        out_shape=[jax.ShapeDtypeStruct((T, W), F32), jax.ShapeDtypeStruct((T, W), BF16),
                   jax.ShapeDtypeStruct((1, LANE), F32)],
        compiler_params=_params(("arbitrary", "arbitrary")),
    )(o, proj, gn, dycat)


def _ffn_act_fwd(up_pre, conv_w, *, name):
    T, F2 = up_pre.shape
    RC = _tile(T, ROWS_FFN_FWD, 8)
    nb = F2 // 2 // LANE
    K = conv_w.shape[0]

    def body(g_ref, v_ref, wg_ref, wv_ref, y_ref):
        def step(i, carry):
            r0 = pl.multiple_of(i * RC, RC)
            gate = _conv_down(_ext(g_ref, r0, T, True, False, RC), wg_ref, K)
            val = _conv_down(_ext(v_ref, r0, T, True, False, RC), wv_ref, K)
            y_ref[pl.ds(r0, RC), :] = (_silu(gate) * val).astype(y_ref.dtype)
            return carry
        lax.fori_loop(0, T // RC, step, 0)

    return pl.pallas_call(
        body, name=name, grid=(nb,),
        in_specs=_col_specs(T, (0, nb)) + [pl.BlockSpec((K, LANE), lambda j: (0, j)),
                                           pl.BlockSpec((K, LANE), lambda j: (0, nb + j))],
        out_specs=pl.BlockSpec((T, LANE), lambda j: (0, j)),
        out_shape=jax.ShapeDtypeStruct((T, F2 // 2), BF16), compiler_params=_params(("parallel",)),
    )(up_pre, up_pre, conv_w, conv_w)


def _ffn_act_bwd(up_pre, conv_w, dact, *, name):
    T, F2 = up_pre.shape
    RC = _tile(T, ROWS_FFN_BWD, 8)
    nb = F2 // 2 // LANE
    K = conv_w.shape[0]

    def body(g_ref, v_ref, wg_ref, wv_ref, da_ref, dg_ref, dv_ref, dwg_ref, dwv_ref):
        def step(i, accs):
            r0 = pl.multiple_of(i * RC, RC)
            g3 = _ext(g_ref, r0, T, True, True, RC)
            v3 = _ext(v_ref, r0, T, True, True, RC)
            gate2 = _conv_down(g3, wg_ref, K)
            val2 = _conv_down(v3, wv_ref, K)
            da2 = _ext(da_ref, r0, T, False, True, RC)
            dgate2 = da2 * val2 * _dsilu(gate2)
            dval2 = da2 * _silu(gate2)
            dgp, dvp, new = None, None, []
            for j in range(K):
                s = K - 1 - j
                tg = _up(dgate2, s, RC) * wg_ref[j:j + 1, :]
                tv = _up(dval2, s, RC) * wv_ref[j:j + 1, :]
                dgp = tg if dgp is None else dgp + tg
                dvp = tv if dvp is None else dvp + tv
                new.append(accs[2 * j] + _fold8(dgate2[:RC] * _down(g3[:RC + 8], s)))
                new.append(accs[2 * j + 1] + _fold8(dval2[:RC] * _down(v3[:RC + 8], s)))
            dg_ref[pl.ds(r0, RC), :] = dgp.astype(dg_ref.dtype)
            dv_ref[pl.ds(r0, RC), :] = dvp.astype(dv_ref.dtype)
            return tuple(new)

        accs = lax.fori_loop(0, T // RC, step, tuple(jnp.zeros((8, LANE), F32) for _ in range(2 * K)))
        for j in range(K):
            dwg_ref[j:j + 1, :] = jnp.sum(accs[2 * j], axis=0, keepdims=True)
            dwv_ref[j:j + 1, :] = jnp.sum(accs[2 * j + 1], axis=0, keepdims=True)

    col = pl.BlockSpec((T, LANE), lambda j: (0, j))
    wsp = pl.BlockSpec((K, LANE), lambda j: (0, j))
    return pl.pallas_call(
        body, name=name, grid=(nb,),
        in_specs=_col_specs(T, (0, nb)) + [wsp, pl.BlockSpec((K, LANE), lambda j: (0, nb + j)), col],
        out_specs=[col, col, wsp, wsp],
        out_shape=[jax.ShapeDtypeStruct((T, F2 // 2), BF16)] * 2 + [jax.ShapeDtypeStruct((K, F2 // 2), F32)] * 2,
        compiler_params=_params(("parallel",)),
    )(up_pre, up_pre, conv_w, conv_w, dact)


CPB = 8
CPB_SCAN = 4
GRP = 8
HP = lax.Precision.HIGH


def _tri(strict=False, upper=False):
    r = lax.broadcasted_iota(jnp.int32, (CHUNK, CHUNK), 0)
    c = lax.broadcasted_iota(jnp.int32, (CHUNK, CHUNK), 1)
    if upper:
        return c >= r
    return (r > c) if strict else (r >= c)


def _mm(a, b, dn="nn", precision=None):
    precision = HP if precision is None else precision
    return lax.dot_general(a, b, _DN[dn], precision=precision, preferred_element_type=F32)


def _mm16(a, b, dn="nn"):
    return lax.dot_general(a.astype(BF16), b.astype(BF16), _DN[dn], preferred_element_type=F32)


def _each(f, *cols):
    return [f(*xs) for xs in zip(*cols)]


def _decay(gam):
    return jnp.exp(jnp.where(_tri(), gam[:, :CHUNK] - gam.T[:CHUNK, :], -1e30))


def _delta_specs(T, H, cpb):
    rows = cpb * CHUNK
    col = lambda o: pl.BlockSpec((rows, LANE), functools.partial(lambda o, h, n: (n, o + h), o))
    bc = pl.BlockSpec((1, rows, LANE), lambda h, n: (h, n, 0))
    sq = pl.BlockSpec((1, cpb, CHUNK, CHUNK), lambda h, n: (h, n, 0, 0))
    vec = pl.BlockSpec((1, cpb, 1, LANE), lambda h, n: (h, n, 0, 0))
    return col, bc, sq, vec


def _delta_prep_fwd(qkv, gamB, bB, H, *, name):
    T = qkv.shape[0]
    N = T // CHUNK
    cpb = _tile(N, CPB, 8)
    grp = min(GRP, cpb)
    col, bc, sq, vec = _delta_specs(T, H, cpb)

    def body(q_ref, k_ref, v_ref, g_ref, b_ref, u_ref, w_ref, qd_ref, kd_ref, qk_ref, ti_ref, gl_ref):
        eye = (lax.broadcasted_iota(jnp.int32, (CHUNK, CHUNK), 0) == lax.broadcasted_iota(jnp.int32, (CHUNK, CHUNK), 1)).astype(F32)
        strict = _tri(strict=True)
        for c0 in range(0, cpb, grp):
            cs = list(range(c0, c0 + grp))
            rows = [slice(c * CHUNK, (c + 1) * CHUNK) for c in cs]
            q, k, v = ([r_[r, :] for r in rows] for r_ in (q_ref, k_ref, v_ref))
            bb = [b_ref[0, r, :] for r in rows]
            gam = [g_ref[0, r, :] for r in rows]
            D = _each(_decay, gam)
            e = _each(jnp.exp, gam)
            kk = _each(lambda k_: _mm(k_, k_, "nt"), k)
            X = _each(lambda kk_, D_, b_: -(jnp.where(strict, kk_ * D_, 0.0) * b_[:, :CHUNK]), kk, D, bb)
            R = _each(lambda x: eye + x, X)
            for _ in range(5):
                X = _each(lambda x: _mm(x, x), X)
                R = _each(lambda r, x: r + _mm(r, x), R, X)
            u = _each(lambda r, b_, v_: _mm(r, b_ * v_), R, bb, v)
            w = _each(lambda r, b_, e_, k_: _mm(r, b_ * e_ * k_), R, bb, e, k)
            qk = _each(lambda q_, k_, D_: _mm(q_, k_, "nt") * D_, q, k, D)
            for i, c in enumerate(cs):
                glast = gam[i][CHUNK - 1:CHUNK, :]
                u_ref[rows[i], :] = u[i]
                w_ref[rows[i], :] = w[i]
                qd_ref[rows[i], :] = e[i] * q[i]
                kd_ref[rows[i], :] = jnp.exp(glast - gam[i]) * k[i]
                qk_ref[0, c] = qk[i]
                ti_ref[0, c] = R[i]
                gl_ref[0, c] = jnp.exp(glast)

    full = jax.ShapeDtypeStruct((T, H * LANE), F32)
    sqs = jax.ShapeDtypeStruct((H, N, CHUNK, CHUNK), F32)
    return pl.pallas_call(
        body, name=name, grid=(H, N // cpb),
        in_specs=[col(0), col(H), col(2 * H), bc, bc],
        out_specs=[col(0)] * 4 + [sq, sq, vec],
        out_shape=[full] * 4 + [sqs, sqs, jax.ShapeDtypeStruct((H, N, 1, LANE), F32)],
        compiler_params=_params(("parallel", "parallel")),
    )(qkv, qkv, qkv, gamB, bB)


def _scan_specs(H, N, cpb, hb, rev):
    nbk = N // cpb
    blk = (lambda n: nbk - 1 - n) if rev else (lambda n: n)
    col = pl.BlockSpec((cpb * CHUNK, hb * LANE), lambda h, n: (blk(n), h))
    sq = pl.BlockSpec((hb, cpb, CHUNK, CHUNK), lambda h, n: (h, blk(n), 0, 0))
    vec = pl.BlockSpec((hb, cpb, 1, LANE), lambda h, n: (h, blk(n), 0, 0))
    st = pl.BlockSpec((hb, cpb, HEAD, HEAD), lambda h, n: (h, blk(n), 0, 0))
    return col, sq, vec, st


def _delta_scan_fwd(u, w, qd, kd, qk, gl, H, *, name):
    T = u.shape[0]
    N = T // CHUNK
    cpb = _tile(N, CPB_SCAN, 4)
    hb = min(GRP, H)
    col, sq, vec, st = _scan_specs(H, N, cpb, hb, False)
    lanes = [slice(j * LANE, (j + 1) * LANE) for j in range(hb)]
    heads = list(range(hb))

    def body(u_ref, w_ref, qd_ref, kd_ref, qk_ref, gl_ref, o_ref, vn_ref, ss_ref, s_scr):
        @pl.when(pl.program_id(1) == 0)
        def _():
            s_scr[...] = jnp.zeros_like(s_scr)

        def step(c, states):
            rows = pl.ds(pl.multiple_of(c * CHUNK, CHUNK), CHUNK)
            S = list(states)
            for j in heads:
                ss_ref[j, c] = S[j]
            wS = _each(lambda ln, s: _mm16(w_ref[rows, ln], s), lanes, S)
            qS = _each(lambda ln, s: _mm16(qd_ref[rows, ln], s), lanes, S)
            vn = _each(lambda ln, ws: u_ref[rows, ln] - ws, lanes, wS)
            o = _each(lambda j, qs, vn_: qs + _mm16(qk_ref[j, c], vn_), heads, qS, vn)
            new = _each(lambda j, ln, s, vn_: s * gl_ref[j, c] + _mm16(kd_ref[rows, ln], vn_, "tn"),
                        heads, lanes, S, vn)
            for j in heads:
                o_ref[rows, lanes[j]] = o[j]
                vn_ref[rows, lanes[j]] = vn[j]
            return tuple(new)
        out = lax.fori_loop(0, cpb, step, tuple(s_scr[j] for j in heads))
        for j in heads:
            s_scr[j] = out[j]

    full = jax.ShapeDtypeStruct((T, H * LANE), F32)
    return pl.pallas_call(
        body, name=name, grid=(H // hb, N // cpb),
        in_specs=[col] * 4 + [sq, vec],
        out_specs=[col, col, st],
        out_shape=[full, full, jax.ShapeDtypeStruct((H, N, HEAD, HEAD), F32)],
        scratch_shapes=[pltpu.VMEM((hb, HEAD, HEAD), F32)],
        compiler_params=_params(("parallel", "arbitrary")),
    )(u, w, qd, kd, qk, gl)


def _delta_scan_bwd(do, w, qd, kd, vn, qk, gl, ss, H, *, name):
    T = do.shape[0]
    N = T // CHUNK
    cpb = _tile(N, CPB_SCAN, 4)
    hb = min(GRP, H)
    col, sq, vec, st = _scan_specs(H, N, cpb, hb, True)
    lanes = [slice(j * LANE, (j + 1) * LANE) for j in range(hb)]
    heads = list(range(hb))

    def body(do_ref, w_ref, qd_ref, kd_ref, vn_ref, qk_ref, gl_ref, ss_ref,
             du_ref, dw_ref, dqd_ref, dkd_ref, dqk_ref, dgl_ref, ds_scr):
        @pl.when(pl.program_id(1) == 0)
        def _():
            ds_scr[...] = jnp.zeros_like(ds_scr)

        def step(i, dstates):
            c = cpb - 1 - i
            rows = pl.ds(pl.multiple_of(c * CHUNK, CHUNK), CHUNK)
            dS = list(dstates)
            S = [ss_ref[j, c] for j in heads]
            dov = [do_ref[rows, ln] for ln in lanes]
            vnv = [vn_ref[rows, ln] for ln in lanes]
            a1 = _each(lambda j, d_: _mm16(qk_ref[j, c], d_, "tn"), heads, dov)
            a2 = _each(lambda ln, ds: _mm16(kd_ref[rows, ln], ds), lanes, dS)
            dvn = _each(lambda x, y: x + y, a1, a2)
            dqd = _each(lambda d_, s: _mm16(d_, s, "nt"), dov, S)
            dkd = _each(lambda v_, ds: _mm16(v_, ds, "nt"), vnv, dS)
            dqk = _each(lambda d_, v_: _mm16(d_, v_, "nt"), dov, vnv)
            dw = _each(lambda dv_, s: -_mm16(dv_, s, "nt"), dvn, S)
            b1 = _each(lambda ln, d_: _mm16(qd_ref[rows, ln], d_, "tn"), lanes, dov)
            b2 = _each(lambda ln, dv_: _mm16(w_ref[rows, ln], dv_, "tn"), lanes, dvn)
            new = _each(lambda j, x, y, ds: x + ds * gl_ref[j, c] - y, heads, b1, b2, dS)
            for j in heads:
                du_ref[rows, lanes[j]] = dvn[j]
                dw_ref[rows, lanes[j]] = dw[j]
                dqd_ref[rows, lanes[j]] = dqd[j]
                dkd_ref[rows, lanes[j]] = dkd[j]
                dqk_ref[j, c] = dqk[j]
                dgl = jnp.sum(jnp.sum(dS[j] * S[j], axis=1, keepdims=True), axis=0, keepdims=True)
                dgl_ref[j, c] = jnp.broadcast_to(dgl, (1, LANE))
            return tuple(new)
        out = lax.fori_loop(0, cpb, step, tuple(ds_scr[j] for j in heads))
        for j in heads:
            ds_scr[j] = out[j]

    full = jax.ShapeDtypeStruct((T, H * LANE), F32)
    return pl.pallas_call(
        body, name=name, grid=(H // hb, N // cpb),
        in_specs=[col] * 5 + [sq, vec, st],
        out_specs=[col] * 4 + [sq, vec],
        out_shape=[full] * 4 + [jax.ShapeDtypeStruct((H, N, CHUNK, CHUNK), F32), jax.ShapeDtypeStruct((H, N, 1, LANE), F32)],
        scratch_shapes=[pltpu.VMEM((hb, HEAD, HEAD), F32)],
        compiler_params=_params(("parallel", "arbitrary")),
    )(do, w, qd, kd, vn, qk, gl, ss)


def _delta_prep_bwd(qkv, gamB, bB, ti, u, w, qk, du, dw, dqd, dkd, dqk, dgl, H, *, name):
    T = qkv.shape[0]
    N = T // CHUNK
    cpb = _tile(N, CPB, 8)
    grp = min(GRP, cpb)
    col, bc, sq, vec = _delta_specs(T, H, cpb)

    def body(q_ref, k_ref, v_ref, g_ref, b_ref, ti_ref, u_ref, w_ref, qk_ref,
             du_ref, dw_ref, dqd_ref, dkd_ref, dqk_ref, dgl_ref,
             dq_ref, dk_ref, dv_ref, dg_ref, db_ref):
        ones = jnp.ones((CHUNK, LANE), F32)
        strict = _tri(strict=True)
        last = lax.broadcasted_iota(jnp.int32, (CHUNK, LANE), 0) == CHUNK - 1
        lsum = lambda x: jnp.sum(x, axis=-1, keepdims=True)
        for c0 in range(0, cpb, grp):
            cs = list(range(c0, c0 + grp))
            rows = [slice(c * CHUNK, (c + 1) * CHUNK) for c in cs]
            ld = lambda r_: [r_[r, :] for r in rows]
            q, k, v, uv, wv, duv, dwv, dqd_v, dkd_v = (ld(r_) for r_ in (q_ref, k_ref, v_ref, u_ref, w_ref, du_ref, dw_ref, dqd_ref, dkd_ref))
            bb = [b_ref[0, r, :] for r in rows]
            gam = [g_ref[0, r, :] for r in rows]
            Ti = [ti_ref[0, c] for c in cs]
            QK = [qk_ref[0, c] for c in cs]
            dqk_v = [dqk_ref[0, c] for c in cs]
            D = _each(_decay, gam)
            e = _each(jnp.exp, gam)
            glast = [g_[CHUNK - 1:CHUNK, :] for g_ in gam]
            eL = _each(lambda gl_, g_: jnp.exp(gl_ - g_), glast, gam)
            kk = _each(lambda k_: _mm(k_, k_, "nt"), k)
            KKD = _each(lambda kk_, D_: jnp.where(strict, kk_ * D_, 0.0), kk, D)
            dru = _each(lambda t, d_: _mm(t, d_, "tn"), Ti, duv)
            drw = _each(lambda t, d_: _mm(t, d_, "tn"), Ti, dwv)
            l1 = _each(lambda a, b: _mm(a, b, "nt"), dru, uv)
            l2 = _each(lambda a, b: _mm(a, b, "nt"), drw, wv)
            dL = _each(lambda a, b: jnp.where(strict, -(a + b), 0.0), l1, l2)
            Mm = _each(lambda dl, b_: dl * b_[:, :CHUNK], dL, bb)
            dKK = _each(lambda m_, D_: m_ * D_, Mm, D)
            dQK = _each(lambda a, D_: a * D_, dqk_v, D)
            P = _each(lambda m_, kkd, a, qk_: m_ * kkd + a * qk_, Mm, KKD, dqk_v, QK)
            q1 = _each(lambda a, k_: _mm(a, k_), dQK, k)
            k1 = _each(lambda a, q_: _mm(a, q_, "tn"), dQK, q)
            k2 = _each(lambda a, k_: _mm(a, k_), dKK, k)
            k3 = _each(lambda a, k_: _mm(a, k_, "tn"), dKK, k)
            s1 = _each(lambda dl, kkd: _mm(dl * kkd, ones), dL, KKD)
            p1 = _each(lambda p_: _mm(p_, ones), P)
            p2 = _each(lambda p_: _mm(p_, ones, "tn"), P)
            for i, c in enumerate(cs):
                r = rows[i]
                bek = bb[i] * e[i]
                kdv = eL[i] * k[i]
                dq_ref[r, :] = q1[i] + e[i] * dqd_v[i]
                dk_ref[r, :] = k1[i] + k2[i] + k3[i] + bek * drw[i] + eL[i] * dkd_v[i]
                dv_ref[r, :] = bb[i] * dru[i]
                db_ref[0, r, :] = s1[i] + lsum(dru[i] * v[i]) + lsum(drw[i] * e[i] * k[i])
                dgam = (p1[i] - p2[i] + lsum(drw[i] * bek * k[i]) + lsum(dqd_v[i] * e[i] * q[i])
                        - lsum(dkd_v[i] * kdv))
                xlast = jnp.sum(lsum(dkd_v[i] * kdv), axis=0, keepdims=True) + jnp.exp(glast[i]) * dgl_ref[0, c]
                dg_ref[0, r, :] = dgam + jnp.where(last, xlast, 0.0)

    full = jax.ShapeDtypeStruct((T, H * LANE), F32)
    bcs = jax.ShapeDtypeStruct((H, T, LANE), F32)
    return pl.pallas_call(
        body, name=name, grid=(H, N // cpb),
        in_specs=[col(0), col(H), col(2 * H), bc, bc, sq, col(0), col(0), sq, col(0), col(0), col(0), col(0), sq, vec],
        out_specs=[col(0), col(0), col(0), bc, bc],
        out_shape=[full, full, full, bcs, bcs],
        compiler_params=_params(("parallel", "parallel")),
    )(qkv, qkv, qkv, gamB, bB, ti, u, w, qk, du, dw, dqd, dkd, dqk, dgl)


def _adam(parts, w, m, v, *, name, own=None, me=None):
    P, R, C = parts.shape
    tr = _tile(R, 256, 8)
    n_own = 0 if own is None else 2

    def body(*refs):
        p_ref, w_ref, m_ref, v_ref, g_ref, d_ref, nm_ref, nv_ref = refs[n_own:]
        g = None
        for i in range(P):
            t = p_ref[i].astype(F32)
            if n_own:
                t = jnp.where(refs[0][0] == i, refs[1][...].astype(F32), t)
            g = t if g is None else g + t
        mn = ADAM_B1 * m_ref[...] + (1.0 - ADAM_B1) * g
        vn = ADAM_B2 * v_ref[...] + (1.0 - ADAM_B2) * (g * g)
        m_hat = mn / (1.0 - ADAM_B1 ** ADAM_STEP)
        v_hat = vn / (1.0 - ADAM_B2 ** ADAM_STEP)
        g_ref[...] = g
        d_ref[...] = -ADAM_LR * (m_hat / (jnp.sqrt(v_hat) + ADAM_EPS) + ADAM_WD * w_ref[...])
        nm_ref[...] = mn
        nv_ref[...] = vn

    blk = pl.BlockSpec((tr, C), lambda i: (i, 0))
    return pl.pallas_call(
        body, name=name, grid=(R // tr,),
        in_specs=[pl.BlockSpec(memory_space=pltpu.SMEM), blk][:n_own] + [pl.BlockSpec((P, tr, C), lambda i: (0, i, 0)), blk, blk, blk],
        out_specs=[blk] * 4, out_shape=[jax.ShapeDtypeStruct((R, C), F32)] * 4,
        compiler_params=_params(("parallel",)),
    )(*([me, own] if n_own else []), parts, w, m, v)


def _mesh_pos():
    return lax.axis_index("x"), lax.axis_index("y"), lax.axis_index("c")


def _peer(k):
    x, y, c = _mesh_pos()
    px, py, pc = x ^ ((k >> 2) & 1), y ^ ((k >> 1) & 1), c ^ (k & 1)
    return (px, py, pc), 4 * px + 2 * py + pc


def _exchange(arrays, scatter, *, name):
    n = len(arrays)
    blocks = [a.shape[1:] if scatter else a.shape for a in arrays]

    def body(*refs):
        srcs, dsts = refs[:n], refs[n:2 * n]
        send_sems, recv_sems, local_sems = refs[2 * n:]
        x, y, c = _mesh_pos()
        me = 4 * x + 2 * y + c
        local, sends = [], []
        for a in range(n):
            cp = pltpu.make_async_copy(srcs[a].at[me] if scatter else srcs[a], dsts[a].at[me], local_sems.at[a])
            cp.start()
            local.append(cp)
            for k in range(1, N_DEV):
                dev, idx = _peer(k)
                cp = pltpu.make_async_remote_copy(
                    src_ref=srcs[a].at[idx] if scatter else srcs[a], dst_ref=dsts[a].at[me],
                    send_sem=send_sems.at[a * N_DEV + k], recv_sem=recv_sems.at[a * N_DEV + k],
                    device_id=dev, device_id_type=MESH)
                cp.start()
                sends.append(cp)
        for a in range(n):
            for k in range(1, N_DEV):
                dev, idx = _peer(k)
                pltpu.make_async_remote_copy(
                    src_ref=srcs[a].at[idx] if scatter else srcs[a], dst_ref=dsts[a].at[idx],
                    send_sem=send_sems.at[a * N_DEV + k], recv_sem=recv_sems.at[a * N_DEV + k],
                    device_id=dev, device_id_type=MESH).wait_recv()
        for cp in sends:
            cp.wait_send()
        for cp in local:
            cp.wait()

    anyspec = pl.BlockSpec(memory_space=pl.ANY)
    return pl.pallas_call(
        body, name=name, in_specs=[anyspec] * n, out_specs=[anyspec] * n,
        out_shape=[jax.ShapeDtypeStruct((N_DEV,) + tuple(b), a.dtype) for a, b in zip(arrays, blocks)],
        scratch_shapes=[pltpu.SemaphoreType.DMA((n * N_DEV,)), pltpu.SemaphoreType.DMA((n * N_DEV,)),
                        pltpu.SemaphoreType.DMA((n,))],
    )(*arrays)


_ANY = pl.BlockSpec(memory_space=pl.ANY)
_SEM = pl.BlockSpec(memory_space=pltpu.SEMAPHORE)
_EFFECT = pltpu.SideEffectType.DATAFLOW_SIDE_EFFECTING


def _in_hbm(a):
    return pltpu.with_memory_space_constraint(a, pltpu.HBM)


def _split_copy(src, land, send, recv, k, me, scatter, landed):
    dev, idx = _peer(k)
    return pltpu.make_async_remote_copy(
        src_ref=src.at[idx] if scatter else src, dst_ref=land.at[idx if landed else me],
        send_sem=send.at[k], recv_sem=recv.at[k], device_id=dev, device_id_type=MESH)


ALL_PEERS = tuple(range(1, N_DEV))
SIBLING = 1
SAME_CORE = (2, 4, 6)


def _split_start(srcs, lands, scatter, *, name, relations=None):
    n = len(srcs)
    relations = relations or [ALL_PEERS] * n

    def body(*refs):
        src, land, send, recv, token = refs[:n], refs[n:2 * n], refs[2 * n:3 * n], refs[3 * n:4 * n], refs[-1]
        x, y, c = _mesh_pos()
        me = 4 * x + 2 * y + c
        for a in range(n):
            for k in relations[a]:
                _split_copy(src[a], land[a], send[a], recv[a], k, me, scatter, False).start()
        token[...] = jnp.zeros_like(token)

    outs = pl.pallas_call(
        body, name=name,
        out_shape=[pltpu.SemaphoreType.DMA((N_DEV,))] * (2 * n) + [pltpu.HBM(t.shape, t.dtype) for t in list(srcs) + list(lands)]
        + [jax.ShapeDtypeStruct((8, LANE), F32)],
        in_specs=[_ANY] * (2 * n), out_specs=[_SEM] * (2 * n) + [_ANY] * (2 * n) + [pl.BlockSpec(memory_space=pltpu.VMEM)],
        input_output_aliases={i: 2 * n + i for i in range(2 * n)},
        compiler_params=pltpu.CompilerParams(has_side_effects=_EFFECT),
    )(*[_in_hbm(t) for t in list(srcs) + list(lands)])
    handles = [(outs[a], outs[n + a], outs[2 * n + a], outs[3 * n + a]) for a in range(n)]
    return handles, outs[-1]


def _split_wait(handle, after, scatter, *, name):
    send, recv, src_thru, land_thru = handle

    def body(src_ref, land_ref, send_ref, recv_ref, after_ref, src_out, land_out):
        x, y, c = _mesh_pos()
        me = 4 * x + 2 * y + c
        for k in range(1, N_DEV):
            cp = _split_copy(src_ref, land_ref, send_ref, recv_ref, k, me, scatter, True)
            cp.wait_send()
            cp.wait_recv()

    return pl.pallas_call(
        body, name=name,
        out_shape=(pltpu.HBM(src_thru.shape, src_thru.dtype), pltpu.HBM(land_thru.shape, land_thru.dtype)),
        in_specs=(_ANY, _ANY, _SEM, _SEM, _ANY), out_specs=(_ANY, _ANY), input_output_aliases={0: 0, 1: 1},
        compiler_params=pltpu.CompilerParams(has_side_effects=_EFFECT),
    )(src_thru, land_thru, send, recv, after)[1]


def _forward_copy(land, fsend, frecv, k, landed):
    x, y, c = _mesh_pos()
    _, idx = _peer(k | SIBLING if landed else k)
    return pltpu.make_async_remote_copy(src_ref=land.at[idx], dst_ref=land.at[idx], send_sem=fsend.at[k],
                                        recv_sem=frecv.at[k], device_id=(x, y, 1 - c), device_id_type=MESH)


def _gather_forward(handle, after, *, name):
    send, recv, src_thru, land_thru = handle

    def body(src_ref, land_ref, send_ref, recv_ref, after_ref, src_out, land_out, fsend, frecv):
        x, y, c = _mesh_pos()
        me = 4 * x + 2 * y + c
        for k in SAME_CORE:
            _split_copy(src_ref, land_ref, send_ref, recv_ref, k, me, False, True).wait_recv()
            _forward_copy(land_ref, fsend, frecv, k, False).start()

    src2, land2, fsend, frecv = pl.pallas_call(
        body, name=name,
        out_shape=(pltpu.HBM(src_thru.shape, src_thru.dtype), pltpu.HBM(land_thru.shape, land_thru.dtype),
                   pltpu.SemaphoreType.DMA((N_DEV,)), pltpu.SemaphoreType.DMA((N_DEV,))),
        in_specs=(_ANY, _ANY, _SEM, _SEM, _ANY), out_specs=(_ANY, _ANY, _SEM, _SEM), input_output_aliases={0: 0, 1: 1},
        compiler_params=pltpu.CompilerParams(has_side_effects=_EFFECT),
    )(src_thru, land_thru, send, recv, after)
    return (send, recv, src2, land2), (fsend, frecv)


def _gather_wait_two_level(handle, fwd, *, name):
    send, recv, src_thru, land_thru = handle
    fsend, frecv = fwd

    def body(src_ref, land_ref, send_ref, recv_ref, fsend_ref, frecv_ref, src_out, land_out):
        x, y, c = _mesh_pos()
        me = 4 * x + 2 * y + c
        for k in (SIBLING,) + SAME_CORE:
            _split_copy(src_ref, land_ref, send_ref, recv_ref, k, me, False, True).wait_send()
        _split_copy(src_ref, land_ref, send_ref, recv_ref, SIBLING, me, False, True).wait_recv()
        for k in SAME_CORE:
            _forward_copy(land_ref, fsend_ref, frecv_ref, k, False).wait_send()
            _forward_copy(land_ref, fsend_ref, frecv_ref, k, True).wait_recv()

    return pl.pallas_call(
        body, name=name,
        out_shape=(pltpu.HBM(src_thru.shape, src_thru.dtype), pltpu.HBM(land_thru.shape, land_thru.dtype)),
        in_specs=(_ANY, _ANY, _SEM, _SEM, _SEM, _SEM), out_specs=(_ANY, _ANY), input_output_aliases={0: 0, 1: 1},
        compiler_params=pltpu.CompilerParams(has_side_effects=_EFFECT),
    )(src_thru, land_thru, send, recv, fsend, frecv)[1]


def _local_step(x, p, tgt, S, wt, conv, emit):
    T, D = x.shape
    CW = DNW = D // 2
    H = DNW // HEAD
    nA, nD = CW // LANE, DNW // LANE
    qkv_off, z_off, ab_off = 3 * nA, 3 * nA + 3 * nD, 3 * nA + 4 * nD
    alog = jnp.pad(S["a_log"], ((0, 0), (0, LANE - H)))
    dtb = jnp.pad(S["dt_bias"], ((0, 0), (0, LANE - H)))

    h1 = _rms_fwd(x, S["g_mix"], name="rms1_fwd")
    w_in, cv = wt("w_in", h1), conv(h1)
    proj = _matmul(h1, w_in, "nn", name="mm_in")
    y_a = _group_a_fwd(proj, cv["conv_a"], CW, name="group_a_fwd")
    qkv = _qkv_fwd(proj, cv["conv_qkv"], qkv_off, H, name="qkv_fwd")
    gb, gamc = _gates_fwd(proj, alog, dtb, ab_off, H, name="gates_fwd")
    bcast = lambda cols: jnp.broadcast_to(cols.T[:, :, None], (H, T, LANE))
    gamB, bB = bcast(gamc[:, :H]), bcast(gb[:, H:2 * H])
    u, w, qd, kd, qk, ti, gl = _delta_prep_fwd(qkv, gamB, bB, H, name="delta_prep_fwd")
    o, vn, ss = _delta_scan_fwd(u, w, qd, kd, qk, gl, H, name="delta_scan_fwd")
    y_b = _gated_norm_fwd(o, proj, S["dn_g"], z_off, name="gated_norm_fwd")
    ycat = jnp.concatenate([y_a, y_b], axis=1)
    w_out = wt("w_out", ycat)
    rows = dict(tm=ROW_TILE, tn=D)
    x1, h2 = _matmul(ycat, w_out, "nn", name="mm_out", out_dtypes=(F32, BF16), epilogue=_epi_residual_rms,
                     extras=(x,), vec_extras=(S["g_ffn"],), **rows)
    w_up = wt("w_up", h2)
    up_pre = _matmul(h2, w_up, "nn", name="mm_up", b_shards=True, tn=SHARD_TILE)
    act = _ffn_act_fwd(up_pre, cv["conv_ffn"], name="ffn_act_fwd")
    w_down = wt("w_down", act)
    x2 = _matmul(act, w_down, "nn", name="mm_down", epilogue=lambda acc, r: (acc + r,), extras=(x1,))
    h3 = _rms_fwd(x2, S["g_ple"], name="rms3_fwd")
    w_pp, w_pg = wt("w_pp", h3), wt("w_pg", h3)
    pp = _matmul(p, w_pp, "nn", name="mm_pp", b_shards=True)

    def ple_epi(acc, x2r, ppr):
        s = jax.nn.sigmoid(acc)
        return x2r + s * ppr, s

    x3, sg = _matmul(h3, w_pg, "nn", name="mm_pg", out_dtypes=(F32, F32), epilogue=ple_epi, extras=(x2, pp), tm=512)
    dx3, dg_final, loss = _final_loss(x3, S["g_final"], tgt, name="final_loss")

    G = {"g_final": dg_final}
    dpg, dpp = _ple_bwd(dx3, pp, sg, name="ple_bwd")
    tok = emit({"w_pp": _matmul(p, dpp, "tn", name="mm_dwpp", out_dtypes=(BF16,), out_shards=True),
                "w_pg": _matmul(h3, dpg, "tn", name="mm_dwpg", out_dtypes=(BF16,))})
    bwd = dict(out_dtypes=(F32, BF16), epilogue=_epi_rms_bwd(2), n_vec=1, **rows)
    dx2, dx2b, G["g_ple"] = _matmul(dpg, w_pg, "nt", name="mm_dh3", after=tok, extras=(x2, dx3),
                                    vec_extras=(S["g_ple"],), **bwd)
    tok = emit({"w_down": _matmul(act, dx2b, "tn", name="mm_dwdown", out_dtypes=(BF16,))})
    dact = _matmul(dx2b, w_down, "nt", name="mm_dact", after=tok)
    dup_g, dup_v, dcf_g, dcf_v = _ffn_act_bwd(up_pre, cv["conv_ffn"], dact, name="ffn_act_bwd")
    G["conv_ffn"] = jnp.concatenate([dcf_g, dcf_v], axis=1)
    dup = jnp.concatenate([dup_g, dup_v], axis=1)
    tok = emit({"w_up": _matmul(h2, dup, "tn", name="mm_dwup", out_dtypes=(BF16,), out_shards=True, tn=SHARD_TILE)})
    dh2 = _matmul(dup, w_up, "nt", name="mm_dh2", after=tok, b_shards=True, tk=SHARD_TILE)
    dx1, dx1b, G["g_ffn"] = _rms_bwd(x1, S["g_ffn"], dh2, dx2, name="rms2_bwd")
    tok = emit({"w_out": _matmul(ycat, dx1b, "tn", name="mm_dwout", out_dtypes=(BF16,))})
    dycat = _matmul(dx1b, w_out, "nt", name="mm_dycat", after=tok)
    do, dz, G["dn_g"] = _gated_norm_bwd(o, proj, S["dn_g"], dycat, z_off, nA, name="gated_norm_bwd")
    du, dw, dqd, dkd, dqk, dgl = _delta_scan_bwd(do, w, qd, kd, vn, qk, gl, ss, H, name="delta_scan_bwd")
    dq, dk, dv, dgB, dbB = _delta_prep_bwd(qkv, gamB, bB, ti, u, w, qk, du, dw, dqd, dkd, dqk, dgl, H,
                                           name="delta_prep_bwd")
    dgb = jnp.pad(jnp.concatenate([dgB[:, :, 0].T, dbB[:, :, 0].T], axis=1), ((0, 0), (0, LANE - 2 * H)))
    dab, dal, ddt = _gates_bwd(proj, alog, dtb, dgb, ab_off, H, name="gates_bwd")
    G["a_log"], G["dt_bias"] = dal[:, :H], ddt[:, :H]
    dqkv, G["conv_qkv"] = _qkv_bwd(proj, cv["conv_qkv"], dq, dk, dv, qkv_off, H, name="qkv_bwd")
    dax, dab_, dac, G["conv_a"] = _group_a_bwd(proj, cv["conv_a"], dycat, CW, name="group_a_bwd")
    in_p = w_in.shape[1]
    dproj = jnp.concatenate([dax, dab_, dac, dqkv, dz, dab, jnp.zeros((T, in_p - (ab_off + 1) * LANE), BF16)], axis=1)
    tok = emit({"w_in": _matmul(h1, dproj, "tn", name="mm_dwin", out_dtypes=(BF16,))})
    dh1 = _matmul(dproj, w_in, "nt", name="mm_dh1", after=tok)
    grad_x, _, G["g_mix"] = _rms_bwd(x, S["g_mix"], dh1, dx1, name="rms1_bwd")
    return loss, grad_x, G


def _pad_cols(a, n):
    return jnp.pad(a, ((0, 0), (0, n - a.shape[1])))


def _col_sharded(landed):
    _, R, C = landed.shape
    return jnp.transpose(landed, (1, 0, 2)).reshape(R, N_DEV * C)


def _col_parts(full):
    R, C8 = full.shape
    return jnp.transpose(full.reshape(R, N_DEV, C8 // N_DEV), (1, 0, 2))


def kernel(x, p, norm_mix_g, w_in, conv_a_w, conv_qkv_w, a_log, dt_bias, dn_norm_g, w_out, norm_ffn_g, w_up, conv_ffn_w, w_down, norm_ple_g, w_ple_gate, w_ple_proj, final_norm_g, loss_target, m_norm_mix_g, m_w_in, m_conv_a_w, m_conv_qkv_w, m_a_log, m_dt_bias, m_dn_norm_g, m_w_out, m_norm_ffn_g, m_w_up, m_conv_ffn_w, m_w_down, m_norm_ple_g, m_w_ple_gate, m_w_ple_proj, m_final_norm_g, v_norm_mix_g, v_w_in, v_conv_a_w, v_conv_qkv_w, v_a_log, v_dt_bias, v_dn_norm_g, v_w_out, v_norm_ffn_g, v_w_up, v_conv_ffn_w, v_w_down, v_norm_ple_g, v_w_ple_gate, v_w_ple_proj, v_final_norm_g):
    T, D = x.shape[1], x.shape[2]
    xd, _, cd = _mesh_pos()
    me = 4 * xd + 2 * lax.axis_index("y") + cd

    conv_sh = [conv_a_w[0], conv_qkv_w[0], conv_ffn_w[0]]
    conv_n = [c.size for c in conv_sh]
    pack_rows = -(-sum(conv_n) // LANE)
    conv_pack = jnp.pad(jnp.concatenate([c.reshape(-1) for c in conv_sh]), (0, pack_rows * LANE - sum(conv_n))).reshape(pack_rows, LANE)
    names = ["w_in", "conv", "w_out", "w_up", "w_down", "w_pg", "w_pp"]
    shards = [w_in[0].astype(BF16), conv_pack, w_out[0].astype(BF16), w_up[0].astype(BF16), w_down[0].astype(BF16),
              w_ple_gate[0].astype(BF16), w_ple_proj[0].astype(BF16)]
    empty_slots = lambda blocks: [lax.empty((N_DEV,) + tuple(b.shape), b.dtype) for b in blocks]
    handles, tok0 = _split_start(shards, empty_slots(shards), False, name="gather_start",
                                 relations=[(SIBLING,) + SAME_CORE] + [ALL_PEERS] * (len(shards) - 1))
    handle = dict(zip(names, handles))
    own = dict(zip(names, shards))
    in_cols = N_DEV * w_in.shape[2]
    in_p = (in_cols // LANE) * LANE + AB_PAD
    in_place = {"w_up", "w_pp"}

    def gathered(name, after):
        if name == "w_in":
            passed, fwd = _gather_forward(handle[name], after, name="gather_forward_w_in")
            landed = _gather_wait_two_level(passed, fwd, name="gather_wait_w_in")
        else:
            landed = _split_wait(handle[name], after, False, name="gather_wait_" + name)
        return lax.dynamic_update_index_in_dim(landed, own[name], me, 0)

    def wt(name, after):
        landed = gathered(name, after)
        if name in in_place:
            return landed
        return _pad_cols(_col_sharded(landed), in_p) if name == "w_in" else landed.reshape(-1, D)

    def conv(after):
        flat = gathered("conv", after).reshape(N_DEV, pack_rows * LANE)
        out, o_ = {}, 0
        for nm, c, n_ in zip(("conv_a", "conv_qkv", "conv_ffn"), conv_sh, conv_n):
            out[nm] = _col_sharded(flat[:, o_:o_ + n_].reshape((N_DEV,) + c.shape))
            o_ += n_
        return out

    pending, mine = {}, {}

    def emit(grads):
        parts = [g if nm in in_place else _col_parts(g[:, :in_cols]) if nm == "w_in" else g.reshape(N_DEV, -1, D)
                 for nm, g in grads.items()]
        hs, tok = _split_start(parts, empty_slots([q[0] for q in parts]), True, name="scatter_start_" + "_".join(grads))
        pending.update(zip(grads, hs))
        mine.update({nm: lax.dynamic_index_in_dim(q, me, 0, keepdims=False) for nm, q in zip(grads, parts)})
        return tok

    S = {
        "g_mix": norm_mix_g + tok0[0, 0], "a_log": a_log, "dt_bias": dt_bias, "dn_g": dn_norm_g, "g_ffn": norm_ffn_g,
        "g_ple": norm_ple_g, "g_final": final_norm_g.reshape(1, D),
    }

    loss_v, grad_x, G = _local_step(x[0], p[0, 0], loss_target[0], S, wt, conv, emit)
    loss = lax.psum(loss_v[0, 0], ("x", "y", "c"))

    small_names = ["g_mix", "g_ffn", "g_ple", "g_final", "dn_g", "a_log", "dt_bias", "conv_a", "conv_qkv", "conv_ffn"]
    small_rows, pieces = [], []
    for nm in small_names:
        g_ = G[nm].reshape(-1)
        r_ = -(-g_.size // (8 * LANE)) * 8
        small_rows.append(r_)
        pieces.append(jnp.pad(g_, (0, r_ * LANE - g_.size)).reshape(r_, LANE))
    (small_l,) = _exchange([jnp.concatenate(pieces, axis=0)], False, name="gather_small_grads")
    landed = {nm: _split_wait(h_, grad_x, True, name="scatter_wait_" + nm) for nm, h_ in pending.items()}
    big_l = [landed[nm] for nm in ("w_in", "w_out", "w_up", "w_down", "w_pg", "w_pp")]

    def small_parts(nm):
        i = small_names.index(nm)
        r0 = sum(small_rows[:i])
        shp = G[nm].shape
        return small_l[:, r0:r0 + small_rows[i], :].reshape(N_DEV, -1)[:, :G[nm].size].reshape((N_DEV,) + shp)

    def conv_parts(nm, shard):
        full = small_parts(nm)
        C = shard.shape[-1]
        return lax.dynamic_slice_in_dim(full, me * C, C, axis=2)

    def adam(parts, w_, m_, v_, nm, own_=None):
        shp = w_.shape
        w2, m2, v2 = (t.reshape(parts.shape[1:]) for t in (w_, m_, v_))
        kw = {} if own_ is None else {"own": own_, "me": me.astype(jnp.int32).reshape(1)}
        return tuple(t.reshape(shp) for t in _adam(parts, w2, m2, v2, name="adam_" + nm, **kw))

    res = [
        adam(small_parts("g_mix"), norm_mix_g, m_norm_mix_g, v_norm_mix_g, "norm_mix_g"),
        adam(big_l[0], w_in, m_w_in, v_w_in, "w_in", mine["w_in"]),
        adam(conv_parts("conv_a", conv_a_w), conv_a_w, m_conv_a_w, v_conv_a_w, "conv_a_w"),
        adam(conv_parts("conv_qkv", conv_qkv_w), conv_qkv_w, m_conv_qkv_w, v_conv_qkv_w, "conv_qkv_w"),
        adam(small_parts("a_log"), a_log, m_a_log, v_a_log, "a_log"),
        adam(small_parts("dt_bias"), dt_bias, m_dt_bias, v_dt_bias, "dt_bias"),
        adam(small_parts("dn_g"), dn_norm_g, m_dn_norm_g, v_dn_norm_g, "dn_norm_g"),
        adam(big_l[1], w_out, m_w_out, v_w_out, "w_out", mine["w_out"]),
        adam(small_parts("g_ffn"), norm_ffn_g, m_norm_ffn_g, v_norm_ffn_g, "norm_ffn_g"),
        adam(big_l[2], w_up, m_w_up, v_w_up, "w_up", mine["w_up"]),
        adam(conv_parts("conv_ffn", conv_ffn_w), conv_ffn_w, m_conv_ffn_w, v_conv_ffn_w, "conv_ffn_w"),
        adam(big_l[3], w_down, m_w_down, v_w_down, "w_down", mine["w_down"]),
        adam(small_parts("g_ple"), norm_ple_g, m_norm_ple_g, v_norm_ple_g, "norm_ple_g"),
        adam(big_l[4], w_ple_gate, m_w_ple_gate, v_w_ple_gate, "w_ple_gate", mine["w_pg"]),
        adam(big_l[5], w_ple_proj, m_w_ple_proj, v_w_ple_proj, "w_ple_proj", mine["w_pp"]),
        adam(small_parts("g_final"), final_norm_g.reshape(1, D), m_final_norm_g.reshape(1, D),
             v_final_norm_g.reshape(1, D), "final_norm_g"),
    ]
    res[-1] = tuple(t.reshape(D) for t in res[-1])
    grads, deltas, new_m, new_v = zip(*res)
    return (loss, grad_x[None], *grads, *deltas, *new_m, *new_v)
```

```python
import functools

import jax
import jax.numpy as jnp
from jax import lax
from jax.experimental import pallas as pl
from jax.experimental.pallas import tpu as pltpu

F32 = jnp.float32
BF16 = jnp.bfloat16

EPS = 1e-6
CHUNK = 64
HEAD = 128
LANE = 128
N_DEV = 8
AB_PAD = 512

ADAM_LR = 0.001
ADAM_B1 = 0.9
ADAM_B2 = 0.999
ADAM_EPS = 1e-08
ADAM_WD = 0.01
ADAM_STEP = 10

MESH = pl.DeviceIdType.MESH


def _tile(dim, target, align=LANE):
    if dim <= target:
        return dim
    t = (target // align) * align
    while t > align and dim % t:
        t -= align
    assert dim % t == 0, (dim, target)
    return t


def _params(sem, vmem_mb=48):
    return pltpu.CompilerParams(dimension_semantics=sem, vmem_limit_bytes=vmem_mb << 20)


_DN = {"nn": (((1,), (0,)), ((), ())), "nt": (((1,), (1,)), ((), ())), "tn": (((0,), (0,)), ((), ()))}
SHARD_TILE = 1408


def _matmul(a, b, mode, *, name, out_dtypes=(F32,), epilogue=None, extras=(), vec_extras=(), n_vec=0, after=None,
            b_shards=False, out_shards=False, tm=1024, tn=1024, tk=2048):
    shard_w = b.shape[2] if b_shards else None
    if b_shards:
        b_rows, b_cols = b.shape[1], N_DEV * shard_w
    else:
        b_rows, b_cols = b.shape
    if mode == "nn":
        (M, K), (K2, N) = a.shape, (b_rows, b_cols)
    elif mode == "nt":
        (M, K), (N, K2) = a.shape, (b_rows, b_cols)
    else:
        (K, M), (K2, N) = a.shape, (b_rows, b_cols)
    assert K == K2, (name, a.shape, b.shape)
    tm = _tile(M, tm)
    tn = _tile(shard_w if (b_shards and mode == "nn") else N // N_DEV if out_shards else N, tn)
    tk = _tile(shard_w if (b_shards and mode == "nt") else K, tk)
    nk = K // tk
    n_ex, n_out = len(extras) + len(vec_extras), len(out_dtypes)
    assert n_vec == 0 or tn == N, (name, tn, N)
    dn = _DN[mode]

    n_tok = 0 if after is None else 1

    def body(a_ref, b_ref, *rest):
        rest = rest[n_tok:]
        ex_refs, out_refs, vec_refs = rest[:n_ex], rest[n_ex:n_ex + n_out], rest[n_ex + n_out:n_ex + n_out + n_vec]
        part = lax.dot_general(a_ref[...].astype(BF16), b_ref[...].astype(BF16), dn, preferred_element_type=F32)
        first_rows = pl.program_id(0) == 0

        def finish(res):
            outs = (res,) if epilogue is None else epilogue(res, *[e[...] for e in ex_refs])
            for o_ref, val in zip(out_refs, outs[:n_out]):
                o_ref[...] = val.astype(o_ref.dtype)
            for v_ref, val in zip(vec_refs, outs[n_out:]):
                @pl.when(first_rows)
                def _(v_ref=v_ref, val=val):
                    v_ref[...] = val

                @pl.when(jnp.logical_not(first_rows))
                def _(v_ref=v_ref, val=val):
                    v_ref[...] += val

        if nk == 1:
            finish(part)
            return
        acc, k = rest[-1], pl.program_id(2)

        @pl.when(k == 0)
        def _():
            acc[...] = part

        @pl.when(k > 0)
        def _():
            acc[...] += part

        @pl.when(k == nk - 1)
        def _():
            finish(acc[...])

    a_spec = pl.BlockSpec((tk, tm), lambda i, j, k: (k, i)) if mode == "tn" else pl.BlockSpec((tm, tk), lambda i, j, k: (i, k))
    if b_shards and mode == "nn":
        per = shard_w // tn
        b_spec = pl.BlockSpec((None, tk, tn), lambda i, j, k: (lax.div(j, per), k, lax.rem(j, per)))
    elif b_shards:
        per = shard_w // tk
        b_spec = pl.BlockSpec((None, tn, tk), lambda i, j, k: (lax.div(k, per), j, lax.rem(k, per)))
    else:
        b_spec = pl.BlockSpec((tn, tk), lambda i, j, k: (j, k)) if mode == "nt" else pl.BlockSpec((tk, tn), lambda i, j, k: (k, j))
    mn_spec = pl.BlockSpec((tm, tn), lambda i, j, k: (i, j))
    vec_spec = pl.BlockSpec((1, tn), lambda i, j, k: (0, j))
    if out_shards:
        assert not extras
        per_o = (N // N_DEV) // tn
        out_spec = pl.BlockSpec((None, tm, tn), lambda i, j, k: (lax.div(j, per_o), i, lax.rem(j, per_o)))
        out_dims = (N_DEV, M, N // N_DEV)
    else:
        out_spec, out_dims = mn_spec, (M, N)
    outs = pl.pallas_call(
        body, name=name, grid=(M // tm, N // tn, nk),
        in_specs=[a_spec, b_spec] + [pl.BlockSpec((8, LANE), lambda i, j, k: (0, 0))] * n_tok
        + [mn_spec] * len(extras) + [vec_spec] * len(vec_extras),
        out_specs=[out_spec] * n_out + [vec_spec] * n_vec,
        out_shape=[jax.ShapeDtypeStruct(out_dims, dt) for dt in out_dtypes] + [jax.ShapeDtypeStruct((1, N), F32)] * n_vec,
        scratch_shapes=[pltpu.VMEM((tm, tn), F32)] if nk > 1 else [],
        compiler_params=_params(("arbitrary" if n_vec else "parallel", "parallel", "arbitrary"), 56),
    )(a, b, *([] if after is None else [after]), *extras, *vec_extras)
    return outs[0] if n_out + n_vec == 1 else outs


def _rms_fwd(x, g, *, name):
    T, D = x.shape
    tr = _tile(T, 256, 8)

    def body(x_ref, g_ref, h_ref):
        xv = x_ref[...]
        r = lax.rsqrt(jnp.mean(xv * xv, axis=-1, keepdims=True) + EPS)
        h_ref[...] = (xv * r * g_ref[...]).astype(h_ref.dtype)

    return pl.pallas_call(
        body, name=name, grid=(T // tr,),
        in_specs=[pl.BlockSpec((tr, D), lambda i: (i, 0)), pl.BlockSpec((1, D), lambda i: (0, 0))],
        out_specs=pl.BlockSpec((tr, D), lambda i: (i, 0)),
        out_shape=jax.ShapeDtypeStruct((T, D), BF16),
        compiler_params=_params(("parallel",)),
    )(x, g)


def _rms_bwd(x, g, dh, dres, *, name):
    T, D = x.shape
    tr = _tile(T, 256, 8)
    epi = _epi_rms_bwd(2)

    def body(x_ref, g_ref, dh_ref, dres_ref, dx_ref, dxb_ref, dg_ref):
        dx, _, dgp = epi(dh_ref[...], x_ref[...], dres_ref[...], g_ref[...])

        @pl.when(pl.program_id(0) == 0)
        def _():
            dg_ref[...] = jnp.zeros_like(dg_ref)

        dg_ref[...] += dgp
        dx_ref[...] = dx
        dxb_ref[...] = dx.astype(dxb_ref.dtype)

    row = pl.BlockSpec((tr, D), lambda i: (i, 0))
    vec = pl.BlockSpec((1, D), lambda i: (0, 0))
    return pl.pallas_call(
        body, name=name, grid=(T // tr,),
        in_specs=[row, vec, row, row], out_specs=[row, row, vec],
        out_shape=[jax.ShapeDtypeStruct((T, D), F32), jax.ShapeDtypeStruct((T, D), BF16), jax.ShapeDtypeStruct((1, D), F32)],
        compiler_params=_params(("arbitrary",)),
    )(x, g, dh, dres)


ROW_TILE = 256


def _epi_residual_rms(acc, res, g):
    xn = acc + res
    r = lax.rsqrt(jnp.mean(xn * xn, axis=-1, keepdims=True) + EPS)
    return xn, xn * r * g


def _epi_rms_bwd(n_copies):
    def epi(dh, x, dres, g):
        r = lax.rsqrt(jnp.mean(x * x, axis=-1, keepdims=True) + EPS)
        xh = x * r
        dxh = dh * g
        dx = dres + r * (dxh - xh * jnp.mean(dxh * xh, axis=-1, keepdims=True))
        return (dx,) * n_copies + (jnp.sum(dh * xh, axis=0, keepdims=True),)
    return epi


def _final_loss(x, g, tgt, *, name):
    T, D = x.shape
    tr = _tile(T, 256, 8)

    def body(x_ref, g_ref, t_ref, dx_ref, dg_ref, loss_ref):
        xv = x_ref[...]
        r = lax.rsqrt(jnp.mean(xv * xv, axis=-1, keepdims=True) + EPS)
        xh = xv * r
        gv = g_ref[...]
        err = xh * gv - t_ref[...]

        @pl.when(pl.program_id(0) == 0)
        def _():
            dg_ref[...] = jnp.zeros_like(dg_ref)
            loss_ref[...] = jnp.zeros_like(loss_ref)

        part = 0.5 * jnp.sum(jnp.mean(err * err, axis=-1, keepdims=True), axis=0, keepdims=True)
        loss_ref[...] += jnp.broadcast_to(part, loss_ref.shape)
        dy = err * (1.0 / D)
        dg_ref[...] += jnp.sum(dy * xh, axis=0, keepdims=True)
        dxh = dy * gv
        dx_ref[...] = r * (dxh - xh * jnp.mean(dxh * xh, axis=-1, keepdims=True))

    row = pl.BlockSpec((tr, D), lambda i: (i, 0))
    vec = pl.BlockSpec((1, D), lambda i: (0, 0))
    return pl.pallas_call(
        body, name=name, grid=(T // tr,),
        in_specs=[row, vec, row], out_specs=[row, vec, pl.BlockSpec((1, LANE), lambda i: (0, 0))],
        out_shape=[jax.ShapeDtypeStruct((T, D), F32), jax.ShapeDtypeStruct((1, D), F32),
                   jax.ShapeDtypeStruct((1, LANE), F32)],
        compiler_params=_params(("arbitrary",)),
    )(x, g, tgt)


def _ple_bwd(dx3, pp, sg, *, name):
    T, D = dx3.shape
    tr = _tile(T, 256, 8)

    def body(dx_ref, pp_ref, sg_ref, dpg_ref, dpp_ref):
        dx, s = dx_ref[...], sg_ref[...]
        dpg_ref[...] = (dx * pp_ref[...] * s * (1.0 - s)).astype(dpg_ref.dtype)
        dpp_ref[...] = (dx * s).astype(dpp_ref.dtype)

    row = pl.BlockSpec((tr, D), lambda i: (i, 0))
    return pl.pallas_call(
        body, name=name, grid=(T // tr,), in_specs=[row, row, row], out_specs=[row, row],
        out_shape=[jax.ShapeDtypeStruct((T, D), BF16)] * 2, compiler_params=_params(("parallel",)),
    )(dx3, pp, sg)


ROWS_QKV_FWD, ROWS_QKV_BWD, ROWS_FFN_FWD, ROWS_FFN_BWD, ROWS_GROUP_A = 512, 256, 256, 128, 256


def _ext(ref, r0, T, before, after, RC):
    parts = []
    if before:
        p0 = pl.multiple_of(jnp.maximum(r0 - 8, 0), 8)
        parts.append(jnp.where(r0 > 0, ref[pl.ds(p0, 8), :], 0.0))
    parts.append(ref[pl.ds(r0, RC), :])
    if after:
        n0 = pl.multiple_of(jnp.minimum(r0 + RC, T - 8), 8)
        parts.append(jnp.where(r0 + RC < T, ref[pl.ds(n0, 8), :], 0.0))
    return parts[0] if len(parts) == 1 else jnp.concatenate(parts, axis=0)


def _down(xx, s):
    return (xx if s == 0 else pltpu.roll(xx, s, 0))[8:, :]


def _up(xx, s, rows):
    return (xx if s == 0 else pltpu.roll(xx, xx.shape[0] - s, 0))[:rows, :]


def _conv_down(xx, w_ref, K):
    y = None
    for j in range(K):
        t = _down(xx, K - 1 - j) * w_ref[j:j + 1, :]
        y = t if y is None else y + t
    return y


def _fold8(x):
    return jnp.sum(x.reshape(x.shape[0] // 8, 8, x.shape[1]), axis=0)


def _silu(x):
    return x * jax.nn.sigmoid(x)


def _dsilu(x):
    s = jax.nn.sigmoid(x)
    return s * (1.0 + x * (1.0 - s))


def _col_specs(T, offs):
    return [pl.BlockSpec((T, LANE), functools.partial(lambda o, j: (0, o + j), o)) for o in offs]


def _group_a_fwd(proj, conv_w, CW, *, name):
    T = proj.shape[0]
    RC = _tile(T, ROWS_GROUP_A, 8)
    nb = CW // LANE
    K = conv_w.shape[0]

    def body(ax_ref, ab_ref, ac_ref, w_ref, y_ref):
        def step(i, carry):
            r0 = pl.multiple_of(i * RC, RC)
            m = _ext(ac_ref, r0, T, True, False, RC) * _ext(ax_ref, r0, T, True, False, RC)
            y_ref[pl.ds(r0, RC), :] = (ab_ref[pl.ds(r0, RC), :] * _conv_down(m, w_ref, K)).astype(y_ref.dtype)
            return carry
        lax.fori_loop(0, T // RC, step, 0)

    return pl.pallas_call(
        body, name=name, grid=(nb,),
        in_specs=_col_specs(T, (0, nb, 2 * nb)) + [pl.BlockSpec((K, LANE), lambda j: (0, j))],
        out_specs=pl.BlockSpec((T, LANE), lambda j: (0, j)),
        out_shape=jax.ShapeDtypeStruct((T, CW), BF16), compiler_params=_params(("parallel",)),
    )(proj, proj, proj, conv_w)


def _group_a_bwd(proj, conv_w, dycat, CW, *, name):
    T = proj.shape[0]
    RC = _tile(T, ROWS_GROUP_A, 8)
    nb = CW // LANE
    K = conv_w.shape[0]

    def body(ax_ref, ab_ref, ac_ref, w_ref, dy_ref, dax_ref, dab_ref, dac_ref, dw_ref):
        def step(i, accs):
            r0 = pl.multiple_of(i * RC, RC)
            ax3 = _ext(ax_ref, r0, T, True, True, RC)
            ac3 = _ext(ac_ref, r0, T, True, True, RC)
            m3 = ax3 * ac3
            c = _conv_down(m3[:RC + 8], w_ref, K)
            dy = dy_ref[pl.ds(r0, RC), :]
            dab_ref[pl.ds(r0, RC), :] = (dy * c).astype(dab_ref.dtype)
            dc2 = _ext(dy_ref, r0, T, False, True, RC) * _ext(ab_ref, r0, T, False, True, RC)
            dm = None
            new = []
            for j in range(K):
                s = K - 1 - j
                t = _up(dc2, s, RC) * w_ref[j:j + 1, :]
                dm = t if dm is None else dm + t
                new.append(accs[j] + _fold8(dc2[:RC] * _down(m3[:RC + 8], s)))
            dax_ref[pl.ds(r0, RC), :] = (dm * ac3[8:RC + 8]).astype(dax_ref.dtype)
            dac_ref[pl.ds(r0, RC), :] = (dm * ax3[8:RC + 8]).astype(dac_ref.dtype)
            return tuple(new)

        accs = lax.fori_loop(0, T // RC, step, tuple(jnp.zeros((8, LANE), F32) for _ in range(K)))
        for j in range(K):
            dw_ref[j:j + 1, :] = jnp.sum(accs[j], axis=0, keepdims=True)

    col = pl.BlockSpec((T, LANE), lambda j: (0, j))
    wsp = pl.BlockSpec((K, LANE), lambda j: (0, j))
    return pl.pallas_call(
        body, name=name, grid=(nb,),
        in_specs=_col_specs(T, (0, nb, 2 * nb)) + [wsp, col],
        out_specs=[col, col, col, wsp],
        out_shape=[jax.ShapeDtypeStruct((T, CW), BF16)] * 3 + [jax.ShapeDtypeStruct((K, CW), F32)],
        compiler_params=_params(("parallel",)),
    )(proj, proj, proj, conv_w, dycat)


def _qkv_fwd(proj, conv_w, off, H, *, name):
    T = proj.shape[0]
    RC = _tile(T, ROWS_QKV_FWD, 8)
    nb = 3 * H
    K = conv_w.shape[0]

    def body(x_ref, w_ref, y_ref):
        j = pl.program_id(0)
        is_qk = j < 2 * H
        scale = jnp.where(j < H, HEAD ** -0.5, 1.0).astype(F32)

        def step(i, carry):
            r0 = pl.multiple_of(i * RC, RC)
            s = _silu(_conv_down(_ext(x_ref, r0, T, True, False, RC), w_ref, K))
            r = lax.rsqrt(jnp.sum(s * s, axis=-1, keepdims=True) + EPS) * scale
            y_ref[pl.ds(r0, RC), :] = s * jnp.where(is_qk, r, 1.0)
            return carry
        lax.fori_loop(0, T // RC, step, 0)

    return pl.pallas_call(
        body, name=name, grid=(nb,),
        in_specs=_col_specs(T, (off,)) + [pl.BlockSpec((K, LANE), lambda j: (0, j))],
        out_specs=pl.BlockSpec((T, LANE), lambda j: (0, j)),
        out_shape=jax.ShapeDtypeStruct((T, nb * LANE), F32), compiler_params=_params(("parallel",)),
    )(proj, conv_w)


def _qkv_bwd(proj, conv_w, dq, dk, dv, off, H, *, name):
    T = proj.shape[0]
    RC = _tile(T, ROWS_QKV_BWD, 8)
    nb = 3 * H
    K = conv_w.shape[0]

    def body(x_ref, w_ref, dq_ref, dk_ref, dv_ref, dx_ref, dw_ref):
        j = pl.program_id(0)
        is_qk = j < 2 * H
        scale = jnp.where(j < H, HEAD ** -0.5, 1.0).astype(F32)

        def step(i, accs):
            r0 = pl.multiple_of(i * RC, RC)
            x3 = _ext(x_ref, r0, T, True, True, RC)
            c2 = _conv_down(x3, w_ref, K)
            s2 = _silu(c2)
            dn2 = jnp.where(j < H, _ext(dq_ref, r0, T, False, True, RC),
                            jnp.where(is_qk, _ext(dk_ref, r0, T, False, True, RC), _ext(dv_ref, r0, T, False, True, RC)))
            r = lax.rsqrt(jnp.sum(s2 * s2, axis=-1, keepdims=True) + EPS)
            nh = s2 * r
            dnp = dn2 * scale
            ds_qk = r * (dnp - nh * jnp.sum(dnp * nh, axis=-1, keepdims=True))
            ds2 = jnp.where(is_qk, ds_qk, dn2)
            dc2 = ds2 * _dsilu(c2)
            dx = None
            new = []
            for jj in range(K):
                s = K - 1 - jj
                t = _up(dc2, s, RC) * w_ref[jj:jj + 1, :]
                dx = t if dx is None else dx + t
                new.append(accs[jj] + _fold8(dc2[:RC] * _down(x3[:RC + 8], s)))
            dx_ref[pl.ds(r0, RC), :] = dx.astype(dx_ref.dtype)
            return tuple(new)

        accs = lax.fori_loop(0, T // RC, step, tuple(jnp.zeros((8, LANE), F32) for _ in range(K)))
        for jj in range(K):
            dw_ref[jj:jj + 1, :] = jnp.sum(accs[jj], axis=0, keepdims=True)

    col = pl.BlockSpec((T, LANE), lambda j: (0, j))
    wsp = pl.BlockSpec((K, LANE), lambda j: (0, j))
    return pl.pallas_call(
        body, name=name, grid=(nb,),
        in_specs=_col_specs(T, (off,)) + [wsp] + [
            pl.BlockSpec((T, LANE), functools.partial(lambda o, j: (0, jnp.clip(j - o, 0, H - 1)), o)) for o in (0, H, 2 * H)],
        out_specs=[col, wsp],
        out_shape=[jax.ShapeDtypeStruct((T, nb * LANE), BF16), jax.ShapeDtypeStruct((K, nb * LANE), F32)],
        compiler_params=_params(("parallel",)),
    )(proj, conv_w, dq, dk, dv)


def _softplus(x):
    return jnp.maximum(x, 0.0) + jnp.log(1.0 + jnp.exp(-jnp.abs(x)))


def _gates_fwd(proj, alog, dtb, off, H, *, name):
    T = proj.shape[0]
    tr = _tile(T, 512, CHUNK)

    def body(ab_ref, al_ref, dt_ref, gb_ref, gam_ref):
        ab = ab_ref[...]
        lane = lax.broadcasted_iota(jnp.int32, ab.shape, 1)
        g = -jnp.exp(al_ref[...]) * _softplus(ab + dt_ref[...])
        gb = jnp.where(lane < H, g, jnp.where(lane < 2 * H, jax.nn.sigmoid(ab), 0.0))
        gb_ref[...] = gb
        tril = _tri().astype(F32)
        for c in range(tr // CHUNK):
            rows = slice(c * CHUNK, (c + 1) * CHUNK)
            gam_ref[rows, :] = _mm(tril, gb[rows, :], precision=lax.Precision.HIGHEST)

    vec = pl.BlockSpec((1, LANE), lambda i: (0, 0))
    row = pl.BlockSpec((tr, LANE), lambda i: (i, 0))
    return pl.pallas_call(
        body, name=name, grid=(T // tr,),
        in_specs=[pl.BlockSpec((tr, LANE), lambda i: (i, off)), vec, vec],
        out_specs=[row, row],
        out_shape=[jax.ShapeDtypeStruct((T, LANE), F32)] * 2, compiler_params=_params(("parallel",)),
    )(proj, alog, dtb)


def _gates_bwd(proj, alog, dtb, dgb, off, H, *, name):
    T = proj.shape[0]
    tr = _tile(T, 512, CHUNK)

    def body(ab_ref, al_ref, dt_ref, d_ref, dab_ref, dal_ref, ddt_ref):
        ab, d = ab_ref[...], d_ref[...]
        lane = lax.broadcasted_iota(jnp.int32, ab.shape, 1)
        is_g = lane < H
        triu = _tri(upper=True).astype(F32)
        dg = jnp.concatenate([_mm(triu, d[c * CHUNK:(c + 1) * CHUNK, :], precision=lax.Precision.HIGHEST)
                              for c in range(tr // CHUNK)], axis=0)
        z = ab + dt_ref[...]
        A = -jnp.exp(al_ref[...])
        da = dg * A * jax.nn.sigmoid(z)
        beta = jax.nn.sigmoid(ab)
        db = d * beta * (1.0 - beta)
        dab_ref[...] = jnp.where(is_g, da, jnp.where(lane < 2 * H, db, 0.0)).astype(dab_ref.dtype)

        @pl.when(pl.program_id(0) == 0)
        def _():
            dal_ref[...] = jnp.zeros_like(dal_ref)
            ddt_ref[...] = jnp.zeros_like(ddt_ref)

        dal_ref[...] += jnp.sum(jnp.where(is_g, dg * A * _softplus(z), 0.0), axis=0, keepdims=True)
        ddt_ref[...] += jnp.sum(jnp.where(is_g, da, 0.0), axis=0, keepdims=True)

    vec = pl.BlockSpec((1, LANE), lambda i: (0, 0))
    row = pl.BlockSpec((tr, LANE), lambda i: (i, 0))
    return pl.pallas_call(
        body, name=name, grid=(T // tr,),
        in_specs=[pl.BlockSpec((tr, LANE), lambda i: (i, off)), vec, vec, row],
        out_specs=[row, vec, vec],
        out_shape=[jax.ShapeDtypeStruct((T, LANE), BF16), jax.ShapeDtypeStruct((1, LANE), F32),
                   jax.ShapeDtypeStruct((1, LANE), F32)],
        compiler_params=_params(("arbitrary",)),
    )(proj, alog, dtb, dgb)


def _gated_norm_fwd(o, proj, gn, zoff, *, name):
    T, W = o.shape
    tr = _tile(T, 512, 8)

    def body(o_ref, z_ref, g_ref, y_ref):
        ov = o_ref[...]
        r = lax.rsqrt(jnp.mean(ov * ov, axis=-1, keepdims=True) + EPS)
        y_ref[...] = (ov * r * g_ref[...] * _silu(z_ref[...])).astype(y_ref.dtype)

    blk = pl.BlockSpec((tr, LANE), lambda i, j: (i, j))
    return pl.pallas_call(
        body, name=name, grid=(T // tr, W // LANE),
        in_specs=[blk, pl.BlockSpec((tr, LANE), lambda i, j: (i, zoff + j)), pl.BlockSpec((1, LANE), lambda i, j: (0, 0))],
        out_specs=blk, out_shape=jax.ShapeDtypeStruct((T, W), BF16), compiler_params=_params(("parallel", "parallel")),
    )(o, proj, gn)


def _gated_norm_bwd(o, proj, gn, dycat, zoff, yoff, *, name):
    T, W = o.shape
    tr = _tile(T, 512, 8)

    def body(o_ref, z_ref, g_ref, dy_ref, do_ref, dz_ref, dg_ref):
        ov, zv, gv, dy = o_ref[...], z_ref[...], g_ref[...], dy_ref[...]
        r = lax.rsqrt(jnp.mean(ov * ov, axis=-1, keepdims=True) + EPS)
        nh = ov * r
        s = _silu(zv)

        @pl.when((pl.program_id(0) == 0) & (pl.program_id(1) == 0))
        def _():
            dg_ref[...] = jnp.zeros_like(dg_ref)

        dg_ref[...] += jnp.sum(dy * nh * s, axis=0, keepdims=True)
        dz_ref[...] = (dy * nh * gv * _dsilu(zv)).astype(dz_ref.dtype)
        dn = dy * gv * s
        do_ref[...] = r * (dn - nh * jnp.mean(dn * nh, axis=-1, keepdims=True))

    blk = pl.BlockSpec((tr, LANE), lambda i, j: (i, j))
    vec = pl.BlockSpec((1, LANE), lambda i, j: (0, 0))
    return pl.pallas_call(
        body, name=name, grid=(T // tr, W // LANE),
        in_specs=[blk, pl.BlockSpec((tr, LANE), lambda i, j: (i, zoff + j)), vec,
                  pl.BlockSpec((tr, LANE), lambda i, j: (i, yoff + j))],
        out_specs=[blk, blk, vec],
        out_shape=[jax.ShapeDtypeStruct((T, W), F32), jax.ShapeDtypeStruct((T, W), BF16),
                   jax.ShapeDtypeStruct((1, LANE), F32)],
        compiler_params=_params(("arbitrary", "arbitrary")),
    )(o, proj, gn, dycat)


def _ffn_act_fwd(up_pre, conv_w, *, name):
    T, F2 = up_pre.shape
    RC = _tile(T, ROWS_FFN_FWD, 8)
    nb = F2 // 2 // LANE
    K = conv_w.shape[0]

    def body(g_ref, v_ref, wg_ref, wv_ref, y_ref):
        def step(i, carry):
            r0 = pl.multiple_of(i * RC, RC)
            gate = _conv_down(_ext(g_ref, r0, T, True, False, RC), wg_ref, K)
            val = _conv_down(_ext(v_ref, r0, T, True, False, RC), wv_ref, K)
            y_ref[pl.ds(r0, RC), :] = (_silu(gate) * val).astype(y_ref.dtype)
            return carry
        lax.fori_loop(0, T // RC, step, 0)

    return pl.pallas_call(
        body, name=name, grid=(nb,),
        in_specs=_col_specs(T, (0, nb)) + [pl.BlockSpec((K, LANE), lambda j: (0, j)),
                                           pl.BlockSpec((K, LANE), lambda j: (0, nb + j))],
        out_specs=pl.BlockSpec((T, LANE), lambda j: (0, j)),
        out_shape=jax.ShapeDtypeStruct((T, F2 // 2), BF16), compiler_params=_params(("parallel",)),
    )(up_pre, up_pre, conv_w, conv_w)


def _ffn_act_bwd(up_pre, conv_w, dact, *, name):
    T, F2 = up_pre.shape
    RC = _tile(T, ROWS_FFN_BWD, 8)
    nb = F2 // 2 // LANE
    K = conv_w.shape[0]

    def body(g_ref, v_ref, wg_ref, wv_ref, da_ref, dg_ref, dv_ref, dwg_ref, dwv_ref):
        def step(i, accs):
            r0 = pl.multiple_of(i * RC, RC)
            g3 = _ext(g_ref, r0, T, True, True, RC)
            v3 = _ext(v_ref, r0, T, True, True, RC)
            gate2 = _conv_down(g3, wg_ref, K)
            val2 = _conv_down(v3, wv_ref, K)
            da2 = _ext(da_ref, r0, T, False, True, RC)
            dgate2 = da2 * val2 * _dsilu(gate2)
            dval2 = da2 * _silu(gate2)
            dgp, dvp, new = None, None, []
            for j in range(K):
                s = K - 1 - j
                tg = _up(dgate2, s, RC) * wg_ref[j:j + 1, :]
                tv = _up(dval2, s, RC) * wv_ref[j:j + 1, :]
                dgp = tg if dgp is None else dgp + tg
                dvp = tv if dvp is None else dvp + tv
                new.append(accs[2 * j] + _fold8(dgate2[:RC] * _down(g3[:RC + 8], s)))
                new.append(accs[2 * j + 1] + _fold8(dval2[:RC] * _down(v3[:RC + 8], s)))
            dg_ref[pl.ds(r0, RC), :] = dgp.astype(dg_ref.dtype)
            dv_ref[pl.ds(r0, RC), :] = dvp.astype(dv_ref.dtype)
            return tuple(new)

        accs = lax.fori_loop(0, T // RC, step, tuple(jnp.zeros((8, LANE), F32) for _ in range(2 * K)))
        for j in range(K):
            dwg_ref[j:j + 1, :] = jnp.sum(accs[2 * j], axis=0, keepdims=True)
            dwv_ref[j:j + 1, :] = jnp.sum(accs[2 * j + 1], axis=0, keepdims=True)

    col = pl.BlockSpec((T, LANE), lambda j: (0, j))
    wsp = pl.BlockSpec((K, LANE), lambda j: (0, j))
    return pl.pallas_call(
        body, name=name, grid=(nb,),
        in_specs=_col_specs(T, (0, nb)) + [wsp, pl.BlockSpec((K, LANE), lambda j: (0, nb + j)), col],
        out_specs=[col, col, wsp, wsp],
        out_shape=[jax.ShapeDtypeStruct((T, F2 // 2), BF16)] * 2 + [jax.ShapeDtypeStruct((K, F2 // 2), F32)] * 2,
        compiler_params=_params(("parallel",)),
    )(up_pre, up_pre, conv_w, conv_w, dact)


CPB = 8
CPB_SCAN = 4
GRP = 8
HP = lax.Precision.HIGH


def _tri(strict=False, upper=False):
    r = lax.broadcasted_iota(jnp.int32, (CHUNK, CHUNK), 0)
    c = lax.broadcasted_iota(jnp.int32, (CHUNK, CHUNK), 1)
    if upper:
        return c >= r
    return (r > c) if strict else (r >= c)


def _mm(a, b, dn="nn", precision=None):
    precision = HP if precision is None else precision
    return lax.dot_general(a, b, _DN[dn], precision=precision, preferred_element_type=F32)


def _mm16(a, b, dn="nn"):
    return lax.dot_general(a.astype(BF16), b.astype(BF16), _DN[dn], preferred_element_type=F32)


def _each(f, *cols):
    return [f(*xs) for xs in zip(*cols)]


def _decay(gam):
    return jnp.exp(jnp.where(_tri(), gam[:, :CHUNK] - gam.T[:CHUNK, :], -1e30))


def _delta_specs(T, H, cpb):
    rows = cpb * CHUNK
    col = lambda o: pl.BlockSpec((rows, LANE), functools.partial(lambda o, h, n: (n, o + h), o))
    bc = pl.BlockSpec((1, rows, LANE), lambda h, n: (h, n, 0))
    sq = pl.BlockSpec((1, cpb, CHUNK, CHUNK), lambda h, n: (h, n, 0, 0))
    vec = pl.BlockSpec((1, cpb, 1, LANE), lambda h, n: (h, n, 0, 0))
    return col, bc, sq, vec


def _delta_prep_fwd(qkv, gamB, bB, H, *, name):
    T = qkv.shape[0]
    N = T // CHUNK
    cpb = _tile(N, CPB, 8)
    grp = min(GRP, cpb)
    col, bc, sq, vec = _delta_specs(T, H, cpb)

    def body(q_ref, k_ref, v_ref, g_ref, b_ref, u_ref, w_ref, qd_ref, kd_ref, qk_ref, ti_ref, gl_ref):
        eye = (lax.broadcasted_iota(jnp.int32, (CHUNK, CHUNK), 0) == lax.broadcasted_iota(jnp.int32, (CHUNK, CHUNK), 1)).astype(F32)
        strict = _tri(strict=True)
        for c0 in range(0, cpb, grp):
            cs = list(range(c0, c0 + grp))
            rows = [slice(c * CHUNK, (c + 1) * CHUNK) for c in cs]
            q, k, v = ([r_[r, :] for r in rows] for r_ in (q_ref, k_ref, v_ref))
            bb = [b_ref[0, r, :] for r in rows]
            gam = [g_ref[0, r, :] for r in rows]
            D = _each(_decay, gam)
            e = _each(jnp.exp, gam)
            kk = _each(lambda k_: _mm(k_, k_, "nt"), k)
            X = _each(lambda kk_, D_, b_: -(jnp.where(strict, kk_ * D_, 0.0) * b_[:, :CHUNK]), kk, D, bb)
            R = _each(lambda x: eye + x, X)
            for _ in range(5):
                X = _each(lambda x: _mm(x, x), X)
                R = _each(lambda r, x: r + _mm(r, x), R, X)
            u = _each(lambda r, b_, v_: _mm(r, b_ * v_), R, bb, v)
            w = _each(lambda r, b_, e_, k_: _mm(r, b_ * e_ * k_), R, bb, e, k)
            qk = _each(lambda q_, k_, D_: _mm(q_, k_, "nt") * D_, q, k, D)
            for i, c in enumerate(cs):
                glast = gam[i][CHUNK - 1:CHUNK, :]
                u_ref[rows[i], :] = u[i]
                w_ref[rows[i], :] = w[i]
                qd_ref[rows[i], :] = e[i] * q[i]
                kd_ref[rows[i], :] = jnp.exp(glast - gam[i]) * k[i]
                qk_ref[0, c] = qk[i]
                ti_ref[0, c] = R[i]
                gl_ref[0, c] = jnp.exp(glast)

    full = jax.ShapeDtypeStruct((T, H * LANE), F32)
    sqs = jax.ShapeDtypeStruct((H, N, CHUNK, CHUNK), F32)
    return pl.pallas_call(
        body, name=name, grid=(H, N // cpb),
        in_specs=[col(0), col(H), col(2 * H), bc, bc],
        out_specs=[col(0)] * 4 + [sq, sq, vec],
        out_shape=[full] * 4 + [sqs, sqs, jax.ShapeDtypeStruct((H, N, 1, LANE), F32)],
        compiler_params=_params(("parallel", "parallel")),
    )(qkv, qkv, qkv, gamB, bB)


def _scan_specs(H, N, cpb, hb, rev):
    nbk = N // cpb
    blk = (lambda n: nbk - 1 - n) if rev else (lambda n: n)
    col = pl.BlockSpec((cpb * CHUNK, hb * LANE), lambda h, n: (blk(n), h))
    sq = pl.BlockSpec((hb, cpb, CHUNK, CHUNK), lambda h, n: (h, blk(n), 0, 0))
    vec = pl.BlockSpec((hb, cpb, 1, LANE), lambda h, n: (h, blk(n), 0, 0))
    st = pl.BlockSpec((hb, cpb, HEAD, HEAD), lambda h, n: (h, blk(n), 0, 0))
    return col, sq, vec, st


def _delta_scan_fwd(u, w, qd, kd, qk, gl, H, *, name):
    T = u.shape[0]
    N = T // CHUNK
    cpb = _tile(N, CPB_SCAN, 4)
    hb = min(GRP, H)
    col, sq, vec, st = _scan_specs(H, N, cpb, hb, False)
    lanes = [slice(j * LANE, (j + 1) * LANE) for j in range(hb)]
    heads = list(range(hb))

    def body(u_ref, w_ref, qd_ref, kd_ref, qk_ref, gl_ref, o_ref, vn_ref, ss_ref, s_scr):
        @pl.when(pl.program_id(1) == 0)
        def _():
            s_scr[...] = jnp.zeros_like(s_scr)

        def step(c, states):
            rows = pl.ds(pl.multiple_of(c * CHUNK, CHUNK), CHUNK)
            S = list(states)
            for j in heads:
                ss_ref[j, c] = S[j]
            wS = _each(lambda ln, s: _mm16(w_ref[rows, ln], s), lanes, S)
            qS = _each(lambda ln, s: _mm16(qd_ref[rows, ln], s), lanes, S)
            vn = _each(lambda ln, ws: u_ref[rows, ln] - ws, lanes, wS)
            o = _each(lambda j, qs, vn_: qs + _mm16(qk_ref[j, c], vn_), heads, qS, vn)
            new = _each(lambda j, ln, s, vn_: s * gl_ref[j, c] + _mm16(kd_ref[rows, ln], vn_, "tn"),
                        heads, lanes, S, vn)
            for j in heads:
                o_ref[rows, lanes[j]] = o[j]
                vn_ref[rows, lanes[j]] = vn[j]
            return tuple(new)
        out = lax.fori_loop(0, cpb, step, tuple(s_scr[j] for j in heads))
        for j in heads:
            s_scr[j] = out[j]

    full = jax.ShapeDtypeStruct((T, H * LANE), F32)
    return pl.pallas_call(
        body, name=name, grid=(H // hb, N // cpb),
        in_specs=[col] * 4 + [sq, vec],
        out_specs=[col, col, st],
        out_shape=[full, full, jax.ShapeDtypeStruct((H, N, HEAD, HEAD), F32)],
        scratch_shapes=[pltpu.VMEM((hb, HEAD, HEAD), F32)],
        compiler_params=_params(("parallel", "arbitrary")),
    )(u, w, qd, kd, qk, gl)


def _delta_scan_bwd(do, w, qd, kd, vn, qk, gl, ss, H, *, name):
    T = do.shape[0]
    N = T // CHUNK
    cpb = _tile(N, CPB_SCAN, 4)
    hb = min(GRP, H)
    col, sq, vec, st = _scan_specs(H, N, cpb, hb, True)
    lanes = [slice(j * LANE, (j + 1) * LANE) for j in range(hb)]
    heads = list(range(hb))

    def body(do_ref, w_ref, qd_ref, kd_ref, vn_ref, qk_ref, gl_ref, ss_ref,
             du_ref, dw_ref, dqd_ref, dkd_ref, dqk_ref, dgl_ref, ds_scr):
        @pl.when(pl.program_id(1) == 0)
        def _():
            ds_scr[...] = jnp.zeros_like(ds_scr)

        def step(i, dstates):
            c = cpb - 1 - i
            rows = pl.ds(pl.multiple_of(c * CHUNK, CHUNK), CHUNK)
            dS = list(dstates)
            S = [ss_ref[j, c] for j in heads]
            dov = [do_ref[rows, ln] for ln in lanes]
            vnv = [vn_ref[rows, ln] for ln in lanes]
            a1 = _each(lambda j, d_: _mm16(qk_ref[j, c], d_, "tn"), heads, dov)
            a2 = _each(lambda ln, ds: _mm16(kd_ref[rows, ln], ds), lanes, dS)
            dvn = _each(lambda x, y: x + y, a1, a2)
            dqd = _each(lambda d_, s: _mm16(d_, s, "nt"), dov, S)
            dkd = _each(lambda v_, ds: _mm16(v_, ds, "nt"), vnv, dS)
            dqk = _each(lambda d_, v_: _mm16(d_, v_, "nt"), dov, vnv)
            dw = _each(lambda dv_, s: -_mm16(dv_, s, "nt"), dvn, S)
            b1 = _each(lambda ln, d_: _mm16(qd_ref[rows, ln], d_, "tn"), lanes, dov)
            b2 = _each(lambda ln, dv_: _mm16(w_ref[rows, ln], dv_, "tn"), lanes, dvn)
            new = _each(lambda j, x, y, ds: x + ds * gl_ref[j, c] - y, heads, b1, b2, dS)
            for j in heads:
                du_ref[rows, lanes[j]] = dvn[j]
                dw_ref[rows, lanes[j]] = dw[j]
                dqd_ref[rows, lanes[j]] = dqd[j]
                dkd_ref[rows, lanes[j]] = dkd[j]
                dqk_ref[j, c] = dqk[j]
                dgl = jnp.sum(jnp.sum(dS[j] * S[j], axis=1, keepdims=True), axis=0, keepdims=True)
                dgl_ref[j, c] = jnp.broadcast_to(dgl, (1, LANE))
            return tuple(new)
        out = lax.fori_loop(0, cpb, step, tuple(ds_scr[j] for j in heads))
        for j in heads:
            ds_scr[j] = out[j]

    full = jax.ShapeDtypeStruct((T, H * LANE), F32)
    return pl.pallas_call(
        body, name=name, grid=(H // hb, N // cpb),
        in_specs=[col] * 5 + [sq, vec, st],
        out_specs=[col] * 4 + [sq, vec],
        out_shape=[full] * 4 + [jax.ShapeDtypeStruct((H, N, CHUNK, CHUNK), F32), jax.ShapeDtypeStruct((H, N, 1, LANE), F32)],
        scratch_shapes=[pltpu.VMEM((hb, HEAD, HEAD), F32)],
        compiler_params=_params(("parallel", "arbitrary")),
    )(do, w, qd, kd, vn, qk, gl, ss)


def _delta_prep_bwd(qkv, gamB, bB, ti, u, w, qk, du, dw, dqd, dkd, dqk, dgl, H, *, name):
    T = qkv.shape[0]
    N = T // CHUNK
    cpb = _tile(N, CPB, 8)
    grp = min(GRP, cpb)
    col, bc, sq, vec = _delta_specs(T, H, cpb)

    def body(q_ref, k_ref, v_ref, g_ref, b_ref, ti_ref, u_ref, w_ref, qk_ref,
             du_ref, dw_ref, dqd_ref, dkd_ref, dqk_ref, dgl_ref,
             dq_ref, dk_ref, dv_ref, dg_ref, db_ref):
        ones = jnp.ones((CHUNK, LANE), F32)
        strict = _tri(strict=True)
        last = lax.broadcasted_iota(jnp.int32, (CHUNK, LANE), 0) == CHUNK - 1
        lsum = lambda x: jnp.sum(x, axis=-1, keepdims=True)
        for c0 in range(0, cpb, grp):
            cs = list(range(c0, c0 + grp))
            rows = [slice(c * CHUNK, (c + 1) * CHUNK) for c in cs]
            ld = lambda r_: [r_[r, :] for r in rows]
            q, k, v, uv, wv, duv, dwv, dqd_v, dkd_v = (ld(r_) for r_ in (q_ref, k_ref, v_ref, u_ref, w_ref, du_ref, dw_ref, dqd_ref, dkd_ref))
            bb = [b_ref[0, r, :] for r in rows]
            gam = [g_ref[0, r, :] for r in rows]
            Ti = [ti_ref[0, c] for c in cs]
            QK = [qk_ref[0, c] for c in cs]
            dqk_v = [dqk_ref[0, c] for c in cs]
            D = _each(_decay, gam)
            e = _each(jnp.exp, gam)
            glast = [g_[CHUNK - 1:CHUNK, :] for g_ in gam]
            eL = _each(lambda gl_, g_: jnp.exp(gl_ - g_), glast, gam)
            kk = _each(lambda k_: _mm(k_, k_, "nt"), k)
            KKD = _each(lambda kk_, D_: jnp.where(strict, kk_ * D_, 0.0), kk, D)
            dru = _each(lambda t, d_: _mm(t, d_, "tn"), Ti, duv)
            drw = _each(lambda t, d_: _mm(t, d_, "tn"), Ti, dwv)
            l1 = _each(lambda a, b: _mm(a, b, "nt"), dru, uv)
            l2 = _each(lambda a, b: _mm(a, b, "nt"), drw, wv)
            dL = _each(lambda a, b: jnp.where(strict, -(a + b), 0.0), l1, l2)
            Mm = _each(lambda dl, b_: dl * b_[:, :CHUNK], dL, bb)
            dKK = _each(lambda m_, D_: m_ * D_, Mm, D)
            dQK = _each(lambda a, D_: a * D_, dqk_v, D)
            P = _each(lambda m_, kkd, a, qk_: m_ * kkd + a * qk_, Mm, KKD, dqk_v, QK)
            q1 = _each(lambda a, k_: _mm(a, k_), dQK, k)
            k1 = _each(lambda a, q_: _mm(a, q_, "tn"), dQK, q)
            k2 = _each(lambda a, k_: _mm(a, k_), dKK, k)
            k3 = _each(lambda a, k_: _mm(a, k_, "tn"), dKK, k)
            s1 = _each(lambda dl, kkd: _mm(dl * kkd, ones), dL, KKD)
            p1 = _each(lambda p_: _mm(p_, ones), P)
            p2 = _each(lambda p_: _mm(p_, ones, "tn"), P)
            for i, c in enumerate(cs):
                r = rows[i]
                bek = bb[i] * e[i]
                kdv = eL[i] * k[i]
                dq_ref[r, :] = q1[i] + e[i] * dqd_v[i]
                dk_ref[r, :] = k1[i] + k2[i] + k3[i] + bek * drw[i] + eL[i] * dkd_v[i]
                dv_ref[r, :] = bb[i] * dru[i]
                db_ref[0, r, :] = s1[i] + lsum(dru[i] * v[i]) + lsum(drw[i] * e[i] * k[i])
                dgam = (p1[i] - p2[i] + lsum(drw[i] * bek * k[i]) + lsum(dqd_v[i] * e[i] * q[i])
                        - lsum(dkd_v[i] * kdv))
                xlast = jnp.sum(lsum(dkd_v[i] * kdv), axis=0, keepdims=True) + jnp.exp(glast[i]) * dgl_ref[0, c]
                dg_ref[0, r, :] = dgam + jnp.where(last, xlast, 0.0)

    full = jax.ShapeDtypeStruct((T, H * LANE), F32)
    bcs = jax.ShapeDtypeStruct((H, T, LANE), F32)
    return pl.pallas_call(
        body, name=name, grid=(H, N // cpb),
        in_specs=[col(0), col(H), col(2 * H), bc, bc, sq, col(0), col(0), sq, col(0), col(0), col(0), col(0), sq, vec],
        out_specs=[col(0), col(0), col(0), bc, bc],
        out_shape=[full, full, full, bcs, bcs],
        compiler_params=_params(("parallel", "parallel")),
    )(qkv, qkv, qkv, gamB, bB, ti, u, w, qk, du, dw, dqd, dkd, dqk, dgl)


def _adam(parts, w, m, v, *, name, own=None, me=None):
    P, R, C = parts.shape
    if R > 256 and R % 8:
        tr, tc = R, _tile(C, 256)
    else:
        tr, tc = _tile(R, 256, 8), C
    n_own = 0 if own is None else 2

    def body(*refs):
        p_ref, w_ref, m_ref, v_ref, g_ref, d_ref, nm_ref, nv_ref = refs[n_own:]
        g = None
        for i in range(P):
            t = p_ref[i].astype(F32)
            if n_own:
                t = jnp.where(refs[0][0] == i, refs[1][...].astype(F32), t)
            g = t if g is None else g + t
        mn = ADAM_B1 * m_ref[...] + (1.0 - ADAM_B1) * g
        vn = ADAM_B2 * v_ref[...] + (1.0 - ADAM_B2) * (g * g)
        m_hat = mn / (1.0 - ADAM_B1 ** ADAM_STEP)
        v_hat = vn / (1.0 - ADAM_B2 ** ADAM_STEP)
        g_ref[...] = g
        d_ref[...] = -ADAM_LR * (m_hat / (jnp.sqrt(v_hat) + ADAM_EPS) + ADAM_WD * w_ref[...])
        nm_ref[...] = mn
        nv_ref[...] = vn

    blk = pl.BlockSpec((tr, tc), lambda i, j: (i, j))
    return pl.pallas_call(
        body, name=name, grid=(R // tr, C // tc),
        in_specs=[pl.BlockSpec(memory_space=pltpu.SMEM), blk][:n_own] + [pl.BlockSpec((P, tr, tc), lambda i, j: (0, i, j)), blk, blk, blk],
        out_specs=[blk] * 4, out_shape=[jax.ShapeDtypeStruct((R, C), F32)] * 4,
        compiler_params=_params(("parallel", "parallel")),
    )(*([me, own] if n_own else []), parts, w, m, v)


def _mesh_pos():
    return lax.axis_index("x"), lax.axis_index("y"), lax.axis_index("c")


def _peer(k):
    x, y, c = _mesh_pos()
    px, py, pc = x ^ ((k >> 2) & 1), y ^ ((k >> 1) & 1), c ^ (k & 1)
    return (px, py, pc), 4 * px + 2 * py + pc


def _exchange(arrays, scatter, *, name):
    n = len(arrays)
    blocks = [a.shape[1:] if scatter else a.shape for a in arrays]

    def body(*refs):
        srcs, dsts = refs[:n], refs[n:2 * n]
        send_sems, recv_sems, local_sems = refs[2 * n:]
        x, y, c = _mesh_pos()
        me = 4 * x + 2 * y + c
        local, sends = [], []
        for a in range(n):
            cp = pltpu.make_async_copy(srcs[a].at[me] if scatter else srcs[a], dsts[a].at[me], local_sems.at[a])
            cp.start()
            local.append(cp)
            for k in range(1, N_DEV):
                dev, idx = _peer(k)
                cp = pltpu.make_async_remote_copy(
                    src_ref=srcs[a].at[idx] if scatter else srcs[a], dst_ref=dsts[a].at[me],
                    send_sem=send_sems.at[a * N_DEV + k], recv_sem=recv_sems.at[a * N_DEV + k],
                    device_id=dev, device_id_type=MESH)
                cp.start()
                sends.append(cp)
        for a in range(n):
            for k in range(1, N_DEV):
                dev, idx = _peer(k)
                pltpu.make_async_remote_copy(
                    src_ref=srcs[a].at[idx] if scatter else srcs[a], dst_ref=dsts[a].at[idx],
                    send_sem=send_sems.at[a * N_DEV + k], recv_sem=recv_sems.at[a * N_DEV + k],
                    device_id=dev, device_id_type=MESH).wait_recv()
        for cp in sends:
            cp.wait_send()
        for cp in local:
            cp.wait()

    anyspec = pl.BlockSpec(memory_space=pl.ANY)
    return pl.pallas_call(
        body, name=name, in_specs=[anyspec] * n, out_specs=[anyspec] * n,
        out_shape=[jax.ShapeDtypeStruct((N_DEV,) + tuple(b), a.dtype) for a, b in zip(arrays, blocks)],
        scratch_shapes=[pltpu.SemaphoreType.DMA((n * N_DEV,)), pltpu.SemaphoreType.DMA((n * N_DEV,)),
                        pltpu.SemaphoreType.DMA((n,))],
    )(*arrays)


_ANY = pl.BlockSpec(memory_space=pl.ANY)
_SEM = pl.BlockSpec(memory_space=pltpu.SEMAPHORE)
_EFFECT = pltpu.SideEffectType.DATAFLOW_SIDE_EFFECTING


def _in_hbm(a):
    return pltpu.with_memory_space_constraint(a, pltpu.HBM)


def _split_copy(src, land, send, recv, k, me, scatter, landed):
    dev, idx = _peer(k)
    return pltpu.make_async_remote_copy(
        src_ref=src.at[idx] if scatter else src, dst_ref=land.at[idx if landed else me],
        send_sem=send.at[k], recv_sem=recv.at[k], device_id=dev, device_id_type=MESH)


ALL_PEERS = tuple(range(1, N_DEV))
SIBLING = 1
SAME_CORE = (2, 4, 6)


def _split_start(srcs, lands, scatter, *, name, relations=None):
    n = len(srcs)
    relations = relations or [ALL_PEERS] * n

    def body(*refs):
        src, land, send, recv, token = refs[:n], refs[n:2 * n], refs[2 * n:3 * n], refs[3 * n:4 * n], refs[-1]
        x, y, c = _mesh_pos()
        me = 4 * x + 2 * y + c
        for a in range(n):
            for k in relations[a]:
                _split_copy(src[a], land[a], send[a], recv[a], k, me, scatter, False).start()
        token[...] = jnp.zeros_like(token)

    outs = pl.pallas_call(
        body, name=name,
        out_shape=[pltpu.SemaphoreType.DMA((N_DEV,))] * (2 * n) + [pltpu.HBM(t.shape, t.dtype) for t in list(srcs) + list(lands)]
        + [jax.ShapeDtypeStruct((8, LANE), F32)],
        in_specs=[_ANY] * (2 * n), out_specs=[_SEM] * (2 * n) + [_ANY] * (2 * n) + [pl.BlockSpec(memory_space=pltpu.VMEM)],
        input_output_aliases={i: 2 * n + i for i in range(2 * n)},
        compiler_params=pltpu.CompilerParams(has_side_effects=_EFFECT),
    )(*[_in_hbm(t) for t in list(srcs) + list(lands)])
    handles = [(outs[a], outs[n + a], outs[2 * n + a], outs[3 * n + a]) for a in range(n)]
    return handles, outs[-1]


def _split_wait(handle, after, scatter, *, name):
    send, recv, src_thru, land_thru = handle

    def body(src_ref, land_ref, send_ref, recv_ref, after_ref, src_out, land_out):
        x, y, c = _mesh_pos()
        me = 4 * x + 2 * y + c
        for k in range(1, N_DEV):
            cp = _split_copy(src_ref, land_ref, send_ref, recv_ref, k, me, scatter, True)
            cp.wait_send()
            cp.wait_recv()

    return pl.pallas_call(
        body, name=name,
        out_shape=(pltpu.HBM(src_thru.shape, src_thru.dtype), pltpu.HBM(land_thru.shape, land_thru.dtype)),
        in_specs=(_ANY, _ANY, _SEM, _SEM, _ANY), out_specs=(_ANY, _ANY), input_output_aliases={0: 0, 1: 1},
        compiler_params=pltpu.CompilerParams(has_side_effects=_EFFECT),
    )(src_thru, land_thru, send, recv, after)[1]


def _forward_copy(land, fsend, frecv, k, landed):
    x, y, c = _mesh_pos()
    _, idx = _peer(k | SIBLING if landed else k)
    return pltpu.make_async_remote_copy(src_ref=land.at[idx], dst_ref=land.at[idx], send_sem=fsend.at[k],
                                        recv_sem=frecv.at[k], device_id=(x, y, 1 - c), device_id_type=MESH)


def _gather_forward(handle, after, *, name):
    send, recv, src_thru, land_thru = handle

    def body(src_ref, land_ref, send_ref, recv_ref, after_ref, src_out, land_out, fsend, frecv):
        x, y, c = _mesh_pos()
        me = 4 * x + 2 * y + c
        for k in SAME_CORE:
            _split_copy(src_ref, land_ref, send_ref, recv_ref, k, me, False, True).wait_recv()
            _forward_copy(land_ref, fsend, frecv, k, False).start()

    src2, land2, fsend, frecv = pl.pallas_call(
        body, name=name,
        out_shape=(pltpu.HBM(src_thru.shape, src_thru.dtype), pltpu.HBM(land_thru.shape, land_thru.dtype),
                   pltpu.SemaphoreType.DMA((N_DEV,)), pltpu.SemaphoreType.DMA((N_DEV,))),
        in_specs=(_ANY, _ANY, _SEM, _SEM, _ANY), out_specs=(_ANY, _ANY, _SEM, _SEM), input_output_aliases={0: 0, 1: 1},
        compiler_params=pltpu.CompilerParams(has_side_effects=_EFFECT),
    )(src_thru, land_thru, send, recv, after)
    return (send, recv, src2, land2), (fsend, frecv)


def _gather_wait_two_level(handle, fwd, *, name):
    send, recv, src_thru, land_thru = handle
    fsend, frecv = fwd

    def body(src_ref, land_ref, send_ref, recv_ref, fsend_ref, frecv_ref, src_out, land_out):
        x, y, c = _mesh_pos()
        me = 4 * x + 2 * y + c
        for k in (SIBLING,) + SAME_CORE:
            _split_copy(src_ref, land_ref, send_ref, recv_ref, k, me, False, True).wait_send()
        _split_copy(src_ref, land_ref, send_ref, recv_ref, SIBLING, me, False, True).wait_recv()
        for k in SAME_CORE:
            _forward_copy(land_ref, fsend_ref, frecv_ref, k, False).wait_send()
            _forward_copy(land_ref, fsend_ref, frecv_ref, k, True).wait_recv()

    return pl.pallas_call(
        body, name=name,
        out_shape=(pltpu.HBM(src_thru.shape, src_thru.dtype), pltpu.HBM(land_thru.shape, land_thru.dtype)),
        in_specs=(_ANY, _ANY, _SEM, _SEM, _SEM, _SEM), out_specs=(_ANY, _ANY), input_output_aliases={0: 0, 1: 1},
        compiler_params=pltpu.CompilerParams(has_side_effects=_EFFECT),
    )(src_thru, land_thru, send, recv, fsend, frecv)[1]


def _local_step(x, p, tgt, S, wt, conv, emit):
    T, D = x.shape
    CW = DNW = D // 2
    H = DNW // HEAD
    nA, nD = CW // LANE, DNW // LANE
    qkv_off, z_off, ab_off = 3 * nA, 3 * nA + 3 * nD, 3 * nA + 4 * nD
    alog = jnp.pad(S["a_log"], ((0, 0), (0, LANE - H)))
    dtb = jnp.pad(S["dt_bias"], ((0, 0), (0, LANE - H)))

    h1 = _rms_fwd(x, S["g_mix"], name="rms1_fwd")
    w_in, cv = wt("w_in", h1), conv(h1)
    proj = _matmul(h1, w_in, "nt", name="mm_in")
    y_a = _group_a_fwd(proj, cv["conv_a"], CW, name="group_a_fwd")
    qkv = _qkv_fwd(proj, cv["conv_qkv"], qkv_off, H, name="qkv_fwd")
    gb, gamc = _gates_fwd(proj, alog, dtb, ab_off, H, name="gates_fwd")
    bcast = lambda cols: jnp.broadcast_to(cols.T[:, :, None], (H, T, LANE))
    gamB, bB = bcast(gamc[:, :H]), bcast(gb[:, H:2 * H])
    u, w, qd, kd, qk, ti, gl = _delta_prep_fwd(qkv, gamB, bB, H, name="delta_prep_fwd")
    o, vn, ss = _delta_scan_fwd(u, w, qd, kd, qk, gl, H, name="delta_scan_fwd")
    y_b = _gated_norm_fwd(o, proj, S["dn_g"], z_off, name="gated_norm_fwd")
    ycat = jnp.concatenate([y_a, y_b], axis=1)
    w_out = wt("w_out", ycat)
    rows = dict(tm=ROW_TILE, tn=D)
    x1, h2 = _matmul(ycat, w_out, "nn", name="mm_out", out_dtypes=(F32, BF16), epilogue=_epi_residual_rms,
                     extras=(x,), vec_extras=(S["g_ffn"],), **rows)
    w_up = wt("w_up", h2)
    up_pre = _matmul(h2, w_up, "nn", name="mm_up", b_shards=True, tn=SHARD_TILE)
    act = _ffn_act_fwd(up_pre, cv["conv_ffn"], name="ffn_act_fwd")
    w_down = wt("w_down", act)
    x2 = _matmul(act, w_down, "nn", name="mm_down", epilogue=lambda acc, r: (acc + r,), extras=(x1,))
    h3 = _rms_fwd(x2, S["g_ple"], name="rms3_fwd")
    w_pp, w_pg = wt("w_pp", h3), wt("w_pg", h3)
    pp = _matmul(p, w_pp, "nn", name="mm_pp", b_shards=True)

    def ple_epi(acc, x2r, ppr):
        s = jax.nn.sigmoid(acc)
        return x2r + s * ppr, s

    x3, sg = _matmul(h3, w_pg, "nn", name="mm_pg", out_dtypes=(F32, F32), epilogue=ple_epi, extras=(x2, pp), tm=512)
    dx3, dg_final, loss = _final_loss(x3, S["g_final"], tgt, name="final_loss")

    G = {"g_final": dg_final}
    dpg, dpp = _ple_bwd(dx3, pp, sg, name="ple_bwd")
    tok = emit({"w_pp": _matmul(p, dpp, "tn", name="mm_dwpp", out_dtypes=(BF16,), out_shards=True),
                "w_pg": _matmul(h3, dpg, "tn", name="mm_dwpg", out_dtypes=(BF16,))})
    bwd = dict(out_dtypes=(F32, BF16), epilogue=_epi_rms_bwd(2), n_vec=1, **rows)
    dx2, dx2b, G["g_ple"] = _matmul(dpg, w_pg, "nt", name="mm_dh3", after=tok, extras=(x2, dx3),
                                    vec_extras=(S["g_ple"],), **bwd)
    tok = emit({"w_down": _matmul(act, dx2b, "tn", name="mm_dwdown", out_dtypes=(BF16,))})
    dact = _matmul(dx2b, w_down, "nt", name="mm_dact", after=tok)
    dup_g, dup_v, dcf_g, dcf_v = _ffn_act_bwd(up_pre, cv["conv_ffn"], dact, name="ffn_act_bwd")
    G["conv_ffn"] = jnp.concatenate([dcf_g, dcf_v], axis=1)
    dup = jnp.concatenate([dup_g, dup_v], axis=1)
    tok = emit({"w_up": _matmul(h2, dup, "tn", name="mm_dwup", out_dtypes=(BF16,), out_shards=True, tn=SHARD_TILE)})
    dh2 = _matmul(dup, w_up, "nt", name="mm_dh2", after=tok, b_shards=True, tk=SHARD_TILE)
    dx1, dx1b, G["g_ffn"] = _rms_bwd(x1, S["g_ffn"], dh2, dx2, name="rms2_bwd")
    tok = emit({"w_out": _matmul(ycat, dx1b, "tn", name="mm_dwout", out_dtypes=(BF16,))})
    dycat = _matmul(dx1b, w_out, "nt", name="mm_dycat", after=tok)
    do, dz, G["dn_g"] = _gated_norm_bwd(o, proj, S["dn_g"], dycat, z_off, nA, name="gated_norm_bwd")
    du, dw, dqd, dkd, dqk, dgl = _delta_scan_bwd(do, w, qd, kd, vn, qk, gl, ss, H, name="delta_scan_bwd")
    dq, dk, dv, dgB, dbB = _delta_prep_bwd(qkv, gamB, bB, ti, u, w, qk, du, dw, dqd, dkd, dqk, dgl, H,
                                           name="delta_prep_bwd")
    dgb = jnp.pad(jnp.concatenate([dgB[:, :, 0].T, dbB[:, :, 0].T], axis=1), ((0, 0), (0, LANE - 2 * H)))
    dab, dal, ddt = _gates_bwd(proj, alog, dtb, dgb, ab_off, H, name="gates_bwd")
    G["a_log"], G["dt_bias"] = dal[:, :H], ddt[:, :H]
    dqkv, G["conv_qkv"] = _qkv_bwd(proj, cv["conv_qkv"], dq, dk, dv, qkv_off, H, name="qkv_bwd")
    dax, dab_, dac, G["conv_a"] = _group_a_bwd(proj, cv["conv_a"], dycat, CW, name="group_a_bwd")
    in_p = w_in.shape[0]
    dproj = jnp.concatenate([dax, dab_, dac, dqkv, dz, dab, jnp.zeros((T, in_p - (ab_off + 1) * LANE), BF16)], axis=1)
    tok = emit({"w_in": _matmul(dproj, h1, "tn", name="mm_dwin", out_dtypes=(BF16,))})
    dh1 = _matmul(dproj, w_in, "nn", name="mm_dh1", after=tok)
    grad_x, _, G["g_mix"] = _rms_bwd(x, S["g_mix"], dh1, dx1, name="rms1_bwd")
    return loss, grad_x, G


def _col_sharded(landed):
    _, R, C = landed.shape
    return jnp.transpose(landed, (1, 0, 2)).reshape(R, N_DEV * C)


def kernel(x, p, norm_mix_g, w_in, conv_a_w, conv_qkv_w, a_log, dt_bias, dn_norm_g, w_out, norm_ffn_g, w_up, conv_ffn_w, w_down, norm_ple_g, w_ple_gate, w_ple_proj, final_norm_g, loss_target, m_norm_mix_g, m_w_in, m_conv_a_w, m_conv_qkv_w, m_a_log, m_dt_bias, m_dn_norm_g, m_w_out, m_norm_ffn_g, m_w_up, m_conv_ffn_w, m_w_down, m_norm_ple_g, m_w_ple_gate, m_w_ple_proj, m_final_norm_g, v_norm_mix_g, v_w_in, v_conv_a_w, v_conv_qkv_w, v_a_log, v_dt_bias, v_dn_norm_g, v_w_out, v_norm_ffn_g, v_w_up, v_conv_ffn_w, v_w_down, v_norm_ple_g, v_w_ple_gate, v_w_ple_proj, v_final_norm_g):
    T, D = x.shape[1], x.shape[2]
    xd, _, cd = _mesh_pos()
    me = 4 * xd + 2 * lax.axis_index("y") + cd

    conv_sh = [conv_a_w[0], conv_qkv_w[0], conv_ffn_w[0]]
    conv_n = [c.size for c in conv_sh]
    pack_rows = -(-sum(conv_n) // LANE)
    conv_pack = jnp.pad(jnp.concatenate([c.reshape(-1) for c in conv_sh]), (0, pack_rows * LANE - sum(conv_n))).reshape(pack_rows, LANE)
    names = ["w_in", "conv", "w_out", "w_up", "w_down", "w_pg", "w_pp"]
    tr_ = lambda t: jnp.swapaxes(t, 1, 2)
    shards = [w_in[0].T.astype(BF16), conv_pack, w_out[0].astype(BF16), w_up[0].astype(BF16), w_down[0].astype(BF16),
              w_ple_gate[0].astype(BF16), w_ple_proj[0].astype(BF16)]
    empty_slots = lambda blocks: [lax.empty((N_DEV,) + tuple(b.shape), b.dtype) for b in blocks]
    handles, tok0 = _split_start(shards, empty_slots(shards), False, name="gather_start",
                                 relations=[(SIBLING,) + SAME_CORE] + [ALL_PEERS] * (len(shards) - 1))
    handle = dict(zip(names, handles))
    own = dict(zip(names, shards))
    in_cols = N_DEV * w_in.shape[2]
    in_p = (in_cols // LANE) * LANE + AB_PAD
    in_place = {"w_up", "w_pp"}

    def gathered(name, after):
        if name == "w_in":
            passed, fwd = _gather_forward(handle[name], after, name="gather_forward_w_in")
            landed = _gather_wait_two_level(passed, fwd, name="gather_wait_w_in")
        else:
            landed = _split_wait(handle[name], after, False, name="gather_wait_" + name)
        return lax.dynamic_update_index_in_dim(landed, own[name], me, 0)

    def wt(name, after):
        landed = gathered(name, after)
        if name in in_place:
            return landed
        full = landed.reshape(-1, D)
        return jnp.pad(full, ((0, in_p - in_cols), (0, 0))) if name == "w_in" else full

    def conv(after):
        flat = gathered("conv", after).reshape(N_DEV, pack_rows * LANE)
        out, o_ = {}, 0
        for nm, c, n_ in zip(("conv_a", "conv_qkv", "conv_ffn"), conv_sh, conv_n):
            out[nm] = _col_sharded(flat[:, o_:o_ + n_].reshape((N_DEV,) + c.shape))
            o_ += n_
        return out

    pending, mine = {}, {}

    def emit(grads):
        parts = [g if nm in in_place else (g[:in_cols] if nm == "w_in" else g).reshape(N_DEV, -1, D)
                 for nm, g in grads.items()]
        hs, tok = _split_start(parts, empty_slots([q[0] for q in parts]), True, name="scatter_start_" + "_".join(grads))
        pending.update(zip(grads, hs))
        mine.update({nm: lax.dynamic_index_in_dim(q, me, 0, keepdims=False) for nm, q in zip(grads, parts)})
        return tok

    S = {
        "g_mix": norm_mix_g + tok0[0, 0], "a_log": a_log, "dt_bias": dt_bias, "dn_g": dn_norm_g, "g_ffn": norm_ffn_g,
        "g_ple": norm_ple_g, "g_final": final_norm_g.reshape(1, D),
    }

    loss_v, grad_x, G = _local_step(x[0], p[0, 0], loss_target[0], S, wt, conv, emit)
    loss = lax.psum(loss_v[0, 0], ("x", "y", "c"))

    small_names = ["g_mix", "g_ffn", "g_ple", "g_final", "dn_g", "a_log", "dt_bias", "conv_a", "conv_qkv", "conv_ffn"]
    small_rows, pieces = [], []
    for nm in small_names:
        g_ = G[nm].reshape(-1)
        r_ = -(-g_.size // (8 * LANE)) * 8
        small_rows.append(r_)
        pieces.append(jnp.pad(g_, (0, r_ * LANE - g_.size)).reshape(r_, LANE))
    (small_l,) = _exchange([jnp.concatenate(pieces, axis=0)], False, name="gather_small_grads")
    landed = {nm: _split_wait(h_, grad_x, True, name="scatter_wait_" + nm) for nm, h_ in pending.items()}
    big_l = [landed[nm] for nm in ("w_in", "w_out", "w_up", "w_down", "w_pg", "w_pp")]

    def small_parts(nm):
        i = small_names.index(nm)
        r0 = sum(small_rows[:i])
        shp = G[nm].shape
        return small_l[:, r0:r0 + small_rows[i], :].reshape(N_DEV, -1)[:, :G[nm].size].reshape((N_DEV,) + shp)

    def conv_parts(nm, shard):
        full = small_parts(nm)
        C = shard.shape[-1]
        return lax.dynamic_slice_in_dim(full, me * C, C, axis=2)

    def adam(parts, w_, m_, v_, nm, own_=None):
        shp = w_.shape
        w2, m2, v2 = (t.reshape(parts.shape[1:]) for t in (w_, m_, v_))
        kw = {} if own_ is None else {"own": own_, "me": me.astype(jnp.int32).reshape(1)}
        return tuple(t.reshape(shp) for t in _adam(parts, w2, m2, v2, name="adam_" + nm, **kw))

    res = [
        adam(small_parts("g_mix"), norm_mix_g, m_norm_mix_g, v_norm_mix_g, "norm_mix_g"),
        tuple(tr_(t) for t in adam(big_l[0], tr_(w_in), tr_(m_w_in), tr_(v_w_in), "w_in", mine["w_in"])),
        adam(conv_parts("conv_a", conv_a_w), conv_a_w, m_conv_a_w, v_conv_a_w, "conv_a_w"),
        adam(conv_parts("conv_qkv", conv_qkv_w), conv_qkv_w, m_conv_qkv_w, v_conv_qkv_w, "conv_qkv_w"),
        adam(small_parts("a_log"), a_log, m_a_log, v_a_log, "a_log"),
        adam(small_parts("dt_bias"), dt_bias, m_dt_bias, v_dt_bias, "dt_bias"),
        adam(small_parts("dn_g"), dn_norm_g, m_dn_norm_g, v_dn_norm_g, "dn_norm_g"),
        adam(big_l[1], w_out, m_w_out, v_w_out, "w_out", mine["w_out"]),
        adam(small_parts("g_ffn"), norm_ffn_g, m_norm_ffn_g, v_norm_ffn_g, "norm_ffn_g"),
        adam(big_l[2], w_up, m_w_up, v_w_up, "w_up", mine["w_up"]),
        adam(conv_parts("conv_ffn", conv_ffn_w), conv_ffn_w, m_conv_ffn_w, v_conv_ffn_w, "conv_ffn_w"),
        adam(big_l[3], w_down, m_w_down, v_w_down, "w_down", mine["w_down"]),
        adam(small_parts("g_ple"), norm_ple_g, m_norm_ple_g, v_norm_ple_g, "norm_ple_g"),
        adam(big_l[4], w_ple_gate, m_w_ple_gate, v_w_ple_gate, "w_ple_gate", mine["w_pg"]),
        adam(big_l[5], w_ple_proj, m_w_ple_proj, v_w_ple_proj, "w_ple_proj", mine["w_pp"]),
        adam(small_parts("g_final"), final_norm_g.reshape(1, D), m_final_norm_g.reshape(1, D),
             v_final_norm_g.reshape(1, D), "final_norm_g"),
    ]
    res[-1] = tuple(t.reshape(D) for t in res[-1])
    grads, deltas, new_m, new_v = zip(*res)
    return (loss, grad_x[None], *grads, *deltas, *new_m, *new_v)
```

```python
import functools

import jax
import jax.numpy as jnp
from jax import lax
from jax.experimental import pallas as pl
from jax.experimental.pallas import tpu as pltpu

F32 = jnp.float32
BF16 = jnp.bfloat16

EPS = 1e-6
CHUNK = 64
HEAD = 128
LANE = 128
N_DEV = 8
AB_PAD = 512

ADAM_LR = 0.001
ADAM_B1 = 0.9
ADAM_B2 = 0.999
ADAM_EPS = 1e-08
ADAM_WD = 0.01
ADAM_STEP = 10

MESH = pl.DeviceIdType.MESH


def _tile(dim, target, align=LANE):
    if dim <= target:
        return dim
    t = (target // align) * align
    while t > align and dim % t:
        t -= align
    assert dim % t == 0, (dim, target)
    return t


def _params(sem, vmem_mb=48):
    return pltpu.CompilerParams(dimension_semantics=sem, vmem_limit_bytes=vmem_mb << 20)


_DN = {"nn": (((1,), (0,)), ((), ())), "nt": (((1,), (1,)), ((), ())), "tn": (((0,), (0,)), ((), ()))}
SHARD_TILE = 1408


def _matmul(a, b, mode, *, name, out_dtypes=(F32,), epilogue=None, extras=(), vec_extras=(), n_vec=0, after=None,
            b_shards=False, out_shards=False, tm=1024, tn=1024, tk=2048):
    shard_w = b.shape[2] if b_shards else None
    if b_shards:
        b_rows, b_cols = b.shape[1], N_DEV * shard_w
    else:
        b_rows, b_cols = b.shape
    if mode == "nn":
        (M, K), (K2, N) = a.shape, (b_rows, b_cols)
    elif mode == "nt":
        (M, K), (N, K2) = a.shape, (b_rows, b_cols)
    else:
        (K, M), (K2, N) = a.shape, (b_rows, b_cols)
    assert K == K2, (name, a.shape, b.shape)
    tm = _tile(M, tm)
    tn = _tile(shard_w if (b_shards and mode == "nn") else N // N_DEV if out_shards else N, tn)
    tk = _tile(shard_w if (b_shards and mode == "nt") else K, tk)
    nk = K // tk
    n_ex, n_out = len(extras) + len(vec_extras), len(out_dtypes)
    assert n_vec == 0 or tn == N, (name, tn, N)
    dn = _DN[mode]

    n_tok = 0 if after is None else 1

    def body(a_ref, b_ref, *rest):
        rest = rest[n_tok:]
        ex_refs, out_refs, vec_refs = rest[:n_ex], rest[n_ex:n_ex + n_out], rest[n_ex + n_out:n_ex + n_out + n_vec]
        part = lax.dot_general(a_ref[...].astype(BF16), b_ref[...].astype(BF16), dn, preferred_element_type=F32)
        first_rows = pl.program_id(0) == 0

        def finish(res):
            outs = (res,) if epilogue is None else epilogue(res, *[e[...] for e in ex_refs])
            for o_ref, val in zip(out_refs, outs[:n_out]):
                o_ref[...] = val.astype(o_ref.dtype)
            for v_ref, val in zip(vec_refs, outs[n_out:]):
                @pl.when(first_rows)
                def _(v_ref=v_ref, val=val):
                    v_ref[...] = val

                @pl.when(jnp.logical_not(first_rows))
                def _(v_ref=v_ref, val=val):
                    v_ref[...] += val

        if nk == 1:
            finish(part)
            return
        acc, k = rest[-1], pl.program_id(2)

        @pl.when(k == 0)
        def _():
            acc[...] = part

        @pl.when(k > 0)
        def _():
            acc[...] += part

        @pl.when(k == nk - 1)
        def _():
            finish(acc[...])

    a_spec = pl.BlockSpec((tk, tm), lambda i, j, k: (k, i)) if mode == "tn" else pl.BlockSpec((tm, tk), lambda i, j, k: (i, k))
    if b_shards and mode == "nn":
        per = shard_w // tn
        b_spec = pl.BlockSpec((None, tk, tn), lambda i, j, k: (lax.div(j, per), k, lax.rem(j, per)))
    elif b_shards:
        per = shard_w // tk
        b_spec = pl.BlockSpec((None, tn, tk), lambda i, j, k: (lax.div(k, per), j, lax.rem(k, per)))
    else:
        b_spec = pl.BlockSpec((tn, tk), lambda i, j, k: (j, k)) if mode == "nt" else pl.BlockSpec((tk, tn), lambda i, j, k: (k, j))
    mn_spec = pl.BlockSpec((tm, tn), lambda i, j, k: (i, j))
    vec_spec = pl.BlockSpec((1, tn), lambda i, j, k: (0, j))
    if out_shards:
        assert not extras
        per_o = (N // N_DEV) // tn
        out_spec = pl.BlockSpec((None, tm, tn), lambda i, j, k: (lax.div(j, per_o), i, lax.rem(j, per_o)))
        out_dims = (N_DEV, M, N // N_DEV)
    else:
        out_spec, out_dims = mn_spec, (M, N)
    outs = pl.pallas_call(
        body, name=name, grid=(M // tm, N // tn, nk),
        in_specs=[a_spec, b_spec] + [pl.BlockSpec((8, LANE), lambda i, j, k: (0, 0))] * n_tok
        + [mn_spec] * len(extras) + [vec_spec] * len(vec_extras),
        out_specs=[out_spec] * n_out + [vec_spec] * n_vec,
        out_shape=[jax.ShapeDtypeStruct(out_dims, dt) for dt in out_dtypes] + [jax.ShapeDtypeStruct((1, N), F32)] * n_vec,
        scratch_shapes=[pltpu.VMEM((tm, tn), F32)] if nk > 1 else [],
        compiler_params=_params(("arbitrary" if n_vec else "parallel", "parallel", "arbitrary"), 56),
    )(a, b, *([] if after is None else [after]), *extras, *vec_extras)
    return outs[0] if n_out + n_vec == 1 else outs


def _rms_fwd(x, g, *, name):
    T, D = x.shape
    tr = _tile(T, 256, 8)

    def body(x_ref, g_ref, h_ref):
        xv = x_ref[...]
        r = lax.rsqrt(jnp.mean(xv * xv, axis=-1, keepdims=True) + EPS)
        h_ref[...] = (xv * r * g_ref[...]).astype(h_ref.dtype)

    return pl.pallas_call(
        body, name=name, grid=(T // tr,),
        in_specs=[pl.BlockSpec((tr, D), lambda i: (i, 0)), pl.BlockSpec((1, D), lambda i: (0, 0))],
        out_specs=pl.BlockSpec((tr, D), lambda i: (i, 0)),
        out_shape=jax.ShapeDtypeStruct((T, D), BF16),
        compiler_params=_params(("parallel",)),
    )(x, g)


def _rms_bwd(x, g, dh, dres, *, name):
    T, D = x.shape
    tr = _tile(T, 256, 8)
    epi = _epi_rms_bwd(2)

    def body(x_ref, g_ref, dh_ref, dres_ref, dx_ref, dxb_ref, dg_ref):
        dx, _, dgp = epi(dh_ref[...], x_ref[...], dres_ref[...], g_ref[...])

        @pl.when(pl.program_id(0) == 0)
        def _():
            dg_ref[...] = jnp.zeros_like(dg_ref)

        dg_ref[...] += dgp
        dx_ref[...] = dx
        dxb_ref[...] = dx.astype(dxb_ref.dtype)

    row = pl.BlockSpec((tr, D), lambda i: (i, 0))
    vec = pl.BlockSpec((1, D), lambda i: (0, 0))
    return pl.pallas_call(
        body, name=name, grid=(T // tr,),
        in_specs=[row, vec, row, row], out_specs=[row, row, vec],
        out_shape=[jax.ShapeDtypeStruct((T, D), F32), jax.ShapeDtypeStruct((T, D), BF16), jax.ShapeDtypeStruct((1, D), F32)],
        compiler_params=_params(("arbitrary",)),
    )(x, g, dh, dres)


ROW_TILE = 256


def _epi_residual_rms(acc, res, g):
    xn = acc + res
    r = lax.rsqrt(jnp.mean(xn * xn, axis=-1, keepdims=True) + EPS)
    return xn, xn * r * g


def _epi_rms_bwd(n_copies):
    def epi(dh, x, dres, g):
        r = lax.rsqrt(jnp.mean(x * x, axis=-1, keepdims=True) + EPS)
        xh = x * r
        dxh = dh * g
        dx = dres + r * (dxh - xh * jnp.mean(dxh * xh, axis=-1, keepdims=True))
        return (dx,) * n_copies + (jnp.sum(dh * xh, axis=0, keepdims=True),)
    return epi


def _final_loss(x, g, tgt, *, name):
    T, D = x.shape
    tr = _tile(T, 256, 8)

    def body(x_ref, g_ref, t_ref, dx_ref, dg_ref, loss_ref):
        xv = x_ref[...]
        r = lax.rsqrt(jnp.mean(xv * xv, axis=-1, keepdims=True) + EPS)
        xh = xv * r
        gv = g_ref[...]
        err = xh * gv - t_ref[...]

        @pl.when(pl.program_id(0) == 0)
        def _():
            dg_ref[...] = jnp.zeros_like(dg_ref)
            loss_ref[...] = jnp.zeros_like(loss_ref)

        part = 0.5 * jnp.sum(jnp.mean(err * err, axis=-1, keepdims=True), axis=0, keepdims=True)
        loss_ref[...] += jnp.broadcast_to(part, loss_ref.shape)
        dy = err * (1.0 / D)
        dg_ref[...] += jnp.sum(dy * xh, axis=0, keepdims=True)
        dxh = dy * gv
        dx_ref[...] = r * (dxh - xh * jnp.mean(dxh * xh, axis=-1, keepdims=True))

    row = pl.BlockSpec((tr, D), lambda i: (i, 0))
    vec = pl.BlockSpec((1, D), lambda i: (0, 0))
    return pl.pallas_call(
        body, name=name, grid=(T // tr,),
        in_specs=[row, vec, row], out_specs=[row, vec, pl.BlockSpec((1, LANE), lambda i: (0, 0))],
        out_shape=[jax.ShapeDtypeStruct((T, D), F32), jax.ShapeDtypeStruct((1, D), F32),
                   jax.ShapeDtypeStruct((1, LANE), F32)],
        compiler_params=_params(("arbitrary",)),
    )(x, g, tgt)


def _ple_bwd(dx3, pp, sg, *, name):
    T, D = dx3.shape
    tr = _tile(T, 256, 8)

    def body(dx_ref, pp_ref, sg_ref, dpg_ref, dpp_ref):
        dx, s = dx_ref[...], sg_ref[...]
        dpg_ref[...] = (dx * pp_ref[...] * s * (1.0 - s)).astype(dpg_ref.dtype)
        dpp_ref[...] = (dx * s).astype(dpp_ref.dtype)

    row = pl.BlockSpec((tr, D), lambda i: (i, 0))
    return pl.pallas_call(
        body, name=name, grid=(T // tr,), in_specs=[row, row, row], out_specs=[row, row],
        out_shape=[jax.ShapeDtypeStruct((T, D), BF16)] * 2, compiler_params=_params(("parallel",)),
    )(dx3, pp, sg)


ROWS_QKV_FWD, ROWS_QKV_BWD, ROWS_FFN_FWD, ROWS_FFN_BWD, ROWS_GROUP_A = 512, 256, 256, 128, 256


def _ext(ref, r0, T, before, after, RC):
    parts = []
    if before:
        p0 = pl.multiple_of(jnp.maximum(r0 - 8, 0), 8)
        parts.append(jnp.where(r0 > 0, ref[pl.ds(p0, 8), :], 0.0))
    parts.append(ref[pl.ds(r0, RC), :])
    if after:
        n0 = pl.multiple_of(jnp.minimum(r0 + RC, T - 8), 8)
        parts.append(jnp.where(r0 + RC < T, ref[pl.ds(n0, 8), :], 0.0))
    return parts[0] if len(parts) == 1 else jnp.concatenate(parts, axis=0)


def _down(xx, s):
    return (xx if s == 0 else pltpu.roll(xx, s, 0))[8:, :]


def _up(xx, s, rows):
    return (xx if s == 0 else pltpu.roll(xx, xx.shape[0] - s, 0))[:rows, :]


def _conv_down(xx, w_ref, K):
    y = None
    for j in range(K):
        t = _down(xx, K - 1 - j) * w_ref[j:j + 1, :]
        y = t if y is None else y + t
    return y


def _fold8(x):
    return jnp.sum(x.reshape(x.shape[0] // 8, 8, x.shape[1]), axis=0)


def _silu(x):
    return x * jax.nn.sigmoid(x)


def _dsilu(x):
    s = jax.nn.sigmoid(x)
    return s * (1.0 + x * (1.0 - s))


def _col_specs(T, offs):
    return [pl.BlockSpec((T, LANE), functools.partial(lambda o, j: (0, o + j), o)) for o in offs]


def _group_a_fwd(proj, conv_w, CW, *, name):
    T = proj.shape[0]
    RC = _tile(T, ROWS_GROUP_A, 8)
    nb = CW // LANE
    K = conv_w.shape[0]

    def body(ax_ref, ab_ref, ac_ref, w_ref, y_ref):
        def step(i, carry):
            r0 = pl.multiple_of(i * RC, RC)
            m = _ext(ac_ref, r0, T, True, False, RC) * _ext(ax_ref, r0, T, True, False, RC)
            y_ref[pl.ds(r0, RC), :] = (ab_ref[pl.ds(r0, RC), :] * _conv_down(m, w_ref, K)).astype(y_ref.dtype)
            return carry
        lax.fori_loop(0, T // RC, step, 0)

    return pl.pallas_call(
        body, name=name, grid=(nb,),
        in_specs=_col_specs(T, (0, nb, 2 * nb)) + [pl.BlockSpec((K, LANE), lambda j: (0, j))],
        out_specs=pl.BlockSpec((T, LANE), lambda j: (0, j)),
        out_shape=jax.ShapeDtypeStruct((T, CW), BF16), compiler_params=_params(("parallel",)),
    )(proj, proj, proj, conv_w)


def _group_a_bwd(proj, conv_w, dycat, CW, *, name):
    T = proj.shape[0]
    RC = _tile(T, ROWS_GROUP_A, 8)
    nb = CW // LANE
    K = conv_w.shape[0]

    def body(ax_ref, ab_ref, ac_ref, w_ref, dy_ref, dax_ref, dab_ref, dac_ref, dw_ref):
        def step(i, accs):
            r0 = pl.multiple_of(i * RC, RC)
            ax3 = _ext(ax_ref, r0, T, True, True, RC)
            ac3 = _ext(ac_ref, r0, T, True, True, RC)
            m3 = ax3 * ac3
            c = _conv_down(m3[:RC + 8], w_ref, K)
            dy = dy_ref[pl.ds(r0, RC), :]
            dab_ref[pl.ds(r0, RC), :] = (dy * c).astype(dab_ref.dtype)
            dc2 = _ext(dy_ref, r0, T, False, True, RC) * _ext(ab_ref, r0, T, False, True, RC)
            dm = None
            new = []
            for j in range(K):
                s = K - 1 - j
                t = _up(dc2, s, RC) * w_ref[j:j + 1, :]
                dm = t if dm is None else dm + t
                new.append(accs[j] + _fold8(dc2[:RC] * _down(m3[:RC + 8], s)))
            dax_ref[pl.ds(r0, RC), :] = (dm * ac3[8:RC + 8]).astype(dax_ref.dtype)
            dac_ref[pl.ds(r0, RC), :] = (dm * ax3[8:RC + 8]).astype(dac_ref.dtype)
            return tuple(new)

        accs = lax.fori_loop(0, T // RC, step, tuple(jnp.zeros((8, LANE), F32) for _ in range(K)))
        for j in range(K):
            dw_ref[j:j + 1, :] = jnp.sum(accs[j], axis=0, keepdims=True)

    col = pl.BlockSpec((T, LANE), lambda j: (0, j))
    wsp = pl.BlockSpec((K, LANE), lambda j: (0, j))
    return pl.pallas_call(
        body, name=name, grid=(nb,),
        in_specs=_col_specs(T, (0, nb, 2 * nb)) + [wsp, col],
        out_specs=[col, col, col, wsp],
        out_shape=[jax.ShapeDtypeStruct((T, CW), BF16)] * 3 + [jax.ShapeDtypeStruct((K, CW), F32)],
        compiler_params=_params(("parallel",)),
    )(proj, proj, proj, conv_w, dycat)


def _qkv_fwd(proj, conv_w, off, H, *, name):
    T = proj.shape[0]
    RC = _tile(T, ROWS_QKV_FWD, 8)
    nb = 3 * H
    K = conv_w.shape[0]

    def body(x_ref, w_ref, y_ref):
        j = pl.program_id(0)
        is_qk = j < 2 * H
        scale = jnp.where(j < H, HEAD ** -0.5, 1.0).astype(F32)

        def step(i, carry):
            r0 = pl.multiple_of(i * RC, RC)
            s = _silu(_conv_down(_ext(x_ref, r0, T, True, False, RC), w_ref, K))
            r = lax.rsqrt(jnp.sum(s * s, axis=-1, keepdims=True) + EPS) * scale
            y_ref[pl.ds(r0, RC), :] = s * jnp.where(is_qk, r, 1.0)
            return carry
        lax.fori_loop(0, T // RC, step, 0)

    return pl.pallas_call(
        body, name=name, grid=(nb,),
        in_specs=_col_specs(T, (off,)) + [pl.BlockSpec((K, LANE), lambda j: (0, j))],
        out_specs=pl.BlockSpec((T, LANE), lambda j: (0, j)),
        out_shape=jax.ShapeDtypeStruct((T, nb * LANE), F32), compiler_params=_params(("parallel",)),
    )(proj, conv_w)


def _qkv_bwd(proj, conv_w, dq, dk, dv, off, H, *, name):
    T = proj.shape[0]
    RC = _tile(T, ROWS_QKV_BWD, 8)
    nb = 3 * H
    K = conv_w.shape[0]

    def body(x_ref, w_ref, dq_ref, dk_ref, dv_ref, dx_ref, dw_ref):
        j = pl.program_id(0)
        is_qk = j < 2 * H
        scale = jnp.where(j < H, HEAD ** -0.5, 1.0).astype(F32)

        def step(i, accs):
            r0 = pl.multiple_of(i * RC, RC)
            x3 = _ext(x_ref, r0, T, True, True, RC)
            c2 = _conv_down(x3, w_ref, K)
            s2 = _silu(c2)
            dn2 = jnp.where(j < H, _ext(dq_ref, r0, T, False, True, RC),
                            jnp.where(is_qk, _ext(dk_ref, r0, T, False, True, RC), _ext(dv_ref, r0, T, False, True, RC)))
            r = lax.rsqrt(jnp.sum(s2 * s2, axis=-1, keepdims=True) + EPS)
            nh = s2 * r
            dnp = dn2 * scale
            ds_qk = r * (dnp - nh * jnp.sum(dnp * nh, axis=-1, keepdims=True))
            ds2 = jnp.where(is_qk, ds_qk, dn2)
            dc2 = ds2 * _dsilu(c2)
            dx = None
            new = []
            for jj in range(K):
                s = K - 1 - jj
                t = _up(dc2, s, RC) * w_ref[jj:jj + 1, :]
                dx = t if dx is None else dx + t
                new.append(accs[jj] + _fold8(dc2[:RC] * _down(x3[:RC + 8], s)))
            dx_ref[pl.ds(r0, RC), :] = dx.astype(dx_ref.dtype)
            return tuple(new)

        accs = lax.fori_loop(0, T // RC, step, tuple(jnp.zeros((8, LANE), F32) for _ in range(K)))
        for jj in range(K):
            dw_ref[jj:jj + 1, :] = jnp.sum(accs[jj], axis=0, keepdims=True)

    col = pl.BlockSpec((T, LANE), lambda j: (0, j))
    wsp = pl.BlockSpec((K, LANE), lambda j: (0, j))
    return pl.pallas_call(
        body, name=name, grid=(nb,),
        in_specs=_col_specs(T, (off,)) + [wsp] + [
            pl.BlockSpec((T, LANE), functools.partial(lambda o, j: (0, jnp.clip(j - o, 0, H - 1)), o)) for o in (0, H, 2 * H)],
        out_specs=[col, wsp],
        out_shape=[jax.ShapeDtypeStruct((T, nb * LANE), BF16), jax.ShapeDtypeStruct((K, nb * LANE), F32)],
        compiler_params=_params(("parallel",)),
    )(proj, conv_w, dq, dk, dv)


def _softplus(x):
    return jnp.maximum(x, 0.0) + jnp.log(1.0 + jnp.exp(-jnp.abs(x)))


def _gates_fwd(proj, alog, dtb, off, H, *, name):
    T = proj.shape[0]
    tr = _tile(T, 512, CHUNK)

    def body(ab_ref, al_ref, dt_ref, gb_ref, gam_ref):
        ab = ab_ref[...]
        lane = lax.broadcasted_iota(jnp.int32, ab.shape, 1)
        g = -jnp.exp(al_ref[...]) * _softplus(ab + dt_ref[...])
        gb = jnp.where(lane < H, g, jnp.where(lane < 2 * H, jax.nn.sigmoid(ab), 0.0))
        gb_ref[...] = gb
        tril = _tri().astype(F32)
        for c in range(tr // CHUNK):
            rows = slice(c * CHUNK, (c + 1) * CHUNK)
            gam_ref[rows, :] = _mm(tril, gb[rows, :], precision=lax.Precision.HIGHEST)

    vec = pl.BlockSpec((1, LANE), lambda i: (0, 0))
    row = pl.BlockSpec((tr, LANE), lambda i: (i, 0))
    return pl.pallas_call(
        body, name=name, grid=(T // tr,),
        in_specs=[pl.BlockSpec((tr, LANE), lambda i: (i, off)), vec, vec],
        out_specs=[row, row],
        out_shape=[jax.ShapeDtypeStruct((T, LANE), F32)] * 2, compiler_params=_params(("parallel",)),
    )(proj, alog, dtb)


def _gates_bwd(proj, alog, dtb, dgb, off, H, *, name):
    T = proj.shape[0]
    tr = _tile(T, 512, CHUNK)

    def body(ab_ref, al_ref, dt_ref, d_ref, dab_ref, dal_ref, ddt_ref):
        ab, d = ab_ref[...], d_ref[...]
        lane = lax.broadcasted_iota(jnp.int32, ab.shape, 1)
        is_g = lane < H
        triu = _tri(upper=True).astype(F32)
        dg = jnp.concatenate([_mm(triu, d[c * CHUNK:(c + 1) * CHUNK, :], precision=lax.Precision.HIGHEST)
                              for c in range(tr // CHUNK)], axis=0)
        z = ab + dt_ref[...]
        A = -jnp.exp(al_ref[...])
        da = dg * A * jax.nn.sigmoid(z)
        beta = jax.nn.sigmoid(ab)
        db = d * beta * (1.0 - beta)
        dab_ref[...] = jnp.where(is_g, da, jnp.where(lane < 2 * H, db, 0.0)).astype(dab_ref.dtype)

        @pl.when(pl.program_id(0) == 0)
        def _():
            dal_ref[...] = jnp.zeros_like(dal_ref)
            ddt_ref[...] = jnp.zeros_like(ddt_ref)

        dal_ref[...] += jnp.sum(jnp.where(is_g, dg * A * _softplus(z), 0.0), axis=0, keepdims=True)
        ddt_ref[...] += jnp.sum(jnp.where(is_g, da, 0.0), axis=0, keepdims=True)

    vec = pl.BlockSpec((1, LANE), lambda i: (0, 0))
    row = pl.BlockSpec((tr, LANE), lambda i: (i, 0))
    return pl.pallas_call(
        body, name=name, grid=(T // tr,),
        in_specs=[pl.BlockSpec((tr, LANE), lambda i: (i, off)), vec, vec, row],
        out_specs=[row, vec, vec],
        out_shape=[jax.ShapeDtypeStruct((T, LANE), BF16), jax.ShapeDtypeStruct((1, LANE), F32),
                   jax.ShapeDtypeStruct((1, LANE), F32)],
        compiler_params=_params(("arbitrary",)),
    )(proj, alog, dtb, dgb)


def _gated_norm_fwd(o, proj, gn, zoff, *, name):
    T, W = o.shape
    tr = _tile(T, 512, 8)

    def body(o_ref, z_ref, g_ref, y_ref):
        ov = o_ref[...]
        r = lax.rsqrt(jnp.mean(ov * ov, axis=-1, keepdims=True) + EPS)
        y_ref[...] = (ov * r * g_ref[...] * _silu(z_ref[...])).astype(y_ref.dtype)

    blk = pl.BlockSpec((tr, LANE), lambda i, j: (i, j))
    return pl.pallas_call(
        body, name=name, grid=(T // tr, W // LANE),
        in_specs=[blk, pl.BlockSpec((tr, LANE), lambda i, j: (i, zoff + j)), pl.BlockSpec((1, LANE), lambda i, j: (0, 0))],
        out_specs=blk, out_shape=jax.ShapeDtypeStruct((T, W), BF16), compiler_params=_params(("parallel", "parallel")),
    )(o, proj, gn)


def _gated_norm_bwd(o, proj, gn, dycat, zoff, yoff, *, name):
    T, W = o.shape
    tr = _tile(T, 512, 8)

    def body(o_ref, z_ref, g_ref, dy_ref, do_ref, dz_ref, dg_ref):
        ov, zv, gv, dy = o_ref[...], z_ref[...], g_ref[...], dy_ref[...]
        r = lax.rsqrt(jnp.mean(ov * ov, axis=-1, keepdims=True) + EPS)
        nh = ov * r
        s = _silu(zv)

        @pl.when((pl.program_id(0) == 0) & (pl.program_id(1) == 0))
        def _():
            dg_ref[...] = jnp.zeros_like(dg_ref)

        dg_ref[...] += jnp.sum(dy * nh * s, axis=0, keepdims=True)
        dz_ref[...] = (dy * nh * gv * _dsilu(zv)).astype(dz_ref.dtype)
        dn = dy * gv * s
        do_ref[...] = r * (dn - nh * jnp.mean(dn * nh, axis=-1, keepdims=True))

    blk = pl.BlockSpec((tr, LANE), lambda i, j: (i, j))
    vec = pl.BlockSpec((1, LANE), lambda i, j: (0, 0))
    return pl.pallas_call(
        body, name=name, grid=(T // tr, W // LANE),
        in_specs=[blk, pl.BlockSpec((tr, LANE), lambda i, j: (i, zoff + j)), vec,
                  pl.BlockSpec((tr, LANE), lambda i, j: (i, yoff + j))],
        out_specs=[blk, blk, vec],
        out_shape=[jax.ShapeDtypeStruct((T, W), F32), jax.ShapeDtypeStruct((T, W), BF16),
                   jax.ShapeDtypeStruct((1, LANE), F32)],
        compiler_params=_params(("arbitrary", "arbitrary")),
    )(o, proj, gn, dycat)


def _ffn_act_fwd(up_pre, conv_w, *, name):
    T, F2 = up_pre.shape
    RC = _tile(T, ROWS_FFN_FWD, 8)
    nb = F2 // 2 // LANE
    K = conv_w.shape[0]

    def body(g_ref, v_ref, wg_ref, wv_ref, y_ref):
        def step(i, carry):
            r0 = pl.multiple_of(i * RC, RC)
            gate = _conv_down(_ext(g_ref, r0, T, True, False, RC), wg_ref, K)
            val = _conv_down(_ext(v_ref, r0, T, True, False, RC), wv_ref, K)
            y_ref[pl.ds(r0, RC), :] = (_silu(gate) * val).astype(y_ref.dtype)
            return carry
        lax.fori_loop(0, T // RC, step, 0)

    return pl.pallas_call(
        body, name=name, grid=(nb,),
        in_specs=_col_specs(T, (0, nb)) + [pl.BlockSpec((K, LANE), lambda j: (0, j)),
                                           pl.BlockSpec((K, LANE), lambda j: (0, nb + j))],
        out_specs=pl.BlockSpec((T, LANE), lambda j: (0, j)),
        out_shape=jax.ShapeDtypeStruct((T, F2 // 2), BF16), compiler_params=_params(("parallel",)),
    )(up_pre, up_pre, conv_w, conv_w)


def _ffn_act_bwd(up_pre, conv_w, dact, *, name):
    T, F2 = up_pre.shape
    RC = _tile(T, ROWS_FFN_BWD, 8)
    nb = F2 // 2 // LANE
    K = conv_w.shape[0]

    def body(g_ref, v_ref, wg_ref, wv_ref, da_ref, dg_ref, dv_ref, dwg_ref, dwv_ref):
        def step(i, accs):
            r0 = pl.multiple_of(i * RC, RC)
            g3 = _ext(g_ref, r0, T, True, True, RC)
            v3 = _ext(v_ref, r0, T, True, True, RC)
            gate2 = _conv_down(g3, wg_ref, K)
            val2 = _conv_down(v3, wv_ref, K)
            da2 = _ext(da_ref, r0, T, False, True, RC)
            dgate2 = da2 * val2 * _dsilu(gate2)
            dval2 = da2 * _silu(gate2)
            dgp, dvp, new = None, None, []
            for j in range(K):
                s = K - 1 - j
                tg = _up(dgate2, s, RC) * wg_ref[j:j + 1, :]
                tv = _up(dval2, s, RC) * wv_ref[j:j + 1, :]
                dgp = tg if dgp is None else dgp + tg
                dvp = tv if dvp is None else dvp + tv
                new.append(accs[2 * j] + _fold8(dgate2[:RC] * _down(g3[:RC + 8], s)))
                new.append(accs[2 * j + 1] + _fold8(dval2[:RC] * _down(v3[:RC + 8], s)))
            dg_ref[pl.ds(r0, RC), :] = dgp.astype(dg_ref.dtype)
            dv_ref[pl.ds(r0, RC), :] = dvp.astype(dv_ref.dtype)
            return tuple(new)

        accs = lax.fori_loop(0, T // RC, step, tuple(jnp.zeros((8, LANE), F32) for _ in range(2 * K)))
        for j in range(K):
            dwg_ref[j:j + 1, :] = jnp.sum(accs[2 * j], axis=0, keepdims=True)
            dwv_ref[j:j + 1, :] = jnp.sum(accs[2 * j + 1], axis=0, keepdims=True)

    col = pl.BlockSpec((T, LANE), lambda j: (0, j))
    wsp = pl.BlockSpec((K, LANE), lambda j: (0, j))
    return pl.pallas_call(
        body, name=name, grid=(nb,),
        in_specs=_col_specs(T, (0, nb)) + [wsp, pl.BlockSpec((K, LANE), lambda j: (0, nb + j)), col],
        out_specs=[col, col, wsp, wsp],
        out_shape=[jax.ShapeDtypeStruct((T, F2 // 2), BF16)] * 2 + [jax.ShapeDtypeStruct((K, F2 // 2), F32)] * 2,
        compiler_params=_params(("parallel",)),
    )(up_pre, up_pre, conv_w, conv_w, dact)


CPB = 8
CPB_SCAN = 4
GRP = 8
HP = lax.Precision.HIGH


def _tri(strict=False, upper=False):
    r = lax.broadcasted_iota(jnp.int32, (CHUNK, CHUNK), 0)
    c = lax.broadcasted_iota(jnp.int32, (CHUNK, CHUNK), 1)
    if upper:
        return c >= r
    return (r > c) if strict else (r >= c)


def _mm(a, b, dn="nn", precision=None):
    precision = HP if precision is None else precision
    return lax.dot_general(a, b, _DN[dn], precision=precision, preferred_element_type=F32)


def _mm16(a, b, dn="nn"):
    return lax.dot_general(a.astype(BF16), b.astype(BF16), _DN[dn], preferred_element_type=F32)


def _each(f, *cols):
    return [f(*xs) for xs in zip(*cols)]


def _decay(gam):
    return jnp.exp(jnp.where(_tri(), gam[:, :CHUNK] - gam.T[:CHUNK, :], -1e30))


def _delta_specs(T, H, cpb):
    rows = cpb * CHUNK
    col = lambda o: pl.BlockSpec((rows, LANE), functools.partial(lambda o, h, n: (n, o + h), o))
    bc = pl.BlockSpec((1, rows, LANE), lambda h, n: (h, n, 0))
    sq = pl.BlockSpec((1, cpb, CHUNK, CHUNK), lambda h, n: (h, n, 0, 0))
    vec = pl.BlockSpec((1, cpb, 1, LANE), lambda h, n: (h, n, 0, 0))
    return col, bc, sq, vec


def _delta_prep_fwd(qkv, gamB, bB, H, *, name):
    T = qkv.shape[0]
    N = T // CHUNK
    cpb = _tile(N, CPB, 8)
    grp = min(GRP, cpb)
    col, bc, sq, vec = _delta_specs(T, H, cpb)

    def body(q_ref, k_ref, v_ref, g_ref, b_ref, u_ref, w_ref, qd_ref, kd_ref, qk_ref, ti_ref, gl_ref):
        eye = (lax.broadcasted_iota(jnp.int32, (CHUNK, CHUNK), 0) == lax.broadcasted_iota(jnp.int32, (CHUNK, CHUNK), 1)).astype(F32)
        strict = _tri(strict=True)
        for c0 in range(0, cpb, grp):
            cs = list(range(c0, c0 + grp))
            rows = [slice(c * CHUNK, (c + 1) * CHUNK) for c in cs]
            q, k, v = ([r_[r, :] for r in rows] for r_ in (q_ref, k_ref, v_ref))
            bb = [b_ref[0, r, :] for r in rows]
            gam = [g_ref[0, r, :] for r in rows]
            D = _each(_decay, gam)
            e = _each(jnp.exp, gam)
            kk = _each(lambda k_: _mm16(k_, k_, "nt"), k)
            X = _each(lambda kk_, D_, b_: -(jnp.where(strict, kk_ * D_, 0.0) * b_[:, :CHUNK]), kk, D, bb)
            R = _each(lambda x: eye + x, X)
            for _ in range(5):
                X = _each(lambda x: _mm(x, x), X)
                R = _each(lambda r, x: r + _mm(r, x), R, X)
            u = _each(lambda r, b_, v_: _mm(r, b_ * v_), R, bb, v)
            w = _each(lambda r, b_, e_, k_: _mm(r, b_ * e_ * k_), R, bb, e, k)
            qk = _each(lambda q_, k_, D_: _mm16(q_, k_, "nt") * D_, q, k, D)
            for i, c in enumerate(cs):
                glast = gam[i][CHUNK - 1:CHUNK, :]
                u_ref[rows[i], :] = u[i]
                w_ref[rows[i], :] = w[i]
                qd_ref[rows[i], :] = e[i] * q[i]
                kd_ref[rows[i], :] = jnp.exp(glast - gam[i]) * k[i]
                qk_ref[0, c] = qk[i]
                ti_ref[0, c] = R[i]
                gl_ref[0, c] = jnp.exp(glast)

    full = jax.ShapeDtypeStruct((T, H * LANE), F32)
    sqs = jax.ShapeDtypeStruct((H, N, CHUNK, CHUNK), F32)
    return pl.pallas_call(
        body, name=name, grid=(H, N // cpb),
        in_specs=[col(0), col(H), col(2 * H), bc, bc],
        out_specs=[col(0)] * 4 + [sq, sq, vec],
        out_shape=[full] * 4 + [sqs, sqs, jax.ShapeDtypeStruct((H, N, 1, LANE), F32)],
        compiler_params=_params(("parallel", "parallel")),
    )(qkv, qkv, qkv, gamB, bB)


def _scan_specs(H, N, cpb, hb, rev):
    nbk = N // cpb
    blk = (lambda n: nbk - 1 - n) if rev else (lambda n: n)
    col = pl.BlockSpec((cpb * CHUNK, hb * LANE), lambda h, n: (blk(n), h))
    sq = pl.BlockSpec((hb, cpb, CHUNK, CHUNK), lambda h, n: (h, blk(n), 0, 0))
    vec = pl.BlockSpec((hb, cpb, 1, LANE), lambda h, n: (h, blk(n), 0, 0))
    st = pl.BlockSpec((hb, cpb, HEAD, HEAD), lambda h, n: (h, blk(n), 0, 0))
    return col, sq, vec, st


def _delta_scan_fwd(u, w, qd, kd, qk, gl, H, *, name):
    T = u.shape[0]
    N = T // CHUNK
    cpb = _tile(N, CPB_SCAN, 4)
    hb = min(GRP, H)
    col, sq, vec, st = _scan_specs(H, N, cpb, hb, False)
    lanes = [slice(j * LANE, (j + 1) * LANE) for j in range(hb)]
    heads = list(range(hb))

    def body(u_ref, w_ref, qd_ref, kd_ref, qk_ref, gl_ref, o_ref, vn_ref, ss_ref, s_scr):
        @pl.when(pl.program_id(1) == 0)
        def _():
            s_scr[...] = jnp.zeros_like(s_scr)

        def step(c, states):
            rows = pl.ds(pl.multiple_of(c * CHUNK, CHUNK), CHUNK)
            S = list(states)
            for j in heads:
                ss_ref[j, c] = S[j]
            wS = _each(lambda ln, s: _mm16(w_ref[rows, ln], s), lanes, S)
            qS = _each(lambda ln, s: _mm16(qd_ref[rows, ln], s), lanes, S)
            vn = _each(lambda ln, ws: u_ref[rows, ln] - ws, lanes, wS)
            o = _each(lambda j, qs, vn_: qs + _mm16(qk_ref[j, c], vn_), heads, qS, vn)
            new = _each(lambda j, ln, s, vn_: s * gl_ref[j, c] + _mm16(kd_ref[rows, ln], vn_, "tn"),
                        heads, lanes, S, vn)
            for j in heads:
                o_ref[rows, lanes[j]] = o[j]
                vn_ref[rows, lanes[j]] = vn[j]
            return tuple(new)
        out = lax.fori_loop(0, cpb, step, tuple(s_scr[j] for j in heads))
        for j in heads:
            s_scr[j] = out[j]

    full = jax.ShapeDtypeStruct((T, H * LANE), F32)
    return pl.pallas_call(
        body, name=name, grid=(H // hb, N // cpb),
        in_specs=[col] * 4 + [sq, vec],
        out_specs=[col, col, st],
        out_shape=[full, full, jax.ShapeDtypeStruct((H, N, HEAD, HEAD), F32)],
        scratch_shapes=[pltpu.VMEM((hb, HEAD, HEAD), F32)],
        compiler_params=_params(("parallel", "arbitrary")),
    )(u, w, qd, kd, qk, gl)


def _delta_scan_bwd(do, w, qd, kd, vn, qk, gl, ss, H, *, name):
    T = do.shape[0]
    N = T // CHUNK
    cpb = _tile(N, CPB_SCAN, 4)
    hb = min(GRP, H)
    col, sq, vec, st = _scan_specs(H, N, cpb, hb, True)
    lanes = [slice(j * LANE, (j + 1) * LANE) for j in range(hb)]
    heads = list(range(hb))

    def body(do_ref, w_ref, qd_ref, kd_ref, vn_ref, qk_ref, gl_ref, ss_ref,
             du_ref, dw_ref, dqd_ref, dkd_ref, dqk_ref, dgl_ref, ds_scr):
        @pl.when(pl.program_id(1) == 0)
        def _():
            ds_scr[...] = jnp.zeros_like(ds_scr)

        def step(i, dstates):
            c = cpb - 1 - i
            rows = pl.ds(pl.multiple_of(c * CHUNK, CHUNK), CHUNK)
            dS = list(dstates)
            S = [ss_ref[j, c] for j in heads]
            dov = [do_ref[rows, ln] for ln in lanes]
            vnv = [vn_ref[rows, ln] for ln in lanes]
            a1 = _each(lambda j, d_: _mm16(qk_ref[j, c], d_, "tn"), heads, dov)
            a2 = _each(lambda ln, ds: _mm16(kd_ref[rows, ln], ds), lanes, dS)
            dvn = _each(lambda x, y: x + y, a1, a2)
            dqd = _each(lambda d_, s: _mm16(d_, s, "nt"), dov, S)
            dkd = _each(lambda v_, ds: _mm16(v_, ds, "nt"), vnv, dS)
            dqk = _each(lambda d_, v_: _mm16(d_, v_, "nt"), dov, vnv)
            dw = _each(lambda dv_, s: -_mm16(dv_, s, "nt"), dvn, S)
            b1 = _each(lambda ln, d_: _mm16(qd_ref[rows, ln], d_, "tn"), lanes, dov)
            b2 = _each(lambda ln, dv_: _mm16(w_ref[rows, ln], dv_, "tn"), lanes, dvn)
            new = _each(lambda j, x, y, ds: x + ds * gl_ref[j, c] - y, heads, b1, b2, dS)
            for j in heads:
                du_ref[rows, lanes[j]] = dvn[j]
                dw_ref[rows, lanes[j]] = dw[j]
                dqd_ref[rows, lanes[j]] = dqd[j]
                dkd_ref[rows, lanes[j]] = dkd[j]
                dqk_ref[j, c] = dqk[j]
                dgl = jnp.sum(jnp.sum(dS[j] * S[j], axis=1, keepdims=True), axis=0, keepdims=True)
                dgl_ref[j, c] = jnp.broadcast_to(dgl, (1, LANE))
            return tuple(new)
        out = lax.fori_loop(0, cpb, step, tuple(ds_scr[j] for j in heads))
        for j in heads:
            ds_scr[j] = out[j]

    full = jax.ShapeDtypeStruct((T, H * LANE), F32)
    return pl.pallas_call(
        body, name=name, grid=(H // hb, N // cpb),
        in_specs=[col] * 5 + [sq, vec, st],
        out_specs=[col] * 4 + [sq, vec],
        out_shape=[full] * 4 + [jax.ShapeDtypeStruct((H, N, CHUNK, CHUNK), F32), jax.ShapeDtypeStruct((H, N, 1, LANE), F32)],
        scratch_shapes=[pltpu.VMEM((hb, HEAD, HEAD), F32)],
        compiler_params=_params(("parallel", "arbitrary")),
    )(do, w, qd, kd, vn, qk, gl, ss)


def _delta_prep_bwd(qkv, gamB, bB, ti, u, w, qk, du, dw, dqd, dkd, dqk, dgl, H, *, name):
    T = qkv.shape[0]
    N = T // CHUNK
    cpb = _tile(N, CPB, 8)
    grp = min(GRP, cpb)
    col, bc, sq, vec = _delta_specs(T, H, cpb)

    def body(q_ref, k_ref, v_ref, g_ref, b_ref, ti_ref, u_ref, w_ref, qk_ref,
             du_ref, dw_ref, dqd_ref, dkd_ref, dqk_ref, dgl_ref,
             dq_ref, dk_ref, dv_ref, dg_ref, db_ref):
        ones = jnp.ones((CHUNK, LANE), F32)
        strict = _tri(strict=True)
        last = lax.broadcasted_iota(jnp.int32, (CHUNK, LANE), 0) == CHUNK - 1
        lsum = lambda x: jnp.sum(x, axis=-1, keepdims=True)
        for c0 in range(0, cpb, grp):
            cs = list(range(c0, c0 + grp))
            rows = [slice(c * CHUNK, (c + 1) * CHUNK) for c in cs]
            ld = lambda r_: [r_[r, :] for r in rows]
            q, k, v, uv, wv, duv, dwv, dqd_v, dkd_v = (ld(r_) for r_ in (q_ref, k_ref, v_ref, u_ref, w_ref, du_ref, dw_ref, dqd_ref, dkd_ref))
            bb = [b_ref[0, r, :] for r in rows]
            gam = [g_ref[0, r, :] for r in rows]
            Ti = [ti_ref[0, c] for c in cs]
            QK = [qk_ref[0, c] for c in cs]
            dqk_v = [dqk_ref[0, c] for c in cs]
            D = _each(_decay, gam)
            e = _each(jnp.exp, gam)
            glast = [g_[CHUNK - 1:CHUNK, :] for g_ in gam]
            eL = _each(lambda gl_, g_: jnp.exp(gl_ - g_), glast, gam)
            kk = _each(lambda k_: _mm16(k_, k_, "nt"), k)
            KKD = _each(lambda kk_, D_: jnp.where(strict, kk_ * D_, 0.0), kk, D)
            dru = _each(lambda t, d_: _mm(t, d_, "tn"), Ti, duv)
            drw = _each(lambda t, d_: _mm(t, d_, "tn"), Ti, dwv)
            l1 = _each(lambda a, b: _mm(a, b, "nt"), dru, uv)
            l2 = _each(lambda a, b: _mm(a, b, "nt"), drw, wv)
            dL = _each(lambda a, b: jnp.where(strict, -(a + b), 0.0), l1, l2)
            Mm = _each(lambda dl, b_: dl * b_[:, :CHUNK], dL, bb)
            dKK = _each(lambda m_, D_: m_ * D_, Mm, D)
            dQK = _each(lambda a, D_: a * D_, dqk_v, D)
            P = _each(lambda m_, kkd, a, qk_: m_ * kkd + a * qk_, Mm, KKD, dqk_v, QK)
            q1 = _each(lambda a, k_: _mm16(a, k_), dQK, k)
            k1 = _each(lambda a, q_: _mm16(a, q_, "tn"), dQK, q)
            k2 = _each(lambda a, k_: _mm16(a, k_), dKK, k)
            k3 = _each(lambda a, k_: _mm16(a, k_, "tn"), dKK, k)
            s1 = _each(lambda dl, kkd: _mm(dl * kkd, ones), dL, KKD)
            p1 = _each(lambda p_: _mm(p_, ones), P)
            p2 = _each(lambda p_: _mm(p_, ones, "tn"), P)
            for i, c in enumerate(cs):
                r = rows[i]
                bek = bb[i] * e[i]
                kdv = eL[i] * k[i]
                dq_ref[r, :] = q1[i] + e[i] * dqd_v[i]
                dk_ref[r, :] = k1[i] + k2[i] + k3[i] + bek * drw[i] + eL[i] * dkd_v[i]
                dv_ref[r, :] = bb[i] * dru[i]
                db_ref[0, r, :] = s1[i] + lsum(dru[i] * v[i]) + lsum(drw[i] * e[i] * k[i])
                dgam = (p1[i] - p2[i] + lsum(drw[i] * bek * k[i]) + lsum(dqd_v[i] * e[i] * q[i])
                        - lsum(dkd_v[i] * kdv))
                xlast = jnp.sum(lsum(dkd_v[i] * kdv), axis=0, keepdims=True) + jnp.exp(glast[i]) * dgl_ref[0, c]
                dg_ref[0, r, :] = dgam + jnp.where(last, xlast, 0.0)

    full = jax.ShapeDtypeStruct((T, H * LANE), F32)
    bcs = jax.ShapeDtypeStruct((H, T, LANE), F32)
    return pl.pallas_call(
        body, name=name, grid=(H, N // cpb),
        in_specs=[col(0), col(H), col(2 * H), bc, bc, sq, col(0), col(0), sq, col(0), col(0), col(0), col(0), sq, vec],
        out_specs=[col(0), col(0), col(0), bc, bc],
        out_shape=[full, full, full, bcs, bcs],
        compiler_params=_params(("parallel", "parallel")),
    )(qkv, qkv, qkv, gamB, bB, ti, u, w, qk, du, dw, dqd, dkd, dqk, dgl)


def _adam(parts, w, m, v, *, name, own=None, me=None):
    P, R, C = parts.shape
    if R > 256 and R % 8:
        tr, tc = R, _tile(C, 256)
    else:
        tr, tc = _tile(R, 256, 8), C
    n_own = 0 if own is None else 2

    def body(*refs):
        p_ref, w_ref, m_ref, v_ref, g_ref, d_ref, nm_ref, nv_ref = refs[n_own:]
        g = None
        for i in range(P):
            t = p_ref[i].astype(F32)
            if n_own:
                t = jnp.where(refs[0][0] == i, refs[1][...].astype(F32), t)
            g = t if g is None else g + t
        mn = ADAM_B1 * m_ref[...] + (1.0 - ADAM_B1) * g
        vn = ADAM_B2 * v_ref[...] + (1.0 - ADAM_B2) * (g * g)
        m_hat = mn / (1.0 - ADAM_B1 ** ADAM_STEP)
        v_hat = vn / (1.0 - ADAM_B2 ** ADAM_STEP)
        g_ref[...] = g
        d_ref[...] = -ADAM_LR * (m_hat / (jnp.sqrt(v_hat) + ADAM_EPS) + ADAM_WD * w_ref[...])
        nm_ref[...] = mn
        nv_ref[...] = vn

    blk = pl.BlockSpec((tr, tc), lambda i, j: (i, j))
    return pl.pallas_call(
        body, name=name, grid=(R // tr, C // tc),
        in_specs=[pl.BlockSpec(memory_space=pltpu.SMEM), blk][:n_own] + [pl.BlockSpec((P, tr, tc), lambda i, j: (0, i, j)), blk, blk, blk],
        out_specs=[blk] * 4, out_shape=[jax.ShapeDtypeStruct((R, C), F32)] * 4,
        compiler_params=_params(("parallel", "parallel")),
    )(*([me, own] if n_own else []), parts, w, m, v)


def _mesh_pos():
    return lax.axis_index("x"), lax.axis_index("y"), lax.axis_index("c")


def _peer(k):
    x, y, c = _mesh_pos()
    px, py, pc = x ^ ((k >> 2) & 1), y ^ ((k >> 1) & 1), c ^ (k & 1)
    return (px, py, pc), 4 * px + 2 * py + pc


def _exchange(arrays, scatter, *, name):
    n = len(arrays)
    blocks = [a.shape[1:] if scatter else a.shape for a in arrays]

    def body(*refs):
        srcs, dsts = refs[:n], refs[n:2 * n]
        send_sems, recv_sems, local_sems = refs[2 * n:]
        x, y, c = _mesh_pos()
        me = 4 * x + 2 * y + c
        local, sends = [], []
        for a in range(n):
            cp = pltpu.make_async_copy(srcs[a].at[me] if scatter else srcs[a], dsts[a].at[me], local_sems.at[a])
            cp.start()
            local.append(cp)
            for k in range(1, N_DEV):
                dev, idx = _peer(k)
                cp = pltpu.make_async_remote_copy(
                    src_ref=srcs[a].at[idx] if scatter else srcs[a], dst_ref=dsts[a].at[me],
                    send_sem=send_sems.at[a * N_DEV + k], recv_sem=recv_sems.at[a * N_DEV + k],
                    device_id=dev, device_id_type=MESH)
                cp.start()
                sends.append(cp)
        for a in range(n):
            for k in range(1, N_DEV):
                dev, idx = _peer(k)
                pltpu.make_async_remote_copy(
                    src_ref=srcs[a].at[idx] if scatter else srcs[a], dst_ref=dsts[a].at[idx],
                    send_sem=send_sems.at[a * N_DEV + k], recv_sem=recv_sems.at[a * N_DEV + k],
                    device_id=dev, device_id_type=MESH).wait_recv()
        for cp in sends:
            cp.wait_send()
        for cp in local:
            cp.wait()

    anyspec = pl.BlockSpec(memory_space=pl.ANY)
    return pl.pallas_call(
        body, name=name, in_specs=[anyspec] * n, out_specs=[anyspec] * n,
        out_shape=[jax.ShapeDtypeStruct((N_DEV,) + tuple(b), a.dtype) for a, b in zip(arrays, blocks)],
        scratch_shapes=[pltpu.SemaphoreType.DMA((n * N_DEV,)), pltpu.SemaphoreType.DMA((n * N_DEV,)),
                        pltpu.SemaphoreType.DMA((n,))],
    )(*arrays)


_ANY = pl.BlockSpec(memory_space=pl.ANY)
_SEM = pl.BlockSpec(memory_space=pltpu.SEMAPHORE)
_EFFECT = pltpu.SideEffectType.DATAFLOW_SIDE_EFFECTING


def _in_hbm(a):
    return pltpu.with_memory_space_constraint(a, pltpu.HBM)


def _split_copy(src, land, send, recv, k, me, scatter, landed):
    dev, idx = _peer(k)
    return pltpu.make_async_remote_copy(
        src_ref=src.at[idx] if scatter else src, dst_ref=land.at[idx if landed else me],
        send_sem=send.at[k], recv_sem=recv.at[k], device_id=dev, device_id_type=MESH)


ALL_PEERS = tuple(range(1, N_DEV))
SIBLING = 1
SAME_CORE = (2, 4, 6)


def _split_start(srcs, lands, scatter, *, name, relations=None):
    n = len(srcs)
    relations = relations or [ALL_PEERS] * n

    def body(*refs):
        src, land, send, recv, token = refs[:n], refs[n:2 * n], refs[2 * n:3 * n], refs[3 * n:4 * n], refs[-1]
        x, y, c = _mesh_pos()
        me = 4 * x + 2 * y + c
        for a in range(n):
            for k in relations[a]:
                _split_copy(src[a], land[a], send[a], recv[a], k, me, scatter, False).start()
        token[...] = jnp.zeros_like(token)

    outs = pl.pallas_call(
        body, name=name,
        out_shape=[pltpu.SemaphoreType.DMA((N_DEV,))] * (2 * n) + [pltpu.HBM(t.shape, t.dtype) for t in list(srcs) + list(lands)]
        + [jax.ShapeDtypeStruct((8, LANE), F32)],
        in_specs=[_ANY] * (2 * n), out_specs=[_SEM] * (2 * n) + [_ANY] * (2 * n) + [pl.BlockSpec(memory_space=pltpu.VMEM)],
        input_output_aliases={i: 2 * n + i for i in range(2 * n)},
        compiler_params=pltpu.CompilerParams(has_side_effects=_EFFECT),
    )(*[_in_hbm(t) for t in list(srcs) + list(lands)])
    handles = [(outs[a], outs[n + a], outs[2 * n + a], outs[3 * n + a]) for a in range(n)]
    return handles, outs[-1]


def _split_wait(handle, after, scatter, *, name):
    send, recv, src_thru, land_thru = handle

    def body(src_ref, land_ref, send_ref, recv_ref, after_ref, src_out, land_out):
        x, y, c = _mesh_pos()
        me = 4 * x + 2 * y + c
        for k in range(1, N_DEV):
            cp = _split_copy(src_ref, land_ref, send_ref, recv_ref, k, me, scatter, True)
            cp.wait_send()
            cp.wait_recv()

    return pl.pallas_call(
        body, name=name,
        out_shape=(pltpu.HBM(src_thru.shape, src_thru.dtype), pltpu.HBM(land_thru.shape, land_thru.dtype)),
        in_specs=(_ANY, _ANY, _SEM, _SEM, _ANY), out_specs=(_ANY, _ANY), input_output_aliases={0: 0, 1: 1},
        compiler_params=pltpu.CompilerParams(has_side_effects=_EFFECT),
    )(src_thru, land_thru, send, recv, after)[1]


def _forward_copy(land, fsend, frecv, k, landed):
    x, y, c = _mesh_pos()
    _, idx = _peer(k | SIBLING if landed else k)
    return pltpu.make_async_remote_copy(src_ref=land.at[idx], dst_ref=land.at[idx], send_sem=fsend.at[k],
                                        recv_sem=frecv.at[k], device_id=(x, y, 1 - c), device_id_type=MESH)


def _gather_forward(handle, after, *, name):
    send, recv, src_thru, land_thru = handle

    def body(src_ref, land_ref, send_ref, recv_ref, after_ref, src_out, land_out, fsend, frecv):
        x, y, c = _mesh_pos()
        me = 4 * x + 2 * y + c
        for k in SAME_CORE:
            _split_copy(src_ref, land_ref, send_ref, recv_ref, k, me, False, True).wait_recv()
            _forward_copy(land_ref, fsend, frecv, k, False).start()

    src2, land2, fsend, frecv = pl.pallas_call(
        body, name=name,
        out_shape=(pltpu.HBM(src_thru.shape, src_thru.dtype), pltpu.HBM(land_thru.shape, land_thru.dtype),
                   pltpu.SemaphoreType.DMA((N_DEV,)), pltpu.SemaphoreType.DMA((N_DEV,))),
        in_specs=(_ANY, _ANY, _SEM, _SEM, _ANY), out_specs=(_ANY, _ANY, _SEM, _SEM), input_output_aliases={0: 0, 1: 1},
        compiler_params=pltpu.CompilerParams(has_side_effects=_EFFECT),
    )(src_thru, land_thru, send, recv, after)
    return (send, recv, src2, land2), (fsend, frecv)


def _gather_wait_two_level(handle, fwd, *, name):
    send, recv, src_thru, land_thru = handle
    fsend, frecv = fwd

    def body(src_ref, land_ref, send_ref, recv_ref, fsend_ref, frecv_ref, src_out, land_out):
        x, y, c = _mesh_pos()
        me = 4 * x + 2 * y + c
        for k in (SIBLING,) + SAME_CORE:
            _split_copy(src_ref, land_ref, send_ref, recv_ref, k, me, False, True).wait_send()
        _split_copy(src_ref, land_ref, send_ref, recv_ref, SIBLING, me, False, True).wait_recv()
        for k in SAME_CORE:
            _forward_copy(land_ref, fsend_ref, frecv_ref, k, False).wait_send()
            _forward_copy(land_ref, fsend_ref, frecv_ref, k, True).wait_recv()

    return pl.pallas_call(
        body, name=name,
        out_shape=(pltpu.HBM(src_thru.shape, src_thru.dtype), pltpu.HBM(land_thru.shape, land_thru.dtype)),
        in_specs=(_ANY, _ANY, _SEM, _SEM, _SEM, _SEM), out_specs=(_ANY, _ANY), input_output_aliases={0: 0, 1: 1},
        compiler_params=pltpu.CompilerParams(has_side_effects=_EFFECT),
    )(src_thru, land_thru, send, recv, fsend, frecv)[1]


def _local_step(x, p, tgt, S, wt, conv, emit):
    T, D = x.shape
    CW = DNW = D // 2
    H = DNW // HEAD
    nA, nD = CW // LANE, DNW // LANE
    qkv_off, z_off, ab_off = 3 * nA, 3 * nA + 3 * nD, 3 * nA + 4 * nD
    alog = jnp.pad(S["a_log"], ((0, 0), (0, LANE - H)))
    dtb = jnp.pad(S["dt_bias"], ((0, 0), (0, LANE - H)))

    h1 = _rms_fwd(x, S["g_mix"], name="rms1_fwd")
    w_in, cv = wt("w_in", h1), conv(h1)
    proj = _matmul(h1, w_in, "nt", name="mm_in")
    y_a = _group_a_fwd(proj, cv["conv_a"], CW, name="group_a_fwd")
    qkv = _qkv_fwd(proj, cv["conv_qkv"], qkv_off, H, name="qkv_fwd")
    gb, gamc = _gates_fwd(proj, alog, dtb, ab_off, H, name="gates_fwd")
    bcast = lambda cols: jnp.broadcast_to(cols.T[:, :, None], (H, T, LANE))
    gamB, bB = bcast(gamc[:, :H]), bcast(gb[:, H:2 * H])
    u, w, qd, kd, qk, ti, gl = _delta_prep_fwd(qkv, gamB, bB, H, name="delta_prep_fwd")
    o, vn, ss = _delta_scan_fwd(u, w, qd, kd, qk, gl, H, name="delta_scan_fwd")
    y_b = _gated_norm_fwd(o, proj, S["dn_g"], z_off, name="gated_norm_fwd")
    ycat = jnp.concatenate([y_a, y_b], axis=1)
    w_out = wt("w_out", ycat)
    rows = dict(tm=ROW_TILE, tn=D)
    x1, h2 = _matmul(ycat, w_out, "nn", name="mm_out", out_dtypes=(F32, BF16), epilogue=_epi_residual_rms,
                     extras=(x,), vec_extras=(S["g_ffn"],), **rows)
    w_up = wt("w_up", h2)
    up_pre = _matmul(h2, w_up, "nn", name="mm_up", b_shards=True, tn=SHARD_TILE)
    act = _ffn_act_fwd(up_pre, cv["conv_ffn"], name="ffn_act_fwd")
    w_down = wt("w_down", act)
    x2 = _matmul(act, w_down, "nn", name="mm_down", epilogue=lambda acc, r: (acc + r,), extras=(x1,))
    h3 = _rms_fwd(x2, S["g_ple"], name="rms3_fwd")
    w_pp, w_pg = wt("w_pp", h3), wt("w_pg", h3)
    pp = _matmul(p, w_pp, "nn", name="mm_pp", b_shards=True)

    def ple_epi(acc, x2r, ppr):
        s = jax.nn.sigmoid(acc)
        return x2r + s * ppr, s

    x3, sg = _matmul(h3, w_pg, "nn", name="mm_pg", out_dtypes=(F32, F32), epilogue=ple_epi, extras=(x2, pp), tm=512)
    dx3, dg_final, loss = _final_loss(x3, S["g_final"], tgt, name="final_loss")

    G = {"g_final": dg_final}
    dpg, dpp = _ple_bwd(dx3, pp, sg, name="ple_bwd")
    tok = emit({"w_pp": _matmul(p, dpp, "tn", name="mm_dwpp", out_dtypes=(BF16,), out_shards=True),
                "w_pg": _matmul(h3, dpg, "tn", name="mm_dwpg", out_dtypes=(BF16,))})
    bwd = dict(out_dtypes=(F32, BF16), epilogue=_epi_rms_bwd(2), n_vec=1, **rows)
    dx2, dx2b, G["g_ple"] = _matmul(dpg, w_pg, "nt", name="mm_dh3", after=tok, extras=(x2, dx3),
                                    vec_extras=(S["g_ple"],), **bwd)
    tok = emit({"w_down": _matmul(act, dx2b, "tn", name="mm_dwdown", out_dtypes=(BF16,))})
    dact = _matmul(dx2b, w_down, "nt", name="mm_dact", after=tok)
    dup_g, dup_v, dcf_g, dcf_v = _ffn_act_bwd(up_pre, cv["conv_ffn"], dact, name="ffn_act_bwd")
    G["conv_ffn"] = jnp.concatenate([dcf_g, dcf_v], axis=1)
    dup = jnp.concatenate([dup_g, dup_v], axis=1)
    tok = emit({"w_up": _matmul(h2, dup, "tn", name="mm_dwup", out_dtypes=(BF16,), out_shards=True, tn=SHARD_TILE)})
    dh2 = _matmul(dup, w_up, "nt", name="mm_dh2", after=tok, b_shards=True, tk=SHARD_TILE)
    dx1, dx1b, G["g_ffn"] = _rms_bwd(x1, S["g_ffn"], dh2, dx2, name="rms2_bwd")
    tok = emit({"w_out": _matmul(ycat, dx1b, "tn", name="mm_dwout", out_dtypes=(BF16,))})
    dycat = _matmul(dx1b, w_out, "nt", name="mm_dycat", after=tok)
    do, dz, G["dn_g"] = _gated_norm_bwd(o, proj, S["dn_g"], dycat, z_off, nA, name="gated_norm_bwd")
    du, dw, dqd, dkd, dqk, dgl = _delta_scan_bwd(do, w, qd, kd, vn, qk, gl, ss, H, name="delta_scan_bwd")
    dq, dk, dv, dgB, dbB = _delta_prep_bwd(qkv, gamB, bB, ti, u, w, qk, du, dw, dqd, dkd, dqk, dgl, H,
                                           name="delta_prep_bwd")
    dgb = jnp.pad(jnp.concatenate([dgB[:, :, 0].T, dbB[:, :, 0].T], axis=1), ((0, 0), (0, LANE - 2 * H)))
    dab, dal, ddt = _gates_bwd(proj, alog, dtb, dgb, ab_off, H, name="gates_bwd")
    G["a_log"], G["dt_bias"] = dal[:, :H], ddt[:, :H]
    dqkv, G["conv_qkv"] = _qkv_bwd(proj, cv["conv_qkv"], dq, dk, dv, qkv_off, H, name="qkv_bwd")
    dax, dab_, dac, G["conv_a"] = _group_a_bwd(proj, cv["conv_a"], dycat, CW, name="group_a_bwd")
    in_p = w_in.shape[0]
    dproj = jnp.concatenate([dax, dab_, dac, dqkv, dz, dab, jnp.zeros((T, in_p - (ab_off + 1) * LANE), BF16)], axis=1)
    tok = emit({"w_in": _matmul(dproj, h1, "tn", name="mm_dwin", out_dtypes=(BF16,))})
    dh1 = _matmul(dproj, w_in, "nn", name="mm_dh1", after=tok)
    grad_x, _, G["g_mix"] = _rms_bwd(x, S["g_mix"], dh1, dx1, name="rms1_bwd")
    return loss, grad_x, G


def _col_sharded(landed):
    _, R, C = landed.shape
    return jnp.transpose(landed, (1, 0, 2)).reshape(R, N_DEV * C)


def kernel(x, p, norm_mix_g, w_in, conv_a_w, conv_qkv_w, a_log, dt_bias, dn_norm_g, w_out, norm_ffn_g, w_up, conv_ffn_w, w_down, norm_ple_g, w_ple_gate, w_ple_proj, final_norm_g, loss_target, m_norm_mix_g, m_w_in, m_conv_a_w, m_conv_qkv_w, m_a_log, m_dt_bias, m_dn_norm_g, m_w_out, m_norm_ffn_g, m_w_up, m_conv_ffn_w, m_w_down, m_norm_ple_g, m_w_ple_gate, m_w_ple_proj, m_final_norm_g, v_norm_mix_g, v_w_in, v_conv_a_w, v_conv_qkv_w, v_a_log, v_dt_bias, v_dn_norm_g, v_w_out, v_norm_ffn_g, v_w_up, v_conv_ffn_w, v_w_down, v_norm_ple_g, v_w_ple_gate, v_w_ple_proj, v_final_norm_g):
    T, D = x.shape[1], x.shape[2]
    xd, _, cd = _mesh_pos()
    me = 4 * xd + 2 * lax.axis_index("y") + cd

    conv_sh = [conv_a_w[0], conv_qkv_w[0], conv_ffn_w[0]]
    conv_n = [c.size for c in conv_sh]
    pack_rows = -(-sum(conv_n) // LANE)
    conv_pack = jnp.pad(jnp.concatenate([c.reshape(-1) for c in conv_sh]), (0, pack_rows * LANE - sum(conv_n))).reshape(pack_rows, LANE)
    names = ["w_in", "conv", "w_out", "w_up", "w_down", "w_pg", "w_pp"]
    tr_ = lambda t: jnp.swapaxes(t, 1, 2)
    shards = [w_in[0].T.astype(BF16), conv_pack, w_out[0].astype(BF16), w_up[0].astype(BF16), w_down[0].astype(BF16),
              w_ple_gate[0].astype(BF16), w_ple_proj[0].astype(BF16)]
    empty_slots = lambda blocks: [lax.empty((N_DEV,) + tuple(b.shape), b.dtype) for b in blocks]
    handles, tok0 = _split_start(shards, empty_slots(shards), False, name="gather_start",
                                 relations=[(SIBLING,) + SAME_CORE] + [ALL_PEERS] * (len(shards) - 1))
    handle = dict(zip(names, handles))
    own = dict(zip(names, shards))
    in_cols = N_DEV * w_in.shape[2]
    in_p = (in_cols // LANE) * LANE + AB_PAD
    in_place = {"w_up", "w_pp"}

    def gathered(name, after):
        if name == "w_in":
            passed, fwd = _gather_forward(handle[name], after, name="gather_forward_w_in")
            landed = _gather_wait_two_level(passed, fwd, name="gather_wait_w_in")
        else:
            landed = _split_wait(handle[name], after, False, name="gather_wait_" + name)
        return lax.dynamic_update_index_in_dim(landed, own[name], me, 0)

    def wt(name, after):
        landed = gathered(name, after)
        if name in in_place:
            return landed
        full = landed.reshape(-1, D)
        return jnp.pad(full, ((0, in_p - in_cols), (0, 0))) if name == "w_in" else full

    def conv(after):
        flat = gathered("conv", after).reshape(N_DEV, pack_rows * LANE)
        out, o_ = {}, 0
        for nm, c, n_ in zip(("conv_a", "conv_qkv", "conv_ffn"), conv_sh, conv_n):
            out[nm] = _col_sharded(flat[:, o_:o_ + n_].reshape((N_DEV,) + c.shape))
            o_ += n_
        return out

    pending, mine = {}, {}

    def emit(grads):
        parts = [g if nm in in_place else (g[:in_cols] if nm == "w_in" else g).reshape(N_DEV, -1, D)
                 for nm, g in grads.items()]
        hs, tok = _split_start(parts, empty_slots([q[0] for q in parts]), True, name="scatter_start_" + "_".join(grads))
        pending.update(zip(grads, hs))
        mine.update({nm: lax.dynamic_index_in_dim(q, me, 0, keepdims=False) for nm, q in zip(grads, parts)})
        return tok

    S = {
        "g_mix": norm_mix_g + tok0[0, 0], "a_log": a_log, "dt_bias": dt_bias, "dn_g": dn_norm_g, "g_ffn": norm_ffn_g,
        "g_ple": norm_ple_g, "g_final": final_norm_g.reshape(1, D),
    }

    loss_v, grad_x, G = _local_step(x[0], p[0, 0], loss_target[0], S, wt, conv, emit)
    loss = lax.psum(loss_v[0, 0], ("x", "y", "c"))

    small_names = ["g_mix", "g_ffn", "g_ple", "g_final", "dn_g", "a_log", "dt_bias", "conv_a", "conv_qkv", "conv_ffn"]
    small_rows, pieces = [], []
    for nm in small_names:
        g_ = G[nm].reshape(-1)
        r_ = -(-g_.size // (8 * LANE)) * 8
        small_rows.append(r_)
        pieces.append(jnp.pad(g_, (0, r_ * LANE - g_.size)).reshape(r_, LANE))
    (small_l,) = _exchange([jnp.concatenate(pieces, axis=0)], False, name="gather_small_grads")
    landed = {nm: _split_wait(h_, grad_x, True, name="scatter_wait_" + nm) for nm, h_ in pending.items() if nm != "w_in"}

    def small_parts(nm):
        i = small_names.index(nm)
        r0 = sum(small_rows[:i])
        shp = G[nm].shape
        return small_l[:, r0:r0 + small_rows[i], :].reshape(N_DEV, -1)[:, :G[nm].size].reshape((N_DEV,) + shp)

    def conv_parts(nm, shard):
        full = small_parts(nm)
        C = shard.shape[-1]
        return lax.dynamic_slice_in_dim(full, me * C, C, axis=2)

    def adam(parts, w_, m_, v_, nm, own_=None):
        shp = w_.shape
        w2, m2, v2 = (t.reshape(parts.shape[1:]) for t in (w_, m_, v_))
        kw = {} if own_ is None else {"own": own_, "me": me.astype(jnp.int32).reshape(1)}
        return tuple(t.reshape(shp) for t in _adam(parts, w2, m2, v2, name="adam_" + nm, **kw))

    res = [
        adam(small_parts("g_mix"), norm_mix_g, m_norm_mix_g, v_norm_mix_g, "norm_mix_g"),
        None,
        adam(conv_parts("conv_a", conv_a_w), conv_a_w, m_conv_a_w, v_conv_a_w, "conv_a_w"),
        adam(conv_parts("conv_qkv", conv_qkv_w), conv_qkv_w, m_conv_qkv_w, v_conv_qkv_w, "conv_qkv_w"),
        adam(small_parts("a_log"), a_log, m_a_log, v_a_log, "a_log"),
        adam(small_parts("dt_bias"), dt_bias, m_dt_bias, v_dt_bias, "dt_bias"),
        adam(small_parts("dn_g"), dn_norm_g, m_dn_norm_g, v_dn_norm_g, "dn_norm_g"),
        adam(landed["w_out"], w_out, m_w_out, v_w_out, "w_out", mine["w_out"]),
        adam(small_parts("g_ffn"), norm_ffn_g, m_norm_ffn_g, v_norm_ffn_g, "norm_ffn_g"),
        adam(landed["w_up"], w_up, m_w_up, v_w_up, "w_up", mine["w_up"]),
        adam(conv_parts("conv_ffn", conv_ffn_w), conv_ffn_w, m_conv_ffn_w, v_conv_ffn_w, "conv_ffn_w"),
        adam(landed["w_down"], w_down, m_w_down, v_w_down, "w_down", mine["w_down"]),
        adam(small_parts("g_ple"), norm_ple_g, m_norm_ple_g, v_norm_ple_g, "norm_ple_g"),
        adam(landed["w_pg"], w_ple_gate, m_w_ple_gate, v_w_ple_gate, "w_ple_gate", mine["w_pg"]),
        adam(landed["w_pp"], w_ple_proj, m_w_ple_proj, v_w_ple_proj, "w_ple_proj", mine["w_pp"]),
        adam(small_parts("g_final"), final_norm_g.reshape(1, D), m_final_norm_g.reshape(1, D),
             v_final_norm_g.reshape(1, D), "final_norm_g"),
    ]
    res[-1] = tuple(t.reshape(D) for t in res[-1])
    others_done = jnp.stack([r[1].reshape(-1)[0] for r in res if r is not None])
    landed_in = _split_wait(pending["w_in"], others_done, True, name="scatter_wait_w_in")
    res[1] = tuple(tr_(t) for t in adam(landed_in, tr_(w_in), tr_(m_w_in), tr_(v_w_in), "w_in", mine["w_in"]))
    grads, deltas, new_m, new_v = zip(*res)
    return (loss, grad_x[None], *grads, *deltas, *new_m, *new_v)
```

```python
import functools

import jax
import jax.numpy as jnp
from jax import lax
from jax.experimental import pallas as pl
from jax.experimental.pallas import tpu as pltpu

F32 = jnp.float32
BF16 = jnp.bfloat16

EPS = 1e-6
CHUNK = 64
HEAD = 128
LANE = 128
N_DEV = 8
AB_PAD = 512

ADAM_LR = 0.001
ADAM_B1 = 0.9
ADAM_B2 = 0.999
ADAM_EPS = 1e-08
ADAM_WD = 0.01
ADAM_STEP = 10

MESH = pl.DeviceIdType.MESH


def _tile(dim, target, align=LANE):
    if dim <= target:
        return dim
    t = (target // align) * align
    while t > align and dim % t:
        t -= align
    assert dim % t == 0, (dim, target)
    return t


def _params(sem, vmem_mb=48):
    return pltpu.CompilerParams(dimension_semantics=sem, vmem_limit_bytes=vmem_mb << 20)


_DN = {"nn": (((1,), (0,)), ((), ())), "nt": (((1,), (1,)), ((), ())), "tn": (((0,), (0,)), ((), ()))}
LONG_K = 4096
SHARD_TILE = 1408


def _matmul(a, b, mode, *, name, out_dtypes=(F32,), epilogue=None, extras=(), vec_extras=(), n_vec=0, after=None,
            b_shards=False, out_shards=False, tm=1024, tn=1024, tk=2048):
    shard_w = b.shape[2] if b_shards else None
    if b_shards:
        b_rows, b_cols = b.shape[1], N_DEV * shard_w
    else:
        b_rows, b_cols = b.shape
    if mode == "nn":
        (M, K), (K2, N) = a.shape, (b_rows, b_cols)
    elif mode == "nt":
        (M, K), (N, K2) = a.shape, (b_rows, b_cols)
    else:
        (K, M), (K2, N) = a.shape, (b_rows, b_cols)
    assert K == K2, (name, a.shape, b.shape)
    tm = _tile(M, tm)
    tn = _tile(shard_w if (b_shards and mode == "nn") else N // N_DEV if out_shards else N, tn)
    grp = 1
    if b_shards and mode == "nt":
        grp = max(g for g in (1, 2, 4, 8) if g <= max(1, tk // shard_w))
    tk = grp * shard_w if grp > 1 else _tile(shard_w if (b_shards and mode == "nt") else K, tk)
    assert K % tk == 0, (name, K, tk)
    nk = K // tk
    n_ex, n_out = len(extras) + len(vec_extras), len(out_dtypes)
    assert n_vec == 0 or tn == N, (name, tn, N)
    dn = _DN[mode]

    n_tok = 0 if after is None else 1

    def body(a_ref, b_ref, *rest):
        rest = rest[n_tok:]
        ex_refs, out_refs, vec_refs = rest[:n_ex], rest[n_ex:n_ex + n_out], rest[n_ex + n_out:n_ex + n_out + n_vec]
        if grp > 1:
            part = sum(lax.dot_general(a_ref[:, s * shard_w:(s + 1) * shard_w].astype(BF16), b_ref[s].astype(BF16), dn,
                                       preferred_element_type=F32) for s in range(grp))
        else:
            part = lax.dot_general(a_ref[...].astype(BF16), b_ref[...].astype(BF16), dn, preferred_element_type=F32)
        first_rows = pl.program_id(0) == 0

        def finish(res):
            outs = (res,) if epilogue is None else epilogue(res, *[e[...] for e in ex_refs])
            for o_ref, val in zip(out_refs, outs[:n_out]):
                o_ref[...] = val.astype(o_ref.dtype)
            for v_ref, val in zip(vec_refs, outs[n_out:]):
                @pl.when(first_rows)
                def _(v_ref=v_ref, val=val):
                    v_ref[...] = val

                @pl.when(jnp.logical_not(first_rows))
                def _(v_ref=v_ref, val=val):
                    v_ref[...] += val

        if nk == 1:
            finish(part)
            return
        acc, k = rest[-1], pl.program_id(2)

        @pl.when(k == 0)
        def _():
            acc[...] = part

        @pl.when(k > 0)
        def _():
            acc[...] += part

        @pl.when(k == nk - 1)
        def _():
            finish(acc[...])

    a_spec = pl.BlockSpec((tk, tm), lambda i, j, k: (k, i)) if mode == "tn" else pl.BlockSpec((tm, tk), lambda i, j, k: (i, k))
    if b_shards and mode == "nn":
        per = shard_w // tn
        b_spec = pl.BlockSpec((None, tk, tn), lambda i, j, k: (lax.div(j, per), k, lax.rem(j, per)))
    elif b_shards and grp > 1:
        b_spec = pl.BlockSpec((grp, tn, shard_w), lambda i, j, k: (k, j, 0))
    elif b_shards:
        per = shard_w // tk
        b_spec = pl.BlockSpec((None, tn, tk), lambda i, j, k: (lax.div(k, per), j, lax.rem(k, per)))
    else:
        b_spec = pl.BlockSpec((tn, tk), lambda i, j, k: (j, k)) if mode == "nt" else pl.BlockSpec((tk, tn), lambda i, j, k: (k, j))
    mn_spec = pl.BlockSpec((tm, tn), lambda i, j, k: (i, j))
    vec_spec = pl.BlockSpec((1, tn), lambda i, j, k: (0, j))
    if out_shards:
        assert not extras
        per_o = (N // N_DEV) // tn
        out_spec = pl.BlockSpec((None, tm, tn), lambda i, j, k: (lax.div(j, per_o), i, lax.rem(j, per_o)))
        out_dims = (N_DEV, M, N // N_DEV)
    else:
        out_spec, out_dims = mn_spec, (M, N)
    outs = pl.pallas_call(
        body, name=name, grid=(M // tm, N // tn, nk),
        in_specs=[a_spec, b_spec] + [pl.BlockSpec((8, LANE), lambda i, j, k: (0, 0))] * n_tok
        + [mn_spec] * len(extras) + [vec_spec] * len(vec_extras),
        out_specs=[out_spec] * n_out + [vec_spec] * n_vec,
        out_shape=[jax.ShapeDtypeStruct(out_dims, dt) for dt in out_dtypes] + [jax.ShapeDtypeStruct((1, N), F32)] * n_vec,
        scratch_shapes=[pltpu.VMEM((tm, tn), F32)] if nk > 1 else [],
        compiler_params=_params(("arbitrary" if n_vec else "parallel", "parallel", "arbitrary"), 56),
    )(a, b, *([] if after is None else [after]), *extras, *vec_extras)
    return outs[0] if n_out + n_vec == 1 else outs


def _rms_fwd(x, g, *, name):
    T, D = x.shape
    tr = _tile(T, 256, 8)

    def body(x_ref, g_ref, h_ref):
        xv = x_ref[...]
        r = lax.rsqrt(jnp.mean(xv * xv, axis=-1, keepdims=True) + EPS)
        h_ref[...] = (xv * r * g_ref[...]).astype(h_ref.dtype)

    return pl.pallas_call(
        body, name=name, grid=(T // tr,),
        in_specs=[pl.BlockSpec((tr, D), lambda i: (i, 0)), pl.BlockSpec((1, D), lambda i: (0, 0))],
        out_specs=pl.BlockSpec((tr, D), lambda i: (i, 0)),
        out_shape=jax.ShapeDtypeStruct((T, D), BF16),
        compiler_params=_params(("parallel",)),
    )(x, g)


def _rms_bwd(x, g, dh, dres, *, name):
    T, D = x.shape
    tr = _tile(T, 256, 8)
    epi = _epi_rms_bwd(2)

    def body(x_ref, g_ref, dh_ref, dres_ref, dx_ref, dxb_ref, dg_ref):
        dx, _, dgp = epi(dh_ref[...], x_ref[...], dres_ref[...], g_ref[...])

        @pl.when(pl.program_id(0) == 0)
        def _():
            dg_ref[...] = jnp.zeros_like(dg_ref)

        dg_ref[...] += dgp
        dx_ref[...] = dx
        dxb_ref[...] = dx.astype(dxb_ref.dtype)

    row = pl.BlockSpec((tr, D), lambda i: (i, 0))
    vec = pl.BlockSpec((1, D), lambda i: (0, 0))
    return pl.pallas_call(
        body, name=name, grid=(T // tr,),
        in_specs=[row, vec, row, row], out_specs=[row, row, vec],
        out_shape=[jax.ShapeDtypeStruct((T, D), F32), jax.ShapeDtypeStruct((T, D), BF16), jax.ShapeDtypeStruct((1, D), F32)],
        compiler_params=_params(("arbitrary",)),
    )(x, g, dh, dres)


ROW_TILE = 256


def _epi_residual_rms(acc, res, g):
    xn = acc + res
    r = lax.rsqrt(jnp.mean(xn * xn, axis=-1, keepdims=True) + EPS)
    return xn, xn * r * g


def _epi_rms_bwd(n_copies):
    def epi(dh, x, dres, g):
        r = lax.rsqrt(jnp.mean(x * x, axis=-1, keepdims=True) + EPS)
        xh = x * r
        dxh = dh * g
        dx = dres + r * (dxh - xh * jnp.mean(dxh * xh, axis=-1, keepdims=True))
        return (dx,) * n_copies + (jnp.sum(dh * xh, axis=0, keepdims=True),)
    return epi


def _final_loss(x, g, tgt, *, name):
    T, D = x.shape
    tr = _tile(T, 256, 8)

    def body(x_ref, g_ref, t_ref, dx_ref, dg_ref, loss_ref):
        xv = x_ref[...]
        r = lax.rsqrt(jnp.mean(xv * xv, axis=-1, keepdims=True) + EPS)
        xh = xv * r
        gv = g_ref[...]
        err = xh * gv - t_ref[...]

        @pl.when(pl.program_id(0) == 0)
        def _():
            dg_ref[...] = jnp.zeros_like(dg_ref)
            loss_ref[...] = jnp.zeros_like(loss_ref)

        part = 0.5 * jnp.sum(jnp.mean(err * err, axis=-1, keepdims=True), axis=0, keepdims=True)
        loss_ref[...] += jnp.broadcast_to(part, loss_ref.shape)
        dy = err * (1.0 / D)
        dg_ref[...] += jnp.sum(dy * xh, axis=0, keepdims=True)
        dxh = dy * gv
        dx_ref[...] = r * (dxh - xh * jnp.mean(dxh * xh, axis=-1, keepdims=True))

    row = pl.BlockSpec((tr, D), lambda i: (i, 0))
    vec = pl.BlockSpec((1, D), lambda i: (0, 0))
    return pl.pallas_call(
        body, name=name, grid=(T // tr,),
        in_specs=[row, vec, row], out_specs=[row, vec, pl.BlockSpec((1, LANE), lambda i: (0, 0))],
        out_shape=[jax.ShapeDtypeStruct((T, D), F32), jax.ShapeDtypeStruct((1, D), F32),
                   jax.ShapeDtypeStruct((1, LANE), F32)],
        compiler_params=_params(("arbitrary",)),
    )(x, g, tgt)


def _ple_bwd(dx3, pp, sg, *, name):
    T, D = dx3.shape
    tr = _tile(T, 256, 8)

    def body(dx_ref, pp_ref, sg_ref, dpg_ref, dpp_ref):
        dx, s = dx_ref[...], sg_ref[...]
        dpg_ref[...] = (dx * pp_ref[...] * s * (1.0 - s)).astype(dpg_ref.dtype)
        dpp_ref[...] = (dx * s).astype(dpp_ref.dtype)

    row = pl.BlockSpec((tr, D), lambda i: (i, 0))
    return pl.pallas_call(
        body, name=name, grid=(T // tr,), in_specs=[row, row, row], out_specs=[row, row],
        out_shape=[jax.ShapeDtypeStruct((T, D), BF16)] * 2, compiler_params=_params(("parallel",)),
    )(dx3, pp, sg)


ROWS_QKV_FWD, ROWS_QKV_BWD, ROWS_FFN_FWD, ROWS_FFN_BWD, ROWS_GROUP_A = 512, 256, 256, 128, 256


def _ext(ref, r0, T, before, after, RC):
    parts = []
    if before:
        p0 = pl.multiple_of(jnp.maximum(r0 - 8, 0), 8)
        parts.append(jnp.where(r0 > 0, ref[pl.ds(p0, 8), :], 0.0))
    parts.append(ref[pl.ds(r0, RC), :])
    if after:
        n0 = pl.multiple_of(jnp.minimum(r0 + RC, T - 8), 8)
        parts.append(jnp.where(r0 + RC < T, ref[pl.ds(n0, 8), :], 0.0))
    return parts[0] if len(parts) == 1 else jnp.concatenate(parts, axis=0)


def _down(xx, s):
    return (xx if s == 0 else pltpu.roll(xx, s, 0))[8:, :]


def _up(xx, s, rows):
    return (xx if s == 0 else pltpu.roll(xx, xx.shape[0] - s, 0))[:rows, :]


def _conv_down(xx, w_ref, K):
    y = None
    for j in range(K):
        t = _down(xx, K - 1 - j) * w_ref[j:j + 1, :]
        y = t if y is None else y + t
    return y


def _fold8(x):
    return jnp.sum(x.reshape(x.shape[0] // 8, 8, x.shape[1]), axis=0)


def _silu(x):
    return x * jax.nn.sigmoid(x)


def _dsilu(x):
    s = jax.nn.sigmoid(x)
    return s * (1.0 + x * (1.0 - s))


def _col_specs(T, offs):
    return [pl.BlockSpec((T, LANE), functools.partial(lambda o, j: (0, o + j), o)) for o in offs]


def _group_a_fwd(proj, conv_w, CW, *, name):
    T = proj.shape[0]
    RC = _tile(T, ROWS_GROUP_A, 8)
    nb = CW // LANE
    K = conv_w.shape[0]

    def body(ax_ref, ab_ref, ac_ref, w_ref, y_ref):
        def step(i, carry):
            r0 = pl.multiple_of(i * RC, RC)
            m = _ext(ac_ref, r0, T, True, False, RC) * _ext(ax_ref, r0, T, True, False, RC)
            y_ref[pl.ds(r0, RC), :] = (ab_ref[pl.ds(r0, RC), :] * _conv_down(m, w_ref, K)).astype(y_ref.dtype)
            return carry
        lax.fori_loop(0, T // RC, step, 0)

    return pl.pallas_call(
        body, name=name, grid=(nb,),
        in_specs=_col_specs(T, (0, nb, 2 * nb)) + [pl.BlockSpec((K, LANE), lambda j: (0, j))],
        out_specs=pl.BlockSpec((T, LANE), lambda j: (0, j)),
        out_shape=jax.ShapeDtypeStruct((T, CW), BF16), compiler_params=_params(("parallel",)),
    )(proj, proj, proj, conv_w)


def _group_a_bwd(proj, conv_w, dycat, CW, *, name):
    T = proj.shape[0]
    RC = _tile(T, ROWS_GROUP_A, 8)
    nb = CW // LANE
    K = conv_w.shape[0]

    def body(ax_ref, ab_ref, ac_ref, w_ref, dy_ref, dax_ref, dab_ref, dac_ref, dw_ref):
        def step(i, accs):
            r0 = pl.multiple_of(i * RC, RC)
            ax3 = _ext(ax_ref, r0, T, True, True, RC)
            ac3 = _ext(ac_ref, r0, T, True, True, RC)
            m3 = ax3 * ac3
            c = _conv_down(m3[:RC + 8], w_ref, K)
            dy = dy_ref[pl.ds(r0, RC), :]
            dab_ref[pl.ds(r0, RC), :] = (dy * c).astype(dab_ref.dtype)
            dc2 = _ext(dy_ref, r0, T, False, True, RC) * _ext(ab_ref, r0, T, False, True, RC)
            dm = None
            new = []
            for j in range(K):
                s = K - 1 - j
                t = _up(dc2, s, RC) * w_ref[j:j + 1, :]
                dm = t if dm is None else dm + t
                new.append(accs[j] + _fold8(dc2[:RC] * _down(m3[:RC + 8], s)))
            dax_ref[pl.ds(r0, RC), :] = (dm * ac3[8:RC + 8]).astype(dax_ref.dtype)
            dac_ref[pl.ds(r0, RC), :] = (dm * ax3[8:RC + 8]).astype(dac_ref.dtype)
            return tuple(new)

        accs = lax.fori_loop(0, T // RC, step, tuple(jnp.zeros((8, LANE), F32) for _ in range(K)))
        for j in range(K):
            dw_ref[j:j + 1, :] = jnp.sum(accs[j], axis=0, keepdims=True)

    col = pl.BlockSpec((T, LANE), lambda j: (0, j))
    wsp = pl.BlockSpec((K, LANE), lambda j: (0, j))
    return pl.pallas_call(
        body, name=name, grid=(nb,),
        in_specs=_col_specs(T, (0, nb, 2 * nb)) + [wsp, col],
        out_specs=[col, col, col, wsp],
        out_shape=[jax.ShapeDtypeStruct((T, CW), BF16)] * 3 + [jax.ShapeDtypeStruct((K, CW), F32)],
        compiler_params=_params(("parallel",)),
    )(proj, proj, proj, conv_w, dycat)


def _qkv_fwd(proj, conv_w, off, H, *, name):
    T = proj.shape[0]
    RC = _tile(T, ROWS_QKV_FWD, 8)
    nb = 3 * H
    K = conv_w.shape[0]

    def body(x_ref, w_ref, y_ref):
        j = pl.program_id(0)
        is_qk = j < 2 * H
        scale = jnp.where(j < H, HEAD ** -0.5, 1.0).astype(F32)

        def step(i, carry):
            r0 = pl.multiple_of(i * RC, RC)
            s = _silu(_conv_down(_ext(x_ref, r0, T, True, False, RC), w_ref, K))
            r = lax.rsqrt(jnp.sum(s * s, axis=-1, keepdims=True) + EPS) * scale
            y_ref[pl.ds(r0, RC), :] = s * jnp.where(is_qk, r, 1.0)
            return carry
        lax.fori_loop(0, T // RC, step, 0)

    return pl.pallas_call(
        body, name=name, grid=(nb,),
        in_specs=_col_specs(T, (off,)) + [pl.BlockSpec((K, LANE), lambda j: (0, j))],
        out_specs=pl.BlockSpec((T, LANE), lambda j: (0, j)),
        out_shape=jax.ShapeDtypeStruct((T, nb * LANE), F32), compiler_params=_params(("parallel",)),
    )(proj, conv_w)


def _qkv_bwd(proj, conv_w, dq, dk, dv, off, H, *, name):
    T = proj.shape[0]
    RC = _tile(T, ROWS_QKV_BWD, 8)
    nb = 3 * H
    K = conv_w.shape[0]

    def body(x_ref, w_ref, dq_ref, dk_ref, dv_ref, dx_ref, dw_ref):
        j = pl.program_id(0)
        is_qk = j < 2 * H
        scale = jnp.where(j < H, HEAD ** -0.5, 1.0).astype(F32)

        def step(i, accs):
            r0 = pl.multiple_of(i * RC, RC)
            x3 = _ext(x_ref, r0, T, True, True, RC)
            c2 = _conv_down(x3, w_ref, K)
            s2 = _silu(c2)
            dn2 = jnp.where(j < H, _ext(dq_ref, r0, T, False, True, RC),
                            jnp.where(is_qk, _ext(dk_ref, r0, T, False, True, RC), _ext(dv_ref, r0, T, False, True, RC)))
            r = lax.rsqrt(jnp.sum(s2 * s2, axis=-1, keepdims=True) + EPS)
            nh = s2 * r
            dnp = dn2 * scale
            ds_qk = r * (dnp - nh * jnp.sum(dnp * nh, axis=-1, keepdims=True))
            ds2 = jnp.where(is_qk, ds_qk, dn2)
            dc2 = ds2 * _dsilu(c2)
            dx = None
            new = []
            for jj in range(K):
                s = K - 1 - jj
                t = _up(dc2, s, RC) * w_ref[jj:jj + 1, :]
                dx = t if dx is None else dx + t
                new.append(accs[jj] + _fold8(dc2[:RC] * _down(x3[:RC + 8], s)))
            dx_ref[pl.ds(r0, RC), :] = dx.astype(dx_ref.dtype)
            return tuple(new)

        accs = lax.fori_loop(0, T // RC, step, tuple(jnp.zeros((8, LANE), F32) for _ in range(K)))
        for jj in range(K):
            dw_ref[jj:jj + 1, :] = jnp.sum(accs[jj], axis=0, keepdims=True)

    col = pl.BlockSpec((T, LANE), lambda j: (0, j))
    wsp = pl.BlockSpec((K, LANE), lambda j: (0, j))
    return pl.pallas_call(
        body, name=name, grid=(nb,),
        in_specs=_col_specs(T, (off,)) + [wsp] + [
            pl.BlockSpec((T, LANE), functools.partial(lambda o, j: (0, jnp.clip(j - o, 0, H - 1)), o)) for o in (0, H, 2 * H)],
        out_specs=[col, wsp],
        out_shape=[jax.ShapeDtypeStruct((T, nb * LANE), BF16), jax.ShapeDtypeStruct((K, nb * LANE), F32)],
        compiler_params=_params(("parallel",)),
    )(proj, conv_w, dq, dk, dv)


def _softplus(x):
    return jnp.maximum(x, 0.0) + jnp.log(1.0 + jnp.exp(-jnp.abs(x)))


def _gates_fwd(proj, alog, dtb, off, H, *, name):
    T = proj.shape[0]
    tr = _tile(T, 512, CHUNK)

    def body(ab_ref, al_ref, dt_ref, gb_ref, gam_ref):
        ab = ab_ref[...]
        lane = lax.broadcasted_iota(jnp.int32, ab.shape, 1)
        g = -jnp.exp(al_ref[...]) * _softplus(ab + dt_ref[...])
        gb = jnp.where(lane < H, g, jnp.where(lane < 2 * H, jax.nn.sigmoid(ab), 0.0))
        gb_ref[...] = gb
        tril = _tri().astype(F32)
        for c in range(tr // CHUNK):
            rows = slice(c * CHUNK, (c + 1) * CHUNK)
            gam_ref[rows, :] = _mm(tril, gb[rows, :], precision=lax.Precision.HIGHEST)

    vec = pl.BlockSpec((1, LANE), lambda i: (0, 0))
    row = pl.BlockSpec((tr, LANE), lambda i: (i, 0))
    return pl.pallas_call(
        body, name=name, grid=(T // tr,),
        in_specs=[pl.BlockSpec((tr, LANE), lambda i: (i, off)), vec, vec],
        out_specs=[row, row],
        out_shape=[jax.ShapeDtypeStruct((T, LANE), F32)] * 2, compiler_params=_params(("parallel",)),
    )(proj, alog, dtb)


def _gates_bwd(proj, alog, dtb, dgb, off, H, *, name):
    T = proj.shape[0]
    tr = _tile(T, 512, CHUNK)

    def body(ab_ref, al_ref, dt_ref, d_ref, dab_ref, dal_ref, ddt_ref):
        ab, d = ab_ref[...], d_ref[...]
        lane = lax.broadcasted_iota(jnp.int32, ab.shape, 1)
        is_g = lane < H
        triu = _tri(upper=True).astype(F32)
        dg = jnp.concatenate([_mm(triu, d[c * CHUNK:(c + 1) * CHUNK, :], precision=lax.Precision.HIGHEST)
                              for c in range(tr // CHUNK)], axis=0)
        z = ab + dt_ref[...]
        A = -jnp.exp(al_ref[...])
        da = dg * A * jax.nn.sigmoid(z)
        beta = jax.nn.sigmoid(ab)
        db = d * beta * (1.0 - beta)
        dab_ref[...] = jnp.where(is_g, da, jnp.where(lane < 2 * H, db, 0.0)).astype(dab_ref.dtype)

        @pl.when(pl.program_id(0) == 0)
        def _():
            dal_ref[...] = jnp.zeros_like(dal_ref)
            ddt_ref[...] = jnp.zeros_like(ddt_ref)

        dal_ref[...] += jnp.sum(jnp.where(is_g, dg * A * _softplus(z), 0.0), axis=0, keepdims=True)
        ddt_ref[...] += jnp.sum(jnp.where(is_g, da, 0.0), axis=0, keepdims=True)

    vec = pl.BlockSpec((1, LANE), lambda i: (0, 0))
    row = pl.BlockSpec((tr, LANE), lambda i: (i, 0))
    return pl.pallas_call(
        body, name=name, grid=(T // tr,),
        in_specs=[pl.BlockSpec((tr, LANE), lambda i: (i, off)), vec, vec, row],
        out_specs=[row, vec, vec],
        out_shape=[jax.ShapeDtypeStruct((T, LANE), BF16), jax.ShapeDtypeStruct((1, LANE), F32),
                   jax.ShapeDtypeStruct((1, LANE), F32)],
        compiler_params=_params(("arbitrary",)),
    )(proj, alog, dtb, dgb)


def _gated_norm_fwd(o, proj, gn, zoff, *, name):
    T, W = o.shape
    tr = _tile(T, 512, 8)

    def body(o_ref, z_ref, g_ref, y_ref):
        ov = o_ref[...]
        r = lax.rsqrt(jnp.mean(ov * ov, axis=-1, keepdims=True) + EPS)
        y_ref[...] = (ov * r * g_ref[...] * _silu(z_ref[...])).astype(y_ref.dtype)

    blk = pl.BlockSpec((tr, LANE), lambda i, j: (i, j))
    return pl.pallas_call(
        body, name=name, grid=(T // tr, W // LANE),
        in_specs=[blk, pl.BlockSpec((tr, LANE), lambda i, j: (i, zoff + j)), pl.BlockSpec((1, LANE), lambda i, j: (0, 0))],
        out_specs=blk, out_shape=jax.ShapeDtypeStruct((T, W), BF16), compiler_params=_params(("parallel", "parallel")),
    )(o, proj, gn)


def _gated_norm_bwd(o, proj, gn, dycat, zoff, yoff, *, name):
    T, W = o.shape
    tr = _tile(T, 512, 8)

    def body(o_ref, z_ref, g_ref, dy_ref, do_ref, dz_ref, dg_ref):
        ov, zv, gv, dy = o_ref[...], z_ref[...], g_ref[...], dy_ref[...]
        r = lax.rsqrt(jnp.mean(ov * ov, axis=-1, keepdims=True) + EPS)
        nh = ov * r
        s = _silu(zv)

        @pl.when((pl.program_id(0) == 0) & (pl.program_id(1) == 0))
        def _():
            dg_ref[...] = jnp.zeros_like(dg_ref)

        dg_ref[...] += jnp.sum(dy * nh * s, axis=0, keepdims=True)
        dz_ref[...] = (dy * nh * gv * _dsilu(zv)).astype(dz_ref.dtype)
        dn = dy * gv * s
        do_ref[...] = r * (dn - nh * jnp.mean(dn * nh, axis=-1, keepdims=True))

    blk = pl.BlockSpec((tr, LANE), lambda i, j: (i, j))
    vec = pl.BlockSpec((1, LANE), lambda i, j: (0, 0))
    return pl.pallas_call(
        body, name=name, grid=(T // tr, W // LANE),
        in_specs=[blk, pl.BlockSpec((tr, LANE), lambda i, j: (i, zoff + j)), vec,
                  pl.BlockSpec((tr, LANE), lambda i, j: (i, yoff + j))],
        out_specs=[blk, blk, vec],
        out_shape=[jax.ShapeDtypeStruct((T, W), F32), jax.ShapeDtypeStruct((T, W), BF16),
                   jax.ShapeDtypeStruct((1, LANE), F32)],
        compiler_params=_params(("arbitrary", "arbitrary")),
    )(o, proj, gn, dycat)


def _ffn_act_fwd(up_pre, conv_w, *, name):
    T, F2 = up_pre.shape
    RC = _tile(T, ROWS_FFN_FWD, 8)
    nb = F2 // 2 // LANE
    K = conv_w.shape[0]

    def body(g_ref, v_ref, wg_ref, wv_ref, y_ref):
        def step(i, carry):
            r0 = pl.multiple_of(i * RC, RC)
            gate = _conv_down(_ext(g_ref, r0, T, True, False, RC), wg_ref, K)
            val = _conv_down(_ext(v_ref, r0, T, True, False, RC), wv_ref, K)
            y_ref[pl.ds(r0, RC), :] = (_silu(gate) * val).astype(y_ref.dtype)
            return carry
        lax.fori_loop(0, T // RC, step, 0)

    return pl.pallas_call(
        body, name=name, grid=(nb,),
        in_specs=_col_specs(T, (0, nb)) + [pl.BlockSpec((K, LANE), lambda j: (0, j)),
                                           pl.BlockSpec((K, LANE), lambda j: (0, nb + j))],
        out_specs=pl.BlockSpec((T, LANE), lambda j: (0, j)),
        out_shape=jax.ShapeDtypeStruct((T, F2 // 2), BF16), compiler_params=_params(("parallel",)),
    )(up_pre, up_pre, conv_w, conv_w)


def _ffn_act_bwd(up_pre, conv_w, dact, *, name):
    T, F2 = up_pre.shape
    RC = _tile(T, ROWS_FFN_BWD, 8)
    nb = F2 // 2 // LANE
    K = conv_w.shape[0]

    def body(g_ref, v_ref, wg_ref, wv_ref, da_ref, dg_ref, dv_ref, dwg_ref, dwv_ref):
        def step(i, accs):
            r0 = pl.multiple_of(i * RC, RC)
            g3 = _ext(g_ref, r0, T, True, True, RC)
            v3 = _ext(v_ref, r0, T, True, True, RC)
            gate2 = _conv_down(g3, wg_ref, K)
            val2 = _conv_down(v3, wv_ref, K)
            da2 = _ext(da_ref, r0, T, False, True, RC)
            dgate2 = da2 * val2 * _dsilu(gate2)
            dval2 = da2 * _silu(gate2)
            dgp, dvp, new = None, None, []
            for j in range(K):
                s = K - 1 - j
                tg = _up(dgate2, s, RC) * wg_ref[j:j + 1, :]
                tv = _up(dval2, s, RC) * wv_ref[j:j + 1, :]
                dgp = tg if dgp is None else dgp + tg
                dvp = tv if dvp is None else dvp + tv
                new.append(accs[2 * j] + _fold8(dgate2[:RC] * _down(g3[:RC + 8], s)))
                new.append(accs[2 * j + 1] + _fold8(dval2[:RC] * _down(v3[:RC + 8], s)))
            dg_ref[pl.ds(r0, RC), :] = dgp.astype(dg_ref.dtype)
            dv_ref[pl.ds(r0, RC), :] = dvp.astype(dv_ref.dtype)
            return tuple(new)

        accs = lax.fori_loop(0, T // RC, step, tuple(jnp.zeros((8, LANE), F32) for _ in range(2 * K)))
        for j in range(K):
            dwg_ref[j:j + 1, :] = jnp.sum(accs[2 * j], axis=0, keepdims=True)
            dwv_ref[j:j + 1, :] = jnp.sum(accs[2 * j + 1], axis=0, keepdims=True)

    col = pl.BlockSpec((T, LANE), lambda j: (0, j))
    wsp = pl.BlockSpec((K, LANE), lambda j: (0, j))
    return pl.pallas_call(
        body, name=name, grid=(nb,),
        in_specs=_col_specs(T, (0, nb)) + [wsp, pl.BlockSpec((K, LANE), lambda j: (0, nb + j)), col],
        out_specs=[col, col, wsp, wsp],
        out_shape=[jax.ShapeDtypeStruct((T, F2 // 2), BF16)] * 2 + [jax.ShapeDtypeStruct((K, F2 // 2), F32)] * 2,
        compiler_params=_params(("parallel",)),
    )(up_pre, up_pre, conv_w, conv_w, dact)


CPB = 8
CPB_SCAN = 4
GRP = 8
HP = lax.Precision.HIGH


def _tri(strict=False, upper=False):
    r = lax.broadcasted_iota(jnp.int32, (CHUNK, CHUNK), 0)
    c = lax.broadcasted_iota(jnp.int32, (CHUNK, CHUNK), 1)
    if upper:
        return c >= r
    return (r > c) if strict else (r >= c)


def _mm(a, b, dn="nn", precision=None):
    precision = HP if precision is None else precision
    return lax.dot_general(a, b, _DN[dn], precision=precision, preferred_element_type=F32)


def _mm16(a, b, dn="nn"):
    return lax.dot_general(a.astype(BF16), b.astype(BF16), _DN[dn], preferred_element_type=F32)


def _each(f, *cols):
    return [f(*xs) for xs in zip(*cols)]


def _decay(gam):
    return jnp.exp(jnp.where(_tri(), gam[:, :CHUNK] - gam.T[:CHUNK, :], -1e30))


def _delta_specs(T, H, cpb):
    rows = cpb * CHUNK
    col = lambda o: pl.BlockSpec((rows, LANE), functools.partial(lambda o, h, n: (n, o + h), o))
    bc = pl.BlockSpec((1, rows, LANE), lambda h, n: (h, n, 0))
    sq = pl.BlockSpec((1, cpb, CHUNK, CHUNK), lambda h, n: (h, n, 0, 0))
    vec = pl.BlockSpec((1, cpb, 1, LANE), lambda h, n: (h, n, 0, 0))
    return col, bc, sq, vec


def _delta_prep_fwd(qkv, gamB, bB, H, *, name):
    T = qkv.shape[0]
    N = T // CHUNK
    cpb = _tile(N, CPB, 8)
    grp = min(GRP, cpb)
    col, bc, sq, vec = _delta_specs(T, H, cpb)

    def body(q_ref, k_ref, v_ref, g_ref, b_ref, u_ref, w_ref, qd_ref, kd_ref, qk_ref, ti_ref, gl_ref):
        eye = (lax.broadcasted_iota(jnp.int32, (CHUNK, CHUNK), 0) == lax.broadcasted_iota(jnp.int32, (CHUNK, CHUNK), 1)).astype(F32)
        strict = _tri(strict=True)
        for c0 in range(0, cpb, grp):
            cs = list(range(c0, c0 + grp))
            rows = [slice(c * CHUNK, (c + 1) * CHUNK) for c in cs]
            q, k, v = ([r_[r, :] for r in rows] for r_ in (q_ref, k_ref, v_ref))
            bb = [b_ref[0, r, :] for r in rows]
            gam = [g_ref[0, r, :] for r in rows]
            D = _each(_decay, gam)
            e = _each(jnp.exp, gam)
            kk = _each(lambda k_: _mm16(k_, k_, "nt"), k)
            X = _each(lambda kk_, D_, b_: -(jnp.where(strict, kk_ * D_, 0.0) * b_[:, :CHUNK]), kk, D, bb)
            R = _each(lambda x: eye + x, X)
            for _ in range(5):
                X = _each(lambda x: _mm(x, x), X)
                R = _each(lambda r, x: r + _mm(r, x), R, X)
            u = _each(lambda r, b_, v_: _mm(r, b_ * v_), R, bb, v)
            w = _each(lambda r, b_, e_, k_: _mm(r, b_ * e_ * k_), R, bb, e, k)
            qk = _each(lambda q_, k_, D_: _mm16(q_, k_, "nt") * D_, q, k, D)
            for i, c in enumerate(cs):
                glast = gam[i][CHUNK - 1:CHUNK, :]
                u_ref[rows[i], :] = u[i]
                w_ref[rows[i], :] = w[i]
                qd_ref[rows[i], :] = e[i] * q[i]
                kd_ref[rows[i], :] = jnp.exp(glast - gam[i]) * k[i]
                qk_ref[0, c] = qk[i]
                ti_ref[0, c] = R[i]
                gl_ref[0, c] = jnp.exp(glast)

    full = jax.ShapeDtypeStruct((T, H * LANE), F32)
    sqs = jax.ShapeDtypeStruct((H, N, CHUNK, CHUNK), F32)
    return pl.pallas_call(
        body, name=name, grid=(H, N // cpb),
        in_specs=[col(0), col(H), col(2 * H), bc, bc],
        out_specs=[col(0)] * 4 + [sq, sq, vec],
        out_shape=[full] * 4 + [sqs, sqs, jax.ShapeDtypeStruct((H, N, 1, LANE), F32)],
        compiler_params=_params(("parallel", "parallel")),
    )(qkv, qkv, qkv, gamB, bB)


def _scan_specs(H, N, cpb, hb, rev):
    nbk = N // cpb
    blk = (lambda n: nbk - 1 - n) if rev else (lambda n: n)
    col = pl.BlockSpec((cpb * CHUNK, hb * LANE), lambda h, n: (blk(n), h))
    sq = pl.BlockSpec((hb, cpb, CHUNK, CHUNK), lambda h, n: (h, blk(n), 0, 0))
    vec = pl.BlockSpec((hb, cpb, 1, LANE), lambda h, n: (h, blk(n), 0, 0))
    st = pl.BlockSpec((hb, cpb, HEAD, HEAD), lambda h, n: (h, blk(n), 0, 0))
    return col, sq, vec, st


def _delta_scan_fwd(u, w, qd, kd, qk, gl, H, *, name):
    T = u.shape[0]
    N = T // CHUNK
    cpb = _tile(N, CPB_SCAN, 4)
    hb = min(GRP, H)
    col, sq, vec, st = _scan_specs(H, N, cpb, hb, False)
    lanes = [slice(j * LANE, (j + 1) * LANE) for j in range(hb)]
    heads = list(range(hb))

    def body(u_ref, w_ref, qd_ref, kd_ref, qk_ref, gl_ref, o_ref, vn_ref, ss_ref, s_scr):
        @pl.when(pl.program_id(1) == 0)
        def _():
            s_scr[...] = jnp.zeros_like(s_scr)

        def step(c, states):
            rows = pl.ds(pl.multiple_of(c * CHUNK, CHUNK), CHUNK)
            S = list(states)
            for j in heads:
                ss_ref[j, c] = S[j]
            wS = _each(lambda ln, s: _mm16(w_ref[rows, ln], s), lanes, S)
            qS = _each(lambda ln, s: _mm16(qd_ref[rows, ln], s), lanes, S)
            vn = _each(lambda ln, ws: u_ref[rows, ln] - ws, lanes, wS)
            o = _each(lambda j, qs, vn_: qs + _mm16(qk_ref[j, c], vn_), heads, qS, vn)
            new = _each(lambda j, ln, s, vn_: s * gl_ref[j, c] + _mm16(kd_ref[rows, ln], vn_, "tn"),
                        heads, lanes, S, vn)
            for j in heads:
                o_ref[rows, lanes[j]] = o[j]
                vn_ref[rows, lanes[j]] = vn[j]
            return tuple(new)
        out = lax.fori_loop(0, cpb, step, tuple(s_scr[j] for j in heads))
        for j in heads:
            s_scr[j] = out[j]

    full = jax.ShapeDtypeStruct((T, H * LANE), F32)
    return pl.pallas_call(
        body, name=name, grid=(H // hb, N // cpb),
        in_specs=[col] * 4 + [sq, vec],
        out_specs=[col, col, st],
        out_shape=[full, full, jax.ShapeDtypeStruct((H, N, HEAD, HEAD), F32)],
        scratch_shapes=[pltpu.VMEM((hb, HEAD, HEAD), F32)],
        compiler_params=_params(("parallel", "arbitrary")),
    )(u, w, qd, kd, qk, gl)


def _delta_scan_bwd(do, w, qd, kd, vn, qk, gl, ss, H, *, name):
    T = do.shape[0]
    N = T // CHUNK
    cpb = _tile(N, CPB_SCAN, 4)
    hb = min(GRP, H)
    col, sq, vec, st = _scan_specs(H, N, cpb, hb, True)
    lanes = [slice(j * LANE, (j + 1) * LANE) for j in range(hb)]
    heads = list(range(hb))

    def body(do_ref, w_ref, qd_ref, kd_ref, vn_ref, qk_ref, gl_ref, ss_ref,
             du_ref, dw_ref, dqd_ref, dkd_ref, dqk_ref, dgl_ref, ds_scr):
        @pl.when(pl.program_id(1) == 0)
        def _():
            ds_scr[...] = jnp.zeros_like(ds_scr)

        def step(i, dstates):
            c = cpb - 1 - i
            rows = pl.ds(pl.multiple_of(c * CHUNK, CHUNK), CHUNK)
            dS = list(dstates)
            S = [ss_ref[j, c] for j in heads]
            dov = [do_ref[rows, ln] for ln in lanes]
            vnv = [vn_ref[rows, ln] for ln in lanes]
            a1 = _each(lambda j, d_: _mm16(qk_ref[j, c], d_, "tn"), heads, dov)
            a2 = _each(lambda ln, ds: _mm16(kd_ref[rows, ln], ds), lanes, dS)
            dvn = _each(lambda x, y: x + y, a1, a2)
            dqd = _each(lambda d_, s: _mm16(d_, s, "nt"), dov, S)
            dkd = _each(lambda v_, ds: _mm16(v_, ds, "nt"), vnv, dS)
            dqk = _each(lambda d_, v_: _mm16(d_, v_, "nt"), dov, vnv)
            dw = _each(lambda dv_, s: -_mm16(dv_, s, "nt"), dvn, S)
            b1 = _each(lambda ln, d_: _mm16(qd_ref[rows, ln], d_, "tn"), lanes, dov)
            b2 = _each(lambda ln, dv_: _mm16(w_ref[rows, ln], dv_, "tn"), lanes, dvn)
            new = _each(lambda j, x, y, ds: x + ds * gl_ref[j, c] - y, heads, b1, b2, dS)
            for j in heads:
                du_ref[rows, lanes[j]] = dvn[j]
                dw_ref[rows, lanes[j]] = dw[j]
                dqd_ref[rows, lanes[j]] = dqd[j]
                dkd_ref[rows, lanes[j]] = dkd[j]
                dqk_ref[j, c] = dqk[j]
                dgl = jnp.sum(jnp.sum(dS[j] * S[j], axis=1, keepdims=True), axis=0, keepdims=True)
                dgl_ref[j, c] = jnp.broadcast_to(dgl, (1, LANE))
            return tuple(new)
        out = lax.fori_loop(0, cpb, step, tuple(ds_scr[j] for j in heads))
        for j in heads:
            ds_scr[j] = out[j]

    full = jax.ShapeDtypeStruct((T, H * LANE), F32)
    return pl.pallas_call(
        body, name=name, grid=(H // hb, N // cpb),
        in_specs=[col] * 5 + [sq, vec, st],
        out_specs=[col] * 4 + [sq, vec],
        out_shape=[full] * 4 + [jax.ShapeDtypeStruct((H, N, CHUNK, CHUNK), F32), jax.ShapeDtypeStruct((H, N, 1, LANE), F32)],
        scratch_shapes=[pltpu.VMEM((hb, HEAD, HEAD), F32)],
        compiler_params=_params(("parallel", "arbitrary")),
    )(do, w, qd, kd, vn, qk, gl, ss)


def _delta_prep_bwd(qkv, gamB, bB, ti, u, w, qk, du, dw, dqd, dkd, dqk, dgl, H, *, name):
    T = qkv.shape[0]
    N = T // CHUNK
    cpb = _tile(N, CPB, 8)
    grp = min(GRP, cpb)
    col, bc, sq, vec = _delta_specs(T, H, cpb)

    def body(q_ref, k_ref, v_ref, g_ref, b_ref, ti_ref, u_ref, w_ref, qk_ref,
             du_ref, dw_ref, dqd_ref, dkd_ref, dqk_ref, dgl_ref,
             dq_ref, dk_ref, dv_ref, dg_ref, db_ref):
        ones = jnp.ones((CHUNK, LANE), F32)
        strict = _tri(strict=True)
        last = lax.broadcasted_iota(jnp.int32, (CHUNK, LANE), 0) == CHUNK - 1
        lsum = lambda x: jnp.sum(x, axis=-1, keepdims=True)
        for c0 in range(0, cpb, grp):
            cs = list(range(c0, c0 + grp))
            rows = [slice(c * CHUNK, (c + 1) * CHUNK) for c in cs]
            ld = lambda r_: [r_[r, :] for r in rows]
            q, k, v, uv, wv, duv, dwv, dqd_v, dkd_v = (ld(r_) for r_ in (q_ref, k_ref, v_ref, u_ref, w_ref, du_ref, dw_ref, dqd_ref, dkd_ref))
            bb = [b_ref[0, r, :] for r in rows]
            gam = [g_ref[0, r, :] for r in rows]
            Ti = [ti_ref[0, c] for c in cs]
            QK = [qk_ref[0, c] for c in cs]
            dqk_v = [dqk_ref[0, c] for c in cs]
            D = _each(_decay, gam)
            e = _each(jnp.exp, gam)
            glast = [g_[CHUNK - 1:CHUNK, :] for g_ in gam]
            eL = _each(lambda gl_, g_: jnp.exp(gl_ - g_), glast, gam)
            kk = _each(lambda k_: _mm16(k_, k_, "nt"), k)
            KKD = _each(lambda kk_, D_: jnp.where(strict, kk_ * D_, 0.0), kk, D)
            dru = _each(lambda t, d_: _mm(t, d_, "tn"), Ti, duv)
            drw = _each(lambda t, d_: _mm(t, d_, "tn"), Ti, dwv)
            l1 = _each(lambda a, b: _mm(a, b, "nt"), dru, uv)
            l2 = _each(lambda a, b: _mm(a, b, "nt"), drw, wv)
            dL = _each(lambda a, b: jnp.where(strict, -(a + b), 0.0), l1, l2)
            Mm = _each(lambda dl, b_: dl * b_[:, :CHUNK], dL, bb)
            dKK = _each(lambda m_, D_: m_ * D_, Mm, D)
            dQK = _each(lambda a, D_: a * D_, dqk_v, D)
            P = _each(lambda m_, kkd, a, qk_: m_ * kkd + a * qk_, Mm, KKD, dqk_v, QK)
            q1 = _each(lambda a, k_: _mm16(a, k_), dQK, k)
            k1 = _each(lambda a, q_: _mm16(a, q_, "tn"), dQK, q)
            k2 = _each(lambda a, k_: _mm16(a, k_), dKK, k)
            k3 = _each(lambda a, k_: _mm16(a, k_, "tn"), dKK, k)
            s1 = _each(lambda dl, kkd: _mm(dl * kkd, ones), dL, KKD)
            p1 = _each(lambda p_: _mm(p_, ones), P)
            p2 = _each(lambda p_: _mm(p_, ones, "tn"), P)
            for i, c in enumerate(cs):
                r = rows[i]
                bek = bb[i] * e[i]
                kdv = eL[i] * k[i]
                dq_ref[r, :] = q1[i] + e[i] * dqd_v[i]
                dk_ref[r, :] = k1[i] + k2[i] + k3[i] + bek * drw[i] + eL[i] * dkd_v[i]
                dv_ref[r, :] = bb[i] * dru[i]
                db_ref[0, r, :] = s1[i] + lsum(dru[i] * v[i]) + lsum(drw[i] * e[i] * k[i])
                dgam = (p1[i] - p2[i] + lsum(drw[i] * bek * k[i]) + lsum(dqd_v[i] * e[i] * q[i])
                        - lsum(dkd_v[i] * kdv))
                xlast = jnp.sum(lsum(dkd_v[i] * kdv), axis=0, keepdims=True) + jnp.exp(glast[i]) * dgl_ref[0, c]
                dg_ref[0, r, :] = dgam + jnp.where(last, xlast, 0.0)

    full = jax.ShapeDtypeStruct((T, H * LANE), F32)
    bcs = jax.ShapeDtypeStruct((H, T, LANE), F32)
    return pl.pallas_call(
        body, name=name, grid=(H, N // cpb),
        in_specs=[col(0), col(H), col(2 * H), bc, bc, sq, col(0), col(0), sq, col(0), col(0), col(0), col(0), sq, vec],
        out_specs=[col(0), col(0), col(0), bc, bc],
        out_shape=[full, full, full, bcs, bcs],
        compiler_params=_params(("parallel", "parallel")),
    )(qkv, qkv, qkv, gamB, bB, ti, u, w, qk, du, dw, dqd, dkd, dqk, dgl)


def _adam(parts, w, m, v, *, name, own=None, me=None):
    P, R, C = parts.shape
    if R > 256 and R % 8:
        tr, tc = R, _tile(C, 256)
    else:
        tr, tc = _tile(R, 256, 8), C
    n_own = 0 if own is None else 2

    def body(*refs):
        p_ref, w_ref, m_ref, v_ref, g_ref, d_ref, nm_ref, nv_ref = refs[n_own:]
        g = None
        for i in range(P):
            t = p_ref[i].astype(F32)
            if n_own:
                t = jnp.where(refs[0][0] == i, refs[1][...].astype(F32), t)
            g = t if g is None else g + t
        mn = ADAM_B1 * m_ref[...] + (1.0 - ADAM_B1) * g
        vn = ADAM_B2 * v_ref[...] + (1.0 - ADAM_B2) * (g * g)
        m_hat = mn / (1.0 - ADAM_B1 ** ADAM_STEP)
        v_hat = vn / (1.0 - ADAM_B2 ** ADAM_STEP)
        g_ref[...] = g
        d_ref[...] = -ADAM_LR * (m_hat / (jnp.sqrt(v_hat) + ADAM_EPS) + ADAM_WD * w_ref[...])
        nm_ref[...] = mn
        nv_ref[...] = vn

    blk = pl.BlockSpec((tr, tc), lambda i, j: (i, j))
    return pl.pallas_call(
        body, name=name, grid=(R // tr, C // tc),
        in_specs=[pl.BlockSpec(memory_space=pltpu.SMEM), blk][:n_own] + [pl.BlockSpec((P, tr, tc), lambda i, j: (0, i, j)), blk, blk, blk],
        out_specs=[blk] * 4, out_shape=[jax.ShapeDtypeStruct((R, C), F32)] * 4,
        compiler_params=_params(("parallel", "parallel")),
    )(*([me, own] if n_own else []), parts, w, m, v)


def _mesh_pos():
    return lax.axis_index("x"), lax.axis_index("y"), lax.axis_index("c")


def _peer(k):
    x, y, c = _mesh_pos()
    px, py, pc = x ^ ((k >> 2) & 1), y ^ ((k >> 1) & 1), c ^ (k & 1)
    return (px, py, pc), 4 * px + 2 * py + pc


def _exchange(arrays, scatter, *, name, after=None):
    n = len(arrays)
    n_in = n if after is None else n + 1
    blocks = [a.shape[1:] if scatter else a.shape for a in arrays]

    def body(*refs):
        srcs, dsts = refs[:n], refs[n_in:n_in + n]
        send_sems, recv_sems, local_sems = refs[n_in + n:]
        x, y, c = _mesh_pos()
        me = 4 * x + 2 * y + c
        local, sends = [], []
        for a in range(n):
            cp = pltpu.make_async_copy(srcs[a].at[me] if scatter else srcs[a], dsts[a].at[me], local_sems.at[a])
            cp.start()
            local.append(cp)
            for k in range(1, N_DEV):
                dev, idx = _peer(k)
                cp = pltpu.make_async_remote_copy(
                    src_ref=srcs[a].at[idx] if scatter else srcs[a], dst_ref=dsts[a].at[me],
                    send_sem=send_sems.at[a * N_DEV + k], recv_sem=recv_sems.at[a * N_DEV + k],
                    device_id=dev, device_id_type=MESH)
                cp.start()
                sends.append(cp)
        for a in range(n):
            for k in range(1, N_DEV):
                dev, idx = _peer(k)
                pltpu.make_async_remote_copy(
                    src_ref=srcs[a].at[idx] if scatter else srcs[a], dst_ref=dsts[a].at[idx],
                    send_sem=send_sems.at[a * N_DEV + k], recv_sem=recv_sems.at[a * N_DEV + k],
                    device_id=dev, device_id_type=MESH).wait_recv()
        for cp in sends:
            cp.wait_send()
        for cp in local:
            cp.wait()

    anyspec = pl.BlockSpec(memory_space=pl.ANY)
    return pl.pallas_call(
        body, name=name, in_specs=[anyspec] * n_in, out_specs=[anyspec] * n,
        out_shape=[jax.ShapeDtypeStruct((N_DEV,) + tuple(b), a.dtype) for a, b in zip(arrays, blocks)],
        scratch_shapes=[pltpu.SemaphoreType.DMA((n * N_DEV,)), pltpu.SemaphoreType.DMA((n * N_DEV,)),
                        pltpu.SemaphoreType.DMA((n,))],
    )(*arrays, *([] if after is None else [after]))


_ANY = pl.BlockSpec(memory_space=pl.ANY)
_SEM = pl.BlockSpec(memory_space=pltpu.SEMAPHORE)
_EFFECT = pltpu.SideEffectType.DATAFLOW_SIDE_EFFECTING


def _in_hbm(a):
    return pltpu.with_memory_space_constraint(a, pltpu.HBM)


def _split_copy(src, land, send, recv, k, me, scatter, landed):
    dev, idx = _peer(k)
    return pltpu.make_async_remote_copy(
        src_ref=src.at[idx] if scatter else src, dst_ref=land.at[idx if landed else me],
        send_sem=send.at[k], recv_sem=recv.at[k], device_id=dev, device_id_type=MESH)


ALL_PEERS = tuple(range(1, N_DEV))
SIBLING = 1
SAME_CORE = (2, 4, 6)


def _split_start(srcs, lands, scatter, *, name, relations=None):
    n = len(srcs)
    relations = relations or [ALL_PEERS] * n

    def body(*refs):
        src, land, send, recv, token = refs[:n], refs[n:2 * n], refs[2 * n:3 * n], refs[3 * n:4 * n], refs[-1]
        x, y, c = _mesh_pos()
        me = 4 * x + 2 * y + c
        for a in range(n):
            for k in relations[a]:
                _split_copy(src[a], land[a], send[a], recv[a], k, me, scatter, False).start()
        token[...] = jnp.zeros_like(token)

    outs = pl.pallas_call(
        body, name=name,
        out_shape=[pltpu.SemaphoreType.DMA((N_DEV,))] * (2 * n) + [pltpu.HBM(t.shape, t.dtype) for t in list(srcs) + list(lands)]
        + [jax.ShapeDtypeStruct((8, LANE), F32)],
        in_specs=[_ANY] * (2 * n), out_specs=[_SEM] * (2 * n) + [_ANY] * (2 * n) + [pl.BlockSpec(memory_space=pltpu.VMEM)],
        input_output_aliases={i: 2 * n + i for i in range(2 * n)},
        compiler_params=pltpu.CompilerParams(has_side_effects=_EFFECT),
    )(*[_in_hbm(t) for t in list(srcs) + list(lands)])
    handles = [(outs[a], outs[n + a], outs[2 * n + a], outs[3 * n + a]) for a in range(n)]
    return handles, outs[-1]


def _split_wait(handle, after, scatter, *, name):
    send, recv, src_thru, land_thru = handle

    def body(src_ref, land_ref, send_ref, recv_ref, after_ref, src_out, land_out):
        x, y, c = _mesh_pos()
        me = 4 * x + 2 * y + c
        for k in range(1, N_DEV):
            cp = _split_copy(src_ref, land_ref, send_ref, recv_ref, k, me, scatter, True)
            cp.wait_send()
            cp.wait_recv()

    return pl.pallas_call(
        body, name=name,
        out_shape=(pltpu.HBM(src_thru.shape, src_thru.dtype), pltpu.HBM(land_thru.shape, land_thru.dtype)),
        in_specs=(_ANY, _ANY, _SEM, _SEM, _ANY), out_specs=(_ANY, _ANY), input_output_aliases={0: 0, 1: 1},
        compiler_params=pltpu.CompilerParams(has_side_effects=_EFFECT),
    )(src_thru, land_thru, send, recv, after)[1]


def _forward_copy(land, fsend, frecv, k, landed):
    x, y, c = _mesh_pos()
    _, idx = _peer(k | SIBLING if landed else k)
    return pltpu.make_async_remote_copy(src_ref=land.at[idx], dst_ref=land.at[idx], send_sem=fsend.at[k],
                                        recv_sem=frecv.at[k], device_id=(x, y, 1 - c), device_id_type=MESH)


def _gather_forward(handle, after, *, name):
    send, recv, src_thru, land_thru = handle

    def body(src_ref, land_ref, send_ref, recv_ref, after_ref, src_out, land_out, fsend, frecv):
        x, y, c = _mesh_pos()
        me = 4 * x + 2 * y + c
        for k in SAME_CORE:
            _split_copy(src_ref, land_ref, send_ref, recv_ref, k, me, False, True).wait_recv()
            _forward_copy(land_ref, fsend, frecv, k, False).start()

    src2, land2, fsend, frecv = pl.pallas_call(
        body, name=name,
        out_shape=(pltpu.HBM(src_thru.shape, src_thru.dtype), pltpu.HBM(land_thru.shape, land_thru.dtype),
                   pltpu.SemaphoreType.DMA((N_DEV,)), pltpu.SemaphoreType.DMA((N_DEV,))),
        in_specs=(_ANY, _ANY, _SEM, _SEM, _ANY), out_specs=(_ANY, _ANY, _SEM, _SEM), input_output_aliases={0: 0, 1: 1},
        compiler_params=pltpu.CompilerParams(has_side_effects=_EFFECT),
    )(src_thru, land_thru, send, recv, after)
    return (send, recv, src2, land2), (fsend, frecv)


def _gather_wait_two_level(handle, fwd, *, name):
    send, recv, src_thru, land_thru = handle
    fsend, frecv = fwd

    def body(src_ref, land_ref, send_ref, recv_ref, fsend_ref, frecv_ref, src_out, land_out):
        x, y, c = _mesh_pos()
        me = 4 * x + 2 * y + c
        for k in (SIBLING,) + SAME_CORE:
            _split_copy(src_ref, land_ref, send_ref, recv_ref, k, me, False, True).wait_send()
        _split_copy(src_ref, land_ref, send_ref, recv_ref, SIBLING, me, False, True).wait_recv()
        for k in SAME_CORE:
            _forward_copy(land_ref, fsend_ref, frecv_ref, k, False).wait_send()
            _forward_copy(land_ref, fsend_ref, frecv_ref, k, True).wait_recv()

    return pl.pallas_call(
        body, name=name,
        out_shape=(pltpu.HBM(src_thru.shape, src_thru.dtype), pltpu.HBM(land_thru.shape, land_thru.dtype)),
        in_specs=(_ANY, _ANY, _SEM, _SEM, _SEM, _SEM), out_specs=(_ANY, _ANY), input_output_aliases={0: 0, 1: 1},
        compiler_params=pltpu.CompilerParams(has_side_effects=_EFFECT),
    )(src_thru, land_thru, send, recv, fsend, frecv)[1]


def _local_step(x, p, tgt, S, wt, conv, emit):
    T, D = x.shape
    CW = DNW = D // 2
    H = DNW // HEAD
    nA, nD = CW // LANE, DNW // LANE
    qkv_off, z_off, ab_off = 3 * nA, 3 * nA + 3 * nD, 3 * nA + 4 * nD
    alog = jnp.pad(S["a_log"], ((0, 0), (0, LANE - H)))
    dtb = jnp.pad(S["dt_bias"], ((0, 0), (0, LANE - H)))

    h1 = _rms_fwd(x, S["g_mix"], name="rms1_fwd")
    w_in, cv = wt("w_in", h1), conv(h1)
    proj = _matmul(h1, w_in, "nt", name="mm_in")
    y_a = _group_a_fwd(proj, cv["conv_a"], CW, name="group_a_fwd")
    qkv = _qkv_fwd(proj, cv["conv_qkv"], qkv_off, H, name="qkv_fwd")
    gb, gamc = _gates_fwd(proj, alog, dtb, ab_off, H, name="gates_fwd")
    bcast = lambda cols: jnp.broadcast_to(cols.T[:, :, None], (H, T, LANE))
    gamB, bB = bcast(gamc[:, :H]), bcast(gb[:, H:2 * H])
    u, w, qd, kd, qk, ti, gl = _delta_prep_fwd(qkv, gamB, bB, H, name="delta_prep_fwd")
    o, vn, ss = _delta_scan_fwd(u, w, qd, kd, qk, gl, H, name="delta_scan_fwd")
    y_b = _gated_norm_fwd(o, proj, S["dn_g"], z_off, name="gated_norm_fwd")
    ycat = jnp.concatenate([y_a, y_b], axis=1)
    w_out = wt("w_out", ycat)
    rows = dict(tm=ROW_TILE, tn=D)
    x1, h2 = _matmul(ycat, w_out, "nn", name="mm_out", out_dtypes=(F32, BF16), epilogue=_epi_residual_rms,
                     extras=(x,), vec_extras=(S["g_ffn"],), **rows)
    w_up = wt("w_up", h2)
    up_pre = _matmul(h2, w_up, "nn", name="mm_up", b_shards=True, tn=SHARD_TILE)
    act = _ffn_act_fwd(up_pre, cv["conv_ffn"], name="ffn_act_fwd")
    w_down = wt("w_down", act)
    x2 = _matmul(act, w_down, "nn", name="mm_down", epilogue=lambda acc, r: (acc + r,), extras=(x1,), tk=LONG_K)
    h3 = _rms_fwd(x2, S["g_ple"], name="rms3_fwd")
    w_pp, w_pg = wt("w_pp", h3), wt("w_pg", h3)
    pp = _matmul(p, w_pp, "nn", name="mm_pp", b_shards=True)

    def ple_epi(acc, x2r, ppr):
        s = jax.nn.sigmoid(acc)
        return x2r + s * ppr, s

    x3, sg = _matmul(h3, w_pg, "nn", name="mm_pg", out_dtypes=(F32, F32), epilogue=ple_epi, extras=(x2, pp), tm=512)
    dx3, dg_final, loss = _final_loss(x3, S["g_final"], tgt, name="final_loss")

    G = {"g_final": dg_final}
    dpg, dpp = _ple_bwd(dx3, pp, sg, name="ple_bwd")
    tok = emit({"w_pp": _matmul(p, dpp, "tn", name="mm_dwpp", out_dtypes=(BF16,), out_shards=True, tk=LONG_K),
                "w_pg": _matmul(h3, dpg, "tn", name="mm_dwpg", out_dtypes=(BF16,), tk=LONG_K)})
    bwd = dict(out_dtypes=(F32, BF16), epilogue=_epi_rms_bwd(2), n_vec=1, **rows)
    dx2, dx2b, G["g_ple"] = _matmul(dpg, w_pg, "nt", name="mm_dh3", after=tok, extras=(x2, dx3),
                                    vec_extras=(S["g_ple"],), **bwd)
    tok = emit({"w_down": _matmul(act, dx2b, "tn", name="mm_dwdown", out_dtypes=(BF16,), tk=LONG_K)})
    dact = _matmul(dx2b, w_down, "nt", name="mm_dact", after=tok)
    dup_g, dup_v, dcf_g, dcf_v = _ffn_act_bwd(up_pre, cv["conv_ffn"], dact, name="ffn_act_bwd")
    G["conv_ffn"] = jnp.concatenate([dcf_g, dcf_v], axis=1)
    dup = jnp.concatenate([dup_g, dup_v], axis=1)
    tok = emit({"w_up": _matmul(h2, dup, "tn", name="mm_dwup", out_dtypes=(BF16,), out_shards=True, tn=SHARD_TILE, tk=LONG_K)})
    dh2 = _matmul(dup, w_up, "nt", name="mm_dh2", after=tok, b_shards=True, tk=2 * SHARD_TILE)
    dx1, dx1b, G["g_ffn"] = _rms_bwd(x1, S["g_ffn"], dh2, dx2, name="rms2_bwd")
    tok = emit({"w_out": _matmul(ycat, dx1b, "tn", name="mm_dwout", out_dtypes=(BF16,), tk=LONG_K)})
    dycat = _matmul(dx1b, w_out, "nt", name="mm_dycat", after=tok)
    do, dz, G["dn_g"] = _gated_norm_bwd(o, proj, S["dn_g"], dycat, z_off, nA, name="gated_norm_bwd")
    du, dw, dqd, dkd, dqk, dgl = _delta_scan_bwd(do, w, qd, kd, vn, qk, gl, ss, H, name="delta_scan_bwd")
    dq, dk, dv, dgB, dbB = _delta_prep_bwd(qkv, gamB, bB, ti, u, w, qk, du, dw, dqd, dkd, dqk, dgl, H,
                                           name="delta_prep_bwd")
    dgb = jnp.pad(jnp.concatenate([dgB[:, :, 0].T, dbB[:, :, 0].T], axis=1), ((0, 0), (0, LANE - 2 * H)))
    dab, dal, ddt = _gates_bwd(proj, alog, dtb, dgb, ab_off, H, name="gates_bwd")
    G["a_log"], G["dt_bias"] = dal[:, :H], ddt[:, :H]
    dqkv, G["conv_qkv"] = _qkv_bwd(proj, cv["conv_qkv"], dq, dk, dv, qkv_off, H, name="qkv_bwd")
    dax, dab_, dac, G["conv_a"] = _group_a_bwd(proj, cv["conv_a"], dycat, CW, name="group_a_bwd")
    in_p = w_in.shape[0]
    dproj = jnp.concatenate([dax, dab_, dac, dqkv, dz, dab, jnp.zeros((T, in_p - (ab_off + 1) * LANE), BF16)], axis=1)
    tok = emit({"w_in": _matmul(dproj, h1, "tn", name="mm_dwin", out_dtypes=(BF16,), tk=LONG_K)})
    dh1 = _matmul(dproj, w_in, "nn", name="mm_dh1", after=tok, tk=LONG_K)
    grad_x, _, G["g_mix"] = _rms_bwd(x, S["g_mix"], dh1, dx1, name="rms1_bwd")
    return loss, grad_x, G


def _col_sharded(landed):
    _, R, C = landed.shape
    return jnp.transpose(landed, (1, 0, 2)).reshape(R, N_DEV * C)


def kernel(x, p, norm_mix_g, w_in, conv_a_w, conv_qkv_w, a_log, dt_bias, dn_norm_g, w_out, norm_ffn_g, w_up, conv_ffn_w, w_down, norm_ple_g, w_ple_gate, w_ple_proj, final_norm_g, loss_target, m_norm_mix_g, m_w_in, m_conv_a_w, m_conv_qkv_w, m_a_log, m_dt_bias, m_dn_norm_g, m_w_out, m_norm_ffn_g, m_w_up, m_conv_ffn_w, m_w_down, m_norm_ple_g, m_w_ple_gate, m_w_ple_proj, m_final_norm_g, v_norm_mix_g, v_w_in, v_conv_a_w, v_conv_qkv_w, v_a_log, v_dt_bias, v_dn_norm_g, v_w_out, v_norm_ffn_g, v_w_up, v_conv_ffn_w, v_w_down, v_norm_ple_g, v_w_ple_gate, v_w_ple_proj, v_final_norm_g):
    T, D = x.shape[1], x.shape[2]
    xd, _, cd = _mesh_pos()
    me = 4 * xd + 2 * lax.axis_index("y") + cd

    conv_sh = [conv_a_w[0], conv_qkv_w[0], conv_ffn_w[0]]
    conv_n = [c.size for c in conv_sh]
    pack_rows = -(-sum(conv_n) // LANE)
    conv_pack = jnp.pad(jnp.concatenate([c.reshape(-1) for c in conv_sh]), (0, pack_rows * LANE - sum(conv_n))).reshape(pack_rows, LANE)
    names = ["w_in", "conv", "w_out", "w_up", "w_down", "w_pg", "w_pp"]
    tr_ = lambda t: jnp.swapaxes(t, 1, 2)
    shards = [w_in[0].T.astype(BF16), conv_pack, w_out[0].astype(BF16), w_up[0].astype(BF16), w_down[0].astype(BF16),
              w_ple_gate[0].astype(BF16), w_ple_proj[0].astype(BF16)]
    empty_slots = lambda blocks: [lax.empty((N_DEV,) + tuple(b.shape), b.dtype) for b in blocks]
    handles, tok0 = _split_start(shards, empty_slots(shards), False, name="gather_start",
                                 relations=[(SIBLING,) + SAME_CORE] + [ALL_PEERS] * (len(shards) - 1))
    handle = dict(zip(names, handles))
    own = dict(zip(names, shards))
    in_cols = N_DEV * w_in.shape[2]
    in_p = (in_cols // LANE) * LANE + AB_PAD
    in_place = {"w_up", "w_pp"}

    def gathered(name, after):
        if name == "w_in":
            passed, fwd = _gather_forward(handle[name], after, name="gather_forward_w_in")
            landed = _gather_wait_two_level(passed, fwd, name="gather_wait_w_in")
        else:
            landed = _split_wait(handle[name], after, False, name="gather_wait_" + name)
        return lax.dynamic_update_index_in_dim(landed, own[name], me, 0)

    def wt(name, after):
        landed = gathered(name, after)
        if name in in_place:
            return landed
        full = landed.reshape(-1, D)
        return jnp.pad(full, ((0, in_p - in_cols), (0, 0))) if name == "w_in" else full

    def conv(after):
        flat = gathered("conv", after).reshape(N_DEV, pack_rows * LANE)
        out, o_ = {}, 0
        for nm, c, n_ in zip(("conv_a", "conv_qkv", "conv_ffn"), conv_sh, conv_n):
            out[nm] = _col_sharded(flat[:, o_:o_ + n_].reshape((N_DEV,) + c.shape))
            o_ += n_
        return out

    pending, mine = {}, {}

    def emit(grads):
        parts = [g if nm in in_place else (g[:in_cols] if nm == "w_in" else g).reshape(N_DEV, -1, D)
                 for nm, g in grads.items()]
        hs, tok = _split_start(parts, empty_slots([q[0] for q in parts]), True, name="scatter_start_" + "_".join(grads))
        pending.update(zip(grads, hs))
        mine.update({nm: lax.dynamic_index_in_dim(q, me, 0, keepdims=False) for nm, q in zip(grads, parts)})
        return tok

    S = {
        "g_mix": norm_mix_g + tok0[0, 0], "a_log": a_log, "dt_bias": dt_bias, "dn_g": dn_norm_g, "g_ffn": norm_ffn_g,
        "g_ple": norm_ple_g, "g_final": final_norm_g.reshape(1, D),
    }

    loss_v, grad_x, G = _local_step(x[0], p[0, 0], loss_target[0], S, wt, conv, emit)
    loss = lax.psum(loss_v[0, 0], ("x", "y", "c"))

    small_names = ["g_mix", "g_ffn", "g_ple", "g_final", "dn_g", "a_log", "dt_bias", "conv_a", "conv_qkv", "conv_ffn"]
    small_rows, pieces = [], []
    for nm in small_names:
        g_ = G[nm].reshape(-1)
        r_ = -(-g_.size // (8 * LANE)) * 8
        small_rows.append(r_)
        pieces.append(jnp.pad(g_, (0, r_ * LANE - g_.size)).reshape(r_, LANE))
    landed = {nm: _split_wait(h_, grad_x, True, name="scatter_wait_" + nm) for nm, h_ in pending.items() if nm != "w_in"}

    def adam(parts, w_, m_, v_, nm, own_=None):
        shp = w_.shape
        w2, m2, v2 = (t.reshape(parts.shape[1:]) for t in (w_, m_, v_))
        kw = {} if own_ is None else {"own": own_, "me": me.astype(jnp.int32).reshape(1)}
        return tuple(t.reshape(shp) for t in _adam(parts, w2, m2, v2, name="adam_" + nm, **kw))

    big = {
        "w_up": adam(landed["w_up"], w_up, m_w_up, v_w_up, "w_up", mine["w_up"]),
        "w_down": adam(landed["w_down"], w_down, m_w_down, v_w_down, "w_down", mine["w_down"]),
        "w_out": adam(landed["w_out"], w_out, m_w_out, v_w_out, "w_out", mine["w_out"]),
        "w_pg": adam(landed["w_pg"], w_ple_gate, m_w_ple_gate, v_w_ple_gate, "w_ple_gate", mine["w_pg"]),
        "w_pp": adam(landed["w_pp"], w_ple_proj, m_w_ple_proj, v_w_ple_proj, "w_ple_proj", mine["w_pp"]),
    }
    (small_l,) = _exchange([jnp.concatenate(pieces, axis=0)], False, name="gather_small_grads", after=big["w_pp"][1])

    def small_parts(nm):
        i = small_names.index(nm)
        r0 = sum(small_rows[:i])
        shp = G[nm].shape
        return small_l[:, r0:r0 + small_rows[i], :].reshape(N_DEV, -1)[:, :G[nm].size].reshape((N_DEV,) + shp)

    def conv_parts(nm, shard):
        full = small_parts(nm)
        C = shard.shape[-1]
        return lax.dynamic_slice_in_dim(full, me * C, C, axis=2)

    res = [
        adam(small_parts("g_mix"), norm_mix_g, m_norm_mix_g, v_norm_mix_g, "norm_mix_g"),
        None,
        adam(conv_parts("conv_a", conv_a_w), conv_a_w, m_conv_a_w, v_conv_a_w, "conv_a_w"),
        adam(conv_parts("conv_qkv", conv_qkv_w), conv_qkv_w, m_conv_qkv_w, v_conv_qkv_w, "conv_qkv_w"),
        adam(small_parts("a_log"), a_log, m_a_log, v_a_log, "a_log"),
        adam(small_parts("dt_bias"), dt_bias, m_dt_bias, v_dt_bias, "dt_bias"),
        adam(small_parts("dn_g"), dn_norm_g, m_dn_norm_g, v_dn_norm_g, "dn_norm_g"),
        big["w_out"],
        adam(small_parts("g_ffn"), norm_ffn_g, m_norm_ffn_g, v_norm_ffn_g, "norm_ffn_g"),
        big["w_up"],
        adam(conv_parts("conv_ffn", conv_ffn_w), conv_ffn_w, m_conv_ffn_w, v_conv_ffn_w, "conv_ffn_w"),
        big["w_down"],
        adam(small_parts("g_ple"), norm_ple_g, m_norm_ple_g, v_norm_ple_g, "norm_ple_g"),
        big["w_pg"],
        big["w_pp"],
        adam(small_parts("g_final"), final_norm_g.reshape(1, D), m_final_norm_g.reshape(1, D),
             v_final_norm_g.reshape(1, D), "final_norm_g"),
    ]
    res[-1] = tuple(t.reshape(D) for t in res[-1])
    landed_in = _split_wait(pending["w_in"], res[10][1], True, name="scatter_wait_w_in")
    res[1] = tuple(tr_(t) for t in adam(landed_in, tr_(w_in), tr_(m_w_in), tr_(v_w_in), "w_in", mine["w_in"]))
    grads, deltas, new_m, new_v = zip(*res)
    return (loss, grad_x[None], *grads, *deltas, *new_m, *new_v)
```

```python
import functools

import jax
import jax.numpy as jnp
from jax import lax
from jax.experimental import pallas as pl
from jax.experimental.pallas import tpu as pltpu

F32 = jnp.float32
BF16 = jnp.bfloat16

EPS = 1e-6
CHUNK = 64
HEAD = 128
LANE = 128
N_DEV = 8
AB_PAD = 512

ADAM_LR = 0.001
ADAM_B1 = 0.9
ADAM_B2 = 0.999
ADAM_EPS = 1e-08
ADAM_WD = 0.01
ADAM_STEP = 10

MESH = pl.DeviceIdType.MESH


def _tile(dim, target, align=LANE):
    if dim <= target:
        return dim
    t = (target // align) * align
    while t > align and dim % t:
        t -= align
    assert dim % t == 0, (dim, target)
    return t


def _params(sem, vmem_mb=48):
    return pltpu.CompilerParams(dimension_semantics=sem, vmem_limit_bytes=vmem_mb << 20)


_DN = {"nn": (((1,), (0,)), ((), ())), "nt": (((1,), (1,)), ((), ())), "tn": (((0,), (0,)), ((), ()))}
LONG_K = 4096
SHARD_TILE = 1408


def _matmul(a, b, mode, *, name, out_dtypes=(F32,), epilogue=None, extras=(), vec_extras=(), n_vec=0, after=None,
            b_shards=False, out_shards=False, tm=1024, tn=1024, tk=2048):
    shard_w = b.shape[2] if b_shards else None
    if b_shards:
        b_rows, b_cols = b.shape[1], N_DEV * shard_w
    else:
        b_rows, b_cols = b.shape
    if mode == "nn":
        (M, K), (K2, N) = a.shape, (b_rows, b_cols)
    elif mode == "nt":
        (M, K), (N, K2) = a.shape, (b_rows, b_cols)
    else:
        (K, M), (K2, N) = a.shape, (b_rows, b_cols)
    assert K == K2, (name, a.shape, b.shape)
    tm = _tile(M, tm)
    tn = _tile(shard_w if (b_shards and mode == "nn") else N // N_DEV if out_shards else N, tn)
    grp = 1
    if b_shards and mode == "nt":
        grp = max(g for g in (1, 2, 4, 8) if g <= max(1, tk // shard_w))
    tk = grp * shard_w if grp > 1 else _tile(shard_w if (b_shards and mode == "nt") else K, tk)
    assert K % tk == 0, (name, K, tk)
    nk = K // tk
    n_ex, n_out = len(extras) + len(vec_extras), len(out_dtypes)
    assert n_vec == 0 or tn == N, (name, tn, N)
    dn = _DN[mode]

    n_tok = 0 if after is None else 1

    def body(a_ref, b_ref, *rest):
        rest = rest[n_tok:]
        ex_refs, out_refs, vec_refs = rest[:n_ex], rest[n_ex:n_ex + n_out], rest[n_ex + n_out:n_ex + n_out + n_vec]
        if grp > 1:
            part = sum(lax.dot_general(a_ref[:, s * shard_w:(s + 1) * shard_w].astype(BF16), b_ref[s].astype(BF16), dn,
                                       preferred_element_type=F32) for s in range(grp))
        else:
            part = lax.dot_general(a_ref[...].astype(BF16), b_ref[...].astype(BF16), dn, preferred_element_type=F32)
        first_rows = pl.program_id(0) == 0

        def finish(res):
            outs = (res,) if epilogue is None else epilogue(res, *[e[...] for e in ex_refs])
            for o_ref, val in zip(out_refs, outs[:n_out]):
                o_ref[...] = val.astype(o_ref.dtype)
            for v_ref, val in zip(vec_refs, outs[n_out:]):
                @pl.when(first_rows)
                def _(v_ref=v_ref, val=val):
                    v_ref[...] = val

                @pl.when(jnp.logical_not(first_rows))
                def _(v_ref=v_ref, val=val):
                    v_ref[...] += val

        if nk == 1:
            finish(part)
            return
        acc, k = rest[-1], pl.program_id(2)

        @pl.when(k == 0)
        def _():
            acc[...] = part

        @pl.when(k > 0)
        def _():
            acc[...] += part

        @pl.when(k == nk - 1)
        def _():
            finish(acc[...])

    a_spec = pl.BlockSpec((tk, tm), lambda i, j, k: (k, i)) if mode == "tn" else pl.BlockSpec((tm, tk), lambda i, j, k: (i, k))
    if b_shards and mode == "nn":
        per = shard_w // tn
        b_spec = pl.BlockSpec((None, tk, tn), lambda i, j, k: (lax.div(j, per), k, lax.rem(j, per)))
    elif b_shards and grp > 1:
        b_spec = pl.BlockSpec((grp, tn, shard_w), lambda i, j, k: (k, j, 0))
    elif b_shards:
        per = shard_w // tk
        b_spec = pl.BlockSpec((None, tn, tk), lambda i, j, k: (lax.div(k, per), j, lax.rem(k, per)))
    else:
        b_spec = pl.BlockSpec((tn, tk), lambda i, j, k: (j, k)) if mode == "nt" else pl.BlockSpec((tk, tn), lambda i, j, k: (k, j))
    mn_spec = pl.BlockSpec((tm, tn), lambda i, j, k: (i, j))
    vec_spec = pl.BlockSpec((1, tn), lambda i, j, k: (0, j))
    if out_shards:
        assert not extras
        per_o = (N // N_DEV) // tn
        out_spec = pl.BlockSpec((None, tm, tn), lambda i, j, k: (lax.div(j, per_o), i, lax.rem(j, per_o)))
        out_dims = (N_DEV, M, N // N_DEV)
    else:
        out_spec, out_dims = mn_spec, (M, N)
    outs = pl.pallas_call(
        body, name=name, grid=(M // tm, N // tn, nk),
        in_specs=[a_spec, b_spec] + [pl.BlockSpec((8, LANE), lambda i, j, k: (0, 0))] * n_tok
        + [mn_spec] * len(extras) + [vec_spec] * len(vec_extras),
        out_specs=[out_spec] * n_out + [vec_spec] * n_vec,
        out_shape=[jax.ShapeDtypeStruct(out_dims, dt) for dt in out_dtypes] + [jax.ShapeDtypeStruct((1, N), F32)] * n_vec,
        scratch_shapes=[pltpu.VMEM((tm, tn), F32)] if nk > 1 else [],
        compiler_params=_params(("arbitrary" if n_vec else "parallel", "parallel", "arbitrary"), 56),
    )(a, b, *([] if after is None else [after]), *extras, *vec_extras)
    return outs[0] if n_out + n_vec == 1 else outs


def _rms_fwd(x, g, *, name):
    T, D = x.shape
    tr = _tile(T, 256, 8)

    def body(x_ref, g_ref, h_ref):
        xv = x_ref[...]
        r = lax.rsqrt(jnp.mean(xv * xv, axis=-1, keepdims=True) + EPS)
        h_ref[...] = (xv * r * g_ref[...]).astype(h_ref.dtype)

    return pl.pallas_call(
        body, name=name, grid=(T // tr,),
        in_specs=[pl.BlockSpec((tr, D), lambda i: (i, 0)), pl.BlockSpec((1, D), lambda i: (0, 0))],
        out_specs=pl.BlockSpec((tr, D), lambda i: (i, 0)),
        out_shape=jax.ShapeDtypeStruct((T, D), BF16),
        compiler_params=_params(("parallel",)),
    )(x, g)


def _rms_bwd(x, g, dh, dres, *, name):
    T, D = x.shape
    tr = _tile(T, 256, 8)
    epi = _epi_rms_bwd(2)

    def body(x_ref, g_ref, dh_ref, dres_ref, dx_ref, dxb_ref, dg_ref):
        dx, _, dgp = epi(dh_ref[...], x_ref[...], dres_ref[...], g_ref[...])

        @pl.when(pl.program_id(0) == 0)
        def _():
            dg_ref[...] = jnp.zeros_like(dg_ref)

        dg_ref[...] += dgp
        dx_ref[...] = dx
        dxb_ref[...] = dx.astype(dxb_ref.dtype)

    row = pl.BlockSpec((tr, D), lambda i: (i, 0))
    vec = pl.BlockSpec((1, D), lambda i: (0, 0))
    return pl.pallas_call(
        body, name=name, grid=(T // tr,),
        in_specs=[row, vec, row, row], out_specs=[row, row, vec],
        out_shape=[jax.ShapeDtypeStruct((T, D), F32), jax.ShapeDtypeStruct((T, D), BF16), jax.ShapeDtypeStruct((1, D), F32)],
        compiler_params=_params(("arbitrary",)),
    )(x, g, dh, dres)


ROW_TILE = 256


def _epi_residual_rms(acc, res, g):
    xn = acc + res
    r = lax.rsqrt(jnp.mean(xn * xn, axis=-1, keepdims=True) + EPS)
    return xn, xn * r * g


def _epi_rms_bwd(n_copies):
    def epi(dh, x, dres, g):
        r = lax.rsqrt(jnp.mean(x * x, axis=-1, keepdims=True) + EPS)
        xh = x * r
        dxh = dh * g
        dx = dres + r * (dxh - xh * jnp.mean(dxh * xh, axis=-1, keepdims=True))
        return (dx,) * n_copies + (jnp.sum(dh * xh, axis=0, keepdims=True),)
    return epi


def _final_loss(x, g, tgt, *, name):
    T, D = x.shape
    tr = _tile(T, 256, 8)

    def body(x_ref, g_ref, t_ref, dx_ref, dg_ref, loss_ref):
        xv = x_ref[...]
        r = lax.rsqrt(jnp.mean(xv * xv, axis=-1, keepdims=True) + EPS)
        xh = xv * r
        gv = g_ref[...]
        err = xh * gv - t_ref[...]

        @pl.when(pl.program_id(0) == 0)
        def _():
            dg_ref[...] = jnp.zeros_like(dg_ref)
            loss_ref[...] = jnp.zeros_like(loss_ref)

        part = 0.5 * jnp.sum(jnp.mean(err * err, axis=-1, keepdims=True), axis=0, keepdims=True)
        loss_ref[...] += jnp.broadcast_to(part, loss_ref.shape)
        dy = err * (1.0 / D)
        dg_ref[...] += jnp.sum(dy * xh, axis=0, keepdims=True)
        dxh = dy * gv
        dx_ref[...] = r * (dxh - xh * jnp.mean(dxh * xh, axis=-1, keepdims=True))

    row = pl.BlockSpec((tr, D), lambda i: (i, 0))
    vec = pl.BlockSpec((1, D), lambda i: (0, 0))
    return pl.pallas_call(
        body, name=name, grid=(T // tr,),
        in_specs=[row, vec, row], out_specs=[row, vec, pl.BlockSpec((1, LANE), lambda i: (0, 0))],
        out_shape=[jax.ShapeDtypeStruct((T, D), F32), jax.ShapeDtypeStruct((1, D), F32),
                   jax.ShapeDtypeStruct((1, LANE), F32)],
        compiler_params=_params(("arbitrary",)),
    )(x, g, tgt)


def _ple_bwd(dx3, pp, sg, *, name):
    T, D = dx3.shape
    tr = _tile(T, 256, 8)

    def body(dx_ref, pp_ref, sg_ref, dpg_ref, dpp_ref):
        dx, s = dx_ref[...], sg_ref[...]
        dpg_ref[...] = (dx * pp_ref[...] * s * (1.0 - s)).astype(dpg_ref.dtype)
        dpp_ref[...] = (dx * s).astype(dpp_ref.dtype)

    row = pl.BlockSpec((tr, D), lambda i: (i, 0))
    return pl.pallas_call(
        body, name=name, grid=(T // tr,), in_specs=[row, row, row], out_specs=[row, row],
        out_shape=[jax.ShapeDtypeStruct((T, D), BF16)] * 2, compiler_params=_params(("parallel",)),
    )(dx3, pp, sg)


ROWS_QKV_FWD, ROWS_QKV_BWD, ROWS_FFN_FWD, ROWS_FFN_BWD, ROWS_GROUP_A = 512, 256, 256, 128, 256


def _ext(ref, r0, T, before, after, RC):
    parts = []
    if before:
        p0 = pl.multiple_of(jnp.maximum(r0 - 8, 0), 8)
        parts.append(jnp.where(r0 > 0, ref[pl.ds(p0, 8), :], 0.0))
    parts.append(ref[pl.ds(r0, RC), :])
    if after:
        n0 = pl.multiple_of(jnp.minimum(r0 + RC, T - 8), 8)
        parts.append(jnp.where(r0 + RC < T, ref[pl.ds(n0, 8), :], 0.0))
    return parts[0] if len(parts) == 1 else jnp.concatenate(parts, axis=0)


def _down(xx, s):
    return (xx if s == 0 else pltpu.roll(xx, s, 0))[8:, :]


def _up(xx, s, rows):
    return (xx if s == 0 else pltpu.roll(xx, xx.shape[0] - s, 0))[:rows, :]


def _conv_down(xx, w_ref, K):
    y = None
    for j in range(K):
        t = _down(xx, K - 1 - j) * w_ref[j:j + 1, :]
        y = t if y is None else y + t
    return y


def _fold8(x):
    return jnp.sum(x.reshape(x.shape[0] // 8, 8, x.shape[1]), axis=0)


def _silu(x):
    return x * jax.nn.sigmoid(x)


def _dsilu(x):
    s = jax.nn.sigmoid(x)
    return s * (1.0 + x * (1.0 - s))


def _col_specs(T, offs):
    return [pl.BlockSpec((T, LANE), functools.partial(lambda o, j: (0, o + j), o)) for o in offs]


def _group_a_fwd(proj, conv_w, CW, *, name):
    T = proj.shape[0]
    RC = _tile(T, ROWS_GROUP_A, 8)
    nb = CW // LANE
    K = conv_w.shape[0]

    def body(ax_ref, ab_ref, ac_ref, w_ref, y_ref):
        def step(i, carry):
            r0 = pl.multiple_of(i * RC, RC)
            m = _ext(ac_ref, r0, T, True, False, RC) * _ext(ax_ref, r0, T, True, False, RC)
            y_ref[pl.ds(r0, RC), :] = (ab_ref[pl.ds(r0, RC), :] * _conv_down(m, w_ref, K)).astype(y_ref.dtype)
            return carry
        lax.fori_loop(0, T // RC, step, 0)

    return pl.pallas_call(
        body, name=name, grid=(nb,),
        in_specs=_col_specs(T, (0, nb, 2 * nb)) + [pl.BlockSpec((K, LANE), lambda j: (0, j))],
        out_specs=pl.BlockSpec((T, LANE), lambda j: (0, j)),
        out_shape=jax.ShapeDtypeStruct((T, CW), BF16), compiler_params=_params(("parallel",)),
    )(proj, proj, proj, conv_w)


def _group_a_bwd(proj, conv_w, dycat, CW, *, name):
    T = proj.shape[0]
    RC = _tile(T, ROWS_GROUP_A, 8)
    nb = CW // LANE
    K = conv_w.shape[0]

    def body(ax_ref, ab_ref, ac_ref, w_ref, dy_ref, dax_ref, dab_ref, dac_ref, dw_ref):
        def step(i, accs):
            r0 = pl.multiple_of(i * RC, RC)
            ax3 = _ext(ax_ref, r0, T, True, True, RC)
            ac3 = _ext(ac_ref, r0, T, True, True, RC)
            m3 = ax3 * ac3
            c = _conv_down(m3[:RC + 8], w_ref, K)
            dy = dy_ref[pl.ds(r0, RC), :]
            dab_ref[pl.ds(r0, RC), :] = (dy * c).astype(dab_ref.dtype)
            dc2 = _ext(dy_ref, r0, T, False, True, RC) * _ext(ab_ref, r0, T, False, True, RC)
            dm = None
            new = []
            for j in range(K):
                s = K - 1 - j
                t = _up(dc2, s, RC) * w_ref[j:j + 1, :]
                dm = t if dm is None else dm + t
                new.append(accs[j] + _fold8(dc2[:RC] * _down(m3[:RC + 8], s)))
            dax_ref[pl.ds(r0, RC), :] = (dm * ac3[8:RC + 8]).astype(dax_ref.dtype)
            dac_ref[pl.ds(r0, RC), :] = (dm * ax3[8:RC + 8]).astype(dac_ref.dtype)
            return tuple(new)

        accs = lax.fori_loop(0, T // RC, step, tuple(jnp.zeros((8, LANE), F32) for _ in range(K)))
        for j in range(K):
            dw_ref[j:j + 1, :] = jnp.sum(accs[j], axis=0, keepdims=True)

    col = pl.BlockSpec((T, LANE), lambda j: (0, j))
    wsp = pl.BlockSpec((K, LANE), lambda j: (0, j))
    return pl.pallas_call(
        body, name=name, grid=(nb,),
        in_specs=_col_specs(T, (0, nb, 2 * nb)) + [wsp, col],
        out_specs=[col, col, col, wsp],
        out_shape=[jax.ShapeDtypeStruct((T, CW), BF16)] * 3 + [jax.ShapeDtypeStruct((K, CW), F32)],
        compiler_params=_params(("parallel",)),
    )(proj, proj, proj, conv_w, dycat)


def _qkv_fwd(proj, conv_w, off, H, *, name):
    T = proj.shape[0]
    RC = _tile(T, ROWS_QKV_FWD, 8)
    nb = 3 * H
    K = conv_w.shape[0]

    def body(x_ref, w_ref, y_ref):
        j = pl.program_id(0)
        is_qk = j < 2 * H
        scale = jnp.where(j < H, HEAD ** -0.5, 1.0).astype(F32)

        def step(i, carry):
            r0 = pl.multiple_of(i * RC, RC)
            s = _silu(_conv_down(_ext(x_ref, r0, T, True, False, RC), w_ref, K))
            r = lax.rsqrt(jnp.sum(s * s, axis=-1, keepdims=True) + EPS) * scale
            y_ref[pl.ds(r0, RC), :] = s * jnp.where(is_qk, r, 1.0)
            return carry
        lax.fori_loop(0, T // RC, step, 0)

    return pl.pallas_call(
        body, name=name, grid=(nb,),
        in_specs=_col_specs(T, (off,)) + [pl.BlockSpec((K, LANE), lambda j: (0, j))],
        out_specs=pl.BlockSpec((T, LANE), lambda j: (0, j)),
        out_shape=jax.ShapeDtypeStruct((T, nb * LANE), F32), compiler_params=_params(("parallel",)),
    )(proj, conv_w)


def _qkv_bwd(proj, conv_w, dq, dk, dv, off, H, *, name):
    T = proj.shape[0]
    RC = _tile(T, ROWS_QKV_BWD, 8)
    nb = 3 * H
    K = conv_w.shape[0]

    def body(x_ref, w_ref, dq_ref, dk_ref, dv_ref, dx_ref, dw_ref):
        j = pl.program_id(0)
        is_qk = j < 2 * H
        scale = jnp.where(j < H, HEAD ** -0.5, 1.0).astype(F32)

        def step(i, accs):
            r0 = pl.multiple_of(i * RC, RC)
            x3 = _ext(x_ref, r0, T, True, True, RC)
            c2 = _conv_down(x3, w_ref, K)
            s2 = _silu(c2)
            dn2 = jnp.where(j < H, _ext(dq_ref, r0, T, False, True, RC),
                            jnp.where(is_qk, _ext(dk_ref, r0, T, False, True, RC), _ext(dv_ref, r0, T, False, True, RC)))
            r = lax.rsqrt(jnp.sum(s2 * s2, axis=-1, keepdims=True) + EPS)
            nh = s2 * r
            dnp = dn2 * scale
            ds_qk = r * (dnp - nh * jnp.sum(dnp * nh, axis=-1, keepdims=True))
            ds2 = jnp.where(is_qk, ds_qk, dn2)
            dc2 = ds2 * _dsilu(c2)
            dx = None
            new = []
            for jj in range(K):
                s = K - 1 - jj
                t = _up(dc2, s, RC) * w_ref[jj:jj + 1, :]
                dx = t if dx is None else dx + t
                new.append(accs[jj] + _fold8(dc2[:RC] * _down(x3[:RC + 8], s)))
            dx_ref[pl.ds(r0, RC), :] = dx.astype(dx_ref.dtype)
            return tuple(new)

        accs = lax.fori_loop(0, T // RC, step, tuple(jnp.zeros((8, LANE), F32) for _ in range(K)))
        for jj in range(K):
            dw_ref[jj:jj + 1, :] = jnp.sum(accs[jj], axis=0, keepdims=True)

    col = pl.BlockSpec((T, LANE), lambda j: (0, j))
    wsp = pl.BlockSpec((K, LANE), lambda j: (0, j))
    return pl.pallas_call(
        body, name=name, grid=(nb,),
        in_specs=_col_specs(T, (off,)) + [wsp] + [
            pl.BlockSpec((T, LANE), functools.partial(lambda o, j: (0, jnp.clip(j - o, 0, H - 1)), o)) for o in (0, H, 2 * H)],
        out_specs=[col, wsp],
        out_shape=[jax.ShapeDtypeStruct((T, nb * LANE), BF16), jax.ShapeDtypeStruct((K, nb * LANE), F32)],
        compiler_params=_params(("parallel",)),
    )(proj, conv_w, dq, dk, dv)


def _softplus(x):
    return jnp.maximum(x, 0.0) + jnp.log(1.0 + jnp.exp(-jnp.abs(x)))


def _gates_fwd(proj, alog, dtb, off, H, *, name):
    T = proj.shape[0]
    tr = _tile(T, 512, CHUNK)

    def body(ab_ref, al_ref, dt_ref, gb_ref, gam_ref):
        ab = ab_ref[...]
        lane = lax.broadcasted_iota(jnp.int32, ab.shape, 1)
        g = -jnp.exp(al_ref[...]) * _softplus(ab + dt_ref[...])
        gb = jnp.where(lane < H, g, jnp.where(lane < 2 * H, jax.nn.sigmoid(ab), 0.0))
        gb_ref[...] = gb
        tril = _tri().astype(F32)
        for c in range(tr // CHUNK):
            rows = slice(c * CHUNK, (c + 1) * CHUNK)
            gam_ref[rows, :] = _mm(tril, gb[rows, :], precision=lax.Precision.HIGHEST)

    vec = pl.BlockSpec((1, LANE), lambda i: (0, 0))
    row = pl.BlockSpec((tr, LANE), lambda i: (i, 0))
    return pl.pallas_call(
        body, name=name, grid=(T // tr,),
        in_specs=[pl.BlockSpec((tr, LANE), lambda i: (i, off)), vec, vec],
        out_specs=[row, row],
        out_shape=[jax.ShapeDtypeStruct((T, LANE), F32)] * 2, compiler_params=_params(("parallel",)),
    )(proj, alog, dtb)


def _gates_bwd(proj, alog, dtb, dgb, off, H, *, name):
    T = proj.shape[0]
    tr = _tile(T, 512, CHUNK)

    def body(ab_ref, al_ref, dt_ref, d_ref, dab_ref, dal_ref, ddt_ref):
        ab, d = ab_ref[...], d_ref[...]
        lane = lax.broadcasted_iota(jnp.int32, ab.shape, 1)
        is_g = lane < H
        triu = _tri(upper=True).astype(F32)
        dg = jnp.concatenate([_mm(triu, d[c * CHUNK:(c + 1) * CHUNK, :], precision=lax.Precision.HIGHEST)
                              for c in range(tr // CHUNK)], axis=0)
        z = ab + dt_ref[...]
        A = -jnp.exp(al_ref[...])
        da = dg * A * jax.nn.sigmoid(z)
        beta = jax.nn.sigmoid(ab)
        db = d * beta * (1.0 - beta)
        dab_ref[...] = jnp.where(is_g, da, jnp.where(lane < 2 * H, db, 0.0)).astype(dab_ref.dtype)

        @pl.when(pl.program_id(0) == 0)
        def _():
            dal_ref[...] = jnp.zeros_like(dal_ref)
            ddt_ref[...] = jnp.zeros_like(ddt_ref)

        dal_ref[...] += jnp.sum(jnp.where(is_g, dg * A * _softplus(z), 0.0), axis=0, keepdims=True)
        ddt_ref[...] += jnp.sum(jnp.where(is_g, da, 0.0), axis=0, keepdims=True)

    vec = pl.BlockSpec((1, LANE), lambda i: (0, 0))
    row = pl.BlockSpec((tr, LANE), lambda i: (i, 0))
    return pl.pallas_call(
        body, name=name, grid=(T // tr,),
        in_specs=[pl.BlockSpec((tr, LANE), lambda i: (i, off)), vec, vec, row],
        out_specs=[row, vec, vec],
        out_shape=[jax.ShapeDtypeStruct((T, LANE), BF16), jax.ShapeDtypeStruct((1, LANE), F32),
                   jax.ShapeDtypeStruct((1, LANE), F32)],
        compiler_params=_params(("arbitrary",)),
    )(proj, alog, dtb, dgb)


def _gated_norm_fwd(o, proj, gn, zoff, *, name):
    T, W = o.shape
    tr = _tile(T, 512, 8)

    def body(o_ref, z_ref, g_ref, y_ref):
        ov = o_ref[...]
        r = lax.rsqrt(jnp.mean(ov * ov, axis=-1, keepdims=True) + EPS)
        y_ref[...] = (ov * r * g_ref[...] * _silu(z_ref[...])).astype(y_ref.dtype)

    blk = pl.BlockSpec((tr, LANE), lambda i, j: (i, j))
    return pl.pallas_call(
        body, name=name, grid=(T // tr, W // LANE),
        in_specs=[blk, pl.BlockSpec((tr, LANE), lambda i, j: (i, zoff + j)), pl.BlockSpec((1, LANE), lambda i, j: (0, 0))],
        out_specs=blk, out_shape=jax.ShapeDtypeStruct((T, W), BF16), compiler_params=_params(("parallel", "parallel")),
    )(o, proj, gn)


def _gated_norm_bwd(o, proj, gn, dycat, zoff, yoff, *, name):
    T, W = o.shape
    tr = _tile(T, 512, 8)

    def body(o_ref, z_ref, g_ref, dy_ref, do_ref, dz_ref, dg_ref):
        ov, zv, gv, dy = o_ref[...], z_ref[...], g_ref[...], dy_ref[...]
        r = lax.rsqrt(jnp.mean(ov * ov, axis=-1, keepdims=True) + EPS)
        nh = ov * r
        s = _silu(zv)

        @pl.when((pl.program_id(0) == 0) & (pl.program_id(1) == 0))
        def _():
            dg_ref[...] = jnp.zeros_like(dg_ref)

        dg_ref[...] += jnp.sum(dy * nh * s, axis=0, keepdims=True)
        dz_ref[...] = (dy * nh * gv * _dsilu(zv)).astype(dz_ref.dtype)
        dn = dy * gv * s
        do_ref[...] = r * (dn - nh * jnp.mean(dn * nh, axis=-1, keepdims=True))

    blk = pl.BlockSpec((tr, LANE), lambda i, j: (i, j))
    vec = pl.BlockSpec((1, LANE), lambda i, j: (0, 0))
    return pl.pallas_call(
        body, name=name, grid=(T // tr, W // LANE),
        in_specs=[blk, pl.BlockSpec((tr, LANE), lambda i, j: (i, zoff + j)), vec,
                  pl.BlockSpec((tr, LANE), lambda i, j: (i, yoff + j))],
        out_specs=[blk, blk, vec],
        out_shape=[jax.ShapeDtypeStruct((T, W), F32), jax.ShapeDtypeStruct((T, W), BF16),
                   jax.ShapeDtypeStruct((1, LANE), F32)],
        compiler_params=_params(("arbitrary", "arbitrary")),
    )(o, proj, gn, dycat)


def _ffn_act_fwd(up_pre, conv_w, *, name):
    T, F2 = up_pre.shape
    RC = _tile(T, ROWS_FFN_FWD, 8)
    nb = F2 // 2 // LANE
    K = conv_w.shape[0]

    def body(g_ref, v_ref, wg_ref, wv_ref, y_ref):
        def step(i, carry):
            r0 = pl.multiple_of(i * RC, RC)
            gate = _conv_down(_ext(g_ref, r0, T, True, False, RC), wg_ref, K)
            val = _conv_down(_ext(v_ref, r0, T, True, False, RC), wv_ref, K)
            y_ref[pl.ds(r0, RC), :] = (_silu(gate) * val).astype(y_ref.dtype)
            return carry
        lax.fori_loop(0, T // RC, step, 0)

    return pl.pallas_call(
        body, name=name, grid=(nb,),
        in_specs=_col_specs(T, (0, nb)) + [pl.BlockSpec((K, LANE), lambda j: (0, j)),
                                           pl.BlockSpec((K, LANE), lambda j: (0, nb + j))],
        out_specs=pl.BlockSpec((T, LANE), lambda j: (0, j)),
        out_shape=jax.ShapeDtypeStruct((T, F2 // 2), BF16), compiler_params=_params(("parallel",)),
    )(up_pre, up_pre, conv_w, conv_w)


def _ffn_act_bwd(up_pre, conv_w, dact, *, name):
    T, F2 = up_pre.shape
    RC = _tile(T, ROWS_FFN_BWD, 8)
    nb = F2 // 2 // LANE
    K = conv_w.shape[0]

    def body(g_ref, v_ref, wg_ref, wv_ref, da_ref, dg_ref, dv_ref, dwg_ref, dwv_ref):
        def step(i, accs):
            r0 = pl.multiple_of(i * RC, RC)
            g3 = _ext(g_ref, r0, T, True, True, RC)
            v3 = _ext(v_ref, r0, T, True, True, RC)
            gate2 = _conv_down(g3, wg_ref, K)
            val2 = _conv_down(v3, wv_ref, K)
            da2 = _ext(da_ref, r0, T, False, True, RC)
            dgate2 = da2 * val2 * _dsilu(gate2)
            dval2 = da2 * _silu(gate2)
            dgp, dvp, new = None, None, []
            for j in range(K):
                s = K - 1 - j
                tg = _up(dgate2, s, RC) * wg_ref[j:j + 1, :]
                tv = _up(dval2, s, RC) * wv_ref[j:j + 1, :]
                dgp = tg if dgp is None else dgp + tg
                dvp = tv if dvp is None else dvp + tv
                new.append(accs[2 * j] + _fold8(dgate2[:RC] * _down(g3[:RC + 8], s)))
                new.append(accs[2 * j + 1] + _fold8(dval2[:RC] * _down(v3[:RC + 8], s)))
            dg_ref[pl.ds(r0, RC), :] = dgp.astype(dg_ref.dtype)
            dv_ref[pl.ds(r0, RC), :] = dvp.astype(dv_ref.dtype)
            return tuple(new)

        accs = lax.fori_loop(0, T // RC, step, tuple(jnp.zeros((8, LANE), F32) for _ in range(2 * K)))
        for j in range(K):
            dwg_ref[j:j + 1, :] = jnp.sum(accs[2 * j], axis=0, keepdims=True)
            dwv_ref[j:j + 1, :] = jnp.sum(accs[2 * j + 1], axis=0, keepdims=True)

    col = pl.BlockSpec((T, LANE), lambda j: (0, j))
    wsp = pl.BlockSpec((K, LANE), lambda j: (0, j))
    return pl.pallas_call(
        body, name=name, grid=(nb,),
        in_specs=_col_specs(T, (0, nb)) + [wsp, pl.BlockSpec((K, LANE), lambda j: (0, nb + j)), col],
        out_specs=[col, col, wsp, wsp],
        out_shape=[jax.ShapeDtypeStruct((T, F2 // 2), BF16)] * 2 + [jax.ShapeDtypeStruct((K, F2 // 2), F32)] * 2,
        compiler_params=_params(("parallel",)),
    )(up_pre, up_pre, conv_w, conv_w, dact)


CPB = 8
CPB_SCAN = 4
GRP = 8
HP = lax.Precision.HIGH


def _tri(strict=False, upper=False):
    r = lax.broadcasted_iota(jnp.int32, (CHUNK, CHUNK), 0)
    c = lax.broadcasted_iota(jnp.int32, (CHUNK, CHUNK), 1)
    if upper:
        return c >= r
    return (r > c) if strict else (r >= c)


def _mm(a, b, dn="nn", precision=None):
    precision = HP if precision is None else precision
    return lax.dot_general(a, b, _DN[dn], precision=precision, preferred_element_type=F32)


def _mm16(a, b, dn="nn"):
    return lax.dot_general(a.astype(BF16), b.astype(BF16), _DN[dn], preferred_element_type=F32)


def _each(f, *cols):
    return [f(*xs) for xs in zip(*cols)]


def _decay(gam):
    return jnp.exp(jnp.where(_tri(), gam[:, :CHUNK] - gam.T[:CHUNK, :], -1e30))


def _delta_specs(T, H, cpb):
    rows = cpb * CHUNK
    col = lambda o: pl.BlockSpec((rows, LANE), functools.partial(lambda o, h, n: (n, o + h), o))
    bc = pl.BlockSpec((1, rows, LANE), lambda h, n: (h, n, 0))
    sq = pl.BlockSpec((1, cpb, CHUNK, CHUNK), lambda h, n: (h, n, 0, 0))
    vec = pl.BlockSpec((1, cpb, 1, LANE), lambda h, n: (h, n, 0, 0))
    return col, bc, sq, vec


def _delta_prep_fwd(qkv, gamB, bB, H, *, name):
    T = qkv.shape[0]
    N = T // CHUNK
    cpb = _tile(N, CPB, 8)
    grp = min(GRP, cpb)
    col, bc, sq, vec = _delta_specs(T, H, cpb)

    def body(q_ref, k_ref, v_ref, g_ref, b_ref, u_ref, w_ref, qd_ref, kd_ref, qk_ref, ti_ref, gl_ref):
        eye = (lax.broadcasted_iota(jnp.int32, (CHUNK, CHUNK), 0) == lax.broadcasted_iota(jnp.int32, (CHUNK, CHUNK), 1)).astype(F32)
        strict = _tri(strict=True)
        for c0 in range(0, cpb, grp):
            cs = list(range(c0, c0 + grp))
            rows = [slice(c * CHUNK, (c + 1) * CHUNK) for c in cs]
            q, k, v = ([r_[r, :] for r in rows] for r_ in (q_ref, k_ref, v_ref))
            bb = [b_ref[0, r, :] for r in rows]
            gam = [g_ref[0, r, :] for r in rows]
            D = _each(_decay, gam)
            e = _each(jnp.exp, gam)
            kk = _each(lambda k_: _mm16(k_, k_, "nt"), k)
            X = _each(lambda kk_, D_, b_: -(jnp.where(strict, kk_ * D_, 0.0) * b_[:, :CHUNK]), kk, D, bb)
            R = _each(lambda x: eye + x, X)
            for _ in range(5):
                X = _each(lambda x: _mm(x, x), X)
                R = _each(lambda r, x: r + _mm(r, x), R, X)
            u = _each(lambda r, b_, v_: _mm(r, b_ * v_), R, bb, v)
            w = _each(lambda r, b_, e_, k_: _mm(r, b_ * e_ * k_), R, bb, e, k)
            qk = _each(lambda q_, k_, D_: _mm16(q_, k_, "nt") * D_, q, k, D)
            for i, c in enumerate(cs):
                glast = gam[i][CHUNK - 1:CHUNK, :]
                u_ref[rows[i], :] = u[i]
                w_ref[rows[i], :] = w[i]
                qd_ref[rows[i], :] = e[i] * q[i]
                kd_ref[rows[i], :] = jnp.exp(glast - gam[i]) * k[i]
                qk_ref[0, c] = qk[i]
                ti_ref[0, c] = R[i]
                gl_ref[0, c] = jnp.exp(glast)

    full = jax.ShapeDtypeStruct((T, H * LANE), F32)
    sqs = jax.ShapeDtypeStruct((H, N, CHUNK, CHUNK), F32)
    return pl.pallas_call(
        body, name=name, grid=(H, N // cpb),
        in_specs=[col(0), col(H), col(2 * H), bc, bc],
        out_specs=[col(0)] * 4 + [sq, sq, vec],
        out_shape=[full] * 4 + [sqs, sqs, jax.ShapeDtypeStruct((H, N, 1, LANE), F32)],
        compiler_params=_params(("parallel", "parallel")),
    )(qkv, qkv, qkv, gamB, bB)


def _scan_specs(H, N, cpb, hb, rev):
    nbk = N // cpb
    blk = (lambda n: nbk - 1 - n) if rev else (lambda n: n)
    col = pl.BlockSpec((cpb * CHUNK, hb * LANE), lambda h, n: (blk(n), h))
    sq = pl.BlockSpec((hb, cpb, CHUNK, CHUNK), lambda h, n: (h, blk(n), 0, 0))
    vec = pl.BlockSpec((hb, cpb, 1, LANE), lambda h, n: (h, blk(n), 0, 0))
    st = pl.BlockSpec((hb, cpb, HEAD, HEAD), lambda h, n: (h, blk(n), 0, 0))
    return col, sq, vec, st


def _delta_scan_fwd(u, w, qd, kd, qk, gl, H, *, name):
    T = u.shape[0]
    N = T // CHUNK
    cpb = _tile(N, CPB_SCAN, 4)
    hb = min(GRP, H)
    col, sq, vec, st = _scan_specs(H, N, cpb, hb, False)
    lanes = [slice(j * LANE, (j + 1) * LANE) for j in range(hb)]
    heads = list(range(hb))

    def body(u_ref, w_ref, qd_ref, kd_ref, qk_ref, gl_ref, o_ref, vn_ref, ss_ref, s_scr):
        @pl.when(pl.program_id(1) == 0)
        def _():
            s_scr[...] = jnp.zeros_like(s_scr)

        def step(c, states):
            rows = pl.ds(pl.multiple_of(c * CHUNK, CHUNK), CHUNK)
            S = list(states)
            for j in heads:
                ss_ref[j, c] = S[j]
            wS = _each(lambda ln, s: _mm16(w_ref[rows, ln], s), lanes, S)
            qS = _each(lambda ln, s: _mm16(qd_ref[rows, ln], s), lanes, S)
            vn = _each(lambda ln, ws: u_ref[rows, ln] - ws, lanes, wS)
            o = _each(lambda j, qs, vn_: qs + _mm16(qk_ref[j, c], vn_), heads, qS, vn)
            new = _each(lambda j, ln, s, vn_: s * gl_ref[j, c] + _mm16(kd_ref[rows, ln], vn_, "tn"),
                        heads, lanes, S, vn)
            for j in heads:
                o_ref[rows, lanes[j]] = o[j]
                vn_ref[rows, lanes[j]] = vn[j]
            return tuple(new)
        out = lax.fori_loop(0, cpb, step, tuple(s_scr[j] for j in heads))
        for j in heads:
            s_scr[j] = out[j]

    full = jax.ShapeDtypeStruct((T, H * LANE), F32)
    return pl.pallas_call(
        body, name=name, grid=(H // hb, N // cpb),
        in_specs=[col] * 4 + [sq, vec],
        out_specs=[col, col, st],
        out_shape=[full, full, jax.ShapeDtypeStruct((H, N, HEAD, HEAD), F32)],
        scratch_shapes=[pltpu.VMEM((hb, HEAD, HEAD), F32)],
        compiler_params=_params(("parallel", "arbitrary")),
    )(u, w, qd, kd, qk, gl)


def _delta_scan_bwd(do, w, qd, kd, vn, qk, gl, ss, H, *, name):
    T = do.shape[0]
    N = T // CHUNK
    cpb = _tile(N, CPB_SCAN, 4)
    hb = min(GRP, H)
    col, sq, vec, st = _scan_specs(H, N, cpb, hb, True)
    lanes = [slice(j * LANE, (j + 1) * LANE) for j in range(hb)]
    heads = list(range(hb))

    def body(do_ref, w_ref, qd_ref, kd_ref, vn_ref, qk_ref, gl_ref, ss_ref,
             du_ref, dw_ref, dqd_ref, dkd_ref, dqk_ref, dgl_ref, ds_scr):
        @pl.when(pl.program_id(1) == 0)
        def _():
            ds_scr[...] = jnp.zeros_like(ds_scr)

        def step(i, dstates):
            c = cpb - 1 - i
            rows = pl.ds(pl.multiple_of(c * CHUNK, CHUNK), CHUNK)
            dS = list(dstates)
            S = [ss_ref[j, c] for j in heads]
            dov = [do_ref[rows, ln] for ln in lanes]
            vnv = [vn_ref[rows, ln] for ln in lanes]
            a1 = _each(lambda j, d_: _mm16(qk_ref[j, c], d_, "tn"), heads, dov)
            a2 = _each(lambda ln, ds: _mm16(kd_ref[rows, ln], ds), lanes, dS)
            dvn = _each(lambda x, y: x + y, a1, a2)
            dqd = _each(lambda d_, s: _mm16(d_, s, "nt"), dov, S)
            dkd = _each(lambda v_, ds: _mm16(v_, ds, "nt"), vnv, dS)
            dqk = _each(lambda d_, v_: _mm16(d_, v_, "nt"), dov, vnv)
            dw = _each(lambda dv_, s: -_mm16(dv_, s, "nt"), dvn, S)
            b1 = _each(lambda ln, d_: _mm16(qd_ref[rows, ln], d_, "tn"), lanes, dov)
            b2 = _each(lambda ln, dv_: _mm16(w_ref[rows, ln], dv_, "tn"), lanes, dvn)
            new = _each(lambda j, x, y, ds: x + ds * gl_ref[j, c] - y, heads, b1, b2, dS)
            for j in heads:
                du_ref[rows, lanes[j]] = dvn[j]
                dw_ref[rows, lanes[j]] = dw[j]
                dqd_ref[rows, lanes[j]] = dqd[j]
                dkd_ref[rows, lanes[j]] = dkd[j]
                dqk_ref[j, c] = dqk[j]
                dgl = jnp.sum(jnp.sum(dS[j] * S[j], axis=1, keepdims=True), axis=0, keepdims=True)
                dgl_ref[j, c] = jnp.broadcast_to(dgl, (1, LANE))
            return tuple(new)
        out = lax.fori_loop(0, cpb, step, tuple(ds_scr[j] for j in heads))
        for j in heads:
            ds_scr[j] = out[j]

    full = jax.ShapeDtypeStruct((T, H * LANE), F32)
    return pl.pallas_call(
        body, name=name, grid=(H // hb, N // cpb),
        in_specs=[col] * 5 + [sq, vec, st],
        out_specs=[col] * 4 + [sq, vec],
        out_shape=[full] * 4 + [jax.ShapeDtypeStruct((H, N, CHUNK, CHUNK), F32), jax.ShapeDtypeStruct((H, N, 1, LANE), F32)],
        scratch_shapes=[pltpu.VMEM((hb, HEAD, HEAD), F32)],
        compiler_params=_params(("parallel", "arbitrary")),
    )(do, w, qd, kd, vn, qk, gl, ss)


def _delta_prep_bwd(qkv, gamB, bB, ti, u, w, qk, du, dw, dqd, dkd, dqk, dgl, H, *, name):
    T = qkv.shape[0]
    N = T // CHUNK
    cpb = _tile(N, CPB, 8)
    grp = min(GRP, cpb)
    col, bc, sq, vec = _delta_specs(T, H, cpb)

    def body(q_ref, k_ref, v_ref, g_ref, b_ref, ti_ref, u_ref, w_ref, qk_ref,
             du_ref, dw_ref, dqd_ref, dkd_ref, dqk_ref, dgl_ref,
             dq_ref, dk_ref, dv_ref, dg_ref, db_ref):
        ones = jnp.ones((CHUNK, LANE), F32)
        strict = _tri(strict=True)
        last = lax.broadcasted_iota(jnp.int32, (CHUNK, LANE), 0) == CHUNK - 1
        lsum = lambda x: jnp.sum(x, axis=-1, keepdims=True)
        for c0 in range(0, cpb, grp):
            cs = list(range(c0, c0 + grp))
            rows = [slice(c * CHUNK, (c + 1) * CHUNK) for c in cs]
            ld = lambda r_: [r_[r, :] for r in rows]
            q, k, v, uv, wv, duv, dwv, dqd_v, dkd_v = (ld(r_) for r_ in (q_ref, k_ref, v_ref, u_ref, w_ref, du_ref, dw_ref, dqd_ref, dkd_ref))
            bb = [b_ref[0, r, :] for r in rows]
            gam = [g_ref[0, r, :] for r in rows]
            Ti = [ti_ref[0, c] for c in cs]
            QK = [qk_ref[0, c] for c in cs]
            dqk_v = [dqk_ref[0, c] for c in cs]
            D = _each(_decay, gam)
            e = _each(jnp.exp, gam)
            glast = [g_[CHUNK - 1:CHUNK, :] for g_ in gam]
            eL = _each(lambda gl_, g_: jnp.exp(gl_ - g_), glast, gam)
            kk = _each(lambda k_: _mm16(k_, k_, "nt"), k)
            KKD = _each(lambda kk_, D_: jnp.where(strict, kk_ * D_, 0.0), kk, D)
            dru = _each(lambda t, d_: _mm(t, d_, "tn"), Ti, duv)
            drw = _each(lambda t, d_: _mm(t, d_, "tn"), Ti, dwv)
            l1 = _each(lambda a, b: _mm(a, b, "nt"), dru, uv)
            l2 = _each(lambda a, b: _mm(a, b, "nt"), drw, wv)
            dL = _each(lambda a, b: jnp.where(strict, -(a + b), 0.0), l1, l2)
            Mm = _each(lambda dl, b_: dl * b_[:, :CHUNK], dL, bb)
            dKK = _each(lambda m_, D_: m_ * D_, Mm, D)
            dQK = _each(lambda a, D_: a * D_, dqk_v, D)
            P = _each(lambda m_, kkd, a, qk_: m_ * kkd + a * qk_, Mm, KKD, dqk_v, QK)
            q1 = _each(lambda a, k_: _mm16(a, k_), dQK, k)
            k1 = _each(lambda a, q_: _mm16(a, q_, "tn"), dQK, q)
            k2 = _each(lambda a, k_: _mm16(a, k_), dKK, k)
            k3 = _each(lambda a, k_: _mm16(a, k_, "tn"), dKK, k)
            s1 = _each(lambda dl, kkd: _mm(dl * kkd, ones), dL, KKD)
            p1 = _each(lambda p_: _mm(p_, ones), P)
            p2 = _each(lambda p_: _mm(p_, ones, "tn"), P)
            for i, c in enumerate(cs):
                r = rows[i]
                bek = bb[i] * e[i]
                kdv = eL[i] * k[i]
                dq_ref[r, :] = q1[i] + e[i] * dqd_v[i]
                dk_ref[r, :] = k1[i] + k2[i] + k3[i] + bek * drw[i] + eL[i] * dkd_v[i]
                dv_ref[r, :] = bb[i] * dru[i]
                db_ref[0, r, :] = s1[i] + lsum(dru[i] * v[i]) + lsum(drw[i] * e[i] * k[i])
                dgam = (p1[i] - p2[i] + lsum(drw[i] * bek * k[i]) + lsum(dqd_v[i] * e[i] * q[i])
                        - lsum(dkd_v[i] * kdv))
                xlast = jnp.sum(lsum(dkd_v[i] * kdv), axis=0, keepdims=True) + jnp.exp(glast[i]) * dgl_ref[0, c]
                dg_ref[0, r, :] = dgam + jnp.where(last, xlast, 0.0)

    full = jax.ShapeDtypeStruct((T, H * LANE), F32)
    bcs = jax.ShapeDtypeStruct((H, T, LANE), F32)
    return pl.pallas_call(
        body, name=name, grid=(H, N // cpb),
        in_specs=[col(0), col(H), col(2 * H), bc, bc, sq, col(0), col(0), sq, col(0), col(0), col(0), col(0), sq, vec],
        out_specs=[col(0), col(0), col(0), bc, bc],
        out_shape=[full, full, full, bcs, bcs],
        compiler_params=_params(("parallel", "parallel")),
    )(qkv, qkv, qkv, gamB, bB, ti, u, w, qk, du, dw, dqd, dkd, dqk, dgl)


def _adam(parts, w, m, v, *, name, own=None, me=None):
    P, R, C = parts.shape
    if R > 256 and R % 8:
        tr, tc = R, _tile(C, 256)
    else:
        tr, tc = _tile(R, 256, 8), C
    n_own = 0 if own is None else 2

    def body(*refs):
        p_ref, w_ref, m_ref, v_ref, g_ref, d_ref, nm_ref, nv_ref = refs[n_own:]
        g = None
        for i in range(P):
            t = p_ref[i].astype(F32)
            if n_own:
                t = jnp.where(refs[0][0] == i, refs[1][...].astype(F32), t)
            g = t if g is None else g + t
        mn = ADAM_B1 * m_ref[...] + (1.0 - ADAM_B1) * g
        vn = ADAM_B2 * v_ref[...] + (1.0 - ADAM_B2) * (g * g)
        m_hat = mn / (1.0 - ADAM_B1 ** ADAM_STEP)
        v_hat = vn / (1.0 - ADAM_B2 ** ADAM_STEP)
        g_ref[...] = g
        d_ref[...] = -ADAM_LR * (m_hat / (jnp.sqrt(v_hat) + ADAM_EPS) + ADAM_WD * w_ref[...])
        nm_ref[...] = mn
        nv_ref[...] = vn

    blk = pl.BlockSpec((tr, tc), lambda i, j: (i, j))
    return pl.pallas_call(
        body, name=name, grid=(R // tr, C // tc),
        in_specs=[pl.BlockSpec(memory_space=pltpu.SMEM), blk][:n_own] + [pl.BlockSpec((P, tr, tc), lambda i, j: (0, i, j)), blk, blk, blk],
        out_specs=[blk] * 4, out_shape=[jax.ShapeDtypeStruct((R, C), F32)] * 4,
        compiler_params=_params(("parallel", "parallel")),
    )(*([me, own] if n_own else []), parts, w, m, v)


def _mesh_pos():
    return lax.axis_index("x"), lax.axis_index("y"), lax.axis_index("c")


def _peer(k):
    x, y, c = _mesh_pos()
    px, py, pc = x ^ ((k >> 2) & 1), y ^ ((k >> 1) & 1), c ^ (k & 1)
    return (px, py, pc), 4 * px + 2 * py + pc


def _exchange(arrays, scatter, *, name, after=None):
    n = len(arrays)
    n_in = n if after is None else n + 1
    blocks = [a.shape[1:] if scatter else a.shape for a in arrays]

    def body(*refs):
        srcs, dsts = refs[:n], refs[n_in:n_in + n]
        send_sems, recv_sems, local_sems = refs[n_in + n:]
        x, y, c = _mesh_pos()
        me = 4 * x + 2 * y + c
        local, sends = [], []
        for a in range(n):
            cp = pltpu.make_async_copy(srcs[a].at[me] if scatter else srcs[a], dsts[a].at[me], local_sems.at[a])
            cp.start()
            local.append(cp)
            for k in range(1, N_DEV):
                dev, idx = _peer(k)
                cp = pltpu.make_async_remote_copy(
                    src_ref=srcs[a].at[idx] if scatter else srcs[a], dst_ref=dsts[a].at[me],
                    send_sem=send_sems.at[a * N_DEV + k], recv_sem=recv_sems.at[a * N_DEV + k],
                    device_id=dev, device_id_type=MESH)
                cp.start()
                sends.append(cp)
        for a in range(n):
            for k in range(1, N_DEV):
                dev, idx = _peer(k)
                pltpu.make_async_remote_copy(
                    src_ref=srcs[a].at[idx] if scatter else srcs[a], dst_ref=dsts[a].at[idx],
                    send_sem=send_sems.at[a * N_DEV + k], recv_sem=recv_sems.at[a * N_DEV + k],
                    device_id=dev, device_id_type=MESH).wait_recv()
        for cp in sends:
            cp.wait_send()
        for cp in local:
            cp.wait()

    anyspec = pl.BlockSpec(memory_space=pl.ANY)
    return pl.pallas_call(
        body, name=name, in_specs=[anyspec] * n_in, out_specs=[anyspec] * n,
        out_shape=[jax.ShapeDtypeStruct((N_DEV,) + tuple(b), a.dtype) for a, b in zip(arrays, blocks)],
        scratch_shapes=[pltpu.SemaphoreType.DMA((n * N_DEV,)), pltpu.SemaphoreType.DMA((n * N_DEV,)),
                        pltpu.SemaphoreType.DMA((n,))],
    )(*arrays, *([] if after is None else [after]))


_ANY = pl.BlockSpec(memory_space=pl.ANY)
_SEM = pl.BlockSpec(memory_space=pltpu.SEMAPHORE)
_EFFECT = pltpu.SideEffectType.DATAFLOW_SIDE_EFFECTING


def _in_hbm(a):
    return pltpu.with_memory_space_constraint(a, pltpu.HBM)


def _split_copy(src, land, send, recv, k, me, scatter, landed, by_chip=False):
    dev, idx = _peer(k)
    if by_chip:
        idx, me = lax.div(idx, 2), lax.div(me, 2)
    return pltpu.make_async_remote_copy(
        src_ref=src.at[idx] if scatter else src, dst_ref=land.at[idx if landed else me],
        send_sem=send.at[k], recv_sem=recv.at[k], device_id=dev, device_id_type=MESH)


ALL_PEERS = tuple(range(1, N_DEV))
SIBLING = 1
SAME_CORE = (2, 4, 6)


def _split_start(srcs, lands, scatter, *, name, relations=None, by_chip=False):
    n = len(srcs)
    relations = relations or [ALL_PEERS] * n

    def body(*refs):
        src, land, send, recv, token = refs[:n], refs[n:2 * n], refs[2 * n:3 * n], refs[3 * n:4 * n], refs[-1]
        x, y, c = _mesh_pos()
        me = 4 * x + 2 * y + c
        for a in range(n):
            for k in relations[a]:
                _split_copy(src[a], land[a], send[a], recv[a], k, me, scatter, False, by_chip).start()
        token[...] = jnp.zeros_like(token)

    outs = pl.pallas_call(
        body, name=name,
        out_shape=[pltpu.SemaphoreType.DMA((N_DEV,))] * (2 * n) + [pltpu.HBM(t.shape, t.dtype) for t in list(srcs) + list(lands)]
        + [jax.ShapeDtypeStruct((8, LANE), F32)],
        in_specs=[_ANY] * (2 * n), out_specs=[_SEM] * (2 * n) + [_ANY] * (2 * n) + [pl.BlockSpec(memory_space=pltpu.VMEM)],
        input_output_aliases={i: 2 * n + i for i in range(2 * n)},
        compiler_params=pltpu.CompilerParams(has_side_effects=_EFFECT),
    )(*[_in_hbm(t) for t in list(srcs) + list(lands)])
    handles = [(outs[a], outs[n + a], outs[2 * n + a], outs[3 * n + a]) for a in range(n)]
    return handles, outs[-1]


def _split_wait(handle, after, scatter, *, name, relations=ALL_PEERS, by_chip=False):
    send, recv, src_thru, land_thru = handle

    def body(src_ref, land_ref, send_ref, recv_ref, after_ref, src_out, land_out):
        x, y, c = _mesh_pos()
        me = 4 * x + 2 * y + c
        for k in relations:
            cp = _split_copy(src_ref, land_ref, send_ref, recv_ref, k, me, scatter, True, by_chip)
            cp.wait_send()
            cp.wait_recv()

    return pl.pallas_call(
        body, name=name,
        out_shape=(pltpu.HBM(src_thru.shape, src_thru.dtype), pltpu.HBM(land_thru.shape, land_thru.dtype)),
        in_specs=(_ANY, _ANY, _SEM, _SEM, _ANY), out_specs=(_ANY, _ANY), input_output_aliases={0: 0, 1: 1},
        compiler_params=pltpu.CompilerParams(has_side_effects=_EFFECT),
    )(src_thru, land_thru, send, recv, after)[1]


def _forward_copy(land, fsend, frecv, k, landed):
    x, y, c = _mesh_pos()
    _, idx = _peer(k | SIBLING if landed else k)
    return pltpu.make_async_remote_copy(src_ref=land.at[idx], dst_ref=land.at[idx], send_sem=fsend.at[k],
                                        recv_sem=frecv.at[k], device_id=(x, y, 1 - c), device_id_type=MESH)


def _gather_forward(handle, after, *, name):
    send, recv, src_thru, land_thru = handle

    def body(src_ref, land_ref, send_ref, recv_ref, after_ref, src_out, land_out, fsend, frecv):
        x, y, c = _mesh_pos()
        me = 4 * x + 2 * y + c
        for k in SAME_CORE:
            _split_copy(src_ref, land_ref, send_ref, recv_ref, k, me, False, True).wait_recv()
            _forward_copy(land_ref, fsend, frecv, k, False).start()

    src2, land2, fsend, frecv = pl.pallas_call(
        body, name=name,
        out_shape=(pltpu.HBM(src_thru.shape, src_thru.dtype), pltpu.HBM(land_thru.shape, land_thru.dtype),
                   pltpu.SemaphoreType.DMA((N_DEV,)), pltpu.SemaphoreType.DMA((N_DEV,))),
        in_specs=(_ANY, _ANY, _SEM, _SEM, _ANY), out_specs=(_ANY, _ANY, _SEM, _SEM), input_output_aliases={0: 0, 1: 1},
        compiler_params=pltpu.CompilerParams(has_side_effects=_EFFECT),
    )(src_thru, land_thru, send, recv, after)
    return (send, recv, src2, land2), (fsend, frecv)


def _gather_wait_two_level(handle, fwd, *, name):
    send, recv, src_thru, land_thru = handle
    fsend, frecv = fwd

    def body(src_ref, land_ref, send_ref, recv_ref, fsend_ref, frecv_ref, src_out, land_out):
        x, y, c = _mesh_pos()
        me = 4 * x + 2 * y + c
        for k in (SIBLING,) + SAME_CORE:
            _split_copy(src_ref, land_ref, send_ref, recv_ref, k, me, False, True).wait_send()
        _split_copy(src_ref, land_ref, send_ref, recv_ref, SIBLING, me, False, True).wait_recv()
        for k in SAME_CORE:
            _forward_copy(land_ref, fsend_ref, frecv_ref, k, False).wait_send()
            _forward_copy(land_ref, fsend_ref, frecv_ref, k, True).wait_recv()

    return pl.pallas_call(
        body, name=name,
        out_shape=(pltpu.HBM(src_thru.shape, src_thru.dtype), pltpu.HBM(land_thru.shape, land_thru.dtype)),
        in_specs=(_ANY, _ANY, _SEM, _SEM, _SEM, _SEM), out_specs=(_ANY, _ANY), input_output_aliases={0: 0, 1: 1},
        compiler_params=pltpu.CompilerParams(has_side_effects=_EFFECT),
    )(src_thru, land_thru, send, recv, fsend, frecv)[1]


def _sibling_copy(src, land, send, recv):
    x, y, c = _mesh_pos()
    return pltpu.make_async_remote_copy(src_ref=src, dst_ref=land, send_sem=send, recv_sem=recv,
                                        device_id=(x, y, 1 - c), device_id_type=MESH)


def _sibling_start(v, *, name):
    def body(v_ref, land_ref, send, recv, v_thru, land_thru, token):
        _sibling_copy(v_ref, land_ref, send, recv).start()
        token[...] = jnp.zeros_like(token)

    send, recv, v_thru, land_thru, token = pl.pallas_call(
        body, name=name,
        out_shape=(pltpu.SemaphoreType.DMA(()), pltpu.SemaphoreType.DMA(()), pltpu.HBM(v.shape, v.dtype),
                   pltpu.HBM(v.shape, v.dtype), jax.ShapeDtypeStruct((8, LANE), F32)),
        in_specs=(_ANY, _ANY), out_specs=(_SEM, _SEM, _ANY, _ANY, pl.BlockSpec(memory_space=pltpu.VMEM)),
        input_output_aliases={0: 2, 1: 3}, compiler_params=pltpu.CompilerParams(has_side_effects=_EFFECT),
    )(_in_hbm(v), _in_hbm(lax.empty(v.shape, v.dtype)))
    return (send, recv, v_thru, land_thru), token


def _sibling_wait(handle, after, *, name):
    send, recv, v_thru, land_thru = handle

    def body(v_ref, land_ref, send_ref, recv_ref, after_ref, v_out, land_out):
        cp = _sibling_copy(v_ref, land_ref, send_ref, recv_ref)
        cp.wait_send()
        cp.wait_recv()

    return pl.pallas_call(
        body, name=name, out_shape=(pltpu.HBM(v_thru.shape, v_thru.dtype), pltpu.HBM(land_thru.shape, land_thru.dtype)),
        in_specs=(_ANY, _ANY, _SEM, _SEM, _ANY), out_specs=(_ANY, _ANY), input_output_aliases={0: 0, 1: 1},
        compiler_params=pltpu.CompilerParams(has_side_effects=_EFFECT),
    )(v_thru, land_thru, send, recv, after)[1]


def _add_partials(a, b, *, name):
    R, C = a.shape
    tc = _tile(C, 256)

    def body(a_ref, b_ref, o_ref):
        o_ref[...] = (a_ref[...].astype(F32) + b_ref[...].astype(F32)).astype(o_ref.dtype)

    blk = pl.BlockSpec((R, tc), lambda j: (0, j))
    return pl.pallas_call(body, name=name, grid=(C // tc,), in_specs=[blk, blk], out_specs=blk,
                          out_shape=jax.ShapeDtypeStruct((R, C), a.dtype), compiler_params=_params(("parallel",)))(a, b)


def _local_step(x, p, tgt, S, wt, conv, emit, emit_late):
    T, D = x.shape
    CW = DNW = D // 2
    H = DNW // HEAD
    nA, nD = CW // LANE, DNW // LANE
    qkv_off, z_off, ab_off = 3 * nA, 3 * nA + 3 * nD, 3 * nA + 4 * nD
    alog = jnp.pad(S["a_log"], ((0, 0), (0, LANE - H)))
    dtb = jnp.pad(S["dt_bias"], ((0, 0), (0, LANE - H)))

    h1 = _rms_fwd(x, S["g_mix"], name="rms1_fwd")
    w_in, cv = wt("w_in", h1), conv(h1)
    proj = _matmul(h1, w_in, "nt", name="mm_in")
    y_a = _group_a_fwd(proj, cv["conv_a"], CW, name="group_a_fwd")
    qkv = _qkv_fwd(proj, cv["conv_qkv"], qkv_off, H, name="qkv_fwd")
    gb, gamc = _gates_fwd(proj, alog, dtb, ab_off, H, name="gates_fwd")
    bcast = lambda cols: jnp.broadcast_to(cols.T[:, :, None], (H, T, LANE))
    gamB, bB = bcast(gamc[:, :H]), bcast(gb[:, H:2 * H])
    u, w, qd, kd, qk, ti, gl = _delta_prep_fwd(qkv, gamB, bB, H, name="delta_prep_fwd")
    o, vn, ss = _delta_scan_fwd(u, w, qd, kd, qk, gl, H, name="delta_scan_fwd")
    y_b = _gated_norm_fwd(o, proj, S["dn_g"], z_off, name="gated_norm_fwd")
    ycat = jnp.concatenate([y_a, y_b], axis=1)
    w_out = wt("w_out", ycat)
    rows = dict(tm=ROW_TILE, tn=D)
    x1, h2 = _matmul(ycat, w_out, "nn", name="mm_out", out_dtypes=(F32, BF16), epilogue=_epi_residual_rms,
                     extras=(x,), vec_extras=(S["g_ffn"],), **rows)
    w_up = wt("w_up", h2)
    up_pre = _matmul(h2, w_up, "nn", name="mm_up", b_shards=True, tn=SHARD_TILE)
    act = _ffn_act_fwd(up_pre, cv["conv_ffn"], name="ffn_act_fwd")
    w_down = wt("w_down", act)
    x2 = _matmul(act, w_down, "nn", name="mm_down", epilogue=lambda acc, r: (acc + r,), extras=(x1,), tk=LONG_K)
    h3 = _rms_fwd(x2, S["g_ple"], name="rms3_fwd")
    w_pp, w_pg = wt("w_pp", h3), wt("w_pg", h3)
    pp = _matmul(p, w_pp, "nn", name="mm_pp", b_shards=True)

    def ple_epi(acc, x2r, ppr):
        s = jax.nn.sigmoid(acc)
        return x2r + s * ppr, s

    x3, sg = _matmul(h3, w_pg, "nn", name="mm_pg", out_dtypes=(F32, F32), epilogue=ple_epi, extras=(x2, pp), tm=512)
    dx3, dg_final, loss = _final_loss(x3, S["g_final"], tgt, name="final_loss")

    G = {"g_final": dg_final}
    dpg, dpp = _ple_bwd(dx3, pp, sg, name="ple_bwd")
    tok = emit({"w_pp": _matmul(p, dpp, "tn", name="mm_dwpp", out_dtypes=(BF16,), out_shards=True, tk=LONG_K),
                "w_pg": _matmul(h3, dpg, "tn", name="mm_dwpg", out_dtypes=(BF16,), tk=LONG_K)})
    bwd = dict(out_dtypes=(F32, BF16), epilogue=_epi_rms_bwd(2), n_vec=1, **rows)
    dx2, dx2b, G["g_ple"] = _matmul(dpg, w_pg, "nt", name="mm_dh3", after=tok, extras=(x2, dx3),
                                    vec_extras=(S["g_ple"],), **bwd)
    tok = emit({"w_down": _matmul(act, dx2b, "tn", name="mm_dwdown", out_dtypes=(BF16,), tk=LONG_K)})
    dact = _matmul(dx2b, w_down, "nt", name="mm_dact", after=tok)
    dup_g, dup_v, dcf_g, dcf_v = _ffn_act_bwd(up_pre, cv["conv_ffn"], dact, name="ffn_act_bwd")
    G["conv_ffn"] = jnp.concatenate([dcf_g, dcf_v], axis=1)
    dup = jnp.concatenate([dup_g, dup_v], axis=1)
    tok = emit({"w_up": _matmul(h2, dup, "tn", name="mm_dwup", out_dtypes=(BF16,), out_shards=True, tn=SHARD_TILE, tk=LONG_K)})
    dh2 = _matmul(dup, w_up, "nt", name="mm_dh2", after=tok, b_shards=True, tk=2 * SHARD_TILE)
    dx1, dx1b, G["g_ffn"] = _rms_bwd(x1, S["g_ffn"], dh2, dx2, name="rms2_bwd")
    tok = emit({"w_out": _matmul(ycat, dx1b, "tn", name="mm_dwout", out_dtypes=(BF16,), tk=LONG_K)})
    dycat = _matmul(dx1b, w_out, "nt", name="mm_dycat", after=tok)
    do, dz, G["dn_g"] = _gated_norm_bwd(o, proj, S["dn_g"], dycat, z_off, nA, name="gated_norm_bwd")
    du, dw, dqd, dkd, dqk, dgl = _delta_scan_bwd(do, w, qd, kd, vn, qk, gl, ss, H, name="delta_scan_bwd")
    dq, dk, dv, dgB, dbB = _delta_prep_bwd(qkv, gamB, bB, ti, u, w, qk, du, dw, dqd, dkd, dqk, dgl, H,
                                           name="delta_prep_bwd")
    dgb = jnp.pad(jnp.concatenate([dgB[:, :, 0].T, dbB[:, :, 0].T], axis=1), ((0, 0), (0, LANE - 2 * H)))
    dab, dal, ddt = _gates_bwd(proj, alog, dtb, dgb, ab_off, H, name="gates_bwd")
    G["a_log"], G["dt_bias"] = dal[:, :H], ddt[:, :H]
    dqkv, G["conv_qkv"] = _qkv_bwd(proj, cv["conv_qkv"], dq, dk, dv, qkv_off, H, name="qkv_bwd")
    dax, dab_, dac, G["conv_a"] = _group_a_bwd(proj, cv["conv_a"], dycat, CW, name="group_a_bwd")
    in_p = w_in.shape[0]
    dproj = jnp.concatenate([dax, dab_, dac, dqkv, dz, dab, jnp.zeros((T, in_p - (ab_off + 1) * LANE), BF16)], axis=1)
    tok = emit({"w_in": _matmul(dproj, h1, "tn", name="mm_dwin", out_dtypes=(BF16,), tk=LONG_K)})
    dh1 = _matmul(dproj, w_in, "nn", name="mm_dh1", after=tok, tk=LONG_K)
    tok = emit_late(dh1)
    g_mix = S["g_mix"] if tok is None else S["g_mix"] + tok[0, 0]
    grad_x, _, G["g_mix"] = _rms_bwd(x, g_mix, dh1, dx1, name="rms1_bwd")
    return loss, grad_x, G


def _col_sharded(landed):
    _, R, C = landed.shape
    return jnp.transpose(landed, (1, 0, 2)).reshape(R, N_DEV * C)


def kernel(x, p, norm_mix_g, w_in, conv_a_w, conv_qkv_w, a_log, dt_bias, dn_norm_g, w_out, norm_ffn_g, w_up, conv_ffn_w, w_down, norm_ple_g, w_ple_gate, w_ple_proj, final_norm_g, loss_target, m_norm_mix_g, m_w_in, m_conv_a_w, m_conv_qkv_w, m_a_log, m_dt_bias, m_dn_norm_g, m_w_out, m_norm_ffn_g, m_w_up, m_conv_ffn_w, m_w_down, m_norm_ple_g, m_w_ple_gate, m_w_ple_proj, m_final_norm_g, v_norm_mix_g, v_w_in, v_conv_a_w, v_conv_qkv_w, v_a_log, v_dt_bias, v_dn_norm_g, v_w_out, v_norm_ffn_g, v_w_up, v_conv_ffn_w, v_w_down, v_norm_ple_g, v_w_ple_gate, v_w_ple_proj, v_final_norm_g):
    T, D = x.shape[1], x.shape[2]
    xd, _, cd = _mesh_pos()
    me = 4 * xd + 2 * lax.axis_index("y") + cd

    conv_sh = [conv_a_w[0], conv_qkv_w[0], conv_ffn_w[0]]
    conv_n = [c.size for c in conv_sh]
    pack_rows = -(-sum(conv_n) // LANE)
    conv_pack = jnp.pad(jnp.concatenate([c.reshape(-1) for c in conv_sh]), (0, pack_rows * LANE - sum(conv_n))).reshape(pack_rows, LANE)
    names = ["w_in", "conv", "w_out", "w_up", "w_down", "w_pg", "w_pp"]
    tr_ = lambda t: jnp.swapaxes(t, 1, 2)
    shards = [w_in[0].T.astype(BF16), conv_pack, w_out[0].astype(BF16), w_up[0].astype(BF16), w_down[0].astype(BF16),
              w_ple_gate[0].astype(BF16), w_ple_proj[0].astype(BF16)]
    empty_slots = lambda blocks: [lax.empty((N_DEV,) + tuple(b.shape), b.dtype) for b in blocks]
    handles, tok0 = _split_start(shards, empty_slots(shards), False, name="gather_start",
                                 relations=[(SIBLING,) + SAME_CORE] + [ALL_PEERS] * (len(shards) - 1))
    handle = dict(zip(names, handles))
    own = dict(zip(names, shards))
    in_cols = N_DEV * w_in.shape[2]
    in_p = (in_cols // LANE) * LANE + AB_PAD
    in_place = {"w_up", "w_pp"}

    def gathered(name, after):
        if name == "w_in":
            passed, fwd = _gather_forward(handle[name], after, name="gather_forward_w_in")
            landed = _gather_wait_two_level(passed, fwd, name="gather_wait_w_in")
        else:
            landed = _split_wait(handle[name], after, False, name="gather_wait_" + name)
        return lax.dynamic_update_index_in_dim(landed, own[name], me, 0)

    def wt(name, after):
        landed = gathered(name, after)
        if name in in_place:
            return landed
        full = landed.reshape(-1, D)
        return jnp.pad(full, ((0, in_p - in_cols), (0, 0))) if name == "w_in" else full

    def conv(after):
        flat = gathered("conv", after).reshape(N_DEV, pack_rows * LANE)
        out, o_ = {}, 0
        for nm, c, n_ in zip(("conv_a", "conv_qkv", "conv_ffn"), conv_sh, conv_n):
            out[nm] = _col_sharded(flat[:, o_:o_ + n_].reshape((N_DEV,) + c.shape))
            o_ += n_
        return out

    pending, mine = {}, {}

    n_chip = N_DEV // 2
    my_chip = lax.div(me, 2)
    staged = {}

    def emit(grads):
        if "w_in" in grads:
            by_core = grads["w_in"][:in_cols].reshape(n_chip, 2, -1, D)
            staged["mine"] = lax.dynamic_index_in_dim(by_core, cd, 1, keepdims=False)
            staged["swap"], tok = _sibling_start(lax.dynamic_index_in_dim(by_core, 1 - cd, 1, keepdims=False),
                                                 name="scatter_sibling_start_w_in")
            return tok
        parts = [g if nm in in_place else g.reshape(N_DEV, -1, D) for nm, g in grads.items()]
        hs, tok = _split_start(parts, empty_slots([q[0] for q in parts]), True, name="scatter_start_" + "_".join(grads))
        pending.update(zip(grads, hs))
        mine.update({nm: lax.dynamic_index_in_dim(q, me, 0, keepdims=False) for nm, q in zip(grads, parts)})
        return tok

    def emit_late(after):
        got = _sibling_wait(staged["swap"], after, name="scatter_sibling_wait_w_in")
        shp = got.shape
        chip_sum = _add_partials(staged["mine"].reshape(-1, D), got.reshape(-1, D), name="add_sibling_w_in").reshape(shp)
        hs, tok = _split_start([chip_sum], [lax.empty(shp, chip_sum.dtype)], True, name="scatter_start_w_in",
                               relations=[SAME_CORE], by_chip=True)
        pending["w_in"] = hs[0]
        mine["w_in"] = lax.dynamic_index_in_dim(chip_sum, my_chip, 0, keepdims=False)
        return tok

    S = {
        "g_mix": norm_mix_g + tok0[0, 0], "a_log": a_log, "dt_bias": dt_bias, "dn_g": dn_norm_g, "g_ffn": norm_ffn_g,
        "g_ple": norm_ple_g, "g_final": final_norm_g.reshape(1, D),
    }

    loss_v, grad_x, G = _local_step(x[0], p[0, 0], loss_target[0], S, wt, conv, emit, emit_late)
    loss = lax.psum(loss_v[0, 0], ("x", "y", "c"))

    small_names = ["g_mix", "g_ffn", "g_ple", "g_final", "dn_g", "a_log", "dt_bias", "conv_a", "conv_qkv", "conv_ffn"]
    small_rows, pieces = [], []
    for nm in small_names:
        g_ = G[nm].reshape(-1)
        r_ = -(-g_.size // (8 * LANE)) * 8
        small_rows.append(r_)
        pieces.append(jnp.pad(g_, (0, r_ * LANE - g_.size)).reshape(r_, LANE))
    landed = {nm: _split_wait(h_, grad_x, True, name="scatter_wait_" + nm) for nm, h_ in pending.items() if nm != "w_in"}

    def adam(parts, w_, m_, v_, nm, own_=None, slot=me):
        shp = w_.shape
        w2, m2, v2 = (t.reshape(parts.shape[1:]) for t in (w_, m_, v_))
        kw = {} if own_ is None else {"own": own_, "me": slot.astype(jnp.int32).reshape(1)}
        return tuple(t.reshape(shp) for t in _adam(parts, w2, m2, v2, name="adam_" + nm, **kw))

    big = {
        "w_up": adam(landed["w_up"], w_up, m_w_up, v_w_up, "w_up", mine["w_up"]),
        "w_down": adam(landed["w_down"], w_down, m_w_down, v_w_down, "w_down", mine["w_down"]),
        "w_out": adam(landed["w_out"], w_out, m_w_out, v_w_out, "w_out", mine["w_out"]),
        "w_pg": adam(landed["w_pg"], w_ple_gate, m_w_ple_gate, v_w_ple_gate, "w_ple_gate", mine["w_pg"]),
        "w_pp": adam(landed["w_pp"], w_ple_proj, m_w_ple_proj, v_w_ple_proj, "w_ple_proj", mine["w_pp"]),
    }
    (small_l,) = _exchange([jnp.concatenate(pieces, axis=0)], False, name="gather_small_grads", after=big["w_pp"][1])

    def small_parts(nm):
        i = small_names.index(nm)
        r0 = sum(small_rows[:i])
        shp = G[nm].shape
        return small_l[:, r0:r0 + small_rows[i], :].reshape(N_DEV, -1)[:, :G[nm].size].reshape((N_DEV,) + shp)

    def conv_parts(nm, shard):
        full = small_parts(nm)
        C = shard.shape[-1]
        return lax.dynamic_slice_in_dim(full, me * C, C, axis=2)

    res = [
        adam(small_parts("g_mix"), norm_mix_g, m_norm_mix_g, v_norm_mix_g, "norm_mix_g"),
        None,
        adam(conv_parts("conv_a", conv_a_w), conv_a_w, m_conv_a_w, v_conv_a_w, "conv_a_w"),
        adam(conv_parts("conv_qkv", conv_qkv_w), conv_qkv_w, m_conv_qkv_w, v_conv_qkv_w, "conv_qkv_w"),
        adam(small_parts("a_log"), a_log, m_a_log, v_a_log, "a_log"),
        adam(small_parts("dt_bias"), dt_bias, m_dt_bias, v_dt_bias, "dt_bias"),
        adam(small_parts("dn_g"), dn_norm_g, m_dn_norm_g, v_dn_norm_g, "dn_norm_g"),
        big["w_out"],
        adam(small_parts("g_ffn"), norm_ffn_g, m_norm_ffn_g, v_norm_ffn_g, "norm_ffn_g"),
        big["w_up"],
        adam(conv_parts("conv_ffn", conv_ffn_w), conv_ffn_w, m_conv_ffn_w, v_conv_ffn_w, "conv_ffn_w"),
        big["w_down"],
        adam(small_parts("g_ple"), norm_ple_g, m_norm_ple_g, v_norm_ple_g, "norm_ple_g"),
        big["w_pg"],
        big["w_pp"],
        adam(small_parts("g_final"), final_norm_g.reshape(1, D), m_final_norm_g.reshape(1, D),
             v_final_norm_g.reshape(1, D), "final_norm_g"),
    ]
    res[-1] = tuple(t.reshape(D) for t in res[-1])
    landed_in = _split_wait(pending["w_in"], res[10][1], True, name="scatter_wait_w_in", relations=SAME_CORE,
                            by_chip=True)
    res[1] = tuple(tr_(t) for t in adam(landed_in, tr_(w_in), tr_(m_w_in), tr_(v_w_in), "w_in", mine["w_in"], my_chip))
    grads, deltas, new_m, new_v = zip(*res)
    return (loss, grad_x[None], *grads, *deltas, *new_m, *new_v)
```

```python
import functools

import jax
import jax.numpy as jnp
from jax import lax
from jax.experimental import pallas as pl
from jax.experimental.pallas import tpu as pltpu

F32 = jnp.float32
BF16 = jnp.bfloat16

EPS = 1e-6
CHUNK = 64
HEAD = 128
LANE = 128
N_DEV = 8
AB_PAD = 512

ADAM_LR = 0.001
ADAM_B1 = 0.9
ADAM_B2 = 0.999
ADAM_EPS = 1e-08
ADAM_WD = 0.01
ADAM_STEP = 10

MESH = pl.DeviceIdType.MESH


def _tile(dim, target, align=LANE):
    if dim <= target:
        return dim
    t = (target // align) * align
    while t > align and dim % t:
        t -= align
    assert dim % t == 0, (dim, target)
    return t


def _params(sem, vmem_mb=48):
    return pltpu.CompilerParams(dimension_semantics=sem, vmem_limit_bytes=vmem_mb << 20)


_DN = {"nn": (((1,), (0,)), ((), ())), "nt": (((1,), (1,)), ((), ())), "tn": (((0,), (0,)), ((), ()))}
LONG_K = 4096
SHARD_TILE = 1408


def _matmul(a, b, mode, *, name, out_dtypes=(F32,), epilogue=None, extras=(), vec_extras=(), n_vec=0, after=None,
            b_shards=False, out_shards=False, tm=1024, tn=1024, tk=2048):
    shard_w = b.shape[2] if b_shards else None
    if b_shards:
        b_rows, b_cols = b.shape[1], N_DEV * shard_w
    else:
        b_rows, b_cols = b.shape
    if mode == "nn":
        (M, K), (K2, N) = a.shape, (b_rows, b_cols)
    elif mode == "nt":
        (M, K), (N, K2) = a.shape, (b_rows, b_cols)
    else:
        (K, M), (K2, N) = a.shape, (b_rows, b_cols)
    assert K == K2, (name, a.shape, b.shape)
    tm = _tile(M, tm)
    tn = _tile(shard_w if (b_shards and mode == "nn") else N // N_DEV if out_shards else N, tn)
    grp = 1
    if b_shards and mode == "nt":
        grp = max(g for g in (1, 2, 4, 8) if g <= max(1, tk // shard_w))
    tk = grp * shard_w if grp > 1 else _tile(shard_w if (b_shards and mode == "nt") else K, tk)
    assert K % tk == 0, (name, K, tk)
    nk = K // tk
    n_ex, n_out = len(extras) + len(vec_extras), len(out_dtypes)
    assert n_vec == 0 or tn == N, (name, tn, N)
    dn = _DN[mode]

    n_tok = 0 if after is None else 1

    def body(a_ref, b_ref, *rest):
        rest = rest[n_tok:]
        ex_refs, out_refs, vec_refs = rest[:n_ex], rest[n_ex:n_ex + n_out], rest[n_ex + n_out:n_ex + n_out + n_vec]
        if grp > 1:
            part = sum(lax.dot_general(a_ref[:, s * shard_w:(s + 1) * shard_w].astype(BF16), b_ref[s].astype(BF16), dn,
                                       preferred_element_type=F32) for s in range(grp))
        else:
            part = lax.dot_general(a_ref[...].astype(BF16), b_ref[...].astype(BF16), dn, preferred_element_type=F32)
        first_rows = pl.program_id(0) == 0

        def finish(res):
            outs = (res,) if epilogue is None else epilogue(res, *[e[...] for e in ex_refs])
            for o_ref, val in zip(out_refs, outs[:n_out]):
                o_ref[...] = val.astype(o_ref.dtype)
            for v_ref, val in zip(vec_refs, outs[n_out:]):
                @pl.when(first_rows)
                def _(v_ref=v_ref, val=val):
                    v_ref[...] = val

                @pl.when(jnp.logical_not(first_rows))
                def _(v_ref=v_ref, val=val):
                    v_ref[...] += val

        if nk == 1:
            finish(part)
            return
        acc, k = rest[-1], pl.program_id(2)

        @pl.when(k == 0)
        def _():
            acc[...] = part

        @pl.when(k > 0)
        def _():
            acc[...] += part

        @pl.when(k == nk - 1)
        def _():
            finish(acc[...])

    a_spec = pl.BlockSpec((tk, tm), lambda i, j, k: (k, i)) if mode == "tn" else pl.BlockSpec((tm, tk), lambda i, j, k: (i, k))
    if b_shards and mode == "nn":
        per = shard_w // tn
        b_spec = pl.BlockSpec((None, tk, tn), lambda i, j, k: (lax.div(j, per), k, lax.rem(j, per)))
    elif b_shards and grp > 1:
        b_spec = pl.BlockSpec((grp, tn, shard_w), lambda i, j, k: (k, j, 0))
    elif b_shards:
        per = shard_w // tk
        b_spec = pl.BlockSpec((None, tn, tk), lambda i, j, k: (lax.div(k, per), j, lax.rem(k, per)))
    else:
        b_spec = pl.BlockSpec((tn, tk), lambda i, j, k: (j, k)) if mode == "nt" else pl.BlockSpec((tk, tn), lambda i, j, k: (k, j))
    mn_spec = pl.BlockSpec((tm, tn), lambda i, j, k: (i, j))
    vec_spec = pl.BlockSpec((1, tn), lambda i, j, k: (0, j))
    if out_shards:
        assert not extras
        per_o = (N // N_DEV) // tn
        out_spec = pl.BlockSpec((None, tm, tn), lambda i, j, k: (lax.div(j, per_o), i, lax.rem(j, per_o)))
        out_dims = (N_DEV, M, N // N_DEV)
    else:
        out_spec, out_dims = mn_spec, (M, N)
    outs = pl.pallas_call(
        body, name=name, grid=(M // tm, N // tn, nk),
        in_specs=[a_spec, b_spec] + [pl.BlockSpec((8, LANE), lambda i, j, k: (0, 0))] * n_tok
        + [mn_spec] * len(extras) + [vec_spec] * len(vec_extras),
        out_specs=[out_spec] * n_out + [vec_spec] * n_vec,
        out_shape=[jax.ShapeDtypeStruct(out_dims, dt) for dt in out_dtypes] + [jax.ShapeDtypeStruct((1, N), F32)] * n_vec,
        scratch_shapes=[pltpu.VMEM((tm, tn), F32)] if nk > 1 else [],
        compiler_params=_params(("arbitrary" if n_vec else "parallel", "parallel", "arbitrary"), 56),
    )(a, b, *([] if after is None else [after]), *extras, *vec_extras)
    return outs[0] if n_out + n_vec == 1 else outs


def _rms_fwd(x, g, *, name):
    T, D = x.shape
    tr = _tile(T, 256, 8)

    def body(x_ref, g_ref, h_ref):
        xv = x_ref[...]
        r = lax.rsqrt(jnp.mean(xv * xv, axis=-1, keepdims=True) + EPS)
        h_ref[...] = (xv * r * g_ref[...]).astype(h_ref.dtype)

    return pl.pallas_call(
        body, name=name, grid=(T // tr,),
        in_specs=[pl.BlockSpec((tr, D), lambda i: (i, 0)), pl.BlockSpec((1, D), lambda i: (0, 0))],
        out_specs=pl.BlockSpec((tr, D), lambda i: (i, 0)),
        out_shape=jax.ShapeDtypeStruct((T, D), BF16),
        compiler_params=_params(("parallel",)),
    )(x, g)


def _rms_bwd(x, g, dh, dres, *, name):
    T, D = x.shape
    tr = _tile(T, 256, 8)
    epi = _epi_rms_bwd(2)

    def body(x_ref, g_ref, dh_ref, dres_ref, dx_ref, dxb_ref, dg_ref):
        dx, _, dgp = epi(dh_ref[...], x_ref[...], dres_ref[...], g_ref[...])

        @pl.when(pl.program_id(0) == 0)
        def _():
            dg_ref[...] = jnp.zeros_like(dg_ref)

        dg_ref[...] += dgp
        dx_ref[...] = dx
        dxb_ref[...] = dx.astype(dxb_ref.dtype)

    row = pl.BlockSpec((tr, D), lambda i: (i, 0))
    vec = pl.BlockSpec((1, D), lambda i: (0, 0))
    return pl.pallas_call(
        body, name=name, grid=(T // tr,),
        in_specs=[row, vec, row, row], out_specs=[row, row, vec],
        out_shape=[jax.ShapeDtypeStruct((T, D), F32), jax.ShapeDtypeStruct((T, D), BF16), jax.ShapeDtypeStruct((1, D), F32)],
        compiler_params=_params(("arbitrary",)),
    )(x, g, dh, dres)


ROW_TILE = 256


def _epi_residual_rms(acc, res, g):
    xn = acc + res
    r = lax.rsqrt(jnp.mean(xn * xn, axis=-1, keepdims=True) + EPS)
    return xn, xn * r * g


def _epi_rms_bwd(n_copies):
    def epi(dh, x, dres, g):
        r = lax.rsqrt(jnp.mean(x * x, axis=-1, keepdims=True) + EPS)
        xh = x * r
        dxh = dh * g
        dx = dres + r * (dxh - xh * jnp.mean(dxh * xh, axis=-1, keepdims=True))
        return (dx,) * n_copies + (jnp.sum(dh * xh, axis=0, keepdims=True),)
    return epi


def _final_loss(x, g, tgt, *, name):
    T, D = x.shape
    tr = _tile(T, 256, 8)

    def body(x_ref, g_ref, t_ref, dx_ref, dg_ref, loss_ref):
        xv = x_ref[...]
        r = lax.rsqrt(jnp.mean(xv * xv, axis=-1, keepdims=True) + EPS)
        xh = xv * r
        gv = g_ref[...]
        err = xh * gv - t_ref[...]

        @pl.when(pl.program_id(0) == 0)
        def _():
            dg_ref[...] = jnp.zeros_like(dg_ref)
            loss_ref[...] = jnp.zeros_like(loss_ref)

        part = 0.5 * jnp.sum(jnp.mean(err * err, axis=-1, keepdims=True), axis=0, keepdims=True)
        loss_ref[...] += jnp.broadcast_to(part, loss_ref.shape)
        dy = err * (1.0 / D)
        dg_ref[...] += jnp.sum(dy * xh, axis=0, keepdims=True)
        dxh = dy * gv
        dx_ref[...] = r * (dxh - xh * jnp.mean(dxh * xh, axis=-1, keepdims=True))

    row = pl.BlockSpec((tr, D), lambda i: (i, 0))
    vec = pl.BlockSpec((1, D), lambda i: (0, 0))
    return pl.pallas_call(
        body, name=name, grid=(T // tr,),
        in_specs=[row, vec, row], out_specs=[row, vec, pl.BlockSpec((1, LANE), lambda i: (0, 0))],
        out_shape=[jax.ShapeDtypeStruct((T, D), F32), jax.ShapeDtypeStruct((1, D), F32),
                   jax.ShapeDtypeStruct((1, LANE), F32)],
        compiler_params=_params(("arbitrary",)),
    )(x, g, tgt)


def _ple_bwd(dx3, pp, sg, *, name):
    T, D = dx3.shape
    tr = _tile(T, 256, 8)

    def body(dx_ref, pp_ref, sg_ref, dpg_ref, dpp_ref):
        dx, s = dx_ref[...], sg_ref[...]
        dpg_ref[...] = (dx * pp_ref[...] * s * (1.0 - s)).astype(dpg_ref.dtype)
        dpp_ref[...] = (dx * s).astype(dpp_ref.dtype)

    row = pl.BlockSpec((tr, D), lambda i: (i, 0))
    return pl.pallas_call(
        body, name=name, grid=(T // tr,), in_specs=[row, row, row], out_specs=[row, row],
        out_shape=[jax.ShapeDtypeStruct((T, D), BF16)] * 2, compiler_params=_params(("parallel",)),
    )(dx3, pp, sg)


ROWS_QKV_FWD, ROWS_QKV_BWD, ROWS_FFN_FWD, ROWS_FFN_BWD, ROWS_GROUP_A = 512, 256, 256, 128, 256


def _ext(ref, r0, T, before, after, RC):
    parts = []
    if before:
        p0 = pl.multiple_of(jnp.maximum(r0 - 8, 0), 8)
        parts.append(jnp.where(r0 > 0, ref[pl.ds(p0, 8), :], 0.0))
    parts.append(ref[pl.ds(r0, RC), :])
    if after:
        n0 = pl.multiple_of(jnp.minimum(r0 + RC, T - 8), 8)
        parts.append(jnp.where(r0 + RC < T, ref[pl.ds(n0, 8), :], 0.0))
    return parts[0] if len(parts) == 1 else jnp.concatenate(parts, axis=0)


def _down(xx, s):
    return (xx if s == 0 else pltpu.roll(xx, s, 0))[8:, :]


def _up(xx, s, rows):
    return (xx if s == 0 else pltpu.roll(xx, xx.shape[0] - s, 0))[:rows, :]


def _conv_down(xx, w_ref, K):
    y = None
    for j in range(K):
        t = _down(xx, K - 1 - j) * w_ref[j:j + 1, :]
        y = t if y is None else y + t
    return y


def _fold8(x):
    return jnp.sum(x.reshape(x.shape[0] // 8, 8, x.shape[1]), axis=0)


def _silu(x):
    return x * jax.nn.sigmoid(x)


def _dsilu(x):
    s = jax.nn.sigmoid(x)
    return s * (1.0 + x * (1.0 - s))


def _col_specs(T, offs):
    return [pl.BlockSpec((T, LANE), functools.partial(lambda o, j: (0, o + j), o)) for o in offs]


def _group_a_fwd(proj, conv_w, CW, *, name):
    T = proj.shape[0]
    RC = _tile(T, ROWS_GROUP_A, 8)
    nb = CW // LANE
    K = conv_w.shape[0]

    def body(ax_ref, ab_ref, ac_ref, w_ref, y_ref):
        def step(i, carry):
            r0 = pl.multiple_of(i * RC, RC)
            m = _ext(ac_ref, r0, T, True, False, RC) * _ext(ax_ref, r0, T, True, False, RC)
            y_ref[pl.ds(r0, RC), :] = (ab_ref[pl.ds(r0, RC), :] * _conv_down(m, w_ref, K)).astype(y_ref.dtype)
            return carry
        lax.fori_loop(0, T // RC, step, 0)

    return pl.pallas_call(
        body, name=name, grid=(nb,),
        in_specs=_col_specs(T, (0, nb, 2 * nb)) + [pl.BlockSpec((K, LANE), lambda j: (0, j))],
        out_specs=pl.BlockSpec((T, LANE), lambda j: (0, j)),
        out_shape=jax.ShapeDtypeStruct((T, CW), BF16), compiler_params=_params(("parallel",)),
    )(proj, proj, proj, conv_w)


def _group_a_bwd(proj, conv_w, dycat, CW, *, name):
    T = proj.shape[0]
    RC = _tile(T, ROWS_GROUP_A, 8)
    nb = CW // LANE
    K = conv_w.shape[0]

    def body(ax_ref, ab_ref, ac_ref, w_ref, dy_ref, dax_ref, dab_ref, dac_ref, dw_ref):
        def step(i, accs):
            r0 = pl.multiple_of(i * RC, RC)
            ax3 = _ext(ax_ref, r0, T, True, True, RC)
            ac3 = _ext(ac_ref, r0, T, True, True, RC)
            m3 = ax3 * ac3
            c = _conv_down(m3[:RC + 8], w_ref, K)
            dy = dy_ref[pl.ds(r0, RC), :]
            dab_ref[pl.ds(r0, RC), :] = (dy * c).astype(dab_ref.dtype)
            dc2 = _ext(dy_ref, r0, T, False, True, RC) * _ext(ab_ref, r0, T, False, True, RC)
            dm = None
            new = []
            for j in range(K):
                s = K - 1 - j
                t = _up(dc2, s, RC) * w_ref[j:j + 1, :]
                dm = t if dm is None else dm + t
                new.append(accs[j] + _fold8(dc2[:RC] * _down(m3[:RC + 8], s)))
            dax_ref[pl.ds(r0, RC), :] = (dm * ac3[8:RC + 8]).astype(dax_ref.dtype)
            dac_ref[pl.ds(r0, RC), :] = (dm * ax3[8:RC + 8]).astype(dac_ref.dtype)
            return tuple(new)

        accs = lax.fori_loop(0, T // RC, step, tuple(jnp.zeros((8, LANE), F32) for _ in range(K)))
        for j in range(K):
            dw_ref[j:j + 1, :] = jnp.sum(accs[j], axis=0, keepdims=True)

    col = pl.BlockSpec((T, LANE), lambda j: (0, j))
    wsp = pl.BlockSpec((K, LANE), lambda j: (0, j))
    return pl.pallas_call(
        body, name=name, grid=(nb,),
        in_specs=_col_specs(T, (0, nb, 2 * nb)) + [wsp, col],
        out_specs=[col, col, col, wsp],
        out_shape=[jax.ShapeDtypeStruct((T, CW), BF16)] * 3 + [jax.ShapeDtypeStruct((K, CW), F32)],
        compiler_params=_params(("parallel",)),
    )(proj, proj, proj, conv_w, dycat)


def _qkv_fwd(proj, conv_w, off, H, *, name):
    T = proj.shape[0]
    RC = _tile(T, ROWS_QKV_FWD, 8)
    nb = 3 * H
    K = conv_w.shape[0]

    def body(x_ref, w_ref, y_ref):
        j = pl.program_id(0)
        is_qk = j < 2 * H
        scale = jnp.where(j < H, HEAD ** -0.5, 1.0).astype(F32)

        def step(i, carry):
            r0 = pl.multiple_of(i * RC, RC)
            s = _silu(_conv_down(_ext(x_ref, r0, T, True, False, RC), w_ref, K))
            r = lax.rsqrt(jnp.sum(s * s, axis=-1, keepdims=True) + EPS) * scale
            y_ref[pl.ds(r0, RC), :] = s * jnp.where(is_qk, r, 1.0)
            return carry
        lax.fori_loop(0, T // RC, step, 0)

    return pl.pallas_call(
        body, name=name, grid=(nb,),
        in_specs=_col_specs(T, (off,)) + [pl.BlockSpec((K, LANE), lambda j: (0, j))],
        out_specs=pl.BlockSpec((T, LANE), lambda j: (0, j)),
        out_shape=jax.ShapeDtypeStruct((T, nb * LANE), F32), compiler_params=_params(("parallel",)),
    )(proj, conv_w)


def _qkv_bwd(proj, conv_w, dq, dk, dv, off, H, *, name):
    T = proj.shape[0]
    RC = _tile(T, ROWS_QKV_BWD, 8)
    nb = 3 * H
    K = conv_w.shape[0]

    def body(x_ref, w_ref, dq_ref, dk_ref, dv_ref, dx_ref, dw_ref):
        j = pl.program_id(0)
        is_qk = j < 2 * H
        scale = jnp.where(j < H, HEAD ** -0.5, 1.0).astype(F32)

        def step(i, accs):
            r0 = pl.multiple_of(i * RC, RC)
            x3 = _ext(x_ref, r0, T, True, True, RC)
            c2 = _conv_down(x3, w_ref, K)
            s2 = _silu(c2)
            dn2 = jnp.where(j < H, _ext(dq_ref, r0, T, False, True, RC),
                            jnp.where(is_qk, _ext(dk_ref, r0, T, False, True, RC), _ext(dv_ref, r0, T, False, True, RC)))
            r = lax.rsqrt(jnp.sum(s2 * s2, axis=-1, keepdims=True) + EPS)
            nh = s2 * r
            dnp = dn2 * scale
            ds_qk = r * (dnp - nh * jnp.sum(dnp * nh, axis=-1, keepdims=True))
            ds2 = jnp.where(is_qk, ds_qk, dn2)
            dc2 = ds2 * _dsilu(c2)
            dx = None
            new = []
            for jj in range(K):
                s = K - 1 - jj
                t = _up(dc2, s, RC) * w_ref[jj:jj + 1, :]
                dx = t if dx is None else dx + t
                new.append(accs[jj] + _fold8(dc2[:RC] * _down(x3[:RC + 8], s)))
            dx_ref[pl.ds(r0, RC), :] = dx.astype(dx_ref.dtype)
            return tuple(new)

        accs = lax.fori_loop(0, T // RC, step, tuple(jnp.zeros((8, LANE), F32) for _ in range(K)))
        for jj in range(K):
            dw_ref[jj:jj + 1, :] = jnp.sum(accs[jj], axis=0, keepdims=True)

    col = pl.BlockSpec((T, LANE), lambda j: (0, j))
    wsp = pl.BlockSpec((K, LANE), lambda j: (0, j))
    return pl.pallas_call(
        body, name=name, grid=(nb,),
        in_specs=_col_specs(T, (off,)) + [wsp] + [
            pl.BlockSpec((T, LANE), functools.partial(lambda o, j: (0, jnp.clip(j - o, 0, H - 1)), o)) for o in (0, H, 2 * H)],
        out_specs=[col, wsp],
        out_shape=[jax.ShapeDtypeStruct((T, nb * LANE), BF16), jax.ShapeDtypeStruct((K, nb * LANE), F32)],
        compiler_params=_params(("parallel",)),
    )(proj, conv_w, dq, dk, dv)


def _softplus(x):
    return jnp.maximum(x, 0.0) + jnp.log(1.0 + jnp.exp(-jnp.abs(x)))


def _gates_fwd(proj, alog, dtb, off, H, *, name):
    T = proj.shape[0]
    tr = _tile(T, 512, CHUNK)

    def body(ab_ref, al_ref, dt_ref, gb_ref, gam_ref):
        ab = ab_ref[...]
        lane = lax.broadcasted_iota(jnp.int32, ab.shape, 1)
        g = -jnp.exp(al_ref[...]) * _softplus(ab + dt_ref[...])
        gb = jnp.where(lane < H, g, jnp.where(lane < 2 * H, jax.nn.sigmoid(ab), 0.0))
        gb_ref[...] = gb
        tril = _tri().astype(F32)
        for c in range(tr // CHUNK):
            rows = slice(c * CHUNK, (c + 1) * CHUNK)
            gam_ref[rows, :] = _mm(tril, gb[rows, :], precision=lax.Precision.HIGHEST)

    vec = pl.BlockSpec((1, LANE), lambda i: (0, 0))
    row = pl.BlockSpec((tr, LANE), lambda i: (i, 0))
    return pl.pallas_call(
        body, name=name, grid=(T // tr,),
        in_specs=[pl.BlockSpec((tr, LANE), lambda i: (i, off)), vec, vec],
        out_specs=[row, row],
        out_shape=[jax.ShapeDtypeStruct((T, LANE), F32)] * 2, compiler_params=_params(("parallel",)),
    )(proj, alog, dtb)


def _gates_bwd(proj, alog, dtb, dgb, off, H, *, name):
    T = proj.shape[0]
    tr = _tile(T, 512, CHUNK)

    def body(ab_ref, al_ref, dt_ref, d_ref, dab_ref, dal_ref, ddt_ref):
        ab, d = ab_ref[...], d_ref[...]
        lane = lax.broadcasted_iota(jnp.int32, ab.shape, 1)
        is_g = lane < H
        triu = _tri(upper=True).astype(F32)
        dg = jnp.concatenate([_mm(triu, d[c * CHUNK:(c + 1) * CHUNK, :], precision=lax.Precision.HIGHEST)
                              for c in range(tr // CHUNK)], axis=0)
        z = ab + dt_ref[...]
        A = -jnp.exp(al_ref[...])
        da = dg * A * jax.nn.sigmoid(z)
        beta = jax.nn.sigmoid(ab)
        db = d * beta * (1.0 - beta)
        dab_ref[...] = jnp.where(is_g, da, jnp.where(lane < 2 * H, db, 0.0)).astype(dab_ref.dtype)

        @pl.when(pl.program_id(0) == 0)
        def _():
            dal_ref[...] = jnp.zeros_like(dal_ref)
            ddt_ref[...] = jnp.zeros_like(ddt_ref)

        dal_ref[...] += jnp.sum(jnp.where(is_g, dg * A * _softplus(z), 0.0), axis=0, keepdims=True)
        ddt_ref[...] += jnp.sum(jnp.where(is_g, da, 0.0), axis=0, keepdims=True)

    vec = pl.BlockSpec((1, LANE), lambda i: (0, 0))
    row = pl.BlockSpec((tr, LANE), lambda i: (i, 0))
    return pl.pallas_call(
        body, name=name, grid=(T // tr,),
        in_specs=[pl.BlockSpec((tr, LANE), lambda i: (i, off)), vec, vec, row],
        out_specs=[row, vec, vec],
        out_shape=[jax.ShapeDtypeStruct((T, LANE), BF16), jax.ShapeDtypeStruct((1, LANE), F32),
                   jax.ShapeDtypeStruct((1, LANE), F32)],
        compiler_params=_params(("arbitrary",)),
    )(proj, alog, dtb, dgb)


def _gated_norm_fwd(o, proj, gn, zoff, *, name):
    T, W = o.shape
    tr = _tile(T, 512, 8)

    def body(o_ref, z_ref, g_ref, y_ref):
        ov = o_ref[...]
        r = lax.rsqrt(jnp.mean(ov * ov, axis=-1, keepdims=True) + EPS)
        y_ref[...] = (ov * r * g_ref[...] * _silu(z_ref[...])).astype(y_ref.dtype)

    blk = pl.BlockSpec((tr, LANE), lambda i, j: (i, j))
    return pl.pallas_call(
        body, name=name, grid=(T // tr, W // LANE),
        in_specs=[blk, pl.BlockSpec((tr, LANE), lambda i, j: (i, zoff + j)), pl.BlockSpec((1, LANE), lambda i, j: (0, 0))],
        out_specs=blk, out_shape=jax.ShapeDtypeStruct((T, W), BF16), compiler_params=_params(("parallel", "parallel")),
    )(o, proj, gn)


def _gated_norm_bwd(o, proj, gn, dycat, zoff, yoff, *, name):
    T, W = o.shape
    tr = _tile(T, 512, 8)

    def body(o_ref, z_ref, g_ref, dy_ref, do_ref, dz_ref, dg_ref):
        ov, zv, gv, dy = o_ref[...], z_ref[...], g_ref[...], dy_ref[...]
        r = lax.rsqrt(jnp.mean(ov * ov, axis=-1, keepdims=True) + EPS)
        nh = ov * r
        s = _silu(zv)

        @pl.when((pl.program_id(0) == 0) & (pl.program_id(1) == 0))
        def _():
            dg_ref[...] = jnp.zeros_like(dg_ref)

        dg_ref[...] += jnp.sum(dy * nh * s, axis=0, keepdims=True)
        dz_ref[...] = (dy * nh * gv * _dsilu(zv)).astype(dz_ref.dtype)
        dn = dy * gv * s
        do_ref[...] = r * (dn - nh * jnp.mean(dn * nh, axis=-1, keepdims=True))

    blk = pl.BlockSpec((tr, LANE), lambda i, j: (i, j))
    vec = pl.BlockSpec((1, LANE), lambda i, j: (0, 0))
    return pl.pallas_call(
        body, name=name, grid=(T // tr, W // LANE),
        in_specs=[blk, pl.BlockSpec((tr, LANE), lambda i, j: (i, zoff + j)), vec,
                  pl.BlockSpec((tr, LANE), lambda i, j: (i, yoff + j))],
        out_specs=[blk, blk, vec],
        out_shape=[jax.ShapeDtypeStruct((T, W), F32), jax.ShapeDtypeStruct((T, W), BF16),
                   jax.ShapeDtypeStruct((1, LANE), F32)],
        compiler_params=_params(("arbitrary", "arbitrary")),
    )(o, proj, gn, dycat)


def _ffn_act_fwd(up_pre, conv_w, *, name):
    T, F2 = up_pre.shape
    RC = _tile(T, ROWS_FFN_FWD, 8)
    nb = F2 // 2 // LANE
    K = conv_w.shape[0]

    def body(g_ref, v_ref, wg_ref, wv_ref, y_ref):
        def step(i, carry):
            r0 = pl.multiple_of(i * RC, RC)
            gate = _conv_down(_ext(g_ref, r0, T, True, False, RC), wg_ref, K)
            val = _conv_down(_ext(v_ref, r0, T, True, False, RC), wv_ref, K)
            y_ref[pl.ds(r0, RC), :] = (_silu(gate) * val).astype(y_ref.dtype)
            return carry
        lax.fori_loop(0, T // RC, step, 0)

    return pl.pallas_call(
        body, name=name, grid=(nb,),
        in_specs=_col_specs(T, (0, nb)) + [pl.BlockSpec((K, LANE), lambda j: (0, j)),
                                           pl.BlockSpec((K, LANE), lambda j: (0, nb + j))],
        out_specs=pl.BlockSpec((T, LANE), lambda j: (0, j)),
        out_shape=jax.ShapeDtypeStruct((T, F2 // 2), BF16), compiler_params=_params(("parallel",)),
    )(up_pre, up_pre, conv_w, conv_w)


def _ffn_act_bwd(up_pre, conv_w, dact, *, name):
    T, F2 = up_pre.shape
    RC = _tile(T, ROWS_FFN_BWD, 8)
    nb = F2 // 2 // LANE
    K = conv_w.shape[0]

    def body(g_ref, v_ref, wg_ref, wv_ref, da_ref, dg_ref, dv_ref, dwg_ref, dwv_ref):
        def step(i, accs):
            r0 = pl.multiple_of(i * RC, RC)
            g3 = _ext(g_ref, r0, T, True, True, RC)
            v3 = _ext(v_ref, r0, T, True, True, RC)
            gate2 = _conv_down(g3, wg_ref, K)
            val2 = _conv_down(v3, wv_ref, K)
            da2 = _ext(da_ref, r0, T, False, True, RC)
            dgate2 = da2 * val2 * _dsilu(gate2)
            dval2 = da2 * _silu(gate2)
            dgp, dvp, new = None, None, []
            for j in range(K):
                s = K - 1 - j
                tg = _up(dgate2, s, RC) * wg_ref[j:j + 1, :]
                tv = _up(dval2, s, RC) * wv_ref[j:j + 1, :]
                dgp = tg if dgp is None else dgp + tg
                dvp = tv if dvp is None else dvp + tv
                new.append(accs[2 * j] + _fold8(dgate2[:RC] * _down(g3[:RC + 8], s)))
                new.append(accs[2 * j + 1] + _fold8(dval2[:RC] * _down(v3[:RC + 8], s)))
            dg_ref[pl.ds(r0, RC), :] = dgp.astype(dg_ref.dtype)
            dv_ref[pl.ds(r0, RC), :] = dvp.astype(dv_ref.dtype)
            return tuple(new)

        accs = lax.fori_loop(0, T // RC, step, tuple(jnp.zeros((8, LANE), F32) for _ in range(2 * K)))
        for j in range(K):
            dwg_ref[j:j + 1, :] = jnp.sum(accs[2 * j], axis=0, keepdims=True)
            dwv_ref[j:j + 1, :] = jnp.sum(accs[2 * j + 1], axis=0, keepdims=True)

    col = pl.BlockSpec((T, LANE), lambda j: (0, j))
    wsp = pl.BlockSpec((K, LANE), lambda j: (0, j))
    return pl.pallas_call(
        body, name=name, grid=(nb,),
        in_specs=_col_specs(T, (0, nb)) + [wsp, pl.BlockSpec((K, LANE), lambda j: (0, nb + j)), col],
        out_specs=[col, col, wsp, wsp],
        out_shape=[jax.ShapeDtypeStruct((T, F2 // 2), BF16)] * 2 + [jax.ShapeDtypeStruct((K, F2 // 2), F32)] * 2,
        compiler_params=_params(("parallel",)),
    )(up_pre, up_pre, conv_w, conv_w, dact)


CPB = 8
CPB_SCAN = 4
GRP = 8
HP = lax.Precision.HIGH


def _tri(strict=False, upper=False):
    r = lax.broadcasted_iota(jnp.int32, (CHUNK, CHUNK), 0)
    c = lax.broadcasted_iota(jnp.int32, (CHUNK, CHUNK), 1)
    if upper:
        return c >= r
    return (r > c) if strict else (r >= c)


def _mm(a, b, dn="nn", precision=None):
    precision = HP if precision is None else precision
    return lax.dot_general(a, b, _DN[dn], precision=precision, preferred_element_type=F32)


def _mm16(a, b, dn="nn"):
    return lax.dot_general(a.astype(BF16), b.astype(BF16), _DN[dn], preferred_element_type=F32)


def _each(f, *cols):
    return [f(*xs) for xs in zip(*cols)]


def _decay(gam):
    return jnp.exp(jnp.where(_tri(), gam[:, :CHUNK] - gam.T[:CHUNK, :], -1e30))


def _delta_specs(T, H, cpb):
    rows = cpb * CHUNK
    col = lambda o: pl.BlockSpec((rows, LANE), functools.partial(lambda o, h, n: (n, o + h), o))
    bc = pl.BlockSpec((1, rows, LANE), lambda h, n: (h, n, 0))
    sq = pl.BlockSpec((1, cpb, CHUNK, CHUNK), lambda h, n: (h, n, 0, 0))
    vec = pl.BlockSpec((1, cpb, 1, LANE), lambda h, n: (h, n, 0, 0))
    return col, bc, sq, vec


def _delta_prep_fwd(qkv, gamB, bB, H, *, name):
    T = qkv.shape[0]
    N = T // CHUNK
    cpb = _tile(N, CPB, 8)
    grp = min(GRP, cpb)
    col, bc, sq, vec = _delta_specs(T, H, cpb)

    def body(q_ref, k_ref, v_ref, g_ref, b_ref, u_ref, w_ref, qd_ref, kd_ref, qk_ref, ti_ref, gl_ref):
        eye = (lax.broadcasted_iota(jnp.int32, (CHUNK, CHUNK), 0) == lax.broadcasted_iota(jnp.int32, (CHUNK, CHUNK), 1)).astype(F32)
        strict = _tri(strict=True)
        for c0 in range(0, cpb, grp):
            cs = list(range(c0, c0 + grp))
            rows = [slice(c * CHUNK, (c + 1) * CHUNK) for c in cs]
            q, k, v = ([r_[r, :] for r in rows] for r_ in (q_ref, k_ref, v_ref))
            bb = [b_ref[0, r, :] for r in rows]
            gam = [g_ref[0, r, :] for r in rows]
            D = _each(_decay, gam)
            e = _each(jnp.exp, gam)
            kk = _each(lambda k_: _mm16(k_, k_, "nt"), k)
            X = _each(lambda kk_, D_, b_: -(jnp.where(strict, kk_ * D_, 0.0) * b_[:, :CHUNK]), kk, D, bb)
            R = _each(lambda x: eye + x, X)
            for _ in range(5):
                X = _each(lambda x: _mm(x, x), X)
                R = _each(lambda r, x: r + _mm(r, x), R, X)
            u = _each(lambda r, b_, v_: _mm(r, b_ * v_), R, bb, v)
            w = _each(lambda r, b_, e_, k_: _mm(r, b_ * e_ * k_), R, bb, e, k)
            qk = _each(lambda q_, k_, D_: _mm16(q_, k_, "nt") * D_, q, k, D)
            for i, c in enumerate(cs):
                glast = gam[i][CHUNK - 1:CHUNK, :]
                u_ref[rows[i], :] = u[i]
                w_ref[rows[i], :] = w[i]
                qd_ref[rows[i], :] = e[i] * q[i]
                kd_ref[rows[i], :] = jnp.exp(glast - gam[i]) * k[i]
                qk_ref[0, c] = qk[i]
                ti_ref[0, c] = R[i]
                gl_ref[0, c] = jnp.exp(glast)

    full = jax.ShapeDtypeStruct((T, H * LANE), F32)
    sqs = jax.ShapeDtypeStruct((H, N, CHUNK, CHUNK), F32)
    return pl.pallas_call(
        body, name=name, grid=(H, N // cpb),
        in_specs=[col(0), col(H), col(2 * H), bc, bc],
        out_specs=[col(0)] * 4 + [sq, sq, vec],
        out_shape=[full] * 4 + [sqs, sqs, jax.ShapeDtypeStruct((H, N, 1, LANE), F32)],
        compiler_params=_params(("parallel", "parallel")),
    )(qkv, qkv, qkv, gamB, bB)


def _scan_specs(H, N, cpb, hb, rev):
    nbk = N // cpb
    blk = (lambda n: nbk - 1 - n) if rev else (lambda n: n)
    col = pl.BlockSpec((cpb * CHUNK, hb * LANE), lambda h, n: (blk(n), h))
    sq = pl.BlockSpec((hb, cpb, CHUNK, CHUNK), lambda h, n: (h, blk(n), 0, 0))
    vec = pl.BlockSpec((hb, cpb, 1, LANE), lambda h, n: (h, blk(n), 0, 0))
    st = pl.BlockSpec((hb, cpb, HEAD, HEAD), lambda h, n: (h, blk(n), 0, 0))
    return col, sq, vec, st


def _delta_scan_fwd(u, w, qd, kd, qk, gl, H, *, name):
    T = u.shape[0]
    N = T // CHUNK
    cpb = _tile(N, CPB_SCAN, 4)
    hb = min(GRP, H)
    col, sq, vec, st = _scan_specs(H, N, cpb, hb, False)
    lanes = [slice(j * LANE, (j + 1) * LANE) for j in range(hb)]
    heads = list(range(hb))

    def body(u_ref, w_ref, qd_ref, kd_ref, qk_ref, gl_ref, o_ref, vn_ref, ss_ref, s_scr):
        @pl.when(pl.program_id(1) == 0)
        def _():
            s_scr[...] = jnp.zeros_like(s_scr)

        def step(c, states):
            rows = pl.ds(pl.multiple_of(c * CHUNK, CHUNK), CHUNK)
            S = list(states)
            for j in heads:
                ss_ref[j, c] = S[j]
            wS = _each(lambda ln, s: _mm16(w_ref[rows, ln], s), lanes, S)
            qS = _each(lambda ln, s: _mm16(qd_ref[rows, ln], s), lanes, S)
            vn = _each(lambda ln, ws: u_ref[rows, ln] - ws, lanes, wS)
            o = _each(lambda j, qs, vn_: qs + _mm16(qk_ref[j, c], vn_), heads, qS, vn)
            new = _each(lambda j, ln, s, vn_: s * gl_ref[j, c] + _mm16(kd_ref[rows, ln], vn_, "tn"),
                        heads, lanes, S, vn)
            for j in heads:
                o_ref[rows, lanes[j]] = o[j]
                vn_ref[rows, lanes[j]] = vn[j]
            return tuple(new)
        out = lax.fori_loop(0, cpb, step, tuple(s_scr[j] for j in heads))
        for j in heads:
            s_scr[j] = out[j]

    full = jax.ShapeDtypeStruct((T, H * LANE), F32)
    return pl.pallas_call(
        body, name=name, grid=(H // hb, N // cpb),
        in_specs=[col] * 4 + [sq, vec],
        out_specs=[col, col, st],
        out_shape=[full, full, jax.ShapeDtypeStruct((H, N, HEAD, HEAD), F32)],
        scratch_shapes=[pltpu.VMEM((hb, HEAD, HEAD), F32)],
        compiler_params=_params(("parallel", "arbitrary")),
    )(u, w, qd, kd, qk, gl)


def _delta_scan_bwd(do, w, qd, kd, vn, qk, gl, ss, H, *, name):
    T = do.shape[0]
    N = T // CHUNK
    cpb = _tile(N, CPB_SCAN, 4)
    hb = min(GRP, H)
    col, sq, vec, st = _scan_specs(H, N, cpb, hb, True)
    lanes = [slice(j * LANE, (j + 1) * LANE) for j in range(hb)]
    heads = list(range(hb))

    def body(do_ref, w_ref, qd_ref, kd_ref, vn_ref, qk_ref, gl_ref, ss_ref,
             du_ref, dw_ref, dqd_ref, dkd_ref, dqk_ref, dgl_ref, ds_scr):
        @pl.when(pl.program_id(1) == 0)
        def _():
            ds_scr[...] = jnp.zeros_like(ds_scr)

        def step(i, dstates):
            c = cpb - 1 - i
            rows = pl.ds(pl.multiple_of(c * CHUNK, CHUNK), CHUNK)
            dS = list(dstates)
            S = [ss_ref[j, c] for j in heads]
            dov = [do_ref[rows, ln] for ln in lanes]
            vnv = [vn_ref[rows, ln] for ln in lanes]
            a1 = _each(lambda j, d_: _mm16(qk_ref[j, c], d_, "tn"), heads, dov)
            a2 = _each(lambda ln, ds: _mm16(kd_ref[rows, ln], ds), lanes, dS)
            dvn = _each(lambda x, y: x + y, a1, a2)
            dqd = _each(lambda d_, s: _mm16(d_, s, "nt"), dov, S)
            dkd = _each(lambda v_, ds: _mm16(v_, ds, "nt"), vnv, dS)
            dqk = _each(lambda d_, v_: _mm16(d_, v_, "nt"), dov, vnv)
            dw = _each(lambda dv_, s: -_mm16(dv_, s, "nt"), dvn, S)
            b1 = _each(lambda ln, d_: _mm16(qd_ref[rows, ln], d_, "tn"), lanes, dov)
            b2 = _each(lambda ln, dv_: _mm16(w_ref[rows, ln], dv_, "tn"), lanes, dvn)
            new = _each(lambda j, x, y, ds: x + ds * gl_ref[j, c] - y, heads, b1, b2, dS)
            for j in heads:
                du_ref[rows, lanes[j]] = dvn[j]
                dw_ref[rows, lanes[j]] = dw[j]
                dqd_ref[rows, lanes[j]] = dqd[j]
                dkd_ref[rows, lanes[j]] = dkd[j]
                dqk_ref[j, c] = dqk[j]
                dgl = jnp.sum(jnp.sum(dS[j] * S[j], axis=1, keepdims=True), axis=0, keepdims=True)
                dgl_ref[j, c] = jnp.broadcast_to(dgl, (1, LANE))
            return tuple(new)
        out = lax.fori_loop(0, cpb, step, tuple(ds_scr[j] for j in heads))
        for j in heads:
            ds_scr[j] = out[j]

    full = jax.ShapeDtypeStruct((T, H * LANE), F32)
    return pl.pallas_call(
        body, name=name, grid=(H // hb, N // cpb),
        in_specs=[col] * 5 + [sq, vec, st],
        out_specs=[col] * 4 + [sq, vec],
        out_shape=[full] * 4 + [jax.ShapeDtypeStruct((H, N, CHUNK, CHUNK), F32), jax.ShapeDtypeStruct((H, N, 1, LANE), F32)],
        scratch_shapes=[pltpu.VMEM((hb, HEAD, HEAD), F32)],
        compiler_params=_params(("parallel", "arbitrary")),
    )(do, w, qd, kd, vn, qk, gl, ss)


def _delta_prep_bwd(qkv, gamB, bB, ti, u, w, qk, du, dw, dqd, dkd, dqk, dgl, H, *, name):
    T = qkv.shape[0]
    N = T // CHUNK
    cpb = _tile(N, CPB, 8)
    grp = min(GRP, cpb)
    col, bc, sq, vec = _delta_specs(T, H, cpb)

    def body(q_ref, k_ref, v_ref, g_ref, b_ref, ti_ref, u_ref, w_ref, qk_ref,
             du_ref, dw_ref, dqd_ref, dkd_ref, dqk_ref, dgl_ref,
             dq_ref, dk_ref, dv_ref, dg_ref, db_ref):
        ones = jnp.ones((CHUNK, LANE), F32)
        strict = _tri(strict=True)
        last = lax.broadcasted_iota(jnp.int32, (CHUNK, LANE), 0) == CHUNK - 1
        lsum = lambda x: jnp.sum(x, axis=-1, keepdims=True)
        for c0 in range(0, cpb, grp):
            cs = list(range(c0, c0 + grp))
            rows = [slice(c * CHUNK, (c + 1) * CHUNK) for c in cs]
            ld = lambda r_: [r_[r, :] for r in rows]
            q, k, v, uv, wv, duv, dwv, dqd_v, dkd_v = (ld(r_) for r_ in (q_ref, k_ref, v_ref, u_ref, w_ref, du_ref, dw_ref, dqd_ref, dkd_ref))
            bb = [b_ref[0, r, :] for r in rows]
            gam = [g_ref[0, r, :] for r in rows]
            Ti = [ti_ref[0, c] for c in cs]
            QK = [qk_ref[0, c] for c in cs]
            dqk_v = [dqk_ref[0, c] for c in cs]
            D = _each(_decay, gam)
            e = _each(jnp.exp, gam)
            glast = [g_[CHUNK - 1:CHUNK, :] for g_ in gam]
            eL = _each(lambda gl_, g_: jnp.exp(gl_ - g_), glast, gam)
            kk = _each(lambda k_: _mm16(k_, k_, "nt"), k)
            KKD = _each(lambda kk_, D_: jnp.where(strict, kk_ * D_, 0.0), kk, D)
            dru = _each(lambda t, d_: _mm(t, d_, "tn"), Ti, duv)
            drw = _each(lambda t, d_: _mm(t, d_, "tn"), Ti, dwv)
            l1 = _each(lambda a, b: _mm(a, b, "nt"), dru, uv)
            l2 = _each(lambda a, b: _mm(a, b, "nt"), drw, wv)
            dL = _each(lambda a, b: jnp.where(strict, -(a + b), 0.0), l1, l2)
            Mm = _each(lambda dl, b_: dl * b_[:, :CHUNK], dL, bb)
            dKK = _each(lambda m_, D_: m_ * D_, Mm, D)
            dQK = _each(lambda a, D_: a * D_, dqk_v, D)
            P = _each(lambda m_, kkd, a, qk_: m_ * kkd + a * qk_, Mm, KKD, dqk_v, QK)
            q1 = _each(lambda a, k_: _mm16(a, k_), dQK, k)
            k1 = _each(lambda a, q_: _mm16(a, q_, "tn"), dQK, q)
            k2 = _each(lambda a, k_: _mm16(a, k_), dKK, k)
            k3 = _each(lambda a, k_: _mm16(a, k_, "tn"), dKK, k)
            s1 = _each(lambda dl, kkd: _mm(dl * kkd, ones), dL, KKD)
            p1 = _each(lambda p_: _mm(p_, ones), P)
            p2 = _each(lambda p_: _mm(p_, ones, "tn"), P)
            for i, c in enumerate(cs):
                r = rows[i]
                bek = bb[i] * e[i]
                kdv = eL[i] * k[i]
                dq_ref[r, :] = q1[i] + e[i] * dqd_v[i]
                dk_ref[r, :] = k1[i] + k2[i] + k3[i] + bek * drw[i] + eL[i] * dkd_v[i]
                dv_ref[r, :] = bb[i] * dru[i]
                db_ref[0, r, :] = s1[i] + lsum(dru[i] * v[i]) + lsum(drw[i] * e[i] * k[i])
                dgam = (p1[i] - p2[i] + lsum(drw[i] * bek * k[i]) + lsum(dqd_v[i] * e[i] * q[i])
                        - lsum(dkd_v[i] * kdv))
                xlast = jnp.sum(lsum(dkd_v[i] * kdv), axis=0, keepdims=True) + jnp.exp(glast[i]) * dgl_ref[0, c]
                dg_ref[0, r, :] = dgam + jnp.where(last, xlast, 0.0)

    full = jax.ShapeDtypeStruct((T, H * LANE), F32)
    bcs = jax.ShapeDtypeStruct((H, T, LANE), F32)
    return pl.pallas_call(
        body, name=name, grid=(H, N // cpb),
        in_specs=[col(0), col(H), col(2 * H), bc, bc, sq, col(0), col(0), sq, col(0), col(0), col(0), col(0), sq, vec],
        out_specs=[col(0), col(0), col(0), bc, bc],
        out_shape=[full, full, full, bcs, bcs],
        compiler_params=_params(("parallel", "parallel")),
    )(qkv, qkv, qkv, gamB, bB, ti, u, w, qk, du, dw, dqd, dkd, dqk, dgl)


def _adam(parts, w, m, v, *, name, own=None, me=None):
    P, R, C = parts.shape
    if R > 256 and R % 8:
        tr, tc = R, _tile(C, 256)
    else:
        tr, tc = _tile(R, 256, 8), C
    n_own = 0 if own is None else 2

    def body(*refs):
        p_ref, w_ref, m_ref, v_ref, g_ref, d_ref, nm_ref, nv_ref = refs[n_own:]
        g = None
        for i in range(P):
            t = p_ref[i].astype(F32)
            if n_own:
                t = jnp.where(refs[0][0] == i, refs[1][...].astype(F32), t)
            g = t if g is None else g + t
        mn = ADAM_B1 * m_ref[...] + (1.0 - ADAM_B1) * g
        vn = ADAM_B2 * v_ref[...] + (1.0 - ADAM_B2) * (g * g)
        m_hat = mn / (1.0 - ADAM_B1 ** ADAM_STEP)
        v_hat = vn / (1.0 - ADAM_B2 ** ADAM_STEP)
        g_ref[...] = g
        d_ref[...] = -ADAM_LR * (m_hat / (jnp.sqrt(v_hat) + ADAM_EPS) + ADAM_WD * w_ref[...])
        nm_ref[...] = mn
        nv_ref[...] = vn

    blk = pl.BlockSpec((tr, tc), lambda i, j: (i, j))
    return pl.pallas_call(
        body, name=name, grid=(R // tr, C // tc),
        in_specs=[pl.BlockSpec(memory_space=pltpu.SMEM), blk][:n_own] + [pl.BlockSpec((P, tr, tc), lambda i, j: (0, i, j)), blk, blk, blk],
        out_specs=[blk] * 4, out_shape=[jax.ShapeDtypeStruct((R, C), F32)] * 4,
        compiler_params=_params(("parallel", "parallel")),
    )(*([me, own] if n_own else []), parts, w, m, v)


def _mesh_pos():
    return lax.axis_index("x"), lax.axis_index("y"), lax.axis_index("c")


def _peer(k):
    x, y, c = _mesh_pos()
    px, py, pc = x ^ ((k >> 2) & 1), y ^ ((k >> 1) & 1), c ^ (k & 1)
    return (px, py, pc), 4 * px + 2 * py + pc


def _exchange(arrays, scatter, *, name, after=None):
    n = len(arrays)
    n_in = n if after is None else n + 1
    blocks = [a.shape[1:] if scatter else a.shape for a in arrays]

    def body(*refs):
        srcs, dsts = refs[:n], refs[n_in:n_in + n]
        send_sems, recv_sems, local_sems = refs[n_in + n:]
        x, y, c = _mesh_pos()
        me = 4 * x + 2 * y + c
        local, sends = [], []
        for a in range(n):
            cp = pltpu.make_async_copy(srcs[a].at[me] if scatter else srcs[a], dsts[a].at[me], local_sems.at[a])
            cp.start()
            local.append(cp)
            for k in range(1, N_DEV):
                dev, idx = _peer(k)
                cp = pltpu.make_async_remote_copy(
                    src_ref=srcs[a].at[idx] if scatter else srcs[a], dst_ref=dsts[a].at[me],
                    send_sem=send_sems.at[a * N_DEV + k], recv_sem=recv_sems.at[a * N_DEV + k],
                    device_id=dev, device_id_type=MESH)
                cp.start()
                sends.append(cp)
        for a in range(n):
            for k in range(1, N_DEV):
                dev, idx = _peer(k)
                pltpu.make_async_remote_copy(
                    src_ref=srcs[a].at[idx] if scatter else srcs[a], dst_ref=dsts[a].at[idx],
                    send_sem=send_sems.at[a * N_DEV + k], recv_sem=recv_sems.at[a * N_DEV + k],
                    device_id=dev, device_id_type=MESH).wait_recv()
        for cp in sends:
            cp.wait_send()
        for cp in local:
            cp.wait()

    anyspec = pl.BlockSpec(memory_space=pl.ANY)
    return pl.pallas_call(
        body, name=name, in_specs=[anyspec] * n_in, out_specs=[anyspec] * n,
        out_shape=[jax.ShapeDtypeStruct((N_DEV,) + tuple(b), a.dtype) for a, b in zip(arrays, blocks)],
        scratch_shapes=[pltpu.SemaphoreType.DMA((n * N_DEV,)), pltpu.SemaphoreType.DMA((n * N_DEV,)),
                        pltpu.SemaphoreType.DMA((n,))],
    )(*arrays, *([] if after is None else [after]))


_ANY = pl.BlockSpec(memory_space=pl.ANY)
_SEM = pl.BlockSpec(memory_space=pltpu.SEMAPHORE)
_EFFECT = pltpu.SideEffectType.DATAFLOW_SIDE_EFFECTING


def _in_hbm(a):
    return pltpu.with_memory_space_constraint(a, pltpu.HBM)


def _split_copy(src, land, send, recv, k, me, scatter, landed):
    dev, idx = _peer(k)
    return pltpu.make_async_remote_copy(
        src_ref=src.at[idx] if scatter else src, dst_ref=land.at[idx if landed else me],
        send_sem=send.at[k], recv_sem=recv.at[k], device_id=dev, device_id_type=MESH)


ALL_PEERS = tuple(range(1, N_DEV))
SIBLING = 1
SAME_CORE = (2, 4, 6)


def _split_start(srcs, lands, scatter, *, name, relations=None):
    n = len(srcs)
    relations = relations or [ALL_PEERS] * n

    def body(*refs):
        src, land, send, recv, token = refs[:n], refs[n:2 * n], refs[2 * n:3 * n], refs[3 * n:4 * n], refs[-1]
        x, y, c = _mesh_pos()
        me = 4 * x + 2 * y + c
        for a in range(n):
            for k in relations[a]:
                _split_copy(src[a], land[a], send[a], recv[a], k, me, scatter, False).start()
        token[...] = jnp.zeros_like(token)

    outs = pl.pallas_call(
        body, name=name,
        out_shape=[pltpu.SemaphoreType.DMA((N_DEV,))] * (2 * n) + [pltpu.HBM(t.shape, t.dtype) for t in list(srcs) + list(lands)]
        + [jax.ShapeDtypeStruct((8, LANE), F32)],
        in_specs=[_ANY] * (2 * n), out_specs=[_SEM] * (2 * n) + [_ANY] * (2 * n) + [pl.BlockSpec(memory_space=pltpu.VMEM)],
        input_output_aliases={i: 2 * n + i for i in range(2 * n)},
        compiler_params=pltpu.CompilerParams(has_side_effects=_EFFECT),
    )(*[_in_hbm(t) for t in list(srcs) + list(lands)])
    handles = [(outs[a], outs[n + a], outs[2 * n + a], outs[3 * n + a]) for a in range(n)]
    return handles, outs[-1]


def _split_wait(handle, after, scatter, *, name):
    send, recv, src_thru, land_thru = handle

    def body(src_ref, land_ref, send_ref, recv_ref, after_ref, src_out, land_out):
        x, y, c = _mesh_pos()
        me = 4 * x + 2 * y + c
        for k in range(1, N_DEV):
            cp = _split_copy(src_ref, land_ref, send_ref, recv_ref, k, me, scatter, True)
            cp.wait_send()
            cp.wait_recv()

    return pl.pallas_call(
        body, name=name,
        out_shape=(pltpu.HBM(src_thru.shape, src_thru.dtype), pltpu.HBM(land_thru.shape, land_thru.dtype)),
        in_specs=(_ANY, _ANY, _SEM, _SEM, _ANY), out_specs=(_ANY, _ANY), input_output_aliases={0: 0, 1: 1},
        compiler_params=pltpu.CompilerParams(has_side_effects=_EFFECT),
    )(src_thru, land_thru, send, recv, after)[1]


def _forward_copy(land, fsend, frecv, k, landed):
    x, y, c = _mesh_pos()
    _, idx = _peer(k | SIBLING if landed else k)
    return pltpu.make_async_remote_copy(src_ref=land.at[idx], dst_ref=land.at[idx], send_sem=fsend.at[k],
                                        recv_sem=frecv.at[k], device_id=(x, y, 1 - c), device_id_type=MESH)


def _gather_forward(handle, after, *, name):
    send, recv, src_thru, land_thru = handle

    def body(src_ref, land_ref, send_ref, recv_ref, after_ref, src_out, land_out, fsend, frecv):
        x, y, c = _mesh_pos()
        me = 4 * x + 2 * y + c
        for k in SAME_CORE:
            _split_copy(src_ref, land_ref, send_ref, recv_ref, k, me, False, True).wait_recv()
            _forward_copy(land_ref, fsend, frecv, k, False).start()

    src2, land2, fsend, frecv = pl.pallas_call(
        body, name=name,
        out_shape=(pltpu.HBM(src_thru.shape, src_thru.dtype), pltpu.HBM(land_thru.shape, land_thru.dtype),
                   pltpu.SemaphoreType.DMA((N_DEV,)), pltpu.SemaphoreType.DMA((N_DEV,))),
        in_specs=(_ANY, _ANY, _SEM, _SEM, _ANY), out_specs=(_ANY, _ANY, _SEM, _SEM), input_output_aliases={0: 0, 1: 1},
        compiler_params=pltpu.CompilerParams(has_side_effects=_EFFECT),
    )(src_thru, land_thru, send, recv, after)
    return (send, recv, src2, land2), (fsend, frecv)


def _gather_wait_two_level(handle, fwd, *, name):
    send, recv, src_thru, land_thru = handle
    fsend, frecv = fwd

    def body(src_ref, land_ref, send_ref, recv_ref, fsend_ref, frecv_ref, src_out, land_out):
        x, y, c = _mesh_pos()
        me = 4 * x + 2 * y + c
        for k in (SIBLING,) + SAME_CORE:
            _split_copy(src_ref, land_ref, send_ref, recv_ref, k, me, False, True).wait_send()
        _split_copy(src_ref, land_ref, send_ref, recv_ref, SIBLING, me, False, True).wait_recv()
        for k in SAME_CORE:
            _forward_copy(land_ref, fsend_ref, frecv_ref, k, False).wait_send()
            _forward_copy(land_ref, fsend_ref, frecv_ref, k, True).wait_recv()

    return pl.pallas_call(
        body, name=name,
        out_shape=(pltpu.HBM(src_thru.shape, src_thru.dtype), pltpu.HBM(land_thru.shape, land_thru.dtype)),
        in_specs=(_ANY, _ANY, _SEM, _SEM, _SEM, _SEM), out_specs=(_ANY, _ANY), input_output_aliases={0: 0, 1: 1},
        compiler_params=pltpu.CompilerParams(has_side_effects=_EFFECT),
    )(src_thru, land_thru, send, recv, fsend, frecv)[1]


def _local_step(x, p, tgt, S, wt, conv, emit):
    T, D = x.shape
    CW = DNW = D // 2
    H = DNW // HEAD
    nA, nD = CW // LANE, DNW // LANE
    qkv_off, z_off, ab_off = 3 * nA, 3 * nA + 3 * nD, 3 * nA + 4 * nD
    alog = jnp.pad(S["a_log"], ((0, 0), (0, LANE - H)))
    dtb = jnp.pad(S["dt_bias"], ((0, 0), (0, LANE - H)))

    h1 = _rms_fwd(x, S["g_mix"], name="rms1_fwd")
    w_in, cv = wt("w_in", h1), conv(h1)
    proj = _matmul(h1, w_in, "nt", name="mm_in")
    y_a = _group_a_fwd(proj, cv["conv_a"], CW, name="group_a_fwd")
    qkv = _qkv_fwd(proj, cv["conv_qkv"], qkv_off, H, name="qkv_fwd")
    gb, gamc = _gates_fwd(proj, alog, dtb, ab_off, H, name="gates_fwd")
    bcast = lambda cols: jnp.broadcast_to(cols.T[:, :, None], (H, T, LANE))
    gamB, bB = bcast(gamc[:, :H]), bcast(gb[:, H:2 * H])
    u, w, qd, kd, qk, ti, gl = _delta_prep_fwd(qkv, gamB, bB, H, name="delta_prep_fwd")
    o, vn, ss = _delta_scan_fwd(u, w, qd, kd, qk, gl, H, name="delta_scan_fwd")
    y_b = _gated_norm_fwd(o, proj, S["dn_g"], z_off, name="gated_norm_fwd")
    ycat = jnp.concatenate([y_a, y_b], axis=1)
    w_out = wt("w_out", ycat)
    rows = dict(tm=ROW_TILE, tn=D)
    x1, h2 = _matmul(ycat, w_out, "nn", name="mm_out", out_dtypes=(F32, BF16), epilogue=_epi_residual_rms,
                     extras=(x,), vec_extras=(S["g_ffn"],), **rows)
    w_up = wt("w_up", h2)
    up_pre = _matmul(h2, w_up, "nn", name="mm_up", b_shards=True, tn=SHARD_TILE)
    act = _ffn_act_fwd(up_pre, cv["conv_ffn"], name="ffn_act_fwd")
    w_down = wt("w_down", act)
    x2 = _matmul(act, w_down, "nn", name="mm_down", epilogue=lambda acc, r: (acc + r,), extras=(x1,), tk=LONG_K)
    h3 = _rms_fwd(x2, S["g_ple"], name="rms3_fwd")
    w_pp, w_pg = wt("w_pp", h3), wt("w_pg", h3)
    pp = _matmul(p, w_pp, "nn", name="mm_pp", b_shards=True)

    def ple_epi(acc, x2r, ppr):
        s = jax.nn.sigmoid(acc)
        return x2r + s * ppr, s

    x3, sg = _matmul(h3, w_pg, "nn", name="mm_pg", out_dtypes=(F32, F32), epilogue=ple_epi, extras=(x2, pp), tm=512)
    dx3, dg_final, loss = _final_loss(x3, S["g_final"], tgt, name="final_loss")

    G = {"g_final": dg_final}
    dpg, dpp = _ple_bwd(dx3, pp, sg, name="ple_bwd")
    tok = emit({"w_pp": _matmul(p, dpp, "tn", name="mm_dwpp", out_dtypes=(BF16,), out_shards=True, tk=LONG_K),
                "w_pg": _matmul(h3, dpg, "tn", name="mm_dwpg", out_dtypes=(BF16,), tk=LONG_K)})
    bwd = dict(out_dtypes=(F32, BF16), epilogue=_epi_rms_bwd(2), n_vec=1, **rows)
    dx2, dx2b, G["g_ple"] = _matmul(dpg, w_pg, "nt", name="mm_dh3", after=tok, extras=(x2, dx3),
                                    vec_extras=(S["g_ple"],), **bwd)
    tok = emit({"w_down": _matmul(act, dx2b, "tn", name="mm_dwdown", out_dtypes=(BF16,), tk=LONG_K)})
    dact = _matmul(dx2b, w_down, "nt", name="mm_dact", after=tok)
    dup_g, dup_v, dcf_g, dcf_v = _ffn_act_bwd(up_pre, cv["conv_ffn"], dact, name="ffn_act_bwd")
    G["conv_ffn"] = jnp.concatenate([dcf_g, dcf_v], axis=1)
    dup = jnp.concatenate([dup_g, dup_v], axis=1)
    tok = emit({"w_up": _matmul(h2, dup, "tn", name="mm_dwup", out_dtypes=(BF16,), out_shards=True, tn=SHARD_TILE, tk=LONG_K)})
    dh2 = _matmul(dup, w_up, "nt", name="mm_dh2", after=tok, b_shards=True, tk=2 * SHARD_TILE)
    dx1, dx1b, G["g_ffn"] = _rms_bwd(x1, S["g_ffn"], dh2, dx2, name="rms2_bwd")
    tok = emit({"w_out": _matmul(ycat, dx1b, "tn", name="mm_dwout", out_dtypes=(BF16,), tk=LONG_K)})
    dycat = _matmul(dx1b, w_out, "nt", name="mm_dycat", after=tok)
    do, dz, G["dn_g"] = _gated_norm_bwd(o, proj, S["dn_g"], dycat, z_off, nA, name="gated_norm_bwd")
    du, dw, dqd, dkd, dqk, dgl = _delta_scan_bwd(do, w, qd, kd, vn, qk, gl, ss, H, name="delta_scan_bwd")
    dq, dk, dv, dgB, dbB = _delta_prep_bwd(qkv, gamB, bB, ti, u, w, qk, du, dw, dqd, dkd, dqk, dgl, H,
                                           name="delta_prep_bwd")
    dgb = jnp.pad(jnp.concatenate([dgB[:, :, 0].T, dbB[:, :, 0].T], axis=1), ((0, 0), (0, LANE - 2 * H)))
    dab, dal, ddt = _gates_bwd(proj, alog, dtb, dgb, ab_off, H, name="gates_bwd")
    G["a_log"], G["dt_bias"] = dal[:, :H], ddt[:, :H]
    dqkv, G["conv_qkv"] = _qkv_bwd(proj, cv["conv_qkv"], dq, dk, dv, qkv_off, H, name="qkv_bwd")
    dax, dab_, dac, G["conv_a"] = _group_a_bwd(proj, cv["conv_a"], dycat, CW, name="group_a_bwd")
    in_p = w_in.shape[0]
    dproj = jnp.concatenate([dax, dab_, dac, dqkv, dz, dab, jnp.zeros((T, in_p - (ab_off + 1) * LANE), BF16)], axis=1)
    tok = emit({"w_in": _matmul(dproj, h1, "tn", name="mm_dwin", out_dtypes=(BF16,), tk=LONG_K)})
    dh1 = _matmul(dproj, w_in, "nn", name="mm_dh1", after=tok, tk=LONG_K)
    grad_x, _, G["g_mix"] = _rms_bwd(x, S["g_mix"], dh1, dx1, name="rms1_bwd")
    return loss, grad_x, G


def _col_sharded(landed):
    _, R, C = landed.shape
    return jnp.transpose(landed, (1, 0, 2)).reshape(R, N_DEV * C)


def kernel(x, p, norm_mix_g, w_in, conv_a_w, conv_qkv_w, a_log, dt_bias, dn_norm_g, w_out, norm_ffn_g, w_up, conv_ffn_w, w_down, norm_ple_g, w_ple_gate, w_ple_proj, final_norm_g, loss_target, m_norm_mix_g, m_w_in, m_conv_a_w, m_conv_qkv_w, m_a_log, m_dt_bias, m_dn_norm_g, m_w_out, m_norm_ffn_g, m_w_up, m_conv_ffn_w, m_w_down, m_norm_ple_g, m_w_ple_gate, m_w_ple_proj, m_final_norm_g, v_norm_mix_g, v_w_in, v_conv_a_w, v_conv_qkv_w, v_a_log, v_dt_bias, v_dn_norm_g, v_w_out, v_norm_ffn_g, v_w_up, v_conv_ffn_w, v_w_down, v_norm_ple_g, v_w_ple_gate, v_w_ple_proj, v_final_norm_g):
    T, D = x.shape[1], x.shape[2]
    xd, _, cd = _mesh_pos()
    me = 4 * xd + 2 * lax.axis_index("y") + cd

    conv_sh = [conv_a_w[0], conv_qkv_w[0], conv_ffn_w[0]]
    conv_n = [c.size for c in conv_sh]
    pack_rows = -(-sum(conv_n) // LANE)
    conv_pack = jnp.pad(jnp.concatenate([c.reshape(-1) for c in conv_sh]), (0, pack_rows * LANE - sum(conv_n))).reshape(pack_rows, LANE)
    names = ["w_in", "conv", "w_out", "w_up", "w_down", "w_pg", "w_pp"]
    tr_ = lambda t: jnp.swapaxes(t, 1, 2)
    shards = [w_in[0].T.astype(BF16), conv_pack, w_out[0].astype(BF16), w_up[0].astype(BF16), w_down[0].astype(BF16),
              w_ple_gate[0].astype(BF16), w_ple_proj[0].astype(BF16)]
    empty_slots = lambda blocks: [lax.empty((N_DEV,) + tuple(b.shape), b.dtype) for b in blocks]
    handles, tok0 = _split_start(shards, empty_slots(shards), False, name="gather_start",
                                 relations=[(SIBLING,) + SAME_CORE] + [ALL_PEERS] * (len(shards) - 1))
    handle = dict(zip(names, handles))
    own = dict(zip(names, shards))
    in_cols = N_DEV * w_in.shape[2]
    in_p = (in_cols // LANE) * LANE + AB_PAD
    in_place = {"w_up", "w_pp"}

    def gathered(name, after):
        if name == "w_in":
            passed, fwd = _gather_forward(handle[name], after, name="gather_forward_w_in")
            landed = _gather_wait_two_level(passed, fwd, name="gather_wait_w_in")
        else:
            landed = _split_wait(handle[name], after, False, name="gather_wait_" + name)
        return lax.dynamic_update_index_in_dim(landed, own[name], me, 0)

    def wt(name, after):
        landed = gathered(name, after)
        if name in in_place:
            return landed
        full = landed.reshape(-1, D)
        return jnp.pad(full, ((0, in_p - in_cols), (0, 0))) if name == "w_in" else full

    def conv(after):
        flat = gathered("conv", after).reshape(N_DEV, pack_rows * LANE)
        out, o_ = {}, 0
        for nm, c, n_ in zip(("conv_a", "conv_qkv", "conv_ffn"), conv_sh, conv_n):
            out[nm] = _col_sharded(flat[:, o_:o_ + n_].reshape((N_DEV,) + c.shape))
            o_ += n_
        return out

    pending, mine = {}, {}

    def emit(grads):
        parts = [g if nm in in_place else (g[:in_cols] if nm == "w_in" else g).reshape(N_DEV, -1, D)
                 for nm, g in grads.items()]
        hs, tok = _split_start(parts, empty_slots([q[0] for q in parts]), True, name="scatter_start_" + "_".join(grads))
        pending.update(zip(grads, hs))
        mine.update({nm: lax.dynamic_index_in_dim(q, me, 0, keepdims=False) for nm, q in zip(grads, parts)})
        return tok

    S = {
        "g_mix": norm_mix_g + tok0[0, 0], "a_log": a_log, "dt_bias": dt_bias, "dn_g": dn_norm_g, "g_ffn": norm_ffn_g,
        "g_ple": norm_ple_g, "g_final": final_norm_g.reshape(1, D),
    }

    loss_v, grad_x, G = _local_step(x[0], p[0, 0], loss_target[0], S, wt, conv, emit)
    loss = lax.psum(loss_v[0, 0], ("x", "y", "c"))

    small_names = ["g_mix", "g_ffn", "g_ple", "g_final", "dn_g", "a_log", "dt_bias", "conv_a", "conv_qkv", "conv_ffn"]
    small_rows, pieces = [], []
    for nm in small_names:
        g_ = G[nm].reshape(-1)
        r_ = -(-g_.size // (8 * LANE)) * 8
        small_rows.append(r_)
        pieces.append(jnp.pad(g_, (0, r_ * LANE - g_.size)).reshape(r_, LANE))
    landed = {nm: _split_wait(h_, grad_x, True, name="scatter_wait_" + nm) for nm, h_ in pending.items() if nm != "w_in"}

    def adam(parts, w_, m_, v_, nm, own_=None):
        shp = w_.shape
        w2, m2, v2 = (t.reshape(parts.shape[1:]) for t in (w_, m_, v_))
        kw = {} if own_ is None else {"own": own_, "me": me.astype(jnp.int32).reshape(1)}
        return tuple(t.reshape(shp) for t in _adam(parts, w2, m2, v2, name="adam_" + nm, **kw))

    big = {
        "w_up": adam(landed["w_up"], w_up, m_w_up, v_w_up, "w_up", mine["w_up"]),
        "w_down": adam(landed["w_down"], w_down, m_w_down, v_w_down, "w_down", mine["w_down"]),
        "w_out": adam(landed["w_out"], w_out, m_w_out, v_w_out, "w_out", mine["w_out"]),
        "w_pg": adam(landed["w_pg"], w_ple_gate, m_w_ple_gate, v_w_ple_gate, "w_ple_gate", mine["w_pg"]),
        "w_pp": adam(landed["w_pp"], w_ple_proj, m_w_ple_proj, v_w_ple_proj, "w_ple_proj", mine["w_pp"]),
    }
    big_done = sum(r[1].reshape(-1)[:1] for r in big.values())
    (small_l,) = _exchange([jnp.concatenate(pieces, axis=0)], False, name="gather_small_grads", after=big_done)

    def small_parts(nm):
        i = small_names.index(nm)
        r0 = sum(small_rows[:i])
        shp = G[nm].shape
        return small_l[:, r0:r0 + small_rows[i], :].reshape(N_DEV, -1)[:, :G[nm].size].reshape((N_DEV,) + shp)

    def conv_parts(nm, shard):
        full = small_parts(nm)
        C = shard.shape[-1]
        return lax.dynamic_slice_in_dim(full, me * C, C, axis=2)

    res = [
        adam(small_parts("g_mix"), norm_mix_g, m_norm_mix_g, v_norm_mix_g, "norm_mix_g"),
        None,
        adam(conv_parts("conv_a", conv_a_w), conv_a_w, m_conv_a_w, v_conv_a_w, "conv_a_w"),
        adam(conv_parts("conv_qkv", conv_qkv_w), conv_qkv_w, m_conv_qkv_w, v_conv_qkv_w, "conv_qkv_w"),
        adam(small_parts("a_log"), a_log, m_a_log, v_a_log, "a_log"),
        adam(small_parts("dt_bias"), dt_bias, m_dt_bias, v_dt_bias, "dt_bias"),
        adam(small_parts("dn_g"), dn_norm_g, m_dn_norm_g, v_dn_norm_g, "dn_norm_g"),
        big["w_out"],
        adam(small_parts("g_ffn"), norm_ffn_g, m_norm_ffn_g, v_norm_ffn_g, "norm_ffn_g"),
        big["w_up"],
        adam(conv_parts("conv_ffn", conv_ffn_w), conv_ffn_w, m_conv_ffn_w, v_conv_ffn_w, "conv_ffn_w"),
        big["w_down"],
        adam(small_parts("g_ple"), norm_ple_g, m_norm_ple_g, v_norm_ple_g, "norm_ple_g"),
        big["w_pg"],
        big["w_pp"],
        adam(small_parts("g_final"), final_norm_g.reshape(1, D), m_final_norm_g.reshape(1, D),
             v_final_norm_g.reshape(1, D), "final_norm_g"),
    ]
    res[-1] = tuple(t.reshape(D) for t in res[-1])
    landed_in = _split_wait(pending["w_in"], res[10][1], True, name="scatter_wait_w_in")
    res[1] = tuple(tr_(t) for t in adam(landed_in, tr_(w_in), tr_(m_w_in), tr_(v_w_in), "w_in", mine["w_in"]))
    grads, deltas, new_m, new_v = zip(*res)
    return (loss, grad_x[None], *grads, *deltas, *new_m, *new_v)
```

```python
import functools

import jax
import jax.numpy as jnp
from jax import lax
from jax.experimental import pallas as pl
from jax.experimental.pallas import tpu as pltpu

F32 = jnp.float32
BF16 = jnp.bfloat16

EPS = 1e-6
CHUNK = 64
HEAD = 128
LANE = 128
N_DEV = 8
AB_PAD = 512

ADAM_LR = 0.001
ADAM_B1 = 0.9
ADAM_B2 = 0.999
ADAM_EPS = 1e-08
ADAM_WD = 0.01
ADAM_STEP = 10

MESH = pl.DeviceIdType.MESH


def _tile(dim, target, align=LANE):
    if dim <= target:
        return dim
    t = (target // align) * align
    while t > align and dim % t:
        t -= align
    assert dim % t == 0, (dim, target)
    return t


def _params(sem, vmem_mb=48):
    return pltpu.CompilerParams(dimension_semantics=sem, vmem_limit_bytes=vmem_mb << 20)


_DN = {"nn": (((1,), (0,)), ((), ())), "nt": (((1,), (1,)), ((), ())), "tn": (((0,), (0,)), ((), ()))}
LONG_K = 4096
SHARD_TILE = 1408


def _matmul(a, b, mode, *, name, out_dtypes=(F32,), epilogue=None, extras=(), vec_extras=(), n_vec=0, after=None,
            b_shards=False, out_shards=False, tm=1024, tn=1024, tk=2048):
    shard_w = b.shape[2] if b_shards else None
    if b_shards:
        b_rows, b_cols = b.shape[1], N_DEV * shard_w
    else:
        b_rows, b_cols = b.shape
    if mode == "nn":
        (M, K), (K2, N) = a.shape, (b_rows, b_cols)
    elif mode == "nt":
        (M, K), (N, K2) = a.shape, (b_rows, b_cols)
    else:
        (K, M), (K2, N) = a.shape, (b_rows, b_cols)
    assert K == K2, (name, a.shape, b.shape)
    tm = _tile(M, tm)
    tn = _tile(shard_w if (b_shards and mode == "nn") else N // N_DEV if out_shards else N, tn)
    grp = 1
    if b_shards and mode == "nt":
        grp = max(g for g in (1, 2, 4, 8) if g <= max(1, tk // shard_w))
    tk = grp * shard_w if grp > 1 else _tile(shard_w if (b_shards and mode == "nt") else K, tk)
    assert K % tk == 0, (name, K, tk)
    nk = K // tk
    n_ex, n_out = len(extras) + len(vec_extras), len(out_dtypes)
    assert n_vec == 0 or tn == N, (name, tn, N)
    dn = _DN[mode]

    n_tok = 0 if after is None else 1

    def body(a_ref, b_ref, *rest):
        rest = rest[n_tok:]
        ex_refs, out_refs, vec_refs = rest[:n_ex], rest[n_ex:n_ex + n_out], rest[n_ex + n_out:n_ex + n_out + n_vec]
        if grp > 1:
            part = sum(lax.dot_general(a_ref[:, s * shard_w:(s + 1) * shard_w].astype(BF16), b_ref[s].astype(BF16), dn,
                                       preferred_element_type=F32) for s in range(grp))
        else:
            part = lax.dot_general(a_ref[...].astype(BF16), b_ref[...].astype(BF16), dn, preferred_element_type=F32)
        first_rows = pl.program_id(0) == 0

        def finish(res):
            outs = (res,) if epilogue is None else epilogue(res, *[e[...] for e in ex_refs])
            for o_ref, val in zip(out_refs, outs[:n_out]):
                o_ref[...] = val.astype(o_ref.dtype)
            for v_ref, val in zip(vec_refs, outs[n_out:]):
                @pl.when(first_rows)
                def _(v_ref=v_ref, val=val):
                    v_ref[...] = val

                @pl.when(jnp.logical_not(first_rows))
                def _(v_ref=v_ref, val=val):
                    v_ref[...] += val

        if nk == 1:
            finish(part)
            return
        acc, k = rest[-1], pl.program_id(2)

        @pl.when(k == 0)
        def _():
            acc[...] = part

        @pl.when(k > 0)
        def _():
            acc[...] += part

        @pl.when(k == nk - 1)
        def _():
            finish(acc[...])

    a_spec = pl.BlockSpec((tk, tm), lambda i, j, k: (k, i)) if mode == "tn" else pl.BlockSpec((tm, tk), lambda i, j, k: (i, k))
    if b_shards and mode == "nn":
        per = shard_w // tn
        b_spec = pl.BlockSpec((None, tk, tn), lambda i, j, k: (lax.div(j, per), k, lax.rem(j, per)))
    elif b_shards and grp > 1:
        b_spec = pl.BlockSpec((grp, tn, shard_w), lambda i, j, k: (k, j, 0))
    elif b_shards:
        per = shard_w // tk
        b_spec = pl.BlockSpec((None, tn, tk), lambda i, j, k: (lax.div(k, per), j, lax.rem(k, per)))
    else:
        b_spec = pl.BlockSpec((tn, tk), lambda i, j, k: (j, k)) if mode == "nt" else pl.BlockSpec((tk, tn), lambda i, j, k: (k, j))
    mn_spec = pl.BlockSpec((tm, tn), lambda i, j, k: (i, j))
    vec_spec = pl.BlockSpec((1, tn), lambda i, j, k: (0, j))
    if out_shards:
        assert not extras
        per_o = (N // N_DEV) // tn
        out_spec = pl.BlockSpec((None, tm, tn), lambda i, j, k: (lax.div(j, per_o), i, lax.rem(j, per_o)))
        out_dims = (N_DEV, M, N // N_DEV)
    else:
        out_spec, out_dims = mn_spec, (M, N)
    outs = pl.pallas_call(
        body, name=name, grid=(M // tm, N // tn, nk),
        in_specs=[a_spec, b_spec] + [pl.BlockSpec((8, LANE), lambda i, j, k: (0, 0))] * n_tok
        + [mn_spec] * len(extras) + [vec_spec] * len(vec_extras),
        out_specs=[out_spec] * n_out + [vec_spec] * n_vec,
        out_shape=[jax.ShapeDtypeStruct(out_dims, dt) for dt in out_dtypes] + [jax.ShapeDtypeStruct((1, N), F32)] * n_vec,
        scratch_shapes=[pltpu.VMEM((tm, tn), F32)] if nk > 1 else [],
        compiler_params=_params(("arbitrary" if n_vec else "parallel", "parallel", "arbitrary"), 56),
    )(a, b, *([] if after is None else [after]), *extras, *vec_extras)
    return outs[0] if n_out + n_vec == 1 else outs


def _rms_fwd(x, g, *, name):
    T, D = x.shape
    tr = _tile(T, 256, 8)

    def body(x_ref, g_ref, h_ref):
        xv = x_ref[...]
        r = lax.rsqrt(jnp.mean(xv * xv, axis=-1, keepdims=True) + EPS)
        h_ref[...] = (xv * r * g_ref[...]).astype(h_ref.dtype)

    return pl.pallas_call(
        body, name=name, grid=(T // tr,),
        in_specs=[pl.BlockSpec((tr, D), lambda i: (i, 0)), pl.BlockSpec((1, D), lambda i: (0, 0))],
        out_specs=pl.BlockSpec((tr, D), lambda i: (i, 0)),
        out_shape=jax.ShapeDtypeStruct((T, D), BF16),
        compiler_params=_params(("parallel",)),
    )(x, g)


def _rms_bwd(x, g, dh, dres, *, name):
    T, D = x.shape
    tr = _tile(T, 256, 8)
    epi = _epi_rms_bwd(2)

    def body(x_ref, g_ref, dh_ref, dres_ref, dx_ref, dxb_ref, dg_ref):
        dx, _, dgp = epi(dh_ref[...], x_ref[...], dres_ref[...], g_ref[...])

        @pl.when(pl.program_id(0) == 0)
        def _():
            dg_ref[...] = jnp.zeros_like(dg_ref)

        dg_ref[...] += dgp
        dx_ref[...] = dx
        dxb_ref[...] = dx.astype(dxb_ref.dtype)

    row = pl.BlockSpec((tr, D), lambda i: (i, 0))
    vec = pl.BlockSpec((1, D), lambda i: (0, 0))
    return pl.pallas_call(
        body, name=name, grid=(T // tr,),
        in_specs=[row, vec, row, row], out_specs=[row, row, vec],
        out_shape=[jax.ShapeDtypeStruct((T, D), F32), jax.ShapeDtypeStruct((T, D), BF16), jax.ShapeDtypeStruct((1, D), F32)],
        compiler_params=_params(("arbitrary",)),
    )(x, g, dh, dres)


ROW_TILE = 256


def _epi_residual_rms(acc, res, g):
    xn = acc + res
    r = lax.rsqrt(jnp.mean(xn * xn, axis=-1, keepdims=True) + EPS)
    return xn, xn * r * g


def _epi_rms_bwd(n_copies):
    def epi(dh, x, dres, g):
        r = lax.rsqrt(jnp.mean(x * x, axis=-1, keepdims=True) + EPS)
        xh = x * r
        dxh = dh * g
        dx = dres + r * (dxh - xh * jnp.mean(dxh * xh, axis=-1, keepdims=True))
        return (dx,) * n_copies + (jnp.sum(dh * xh, axis=0, keepdims=True),)
    return epi


def _final_loss(x, g, tgt, *, name):
    T, D = x.shape
    tr = _tile(T, 256, 8)

    def body(x_ref, g_ref, t_ref, dx_ref, dg_ref, loss_ref):
        xv = x_ref[...]
        r = lax.rsqrt(jnp.mean(xv * xv, axis=-1, keepdims=True) + EPS)
        xh = xv * r
        gv = g_ref[...]
        err = xh * gv - t_ref[...]

        @pl.when(pl.program_id(0) == 0)
        def _():
            dg_ref[...] = jnp.zeros_like(dg_ref)
            loss_ref[...] = jnp.zeros_like(loss_ref)

        part = 0.5 * jnp.sum(jnp.mean(err * err, axis=-1, keepdims=True), axis=0, keepdims=True)
        loss_ref[...] += jnp.broadcast_to(part, loss_ref.shape)
        dy = err * (1.0 / D)
        dg_ref[...] += jnp.sum(dy * xh, axis=0, keepdims=True)
        dxh = dy * gv
        dx_ref[...] = r * (dxh - xh * jnp.mean(dxh * xh, axis=-1, keepdims=True))

    row = pl.BlockSpec((tr, D), lambda i: (i, 0))
    vec = pl.BlockSpec((1, D), lambda i: (0, 0))
    return pl.pallas_call(
        body, name=name, grid=(T // tr,),
        in_specs=[row, vec, row], out_specs=[row, vec, pl.BlockSpec((1, LANE), lambda i: (0, 0))],
        out_shape=[jax.ShapeDtypeStruct((T, D), F32), jax.ShapeDtypeStruct((1, D), F32),
                   jax.ShapeDtypeStruct((1, LANE), F32)],
        compiler_params=_params(("arbitrary",)),
    )(x, g, tgt)


def _ple_bwd(dx3, pp, sg, *, name):
    T, D = dx3.shape
    tr = _tile(T, 256, 8)

    def body(dx_ref, pp_ref, sg_ref, dpg_ref, dpp_ref):
        dx, s = dx_ref[...], sg_ref[...]
        dpg_ref[...] = (dx * pp_ref[...] * s * (1.0 - s)).astype(dpg_ref.dtype)
        dpp_ref[...] = (dx * s).astype(dpp_ref.dtype)

    row = pl.BlockSpec((tr, D), lambda i: (i, 0))
    return pl.pallas_call(
        body, name=name, grid=(T // tr,), in_specs=[row, row, row], out_specs=[row, row],
        out_shape=[jax.ShapeDtypeStruct((T, D), BF16)] * 2, compiler_params=_params(("parallel",)),
    )(dx3, pp, sg)


ROWS_QKV_FWD, ROWS_QKV_BWD, ROWS_FFN_FWD, ROWS_FFN_BWD, ROWS_GROUP_A = 512, 256, 256, 128, 256


def _ext(ref, r0, T, before, after, RC):
    parts = []
    if before:
        p0 = pl.multiple_of(jnp.maximum(r0 - 8, 0), 8)
        parts.append(jnp.where(r0 > 0, ref[pl.ds(p0, 8), :], 0.0))
    parts.append(ref[pl.ds(r0, RC), :])
    if after:
        n0 = pl.multiple_of(jnp.minimum(r0 + RC, T - 8), 8)
        parts.append(jnp.where(r0 + RC < T, ref[pl.ds(n0, 8), :], 0.0))
    return parts[0] if len(parts) == 1 else jnp.concatenate(parts, axis=0)


def _down(xx, s):
    return (xx if s == 0 else pltpu.roll(xx, s, 0))[8:, :]


def _up(xx, s, rows):
    return (xx if s == 0 else pltpu.roll(xx, xx.shape[0] - s, 0))[:rows, :]


def _conv_down(xx, w_ref, K):
    y = None
    for j in range(K):
        t = _down(xx, K - 1 - j) * w_ref[j:j + 1, :]
        y = t if y is None else y + t
    return y


def _fold8(x):
    return jnp.sum(x.reshape(x.shape[0] // 8, 8, x.shape[1]), axis=0)


def _win(ref, r0, lo, n, T, RC, edge):
    if not edge:
        return ref[pl.ds(r0 + lo, n), :]
    xx = _ext(ref, r0, T, True, True, RC)
    a = 8 + lo
    return (xx if a == 0 else pltpu.roll(xx, xx.shape[0] - a, 0))[:n, :]


def _taps(ref, w_ref, K, r0, n, T, RC, edge):
    wins = [_win(ref, r0, -(K - 1 - j), n, T, RC, edge) for j in range(K)]
    y = wins[0] * w_ref[0:1, :]
    for j in range(1, K):
        y = y + wins[j] * w_ref[j:j + 1, :]
    return wins, y


def _untaps(scr_ref, val, w_ref, K, RC):
    scr_ref[0:val.shape[0], :] = val
    y = scr_ref[K - 1:K - 1 + RC, :] * w_ref[0:1, :]
    for j in range(1, K):
        s = K - 1 - j
        y = y + scr_ref[s:s + RC, :] * w_ref[j:j + 1, :]
    return y


def _peeled(n_chunks, RC, step, init):
    carry = step(0, init, True)
    if n_chunks > 2:
        carry = lax.fori_loop(1, n_chunks - 1, lambda i, c: step(pl.multiple_of(i * RC, RC), c, False), carry)
    if n_chunks > 1:
        carry = step((n_chunks - 1) * RC, carry, True)
    return carry


def _silu(x):
    return x * jax.nn.sigmoid(x)


def _dsilu(x):
    s = jax.nn.sigmoid(x)
    return s * (1.0 + x * (1.0 - s))


def _col_specs(T, offs):
    return [pl.BlockSpec((T, LANE), functools.partial(lambda o, j: (0, o + j), o)) for o in offs]


def _group_a_fwd(proj, conv_w, CW, *, name):
    T = proj.shape[0]
    RC = _tile(T, ROWS_GROUP_A, 8)
    nb = CW // LANE
    K = conv_w.shape[0]

    def body(ax_ref, ab_ref, ac_ref, w_ref, y_ref):
        def step(i, carry):
            r0 = pl.multiple_of(i * RC, RC)
            m = _ext(ac_ref, r0, T, True, False, RC) * _ext(ax_ref, r0, T, True, False, RC)
            y_ref[pl.ds(r0, RC), :] = (ab_ref[pl.ds(r0, RC), :] * _conv_down(m, w_ref, K)).astype(y_ref.dtype)
            return carry
        lax.fori_loop(0, T // RC, step, 0)

    return pl.pallas_call(
        body, name=name, grid=(nb,),
        in_specs=_col_specs(T, (0, nb, 2 * nb)) + [pl.BlockSpec((K, LANE), lambda j: (0, j))],
        out_specs=pl.BlockSpec((T, LANE), lambda j: (0, j)),
        out_shape=jax.ShapeDtypeStruct((T, CW), BF16), compiler_params=_params(("parallel",)),
    )(proj, proj, proj, conv_w)


def _group_a_bwd(proj, conv_w, dycat, CW, *, name):
    T = proj.shape[0]
    RC = _tile(T, ROWS_GROUP_A, 8)
    nb = CW // LANE
    K = conv_w.shape[0]

    def body(ax_ref, ab_ref, ac_ref, w_ref, dy_ref, dax_ref, dab_ref, dac_ref, dw_ref):
        def step(i, accs):
            r0 = pl.multiple_of(i * RC, RC)
            ax3 = _ext(ax_ref, r0, T, True, True, RC)
            ac3 = _ext(ac_ref, r0, T, True, True, RC)
            m3 = ax3 * ac3
            c = _conv_down(m3[:RC + 8], w_ref, K)
            dy = dy_ref[pl.ds(r0, RC), :]
            dab_ref[pl.ds(r0, RC), :] = (dy * c).astype(dab_ref.dtype)
            dc2 = _ext(dy_ref, r0, T, False, True, RC) * _ext(ab_ref, r0, T, False, True, RC)
            dm = None
            new = []
            for j in range(K):
                s = K - 1 - j
                t = _up(dc2, s, RC) * w_ref[j:j + 1, :]
                dm = t if dm is None else dm + t
                new.append(accs[j] + _fold8(dc2[:RC] * _down(m3[:RC + 8], s)))
            dax_ref[pl.ds(r0, RC), :] = (dm * ac3[8:RC + 8]).astype(dax_ref.dtype)
            dac_ref[pl.ds(r0, RC), :] = (dm * ax3[8:RC + 8]).astype(dac_ref.dtype)
            return tuple(new)

        accs = lax.fori_loop(0, T // RC, step, tuple(jnp.zeros((8, LANE), F32) for _ in range(K)))
        for j in range(K):
            dw_ref[j:j + 1, :] = jnp.sum(accs[j], axis=0, keepdims=True)

    col = pl.BlockSpec((T, LANE), lambda j: (0, j))
    wsp = pl.BlockSpec((K, LANE), lambda j: (0, j))
    return pl.pallas_call(
        body, name=name, grid=(nb,),
        in_specs=_col_specs(T, (0, nb, 2 * nb)) + [wsp, col],
        out_specs=[col, col, col, wsp],
        out_shape=[jax.ShapeDtypeStruct((T, CW), BF16)] * 3 + [jax.ShapeDtypeStruct((K, CW), F32)],
        compiler_params=_params(("parallel",)),
    )(proj, proj, proj, conv_w, dycat)


def _qkv_fwd(proj, conv_w, off, H, *, name):
    T = proj.shape[0]
    RC = _tile(T, ROWS_QKV_FWD, 8)
    nb = 3 * H
    K = conv_w.shape[0]

    def body(x_ref, w_ref, y_ref):
        j = pl.program_id(0)
        is_qk = j < 2 * H
        scale = jnp.where(j < H, HEAD ** -0.5, 1.0).astype(F32)

        def step(r0, carry, edge):
            s = _silu(_taps(x_ref, w_ref, K, r0, RC, T, RC, edge)[1])
            r = lax.rsqrt(jnp.sum(s * s, axis=-1, keepdims=True) + EPS) * scale
            y_ref[pl.ds(r0, RC), :] = s * jnp.where(is_qk, r, 1.0)
            return carry
        _peeled(T // RC, RC, step, 0)

    return pl.pallas_call(
        body, name=name, grid=(nb,),
        in_specs=_col_specs(T, (off,)) + [pl.BlockSpec((K, LANE), lambda j: (0, j))],
        out_specs=pl.BlockSpec((T, LANE), lambda j: (0, j)),
        out_shape=jax.ShapeDtypeStruct((T, nb * LANE), F32), compiler_params=_params(("parallel",)),
    )(proj, conv_w)


def _qkv_bwd(proj, conv_w, dq, dk, dv, off, H, *, name):
    T = proj.shape[0]
    RC = _tile(T, ROWS_QKV_BWD, 8)
    nb = 3 * H
    K = conv_w.shape[0]

    def body(x_ref, w_ref, dq_ref, dk_ref, dv_ref, dx_ref, dw_ref, scr_ref):
        j = pl.program_id(0)
        is_qk = j < 2 * H
        scale = jnp.where(j < H, HEAD ** -0.5, 1.0).astype(F32)

        def step(r0, accs, edge):
            xs, c2 = _taps(x_ref, w_ref, K, r0, RC + 8, T, RC, edge)
            s2 = _silu(c2)
            dn2 = jnp.where(j < H, _win(dq_ref, r0, 0, RC + 8, T, RC, edge),
                            jnp.where(is_qk, _win(dk_ref, r0, 0, RC + 8, T, RC, edge),
                                      _win(dv_ref, r0, 0, RC + 8, T, RC, edge)))
            r = lax.rsqrt(jnp.sum(s2 * s2, axis=-1, keepdims=True) + EPS)
            nh = s2 * r
            dnp = dn2 * scale
            ds_qk = r * (dnp - nh * jnp.sum(dnp * nh, axis=-1, keepdims=True))
            ds2 = jnp.where(is_qk, ds_qk, dn2)
            dc2 = ds2 * _dsilu(c2)
            dx_ref[pl.ds(r0, RC), :] = _untaps(scr_ref, dc2, w_ref, K, RC).astype(dx_ref.dtype)
            return tuple(accs[jj] + _fold8(dc2[:RC] * xs[jj][:RC]) for jj in range(K))

        accs = _peeled(T // RC, RC, step, tuple(jnp.zeros((8, LANE), F32) for _ in range(K)))
        for jj in range(K):
            dw_ref[jj:jj + 1, :] = jnp.sum(accs[jj], axis=0, keepdims=True)

    col = pl.BlockSpec((T, LANE), lambda j: (0, j))
    wsp = pl.BlockSpec((K, LANE), lambda j: (0, j))
    return pl.pallas_call(
        body, name=name, grid=(nb,),
        in_specs=_col_specs(T, (off,)) + [wsp] + [
            pl.BlockSpec((T, LANE), functools.partial(lambda o, j: (0, jnp.clip(j - o, 0, H - 1)), o)) for o in (0, H, 2 * H)],
        out_specs=[col, wsp],
        out_shape=[jax.ShapeDtypeStruct((T, nb * LANE), BF16), jax.ShapeDtypeStruct((K, nb * LANE), F32)],
        scratch_shapes=[pltpu.VMEM((RC + 8, LANE), F32)],
        compiler_params=_params(("parallel",)),
    )(proj, conv_w, dq, dk, dv)


def _softplus(x):
    return jnp.maximum(x, 0.0) + jnp.log(1.0 + jnp.exp(-jnp.abs(x)))


def _gates_fwd(proj, alog, dtb, off, H, *, name):
    T = proj.shape[0]
    tr = _tile(T, 512, CHUNK)

    def body(ab_ref, al_ref, dt_ref, gb_ref, gam_ref):
        ab = ab_ref[...]
        lane = lax.broadcasted_iota(jnp.int32, ab.shape, 1)
        g = -jnp.exp(al_ref[...]) * _softplus(ab + dt_ref[...])
        gb = jnp.where(lane < H, g, jnp.where(lane < 2 * H, jax.nn.sigmoid(ab), 0.0))
        gb_ref[...] = gb
        tril = _tri().astype(F32)
        for c in range(tr // CHUNK):
            rows = slice(c * CHUNK, (c + 1) * CHUNK)
            gam_ref[rows, :] = _mm(tril, gb[rows, :], precision=lax.Precision.HIGHEST)

    vec = pl.BlockSpec((1, LANE), lambda i: (0, 0))
    row = pl.BlockSpec((tr, LANE), lambda i: (i, 0))
    return pl.pallas_call(
        body, name=name, grid=(T // tr,),
        in_specs=[pl.BlockSpec((tr, LANE), lambda i: (i, off)), vec, vec],
        out_specs=[row, row],
        out_shape=[jax.ShapeDtypeStruct((T, LANE), F32)] * 2, compiler_params=_params(("parallel",)),
    )(proj, alog, dtb)


def _gates_bwd(proj, alog, dtb, dgb, off, H, *, name):
    T = proj.shape[0]
    tr = _tile(T, 512, CHUNK)

    def body(ab_ref, al_ref, dt_ref, d_ref, dab_ref, dal_ref, ddt_ref):
        ab, d = ab_ref[...], d_ref[...]
        lane = lax.broadcasted_iota(jnp.int32, ab.shape, 1)
        is_g = lane < H
        triu = _tri(upper=True).astype(F32)
        dg = jnp.concatenate([_mm(triu, d[c * CHUNK:(c + 1) * CHUNK, :], precision=lax.Precision.HIGHEST)
                              for c in range(tr // CHUNK)], axis=0)
        z = ab + dt_ref[...]
        A = -jnp.exp(al_ref[...])
        da = dg * A * jax.nn.sigmoid(z)
        beta = jax.nn.sigmoid(ab)
        db = d * beta * (1.0 - beta)
        dab_ref[...] = jnp.where(is_g, da, jnp.where(lane < 2 * H, db, 0.0)).astype(dab_ref.dtype)

        @pl.when(pl.program_id(0) == 0)
        def _():
            dal_ref[...] = jnp.zeros_like(dal_ref)
            ddt_ref[...] = jnp.zeros_like(ddt_ref)

        dal_ref[...] += jnp.sum(jnp.where(is_g, dg * A * _softplus(z), 0.0), axis=0, keepdims=True)
        ddt_ref[...] += jnp.sum(jnp.where(is_g, da, 0.0), axis=0, keepdims=True)

    vec = pl.BlockSpec((1, LANE), lambda i: (0, 0))
    row = pl.BlockSpec((tr, LANE), lambda i: (i, 0))
    return pl.pallas_call(
        body, name=name, grid=(T // tr,),
        in_specs=[pl.BlockSpec((tr, LANE), lambda i: (i, off)), vec, vec, row],
        out_specs=[row, vec, vec],
        out_shape=[jax.ShapeDtypeStruct((T, LANE), BF16), jax.ShapeDtypeStruct((1, LANE), F32),
                   jax.ShapeDtypeStruct((1, LANE), F32)],
        compiler_params=_params(("arbitrary",)),
    )(proj, alog, dtb, dgb)


def _gated_norm_fwd(o, proj, gn, zoff, *, name):
    T, W = o.shape
    tr = _tile(T, 512, 8)

    def body(o_ref, z_ref, g_ref, y_ref):
        ov = o_ref[...]
        r = lax.rsqrt(jnp.mean(ov * ov, axis=-1, keepdims=True) + EPS)
        y_ref[...] = (ov * r * g_ref[...] * _silu(z_ref[...])).astype(y_ref.dtype)

    blk = pl.BlockSpec((tr, LANE), lambda i, j: (i, j))
    return pl.pallas_call(
        body, name=name, grid=(T // tr, W // LANE),
        in_specs=[blk, pl.BlockSpec((tr, LANE), lambda i, j: (i, zoff + j)), pl.BlockSpec((1, LANE), lambda i, j: (0, 0))],
        out_specs=blk, out_shape=jax.ShapeDtypeStruct((T, W), BF16), compiler_params=_params(("parallel", "parallel")),
    )(o, proj, gn)


def _gated_norm_bwd(o, proj, gn, dycat, zoff, yoff, *, name):
    T, W = o.shape
    tr = _tile(T, 512, 8)

    def body(o_ref, z_ref, g_ref, dy_ref, do_ref, dz_ref, dg_ref):
        ov, zv, gv, dy = o_ref[...], z_ref[...], g_ref[...], dy_ref[...]
        r = lax.rsqrt(jnp.mean(ov * ov, axis=-1, keepdims=True) + EPS)
        nh = ov * r
        s = _silu(zv)

        @pl.when((pl.program_id(0) == 0) & (pl.program_id(1) == 0))
        def _():
            dg_ref[...] = jnp.zeros_like(dg_ref)

        dg_ref[...] += jnp.sum(dy * nh * s, axis=0, keepdims=True)
        dz_ref[...] = (dy * nh * gv * _dsilu(zv)).astype(dz_ref.dtype)
        dn = dy * gv * s
        do_ref[...] = r * (dn - nh * jnp.mean(dn * nh, axis=-1, keepdims=True))

    blk = pl.BlockSpec((tr, LANE), lambda i, j: (i, j))
    vec = pl.BlockSpec((1, LANE), lambda i, j: (0, 0))
    return pl.pallas_call(
        body, name=name, grid=(T // tr, W // LANE),
        in_specs=[blk, pl.BlockSpec((tr, LANE), lambda i, j: (i, zoff + j)), vec,
                  pl.BlockSpec((tr, LANE), lambda i, j: (i, yoff + j))],
        out_specs=[blk, blk, vec],
        out_shape=[jax.ShapeDtypeStruct((T, W), F32), jax.ShapeDtypeStruct((T, W), BF16),
                   jax.ShapeDtypeStruct((1, LANE), F32)],
        compiler_params=_params(("arbitrary", "arbitrary")),
    )(o, proj, gn, dycat)


def _ffn_act_fwd(up_pre, conv_w, *, name):
    T, F2 = up_pre.shape
    RC = _tile(T, ROWS_FFN_FWD, 8)
    nb = F2 // 2 // LANE
    K = conv_w.shape[0]

    def body(g_ref, v_ref, wg_ref, wv_ref, y_ref):
        def step(r0, carry, edge):
            _, gate = _taps(g_ref, wg_ref, K, r0, RC, T, RC, edge)
            _, val = _taps(v_ref, wv_ref, K, r0, RC, T, RC, edge)
            y_ref[pl.ds(r0, RC), :] = (_silu(gate) * val).astype(y_ref.dtype)
            return carry
        _peeled(T // RC, RC, step, 0)

    return pl.pallas_call(
        body, name=name, grid=(nb,),
        in_specs=_col_specs(T, (0, nb)) + [pl.BlockSpec((K, LANE), lambda j: (0, j)),
                                           pl.BlockSpec((K, LANE), lambda j: (0, nb + j))],
        out_specs=pl.BlockSpec((T, LANE), lambda j: (0, j)),
        out_shape=jax.ShapeDtypeStruct((T, F2 // 2), BF16), compiler_params=_params(("parallel",)),
    )(up_pre, up_pre, conv_w, conv_w)


def _ffn_act_bwd(up_pre, conv_w, dact, *, name):
    T, F2 = up_pre.shape
    RC = _tile(T, ROWS_FFN_BWD, 8)
    nb = F2 // 2 // LANE
    K = conv_w.shape[0]

    def body(g_ref, v_ref, wg_ref, wv_ref, da_ref, dg_ref, dv_ref, dwg_ref, dwv_ref, sg_ref, sv_ref):
        def step(r0, accs, edge):
            gs, gate2 = _taps(g_ref, wg_ref, K, r0, RC + 8, T, RC, edge)
            vs, val2 = _taps(v_ref, wv_ref, K, r0, RC + 8, T, RC, edge)
            da2 = _win(da_ref, r0, 0, RC + 8, T, RC, edge)
            dgate2 = da2 * val2 * _dsilu(gate2)
            dval2 = da2 * _silu(gate2)
            dg_ref[pl.ds(r0, RC), :] = _untaps(sg_ref, dgate2, wg_ref, K, RC).astype(dg_ref.dtype)
            dv_ref[pl.ds(r0, RC), :] = _untaps(sv_ref, dval2, wv_ref, K, RC).astype(dv_ref.dtype)
            new = []
            for j in range(K):
                new.append(accs[2 * j] + _fold8(dgate2[:RC] * gs[j][:RC]))
                new.append(accs[2 * j + 1] + _fold8(dval2[:RC] * vs[j][:RC]))
            return tuple(new)

        accs = _peeled(T // RC, RC, step, tuple(jnp.zeros((8, LANE), F32) for _ in range(2 * K)))
        for j in range(K):
            dwg_ref[j:j + 1, :] = jnp.sum(accs[2 * j], axis=0, keepdims=True)
            dwv_ref[j:j + 1, :] = jnp.sum(accs[2 * j + 1], axis=0, keepdims=True)

    col = pl.BlockSpec((T, LANE), lambda j: (0, j))
    wsp = pl.BlockSpec((K, LANE), lambda j: (0, j))
    return pl.pallas_call(
        body, name=name, grid=(nb,),
        in_specs=_col_specs(T, (0, nb)) + [wsp, pl.BlockSpec((K, LANE), lambda j: (0, nb + j)), col],
        out_specs=[col, col, wsp, wsp],
        out_shape=[jax.ShapeDtypeStruct((T, F2 // 2), BF16)] * 2 + [jax.ShapeDtypeStruct((K, F2 // 2), F32)] * 2,
        scratch_shapes=[pltpu.VMEM((RC + 8, LANE), F32)] * 2,
        compiler_params=_params(("parallel",)),
    )(up_pre, up_pre, conv_w, conv_w, dact)


CPB = 8
CPB_SCAN = 4
GRP = 8
HP = lax.Precision.HIGH


def _tri(strict=False, upper=False):
    r = lax.broadcasted_iota(jnp.int32, (CHUNK, CHUNK), 0)
    c = lax.broadcasted_iota(jnp.int32, (CHUNK, CHUNK), 1)
    if upper:
        return c >= r
    return (r > c) if strict else (r >= c)


def _mm(a, b, dn="nn", precision=None):
    precision = HP if precision is None else precision
    return lax.dot_general(a, b, _DN[dn], precision=precision, preferred_element_type=F32)


def _mm16(a, b, dn="nn"):
    return lax.dot_general(a.astype(BF16), b.astype(BF16), _DN[dn], preferred_element_type=F32)


def _each(f, *cols):
    return [f(*xs) for xs in zip(*cols)]


def _decay(gam):
    return jnp.exp(jnp.where(_tri(), gam[:, :CHUNK] - gam.T[:CHUNK, :], -1e30))


def _delta_specs(T, H, cpb):
    rows = cpb * CHUNK
    col = lambda o: pl.BlockSpec((rows, LANE), functools.partial(lambda o, h, n: (n, o + h), o))
    bc = pl.BlockSpec((1, rows, LANE), lambda h, n: (h, n, 0))
    sq = pl.BlockSpec((1, cpb, CHUNK, CHUNK), lambda h, n: (h, n, 0, 0))
    vec = pl.BlockSpec((1, cpb, 1, LANE), lambda h, n: (h, n, 0, 0))
    return col, bc, sq, vec


def _delta_prep_fwd(qkv, gamB, bB, H, *, name):
    T = qkv.shape[0]
    N = T // CHUNK
    cpb = _tile(N, CPB, 8)
    grp = min(GRP, cpb)
    col, bc, sq, vec = _delta_specs(T, H, cpb)

    def body(q_ref, k_ref, v_ref, g_ref, b_ref, u_ref, w_ref, qd_ref, kd_ref, qk_ref, ti_ref, gl_ref):
        eye = (lax.broadcasted_iota(jnp.int32, (CHUNK, CHUNK), 0) == lax.broadcasted_iota(jnp.int32, (CHUNK, CHUNK), 1)).astype(F32)
        strict = _tri(strict=True)
        for c0 in range(0, cpb, grp):
            cs = list(range(c0, c0 + grp))
            rows = [slice(c * CHUNK, (c + 1) * CHUNK) for c in cs]
            q, k, v = ([r_[r, :] for r in rows] for r_ in (q_ref, k_ref, v_ref))
            bb = [b_ref[0, r, :] for r in rows]
            gam = [g_ref[0, r, :] for r in rows]
            D = _each(_decay, gam)
            e = _each(jnp.exp, gam)
            kk = _each(lambda k_: _mm16(k_, k_, "nt"), k)
            X = _each(lambda kk_, D_, b_: -(jnp.where(strict, kk_ * D_, 0.0) * b_[:, :CHUNK]), kk, D, bb)
            R = _each(lambda x: eye + x, X)
            for _ in range(5):
                X = _each(lambda x: _mm(x, x), X)
                R = _each(lambda r, x: r + _mm(r, x), R, X)
            u = _each(lambda r, b_, v_: _mm(r, b_ * v_), R, bb, v)
            w = _each(lambda r, b_, e_, k_: _mm(r, b_ * e_ * k_), R, bb, e, k)
            qk = _each(lambda q_, k_, D_: _mm16(q_, k_, "nt") * D_, q, k, D)
            for i, c in enumerate(cs):
                glast = gam[i][CHUNK - 1:CHUNK, :]
                u_ref[rows[i], :] = u[i]
                w_ref[rows[i], :] = w[i]
                qd_ref[rows[i], :] = e[i] * q[i]
                kd_ref[rows[i], :] = jnp.exp(glast - gam[i]) * k[i]
                qk_ref[0, c] = qk[i]
                ti_ref[0, c] = R[i]
                gl_ref[0, c] = jnp.exp(glast)

    full = jax.ShapeDtypeStruct((T, H * LANE), F32)
    sqs = jax.ShapeDtypeStruct((H, N, CHUNK, CHUNK), F32)
    return pl.pallas_call(
        body, name=name, grid=(H, N // cpb),
        in_specs=[col(0), col(H), col(2 * H), bc, bc],
        out_specs=[col(0)] * 4 + [sq, sq, vec],
        out_shape=[full] * 4 + [sqs, sqs, jax.ShapeDtypeStruct((H, N, 1, LANE), F32)],
        compiler_params=_params(("parallel", "parallel")),
    )(qkv, qkv, qkv, gamB, bB)


def _scan_specs(H, N, cpb, hb, rev):
    nbk = N // cpb
    blk = (lambda n: nbk - 1 - n) if rev else (lambda n: n)
    col = pl.BlockSpec((cpb * CHUNK, hb * LANE), lambda h, n: (blk(n), h))
    sq = pl.BlockSpec((hb, cpb, CHUNK, CHUNK), lambda h, n: (h, blk(n), 0, 0))
    vec = pl.BlockSpec((hb, cpb, 1, LANE), lambda h, n: (h, blk(n), 0, 0))
    st = pl.BlockSpec((hb, cpb, HEAD, HEAD), lambda h, n: (h, blk(n), 0, 0))
    return col, sq, vec, st


def _delta_scan_fwd(u, w, qd, kd, qk, gl, H, *, name):
    T = u.shape[0]
    N = T // CHUNK
    cpb = _tile(N, CPB_SCAN, 4)
    hb = min(GRP, H)
    col, sq, vec, st = _scan_specs(H, N, cpb, hb, False)
    lanes = [slice(j * LANE, (j + 1) * LANE) for j in range(hb)]
    heads = list(range(hb))

    def body(u_ref, w_ref, qd_ref, kd_ref, qk_ref, gl_ref, o_ref, vn_ref, ss_ref, s_scr):
        @pl.when(pl.program_id(1) == 0)
        def _():
            s_scr[...] = jnp.zeros_like(s_scr)

        def step(c, states):
            rows = pl.ds(pl.multiple_of(c * CHUNK, CHUNK), CHUNK)
            S = list(states)
            for j in heads:
                ss_ref[j, c] = S[j]
            wS = _each(lambda ln, s: _mm16(w_ref[rows, ln], s), lanes, S)
            qS = _each(lambda ln, s: _mm16(qd_ref[rows, ln], s), lanes, S)
            vn = _each(lambda ln, ws: u_ref[rows, ln] - ws, lanes, wS)
            o = _each(lambda j, qs, vn_: qs + _mm16(qk_ref[j, c], vn_), heads, qS, vn)
            new = _each(lambda j, ln, s, vn_: s * gl_ref[j, c] + _mm16(kd_ref[rows, ln], vn_, "tn"),
                        heads, lanes, S, vn)
            for j in heads:
                o_ref[rows, lanes[j]] = o[j]
                vn_ref[rows, lanes[j]] = vn[j]
            return tuple(new)
        out = lax.fori_loop(0, cpb, step, tuple(s_scr[j] for j in heads))
        for j in heads:
            s_scr[j] = out[j]

    full = jax.ShapeDtypeStruct((T, H * LANE), F32)
    return pl.pallas_call(
        body, name=name, grid=(H // hb, N // cpb),
        in_specs=[col] * 4 + [sq, vec],
        out_specs=[col, col, st],
        out_shape=[full, full, jax.ShapeDtypeStruct((H, N, HEAD, HEAD), F32)],
        scratch_shapes=[pltpu.VMEM((hb, HEAD, HEAD), F32)],
        compiler_params=_params(("parallel", "arbitrary")),
    )(u, w, qd, kd, qk, gl)


def _delta_scan_bwd(do, w, qd, kd, vn, qk, gl, ss, H, *, name):
    T = do.shape[0]
    N = T // CHUNK
    cpb = _tile(N, CPB_SCAN, 4)
    hb = min(GRP, H)
    col, sq, vec, st = _scan_specs(H, N, cpb, hb, True)
    lanes = [slice(j * LANE, (j + 1) * LANE) for j in range(hb)]
    heads = list(range(hb))

    def body(do_ref, w_ref, qd_ref, kd_ref, vn_ref, qk_ref, gl_ref, ss_ref,
             du_ref, dw_ref, dqd_ref, dkd_ref, dqk_ref, dgl_ref, ds_scr):
        @pl.when(pl.program_id(1) == 0)
        def _():
            ds_scr[...] = jnp.zeros_like(ds_scr)

        def step(i, dstates):
            c = cpb - 1 - i
            rows = pl.ds(pl.multiple_of(c * CHUNK, CHUNK), CHUNK)
            dS = list(dstates)
            S = [ss_ref[j, c] for j in heads]
            dov = [do_ref[rows, ln] for ln in lanes]
            vnv = [vn_ref[rows, ln] for ln in lanes]
            a1 = _each(lambda j, d_: _mm16(qk_ref[j, c], d_, "tn"), heads, dov)
            a2 = _each(lambda ln, ds: _mm16(kd_ref[rows, ln], ds), lanes, dS)
            dvn = _each(lambda x, y: x + y, a1, a2)
            dqd = _each(lambda d_, s: _mm16(d_, s, "nt"), dov, S)
            dkd = _each(lambda v_, ds: _mm16(v_, ds, "nt"), vnv, dS)
            dqk = _each(lambda d_, v_: _mm16(d_, v_, "nt"), dov, vnv)
            dw = _each(lambda dv_, s: -_mm16(dv_, s, "nt"), dvn, S)
            b1 = _each(lambda ln, d_: _mm16(qd_ref[rows, ln], d_, "tn"), lanes, dov)
            b2 = _each(lambda ln, dv_: _mm16(w_ref[rows, ln], dv_, "tn"), lanes, dvn)
            new = _each(lambda j, x, y, ds: x + ds * gl_ref[j, c] - y, heads, b1, b2, dS)
            for j in heads:
                du_ref[rows, lanes[j]] = dvn[j]
                dw_ref[rows, lanes[j]] = dw[j]
                dqd_ref[rows, lanes[j]] = dqd[j]
                dkd_ref[rows, lanes[j]] = dkd[j]
                dqk_ref[j, c] = dqk[j]
                dgl = jnp.sum(jnp.sum(dS[j] * S[j], axis=1, keepdims=True), axis=0, keepdims=True)
                dgl_ref[j, c] = jnp.broadcast_to(dgl, (1, LANE))
            return tuple(new)
        out = lax.fori_loop(0, cpb, step, tuple(ds_scr[j] for j in heads))
        for j in heads:
            ds_scr[j] = out[j]

    full = jax.ShapeDtypeStruct((T, H * LANE), F32)
    return pl.pallas_call(
        body, name=name, grid=(H // hb, N // cpb),
        in_specs=[col] * 5 + [sq, vec, st],
        out_specs=[col] * 4 + [sq, vec],
        out_shape=[full] * 4 + [jax.ShapeDtypeStruct((H, N, CHUNK, CHUNK), F32), jax.ShapeDtypeStruct((H, N, 1, LANE), F32)],
        scratch_shapes=[pltpu.VMEM((hb, HEAD, HEAD), F32)],
        compiler_params=_params(("parallel", "arbitrary")),
    )(do, w, qd, kd, vn, qk, gl, ss)


def _delta_prep_bwd(qkv, gamB, bB, ti, u, w, qk, du, dw, dqd, dkd, dqk, dgl, H, *, name):
    T = qkv.shape[0]
    N = T // CHUNK
    cpb = _tile(N, CPB, 8)
    grp = min(GRP, cpb)
    col, bc, sq, vec = _delta_specs(T, H, cpb)

    def body(q_ref, k_ref, v_ref, g_ref, b_ref, ti_ref, u_ref, w_ref, qk_ref,
             du_ref, dw_ref, dqd_ref, dkd_ref, dqk_ref, dgl_ref,
             dq_ref, dk_ref, dv_ref, dg_ref, db_ref):
        ones = jnp.ones((CHUNK, LANE), F32)
        strict = _tri(strict=True)
        last = lax.broadcasted_iota(jnp.int32, (CHUNK, LANE), 0) == CHUNK - 1
        lsum = lambda x: jnp.sum(x, axis=-1, keepdims=True)
        for c0 in range(0, cpb, grp):
            cs = list(range(c0, c0 + grp))
            rows = [slice(c * CHUNK, (c + 1) * CHUNK) for c in cs]
            ld = lambda r_: [r_[r, :] for r in rows]
            q, k, v, uv, wv, duv, dwv, dqd_v, dkd_v = (ld(r_) for r_ in (q_ref, k_ref, v_ref, u_ref, w_ref, du_ref, dw_ref, dqd_ref, dkd_ref))
            bb = [b_ref[0, r, :] for r in rows]
            gam = [g_ref[0, r, :] for r in rows]
            Ti = [ti_ref[0, c] for c in cs]
            QK = [qk_ref[0, c] for c in cs]
            dqk_v = [dqk_ref[0, c] for c in cs]
            D = _each(_decay, gam)
            e = _each(jnp.exp, gam)
            glast = [g_[CHUNK - 1:CHUNK, :] for g_ in gam]
            eL = _each(lambda gl_, g_: jnp.exp(gl_ - g_), glast, gam)
            kk = _each(lambda k_: _mm16(k_, k_, "nt"), k)
            KKD = _each(lambda kk_, D_: jnp.where(strict, kk_ * D_, 0.0), kk, D)
            dru = _each(lambda t, d_: _mm(t, d_, "tn"), Ti, duv)
            drw = _each(lambda t, d_: _mm(t, d_, "tn"), Ti, dwv)
            l1 = _each(lambda a, b: _mm(a, b, "nt"), dru, uv)
            l2 = _each(lambda a, b: _mm(a, b, "nt"), drw, wv)
            dL = _each(lambda a, b: jnp.where(strict, -(a + b), 0.0), l1, l2)
            Mm = _each(lambda dl, b_: dl * b_[:, :CHUNK], dL, bb)
            dKK = _each(lambda m_, D_: m_ * D_, Mm, D)
            dQK = _each(lambda a, D_: a * D_, dqk_v, D)
            P = _each(lambda m_, kkd, a, qk_: m_ * kkd + a * qk_, Mm, KKD, dqk_v, QK)
            q1 = _each(lambda a, k_: _mm16(a, k_), dQK, k)
            k1 = _each(lambda a, q_: _mm16(a, q_, "tn"), dQK, q)
            k2 = _each(lambda a, k_: _mm16(a, k_), dKK, k)
            k3 = _each(lambda a, k_: _mm16(a, k_, "tn"), dKK, k)
            s1 = _each(lambda dl, kkd: _mm(dl * kkd, ones), dL, KKD)
            p1 = _each(lambda p_: _mm(p_, ones), P)
            p2 = _each(lambda p_: _mm(p_, ones, "tn"), P)
            for i, c in enumerate(cs):
                r = rows[i]
                bek = bb[i] * e[i]
                kdv = eL[i] * k[i]
                dq_ref[r, :] = q1[i] + e[i] * dqd_v[i]
                dk_ref[r, :] = k1[i] + k2[i] + k3[i] + bek * drw[i] + eL[i] * dkd_v[i]
                dv_ref[r, :] = bb[i] * dru[i]
                db_ref[0, r, :] = s1[i] + lsum(dru[i] * v[i]) + lsum(drw[i] * e[i] * k[i])
                dgam = (p1[i] - p2[i] + lsum(drw[i] * bek * k[i]) + lsum(dqd_v[i] * e[i] * q[i])
                        - lsum(dkd_v[i] * kdv))
                xlast = jnp.sum(lsum(dkd_v[i] * kdv), axis=0, keepdims=True) + jnp.exp(glast[i]) * dgl_ref[0, c]
                dg_ref[0, r, :] = dgam + jnp.where(last, xlast, 0.0)

    full = jax.ShapeDtypeStruct((T, H * LANE), F32)
    bcs = jax.ShapeDtypeStruct((H, T, LANE), F32)
    return pl.pallas_call(
        body, name=name, grid=(H, N // cpb),
        in_specs=[col(0), col(H), col(2 * H), bc, bc, sq, col(0), col(0), sq, col(0), col(0), col(0), col(0), sq, vec],
        out_specs=[col(0), col(0), col(0), bc, bc],
        out_shape=[full, full, full, bcs, bcs],
        compiler_params=_params(("parallel", "parallel")),
    )(qkv, qkv, qkv, gamB, bB, ti, u, w, qk, du, dw, dqd, dkd, dqk, dgl)


def _adam(parts, w, m, v, *, name, own=None, me=None):
    P, R, C = parts.shape
    if R > 256 and R % 8:
        tr, tc = R, _tile(C, 256)
    else:
        tr, tc = _tile(R, 256, 8), C
    n_own = 0 if own is None else 2

    def body(*refs):
        p_ref, w_ref, m_ref, v_ref, g_ref, d_ref, nm_ref, nv_ref = refs[n_own:]
        g = None
        for i in range(P):
            t = p_ref[i].astype(F32)
            if n_own:
                t = jnp.where(refs[0][0] == i, refs[1][...].astype(F32), t)
            g = t if g is None else g + t
        mn = ADAM_B1 * m_ref[...] + (1.0 - ADAM_B1) * g
        vn = ADAM_B2 * v_ref[...] + (1.0 - ADAM_B2) * (g * g)
        m_hat = mn / (1.0 - ADAM_B1 ** ADAM_STEP)
        v_hat = vn / (1.0 - ADAM_B2 ** ADAM_STEP)
        g_ref[...] = g
        d_ref[...] = -ADAM_LR * (m_hat / (jnp.sqrt(v_hat) + ADAM_EPS) + ADAM_WD * w_ref[...])
        nm_ref[...] = mn
        nv_ref[...] = vn

    blk = pl.BlockSpec((tr, tc), lambda i, j: (i, j))
    return pl.pallas_call(
        body, name=name, grid=(R // tr, C // tc),
        in_specs=[pl.BlockSpec(memory_space=pltpu.SMEM), blk][:n_own] + [pl.BlockSpec((P, tr, tc), lambda i, j: (0, i, j)), blk, blk, blk],
        out_specs=[blk] * 4, out_shape=[jax.ShapeDtypeStruct((R, C), F32)] * 4,
        compiler_params=_params(("parallel", "parallel")),
    )(*([me, own] if n_own else []), parts, w, m, v)


def _mesh_pos():
    return lax.axis_index("x"), lax.axis_index("y"), lax.axis_index("c")


def _peer(k):
    x, y, c = _mesh_pos()
    px, py, pc = x ^ ((k >> 2) & 1), y ^ ((k >> 1) & 1), c ^ (k & 1)
    return (px, py, pc), 4 * px + 2 * py + pc


def _exchange(arrays, scatter, *, name, after=None):
    n = len(arrays)
    n_in = n if after is None else n + 1
    blocks = [a.shape[1:] if scatter else a.shape for a in arrays]

    def body(*refs):
        srcs, dsts = refs[:n], refs[n_in:n_in + n]
        send_sems, recv_sems, local_sems = refs[n_in + n:]
        x, y, c = _mesh_pos()
        me = 4 * x + 2 * y + c
        local, sends = [], []
        for a in range(n):
            cp = pltpu.make_async_copy(srcs[a].at[me] if scatter else srcs[a], dsts[a].at[me], local_sems.at[a])
            cp.start()
            local.append(cp)
            for k in range(1, N_DEV):
                dev, idx = _peer(k)
                cp = pltpu.make_async_remote_copy(
                    src_ref=srcs[a].at[idx] if scatter else srcs[a], dst_ref=dsts[a].at[me],
                    send_sem=send_sems.at[a * N_DEV + k], recv_sem=recv_sems.at[a * N_DEV + k],
                    device_id=dev, device_id_type=MESH)
                cp.start()
                sends.append(cp)
        for a in range(n):
            for k in range(1, N_DEV):
                dev, idx = _peer(k)
                pltpu.make_async_remote_copy(
                    src_ref=srcs[a].at[idx] if scatter else srcs[a], dst_ref=dsts[a].at[idx],
                    send_sem=send_sems.at[a * N_DEV + k], recv_sem=recv_sems.at[a * N_DEV + k],
                    device_id=dev, device_id_type=MESH).wait_recv()
        for cp in sends:
            cp.wait_send()
        for cp in local:
            cp.wait()

    anyspec = pl.BlockSpec(memory_space=pl.ANY)
    return pl.pallas_call(
        body, name=name, in_specs=[anyspec] * n_in, out_specs=[anyspec] * n,
        out_shape=[jax.ShapeDtypeStruct((N_DEV,) + tuple(b), a.dtype) for a, b in zip(arrays, blocks)],
        scratch_shapes=[pltpu.SemaphoreType.DMA((n * N_DEV,)), pltpu.SemaphoreType.DMA((n * N_DEV,)),
                        pltpu.SemaphoreType.DMA((n,))],
    )(*arrays, *([] if after is None else [after]))


_ANY = pl.BlockSpec(memory_space=pl.ANY)
_SEM = pl.BlockSpec(memory_space=pltpu.SEMAPHORE)
_EFFECT = pltpu.SideEffectType.DATAFLOW_SIDE_EFFECTING


def _in_hbm(a):
    return pltpu.with_memory_space_constraint(a, pltpu.HBM)


def _split_copy(src, land, send, recv, k, me, scatter, landed):
    dev, idx = _peer(k)
    return pltpu.make_async_remote_copy(
        src_ref=src.at[idx] if scatter else src, dst_ref=land.at[idx if landed else me],
        send_sem=send.at[k], recv_sem=recv.at[k], device_id=dev, device_id_type=MESH)


ALL_PEERS = tuple(range(1, N_DEV))
SIBLING = 1
SAME_CORE = (2, 4, 6)


def _split_start(srcs, lands, scatter, *, name, relations=None):
    n = len(srcs)
    relations = relations or [ALL_PEERS] * n

    def body(*refs):
        src, land, send, recv, token = refs[:n], refs[n:2 * n], refs[2 * n:3 * n], refs[3 * n:4 * n], refs[-1]
        x, y, c = _mesh_pos()
        me = 4 * x + 2 * y + c
        for a in range(n):
            for k in relations[a]:
                _split_copy(src[a], land[a], send[a], recv[a], k, me, scatter, False).start()
        token[...] = jnp.zeros_like(token)

    outs = pl.pallas_call(
        body, name=name,
        out_shape=[pltpu.SemaphoreType.DMA((N_DEV,))] * (2 * n) + [pltpu.HBM(t.shape, t.dtype) for t in list(srcs) + list(lands)]
        + [jax.ShapeDtypeStruct((8, LANE), F32)],
        in_specs=[_ANY] * (2 * n), out_specs=[_SEM] * (2 * n) + [_ANY] * (2 * n) + [pl.BlockSpec(memory_space=pltpu.VMEM)],
        input_output_aliases={i: 2 * n + i for i in range(2 * n)},
        compiler_params=pltpu.CompilerParams(has_side_effects=_EFFECT),
    )(*[_in_hbm(t) for t in list(srcs) + list(lands)])
    handles = [(outs[a], outs[n + a], outs[2 * n + a], outs[3 * n + a]) for a in range(n)]
    return handles, outs[-1]


def _split_wait(handle, after, scatter, *, name):
    send, recv, src_thru, land_thru = handle

    def body(src_ref, land_ref, send_ref, recv_ref, after_ref, src_out, land_out):
        x, y, c = _mesh_pos()
        me = 4 * x + 2 * y + c
        for k in range(1, N_DEV):
            cp = _split_copy(src_ref, land_ref, send_ref, recv_ref, k, me, scatter, True)
            cp.wait_send()
            cp.wait_recv()

    return pl.pallas_call(
        body, name=name,
        out_shape=(pltpu.HBM(src_thru.shape, src_thru.dtype), pltpu.HBM(land_thru.shape, land_thru.dtype)),
        in_specs=(_ANY, _ANY, _SEM, _SEM, _ANY), out_specs=(_ANY, _ANY), input_output_aliases={0: 0, 1: 1},
        compiler_params=pltpu.CompilerParams(has_side_effects=_EFFECT),
    )(src_thru, land_thru, send, recv, after)[1]


def _forward_copy(land, fsend, frecv, k, landed):
    x, y, c = _mesh_pos()
    _, idx = _peer(k | SIBLING if landed else k)
    return pltpu.make_async_remote_copy(src_ref=land.at[idx], dst_ref=land.at[idx], send_sem=fsend.at[k],
                                        recv_sem=frecv.at[k], device_id=(x, y, 1 - c), device_id_type=MESH)


def _gather_forward(handle, after, *, name):
    send, recv, src_thru, land_thru = handle

    def body(src_ref, land_ref, send_ref, recv_ref, after_ref, src_out, land_out, fsend, frecv):
        x, y, c = _mesh_pos()
        me = 4 * x + 2 * y + c
        for k in SAME_CORE:
            _split_copy(src_ref, land_ref, send_ref, recv_ref, k, me, False, True).wait_recv()
            _forward_copy(land_ref, fsend, frecv, k, False).start()

    src2, land2, fsend, frecv = pl.pallas_call(
        body, name=name,
        out_shape=(pltpu.HBM(src_thru.shape, src_thru.dtype), pltpu.HBM(land_thru.shape, land_thru.dtype),
                   pltpu.SemaphoreType.DMA((N_DEV,)), pltpu.SemaphoreType.DMA((N_DEV,))),
        in_specs=(_ANY, _ANY, _SEM, _SEM, _ANY), out_specs=(_ANY, _ANY, _SEM, _SEM), input_output_aliases={0: 0, 1: 1},
        compiler_params=pltpu.CompilerParams(has_side_effects=_EFFECT),
    )(src_thru, land_thru, send, recv, after)
    return (send, recv, src2, land2), (fsend, frecv)


def _gather_wait_two_level(handle, fwd, *, name):
    send, recv, src_thru, land_thru = handle
    fsend, frecv = fwd

    def body(src_ref, land_ref, send_ref, recv_ref, fsend_ref, frecv_ref, src_out, land_out):
        x, y, c = _mesh_pos()
        me = 4 * x + 2 * y + c
        for k in (SIBLING,) + SAME_CORE:
            _split_copy(src_ref, land_ref, send_ref, recv_ref, k, me, False, True).wait_send()
        _split_copy(src_ref, land_ref, send_ref, recv_ref, SIBLING, me, False, True).wait_recv()
        for k in SAME_CORE:
            _forward_copy(land_ref, fsend_ref, frecv_ref, k, False).wait_send()
            _forward_copy(land_ref, fsend_ref, frecv_ref, k, True).wait_recv()

    return pl.pallas_call(
        body, name=name,
        out_shape=(pltpu.HBM(src_thru.shape, src_thru.dtype), pltpu.HBM(land_thru.shape, land_thru.dtype)),
        in_specs=(_ANY, _ANY, _SEM, _SEM, _SEM, _SEM), out_specs=(_ANY, _ANY), input_output_aliases={0: 0, 1: 1},
        compiler_params=pltpu.CompilerParams(has_side_effects=_EFFECT),
    )(src_thru, land_thru, send, recv, fsend, frecv)[1]


def _local_step(x, p, tgt, S, wt, conv, emit):
    T, D = x.shape
    CW = DNW = D // 2
    H = DNW // HEAD
    nA, nD = CW // LANE, DNW // LANE
    qkv_off, z_off, ab_off = 3 * nA, 3 * nA + 3 * nD, 3 * nA + 4 * nD
    alog = jnp.pad(S["a_log"], ((0, 0), (0, LANE - H)))
    dtb = jnp.pad(S["dt_bias"], ((0, 0), (0, LANE - H)))

    h1 = _rms_fwd(x, S["g_mix"], name="rms1_fwd")
    w_in, cv = wt("w_in", h1), conv(h1)
    proj = _matmul(h1, w_in, "nt", name="mm_in")
    y_a = _group_a_fwd(proj, cv["conv_a"], CW, name="group_a_fwd")
    qkv = _qkv_fwd(proj, cv["conv_qkv"], qkv_off, H, name="qkv_fwd")
    gb, gamc = _gates_fwd(proj, alog, dtb, ab_off, H, name="gates_fwd")
    bcast = lambda cols: jnp.broadcast_to(cols.T[:, :, None], (H, T, LANE))
    gamB, bB = bcast(gamc[:, :H]), bcast(gb[:, H:2 * H])
    u, w, qd, kd, qk, ti, gl = _delta_prep_fwd(qkv, gamB, bB, H, name="delta_prep_fwd")
    o, vn, ss = _delta_scan_fwd(u, w, qd, kd, qk, gl, H, name="delta_scan_fwd")
    y_b = _gated_norm_fwd(o, proj, S["dn_g"], z_off, name="gated_norm_fwd")
    ycat = jnp.concatenate([y_a, y_b], axis=1)
    w_out = wt("w_out", ycat)
    rows = dict(tm=ROW_TILE, tn=D)
    x1, h2 = _matmul(ycat, w_out, "nn", name="mm_out", out_dtypes=(F32, BF16), epilogue=_epi_residual_rms,
                     extras=(x,), vec_extras=(S["g_ffn"],), **rows)
    w_up = wt("w_up", h2)
    up_pre = _matmul(h2, w_up, "nn", name="mm_up", b_shards=True, tn=SHARD_TILE)
    act = _ffn_act_fwd(up_pre, cv["conv_ffn"], name="ffn_act_fwd")
    w_down = wt("w_down", act)
    x2 = _matmul(act, w_down, "nn", name="mm_down", epilogue=lambda acc, r: (acc + r,), extras=(x1,), tk=LONG_K)
    h3 = _rms_fwd(x2, S["g_ple"], name="rms3_fwd")
    w_pp, w_pg = wt("w_pp", h3), wt("w_pg", h3)
    pp = _matmul(p, w_pp, "nn", name="mm_pp", b_shards=True)

    def ple_epi(acc, x2r, ppr):
        s = jax.nn.sigmoid(acc)
        return x2r + s * ppr, s

    x3, sg = _matmul(h3, w_pg, "nn", name="mm_pg", out_dtypes=(F32, F32), epilogue=ple_epi, extras=(x2, pp), tm=512)
    dx3, dg_final, loss = _final_loss(x3, S["g_final"], tgt, name="final_loss")

    G = {"g_final": dg_final}
    dpg, dpp = _ple_bwd(dx3, pp, sg, name="ple_bwd")
    tok = emit({"w_pp": _matmul(p, dpp, "tn", name="mm_dwpp", out_dtypes=(BF16,), out_shards=True, tk=LONG_K),
                "w_pg": _matmul(h3, dpg, "tn", name="mm_dwpg", out_dtypes=(BF16,), tk=LONG_K)})
    bwd = dict(out_dtypes=(F32, BF16), epilogue=_epi_rms_bwd(2), n_vec=1, **rows)
    dx2, dx2b, G["g_ple"] = _matmul(dpg, w_pg, "nt", name="mm_dh3", after=tok, extras=(x2, dx3),
                                    vec_extras=(S["g_ple"],), **bwd)
    tok = emit({"w_down": _matmul(act, dx2b, "tn", name="mm_dwdown", out_dtypes=(BF16,), tk=LONG_K)})
    dact = _matmul(dx2b, w_down, "nt", name="mm_dact", after=tok)
    dup_g, dup_v, dcf_g, dcf_v = _ffn_act_bwd(up_pre, cv["conv_ffn"], dact, name="ffn_act_bwd")
    G["conv_ffn"] = jnp.concatenate([dcf_g, dcf_v], axis=1)
    dup = jnp.concatenate([dup_g, dup_v], axis=1)
    tok = emit({"w_up": _matmul(h2, dup, "tn", name="mm_dwup", out_dtypes=(BF16,), out_shards=True, tn=SHARD_TILE, tk=LONG_K)})
    dh2 = _matmul(dup, w_up, "nt", name="mm_dh2", after=tok, b_shards=True, tk=2 * SHARD_TILE)
    dx1, dx1b, G["g_ffn"] = _rms_bwd(x1, S["g_ffn"], dh2, dx2, name="rms2_bwd")
    tok = emit({"w_out": _matmul(ycat, dx1b, "tn", name="mm_dwout", out_dtypes=(BF16,), tk=LONG_K)})
    dycat = _matmul(dx1b, w_out, "nt", name="mm_dycat", after=tok)
    do, dz, G["dn_g"] = _gated_norm_bwd(o, proj, S["dn_g"], dycat, z_off, nA, name="gated_norm_bwd")
    du, dw, dqd, dkd, dqk, dgl = _delta_scan_bwd(do, w, qd, kd, vn, qk, gl, ss, H, name="delta_scan_bwd")
    dq, dk, dv, dgB, dbB = _delta_prep_bwd(qkv, gamB, bB, ti, u, w, qk, du, dw, dqd, dkd, dqk, dgl, H,
                                           name="delta_prep_bwd")
    dgb = jnp.pad(jnp.concatenate([dgB[:, :, 0].T, dbB[:, :, 0].T], axis=1), ((0, 0), (0, LANE - 2 * H)))
    dab, dal, ddt = _gates_bwd(proj, alog, dtb, dgb, ab_off, H, name="gates_bwd")
    G["a_log"], G["dt_bias"] = dal[:, :H], ddt[:, :H]
    dqkv, G["conv_qkv"] = _qkv_bwd(proj, cv["conv_qkv"], dq, dk, dv, qkv_off, H, name="qkv_bwd")
    dax, dab_, dac, G["conv_a"] = _group_a_bwd(proj, cv["conv_a"], dycat, CW, name="group_a_bwd")
    in_p = w_in.shape[0]
    dproj = jnp.concatenate([dax, dab_, dac, dqkv, dz, dab, jnp.zeros((T, in_p - (ab_off + 1) * LANE), BF16)], axis=1)
    tok = emit({"w_in": _matmul(dproj, h1, "tn", name="mm_dwin", out_dtypes=(BF16,), tk=LONG_K)})
    dh1 = _matmul(dproj, w_in, "nn", name="mm_dh1", after=tok, tk=LONG_K)
    grad_x, _, G["g_mix"] = _rms_bwd(x, S["g_mix"], dh1, dx1, name="rms1_bwd")
    return loss, grad_x, G


def _col_sharded(landed):
    _, R, C = landed.shape
    return jnp.transpose(landed, (1, 0, 2)).reshape(R, N_DEV * C)


def kernel(x, p, norm_mix_g, w_in, conv_a_w, conv_qkv_w, a_log, dt_bias, dn_norm_g, w_out, norm_ffn_g, w_up, conv_ffn_w, w_down, norm_ple_g, w_ple_gate, w_ple_proj, final_norm_g, loss_target, m_norm_mix_g, m_w_in, m_conv_a_w, m_conv_qkv_w, m_a_log, m_dt_bias, m_dn_norm_g, m_w_out, m_norm_ffn_g, m_w_up, m_conv_ffn_w, m_w_down, m_norm_ple_g, m_w_ple_gate, m_w_ple_proj, m_final_norm_g, v_norm_mix_g, v_w_in, v_conv_a_w, v_conv_qkv_w, v_a_log, v_dt_bias, v_dn_norm_g, v_w_out, v_norm_ffn_g, v_w_up, v_conv_ffn_w, v_w_down, v_norm_ple_g, v_w_ple_gate, v_w_ple_proj, v_final_norm_g):
    T, D = x.shape[1], x.shape[2]
    xd, _, cd = _mesh_pos()
    me = 4 * xd + 2 * lax.axis_index("y") + cd

    conv_sh = [conv_a_w[0], conv_qkv_w[0], conv_ffn_w[0]]
    conv_n = [c.size for c in conv_sh]
    pack_rows = -(-sum(conv_n) // LANE)
    conv_pack = jnp.pad(jnp.concatenate([c.reshape(-1) for c in conv_sh]), (0, pack_rows * LANE - sum(conv_n))).reshape(pack_rows, LANE)
    names = ["w_in", "conv", "w_out", "w_up", "w_down", "w_pg", "w_pp"]
    tr_ = lambda t: jnp.swapaxes(t, 1, 2)
    shards = [w_in[0].T.astype(BF16), conv_pack, w_out[0].astype(BF16), w_up[0].astype(BF16), w_down[0].astype(BF16),
              w_ple_gate[0].astype(BF16), w_ple_proj[0].astype(BF16)]
    empty_slots = lambda blocks: [lax.empty((N_DEV,) + tuple(b.shape), b.dtype) for b in blocks]
    handles, tok0 = _split_start(shards, empty_slots(shards), False, name="gather_start",
                                 relations=[(SIBLING,) + SAME_CORE] + [ALL_PEERS] * (len(shards) - 1))
    handle = dict(zip(names, handles))
    own = dict(zip(names, shards))
    in_cols = N_DEV * w_in.shape[2]
    in_p = (in_cols // LANE) * LANE + AB_PAD
    in_place = {"w_up", "w_pp"}

    def gathered(name, after):
        if name == "w_in":
            passed, fwd = _gather_forward(handle[name], after, name="gather_forward_w_in")
            landed = _gather_wait_two_level(passed, fwd, name="gather_wait_w_in")
        else:
            landed = _split_wait(handle[name], after, False, name="gather_wait_" + name)
        return lax.dynamic_update_index_in_dim(landed, own[name], me, 0)

    def wt(name, after):
        landed = gathered(name, after)
        if name in in_place:
            return landed
        full = landed.reshape(-1, D)
        return jnp.pad(full, ((0, in_p - in_cols), (0, 0))) if name == "w_in" else full

    def conv(after):
        flat = gathered("conv", after).reshape(N_DEV, pack_rows * LANE)
        out, o_ = {}, 0
        for nm, c, n_ in zip(("conv_a", "conv_qkv", "conv_ffn"), conv_sh, conv_n):
            out[nm] = _col_sharded(flat[:, o_:o_ + n_].reshape((N_DEV,) + c.shape))
            o_ += n_
        return out

    pending, mine = {}, {}

    def emit(grads):
        parts = [g if nm in in_place else (g[:in_cols] if nm == "w_in" else g).reshape(N_DEV, -1, D)
                 for nm, g in grads.items()]
        hs, tok = _split_start(parts, empty_slots([q[0] for q in parts]), True, name="scatter_start_" + "_".join(grads))
        pending.update(zip(grads, hs))
        mine.update({nm: lax.dynamic_index_in_dim(q, me, 0, keepdims=False) for nm, q in zip(grads, parts)})
        return tok

    S = {
        "g_mix": norm_mix_g + tok0[0, 0], "a_log": a_log, "dt_bias": dt_bias, "dn_g": dn_norm_g, "g_ffn": norm_ffn_g,
        "g_ple": norm_ple_g, "g_final": final_norm_g.reshape(1, D),
    }

    loss_v, grad_x, G = _local_step(x[0], p[0, 0], loss_target[0], S, wt, conv, emit)
    loss = lax.psum(loss_v[0, 0], ("x", "y", "c"))

    small_names = ["g_mix", "g_ffn", "g_ple", "g_final", "dn_g", "a_log", "dt_bias", "conv_a", "conv_qkv", "conv_ffn"]
    small_rows, pieces = [], []
    for nm in small_names:
        g_ = G[nm].reshape(-1)
        r_ = -(-g_.size // (8 * LANE)) * 8
        small_rows.append(r_)
        pieces.append(jnp.pad(g_, (0, r_ * LANE - g_.size)).reshape(r_, LANE))
    landed = {nm: _split_wait(h_, grad_x, True, name="scatter_wait_" + nm) for nm, h_ in pending.items() if nm != "w_in"}

    def adam(parts, w_, m_, v_, nm, own_=None):
        shp = w_.shape
        w2, m2, v2 = (t.reshape(parts.shape[1:]) for t in (w_, m_, v_))
        kw = {} if own_ is None else {"own": own_, "me": me.astype(jnp.int32).reshape(1)}
        return tuple(t.reshape(shp) for t in _adam(parts, w2, m2, v2, name="adam_" + nm, **kw))

    big = {
        "w_up": adam(landed["w_up"], w_up, m_w_up, v_w_up, "w_up", mine["w_up"]),
        "w_down": adam(landed["w_down"], w_down, m_w_down, v_w_down, "w_down", mine["w_down"]),
        "w_out": adam(landed["w_out"], w_out, m_w_out, v_w_out, "w_out", mine["w_out"]),
        "w_pg": adam(landed["w_pg"], w_ple_gate, m_w_ple_gate, v_w_ple_gate, "w_ple_gate", mine["w_pg"]),
        "w_pp": adam(landed["w_pp"], w_ple_proj, m_w_ple_proj, v_w_ple_proj, "w_ple_proj", mine["w_pp"]),
    }
    big_done = sum(r[1].reshape(-1)[:1] for r in big.values())
    (small_l,) = _exchange([jnp.concatenate(pieces, axis=0)], False, name="gather_small_grads", after=big_done)

    def small_parts(nm):
        i = small_names.index(nm)
        r0 = sum(small_rows[:i])
        shp = G[nm].shape
        return small_l[:, r0:r0 + small_rows[i], :].reshape(N_DEV, -1)[:, :G[nm].size].reshape((N_DEV,) + shp)

    def conv_parts(nm, shard):
        full = small_parts(nm)
        C = shard.shape[-1]
        return lax.dynamic_slice_in_dim(full, me * C, C, axis=2)

    res = [
        adam(small_parts("g_mix"), norm_mix_g, m_norm_mix_g, v_norm_mix_g, "norm_mix_g"),
        None,
        adam(conv_parts("conv_a", conv_a_w), conv_a_w, m_conv_a_w, v_conv_a_w, "conv_a_w"),
        adam(conv_parts("conv_qkv", conv_qkv_w), conv_qkv_w, m_conv_qkv_w, v_conv_qkv_w, "conv_qkv_w"),
        adam(small_parts("a_log"), a_log, m_a_log, v_a_log, "a_log"),
        adam(small_parts("dt_bias"), dt_bias, m_dt_bias, v_dt_bias, "dt_bias"),
        adam(small_parts("dn_g"), dn_norm_g, m_dn_norm_g, v_dn_norm_g, "dn_norm_g"),
        big["w_out"],
        adam(small_parts("g_ffn"), norm_ffn_g, m_norm_ffn_g, v_norm_ffn_g, "norm_ffn_g"),
        big["w_up"],
        adam(conv_parts("conv_ffn", conv_ffn_w), conv_ffn_w, m_conv_ffn_w, v_conv_ffn_w, "conv_ffn_w"),
        big["w_down"],
        adam(small_parts("g_ple"), norm_ple_g, m_norm_ple_g, v_norm_ple_g, "norm_ple_g"),
        big["w_pg"],
        big["w_pp"],
        adam(small_parts("g_final"), final_norm_g.reshape(1, D), m_final_norm_g.reshape(1, D),
             v_final_norm_g.reshape(1, D), "final_norm_g"),
    ]
    res[-1] = tuple(t.reshape(D) for t in res[-1])
    landed_in = _split_wait(pending["w_in"], res[10][1], True, name="scatter_wait_w_in")
    res[1] = tuple(tr_(t) for t in adam(landed_in, tr_(w_in), tr_(m_w_in), tr_(v_w_in), "w_in", mine["w_in"]))
    grads, deltas, new_m, new_v = zip(*res)
    return (loss, grad_x[None], *grads, *deltas, *new_m, *new_v)
```

```python
import functools

import jax
import jax.numpy as jnp
from jax import lax
from jax.experimental import pallas as pl
from jax.experimental.pallas import tpu as pltpu

F32 = jnp.float32
BF16 = jnp.bfloat16

EPS = 1e-6
CHUNK = 64
HEAD = 128
LANE = 128
N_DEV = 8
AB_PAD = 512

ADAM_LR = 0.001
ADAM_B1 = 0.9
ADAM_B2 = 0.999
ADAM_EPS = 1e-08
ADAM_WD = 0.01
ADAM_STEP = 10

MESH = pl.DeviceIdType.MESH


def _tile(dim, target, align=LANE):
    if dim <= target:
        return dim
    t = (target // align) * align
    while t > align and dim % t:
        t -= align
    assert dim % t == 0, (dim, target)
    return t


def _params(sem, vmem_mb=48):
    return pltpu.CompilerParams(dimension_semantics=sem, vmem_limit_bytes=vmem_mb << 20)


_DN = {"nn": (((1,), (0,)), ((), ())), "nt": (((1,), (1,)), ((), ())), "tn": (((0,), (0,)), ((), ()))}
LONG_K = 4096
SHARD_TILE = 1408


def _matmul(a, b, mode, *, name, out_dtypes=(F32,), epilogue=None, extras=(), vec_extras=(), n_vec=0, after=None,
            a_shards=False, b_shards=False, out_shards=False, tm=1024, tn=1024, tk=2048):
    shard_w = b.shape[2] if b_shards else None
    if b_shards:
        b_rows, b_cols = b.shape[1], b.shape[0] * shard_w
    else:
        b_rows, b_cols = b.shape
    a_w = a.shape[2] if a_shards else None
    a_dims = (a.shape[1], a.shape[0] * a_w) if a_shards else a.shape
    if mode == "nn":
        (M, K), (K2, N) = a_dims, (b_rows, b_cols)
    elif mode == "nt":
        (M, K), (N, K2) = a_dims, (b_rows, b_cols)
    else:
        (K, M), (K2, N) = a_dims, (b_rows, b_cols)
    assert K == K2, (name, a.shape, b.shape)
    tm = _tile(M, tm)
    n_dims = [N] + ([shard_w] if (b_shards and mode != "nt") else []) + ([N // N_DEV] if out_shards else [])
    tn = _tile(min(n_dims), tn)
    assert all(d % tn == 0 for d in n_dims), (name, n_dims, tn)
    grp = 1
    if b_shards and mode == "nt":
        grp = max(g for g in (1, 2, 4, 8) if g <= max(1, tk // shard_w) and (a_w is None or a_w % (g * shard_w) == 0))
    k_dims = [K] + ([shard_w] if (b_shards and mode == "nt") else []) + ([a_w] if a_shards else [])
    tk = grp * shard_w if grp > 1 else _tile(min(k_dims), tk)
    assert K % tk == 0, (name, K, tk)
    nk = K // tk
    n_ex, n_out = len(extras) + len(vec_extras), len(out_dtypes)
    assert n_vec == 0 or tn == N, (name, tn, N)
    dn = _DN[mode]

    n_tok = 0 if after is None else 1

    def body(a_ref, b_ref, *rest):
        rest = rest[n_tok:]
        ex_refs, out_refs, vec_refs = rest[:n_ex], rest[n_ex:n_ex + n_out], rest[n_ex + n_out:n_ex + n_out + n_vec]
        if grp > 1:
            part = sum(lax.dot_general(a_ref[:, s * shard_w:(s + 1) * shard_w].astype(BF16), b_ref[s].astype(BF16), dn,
                                       preferred_element_type=F32) for s in range(grp))
        else:
            part = lax.dot_general(a_ref[...].astype(BF16), b_ref[...].astype(BF16), dn, preferred_element_type=F32)
        first_rows = pl.program_id(0) == 0

        def finish(res):
            outs = (res,) if epilogue is None else epilogue(res, *[e[...] for e in ex_refs])
            for o_ref, val in zip(out_refs, outs[:n_out]):
                o_ref[...] = val.astype(o_ref.dtype)
            for v_ref, val in zip(vec_refs, outs[n_out:]):
                @pl.when(first_rows)
                def _(v_ref=v_ref, val=val):
                    v_ref[...] = val

                @pl.when(jnp.logical_not(first_rows))
                def _(v_ref=v_ref, val=val):
                    v_ref[...] += val

        if nk == 1:
            finish(part)
            return
        acc, k = rest[-1], pl.program_id(2)

        @pl.when(k == 0)
        def _():
            acc[...] = part

        @pl.when(k > 0)
        def _():
            acc[...] += part

        @pl.when(k == nk - 1)
        def _():
            finish(acc[...])

    if a_shards:
        assert mode == "nt" and a_w % tk == 0, (name, mode, a_w, tk)
        per_a = a_w // tk
        a_spec = pl.BlockSpec((None, tm, tk), lambda i, j, k: (lax.div(k, per_a), i, lax.rem(k, per_a)))
    else:
        a_spec = pl.BlockSpec((tk, tm), lambda i, j, k: (k, i)) if mode == "tn" else pl.BlockSpec((tm, tk), lambda i, j, k: (i, k))
    if b_shards and mode != "nt":
        per = shard_w // tn
        b_spec = pl.BlockSpec((None, tk, tn), lambda i, j, k: (lax.div(j, per), k, lax.rem(j, per)))
    elif b_shards and grp > 1:
        b_spec = pl.BlockSpec((grp, tn, shard_w), lambda i, j, k: (k, j, 0))
    elif b_shards:
        per = shard_w // tk
        b_spec = pl.BlockSpec((None, tn, tk), lambda i, j, k: (lax.div(k, per), j, lax.rem(k, per)))
    else:
        b_spec = pl.BlockSpec((tn, tk), lambda i, j, k: (j, k)) if mode == "nt" else pl.BlockSpec((tk, tn), lambda i, j, k: (k, j))
    mn_spec = pl.BlockSpec((tm, tn), lambda i, j, k: (i, j))
    vec_spec = pl.BlockSpec((1, tn), lambda i, j, k: (0, j))
    if out_shards:
        assert not extras
        per_o = (N // N_DEV) // tn
        out_spec = pl.BlockSpec((None, tm, tn), lambda i, j, k: (lax.div(j, per_o), i, lax.rem(j, per_o)))
        out_dims = (N_DEV, M, N // N_DEV)
    else:
        out_spec, out_dims = mn_spec, (M, N)
    outs = pl.pallas_call(
        body, name=name, grid=(M // tm, N // tn, nk),
        in_specs=[a_spec, b_spec] + [pl.BlockSpec((8, LANE), lambda i, j, k: (0, 0))] * n_tok
        + [mn_spec] * len(extras) + [vec_spec] * len(vec_extras),
        out_specs=[out_spec] * n_out + [vec_spec] * n_vec,
        out_shape=[jax.ShapeDtypeStruct(out_dims, dt) for dt in out_dtypes] + [jax.ShapeDtypeStruct((1, N), F32)] * n_vec,
        scratch_shapes=[pltpu.VMEM((tm, tn), F32)] if nk > 1 else [],
        compiler_params=_params(("arbitrary" if n_vec else "parallel", "parallel", "arbitrary"), 56),
    )(a, b, *([] if after is None else [after]), *extras, *vec_extras)
    return outs[0] if n_out + n_vec == 1 else outs


def _rms_fwd(x, g, *, name):
    T, D = x.shape
    tr = _tile(T, 256, 8)

    def body(x_ref, g_ref, h_ref):
        xv = x_ref[...]
        r = lax.rsqrt(jnp.mean(xv * xv, axis=-1, keepdims=True) + EPS)
        h_ref[...] = (xv * r * g_ref[...]).astype(h_ref.dtype)

    return pl.pallas_call(
        body, name=name, grid=(T // tr,),
        in_specs=[pl.BlockSpec((tr, D), lambda i: (i, 0)), pl.BlockSpec((1, D), lambda i: (0, 0))],
        out_specs=pl.BlockSpec((tr, D), lambda i: (i, 0)),
        out_shape=jax.ShapeDtypeStruct((T, D), BF16),
        compiler_params=_params(("parallel",)),
    )(x, g)


def _rms_bwd(x, g, dh, dres, *, name):
    T, D = x.shape
    tr = _tile(T, 256, 8)
    epi = _epi_rms_bwd(2)

    def body(x_ref, g_ref, dh_ref, dres_ref, dx_ref, dxb_ref, dg_ref):
        dx, _, dgp = epi(dh_ref[...], x_ref[...], dres_ref[...], g_ref[...])

        @pl.when(pl.program_id(0) == 0)
        def _():
            dg_ref[...] = jnp.zeros_like(dg_ref)

        dg_ref[...] += dgp
        dx_ref[...] = dx
        dxb_ref[...] = dx.astype(dxb_ref.dtype)

    row = pl.BlockSpec((tr, D), lambda i: (i, 0))
    vec = pl.BlockSpec((1, D), lambda i: (0, 0))
    return pl.pallas_call(
        body, name=name, grid=(T // tr,),
        in_specs=[row, vec, row, row], out_specs=[row, row, vec],
        out_shape=[jax.ShapeDtypeStruct((T, D), F32), jax.ShapeDtypeStruct((T, D), BF16), jax.ShapeDtypeStruct((1, D), F32)],
        compiler_params=_params(("arbitrary",)),
    )(x, g, dh, dres)


ROW_TILE = 256


def _epi_residual_rms(acc, res, g):
    xn = acc + res
    r = lax.rsqrt(jnp.mean(xn * xn, axis=-1, keepdims=True) + EPS)
    return xn, xn * r * g


def _epi_rms_bwd(n_copies):
    def epi(dh, x, dres, g):
        r = lax.rsqrt(jnp.mean(x * x, axis=-1, keepdims=True) + EPS)
        xh = x * r
        dxh = dh * g
        dx = dres + r * (dxh - xh * jnp.mean(dxh * xh, axis=-1, keepdims=True))
        return (dx,) * n_copies + (jnp.sum(dh * xh, axis=0, keepdims=True),)
    return epi


def _final_loss(x, g, tgt, *, name):
    T, D = x.shape
    tr = _tile(T, 256, 8)

    def body(x_ref, g_ref, t_ref, dx_ref, dg_ref, loss_ref):
        xv = x_ref[...]
        r = lax.rsqrt(jnp.mean(xv * xv, axis=-1, keepdims=True) + EPS)
        xh = xv * r
        gv = g_ref[...]
        err = xh * gv - t_ref[...]

        @pl.when(pl.program_id(0) == 0)
        def _():
            dg_ref[...] = jnp.zeros_like(dg_ref)
            loss_ref[...] = jnp.zeros_like(loss_ref)

        part = 0.5 * jnp.sum(jnp.mean(err * err, axis=-1, keepdims=True), axis=0, keepdims=True)
        loss_ref[...] += jnp.broadcast_to(part, loss_ref.shape)
        dy = err * (1.0 / D)
        dg_ref[...] += jnp.sum(dy * xh, axis=0, keepdims=True)
        dxh = dy * gv
        dx_ref[...] = r * (dxh - xh * jnp.mean(dxh * xh, axis=-1, keepdims=True))

    row = pl.BlockSpec((tr, D), lambda i: (i, 0))
    vec = pl.BlockSpec((1, D), lambda i: (0, 0))
    return pl.pallas_call(
        body, name=name, grid=(T // tr,),
        in_specs=[row, vec, row], out_specs=[row, vec, pl.BlockSpec((1, LANE), lambda i: (0, 0))],
        out_shape=[jax.ShapeDtypeStruct((T, D), F32), jax.ShapeDtypeStruct((1, D), F32),
                   jax.ShapeDtypeStruct((1, LANE), F32)],
        compiler_params=_params(("arbitrary",)),
    )(x, g, tgt)


def _ple_bwd(dx3, pp, sg, *, name):
    T, D = dx3.shape
    tr = _tile(T, 256, 8)

    def body(dx_ref, pp_ref, sg_ref, dpg_ref, dpp_ref):
        dx, s = dx_ref[...], sg_ref[...]
        dpg_ref[...] = (dx * pp_ref[...] * s * (1.0 - s)).astype(dpg_ref.dtype)
        dpp_ref[...] = (dx * s).astype(dpp_ref.dtype)

    row = pl.BlockSpec((tr, D), lambda i: (i, 0))
    return pl.pallas_call(
        body, name=name, grid=(T // tr,), in_specs=[row, row, row], out_specs=[row, row],
        out_shape=[jax.ShapeDtypeStruct((T, D), BF16)] * 2, compiler_params=_params(("parallel",)),
    )(dx3, pp, sg)


ROWS_QKV_FWD, ROWS_QKV_BWD, ROWS_FFN_FWD, ROWS_FFN_BWD, ROWS_GROUP_A = 512, 256, 256, 128, 256


def _ext(ref, r0, T, before, after, RC):
    parts = []
    if before:
        p0 = pl.multiple_of(jnp.maximum(r0 - 8, 0), 8)
        parts.append(jnp.where(r0 > 0, ref[pl.ds(p0, 8), :], 0.0))
    parts.append(ref[pl.ds(r0, RC), :])
    if after:
        n0 = pl.multiple_of(jnp.minimum(r0 + RC, T - 8), 8)
        parts.append(jnp.where(r0 + RC < T, ref[pl.ds(n0, 8), :], 0.0))
    return parts[0] if len(parts) == 1 else jnp.concatenate(parts, axis=0)


def _down(xx, s):
    return (xx if s == 0 else pltpu.roll(xx, s, 0))[8:, :]


def _up(xx, s, rows):
    return (xx if s == 0 else pltpu.roll(xx, xx.shape[0] - s, 0))[:rows, :]


def _conv_down(xx, w_ref, K):
    y = None
    for j in range(K):
        t = _down(xx, K - 1 - j) * w_ref[j:j + 1, :]
        y = t if y is None else y + t
    return y


def _fold8(x):
    return jnp.sum(x.reshape(x.shape[0] // 8, 8, x.shape[1]), axis=0)


def _win(ref, r0, lo, n, T, RC, edge):
    if not edge:
        return ref[pl.ds(r0 + lo, n), :]
    xx = _ext(ref, r0, T, True, True, RC)
    a = 8 + lo
    return (xx if a == 0 else pltpu.roll(xx, xx.shape[0] - a, 0))[:n, :]


def _taps(ref, w_ref, K, r0, n, T, RC, edge):
    wins = [_win(ref, r0, -(K - 1 - j), n, T, RC, edge) for j in range(K)]
    y = wins[0] * w_ref[0:1, :]
    for j in range(1, K):
        y = y + wins[j] * w_ref[j:j + 1, :]
    return wins, y


def _untaps(scr_ref, val, w_ref, K, RC):
    scr_ref[0:val.shape[0], :] = val
    y = scr_ref[K - 1:K - 1 + RC, :] * w_ref[0:1, :]
    for j in range(1, K):
        s = K - 1 - j
        y = y + scr_ref[s:s + RC, :] * w_ref[j:j + 1, :]
    return y


def _peeled(n_chunks, RC, step, init):
    carry = step(0, init, True)
    if n_chunks > 2:
        carry = lax.fori_loop(1, n_chunks - 1, lambda i, c: step(pl.multiple_of(i * RC, RC), c, False), carry)
    if n_chunks > 1:
        carry = step((n_chunks - 1) * RC, carry, True)
    return carry


def _silu(x):
    return x * jax.nn.sigmoid(x)


def _dsilu(x):
    s = jax.nn.sigmoid(x)
    return s * (1.0 + x * (1.0 - s))


def _col_specs(T, offs):
    return [pl.BlockSpec((T, LANE), functools.partial(lambda o, j: (0, o + j), o)) for o in offs]


def _group_a_fwd(proj, conv_w, CW, *, name):
    T = proj.shape[0]
    RC = _tile(T, ROWS_GROUP_A, 8)
    nb = CW // LANE
    K = conv_w.shape[0]

    def body(ax_ref, ab_ref, ac_ref, w_ref, y_ref):
        def step(i, carry):
            r0 = pl.multiple_of(i * RC, RC)
            m = _ext(ac_ref, r0, T, True, False, RC) * _ext(ax_ref, r0, T, True, False, RC)
            y_ref[pl.ds(r0, RC), :] = (ab_ref[pl.ds(r0, RC), :] * _conv_down(m, w_ref, K)).astype(y_ref.dtype)
            return carry
        lax.fori_loop(0, T // RC, step, 0)

    return pl.pallas_call(
        body, name=name, grid=(nb,),
        in_specs=_col_specs(T, (0, nb, 2 * nb)) + [pl.BlockSpec((K, LANE), lambda j: (0, j))],
        out_specs=pl.BlockSpec((T, LANE), lambda j: (0, j)),
        out_shape=jax.ShapeDtypeStruct((T, CW), BF16), compiler_params=_params(("parallel",)),
    )(proj, proj, proj, conv_w)


def _group_a_bwd(proj, conv_w, dycat, CW, *, name):
    T = proj.shape[0]
    RC = _tile(T, ROWS_GROUP_A, 8)
    nb = CW // LANE
    K = conv_w.shape[0]

    def body(ax_ref, ab_ref, ac_ref, w_ref, dy_ref, dax_ref, dab_ref, dac_ref, dw_ref):
        def step(i, accs):
            r0 = pl.multiple_of(i * RC, RC)
            ax3 = _ext(ax_ref, r0, T, True, True, RC)
            ac3 = _ext(ac_ref, r0, T, True, True, RC)
            m3 = ax3 * ac3
            c = _conv_down(m3[:RC + 8], w_ref, K)
            dy = dy_ref[pl.ds(r0, RC), :]
            dab_ref[pl.ds(r0, RC), :] = (dy * c).astype(dab_ref.dtype)
            dc2 = _ext(dy_ref, r0, T, False, True, RC) * _ext(ab_ref, r0, T, False, True, RC)
            dm = None
            new = []
            for j in range(K):
                s = K - 1 - j
                t = _up(dc2, s, RC) * w_ref[j:j + 1, :]
                dm = t if dm is None else dm + t
                new.append(accs[j] + _fold8(dc2[:RC] * _down(m3[:RC + 8], s)))
            dax_ref[pl.ds(r0, RC), :] = (dm * ac3[8:RC + 8]).astype(dax_ref.dtype)
            dac_ref[pl.ds(r0, RC), :] = (dm * ax3[8:RC + 8]).astype(dac_ref.dtype)
            return tuple(new)

        accs = lax.fori_loop(0, T // RC, step, tuple(jnp.zeros((8, LANE), F32) for _ in range(K)))
        for j in range(K):
            dw_ref[j:j + 1, :] = jnp.sum(accs[j], axis=0, keepdims=True)

    col = pl.BlockSpec((T, LANE), lambda j: (0, j))
    wsp = pl.BlockSpec((K, LANE), lambda j: (0, j))
    return pl.pallas_call(
        body, name=name, grid=(nb,),
        in_specs=_col_specs(T, (0, nb, 2 * nb)) + [wsp, col],
        out_specs=[col, col, col, wsp],
        out_shape=[jax.ShapeDtypeStruct((T, CW), BF16)] * 3 + [jax.ShapeDtypeStruct((K, CW), F32)],
        compiler_params=_params(("parallel",)),
    )(proj, proj, proj, conv_w, dycat)


def _qkv_fwd(proj, conv_w, off, H, *, name):
    T = proj.shape[0]
    RC = _tile(T, ROWS_QKV_FWD, 8)
    nb = 3 * H
    K = conv_w.shape[0]

    def body(x_ref, w_ref, y_ref):
        j = pl.program_id(0)
        is_qk = j < 2 * H
        scale = jnp.where(j < H, HEAD ** -0.5, 1.0).astype(F32)

        def step(r0, carry, edge):
            s = _silu(_taps(x_ref, w_ref, K, r0, RC, T, RC, edge)[1])
            r = lax.rsqrt(jnp.sum(s * s, axis=-1, keepdims=True) + EPS) * scale
            y_ref[pl.ds(r0, RC), :] = s * jnp.where(is_qk, r, 1.0)
            return carry
        _peeled(T // RC, RC, step, 0)

    return pl.pallas_call(
        body, name=name, grid=(nb,),
        in_specs=_col_specs(T, (off,)) + [pl.BlockSpec((K, LANE), lambda j: (0, j))],
        out_specs=pl.BlockSpec((T, LANE), lambda j: (0, j)),
        out_shape=jax.ShapeDtypeStruct((T, nb * LANE), F32), compiler_params=_params(("parallel",)),
    )(proj, conv_w)


def _qkv_bwd(proj, conv_w, dq, dk, dv, off, H, *, name):
    T = proj.shape[0]
    RC = _tile(T, ROWS_QKV_BWD, 8)
    nb = 3 * H
    K = conv_w.shape[0]

    def body(x_ref, w_ref, dq_ref, dk_ref, dv_ref, dx_ref, dw_ref, scr_ref):
        j = pl.program_id(0)
        is_qk = j < 2 * H
        scale = jnp.where(j < H, HEAD ** -0.5, 1.0).astype(F32)

        def step(r0, accs, edge):
            xs, c2 = _taps(x_ref, w_ref, K, r0, RC + 8, T, RC, edge)
            s2 = _silu(c2)
            dn2 = jnp.where(j < H, _win(dq_ref, r0, 0, RC + 8, T, RC, edge),
                            jnp.where(is_qk, _win(dk_ref, r0, 0, RC + 8, T, RC, edge),
                                      _win(dv_ref, r0, 0, RC + 8, T, RC, edge)))
            r = lax.rsqrt(jnp.sum(s2 * s2, axis=-1, keepdims=True) + EPS)
            nh = s2 * r
            dnp = dn2 * scale
            ds_qk = r * (dnp - nh * jnp.sum(dnp * nh, axis=-1, keepdims=True))
            ds2 = jnp.where(is_qk, ds_qk, dn2)
            dc2 = ds2 * _dsilu(c2)
            dx_ref[pl.ds(r0, RC), :] = _untaps(scr_ref, dc2, w_ref, K, RC).astype(dx_ref.dtype)
            return tuple(accs[jj] + _fold8(dc2[:RC] * xs[jj][:RC]) for jj in range(K))

        accs = _peeled(T // RC, RC, step, tuple(jnp.zeros((8, LANE), F32) for _ in range(K)))
        for jj in range(K):
            dw_ref[jj:jj + 1, :] = jnp.sum(accs[jj], axis=0, keepdims=True)

    col = pl.BlockSpec((T, LANE), lambda j: (0, j))
    wsp = pl.BlockSpec((K, LANE), lambda j: (0, j))
    return pl.pallas_call(
        body, name=name, grid=(nb,),
        in_specs=_col_specs(T, (off,)) + [wsp] + [
            pl.BlockSpec((T, LANE), functools.partial(lambda o, j: (0, jnp.clip(j - o, 0, H - 1)), o)) for o in (0, H, 2 * H)],
        out_specs=[col, wsp],
        out_shape=[jax.ShapeDtypeStruct((T, nb * LANE), BF16), jax.ShapeDtypeStruct((K, nb * LANE), F32)],
        scratch_shapes=[pltpu.VMEM((RC + 8, LANE), F32)],
        compiler_params=_params(("parallel",)),
    )(proj, conv_w, dq, dk, dv)


def _softplus(x):
    return jnp.maximum(x, 0.0) + jnp.log(1.0 + jnp.exp(-jnp.abs(x)))


def _gates_fwd(proj, alog, dtb, off, H, *, name):
    T = proj.shape[0]
    tr = _tile(T, 512, CHUNK)

    def body(ab_ref, al_ref, dt_ref, gb_ref, gam_ref):
        ab = ab_ref[...]
        lane = lax.broadcasted_iota(jnp.int32, ab.shape, 1)
        g = -jnp.exp(al_ref[...]) * _softplus(ab + dt_ref[...])
        gb = jnp.where(lane < H, g, jnp.where(lane < 2 * H, jax.nn.sigmoid(ab), 0.0))
        gb_ref[...] = gb
        tril = _tri().astype(F32)
        for c in range(tr // CHUNK):
            rows = slice(c * CHUNK, (c + 1) * CHUNK)
            gam_ref[rows, :] = _mm(tril, gb[rows, :], precision=lax.Precision.HIGHEST)

    vec = pl.BlockSpec((1, LANE), lambda i: (0, 0))
    row = pl.BlockSpec((tr, LANE), lambda i: (i, 0))
    return pl.pallas_call(
        body, name=name, grid=(T // tr,),
        in_specs=[pl.BlockSpec((tr, LANE), lambda i: (i, off)), vec, vec],
        out_specs=[row, row],
        out_shape=[jax.ShapeDtypeStruct((T, LANE), F32)] * 2, compiler_params=_params(("parallel",)),
    )(proj, alog, dtb)


def _gates_bwd(proj, alog, dtb, dgb, off, H, *, name):
    T = proj.shape[0]
    tr = _tile(T, 512, CHUNK)

    def body(ab_ref, al_ref, dt_ref, d_ref, dab_ref, dal_ref, ddt_ref):
        ab, d = ab_ref[...], d_ref[...]
        lane = lax.broadcasted_iota(jnp.int32, ab.shape, 1)
        is_g = lane < H
        triu = _tri(upper=True).astype(F32)
        dg = jnp.concatenate([_mm(triu, d[c * CHUNK:(c + 1) * CHUNK, :], precision=lax.Precision.HIGHEST)
                              for c in range(tr // CHUNK)], axis=0)
        z = ab + dt_ref[...]
        A = -jnp.exp(al_ref[...])
        da = dg * A * jax.nn.sigmoid(z)
        beta = jax.nn.sigmoid(ab)
        db = d * beta * (1.0 - beta)
        dab_ref[...] = jnp.where(is_g, da, jnp.where(lane < 2 * H, db, 0.0)).astype(dab_ref.dtype)

        @pl.when(pl.program_id(0) == 0)
        def _():
            dal_ref[...] = jnp.zeros_like(dal_ref)
            ddt_ref[...] = jnp.zeros_like(ddt_ref)

        dal_ref[...] += jnp.sum(jnp.where(is_g, dg * A * _softplus(z), 0.0), axis=0, keepdims=True)
        ddt_ref[...] += jnp.sum(jnp.where(is_g, da, 0.0), axis=0, keepdims=True)

    vec = pl.BlockSpec((1, LANE), lambda i: (0, 0))
    row = pl.BlockSpec((tr, LANE), lambda i: (i, 0))
    return pl.pallas_call(
        body, name=name, grid=(T // tr,),
        in_specs=[pl.BlockSpec((tr, LANE), lambda i: (i, off)), vec, vec, row],
        out_specs=[row, vec, vec],
        out_shape=[jax.ShapeDtypeStruct((T, LANE), BF16), jax.ShapeDtypeStruct((1, LANE), F32),
                   jax.ShapeDtypeStruct((1, LANE), F32)],
        compiler_params=_params(("arbitrary",)),
    )(proj, alog, dtb, dgb)


def _gated_norm_fwd(o, proj, gn, zoff, *, name):
    T, W = o.shape
    tr = _tile(T, 512, 8)

    def body(o_ref, z_ref, g_ref, y_ref):
        ov = o_ref[...]
        r = lax.rsqrt(jnp.mean(ov * ov, axis=-1, keepdims=True) + EPS)
        y_ref[...] = (ov * r * g_ref[...] * _silu(z_ref[...])).astype(y_ref.dtype)

    blk = pl.BlockSpec((tr, LANE), lambda i, j: (i, j))
    return pl.pallas_call(
        body, name=name, grid=(T // tr, W // LANE),
        in_specs=[blk, pl.BlockSpec((tr, LANE), lambda i, j: (i, zoff + j)), pl.BlockSpec((1, LANE), lambda i, j: (0, 0))],
        out_specs=blk, out_shape=jax.ShapeDtypeStruct((T, W), BF16), compiler_params=_params(("parallel", "parallel")),
    )(o, proj, gn)


def _gated_norm_bwd(o, proj, gn, dycat, zoff, yoff, *, name):
    T, W = o.shape
    tr = _tile(T, 512, 8)

    def body(o_ref, z_ref, g_ref, dy_ref, do_ref, dz_ref, dg_ref):
        ov, zv, gv, dy = o_ref[...], z_ref[...], g_ref[...], dy_ref[...]
        r = lax.rsqrt(jnp.mean(ov * ov, axis=-1, keepdims=True) + EPS)
        nh = ov * r
        s = _silu(zv)

        @pl.when((pl.program_id(0) == 0) & (pl.program_id(1) == 0))
        def _():
            dg_ref[...] = jnp.zeros_like(dg_ref)

        dg_ref[...] += jnp.sum(dy * nh * s, axis=0, keepdims=True)
        dz_ref[...] = (dy * nh * gv * _dsilu(zv)).astype(dz_ref.dtype)
        dn = dy * gv * s
        do_ref[...] = r * (dn - nh * jnp.mean(dn * nh, axis=-1, keepdims=True))

    blk = pl.BlockSpec((tr, LANE), lambda i, j: (i, j))
    vec = pl.BlockSpec((1, LANE), lambda i, j: (0, 0))
    return pl.pallas_call(
        body, name=name, grid=(T // tr, W // LANE),
        in_specs=[blk, pl.BlockSpec((tr, LANE), lambda i, j: (i, zoff + j)), vec,
                  pl.BlockSpec((tr, LANE), lambda i, j: (i, yoff + j))],
        out_specs=[blk, blk, vec],
        out_shape=[jax.ShapeDtypeStruct((T, W), F32), jax.ShapeDtypeStruct((T, W), BF16),
                   jax.ShapeDtypeStruct((1, LANE), F32)],
        compiler_params=_params(("arbitrary", "arbitrary")),
    )(o, proj, gn, dycat)


def _ffn_act_fwd(up_pre, conv_w, *, name):
    T, F2 = up_pre.shape
    RC = _tile(T, ROWS_FFN_FWD, 8)
    nb = F2 // 2 // LANE
    K = conv_w.shape[0]

    def body(g_ref, v_ref, wg_ref, wv_ref, y_ref):
        def step(r0, carry, edge):
            _, gate = _taps(g_ref, wg_ref, K, r0, RC, T, RC, edge)
            _, val = _taps(v_ref, wv_ref, K, r0, RC, T, RC, edge)
            y_ref[pl.ds(r0, RC), :] = (_silu(gate) * val).astype(y_ref.dtype)
            return carry
        _peeled(T // RC, RC, step, 0)

    return pl.pallas_call(
        body, name=name, grid=(nb,),
        in_specs=_col_specs(T, (0, nb)) + [pl.BlockSpec((K, LANE), lambda j: (0, j)),
                                           pl.BlockSpec((K, LANE), lambda j: (0, nb + j))],
        out_specs=pl.BlockSpec((T, LANE), lambda j: (0, j)),
        out_shape=jax.ShapeDtypeStruct((T, F2 // 2), BF16), compiler_params=_params(("parallel",)),
    )(up_pre, up_pre, conv_w, conv_w)


def _ffn_act_bwd(up_pre, conv_w, dact, *, name):
    T, F2 = up_pre.shape
    RC = _tile(T, ROWS_FFN_BWD, 8)
    nb = F2 // 2 // LANE
    K = conv_w.shape[0]

    def body(g_ref, v_ref, wg_ref, wv_ref, da_ref, d_ref, dwg_ref, dwv_ref, sg_ref, sv_ref):
        def step(r0, accs, edge):
            gs, gate2 = _taps(g_ref, wg_ref, K, r0, RC + 8, T, RC, edge)
            vs, val2 = _taps(v_ref, wv_ref, K, r0, RC + 8, T, RC, edge)
            da2 = _win(da_ref, r0, 0, RC + 8, T, RC, edge)
            dgate2 = da2 * val2 * _dsilu(gate2)
            dval2 = da2 * _silu(gate2)
            d_ref[0, pl.ds(r0, RC), :] = _untaps(sg_ref, dgate2, wg_ref, K, RC).astype(d_ref.dtype)
            d_ref[1, pl.ds(r0, RC), :] = _untaps(sv_ref, dval2, wv_ref, K, RC).astype(d_ref.dtype)
            new = []
            for j in range(K):
                new.append(accs[2 * j] + _fold8(dgate2[:RC] * gs[j][:RC]))
                new.append(accs[2 * j + 1] + _fold8(dval2[:RC] * vs[j][:RC]))
            return tuple(new)

        accs = _peeled(T // RC, RC, step, tuple(jnp.zeros((8, LANE), F32) for _ in range(2 * K)))
        for j in range(K):
            dwg_ref[j:j + 1, :] = jnp.sum(accs[2 * j], axis=0, keepdims=True)
            dwv_ref[j:j + 1, :] = jnp.sum(accs[2 * j + 1], axis=0, keepdims=True)

    col = pl.BlockSpec((T, LANE), lambda j: (0, j))
    wsp = pl.BlockSpec((K, LANE), lambda j: (0, j))
    return pl.pallas_call(
        body, name=name, grid=(nb,),
        in_specs=_col_specs(T, (0, nb)) + [wsp, pl.BlockSpec((K, LANE), lambda j: (0, nb + j)), col],
        out_specs=[pl.BlockSpec((2, T, LANE), lambda j: (0, 0, j)), wsp, wsp],
        out_shape=[jax.ShapeDtypeStruct((2, T, F2 // 2), BF16)] + [jax.ShapeDtypeStruct((K, F2 // 2), F32)] * 2,
        scratch_shapes=[pltpu.VMEM((RC + 8, LANE), F32)] * 2,
        compiler_params=_params(("parallel",)),
    )(up_pre, up_pre, conv_w, conv_w, dact)


CPB = 8
CPB_SCAN = 4
GRP = 8
HP = lax.Precision.HIGH


def _tri(strict=False, upper=False):
    r = lax.broadcasted_iota(jnp.int32, (CHUNK, CHUNK), 0)
    c = lax.broadcasted_iota(jnp.int32, (CHUNK, CHUNK), 1)
    if upper:
        return c >= r
    return (r > c) if strict else (r >= c)


def _mm(a, b, dn="nn", precision=None):
    precision = HP if precision is None else precision
    return lax.dot_general(a, b, _DN[dn], precision=precision, preferred_element_type=F32)


def _mm16(a, b, dn="nn"):
    return lax.dot_general(a.astype(BF16), b.astype(BF16), _DN[dn], preferred_element_type=F32)


def _each(f, *cols):
    return [f(*xs) for xs in zip(*cols)]


def _decay(gam):
    return jnp.exp(jnp.where(_tri(), gam[:, :CHUNK] - gam.T[:CHUNK, :], -1e30))


def _delta_specs(T, H, cpb):
    rows = cpb * CHUNK
    col = lambda o: pl.BlockSpec((rows, LANE), functools.partial(lambda o, h, n: (n, o + h), o))
    bc = pl.BlockSpec((1, rows, LANE), lambda h, n: (h, n, 0))
    sq = pl.BlockSpec((1, cpb, CHUNK, CHUNK), lambda h, n: (h, n, 0, 0))
    vec = pl.BlockSpec((1, cpb, 1, LANE), lambda h, n: (h, n, 0, 0))
    return col, bc, sq, vec


def _delta_prep_fwd(qkv, gamB, bB, H, *, name):
    T = qkv.shape[0]
    N = T // CHUNK
    cpb = _tile(N, CPB, 8)
    grp = min(GRP, cpb)
    col, bc, sq, vec = _delta_specs(T, H, cpb)

    def body(q_ref, k_ref, v_ref, g_ref, b_ref, u_ref, w_ref, qd_ref, kd_ref, qk_ref, ti_ref, gl_ref):
        eye = (lax.broadcasted_iota(jnp.int32, (CHUNK, CHUNK), 0) == lax.broadcasted_iota(jnp.int32, (CHUNK, CHUNK), 1)).astype(F32)
        strict = _tri(strict=True)
        for c0 in range(0, cpb, grp):
            cs = list(range(c0, c0 + grp))
            rows = [slice(c * CHUNK, (c + 1) * CHUNK) for c in cs]
            q, k, v = ([r_[r, :] for r in rows] for r_ in (q_ref, k_ref, v_ref))
            bb = [b_ref[0, r, :] for r in rows]
            gam = [g_ref[0, r, :] for r in rows]
            D = _each(_decay, gam)
            e = _each(jnp.exp, gam)
            kk = _each(lambda k_: _mm16(k_, k_, "nt"), k)
            X = _each(lambda kk_, D_, b_: -(jnp.where(strict, kk_ * D_, 0.0) * b_[:, :CHUNK]), kk, D, bb)
            R = _each(lambda x: eye + x, X)
            for _ in range(5):
                X = _each(lambda x: _mm(x, x), X)
                R = _each(lambda r, x: r + _mm(r, x), R, X)
            u = _each(lambda r, b_, v_: _mm(r, b_ * v_), R, bb, v)
            w = _each(lambda r, b_, e_, k_: _mm(r, b_ * e_ * k_), R, bb, e, k)
            qk = _each(lambda q_, k_, D_: _mm16(q_, k_, "nt") * D_, q, k, D)
            for i, c in enumerate(cs):
                glast = gam[i][CHUNK - 1:CHUNK, :]
                u_ref[rows[i], :] = u[i]
                w_ref[rows[i], :] = w[i]
                qd_ref[rows[i], :] = e[i] * q[i]
                kd_ref[rows[i], :] = jnp.exp(glast - gam[i]) * k[i]
                qk_ref[0, c] = qk[i]
                ti_ref[0, c] = R[i]
                gl_ref[0, c] = jnp.exp(glast)

    full = jax.ShapeDtypeStruct((T, H * LANE), F32)
    sqs = jax.ShapeDtypeStruct((H, N, CHUNK, CHUNK), F32)
    return pl.pallas_call(
        body, name=name, grid=(H, N // cpb),
        in_specs=[col(0), col(H), col(2 * H), bc, bc],
        out_specs=[col(0)] * 4 + [sq, sq, vec],
        out_shape=[full] * 4 + [sqs, sqs, jax.ShapeDtypeStruct((H, N, 1, LANE), F32)],
        compiler_params=_params(("parallel", "parallel")),
    )(qkv, qkv, qkv, gamB, bB)


def _scan_specs(H, N, cpb, hb, rev):
    nbk = N // cpb
    blk = (lambda n: nbk - 1 - n) if rev else (lambda n: n)
    col = pl.BlockSpec((cpb * CHUNK, hb * LANE), lambda h, n: (blk(n), h))
    sq = pl.BlockSpec((hb, cpb, CHUNK, CHUNK), lambda h, n: (h, blk(n), 0, 0))
    vec = pl.BlockSpec((hb, cpb, 1, LANE), lambda h, n: (h, blk(n), 0, 0))
    st = pl.BlockSpec((hb, cpb, HEAD, HEAD), lambda h, n: (h, blk(n), 0, 0))
    return col, sq, vec, st


def _delta_scan_fwd(u, w, qd, kd, qk, gl, H, *, name):
    T = u.shape[0]
    N = T // CHUNK
    cpb = _tile(N, CPB_SCAN, 4)
    hb = min(GRP, H)
    col, sq, vec, st = _scan_specs(H, N, cpb, hb, False)
    lanes = [slice(j * LANE, (j + 1) * LANE) for j in range(hb)]
    heads = list(range(hb))

    def body(u_ref, w_ref, qd_ref, kd_ref, qk_ref, gl_ref, o_ref, vn_ref, ss_ref, s_scr):
        @pl.when(pl.program_id(1) == 0)
        def _():
            s_scr[...] = jnp.zeros_like(s_scr)

        def step(c, states):
            rows = pl.ds(pl.multiple_of(c * CHUNK, CHUNK), CHUNK)
            S = list(states)
            for j in heads:
                ss_ref[j, c] = S[j]
            wS = _each(lambda ln, s: _mm16(w_ref[rows, ln], s), lanes, S)
            qS = _each(lambda ln, s: _mm16(qd_ref[rows, ln], s), lanes, S)
            vn = _each(lambda ln, ws: u_ref[rows, ln] - ws, lanes, wS)
            o = _each(lambda j, qs, vn_: qs + _mm16(qk_ref[j, c], vn_), heads, qS, vn)
            new = _each(lambda j, ln, s, vn_: s * gl_ref[j, c] + _mm16(kd_ref[rows, ln], vn_, "tn"),
                        heads, lanes, S, vn)
            for j in heads:
                o_ref[rows, lanes[j]] = o[j]
                vn_ref[rows, lanes[j]] = vn[j]
            return tuple(new)
        out = lax.fori_loop(0, cpb, step, tuple(s_scr[j] for j in heads))
        for j in heads:
            s_scr[j] = out[j]

    full = jax.ShapeDtypeStruct((T, H * LANE), F32)
    return pl.pallas_call(
        body, name=name, grid=(H // hb, N // cpb),
        in_specs=[col] * 4 + [sq, vec],
        out_specs=[col, col, st],
        out_shape=[full, full, jax.ShapeDtypeStruct((H, N, HEAD, HEAD), F32)],
        scratch_shapes=[pltpu.VMEM((hb, HEAD, HEAD), F32)],
        compiler_params=_params(("parallel", "arbitrary")),
    )(u, w, qd, kd, qk, gl)


def _delta_scan_bwd(do, w, qd, kd, vn, qk, gl, ss, H, *, name):
    T = do.shape[0]
    N = T // CHUNK
    cpb = _tile(N, CPB_SCAN, 4)
    hb = min(GRP, H)
    col, sq, vec, st = _scan_specs(H, N, cpb, hb, True)
    lanes = [slice(j * LANE, (j + 1) * LANE) for j in range(hb)]
    heads = list(range(hb))

    def body(do_ref, w_ref, qd_ref, kd_ref, vn_ref, qk_ref, gl_ref, ss_ref,
             du_ref, dw_ref, dqd_ref, dkd_ref, dqk_ref, dgl_ref, ds_scr):
        @pl.when(pl.program_id(1) == 0)
        def _():
            ds_scr[...] = jnp.zeros_like(ds_scr)

        def step(i, dstates):
            c = cpb - 1 - i
            rows = pl.ds(pl.multiple_of(c * CHUNK, CHUNK), CHUNK)
            dS = list(dstates)
            S = [ss_ref[j, c] for j in heads]
            dov = [do_ref[rows, ln] for ln in lanes]
            vnv = [vn_ref[rows, ln] for ln in lanes]
            a1 = _each(lambda j, d_: _mm16(qk_ref[j, c], d_, "tn"), heads, dov)
            a2 = _each(lambda ln, ds: _mm16(kd_ref[rows, ln], ds), lanes, dS)
            dvn = _each(lambda x, y: x + y, a1, a2)
            dqd = _each(lambda d_, s: _mm16(d_, s, "nt"), dov, S)
            dkd = _each(lambda v_, ds: _mm16(v_, ds, "nt"), vnv, dS)
            dqk = _each(lambda d_, v_: _mm16(d_, v_, "nt"), dov, vnv)
            dw = _each(lambda dv_, s: -_mm16(dv_, s, "nt"), dvn, S)
            b1 = _each(lambda ln, d_: _mm16(qd_ref[rows, ln], d_, "tn"), lanes, dov)
            b2 = _each(lambda ln, dv_: _mm16(w_ref[rows, ln], dv_, "tn"), lanes, dvn)
            new = _each(lambda j, x, y, ds: x + ds * gl_ref[j, c] - y, heads, b1, b2, dS)
            for j in heads:
                du_ref[rows, lanes[j]] = dvn[j]
                dw_ref[rows, lanes[j]] = dw[j]
                dqd_ref[rows, lanes[j]] = dqd[j]
                dkd_ref[rows, lanes[j]] = dkd[j]
                dqk_ref[j, c] = dqk[j]
                dgl = jnp.sum(jnp.sum(dS[j] * S[j], axis=1, keepdims=True), axis=0, keepdims=True)
                dgl_ref[j, c] = jnp.broadcast_to(dgl, (1, LANE))
            return tuple(new)
        out = lax.fori_loop(0, cpb, step, tuple(ds_scr[j] for j in heads))
        for j in heads:
            ds_scr[j] = out[j]

    full = jax.ShapeDtypeStruct((T, H * LANE), F32)
    return pl.pallas_call(
        body, name=name, grid=(H // hb, N // cpb),
        in_specs=[col] * 5 + [sq, vec, st],
        out_specs=[col] * 4 + [sq, vec],
        out_shape=[full] * 4 + [jax.ShapeDtypeStruct((H, N, CHUNK, CHUNK), F32), jax.ShapeDtypeStruct((H, N, 1, LANE), F32)],
        scratch_shapes=[pltpu.VMEM((hb, HEAD, HEAD), F32)],
        compiler_params=_params(("parallel", "arbitrary")),
    )(do, w, qd, kd, vn, qk, gl, ss)


def _delta_prep_bwd(qkv, gamB, bB, ti, u, w, qk, du, dw, dqd, dkd, dqk, dgl, H, *, name):
    T = qkv.shape[0]
    N = T // CHUNK
    cpb = _tile(N, CPB, 8)
    grp = min(GRP, cpb)
    col, bc, sq, vec = _delta_specs(T, H, cpb)

    def body(q_ref, k_ref, v_ref, g_ref, b_ref, ti_ref, u_ref, w_ref, qk_ref,
             du_ref, dw_ref, dqd_ref, dkd_ref, dqk_ref, dgl_ref,
             dq_ref, dk_ref, dv_ref, dg_ref, db_ref):
        ones = jnp.ones((CHUNK, LANE), F32)
        strict = _tri(strict=True)
        last = lax.broadcasted_iota(jnp.int32, (CHUNK, LANE), 0) == CHUNK - 1
        lsum = lambda x: jnp.sum(x, axis=-1, keepdims=True)
        for c0 in range(0, cpb, grp):
            cs = list(range(c0, c0 + grp))
            rows = [slice(c * CHUNK, (c + 1) * CHUNK) for c in cs]
            ld = lambda r_: [r_[r, :] for r in rows]
            q, k, v, uv, wv, duv, dwv, dqd_v, dkd_v = (ld(r_) for r_ in (q_ref, k_ref, v_ref, u_ref, w_ref, du_ref, dw_ref, dqd_ref, dkd_ref))
            bb = [b_ref[0, r, :] for r in rows]
            gam = [g_ref[0, r, :] for r in rows]
            Ti = [ti_ref[0, c] for c in cs]
            QK = [qk_ref[0, c] for c in cs]
            dqk_v = [dqk_ref[0, c] for c in cs]
            D = _each(_decay, gam)
            e = _each(jnp.exp, gam)
            glast = [g_[CHUNK - 1:CHUNK, :] for g_ in gam]
            eL = _each(lambda gl_, g_: jnp.exp(gl_ - g_), glast, gam)
            kk = _each(lambda k_: _mm16(k_, k_, "nt"), k)
            KKD = _each(lambda kk_, D_: jnp.where(strict, kk_ * D_, 0.0), kk, D)
            dru = _each(lambda t, d_: _mm(t, d_, "tn"), Ti, duv)
            drw = _each(lambda t, d_: _mm(t, d_, "tn"), Ti, dwv)
            l1 = _each(lambda a, b: _mm(a, b, "nt"), dru, uv)
            l2 = _each(lambda a, b: _mm(a, b, "nt"), drw, wv)
            dL = _each(lambda a, b: jnp.where(strict, -(a + b), 0.0), l1, l2)
            Mm = _each(lambda dl, b_: dl * b_[:, :CHUNK], dL, bb)
            dKK = _each(lambda m_, D_: m_ * D_, Mm, D)
            dQK = _each(lambda a, D_: a * D_, dqk_v, D)
            P = _each(lambda m_, kkd, a, qk_: m_ * kkd + a * qk_, Mm, KKD, dqk_v, QK)
            q1 = _each(lambda a, k_: _mm16(a, k_), dQK, k)
            k1 = _each(lambda a, q_: _mm16(a, q_, "tn"), dQK, q)
            k2 = _each(lambda a, k_: _mm16(a, k_), dKK, k)
            k3 = _each(lambda a, k_: _mm16(a, k_, "tn"), dKK, k)
            s1 = _each(lambda dl, kkd: _mm(dl * kkd, ones), dL, KKD)
            p1 = _each(lambda p_: _mm(p_, ones), P)
            p2 = _each(lambda p_: _mm(p_, ones, "tn"), P)
            for i, c in enumerate(cs):
                r = rows[i]
                bek = bb[i] * e[i]
                kdv = eL[i] * k[i]
                dq_ref[r, :] = q1[i] + e[i] * dqd_v[i]
                dk_ref[r, :] = k1[i] + k2[i] + k3[i] + bek * drw[i] + eL[i] * dkd_v[i]
                dv_ref[r, :] = bb[i] * dru[i]
                db_ref[0, r, :] = s1[i] + lsum(dru[i] * v[i]) + lsum(drw[i] * e[i] * k[i])
                dgam = (p1[i] - p2[i] + lsum(drw[i] * bek * k[i]) + lsum(dqd_v[i] * e[i] * q[i])
                        - lsum(dkd_v[i] * kdv))
                xlast = jnp.sum(lsum(dkd_v[i] * kdv), axis=0, keepdims=True) + jnp.exp(glast[i]) * dgl_ref[0, c]
                dg_ref[0, r, :] = dgam + jnp.where(last, xlast, 0.0)

    full = jax.ShapeDtypeStruct((T, H * LANE), F32)
    bcs = jax.ShapeDtypeStruct((H, T, LANE), F32)
    return pl.pallas_call(
        body, name=name, grid=(H, N // cpb),
        in_specs=[col(0), col(H), col(2 * H), bc, bc, sq, col(0), col(0), sq, col(0), col(0), col(0), col(0), sq, vec],
        out_specs=[col(0), col(0), col(0), bc, bc],
        out_shape=[full, full, full, bcs, bcs],
        compiler_params=_params(("parallel", "parallel")),
    )(qkv, qkv, qkv, gamB, bB, ti, u, w, qk, du, dw, dqd, dkd, dqk, dgl)


def _adam(parts, w, m, v, *, name, own=None, me=None):
    P, R, C = parts.shape
    if R > 256 and R % 8:
        tr, tc = R, _tile(C, 256)
    else:
        tr, tc = _tile(R, 256, 8), C
    n_own = 0 if own is None else 2

    def body(*refs):
        p_ref, w_ref, m_ref, v_ref, g_ref, d_ref, nm_ref, nv_ref = refs[n_own:]
        g = None
        for i in range(P):
            t = p_ref[i].astype(F32)
            if n_own:
                t = jnp.where(refs[0][0] == i, refs[1][...].astype(F32), t)
            g = t if g is None else g + t
        mn = ADAM_B1 * m_ref[...] + (1.0 - ADAM_B1) * g
        vn = ADAM_B2 * v_ref[...] + (1.0 - ADAM_B2) * (g * g)
        m_hat = mn / (1.0 - ADAM_B1 ** ADAM_STEP)
        v_hat = vn / (1.0 - ADAM_B2 ** ADAM_STEP)
        g_ref[...] = g
        d_ref[...] = -ADAM_LR * (m_hat / (jnp.sqrt(v_hat) + ADAM_EPS) + ADAM_WD * w_ref[...])
        nm_ref[...] = mn
        nv_ref[...] = vn

    blk = pl.BlockSpec((tr, tc), lambda i, j: (i, j))
    return pl.pallas_call(
        body, name=name, grid=(R // tr, C // tc),
        in_specs=[pl.BlockSpec(memory_space=pltpu.SMEM), blk][:n_own] + [pl.BlockSpec((P, tr, tc), lambda i, j: (0, i, j)), blk, blk, blk],
        out_specs=[blk] * 4, out_shape=[jax.ShapeDtypeStruct((R, C), F32)] * 4,
        compiler_params=_params(("parallel", "parallel")),
    )(*([me, own] if n_own else []), parts, w, m, v)


def _mesh_pos():
    return lax.axis_index("x"), lax.axis_index("y"), lax.axis_index("c")


def _peer(k):
    x, y, c = _mesh_pos()
    px, py, pc = x ^ ((k >> 2) & 1), y ^ ((k >> 1) & 1), c ^ (k & 1)
    return (px, py, pc), 4 * px + 2 * py + pc


def _exchange(arrays, scatter, *, name, after=None):
    n = len(arrays)
    n_in = n if after is None else n + 1
    blocks = [a.shape[1:] if scatter else a.shape for a in arrays]

    def body(*refs):
        srcs, dsts = refs[:n], refs[n_in:n_in + n]
        send_sems, recv_sems, local_sems = refs[n_in + n:]
        x, y, c = _mesh_pos()
        me = 4 * x + 2 * y + c
        local, sends = [], []
        for a in range(n):
            cp = pltpu.make_async_copy(srcs[a].at[me] if scatter else srcs[a], dsts[a].at[me], local_sems.at[a])
            cp.start()
            local.append(cp)
            for k in range(1, N_DEV):
                dev, idx = _peer(k)
                cp = pltpu.make_async_remote_copy(
                    src_ref=srcs[a].at[idx] if scatter else srcs[a], dst_ref=dsts[a].at[me],
                    send_sem=send_sems.at[a * N_DEV + k], recv_sem=recv_sems.at[a * N_DEV + k],
                    device_id=dev, device_id_type=MESH)
                cp.start()
                sends.append(cp)
        for a in range(n):
            for k in range(1, N_DEV):
                dev, idx = _peer(k)
                pltpu.make_async_remote_copy(
                    src_ref=srcs[a].at[idx] if scatter else srcs[a], dst_ref=dsts[a].at[idx],
                    send_sem=send_sems.at[a * N_DEV + k], recv_sem=recv_sems.at[a * N_DEV + k],
                    device_id=dev, device_id_type=MESH).wait_recv()
        for cp in sends:
            cp.wait_send()
        for cp in local:
            cp.wait()

    anyspec = pl.BlockSpec(memory_space=pl.ANY)
    return pl.pallas_call(
        body, name=name, in_specs=[anyspec] * n_in, out_specs=[anyspec] * n,
        out_shape=[jax.ShapeDtypeStruct((N_DEV,) + tuple(b), a.dtype) for a, b in zip(arrays, blocks)],
        scratch_shapes=[pltpu.SemaphoreType.DMA((n * N_DEV,)), pltpu.SemaphoreType.DMA((n * N_DEV,)),
                        pltpu.SemaphoreType.DMA((n,))],
    )(*arrays, *([] if after is None else [after]))


_ANY = pl.BlockSpec(memory_space=pl.ANY)
_SEM = pl.BlockSpec(memory_space=pltpu.SEMAPHORE)
_EFFECT = pltpu.SideEffectType.DATAFLOW_SIDE_EFFECTING


def _in_hbm(a):
    return pltpu.with_memory_space_constraint(a, pltpu.HBM)


def _split_copy(src, land, send, recv, k, me, scatter, landed):
    dev, idx = _peer(k)
    return pltpu.make_async_remote_copy(
        src_ref=src.at[idx] if scatter else src, dst_ref=land.at[idx if landed else me],
        send_sem=send.at[k], recv_sem=recv.at[k], device_id=dev, device_id_type=MESH)


ALL_PEERS = tuple(range(1, N_DEV))
SIBLING = 1
SAME_CORE = (2, 4, 6)


def _split_start(srcs, lands, scatter, *, name, relations=None):
    n = len(srcs)
    relations = relations or [ALL_PEERS] * n

    def body(*refs):
        src, land, send, recv, token = refs[:n], refs[n:2 * n], refs[2 * n:3 * n], refs[3 * n:4 * n], refs[-1]
        x, y, c = _mesh_pos()
        me = 4 * x + 2 * y + c
        for a in range(n):
            for k in relations[a]:
                _split_copy(src[a], land[a], send[a], recv[a], k, me, scatter, False).start()
        token[...] = jnp.zeros_like(token)

    outs = pl.pallas_call(
        body, name=name,
        out_shape=[pltpu.SemaphoreType.DMA((N_DEV,))] * (2 * n) + [pltpu.HBM(t.shape, t.dtype) for t in list(srcs) + list(lands)]
        + [jax.ShapeDtypeStruct((8, LANE), F32)],
        in_specs=[_ANY] * (2 * n), out_specs=[_SEM] * (2 * n) + [_ANY] * (2 * n) + [pl.BlockSpec(memory_space=pltpu.VMEM)],
        input_output_aliases={i: 2 * n + i for i in range(2 * n)},
        compiler_params=pltpu.CompilerParams(has_side_effects=_EFFECT),
    )(*[_in_hbm(t) for t in list(srcs) + list(lands)])
    handles = [(outs[a], outs[n + a], outs[2 * n + a], outs[3 * n + a]) for a in range(n)]
    return handles, outs[-1]


def _split_wait(handle, after, scatter, *, name):
    send, recv, src_thru, land_thru = handle

    def body(src_ref, land_ref, send_ref, recv_ref, after_ref, src_out, land_out):
        x, y, c = _mesh_pos()
        me = 4 * x + 2 * y + c
        for k in range(1, N_DEV):
            cp = _split_copy(src_ref, land_ref, send_ref, recv_ref, k, me, scatter, True)
            cp.wait_send()
            cp.wait_recv()

    return pl.pallas_call(
        body, name=name,
        out_shape=(pltpu.HBM(src_thru.shape, src_thru.dtype), pltpu.HBM(land_thru.shape, land_thru.dtype)),
        in_specs=(_ANY, _ANY, _SEM, _SEM, _ANY), out_specs=(_ANY, _ANY), input_output_aliases={0: 0, 1: 1},
        compiler_params=pltpu.CompilerParams(has_side_effects=_EFFECT),
    )(src_thru, land_thru, send, recv, after)[1]


def _forward_copy(land, fsend, frecv, k, landed):
    x, y, c = _mesh_pos()
    _, idx = _peer(k | SIBLING if landed else k)
    return pltpu.make_async_remote_copy(src_ref=land.at[idx], dst_ref=land.at[idx], send_sem=fsend.at[k],
                                        recv_sem=frecv.at[k], device_id=(x, y, 1 - c), device_id_type=MESH)


def _gather_forward(handle, after, *, name):
    send, recv, src_thru, land_thru = handle

    def body(src_ref, land_ref, send_ref, recv_ref, after_ref, src_out, land_out, fsend, frecv):
        x, y, c = _mesh_pos()
        me = 4 * x + 2 * y + c
        for k in SAME_CORE:
            _split_copy(src_ref, land_ref, send_ref, recv_ref, k, me, False, True).wait_recv()
            _forward_copy(land_ref, fsend, frecv, k, False).start()

    src2, land2, fsend, frecv = pl.pallas_call(
        body, name=name,
        out_shape=(pltpu.HBM(src_thru.shape, src_thru.dtype), pltpu.HBM(land_thru.shape, land_thru.dtype),
                   pltpu.SemaphoreType.DMA((N_DEV,)), pltpu.SemaphoreType.DMA((N_DEV,))),
        in_specs=(_ANY, _ANY, _SEM, _SEM, _ANY), out_specs=(_ANY, _ANY, _SEM, _SEM), input_output_aliases={0: 0, 1: 1},
        compiler_params=pltpu.CompilerParams(has_side_effects=_EFFECT),
    )(src_thru, land_thru, send, recv, after)
    return (send, recv, src2, land2), (fsend, frecv)


def _gather_wait_two_level(handle, fwd, *, name):
    send, recv, src_thru, land_thru = handle
    fsend, frecv = fwd

    def body(src_ref, land_ref, send_ref, recv_ref, fsend_ref, frecv_ref, src_out, land_out):
        x, y, c = _mesh_pos()
        me = 4 * x + 2 * y + c
        for k in (SIBLING,) + SAME_CORE:
            _split_copy(src_ref, land_ref, send_ref, recv_ref, k, me, False, True).wait_send()
        _split_copy(src_ref, land_ref, send_ref, recv_ref, SIBLING, me, False, True).wait_recv()
        for k in SAME_CORE:
            _forward_copy(land_ref, fsend_ref, frecv_ref, k, False).wait_send()
            _forward_copy(land_ref, fsend_ref, frecv_ref, k, True).wait_recv()

    return pl.pallas_call(
        body, name=name,
        out_shape=(pltpu.HBM(src_thru.shape, src_thru.dtype), pltpu.HBM(land_thru.shape, land_thru.dtype)),
        in_specs=(_ANY, _ANY, _SEM, _SEM, _SEM, _SEM), out_specs=(_ANY, _ANY), input_output_aliases={0: 0, 1: 1},
        compiler_params=pltpu.CompilerParams(has_side_effects=_EFFECT),
    )(src_thru, land_thru, send, recv, fsend, frecv)[1]


def _local_step(x, p, tgt, S, wt, conv, emit):
    T, D = x.shape
    CW = DNW = D // 2
    H = DNW // HEAD
    nA, nD = CW // LANE, DNW // LANE
    qkv_off, z_off, ab_off = 3 * nA, 3 * nA + 3 * nD, 3 * nA + 4 * nD
    alog = jnp.pad(S["a_log"], ((0, 0), (0, LANE - H)))
    dtb = jnp.pad(S["dt_bias"], ((0, 0), (0, LANE - H)))

    h1 = _rms_fwd(x, S["g_mix"], name="rms1_fwd")
    w_in, cv = wt("w_in", h1), conv(h1)
    proj = _matmul(h1, w_in, "nt", name="mm_in")
    y_a = _group_a_fwd(proj, cv["conv_a"], CW, name="group_a_fwd")
    qkv = _qkv_fwd(proj, cv["conv_qkv"], qkv_off, H, name="qkv_fwd")
    gb, gamc = _gates_fwd(proj, alog, dtb, ab_off, H, name="gates_fwd")
    bcast = lambda cols: jnp.broadcast_to(cols.T[:, :, None], (H, T, LANE))
    gamB, bB = bcast(gamc[:, :H]), bcast(gb[:, H:2 * H])
    u, w, qd, kd, qk, ti, gl = _delta_prep_fwd(qkv, gamB, bB, H, name="delta_prep_fwd")
    o, vn, ss = _delta_scan_fwd(u, w, qd, kd, qk, gl, H, name="delta_scan_fwd")
    y_b = _gated_norm_fwd(o, proj, S["dn_g"], z_off, name="gated_norm_fwd")
    ycat = jnp.concatenate([y_a, y_b], axis=1)
    w_out = wt("w_out", ycat)
    rows = dict(tm=ROW_TILE, tn=D)
    x1, h2 = _matmul(ycat, w_out, "nn", name="mm_out", out_dtypes=(F32, BF16), epilogue=_epi_residual_rms,
                     extras=(x,), vec_extras=(S["g_ffn"],), **rows)
    w_up = wt("w_up", h2)
    up_pre = _matmul(h2, w_up, "nn", name="mm_up", b_shards=True, tn=SHARD_TILE)
    act = _ffn_act_fwd(up_pre, cv["conv_ffn"], name="ffn_act_fwd")
    w_down = wt("w_down", act)
    x2 = _matmul(act, w_down, "nn", name="mm_down", epilogue=lambda acc, r: (acc + r,), extras=(x1,), tk=LONG_K)
    h3 = _rms_fwd(x2, S["g_ple"], name="rms3_fwd")
    w_pp, w_pg = wt("w_pp", h3), wt("w_pg", h3)
    pp = _matmul(p, w_pp, "nn", name="mm_pp", b_shards=True)

    def ple_epi(acc, x2r, ppr):
        s = jax.nn.sigmoid(acc)
        return x2r + s * ppr, s

    x3, sg = _matmul(h3, w_pg, "nn", name="mm_pg", out_dtypes=(F32, F32), epilogue=ple_epi, extras=(x2, pp), tm=512)
    dx3, dg_final, loss = _final_loss(x3, S["g_final"], tgt, name="final_loss")

    G = {"g_final": dg_final}
    dpg, dpp = _ple_bwd(dx3, pp, sg, name="ple_bwd")
    tok = emit({"w_pp": _matmul(p, dpp, "tn", name="mm_dwpp", out_dtypes=(BF16,), out_shards=True, tk=LONG_K),
                "w_pg": _matmul(h3, dpg, "tn", name="mm_dwpg", out_dtypes=(BF16,), tk=LONG_K)})
    bwd = dict(out_dtypes=(F32, BF16), epilogue=_epi_rms_bwd(2), n_vec=1, **rows)
    dx2, dx2b, G["g_ple"] = _matmul(dpg, w_pg, "nt", name="mm_dh3", after=tok, extras=(x2, dx3),
                                    vec_extras=(S["g_ple"],), **bwd)
    tok = emit({"w_down": _matmul(act, dx2b, "tn", name="mm_dwdown", out_dtypes=(BF16,), tk=LONG_K)})
    dact = _matmul(dx2b, w_down, "nt", name="mm_dact", after=tok)
    dup, dcf_g, dcf_v = _ffn_act_bwd(up_pre, cv["conv_ffn"], dact, name="ffn_act_bwd")
    G["conv_ffn"] = jnp.concatenate([dcf_g, dcf_v], axis=1)
    tok = emit({"w_up": _matmul(h2, dup, "tn", name="mm_dwup", out_dtypes=(BF16,), b_shards=True, out_shards=True,
                                tn=SHARD_TILE, tk=LONG_K)})
    dh2 = _matmul(dup, w_up, "nt", name="mm_dh2", after=tok, a_shards=True, b_shards=True, tk=2 * SHARD_TILE)
    dx1, dx1b, G["g_ffn"] = _rms_bwd(x1, S["g_ffn"], dh2, dx2, name="rms2_bwd")
    tok = emit({"w_out": _matmul(ycat, dx1b, "tn", name="mm_dwout", out_dtypes=(BF16,), tk=LONG_K)})
    dycat = _matmul(dx1b, w_out, "nt", name="mm_dycat", after=tok)
    do, dz, G["dn_g"] = _gated_norm_bwd(o, proj, S["dn_g"], dycat, z_off, nA, name="gated_norm_bwd")
    du, dw, dqd, dkd, dqk, dgl = _delta_scan_bwd(do, w, qd, kd, vn, qk, gl, ss, H, name="delta_scan_bwd")
    dq, dk, dv, dgB, dbB = _delta_prep_bwd(qkv, gamB, bB, ti, u, w, qk, du, dw, dqd, dkd, dqk, dgl, H,
                                           name="delta_prep_bwd")
    dgb = jnp.pad(jnp.concatenate([dgB[:, :, 0].T, dbB[:, :, 0].T], axis=1), ((0, 0), (0, LANE - 2 * H)))
    dab, dal, ddt = _gates_bwd(proj, alog, dtb, dgb, ab_off, H, name="gates_bwd")
    G["a_log"], G["dt_bias"] = dal[:, :H], ddt[:, :H]
    dqkv, G["conv_qkv"] = _qkv_bwd(proj, cv["conv_qkv"], dq, dk, dv, qkv_off, H, name="qkv_bwd")
    dax, dab_, dac, G["conv_a"] = _group_a_bwd(proj, cv["conv_a"], dycat, CW, name="group_a_bwd")
    in_p = w_in.shape[0]
    dproj = jnp.concatenate([dax, dab_, dac, dqkv, dz, dab, jnp.zeros((T, in_p - (ab_off + 1) * LANE), BF16)], axis=1)
    tok = emit({"w_in": _matmul(dproj, h1, "tn", name="mm_dwin", out_dtypes=(BF16,), tk=LONG_K)})
    dh1 = _matmul(dproj, w_in, "nn", name="mm_dh1", after=tok, tk=LONG_K)
    grad_x, _, G["g_mix"] = _rms_bwd(x, S["g_mix"], dh1, dx1, name="rms1_bwd")
    return loss, grad_x, G


def _col_sharded(landed):
    _, R, C = landed.shape
    return jnp.transpose(landed, (1, 0, 2)).reshape(R, N_DEV * C)


def kernel(x, p, norm_mix_g, w_in, conv_a_w, conv_qkv_w, a_log, dt_bias, dn_norm_g, w_out, norm_ffn_g, w_up, conv_ffn_w, w_down, norm_ple_g, w_ple_gate, w_ple_proj, final_norm_g, loss_target, m_norm_mix_g, m_w_in, m_conv_a_w, m_conv_qkv_w, m_a_log, m_dt_bias, m_dn_norm_g, m_w_out, m_norm_ffn_g, m_w_up, m_conv_ffn_w, m_w_down, m_norm_ple_g, m_w_ple_gate, m_w_ple_proj, m_final_norm_g, v_norm_mix_g, v_w_in, v_conv_a_w, v_conv_qkv_w, v_a_log, v_dt_bias, v_dn_norm_g, v_w_out, v_norm_ffn_g, v_w_up, v_conv_ffn_w, v_w_down, v_norm_ple_g, v_w_ple_gate, v_w_ple_proj, v_final_norm_g):
    T, D = x.shape[1], x.shape[2]
    xd, _, cd = _mesh_pos()
    me = 4 * xd + 2 * lax.axis_index("y") + cd

    conv_sh = [conv_a_w[0], conv_qkv_w[0], conv_ffn_w[0]]
    conv_n = [c.size for c in conv_sh]
    pack_rows = -(-sum(conv_n) // LANE)
    conv_pack = jnp.pad(jnp.concatenate([c.reshape(-1) for c in conv_sh]), (0, pack_rows * LANE - sum(conv_n))).reshape(pack_rows, LANE)
    names = ["w_in", "conv", "w_out", "w_up", "w_down", "w_pg", "w_pp"]
    tr_ = lambda t: jnp.swapaxes(t, 1, 2)
    shards = [w_in[0].T.astype(BF16), conv_pack, w_out[0].astype(BF16), w_up[0].astype(BF16), w_down[0].astype(BF16),
              w_ple_gate[0].astype(BF16), w_ple_proj[0].astype(BF16)]
    empty_slots = lambda blocks: [lax.empty((N_DEV,) + tuple(b.shape), b.dtype) for b in blocks]
    handles, tok0 = _split_start(shards, empty_slots(shards), False, name="gather_start",
                                 relations=[(SIBLING,) + SAME_CORE] + [ALL_PEERS] * (len(shards) - 1))
    handle = dict(zip(names, handles))
    own = dict(zip(names, shards))
    in_cols = N_DEV * w_in.shape[2]
    in_p = (in_cols // LANE) * LANE + AB_PAD
    in_place = {"w_up", "w_pp"}

    def gathered(name, after):
        if name == "w_in":
            passed, fwd = _gather_forward(handle[name], after, name="gather_forward_w_in")
            landed = _gather_wait_two_level(passed, fwd, name="gather_wait_w_in")
        else:
            landed = _split_wait(handle[name], after, False, name="gather_wait_" + name)
        return lax.dynamic_update_index_in_dim(landed, own[name], me, 0)

    def wt(name, after):
        landed = gathered(name, after)
        if name in in_place:
            return landed
        full = landed.reshape(-1, D)
        return jnp.pad(full, ((0, in_p - in_cols), (0, 0))) if name == "w_in" else full

    def conv(after):
        flat = gathered("conv", after).reshape(N_DEV, pack_rows * LANE)
        out, o_ = {}, 0
        for nm, c, n_ in zip(("conv_a", "conv_qkv", "conv_ffn"), conv_sh, conv_n):
            out[nm] = _col_sharded(flat[:, o_:o_ + n_].reshape((N_DEV,) + c.shape))
            o_ += n_
        return out

    pending, mine = {}, {}

    def emit(grads):
        parts = [g if nm in in_place else (g[:in_cols] if nm == "w_in" else g).reshape(N_DEV, -1, D)
                 for nm, g in grads.items()]
        hs, tok = _split_start(parts, empty_slots([q[0] for q in parts]), True, name="scatter_start_" + "_".join(grads))
        pending.update(zip(grads, hs))
        mine.update({nm: lax.dynamic_index_in_dim(q, me, 0, keepdims=False) for nm, q in zip(grads, parts)})
        return tok

    S = {
        "g_mix": norm_mix_g + tok0[0, 0], "a_log": a_log, "dt_bias": dt_bias, "dn_g": dn_norm_g, "g_ffn": norm_ffn_g,
        "g_ple": norm_ple_g, "g_final": final_norm_g.reshape(1, D),
    }

    loss_v, grad_x, G = _local_step(x[0], p[0, 0], loss_target[0], S, wt, conv, emit)
    loss = lax.psum(loss_v[0, 0], ("x", "y", "c"))

    small_names = ["g_mix", "g_ffn", "g_ple", "g_final", "dn_g", "a_log", "dt_bias", "conv_a", "conv_qkv", "conv_ffn"]
    small_rows, pieces = [], []
    for nm in small_names:
        g_ = G[nm].reshape(-1)
        r_ = -(-g_.size // (8 * LANE)) * 8
        small_rows.append(r_)
        pieces.append(jnp.pad(g_, (0, r_ * LANE - g_.size)).reshape(r_, LANE))
    landed = {nm: _split_wait(h_, grad_x, True, name="scatter_wait_" + nm) for nm, h_ in pending.items() if nm != "w_in"}

    def adam(parts, w_, m_, v_, nm, own_=None):
        shp = w_.shape
        w2, m2, v2 = (t.reshape(parts.shape[1:]) for t in (w_, m_, v_))
        kw = {} if own_ is None else {"own": own_, "me": me.astype(jnp.int32).reshape(1)}
        return tuple(t.reshape(shp) for t in _adam(parts, w2, m2, v2, name="adam_" + nm, **kw))

    big = {
        "w_up": adam(landed["w_up"], w_up, m_w_up, v_w_up, "w_up", mine["w_up"]),
        "w_down": adam(landed["w_down"], w_down, m_w_down, v_w_down, "w_down", mine["w_down"]),
        "w_out": adam(landed["w_out"], w_out, m_w_out, v_w_out, "w_out", mine["w_out"]),
        "w_pg": adam(landed["w_pg"], w_ple_gate, m_w_ple_gate, v_w_ple_gate, "w_ple_gate", mine["w_pg"]),
        "w_pp": adam(landed["w_pp"], w_ple_proj, m_w_ple_proj, v_w_ple_proj, "w_ple_proj", mine["w_pp"]),
    }
    first = lambda t: lax.slice(t, (0,) * t.ndim, (1,) * t.ndim).reshape(1)
    big_done = sum(first(r[1]) for r in big.values())
    (small_l,) = _exchange([jnp.concatenate(pieces, axis=0)], False, name="gather_small_grads", after=big_done)

    def small_parts(nm):
        i = small_names.index(nm)
        r0 = sum(small_rows[:i])
        shp = G[nm].shape
        return small_l[:, r0:r0 + small_rows[i], :].reshape(N_DEV, -1)[:, :G[nm].size].reshape((N_DEV,) + shp)

    def conv_parts(nm, shard):
        full = small_parts(nm)
        C = shard.shape[-1]
        return lax.dynamic_slice_in_dim(full, me * C, C, axis=2)

    res = [
        adam(small_parts("g_mix"), norm_mix_g, m_norm_mix_g, v_norm_mix_g, "norm_mix_g"),
        None,
        adam(conv_parts("conv_a", conv_a_w), conv_a_w, m_conv_a_w, v_conv_a_w, "conv_a_w"),
        adam(conv_parts("conv_qkv", conv_qkv_w), conv_qkv_w, m_conv_qkv_w, v_conv_qkv_w, "conv_qkv_w"),
        adam(small_parts("a_log"), a_log, m_a_log, v_a_log, "a_log"),
        adam(small_parts("dt_bias"), dt_bias, m_dt_bias, v_dt_bias, "dt_bias"),
        adam(small_parts("dn_g"), dn_norm_g, m_dn_norm_g, v_dn_norm_g, "dn_norm_g"),
        big["w_out"],
        adam(small_parts("g_ffn"), norm_ffn_g, m_norm_ffn_g, v_norm_ffn_g, "norm_ffn_g"),
        big["w_up"],
        adam(conv_parts("conv_ffn", conv_ffn_w), conv_ffn_w, m_conv_ffn_w, v_conv_ffn_w, "conv_ffn_w"),
        big["w_down"],
        adam(small_parts("g_ple"), norm_ple_g, m_norm_ple_g, v_norm_ple_g, "norm_ple_g"),
        big["w_pg"],
        big["w_pp"],
        adam(small_parts("g_final"), final_norm_g.reshape(1, D), m_final_norm_g.reshape(1, D),
             v_final_norm_g.reshape(1, D), "final_norm_g"),
    ]
    res[-1] = tuple(t.reshape(D) for t in res[-1])
    landed_in = _split_wait(pending["w_in"], res[10][1], True, name="scatter_wait_w_in")
    res[1] = tuple(tr_(t) for t in adam(landed_in, tr_(w_in), tr_(m_w_in), tr_(v_w_in), "w_in", mine["w_in"]))
    grads, deltas, new_m, new_v = zip(*res)
    return (loss, grad_x[None], *grads, *deltas, *new_m, *new_v)
```

```python
import functools

import jax
import jax.numpy as jnp
from jax import lax
from jax.experimental import pallas as pl
from jax.experimental.pallas import tpu as pltpu

F32 = jnp.float32
BF16 = jnp.bfloat16

EPS = 1e-6
CHUNK = 64
HEAD = 128
LANE = 128
N_DEV = 8
AB_PAD = 512

ADAM_LR = 0.001
ADAM_B1 = 0.9
ADAM_B2 = 0.999
ADAM_EPS = 1e-08
ADAM_WD = 0.01
ADAM_STEP = 10

MESH = pl.DeviceIdType.MESH


def _tile(dim, target, align=LANE):
    if dim <= target:
        return dim
    t = (target // align) * align
    while t > align and dim % t:
        t -= align
    assert dim % t == 0, (dim, target)
    return t


def _params(sem, vmem_mb=48):
    return pltpu.CompilerParams(dimension_semantics=sem, vmem_limit_bytes=vmem_mb << 20)


_DN = {"nn": (((1,), (0,)), ((), ())), "nt": (((1,), (1,)), ((), ())), "tn": (((0,), (0,)), ((), ()))}
LONG_K = 4096
SHARD_TILE = 1408


def _matmul(a, b, mode, *, name, out_dtypes=(F32,), epilogue=None, extras=(), vec_extras=(), n_vec=0, after=None,
            a_shards=False, b_shards=False, out_shards=False, tm=1024, tn=1024, tk=2048):
    shard_w = b.shape[2] if b_shards else None
    if b_shards:
        b_rows, b_cols = b.shape[1], b.shape[0] * shard_w
    else:
        b_rows, b_cols = b.shape
    a_w = a.shape[2] if a_shards else None
    a_dims = (a.shape[1], a.shape[0] * a_w) if a_shards else a.shape
    if mode == "nn":
        (M, K), (K2, N) = a_dims, (b_rows, b_cols)
    elif mode == "nt":
        (M, K), (N, K2) = a_dims, (b_rows, b_cols)
    else:
        (K, M), (K2, N) = a_dims, (b_rows, b_cols)
    assert K == K2, (name, a.shape, b.shape)
    tm = _tile(M, tm)
    n_dims = [N] + ([shard_w] if (b_shards and mode != "nt") else []) + ([N // N_DEV] if out_shards else [])
    tn = _tile(min(n_dims), tn)
    assert all(d % tn == 0 for d in n_dims), (name, n_dims, tn)
    grp = 1
    if b_shards and mode == "nt":
        grp = max(g for g in (1, 2, 4, 8) if g <= max(1, tk // shard_w) and (a_w is None or a_w % (g * shard_w) == 0))
    k_dims = [K] + ([shard_w] if (b_shards and mode == "nt") else []) + ([a_w] if a_shards else [])
    tk = grp * shard_w if grp > 1 else _tile(min(k_dims), tk)
    assert K % tk == 0, (name, K, tk)
    nk = K // tk
    n_ex, n_out = len(extras) + len(vec_extras), len(out_dtypes)
    assert n_vec == 0 or tn == N, (name, tn, N)
    dn = _DN[mode]

    n_tok = 0 if after is None else 1

    def body(a_ref, b_ref, *rest):
        rest = rest[n_tok:]
        ex_refs, out_refs, vec_refs = rest[:n_ex], rest[n_ex:n_ex + n_out], rest[n_ex + n_out:n_ex + n_out + n_vec]
        if grp > 1:
            part = sum(lax.dot_general(a_ref[:, s * shard_w:(s + 1) * shard_w].astype(BF16), b_ref[s].astype(BF16), dn,
                                       preferred_element_type=F32) for s in range(grp))
        else:
            part = lax.dot_general(a_ref[...].astype(BF16), b_ref[...].astype(BF16), dn, preferred_element_type=F32)
        first_rows = pl.program_id(0) == 0

        def finish(res):
            outs = (res,) if epilogue is None else epilogue(res, *[e[...] for e in ex_refs])
            for o_ref, val in zip(out_refs, outs[:n_out]):
                o_ref[...] = val.astype(o_ref.dtype)
            for v_ref, val in zip(vec_refs, outs[n_out:]):
                @pl.when(first_rows)
                def _(v_ref=v_ref, val=val):
                    v_ref[...] = val

                @pl.when(jnp.logical_not(first_rows))
                def _(v_ref=v_ref, val=val):
                    v_ref[...] += val

        if nk == 1:
            finish(part)
            return
        acc, k = rest[-1], pl.program_id(2)

        @pl.when(k == 0)
        def _():
            acc[...] = part

        @pl.when(k > 0)
        def _():
            acc[...] += part

        @pl.when(k == nk - 1)
        def _():
            finish(acc[...])

    if a_shards:
        assert mode == "nt" and a_w % tk == 0, (name, mode, a_w, tk)
        per_a = a_w // tk
        a_spec = pl.BlockSpec((None, tm, tk), lambda i, j, k: (lax.div(k, per_a), i, lax.rem(k, per_a)))
    else:
        a_spec = pl.BlockSpec((tk, tm), lambda i, j, k: (k, i)) if mode == "tn" else pl.BlockSpec((tm, tk), lambda i, j, k: (i, k))
    if b_shards and mode != "nt":
        per = shard_w // tn
        b_spec = pl.BlockSpec((None, tk, tn), lambda i, j, k: (lax.div(j, per), k, lax.rem(j, per)))
    elif b_shards and grp > 1:
        b_spec = pl.BlockSpec((grp, tn, shard_w), lambda i, j, k: (k, j, 0))
    elif b_shards:
        per = shard_w // tk
        b_spec = pl.BlockSpec((None, tn, tk), lambda i, j, k: (lax.div(k, per), j, lax.rem(k, per)))
    else:
        b_spec = pl.BlockSpec((tn, tk), lambda i, j, k: (j, k)) if mode == "nt" else pl.BlockSpec((tk, tn), lambda i, j, k: (k, j))
    mn_spec = pl.BlockSpec((tm, tn), lambda i, j, k: (i, j))
    vec_spec = pl.BlockSpec((1, tn), lambda i, j, k: (0, j))
    if out_shards:
        assert not extras
        per_o = (N // N_DEV) // tn
        out_spec = pl.BlockSpec((None, tm, tn), lambda i, j, k: (lax.div(j, per_o), i, lax.rem(j, per_o)))
        out_dims = (N_DEV, M, N // N_DEV)
    else:
        out_spec, out_dims = mn_spec, (M, N)
    outs = pl.pallas_call(
        body, name=name, grid=(M // tm, N // tn, nk),
        in_specs=[a_spec, b_spec] + [pl.BlockSpec((8, LANE), lambda i, j, k: (0, 0))] * n_tok
        + [mn_spec] * len(extras) + [vec_spec] * len(vec_extras),
        out_specs=[out_spec] * n_out + [vec_spec] * n_vec,
        out_shape=[jax.ShapeDtypeStruct(out_dims, dt) for dt in out_dtypes] + [jax.ShapeDtypeStruct((1, N), F32)] * n_vec,
        scratch_shapes=[pltpu.VMEM((tm, tn), F32)] if nk > 1 else [],
        compiler_params=_params(("arbitrary" if n_vec else "parallel", "parallel", "arbitrary"), 56),
    )(a, b, *([] if after is None else [after]), *extras, *vec_extras)
    return outs[0] if n_out + n_vec == 1 else outs


def _rms_fwd(x, g, *, name):
    T, D = x.shape
    tr = _tile(T, 256, 8)

    def body(x_ref, g_ref, h_ref):
        xv = x_ref[...]
        r = lax.rsqrt(jnp.mean(xv * xv, axis=-1, keepdims=True) + EPS)
        h_ref[...] = (xv * r * g_ref[...]).astype(h_ref.dtype)

    return pl.pallas_call(
        body, name=name, grid=(T // tr,),
        in_specs=[pl.BlockSpec((tr, D), lambda i: (i, 0)), pl.BlockSpec((1, D), lambda i: (0, 0))],
        out_specs=pl.BlockSpec((tr, D), lambda i: (i, 0)),
        out_shape=jax.ShapeDtypeStruct((T, D), BF16),
        compiler_params=_params(("parallel",)),
    )(x, g)


def _rms_bwd(x, g, dh, dres, *, name):
    T, D = x.shape
    tr = _tile(T, 256, 8)
    epi = _epi_rms_bwd(2)

    def body(x_ref, g_ref, dh_ref, dres_ref, dx_ref, dxb_ref, dg_ref):
        dx, _, dgp = epi(dh_ref[...], x_ref[...], dres_ref[...], g_ref[...])

        @pl.when(pl.program_id(0) == 0)
        def _():
            dg_ref[...] = jnp.zeros_like(dg_ref)

        dg_ref[...] += dgp
        dx_ref[...] = dx
        dxb_ref[...] = dx.astype(dxb_ref.dtype)

    row = pl.BlockSpec((tr, D), lambda i: (i, 0))
    vec = pl.BlockSpec((1, D), lambda i: (0, 0))
    return pl.pallas_call(
        body, name=name, grid=(T // tr,),
        in_specs=[row, vec, row, row], out_specs=[row, row, vec],
        out_shape=[jax.ShapeDtypeStruct((T, D), F32), jax.ShapeDtypeStruct((T, D), BF16), jax.ShapeDtypeStruct((1, D), F32)],
        compiler_params=_params(("arbitrary",)),
    )(x, g, dh, dres)


ROW_TILE = 256


def _epi_residual_rms(acc, res, g):
    xn = acc + res
    r = lax.rsqrt(jnp.mean(xn * xn, axis=-1, keepdims=True) + EPS)
    return xn, xn * r * g


def _epi_rms_bwd(n_copies):
    def epi(dh, x, dres, g):
        r = lax.rsqrt(jnp.mean(x * x, axis=-1, keepdims=True) + EPS)
        xh = x * r
        dxh = dh * g
        dx = dres + r * (dxh - xh * jnp.mean(dxh * xh, axis=-1, keepdims=True))
        return (dx,) * n_copies + (jnp.sum(dh * xh, axis=0, keepdims=True),)
    return epi


def _final_loss(x, g, tgt, *, name):
    T, D = x.shape
    tr = _tile(T, 256, 8)

    def body(x_ref, g_ref, t_ref, dx_ref, dg_ref, loss_ref):
        xv = x_ref[...]
        r = lax.rsqrt(jnp.mean(xv * xv, axis=-1, keepdims=True) + EPS)
        xh = xv * r
        gv = g_ref[...]
        err = xh * gv - t_ref[...]

        @pl.when(pl.program_id(0) == 0)
        def _():
            dg_ref[...] = jnp.zeros_like(dg_ref)
            loss_ref[...] = jnp.zeros_like(loss_ref)

        part = 0.5 * jnp.sum(jnp.mean(err * err, axis=-1, keepdims=True), axis=0, keepdims=True)
        loss_ref[...] += jnp.broadcast_to(part, loss_ref.shape)
        dy = err * (1.0 / D)
        dg_ref[...] += jnp.sum(dy * xh, axis=0, keepdims=True)
        dxh = dy * gv
        dx_ref[...] = r * (dxh - xh * jnp.mean(dxh * xh, axis=-1, keepdims=True))

    row = pl.BlockSpec((tr, D), lambda i: (i, 0))
    vec = pl.BlockSpec((1, D), lambda i: (0, 0))
    return pl.pallas_call(
        body, name=name, grid=(T // tr,),
        in_specs=[row, vec, row], out_specs=[row, vec, pl.BlockSpec((1, LANE), lambda i: (0, 0))],
        out_shape=[jax.ShapeDtypeStruct((T, D), F32), jax.ShapeDtypeStruct((1, D), F32),
                   jax.ShapeDtypeStruct((1, LANE), F32)],
        compiler_params=_params(("arbitrary",)),
    )(x, g, tgt)


def _ple_bwd(dx3, pp, sg, *, name):
    T, D = dx3.shape
    tr = _tile(T, 256, 8)

    def body(dx_ref, pp_ref, sg_ref, dpg_ref, dpp_ref):
        dx, s = dx_ref[...], sg_ref[...]
        dpg_ref[...] = (dx * pp_ref[...] * s * (1.0 - s)).astype(dpg_ref.dtype)
        dpp_ref[...] = (dx * s).astype(dpp_ref.dtype)

    row = pl.BlockSpec((tr, D), lambda i: (i, 0))
    return pl.pallas_call(
        body, name=name, grid=(T // tr,), in_specs=[row, row, row], out_specs=[row, row],
        out_shape=[jax.ShapeDtypeStruct((T, D), BF16)] * 2, compiler_params=_params(("parallel",)),
    )(dx3, pp, sg)


ROWS_QKV_FWD, ROWS_QKV_BWD, ROWS_FFN_FWD, ROWS_FFN_BWD, ROWS_GROUP_A = 512, 256, 256, 128, 256


def _ext(ref, r0, T, before, after, RC):
    parts = []
    if before:
        p0 = pl.multiple_of(jnp.maximum(r0 - 8, 0), 8)
        parts.append(jnp.where(r0 > 0, ref[pl.ds(p0, 8), :], 0.0))
    parts.append(ref[pl.ds(r0, RC), :])
    if after:
        n0 = pl.multiple_of(jnp.minimum(r0 + RC, T - 8), 8)
        parts.append(jnp.where(r0 + RC < T, ref[pl.ds(n0, 8), :], 0.0))
    return parts[0] if len(parts) == 1 else jnp.concatenate(parts, axis=0)


def _down(xx, s):
    return (xx if s == 0 else pltpu.roll(xx, s, 0))[8:, :]


def _up(xx, s, rows):
    return (xx if s == 0 else pltpu.roll(xx, xx.shape[0] - s, 0))[:rows, :]


def _conv_down(xx, w_ref, K):
    y = None
    for j in range(K):
        t = _down(xx, K - 1 - j) * w_ref[j:j + 1, :]
        y = t if y is None else y + t
    return y


def _fold8(x):
    return jnp.sum(x.reshape(x.shape[0] // 8, 8, x.shape[1]), axis=0)


def _win(ref, r0, lo, n, T, RC, edge):
    if not edge:
        return ref[pl.ds(r0 + lo, n), :]
    xx = _ext(ref, r0, T, True, True, RC)
    a = 8 + lo
    return (xx if a == 0 else pltpu.roll(xx, xx.shape[0] - a, 0))[:n, :]


def _taps(ref, w_ref, K, r0, n, T, RC, edge):
    wins = [_win(ref, r0, -(K - 1 - j), n, T, RC, edge) for j in range(K)]
    y = wins[0] * w_ref[0:1, :]
    for j in range(1, K):
        y = y + wins[j] * w_ref[j:j + 1, :]
    return wins, y


def _untaps(scr_ref, val, w_ref, K, RC):
    scr_ref[0:val.shape[0], :] = val
    y = scr_ref[K - 1:K - 1 + RC, :] * w_ref[0:1, :]
    for j in range(1, K):
        s = K - 1 - j
        y = y + scr_ref[s:s + RC, :] * w_ref[j:j + 1, :]
    return y


def _peeled(n_chunks, RC, step, init):
    carry = step(0, init, True)
    if n_chunks > 2:
        carry = lax.fori_loop(1, n_chunks - 1, lambda i, c: step(pl.multiple_of(i * RC, RC), c, False), carry)
    if n_chunks > 1:
        carry = step((n_chunks - 1) * RC, carry, True)
    return carry


def _silu(x):
    return x * jax.nn.sigmoid(x)


def _dsilu(x):
    s = jax.nn.sigmoid(x)
    return s * (1.0 + x * (1.0 - s))


def _col_specs(T, offs):
    return [pl.BlockSpec((T, LANE), functools.partial(lambda o, j: (0, o + j), o)) for o in offs]


def _group_a_fwd(proj, conv_w, CW, *, name):
    T = proj.shape[0]
    RC = _tile(T, ROWS_GROUP_A, 8)
    nb = CW // LANE
    K = conv_w.shape[0]

    def body(ax_ref, ab_ref, ac_ref, w_ref, y_ref):
        def step(i, carry):
            r0 = pl.multiple_of(i * RC, RC)
            m = _ext(ac_ref, r0, T, True, False, RC) * _ext(ax_ref, r0, T, True, False, RC)
            y_ref[pl.ds(r0, RC), :] = (ab_ref[pl.ds(r0, RC), :] * _conv_down(m, w_ref, K)).astype(y_ref.dtype)
            return carry
        lax.fori_loop(0, T // RC, step, 0)

    return pl.pallas_call(
        body, name=name, grid=(nb,),
        in_specs=_col_specs(T, (0, nb, 2 * nb)) + [pl.BlockSpec((K, LANE), lambda j: (0, j))],
        out_specs=pl.BlockSpec((T, LANE), lambda j: (0, j)),
        out_shape=jax.ShapeDtypeStruct((T, CW), BF16), compiler_params=_params(("parallel",)),
    )(proj, proj, proj, conv_w)


def _group_a_bwd(proj, conv_w, dycat, CW, *, name):
    T = proj.shape[0]
    RC = _tile(T, ROWS_GROUP_A, 8)
    nb = CW // LANE
    K = conv_w.shape[0]

    def body(ax_ref, ab_ref, ac_ref, w_ref, dy_ref, dax_ref, dab_ref, dac_ref, dw_ref):
        def step(i, accs):
            r0 = pl.multiple_of(i * RC, RC)
            ax3 = _ext(ax_ref, r0, T, True, True, RC)
            ac3 = _ext(ac_ref, r0, T, True, True, RC)
            m3 = ax3 * ac3
            c = _conv_down(m3[:RC + 8], w_ref, K)
            dy = dy_ref[pl.ds(r0, RC), :]
            dab_ref[pl.ds(r0, RC), :] = (dy * c).astype(dab_ref.dtype)
            dc2 = _ext(dy_ref, r0, T, False, True, RC) * _ext(ab_ref, r0, T, False, True, RC)
            dm = None
            new = []
            for j in range(K):
                s = K - 1 - j
                t = _up(dc2, s, RC) * w_ref[j:j + 1, :]
                dm = t if dm is None else dm + t
                new.append(accs[j] + _fold8(dc2[:RC] * _down(m3[:RC + 8], s)))
            dax_ref[pl.ds(r0, RC), :] = (dm * ac3[8:RC + 8]).astype(dax_ref.dtype)
            dac_ref[pl.ds(r0, RC), :] = (dm * ax3[8:RC + 8]).astype(dac_ref.dtype)
            return tuple(new)

        accs = lax.fori_loop(0, T // RC, step, tuple(jnp.zeros((8, LANE), F32) for _ in range(K)))
        for j in range(K):
            dw_ref[j:j + 1, :] = jnp.sum(accs[j], axis=0, keepdims=True)

    col = pl.BlockSpec((T, LANE), lambda j: (0, j))
    wsp = pl.BlockSpec((K, LANE), lambda j: (0, j))
    return pl.pallas_call(
        body, name=name, grid=(nb,),
        in_specs=_col_specs(T, (0, nb, 2 * nb)) + [wsp, col],
        out_specs=[col, col, col, wsp],
        out_shape=[jax.ShapeDtypeStruct((T, CW), BF16)] * 3 + [jax.ShapeDtypeStruct((K, CW), F32)],
        compiler_params=_params(("parallel",)),
    )(proj, proj, proj, conv_w, dycat)


def _qkv_fwd(proj, conv_w, off, H, *, name):
    T = proj.shape[0]
    RC = _tile(T, ROWS_QKV_FWD, 8)
    nb = 3 * H
    K = conv_w.shape[0]

    def body(x_ref, w_ref, y_ref):
        j = pl.program_id(0)
        is_qk = j < 2 * H
        scale = jnp.where(j < H, HEAD ** -0.5, 1.0).astype(F32)

        def step(r0, carry, edge):
            s = _silu(_taps(x_ref, w_ref, K, r0, RC, T, RC, edge)[1])
            r = lax.rsqrt(jnp.sum(s * s, axis=-1, keepdims=True) + EPS) * scale
            y_ref[pl.ds(r0, RC), :] = s * jnp.where(is_qk, r, 1.0)
            return carry
        _peeled(T // RC, RC, step, 0)

    return pl.pallas_call(
        body, name=name, grid=(nb,),
        in_specs=_col_specs(T, (off,)) + [pl.BlockSpec((K, LANE), lambda j: (0, j))],
        out_specs=pl.BlockSpec((T, LANE), lambda j: (0, j)),
        out_shape=jax.ShapeDtypeStruct((T, nb * LANE), F32), compiler_params=_params(("parallel",)),
    )(proj, conv_w)


def _qkv_bwd(proj, conv_w, dq, dk, dv, off, H, *, name):
    T = proj.shape[0]
    RC = _tile(T, ROWS_QKV_BWD, 8)
    nb = 3 * H
    K = conv_w.shape[0]

    def body(x_ref, w_ref, dq_ref, dk_ref, dv_ref, dx_ref, dw_ref, scr_ref):
        j = pl.program_id(0)
        is_qk = j < 2 * H
        scale = jnp.where(j < H, HEAD ** -0.5, 1.0).astype(F32)

        def step(r0, accs, edge):
            xs, c2 = _taps(x_ref, w_ref, K, r0, RC + 8, T, RC, edge)
            s2 = _silu(c2)
            dn2 = jnp.where(j < H, _win(dq_ref, r0, 0, RC + 8, T, RC, edge),
                            jnp.where(is_qk, _win(dk_ref, r0, 0, RC + 8, T, RC, edge),
                                      _win(dv_ref, r0, 0, RC + 8, T, RC, edge)))
            r = lax.rsqrt(jnp.sum(s2 * s2, axis=-1, keepdims=True) + EPS)
            nh = s2 * r
            dnp = dn2 * scale
            ds_qk = r * (dnp - nh * jnp.sum(dnp * nh, axis=-1, keepdims=True))
            ds2 = jnp.where(is_qk, ds_qk, dn2)
            dc2 = ds2 * _dsilu(c2)
            dx_ref[pl.ds(r0, RC), :] = _untaps(scr_ref, dc2, w_ref, K, RC).astype(dx_ref.dtype)
            return tuple(accs[jj] + _fold8(dc2[:RC] * xs[jj][:RC]) for jj in range(K))

        accs = _peeled(T // RC, RC, step, tuple(jnp.zeros((8, LANE), F32) for _ in range(K)))
        for jj in range(K):
            dw_ref[jj:jj + 1, :] = jnp.sum(accs[jj], axis=0, keepdims=True)

    col = pl.BlockSpec((T, LANE), lambda j: (0, j))
    wsp = pl.BlockSpec((K, LANE), lambda j: (0, j))
    return pl.pallas_call(
        body, name=name, grid=(nb,),
        in_specs=_col_specs(T, (off,)) + [wsp] + [
            pl.BlockSpec((T, LANE), functools.partial(lambda o, j: (0, jnp.clip(j - o, 0, H - 1)), o)) for o in (0, H, 2 * H)],
        out_specs=[col, wsp],
        out_shape=[jax.ShapeDtypeStruct((T, nb * LANE), BF16), jax.ShapeDtypeStruct((K, nb * LANE), F32)],
        scratch_shapes=[pltpu.VMEM((RC + 8, LANE), F32)],
        compiler_params=_params(("parallel",)),
    )(proj, conv_w, dq, dk, dv)


def _softplus(x):
    return jnp.maximum(x, 0.0) + jnp.log(1.0 + jnp.exp(-jnp.abs(x)))


def _gates_fwd(proj, alog, dtb, off, H, *, name):
    T = proj.shape[0]
    tr = _tile(T, 512, CHUNK)

    def body(ab_ref, al_ref, dt_ref, gb_ref, gam_ref):
        ab = ab_ref[...]
        lane = lax.broadcasted_iota(jnp.int32, ab.shape, 1)
        g = -jnp.exp(al_ref[...]) * _softplus(ab + dt_ref[...])
        gb = jnp.where(lane < H, g, jnp.where(lane < 2 * H, jax.nn.sigmoid(ab), 0.0))
        gb_ref[...] = gb
        tril = _tri().astype(F32)
        for c in range(tr // CHUNK):
            rows = slice(c * CHUNK, (c + 1) * CHUNK)
            gam_ref[rows, :] = _mm(tril, gb[rows, :], precision=lax.Precision.HIGHEST)

    vec = pl.BlockSpec((1, LANE), lambda i: (0, 0))
    row = pl.BlockSpec((tr, LANE), lambda i: (i, 0))
    return pl.pallas_call(
        body, name=name, grid=(T // tr,),
        in_specs=[pl.BlockSpec((tr, LANE), lambda i: (i, off)), vec, vec],
        out_specs=[row, row],
        out_shape=[jax.ShapeDtypeStruct((T, LANE), F32)] * 2, compiler_params=_params(("parallel",)),
    )(proj, alog, dtb)


def _gates_bwd(proj, alog, dtb, dgb, off, H, *, name):
    T = proj.shape[0]
    tr = _tile(T, 512, CHUNK)

    def body(ab_ref, al_ref, dt_ref, d_ref, dab_ref, dal_ref, ddt_ref):
        ab, d = ab_ref[...], d_ref[...]
        lane = lax.broadcasted_iota(jnp.int32, ab.shape, 1)
        is_g = lane < H
        triu = _tri(upper=True).astype(F32)
        dg = jnp.concatenate([_mm(triu, d[c * CHUNK:(c + 1) * CHUNK, :], precision=lax.Precision.HIGHEST)
                              for c in range(tr // CHUNK)], axis=0)
        z = ab + dt_ref[...]
        A = -jnp.exp(al_ref[...])
        da = dg * A * jax.nn.sigmoid(z)
        beta = jax.nn.sigmoid(ab)
        db = d * beta * (1.0 - beta)
        dab_ref[...] = jnp.where(is_g, da, jnp.where(lane < 2 * H, db, 0.0)).astype(dab_ref.dtype)

        @pl.when(pl.program_id(0) == 0)
        def _():
            dal_ref[...] = jnp.zeros_like(dal_ref)
            ddt_ref[...] = jnp.zeros_like(ddt_ref)

        dal_ref[...] += jnp.sum(jnp.where(is_g, dg * A * _softplus(z), 0.0), axis=0, keepdims=True)
        ddt_ref[...] += jnp.sum(jnp.where(is_g, da, 0.0), axis=0, keepdims=True)

    vec = pl.BlockSpec((1, LANE), lambda i: (0, 0))
    row = pl.BlockSpec((tr, LANE), lambda i: (i, 0))
    return pl.pallas_call(
        body, name=name, grid=(T // tr,),
        in_specs=[pl.BlockSpec((tr, LANE), lambda i: (i, off)), vec, vec, row],
        out_specs=[row, vec, vec],
        out_shape=[jax.ShapeDtypeStruct((T, LANE), BF16), jax.ShapeDtypeStruct((1, LANE), F32),
                   jax.ShapeDtypeStruct((1, LANE), F32)],
        compiler_params=_params(("arbitrary",)),
    )(proj, alog, dtb, dgb)


def _gated_norm_fwd(o, proj, gn, zoff, *, name):
    T, W = o.shape
    tr = _tile(T, 512, 8)

    def body(o_ref, z_ref, g_ref, y_ref):
        ov = o_ref[...]
        r = lax.rsqrt(jnp.mean(ov * ov, axis=-1, keepdims=True) + EPS)
        y_ref[...] = (ov * r * g_ref[...] * _silu(z_ref[...])).astype(y_ref.dtype)

    blk = pl.BlockSpec((tr, LANE), lambda i, j: (i, j))
    return pl.pallas_call(
        body, name=name, grid=(T // tr, W // LANE),
        in_specs=[blk, pl.BlockSpec((tr, LANE), lambda i, j: (i, zoff + j)), pl.BlockSpec((1, LANE), lambda i, j: (0, 0))],
        out_specs=blk, out_shape=jax.ShapeDtypeStruct((T, W), BF16), compiler_params=_params(("parallel", "parallel")),
    )(o, proj, gn)


def _gated_norm_bwd(o, proj, gn, dycat, zoff, yoff, *, name):
    T, W = o.shape
    tr = _tile(T, 512, 8)

    def body(o_ref, z_ref, g_ref, dy_ref, do_ref, dz_ref, dg_ref):
        ov, zv, gv, dy = o_ref[...], z_ref[...], g_ref[...], dy_ref[...]
        r = lax.rsqrt(jnp.mean(ov * ov, axis=-1, keepdims=True) + EPS)
        nh = ov * r
        s = _silu(zv)

        @pl.when((pl.program_id(0) == 0) & (pl.program_id(1) == 0))
        def _():
            dg_ref[...] = jnp.zeros_like(dg_ref)

        dg_ref[...] += jnp.sum(dy * nh * s, axis=0, keepdims=True)
        dz_ref[...] = (dy * nh * gv * _dsilu(zv)).astype(dz_ref.dtype)
        dn = dy * gv * s
        do_ref[...] = r * (dn - nh * jnp.mean(dn * nh, axis=-1, keepdims=True))

    blk = pl.BlockSpec((tr, LANE), lambda i, j: (i, j))
    vec = pl.BlockSpec((1, LANE), lambda i, j: (0, 0))
    return pl.pallas_call(
        body, name=name, grid=(T // tr, W // LANE),
        in_specs=[blk, pl.BlockSpec((tr, LANE), lambda i, j: (i, zoff + j)), vec,
                  pl.BlockSpec((tr, LANE), lambda i, j: (i, yoff + j))],
        out_specs=[blk, blk, vec],
        out_shape=[jax.ShapeDtypeStruct((T, W), F32), jax.ShapeDtypeStruct((T, W), BF16),
                   jax.ShapeDtypeStruct((1, LANE), F32)],
        compiler_params=_params(("arbitrary", "arbitrary")),
    )(o, proj, gn, dycat)


def _ffn_act_fwd(up_pre, conv_w, *, name):
    T, F2 = up_pre.shape
    RC = _tile(T, ROWS_FFN_FWD, 8)
    nb = F2 // 2 // LANE
    K = conv_w.shape[0]

    def body(g_ref, v_ref, wg_ref, wv_ref, y_ref):
        def step(r0, carry, edge):
            _, gate = _taps(g_ref, wg_ref, K, r0, RC, T, RC, edge)
            _, val = _taps(v_ref, wv_ref, K, r0, RC, T, RC, edge)
            y_ref[pl.ds(r0, RC), :] = (_silu(gate) * val).astype(y_ref.dtype)
            return carry
        _peeled(T // RC, RC, step, 0)

    return pl.pallas_call(
        body, name=name, grid=(nb,),
        in_specs=_col_specs(T, (0, nb)) + [pl.BlockSpec((K, LANE), lambda j: (0, j)),
                                           pl.BlockSpec((K, LANE), lambda j: (0, nb + j))],
        out_specs=pl.BlockSpec((T, LANE), lambda j: (0, j)),
        out_shape=jax.ShapeDtypeStruct((T, F2 // 2), BF16), compiler_params=_params(("parallel",)),
    )(up_pre, up_pre, conv_w, conv_w)


def _ffn_act_bwd(up_pre, conv_w, dact, *, name):
    T, F2 = up_pre.shape
    RC = _tile(T, ROWS_FFN_BWD, 8)
    nb = F2 // 2 // LANE
    K = conv_w.shape[0]

    def body(g_ref, v_ref, wg_ref, wv_ref, da_ref, d_ref, dwg_ref, dwv_ref, sg_ref, sv_ref):
        def step(r0, accs, edge):
            gs, gate2 = _taps(g_ref, wg_ref, K, r0, RC + 8, T, RC, edge)
            vs, val2 = _taps(v_ref, wv_ref, K, r0, RC + 8, T, RC, edge)
            da2 = _win(da_ref, r0, 0, RC + 8, T, RC, edge)
            dgate2 = da2 * val2 * _dsilu(gate2)
            dval2 = da2 * _silu(gate2)
            d_ref[0, pl.ds(r0, RC), :] = _untaps(sg_ref, dgate2, wg_ref, K, RC).astype(d_ref.dtype)
            d_ref[1, pl.ds(r0, RC), :] = _untaps(sv_ref, dval2, wv_ref, K, RC).astype(d_ref.dtype)
            new = []
            for j in range(K):
                new.append(accs[2 * j] + _fold8(dgate2[:RC] * gs[j][:RC]))
                new.append(accs[2 * j + 1] + _fold8(dval2[:RC] * vs[j][:RC]))
            return tuple(new)

        accs = _peeled(T // RC, RC, step, tuple(jnp.zeros((8, LANE), F32) for _ in range(2 * K)))
        for j in range(K):
            dwg_ref[j:j + 1, :] = jnp.sum(accs[2 * j], axis=0, keepdims=True)
            dwv_ref[j:j + 1, :] = jnp.sum(accs[2 * j + 1], axis=0, keepdims=True)

    col = pl.BlockSpec((T, LANE), lambda j: (0, j))
    wsp = pl.BlockSpec((K, LANE), lambda j: (0, j))
    return pl.pallas_call(
        body, name=name, grid=(nb,),
        in_specs=_col_specs(T, (0, nb)) + [wsp, pl.BlockSpec((K, LANE), lambda j: (0, nb + j)), col],
        out_specs=[pl.BlockSpec((2, T, LANE), lambda j: (0, 0, j)), wsp, wsp],
        out_shape=[jax.ShapeDtypeStruct((2, T, F2 // 2), BF16)] + [jax.ShapeDtypeStruct((K, F2 // 2), F32)] * 2,
        scratch_shapes=[pltpu.VMEM((RC + 8, LANE), F32)] * 2,
        compiler_params=_params(("parallel",)),
    )(up_pre, up_pre, conv_w, conv_w, dact)


CPB = 8
CPB_SCAN = 4
GRP = 8
HP = lax.Precision.HIGH


def _tri(strict=False, upper=False):
    r = lax.broadcasted_iota(jnp.int32, (CHUNK, CHUNK), 0)
    c = lax.broadcasted_iota(jnp.int32, (CHUNK, CHUNK), 1)
    if upper:
        return c >= r
    return (r > c) if strict else (r >= c)


def _mm(a, b, dn="nn", precision=None):
    precision = HP if precision is None else precision
    return lax.dot_general(a, b, _DN[dn], precision=precision, preferred_element_type=F32)


def _mm16(a, b, dn="nn"):
    return lax.dot_general(a.astype(BF16), b.astype(BF16), _DN[dn], preferred_element_type=F32)


def _each(f, *cols):
    return [f(*xs) for xs in zip(*cols)]


def _decay(gam):
    return jnp.exp(jnp.where(_tri(), gam[:, :CHUNK] - gam.T[:CHUNK, :], -1e30))


def _delta_specs(T, H, cpb):
    rows = cpb * CHUNK
    col = lambda o: pl.BlockSpec((rows, LANE), functools.partial(lambda o, h, n: (n, o + h), o))
    bc = pl.BlockSpec((1, rows, LANE), lambda h, n: (h, n, 0))
    sq = pl.BlockSpec((1, cpb, CHUNK, CHUNK), lambda h, n: (h, n, 0, 0))
    vec = pl.BlockSpec((1, cpb, 1, LANE), lambda h, n: (h, n, 0, 0))
    return col, bc, sq, vec


def _delta_prep_fwd(qkv, gamB, bB, H, *, name):
    T = qkv.shape[0]
    N = T // CHUNK
    cpb = _tile(N, CPB, 8)
    grp = min(GRP, cpb)
    col, bc, sq, vec = _delta_specs(T, H, cpb)

    def body(q_ref, k_ref, v_ref, g_ref, b_ref, u_ref, w_ref, qd_ref, kd_ref, qk_ref, ti_ref, gl_ref):
        eye = (lax.broadcasted_iota(jnp.int32, (CHUNK, CHUNK), 0) == lax.broadcasted_iota(jnp.int32, (CHUNK, CHUNK), 1)).astype(F32)
        strict = _tri(strict=True)
        for c0 in range(0, cpb, grp):
            cs = list(range(c0, c0 + grp))
            rows = [slice(c * CHUNK, (c + 1) * CHUNK) for c in cs]
            q, k, v = ([r_[r, :] for r in rows] for r_ in (q_ref, k_ref, v_ref))
            bb = [b_ref[0, r, :] for r in rows]
            gam = [g_ref[0, r, :] for r in rows]
            D = _each(_decay, gam)
            e = _each(jnp.exp, gam)
            kk = _each(lambda k_: _mm16(k_, k_, "nt"), k)
            X = _each(lambda kk_, D_, b_: -(jnp.where(strict, kk_ * D_, 0.0) * b_[:, :CHUNK]), kk, D, bb)
            R = _each(lambda x: eye + x, X)
            for _ in range(5):
                X = _each(lambda x: _mm(x, x), X)
                R = _each(lambda r, x: r + _mm(r, x), R, X)
            u = _each(lambda r, b_, v_: _mm(r, b_ * v_), R, bb, v)
            w = _each(lambda r, b_, e_, k_: _mm(r, b_ * e_ * k_), R, bb, e, k)
            qk = _each(lambda q_, k_, D_: _mm16(q_, k_, "nt") * D_, q, k, D)
            for i, c in enumerate(cs):
                glast = gam[i][CHUNK - 1:CHUNK, :]
                u_ref[rows[i], :] = u[i]
                w_ref[rows[i], :] = w[i]
                qd_ref[rows[i], :] = e[i] * q[i]
                kd_ref[rows[i], :] = jnp.exp(glast - gam[i]) * k[i]
                qk_ref[0, c] = qk[i]
                ti_ref[0, c] = R[i]
                gl_ref[0, c] = jnp.exp(glast)

    full = jax.ShapeDtypeStruct((T, H * LANE), F32)
    sqs = jax.ShapeDtypeStruct((H, N, CHUNK, CHUNK), F32)
    return pl.pallas_call(
        body, name=name, grid=(H, N // cpb),
        in_specs=[col(0), col(H), col(2 * H), bc, bc],
        out_specs=[col(0)] * 4 + [sq, sq, vec],
        out_shape=[full] * 4 + [sqs, sqs, jax.ShapeDtypeStruct((H, N, 1, LANE), F32)],
        compiler_params=_params(("parallel", "parallel")),
    )(qkv, qkv, qkv, gamB, bB)


def _scan_specs(H, N, cpb, hb, rev):
    nbk = N // cpb
    blk = (lambda n: nbk - 1 - n) if rev else (lambda n: n)
    col = pl.BlockSpec((cpb * CHUNK, hb * LANE), lambda h, n: (blk(n), h))
    sq = pl.BlockSpec((hb, cpb, CHUNK, CHUNK), lambda h, n: (h, blk(n), 0, 0))
    vec = pl.BlockSpec((hb, cpb, 1, LANE), lambda h, n: (h, blk(n), 0, 0))
    st = pl.BlockSpec((hb, cpb, HEAD, HEAD), lambda h, n: (h, blk(n), 0, 0))
    return col, sq, vec, st


def _delta_scan_fwd(u, w, qd, kd, qk, gl, H, *, name):
    T = u.shape[0]
    N = T // CHUNK
    cpb = _tile(N, CPB_SCAN, 4)
    hb = min(GRP, H)
    col, sq, vec, st = _scan_specs(H, N, cpb, hb, False)
    lanes = [slice(j * LANE, (j + 1) * LANE) for j in range(hb)]
    heads = list(range(hb))

    def body(u_ref, w_ref, qd_ref, kd_ref, qk_ref, gl_ref, o_ref, vn_ref, ss_ref, s_scr):
        @pl.when(pl.program_id(1) == 0)
        def _():
            s_scr[...] = jnp.zeros_like(s_scr)

        def step(c, states):
            rows = pl.ds(pl.multiple_of(c * CHUNK, CHUNK), CHUNK)
            S = list(states)
            for j in heads:
                ss_ref[j, c] = S[j]
            wS = _each(lambda ln, s: _mm16(w_ref[rows, ln], s), lanes, S)
            qS = _each(lambda ln, s: _mm16(qd_ref[rows, ln], s), lanes, S)
            vn = _each(lambda ln, ws: u_ref[rows, ln] - ws, lanes, wS)
            o = _each(lambda j, qs, vn_: qs + _mm16(qk_ref[j, c], vn_), heads, qS, vn)
            new = _each(lambda j, ln, s, vn_: s * gl_ref[j, c] + _mm16(kd_ref[rows, ln], vn_, "tn"),
                        heads, lanes, S, vn)
            for j in heads:
                o_ref[rows, lanes[j]] = o[j]
                vn_ref[rows, lanes[j]] = vn[j]
            return tuple(new)
        out = lax.fori_loop(0, cpb, step, tuple(s_scr[j] for j in heads))
        for j in heads:
            s_scr[j] = out[j]

    full = jax.ShapeDtypeStruct((T, H * LANE), F32)
    return pl.pallas_call(
        body, name=name, grid=(H // hb, N // cpb),
        in_specs=[col] * 4 + [sq, vec],
        out_specs=[col, col, st],
        out_shape=[full, full, jax.ShapeDtypeStruct((H, N, HEAD, HEAD), F32)],
        scratch_shapes=[pltpu.VMEM((hb, HEAD, HEAD), F32)],
        compiler_params=_params(("parallel", "arbitrary")),
    )(u, w, qd, kd, qk, gl)


def _delta_scan_bwd(do, w, qd, kd, vn, qk, gl, ss, H, *, name):
    T = do.shape[0]
    N = T // CHUNK
    cpb = _tile(N, CPB_SCAN, 4)
    hb = min(GRP, H)
    col, sq, vec, st = _scan_specs(H, N, cpb, hb, True)
    lanes = [slice(j * LANE, (j + 1) * LANE) for j in range(hb)]
    heads = list(range(hb))

    def body(do_ref, w_ref, qd_ref, kd_ref, vn_ref, qk_ref, gl_ref, ss_ref,
             du_ref, dw_ref, dqd_ref, dkd_ref, dqk_ref, dgl_ref, ds_scr):
        @pl.when(pl.program_id(1) == 0)
        def _():
            ds_scr[...] = jnp.zeros_like(ds_scr)

        def step(i, dstates):
            c = cpb - 1 - i
            rows = pl.ds(pl.multiple_of(c * CHUNK, CHUNK), CHUNK)
            dS = list(dstates)
            S = [ss_ref[j, c] for j in heads]
            dov = [do_ref[rows, ln] for ln in lanes]
            vnv = [vn_ref[rows, ln] for ln in lanes]
            a1 = _each(lambda j, d_: _mm16(qk_ref[j, c], d_, "tn"), heads, dov)
            a2 = _each(lambda ln, ds: _mm16(kd_ref[rows, ln], ds), lanes, dS)
            dvn = _each(lambda x, y: x + y, a1, a2)
            dqd = _each(lambda d_, s: _mm16(d_, s, "nt"), dov, S)
            dkd = _each(lambda v_, ds: _mm16(v_, ds, "nt"), vnv, dS)
            dqk = _each(lambda d_, v_: _mm16(d_, v_, "nt"), dov, vnv)
            dw = _each(lambda dv_, s: -_mm16(dv_, s, "nt"), dvn, S)
            b1 = _each(lambda ln, d_: _mm16(qd_ref[rows, ln], d_, "tn"), lanes, dov)
            b2 = _each(lambda ln, dv_: _mm16(w_ref[rows, ln], dv_, "tn"), lanes, dvn)
            new = _each(lambda j, x, y, ds: x + ds * gl_ref[j, c] - y, heads, b1, b2, dS)
            for j in heads:
                du_ref[rows, lanes[j]] = dvn[j]
                dw_ref[rows, lanes[j]] = dw[j]
                dqd_ref[rows, lanes[j]] = dqd[j]
                dkd_ref[rows, lanes[j]] = dkd[j]
                dqk_ref[j, c] = dqk[j]
                dgl = jnp.sum(jnp.sum(dS[j] * S[j], axis=1, keepdims=True), axis=0, keepdims=True)
                dgl_ref[j, c] = jnp.broadcast_to(dgl, (1, LANE))
            return tuple(new)
        out = lax.fori_loop(0, cpb, step, tuple(ds_scr[j] for j in heads))
        for j in heads:
            ds_scr[j] = out[j]

    full = jax.ShapeDtypeStruct((T, H * LANE), F32)
    return pl.pallas_call(
        body, name=name, grid=(H // hb, N // cpb),
        in_specs=[col] * 5 + [sq, vec, st],
        out_specs=[col] * 4 + [sq, vec],
        out_shape=[full] * 4 + [jax.ShapeDtypeStruct((H, N, CHUNK, CHUNK), F32), jax.ShapeDtypeStruct((H, N, 1, LANE), F32)],
        scratch_shapes=[pltpu.VMEM((hb, HEAD, HEAD), F32)],
        compiler_params=_params(("parallel", "arbitrary")),
    )(do, w, qd, kd, vn, qk, gl, ss)


def _delta_prep_bwd(qkv, gamB, bB, ti, u, w, qk, du, dw, dqd, dkd, dqk, dgl, H, *, name):
    T = qkv.shape[0]
    N = T // CHUNK
    cpb = _tile(N, CPB, 8)
    grp = min(GRP, cpb)
    col, bc, sq, vec = _delta_specs(T, H, cpb)

    def body(q_ref, k_ref, v_ref, g_ref, b_ref, ti_ref, u_ref, w_ref, qk_ref,
             du_ref, dw_ref, dqd_ref, dkd_ref, dqk_ref, dgl_ref,
             dq_ref, dk_ref, dv_ref, dg_ref, db_ref):
        ones = jnp.ones((CHUNK, LANE), F32)
        strict = _tri(strict=True)
        last = lax.broadcasted_iota(jnp.int32, (CHUNK, LANE), 0) == CHUNK - 1
        lsum = lambda x: jnp.sum(x, axis=-1, keepdims=True)
        for c0 in range(0, cpb, grp):
            cs = list(range(c0, c0 + grp))
            rows = [slice(c * CHUNK, (c + 1) * CHUNK) for c in cs]
            ld = lambda r_: [r_[r, :] for r in rows]
            q, k, v, uv, wv, duv, dwv, dqd_v, dkd_v = (ld(r_) for r_ in (q_ref, k_ref, v_ref, u_ref, w_ref, du_ref, dw_ref, dqd_ref, dkd_ref))
            bb = [b_ref[0, r, :] for r in rows]
            gam = [g_ref[0, r, :] for r in rows]
            Ti = [ti_ref[0, c] for c in cs]
            QK = [qk_ref[0, c] for c in cs]
            dqk_v = [dqk_ref[0, c] for c in cs]
            D = _each(_decay, gam)
            e = _each(jnp.exp, gam)
            glast = [g_[CHUNK - 1:CHUNK, :] for g_ in gam]
            eL = _each(lambda gl_, g_: jnp.exp(gl_ - g_), glast, gam)
            kk = _each(lambda k_: _mm16(k_, k_, "nt"), k)
            KKD = _each(lambda kk_, D_: jnp.where(strict, kk_ * D_, 0.0), kk, D)
            dru = _each(lambda t, d_: _mm(t, d_, "tn"), Ti, duv)
            drw = _each(lambda t, d_: _mm(t, d_, "tn"), Ti, dwv)
            l1 = _each(lambda a, b: _mm(a, b, "nt"), dru, uv)
            l2 = _each(lambda a, b: _mm(a, b, "nt"), drw, wv)
            dL = _each(lambda a, b: jnp.where(strict, -(a + b), 0.0), l1, l2)
            Mm = _each(lambda dl, b_: dl * b_[:, :CHUNK], dL, bb)
            dKK = _each(lambda m_, D_: m_ * D_, Mm, D)
            dQK = _each(lambda a, D_: a * D_, dqk_v, D)
            P = _each(lambda m_, kkd, a, qk_: m_ * kkd + a * qk_, Mm, KKD, dqk_v, QK)
            q1 = _each(lambda a, k_: _mm16(a, k_), dQK, k)
            k1 = _each(lambda a, q_: _mm16(a, q_, "tn"), dQK, q)
            k2 = _each(lambda a, k_: _mm16(a, k_), dKK, k)
            k3 = _each(lambda a, k_: _mm16(a, k_, "tn"), dKK, k)
            s1 = _each(lambda dl, kkd: _mm(dl * kkd, ones), dL, KKD)
            p1 = _each(lambda p_: _mm(p_, ones), P)
            p2 = _each(lambda p_: _mm(p_, ones, "tn"), P)
            for i, c in enumerate(cs):
                r = rows[i]
                bek = bb[i] * e[i]
                kdv = eL[i] * k[i]
                dq_ref[r, :] = q1[i] + e[i] * dqd_v[i]
                dk_ref[r, :] = k1[i] + k2[i] + k3[i] + bek * drw[i] + eL[i] * dkd_v[i]
                dv_ref[r, :] = bb[i] * dru[i]
                db_ref[0, r, :] = s1[i] + lsum(dru[i] * v[i]) + lsum(drw[i] * e[i] * k[i])
                dgam = (p1[i] - p2[i] + lsum(drw[i] * bek * k[i]) + lsum(dqd_v[i] * e[i] * q[i])
                        - lsum(dkd_v[i] * kdv))
                xlast = jnp.sum(lsum(dkd_v[i] * kdv), axis=0, keepdims=True) + jnp.exp(glast[i]) * dgl_ref[0, c]
                dg_ref[0, r, :] = dgam + jnp.where(last, xlast, 0.0)

    full = jax.ShapeDtypeStruct((T, H * LANE), F32)
    bcs = jax.ShapeDtypeStruct((H, T, LANE), F32)
    return pl.pallas_call(
        body, name=name, grid=(H, N // cpb),
        in_specs=[col(0), col(H), col(2 * H), bc, bc, sq, col(0), col(0), sq, col(0), col(0), col(0), col(0), sq, vec],
        out_specs=[col(0), col(0), col(0), bc, bc],
        out_shape=[full, full, full, bcs, bcs],
        compiler_params=_params(("parallel", "parallel")),
    )(qkv, qkv, qkv, gamB, bB, ti, u, w, qk, du, dw, dqd, dkd, dqk, dgl)


def _adam(parts, w, m, v, *, name, own=None, me=None):
    P, R, C = parts.shape
    if R > 256 and R % 8:
        tr, tc = R, _tile(C, 256)
    else:
        tr, tc = _tile(R, 256, 8), C
    n_own = 0 if own is None else 2

    def body(*refs):
        p_ref, w_ref, m_ref, v_ref, g_ref, d_ref, nm_ref, nv_ref = refs[n_own:]
        g = None
        for i in range(P):
            t = p_ref[i].astype(F32)
            if n_own:
                t = jnp.where(refs[0][0] == i, refs[1][...].astype(F32), t)
            g = t if g is None else g + t
        mn = ADAM_B1 * m_ref[...] + (1.0 - ADAM_B1) * g
        vn = ADAM_B2 * v_ref[...] + (1.0 - ADAM_B2) * (g * g)
        m_hat = mn / (1.0 - ADAM_B1 ** ADAM_STEP)
        v_hat = vn / (1.0 - ADAM_B2 ** ADAM_STEP)
        g_ref[...] = g
        d_ref[...] = -ADAM_LR * (m_hat / (jnp.sqrt(v_hat) + ADAM_EPS) + ADAM_WD * w_ref[...])
        nm_ref[...] = mn
        nv_ref[...] = vn

    blk = pl.BlockSpec((tr, tc), lambda i, j: (i, j))
    return pl.pallas_call(
        body, name=name, grid=(R // tr, C // tc),
        in_specs=[pl.BlockSpec(memory_space=pltpu.SMEM), blk][:n_own] + [pl.BlockSpec((P, tr, tc), lambda i, j: (0, i, j)), blk, blk, blk],
        out_specs=[blk] * 4, out_shape=[jax.ShapeDtypeStruct((R, C), F32)] * 4,
        compiler_params=_params(("parallel", "parallel")),
    )(*([me, own] if n_own else []), parts, w, m, v)


def _mesh_pos():
    return lax.axis_index("x"), lax.axis_index("y"), lax.axis_index("c")


def _peer(k):
    x, y, c = _mesh_pos()
    px, py, pc = x ^ ((k >> 2) & 1), y ^ ((k >> 1) & 1), c ^ (k & 1)
    return (px, py, pc), 4 * px + 2 * py + pc


def _exchange(arrays, scatter, *, name, after=None):
    n = len(arrays)
    n_in = n if after is None else n + 1
    blocks = [a.shape[1:] if scatter else a.shape for a in arrays]

    def body(*refs):
        srcs, dsts = refs[:n], refs[n_in:n_in + n]
        send_sems, recv_sems, local_sems = refs[n_in + n:]
        x, y, c = _mesh_pos()
        me = 4 * x + 2 * y + c
        local, sends = [], []
        for a in range(n):
            cp = pltpu.make_async_copy(srcs[a].at[me] if scatter else srcs[a], dsts[a].at[me], local_sems.at[a])
            cp.start()
            local.append(cp)
            for k in range(1, N_DEV):
                dev, idx = _peer(k)
                cp = pltpu.make_async_remote_copy(
                    src_ref=srcs[a].at[idx] if scatter else srcs[a], dst_ref=dsts[a].at[me],
                    send_sem=send_sems.at[a * N_DEV + k], recv_sem=recv_sems.at[a * N_DEV + k],
                    device_id=dev, device_id_type=MESH)
                cp.start()
                sends.append(cp)
        for a in range(n):
            for k in range(1, N_DEV):
                dev, idx = _peer(k)
                pltpu.make_async_remote_copy(
                    src_ref=srcs[a].at[idx] if scatter else srcs[a], dst_ref=dsts[a].at[idx],
                    send_sem=send_sems.at[a * N_DEV + k], recv_sem=recv_sems.at[a * N_DEV + k],
                    device_id=dev, device_id_type=MESH).wait_recv()
        for cp in sends:
            cp.wait_send()
        for cp in local:
            cp.wait()

    anyspec = pl.BlockSpec(memory_space=pl.ANY)
    return pl.pallas_call(
        body, name=name, in_specs=[anyspec] * n_in, out_specs=[anyspec] * n,
        out_shape=[jax.ShapeDtypeStruct((N_DEV,) + tuple(b), a.dtype) for a, b in zip(arrays, blocks)],
        scratch_shapes=[pltpu.SemaphoreType.DMA((n * N_DEV,)), pltpu.SemaphoreType.DMA((n * N_DEV,)),
                        pltpu.SemaphoreType.DMA((n,))],
    )(*arrays, *([] if after is None else [after]))


_ANY = pl.BlockSpec(memory_space=pl.ANY)
_SEM = pl.BlockSpec(memory_space=pltpu.SEMAPHORE)
_EFFECT = pltpu.SideEffectType.DATAFLOW_SIDE_EFFECTING


def _in_hbm(a):
    return pltpu.with_memory_space_constraint(a, pltpu.HBM)


def _split_copy(src, land, send, recv, k, me, scatter, landed):
    dev, idx = _peer(k)
    return pltpu.make_async_remote_copy(
        src_ref=src.at[idx] if scatter else src, dst_ref=land.at[idx if landed else me],
        send_sem=send.at[k], recv_sem=recv.at[k], device_id=dev, device_id_type=MESH)


ALL_PEERS = tuple(range(1, N_DEV))
SIBLING = 1
SAME_CORE = (2, 4, 6)


def _split_start(srcs, lands, scatter, *, name, relations=None):
    n = len(srcs)
    relations = relations or [ALL_PEERS] * n

    def body(*refs):
        src, land, send, recv, token = refs[:n], refs[n:2 * n], refs[2 * n:3 * n], refs[3 * n:4 * n], refs[-1]
        x, y, c = _mesh_pos()
        me = 4 * x + 2 * y + c
        for a in range(n):
            for k in relations[a]:
                _split_copy(src[a], land[a], send[a], recv[a], k, me, scatter, False).start()
        token[...] = jnp.zeros_like(token)

    outs = pl.pallas_call(
        body, name=name,
        out_shape=[pltpu.SemaphoreType.DMA((N_DEV,))] * (2 * n) + [pltpu.HBM(t.shape, t.dtype) for t in list(srcs) + list(lands)]
        + [jax.ShapeDtypeStruct((8, LANE), F32)],
        in_specs=[_ANY] * (2 * n), out_specs=[_SEM] * (2 * n) + [_ANY] * (2 * n) + [pl.BlockSpec(memory_space=pltpu.VMEM)],
        input_output_aliases={i: 2 * n + i for i in range(2 * n)},
        compiler_params=pltpu.CompilerParams(has_side_effects=_EFFECT),
    )(*[_in_hbm(t) for t in list(srcs) + list(lands)])
    handles = [(outs[a], outs[n + a], outs[2 * n + a], outs[3 * n + a]) for a in range(n)]
    return handles, outs[-1]


def _split_wait(handle, after, scatter, *, name):
    send, recv, src_thru, land_thru = handle

    def body(src_ref, land_ref, send_ref, recv_ref, after_ref, src_out, land_out):
        x, y, c = _mesh_pos()
        me = 4 * x + 2 * y + c
        for k in range(1, N_DEV):
            cp = _split_copy(src_ref, land_ref, send_ref, recv_ref, k, me, scatter, True)
            cp.wait_send()
            cp.wait_recv()

    return pl.pallas_call(
        body, name=name,
        out_shape=(pltpu.HBM(src_thru.shape, src_thru.dtype), pltpu.HBM(land_thru.shape, land_thru.dtype)),
        in_specs=(_ANY, _ANY, _SEM, _SEM, _ANY), out_specs=(_ANY, _ANY), input_output_aliases={0: 0, 1: 1},
        compiler_params=pltpu.CompilerParams(has_side_effects=_EFFECT),
    )(src_thru, land_thru, send, recv, after)[1]


def _forward_copy(land, fsend, frecv, k, landed):
    x, y, c = _mesh_pos()
    _, idx = _peer(k | SIBLING if landed else k)
    return pltpu.make_async_remote_copy(src_ref=land.at[idx], dst_ref=land.at[idx], send_sem=fsend.at[k],
                                        recv_sem=frecv.at[k], device_id=(x, y, 1 - c), device_id_type=MESH)


def _gather_forward(handle, after, *, name):
    send, recv, src_thru, land_thru = handle

    def body(src_ref, land_ref, send_ref, recv_ref, after_ref, src_out, land_out, fsend, frecv):
        x, y, c = _mesh_pos()
        me = 4 * x + 2 * y + c
        for k in SAME_CORE:
            _split_copy(src_ref, land_ref, send_ref, recv_ref, k, me, False, True).wait_recv()
            _forward_copy(land_ref, fsend, frecv, k, False).start()

    src2, land2, fsend, frecv = pl.pallas_call(
        body, name=name,
        out_shape=(pltpu.HBM(src_thru.shape, src_thru.dtype), pltpu.HBM(land_thru.shape, land_thru.dtype),
                   pltpu.SemaphoreType.DMA((N_DEV,)), pltpu.SemaphoreType.DMA((N_DEV,))),
        in_specs=(_ANY, _ANY, _SEM, _SEM, _ANY), out_specs=(_ANY, _ANY, _SEM, _SEM), input_output_aliases={0: 0, 1: 1},
        compiler_params=pltpu.CompilerParams(has_side_effects=_EFFECT),
    )(src_thru, land_thru, send, recv, after)
    return (send, recv, src2, land2), (fsend, frecv)


def _gather_wait_two_level(handle, fwd, *, name):
    send, recv, src_thru, land_thru = handle
    fsend, frecv = fwd

    def body(src_ref, land_ref, send_ref, recv_ref, fsend_ref, frecv_ref, src_out, land_out):
        x, y, c = _mesh_pos()
        me = 4 * x + 2 * y + c
        for k in (SIBLING,) + SAME_CORE:
            _split_copy(src_ref, land_ref, send_ref, recv_ref, k, me, False, True).wait_send()
        _split_copy(src_ref, land_ref, send_ref, recv_ref, SIBLING, me, False, True).wait_recv()
        for k in SAME_CORE:
            _forward_copy(land_ref, fsend_ref, frecv_ref, k, False).wait_send()
            _forward_copy(land_ref, fsend_ref, frecv_ref, k, True).wait_recv()

    return pl.pallas_call(
        body, name=name,
        out_shape=(pltpu.HBM(src_thru.shape, src_thru.dtype), pltpu.HBM(land_thru.shape, land_thru.dtype)),
        in_specs=(_ANY, _ANY, _SEM, _SEM, _SEM, _SEM), out_specs=(_ANY, _ANY), input_output_aliases={0: 0, 1: 1},
        compiler_params=pltpu.CompilerParams(has_side_effects=_EFFECT),
    )(src_thru, land_thru, send, recv, fsend, frecv)[1]


def _local_step(x, p, tgt, S, wt, conv, emit):
    T, D = x.shape
    CW = DNW = D // 2
    H = DNW // HEAD
    nA, nD = CW // LANE, DNW // LANE
    qkv_off, z_off, ab_off = 3 * nA, 3 * nA + 3 * nD, 3 * nA + 4 * nD
    alog = jnp.pad(S["a_log"], ((0, 0), (0, LANE - H)))
    dtb = jnp.pad(S["dt_bias"], ((0, 0), (0, LANE - H)))

    h1 = _rms_fwd(x, S["g_mix"], name="rms1_fwd")
    pp = _matmul(p, wt("w_pp", h1), "nn", name="mm_pp", b_shards=True)
    w_in, cv = wt("w_in", pp), conv(pp)
    proj = _matmul(h1, w_in, "nt", name="mm_in")
    y_a = _group_a_fwd(proj, cv["conv_a"], CW, name="group_a_fwd")
    qkv = _qkv_fwd(proj, cv["conv_qkv"], qkv_off, H, name="qkv_fwd")
    gb, gamc = _gates_fwd(proj, alog, dtb, ab_off, H, name="gates_fwd")
    bcast = lambda cols: jnp.broadcast_to(cols.T[:, :, None], (H, T, LANE))
    gamB, bB = bcast(gamc[:, :H]), bcast(gb[:, H:2 * H])
    u, w, qd, kd, qk, ti, gl = _delta_prep_fwd(qkv, gamB, bB, H, name="delta_prep_fwd")
    o, vn, ss = _delta_scan_fwd(u, w, qd, kd, qk, gl, H, name="delta_scan_fwd")
    y_b = _gated_norm_fwd(o, proj, S["dn_g"], z_off, name="gated_norm_fwd")
    ycat = jnp.concatenate([y_a, y_b], axis=1)
    w_out = wt("w_out", ycat)
    rows = dict(tm=ROW_TILE, tn=D)
    x1, h2 = _matmul(ycat, w_out, "nn", name="mm_out", out_dtypes=(F32, BF16), epilogue=_epi_residual_rms,
                     extras=(x,), vec_extras=(S["g_ffn"],), **rows)
    w_up = wt("w_up", h2)
    up_pre = _matmul(h2, w_up, "nn", name="mm_up", b_shards=True, tn=SHARD_TILE)
    act = _ffn_act_fwd(up_pre, cv["conv_ffn"], name="ffn_act_fwd")
    w_down = wt("w_down", act)
    x2 = _matmul(act, w_down, "nn", name="mm_down", epilogue=lambda acc, r: (acc + r,), extras=(x1,), tk=LONG_K)
    h3 = _rms_fwd(x2, S["g_ple"], name="rms3_fwd")
    w_pg = wt("w_pg", h3)

    def ple_epi(acc, x2r, ppr):
        s = jax.nn.sigmoid(acc)
        return x2r + s * ppr, s

    x3, sg = _matmul(h3, w_pg, "nn", name="mm_pg", out_dtypes=(F32, F32), epilogue=ple_epi, extras=(x2, pp), tm=512)
    dx3, dg_final, loss = _final_loss(x3, S["g_final"], tgt, name="final_loss")

    G = {"g_final": dg_final}
    dpg, dpp = _ple_bwd(dx3, pp, sg, name="ple_bwd")
    tok = emit({"w_pp": _matmul(p, dpp, "tn", name="mm_dwpp", out_dtypes=(BF16,), out_shards=True, tk=LONG_K),
                "w_pg": _matmul(h3, dpg, "tn", name="mm_dwpg", out_dtypes=(BF16,), tk=LONG_K)})
    bwd = dict(out_dtypes=(F32, BF16), epilogue=_epi_rms_bwd(2), n_vec=1, **rows)
    dx2, dx2b, G["g_ple"] = _matmul(dpg, w_pg, "nt", name="mm_dh3", after=tok, extras=(x2, dx3),
                                    vec_extras=(S["g_ple"],), **bwd)
    tok = emit({"w_down": _matmul(act, dx2b, "tn", name="mm_dwdown", out_dtypes=(BF16,), tk=LONG_K)})
    dact = _matmul(dx2b, w_down, "nt", name="mm_dact", after=tok, tn=SHARD_TILE)
    dup, dcf_g, dcf_v = _ffn_act_bwd(up_pre, cv["conv_ffn"], dact, name="ffn_act_bwd")
    G["conv_ffn"] = jnp.concatenate([dcf_g, dcf_v], axis=1)
    tok = emit({"w_up": _matmul(h2, dup, "tn", name="mm_dwup", out_dtypes=(BF16,), b_shards=True, out_shards=True,
                                tn=SHARD_TILE, tk=LONG_K)})
    dh2 = _matmul(dup, w_up, "nt", name="mm_dh2", after=tok, a_shards=True, b_shards=True, tk=2 * SHARD_TILE)
    dx1, dx1b, G["g_ffn"] = _rms_bwd(x1, S["g_ffn"], dh2, dx2, name="rms2_bwd")
    tok = emit({"w_out": _matmul(ycat, dx1b, "tn", name="mm_dwout", out_dtypes=(BF16,), tk=LONG_K)})
    dycat = _matmul(dx1b, w_out, "nt", name="mm_dycat", after=tok)
    do, dz, G["dn_g"] = _gated_norm_bwd(o, proj, S["dn_g"], dycat, z_off, nA, name="gated_norm_bwd")
    du, dw, dqd, dkd, dqk, dgl = _delta_scan_bwd(do, w, qd, kd, vn, qk, gl, ss, H, name="delta_scan_bwd")
    dq, dk, dv, dgB, dbB = _delta_prep_bwd(qkv, gamB, bB, ti, u, w, qk, du, dw, dqd, dkd, dqk, dgl, H,
                                           name="delta_prep_bwd")
    dgb = jnp.pad(jnp.concatenate([dgB[:, :, 0].T, dbB[:, :, 0].T], axis=1), ((0, 0), (0, LANE - 2 * H)))
    dab, dal, ddt = _gates_bwd(proj, alog, dtb, dgb, ab_off, H, name="gates_bwd")
    G["a_log"], G["dt_bias"] = dal[:, :H], ddt[:, :H]
    dqkv, G["conv_qkv"] = _qkv_bwd(proj, cv["conv_qkv"], dq, dk, dv, qkv_off, H, name="qkv_bwd")
    dax, dab_, dac, G["conv_a"] = _group_a_bwd(proj, cv["conv_a"], dycat, CW, name="group_a_bwd")
    in_p = w_in.shape[0]
    dproj = jnp.concatenate([dax, dab_, dac, dqkv, dz, dab, jnp.zeros((T, in_p - (ab_off + 1) * LANE), BF16)], axis=1)
    tok = emit({"w_in": _matmul(dproj, h1, "tn", name="mm_dwin", out_dtypes=(BF16,), tk=LONG_K)})
    dh1 = _matmul(dproj, w_in, "nn", name="mm_dh1", after=tok, tk=LONG_K)
    grad_x, _, G["g_mix"] = _rms_bwd(x, S["g_mix"], dh1, dx1, name="rms1_bwd")
    return loss, grad_x, G


def _col_sharded(landed):
    _, R, C = landed.shape
    return jnp.transpose(landed, (1, 0, 2)).reshape(R, N_DEV * C)


def kernel(x, p, norm_mix_g, w_in, conv_a_w, conv_qkv_w, a_log, dt_bias, dn_norm_g, w_out, norm_ffn_g, w_up, conv_ffn_w, w_down, norm_ple_g, w_ple_gate, w_ple_proj, final_norm_g, loss_target, m_norm_mix_g, m_w_in, m_conv_a_w, m_conv_qkv_w, m_a_log, m_dt_bias, m_dn_norm_g, m_w_out, m_norm_ffn_g, m_w_up, m_conv_ffn_w, m_w_down, m_norm_ple_g, m_w_ple_gate, m_w_ple_proj, m_final_norm_g, v_norm_mix_g, v_w_in, v_conv_a_w, v_conv_qkv_w, v_a_log, v_dt_bias, v_dn_norm_g, v_w_out, v_norm_ffn_g, v_w_up, v_conv_ffn_w, v_w_down, v_norm_ple_g, v_w_ple_gate, v_w_ple_proj, v_final_norm_g):
    T, D = x.shape[1], x.shape[2]
    xd, _, cd = _mesh_pos()
    me = 4 * xd + 2 * lax.axis_index("y") + cd

    conv_sh = [conv_a_w[0], conv_qkv_w[0], conv_ffn_w[0]]
    conv_n = [c.size for c in conv_sh]
    pack_rows = -(-sum(conv_n) // LANE)
    conv_pack = jnp.pad(jnp.concatenate([c.reshape(-1) for c in conv_sh]), (0, pack_rows * LANE - sum(conv_n))).reshape(pack_rows, LANE)
    names = ["w_pp", "w_in", "conv", "w_out", "w_up", "w_down", "w_pg"]
    tr_ = lambda t: jnp.swapaxes(t, 1, 2)
    shards = [w_ple_proj[0].astype(BF16), w_in[0].T.astype(BF16), conv_pack, w_out[0].astype(BF16), w_up[0].astype(BF16),
              w_down[0].astype(BF16), w_ple_gate[0].astype(BF16)]
    empty_slots = lambda blocks: [lax.empty((N_DEV,) + tuple(b.shape), b.dtype) for b in blocks]
    handles, tok0 = _split_start(shards, empty_slots(shards), False, name="gather_start",
                                 relations=[(SIBLING,) + SAME_CORE if nm == "w_in" else ALL_PEERS for nm in names])
    handle = dict(zip(names, handles))
    own = dict(zip(names, shards))
    in_cols = N_DEV * w_in.shape[2]
    in_p = (in_cols // LANE) * LANE + AB_PAD
    in_place = {"w_up", "w_pp"}

    def gathered(name, after):
        if name == "w_in":
            passed, fwd = _gather_forward(handle[name], after, name="gather_forward_w_in")
            landed = _gather_wait_two_level(passed, fwd, name="gather_wait_w_in")
        else:
            landed = _split_wait(handle[name], after, False, name="gather_wait_" + name)
        return lax.dynamic_update_index_in_dim(landed, own[name], me, 0)

    def wt(name, after):
        landed = gathered(name, after)
        if name in in_place:
            return landed
        full = landed.reshape(-1, D)
        return jnp.pad(full, ((0, in_p - in_cols), (0, 0))) if name == "w_in" else full

    def conv(after):
        flat = gathered("conv", after).reshape(N_DEV, pack_rows * LANE)
        out, o_ = {}, 0
        for nm, c, n_ in zip(("conv_a", "conv_qkv", "conv_ffn"), conv_sh, conv_n):
            out[nm] = _col_sharded(flat[:, o_:o_ + n_].reshape((N_DEV,) + c.shape))
            o_ += n_
        return out

    pending, mine = {}, {}

    def emit(grads):
        parts = [g if nm in in_place else (g[:in_cols] if nm == "w_in" else g).reshape(N_DEV, -1, D)
                 for nm, g in grads.items()]
        hs, tok = _split_start(parts, empty_slots([q[0] for q in parts]), True, name="scatter_start_" + "_".join(grads))
        pending.update(zip(grads, hs))
        mine.update({nm: lax.dynamic_index_in_dim(q, me, 0, keepdims=False) for nm, q in zip(grads, parts)})
        return tok

    S = {
        "g_mix": norm_mix_g + tok0[0, 0], "a_log": a_log, "dt_bias": dt_bias, "dn_g": dn_norm_g, "g_ffn": norm_ffn_g,
        "g_ple": norm_ple_g, "g_final": final_norm_g.reshape(1, D),
    }

    loss_v, grad_x, G = _local_step(x[0], p[0, 0], loss_target[0], S, wt, conv, emit)
    loss = lax.psum(loss_v[0, 0], ("x", "y", "c"))

    small_names = ["g_mix", "g_ffn", "g_ple", "g_final", "dn_g", "a_log", "dt_bias", "conv_a", "conv_qkv", "conv_ffn"]
    small_rows, pieces = [], []
    for nm in small_names:
        g_ = G[nm].reshape(-1)
        r_ = -(-g_.size // (8 * LANE)) * 8
        small_rows.append(r_)
        pieces.append(jnp.pad(g_, (0, r_ * LANE - g_.size)).reshape(r_, LANE))
    landed = {nm: _split_wait(h_, grad_x, True, name="scatter_wait_" + nm) for nm, h_ in pending.items() if nm != "w_in"}

    def adam(parts, w_, m_, v_, nm, own_=None):
        shp = w_.shape
        w2, m2, v2 = (t.reshape(parts.shape[1:]) for t in (w_, m_, v_))
        kw = {} if own_ is None else {"own": own_, "me": me.astype(jnp.int32).reshape(1)}
        return tuple(t.reshape(shp) for t in _adam(parts, w2, m2, v2, name="adam_" + nm, **kw))

    big = {
        "w_up": adam(landed["w_up"], w_up, m_w_up, v_w_up, "w_up", mine["w_up"]),
        "w_down": adam(landed["w_down"], w_down, m_w_down, v_w_down, "w_down", mine["w_down"]),
        "w_out": adam(landed["w_out"], w_out, m_w_out, v_w_out, "w_out", mine["w_out"]),
        "w_pg": adam(landed["w_pg"], w_ple_gate, m_w_ple_gate, v_w_ple_gate, "w_ple_gate", mine["w_pg"]),
        "w_pp": adam(landed["w_pp"], w_ple_proj, m_w_ple_proj, v_w_ple_proj, "w_ple_proj", mine["w_pp"]),
    }
    first = lambda t: lax.slice(t, (0,) * t.ndim, (1,) * t.ndim).reshape(1)
    big_done = sum(first(r[1]) for r in big.values())
    (small_l,) = _exchange([jnp.concatenate(pieces, axis=0)], False, name="gather_small_grads", after=big_done)

    def small_parts(nm):
        i = small_names.index(nm)
        r0 = sum(small_rows[:i])
        shp = G[nm].shape
        return small_l[:, r0:r0 + small_rows[i], :].reshape(N_DEV, -1)[:, :G[nm].size].reshape((N_DEV,) + shp)

    def conv_parts(nm, shard):
        full = small_parts(nm)
        C = shard.shape[-1]
        return lax.dynamic_slice_in_dim(full, me * C, C, axis=2)

    res = [
        adam(small_parts("g_mix"), norm_mix_g, m_norm_mix_g, v_norm_mix_g, "norm_mix_g"),
        None,
        adam(conv_parts("conv_a", conv_a_w), conv_a_w, m_conv_a_w, v_conv_a_w, "conv_a_w"),
        adam(conv_parts("conv_qkv", conv_qkv_w), conv_qkv_w, m_conv_qkv_w, v_conv_qkv_w, "conv_qkv_w"),
        adam(small_parts("a_log"), a_log, m_a_log, v_a_log, "a_log"),
        adam(small_parts("dt_bias"), dt_bias, m_dt_bias, v_dt_bias, "dt_bias"),
        adam(small_parts("dn_g"), dn_norm_g, m_dn_norm_g, v_dn_norm_g, "dn_norm_g"),
        big["w_out"],
        adam(small_parts("g_ffn"), norm_ffn_g, m_norm_ffn_g, v_norm_ffn_g, "norm_ffn_g"),
        big["w_up"],
        adam(conv_parts("conv_ffn", conv_ffn_w), conv_ffn_w, m_conv_ffn_w, v_conv_ffn_w, "conv_ffn_w"),
        big["w_down"],
        adam(small_parts("g_ple"), norm_ple_g, m_norm_ple_g, v_norm_ple_g, "norm_ple_g"),
        big["w_pg"],
        big["w_pp"],
        adam(small_parts("g_final"), final_norm_g.reshape(1, D), m_final_norm_g.reshape(1, D),
             v_final_norm_g.reshape(1, D), "final_norm_g"),
    ]
    res[-1] = tuple(t.reshape(D) for t in res[-1])
    landed_in = _split_wait(pending["w_in"], res[10][1], True, name="scatter_wait_w_in")
    res[1] = tuple(tr_(t) for t in adam(landed_in, tr_(w_in), tr_(m_w_in), tr_(v_w_in), "w_in", mine["w_in"]))
    grads, deltas, new_m, new_v = zip(*res)
    return (loss, grad_x[None], *grads, *deltas, *new_m, *new_v)
```

```python
import functools

import jax
import jax.numpy as jnp
from jax import lax
from jax.experimental import pallas as pl
from jax.experimental.pallas import tpu as pltpu

F32 = jnp.float32
BF16 = jnp.bfloat16

EPS = 1e-6
CHUNK = 64
HEAD = 128
LANE = 128
N_DEV = 8
AB_PAD = 512

ADAM_LR = 0.001
ADAM_B1 = 0.9
ADAM_B2 = 0.999
ADAM_EPS = 1e-08
ADAM_WD = 0.01
ADAM_STEP = 10

MESH = pl.DeviceIdType.MESH


def _tile(dim, target, align=LANE):
    if dim <= target:
        return dim
    t = (target // align) * align
    while t > align and dim % t:
        t -= align
    assert dim % t == 0, (dim, target)
    return t


def _params(sem, vmem_mb=48):
    return pltpu.CompilerParams(dimension_semantics=sem, vmem_limit_bytes=vmem_mb << 20)


_DN = {"nn": (((1,), (0,)), ((), ())), "nt": (((1,), (1,)), ((), ())), "tn": (((0,), (0,)), ((), ()))}
LONG_K = 4096
SHARD_TILE = 1408


def _matmul(a, b, mode, *, name, out_dtypes=(F32,), epilogue=None, extras=(), vec_extras=(), n_vec=0, after=None,
            a_shards=False, b_shards=False, out_shards=False, tm=1024, tn=1024, tk=2048):
    shard_w = b.shape[2] if b_shards else None
    if b_shards:
        b_rows, b_cols = b.shape[1], b.shape[0] * shard_w
    else:
        b_rows, b_cols = b.shape
    a_w = a.shape[2] if a_shards else None
    a_dims = (a.shape[1], a.shape[0] * a_w) if a_shards else a.shape
    if mode == "nn":
        (M, K), (K2, N) = a_dims, (b_rows, b_cols)
    elif mode == "nt":
        (M, K), (N, K2) = a_dims, (b_rows, b_cols)
    else:
        (K, M), (K2, N) = a_dims, (b_rows, b_cols)
    assert K == K2, (name, a.shape, b.shape)
    tm = _tile(M, tm)
    n_dims = [N] + ([shard_w] if (b_shards and mode != "nt") else []) + ([N // N_DEV] if out_shards else [])
    tn = _tile(min(n_dims), tn)
    assert all(d % tn == 0 for d in n_dims), (name, n_dims, tn)
    grp = 1
    if b_shards and mode == "nt":
        grp = max(g for g in (1, 2, 4, 8) if g <= max(1, tk // shard_w) and (a_w is None or a_w % (g * shard_w) == 0))
    k_dims = [K] + ([shard_w] if (b_shards and mode == "nt") else []) + ([a_w] if a_shards else [])
    tk = grp * shard_w if grp > 1 else _tile(min(k_dims), tk)
    assert K % tk == 0, (name, K, tk)
    nk = K // tk
    n_ex, n_out = len(extras) + len(vec_extras), len(out_dtypes)
    assert n_vec == 0 or tn == N, (name, tn, N)
    dn = _DN[mode]

    n_tok = 0 if after is None else 1

    def body(a_ref, b_ref, *rest):
        rest = rest[n_tok:]
        ex_refs, out_refs, vec_refs = rest[:n_ex], rest[n_ex:n_ex + n_out], rest[n_ex + n_out:n_ex + n_out + n_vec]
        if grp > 1:
            part = sum(lax.dot_general(a_ref[:, s * shard_w:(s + 1) * shard_w].astype(BF16), b_ref[s].astype(BF16), dn,
                                       preferred_element_type=F32) for s in range(grp))
        else:
            part = lax.dot_general(a_ref[...].astype(BF16), b_ref[...].astype(BF16), dn, preferred_element_type=F32)
        first_rows = pl.program_id(0) == 0

        def finish(res):
            outs = (res,) if epilogue is None else epilogue(res, *[e[...] for e in ex_refs])
            for o_ref, val in zip(out_refs, outs[:n_out]):
                o_ref[...] = val.astype(o_ref.dtype)
            for v_ref, val in zip(vec_refs, outs[n_out:]):
                @pl.when(first_rows)
                def _(v_ref=v_ref, val=val):
                    v_ref[...] = val

                @pl.when(jnp.logical_not(first_rows))
                def _(v_ref=v_ref, val=val):
                    v_ref[...] += val

        if nk == 1:
            finish(part)
            return
        acc, k = rest[-1], pl.program_id(2)

        @pl.when(k == 0)
        def _():
            acc[...] = part

        @pl.when(k > 0)
        def _():
            acc[...] += part

        @pl.when(k == nk - 1)
        def _():
            finish(acc[...])

    if a_shards:
        assert mode == "nt" and a_w % tk == 0, (name, mode, a_w, tk)
        per_a = a_w // tk
        a_spec = pl.BlockSpec((None, tm, tk), lambda i, j, k: (lax.div(k, per_a), i, lax.rem(k, per_a)))
    else:
        a_spec = pl.BlockSpec((tk, tm), lambda i, j, k: (k, i)) if mode == "tn" else pl.BlockSpec((tm, tk), lambda i, j, k: (i, k))
    if b_shards and mode != "nt":
        per = shard_w // tn
        b_spec = pl.BlockSpec((None, tk, tn), lambda i, j, k: (lax.div(j, per), k, lax.rem(j, per)))
    elif b_shards and grp > 1:
        b_spec = pl.BlockSpec((grp, tn, shard_w), lambda i, j, k: (k, j, 0))
    elif b_shards:
        per = shard_w // tk
        b_spec = pl.BlockSpec((None, tn, tk), lambda i, j, k: (lax.div(k, per), j, lax.rem(k, per)))
    else:
        b_spec = pl.BlockSpec((tn, tk), lambda i, j, k: (j, k)) if mode == "nt" else pl.BlockSpec((tk, tn), lambda i, j, k: (k, j))
    mn_spec = pl.BlockSpec((tm, tn), lambda i, j, k: (i, j))
    vec_spec = pl.BlockSpec((1, tn), lambda i, j, k: (0, j))
    if out_shards:
        assert not extras
        per_o = (N // N_DEV) // tn
        out_spec = pl.BlockSpec((None, tm, tn), lambda i, j, k: (lax.div(j, per_o), i, lax.rem(j, per_o)))
        out_dims = (N_DEV, M, N // N_DEV)
    else:
        out_spec, out_dims = mn_spec, (M, N)
    outs = pl.pallas_call(
        body, name=name, grid=(M // tm, N // tn, nk),
        in_specs=[a_spec, b_spec] + [pl.BlockSpec((8, LANE), lambda i, j, k: (0, 0))] * n_tok
        + [mn_spec] * len(extras) + [vec_spec] * len(vec_extras),
        out_specs=[out_spec] * n_out + [vec_spec] * n_vec,
        out_shape=[jax.ShapeDtypeStruct(out_dims, dt) for dt in out_dtypes] + [jax.ShapeDtypeStruct((1, N), F32)] * n_vec,
        scratch_shapes=[pltpu.VMEM((tm, tn), F32)] if nk > 1 else [],
        compiler_params=_params(("arbitrary" if n_vec else "parallel", "parallel", "arbitrary"), 56),
    )(a, b, *([] if after is None else [after]), *extras, *vec_extras)
    return outs[0] if n_out + n_vec == 1 else outs


def _rms_fwd(x, g, *, name):
    T, D = x.shape
    tr = _tile(T, 256, 8)

    def body(x_ref, g_ref, h_ref):
        xv = x_ref[...]
        r = lax.rsqrt(jnp.mean(xv * xv, axis=-1, keepdims=True) + EPS)
        h_ref[...] = (xv * r * g_ref[...]).astype(h_ref.dtype)

    return pl.pallas_call(
        body, name=name, grid=(T // tr,),
        in_specs=[pl.BlockSpec((tr, D), lambda i: (i, 0)), pl.BlockSpec((1, D), lambda i: (0, 0))],
        out_specs=pl.BlockSpec((tr, D), lambda i: (i, 0)),
        out_shape=jax.ShapeDtypeStruct((T, D), BF16),
        compiler_params=_params(("parallel",)),
    )(x, g)


def _rms_bwd(x, g, dh, dres, *, name):
    T, D = x.shape
    tr = _tile(T, 256, 8)
    epi = _epi_rms_bwd(2)

    def body(x_ref, g_ref, dh_ref, dres_ref, dx_ref, dxb_ref, dg_ref):
        dx, _, dgp = epi(dh_ref[...], x_ref[...], dres_ref[...], g_ref[...])

        @pl.when(pl.program_id(0) == 0)
        def _():
            dg_ref[...] = jnp.zeros_like(dg_ref)

        dg_ref[...] += dgp
        dx_ref[...] = dx
        dxb_ref[...] = dx.astype(dxb_ref.dtype)

    row = pl.BlockSpec((tr, D), lambda i: (i, 0))
    vec = pl.BlockSpec((1, D), lambda i: (0, 0))
    return pl.pallas_call(
        body, name=name, grid=(T // tr,),
        in_specs=[row, vec, row, row], out_specs=[row, row, vec],
        out_shape=[jax.ShapeDtypeStruct((T, D), F32), jax.ShapeDtypeStruct((T, D), BF16), jax.ShapeDtypeStruct((1, D), F32)],
        compiler_params=_params(("arbitrary",)),
    )(x, g, dh, dres)


ROW_TILE = 256


def _epi_residual_rms(acc, res, g):
    xn = acc + res
    r = lax.rsqrt(jnp.mean(xn * xn, axis=-1, keepdims=True) + EPS)
    return xn, xn * r * g


def _epi_rms_bwd(n_copies):
    def epi(dh, x, dres, g):
        r = lax.rsqrt(jnp.mean(x * x, axis=-1, keepdims=True) + EPS)
        xh = x * r
        dxh = dh * g
        dx = dres + r * (dxh - xh * jnp.mean(dxh * xh, axis=-1, keepdims=True))
        return (dx,) * n_copies + (jnp.sum(dh * xh, axis=0, keepdims=True),)
    return epi


def _final_loss(x, g, tgt, pp, sg, *, name):
    T, D = x.shape
    tr = _tile(T, 256, 8)

    def body(x_ref, g_ref, t_ref, pp_ref, sg_ref, dx_ref, dg_ref, loss_ref, dpg_ref, dpp_ref):
        xv = x_ref[...]
        r = lax.rsqrt(jnp.mean(xv * xv, axis=-1, keepdims=True) + EPS)
        xh = xv * r
        gv = g_ref[...]
        err = xh * gv - t_ref[...]

        @pl.when(pl.program_id(0) == 0)
        def _():
            dg_ref[...] = jnp.zeros_like(dg_ref)
            loss_ref[...] = jnp.zeros_like(loss_ref)

        part = 0.5 * jnp.sum(jnp.mean(err * err, axis=-1, keepdims=True), axis=0, keepdims=True)
        loss_ref[...] += jnp.broadcast_to(part, loss_ref.shape)
        dy = err * (1.0 / D)
        dg_ref[...] += jnp.sum(dy * xh, axis=0, keepdims=True)
        dxh = dy * gv
        dx = r * (dxh - xh * jnp.mean(dxh * xh, axis=-1, keepdims=True))
        dx_ref[...] = dx
        s = sg_ref[...]
        dpg_ref[...] = (dx * pp_ref[...] * s * (1.0 - s)).astype(dpg_ref.dtype)
        dpp_ref[...] = (dx * s).astype(dpp_ref.dtype)

    row = pl.BlockSpec((tr, D), lambda i: (i, 0))
    vec = pl.BlockSpec((1, D), lambda i: (0, 0))
    return pl.pallas_call(
        body, name=name, grid=(T // tr,),
        in_specs=[row, vec, row, row, row], out_specs=[row, vec, pl.BlockSpec((1, LANE), lambda i: (0, 0)), row, row],
        out_shape=[jax.ShapeDtypeStruct((T, D), F32), jax.ShapeDtypeStruct((1, D), F32),
                   jax.ShapeDtypeStruct((1, LANE), F32)] + [jax.ShapeDtypeStruct((T, D), BF16)] * 2,
        compiler_params=_params(("arbitrary",)),
    )(x, g, tgt, pp, sg)


ROWS_QKV_FWD, ROWS_QKV_BWD, ROWS_FFN_FWD, ROWS_FFN_BWD, ROWS_GROUP_A = 512, 256, 256, 128, 256


def _ext(ref, r0, T, before, after, RC):
    parts = []
    if before:
        p0 = pl.multiple_of(jnp.maximum(r0 - 8, 0), 8)
        parts.append(jnp.where(r0 > 0, ref[pl.ds(p0, 8), :], 0.0))
    parts.append(ref[pl.ds(r0, RC), :])
    if after:
        n0 = pl.multiple_of(jnp.minimum(r0 + RC, T - 8), 8)
        parts.append(jnp.where(r0 + RC < T, ref[pl.ds(n0, 8), :], 0.0))
    return parts[0] if len(parts) == 1 else jnp.concatenate(parts, axis=0)


def _fold8(x):
    return jnp.sum(x.reshape(x.shape[0] // 8, 8, x.shape[1]), axis=0)


def _win(ref, r0, lo, n, T, RC, edge):
    if not edge:
        return ref[pl.ds(r0 + lo, n), :]
    xx = _ext(ref, r0, T, True, True, RC)
    a = 8 + lo
    return (xx if a == 0 else pltpu.roll(xx, xx.shape[0] - a, 0))[:n, :]


def _taps(ref, w_ref, K, r0, n, T, RC, edge):
    wins = [_win(ref, r0, -(K - 1 - j), n, T, RC, edge) for j in range(K)]
    y = wins[0] * w_ref[0:1, :]
    for j in range(1, K):
        y = y + wins[j] * w_ref[j:j + 1, :]
    return wins, y


def _untaps(scr_ref, val, w_ref, K, RC):
    scr_ref[0:val.shape[0], :] = val
    y = scr_ref[K - 1:K - 1 + RC, :] * w_ref[0:1, :]
    for j in range(1, K):
        s = K - 1 - j
        y = y + scr_ref[s:s + RC, :] * w_ref[j:j + 1, :]
    return y


def _peeled(n_chunks, RC, step, init):
    carry = step(0, init, True)
    if n_chunks > 2:
        carry = lax.fori_loop(1, n_chunks - 1, lambda i, c: step(pl.multiple_of(i * RC, RC), c, False), carry)
    if n_chunks > 1:
        carry = step((n_chunks - 1) * RC, carry, True)
    return carry


def _silu(x):
    return x * jax.nn.sigmoid(x)


def _dsilu(x):
    s = jax.nn.sigmoid(x)
    return s * (1.0 + x * (1.0 - s))


def _col_specs(T, offs):
    return [pl.BlockSpec((T, LANE), functools.partial(lambda o, j: (0, o + j), o)) for o in offs]


def _group_a_fwd(proj, conv_w, CW, *, name):
    T = proj.shape[0]
    RC = _tile(T, ROWS_GROUP_A, 8)
    nb = CW // LANE
    K = conv_w.shape[0]

    def body(ax_ref, ab_ref, ac_ref, w_ref, y_ref):
        def step(r0, carry, edge):
            c = None
            for j in range(K):
                lo = -(K - 1 - j)
                t = _win(ac_ref, r0, lo, RC, T, RC, edge) * _win(ax_ref, r0, lo, RC, T, RC, edge) * w_ref[j:j + 1, :]
                c = t if c is None else c + t
            y_ref[pl.ds(r0, RC), :] = (ab_ref[pl.ds(r0, RC), :] * c).astype(y_ref.dtype)
            return carry
        _peeled(T // RC, RC, step, 0)

    return pl.pallas_call(
        body, name=name, grid=(nb,),
        in_specs=_col_specs(T, (0, nb, 2 * nb)) + [pl.BlockSpec((K, LANE), lambda j: (0, j))],
        out_specs=pl.BlockSpec((T, LANE), lambda j: (0, j)),
        out_shape=jax.ShapeDtypeStruct((T, CW), BF16), compiler_params=_params(("parallel",)),
    )(proj, proj, proj, conv_w)


def _group_a_bwd(proj, conv_w, dycat, CW, *, name):
    T = proj.shape[0]
    RC = _tile(T, ROWS_GROUP_A, 8)
    nb = CW // LANE
    K = conv_w.shape[0]

    def body(ax_ref, ab_ref, ac_ref, w_ref, dy_ref, dax_ref, dab_ref, dac_ref, dw_ref, scr_ref):
        def step(r0, accs, edge):
            ms = [_win(ac_ref, r0, -(K - 1 - j), RC, T, RC, edge) * _win(ax_ref, r0, -(K - 1 - j), RC, T, RC, edge)
                  for j in range(K)]
            c = ms[0] * w_ref[0:1, :]
            for j in range(1, K):
                c = c + ms[j] * w_ref[j:j + 1, :]
            dy = dy_ref[pl.ds(r0, RC), :]
            dab_ref[pl.ds(r0, RC), :] = (dy * c).astype(dab_ref.dtype)
            dc2 = _win(dy_ref, r0, 0, RC + 8, T, RC, edge) * _win(ab_ref, r0, 0, RC + 8, T, RC, edge)
            dm = _untaps(scr_ref, dc2, w_ref, K, RC)
            dax_ref[pl.ds(r0, RC), :] = (dm * ac_ref[pl.ds(r0, RC), :]).astype(dax_ref.dtype)
            dac_ref[pl.ds(r0, RC), :] = (dm * ax_ref[pl.ds(r0, RC), :]).astype(dac_ref.dtype)
            return tuple(accs[j] + _fold8(dc2[:RC] * ms[j]) for j in range(K))

        accs = _peeled(T // RC, RC, step, tuple(jnp.zeros((8, LANE), F32) for _ in range(K)))
        for j in range(K):
            dw_ref[j:j + 1, :] = jnp.sum(accs[j], axis=0, keepdims=True)

    col = pl.BlockSpec((T, LANE), lambda j: (0, j))
    wsp = pl.BlockSpec((K, LANE), lambda j: (0, j))
    return pl.pallas_call(
        body, name=name, grid=(nb,),
        in_specs=_col_specs(T, (0, nb, 2 * nb)) + [wsp, col],
        out_specs=[col, col, col, wsp],
        out_shape=[jax.ShapeDtypeStruct((T, CW), BF16)] * 3 + [jax.ShapeDtypeStruct((K, CW), F32)],
        scratch_shapes=[pltpu.VMEM((RC + 8, LANE), F32)],
        compiler_params=_params(("parallel",)),
    )(proj, proj, proj, conv_w, dycat)


def _qkv_fwd(proj, conv_w, off, H, *, name):
    T = proj.shape[0]
    RC = _tile(T, ROWS_QKV_FWD, 8)
    nb = 3 * H
    K = conv_w.shape[0]

    def body(x_ref, w_ref, y_ref):
        j = pl.program_id(0)
        is_qk = j < 2 * H
        scale = jnp.where(j < H, HEAD ** -0.5, 1.0).astype(F32)

        def step(r0, carry, edge):
            s = _silu(_taps(x_ref, w_ref, K, r0, RC, T, RC, edge)[1])
            r = lax.rsqrt(jnp.sum(s * s, axis=-1, keepdims=True) + EPS) * scale
            y_ref[pl.ds(r0, RC), :] = s * jnp.where(is_qk, r, 1.0)
            return carry
        _peeled(T // RC, RC, step, 0)

    return pl.pallas_call(
        body, name=name, grid=(nb,),
        in_specs=_col_specs(T, (off,)) + [pl.BlockSpec((K, LANE), lambda j: (0, j))],
        out_specs=pl.BlockSpec((T, LANE), lambda j: (0, j)),
        out_shape=jax.ShapeDtypeStruct((T, nb * LANE), F32), compiler_params=_params(("parallel",)),
    )(proj, conv_w)


def _qkv_bwd(proj, conv_w, dq, dk, dv, off, H, *, name):
    T = proj.shape[0]
    RC = _tile(T, ROWS_QKV_BWD, 8)
    nb = 3 * H
    K = conv_w.shape[0]

    def body(x_ref, w_ref, dq_ref, dk_ref, dv_ref, dx_ref, dw_ref, scr_ref):
        j = pl.program_id(0)
        is_qk = j < 2 * H
        scale = jnp.where(j < H, HEAD ** -0.5, 1.0).astype(F32)

        def step(r0, accs, edge):
            xs, c2 = _taps(x_ref, w_ref, K, r0, RC + 8, T, RC, edge)
            s2 = _silu(c2)
            dn2 = jnp.where(j < H, _win(dq_ref, r0, 0, RC + 8, T, RC, edge),
                            jnp.where(is_qk, _win(dk_ref, r0, 0, RC + 8, T, RC, edge),
                                      _win(dv_ref, r0, 0, RC + 8, T, RC, edge)))
            r = lax.rsqrt(jnp.sum(s2 * s2, axis=-1, keepdims=True) + EPS)
            nh = s2 * r
            dnp = dn2 * scale
            ds_qk = r * (dnp - nh * jnp.sum(dnp * nh, axis=-1, keepdims=True))
            ds2 = jnp.where(is_qk, ds_qk, dn2)
            dc2 = ds2 * _dsilu(c2)
            dx_ref[pl.ds(r0, RC), :] = _untaps(scr_ref, dc2, w_ref, K, RC).astype(dx_ref.dtype)
            return tuple(accs[jj] + _fold8(dc2[:RC] * xs[jj][:RC]) for jj in range(K))

        accs = _peeled(T // RC, RC, step, tuple(jnp.zeros((8, LANE), F32) for _ in range(K)))
        for jj in range(K):
            dw_ref[jj:jj + 1, :] = jnp.sum(accs[jj], axis=0, keepdims=True)

    col = pl.BlockSpec((T, LANE), lambda j: (0, j))
    wsp = pl.BlockSpec((K, LANE), lambda j: (0, j))
    return pl.pallas_call(
        body, name=name, grid=(nb,),
        in_specs=_col_specs(T, (off,)) + [wsp] + [
            pl.BlockSpec((T, LANE), functools.partial(lambda o, j: (0, jnp.clip(j - o, 0, H - 1)), o)) for o in (0, H, 2 * H)],
        out_specs=[col, wsp],
        out_shape=[jax.ShapeDtypeStruct((T, nb * LANE), BF16), jax.ShapeDtypeStruct((K, nb * LANE), F32)],
        scratch_shapes=[pltpu.VMEM((RC + 8, LANE), F32)],
        compiler_params=_params(("parallel",)),
    )(proj, conv_w, dq, dk, dv)


def _softplus(x):
    return jnp.maximum(x, 0.0) + jnp.log(1.0 + jnp.exp(-jnp.abs(x)))


def _gates_fwd(proj, alog, dtb, off, H, *, name):
    T = proj.shape[0]
    tr = _tile(T, 512, CHUNK)

    def body(ab_ref, al_ref, dt_ref, gb_ref, gam_ref):
        ab = ab_ref[...]
        lane = lax.broadcasted_iota(jnp.int32, ab.shape, 1)
        g = -jnp.exp(al_ref[...]) * _softplus(ab + dt_ref[...])
        gb = jnp.where(lane < H, g, jnp.where(lane < 2 * H, jax.nn.sigmoid(ab), 0.0))
        gb_ref[...] = gb
        tril = _tri().astype(F32)
        for c in range(tr // CHUNK):
            rows = slice(c * CHUNK, (c + 1) * CHUNK)
            gam_ref[rows, :] = _mm(tril, gb[rows, :], precision=lax.Precision.HIGHEST)

    vec = pl.BlockSpec((1, LANE), lambda i: (0, 0))
    row = pl.BlockSpec((tr, LANE), lambda i: (i, 0))
    return pl.pallas_call(
        body, name=name, grid=(T // tr,),
        in_specs=[pl.BlockSpec((tr, LANE), lambda i: (i, off)), vec, vec],
        out_specs=[row, row],
        out_shape=[jax.ShapeDtypeStruct((T, LANE), F32)] * 2, compiler_params=_params(("parallel",)),
    )(proj, alog, dtb)


def _gates_bwd(proj, alog, dtb, dgb, off, H, *, name):
    T = proj.shape[0]
    tr = _tile(T, 512, CHUNK)

    def body(ab_ref, al_ref, dt_ref, d_ref, dab_ref, dal_ref, ddt_ref):
        ab, d = ab_ref[...], d_ref[...]
        lane = lax.broadcasted_iota(jnp.int32, ab.shape, 1)
        is_g = lane < H
        triu = _tri(upper=True).astype(F32)
        dg = jnp.concatenate([_mm(triu, d[c * CHUNK:(c + 1) * CHUNK, :], precision=lax.Precision.HIGHEST)
                              for c in range(tr // CHUNK)], axis=0)
        z = ab + dt_ref[...]
        A = -jnp.exp(al_ref[...])
        da = dg * A * jax.nn.sigmoid(z)
        beta = jax.nn.sigmoid(ab)
        db = d * beta * (1.0 - beta)
        dab_ref[...] = jnp.where(is_g, da, jnp.where(lane < 2 * H, db, 0.0)).astype(dab_ref.dtype)

        @pl.when(pl.program_id(0) == 0)
        def _():
            dal_ref[...] = jnp.zeros_like(dal_ref)
            ddt_ref[...] = jnp.zeros_like(ddt_ref)

        dal_ref[...] += jnp.sum(jnp.where(is_g, dg * A * _softplus(z), 0.0), axis=0, keepdims=True)
        ddt_ref[...] += jnp.sum(jnp.where(is_g, da, 0.0), axis=0, keepdims=True)

    vec = pl.BlockSpec((1, LANE), lambda i: (0, 0))
    row = pl.BlockSpec((tr, LANE), lambda i: (i, 0))
    return pl.pallas_call(
        body, name=name, grid=(T // tr,),
        in_specs=[pl.BlockSpec((tr, LANE), lambda i: (i, off)), vec, vec, row],
        out_specs=[row, vec, vec],
        out_shape=[jax.ShapeDtypeStruct((T, LANE), BF16), jax.ShapeDtypeStruct((1, LANE), F32),
                   jax.ShapeDtypeStruct((1, LANE), F32)],
        compiler_params=_params(("arbitrary",)),
    )(proj, alog, dtb, dgb)


def _gated_norm_fwd(o, proj, gn, zoff, *, name):
    T, W = o.shape
    tr = _tile(T, 512, 8)

    def body(o_ref, z_ref, g_ref, y_ref):
        ov = o_ref[...]
        r = lax.rsqrt(jnp.mean(ov * ov, axis=-1, keepdims=True) + EPS)
        y_ref[...] = (ov * r * g_ref[...] * _silu(z_ref[...])).astype(y_ref.dtype)

    blk = pl.BlockSpec((tr, LANE), lambda i, j: (i, j))
    return pl.pallas_call(
        body, name=name, grid=(T // tr, W // LANE),
        in_specs=[blk, pl.BlockSpec((tr, LANE), lambda i, j: (i, zoff + j)), pl.BlockSpec((1, LANE), lambda i, j: (0, 0))],
        out_specs=blk, out_shape=jax.ShapeDtypeStruct((T, W), BF16), compiler_params=_params(("parallel", "parallel")),
    )(o, proj, gn)


def _gated_norm_bwd(o, proj, gn, dycat, zoff, yoff, *, name):
    T, W = o.shape
    tr = _tile(T, 512, 8)

    def body(o_ref, z_ref, g_ref, dy_ref, do_ref, dz_ref, dg_ref):
        ov, zv, gv, dy = o_ref[...], z_ref[...], g_ref[...], dy_ref[...]
        r = lax.rsqrt(jnp.mean(ov * ov, axis=-1, keepdims=True) + EPS)
        nh = ov * r
        s = _silu(zv)

        @pl.when((pl.program_id(0) == 0) & (pl.program_id(1) == 0))
        def _():
            dg_ref[...] = jnp.zeros_like(dg_ref)

        dg_ref[...] += jnp.sum(dy * nh * s, axis=0, keepdims=True)
        dz_ref[...] = (dy * nh * gv * _dsilu(zv)).astype(dz_ref.dtype)
        dn = dy * gv * s
        do_ref[...] = r * (dn - nh * jnp.mean(dn * nh, axis=-1, keepdims=True))

    blk = pl.BlockSpec((tr, LANE), lambda i, j: (i, j))
    vec = pl.BlockSpec((1, LANE), lambda i, j: (0, 0))
    return pl.pallas_call(
        body, name=name, grid=(T // tr, W // LANE),
        in_specs=[blk, pl.BlockSpec((tr, LANE), lambda i, j: (i, zoff + j)), vec,
                  pl.BlockSpec((tr, LANE), lambda i, j: (i, yoff + j))],
        out_specs=[blk, blk, vec],
        out_shape=[jax.ShapeDtypeStruct((T, W), F32), jax.ShapeDtypeStruct((T, W), BF16),
                   jax.ShapeDtypeStruct((1, LANE), F32)],
        compiler_params=_params(("arbitrary", "arbitrary")),
    )(o, proj, gn, dycat)


def _ffn_act_fwd(up_pre, conv_w, *, name):
    T, F2 = up_pre.shape
    RC = _tile(T, ROWS_FFN_FWD, 8)
    nb = F2 // 2 // LANE
    K = conv_w.shape[0]

    def body(g_ref, v_ref, wg_ref, wv_ref, y_ref):
        def step(r0, carry, edge):
            _, gate = _taps(g_ref, wg_ref, K, r0, RC, T, RC, edge)
            _, val = _taps(v_ref, wv_ref, K, r0, RC, T, RC, edge)
            y_ref[pl.ds(r0, RC), :] = (_silu(gate) * val).astype(y_ref.dtype)
            return carry
        _peeled(T // RC, RC, step, 0)

    return pl.pallas_call(
        body, name=name, grid=(nb,),
        in_specs=_col_specs(T, (0, nb)) + [pl.BlockSpec((K, LANE), lambda j: (0, j)),
                                           pl.BlockSpec((K, LANE), lambda j: (0, nb + j))],
        out_specs=pl.BlockSpec((T, LANE), lambda j: (0, j)),
        out_shape=jax.ShapeDtypeStruct((T, F2 // 2), BF16), compiler_params=_params(("parallel",)),
    )(up_pre, up_pre, conv_w, conv_w)


def _ffn_act_bwd(up_pre, conv_w, dact, *, name):
    T, F2 = up_pre.shape
    RC = _tile(T, ROWS_FFN_BWD, 8)
    nb = F2 // 2 // LANE
    K = conv_w.shape[0]

    def body(g_ref, v_ref, wg_ref, wv_ref, da_ref, d_ref, dwg_ref, dwv_ref, sg_ref, sv_ref):
        def step(r0, accs, edge):
            gs, gate2 = _taps(g_ref, wg_ref, K, r0, RC + 8, T, RC, edge)
            vs, val2 = _taps(v_ref, wv_ref, K, r0, RC + 8, T, RC, edge)
            da2 = _win(da_ref, r0, 0, RC + 8, T, RC, edge)
            dgate2 = da2 * val2 * _dsilu(gate2)
            dval2 = da2 * _silu(gate2)
            d_ref[0, pl.ds(r0, RC), :] = _untaps(sg_ref, dgate2, wg_ref, K, RC).astype(d_ref.dtype)
            d_ref[1, pl.ds(r0, RC), :] = _untaps(sv_ref, dval2, wv_ref, K, RC).astype(d_ref.dtype)
            new = []
            for j in range(K):
                new.append(accs[2 * j] + _fold8(dgate2[:RC] * gs[j][:RC]))
                new.append(accs[2 * j + 1] + _fold8(dval2[:RC] * vs[j][:RC]))
            return tuple(new)

        accs = _peeled(T // RC, RC, step, tuple(jnp.zeros((8, LANE), F32) for _ in range(2 * K)))
        for j in range(K):
            dwg_ref[j:j + 1, :] = jnp.sum(accs[2 * j], axis=0, keepdims=True)
            dwv_ref[j:j + 1, :] = jnp.sum(accs[2 * j + 1], axis=0, keepdims=True)

    col = pl.BlockSpec((T, LANE), lambda j: (0, j))
    wsp = pl.BlockSpec((K, LANE), lambda j: (0, j))
    return pl.pallas_call(
        body, name=name, grid=(nb,),
        in_specs=_col_specs(T, (0, nb)) + [wsp, pl.BlockSpec((K, LANE), lambda j: (0, nb + j)), col],
        out_specs=[pl.BlockSpec((2, T, LANE), lambda j: (0, 0, j)), wsp, wsp],
        out_shape=[jax.ShapeDtypeStruct((2, T, F2 // 2), BF16)] + [jax.ShapeDtypeStruct((K, F2 // 2), F32)] * 2,
        scratch_shapes=[pltpu.VMEM((RC + 8, LANE), F32)] * 2,
        compiler_params=_params(("parallel",)),
    )(up_pre, up_pre, conv_w, conv_w, dact)


CPB = 8
CPB_SCAN = 4
GRP = 8
HP = lax.Precision.HIGH


def _tri(strict=False, upper=False):
    r = lax.broadcasted_iota(jnp.int32, (CHUNK, CHUNK), 0)
    c = lax.broadcasted_iota(jnp.int32, (CHUNK, CHUNK), 1)
    if upper:
        return c >= r
    return (r > c) if strict else (r >= c)


def _mm(a, b, dn="nn", precision=None):
    precision = HP if precision is None else precision
    return lax.dot_general(a, b, _DN[dn], precision=precision, preferred_element_type=F32)


def _mm16(a, b, dn="nn"):
    return lax.dot_general(a.astype(BF16), b.astype(BF16), _DN[dn], preferred_element_type=F32)


def _each(f, *cols):
    return [f(*xs) for xs in zip(*cols)]


def _decay(gam):
    return jnp.exp(jnp.where(_tri(), gam[:, :CHUNK] - gam.T[:CHUNK, :], -1e30))


def _delta_specs(T, H, cpb):
    rows = cpb * CHUNK
    col = lambda o: pl.BlockSpec((rows, LANE), functools.partial(lambda o, h, n: (n, o + h), o))
    bc = pl.BlockSpec((1, rows, LANE), lambda h, n: (h, n, 0))
    sq = pl.BlockSpec((1, cpb, CHUNK, CHUNK), lambda h, n: (h, n, 0, 0))
    vec = pl.BlockSpec((1, cpb, 1, LANE), lambda h, n: (h, n, 0, 0))
    return col, bc, sq, vec


def _delta_prep_fwd(qkv, gamB, bB, H, *, name):
    T = qkv.shape[0]
    N = T // CHUNK
    cpb = _tile(N, CPB, 8)
    grp = min(GRP, cpb)
    col, bc, sq, vec = _delta_specs(T, H, cpb)

    def body(q_ref, k_ref, v_ref, g_ref, b_ref, u_ref, w_ref, qd_ref, kd_ref, qk_ref, ti_ref, gl_ref):
        eye = (lax.broadcasted_iota(jnp.int32, (CHUNK, CHUNK), 0) == lax.broadcasted_iota(jnp.int32, (CHUNK, CHUNK), 1)).astype(F32)
        strict = _tri(strict=True)
        for c0 in range(0, cpb, grp):
            cs = list(range(c0, c0 + grp))
            rows = [slice(c * CHUNK, (c + 1) * CHUNK) for c in cs]
            q, k, v = ([r_[r, :] for r in rows] for r_ in (q_ref, k_ref, v_ref))
            bb = [b_ref[0, r, :] for r in rows]
            gam = [g_ref[0, r, :] for r in rows]
            D = _each(_decay, gam)
            e = _each(jnp.exp, gam)
            kk = _each(lambda k_: _mm16(k_, k_, "nt"), k)
            X = _each(lambda kk_, D_, b_: -(jnp.where(strict, kk_ * D_, 0.0) * b_[:, :CHUNK]), kk, D, bb)
            R = _each(lambda x: eye + x, X)
            for _ in range(5):
                X = _each(lambda x: _mm(x, x), X)
                R = _each(lambda r, x: r + _mm(r, x), R, X)
            u = _each(lambda r, b_, v_: _mm(r, b_ * v_), R, bb, v)
            w = _each(lambda r, b_, e_, k_: _mm(r, b_ * e_ * k_), R, bb, e, k)
            qk = _each(lambda q_, k_, D_: _mm16(q_, k_, "nt") * D_, q, k, D)
            for i, c in enumerate(cs):
                glast = gam[i][CHUNK - 1:CHUNK, :]
                u_ref[rows[i], :] = u[i]
                w_ref[rows[i], :] = w[i]
                qd_ref[rows[i], :] = e[i] * q[i]
                kd_ref[rows[i], :] = jnp.exp(glast - gam[i]) * k[i]
                qk_ref[0, c] = qk[i]
                ti_ref[0, c] = R[i]
                gl_ref[0, c] = jnp.exp(glast)

    full = jax.ShapeDtypeStruct((T, H * LANE), F32)
    sqs = jax.ShapeDtypeStruct((H, N, CHUNK, CHUNK), F32)
    return pl.pallas_call(
        body, name=name, grid=(H, N // cpb),
        in_specs=[col(0), col(H), col(2 * H), bc, bc],
        out_specs=[col(0)] * 4 + [sq, sq, vec],
        out_shape=[full] * 4 + [sqs, sqs, jax.ShapeDtypeStruct((H, N, 1, LANE), F32)],
        compiler_params=_params(("parallel", "parallel")),
    )(qkv, qkv, qkv, gamB, bB)


def _scan_specs(H, N, cpb, hb, rev):
    nbk = N // cpb
    blk = (lambda n: nbk - 1 - n) if rev else (lambda n: n)
    col = pl.BlockSpec((cpb * CHUNK, hb * LANE), lambda h, n: (blk(n), h))
    sq = pl.BlockSpec((hb, cpb, CHUNK, CHUNK), lambda h, n: (h, blk(n), 0, 0))
    vec = pl.BlockSpec((hb, cpb, 1, LANE), lambda h, n: (h, blk(n), 0, 0))
    st = pl.BlockSpec((hb, cpb, HEAD, HEAD), lambda h, n: (h, blk(n), 0, 0))
    return col, sq, vec, st


def _delta_scan_fwd(u, w, qd, kd, qk, gl, H, *, name):
    T = u.shape[0]
    N = T // CHUNK
    cpb = _tile(N, CPB_SCAN, 4)
    hb = min(GRP, H)
    col, sq, vec, st = _scan_specs(H, N, cpb, hb, False)
    lanes = [slice(j * LANE, (j + 1) * LANE) for j in range(hb)]
    heads = list(range(hb))

    def body(u_ref, w_ref, qd_ref, kd_ref, qk_ref, gl_ref, o_ref, vn_ref, ss_ref, s_scr):
        @pl.when(pl.program_id(1) == 0)
        def _():
            s_scr[...] = jnp.zeros_like(s_scr)

        def step(c, states):
            rows = pl.ds(pl.multiple_of(c * CHUNK, CHUNK), CHUNK)
            S = list(states)
            for j in heads:
                ss_ref[j, c] = S[j]
            wS = _each(lambda ln, s: _mm16(w_ref[rows, ln], s), lanes, S)
            qS = _each(lambda ln, s: _mm16(qd_ref[rows, ln], s), lanes, S)
            vn = _each(lambda ln, ws: u_ref[rows, ln] - ws, lanes, wS)
            o = _each(lambda j, qs, vn_: qs + _mm16(qk_ref[j, c], vn_), heads, qS, vn)
            new = _each(lambda j, ln, s, vn_: s * gl_ref[j, c] + _mm16(kd_ref[rows, ln], vn_, "tn"),
                        heads, lanes, S, vn)
            for j in heads:
                o_ref[rows, lanes[j]] = o[j]
                vn_ref[rows, lanes[j]] = vn[j]
            return tuple(new)
        out = lax.fori_loop(0, cpb, step, tuple(s_scr[j] for j in heads))
        for j in heads:
            s_scr[j] = out[j]

    full = jax.ShapeDtypeStruct((T, H * LANE), F32)
    return pl.pallas_call(
        body, name=name, grid=(H // hb, N // cpb),
        in_specs=[col] * 4 + [sq, vec],
        out_specs=[col, col, st],
        out_shape=[full, full, jax.ShapeDtypeStruct((H, N, HEAD, HEAD), F32)],
        scratch_shapes=[pltpu.VMEM((hb, HEAD, HEAD), F32)],
        compiler_params=_params(("parallel", "arbitrary")),
    )(u, w, qd, kd, qk, gl)


def _delta_scan_bwd(do, w, qd, kd, vn, qk, gl, ss, H, *, name):
    T = do.shape[0]
    N = T // CHUNK
    cpb = _tile(N, CPB_SCAN, 4)
    hb = min(GRP, H)
    col, sq, vec, st = _scan_specs(H, N, cpb, hb, True)
    lanes = [slice(j * LANE, (j + 1) * LANE) for j in range(hb)]
    heads = list(range(hb))

    def body(do_ref, w_ref, qd_ref, kd_ref, vn_ref, qk_ref, gl_ref, ss_ref,
             du_ref, dw_ref, dqd_ref, dkd_ref, dqk_ref, dgl_ref, ds_scr):
        @pl.when(pl.program_id(1) == 0)
        def _():
            ds_scr[...] = jnp.zeros_like(ds_scr)

        def step(i, dstates):
            c = cpb - 1 - i
            rows = pl.ds(pl.multiple_of(c * CHUNK, CHUNK), CHUNK)
            dS = list(dstates)
            S = [ss_ref[j, c] for j in heads]
            dov = [do_ref[rows, ln] for ln in lanes]
            vnv = [vn_ref[rows, ln] for ln in lanes]
            a1 = _each(lambda j, d_: _mm16(qk_ref[j, c], d_, "tn"), heads, dov)
            a2 = _each(lambda ln, ds: _mm16(kd_ref[rows, ln], ds), lanes, dS)
            dvn = _each(lambda x, y: x + y, a1, a2)
            dqd = _each(lambda d_, s: _mm16(d_, s, "nt"), dov, S)
            dkd = _each(lambda v_, ds: _mm16(v_, ds, "nt"), vnv, dS)
            dqk = _each(lambda d_, v_: _mm16(d_, v_, "nt"), dov, vnv)
            dw = _each(lambda dv_, s: -_mm16(dv_, s, "nt"), dvn, S)
            b1 = _each(lambda ln, d_: _mm16(qd_ref[rows, ln], d_, "tn"), lanes, dov)
            b2 = _each(lambda ln, dv_: _mm16(w_ref[rows, ln], dv_, "tn"), lanes, dvn)
            new = _each(lambda j, x, y, ds: x + ds * gl_ref[j, c] - y, heads, b1, b2, dS)
            for j in heads:
                du_ref[rows, lanes[j]] = dvn[j]
                dw_ref[rows, lanes[j]] = dw[j]
                dqd_ref[rows, lanes[j]] = dqd[j]
                dkd_ref[rows, lanes[j]] = dkd[j]
                dqk_ref[j, c] = dqk[j]
                dgl = jnp.sum(jnp.sum(dS[j] * S[j], axis=1, keepdims=True), axis=0, keepdims=True)
                dgl_ref[j, c] = jnp.broadcast_to(dgl, (1, LANE))
            return tuple(new)
        out = lax.fori_loop(0, cpb, step, tuple(ds_scr[j] for j in heads))
        for j in heads:
            ds_scr[j] = out[j]

    full = jax.ShapeDtypeStruct((T, H * LANE), F32)
    return pl.pallas_call(
        body, name=name, grid=(H // hb, N // cpb),
        in_specs=[col] * 5 + [sq, vec, st],
        out_specs=[col] * 4 + [sq, vec],
        out_shape=[full] * 4 + [jax.ShapeDtypeStruct((H, N, CHUNK, CHUNK), F32), jax.ShapeDtypeStruct((H, N, 1, LANE), F32)],
        scratch_shapes=[pltpu.VMEM((hb, HEAD, HEAD), F32)],
        compiler_params=_params(("parallel", "arbitrary")),
    )(do, w, qd, kd, vn, qk, gl, ss)


def _delta_prep_bwd(qkv, gamB, bB, ti, u, w, qk, du, dw, dqd, dkd, dqk, dgl, H, *, name):
    T = qkv.shape[0]
    N = T // CHUNK
    cpb = _tile(N, CPB, 8)
    grp = min(GRP, cpb)
    col, bc, sq, vec = _delta_specs(T, H, cpb)

    def body(q_ref, k_ref, v_ref, g_ref, b_ref, ti_ref, u_ref, w_ref, qk_ref,
             du_ref, dw_ref, dqd_ref, dkd_ref, dqk_ref, dgl_ref,
             dq_ref, dk_ref, dv_ref, dg_ref, db_ref):
        strict = _tri(strict=True)
        last = lax.broadcasted_iota(jnp.int32, (CHUNK, LANE), 0) == CHUNK - 1
        lsum = lambda x: jnp.sum(x, axis=-1, keepdims=True)
        for c0 in range(0, cpb, grp):
            cs = list(range(c0, c0 + grp))
            rows = [slice(c * CHUNK, (c + 1) * CHUNK) for c in cs]
            ld = lambda r_: [r_[r, :] for r in rows]
            q, k, v, uv, wv, duv, dwv, dqd_v, dkd_v = (ld(r_) for r_ in (q_ref, k_ref, v_ref, u_ref, w_ref, du_ref, dw_ref, dqd_ref, dkd_ref))
            bb = [b_ref[0, r, :] for r in rows]
            gam = [g_ref[0, r, :] for r in rows]
            Ti = [ti_ref[0, c] for c in cs]
            QK = [qk_ref[0, c] for c in cs]
            dqk_v = [dqk_ref[0, c] for c in cs]
            D = _each(_decay, gam)
            e = _each(jnp.exp, gam)
            glast = [g_[CHUNK - 1:CHUNK, :] for g_ in gam]
            eL = _each(lambda gl_, g_: jnp.exp(gl_ - g_), glast, gam)
            kk = _each(lambda k_: _mm16(k_, k_, "nt"), k)
            KKD = _each(lambda kk_, D_: jnp.where(strict, kk_ * D_, 0.0), kk, D)
            dru = _each(lambda t, d_: _mm(t, d_, "tn"), Ti, duv)
            drw = _each(lambda t, d_: _mm(t, d_, "tn"), Ti, dwv)
            l1 = _each(lambda a, b: _mm(a, b, "nt"), dru, uv)
            l2 = _each(lambda a, b: _mm(a, b, "nt"), drw, wv)
            dL = _each(lambda a, b: jnp.where(strict, -(a + b), 0.0), l1, l2)
            Mm = _each(lambda dl, b_: dl * b_[:, :CHUNK], dL, bb)
            dKK = _each(lambda m_, D_: m_ * D_, Mm, D)
            dQK = _each(lambda a, D_: a * D_, dqk_v, D)
            P = _each(lambda m_, kkd, a, qk_: m_ * kkd + a * qk_, Mm, KKD, dqk_v, QK)
            q1 = _each(lambda a, k_: _mm16(a, k_), dQK, k)
            k1 = _each(lambda a, q_: _mm16(a, q_, "tn"), dQK, q)
            k2 = _each(lambda a, k_: _mm16(a, k_), dKK, k)
            k3 = _each(lambda a, k_: _mm16(a, k_, "tn"), dKK, k)
            s1 = _each(lambda dl, kkd: lsum(dl * kkd), dL, KKD)
            p1 = _each(lsum, P)
            p2 = _each(lambda p_: lsum(p_.T), P)
            for i, c in enumerate(cs):
                r = rows[i]
                bek = bb[i] * e[i]
                kdv = eL[i] * k[i]
                dq_ref[r, :] = q1[i] + e[i] * dqd_v[i]
                dk_ref[r, :] = k1[i] + k2[i] + k3[i] + bek * drw[i] + eL[i] * dkd_v[i]
                dv_ref[r, :] = bb[i] * dru[i]
                db_ref[0, r, :] = jnp.broadcast_to(s1[i] + lsum(dru[i] * v[i]) + lsum(drw[i] * e[i] * k[i]), (CHUNK, LANE))
                dgam = (p1[i] - p2[i] + lsum(drw[i] * bek * k[i]) + lsum(dqd_v[i] * e[i] * q[i])
                        - lsum(dkd_v[i] * kdv))
                xlast = jnp.sum(lsum(dkd_v[i] * kdv), axis=0, keepdims=True) + jnp.exp(glast[i]) * dgl_ref[0, c]
                dg_ref[0, r, :] = dgam + jnp.where(last, xlast, 0.0)

    full = jax.ShapeDtypeStruct((T, H * LANE), F32)
    bcs = jax.ShapeDtypeStruct((H, T, LANE), F32)
    return pl.pallas_call(
        body, name=name, grid=(H, N // cpb),
        in_specs=[col(0), col(H), col(2 * H), bc, bc, sq, col(0), col(0), sq, col(0), col(0), col(0), col(0), sq, vec],
        out_specs=[col(0), col(0), col(0), bc, bc],
        out_shape=[full, full, full, bcs, bcs],
        compiler_params=_params(("parallel", "parallel")),
    )(qkv, qkv, qkv, gamB, bB, ti, u, w, qk, du, dw, dqd, dkd, dqk, dgl)


def _adam(parts, w, m, v, *, name, own=None, me=None):
    P, R, C = parts.shape
    if R > 256 and R % 8:
        tr, tc = R, _tile(C, 256)
    else:
        tr, tc = _tile(R, 256, 8), C
    n_own = 0 if own is None else 2

    def body(*refs):
        p_ref, w_ref, m_ref, v_ref, g_ref, d_ref, nm_ref, nv_ref = refs[n_own:]
        g = None
        for i in range(P):
            t = p_ref[i].astype(F32)
            if n_own:
                t = jnp.where(refs[0][0] == i, refs[1][...].astype(F32), t)
            g = t if g is None else g + t
        mn = ADAM_B1 * m_ref[...] + (1.0 - ADAM_B1) * g
        vn = ADAM_B2 * v_ref[...] + (1.0 - ADAM_B2) * (g * g)
        m_hat = mn / (1.0 - ADAM_B1 ** ADAM_STEP)
        v_hat = vn / (1.0 - ADAM_B2 ** ADAM_STEP)
        g_ref[...] = g
        d_ref[...] = -ADAM_LR * (m_hat / (jnp.sqrt(v_hat) + ADAM_EPS) + ADAM_WD * w_ref[...])
        nm_ref[...] = mn
        nv_ref[...] = vn

    blk = pl.BlockSpec((tr, tc), lambda i, j: (i, j))
    return pl.pallas_call(
        body, name=name, grid=(R // tr, C // tc),
        in_specs=[pl.BlockSpec(memory_space=pltpu.SMEM), blk][:n_own] + [pl.BlockSpec((P, tr, tc), lambda i, j: (0, i, j)), blk, blk, blk],
        out_specs=[blk] * 4, out_shape=[jax.ShapeDtypeStruct((R, C), F32)] * 4,
        compiler_params=_params(("parallel", "parallel")),
    )(*([me, own] if n_own else []), parts, w, m, v)


def _mesh_pos():
    return lax.axis_index("x"), lax.axis_index("y"), lax.axis_index("c")


def _peer(k):
    x, y, c = _mesh_pos()
    px, py, pc = x ^ ((k >> 2) & 1), y ^ ((k >> 1) & 1), c ^ (k & 1)
    return (px, py, pc), 4 * px + 2 * py + pc


def _exchange(arrays, scatter, *, name, after=None):
    n = len(arrays)
    n_in = n if after is None else n + 1
    blocks = [a.shape[1:] if scatter else a.shape for a in arrays]

    def body(*refs):
        srcs, dsts = refs[:n], refs[n_in:n_in + n]
        send_sems, recv_sems, local_sems = refs[n_in + n:]
        x, y, c = _mesh_pos()
        me = 4 * x + 2 * y + c
        local, sends = [], []
        for a in range(n):
            cp = pltpu.make_async_copy(srcs[a].at[me] if scatter else srcs[a], dsts[a].at[me], local_sems.at[a])
            cp.start()
            local.append(cp)
            for k in range(1, N_DEV):
                dev, idx = _peer(k)
                cp = pltpu.make_async_remote_copy(
                    src_ref=srcs[a].at[idx] if scatter else srcs[a], dst_ref=dsts[a].at[me],
                    send_sem=send_sems.at[a * N_DEV + k], recv_sem=recv_sems.at[a * N_DEV + k],
                    device_id=dev, device_id_type=MESH)
                cp.start()
                sends.append(cp)
        for a in range(n):
            for k in range(1, N_DEV):
                dev, idx = _peer(k)
                pltpu.make_async_remote_copy(
                    src_ref=srcs[a].at[idx] if scatter else srcs[a], dst_ref=dsts[a].at[idx],
                    send_sem=send_sems.at[a * N_DEV + k], recv_sem=recv_sems.at[a * N_DEV + k],
                    device_id=dev, device_id_type=MESH).wait_recv()
        for cp in sends:
            cp.wait_send()
        for cp in local:
            cp.wait()

    anyspec = pl.BlockSpec(memory_space=pl.ANY)
    return pl.pallas_call(
        body, name=name, in_specs=[anyspec] * n_in, out_specs=[anyspec] * n,
        out_shape=[jax.ShapeDtypeStruct((N_DEV,) + tuple(b), a.dtype) for a, b in zip(arrays, blocks)],
        scratch_shapes=[pltpu.SemaphoreType.DMA((n * N_DEV,)), pltpu.SemaphoreType.DMA((n * N_DEV,)),
                        pltpu.SemaphoreType.DMA((n,))],
    )(*arrays, *([] if after is None else [after]))


_ANY = pl.BlockSpec(memory_space=pl.ANY)
_SEM = pl.BlockSpec(memory_space=pltpu.SEMAPHORE)
_EFFECT = pltpu.SideEffectType.DATAFLOW_SIDE_EFFECTING


def _in_hbm(a):
    return pltpu.with_memory_space_constraint(a, pltpu.HBM)


def _split_copy(src, land, send, recv, k, me, scatter, landed):
    dev, idx = _peer(k)
    return pltpu.make_async_remote_copy(
        src_ref=src.at[idx] if scatter else src, dst_ref=land.at[idx if landed else me],
        send_sem=send.at[k], recv_sem=recv.at[k], device_id=dev, device_id_type=MESH)


ALL_PEERS = tuple(range(1, N_DEV))
SIBLING = 1
SAME_CORE = (2, 4, 6)


def _split_start(srcs, lands, scatter, *, name, relations=None):
    n = len(srcs)
    relations = relations or [ALL_PEERS] * n

    def body(*refs):
        src, land, send, recv, token = refs[:n], refs[n:2 * n], refs[2 * n:3 * n], refs[3 * n:4 * n], refs[-1]
        x, y, c = _mesh_pos()
        me = 4 * x + 2 * y + c
        for a in range(n):
            for k in relations[a]:
                _split_copy(src[a], land[a], send[a], recv[a], k, me, scatter, False).start()
        token[...] = jnp.zeros_like(token)

    outs = pl.pallas_call(
        body, name=name,
        out_shape=[pltpu.SemaphoreType.DMA((N_DEV,))] * (2 * n) + [pltpu.HBM(t.shape, t.dtype) for t in list(srcs) + list(lands)]
        + [jax.ShapeDtypeStruct((8, LANE), F32)],
        in_specs=[_ANY] * (2 * n), out_specs=[_SEM] * (2 * n) + [_ANY] * (2 * n) + [pl.BlockSpec(memory_space=pltpu.VMEM)],
        input_output_aliases={i: 2 * n + i for i in range(2 * n)},
        compiler_params=pltpu.CompilerParams(has_side_effects=_EFFECT),
    )(*[_in_hbm(t) for t in list(srcs) + list(lands)])
    handles = [(outs[a], outs[n + a], outs[2 * n + a], outs[3 * n + a]) for a in range(n)]
    return handles, outs[-1]


def _split_wait(handle, after, scatter, *, name):
    send, recv, src_thru, land_thru = handle

    def body(src_ref, land_ref, send_ref, recv_ref, after_ref, src_out, land_out):
        x, y, c = _mesh_pos()
        me = 4 * x + 2 * y + c
        for k in range(1, N_DEV):
            cp = _split_copy(src_ref, land_ref, send_ref, recv_ref, k, me, scatter, True)
            cp.wait_send()
            cp.wait_recv()

    return pl.pallas_call(
        body, name=name,
        out_shape=(pltpu.HBM(src_thru.shape, src_thru.dtype), pltpu.HBM(land_thru.shape, land_thru.dtype)),
        in_specs=(_ANY, _ANY, _SEM, _SEM, _ANY), out_specs=(_ANY, _ANY), input_output_aliases={0: 0, 1: 1},
        compiler_params=pltpu.CompilerParams(has_side_effects=_EFFECT),
    )(src_thru, land_thru, send, recv, after)[1]


def _forward_copy(land, fsend, frecv, k, landed):
    x, y, c = _mesh_pos()
    _, idx = _peer(k | SIBLING if landed else k)
    return pltpu.make_async_remote_copy(src_ref=land.at[idx], dst_ref=land.at[idx], send_sem=fsend.at[k],
                                        recv_sem=frecv.at[k], device_id=(x, y, 1 - c), device_id_type=MESH)


def _gather_forward(handle, after, *, name):
    send, recv, src_thru, land_thru = handle

    def body(src_ref, land_ref, send_ref, recv_ref, after_ref, src_out, land_out, fsend, frecv):
        x, y, c = _mesh_pos()
        me = 4 * x + 2 * y + c
        for k in SAME_CORE:
            _split_copy(src_ref, land_ref, send_ref, recv_ref, k, me, False, True).wait_recv()
            _forward_copy(land_ref, fsend, frecv, k, False).start()

    src2, land2, fsend, frecv = pl.pallas_call(
        body, name=name,
        out_shape=(pltpu.HBM(src_thru.shape, src_thru.dtype), pltpu.HBM(land_thru.shape, land_thru.dtype),
                   pltpu.SemaphoreType.DMA((N_DEV,)), pltpu.SemaphoreType.DMA((N_DEV,))),
        in_specs=(_ANY, _ANY, _SEM, _SEM, _ANY), out_specs=(_ANY, _ANY, _SEM, _SEM), input_output_aliases={0: 0, 1: 1},
        compiler_params=pltpu.CompilerParams(has_side_effects=_EFFECT),
    )(src_thru, land_thru, send, recv, after)
    return (send, recv, src2, land2), (fsend, frecv)


def _gather_wait_two_level(handle, fwd, *, name):
    send, recv, src_thru, land_thru = handle
    fsend, frecv = fwd

    def body(src_ref, land_ref, send_ref, recv_ref, fsend_ref, frecv_ref, src_out, land_out):
        x, y, c = _mesh_pos()
        me = 4 * x + 2 * y + c
        for k in (SIBLING,) + SAME_CORE:
            _split_copy(src_ref, land_ref, send_ref, recv_ref, k, me, False, True).wait_send()
        _split_copy(src_ref, land_ref, send_ref, recv_ref, SIBLING, me, False, True).wait_recv()
        for k in SAME_CORE:
            _forward_copy(land_ref, fsend_ref, frecv_ref, k, False).wait_send()
            _forward_copy(land_ref, fsend_ref, frecv_ref, k, True).wait_recv()

    return pl.pallas_call(
        body, name=name,
        out_shape=(pltpu.HBM(src_thru.shape, src_thru.dtype), pltpu.HBM(land_thru.shape, land_thru.dtype)),
        in_specs=(_ANY, _ANY, _SEM, _SEM, _SEM, _SEM), out_specs=(_ANY, _ANY), input_output_aliases={0: 0, 1: 1},
        compiler_params=pltpu.CompilerParams(has_side_effects=_EFFECT),
    )(src_thru, land_thru, send, recv, fsend, frecv)[1]


def _local_step(x, p, tgt, S, wt, conv, emit):
    T, D = x.shape
    CW = DNW = D // 2
    H = DNW // HEAD
    nA, nD = CW // LANE, DNW // LANE
    qkv_off, z_off, ab_off = 3 * nA, 3 * nA + 3 * nD, 3 * nA + 4 * nD
    alog = jnp.pad(S["a_log"], ((0, 0), (0, LANE - H)))
    dtb = jnp.pad(S["dt_bias"], ((0, 0), (0, LANE - H)))

    h1 = _rms_fwd(x, S["g_mix"], name="rms1_fwd")
    pp = _matmul(p, wt("w_pp", h1), "nn", name="mm_pp", b_shards=True)
    w_in, cv = wt("w_in", pp), conv(pp)
    proj = _matmul(h1, w_in, "nt", name="mm_in")
    y_a = _group_a_fwd(proj, cv["conv_a"], CW, name="group_a_fwd")
    qkv = _qkv_fwd(proj, cv["conv_qkv"], qkv_off, H, name="qkv_fwd")
    gb, gamc = _gates_fwd(proj, alog, dtb, ab_off, H, name="gates_fwd")
    bcast = lambda cols: jnp.broadcast_to(cols.T[:, :, None], (H, T, LANE))
    gamB, bB = bcast(gamc[:, :H]), bcast(gb[:, H:2 * H])
    u, w, qd, kd, qk, ti, gl = _delta_prep_fwd(qkv, gamB, bB, H, name="delta_prep_fwd")
    o, vn, ss = _delta_scan_fwd(u, w, qd, kd, qk, gl, H, name="delta_scan_fwd")
    y_b = _gated_norm_fwd(o, proj, S["dn_g"], z_off, name="gated_norm_fwd")
    ycat = jnp.concatenate([y_a, y_b], axis=1)
    w_out = wt("w_out", ycat)
    rows = dict(tm=ROW_TILE, tn=D)
    x1, h2 = _matmul(ycat, w_out, "nn", name="mm_out", out_dtypes=(F32, BF16), epilogue=_epi_residual_rms,
                     extras=(x,), vec_extras=(S["g_ffn"],), **rows)
    w_up = wt("w_up", h2)
    up_pre = _matmul(h2, w_up, "nn", name="mm_up", b_shards=True, tn=SHARD_TILE)
    act = _ffn_act_fwd(up_pre, cv["conv_ffn"], name="ffn_act_fwd")
    w_down = wt("w_down", act)
    x2 = _matmul(act, w_down, "nn", name="mm_down", epilogue=lambda acc, r: (acc + r,), extras=(x1,), tk=LONG_K)
    h3 = _rms_fwd(x2, S["g_ple"], name="rms3_fwd")
    w_pg = wt("w_pg", h3)

    def ple_epi(acc, x2r, ppr):
        s = jax.nn.sigmoid(acc)
        return x2r + s * ppr, s

    x3, sg = _matmul(h3, w_pg, "nn", name="mm_pg", out_dtypes=(F32, F32), epilogue=ple_epi, extras=(x2, pp), tm=512)
    dx3, dg_final, loss, dpg, dpp = _final_loss(x3, S["g_final"], tgt, pp, sg, name="final_loss")

    G = {"g_final": dg_final}
    tok = emit({"w_pp": _matmul(p, dpp, "tn", name="mm_dwpp", out_dtypes=(BF16,), out_shards=True, tk=LONG_K),
                "w_pg": _matmul(h3, dpg, "tn", name="mm_dwpg", out_dtypes=(BF16,), tk=LONG_K)})
    bwd = dict(out_dtypes=(F32, BF16), epilogue=_epi_rms_bwd(2), n_vec=1, **rows)
    dx2, dx2b, G["g_ple"] = _matmul(dpg, w_pg, "nt", name="mm_dh3", after=tok, extras=(x2, dx3),
                                    vec_extras=(S["g_ple"],), **bwd)
    tok = emit({"w_down": _matmul(act, dx2b, "tn", name="mm_dwdown", out_dtypes=(BF16,), tk=LONG_K)})
    dact = _matmul(dx2b, w_down, "nt", name="mm_dact", after=tok, tn=SHARD_TILE)
    dup, dcf_g, dcf_v = _ffn_act_bwd(up_pre, cv["conv_ffn"], dact, name="ffn_act_bwd")
    G["conv_ffn"] = jnp.concatenate([dcf_g, dcf_v], axis=1)
    tok = emit({"w_up": _matmul(h2, dup, "tn", name="mm_dwup", out_dtypes=(BF16,), b_shards=True, out_shards=True,
                                tn=SHARD_TILE, tk=LONG_K)})
    dh2 = _matmul(dup, w_up, "nt", name="mm_dh2", after=tok, a_shards=True, b_shards=True, tk=2 * SHARD_TILE)
    dx1, dx1b, G["g_ffn"] = _rms_bwd(x1, S["g_ffn"], dh2, dx2, name="rms2_bwd")
    tok = emit({"w_out": _matmul(ycat, dx1b, "tn", name="mm_dwout", out_dtypes=(BF16,), tk=LONG_K)})
    dycat = _matmul(dx1b, w_out, "nt", name="mm_dycat", after=tok)
    do, dz, G["dn_g"] = _gated_norm_bwd(o, proj, S["dn_g"], dycat, z_off, nA, name="gated_norm_bwd")
    du, dw, dqd, dkd, dqk, dgl = _delta_scan_bwd(do, w, qd, kd, vn, qk, gl, ss, H, name="delta_scan_bwd")
    dq, dk, dv, dgB, dbB = _delta_prep_bwd(qkv, gamB, bB, ti, u, w, qk, du, dw, dqd, dkd, dqk, dgl, H,
                                           name="delta_prep_bwd")
    dgb = jnp.pad(jnp.concatenate([dgB[:, :, 0].T, dbB[:, :, 0].T], axis=1), ((0, 0), (0, LANE - 2 * H)))
    dab, dal, ddt = _gates_bwd(proj, alog, dtb, dgb, ab_off, H, name="gates_bwd")
    G["a_log"], G["dt_bias"] = dal[:, :H], ddt[:, :H]
    dqkv, G["conv_qkv"] = _qkv_bwd(proj, cv["conv_qkv"], dq, dk, dv, qkv_off, H, name="qkv_bwd")
    dax, dab_, dac, G["conv_a"] = _group_a_bwd(proj, cv["conv_a"], dycat, CW, name="group_a_bwd")
    in_p = w_in.shape[0]
    dproj = jnp.concatenate([dax, dab_, dac, dqkv, dz, dab, jnp.zeros((T, in_p - (ab_off + 1) * LANE), BF16)], axis=1)
    tok = emit({"w_in": _matmul(dproj, h1, "tn", name="mm_dwin", out_dtypes=(BF16,), tk=LONG_K)})
    dh1 = _matmul(dproj, w_in, "nn", name="mm_dh1", after=tok, tk=LONG_K)
    grad_x, _, G["g_mix"] = _rms_bwd(x, S["g_mix"], dh1, dx1, name="rms1_bwd")
    return loss, grad_x, G


def _col_sharded(landed):
    _, R, C = landed.shape
    return jnp.transpose(landed, (1, 0, 2)).reshape(R, N_DEV * C)


def kernel(x, p, norm_mix_g, w_in, conv_a_w, conv_qkv_w, a_log, dt_bias, dn_norm_g, w_out, norm_ffn_g, w_up, conv_ffn_w, w_down, norm_ple_g, w_ple_gate, w_ple_proj, final_norm_g, loss_target, m_norm_mix_g, m_w_in, m_conv_a_w, m_conv_qkv_w, m_a_log, m_dt_bias, m_dn_norm_g, m_w_out, m_norm_ffn_g, m_w_up, m_conv_ffn_w, m_w_down, m_norm_ple_g, m_w_ple_gate, m_w_ple_proj, m_final_norm_g, v_norm_mix_g, v_w_in, v_conv_a_w, v_conv_qkv_w, v_a_log, v_dt_bias, v_dn_norm_g, v_w_out, v_norm_ffn_g, v_w_up, v_conv_ffn_w, v_w_down, v_norm_ple_g, v_w_ple_gate, v_w_ple_proj, v_final_norm_g):
    T, D = x.shape[1], x.shape[2]
    xd, _, cd = _mesh_pos()
    me = 4 * xd + 2 * lax.axis_index("y") + cd

    conv_sh = [conv_a_w[0], conv_qkv_w[0], conv_ffn_w[0]]
    conv_n = [c.size for c in conv_sh]
    pack_rows = -(-sum(conv_n) // LANE)
    conv_pack = jnp.pad(jnp.concatenate([c.reshape(-1) for c in conv_sh]), (0, pack_rows * LANE - sum(conv_n))).reshape(pack_rows, LANE)
    names = ["w_pp", "w_in", "conv", "w_out", "w_up", "w_down", "w_pg"]
    tr_ = lambda t: jnp.swapaxes(t, 1, 2)
    shards = [w_ple_proj[0].astype(BF16), w_in[0].T.astype(BF16), conv_pack, w_out[0].astype(BF16), w_up[0].astype(BF16),
              w_down[0].astype(BF16), w_ple_gate[0].astype(BF16)]
    empty_slots = lambda blocks: [lax.empty((N_DEV,) + tuple(b.shape), b.dtype) for b in blocks]
    handles, tok0 = _split_start(shards, empty_slots(shards), False, name="gather_start",
                                 relations=[(SIBLING,) + SAME_CORE if nm == "w_in" else ALL_PEERS for nm in names])
    handle = dict(zip(names, handles))
    own = dict(zip(names, shards))
    in_cols = N_DEV * w_in.shape[2]
    in_p = (in_cols // LANE) * LANE + AB_PAD
    in_place = {"w_up", "w_pp"}

    def gathered(name, after):
        if name == "w_in":
            passed, fwd = _gather_forward(handle[name], after, name="gather_forward_w_in")
            landed = _gather_wait_two_level(passed, fwd, name="gather_wait_w_in")
        else:
            landed = _split_wait(handle[name], after, False, name="gather_wait_" + name)
        return lax.dynamic_update_index_in_dim(landed, own[name], me, 0)

    def wt(name, after):
        landed = gathered(name, after)
        if name in in_place:
            return landed
        full = landed.reshape(-1, D)
        return jnp.pad(full, ((0, in_p - in_cols), (0, 0))) if name == "w_in" else full

    def conv(after):
        flat = gathered("conv", after).reshape(N_DEV, pack_rows * LANE)
        out, o_ = {}, 0
        for nm, c, n_ in zip(("conv_a", "conv_qkv", "conv_ffn"), conv_sh, conv_n):
            out[nm] = _col_sharded(flat[:, o_:o_ + n_].reshape((N_DEV,) + c.shape))
            o_ += n_
        return out

    pending, mine = {}, {}

    def emit(grads):
        parts = [g if nm in in_place else (g[:in_cols] if nm == "w_in" else g).reshape(N_DEV, -1, D)
                 for nm, g in grads.items()]
        hs, tok = _split_start(parts, empty_slots([q[0] for q in parts]), True, name="scatter_start_" + "_".join(grads))
        pending.update(zip(grads, hs))
        mine.update({nm: lax.dynamic_index_in_dim(q, me, 0, keepdims=False) for nm, q in zip(grads, parts)})
        return tok

    S = {
        "g_mix": norm_mix_g + tok0[0, 0], "a_log": a_log, "dt_bias": dt_bias, "dn_g": dn_norm_g, "g_ffn": norm_ffn_g,
        "g_ple": norm_ple_g, "g_final": final_norm_g.reshape(1, D),
    }

    loss_v, grad_x, G = _local_step(x[0], p[0, 0], loss_target[0], S, wt, conv, emit)
    loss = lax.psum(loss_v[0, 0], ("x", "y", "c"))

    small_names = ["g_mix", "g_ffn", "g_ple", "g_final", "dn_g", "a_log", "dt_bias", "conv_a", "conv_qkv", "conv_ffn"]
    small_rows, pieces = [], []
    for nm in small_names:
        g_ = G[nm].reshape(-1)
        r_ = -(-g_.size // (8 * LANE)) * 8
        small_rows.append(r_)
        pieces.append(jnp.pad(g_, (0, r_ * LANE - g_.size)).reshape(r_, LANE))
    landed = {nm: _split_wait(h_, grad_x, True, name="scatter_wait_" + nm) for nm, h_ in pending.items() if nm != "w_in"}

    def adam(parts, w_, m_, v_, nm, own_=None):
        shp = w_.shape
        w2, m2, v2 = (t.reshape(parts.shape[1:]) for t in (w_, m_, v_))
        kw = {} if own_ is None else {"own": own_, "me": me.astype(jnp.int32).reshape(1)}
        return tuple(t.reshape(shp) for t in _adam(parts, w2, m2, v2, name="adam_" + nm, **kw))

    big = {
        "w_up": adam(landed["w_up"], w_up, m_w_up, v_w_up, "w_up", mine["w_up"]),
        "w_down": adam(landed["w_down"], w_down, m_w_down, v_w_down, "w_down", mine["w_down"]),
        "w_out": adam(landed["w_out"], w_out, m_w_out, v_w_out, "w_out", mine["w_out"]),
        "w_pg": adam(landed["w_pg"], w_ple_gate, m_w_ple_gate, v_w_ple_gate, "w_ple_gate", mine["w_pg"]),
        "w_pp": adam(landed["w_pp"], w_ple_proj, m_w_ple_proj, v_w_ple_proj, "w_ple_proj", mine["w_pp"]),
    }
    first = lambda t: lax.slice(t, (0,) * t.ndim, (1,) * t.ndim).reshape(1)
    big_done = sum(first(r[1]) for r in big.values())
    (small_l,) = _exchange([jnp.concatenate(pieces, axis=0)], False, name="gather_small_grads", after=big_done)

    def small_parts(nm):
        i = small_names.index(nm)
        r0 = sum(small_rows[:i])
        shp = G[nm].shape
        return small_l[:, r0:r0 + small_rows[i], :].reshape(N_DEV, -1)[:, :G[nm].size].reshape((N_DEV,) + shp)

    def conv_parts(nm, shard):
        full = small_parts(nm)
        C = shard.shape[-1]
        return lax.dynamic_slice_in_dim(full, me * C, C, axis=2)

    res = [
        adam(small_parts("g_mix"), norm_mix_g, m_norm_mix_g, v_norm_mix_g, "norm_mix_g"),
        None,
        adam(conv_parts("conv_a", conv_a_w), conv_a_w, m_conv_a_w, v_conv_a_w, "conv_a_w"),
        adam(conv_parts("conv_qkv", conv_qkv_w), conv_qkv_w, m_conv_qkv_w, v_conv_qkv_w, "conv_qkv_w"),
        adam(small_parts("a_log"), a_log, m_a_log, v_a_log, "a_log"),
        adam(small_parts("dt_bias"), dt_bias, m_dt_bias, v_dt_bias, "dt_bias"),
        adam(small_parts("dn_g"), dn_norm_g, m_dn_norm_g, v_dn_norm_g, "dn_norm_g"),
        big["w_out"],
        adam(small_parts("g_ffn"), norm_ffn_g, m_norm_ffn_g, v_norm_ffn_g, "norm_ffn_g"),
        big["w_up"],
        adam(conv_parts("conv_ffn", conv_ffn_w), conv_ffn_w, m_conv_ffn_w, v_conv_ffn_w, "conv_ffn_w"),
        big["w_down"],
        adam(small_parts("g_ple"), norm_ple_g, m_norm_ple_g, v_norm_ple_g, "norm_ple_g"),
        big["w_pg"],
        big["w_pp"],
        adam(small_parts("g_final"), final_norm_g.reshape(1, D), m_final_norm_g.reshape(1, D),
             v_final_norm_g.reshape(1, D), "final_norm_g"),
    ]
    res[-1] = tuple(t.reshape(D) for t in res[-1])
    landed_in = _split_wait(pending["w_in"], res[10][1], True, name="scatter_wait_w_in")
    res[1] = tuple(tr_(t) for t in adam(landed_in, tr_(w_in), tr_(m_w_in), tr_(v_w_in), "w_in", mine["w_in"]))
    grads, deltas, new_m, new_v = zip(*res)
    return (loss, grad_x[None], *grads, *deltas, *new_m, *new_v)
```

```python
import functools

import jax
import jax.numpy as jnp
from jax import lax
from jax.experimental import pallas as pl
from jax.experimental.pallas import tpu as pltpu

F32 = jnp.float32
BF16 = jnp.bfloat16

EPS = 1e-6
CHUNK = 64
HEAD = 128
LANE = 128
N_DEV = 8
AB_PAD = 512

ADAM_LR = 0.001
ADAM_B1 = 0.9
ADAM_B2 = 0.999
ADAM_EPS = 1e-08
ADAM_WD = 0.01
ADAM_STEP = 10

MESH = pl.DeviceIdType.MESH


def _tile(dim, target, align=LANE):
    if dim <= target:
        return dim
    t = (target // align) * align
    while t > align and dim % t:
        t -= align
    assert dim % t == 0, (dim, target)
    return t


def _params(sem, vmem_mb=48):
    return pltpu.CompilerParams(dimension_semantics=sem, vmem_limit_bytes=vmem_mb << 20)


_DN = {"nn": (((1,), (0,)), ((), ())), "nt": (((1,), (1,)), ((), ())), "tn": (((0,), (0,)), ((), ()))}
LONG_K = 4096
SHARD_TILE = 1408


def _matmul(a, b, mode, *, name, out_dtypes=(F32,), epilogue=None, extras=(), vec_extras=(), n_vec=0, after=None,
            a_shards=False, b_shards=False, out_shards=False, tm=1024, tn=1024, tk=2048):
    shard_w = b.shape[2] if b_shards else None
    if b_shards:
        b_rows, b_cols = b.shape[1], b.shape[0] * shard_w
    else:
        b_rows, b_cols = b.shape
    a_w = a.shape[2] if a_shards else None
    a_dims = (a.shape[1], a.shape[0] * a_w) if a_shards else a.shape
    if mode == "nn":
        (M, K), (K2, N) = a_dims, (b_rows, b_cols)
    elif mode == "nt":
        (M, K), (N, K2) = a_dims, (b_rows, b_cols)
    else:
        (K, M), (K2, N) = a_dims, (b_rows, b_cols)
    assert K == K2, (name, a.shape, b.shape)
    tm = _tile(M, tm)
    n_dims = [N] + ([shard_w] if (b_shards and mode != "nt") else []) + ([N // N_DEV] if out_shards else [])
    tn = _tile(min(n_dims), tn)
    assert all(d % tn == 0 for d in n_dims), (name, n_dims, tn)
    grp = 1
    if b_shards and mode == "nt":
        grp = max(g for g in (1, 2, 4, 8) if g <= max(1, tk // shard_w) and (a_w is None or a_w % (g * shard_w) == 0))
    k_dims = [K] + ([shard_w] if (b_shards and mode == "nt") else []) + ([a_w] if a_shards else [])
    tk = grp * shard_w if grp > 1 else _tile(min(k_dims), tk)
    assert K % tk == 0, (name, K, tk)
    nk = K // tk
    n_ex, n_out = len(extras) + len(vec_extras), len(out_dtypes)
    assert n_vec == 0 or tn == N, (name, tn, N)
    dn = _DN[mode]

    n_tok = 0 if after is None else 1

    def body(a_ref, b_ref, *rest):
        rest = rest[n_tok:]
        ex_refs, out_refs, vec_refs = rest[:n_ex], rest[n_ex:n_ex + n_out], rest[n_ex + n_out:n_ex + n_out + n_vec]
        if grp > 1:
            part = sum(lax.dot_general(a_ref[:, s * shard_w:(s + 1) * shard_w].astype(BF16), b_ref[s].astype(BF16), dn,
                                       preferred_element_type=F32) for s in range(grp))
        else:
            part = lax.dot_general(a_ref[...].astype(BF16), b_ref[...].astype(BF16), dn, preferred_element_type=F32)
        first_rows = pl.program_id(0) == 0

        def finish(res):
            outs = (res,) if epilogue is None else epilogue(res, *[e[...] for e in ex_refs])
            for o_ref, val in zip(out_refs, outs[:n_out]):
                o_ref[...] = val.astype(o_ref.dtype)
            for v_ref, val in zip(vec_refs, outs[n_out:]):
                @pl.when(first_rows)
                def _(v_ref=v_ref, val=val):
                    v_ref[...] = val

                @pl.when(jnp.logical_not(first_rows))
                def _(v_ref=v_ref, val=val):
                    v_ref[...] += val

        if nk == 1:
            finish(part)
            return
        acc, k = rest[-1], pl.program_id(2)

        @pl.when(k == 0)
        def _():
            acc[...] = part

        @pl.when(k > 0)
        def _():
            acc[...] += part

        @pl.when(k == nk - 1)
        def _():
            finish(acc[...])

    if a_shards:
        assert mode == "nt" and a_w % tk == 0, (name, mode, a_w, tk)
        per_a = a_w // tk
        a_spec = pl.BlockSpec((None, tm, tk), lambda i, j, k: (lax.div(k, per_a), i, lax.rem(k, per_a)))
    else:
        a_spec = pl.BlockSpec((tk, tm), lambda i, j, k: (k, i)) if mode == "tn" else pl.BlockSpec((tm, tk), lambda i, j, k: (i, k))
    if b_shards and mode != "nt":
        per = shard_w // tn
        b_spec = pl.BlockSpec((None, tk, tn), lambda i, j, k: (lax.div(j, per), k, lax.rem(j, per)))
    elif b_shards and grp > 1:
        b_spec = pl.BlockSpec((grp, tn, shard_w), lambda i, j, k: (k, j, 0))
    elif b_shards:
        per = shard_w // tk
        b_spec = pl.BlockSpec((None, tn, tk), lambda i, j, k: (lax.div(k, per), j, lax.rem(k, per)))
    else:
        b_spec = pl.BlockSpec((tn, tk), lambda i, j, k: (j, k)) if mode == "nt" else pl.BlockSpec((tk, tn), lambda i, j, k: (k, j))
    mn_spec = pl.BlockSpec((tm, tn), lambda i, j, k: (i, j))
    vec_spec = pl.BlockSpec((1, tn), lambda i, j, k: (0, j))
    if out_shards:
        assert not extras
        per_o = (N // N_DEV) // tn
        out_spec = pl.BlockSpec((None, tm, tn), lambda i, j, k: (lax.div(j, per_o), i, lax.rem(j, per_o)))
        out_dims = (N_DEV, M, N // N_DEV)
    else:
        out_spec, out_dims = mn_spec, (M, N)
    outs = pl.pallas_call(
        body, name=name, grid=(M // tm, N // tn, nk),
        in_specs=[a_spec, b_spec] + [pl.BlockSpec((8, LANE), lambda i, j, k: (0, 0))] * n_tok
        + [mn_spec] * len(extras) + [vec_spec] * len(vec_extras),
        out_specs=[out_spec] * n_out + [vec_spec] * n_vec,
        out_shape=[jax.ShapeDtypeStruct(out_dims, dt) for dt in out_dtypes] + [jax.ShapeDtypeStruct((1, N), F32)] * n_vec,
        scratch_shapes=[pltpu.VMEM((tm, tn), F32)] if nk > 1 else [],
        compiler_params=_params(("arbitrary" if n_vec else "parallel", "parallel", "arbitrary"), 56),
    )(a, b, *([] if after is None else [after]), *extras, *vec_extras)
    return outs[0] if n_out + n_vec == 1 else outs


def _rms_fwd(x, g, *, name):
    T, D = x.shape
    tr = _tile(T, 256, 8)

    def body(x_ref, g_ref, h_ref):
        xv = x_ref[...]
        r = lax.rsqrt(jnp.mean(xv * xv, axis=-1, keepdims=True) + EPS)
        h_ref[...] = (xv * r * g_ref[...]).astype(h_ref.dtype)

    return pl.pallas_call(
        body, name=name, grid=(T // tr,),
        in_specs=[pl.BlockSpec((tr, D), lambda i: (i, 0)), pl.BlockSpec((1, D), lambda i: (0, 0))],
        out_specs=pl.BlockSpec((tr, D), lambda i: (i, 0)),
        out_shape=jax.ShapeDtypeStruct((T, D), BF16),
        compiler_params=_params(("parallel",)),
    )(x, g)


def _rms_bwd(x, g, dh, dres, *, name):
    T, D = x.shape
    tr = _tile(T, 256, 8)
    epi = _epi_rms_bwd(2)

    def body(x_ref, g_ref, dh_ref, dres_ref, dx_ref, dxb_ref, dg_ref):
        dx, _, dgp = epi(dh_ref[...], x_ref[...], dres_ref[...], g_ref[...])

        @pl.when(pl.program_id(0) == 0)
        def _():
            dg_ref[...] = jnp.zeros_like(dg_ref)

        dg_ref[...] += dgp
        dx_ref[...] = dx
        dxb_ref[...] = dx.astype(dxb_ref.dtype)

    row = pl.BlockSpec((tr, D), lambda i: (i, 0))
    vec = pl.BlockSpec((1, D), lambda i: (0, 0))
    return pl.pallas_call(
        body, name=name, grid=(T // tr,),
        in_specs=[row, vec, row, row], out_specs=[row, row, vec],
        out_shape=[jax.ShapeDtypeStruct((T, D), F32), jax.ShapeDtypeStruct((T, D), BF16), jax.ShapeDtypeStruct((1, D), F32)],
        compiler_params=_params(("arbitrary",)),
    )(x, g, dh, dres)


ROW_TILE = 256


def _epi_residual_rms(acc, res, g):
    xn = acc + res
    r = lax.rsqrt(jnp.mean(xn * xn, axis=-1, keepdims=True) + EPS)
    return xn, xn * r * g


def _epi_rms_bwd(n_copies):
    def epi(dh, x, dres, g):
        r = lax.rsqrt(jnp.mean(x * x, axis=-1, keepdims=True) + EPS)
        xh = x * r
        dxh = dh * g
        dx = dres + r * (dxh - xh * jnp.mean(dxh * xh, axis=-1, keepdims=True))
        return (dx,) * n_copies + (jnp.sum(dh * xh, axis=0, keepdims=True),)
    return epi


def _final_loss(x, g, tgt, pp, sg, *, name):
    T, D = x.shape
    tr = _tile(T, 256, 8)

    def body(x_ref, g_ref, t_ref, pp_ref, sg_ref, dx_ref, dg_ref, loss_ref, dpg_ref, dpp_ref):
        xv = x_ref[...]
        r = lax.rsqrt(jnp.mean(xv * xv, axis=-1, keepdims=True) + EPS)
        xh = xv * r
        gv = g_ref[...]
        err = xh * gv - t_ref[...]

        @pl.when(pl.program_id(0) == 0)
        def _():
            dg_ref[...] = jnp.zeros_like(dg_ref)
            loss_ref[...] = jnp.zeros_like(loss_ref)

        part = 0.5 * jnp.sum(jnp.mean(err * err, axis=-1, keepdims=True), axis=0, keepdims=True)
        loss_ref[...] += jnp.broadcast_to(part, loss_ref.shape)
        dy = err * (1.0 / D)
        dg_ref[...] += jnp.sum(dy * xh, axis=0, keepdims=True)
        dxh = dy * gv
        dx = r * (dxh - xh * jnp.mean(dxh * xh, axis=-1, keepdims=True))
        dx_ref[...] = dx
        s = sg_ref[...]
        dpg_ref[...] = (dx * pp_ref[...] * s * (1.0 - s)).astype(dpg_ref.dtype)
        dpp_ref[...] = (dx * s).astype(dpp_ref.dtype)

    row = pl.BlockSpec((tr, D), lambda i: (i, 0))
    vec = pl.BlockSpec((1, D), lambda i: (0, 0))
    return pl.pallas_call(
        body, name=name, grid=(T // tr,),
        in_specs=[row, vec, row, row, row], out_specs=[row, vec, pl.BlockSpec((1, LANE), lambda i: (0, 0)), row, row],
        out_shape=[jax.ShapeDtypeStruct((T, D), F32), jax.ShapeDtypeStruct((1, D), F32),
                   jax.ShapeDtypeStruct((1, LANE), F32)] + [jax.ShapeDtypeStruct((T, D), BF16)] * 2,
        compiler_params=_params(("arbitrary",)),
    )(x, g, tgt, pp, sg)


ROWS_QKV_FWD, ROWS_QKV_BWD, ROWS_FFN_FWD, ROWS_FFN_BWD, ROWS_GROUP_A = 512, 256, 256, 128, 256


def _ext(ref, r0, T, before, after, RC):
    parts = []
    if before:
        p0 = pl.multiple_of(jnp.maximum(r0 - 8, 0), 8)
        parts.append(jnp.where(r0 > 0, ref[pl.ds(p0, 8), :], 0.0))
    parts.append(ref[pl.ds(r0, RC), :])
    if after:
        n0 = pl.multiple_of(jnp.minimum(r0 + RC, T - 8), 8)
        parts.append(jnp.where(r0 + RC < T, ref[pl.ds(n0, 8), :], 0.0))
    return parts[0] if len(parts) == 1 else jnp.concatenate(parts, axis=0)


def _fold8(x):
    return jnp.sum(x.reshape(x.shape[0] // 8, 8, x.shape[1]), axis=0)


def _win(ref, r0, lo, n, T, RC, edge):
    if not edge:
        return ref[pl.ds(r0 + lo, n), :]
    xx = _ext(ref, r0, T, True, True, RC)
    a = 8 + lo
    return (xx if a == 0 else pltpu.roll(xx, xx.shape[0] - a, 0))[:n, :]


def _taps(ref, w_ref, K, r0, n, T, RC, edge):
    wins = [_win(ref, r0, -(K - 1 - j), n, T, RC, edge) for j in range(K)]
    y = wins[0] * w_ref[0:1, :]
    for j in range(1, K):
        y = y + wins[j] * w_ref[j:j + 1, :]
    return wins, y


def _untaps(scr_ref, val, w_ref, K, RC):
    scr_ref[0:val.shape[0], :] = val
    y = scr_ref[K - 1:K - 1 + RC, :] * w_ref[0:1, :]
    for j in range(1, K):
        s = K - 1 - j
        y = y + scr_ref[s:s + RC, :] * w_ref[j:j + 1, :]
    return y


def _peeled(n_chunks, RC, step, init):
    carry = step(0, init, True)
    if n_chunks > 2:
        carry = lax.fori_loop(1, n_chunks - 1, lambda i, c: step(pl.multiple_of(i * RC, RC), c, False), carry)
    if n_chunks > 1:
        carry = step((n_chunks - 1) * RC, carry, True)
    return carry


def _silu(x):
    return x * jax.nn.sigmoid(x)


def _dsilu(x):
    s = jax.nn.sigmoid(x)
    return s * (1.0 + x * (1.0 - s))


def _col_specs(T, offs):
    return [pl.BlockSpec((T, LANE), functools.partial(lambda o, j: (0, o + j), o)) for o in offs]


def _group_a_fwd(proj, conv_w, CW, *, name):
    T = proj.shape[0]
    RC = _tile(T, ROWS_GROUP_A, 8)
    nb = CW // LANE
    K = conv_w.shape[0]

    def body(ax_ref, ab_ref, ac_ref, w_ref, y_ref):
        def step(r0, carry, edge):
            c = None
            for j in range(K):
                lo = -(K - 1 - j)
                t = _win(ac_ref, r0, lo, RC, T, RC, edge) * _win(ax_ref, r0, lo, RC, T, RC, edge) * w_ref[j:j + 1, :]
                c = t if c is None else c + t
            y_ref[pl.ds(r0, RC), :] = (ab_ref[pl.ds(r0, RC), :] * c).astype(y_ref.dtype)
            return carry
        _peeled(T // RC, RC, step, 0)

    return pl.pallas_call(
        body, name=name, grid=(nb,),
        in_specs=_col_specs(T, (0, nb, 2 * nb)) + [pl.BlockSpec((K, LANE), lambda j: (0, j))],
        out_specs=pl.BlockSpec((T, LANE), lambda j: (0, j)),
        out_shape=jax.ShapeDtypeStruct((T, CW), BF16), compiler_params=_params(("parallel",)),
    )(proj, proj, proj, conv_w)


def _group_a_bwd(proj, conv_w, dycat, CW, *, name):
    T = proj.shape[0]
    RC = _tile(T, ROWS_GROUP_A, 8)
    nb = CW // LANE
    K = conv_w.shape[0]

    def body(ax_ref, ab_ref, ac_ref, w_ref, dy_ref, dax_ref, dab_ref, dac_ref, dw_ref, scr_ref):
        def step(r0, accs, edge):
            ms = [_win(ac_ref, r0, -(K - 1 - j), RC, T, RC, edge) * _win(ax_ref, r0, -(K - 1 - j), RC, T, RC, edge)
                  for j in range(K)]
            c = ms[0] * w_ref[0:1, :]
            for j in range(1, K):
                c = c + ms[j] * w_ref[j:j + 1, :]
            dy = dy_ref[pl.ds(r0, RC), :]
            dab_ref[pl.ds(r0, RC), :] = (dy * c).astype(dab_ref.dtype)
            dc2 = _win(dy_ref, r0, 0, RC + 8, T, RC, edge) * _win(ab_ref, r0, 0, RC + 8, T, RC, edge)
            dm = _untaps(scr_ref, dc2, w_ref, K, RC)
            dax_ref[pl.ds(r0, RC), :] = (dm * ac_ref[pl.ds(r0, RC), :]).astype(dax_ref.dtype)
            dac_ref[pl.ds(r0, RC), :] = (dm * ax_ref[pl.ds(r0, RC), :]).astype(dac_ref.dtype)
            return tuple(accs[j] + _fold8(dc2[:RC] * ms[j]) for j in range(K))

        accs = _peeled(T // RC, RC, step, tuple(jnp.zeros((8, LANE), F32) for _ in range(K)))
        for j in range(K):
            dw_ref[j:j + 1, :] = jnp.sum(accs[j], axis=0, keepdims=True)

    col = pl.BlockSpec((T, LANE), lambda j: (0, j))
    wsp = pl.BlockSpec((K, LANE), lambda j: (0, j))
    return pl.pallas_call(
        body, name=name, grid=(nb,),
        in_specs=_col_specs(T, (0, nb, 2 * nb)) + [wsp, col],
        out_specs=[col, col, col, wsp],
        out_shape=[jax.ShapeDtypeStruct((T, CW), BF16)] * 3 + [jax.ShapeDtypeStruct((K, CW), F32)],
        scratch_shapes=[pltpu.VMEM((RC + 8, LANE), F32)],
        compiler_params=_params(("parallel",)),
    )(proj, proj, proj, conv_w, dycat)


def _qkv_fwd(proj, conv_w, off, H, *, name):
    T = proj.shape[0]
    RC = _tile(T, ROWS_QKV_FWD, 8)
    nb = 3 * H
    K = conv_w.shape[0]

    def body(x_ref, w_ref, y_ref):
        j = pl.program_id(0)
        is_qk = j < 2 * H
        scale = jnp.where(j < H, HEAD ** -0.5, 1.0).astype(F32)

        def step(r0, carry, edge):
            s = _silu(_taps(x_ref, w_ref, K, r0, RC, T, RC, edge)[1])
            r = lax.rsqrt(jnp.sum(s * s, axis=-1, keepdims=True) + EPS) * scale
            y_ref[pl.ds(r0, RC), :] = s * jnp.where(is_qk, r, 1.0)
            return carry
        _peeled(T // RC, RC, step, 0)

    return pl.pallas_call(
        body, name=name, grid=(nb,),
        in_specs=_col_specs(T, (off,)) + [pl.BlockSpec((K, LANE), lambda j: (0, j))],
        out_specs=pl.BlockSpec((T, LANE), lambda j: (0, j)),
        out_shape=jax.ShapeDtypeStruct((T, nb * LANE), F32), compiler_params=_params(("parallel",)),
    )(proj, conv_w)


def _qkv_bwd(proj, conv_w, dq, dk, dv, off, H, *, name):
    T = proj.shape[0]
    RC = _tile(T, ROWS_QKV_BWD, 8)
    nb = 3 * H
    K = conv_w.shape[0]

    def body(x_ref, w_ref, dq_ref, dk_ref, dv_ref, dx_ref, dw_ref, scr_ref):
        j = pl.program_id(0)
        is_qk = j < 2 * H
        scale = jnp.where(j < H, HEAD ** -0.5, 1.0).astype(F32)

        def step(r0, accs, edge):
            xs, c2 = _taps(x_ref, w_ref, K, r0, RC + 8, T, RC, edge)
            s2 = _silu(c2)
            dn2 = jnp.where(j < H, _win(dq_ref, r0, 0, RC + 8, T, RC, edge),
                            jnp.where(is_qk, _win(dk_ref, r0, 0, RC + 8, T, RC, edge),
                                      _win(dv_ref, r0, 0, RC + 8, T, RC, edge)))
            r = lax.rsqrt(jnp.sum(s2 * s2, axis=-1, keepdims=True) + EPS)
            nh = s2 * r
            dnp = dn2 * scale
            ds_qk = r * (dnp - nh * jnp.sum(dnp * nh, axis=-1, keepdims=True))
            ds2 = jnp.where(is_qk, ds_qk, dn2)
            dc2 = ds2 * _dsilu(c2)
            dx_ref[pl.ds(r0, RC), :] = _untaps(scr_ref, dc2, w_ref, K, RC).astype(dx_ref.dtype)
            return tuple(accs[jj] + _fold8(dc2[:RC] * xs[jj][:RC]) for jj in range(K))

        accs = _peeled(T // RC, RC, step, tuple(jnp.zeros((8, LANE), F32) for _ in range(K)))
        for jj in range(K):
            dw_ref[jj:jj + 1, :] = jnp.sum(accs[jj], axis=0, keepdims=True)

    col = pl.BlockSpec((T, LANE), lambda j: (0, j))
    wsp = pl.BlockSpec((K, LANE), lambda j: (0, j))
    return pl.pallas_call(
        body, name=name, grid=(nb,),
        in_specs=_col_specs(T, (off,)) + [wsp] + [
            pl.BlockSpec((T, LANE), functools.partial(lambda o, j: (0, jnp.clip(j - o, 0, H - 1)), o)) for o in (0, H, 2 * H)],
        out_specs=[col, wsp],
        out_shape=[jax.ShapeDtypeStruct((T, nb * LANE), BF16), jax.ShapeDtypeStruct((K, nb * LANE), F32)],
        scratch_shapes=[pltpu.VMEM((RC + 8, LANE), F32)],
        compiler_params=_params(("parallel",)),
    )(proj, conv_w, dq, dk, dv)


def _softplus(x):
    return jnp.maximum(x, 0.0) + jnp.log(1.0 + jnp.exp(-jnp.abs(x)))


def _gates_fwd(proj, alog, dtb, off, H, *, name):
    T = proj.shape[0]
    tr = _tile(T, 512, CHUNK)

    def body(ab_ref, al_ref, dt_ref, gam_ref, beta_ref):
        ab = ab_ref[...]
        lane = lax.broadcasted_iota(jnp.int32, ab.shape, 1)
        g = -jnp.exp(al_ref[...]) * _softplus(ab + dt_ref[...])
        gb = jnp.where(lane < H, g, jnp.where(lane < 2 * H, jax.nn.sigmoid(ab), 0.0))
        tril = _tri().astype(F32)
        gam = jnp.concatenate([_mm(tril, gb[c * CHUNK:(c + 1) * CHUNK, :], precision=lax.Precision.HIGHEST)
                               for c in range(tr // CHUNK)], axis=0)
        for h in range(H):
            gam_ref[h] = jnp.broadcast_to(gam[:, h:h + 1], (tr, LANE))
            beta_ref[h] = jnp.broadcast_to(gb[:, H + h:H + h + 1], (tr, LANE))

    vec = pl.BlockSpec((1, LANE), lambda i: (0, 0))
    heads = pl.BlockSpec((H, tr, LANE), lambda i: (0, i, 0))
    return pl.pallas_call(
        body, name=name, grid=(T // tr,),
        in_specs=[pl.BlockSpec((tr, LANE), lambda i: (i, off)), vec, vec],
        out_specs=[heads, heads],
        out_shape=[jax.ShapeDtypeStruct((H, T, LANE), F32)] * 2, compiler_params=_params(("parallel",)),
    )(proj, alog, dtb)


def _gates_bwd(proj, alog, dtb, dgamB, dbB, off, H, *, name):
    T = proj.shape[0]
    tr = _tile(T, 512, CHUNK)

    def body(ab_ref, al_ref, dt_ref, dgam_ref, dbeta_ref, dab_ref, dal_ref, ddt_ref):
        ab = ab_ref[...]
        lane = lax.broadcasted_iota(jnp.int32, ab.shape, 1)
        is_g = lane < H
        d = jnp.zeros_like(ab)
        for h in range(H):
            d = jnp.where(lane == h, dgam_ref[h], jnp.where(lane == H + h, dbeta_ref[h], d))
        triu = _tri(upper=True).astype(F32)
        dg = jnp.concatenate([_mm(triu, d[c * CHUNK:(c + 1) * CHUNK, :], precision=lax.Precision.HIGHEST)
                              for c in range(tr // CHUNK)], axis=0)
        z = ab + dt_ref[...]
        A = -jnp.exp(al_ref[...])
        da = dg * A * jax.nn.sigmoid(z)
        beta = jax.nn.sigmoid(ab)
        db = d * beta * (1.0 - beta)
        dab_ref[...] = jnp.where(is_g, da, jnp.where(lane < 2 * H, db, 0.0)).astype(dab_ref.dtype)

        @pl.when(pl.program_id(0) == 0)
        def _():
            dal_ref[...] = jnp.zeros_like(dal_ref)
            ddt_ref[...] = jnp.zeros_like(ddt_ref)

        dal_ref[...] += jnp.sum(jnp.where(is_g, dg * A * _softplus(z), 0.0), axis=0, keepdims=True)
        ddt_ref[...] += jnp.sum(jnp.where(is_g, da, 0.0), axis=0, keepdims=True)

    vec = pl.BlockSpec((1, LANE), lambda i: (0, 0))
    row = pl.BlockSpec((tr, LANE), lambda i: (i, 0))
    heads = pl.BlockSpec((H, tr, LANE), lambda i: (0, i, 0))
    return pl.pallas_call(
        body, name=name, grid=(T // tr,),
        in_specs=[pl.BlockSpec((tr, LANE), lambda i: (i, off)), vec, vec, heads, heads],
        out_specs=[row, vec, vec],
        out_shape=[jax.ShapeDtypeStruct((T, LANE), BF16), jax.ShapeDtypeStruct((1, LANE), F32),
                   jax.ShapeDtypeStruct((1, LANE), F32)],
        compiler_params=_params(("arbitrary",)),
    )(proj, alog, dtb, dgamB, dbB)


def _gated_norm_fwd(o, proj, gn, zoff, *, name):
    T, W = o.shape
    tr = _tile(T, 512, 8)

    def body(o_ref, z_ref, g_ref, y_ref):
        ov = o_ref[...]
        r = lax.rsqrt(jnp.mean(ov * ov, axis=-1, keepdims=True) + EPS)
        y_ref[...] = (ov * r * g_ref[...] * _silu(z_ref[...])).astype(y_ref.dtype)

    blk = pl.BlockSpec((tr, LANE), lambda i, j: (i, j))
    return pl.pallas_call(
        body, name=name, grid=(T // tr, W // LANE),
        in_specs=[blk, pl.BlockSpec((tr, LANE), lambda i, j: (i, zoff + j)), pl.BlockSpec((1, LANE), lambda i, j: (0, 0))],
        out_specs=blk, out_shape=jax.ShapeDtypeStruct((T, W), BF16), compiler_params=_params(("parallel", "parallel")),
    )(o, proj, gn)


def _gated_norm_bwd(o, proj, gn, dycat, zoff, yoff, *, name):
    T, W = o.shape
    tr = _tile(T, 512, 8)

    def body(o_ref, z_ref, g_ref, dy_ref, do_ref, dz_ref, dg_ref):
        ov, zv, gv, dy = o_ref[...], z_ref[...], g_ref[...], dy_ref[...]
        r = lax.rsqrt(jnp.mean(ov * ov, axis=-1, keepdims=True) + EPS)
        nh = ov * r
        s = _silu(zv)

        @pl.when((pl.program_id(0) == 0) & (pl.program_id(1) == 0))
        def _():
            dg_ref[...] = jnp.zeros_like(dg_ref)

        dg_ref[...] += jnp.sum(dy * nh * s, axis=0, keepdims=True)
        dz_ref[...] = (dy * nh * gv * _dsilu(zv)).astype(dz_ref.dtype)
        dn = dy * gv * s
        do_ref[...] = r * (dn - nh * jnp.mean(dn * nh, axis=-1, keepdims=True))

    blk = pl.BlockSpec((tr, LANE), lambda i, j: (i, j))
    vec = pl.BlockSpec((1, LANE), lambda i, j: (0, 0))
    return pl.pallas_call(
        body, name=name, grid=(T // tr, W // LANE),
        in_specs=[blk, pl.BlockSpec((tr, LANE), lambda i, j: (i, zoff + j)), vec,
                  pl.BlockSpec((tr, LANE), lambda i, j: (i, yoff + j))],
        out_specs=[blk, blk, vec],
        out_shape=[jax.ShapeDtypeStruct((T, W), F32), jax.ShapeDtypeStruct((T, W), BF16),
                   jax.ShapeDtypeStruct((1, LANE), F32)],
        compiler_params=_params(("arbitrary", "arbitrary")),
    )(o, proj, gn, dycat)


def _ffn_act_fwd(up_pre, conv_w, *, name):
    T, F2 = up_pre.shape
    RC = _tile(T, ROWS_FFN_FWD, 8)
    nb = F2 // 2 // LANE
    K = conv_w.shape[0]

    def body(g_ref, v_ref, wg_ref, wv_ref, y_ref):
        def step(r0, carry, edge):
            _, gate = _taps(g_ref, wg_ref, K, r0, RC, T, RC, edge)
            _, val = _taps(v_ref, wv_ref, K, r0, RC, T, RC, edge)
            y_ref[pl.ds(r0, RC), :] = (_silu(gate) * val).astype(y_ref.dtype)
            return carry
        _peeled(T // RC, RC, step, 0)

    return pl.pallas_call(
        body, name=name, grid=(nb,),
        in_specs=_col_specs(T, (0, nb)) + [pl.BlockSpec((K, LANE), lambda j: (0, j)),
                                           pl.BlockSpec((K, LANE), lambda j: (0, nb + j))],
        out_specs=pl.BlockSpec((T, LANE), lambda j: (0, j)),
        out_shape=jax.ShapeDtypeStruct((T, F2 // 2), BF16), compiler_params=_params(("parallel",)),
    )(up_pre, up_pre, conv_w, conv_w)


def _ffn_act_bwd(up_pre, conv_w, dact, *, name):
    T, F2 = up_pre.shape
    RC = _tile(T, ROWS_FFN_BWD, 8)
    nb = F2 // 2 // LANE
    K = conv_w.shape[0]

    def body(g_ref, v_ref, wg_ref, wv_ref, da_ref, d_ref, dwg_ref, dwv_ref, sg_ref, sv_ref):
        def step(r0, accs, edge):
            gs, gate2 = _taps(g_ref, wg_ref, K, r0, RC + 8, T, RC, edge)
            vs, val2 = _taps(v_ref, wv_ref, K, r0, RC + 8, T, RC, edge)
            da2 = _win(da_ref, r0, 0, RC + 8, T, RC, edge)
            dgate2 = da2 * val2 * _dsilu(gate2)
            dval2 = da2 * _silu(gate2)
            d_ref[0, pl.ds(r0, RC), :] = _untaps(sg_ref, dgate2, wg_ref, K, RC).astype(d_ref.dtype)
            d_ref[1, pl.ds(r0, RC), :] = _untaps(sv_ref, dval2, wv_ref, K, RC).astype(d_ref.dtype)
            new = []
            for j in range(K):
                new.append(accs[2 * j] + _fold8(dgate2[:RC] * gs[j][:RC]))
                new.append(accs[2 * j + 1] + _fold8(dval2[:RC] * vs[j][:RC]))
            return tuple(new)

        accs = _peeled(T // RC, RC, step, tuple(jnp.zeros((8, LANE), F32) for _ in range(2 * K)))
        for j in range(K):
            dwg_ref[j:j + 1, :] = jnp.sum(accs[2 * j], axis=0, keepdims=True)
            dwv_ref[j:j + 1, :] = jnp.sum(accs[2 * j + 1], axis=0, keepdims=True)

    col = pl.BlockSpec((T, LANE), lambda j: (0, j))
    wsp = pl.BlockSpec((K, LANE), lambda j: (0, j))
    return pl.pallas_call(
        body, name=name, grid=(nb,),
        in_specs=_col_specs(T, (0, nb)) + [wsp, pl.BlockSpec((K, LANE), lambda j: (0, nb + j)), col],
        out_specs=[pl.BlockSpec((2, T, LANE), lambda j: (0, 0, j)), wsp, wsp],
        out_shape=[jax.ShapeDtypeStruct((2, T, F2 // 2), BF16)] + [jax.ShapeDtypeStruct((K, F2 // 2), F32)] * 2,
        scratch_shapes=[pltpu.VMEM((RC + 8, LANE), F32)] * 2,
        compiler_params=_params(("parallel",)),
    )(up_pre, up_pre, conv_w, conv_w, dact)


CPB = 8
CPB_SCAN = 4
GRP = 8
HP = lax.Precision.HIGH


def _tri(strict=False, upper=False):
    r = lax.broadcasted_iota(jnp.int32, (CHUNK, CHUNK), 0)
    c = lax.broadcasted_iota(jnp.int32, (CHUNK, CHUNK), 1)
    if upper:
        return c >= r
    return (r > c) if strict else (r >= c)


def _mm(a, b, dn="nn", precision=None):
    precision = HP if precision is None else precision
    return lax.dot_general(a, b, _DN[dn], precision=precision, preferred_element_type=F32)


def _mm16(a, b, dn="nn"):
    return lax.dot_general(a.astype(BF16), b.astype(BF16), _DN[dn], preferred_element_type=F32)


def _each(f, *cols):
    return [f(*xs) for xs in zip(*cols)]


def _decay(gam):
    return jnp.exp(jnp.where(_tri(), gam[:, :CHUNK] - gam.T[:CHUNK, :], -1e30))


def _delta_specs(T, H, cpb):
    rows = cpb * CHUNK
    col = lambda o: pl.BlockSpec((rows, LANE), functools.partial(lambda o, h, n: (n, o + h), o))
    bc = pl.BlockSpec((1, rows, LANE), lambda h, n: (h, n, 0))
    sq = pl.BlockSpec((1, cpb, CHUNK, CHUNK), lambda h, n: (h, n, 0, 0))
    vec = pl.BlockSpec((1, cpb, 1, LANE), lambda h, n: (h, n, 0, 0))
    return col, bc, sq, vec


def _delta_prep_fwd(qkv, gamB, bB, H, *, name):
    T = qkv.shape[0]
    N = T // CHUNK
    cpb = _tile(N, CPB, 8)
    grp = min(GRP, cpb)
    col, bc, sq, vec = _delta_specs(T, H, cpb)

    def body(q_ref, k_ref, v_ref, g_ref, b_ref, u_ref, w_ref, qd_ref, kd_ref, qk_ref, ti_ref, gl_ref):
        eye = (lax.broadcasted_iota(jnp.int32, (CHUNK, CHUNK), 0) == lax.broadcasted_iota(jnp.int32, (CHUNK, CHUNK), 1)).astype(F32)
        strict = _tri(strict=True)
        for c0 in range(0, cpb, grp):
            cs = list(range(c0, c0 + grp))
            rows = [slice(c * CHUNK, (c + 1) * CHUNK) for c in cs]
            q, k, v = ([r_[r, :] for r in rows] for r_ in (q_ref, k_ref, v_ref))
            bb = [b_ref[0, r, :] for r in rows]
            gam = [g_ref[0, r, :] for r in rows]
            D = _each(_decay, gam)
            e = _each(jnp.exp, gam)
            kk = _each(lambda k_: _mm16(k_, k_, "nt"), k)
            X = _each(lambda kk_, D_, b_: -(jnp.where(strict, kk_ * D_, 0.0) * b_[:, :CHUNK]), kk, D, bb)
            R = _each(lambda x: eye + x, X)
            for _ in range(5):
                X = _each(lambda x: _mm(x, x), X)
                R = _each(lambda r, x: r + _mm(r, x), R, X)
            u = _each(lambda r, b_, v_: _mm(r, b_ * v_), R, bb, v)
            w = _each(lambda r, b_, e_, k_: _mm(r, b_ * e_ * k_), R, bb, e, k)
            qk = _each(lambda q_, k_, D_: _mm16(q_, k_, "nt") * D_, q, k, D)
            for i, c in enumerate(cs):
                glast = gam[i][CHUNK - 1:CHUNK, :]
                u_ref[rows[i], :] = u[i]
                w_ref[rows[i], :] = w[i]
                qd_ref[rows[i], :] = e[i] * q[i]
                kd_ref[rows[i], :] = jnp.exp(glast - gam[i]) * k[i]
                qk_ref[0, c] = qk[i]
                ti_ref[0, c] = R[i]
                gl_ref[0, c] = jnp.exp(glast)

    full = jax.ShapeDtypeStruct((T, H * LANE), F32)
    sqs = jax.ShapeDtypeStruct((H, N, CHUNK, CHUNK), F32)
    return pl.pallas_call(
        body, name=name, grid=(H, N // cpb),
        in_specs=[col(0), col(H), col(2 * H), bc, bc],
        out_specs=[col(0)] * 4 + [sq, sq, vec],
        out_shape=[full] * 4 + [sqs, sqs, jax.ShapeDtypeStruct((H, N, 1, LANE), F32)],
        compiler_params=_params(("parallel", "parallel")),
    )(qkv, qkv, qkv, gamB, bB)


def _scan_specs(H, N, cpb, hb, rev):
    nbk = N // cpb
    blk = (lambda n: nbk - 1 - n) if rev else (lambda n: n)
    col = pl.BlockSpec((cpb * CHUNK, hb * LANE), lambda h, n: (blk(n), h))
    sq = pl.BlockSpec((hb, cpb, CHUNK, CHUNK), lambda h, n: (h, blk(n), 0, 0))
    vec = pl.BlockSpec((hb, cpb, 1, LANE), lambda h, n: (h, blk(n), 0, 0))
    st = pl.BlockSpec((hb, cpb, HEAD, HEAD), lambda h, n: (h, blk(n), 0, 0))
    return col, sq, vec, st


def _delta_scan_fwd(u, w, qd, kd, qk, gl, H, *, name):
    T = u.shape[0]
    N = T // CHUNK
    cpb = _tile(N, CPB_SCAN, 4)
    hb = min(GRP, H)
    col, sq, vec, st = _scan_specs(H, N, cpb, hb, False)
    lanes = [slice(j * LANE, (j + 1) * LANE) for j in range(hb)]
    heads = list(range(hb))

    def body(u_ref, w_ref, qd_ref, kd_ref, qk_ref, gl_ref, o_ref, vn_ref, ss_ref, s_scr):
        @pl.when(pl.program_id(1) == 0)
        def _():
            s_scr[...] = jnp.zeros_like(s_scr)

        def step(c, states):
            rows = pl.ds(pl.multiple_of(c * CHUNK, CHUNK), CHUNK)
            S = list(states)
            for j in heads:
                ss_ref[j, c] = S[j]
            wS = _each(lambda ln, s: _mm16(w_ref[rows, ln], s), lanes, S)
            qS = _each(lambda ln, s: _mm16(qd_ref[rows, ln], s), lanes, S)
            vn = _each(lambda ln, ws: u_ref[rows, ln] - ws, lanes, wS)
            o = _each(lambda j, qs, vn_: qs + _mm16(qk_ref[j, c], vn_), heads, qS, vn)
            new = _each(lambda j, ln, s, vn_: s * gl_ref[j, c] + _mm16(kd_ref[rows, ln], vn_, "tn"),
                        heads, lanes, S, vn)
            for j in heads:
                o_ref[rows, lanes[j]] = o[j]
                vn_ref[rows, lanes[j]] = vn[j]
            return tuple(new)
        out = lax.fori_loop(0, cpb, step, tuple(s_scr[j] for j in heads))
        for j in heads:
            s_scr[j] = out[j]

    full = jax.ShapeDtypeStruct((T, H * LANE), F32)
    return pl.pallas_call(
        body, name=name, grid=(H // hb, N // cpb),
        in_specs=[col] * 4 + [sq, vec],
        out_specs=[col, col, st],
        out_shape=[full, full, jax.ShapeDtypeStruct((H, N, HEAD, HEAD), F32)],
        scratch_shapes=[pltpu.VMEM((hb, HEAD, HEAD), F32)],
        compiler_params=_params(("parallel", "arbitrary")),
    )(u, w, qd, kd, qk, gl)


def _delta_scan_bwd(do, w, qd, kd, vn, qk, gl, ss, H, *, name):
    T = do.shape[0]
    N = T // CHUNK
    cpb = _tile(N, CPB_SCAN, 4)
    hb = min(GRP, H)
    col, sq, vec, st = _scan_specs(H, N, cpb, hb, True)
    lanes = [slice(j * LANE, (j + 1) * LANE) for j in range(hb)]
    heads = list(range(hb))

    def body(do_ref, w_ref, qd_ref, kd_ref, vn_ref, qk_ref, gl_ref, ss_ref,
             du_ref, dw_ref, dqd_ref, dkd_ref, dqk_ref, dgl_ref, ds_scr):
        @pl.when(pl.program_id(1) == 0)
        def _():
            ds_scr[...] = jnp.zeros_like(ds_scr)

        def step(i, dstates):
            c = cpb - 1 - i
            rows = pl.ds(pl.multiple_of(c * CHUNK, CHUNK), CHUNK)
            dS = list(dstates)
            S = [ss_ref[j, c] for j in heads]
            dov = [do_ref[rows, ln] for ln in lanes]
            vnv = [vn_ref[rows, ln] for ln in lanes]
            a1 = _each(lambda j, d_: _mm16(qk_ref[j, c], d_, "tn"), heads, dov)
            a2 = _each(lambda ln, ds: _mm16(kd_ref[rows, ln], ds), lanes, dS)
            dvn = _each(lambda x, y: x + y, a1, a2)
            dqd = _each(lambda d_, s: _mm16(d_, s, "nt"), dov, S)
            dkd = _each(lambda v_, ds: _mm16(v_, ds, "nt"), vnv, dS)
            dqk = _each(lambda d_, v_: _mm16(d_, v_, "nt"), dov, vnv)
            dw = _each(lambda dv_, s: -_mm16(dv_, s, "nt"), dvn, S)
            b1 = _each(lambda ln, d_: _mm16(qd_ref[rows, ln], d_, "tn"), lanes, dov)
            b2 = _each(lambda ln, dv_: _mm16(w_ref[rows, ln], dv_, "tn"), lanes, dvn)
            new = _each(lambda j, x, y, ds: x + ds * gl_ref[j, c] - y, heads, b1, b2, dS)
            for j in heads:
                du_ref[rows, lanes[j]] = dvn[j]
                dw_ref[rows, lanes[j]] = dw[j]
                dqd_ref[rows, lanes[j]] = dqd[j]
                dkd_ref[rows, lanes[j]] = dkd[j]
                dqk_ref[j, c] = dqk[j]
                dgl = jnp.sum(jnp.sum(dS[j] * S[j], axis=1, keepdims=True), axis=0, keepdims=True)
                dgl_ref[j, c] = jnp.broadcast_to(dgl, (1, LANE))
            return tuple(new)
        out = lax.fori_loop(0, cpb, step, tuple(ds_scr[j] for j in heads))
        for j in heads:
            ds_scr[j] = out[j]

    full = jax.ShapeDtypeStruct((T, H * LANE), F32)
    return pl.pallas_call(
        body, name=name, grid=(H // hb, N // cpb),
        in_specs=[col] * 5 + [sq, vec, st],
        out_specs=[col] * 4 + [sq, vec],
        out_shape=[full] * 4 + [jax.ShapeDtypeStruct((H, N, CHUNK, CHUNK), F32), jax.ShapeDtypeStruct((H, N, 1, LANE), F32)],
        scratch_shapes=[pltpu.VMEM((hb, HEAD, HEAD), F32)],
        compiler_params=_params(("parallel", "arbitrary")),
    )(do, w, qd, kd, vn, qk, gl, ss)


def _delta_prep_bwd(qkv, gamB, bB, ti, u, w, qk, du, dw, dqd, dkd, dqk, dgl, H, *, name):
    T = qkv.shape[0]
    N = T // CHUNK
    cpb = _tile(N, CPB, 8)
    grp = min(GRP, cpb)
    col, bc, sq, vec = _delta_specs(T, H, cpb)

    def body(q_ref, k_ref, v_ref, g_ref, b_ref, ti_ref, u_ref, w_ref, qk_ref,
             du_ref, dw_ref, dqd_ref, dkd_ref, dqk_ref, dgl_ref,
             dq_ref, dk_ref, dv_ref, dg_ref, db_ref):
        ones = jnp.ones((CHUNK, LANE), F32)
        strict = _tri(strict=True)
        last = lax.broadcasted_iota(jnp.int32, (CHUNK, LANE), 0) == CHUNK - 1
        lsum = lambda x: jnp.sum(x, axis=-1, keepdims=True)
        for c0 in range(0, cpb, grp):
            cs = list(range(c0, c0 + grp))
            rows = [slice(c * CHUNK, (c + 1) * CHUNK) for c in cs]
            ld = lambda r_: [r_[r, :] for r in rows]
            q, k, v, uv, wv, duv, dwv, dqd_v, dkd_v = (ld(r_) for r_ in (q_ref, k_ref, v_ref, u_ref, w_ref, du_ref, dw_ref, dqd_ref, dkd_ref))
            bb = [b_ref[0, r, :] for r in rows]
            gam = [g_ref[0, r, :] for r in rows]
            Ti = [ti_ref[0, c] for c in cs]
            QK = [qk_ref[0, c] for c in cs]
            dqk_v = [dqk_ref[0, c] for c in cs]
            D = _each(_decay, gam)
            e = _each(jnp.exp, gam)
            glast = [g_[CHUNK - 1:CHUNK, :] for g_ in gam]
            eL = _each(lambda gl_, g_: jnp.exp(gl_ - g_), glast, gam)
            kk = _each(lambda k_: _mm16(k_, k_, "nt"), k)
            KKD = _each(lambda kk_, D_: jnp.where(strict, kk_ * D_, 0.0), kk, D)
            dru = _each(lambda t, d_: _mm(t, d_, "tn"), Ti, duv)
            drw = _each(lambda t, d_: _mm(t, d_, "tn"), Ti, dwv)
            l1 = _each(lambda a, b: _mm(a, b, "nt"), dru, uv)
            l2 = _each(lambda a, b: _mm(a, b, "nt"), drw, wv)
            dL = _each(lambda a, b: jnp.where(strict, -(a + b), 0.0), l1, l2)
            Mm = _each(lambda dl, b_: dl * b_[:, :CHUNK], dL, bb)
            dKK = _each(lambda m_, D_: m_ * D_, Mm, D)
            dQK = _each(lambda a, D_: a * D_, dqk_v, D)
            P = _each(lambda m_, kkd, a, qk_: m_ * kkd + a * qk_, Mm, KKD, dqk_v, QK)
            q1 = _each(lambda a, k_: _mm16(a, k_), dQK, k)
            k1 = _each(lambda a, q_: _mm16(a, q_, "tn"), dQK, q)
            k2 = _each(lambda a, k_: _mm16(a, k_), dKK, k)
            k3 = _each(lambda a, k_: _mm16(a, k_, "tn"), dKK, k)
            s1 = _each(lambda dl, kkd: _mm(dl * kkd, ones), dL, KKD)
            p1 = _each(lambda p_: _mm(p_, ones), P)
            p2 = _each(lambda p_: _mm(p_, ones, "tn"), P)
            for i, c in enumerate(cs):
                r = rows[i]
                bek = bb[i] * e[i]
                kdv = eL[i] * k[i]
                dq_ref[r, :] = q1[i] + e[i] * dqd_v[i]
                dk_ref[r, :] = k1[i] + k2[i] + k3[i] + bek * drw[i] + eL[i] * dkd_v[i]
                dv_ref[r, :] = bb[i] * dru[i]
                db_ref[0, r, :] = s1[i] + lsum(dru[i] * v[i]) + lsum(drw[i] * e[i] * k[i])
                dgam = (p1[i] - p2[i] + lsum(drw[i] * bek * k[i]) + lsum(dqd_v[i] * e[i] * q[i])
                        - lsum(dkd_v[i] * kdv))
                xlast = jnp.sum(lsum(dkd_v[i] * kdv), axis=0, keepdims=True) + jnp.exp(glast[i]) * dgl_ref[0, c]
                dg_ref[0, r, :] = dgam + jnp.where(last, xlast, 0.0)

    full = jax.ShapeDtypeStruct((T, H * LANE), F32)
    bcs = jax.ShapeDtypeStruct((H, T, LANE), F32)
    return pl.pallas_call(
        body, name=name, grid=(H, N // cpb),
        in_specs=[col(0), col(H), col(2 * H), bc, bc, sq, col(0), col(0), sq, col(0), col(0), col(0), col(0), sq, vec],
        out_specs=[col(0), col(0), col(0), bc, bc],
        out_shape=[full, full, full, bcs, bcs],
        compiler_params=_params(("parallel", "parallel")),
    )(qkv, qkv, qkv, gamB, bB, ti, u, w, qk, du, dw, dqd, dkd, dqk, dgl)


def _adam(parts, w, m, v, *, name, own=None, me=None):
    P, R, C = parts.shape
    if R > 256 and R % 8:
        tr, tc = R, _tile(C, 256)
    else:
        tr, tc = _tile(R, 256, 8), C
    n_own = 0 if own is None else 2

    def body(*refs):
        p_ref, w_ref, m_ref, v_ref, g_ref, d_ref, nm_ref, nv_ref = refs[n_own:]
        g = None
        for i in range(P):
            t = p_ref[i].astype(F32)
            if n_own:
                t = jnp.where(refs[0][0] == i, refs[1][...].astype(F32), t)
            g = t if g is None else g + t
        mn = ADAM_B1 * m_ref[...] + (1.0 - ADAM_B1) * g
        vn = ADAM_B2 * v_ref[...] + (1.0 - ADAM_B2) * (g * g)
        m_hat = mn / (1.0 - ADAM_B1 ** ADAM_STEP)
        v_hat = vn / (1.0 - ADAM_B2 ** ADAM_STEP)
        g_ref[...] = g
        d_ref[...] = -ADAM_LR * (m_hat / (jnp.sqrt(v_hat) + ADAM_EPS) + ADAM_WD * w_ref[...])
        nm_ref[...] = mn
        nv_ref[...] = vn

    blk = pl.BlockSpec((tr, tc), lambda i, j: (i, j))
    return pl.pallas_call(
        body, name=name, grid=(R // tr, C // tc),
        in_specs=[pl.BlockSpec(memory_space=pltpu.SMEM), blk][:n_own] + [pl.BlockSpec((P, tr, tc), lambda i, j: (0, i, j)), blk, blk, blk],
        out_specs=[blk] * 4, out_shape=[jax.ShapeDtypeStruct((R, C), F32)] * 4,
        compiler_params=_params(("parallel", "parallel")),
    )(*([me, own] if n_own else []), parts, w, m, v)


def _mesh_pos():
    return lax.axis_index("x"), lax.axis_index("y"), lax.axis_index("c")


def _peer(k):
    x, y, c = _mesh_pos()
    px, py, pc = x ^ ((k >> 2) & 1), y ^ ((k >> 1) & 1), c ^ (k & 1)
    return (px, py, pc), 4 * px + 2 * py + pc


def _exchange(arrays, scatter, *, name, after=None):
    n = len(arrays)
    n_in = n if after is None else n + 1
    blocks = [a.shape[1:] if scatter else a.shape for a in arrays]

    def body(*refs):
        srcs, dsts = refs[:n], refs[n_in:n_in + n]
        send_sems, recv_sems, local_sems = refs[n_in + n:]
        x, y, c = _mesh_pos()
        me = 4 * x + 2 * y + c
        local, sends = [], []
        for a in range(n):
            cp = pltpu.make_async_copy(srcs[a].at[me] if scatter else srcs[a], dsts[a].at[me], local_sems.at[a])
            cp.start()
            local.append(cp)
            for k in range(1, N_DEV):
                dev, idx = _peer(k)
                cp = pltpu.make_async_remote_copy(
                    src_ref=srcs[a].at[idx] if scatter else srcs[a], dst_ref=dsts[a].at[me],
                    send_sem=send_sems.at[a * N_DEV + k], recv_sem=recv_sems.at[a * N_DEV + k],
                    device_id=dev, device_id_type=MESH)
                cp.start()
                sends.append(cp)
        for a in range(n):
            for k in range(1, N_DEV):
                dev, idx = _peer(k)
                pltpu.make_async_remote_copy(
                    src_ref=srcs[a].at[idx] if scatter else srcs[a], dst_ref=dsts[a].at[idx],
                    send_sem=send_sems.at[a * N_DEV + k], recv_sem=recv_sems.at[a * N_DEV + k],
                    device_id=dev, device_id_type=MESH).wait_recv()
        for cp in sends:
            cp.wait_send()
        for cp in local:
            cp.wait()

    anyspec = pl.BlockSpec(memory_space=pl.ANY)
    return pl.pallas_call(
        body, name=name, in_specs=[anyspec] * n_in, out_specs=[anyspec] * n,
        out_shape=[jax.ShapeDtypeStruct((N_DEV,) + tuple(b), a.dtype) for a, b in zip(arrays, blocks)],
        scratch_shapes=[pltpu.SemaphoreType.DMA((n * N_DEV,)), pltpu.SemaphoreType.DMA((n * N_DEV,)),
                        pltpu.SemaphoreType.DMA((n,))],
    )(*arrays, *([] if after is None else [after]))


_ANY = pl.BlockSpec(memory_space=pl.ANY)
_SEM = pl.BlockSpec(memory_space=pltpu.SEMAPHORE)
_EFFECT = pltpu.SideEffectType.DATAFLOW_SIDE_EFFECTING


def _in_hbm(a):
    return pltpu.with_memory_space_constraint(a, pltpu.HBM)


def _split_copy(src, land, send, recv, k, me, scatter, landed):
    dev, idx = _peer(k)
    return pltpu.make_async_remote_copy(
        src_ref=src.at[idx] if scatter else src, dst_ref=land.at[idx if landed else me],
        send_sem=send.at[k], recv_sem=recv.at[k], device_id=dev, device_id_type=MESH)


ALL_PEERS = tuple(range(1, N_DEV))
SIBLING = 1
SAME_CORE = (2, 4, 6)


def _split_start(srcs, lands, scatter, *, name, relations=None):
    n = len(srcs)
    relations = relations or [ALL_PEERS] * n

    def body(*refs):
        src, land, send, recv, token = refs[:n], refs[n:2 * n], refs[2 * n:3 * n], refs[3 * n:4 * n], refs[-1]
        x, y, c = _mesh_pos()
        me = 4 * x + 2 * y + c
        for a in range(n):
            for k in relations[a]:
                _split_copy(src[a], land[a], send[a], recv[a], k, me, scatter, False).start()
        token[...] = jnp.zeros_like(token)

    outs = pl.pallas_call(
        body, name=name,
        out_shape=[pltpu.SemaphoreType.DMA((N_DEV,))] * (2 * n) + [pltpu.HBM(t.shape, t.dtype) for t in list(srcs) + list(lands)]
        + [jax.ShapeDtypeStruct((8, LANE), F32)],
        in_specs=[_ANY] * (2 * n), out_specs=[_SEM] * (2 * n) + [_ANY] * (2 * n) + [pl.BlockSpec(memory_space=pltpu.VMEM)],
        input_output_aliases={i: 2 * n + i for i in range(2 * n)},
        compiler_params=pltpu.CompilerParams(has_side_effects=_EFFECT),
    )(*[_in_hbm(t) for t in list(srcs) + list(lands)])
    handles = [(outs[a], outs[n + a], outs[2 * n + a], outs[3 * n + a]) for a in range(n)]
    return handles, outs[-1]


def _split_wait(handle, after, scatter, *, name):
    send, recv, src_thru, land_thru = handle

    def body(src_ref, land_ref, send_ref, recv_ref, after_ref, src_out, land_out):
        x, y, c = _mesh_pos()
        me = 4 * x + 2 * y + c
        for k in range(1, N_DEV):
            cp = _split_copy(src_ref, land_ref, send_ref, recv_ref, k, me, scatter, True)
            cp.wait_send()
            cp.wait_recv()

    return pl.pallas_call(
        body, name=name,
        out_shape=(pltpu.HBM(src_thru.shape, src_thru.dtype), pltpu.HBM(land_thru.shape, land_thru.dtype)),
        in_specs=(_ANY, _ANY, _SEM, _SEM, _ANY), out_specs=(_ANY, _ANY), input_output_aliases={0: 0, 1: 1},
        compiler_params=pltpu.CompilerParams(has_side_effects=_EFFECT),
    )(src_thru, land_thru, send, recv, after)[1]


def _forward_copy(land, fsend, frecv, k, landed):
    x, y, c = _mesh_pos()
    _, idx = _peer(k | SIBLING if landed else k)
    return pltpu.make_async_remote_copy(src_ref=land.at[idx], dst_ref=land.at[idx], send_sem=fsend.at[k],
                                        recv_sem=frecv.at[k], device_id=(x, y, 1 - c), device_id_type=MESH)


def _gather_forward(handle, after, *, name):
    send, recv, src_thru, land_thru = handle

    def body(src_ref, land_ref, send_ref, recv_ref, after_ref, src_out, land_out, fsend, frecv):
        x, y, c = _mesh_pos()
        me = 4 * x + 2 * y + c
        for k in SAME_CORE:
            _split_copy(src_ref, land_ref, send_ref, recv_ref, k, me, False, True).wait_recv()
            _forward_copy(land_ref, fsend, frecv, k, False).start()

    src2, land2, fsend, frecv = pl.pallas_call(
        body, name=name,
        out_shape=(pltpu.HBM(src_thru.shape, src_thru.dtype), pltpu.HBM(land_thru.shape, land_thru.dtype),
                   pltpu.SemaphoreType.DMA((N_DEV,)), pltpu.SemaphoreType.DMA((N_DEV,))),
        in_specs=(_ANY, _ANY, _SEM, _SEM, _ANY), out_specs=(_ANY, _ANY, _SEM, _SEM), input_output_aliases={0: 0, 1: 1},
        compiler_params=pltpu.CompilerParams(has_side_effects=_EFFECT),
    )(src_thru, land_thru, send, recv, after)
    return (send, recv, src2, land2), (fsend, frecv)


def _gather_wait_two_level(handle, fwd, *, name):
    send, recv, src_thru, land_thru = handle
    fsend, frecv = fwd

    def body(src_ref, land_ref, send_ref, recv_ref, fsend_ref, frecv_ref, src_out, land_out):
        x, y, c = _mesh_pos()
        me = 4 * x + 2 * y + c
        for k in (SIBLING,) + SAME_CORE:
            _split_copy(src_ref, land_ref, send_ref, recv_ref, k, me, False, True).wait_send()
        _split_copy(src_ref, land_ref, send_ref, recv_ref, SIBLING, me, False, True).wait_recv()
        for k in SAME_CORE:
            _forward_copy(land_ref, fsend_ref, frecv_ref, k, False).wait_send()
            _forward_copy(land_ref, fsend_ref, frecv_ref, k, True).wait_recv()

    return pl.pallas_call(
        body, name=name,
        out_shape=(pltpu.HBM(src_thru.shape, src_thru.dtype), pltpu.HBM(land_thru.shape, land_thru.dtype)),
        in_specs=(_ANY, _ANY, _SEM, _SEM, _SEM, _SEM), out_specs=(_ANY, _ANY), input_output_aliases={0: 0, 1: 1},
        compiler_params=pltpu.CompilerParams(has_side_effects=_EFFECT),
    )(src_thru, land_thru, send, recv, fsend, frecv)[1]


def _local_step(x, p, tgt, S, wt, conv, emit):
    T, D = x.shape
    CW = DNW = D // 2
    H = DNW // HEAD
    nA, nD = CW // LANE, DNW // LANE
    qkv_off, z_off, ab_off = 3 * nA, 3 * nA + 3 * nD, 3 * nA + 4 * nD
    alog = jnp.pad(S["a_log"], ((0, 0), (0, LANE - H)))
    dtb = jnp.pad(S["dt_bias"], ((0, 0), (0, LANE - H)))

    h1 = _rms_fwd(x, S["g_mix"], name="rms1_fwd")
    pp = _matmul(p, wt("w_pp", h1), "nn", name="mm_pp", b_shards=True)
    w_in, cv = wt("w_in", pp), conv(pp)
    proj = _matmul(h1, w_in, "nt", name="mm_in")
    y_a = _group_a_fwd(proj, cv["conv_a"], CW, name="group_a_fwd")
    qkv = _qkv_fwd(proj, cv["conv_qkv"], qkv_off, H, name="qkv_fwd")
    gamB, bB = _gates_fwd(proj, alog, dtb, ab_off, H, name="gates_fwd")
    u, w, qd, kd, qk, ti, gl = _delta_prep_fwd(qkv, gamB, bB, H, name="delta_prep_fwd")
    o, vn, ss = _delta_scan_fwd(u, w, qd, kd, qk, gl, H, name="delta_scan_fwd")
    y_b = _gated_norm_fwd(o, proj, S["dn_g"], z_off, name="gated_norm_fwd")
    ycat = jnp.concatenate([y_a, y_b], axis=1)
    w_out = wt("w_out", ycat)
    rows = dict(tm=ROW_TILE, tn=D)
    x1, h2 = _matmul(ycat, w_out, "nn", name="mm_out", out_dtypes=(F32, BF16), epilogue=_epi_residual_rms,
                     extras=(x,), vec_extras=(S["g_ffn"],), **rows)
    w_up = wt("w_up", h2)
    up_pre = _matmul(h2, w_up, "nn", name="mm_up", b_shards=True, tn=SHARD_TILE)
    act = _ffn_act_fwd(up_pre, cv["conv_ffn"], name="ffn_act_fwd")
    w_down = wt("w_down", act)
    x2 = _matmul(act, w_down, "nn", name="mm_down", epilogue=lambda acc, r: (acc + r,), extras=(x1,), tk=LONG_K)
    h3 = _rms_fwd(x2, S["g_ple"], name="rms3_fwd")
    w_pg = wt("w_pg", h3)

    def ple_epi(acc, x2r, ppr):
        s = jax.nn.sigmoid(acc)
        return x2r + s * ppr, s

    x3, sg = _matmul(h3, w_pg, "nn", name="mm_pg", out_dtypes=(F32, F32), epilogue=ple_epi, extras=(x2, pp), tm=512)
    dx3, dg_final, loss, dpg, dpp = _final_loss(x3, S["g_final"], tgt, pp, sg, name="final_loss")

    G = {"g_final": dg_final}
    tok = emit({"w_pp": _matmul(p, dpp, "tn", name="mm_dwpp", out_dtypes=(BF16,), out_shards=True, tk=LONG_K),
                "w_pg": _matmul(h3, dpg, "tn", name="mm_dwpg", out_dtypes=(BF16,), tk=LONG_K)})
    bwd = dict(out_dtypes=(F32, BF16), epilogue=_epi_rms_bwd(2), n_vec=1, **rows)
    dx2, dx2b, G["g_ple"] = _matmul(dpg, w_pg, "nt", name="mm_dh3", after=tok, extras=(x2, dx3),
                                    vec_extras=(S["g_ple"],), **bwd)
    tok = emit({"w_down": _matmul(act, dx2b, "tn", name="mm_dwdown", out_dtypes=(BF16,), tk=LONG_K)})
    dact = _matmul(dx2b, w_down, "nt", name="mm_dact", after=tok, tn=SHARD_TILE)
    dup, dcf_g, dcf_v = _ffn_act_bwd(up_pre, cv["conv_ffn"], dact, name="ffn_act_bwd")
    G["conv_ffn"] = jnp.concatenate([dcf_g, dcf_v], axis=1)
    tok = emit({"w_up": _matmul(h2, dup, "tn", name="mm_dwup", out_dtypes=(BF16,), b_shards=True, out_shards=True,
                                tn=SHARD_TILE, tk=LONG_K)})
    dh2 = _matmul(dup, w_up, "nt", name="mm_dh2", after=tok, a_shards=True, b_shards=True, tk=2 * SHARD_TILE)
    dx1, dx1b, G["g_ffn"] = _rms_bwd(x1, S["g_ffn"], dh2, dx2, name="rms2_bwd")
    tok = emit({"w_out": _matmul(ycat, dx1b, "tn", name="mm_dwout", out_dtypes=(BF16,), tk=LONG_K)})
    dycat = _matmul(dx1b, w_out, "nt", name="mm_dycat", after=tok)
    do, dz, G["dn_g"] = _gated_norm_bwd(o, proj, S["dn_g"], dycat, z_off, nA, name="gated_norm_bwd")
    du, dw, dqd, dkd, dqk, dgl = _delta_scan_bwd(do, w, qd, kd, vn, qk, gl, ss, H, name="delta_scan_bwd")
    dq, dk, dv, dgB, dbB = _delta_prep_bwd(qkv, gamB, bB, ti, u, w, qk, du, dw, dqd, dkd, dqk, dgl, H,
                                           name="delta_prep_bwd")
    dab, dal, ddt = _gates_bwd(proj, alog, dtb, dgB, dbB, ab_off, H, name="gates_bwd")
    G["a_log"], G["dt_bias"] = dal[:, :H], ddt[:, :H]
    dqkv, G["conv_qkv"] = _qkv_bwd(proj, cv["conv_qkv"], dq, dk, dv, qkv_off, H, name="qkv_bwd")
    dax, dab_, dac, G["conv_a"] = _group_a_bwd(proj, cv["conv_a"], dycat, CW, name="group_a_bwd")
    in_p = w_in.shape[0]
    dproj = jnp.concatenate([dax, dab_, dac, dqkv, dz, dab, jnp.zeros((T, in_p - (ab_off + 1) * LANE), BF16)], axis=1)
    tok = emit({"w_in": _matmul(dproj, h1, "tn", name="mm_dwin", out_dtypes=(BF16,), tk=LONG_K)})
    dh1 = _matmul(dproj, w_in, "nn", name="mm_dh1", after=tok, tk=LONG_K)
    grad_x, _, G["g_mix"] = _rms_bwd(x, S["g_mix"], dh1, dx1, name="rms1_bwd")
    return loss, grad_x, G


def _col_sharded(landed):
    _, R, C = landed.shape
    return jnp.transpose(landed, (1, 0, 2)).reshape(R, N_DEV * C)


def kernel(x, p, norm_mix_g, w_in, conv_a_w, conv_qkv_w, a_log, dt_bias, dn_norm_g, w_out, norm_ffn_g, w_up, conv_ffn_w, w_down, norm_ple_g, w_ple_gate, w_ple_proj, final_norm_g, loss_target, m_norm_mix_g, m_w_in, m_conv_a_w, m_conv_qkv_w, m_a_log, m_dt_bias, m_dn_norm_g, m_w_out, m_norm_ffn_g, m_w_up, m_conv_ffn_w, m_w_down, m_norm_ple_g, m_w_ple_gate, m_w_ple_proj, m_final_norm_g, v_norm_mix_g, v_w_in, v_conv_a_w, v_conv_qkv_w, v_a_log, v_dt_bias, v_dn_norm_g, v_w_out, v_norm_ffn_g, v_w_up, v_conv_ffn_w, v_w_down, v_norm_ple_g, v_w_ple_gate, v_w_ple_proj, v_final_norm_g):
    T, D = x.shape[1], x.shape[2]
    xd, _, cd = _mesh_pos()
    me = 4 * xd + 2 * lax.axis_index("y") + cd

    conv_sh = [conv_a_w[0], conv_qkv_w[0], conv_ffn_w[0]]
    conv_n = [c.size for c in conv_sh]
    pack_rows = -(-sum(conv_n) // LANE)
    conv_pack = jnp.pad(jnp.concatenate([c.reshape(-1) for c in conv_sh]), (0, pack_rows * LANE - sum(conv_n))).reshape(pack_rows, LANE)
    names = ["w_pp", "w_in", "conv", "w_out", "w_up", "w_down", "w_pg"]
    tr_ = lambda t: jnp.swapaxes(t, 1, 2)
    shards = [w_ple_proj[0].astype(BF16), w_in[0].T.astype(BF16), conv_pack, w_out[0].astype(BF16), w_up[0].astype(BF16),
              w_down[0].astype(BF16), w_ple_gate[0].astype(BF16)]
    empty_slots = lambda blocks: [lax.empty((N_DEV,) + tuple(b.shape), b.dtype) for b in blocks]
    handles, tok0 = _split_start(shards, empty_slots(shards), False, name="gather_start",
                                 relations=[(SIBLING,) + SAME_CORE if nm == "w_in" else ALL_PEERS for nm in names])
    handle = dict(zip(names, handles))
    own = dict(zip(names, shards))
    in_cols = N_DEV * w_in.shape[2]
    in_p = (in_cols // LANE) * LANE + AB_PAD
    in_place = {"w_up", "w_pp"}

    def gathered(name, after):
        if name == "w_in":
            passed, fwd = _gather_forward(handle[name], after, name="gather_forward_w_in")
            landed = _gather_wait_two_level(passed, fwd, name="gather_wait_w_in")
        else:
            landed = _split_wait(handle[name], after, False, name="gather_wait_" + name)
        return lax.dynamic_update_index_in_dim(landed, own[name], me, 0)

    def wt(name, after):
        landed = gathered(name, after)
        if name in in_place:
            return landed
        full = landed.reshape(-1, D)
        return jnp.pad(full, ((0, in_p - in_cols), (0, 0))) if name == "w_in" else full

    def conv(after):
        flat = gathered("conv", after).reshape(N_DEV, pack_rows * LANE)
        out, o_ = {}, 0
        for nm, c, n_ in zip(("conv_a", "conv_qkv", "conv_ffn"), conv_sh, conv_n):
            out[nm] = _col_sharded(flat[:, o_:o_ + n_].reshape((N_DEV,) + c.shape))
            o_ += n_
        return out

    pending, mine = {}, {}

    def emit(grads):
        parts = [g if nm in in_place else (g[:in_cols] if nm == "w_in" else g).reshape(N_DEV, -1, D)
                 for nm, g in grads.items()]
        hs, tok = _split_start(parts, empty_slots([q[0] for q in parts]), True, name="scatter_start_" + "_".join(grads))
        pending.update(zip(grads, hs))
        mine.update({nm: lax.dynamic_index_in_dim(q, me, 0, keepdims=False) for nm, q in zip(grads, parts)})
        return tok

    S = {
        "g_mix": norm_mix_g + tok0[0, 0], "a_log": a_log, "dt_bias": dt_bias, "dn_g": dn_norm_g, "g_ffn": norm_ffn_g,
        "g_ple": norm_ple_g, "g_final": final_norm_g.reshape(1, D),
    }

    loss_v, grad_x, G = _local_step(x[0], p[0, 0], loss_target[0], S, wt, conv, emit)
    loss = lax.psum(loss_v[0, 0], ("x", "y", "c"))

    small_names = ["g_mix", "g_ffn", "g_ple", "g_final", "dn_g", "a_log", "dt_bias", "conv_a", "conv_qkv", "conv_ffn"]
    small_rows, pieces = [], []
    for nm in small_names:
        g_ = G[nm].reshape(-1)
        r_ = -(-g_.size // (8 * LANE)) * 8
        small_rows.append(r_)
        pieces.append(jnp.pad(g_, (0, r_ * LANE - g_.size)).reshape(r_, LANE))
    landed = {nm: _split_wait(h_, grad_x, True, name="scatter_wait_" + nm) for nm, h_ in pending.items() if nm != "w_in"}

    def adam(parts, w_, m_, v_, nm, own_=None):
        shp = w_.shape
        w2, m2, v2 = (t.reshape(parts.shape[1:]) for t in (w_, m_, v_))
        kw = {} if own_ is None else {"own": own_, "me": me.astype(jnp.int32).reshape(1)}
        return tuple(t.reshape(shp) for t in _adam(parts, w2, m2, v2, name="adam_" + nm, **kw))

    big = {
        "w_up": adam(landed["w_up"], w_up, m_w_up, v_w_up, "w_up", mine["w_up"]),
        "w_down": adam(landed["w_down"], w_down, m_w_down, v_w_down, "w_down", mine["w_down"]),
        "w_out": adam(landed["w_out"], w_out, m_w_out, v_w_out, "w_out", mine["w_out"]),
        "w_pg": adam(landed["w_pg"], w_ple_gate, m_w_ple_gate, v_w_ple_gate, "w_ple_gate", mine["w_pg"]),
        "w_pp": adam(landed["w_pp"], w_ple_proj, m_w_ple_proj, v_w_ple_proj, "w_ple_proj", mine["w_pp"]),
    }
    first = lambda t: lax.slice(t, (0,) * t.ndim, (1,) * t.ndim).reshape(1)
    big_done = sum(first(r[1]) for r in big.values())
    (small_l,) = _exchange([jnp.concatenate(pieces, axis=0)], False, name="gather_small_grads", after=big_done)

    def small_parts(nm):
        i = small_names.index(nm)
        r0 = sum(small_rows[:i])
        shp = G[nm].shape
        return small_l[:, r0:r0 + small_rows[i], :].reshape(N_DEV, -1)[:, :G[nm].size].reshape((N_DEV,) + shp)

    def conv_parts(nm, shard):
        full = small_parts(nm)
        C = shard.shape[-1]
        return lax.dynamic_slice_in_dim(full, me * C, C, axis=2)

    res = [
        adam(small_parts("g_mix"), norm_mix_g, m_norm_mix_g, v_norm_mix_g, "norm_mix_g"),
        None,
        adam(conv_parts("conv_a", conv_a_w), conv_a_w, m_conv_a_w, v_conv_a_w, "conv_a_w"),
        adam(conv_parts("conv_qkv", conv_qkv_w), conv_qkv_w, m_conv_qkv_w, v_conv_qkv_w, "conv_qkv_w"),
        adam(small_parts("a_log"), a_log, m_a_log, v_a_log, "a_log"),
        adam(small_parts("dt_bias"), dt_bias, m_dt_bias, v_dt_bias, "dt_bias"),
        adam(small_parts("dn_g"), dn_norm_g, m_dn_norm_g, v_dn_norm_g, "dn_norm_g"),
        big["w_out"],
        adam(small_parts("g_ffn"), norm_ffn_g, m_norm_ffn_g, v_norm_ffn_g, "norm_ffn_g"),
        big["w_up"],
        adam(conv_parts("conv_ffn", conv_ffn_w), conv_ffn_w, m_conv_ffn_w, v_conv_ffn_w, "conv_ffn_w"),
        big["w_down"],
        adam(small_parts("g_ple"), norm_ple_g, m_norm_ple_g, v_norm_ple_g, "norm_ple_g"),
        big["w_pg"],
        big["w_pp"],
        adam(small_parts("g_final"), final_norm_g.reshape(1, D), m_final_norm_g.reshape(1, D),
             v_final_norm_g.reshape(1, D), "final_norm_g"),
    ]
    res[-1] = tuple(t.reshape(D) for t in res[-1])
    landed_in = _split_wait(pending["w_in"], res[10][1], True, name="scatter_wait_w_in")
    res[1] = tuple(tr_(t) for t in adam(landed_in, tr_(w_in), tr_(m_w_in), tr_(v_w_in), "w_in", mine["w_in"]))
    grads, deltas, new_m, new_v = zip(*res)
    return (loss, grad_x[None], *grads, *deltas, *new_m, *new_v)
```

```python
import functools

import jax
import jax.numpy as jnp
from jax import lax
from jax.experimental import pallas as pl
from jax.experimental.pallas import tpu as pltpu

F32 = jnp.float32
BF16 = jnp.bfloat16

EPS = 1e-6
CHUNK = 64
HEAD = 128
LANE = 128
N_DEV = 8
AB_PAD = 512

ADAM_LR = 0.001
ADAM_B1 = 0.9
ADAM_B2 = 0.999
ADAM_EPS = 1e-08
ADAM_WD = 0.01
ADAM_STEP = 10

MESH = pl.DeviceIdType.MESH


def _tile(dim, target, align=LANE):
    if dim <= target:
        return dim
    t = (target // align) * align
    while t > align and dim % t:
        t -= align
    assert dim % t == 0, (dim, target)
    return t


def _params(sem, vmem_mb=48):
    return pltpu.CompilerParams(dimension_semantics=sem, vmem_limit_bytes=vmem_mb << 20)


_DN = {"nn": (((1,), (0,)), ((), ())), "nt": (((1,), (1,)), ((), ())), "tn": (((0,), (0,)), ((), ()))}
LONG_K = 4096
SHARD_TILE = 1408


def _matmul(a, b, mode, *, name, out_dtypes=(F32,), epilogue=None, extras=(), vec_extras=(), n_vec=0, after=None,
            a_shards=False, b_shards=False, out_shards=False, tm=1024, tn=1024, tk=2048):
    shard_w = b.shape[2] if b_shards else None
    if b_shards:
        b_rows, b_cols = b.shape[1], b.shape[0] * shard_w
    else:
        b_rows, b_cols = b.shape
    a_w = a.shape[2] if a_shards else None
    a_dims = (a.shape[1], a.shape[0] * a_w) if a_shards else a.shape
    if mode == "nn":
        (M, K), (K2, N) = a_dims, (b_rows, b_cols)
    elif mode == "nt":
        (M, K), (N, K2) = a_dims, (b_rows, b_cols)
    else:
        (K, M), (K2, N) = a_dims, (b_rows, b_cols)
    assert K == K2, (name, a.shape, b.shape)
    tm = _tile(M, tm)
    n_dims = [N] + ([shard_w] if (b_shards and mode != "nt") else []) + ([N // N_DEV] if out_shards else [])
    tn = _tile(min(n_dims), tn)
    assert all(d % tn == 0 for d in n_dims), (name, n_dims, tn)
    grp = 1
    if b_shards and mode == "nt":
        grp = max(g for g in (1, 2, 4, 8) if g <= max(1, tk // shard_w) and (a_w is None or a_w % (g * shard_w) == 0))
    k_dims = [K] + ([shard_w] if (b_shards and mode == "nt") else []) + ([a_w] if a_shards else [])
    tk = grp * shard_w if grp > 1 else _tile(min(k_dims), tk)
    assert K % tk == 0, (name, K, tk)
    nk = K // tk
    n_ex, n_out = len(extras) + len(vec_extras), len(out_dtypes)
    assert n_vec == 0 or tn == N, (name, tn, N)
    dn = _DN[mode]

    n_tok = 0 if after is None else 1

    def body(a_ref, b_ref, *rest):
        rest = rest[n_tok:]
        ex_refs, out_refs, vec_refs = rest[:n_ex], rest[n_ex:n_ex + n_out], rest[n_ex + n_out:n_ex + n_out + n_vec]
        if grp > 1:
            part = sum(lax.dot_general(a_ref[:, s * shard_w:(s + 1) * shard_w].astype(BF16), b_ref[s].astype(BF16), dn,
                                       preferred_element_type=F32) for s in range(grp))
        else:
            part = lax.dot_general(a_ref[...].astype(BF16), b_ref[...].astype(BF16), dn, preferred_element_type=F32)
        first_rows = pl.program_id(0) == 0

        def finish(res):
            outs = (res,) if epilogue is None else epilogue(res, *[e[...] for e in ex_refs])
            for o_ref, val in zip(out_refs, outs[:n_out]):
                o_ref[...] = val.astype(o_ref.dtype)
            for v_ref, val in zip(vec_refs, outs[n_out:]):
                @pl.when(first_rows)
                def _(v_ref=v_ref, val=val):
                    v_ref[...] = val

                @pl.when(jnp.logical_not(first_rows))
                def _(v_ref=v_ref, val=val):
                    v_ref[...] += val

        if nk == 1:
            finish(part)
            return
        acc, k = rest[-1], pl.program_id(2)

        @pl.when(k == 0)
        def _():
            acc[...] = part

        @pl.when(k > 0)
        def _():
            acc[...] += part

        @pl.when(k == nk - 1)
        def _():
            finish(acc[...])

    if a_shards:
        assert mode == "nt" and a_w % tk == 0, (name, mode, a_w, tk)
        per_a = a_w // tk
        a_spec = pl.BlockSpec((None, tm, tk), lambda i, j, k: (lax.div(k, per_a), i, lax.rem(k, per_a)))
    else:
        a_spec = pl.BlockSpec((tk, tm), lambda i, j, k: (k, i)) if mode == "tn" else pl.BlockSpec((tm, tk), lambda i, j, k: (i, k))
    if b_shards and mode != "nt":
        per = shard_w // tn
        b_spec = pl.BlockSpec((None, tk, tn), lambda i, j, k: (lax.div(j, per), k, lax.rem(j, per)))
    elif b_shards and grp > 1:
        b_spec = pl.BlockSpec((grp, tn, shard_w), lambda i, j, k: (k, j, 0))
    elif b_shards:
        per = shard_w // tk
        b_spec = pl.BlockSpec((None, tn, tk), lambda i, j, k: (lax.div(k, per), j, lax.rem(k, per)))
    else:
        b_spec = pl.BlockSpec((tn, tk), lambda i, j, k: (j, k)) if mode == "nt" else pl.BlockSpec((tk, tn), lambda i, j, k: (k, j))
    mn_spec = pl.BlockSpec((tm, tn), lambda i, j, k: (i, j))
    vec_spec = pl.BlockSpec((1, tn), lambda i, j, k: (0, j))
    if out_shards:
        assert not extras
        per_o = (N // N_DEV) // tn
        out_spec = pl.BlockSpec((None, tm, tn), lambda i, j, k: (lax.div(j, per_o), i, lax.rem(j, per_o)))
        out_dims = (N_DEV, M, N // N_DEV)
    else:
        out_spec, out_dims = mn_spec, (M, N)
    outs = pl.pallas_call(
        body, name=name, grid=(M // tm, N // tn, nk),
        in_specs=[a_spec, b_spec] + [pl.BlockSpec((8, LANE), lambda i, j, k: (0, 0))] * n_tok
        + [mn_spec] * len(extras) + [vec_spec] * len(vec_extras),
        out_specs=[out_spec] * n_out + [vec_spec] * n_vec,
        out_shape=[jax.ShapeDtypeStruct(out_dims, dt) for dt in out_dtypes] + [jax.ShapeDtypeStruct((1, N), F32)] * n_vec,
        scratch_shapes=[pltpu.VMEM((tm, tn), F32)] if nk > 1 else [],
        compiler_params=_params(("arbitrary" if n_vec else "parallel", "parallel", "arbitrary"), 56),
    )(a, b, *([] if after is None else [after]), *extras, *vec_extras)
    return outs[0] if n_out + n_vec == 1 else outs


def _rms_fwd(x, g, *, name):
    T, D = x.shape
    tr = _tile(T, 256, 8)

    def body(x_ref, g_ref, h_ref):
        xv = x_ref[...]
        r = lax.rsqrt(jnp.mean(xv * xv, axis=-1, keepdims=True) + EPS)
        h_ref[...] = (xv * r * g_ref[...]).astype(h_ref.dtype)

    return pl.pallas_call(
        body, name=name, grid=(T // tr,),
        in_specs=[pl.BlockSpec((tr, D), lambda i: (i, 0)), pl.BlockSpec((1, D), lambda i: (0, 0))],
        out_specs=pl.BlockSpec((tr, D), lambda i: (i, 0)),
        out_shape=jax.ShapeDtypeStruct((T, D), BF16),
        compiler_params=_params(("parallel",)),
    )(x, g)


def _rms_bwd(x, g, dh, dres, *, name):
    T, D = x.shape
    tr = _tile(T, 256, 8)
    epi = _epi_rms_bwd(2)

    def body(x_ref, g_ref, dh_ref, dres_ref, dx_ref, dxb_ref, dg_ref):
        dx, _, dgp = epi(dh_ref[...], x_ref[...], dres_ref[...], g_ref[...])

        @pl.when(pl.program_id(0) == 0)
        def _():
            dg_ref[...] = jnp.zeros_like(dg_ref)

        dg_ref[...] += dgp
        dx_ref[...] = dx
        dxb_ref[...] = dx.astype(dxb_ref.dtype)

    row = pl.BlockSpec((tr, D), lambda i: (i, 0))
    vec = pl.BlockSpec((1, D), lambda i: (0, 0))
    return pl.pallas_call(
        body, name=name, grid=(T // tr,),
        in_specs=[row, vec, row, row], out_specs=[row, row, vec],
        out_shape=[jax.ShapeDtypeStruct((T, D), F32), jax.ShapeDtypeStruct((T, D), BF16), jax.ShapeDtypeStruct((1, D), F32)],
        compiler_params=_params(("arbitrary",)),
    )(x, g, dh, dres)


ROW_TILE = 256


def _epi_residual_rms(acc, res, g):
    xn = acc + res
    r = lax.rsqrt(jnp.mean(xn * xn, axis=-1, keepdims=True) + EPS)
    return xn, xn * r * g


def _epi_rms_bwd(n_copies):
    def epi(dh, x, dres, g):
        r = lax.rsqrt(jnp.mean(x * x, axis=-1, keepdims=True) + EPS)
        xh = x * r
        dxh = dh * g
        dx = dres + r * (dxh - xh * jnp.mean(dxh * xh, axis=-1, keepdims=True))
        return (dx,) * n_copies + (jnp.sum(dh * xh, axis=0, keepdims=True),)
    return epi


def _final_loss(x, g, tgt, pp, sg, *, name):
    T, D = x.shape
    tr = _tile(T, 256, 8)

    def body(x_ref, g_ref, t_ref, pp_ref, sg_ref, dx_ref, dg_ref, loss_ref, dpg_ref, dpp_ref):
        xv = x_ref[...]
        r = lax.rsqrt(jnp.mean(xv * xv, axis=-1, keepdims=True) + EPS)
        xh = xv * r
        gv = g_ref[...]
        err = xh * gv - t_ref[...]

        @pl.when(pl.program_id(0) == 0)
        def _():
            dg_ref[...] = jnp.zeros_like(dg_ref)
            loss_ref[...] = jnp.zeros_like(loss_ref)

        part = 0.5 * jnp.sum(jnp.mean(err * err, axis=-1, keepdims=True), axis=0, keepdims=True)
        loss_ref[...] += jnp.broadcast_to(part, loss_ref.shape)
        dy = err * (1.0 / D)
        dg_ref[...] += jnp.sum(dy * xh, axis=0, keepdims=True)
        dxh = dy * gv
        dx = r * (dxh - xh * jnp.mean(dxh * xh, axis=-1, keepdims=True))
        dx_ref[...] = dx
        s = sg_ref[...]
        dpg_ref[...] = (dx * pp_ref[...] * s * (1.0 - s)).astype(dpg_ref.dtype)
        dpp_ref[...] = (dx * s).astype(dpp_ref.dtype)

    row = pl.BlockSpec((tr, D), lambda i: (i, 0))
    vec = pl.BlockSpec((1, D), lambda i: (0, 0))
    return pl.pallas_call(
        body, name=name, grid=(T // tr,),
        in_specs=[row, vec, row, row, row], out_specs=[row, vec, pl.BlockSpec((1, LANE), lambda i: (0, 0)), row, row],
        out_shape=[jax.ShapeDtypeStruct((T, D), F32), jax.ShapeDtypeStruct((1, D), F32),
                   jax.ShapeDtypeStruct((1, LANE), F32)] + [jax.ShapeDtypeStruct((T, D), BF16)] * 2,
        compiler_params=_params(("arbitrary",)),
    )(x, g, tgt, pp, sg)


ROWS_QKV_FWD, ROWS_QKV_BWD, ROWS_FFN_FWD, ROWS_FFN_BWD, ROWS_GROUP_A = 512, 256, 256, 128, 256


def _ext(ref, r0, T, before, after, RC):
    parts = []
    if before:
        p0 = pl.multiple_of(jnp.maximum(r0 - 8, 0), 8)
        parts.append(jnp.where(r0 > 0, ref[pl.ds(p0, 8), :], 0.0))
    parts.append(ref[pl.ds(r0, RC), :])
    if after:
        n0 = pl.multiple_of(jnp.minimum(r0 + RC, T - 8), 8)
        parts.append(jnp.where(r0 + RC < T, ref[pl.ds(n0, 8), :], 0.0))
    return parts[0] if len(parts) == 1 else jnp.concatenate(parts, axis=0)


def _fold8(x):
    return jnp.sum(x.reshape(x.shape[0] // 8, 8, x.shape[1]), axis=0)


def _win(ref, r0, lo, n, T, RC, edge):
    if not edge:
        return ref[pl.ds(r0 + lo, n), :]
    xx = _ext(ref, r0, T, True, True, RC)
    a = 8 + lo
    return (xx if a == 0 else pltpu.roll(xx, xx.shape[0] - a, 0))[:n, :]


def _taps(ref, w_ref, K, r0, n, T, RC, edge):
    wins = [_win(ref, r0, -(K - 1 - j), n, T, RC, edge) for j in range(K)]
    y = wins[0] * w_ref[0:1, :]
    for j in range(1, K):
        y = y + wins[j] * w_ref[j:j + 1, :]
    return wins, y


def _untaps(scr_ref, val, w_ref, K, RC):
    scr_ref[0:val.shape[0], :] = val
    y = scr_ref[K - 1:K - 1 + RC, :] * w_ref[0:1, :]
    for j in range(1, K):
        s = K - 1 - j
        y = y + scr_ref[s:s + RC, :] * w_ref[j:j + 1, :]
    return y


def _peeled(n_chunks, RC, step, init):
    carry = step(0, init, True)
    if n_chunks > 2:
        carry = lax.fori_loop(1, n_chunks - 1, lambda i, c: step(pl.multiple_of(i * RC, RC), c, False), carry)
    if n_chunks > 1:
        carry = step((n_chunks - 1) * RC, carry, True)
    return carry


def _silu(x):
    return x * jax.nn.sigmoid(x)


def _dsilu(x):
    s = jax.nn.sigmoid(x)
    return s * (1.0 + x * (1.0 - s))


def _col_specs(T, offs):
    return [pl.BlockSpec((T, LANE), functools.partial(lambda o, j: (0, o + j), o)) for o in offs]


def _group_a_fwd(proj, conv_w, CW, out_cols, *, name):
    T = proj.shape[0]
    RC = _tile(T, ROWS_GROUP_A, 8)
    nb = CW // LANE
    K = conv_w.shape[0]

    def body(ax_ref, ab_ref, ac_ref, w_ref, y_ref):
        def step(r0, carry, edge):
            c = None
            for j in range(K):
                lo = -(K - 1 - j)
                t = _win(ac_ref, r0, lo, RC, T, RC, edge) * _win(ax_ref, r0, lo, RC, T, RC, edge) * w_ref[j:j + 1, :]
                c = t if c is None else c + t
            y_ref[pl.ds(r0, RC), :] = (ab_ref[pl.ds(r0, RC), :] * c).astype(y_ref.dtype)
            return carry
        _peeled(T // RC, RC, step, 0)

    return pl.pallas_call(
        body, name=name, grid=(nb,),
        in_specs=_col_specs(T, (0, nb, 2 * nb)) + [pl.BlockSpec((K, LANE), lambda j: (0, j))],
        out_specs=pl.BlockSpec((T, LANE), lambda j: (0, j)),
        out_shape=jax.ShapeDtypeStruct((T, out_cols), BF16), compiler_params=_params(("parallel",)),
    )(proj, proj, proj, conv_w)


def _group_a_bwd(proj, conv_w, dycat, CW, *, name):
    T = proj.shape[0]
    RC = _tile(T, ROWS_GROUP_A, 8)
    nb = CW // LANE
    K = conv_w.shape[0]

    def body(ax_ref, ab_ref, ac_ref, w_ref, dy_ref, dax_ref, dab_ref, dac_ref, dw_ref, scr_ref):
        def step(r0, accs, edge):
            ms = [_win(ac_ref, r0, -(K - 1 - j), RC, T, RC, edge) * _win(ax_ref, r0, -(K - 1 - j), RC, T, RC, edge)
                  for j in range(K)]
            c = ms[0] * w_ref[0:1, :]
            for j in range(1, K):
                c = c + ms[j] * w_ref[j:j + 1, :]
            dy = dy_ref[pl.ds(r0, RC), :]
            dab_ref[pl.ds(r0, RC), :] = (dy * c).astype(dab_ref.dtype)
            dc2 = _win(dy_ref, r0, 0, RC + 8, T, RC, edge) * _win(ab_ref, r0, 0, RC + 8, T, RC, edge)
            dm = _untaps(scr_ref, dc2, w_ref, K, RC)
            dax_ref[pl.ds(r0, RC), :] = (dm * ac_ref[pl.ds(r0, RC), :]).astype(dax_ref.dtype)
            dac_ref[pl.ds(r0, RC), :] = (dm * ax_ref[pl.ds(r0, RC), :]).astype(dac_ref.dtype)
            return tuple(accs[j] + _fold8(dc2[:RC] * ms[j]) for j in range(K))

        accs = _peeled(T // RC, RC, step, tuple(jnp.zeros((8, LANE), F32) for _ in range(K)))
        for j in range(K):
            dw_ref[j:j + 1, :] = jnp.sum(accs[j], axis=0, keepdims=True)

    col = pl.BlockSpec((T, LANE), lambda j: (0, j))
    wsp = pl.BlockSpec((K, LANE), lambda j: (0, j))
    return pl.pallas_call(
        body, name=name, grid=(nb,),
        in_specs=_col_specs(T, (0, nb, 2 * nb)) + [wsp, col],
        out_specs=[col, col, col, wsp],
        out_shape=[jax.ShapeDtypeStruct((T, CW), BF16)] * 3 + [jax.ShapeDtypeStruct((K, CW), F32)],
        scratch_shapes=[pltpu.VMEM((RC + 8, LANE), F32)],
        compiler_params=_params(("parallel",)),
    )(proj, proj, proj, conv_w, dycat)


def _qkv_fwd(proj, conv_w, off, H, *, name):
    T = proj.shape[0]
    RC = _tile(T, ROWS_QKV_FWD, 8)
    nb = 3 * H
    K = conv_w.shape[0]

    def body(x_ref, w_ref, y_ref):
        j = pl.program_id(0)
        is_qk = j < 2 * H
        scale = jnp.where(j < H, HEAD ** -0.5, 1.0).astype(F32)

        def step(r0, carry, edge):
            s = _silu(_taps(x_ref, w_ref, K, r0, RC, T, RC, edge)[1])
            r = lax.rsqrt(jnp.sum(s * s, axis=-1, keepdims=True) + EPS) * scale
            y_ref[pl.ds(r0, RC), :] = s * jnp.where(is_qk, r, 1.0)
            return carry
        _peeled(T // RC, RC, step, 0)

    return pl.pallas_call(
        body, name=name, grid=(nb,),
        in_specs=_col_specs(T, (off,)) + [pl.BlockSpec((K, LANE), lambda j: (0, j))],
        out_specs=pl.BlockSpec((T, LANE), lambda j: (0, j)),
        out_shape=jax.ShapeDtypeStruct((T, nb * LANE), F32), compiler_params=_params(("parallel",)),
    )(proj, conv_w)


def _qkv_bwd(proj, conv_w, dq, dk, dv, off, H, *, name):
    T = proj.shape[0]
    RC = _tile(T, ROWS_QKV_BWD, 8)
    nb = 3 * H
    K = conv_w.shape[0]

    def body(x_ref, w_ref, dq_ref, dk_ref, dv_ref, dx_ref, dw_ref, scr_ref):
        j = pl.program_id(0)
        is_qk = j < 2 * H
        scale = jnp.where(j < H, HEAD ** -0.5, 1.0).astype(F32)

        def step(r0, accs, edge):
            xs, c2 = _taps(x_ref, w_ref, K, r0, RC + 8, T, RC, edge)
            s2 = _silu(c2)
            dn2 = jnp.where(j < H, _win(dq_ref, r0, 0, RC + 8, T, RC, edge),
                            jnp.where(is_qk, _win(dk_ref, r0, 0, RC + 8, T, RC, edge),
                                      _win(dv_ref, r0, 0, RC + 8, T, RC, edge)))
            r = lax.rsqrt(jnp.sum(s2 * s2, axis=-1, keepdims=True) + EPS)
            nh = s2 * r
            dnp = dn2 * scale
            ds_qk = r * (dnp - nh * jnp.sum(dnp * nh, axis=-1, keepdims=True))
            ds2 = jnp.where(is_qk, ds_qk, dn2)
            dc2 = ds2 * _dsilu(c2)
            dx_ref[pl.ds(r0, RC), :] = _untaps(scr_ref, dc2, w_ref, K, RC).astype(dx_ref.dtype)
            return tuple(accs[jj] + _fold8(dc2[:RC] * xs[jj][:RC]) for jj in range(K))

        accs = _peeled(T // RC, RC, step, tuple(jnp.zeros((8, LANE), F32) for _ in range(K)))
        for jj in range(K):
            dw_ref[jj:jj + 1, :] = jnp.sum(accs[jj], axis=0, keepdims=True)

    col = pl.BlockSpec((T, LANE), lambda j: (0, j))
    wsp = pl.BlockSpec((K, LANE), lambda j: (0, j))
    return pl.pallas_call(
        body, name=name, grid=(nb,),
        in_specs=_col_specs(T, (off,)) + [wsp] + [
            pl.BlockSpec((T, LANE), functools.partial(lambda o, j: (0, jnp.clip(j - o, 0, H - 1)), o)) for o in (0, H, 2 * H)],
        out_specs=[col, wsp],
        out_shape=[jax.ShapeDtypeStruct((T, nb * LANE), BF16), jax.ShapeDtypeStruct((K, nb * LANE), F32)],
        scratch_shapes=[pltpu.VMEM((RC + 8, LANE), F32)],
        compiler_params=_params(("parallel",)),
    )(proj, conv_w, dq, dk, dv)


def _softplus(x):
    return jnp.maximum(x, 0.0) + jnp.log(1.0 + jnp.exp(-jnp.abs(x)))


def _gates_fwd(proj, alog, dtb, off, H, *, name):
    T = proj.shape[0]
    tr = _tile(T, 512, CHUNK)

    def body(ab_ref, al_ref, dt_ref, gam_ref, beta_ref):
        ab = ab_ref[...]
        lane = lax.broadcasted_iota(jnp.int32, ab.shape, 1)
        g = -jnp.exp(al_ref[...]) * _softplus(ab + dt_ref[...])
        gb = jnp.where(lane < H, g, jnp.where(lane < 2 * H, jax.nn.sigmoid(ab), 0.0))
        tril = _tri().astype(F32)
        gam = jnp.concatenate([_mm(tril, gb[c * CHUNK:(c + 1) * CHUNK, :], precision=lax.Precision.HIGHEST)
                               for c in range(tr // CHUNK)], axis=0)
        for h in range(H):
            gam_ref[h] = jnp.broadcast_to(gam[:, h:h + 1], (tr, LANE))
            beta_ref[h] = jnp.broadcast_to(gb[:, H + h:H + h + 1], (tr, LANE))

    vec = pl.BlockSpec((1, LANE), lambda i: (0, 0))
    heads = pl.BlockSpec((H, tr, LANE), lambda i: (0, i, 0))
    return pl.pallas_call(
        body, name=name, grid=(T // tr,),
        in_specs=[pl.BlockSpec((tr, LANE), lambda i: (i, off)), vec, vec],
        out_specs=[heads, heads],
        out_shape=[jax.ShapeDtypeStruct((H, T, LANE), F32)] * 2, compiler_params=_params(("parallel",)),
    )(proj, alog, dtb)


def _gates_bwd(proj, alog, dtb, dgamB, dbB, off, H, *, name):
    T = proj.shape[0]
    tr = _tile(T, 512, CHUNK)

    def body(ab_ref, al_ref, dt_ref, dgam_ref, dbeta_ref, dab_ref, dal_ref, ddt_ref):
        ab = ab_ref[...]
        lane = lax.broadcasted_iota(jnp.int32, ab.shape, 1)
        is_g = lane < H
        d = jnp.zeros_like(ab)
        for h in range(H):
            d = jnp.where(lane == h, dgam_ref[h], jnp.where(lane == H + h, dbeta_ref[h], d))
        triu = _tri(upper=True).astype(F32)
        dg = jnp.concatenate([_mm(triu, d[c * CHUNK:(c + 1) * CHUNK, :], precision=lax.Precision.HIGHEST)
                              for c in range(tr // CHUNK)], axis=0)
        z = ab + dt_ref[...]
        A = -jnp.exp(al_ref[...])
        da = dg * A * jax.nn.sigmoid(z)
        beta = jax.nn.sigmoid(ab)
        db = d * beta * (1.0 - beta)
        dab_ref[...] = jnp.where(is_g, da, jnp.where(lane < 2 * H, db, 0.0)).astype(dab_ref.dtype)

        @pl.when(pl.program_id(0) == 0)
        def _():
            dal_ref[...] = jnp.zeros_like(dal_ref)
            ddt_ref[...] = jnp.zeros_like(ddt_ref)

        dal_ref[...] += jnp.sum(jnp.where(is_g, dg * A * _softplus(z), 0.0), axis=0, keepdims=True)
        ddt_ref[...] += jnp.sum(jnp.where(is_g, da, 0.0), axis=0, keepdims=True)

    vec = pl.BlockSpec((1, LANE), lambda i: (0, 0))
    row = pl.BlockSpec((tr, LANE), lambda i: (i, 0))
    heads = pl.BlockSpec((H, tr, LANE), lambda i: (0, i, 0))
    return pl.pallas_call(
        body, name=name, grid=(T // tr,),
        in_specs=[pl.BlockSpec((tr, LANE), lambda i: (i, off)), vec, vec, heads, heads],
        out_specs=[row, vec, vec],
        out_shape=[jax.ShapeDtypeStruct((T, LANE), BF16), jax.ShapeDtypeStruct((1, LANE), F32),
                   jax.ShapeDtypeStruct((1, LANE), F32)],
        compiler_params=_params(("arbitrary",)),
    )(proj, alog, dtb, dgamB, dbB)


def _gated_norm_fwd(o, proj, gn, zoff, ycat, *, name):
    T, W = o.shape
    tr = _tile(T, 256, 8)
    nh_, zblk = W // LANE, (zoff * LANE) // W
    assert zblk * W == zoff * LANE

    def body(o_ref, z_ref, g_ref, ycat_ref, y_ref):
        for h in range(nh_):
            ln = slice(h * LANE, (h + 1) * LANE)
            ov = o_ref[:, ln]
            r = lax.rsqrt(jnp.mean(ov * ov, axis=-1, keepdims=True) + EPS)
            y_ref[:, ln] = (ov * r * g_ref[...] * _silu(z_ref[:, ln])).astype(y_ref.dtype)

    assert ycat.shape == (T, 2 * W), ycat.shape
    blk = pl.BlockSpec((tr, W), lambda i: (i, 0))
    return pl.pallas_call(
        body, name=name, grid=(T // tr,),
        in_specs=[blk, pl.BlockSpec((tr, W), lambda i: (i, zblk)), pl.BlockSpec((1, LANE), lambda i: (0, 0)),
                  pl.BlockSpec(memory_space=pl.ANY)],
        out_specs=pl.BlockSpec((tr, W), lambda i: (i, 1)), out_shape=jax.ShapeDtypeStruct(ycat.shape, ycat.dtype),
        input_output_aliases={3: 0}, compiler_params=_params(("parallel",)),
    )(o, proj, gn, ycat)


def _gated_norm_bwd(o, proj, gn, dycat, zoff, yoff, *, name):
    T, W = o.shape
    tr = _tile(T, 256, 8)
    nh_, zblk, yblk = W // LANE, (zoff * LANE) // W, (yoff * LANE) // W
    assert zblk * W == zoff * LANE and yblk * W == yoff * LANE

    def body(o_ref, z_ref, g_ref, dy_ref, do_ref, dz_ref, dg_ref):
        @pl.when(pl.program_id(0) == 0)
        def _():
            dg_ref[...] = jnp.zeros_like(dg_ref)

        gv = g_ref[...]
        dg = jnp.zeros_like(gv)
        for h in range(nh_):
            ln = slice(h * LANE, (h + 1) * LANE)
            ov, zv, dy = o_ref[:, ln], z_ref[:, ln], dy_ref[:, ln]
            r = lax.rsqrt(jnp.mean(ov * ov, axis=-1, keepdims=True) + EPS)
            nh = ov * r
            s = _silu(zv)
            dg = dg + jnp.sum(dy * nh * s, axis=0, keepdims=True)
            dz_ref[:, ln] = (dy * nh * gv * _dsilu(zv)).astype(dz_ref.dtype)
            dn = dy * gv * s
            do_ref[:, ln] = r * (dn - nh * jnp.mean(dn * nh, axis=-1, keepdims=True))
        dg_ref[...] += dg

    blk = pl.BlockSpec((tr, W), lambda i: (i, 0))
    vec = pl.BlockSpec((1, LANE), lambda i: (0, 0))
    return pl.pallas_call(
        body, name=name, grid=(T // tr,),
        in_specs=[blk, pl.BlockSpec((tr, W), lambda i: (i, zblk)), vec, pl.BlockSpec((tr, W), lambda i: (i, yblk))],
        out_specs=[blk, blk, vec],
        out_shape=[jax.ShapeDtypeStruct((T, W), F32), jax.ShapeDtypeStruct((T, W), BF16),
                   jax.ShapeDtypeStruct((1, LANE), F32)],
        compiler_params=_params(("arbitrary",)),
    )(o, proj, gn, dycat)


def _ffn_act_fwd(up_pre, conv_w, *, name):
    T, F2 = up_pre.shape
    RC = _tile(T, ROWS_FFN_FWD, 8)
    nb = F2 // 2 // LANE
    K = conv_w.shape[0]

    def body(g_ref, v_ref, wg_ref, wv_ref, y_ref):
        def step(r0, carry, edge):
            _, gate = _taps(g_ref, wg_ref, K, r0, RC, T, RC, edge)
            _, val = _taps(v_ref, wv_ref, K, r0, RC, T, RC, edge)
            y_ref[pl.ds(r0, RC), :] = (_silu(gate) * val).astype(y_ref.dtype)
            return carry
        _peeled(T // RC, RC, step, 0)

    return pl.pallas_call(
        body, name=name, grid=(nb,),
        in_specs=_col_specs(T, (0, nb)) + [pl.BlockSpec((K, LANE), lambda j: (0, j)),
                                           pl.BlockSpec((K, LANE), lambda j: (0, nb + j))],
        out_specs=pl.BlockSpec((T, LANE), lambda j: (0, j)),
        out_shape=jax.ShapeDtypeStruct((T, F2 // 2), BF16), compiler_params=_params(("parallel",)),
    )(up_pre, up_pre, conv_w, conv_w)


def _ffn_act_bwd(up_pre, conv_w, dact, *, name):
    T, F2 = up_pre.shape
    RC = _tile(T, ROWS_FFN_BWD, 8)
    nb = F2 // 2 // LANE
    K = conv_w.shape[0]

    def body(g_ref, v_ref, wg_ref, wv_ref, da_ref, d_ref, dwg_ref, dwv_ref, sg_ref, sv_ref):
        def step(r0, accs, edge):
            gs, gate2 = _taps(g_ref, wg_ref, K, r0, RC + 8, T, RC, edge)
            vs, val2 = _taps(v_ref, wv_ref, K, r0, RC + 8, T, RC, edge)
            da2 = _win(da_ref, r0, 0, RC + 8, T, RC, edge)
            dgate2 = da2 * val2 * _dsilu(gate2)
            dval2 = da2 * _silu(gate2)
            d_ref[0, pl.ds(r0, RC), :] = _untaps(sg_ref, dgate2, wg_ref, K, RC).astype(d_ref.dtype)
            d_ref[1, pl.ds(r0, RC), :] = _untaps(sv_ref, dval2, wv_ref, K, RC).astype(d_ref.dtype)
            new = []
            for j in range(K):
                new.append(accs[2 * j] + _fold8(dgate2[:RC] * gs[j][:RC]))
                new.append(accs[2 * j + 1] + _fold8(dval2[:RC] * vs[j][:RC]))
            return tuple(new)

        accs = _peeled(T // RC, RC, step, tuple(jnp.zeros((8, LANE), F32) for _ in range(2 * K)))
        for j in range(K):
            dwg_ref[j:j + 1, :] = jnp.sum(accs[2 * j], axis=0, keepdims=True)
            dwv_ref[j:j + 1, :] = jnp.sum(accs[2 * j + 1], axis=0, keepdims=True)

    col = pl.BlockSpec((T, LANE), lambda j: (0, j))
    wsp = pl.BlockSpec((K, LANE), lambda j: (0, j))
    return pl.pallas_call(
        body, name=name, grid=(nb,),
        in_specs=_col_specs(T, (0, nb)) + [wsp, pl.BlockSpec((K, LANE), lambda j: (0, nb + j)), col],
        out_specs=[pl.BlockSpec((2, T, LANE), lambda j: (0, 0, j)), wsp, wsp],
        out_shape=[jax.ShapeDtypeStruct((2, T, F2 // 2), BF16)] + [jax.ShapeDtypeStruct((K, F2 // 2), F32)] * 2,
        scratch_shapes=[pltpu.VMEM((RC + 8, LANE), F32)] * 2,
        compiler_params=_params(("parallel",)),
    )(up_pre, up_pre, conv_w, conv_w, dact)


CPB = 8
CPB_SCAN = 4
GRP = 8
HP = lax.Precision.HIGH


def _tri(strict=False, upper=False):
    r = lax.broadcasted_iota(jnp.int32, (CHUNK, CHUNK), 0)
    c = lax.broadcasted_iota(jnp.int32, (CHUNK, CHUNK), 1)
    if upper:
        return c >= r
    return (r > c) if strict else (r >= c)


def _mm(a, b, dn="nn", precision=None):
    precision = HP if precision is None else precision
    return lax.dot_general(a, b, _DN[dn], precision=precision, preferred_element_type=F32)


def _mm16(a, b, dn="nn"):
    return lax.dot_general(a.astype(BF16), b.astype(BF16), _DN[dn], preferred_element_type=F32)


def _each(f, *cols):
    return [f(*xs) for xs in zip(*cols)]


def _decay(gam):
    return jnp.exp(jnp.where(_tri(), gam[:, :CHUNK] - gam.T[:CHUNK, :], -1e30))


def _delta_specs(T, H, cpb):
    rows = cpb * CHUNK
    col = lambda o: pl.BlockSpec((rows, LANE), functools.partial(lambda o, h, n: (n, o + h), o))
    bc = pl.BlockSpec((1, rows, LANE), lambda h, n: (h, n, 0))
    sq = pl.BlockSpec((1, cpb, CHUNK, CHUNK), lambda h, n: (h, n, 0, 0))
    vec = pl.BlockSpec((1, cpb, 1, LANE), lambda h, n: (h, n, 0, 0))
    return col, bc, sq, vec


def _delta_prep_fwd(qkv, gamB, bB, H, *, name):
    T = qkv.shape[0]
    N = T // CHUNK
    cpb = _tile(N, CPB, 8)
    grp = min(GRP, cpb)
    col, bc, sq, vec = _delta_specs(T, H, cpb)

    def body(q_ref, k_ref, v_ref, g_ref, b_ref, u_ref, w_ref, qd_ref, kd_ref, qk_ref, ti_ref, gl_ref):
        eye = (lax.broadcasted_iota(jnp.int32, (CHUNK, CHUNK), 0) == lax.broadcasted_iota(jnp.int32, (CHUNK, CHUNK), 1)).astype(F32)
        strict = _tri(strict=True)
        for c0 in range(0, cpb, grp):
            cs = list(range(c0, c0 + grp))
            rows = [slice(c * CHUNK, (c + 1) * CHUNK) for c in cs]
            q, k, v = ([r_[r, :] for r in rows] for r_ in (q_ref, k_ref, v_ref))
            bb = [b_ref[0, r, :] for r in rows]
            gam = [g_ref[0, r, :] for r in rows]
            D = _each(_decay, gam)
            e = _each(jnp.exp, gam)
            kk = _each(lambda k_: _mm16(k_, k_, "nt"), k)
            X = _each(lambda kk_, D_, b_: -(jnp.where(strict, kk_ * D_, 0.0) * b_[:, :CHUNK]), kk, D, bb)
            R = _each(lambda x: eye + x, X)
            for _ in range(5):
                X = _each(lambda x: _mm(x, x), X)
                R = _each(lambda r, x: r + _mm(r, x), R, X)
            u = _each(lambda r, b_, v_: _mm(r, b_ * v_), R, bb, v)
            w = _each(lambda r, b_, e_, k_: _mm(r, b_ * e_ * k_), R, bb, e, k)
            qk = _each(lambda q_, k_, D_: _mm16(q_, k_, "nt") * D_, q, k, D)
            for i, c in enumerate(cs):
                glast = gam[i][CHUNK - 1:CHUNK, :]
                u_ref[rows[i], :] = u[i]
                w_ref[rows[i], :] = w[i]
                qd_ref[rows[i], :] = e[i] * q[i]
                kd_ref[rows[i], :] = jnp.exp(glast - gam[i]) * k[i]
                qk_ref[0, c] = qk[i]
                ti_ref[0, c] = R[i]
                gl_ref[0, c] = jnp.exp(glast)

    full = jax.ShapeDtypeStruct((T, H * LANE), F32)
    sqs = jax.ShapeDtypeStruct((H, N, CHUNK, CHUNK), F32)
    return pl.pallas_call(
        body, name=name, grid=(H, N // cpb),
        in_specs=[col(0), col(H), col(2 * H), bc, bc],
        out_specs=[col(0)] * 4 + [sq, sq, vec],
        out_shape=[full] * 4 + [sqs, sqs, jax.ShapeDtypeStruct((H, N, 1, LANE), F32)],
        compiler_params=_params(("parallel", "parallel")),
    )(qkv, qkv, qkv, gamB, bB)


def _scan_specs(H, N, cpb, hb, rev):
    nbk = N // cpb
    blk = (lambda n: nbk - 1 - n) if rev else (lambda n: n)
    col = pl.BlockSpec((cpb * CHUNK, hb * LANE), lambda h, n: (blk(n), h))
    sq = pl.BlockSpec((hb, cpb, CHUNK, CHUNK), lambda h, n: (h, blk(n), 0, 0))
    vec = pl.BlockSpec((hb, cpb, 1, LANE), lambda h, n: (h, blk(n), 0, 0))
    st = pl.BlockSpec((hb, cpb, HEAD, HEAD), lambda h, n: (h, blk(n), 0, 0))
    return col, sq, vec, st


def _delta_scan_fwd(u, w, qd, kd, qk, gl, H, *, name):
    T = u.shape[0]
    N = T // CHUNK
    cpb = _tile(N, CPB_SCAN, 4)
    hb = min(GRP, H)
    col, sq, vec, st = _scan_specs(H, N, cpb, hb, False)
    lanes = [slice(j * LANE, (j + 1) * LANE) for j in range(hb)]
    heads = list(range(hb))

    def body(u_ref, w_ref, qd_ref, kd_ref, qk_ref, gl_ref, o_ref, vn_ref, ss_ref, s_scr):
        @pl.when(pl.program_id(1) == 0)
        def _():
            s_scr[...] = jnp.zeros_like(s_scr)

        def step(c, states):
            rows = pl.ds(pl.multiple_of(c * CHUNK, CHUNK), CHUNK)
            S = list(states)
            for j in heads:
                ss_ref[j, c] = S[j]
            wS = _each(lambda ln, s: _mm16(w_ref[rows, ln], s), lanes, S)
            qS = _each(lambda ln, s: _mm16(qd_ref[rows, ln], s), lanes, S)
            vn = _each(lambda ln, ws: u_ref[rows, ln] - ws, lanes, wS)
            o = _each(lambda j, qs, vn_: qs + _mm16(qk_ref[j, c], vn_), heads, qS, vn)
            new = _each(lambda j, ln, s, vn_: s * gl_ref[j, c] + _mm16(kd_ref[rows, ln], vn_, "tn"),
                        heads, lanes, S, vn)
            for j in heads:
                o_ref[rows, lanes[j]] = o[j]
                vn_ref[rows, lanes[j]] = vn[j]
            return tuple(new)
        out = lax.fori_loop(0, cpb, step, tuple(s_scr[j] for j in heads))
        for j in heads:
            s_scr[j] = out[j]

    full = jax.ShapeDtypeStruct((T, H * LANE), F32)
    return pl.pallas_call(
        body, name=name, grid=(H // hb, N // cpb),
        in_specs=[col] * 4 + [sq, vec],
        out_specs=[col, col, st],
        out_shape=[full, full, jax.ShapeDtypeStruct((H, N, HEAD, HEAD), F32)],
        scratch_shapes=[pltpu.VMEM((hb, HEAD, HEAD), F32)],
        compiler_params=_params(("parallel", "arbitrary")),
    )(u, w, qd, kd, qk, gl)


def _delta_scan_bwd(do, w, qd, kd, vn, qk, gl, ss, H, *, name):
    T = do.shape[0]
    N = T // CHUNK
    cpb = _tile(N, CPB_SCAN, 4)
    hb = min(GRP, H)
    col, sq, vec, st = _scan_specs(H, N, cpb, hb, True)
    lanes = [slice(j * LANE, (j + 1) * LANE) for j in range(hb)]
    heads = list(range(hb))

    def body(do_ref, w_ref, qd_ref, kd_ref, vn_ref, qk_ref, gl_ref, ss_ref,
             du_ref, dw_ref, dqd_ref, dkd_ref, dqk_ref, dgl_ref, ds_scr):
        @pl.when(pl.program_id(1) == 0)
        def _():
            ds_scr[...] = jnp.zeros_like(ds_scr)

        def step(i, dstates):
            c = cpb - 1 - i
            rows = pl.ds(pl.multiple_of(c * CHUNK, CHUNK), CHUNK)
            dS = list(dstates)
            S = [ss_ref[j, c] for j in heads]
            dov = [do_ref[rows, ln] for ln in lanes]
            vnv = [vn_ref[rows, ln] for ln in lanes]
            a1 = _each(lambda j, d_: _mm16(qk_ref[j, c], d_, "tn"), heads, dov)
            a2 = _each(lambda ln, ds: _mm16(kd_ref[rows, ln], ds), lanes, dS)
            dvn = _each(lambda x, y: x + y, a1, a2)
            dqd = _each(lambda d_, s: _mm16(d_, s, "nt"), dov, S)
            dkd = _each(lambda v_, ds: _mm16(v_, ds, "nt"), vnv, dS)
            dqk = _each(lambda d_, v_: _mm16(d_, v_, "nt"), dov, vnv)
            dw = _each(lambda dv_, s: -_mm16(dv_, s, "nt"), dvn, S)
            b1 = _each(lambda ln, d_: _mm16(qd_ref[rows, ln], d_, "tn"), lanes, dov)
            b2 = _each(lambda ln, dv_: _mm16(w_ref[rows, ln], dv_, "tn"), lanes, dvn)
            new = _each(lambda j, x, y, ds: x + ds * gl_ref[j, c] - y, heads, b1, b2, dS)
            for j in heads:
                du_ref[rows, lanes[j]] = dvn[j]
                dw_ref[rows, lanes[j]] = dw[j]
                dqd_ref[rows, lanes[j]] = dqd[j]
                dkd_ref[rows, lanes[j]] = dkd[j]
                dqk_ref[j, c] = dqk[j]
                dgl = jnp.sum(jnp.sum(dS[j] * S[j], axis=1, keepdims=True), axis=0, keepdims=True)
                dgl_ref[j, c] = jnp.broadcast_to(dgl, (1, LANE))
            return tuple(new)
        out = lax.fori_loop(0, cpb, step, tuple(ds_scr[j] for j in heads))
        for j in heads:
            ds_scr[j] = out[j]

    full = jax.ShapeDtypeStruct((T, H * LANE), F32)
    return pl.pallas_call(
        body, name=name, grid=(H // hb, N // cpb),
        in_specs=[col] * 5 + [sq, vec, st],
        out_specs=[col] * 4 + [sq, vec],
        out_shape=[full] * 4 + [jax.ShapeDtypeStruct((H, N, CHUNK, CHUNK), F32), jax.ShapeDtypeStruct((H, N, 1, LANE), F32)],
        scratch_shapes=[pltpu.VMEM((hb, HEAD, HEAD), F32)],
        compiler_params=_params(("parallel", "arbitrary")),
    )(do, w, qd, kd, vn, qk, gl, ss)


def _delta_prep_bwd(qkv, gamB, bB, ti, u, w, qk, du, dw, dqd, dkd, dqk, dgl, H, *, name):
    T = qkv.shape[0]
    N = T // CHUNK
    cpb = _tile(N, CPB, 8)
    grp = min(GRP, cpb)
    col, bc, sq, vec = _delta_specs(T, H, cpb)

    def body(q_ref, k_ref, v_ref, g_ref, b_ref, ti_ref, u_ref, w_ref, qk_ref,
             du_ref, dw_ref, dqd_ref, dkd_ref, dqk_ref, dgl_ref,
             dq_ref, dk_ref, dv_ref, dg_ref, db_ref):
        ones = jnp.ones((CHUNK, LANE), F32)
        strict = _tri(strict=True)
        last = lax.broadcasted_iota(jnp.int32, (CHUNK, LANE), 0) == CHUNK - 1
        lsum = lambda x: jnp.sum(x, axis=-1, keepdims=True)
        for c0 in range(0, cpb, grp):
            cs = list(range(c0, c0 + grp))
            rows = [slice(c * CHUNK, (c + 1) * CHUNK) for c in cs]
            ld = lambda r_: [r_[r, :] for r in rows]
            q, k, v, uv, wv, duv, dwv, dqd_v, dkd_v = (ld(r_) for r_ in (q_ref, k_ref, v_ref, u_ref, w_ref, du_ref, dw_ref, dqd_ref, dkd_ref))
            bb = [b_ref[0, r, :] for r in rows]
            gam = [g_ref[0, r, :] for r in rows]
            Ti = [ti_ref[0, c] for c in cs]
            QK = [qk_ref[0, c] for c in cs]
            dqk_v = [dqk_ref[0, c] for c in cs]
            D = _each(_decay, gam)
            e = _each(jnp.exp, gam)
            glast = [g_[CHUNK - 1:CHUNK, :] for g_ in gam]
            eL = _each(lambda gl_, g_: jnp.exp(gl_ - g_), glast, gam)
            kk = _each(lambda k_: _mm16(k_, k_, "nt"), k)
            KKD = _each(lambda kk_, D_: jnp.where(strict, kk_ * D_, 0.0), kk, D)
            dru = _each(lambda t, d_: _mm(t, d_, "tn"), Ti, duv)
            drw = _each(lambda t, d_: _mm(t, d_, "tn"), Ti, dwv)
            l1 = _each(lambda a, b: _mm(a, b, "nt"), dru, uv)
            l2 = _each(lambda a, b: _mm(a, b, "nt"), drw, wv)
            dL = _each(lambda a, b: jnp.where(strict, -(a + b), 0.0), l1, l2)
            Mm = _each(lambda dl, b_: dl * b_[:, :CHUNK], dL, bb)
            dKK = _each(lambda m_, D_: m_ * D_, Mm, D)
            dQK = _each(lambda a, D_: a * D_, dqk_v, D)
            P = _each(lambda m_, kkd, a, qk_: m_ * kkd + a * qk_, Mm, KKD, dqk_v, QK)
            q1 = _each(lambda a, k_: _mm16(a, k_), dQK, k)
            k1 = _each(lambda a, q_: _mm16(a, q_, "tn"), dQK, q)
            k2 = _each(lambda a, k_: _mm16(a, k_), dKK, k)
            k3 = _each(lambda a, k_: _mm16(a, k_, "tn"), dKK, k)
            s1 = _each(lambda dl, kkd: _mm(dl * kkd, ones), dL, KKD)
            p1 = _each(lambda p_: _mm(p_, ones), P)
            p2 = _each(lambda p_: _mm(p_, ones, "tn"), P)
            for i, c in enumerate(cs):
                r = rows[i]
                bek = bb[i] * e[i]
                kdv = eL[i] * k[i]
                dq_ref[r, :] = q1[i] + e[i] * dqd_v[i]
                dk_ref[r, :] = k1[i] + k2[i] + k3[i] + bek * drw[i] + eL[i] * dkd_v[i]
                dv_ref[r, :] = bb[i] * dru[i]
                db_ref[0, r, :] = s1[i] + lsum(dru[i] * v[i]) + lsum(drw[i] * e[i] * k[i])
                dgam = (p1[i] - p2[i] + lsum(drw[i] * bek * k[i]) + lsum(dqd_v[i] * e[i] * q[i])
                        - lsum(dkd_v[i] * kdv))
                xlast = jnp.sum(lsum(dkd_v[i] * kdv), axis=0, keepdims=True) + jnp.exp(glast[i]) * dgl_ref[0, c]
                dg_ref[0, r, :] = dgam + jnp.where(last, xlast, 0.0)

    full = jax.ShapeDtypeStruct((T, H * LANE), F32)
    bcs = jax.ShapeDtypeStruct((H, T, LANE), F32)
    return pl.pallas_call(
        body, name=name, grid=(H, N // cpb),
        in_specs=[col(0), col(H), col(2 * H), bc, bc, sq, col(0), col(0), sq, col(0), col(0), col(0), col(0), sq, vec],
        out_specs=[col(0), col(0), col(0), bc, bc],
        out_shape=[full, full, full, bcs, bcs],
        compiler_params=_params(("parallel", "parallel")),
    )(qkv, qkv, qkv, gamB, bB, ti, u, w, qk, du, dw, dqd, dkd, dqk, dgl)


def _adam(parts, w, m, v, *, name, own=None, me=None):
    P, R, C = parts.shape
    if R > 256 and R % 8:
        tr, tc = R, _tile(C, 256)
    else:
        tr, tc = _tile(R, 256, 8), C
    n_own = 0 if own is None else 2

    def body(*refs):
        p_ref, w_ref, m_ref, v_ref, g_ref, d_ref, nm_ref, nv_ref = refs[n_own:]
        g = None
        for i in range(P):
            t = p_ref[i].astype(F32)
            if n_own:
                t = jnp.where(refs[0][0] == i, refs[1][...].astype(F32), t)
            g = t if g is None else g + t
        mn = ADAM_B1 * m_ref[...] + (1.0 - ADAM_B1) * g
        vn = ADAM_B2 * v_ref[...] + (1.0 - ADAM_B2) * (g * g)
        m_hat = mn / (1.0 - ADAM_B1 ** ADAM_STEP)
        v_hat = vn / (1.0 - ADAM_B2 ** ADAM_STEP)
        g_ref[...] = g
        d_ref[...] = -ADAM_LR * (m_hat / (jnp.sqrt(v_hat) + ADAM_EPS) + ADAM_WD * w_ref[...])
        nm_ref[...] = mn
        nv_ref[...] = vn

    blk = pl.BlockSpec((tr, tc), lambda i, j: (i, j))
    return pl.pallas_call(
        body, name=name, grid=(R // tr, C // tc),
        in_specs=[pl.BlockSpec(memory_space=pltpu.SMEM), blk][:n_own] + [pl.BlockSpec((P, tr, tc), lambda i, j: (0, i, j)), blk, blk, blk],
        out_specs=[blk] * 4, out_shape=[jax.ShapeDtypeStruct((R, C), F32)] * 4,
        compiler_params=_params(("parallel", "parallel")),
    )(*([me, own] if n_own else []), parts, w, m, v)


def _mesh_pos():
    return lax.axis_index("x"), lax.axis_index("y"), lax.axis_index("c")


def _peer(k):
    x, y, c = _mesh_pos()
    px, py, pc = x ^ ((k >> 2) & 1), y ^ ((k >> 1) & 1), c ^ (k & 1)
    return (px, py, pc), 4 * px + 2 * py + pc


def _exchange(arrays, scatter, *, name, after=None):
    n = len(arrays)
    n_in = n if after is None else n + 1
    blocks = [a.shape[1:] if scatter else a.shape for a in arrays]

    def body(*refs):
        srcs, dsts = refs[:n], refs[n_in:n_in + n]
        send_sems, recv_sems, local_sems = refs[n_in + n:]
        x, y, c = _mesh_pos()
        me = 4 * x + 2 * y + c
        local, sends = [], []
        for a in range(n):
            cp = pltpu.make_async_copy(srcs[a].at[me] if scatter else srcs[a], dsts[a].at[me], local_sems.at[a])
            cp.start()
            local.append(cp)
            for k in range(1, N_DEV):
                dev, idx = _peer(k)
                cp = pltpu.make_async_remote_copy(
                    src_ref=srcs[a].at[idx] if scatter else srcs[a], dst_ref=dsts[a].at[me],
                    send_sem=send_sems.at[a * N_DEV + k], recv_sem=recv_sems.at[a * N_DEV + k],
                    device_id=dev, device_id_type=MESH)
                cp.start()
                sends.append(cp)
        for a in range(n):
            for k in range(1, N_DEV):
                dev, idx = _peer(k)
                pltpu.make_async_remote_copy(
                    src_ref=srcs[a].at[idx] if scatter else srcs[a], dst_ref=dsts[a].at[idx],
                    send_sem=send_sems.at[a * N_DEV + k], recv_sem=recv_sems.at[a * N_DEV + k],
                    device_id=dev, device_id_type=MESH).wait_recv()
        for cp in sends:
            cp.wait_send()
        for cp in local:
            cp.wait()

    anyspec = pl.BlockSpec(memory_space=pl.ANY)
    return pl.pallas_call(
        body, name=name, in_specs=[anyspec] * n_in, out_specs=[anyspec] * n,
        out_shape=[jax.ShapeDtypeStruct((N_DEV,) + tuple(b), a.dtype) for a, b in zip(arrays, blocks)],
        scratch_shapes=[pltpu.SemaphoreType.DMA((n * N_DEV,)), pltpu.SemaphoreType.DMA((n * N_DEV,)),
                        pltpu.SemaphoreType.DMA((n,))],
    )(*arrays, *([] if after is None else [after]))


_ANY = pl.BlockSpec(memory_space=pl.ANY)
_SEM = pl.BlockSpec(memory_space=pltpu.SEMAPHORE)
_EFFECT = pltpu.SideEffectType.DATAFLOW_SIDE_EFFECTING


def _in_hbm(a):
    return pltpu.with_memory_space_constraint(a, pltpu.HBM)


def _split_copy(src, land, send, recv, k, me, scatter, landed):
    dev, idx = _peer(k)
    return pltpu.make_async_remote_copy(
        src_ref=src.at[idx] if scatter else src, dst_ref=land.at[idx if landed else me],
        send_sem=send.at[k], recv_sem=recv.at[k], device_id=dev, device_id_type=MESH)


ALL_PEERS = tuple(range(1, N_DEV))
SIBLING = 1
SAME_CORE = (2, 4, 6)


def _split_start(srcs, lands, scatter, *, name, relations=None):
    n = len(srcs)
    relations = relations or [ALL_PEERS] * n

    def body(*refs):
        src, land, send, recv, token = refs[:n], refs[n:2 * n], refs[2 * n:3 * n], refs[3 * n:4 * n], refs[-1]
        x, y, c = _mesh_pos()
        me = 4 * x + 2 * y + c
        for a in range(n):
            for k in relations[a]:
                _split_copy(src[a], land[a], send[a], recv[a], k, me, scatter, False).start()
        token[...] = jnp.zeros_like(token)

    outs = pl.pallas_call(
        body, name=name,
        out_shape=[pltpu.SemaphoreType.DMA((N_DEV,))] * (2 * n) + [pltpu.HBM(t.shape, t.dtype) for t in list(srcs) + list(lands)]
        + [jax.ShapeDtypeStruct((8, LANE), F32)],
        in_specs=[_ANY] * (2 * n), out_specs=[_SEM] * (2 * n) + [_ANY] * (2 * n) + [pl.BlockSpec(memory_space=pltpu.VMEM)],
        input_output_aliases={i: 2 * n + i for i in range(2 * n)},
        compiler_params=pltpu.CompilerParams(has_side_effects=_EFFECT),
    )(*[_in_hbm(t) for t in list(srcs) + list(lands)])
    handles = [(outs[a], outs[n + a], outs[2 * n + a], outs[3 * n + a]) for a in range(n)]
    return handles, outs[-1]


def _split_wait(handle, after, scatter, *, name):
    send, recv, src_thru, land_thru = handle

    def body(src_ref, land_ref, send_ref, recv_ref, after_ref, src_out, land_out):
        x, y, c = _mesh_pos()
        me = 4 * x + 2 * y + c
        for k in range(1, N_DEV):
            cp = _split_copy(src_ref, land_ref, send_ref, recv_ref, k, me, scatter, True)
            cp.wait_send()
            cp.wait_recv()

    return pl.pallas_call(
        body, name=name,
        out_shape=(pltpu.HBM(src_thru.shape, src_thru.dtype), pltpu.HBM(land_thru.shape, land_thru.dtype)),
        in_specs=(_ANY, _ANY, _SEM, _SEM, _ANY), out_specs=(_ANY, _ANY), input_output_aliases={0: 0, 1: 1},
        compiler_params=pltpu.CompilerParams(has_side_effects=_EFFECT),
    )(src_thru, land_thru, send, recv, after)[1]


def _forward_copy(land, fsend, frecv, k, landed):
    x, y, c = _mesh_pos()
    _, idx = _peer(k | SIBLING if landed else k)
    return pltpu.make_async_remote_copy(src_ref=land.at[idx], dst_ref=land.at[idx], send_sem=fsend.at[k],
                                        recv_sem=frecv.at[k], device_id=(x, y, 1 - c), device_id_type=MESH)


def _gather_forward(handle, after, *, name):
    send, recv, src_thru, land_thru = handle

    def body(src_ref, land_ref, send_ref, recv_ref, after_ref, src_out, land_out, fsend, frecv):
        x, y, c = _mesh_pos()
        me = 4 * x + 2 * y + c
        for k in SAME_CORE:
            _split_copy(src_ref, land_ref, send_ref, recv_ref, k, me, False, True).wait_recv()
            _forward_copy(land_ref, fsend, frecv, k, False).start()

    src2, land2, fsend, frecv = pl.pallas_call(
        body, name=name,
        out_shape=(pltpu.HBM(src_thru.shape, src_thru.dtype), pltpu.HBM(land_thru.shape, land_thru.dtype),
                   pltpu.SemaphoreType.DMA((N_DEV,)), pltpu.SemaphoreType.DMA((N_DEV,))),
        in_specs=(_ANY, _ANY, _SEM, _SEM, _ANY), out_specs=(_ANY, _ANY, _SEM, _SEM), input_output_aliases={0: 0, 1: 1},
        compiler_params=pltpu.CompilerParams(has_side_effects=_EFFECT),
    )(src_thru, land_thru, send, recv, after)
    return (send, recv, src2, land2), (fsend, frecv)


def _gather_wait_two_level(handle, fwd, *, name):
    send, recv, src_thru, land_thru = handle
    fsend, frecv = fwd

    def body(src_ref, land_ref, send_ref, recv_ref, fsend_ref, frecv_ref, src_out, land_out):
        x, y, c = _mesh_pos()
        me = 4 * x + 2 * y + c
        for k in (SIBLING,) + SAME_CORE:
            _split_copy(src_ref, land_ref, send_ref, recv_ref, k, me, False, True).wait_send()
        _split_copy(src_ref, land_ref, send_ref, recv_ref, SIBLING, me, False, True).wait_recv()
        for k in SAME_CORE:
            _forward_copy(land_ref, fsend_ref, frecv_ref, k, False).wait_send()
            _forward_copy(land_ref, fsend_ref, frecv_ref, k, True).wait_recv()

    return pl.pallas_call(
        body, name=name,
        out_shape=(pltpu.HBM(src_thru.shape, src_thru.dtype), pltpu.HBM(land_thru.shape, land_thru.dtype)),
        in_specs=(_ANY, _ANY, _SEM, _SEM, _SEM, _SEM), out_specs=(_ANY, _ANY), input_output_aliases={0: 0, 1: 1},
        compiler_params=pltpu.CompilerParams(has_side_effects=_EFFECT),
    )(src_thru, land_thru, send, recv, fsend, frecv)[1]


def _local_step(x, p, tgt, S, wt, conv, emit):
    T, D = x.shape
    CW = DNW = D // 2
    H = DNW // HEAD
    nA, nD = CW // LANE, DNW // LANE
    qkv_off, z_off, ab_off = 3 * nA, 3 * nA + 3 * nD, 3 * nA + 4 * nD
    alog = jnp.pad(S["a_log"], ((0, 0), (0, LANE - H)))
    dtb = jnp.pad(S["dt_bias"], ((0, 0), (0, LANE - H)))

    h1 = _rms_fwd(x, S["g_mix"], name="rms1_fwd")
    pp = _matmul(p, wt("w_pp", h1), "nn", name="mm_pp", b_shards=True)
    w_in, cv = wt("w_in", pp), conv(pp)
    proj = _matmul(h1, w_in, "nt", name="mm_in")
    y_a = _group_a_fwd(proj, cv["conv_a"], CW, D, name="group_a_fwd")
    qkv = _qkv_fwd(proj, cv["conv_qkv"], qkv_off, H, name="qkv_fwd")
    gamB, bB = _gates_fwd(proj, alog, dtb, ab_off, H, name="gates_fwd")
    u, w, qd, kd, qk, ti, gl = _delta_prep_fwd(qkv, gamB, bB, H, name="delta_prep_fwd")
    o, vn, ss = _delta_scan_fwd(u, w, qd, kd, qk, gl, H, name="delta_scan_fwd")
    ycat = _gated_norm_fwd(o, proj, S["dn_g"], z_off, y_a, name="gated_norm_fwd")
    w_out = wt("w_out", ycat)
    rows = dict(tm=ROW_TILE, tn=D)
    x1, h2 = _matmul(ycat, w_out, "nn", name="mm_out", out_dtypes=(F32, BF16), epilogue=_epi_residual_rms,
                     extras=(x,), vec_extras=(S["g_ffn"],), **rows)
    w_up = wt("w_up", h2)
    up_pre = _matmul(h2, w_up, "nn", name="mm_up", b_shards=True, tn=SHARD_TILE)
    act = _ffn_act_fwd(up_pre, cv["conv_ffn"], name="ffn_act_fwd")
    w_down = wt("w_down", act)
    x2 = _matmul(act, w_down, "nn", name="mm_down", epilogue=lambda acc, r: (acc + r,), extras=(x1,), tk=LONG_K)
    h3 = _rms_fwd(x2, S["g_ple"], name="rms3_fwd")
    w_pg = wt("w_pg", h3)

    def ple_epi(acc, x2r, ppr):
        s = jax.nn.sigmoid(acc)
        return x2r + s * ppr, s

    x3, sg = _matmul(h3, w_pg, "nn", name="mm_pg", out_dtypes=(F32, F32), epilogue=ple_epi, extras=(x2, pp), tm=512)
    dx3, dg_final, loss, dpg, dpp = _final_loss(x3, S["g_final"], tgt, pp, sg, name="final_loss")

    G = {"g_final": dg_final}
    tok = emit({"w_pp": _matmul(p, dpp, "tn", name="mm_dwpp", out_dtypes=(BF16,), out_shards=True, tk=LONG_K),
                "w_pg": _matmul(h3, dpg, "tn", name="mm_dwpg", out_dtypes=(BF16,), tk=LONG_K)})
    bwd = dict(out_dtypes=(F32, BF16), epilogue=_epi_rms_bwd(2), n_vec=1, **rows)
    dx2, dx2b, G["g_ple"] = _matmul(dpg, w_pg, "nt", name="mm_dh3", after=tok, extras=(x2, dx3),
                                    vec_extras=(S["g_ple"],), **bwd)
    tok = emit({"w_down": _matmul(act, dx2b, "tn", name="mm_dwdown", out_dtypes=(BF16,), tk=LONG_K)})
    dact = _matmul(dx2b, w_down, "nt", name="mm_dact", after=tok, tn=SHARD_TILE)
    dup, dcf_g, dcf_v = _ffn_act_bwd(up_pre, cv["conv_ffn"], dact, name="ffn_act_bwd")
    G["conv_ffn"] = jnp.concatenate([dcf_g, dcf_v], axis=1)
    tok = emit({"w_up": _matmul(h2, dup, "tn", name="mm_dwup", out_dtypes=(BF16,), b_shards=True, out_shards=True,
                                tn=SHARD_TILE, tk=LONG_K)})
    dh2 = _matmul(dup, w_up, "nt", name="mm_dh2", after=tok, a_shards=True, b_shards=True, tk=2 * SHARD_TILE)
    dx1, dx1b, G["g_ffn"] = _rms_bwd(x1, S["g_ffn"], dh2, dx2, name="rms2_bwd")
    tok = emit({"w_out": _matmul(ycat, dx1b, "tn", name="mm_dwout", out_dtypes=(BF16,), tk=LONG_K)})
    dycat = _matmul(dx1b, w_out, "nt", name="mm_dycat", after=tok)
    do, dz, G["dn_g"] = _gated_norm_bwd(o, proj, S["dn_g"], dycat, z_off, nA, name="gated_norm_bwd")
    du, dw, dqd, dkd, dqk, dgl = _delta_scan_bwd(do, w, qd, kd, vn, qk, gl, ss, H, name="delta_scan_bwd")
    dq, dk, dv, dgB, dbB = _delta_prep_bwd(qkv, gamB, bB, ti, u, w, qk, du, dw, dqd, dkd, dqk, dgl, H,
                                           name="delta_prep_bwd")
    dab, dal, ddt = _gates_bwd(proj, alog, dtb, dgB, dbB, ab_off, H, name="gates_bwd")
    G["a_log"], G["dt_bias"] = dal[:, :H], ddt[:, :H]
    dqkv, G["conv_qkv"] = _qkv_bwd(proj, cv["conv_qkv"], dq, dk, dv, qkv_off, H, name="qkv_bwd")
    dax, dab_, dac, G["conv_a"] = _group_a_bwd(proj, cv["conv_a"], dycat, CW, name="group_a_bwd")
    in_p = w_in.shape[0]
    dproj = jnp.concatenate([dax, dab_, dac, dqkv, dz, dab, jnp.zeros((T, in_p - (ab_off + 1) * LANE), BF16)], axis=1)
    tok = emit({"w_in": _matmul(dproj, h1, "tn", name="mm_dwin", out_dtypes=(BF16,), tk=LONG_K)})
    dh1 = _matmul(dproj, w_in, "nn", name="mm_dh1", after=tok, tk=LONG_K)
    grad_x, _, G["g_mix"] = _rms_bwd(x, S["g_mix"], dh1, dx1, name="rms1_bwd")
    return loss, grad_x, G


def _col_sharded(landed):
    _, R, C = landed.shape
    return jnp.transpose(landed, (1, 0, 2)).reshape(R, N_DEV * C)


def kernel(x, p, norm_mix_g, w_in, conv_a_w, conv_qkv_w, a_log, dt_bias, dn_norm_g, w_out, norm_ffn_g, w_up, conv_ffn_w, w_down, norm_ple_g, w_ple_gate, w_ple_proj, final_norm_g, loss_target, m_norm_mix_g, m_w_in, m_conv_a_w, m_conv_qkv_w, m_a_log, m_dt_bias, m_dn_norm_g, m_w_out, m_norm_ffn_g, m_w_up, m_conv_ffn_w, m_w_down, m_norm_ple_g, m_w_ple_gate, m_w_ple_proj, m_final_norm_g, v_norm_mix_g, v_w_in, v_conv_a_w, v_conv_qkv_w, v_a_log, v_dt_bias, v_dn_norm_g, v_w_out, v_norm_ffn_g, v_w_up, v_conv_ffn_w, v_w_down, v_norm_ple_g, v_w_ple_gate, v_w_ple_proj, v_final_norm_g):
    T, D = x.shape[1], x.shape[2]
    xd, _, cd = _mesh_pos()
    me = 4 * xd + 2 * lax.axis_index("y") + cd

    conv_sh = [conv_a_w[0], conv_qkv_w[0], conv_ffn_w[0]]
    conv_n = [c.size for c in conv_sh]
    pack_rows = -(-sum(conv_n) // LANE)
    conv_pack = jnp.pad(jnp.concatenate([c.reshape(-1) for c in conv_sh]), (0, pack_rows * LANE - sum(conv_n))).reshape(pack_rows, LANE)
    names = ["w_pp", "w_in", "conv", "w_out", "w_up", "w_down", "w_pg"]
    tr_ = lambda t: jnp.swapaxes(t, 1, 2)
    shards = [w_ple_proj[0].astype(BF16), w_in[0].T.astype(BF16), conv_pack, w_out[0].astype(BF16), w_up[0].astype(BF16),
              w_down[0].astype(BF16), w_ple_gate[0].astype(BF16)]
    empty_slots = lambda blocks: [lax.empty((N_DEV,) + tuple(b.shape), b.dtype) for b in blocks]
    handles, tok0 = _split_start(shards, empty_slots(shards), False, name="gather_start",
                                 relations=[(SIBLING,) + SAME_CORE if nm == "w_in" else ALL_PEERS for nm in names])
    handle = dict(zip(names, handles))
    own = dict(zip(names, shards))
    in_cols = N_DEV * w_in.shape[2]
    in_p = (in_cols // LANE) * LANE + AB_PAD
    in_place = {"w_up", "w_pp"}

    def gathered(name, after):
        if name == "w_in":
            passed, fwd = _gather_forward(handle[name], after, name="gather_forward_w_in")
            landed = _gather_wait_two_level(passed, fwd, name="gather_wait_w_in")
        else:
            landed = _split_wait(handle[name], after, False, name="gather_wait_" + name)
        return lax.dynamic_update_index_in_dim(landed, own[name], me, 0)

    def wt(name, after):
        landed = gathered(name, after)
        if name in in_place:
            return landed
        full = landed.reshape(-1, D)
        return jnp.pad(full, ((0, in_p - in_cols), (0, 0))) if name == "w_in" else full

    def conv(after):
        flat = gathered("conv", after).reshape(N_DEV, pack_rows * LANE)
        out, o_ = {}, 0
        for nm, c, n_ in zip(("conv_a", "conv_qkv", "conv_ffn"), conv_sh, conv_n):
            out[nm] = _col_sharded(flat[:, o_:o_ + n_].reshape((N_DEV,) + c.shape))
            o_ += n_
        return out

    pending, mine = {}, {}

    def emit(grads):
        parts = [g if nm in in_place else (g[:in_cols] if nm == "w_in" else g).reshape(N_DEV, -1, D)
                 for nm, g in grads.items()]
        hs, tok = _split_start(parts, empty_slots([q[0] for q in parts]), True, name="scatter_start_" + "_".join(grads))
        pending.update(zip(grads, hs))
        mine.update({nm: lax.dynamic_index_in_dim(q, me, 0, keepdims=False) for nm, q in zip(grads, parts)})
        return tok

    S = {
        "g_mix": norm_mix_g + tok0[0, 0], "a_log": a_log, "dt_bias": dt_bias, "dn_g": dn_norm_g, "g_ffn": norm_ffn_g,
        "g_ple": norm_ple_g, "g_final": final_norm_g.reshape(1, D),
    }

    loss_v, grad_x, G = _local_step(x[0], p[0, 0], loss_target[0], S, wt, conv, emit)
    loss = lax.psum(loss_v[0, 0], ("x", "y", "c"))

    small_names = ["g_mix", "g_ffn", "g_ple", "g_final", "dn_g", "a_log", "dt_bias", "conv_a", "conv_qkv", "conv_ffn"]
    small_rows, pieces = [], []
    for nm in small_names:
        g_ = G[nm].reshape(-1)
        r_ = -(-g_.size // (8 * LANE)) * 8
        small_rows.append(r_)
        pieces.append(jnp.pad(g_, (0, r_ * LANE - g_.size)).reshape(r_, LANE))
    landed = {nm: _split_wait(h_, grad_x, True, name="scatter_wait_" + nm) for nm, h_ in pending.items() if nm != "w_in"}

    def adam(parts, w_, m_, v_, nm, own_=None):
        shp = w_.shape
        w2, m2, v2 = (t.reshape(parts.shape[1:]) for t in (w_, m_, v_))
        kw = {} if own_ is None else {"own": own_, "me": me.astype(jnp.int32).reshape(1)}
        return tuple(t.reshape(shp) for t in _adam(parts, w2, m2, v2, name="adam_" + nm, **kw))

    big = {
        "w_up": adam(landed["w_up"], w_up, m_w_up, v_w_up, "w_up", mine["w_up"]),
        "w_down": adam(landed["w_down"], w_down, m_w_down, v_w_down, "w_down", mine["w_down"]),
        "w_out": adam(landed["w_out"], w_out, m_w_out, v_w_out, "w_out", mine["w_out"]),
        "w_pg": adam(landed["w_pg"], w_ple_gate, m_w_ple_gate, v_w_ple_gate, "w_ple_gate", mine["w_pg"]),
        "w_pp": adam(landed["w_pp"], w_ple_proj, m_w_ple_proj, v_w_ple_proj, "w_ple_proj", mine["w_pp"]),
    }
    first = lambda t: lax.slice(t, (0,) * t.ndim, (1,) * t.ndim).reshape(1)
    big_done = sum(first(r[1]) for r in big.values())
    (small_l,) = _exchange([jnp.concatenate(pieces, axis=0)], False, name="gather_small_grads", after=big_done)

    def small_parts(nm):
        i = small_names.index(nm)
        r0 = sum(small_rows[:i])
        shp = G[nm].shape
        return small_l[:, r0:r0 + small_rows[i], :].reshape(N_DEV, -1)[:, :G[nm].size].reshape((N_DEV,) + shp)

    def conv_parts(nm, shard):
        full = small_parts(nm)
        C = shard.shape[-1]
        return lax.dynamic_slice_in_dim(full, me * C, C, axis=2)

    res = [
        adam(small_parts("g_mix"), norm_mix_g, m_norm_mix_g, v_norm_mix_g, "norm_mix_g"),
        None,
        adam(conv_parts("conv_a", conv_a_w), conv_a_w, m_conv_a_w, v_conv_a_w, "conv_a_w"),
        adam(conv_parts("conv_qkv", conv_qkv_w), conv_qkv_w, m_conv_qkv_w, v_conv_qkv_w, "conv_qkv_w"),
        adam(small_parts("a_log"), a_log, m_a_log, v_a_log, "a_log"),
        adam(small_parts("dt_bias"), dt_bias, m_dt_bias, v_dt_bias, "dt_bias"),
        adam(small_parts("dn_g"), dn_norm_g, m_dn_norm_g, v_dn_norm_g, "dn_norm_g"),
        big["w_out"],
        adam(small_parts("g_ffn"), norm_ffn_g, m_norm_ffn_g, v_norm_ffn_g, "norm_ffn_g"),
        big["w_up"],
        adam(conv_parts("conv_ffn", conv_ffn_w), conv_ffn_w, m_conv_ffn_w, v_conv_ffn_w, "conv_ffn_w"),
        big["w_down"],
        adam(small_parts("g_ple"), norm_ple_g, m_norm_ple_g, v_norm_ple_g, "norm_ple_g"),
        big["w_pg"],
        big["w_pp"],
        adam(small_parts("g_final"), final_norm_g.reshape(1, D), m_final_norm_g.reshape(1, D),
             v_final_norm_g.reshape(1, D), "final_norm_g"),
    ]
    res[-1] = tuple(t.reshape(D) for t in res[-1])
    landed_in = _split_wait(pending["w_in"], res[10][1], True, name="scatter_wait_w_in")
    res[1] = tuple(tr_(t) for t in adam(landed_in, tr_(w_in), tr_(m_w_in), tr_(v_w_in), "w_in", mine["w_in"]))
    grads, deltas, new_m, new_v = zip(*res)
    return (loss, grad_x[None], *grads, *deltas, *new_m, *new_v)
```

```python
import functools

import jax
import jax.numpy as jnp
from jax import lax
from jax.experimental import pallas as pl
from jax.experimental.pallas import tpu as pltpu

F32 = jnp.float32
BF16 = jnp.bfloat16

EPS = 1e-6
CHUNK = 64
HEAD = 128
LANE = 128
N_DEV = 8
AB_PAD = 512

ADAM_LR = 0.001
ADAM_B1 = 0.9
ADAM_B2 = 0.999
ADAM_EPS = 1e-08
ADAM_WD = 0.01
ADAM_STEP = 10

MESH = pl.DeviceIdType.MESH


def _tile(dim, target, align=LANE):
    if dim <= target:
        return dim
    t = (target // align) * align
    while t > align and dim % t:
        t -= align
    assert dim % t == 0, (dim, target)
    return t


def _params(sem, vmem_mb=48):
    return pltpu.CompilerParams(dimension_semantics=sem, vmem_limit_bytes=vmem_mb << 20)


_DN = {"nn": (((1,), (0,)), ((), ())), "nt": (((1,), (1,)), ((), ())), "tn": (((0,), (0,)), ((), ()))}
LONG_K = 4096
SHARD_TILE = 1408


def _matmul(a, b, mode, *, name, out_dtypes=(F32,), epilogue=None, extras=(), vec_extras=(), n_vec=0, after=None,
            a_shards=False, b_shards=False, out_shards=False, out_lanes=False, tm=1024, tn=1024, tk=2048):
    shard_w = b.shape[2] if b_shards else None
    if b_shards:
        b_rows, b_cols = b.shape[1], b.shape[0] * shard_w
    else:
        b_rows, b_cols = b.shape
    a_w = a.shape[2] if a_shards else None
    a_dims = (a.shape[1], a.shape[0] * a_w) if a_shards else a.shape
    if mode == "nn":
        (M, K), (K2, N) = a_dims, (b_rows, b_cols)
    elif mode == "nt":
        (M, K), (N, K2) = a_dims, (b_rows, b_cols)
    else:
        (K, M), (K2, N) = a_dims, (b_rows, b_cols)
    assert K == K2, (name, a.shape, b.shape)
    tm = _tile(M, tm)
    n_dims = [N] + ([shard_w] if (b_shards and mode != "nt") else []) + ([N // N_DEV] if out_shards else [])
    tn = _tile(min(n_dims), tn)
    assert all(d % tn == 0 for d in n_dims), (name, n_dims, tn)
    grp = 1
    if b_shards and mode == "nt":
        grp = max(g for g in (1, 2, 4, 8) if g <= max(1, tk // shard_w) and (a_w is None or a_w % (g * shard_w) == 0))
    k_dims = [K] + ([shard_w] if (b_shards and mode == "nt") else []) + ([a_w] if a_shards else [])
    tk = grp * shard_w if grp > 1 else _tile(min(k_dims), tk)
    assert K % tk == 0, (name, K, tk)
    nk = K // tk
    n_ex, n_out = len(extras) + len(vec_extras), len(out_dtypes)
    assert n_vec == 0 or tn == N, (name, tn, N)
    dn = _DN[mode]

    n_tok = 0 if after is None else 1

    def body(a_ref, b_ref, *rest):
        rest = rest[n_tok:]
        ex_refs, out_refs, vec_refs = rest[:n_ex], rest[n_ex:n_ex + n_out], rest[n_ex + n_out:n_ex + n_out + n_vec]
        if grp > 1:
            part = sum(lax.dot_general(a_ref[:, s * shard_w:(s + 1) * shard_w].astype(BF16), b_ref[s].astype(BF16), dn,
                                       preferred_element_type=F32) for s in range(grp))
        else:
            part = lax.dot_general(a_ref[...].astype(BF16), b_ref[...].astype(BF16), dn, preferred_element_type=F32)
        first_rows = pl.program_id(0) == 0

        def finish(res):
            outs = (res,) if epilogue is None else epilogue(res, *[e[...] for e in ex_refs])
            for o_ref, val in zip(out_refs, outs[:n_out]):
                if out_lanes:
                    for c in range(tn // LANE):
                        o_ref[c] = val[:, c * LANE:(c + 1) * LANE].astype(o_ref.dtype)
                else:
                    o_ref[...] = val.astype(o_ref.dtype)
            for v_ref, val in zip(vec_refs, outs[n_out:]):
                @pl.when(first_rows)
                def _(v_ref=v_ref, val=val):
                    v_ref[...] = val

                @pl.when(jnp.logical_not(first_rows))
                def _(v_ref=v_ref, val=val):
                    v_ref[...] += val

        if nk == 1:
            finish(part)
            return
        acc, k = rest[-1], pl.program_id(2)

        @pl.when(k == 0)
        def _():
            acc[...] = part

        @pl.when(k > 0)
        def _():
            acc[...] += part

        @pl.when(k == nk - 1)
        def _():
            finish(acc[...])

    if a_shards:
        assert mode == "nt" and a_w % tk == 0, (name, mode, a_w, tk)
        per_a = a_w // tk
        a_spec = pl.BlockSpec((None, tm, tk), lambda i, j, k: (lax.div(k, per_a), i, lax.rem(k, per_a)))
    else:
        a_spec = pl.BlockSpec((tk, tm), lambda i, j, k: (k, i)) if mode == "tn" else pl.BlockSpec((tm, tk), lambda i, j, k: (i, k))
    if b_shards and mode != "nt":
        per = shard_w // tn
        b_spec = pl.BlockSpec((None, tk, tn), lambda i, j, k: (lax.div(j, per), k, lax.rem(j, per)))
    elif b_shards and grp > 1:
        b_spec = pl.BlockSpec((grp, tn, shard_w), lambda i, j, k: (k, j, 0))
    elif b_shards:
        per = shard_w // tk
        b_spec = pl.BlockSpec((None, tn, tk), lambda i, j, k: (lax.div(k, per), j, lax.rem(k, per)))
    else:
        b_spec = pl.BlockSpec((tn, tk), lambda i, j, k: (j, k)) if mode == "nt" else pl.BlockSpec((tk, tn), lambda i, j, k: (k, j))
    mn_spec = pl.BlockSpec((tm, tn), lambda i, j, k: (i, j))
    vec_spec = pl.BlockSpec((1, tn), lambda i, j, k: (0, j))
    if out_shards:
        assert not extras
        per_o = (N // N_DEV) // tn
        out_spec = pl.BlockSpec((None, tm, tn), lambda i, j, k: (lax.div(j, per_o), i, lax.rem(j, per_o)))
        out_dims = (N_DEV, M, N // N_DEV)
    elif out_lanes:
        assert not extras
        out_spec = pl.BlockSpec((tn // LANE, tm, LANE), lambda i, j, k: (j, i, 0))
        out_dims = (N // LANE, M, LANE)
    else:
        out_spec, out_dims = mn_spec, (M, N)
    outs = pl.pallas_call(
        body, name=name, grid=(M // tm, N // tn, nk),
        in_specs=[a_spec, b_spec] + [pl.BlockSpec((8, LANE), lambda i, j, k: (0, 0))] * n_tok
        + [mn_spec] * len(extras) + [vec_spec] * len(vec_extras),
        out_specs=[out_spec] * n_out + [vec_spec] * n_vec,
        out_shape=[jax.ShapeDtypeStruct(out_dims, dt) for dt in out_dtypes] + [jax.ShapeDtypeStruct((1, N), F32)] * n_vec,
        scratch_shapes=[pltpu.VMEM((tm, tn), F32)] if nk > 1 else [],
        compiler_params=_params(("arbitrary" if n_vec else "parallel", "parallel", "arbitrary"), 56),
    )(a, b, *([] if after is None else [after]), *extras, *vec_extras)
    return outs[0] if n_out + n_vec == 1 else outs


def _rms_fwd(x, g, *, name):
    T, D = x.shape
    tr = _tile(T, 256, 8)

    def body(x_ref, g_ref, h_ref):
        xv = x_ref[...]
        r = lax.rsqrt(jnp.mean(xv * xv, axis=-1, keepdims=True) + EPS)
        h_ref[...] = (xv * r * g_ref[...]).astype(h_ref.dtype)

    return pl.pallas_call(
        body, name=name, grid=(T // tr,),
        in_specs=[pl.BlockSpec((tr, D), lambda i: (i, 0)), pl.BlockSpec((1, D), lambda i: (0, 0))],
        out_specs=pl.BlockSpec((tr, D), lambda i: (i, 0)),
        out_shape=jax.ShapeDtypeStruct((T, D), BF16),
        compiler_params=_params(("parallel",)),
    )(x, g)


def _rms_bwd(x, g, dh, dres, *, name):
    T, D = x.shape
    tr = _tile(T, 256, 8)
    epi = _epi_rms_bwd(2)

    def body(x_ref, g_ref, dh_ref, dres_ref, dx_ref, dxb_ref, dg_ref):
        dx, _, dgp = epi(dh_ref[...], x_ref[...], dres_ref[...], g_ref[...])

        @pl.when(pl.program_id(0) == 0)
        def _():
            dg_ref[...] = jnp.zeros_like(dg_ref)

        dg_ref[...] += dgp
        dx_ref[...] = dx
        dxb_ref[...] = dx.astype(dxb_ref.dtype)

    row = pl.BlockSpec((tr, D), lambda i: (i, 0))
    vec = pl.BlockSpec((1, D), lambda i: (0, 0))
    return pl.pallas_call(
        body, name=name, grid=(T // tr,),
        in_specs=[row, vec, row, row], out_specs=[row, row, vec],
        out_shape=[jax.ShapeDtypeStruct((T, D), F32), jax.ShapeDtypeStruct((T, D), BF16), jax.ShapeDtypeStruct((1, D), F32)],
        compiler_params=_params(("arbitrary",)),
    )(x, g, dh, dres)


ROW_TILE = 256


def _epi_residual_rms(acc, res, g):
    xn = acc + res
    r = lax.rsqrt(jnp.mean(xn * xn, axis=-1, keepdims=True) + EPS)
    return xn, xn * r * g


def _epi_rms_bwd(n_copies):
    def epi(dh, x, dres, g):
        r = lax.rsqrt(jnp.mean(x * x, axis=-1, keepdims=True) + EPS)
        xh = x * r
        dxh = dh * g
        dx = dres + r * (dxh - xh * jnp.mean(dxh * xh, axis=-1, keepdims=True))
        return (dx,) * n_copies + (jnp.sum(dh * xh, axis=0, keepdims=True),)
    return epi


def _final_loss(x, g, tgt, pp, sg, *, name):
    T, D = x.shape
    tr = _tile(T, 256, 8)

    def body(x_ref, g_ref, t_ref, pp_ref, sg_ref, dx_ref, dg_ref, loss_ref, dpg_ref, dpp_ref):
        xv = x_ref[...]
        r = lax.rsqrt(jnp.mean(xv * xv, axis=-1, keepdims=True) + EPS)
        xh = xv * r
        gv = g_ref[...]
        err = xh * gv - t_ref[...]

        @pl.when(pl.program_id(0) == 0)
        def _():
            dg_ref[...] = jnp.zeros_like(dg_ref)
            loss_ref[...] = jnp.zeros_like(loss_ref)

        part = 0.5 * jnp.sum(jnp.mean(err * err, axis=-1, keepdims=True), axis=0, keepdims=True)
        loss_ref[...] += jnp.broadcast_to(part, loss_ref.shape)
        dy = err * (1.0 / D)
        dg_ref[...] += jnp.sum(dy * xh, axis=0, keepdims=True)
        dxh = dy * gv
        dx = r * (dxh - xh * jnp.mean(dxh * xh, axis=-1, keepdims=True))
        dx_ref[...] = dx
        s = sg_ref[...]
        dpg_ref[...] = (dx * pp_ref[...] * s * (1.0 - s)).astype(dpg_ref.dtype)
        dpp_ref[...] = (dx * s).astype(dpp_ref.dtype)

    row = pl.BlockSpec((tr, D), lambda i: (i, 0))
    vec = pl.BlockSpec((1, D), lambda i: (0, 0))
    return pl.pallas_call(
        body, name=name, grid=(T // tr,),
        in_specs=[row, vec, row, row, row], out_specs=[row, vec, pl.BlockSpec((1, LANE), lambda i: (0, 0)), row, row],
        out_shape=[jax.ShapeDtypeStruct((T, D), F32), jax.ShapeDtypeStruct((1, D), F32),
                   jax.ShapeDtypeStruct((1, LANE), F32)] + [jax.ShapeDtypeStruct((T, D), BF16)] * 2,
        compiler_params=_params(("arbitrary",)),
    )(x, g, tgt, pp, sg)


ROWS_QKV_FWD, ROWS_QKV_BWD, ROWS_FFN_FWD, ROWS_FFN_BWD, ROWS_GROUP_A = 512, 256, 256, 128, 256


def _ext(ref, r0, T, before, after, RC):
    parts = []
    if before:
        p0 = pl.multiple_of(jnp.maximum(r0 - 8, 0), 8)
        parts.append(jnp.where(r0 > 0, ref[pl.ds(p0, 8), :], 0.0))
    parts.append(ref[pl.ds(r0, RC), :])
    if after:
        n0 = pl.multiple_of(jnp.minimum(r0 + RC, T - 8), 8)
        parts.append(jnp.where(r0 + RC < T, ref[pl.ds(n0, 8), :], 0.0))
    return parts[0] if len(parts) == 1 else jnp.concatenate(parts, axis=0)


def _fold8(x):
    return jnp.sum(x.reshape(x.shape[0] // 8, 8, x.shape[1]), axis=0)


def _win(ref, r0, lo, n, T, RC, edge):
    if not edge:
        return ref[pl.ds(r0 + lo, n), :]
    xx = _ext(ref, r0, T, True, True, RC)
    a = 8 + lo
    return (xx if a == 0 else pltpu.roll(xx, xx.shape[0] - a, 0))[:n, :]


def _taps(ref, w_ref, K, r0, n, T, RC, edge):
    wins = [_win(ref, r0, -(K - 1 - j), n, T, RC, edge) for j in range(K)]
    y = wins[0] * w_ref[0:1, :]
    for j in range(1, K):
        y = y + wins[j] * w_ref[j:j + 1, :]
    return wins, y


def _untaps(scr_ref, val, w_ref, K, RC):
    scr_ref[0:val.shape[0], :] = val
    y = scr_ref[K - 1:K - 1 + RC, :] * w_ref[0:1, :]
    for j in range(1, K):
        s = K - 1 - j
        y = y + scr_ref[s:s + RC, :] * w_ref[j:j + 1, :]
    return y


def _peeled(n_chunks, RC, step, init):
    carry = step(0, init, True)
    if n_chunks > 2:
        carry = lax.fori_loop(1, n_chunks - 1, lambda i, c: step(pl.multiple_of(i * RC, RC), c, False), carry)
    if n_chunks > 1:
        carry = step((n_chunks - 1) * RC, carry, True)
    return carry


def _silu(x):
    return x * jax.nn.sigmoid(x)


def _dsilu(x):
    s = jax.nn.sigmoid(x)
    return s * (1.0 + x * (1.0 - s))


def _col_specs(T, offs):
    return [pl.BlockSpec((T, LANE), functools.partial(lambda o, j: (0, o + j), o)) for o in offs]


def _group_a_fwd(proj, conv_w, CW, out_cols, *, name):
    T = proj.shape[0]
    RC = _tile(T, ROWS_GROUP_A, 8)
    nb = CW // LANE
    K = conv_w.shape[0]

    def body(ax_ref, ab_ref, ac_ref, w_ref, y_ref):
        def step(r0, carry, edge):
            c = None
            for j in range(K):
                lo = -(K - 1 - j)
                t = _win(ac_ref, r0, lo, RC, T, RC, edge) * _win(ax_ref, r0, lo, RC, T, RC, edge) * w_ref[j:j + 1, :]
                c = t if c is None else c + t
            y_ref[pl.ds(r0, RC), :] = (ab_ref[pl.ds(r0, RC), :] * c).astype(y_ref.dtype)
            return carry
        _peeled(T // RC, RC, step, 0)

    return pl.pallas_call(
        body, name=name, grid=(nb,),
        in_specs=_col_specs(T, (0, nb, 2 * nb)) + [pl.BlockSpec((K, LANE), lambda j: (0, j))],
        out_specs=pl.BlockSpec((T, LANE), lambda j: (0, j)),
        out_shape=jax.ShapeDtypeStruct((T, out_cols), BF16), compiler_params=_params(("parallel",)),
    )(proj, proj, proj, conv_w)


def _group_a_bwd(proj, conv_w, dycat, CW, *, name):
    T = proj.shape[0]
    RC = _tile(T, ROWS_GROUP_A, 8)
    nb = CW // LANE
    K = conv_w.shape[0]

    def body(ax_ref, ab_ref, ac_ref, w_ref, dy_ref, dax_ref, dab_ref, dac_ref, dw_ref, scr_ref):
        def step(r0, accs, edge):
            ms = [_win(ac_ref, r0, -(K - 1 - j), RC, T, RC, edge) * _win(ax_ref, r0, -(K - 1 - j), RC, T, RC, edge)
                  for j in range(K)]
            c = ms[0] * w_ref[0:1, :]
            for j in range(1, K):
                c = c + ms[j] * w_ref[j:j + 1, :]
            dy = dy_ref[pl.ds(r0, RC), :]
            dab_ref[pl.ds(r0, RC), :] = (dy * c).astype(dab_ref.dtype)
            dc2 = _win(dy_ref, r0, 0, RC + 8, T, RC, edge) * _win(ab_ref, r0, 0, RC + 8, T, RC, edge)
            dm = _untaps(scr_ref, dc2, w_ref, K, RC)
            dax_ref[pl.ds(r0, RC), :] = (dm * ac_ref[pl.ds(r0, RC), :]).astype(dax_ref.dtype)
            dac_ref[pl.ds(r0, RC), :] = (dm * ax_ref[pl.ds(r0, RC), :]).astype(dac_ref.dtype)
            return tuple(accs[j] + _fold8(dc2[:RC] * ms[j]) for j in range(K))

        accs = _peeled(T // RC, RC, step, tuple(jnp.zeros((8, LANE), F32) for _ in range(K)))
        for j in range(K):
            dw_ref[j:j + 1, :] = jnp.sum(accs[j], axis=0, keepdims=True)

    col = pl.BlockSpec((T, LANE), lambda j: (0, j))
    wsp = pl.BlockSpec((K, LANE), lambda j: (0, j))
    return pl.pallas_call(
        body, name=name, grid=(nb,),
        in_specs=_col_specs(T, (0, nb, 2 * nb)) + [wsp, col],
        out_specs=[col, col, col, wsp],
        out_shape=[jax.ShapeDtypeStruct((T, CW), BF16)] * 3 + [jax.ShapeDtypeStruct((K, CW), F32)],
        scratch_shapes=[pltpu.VMEM((RC + 8, LANE), F32)],
        compiler_params=_params(("parallel",)),
    )(proj, proj, proj, conv_w, dycat)


def _qkv_fwd(proj, conv_w, off, H, *, name):
    T = proj.shape[0]
    RC = _tile(T, ROWS_QKV_FWD, 8)
    nb = 3 * H
    K = conv_w.shape[0]

    def body(x_ref, w_ref, y_ref):
        j = pl.program_id(0)
        is_qk = j < 2 * H
        scale = jnp.where(j < H, HEAD ** -0.5, 1.0).astype(F32)

        def step(r0, carry, edge):
            s = _silu(_taps(x_ref, w_ref, K, r0, RC, T, RC, edge)[1])
            r = lax.rsqrt(jnp.sum(s * s, axis=-1, keepdims=True) + EPS) * scale
            y_ref[pl.ds(r0, RC), :] = s * jnp.where(is_qk, r, 1.0)
            return carry
        _peeled(T // RC, RC, step, 0)

    return pl.pallas_call(
        body, name=name, grid=(nb,),
        in_specs=_col_specs(T, (off,)) + [pl.BlockSpec((K, LANE), lambda j: (0, j))],
        out_specs=pl.BlockSpec((T, LANE), lambda j: (0, j)),
        out_shape=jax.ShapeDtypeStruct((T, nb * LANE), F32), compiler_params=_params(("parallel",)),
    )(proj, conv_w)


def _qkv_bwd(proj, conv_w, dq, dk, dv, off, H, *, name):
    T = proj.shape[0]
    RC = _tile(T, ROWS_QKV_BWD, 8)
    nb = 3 * H
    K = conv_w.shape[0]

    def body(x_ref, w_ref, dq_ref, dk_ref, dv_ref, dx_ref, dw_ref, scr_ref):
        j = pl.program_id(0)
        is_qk = j < 2 * H
        scale = jnp.where(j < H, HEAD ** -0.5, 1.0).astype(F32)

        def step(r0, accs, edge):
            xs, c2 = _taps(x_ref, w_ref, K, r0, RC + 8, T, RC, edge)
            s2 = _silu(c2)
            dn2 = jnp.where(j < H, _win(dq_ref, r0, 0, RC + 8, T, RC, edge),
                            jnp.where(is_qk, _win(dk_ref, r0, 0, RC + 8, T, RC, edge),
                                      _win(dv_ref, r0, 0, RC + 8, T, RC, edge)))
            r = lax.rsqrt(jnp.sum(s2 * s2, axis=-1, keepdims=True) + EPS)
            nh = s2 * r
            dnp = dn2 * scale
            ds_qk = r * (dnp - nh * jnp.sum(dnp * nh, axis=-1, keepdims=True))
            ds2 = jnp.where(is_qk, ds_qk, dn2)
            dc2 = ds2 * _dsilu(c2)
            dx_ref[pl.ds(r0, RC), :] = _untaps(scr_ref, dc2, w_ref, K, RC).astype(dx_ref.dtype)
            return tuple(accs[jj] + _fold8(dc2[:RC] * xs[jj][:RC]) for jj in range(K))

        accs = _peeled(T // RC, RC, step, tuple(jnp.zeros((8, LANE), F32) for _ in range(K)))
        for jj in range(K):
            dw_ref[jj:jj + 1, :] = jnp.sum(accs[jj], axis=0, keepdims=True)

    col = pl.BlockSpec((T, LANE), lambda j: (0, j))
    wsp = pl.BlockSpec((K, LANE), lambda j: (0, j))
    return pl.pallas_call(
        body, name=name, grid=(nb,),
        in_specs=_col_specs(T, (off,)) + [wsp] + [
            pl.BlockSpec((T, LANE), functools.partial(lambda o, j: (0, jnp.clip(j - o, 0, H - 1)), o)) for o in (0, H, 2 * H)],
        out_specs=[col, wsp],
        out_shape=[jax.ShapeDtypeStruct((T, nb * LANE), BF16), jax.ShapeDtypeStruct((K, nb * LANE), F32)],
        scratch_shapes=[pltpu.VMEM((RC + 8, LANE), F32)],
        compiler_params=_params(("parallel",)),
    )(proj, conv_w, dq, dk, dv)


def _softplus(x):
    return jnp.maximum(x, 0.0) + jnp.log(1.0 + jnp.exp(-jnp.abs(x)))


def _gates_fwd(proj, alog, dtb, off, H, *, name):
    T = proj.shape[0]
    tr = _tile(T, 512, CHUNK)

    def body(ab_ref, al_ref, dt_ref, gam_ref, beta_ref):
        ab = ab_ref[...]
        lane = lax.broadcasted_iota(jnp.int32, ab.shape, 1)
        g = -jnp.exp(al_ref[...]) * _softplus(ab + dt_ref[...])
        gb = jnp.where(lane < H, g, jnp.where(lane < 2 * H, jax.nn.sigmoid(ab), 0.0))
        tril = _tri().astype(F32)
        gam = jnp.concatenate([_mm(tril, gb[c * CHUNK:(c + 1) * CHUNK, :], precision=lax.Precision.HIGHEST)
                               for c in range(tr // CHUNK)], axis=0)
        for h in range(H):
            gam_ref[h] = jnp.broadcast_to(gam[:, h:h + 1], (tr, LANE))
            beta_ref[h] = jnp.broadcast_to(gb[:, H + h:H + h + 1], (tr, LANE))

    vec = pl.BlockSpec((1, LANE), lambda i: (0, 0))
    heads = pl.BlockSpec((H, tr, LANE), lambda i: (0, i, 0))
    return pl.pallas_call(
        body, name=name, grid=(T // tr,),
        in_specs=[pl.BlockSpec((tr, LANE), lambda i: (i, off)), vec, vec],
        out_specs=[heads, heads],
        out_shape=[jax.ShapeDtypeStruct((H, T, LANE), F32)] * 2, compiler_params=_params(("parallel",)),
    )(proj, alog, dtb)


def _gates_bwd(proj, alog, dtb, dgamB, dbB, off, H, *, name):
    T = proj.shape[0]
    tr = _tile(T, 512, CHUNK)

    def body(ab_ref, al_ref, dt_ref, dgam_ref, dbeta_ref, dab_ref, dal_ref, ddt_ref):
        ab = ab_ref[...]
        lane = lax.broadcasted_iota(jnp.int32, ab.shape, 1)
        is_g = lane < H
        d = jnp.zeros_like(ab)
        for h in range(H):
            d = jnp.where(lane == h, dgam_ref[h], jnp.where(lane == H + h, dbeta_ref[h], d))
        triu = _tri(upper=True).astype(F32)
        dg = jnp.concatenate([_mm(triu, d[c * CHUNK:(c + 1) * CHUNK, :], precision=lax.Precision.HIGHEST)
                              for c in range(tr // CHUNK)], axis=0)
        z = ab + dt_ref[...]
        A = -jnp.exp(al_ref[...])
        da = dg * A * jax.nn.sigmoid(z)
        beta = jax.nn.sigmoid(ab)
        db = d * beta * (1.0 - beta)
        dab_ref[...] = jnp.where(is_g, da, jnp.where(lane < 2 * H, db, 0.0)).astype(dab_ref.dtype)

        @pl.when(pl.program_id(0) == 0)
        def _():
            dal_ref[...] = jnp.zeros_like(dal_ref)
            ddt_ref[...] = jnp.zeros_like(ddt_ref)

        dal_ref[...] += jnp.sum(jnp.where(is_g, dg * A * _softplus(z), 0.0), axis=0, keepdims=True)
        ddt_ref[...] += jnp.sum(jnp.where(is_g, da, 0.0), axis=0, keepdims=True)

    vec = pl.BlockSpec((1, LANE), lambda i: (0, 0))
    row = pl.BlockSpec((tr, LANE), lambda i: (i, 0))
    heads = pl.BlockSpec((H, tr, LANE), lambda i: (0, i, 0))
    return pl.pallas_call(
        body, name=name, grid=(T // tr,),
        in_specs=[pl.BlockSpec((tr, LANE), lambda i: (i, off)), vec, vec, heads, heads],
        out_specs=[row, vec, vec],
        out_shape=[jax.ShapeDtypeStruct((T, LANE), BF16), jax.ShapeDtypeStruct((1, LANE), F32),
                   jax.ShapeDtypeStruct((1, LANE), F32)],
        compiler_params=_params(("arbitrary",)),
    )(proj, alog, dtb, dgamB, dbB)


def _gated_norm_fwd(o, proj, gn, zoff, ycat, *, name):
    T, W = o.shape
    tr = _tile(T, 256, 8)
    nh_, zblk = W // LANE, (zoff * LANE) // W
    assert zblk * W == zoff * LANE

    def body(o_ref, z_ref, g_ref, ycat_ref, y_ref):
        for h in range(nh_):
            ln = slice(h * LANE, (h + 1) * LANE)
            ov = o_ref[:, ln]
            r = lax.rsqrt(jnp.mean(ov * ov, axis=-1, keepdims=True) + EPS)
            y_ref[:, ln] = (ov * r * g_ref[...] * _silu(z_ref[:, ln])).astype(y_ref.dtype)

    assert ycat.shape == (T, 2 * W), ycat.shape
    blk = pl.BlockSpec((tr, W), lambda i: (i, 0))
    return pl.pallas_call(
        body, name=name, grid=(T // tr,),
        in_specs=[blk, pl.BlockSpec((tr, W), lambda i: (i, zblk)), pl.BlockSpec((1, LANE), lambda i: (0, 0)),
                  pl.BlockSpec(memory_space=pl.ANY)],
        out_specs=pl.BlockSpec((tr, W), lambda i: (i, 1)), out_shape=jax.ShapeDtypeStruct(ycat.shape, ycat.dtype),
        input_output_aliases={3: 0}, compiler_params=_params(("parallel",)),
    )(o, proj, gn, ycat)


def _gated_norm_bwd(o, proj, gn, dycat, zoff, yoff, *, name):
    T, W = o.shape
    tr = _tile(T, 256, 8)
    nh_, zblk, yblk = W // LANE, (zoff * LANE) // W, (yoff * LANE) // W
    assert zblk * W == zoff * LANE and yblk * W == yoff * LANE

    def body(o_ref, z_ref, g_ref, dy_ref, do_ref, dz_ref, dg_ref):
        @pl.when(pl.program_id(0) == 0)
        def _():
            dg_ref[...] = jnp.zeros_like(dg_ref)

        gv = g_ref[...]
        dg = jnp.zeros_like(gv)
        for h in range(nh_):
            ln = slice(h * LANE, (h + 1) * LANE)
            ov, zv, dy = o_ref[:, ln], z_ref[:, ln], dy_ref[:, ln]
            r = lax.rsqrt(jnp.mean(ov * ov, axis=-1, keepdims=True) + EPS)
            nh = ov * r
            s = _silu(zv)
            dg = dg + jnp.sum(dy * nh * s, axis=0, keepdims=True)
            dz_ref[:, ln] = (dy * nh * gv * _dsilu(zv)).astype(dz_ref.dtype)
            dn = dy * gv * s
            do_ref[:, ln] = r * (dn - nh * jnp.mean(dn * nh, axis=-1, keepdims=True))
        dg_ref[...] += dg

    blk = pl.BlockSpec((tr, W), lambda i: (i, 0))
    vec = pl.BlockSpec((1, LANE), lambda i: (0, 0))
    return pl.pallas_call(
        body, name=name, grid=(T // tr,),
        in_specs=[blk, pl.BlockSpec((tr, W), lambda i: (i, zblk)), vec, pl.BlockSpec((tr, W), lambda i: (i, yblk))],
        out_specs=[blk, blk, vec],
        out_shape=[jax.ShapeDtypeStruct((T, W), F32), jax.ShapeDtypeStruct((T, W), BF16),
                   jax.ShapeDtypeStruct((1, LANE), F32)],
        compiler_params=_params(("arbitrary",)),
    )(o, proj, gn, dycat)


def _ffn_act_fwd(up_pre, conv_w, *, name):
    T, F2 = up_pre.shape[1], up_pre.shape[0] * LANE
    RC = _tile(T, ROWS_FFN_FWD, 8)
    nb = F2 // 2 // LANE
    K = conv_w.shape[0]

    def body(g_ref, v_ref, wg_ref, wv_ref, y_ref):
        def step(r0, carry, edge):
            _, gate = _taps(g_ref, wg_ref, K, r0, RC, T, RC, edge)
            _, val = _taps(v_ref, wv_ref, K, r0, RC, T, RC, edge)
            y_ref[pl.ds(r0, RC), :] = (_silu(gate) * val).astype(y_ref.dtype)
            return carry
        _peeled(T // RC, RC, step, 0)

    return pl.pallas_call(
        body, name=name, grid=(nb,),
        in_specs=[pl.BlockSpec((None, T, LANE), lambda j: (j, 0, 0)), pl.BlockSpec((None, T, LANE), lambda j: (nb + j, 0, 0)),
                  pl.BlockSpec((K, LANE), lambda j: (0, j)), pl.BlockSpec((K, LANE), lambda j: (0, nb + j))],
        out_specs=pl.BlockSpec((T, LANE), lambda j: (0, j)),
        out_shape=jax.ShapeDtypeStruct((T, F2 // 2), BF16), compiler_params=_params(("parallel",)),
    )(up_pre, up_pre, conv_w, conv_w)


def _ffn_act_bwd(up_pre, conv_w, dact, *, name):
    T, F2 = up_pre.shape[1], up_pre.shape[0] * LANE
    RC = _tile(T, ROWS_FFN_BWD, 8)
    nb = F2 // 2 // LANE
    K = conv_w.shape[0]

    def body(g_ref, v_ref, wg_ref, wv_ref, da_ref, d_ref, dwg_ref, dwv_ref, sg_ref, sv_ref):
        def step(r0, accs, edge):
            gs, gate2 = _taps(g_ref, wg_ref, K, r0, RC + 8, T, RC, edge)
            vs, val2 = _taps(v_ref, wv_ref, K, r0, RC + 8, T, RC, edge)
            da2 = _win(da_ref, r0, 0, RC + 8, T, RC, edge)
            dgate2 = da2 * val2 * _dsilu(gate2)
            dval2 = da2 * _silu(gate2)
            d_ref[0, pl.ds(r0, RC), :] = _untaps(sg_ref, dgate2, wg_ref, K, RC).astype(d_ref.dtype)
            d_ref[1, pl.ds(r0, RC), :] = _untaps(sv_ref, dval2, wv_ref, K, RC).astype(d_ref.dtype)
            new = []
            for j in range(K):
                new.append(accs[2 * j] + _fold8(dgate2[:RC] * gs[j][:RC]))
                new.append(accs[2 * j + 1] + _fold8(dval2[:RC] * vs[j][:RC]))
            return tuple(new)

        accs = _peeled(T // RC, RC, step, tuple(jnp.zeros((8, LANE), F32) for _ in range(2 * K)))
        for j in range(K):
            dwg_ref[j:j + 1, :] = jnp.sum(accs[2 * j], axis=0, keepdims=True)
            dwv_ref[j:j + 1, :] = jnp.sum(accs[2 * j + 1], axis=0, keepdims=True)

    col = pl.BlockSpec((T, LANE), lambda j: (0, j))
    wsp = pl.BlockSpec((K, LANE), lambda j: (0, j))
    return pl.pallas_call(
        body, name=name, grid=(nb,),
        in_specs=[pl.BlockSpec((None, T, LANE), lambda j: (j, 0, 0)), pl.BlockSpec((None, T, LANE), lambda j: (nb + j, 0, 0)),
                  wsp, pl.BlockSpec((K, LANE), lambda j: (0, nb + j)), col],
        out_specs=[pl.BlockSpec((2, T, LANE), lambda j: (0, 0, j)), wsp, wsp],
        out_shape=[jax.ShapeDtypeStruct((2, T, F2 // 2), BF16)] + [jax.ShapeDtypeStruct((K, F2 // 2), F32)] * 2,
        scratch_shapes=[pltpu.VMEM((RC + 8, LANE), F32)] * 2,
        compiler_params=_params(("parallel",)),
    )(up_pre, up_pre, conv_w, conv_w, dact)


CPB = 8
CPB_SCAN = 4
GRP = 8
HP = lax.Precision.HIGH


def _tri(strict=False, upper=False):
    r = lax.broadcasted_iota(jnp.int32, (CHUNK, CHUNK), 0)
    c = lax.broadcasted_iota(jnp.int32, (CHUNK, CHUNK), 1)
    if upper:
        return c >= r
    return (r > c) if strict else (r >= c)


def _mm(a, b, dn="nn", precision=None):
    precision = HP if precision is None else precision
    return lax.dot_general(a, b, _DN[dn], precision=precision, preferred_element_type=F32)


def _mm16(a, b, dn="nn"):
    return lax.dot_general(a.astype(BF16), b.astype(BF16), _DN[dn], preferred_element_type=F32)


def _each(f, *cols):
    return [f(*xs) for xs in zip(*cols)]


def _decay(gam):
    return jnp.exp(jnp.where(_tri(), gam[:, :CHUNK] - gam.T[:CHUNK, :], -1e30))


def _delta_specs(T, H, cpb):
    rows = cpb * CHUNK
    col = lambda o: pl.BlockSpec((rows, LANE), functools.partial(lambda o, h, n: (n, o + h), o))
    bc = pl.BlockSpec((1, rows, LANE), lambda h, n: (h, n, 0))
    sq = pl.BlockSpec((1, cpb, CHUNK, CHUNK), lambda h, n: (h, n, 0, 0))
    vec = pl.BlockSpec((1, cpb, 1, LANE), lambda h, n: (h, n, 0, 0))
    return col, bc, sq, vec


def _delta_prep_fwd(qkv, gamB, bB, H, *, name):
    T = qkv.shape[0]
    N = T // CHUNK
    cpb = _tile(N, CPB, 8)
    grp = min(GRP, cpb)
    col, bc, sq, vec = _delta_specs(T, H, cpb)

    def body(q_ref, k_ref, v_ref, g_ref, b_ref, u_ref, w_ref, qd_ref, kd_ref, qk_ref, ti_ref, gl_ref):
        eye = (lax.broadcasted_iota(jnp.int32, (CHUNK, CHUNK), 0) == lax.broadcasted_iota(jnp.int32, (CHUNK, CHUNK), 1)).astype(F32)
        strict = _tri(strict=True)
        for c0 in range(0, cpb, grp):
            cs = list(range(c0, c0 + grp))
            rows = [slice(c * CHUNK, (c + 1) * CHUNK) for c in cs]
            q, k, v = ([r_[r, :] for r in rows] for r_ in (q_ref, k_ref, v_ref))
            bb = [b_ref[0, r, :] for r in rows]
            gam = [g_ref[0, r, :] for r in rows]
            D = _each(_decay, gam)
            e = _each(jnp.exp, gam)
            kk = _each(lambda k_: _mm16(k_, k_, "nt"), k)
            X = _each(lambda kk_, D_, b_: -(jnp.where(strict, kk_ * D_, 0.0) * b_[:, :CHUNK]), kk, D, bb)
            R = _each(lambda x: eye + x, X)
            for _ in range(5):
                X = _each(lambda x: _mm(x, x), X)
                R = _each(lambda r, x: r + _mm(r, x), R, X)
            u = _each(lambda r, b_, v_: _mm(r, b_ * v_), R, bb, v)
            w = _each(lambda r, b_, e_, k_: _mm(r, b_ * e_ * k_), R, bb, e, k)
            qk = _each(lambda q_, k_, D_: _mm16(q_, k_, "nt") * D_, q, k, D)
            for i, c in enumerate(cs):
                glast = gam[i][CHUNK - 1:CHUNK, :]
                u_ref[rows[i], :] = u[i]
                w_ref[rows[i], :] = w[i]
                qd_ref[rows[i], :] = e[i] * q[i]
                kd_ref[rows[i], :] = jnp.exp(glast - gam[i]) * k[i]
                qk_ref[0, c] = qk[i]
                ti_ref[0, c] = R[i]
                gl_ref[0, c] = jnp.exp(glast)

    full = jax.ShapeDtypeStruct((T, H * LANE), F32)
    sqs = jax.ShapeDtypeStruct((H, N, CHUNK, CHUNK), F32)
    return pl.pallas_call(
        body, name=name, grid=(H, N // cpb),
        in_specs=[col(0), col(H), col(2 * H), bc, bc],
        out_specs=[col(0)] * 4 + [sq, sq, vec],
        out_shape=[full] * 4 + [sqs, sqs, jax.ShapeDtypeStruct((H, N, 1, LANE), F32)],
        compiler_params=_params(("parallel", "parallel")),
    )(qkv, qkv, qkv, gamB, bB)


def _scan_specs(H, N, cpb, hb, rev):
    nbk = N // cpb
    blk = (lambda n: nbk - 1 - n) if rev else (lambda n: n)
    col = pl.BlockSpec((cpb * CHUNK, hb * LANE), lambda h, n: (blk(n), h))
    sq = pl.BlockSpec((hb, cpb, CHUNK, CHUNK), lambda h, n: (h, blk(n), 0, 0))
    vec = pl.BlockSpec((hb, cpb, 1, LANE), lambda h, n: (h, blk(n), 0, 0))
    st = pl.BlockSpec((hb, cpb, HEAD, HEAD), lambda h, n: (h, blk(n), 0, 0))
    return col, sq, vec, st


def _delta_scan_fwd(u, w, qd, kd, qk, gl, H, *, name):
    T = u.shape[0]
    N = T // CHUNK
    cpb = _tile(N, CPB_SCAN, 4)
    hb = min(GRP, H)
    col, sq, vec, st = _scan_specs(H, N, cpb, hb, False)
    lanes = [slice(j * LANE, (j + 1) * LANE) for j in range(hb)]
    heads = list(range(hb))

    def body(u_ref, w_ref, qd_ref, kd_ref, qk_ref, gl_ref, o_ref, vn_ref, ss_ref, s_scr):
        @pl.when(pl.program_id(1) == 0)
        def _():
            s_scr[...] = jnp.zeros_like(s_scr)

        def step(c, states):
            rows = pl.ds(pl.multiple_of(c * CHUNK, CHUNK), CHUNK)
            S = list(states)
            for j in heads:
                ss_ref[j, c] = S[j]
            wS = _each(lambda ln, s: _mm16(w_ref[rows, ln], s), lanes, S)
            qS = _each(lambda ln, s: _mm16(qd_ref[rows, ln], s), lanes, S)
            vn = _each(lambda ln, ws: u_ref[rows, ln] - ws, lanes, wS)
            o = _each(lambda j, qs, vn_: qs + _mm16(qk_ref[j, c], vn_), heads, qS, vn)
            new = _each(lambda j, ln, s, vn_: s * gl_ref[j, c] + _mm16(kd_ref[rows, ln], vn_, "tn"),
                        heads, lanes, S, vn)
            for j in heads:
                o_ref[rows, lanes[j]] = o[j]
                vn_ref[rows, lanes[j]] = vn[j]
            return tuple(new)
        out = lax.fori_loop(0, cpb, step, tuple(s_scr[j] for j in heads))
        for j in heads:
            s_scr[j] = out[j]

    full = jax.ShapeDtypeStruct((T, H * LANE), F32)
    return pl.pallas_call(
        body, name=name, grid=(H // hb, N // cpb),
        in_specs=[col] * 4 + [sq, vec],
        out_specs=[col, col, st],
        out_shape=[full, full, jax.ShapeDtypeStruct((H, N, HEAD, HEAD), F32)],
        scratch_shapes=[pltpu.VMEM((hb, HEAD, HEAD), F32)],
        compiler_params=_params(("parallel", "arbitrary")),
    )(u, w, qd, kd, qk, gl)


def _delta_scan_bwd(do, w, qd, kd, vn, qk, gl, ss, H, *, name):
    T = do.shape[0]
    N = T // CHUNK
    cpb = _tile(N, CPB_SCAN, 4)
    hb = min(GRP, H)
    col, sq, vec, st = _scan_specs(H, N, cpb, hb, True)
    lanes = [slice(j * LANE, (j + 1) * LANE) for j in range(hb)]
    heads = list(range(hb))

    def body(do_ref, w_ref, qd_ref, kd_ref, vn_ref, qk_ref, gl_ref, ss_ref,
             du_ref, dw_ref, dqd_ref, dkd_ref, dqk_ref, dgl_ref, ds_scr):
        @pl.when(pl.program_id(1) == 0)
        def _():
            ds_scr[...] = jnp.zeros_like(ds_scr)

        def step(i, dstates):
            c = cpb - 1 - i
            rows = pl.ds(pl.multiple_of(c * CHUNK, CHUNK), CHUNK)
            dS = list(dstates)
            S = [ss_ref[j, c] for j in heads]
            dov = [do_ref[rows, ln] for ln in lanes]
            vnv = [vn_ref[rows, ln] for ln in lanes]
            a1 = _each(lambda j, d_: _mm16(qk_ref[j, c], d_, "tn"), heads, dov)
            a2 = _each(lambda ln, ds: _mm16(kd_ref[rows, ln], ds), lanes, dS)
            dvn = _each(lambda x, y: x + y, a1, a2)
            dqd = _each(lambda d_, s: _mm16(d_, s, "nt"), dov, S)
            dkd = _each(lambda v_, ds: _mm16(v_, ds, "nt"), vnv, dS)
            dqk = _each(lambda d_, v_: _mm16(d_, v_, "nt"), dov, vnv)
            dw = _each(lambda dv_, s: -_mm16(dv_, s, "nt"), dvn, S)
            b1 = _each(lambda ln, d_: _mm16(qd_ref[rows, ln], d_, "tn"), lanes, dov)
            b2 = _each(lambda ln, dv_: _mm16(w_ref[rows, ln], dv_, "tn"), lanes, dvn)
            new = _each(lambda j, x, y, ds: x + ds * gl_ref[j, c] - y, heads, b1, b2, dS)
            for j in heads:
                du_ref[rows, lanes[j]] = dvn[j]
                dw_ref[rows, lanes[j]] = dw[j]
                dqd_ref[rows, lanes[j]] = dqd[j]
                dkd_ref[rows, lanes[j]] = dkd[j]
                dqk_ref[j, c] = dqk[j]
                dgl = jnp.sum(jnp.sum(dS[j] * S[j], axis=1, keepdims=True), axis=0, keepdims=True)
                dgl_ref[j, c] = jnp.broadcast_to(dgl, (1, LANE))
            return tuple(new)
        out = lax.fori_loop(0, cpb, step, tuple(ds_scr[j] for j in heads))
        for j in heads:
            ds_scr[j] = out[j]

    full = jax.ShapeDtypeStruct((T, H * LANE), F32)
    return pl.pallas_call(
        body, name=name, grid=(H // hb, N // cpb),
        in_specs=[col] * 5 + [sq, vec, st],
        out_specs=[col] * 4 + [sq, vec],
        out_shape=[full] * 4 + [jax.ShapeDtypeStruct((H, N, CHUNK, CHUNK), F32), jax.ShapeDtypeStruct((H, N, 1, LANE), F32)],
        scratch_shapes=[pltpu.VMEM((hb, HEAD, HEAD), F32)],
        compiler_params=_params(("parallel", "arbitrary")),
    )(do, w, qd, kd, vn, qk, gl, ss)


def _delta_prep_bwd(qkv, gamB, bB, ti, u, w, qk, du, dw, dqd, dkd, dqk, dgl, H, *, name):
    T = qkv.shape[0]
    N = T // CHUNK
    cpb = _tile(N, CPB, 8)
    grp = min(GRP, cpb)
    col, bc, sq, vec = _delta_specs(T, H, cpb)

    def body(q_ref, k_ref, v_ref, g_ref, b_ref, ti_ref, u_ref, w_ref, qk_ref,
             du_ref, dw_ref, dqd_ref, dkd_ref, dqk_ref, dgl_ref,
             dq_ref, dk_ref, dv_ref, dg_ref, db_ref):
        ones = jnp.ones((CHUNK, LANE), F32)
        strict = _tri(strict=True)
        last = lax.broadcasted_iota(jnp.int32, (CHUNK, LANE), 0) == CHUNK - 1
        lsum = lambda x: jnp.sum(x, axis=-1, keepdims=True)
        for c0 in range(0, cpb, grp):
            cs = list(range(c0, c0 + grp))
            rows = [slice(c * CHUNK, (c + 1) * CHUNK) for c in cs]
            ld = lambda r_: [r_[r, :] for r in rows]
            q, k, v, uv, wv, duv, dwv, dqd_v, dkd_v = (ld(r_) for r_ in (q_ref, k_ref, v_ref, u_ref, w_ref, du_ref, dw_ref, dqd_ref, dkd_ref))
            bb = [b_ref[0, r, :] for r in rows]
            gam = [g_ref[0, r, :] for r in rows]
            Ti = [ti_ref[0, c] for c in cs]
            QK = [qk_ref[0, c] for c in cs]
            dqk_v = [dqk_ref[0, c] for c in cs]
            D = _each(_decay, gam)
            e = _each(jnp.exp, gam)
            glast = [g_[CHUNK - 1:CHUNK, :] for g_ in gam]
            eL = _each(lambda gl_, g_: jnp.exp(gl_ - g_), glast, gam)
            kk = _each(lambda k_: _mm16(k_, k_, "nt"), k)
            KKD = _each(lambda kk_, D_: jnp.where(strict, kk_ * D_, 0.0), kk, D)
            dru = _each(lambda t, d_: _mm(t, d_, "tn"), Ti, duv)
            drw = _each(lambda t, d_: _mm(t, d_, "tn"), Ti, dwv)
            l1 = _each(lambda a, b: _mm(a, b, "nt"), dru, uv)
            l2 = _each(lambda a, b: _mm(a, b, "nt"), drw, wv)
            dL = _each(lambda a, b: jnp.where(strict, -(a + b), 0.0), l1, l2)
            Mm = _each(lambda dl, b_: dl * b_[:, :CHUNK], dL, bb)
            dKK = _each(lambda m_, D_: m_ * D_, Mm, D)
            dQK = _each(lambda a, D_: a * D_, dqk_v, D)
            P = _each(lambda m_, kkd, a, qk_: m_ * kkd + a * qk_, Mm, KKD, dqk_v, QK)
            q1 = _each(lambda a, k_: _mm16(a, k_), dQK, k)
            k1 = _each(lambda a, q_: _mm16(a, q_, "tn"), dQK, q)
            k2 = _each(lambda a, k_: _mm16(a, k_), dKK, k)
            k3 = _each(lambda a, k_: _mm16(a, k_, "tn"), dKK, k)
            s1 = _each(lambda dl, kkd: _mm(dl * kkd, ones), dL, KKD)
            p1 = _each(lambda p_: _mm(p_, ones), P)
            p2 = _each(lambda p_: _mm(p_, ones, "tn"), P)
            for i, c in enumerate(cs):
                r = rows[i]
                bek = bb[i] * e[i]
                kdv = eL[i] * k[i]
                dq_ref[r, :] = q1[i] + e[i] * dqd_v[i]
                dk_ref[r, :] = k1[i] + k2[i] + k3[i] + bek * drw[i] + eL[i] * dkd_v[i]
                dv_ref[r, :] = bb[i] * dru[i]
                db_ref[0, r, :] = s1[i] + lsum(dru[i] * v[i]) + lsum(drw[i] * e[i] * k[i])
                dgam = (p1[i] - p2[i] + lsum(drw[i] * bek * k[i]) + lsum(dqd_v[i] * e[i] * q[i])
                        - lsum(dkd_v[i] * kdv))
                xlast = jnp.sum(lsum(dkd_v[i] * kdv), axis=0, keepdims=True) + jnp.exp(glast[i]) * dgl_ref[0, c]
                dg_ref[0, r, :] = dgam + jnp.where(last, xlast, 0.0)

    full = jax.ShapeDtypeStruct((T, H * LANE), F32)
    bcs = jax.ShapeDtypeStruct((H, T, LANE), F32)
    return pl.pallas_call(
        body, name=name, grid=(H, N // cpb),
        in_specs=[col(0), col(H), col(2 * H), bc, bc, sq, col(0), col(0), sq, col(0), col(0), col(0), col(0), sq, vec],
        out_specs=[col(0), col(0), col(0), bc, bc],
        out_shape=[full, full, full, bcs, bcs],
        compiler_params=_params(("parallel", "parallel")),
    )(qkv, qkv, qkv, gamB, bB, ti, u, w, qk, du, dw, dqd, dkd, dqk, dgl)


def _adam(parts, w, m, v, *, name, own=None, me=None):
    P, R, C = parts.shape
    if R > 256 and R % 8:
        tr, tc = R, _tile(C, 256)
    else:
        tr, tc = _tile(R, 256, 8), C
    n_own = 0 if own is None else 2

    def body(*refs):
        p_ref, w_ref, m_ref, v_ref, g_ref, d_ref, nm_ref, nv_ref = refs[n_own:]
        g = None
        for i in range(P):
            t = p_ref[i].astype(F32)
            if n_own:
                t = jnp.where(refs[0][0] == i, refs[1][...].astype(F32), t)
            g = t if g is None else g + t
        mn = ADAM_B1 * m_ref[...] + (1.0 - ADAM_B1) * g
        vn = ADAM_B2 * v_ref[...] + (1.0 - ADAM_B2) * (g * g)
        m_hat = mn / (1.0 - ADAM_B1 ** ADAM_STEP)
        v_hat = vn / (1.0 - ADAM_B2 ** ADAM_STEP)
        g_ref[...] = g
        d_ref[...] = -ADAM_LR * (m_hat / (jnp.sqrt(v_hat) + ADAM_EPS) + ADAM_WD * w_ref[...])
        nm_ref[...] = mn
        nv_ref[...] = vn

    blk = pl.BlockSpec((tr, tc), lambda i, j: (i, j))
    return pl.pallas_call(
        body, name=name, grid=(R // tr, C // tc),
        in_specs=[pl.BlockSpec(memory_space=pltpu.SMEM), blk][:n_own] + [pl.BlockSpec((P, tr, tc), lambda i, j: (0, i, j)), blk, blk, blk],
        out_specs=[blk] * 4, out_shape=[jax.ShapeDtypeStruct((R, C), F32)] * 4,
        compiler_params=_params(("parallel", "parallel")),
    )(*([me, own] if n_own else []), parts, w, m, v)


def _mesh_pos():
    return lax.axis_index("x"), lax.axis_index("y"), lax.axis_index("c")


def _peer(k):
    x, y, c = _mesh_pos()
    px, py, pc = x ^ ((k >> 2) & 1), y ^ ((k >> 1) & 1), c ^ (k & 1)
    return (px, py, pc), 4 * px + 2 * py + pc


def _exchange(arrays, scatter, *, name, after=None):
    n = len(arrays)
    n_in = n if after is None else n + 1
    blocks = [a.shape[1:] if scatter else a.shape for a in arrays]

    def body(*refs):
        srcs, dsts = refs[:n], refs[n_in:n_in + n]
        send_sems, recv_sems, local_sems = refs[n_in + n:]
        x, y, c = _mesh_pos()
        me = 4 * x + 2 * y + c
        local, sends = [], []
        for a in range(n):
            cp = pltpu.make_async_copy(srcs[a].at[me] if scatter else srcs[a], dsts[a].at[me], local_sems.at[a])
            cp.start()
            local.append(cp)
            for k in range(1, N_DEV):
                dev, idx = _peer(k)
                cp = pltpu.make_async_remote_copy(
                    src_ref=srcs[a].at[idx] if scatter else srcs[a], dst_ref=dsts[a].at[me],
                    send_sem=send_sems.at[a * N_DEV + k], recv_sem=recv_sems.at[a * N_DEV + k],
                    device_id=dev, device_id_type=MESH)
                cp.start()
                sends.append(cp)
        for a in range(n):
            for k in range(1, N_DEV):
                dev, idx = _peer(k)
                pltpu.make_async_remote_copy(
                    src_ref=srcs[a].at[idx] if scatter else srcs[a], dst_ref=dsts[a].at[idx],
                    send_sem=send_sems.at[a * N_DEV + k], recv_sem=recv_sems.at[a * N_DEV + k],
                    device_id=dev, device_id_type=MESH).wait_recv()
        for cp in sends:
            cp.wait_send()
        for cp in local:
            cp.wait()

    anyspec = pl.BlockSpec(memory_space=pl.ANY)
    return pl.pallas_call(
        body, name=name, in_specs=[anyspec] * n_in, out_specs=[anyspec] * n,
        out_shape=[jax.ShapeDtypeStruct((N_DEV,) + tuple(b), a.dtype) for a, b in zip(arrays, blocks)],
        scratch_shapes=[pltpu.SemaphoreType.DMA((n * N_DEV,)), pltpu.SemaphoreType.DMA((n * N_DEV,)),
                        pltpu.SemaphoreType.DMA((n,))],
    )(*arrays, *([] if after is None else [after]))


_ANY = pl.BlockSpec(memory_space=pl.ANY)
_SEM = pl.BlockSpec(memory_space=pltpu.SEMAPHORE)
_EFFECT = pltpu.SideEffectType.DATAFLOW_SIDE_EFFECTING


def _in_hbm(a):
    return pltpu.with_memory_space_constraint(a, pltpu.HBM)


def _split_copy(src, land, send, recv, k, me, scatter, landed):
    dev, idx = _peer(k)
    return pltpu.make_async_remote_copy(
        src_ref=src.at[idx] if scatter else src, dst_ref=land.at[idx if landed else me],
        send_sem=send.at[k], recv_sem=recv.at[k], device_id=dev, device_id_type=MESH)


ALL_PEERS = tuple(range(1, N_DEV))
SIBLING = 1
SAME_CORE = (2, 4, 6)


def _split_start(srcs, lands, scatter, *, name, relations=None):
    n = len(srcs)
    relations = relations or [ALL_PEERS] * n

    def body(*refs):
        src, land, send, recv, token = refs[:n], refs[n:2 * n], refs[2 * n:3 * n], refs[3 * n:4 * n], refs[-1]
        x, y, c = _mesh_pos()
        me = 4 * x + 2 * y + c
        for a in range(n):
            for k in relations[a]:
                _split_copy(src[a], land[a], send[a], recv[a], k, me, scatter, False).start()
        token[...] = jnp.zeros_like(token)

    outs = pl.pallas_call(
        body, name=name,
        out_shape=[pltpu.SemaphoreType.DMA((N_DEV,))] * (2 * n) + [pltpu.HBM(t.shape, t.dtype) for t in list(srcs) + list(lands)]
        + [jax.ShapeDtypeStruct((8, LANE), F32)],
        in_specs=[_ANY] * (2 * n), out_specs=[_SEM] * (2 * n) + [_ANY] * (2 * n) + [pl.BlockSpec(memory_space=pltpu.VMEM)],
        input_output_aliases={i: 2 * n + i for i in range(2 * n)},
        compiler_params=pltpu.CompilerParams(has_side_effects=_EFFECT),
    )(*[_in_hbm(t) for t in list(srcs) + list(lands)])
    handles = [(outs[a], outs[n + a], outs[2 * n + a], outs[3 * n + a]) for a in range(n)]
    return handles, outs[-1]


def _split_wait(handle, after, scatter, *, name):
    send, recv, src_thru, land_thru = handle

    def body(src_ref, land_ref, send_ref, recv_ref, after_ref, src_out, land_out):
        x, y, c = _mesh_pos()
        me = 4 * x + 2 * y + c
        for k in range(1, N_DEV):
            cp = _split_copy(src_ref, land_ref, send_ref, recv_ref, k, me, scatter, True)
            cp.wait_send()
            cp.wait_recv()

    return pl.pallas_call(
        body, name=name,
        out_shape=(pltpu.HBM(src_thru.shape, src_thru.dtype), pltpu.HBM(land_thru.shape, land_thru.dtype)),
        in_specs=(_ANY, _ANY, _SEM, _SEM, _ANY), out_specs=(_ANY, _ANY), input_output_aliases={0: 0, 1: 1},
        compiler_params=pltpu.CompilerParams(has_side_effects=_EFFECT),
    )(src_thru, land_thru, send, recv, after)[1]


def _forward_copy(land, fsend, frecv, k, landed):
    x, y, c = _mesh_pos()
    _, idx = _peer(k | SIBLING if landed else k)
    return pltpu.make_async_remote_copy(src_ref=land.at[idx], dst_ref=land.at[idx], send_sem=fsend.at[k],
                                        recv_sem=frecv.at[k], device_id=(x, y, 1 - c), device_id_type=MESH)


def _gather_forward(handle, after, *, name):
    send, recv, src_thru, land_thru = handle

    def body(src_ref, land_ref, send_ref, recv_ref, after_ref, src_out, land_out, fsend, frecv):
        x, y, c = _mesh_pos()
        me = 4 * x + 2 * y + c
        for k in SAME_CORE:
            _split_copy(src_ref, land_ref, send_ref, recv_ref, k, me, False, True).wait_recv()
            _forward_copy(land_ref, fsend, frecv, k, False).start()

    src2, land2, fsend, frecv = pl.pallas_call(
        body, name=name,
        out_shape=(pltpu.HBM(src_thru.shape, src_thru.dtype), pltpu.HBM(land_thru.shape, land_thru.dtype),
                   pltpu.SemaphoreType.DMA((N_DEV,)), pltpu.SemaphoreType.DMA((N_DEV,))),
        in_specs=(_ANY, _ANY, _SEM, _SEM, _ANY), out_specs=(_ANY, _ANY, _SEM, _SEM), input_output_aliases={0: 0, 1: 1},
        compiler_params=pltpu.CompilerParams(has_side_effects=_EFFECT),
    )(src_thru, land_thru, send, recv, after)
    return (send, recv, src2, land2), (fsend, frecv)


def _gather_wait_two_level(handle, fwd, *, name):
    send, recv, src_thru, land_thru = handle
    fsend, frecv = fwd

    def body(src_ref, land_ref, send_ref, recv_ref, fsend_ref, frecv_ref, src_out, land_out):
        x, y, c = _mesh_pos()
        me = 4 * x + 2 * y + c
        for k in (SIBLING,) + SAME_CORE:
            _split_copy(src_ref, land_ref, send_ref, recv_ref, k, me, False, True).wait_send()
        _split_copy(src_ref, land_ref, send_ref, recv_ref, SIBLING, me, False, True).wait_recv()
        for k in SAME_CORE:
            _forward_copy(land_ref, fsend_ref, frecv_ref, k, False).wait_send()
            _forward_copy(land_ref, fsend_ref, frecv_ref, k, True).wait_recv()

    return pl.pallas_call(
        body, name=name,
        out_shape=(pltpu.HBM(src_thru.shape, src_thru.dtype), pltpu.HBM(land_thru.shape, land_thru.dtype)),
        in_specs=(_ANY, _ANY, _SEM, _SEM, _SEM, _SEM), out_specs=(_ANY, _ANY), input_output_aliases={0: 0, 1: 1},
        compiler_params=pltpu.CompilerParams(has_side_effects=_EFFECT),
    )(src_thru, land_thru, send, recv, fsend, frecv)[1]


def _local_step(x, p, tgt, S, wt, conv, emit):
    T, D = x.shape
    CW = DNW = D // 2
    H = DNW // HEAD
    nA, nD = CW // LANE, DNW // LANE
    qkv_off, z_off, ab_off = 3 * nA, 3 * nA + 3 * nD, 3 * nA + 4 * nD
    alog = jnp.pad(S["a_log"], ((0, 0), (0, LANE - H)))
    dtb = jnp.pad(S["dt_bias"], ((0, 0), (0, LANE - H)))

    h1 = _rms_fwd(x, S["g_mix"], name="rms1_fwd")
    pp = _matmul(p, wt("w_pp", h1), "nn", name="mm_pp", b_shards=True)
    w_in, cv = wt("w_in", pp), conv(pp)
    proj = _matmul(h1, w_in, "nt", name="mm_in")
    y_a = _group_a_fwd(proj, cv["conv_a"], CW, D, name="group_a_fwd")
    qkv = _qkv_fwd(proj, cv["conv_qkv"], qkv_off, H, name="qkv_fwd")
    gamB, bB = _gates_fwd(proj, alog, dtb, ab_off, H, name="gates_fwd")
    u, w, qd, kd, qk, ti, gl = _delta_prep_fwd(qkv, gamB, bB, H, name="delta_prep_fwd")
    o, vn, ss = _delta_scan_fwd(u, w, qd, kd, qk, gl, H, name="delta_scan_fwd")
    ycat = _gated_norm_fwd(o, proj, S["dn_g"], z_off, y_a, name="gated_norm_fwd")
    w_out = wt("w_out", ycat)
    rows = dict(tm=ROW_TILE, tn=D)
    x1, h2 = _matmul(ycat, w_out, "nn", name="mm_out", out_dtypes=(F32, BF16), epilogue=_epi_residual_rms,
                     extras=(x,), vec_extras=(S["g_ffn"],), **rows)
    w_up = wt("w_up", h2)
    up_pre = _matmul(h2, w_up, "nn", name="mm_up", b_shards=True, tn=SHARD_TILE, out_lanes=True)
    act = _ffn_act_fwd(up_pre, cv["conv_ffn"], name="ffn_act_fwd")
    w_down = wt("w_down", act)
    x2 = _matmul(act, w_down, "nn", name="mm_down", epilogue=lambda acc, r: (acc + r,), extras=(x1,), tk=LONG_K)
    h3 = _rms_fwd(x2, S["g_ple"], name="rms3_fwd")
    w_pg = wt("w_pg", h3)

    def ple_epi(acc, x2r, ppr):
        s = jax.nn.sigmoid(acc)
        return x2r + s * ppr, s

    x3, sg = _matmul(h3, w_pg, "nn", name="mm_pg", out_dtypes=(F32, F32), epilogue=ple_epi, extras=(x2, pp), tm=512)
    dx3, dg_final, loss, dpg, dpp = _final_loss(x3, S["g_final"], tgt, pp, sg, name="final_loss")

    G = {"g_final": dg_final}
    tok = emit({"w_pp": _matmul(p, dpp, "tn", name="mm_dwpp", out_dtypes=(BF16,), out_shards=True, tk=LONG_K),
                "w_pg": _matmul(h3, dpg, "tn", name="mm_dwpg", out_dtypes=(BF16,), tk=LONG_K)})
    bwd = dict(out_dtypes=(F32, BF16), epilogue=_epi_rms_bwd(2), n_vec=1, **rows)
    dx2, dx2b, G["g_ple"] = _matmul(dpg, w_pg, "nt", name="mm_dh3", after=tok, extras=(x2, dx3),
                                    vec_extras=(S["g_ple"],), **bwd)
    tok = emit({"w_down": _matmul(act, dx2b, "tn", name="mm_dwdown", out_dtypes=(BF16,), tk=LONG_K)})
    dact = _matmul(dx2b, w_down, "nt", name="mm_dact", after=tok, tn=SHARD_TILE)
    dup, dcf_g, dcf_v = _ffn_act_bwd(up_pre, cv["conv_ffn"], dact, name="ffn_act_bwd")
    G["conv_ffn"] = jnp.concatenate([dcf_g, dcf_v], axis=1)
    tok = emit({"w_up": _matmul(h2, dup, "tn", name="mm_dwup", out_dtypes=(BF16,), b_shards=True, out_shards=True,
                                tn=SHARD_TILE, tk=LONG_K)})
    dh2 = _matmul(dup, w_up, "nt", name="mm_dh2", after=tok, a_shards=True, b_shards=True, tk=2 * SHARD_TILE)
    dx1, dx1b, G["g_ffn"] = _rms_bwd(x1, S["g_ffn"], dh2, dx2, name="rms2_bwd")
    tok = emit({"w_out": _matmul(ycat, dx1b, "tn", name="mm_dwout", out_dtypes=(BF16,), tk=LONG_K)})
    dycat = _matmul(dx1b, w_out, "nt", name="mm_dycat", after=tok)
    do, dz, G["dn_g"] = _gated_norm_bwd(o, proj, S["dn_g"], dycat, z_off, nA, name="gated_norm_bwd")
    du, dw, dqd, dkd, dqk, dgl = _delta_scan_bwd(do, w, qd, kd, vn, qk, gl, ss, H, name="delta_scan_bwd")
    dq, dk, dv, dgB, dbB = _delta_prep_bwd(qkv, gamB, bB, ti, u, w, qk, du, dw, dqd, dkd, dqk, dgl, H,
                                           name="delta_prep_bwd")
    dab, dal, ddt = _gates_bwd(proj, alog, dtb, dgB, dbB, ab_off, H, name="gates_bwd")
    G["a_log"], G["dt_bias"] = dal[:, :H], ddt[:, :H]
    dqkv, G["conv_qkv"] = _qkv_bwd(proj, cv["conv_qkv"], dq, dk, dv, qkv_off, H, name="qkv_bwd")
    dax, dab_, dac, G["conv_a"] = _group_a_bwd(proj, cv["conv_a"], dycat, CW, name="group_a_bwd")
    in_p = w_in.shape[0]
    dproj = jnp.concatenate([dax, dab_, dac, dqkv, dz, dab, jnp.zeros((T, in_p - (ab_off + 1) * LANE), BF16)], axis=1)
    tok = emit({"w_in": _matmul(dproj, h1, "tn", name="mm_dwin", out_dtypes=(BF16,), tk=LONG_K)})
    dh1 = _matmul(dproj, w_in, "nn", name="mm_dh1", after=tok, tk=LONG_K)
    grad_x, _, G["g_mix"] = _rms_bwd(x, S["g_mix"], dh1, dx1, name="rms1_bwd")
    return loss, grad_x, G


def _col_sharded(landed):
    _, R, C = landed.shape
    return jnp.transpose(landed, (1, 0, 2)).reshape(R, N_DEV * C)


def kernel(x, p, norm_mix_g, w_in, conv_a_w, conv_qkv_w, a_log, dt_bias, dn_norm_g, w_out, norm_ffn_g, w_up, conv_ffn_w, w_down, norm_ple_g, w_ple_gate, w_ple_proj, final_norm_g, loss_target, m_norm_mix_g, m_w_in, m_conv_a_w, m_conv_qkv_w, m_a_log, m_dt_bias, m_dn_norm_g, m_w_out, m_norm_ffn_g, m_w_up, m_conv_ffn_w, m_w_down, m_norm_ple_g, m_w_ple_gate, m_w_ple_proj, m_final_norm_g, v_norm_mix_g, v_w_in, v_conv_a_w, v_conv_qkv_w, v_a_log, v_dt_bias, v_dn_norm_g, v_w_out, v_norm_ffn_g, v_w_up, v_conv_ffn_w, v_w_down, v_norm_ple_g, v_w_ple_gate, v_w_ple_proj, v_final_norm_g):
    T, D = x.shape[1], x.shape[2]
    xd, _, cd = _mesh_pos()
    me = 4 * xd + 2 * lax.axis_index("y") + cd

    conv_sh = [conv_a_w[0], conv_qkv_w[0], conv_ffn_w[0]]
    conv_n = [c.size for c in conv_sh]
    pack_rows = -(-sum(conv_n) // LANE)
    conv_pack = jnp.pad(jnp.concatenate([c.reshape(-1) for c in conv_sh]), (0, pack_rows * LANE - sum(conv_n))).reshape(pack_rows, LANE)
    names = ["w_pp", "w_in", "conv", "w_out", "w_up", "w_down", "w_pg"]
    tr_ = lambda t: jnp.swapaxes(t, 1, 2)
    shards = [w_ple_proj[0].astype(BF16), w_in[0].T.astype(BF16), conv_pack, w_out[0].astype(BF16), w_up[0].astype(BF16),
              w_down[0].astype(BF16), w_ple_gate[0].astype(BF16)]
    empty_slots = lambda blocks: [lax.empty((N_DEV,) + tuple(b.shape), b.dtype) for b in blocks]
    handles, tok0 = _split_start(shards, empty_slots(shards), False, name="gather_start",
                                 relations=[(SIBLING,) + SAME_CORE if nm == "w_in" else ALL_PEERS for nm in names])
    handle = dict(zip(names, handles))
    own = dict(zip(names, shards))
    in_cols = N_DEV * w_in.shape[2]
    in_p = (in_cols // LANE) * LANE + AB_PAD
    in_place = {"w_up", "w_pp"}

    def gathered(name, after):
        if name == "w_in":
            passed, fwd = _gather_forward(handle[name], after, name="gather_forward_w_in")
            landed = _gather_wait_two_level(passed, fwd, name="gather_wait_w_in")
        else:
            landed = _split_wait(handle[name], after, False, name="gather_wait_" + name)
        return lax.dynamic_update_index_in_dim(landed, own[name], me, 0)

    def wt(name, after):
        landed = gathered(name, after)
        if name in in_place:
            return landed
        full = landed.reshape(-1, D)
        return jnp.pad(full, ((0, in_p - in_cols), (0, 0))) if name == "w_in" else full

    def conv(after):
        flat = gathered("conv", after).reshape(N_DEV, pack_rows * LANE)
        out, o_ = {}, 0
        for nm, c, n_ in zip(("conv_a", "conv_qkv", "conv_ffn"), conv_sh, conv_n):
            out[nm] = _col_sharded(flat[:, o_:o_ + n_].reshape((N_DEV,) + c.shape))
            o_ += n_
        return out

    pending, mine = {}, {}

    def emit(grads):
        parts = [g if nm in in_place else (g[:in_cols] if nm == "w_in" else g).reshape(N_DEV, -1, D)
                 for nm, g in grads.items()]
        hs, tok = _split_start(parts, empty_slots([q[0] for q in parts]), True, name="scatter_start_" + "_".join(grads))
        pending.update(zip(grads, hs))
        mine.update({nm: lax.dynamic_index_in_dim(q, me, 0, keepdims=False) for nm, q in zip(grads, parts)})
        return tok

    S = {
        "g_mix": norm_mix_g + tok0[0, 0], "a_log": a_log, "dt_bias": dt_bias, "dn_g": dn_norm_g, "g_ffn": norm_ffn_g,
        "g_ple": norm_ple_g, "g_final": final_norm_g.reshape(1, D),
    }

    loss_v, grad_x, G = _local_step(x[0], p[0, 0], loss_target[0], S, wt, conv, emit)
    loss = lax.psum(loss_v[0, 0], ("x", "y", "c"))

    small_names = ["g_mix", "g_ffn", "g_ple", "g_final", "dn_g", "a_log", "dt_bias", "conv_a", "conv_qkv", "conv_ffn"]
    small_rows, pieces = [], []
    for nm in small_names:
        g_ = G[nm].reshape(-1)
        r_ = -(-g_.size // (8 * LANE)) * 8
        small_rows.append(r_)
        pieces.append(jnp.pad(g_, (0, r_ * LANE - g_.size)).reshape(r_, LANE))
    landed = {nm: _split_wait(h_, grad_x, True, name="scatter_wait_" + nm) for nm, h_ in pending.items() if nm != "w_in"}

    def adam(parts, w_, m_, v_, nm, own_=None):
        shp = w_.shape
        w2, m2, v2 = (t.reshape(parts.shape[1:]) for t in (w_, m_, v_))
        kw = {} if own_ is None else {"own": own_, "me": me.astype(jnp.int32).reshape(1)}
        return tuple(t.reshape(shp) for t in _adam(parts, w2, m2, v2, name="adam_" + nm, **kw))

    big = {
        "w_up": adam(landed["w_up"], w_up, m_w_up, v_w_up, "w_up", mine["w_up"]),
        "w_down": adam(landed["w_down"], w_down, m_w_down, v_w_down, "w_down", mine["w_down"]),
        "w_out": adam(landed["w_out"], w_out, m_w_out, v_w_out, "w_out", mine["w_out"]),
        "w_pg": adam(landed["w_pg"], w_ple_gate, m_w_ple_gate, v_w_ple_gate, "w_ple_gate", mine["w_pg"]),
        "w_pp": adam(landed["w_pp"], w_ple_proj, m_w_ple_proj, v_w_ple_proj, "w_ple_proj", mine["w_pp"]),
    }
    first = lambda t: lax.slice(t, (0,) * t.ndim, (1,) * t.ndim).reshape(1)
    big_done = sum(first(r[1]) for r in big.values())
    (small_l,) = _exchange([jnp.concatenate(pieces, axis=0)], False, name="gather_small_grads", after=big_done)

    def small_parts(nm):
        i = small_names.index(nm)
        r0 = sum(small_rows[:i])
        shp = G[nm].shape
        return small_l[:, r0:r0 + small_rows[i], :].reshape(N_DEV, -1)[:, :G[nm].size].reshape((N_DEV,) + shp)

    def conv_parts(nm, shard):
        full = small_parts(nm)
        C = shard.shape[-1]
        return lax.dynamic_slice_in_dim(full, me * C, C, axis=2)

    res = [
        adam(small_parts("g_mix"), norm_mix_g, m_norm_mix_g, v_norm_mix_g, "norm_mix_g"),
        None,
        adam(conv_parts("conv_a", conv_a_w), conv_a_w, m_conv_a_w, v_conv_a_w, "conv_a_w"),
        adam(conv_parts("conv_qkv", conv_qkv_w), conv_qkv_w, m_conv_qkv_w, v_conv_qkv_w, "conv_qkv_w"),
        adam(small_parts("a_log"), a_log, m_a_log, v_a_log, "a_log"),
        adam(small_parts("dt_bias"), dt_bias, m_dt_bias, v_dt_bias, "dt_bias"),
        adam(small_parts("dn_g"), dn_norm_g, m_dn_norm_g, v_dn_norm_g, "dn_norm_g"),
        big["w_out"],
        adam(small_parts("g_ffn"), norm_ffn_g, m_norm_ffn_g, v_norm_ffn_g, "norm_ffn_g"),
        big["w_up"],
        adam(conv_parts("conv_ffn", conv_ffn_w), conv_ffn_w, m_conv_ffn_w, v_conv_ffn_w, "conv_ffn_w"),
        big["w_down"],
        adam(small_parts("g_ple"), norm_ple_g, m_norm_ple_g, v_norm_ple_g, "norm_ple_g"),
        big["w_pg"],
        big["w_pp"],
        adam(small_parts("g_final"), final_norm_g.reshape(1, D), m_final_norm_g.reshape(1, D),
             v_final_norm_g.reshape(1, D), "final_norm_g"),
    ]
    res[-1] = tuple(t.reshape(D) for t in res[-1])
    landed_in = _split_wait(pending["w_in"], res[10][1], True, name="scatter_wait_w_in")
    res[1] = tuple(tr_(t) for t in adam(landed_in, tr_(w_in), tr_(m_w_in), tr_(v_w_in), "w_in", mine["w_in"]))
    grads, deltas, new_m, new_v = zip(*res)
    return (loss, grad_x[None], *grads, *deltas, *new_m, *new_v)
```

```python
import functools

import jax
import jax.numpy as jnp
from jax import lax
from jax.experimental import pallas as pl
from jax.experimental.pallas import tpu as pltpu

F32 = jnp.float32
BF16 = jnp.bfloat16

EPS = 1e-6
CHUNK = 64
HEAD = 128
LANE = 128
N_DEV = 8
AB_PAD = 512

ADAM_LR = 0.001
ADAM_B1 = 0.9
ADAM_B2 = 0.999
ADAM_EPS = 1e-08
ADAM_WD = 0.01
ADAM_STEP = 10

MESH = pl.DeviceIdType.MESH


def _tile(dim, target, align=LANE):
    if dim <= target:
        return dim
    t = (target // align) * align
    while t > align and dim % t:
        t -= align
    assert dim % t == 0, (dim, target)
    return t


def _params(sem, vmem_mb=48):
    return pltpu.CompilerParams(dimension_semantics=sem, vmem_limit_bytes=vmem_mb << 20)


_DN = {"nn": (((1,), (0,)), ((), ())), "nt": (((1,), (1,)), ((), ())), "tn": (((0,), (0,)), ((), ()))}
LONG_K = 4096
SHARD_TILE = 1408


def _matmul(a, b, mode, *, name, out_dtypes=(F32,), epilogue=None, extras=(), vec_extras=(), n_vec=0, after=None,
            a_shards=False, b_shards=False, out_shards=False, out_lanes=False, tm=1024, tn=1024, tk=2048):
    shard_w = b.shape[2] if b_shards else None
    if b_shards:
        b_rows, b_cols = b.shape[1], b.shape[0] * shard_w
    else:
        b_rows, b_cols = b.shape
    a_w = a.shape[2] if a_shards else None
    a_dims = (a.shape[1], a.shape[0] * a_w) if a_shards else a.shape
    if mode == "nn":
        (M, K), (K2, N) = a_dims, (b_rows, b_cols)
    elif mode == "nt":
        (M, K), (N, K2) = a_dims, (b_rows, b_cols)
    else:
        (K, M), (K2, N) = a_dims, (b_rows, b_cols)
    assert K == K2, (name, a.shape, b.shape)
    tm = _tile(M, tm)
    n_dims = [N] + ([shard_w] if (b_shards and mode != "nt") else []) + ([N // N_DEV] if out_shards else [])
    tn = _tile(min(n_dims), tn)
    assert all(d % tn == 0 for d in n_dims), (name, n_dims, tn)
    grp = 1
    if b_shards and mode == "nt":
        grp = max(g for g in (1, 2, 4, 8) if g <= max(1, tk // shard_w) and (a_w is None or a_w % (g * shard_w) == 0))
    k_dims = [K] + ([shard_w] if (b_shards and mode == "nt") else []) + ([a_w] if a_shards else [])
    tk = grp * shard_w if grp > 1 else _tile(min(k_dims), tk)
    assert K % tk == 0, (name, K, tk)
    nk = K // tk
    n_ex, n_out = len(extras) + len(vec_extras), len(out_dtypes)
    assert n_vec == 0 or tn == N, (name, tn, N)
    dn = _DN[mode]

    n_tok = 0 if after is None else 1

    def body(a_ref, b_ref, *rest):
        rest = rest[n_tok:]
        ex_refs, out_refs, vec_refs = rest[:n_ex], rest[n_ex:n_ex + n_out], rest[n_ex + n_out:n_ex + n_out + n_vec]
        if grp > 1:
            part = sum(lax.dot_general(a_ref[:, s * shard_w:(s + 1) * shard_w].astype(BF16), b_ref[s].astype(BF16), dn,
                                       preferred_element_type=F32) for s in range(grp))
        else:
            part = lax.dot_general(a_ref[...].astype(BF16), b_ref[...].astype(BF16), dn, preferred_element_type=F32)
        first_rows = pl.program_id(0) == 0

        def finish(res):
            outs = (res,) if epilogue is None else epilogue(res, *[e[...] for e in ex_refs])
            for o_ref, val in zip(out_refs, outs[:n_out]):
                if out_lanes:
                    for c in range(tn // LANE):
                        o_ref[c] = val[:, c * LANE:(c + 1) * LANE].astype(o_ref.dtype)
                else:
                    o_ref[...] = val.astype(o_ref.dtype)
            for v_ref, val in zip(vec_refs, outs[n_out:]):
                @pl.when(first_rows)
                def _(v_ref=v_ref, val=val):
                    v_ref[...] = val

                @pl.when(jnp.logical_not(first_rows))
                def _(v_ref=v_ref, val=val):
                    v_ref[...] += val

        if nk == 1:
            finish(part)
            return
        acc, k = rest[-1], pl.program_id(2)

        @pl.when(k == 0)
        def _():
            acc[...] = part

        @pl.when(k > 0)
        def _():
            acc[...] += part

        @pl.when(k == nk - 1)
        def _():
            finish(acc[...])

    if a_shards:
        assert mode == "nt" and a_w % tk == 0, (name, mode, a_w, tk)
        per_a = a_w // tk
        a_spec = pl.BlockSpec((None, tm, tk), lambda i, j, k: (lax.div(k, per_a), i, lax.rem(k, per_a)))
    else:
        a_spec = pl.BlockSpec((tk, tm), lambda i, j, k: (k, i)) if mode == "tn" else pl.BlockSpec((tm, tk), lambda i, j, k: (i, k))
    if b_shards and mode != "nt":
        per = shard_w // tn
        b_spec = pl.BlockSpec((None, tk, tn), lambda i, j, k: (lax.div(j, per), k, lax.rem(j, per)))
    elif b_shards and grp > 1:
        b_spec = pl.BlockSpec((grp, tn, shard_w), lambda i, j, k: (k, j, 0))
    elif b_shards:
        per = shard_w // tk
        b_spec = pl.BlockSpec((None, tn, tk), lambda i, j, k: (lax.div(k, per), j, lax.rem(k, per)))
    else:
        b_spec = pl.BlockSpec((tn, tk), lambda i, j, k: (j, k)) if mode == "nt" else pl.BlockSpec((tk, tn), lambda i, j, k: (k, j))
    mn_spec = pl.BlockSpec((tm, tn), lambda i, j, k: (i, j))
    vec_spec = pl.BlockSpec((1, tn), lambda i, j, k: (0, j))
    if out_shards:
        assert not extras
        per_o = (N // N_DEV) // tn
        out_spec = pl.BlockSpec((None, tm, tn), lambda i, j, k: (lax.div(j, per_o), i, lax.rem(j, per_o)))
        out_dims = (N_DEV, M, N // N_DEV)
    elif out_lanes:
        assert not extras
        out_spec = pl.BlockSpec((tn // LANE, tm, LANE), lambda i, j, k: (j, i, 0))
        out_dims = (N // LANE, M, LANE)
    else:
        out_spec, out_dims = mn_spec, (M, N)
    outs = pl.pallas_call(
        body, name=name, grid=(M // tm, N // tn, nk),
        in_specs=[a_spec, b_spec] + [pl.BlockSpec((8, LANE), lambda i, j, k: (0, 0))] * n_tok
        + [mn_spec] * len(extras) + [vec_spec] * len(vec_extras),
        out_specs=[out_spec] * n_out + [vec_spec] * n_vec,
        out_shape=[jax.ShapeDtypeStruct(out_dims, dt) for dt in out_dtypes] + [jax.ShapeDtypeStruct((1, N), F32)] * n_vec,
        scratch_shapes=[pltpu.VMEM((tm, tn), F32)] if nk > 1 else [],
        compiler_params=_params(("arbitrary" if n_vec else "parallel", "parallel", "arbitrary"), 56),
    )(a, b, *([] if after is None else [after]), *extras, *vec_extras)
    return outs[0] if n_out + n_vec == 1 else outs


def _rms_fwd(x, g, *, name):
    T, D = x.shape
    tr = _tile(T, 512, 8)

    def body(x_ref, g_ref, h_ref):
        xv = x_ref[...]
        r = lax.rsqrt(jnp.mean(xv * xv, axis=-1, keepdims=True) + EPS)
        h_ref[...] = (xv * r * g_ref[...]).astype(h_ref.dtype)

    return pl.pallas_call(
        body, name=name, grid=(T // tr,),
        in_specs=[pl.BlockSpec((tr, D), lambda i: (i, 0)), pl.BlockSpec((1, D), lambda i: (0, 0))],
        out_specs=pl.BlockSpec((tr, D), lambda i: (i, 0)),
        out_shape=jax.ShapeDtypeStruct((T, D), BF16),
        compiler_params=_params(("parallel",)),
    )(x, g)


def _rms_bwd(x, g, dh, dres, *, name):
    T, D = x.shape
    tr = _tile(T, 512, 8)
    epi = _epi_rms_bwd(2)

    def body(x_ref, g_ref, dh_ref, dres_ref, dx_ref, dxb_ref, dg_ref):
        dx, _, dgp = epi(dh_ref[...], x_ref[...], dres_ref[...], g_ref[...])

        @pl.when(pl.program_id(0) == 0)
        def _():
            dg_ref[...] = jnp.zeros_like(dg_ref)

        dg_ref[...] += dgp
        dx_ref[...] = dx
        dxb_ref[...] = dx.astype(dxb_ref.dtype)

    row = pl.BlockSpec((tr, D), lambda i: (i, 0))
    vec = pl.BlockSpec((1, D), lambda i: (0, 0))
    return pl.pallas_call(
        body, name=name, grid=(T // tr,),
        in_specs=[row, vec, row, row], out_specs=[row, row, vec],
        out_shape=[jax.ShapeDtypeStruct((T, D), F32), jax.ShapeDtypeStruct((T, D), BF16), jax.ShapeDtypeStruct((1, D), F32)],
        compiler_params=_params(("arbitrary",)),
    )(x, g, dh, dres)


ROW_TILE = 256


def _epi_residual_rms(acc, res, g):
    xn = acc + res
    r = lax.rsqrt(jnp.mean(xn * xn, axis=-1, keepdims=True) + EPS)
    return xn, xn * r * g


def _epi_rms_bwd(n_copies):
    def epi(dh, x, dres, g):
        r = lax.rsqrt(jnp.mean(x * x, axis=-1, keepdims=True) + EPS)
        xh = x * r
        dxh = dh * g
        dx = dres + r * (dxh - xh * jnp.mean(dxh * xh, axis=-1, keepdims=True))
        return (dx,) * n_copies + (jnp.sum(dh * xh, axis=0, keepdims=True),)
    return epi


def _final_loss(x, g, tgt, pp, sg, *, name):
    T, D = x.shape
    tr = _tile(T, 256, 8)

    def body(x_ref, g_ref, t_ref, pp_ref, sg_ref, dx_ref, dg_ref, loss_ref, dpg_ref, dpp_ref):
        xv = x_ref[...]
        r = lax.rsqrt(jnp.mean(xv * xv, axis=-1, keepdims=True) + EPS)
        xh = xv * r
        gv = g_ref[...]
        err = xh * gv - t_ref[...]

        @pl.when(pl.program_id(0) == 0)
        def _():
            dg_ref[...] = jnp.zeros_like(dg_ref)
            loss_ref[...] = jnp.zeros_like(loss_ref)

        part = 0.5 * jnp.sum(jnp.mean(err * err, axis=-1, keepdims=True), axis=0, keepdims=True)
        loss_ref[...] += jnp.broadcast_to(part, loss_ref.shape)
        dy = err * (1.0 / D)
        dg_ref[...] += jnp.sum(dy * xh, axis=0, keepdims=True)
        dxh = dy * gv
        dx = r * (dxh - xh * jnp.mean(dxh * xh, axis=-1, keepdims=True))
        dx_ref[...] = dx
        s = sg_ref[...]
        dpg_ref[...] = (dx * pp_ref[...] * s * (1.0 - s)).astype(dpg_ref.dtype)
        dpp_ref[...] = (dx * s).astype(dpp_ref.dtype)

    row = pl.BlockSpec((tr, D), lambda i: (i, 0))
    vec = pl.BlockSpec((1, D), lambda i: (0, 0))
    return pl.pallas_call(
        body, name=name, grid=(T // tr,),
        in_specs=[row, vec, row, row, row], out_specs=[row, vec, pl.BlockSpec((1, LANE), lambda i: (0, 0)), row, row],
        out_shape=[jax.ShapeDtypeStruct((T, D), F32), jax.ShapeDtypeStruct((1, D), F32),
                   jax.ShapeDtypeStruct((1, LANE), F32)] + [jax.ShapeDtypeStruct((T, D), BF16)] * 2,
        compiler_params=_params(("arbitrary",)),
    )(x, g, tgt, pp, sg)


ROWS_QKV_FWD, ROWS_QKV_BWD, ROWS_FFN_FWD, ROWS_FFN_BWD, ROWS_GROUP_A = 512, 256, 256, 128, 256


def _ext(ref, r0, T, before, after, RC):
    parts = []
    if before:
        p0 = pl.multiple_of(jnp.maximum(r0 - 8, 0), 8)
        parts.append(jnp.where(r0 > 0, ref[pl.ds(p0, 8), :], 0.0))
    parts.append(ref[pl.ds(r0, RC), :])
    if after:
        n0 = pl.multiple_of(jnp.minimum(r0 + RC, T - 8), 8)
        parts.append(jnp.where(r0 + RC < T, ref[pl.ds(n0, 8), :], 0.0))
    return parts[0] if len(parts) == 1 else jnp.concatenate(parts, axis=0)


def _fold8(x):
    return jnp.sum(x.reshape(x.shape[0] // 8, 8, x.shape[1]), axis=0)


def _win(ref, r0, lo, n, T, RC, edge):
    if not edge:
        return ref[pl.ds(r0 + lo, n), :]
    xx = _ext(ref, r0, T, True, True, RC)
    a = 8 + lo
    return (xx if a == 0 else pltpu.roll(xx, xx.shape[0] - a, 0))[:n, :]


def _taps(ref, w_ref, K, r0, n, T, RC, edge):
    wins = [_win(ref, r0, -(K - 1 - j), n, T, RC, edge) for j in range(K)]
    y = wins[0] * w_ref[0:1, :]
    for j in range(1, K):
        y = y + wins[j] * w_ref[j:j + 1, :]
    return wins, y


def _untaps(scr_ref, val, w_ref, K, RC):
    scr_ref[0:val.shape[0], :] = val
    y = scr_ref[K - 1:K - 1 + RC, :] * w_ref[0:1, :]
    for j in range(1, K):
        s = K - 1 - j
        y = y + scr_ref[s:s + RC, :] * w_ref[j:j + 1, :]
    return y


def _peeled(n_chunks, RC, step, init):
    carry = step(0, init, True)
    if n_chunks > 2:
        carry = lax.fori_loop(1, n_chunks - 1, lambda i, c: step(pl.multiple_of(i * RC, RC), c, False), carry)
    if n_chunks > 1:
        carry = step((n_chunks - 1) * RC, carry, True)
    return carry


def _silu(x):
    return x * jax.nn.sigmoid(x)


def _dsilu(x):
    s = jax.nn.sigmoid(x)
    return s * (1.0 + x * (1.0 - s))


def _col_specs(T, offs):
    return [pl.BlockSpec((T, LANE), functools.partial(lambda o, j: (0, o + j), o)) for o in offs]


def _group_a_fwd(proj, conv_w, CW, out_cols, *, name):
    T = proj.shape[0]
    RC = _tile(T, ROWS_GROUP_A, 8)
    nb = CW // LANE
    K = conv_w.shape[0]

    def body(ax_ref, ab_ref, ac_ref, w_ref, y_ref):
        def step(r0, carry, edge):
            c = None
            for j in range(K):
                lo = -(K - 1 - j)
                t = _win(ac_ref, r0, lo, RC, T, RC, edge) * _win(ax_ref, r0, lo, RC, T, RC, edge) * w_ref[j:j + 1, :]
                c = t if c is None else c + t
            y_ref[pl.ds(r0, RC), :] = (ab_ref[pl.ds(r0, RC), :] * c).astype(y_ref.dtype)
            return carry
        _peeled(T // RC, RC, step, 0)

    return pl.pallas_call(
        body, name=name, grid=(nb,),
        in_specs=_col_specs(T, (0, nb, 2 * nb)) + [pl.BlockSpec((K, LANE), lambda j: (0, j))],
        out_specs=pl.BlockSpec((T, LANE), lambda j: (0, j)),
        out_shape=jax.ShapeDtypeStruct((T, out_cols), BF16), compiler_params=_params(("parallel",)),
    )(proj, proj, proj, conv_w)


def _group_a_bwd(proj, conv_w, dycat, CW, *, name):
    T = proj.shape[0]
    RC = _tile(T, ROWS_GROUP_A, 8)
    nb = CW // LANE
    K = conv_w.shape[0]

    def body(ax_ref, ab_ref, ac_ref, w_ref, dy_ref, dax_ref, dab_ref, dac_ref, dw_ref, scr_ref):
        def step(r0, accs, edge):
            ms = [_win(ac_ref, r0, -(K - 1 - j), RC, T, RC, edge) * _win(ax_ref, r0, -(K - 1 - j), RC, T, RC, edge)
                  for j in range(K)]
            c = ms[0] * w_ref[0:1, :]
            for j in range(1, K):
                c = c + ms[j] * w_ref[j:j + 1, :]
            dy = dy_ref[pl.ds(r0, RC), :]
            dab_ref[pl.ds(r0, RC), :] = (dy * c).astype(dab_ref.dtype)
            dc2 = _win(dy_ref, r0, 0, RC + 8, T, RC, edge) * _win(ab_ref, r0, 0, RC + 8, T, RC, edge)
            dm = _untaps(scr_ref, dc2, w_ref, K, RC)
            dax_ref[pl.ds(r0, RC), :] = (dm * ac_ref[pl.ds(r0, RC), :]).astype(dax_ref.dtype)
            dac_ref[pl.ds(r0, RC), :] = (dm * ax_ref[pl.ds(r0, RC), :]).astype(dac_ref.dtype)
            return tuple(accs[j] + _fold8(dc2[:RC] * ms[j]) for j in range(K))

        accs = _peeled(T // RC, RC, step, tuple(jnp.zeros((8, LANE), F32) for _ in range(K)))
        for j in range(K):
            dw_ref[j:j + 1, :] = jnp.sum(accs[j], axis=0, keepdims=True)

    col = pl.BlockSpec((T, LANE), lambda j: (0, j))
    wsp = pl.BlockSpec((K, LANE), lambda j: (0, j))
    return pl.pallas_call(
        body, name=name, grid=(nb,),
        in_specs=_col_specs(T, (0, nb, 2 * nb)) + [wsp, col],
        out_specs=[col, col, col, wsp],
        out_shape=[jax.ShapeDtypeStruct((T, CW), BF16)] * 3 + [jax.ShapeDtypeStruct((K, CW), F32)],
        scratch_shapes=[pltpu.VMEM((RC + 8, LANE), F32)],
        compiler_params=_params(("parallel",)),
    )(proj, proj, proj, conv_w, dycat)


def _qkv_fwd(proj, conv_w, off, H, *, name):
    T = proj.shape[0]
    RC = _tile(T, ROWS_QKV_FWD, 8)
    nb = 3 * H
    K = conv_w.shape[0]

    def body(x_ref, w_ref, y_ref):
        j = pl.program_id(0)
        is_qk = j < 2 * H
        scale = jnp.where(j < H, HEAD ** -0.5, 1.0).astype(F32)

        def step(r0, carry, edge):
            s = _silu(_taps(x_ref, w_ref, K, r0, RC, T, RC, edge)[1])
            r = lax.rsqrt(jnp.sum(s * s, axis=-1, keepdims=True) + EPS) * scale
            y_ref[pl.ds(r0, RC), :] = s * jnp.where(is_qk, r, 1.0)
            return carry
        _peeled(T // RC, RC, step, 0)

    return pl.pallas_call(
        body, name=name, grid=(nb,),
        in_specs=_col_specs(T, (off,)) + [pl.BlockSpec((K, LANE), lambda j: (0, j))],
        out_specs=pl.BlockSpec((T, LANE), lambda j: (0, j)),
        out_shape=jax.ShapeDtypeStruct((T, nb * LANE), F32), compiler_params=_params(("parallel",)),
    )(proj, conv_w)


def _qkv_bwd(proj, conv_w, dq, dk, dv, off, H, *, name):
    T = proj.shape[0]
    RC = _tile(T, ROWS_QKV_BWD, 8)
    nb = 3 * H
    K = conv_w.shape[0]

    def body(x_ref, w_ref, dq_ref, dk_ref, dv_ref, dx_ref, dw_ref, scr_ref):
        j = pl.program_id(0)
        is_qk = j < 2 * H
        scale = jnp.where(j < H, HEAD ** -0.5, 1.0).astype(F32)

        def step(r0, accs, edge):
            xs, c2 = _taps(x_ref, w_ref, K, r0, RC + 8, T, RC, edge)
            s2 = _silu(c2)
            dn2 = jnp.where(j < H, _win(dq_ref, r0, 0, RC + 8, T, RC, edge),
                            jnp.where(is_qk, _win(dk_ref, r0, 0, RC + 8, T, RC, edge),
                                      _win(dv_ref, r0, 0, RC + 8, T, RC, edge)))
            r = lax.rsqrt(jnp.sum(s2 * s2, axis=-1, keepdims=True) + EPS)
            nh = s2 * r
            dnp = dn2 * scale
            ds_qk = r * (dnp - nh * jnp.sum(dnp * nh, axis=-1, keepdims=True))
            ds2 = jnp.where(is_qk, ds_qk, dn2)
            dc2 = ds2 * _dsilu(c2)
            dx_ref[pl.ds(r0, RC), :] = _untaps(scr_ref, dc2, w_ref, K, RC).astype(dx_ref.dtype)
            return tuple(accs[jj] + _fold8(dc2[:RC] * xs[jj][:RC]) for jj in range(K))

        accs = _peeled(T // RC, RC, step, tuple(jnp.zeros((8, LANE), F32) for _ in range(K)))
        for jj in range(K):
            dw_ref[jj:jj + 1, :] = jnp.sum(accs[jj], axis=0, keepdims=True)

    col = pl.BlockSpec((T, LANE), lambda j: (0, j))
    wsp = pl.BlockSpec((K, LANE), lambda j: (0, j))
    return pl.pallas_call(
        body, name=name, grid=(nb,),
        in_specs=_col_specs(T, (off,)) + [wsp] + [
            pl.BlockSpec((T, LANE), functools.partial(lambda o, j: (0, jnp.clip(j - o, 0, H - 1)), o)) for o in (0, H, 2 * H)],
        out_specs=[col, wsp],
        out_shape=[jax.ShapeDtypeStruct((T, nb * LANE), BF16), jax.ShapeDtypeStruct((K, nb * LANE), F32)],
        scratch_shapes=[pltpu.VMEM((RC + 8, LANE), F32)],
        compiler_params=_params(("parallel",)),
    )(proj, conv_w, dq, dk, dv)


def _softplus(x):
    return jnp.maximum(x, 0.0) + jnp.log(1.0 + jnp.exp(-jnp.abs(x)))


def _gates_fwd(proj, alog, dtb, off, H, *, name):
    T = proj.shape[0]
    tr = _tile(T, 512, CHUNK)

    def body(ab_ref, al_ref, dt_ref, gam_ref, beta_ref):
        ab = ab_ref[...]
        lane = lax.broadcasted_iota(jnp.int32, ab.shape, 1)
        g = -jnp.exp(al_ref[...]) * _softplus(ab + dt_ref[...])
        gb = jnp.where(lane < H, g, jnp.where(lane < 2 * H, jax.nn.sigmoid(ab), 0.0))
        tril = _tri().astype(F32)
        gam = jnp.concatenate([_mm(tril, gb[c * CHUNK:(c + 1) * CHUNK, :], precision=lax.Precision.HIGHEST)
                               for c in range(tr // CHUNK)], axis=0)
        for h in range(H):
            gam_ref[h] = jnp.broadcast_to(gam[:, h:h + 1], (tr, LANE))
            beta_ref[h] = jnp.broadcast_to(gb[:, H + h:H + h + 1], (tr, LANE))

    vec = pl.BlockSpec((1, LANE), lambda i: (0, 0))
    heads = pl.BlockSpec((H, tr, LANE), lambda i: (0, i, 0))
    return pl.pallas_call(
        body, name=name, grid=(T // tr,),
        in_specs=[pl.BlockSpec((tr, LANE), lambda i: (i, off)), vec, vec],
        out_specs=[heads, heads],
        out_shape=[jax.ShapeDtypeStruct((H, T, LANE), F32)] * 2, compiler_params=_params(("parallel",)),
    )(proj, alog, dtb)


def _gates_bwd(proj, alog, dtb, dgamB, dbB, off, H, *, name):
    T = proj.shape[0]
    tr = _tile(T, 512, CHUNK)

    def body(ab_ref, al_ref, dt_ref, dgam_ref, dbeta_ref, dab_ref, dal_ref, ddt_ref):
        ab = ab_ref[...]
        lane = lax.broadcasted_iota(jnp.int32, ab.shape, 1)
        is_g = lane < H
        d = jnp.zeros_like(ab)
        for h in range(H):
            d = jnp.where(lane == h, dgam_ref[h], jnp.where(lane == H + h, dbeta_ref[h], d))
        triu = _tri(upper=True).astype(F32)
        dg = jnp.concatenate([_mm(triu, d[c * CHUNK:(c + 1) * CHUNK, :], precision=lax.Precision.HIGHEST)
                              for c in range(tr // CHUNK)], axis=0)
        z = ab + dt_ref[...]
        A = -jnp.exp(al_ref[...])
        da = dg * A * jax.nn.sigmoid(z)
        beta = jax.nn.sigmoid(ab)
        db = d * beta * (1.0 - beta)
        dab_ref[...] = jnp.where(is_g, da, jnp.where(lane < 2 * H, db, 0.0)).astype(dab_ref.dtype)

        @pl.when(pl.program_id(0) == 0)
        def _():
            dal_ref[...] = jnp.zeros_like(dal_ref)
            ddt_ref[...] = jnp.zeros_like(ddt_ref)

        dal_ref[...] += jnp.sum(jnp.where(is_g, dg * A * _softplus(z), 0.0), axis=0, keepdims=True)
        ddt_ref[...] += jnp.sum(jnp.where(is_g, da, 0.0), axis=0, keepdims=True)

    vec = pl.BlockSpec((1, LANE), lambda i: (0, 0))
    row = pl.BlockSpec((tr, LANE), lambda i: (i, 0))
    heads = pl.BlockSpec((H, tr, LANE), lambda i: (0, i, 0))
    return pl.pallas_call(
        body, name=name, grid=(T // tr,),
        in_specs=[pl.BlockSpec((tr, LANE), lambda i: (i, off)), vec, vec, heads, heads],
        out_specs=[row, vec, vec],
        out_shape=[jax.ShapeDtypeStruct((T, LANE), BF16), jax.ShapeDtypeStruct((1, LANE), F32),
                   jax.ShapeDtypeStruct((1, LANE), F32)],
        compiler_params=_params(("arbitrary",)),
    )(proj, alog, dtb, dgamB, dbB)


def _gated_norm_fwd(o, proj, gn, zoff, ycat, *, name):
    T, W = o.shape
    tr = _tile(T, 512, 8)
    nh_, zblk = W // LANE, (zoff * LANE) // W
    assert zblk * W == zoff * LANE

    def body(o_ref, z_ref, g_ref, ycat_ref, y_ref):
        for h in range(nh_):
            ln = slice(h * LANE, (h + 1) * LANE)
            ov = o_ref[:, ln]
            r = lax.rsqrt(jnp.mean(ov * ov, axis=-1, keepdims=True) + EPS)
            y_ref[:, ln] = (ov * r * g_ref[...] * _silu(z_ref[:, ln])).astype(y_ref.dtype)

    assert ycat.shape == (T, 2 * W), ycat.shape
    blk = pl.BlockSpec((tr, W), lambda i: (i, 0))
    return pl.pallas_call(
        body, name=name, grid=(T // tr,),
        in_specs=[blk, pl.BlockSpec((tr, W), lambda i: (i, zblk)), pl.BlockSpec((1, LANE), lambda i: (0, 0)),
                  pl.BlockSpec(memory_space=pl.ANY)],
        out_specs=pl.BlockSpec((tr, W), lambda i: (i, 1)), out_shape=jax.ShapeDtypeStruct(ycat.shape, ycat.dtype),
        input_output_aliases={3: 0}, compiler_params=_params(("parallel",)),
    )(o, proj, gn, ycat)


def _gated_norm_bwd(o, proj, gn, dycat, zoff, yoff, *, name):
    T, W = o.shape
    tr = _tile(T, 512, 8)
    nh_, zblk, yblk = W // LANE, (zoff * LANE) // W, (yoff * LANE) // W
    assert zblk * W == zoff * LANE and yblk * W == yoff * LANE

    def body(o_ref, z_ref, g_ref, dy_ref, do_ref, dz_ref, dg_ref):
        @pl.when(pl.program_id(0) == 0)
        def _():
            dg_ref[...] = jnp.zeros_like(dg_ref)

        gv = g_ref[...]
        dg = jnp.zeros_like(gv)
        for h in range(nh_):
            ln = slice(h * LANE, (h + 1) * LANE)
            ov, zv, dy = o_ref[:, ln], z_ref[:, ln], dy_ref[:, ln]
            r = lax.rsqrt(jnp.mean(ov * ov, axis=-1, keepdims=True) + EPS)
            nh = ov * r
            s = _silu(zv)
            dg = dg + jnp.sum(dy * nh * s, axis=0, keepdims=True)
            dz_ref[:, ln] = (dy * nh * gv * _dsilu(zv)).astype(dz_ref.dtype)
            dn = dy * gv * s
            do_ref[:, ln] = r * (dn - nh * jnp.mean(dn * nh, axis=-1, keepdims=True))
        dg_ref[...] += dg

    blk = pl.BlockSpec((tr, W), lambda i: (i, 0))
    vec = pl.BlockSpec((1, LANE), lambda i: (0, 0))
    return pl.pallas_call(
        body, name=name, grid=(T // tr,),
        in_specs=[blk, pl.BlockSpec((tr, W), lambda i: (i, zblk)), vec, pl.BlockSpec((tr, W), lambda i: (i, yblk))],
        out_specs=[blk, blk, vec],
        out_shape=[jax.ShapeDtypeStruct((T, W), F32), jax.ShapeDtypeStruct((T, W), BF16),
                   jax.ShapeDtypeStruct((1, LANE), F32)],
        compiler_params=_params(("arbitrary",)),
    )(o, proj, gn, dycat)


def _ffn_act_fwd(up_pre, conv_w, *, name):
    T, F2 = up_pre.shape[1], up_pre.shape[0] * LANE
    RC = _tile(T, ROWS_FFN_FWD, 8)
    nb = F2 // 2 // LANE
    K = conv_w.shape[0]

    def body(g_ref, v_ref, wg_ref, wv_ref, y_ref):
        def step(r0, carry, edge):
            _, gate = _taps(g_ref, wg_ref, K, r0, RC, T, RC, edge)
            _, val = _taps(v_ref, wv_ref, K, r0, RC, T, RC, edge)
            y_ref[pl.ds(r0, RC), :] = (_silu(gate) * val).astype(y_ref.dtype)
            return carry
        _peeled(T // RC, RC, step, 0)

    return pl.pallas_call(
        body, name=name, grid=(nb,),
        in_specs=[pl.BlockSpec((None, T, LANE), lambda j: (j, 0, 0)), pl.BlockSpec((None, T, LANE), lambda j: (nb + j, 0, 0)),
                  pl.BlockSpec((K, LANE), lambda j: (0, j)), pl.BlockSpec((K, LANE), lambda j: (0, nb + j))],
        out_specs=pl.BlockSpec((T, LANE), lambda j: (0, j)),
        out_shape=jax.ShapeDtypeStruct((T, F2 // 2), BF16), compiler_params=_params(("parallel",)),
    )(up_pre, up_pre, conv_w, conv_w)


def _ffn_act_bwd(up_pre, conv_w, dact, *, name):
    T, F2 = up_pre.shape[1], up_pre.shape[0] * LANE
    RC = _tile(T, ROWS_FFN_BWD, 8)
    nb = F2 // 2 // LANE
    K = conv_w.shape[0]

    def body(g_ref, v_ref, wg_ref, wv_ref, da_ref, d_ref, dwg_ref, dwv_ref, sg_ref, sv_ref):
        def step(r0, accs, edge):
            gs, gate2 = _taps(g_ref, wg_ref, K, r0, RC + 8, T, RC, edge)
            vs, val2 = _taps(v_ref, wv_ref, K, r0, RC + 8, T, RC, edge)
            da2 = _win(da_ref, r0, 0, RC + 8, T, RC, edge)
            dgate2 = da2 * val2 * _dsilu(gate2)
            dval2 = da2 * _silu(gate2)
            d_ref[0, pl.ds(r0, RC), :] = _untaps(sg_ref, dgate2, wg_ref, K, RC).astype(d_ref.dtype)
            d_ref[1, pl.ds(r0, RC), :] = _untaps(sv_ref, dval2, wv_ref, K, RC).astype(d_ref.dtype)
            new = []
            for j in range(K):
                new.append(accs[2 * j] + _fold8(dgate2[:RC] * gs[j][:RC]))
                new.append(accs[2 * j + 1] + _fold8(dval2[:RC] * vs[j][:RC]))
            return tuple(new)

        accs = _peeled(T // RC, RC, step, tuple(jnp.zeros((8, LANE), F32) for _ in range(2 * K)))
        for j in range(K):
            dwg_ref[j:j + 1, :] = jnp.sum(accs[2 * j], axis=0, keepdims=True)
            dwv_ref[j:j + 1, :] = jnp.sum(accs[2 * j + 1], axis=0, keepdims=True)

    col = pl.BlockSpec((T, LANE), lambda j: (0, j))
    wsp = pl.BlockSpec((K, LANE), lambda j: (0, j))
    return pl.pallas_call(
        body, name=name, grid=(nb,),
        in_specs=[pl.BlockSpec((None, T, LANE), lambda j: (j, 0, 0)), pl.BlockSpec((None, T, LANE), lambda j: (nb + j, 0, 0)),
                  wsp, pl.BlockSpec((K, LANE), lambda j: (0, nb + j)), col],
        out_specs=[pl.BlockSpec((2, T, LANE), lambda j: (0, 0, j)), wsp, wsp],
        out_shape=[jax.ShapeDtypeStruct((2, T, F2 // 2), BF16)] + [jax.ShapeDtypeStruct((K, F2 // 2), F32)] * 2,
        scratch_shapes=[pltpu.VMEM((RC + 8, LANE), F32)] * 2,
        compiler_params=_params(("parallel",)),
    )(up_pre, up_pre, conv_w, conv_w, dact)


CPB = 8
CPB_SCAN = 4
GRP = 8
HP = lax.Precision.HIGH


def _tri(strict=False, upper=False):
    r = lax.broadcasted_iota(jnp.int32, (CHUNK, CHUNK), 0)
    c = lax.broadcasted_iota(jnp.int32, (CHUNK, CHUNK), 1)
    if upper:
        return c >= r
    return (r > c) if strict else (r >= c)


def _mm(a, b, dn="nn", precision=None):
    precision = HP if precision is None else precision
    return lax.dot_general(a, b, _DN[dn], precision=precision, preferred_element_type=F32)


def _mm16(a, b, dn="nn"):
    return lax.dot_general(a.astype(BF16), b.astype(BF16), _DN[dn], preferred_element_type=F32)


def _each(f, *cols):
    return [f(*xs) for xs in zip(*cols)]


def _decay(gam):
    return jnp.exp(jnp.where(_tri(), gam[:, :CHUNK] - gam.T[:CHUNK, :], -1e30))


def _delta_specs(T, H, cpb):
    rows = cpb * CHUNK
    col = lambda o: pl.BlockSpec((rows, LANE), functools.partial(lambda o, h, n: (n, o + h), o))
    bc = pl.BlockSpec((1, rows, LANE), lambda h, n: (h, n, 0))
    sq = pl.BlockSpec((1, cpb, CHUNK, CHUNK), lambda h, n: (h, n, 0, 0))
    vec = pl.BlockSpec((1, cpb, 1, LANE), lambda h, n: (h, n, 0, 0))
    return col, bc, sq, vec


def _delta_prep_fwd(qkv, gamB, bB, H, *, name):
    T = qkv.shape[0]
    N = T // CHUNK
    cpb = _tile(N, CPB, 8)
    grp = min(GRP, cpb)
    col, bc, sq, vec = _delta_specs(T, H, cpb)

    def body(q_ref, k_ref, v_ref, g_ref, b_ref, u_ref, w_ref, qd_ref, kd_ref, qk_ref, ti_ref, gl_ref):
        eye = (lax.broadcasted_iota(jnp.int32, (CHUNK, CHUNK), 0) == lax.broadcasted_iota(jnp.int32, (CHUNK, CHUNK), 1)).astype(F32)
        strict = _tri(strict=True)
        for c0 in range(0, cpb, grp):
            cs = list(range(c0, c0 + grp))
            rows = [slice(c * CHUNK, (c + 1) * CHUNK) for c in cs]
            q, k, v = ([r_[r, :] for r in rows] for r_ in (q_ref, k_ref, v_ref))
            bb = [b_ref[0, r, :] for r in rows]
            gam = [g_ref[0, r, :] for r in rows]
            D = _each(_decay, gam)
            e = _each(jnp.exp, gam)
            kk = _each(lambda k_: _mm16(k_, k_, "nt"), k)
            X = _each(lambda kk_, D_, b_: -(jnp.where(strict, kk_ * D_, 0.0) * b_[:, :CHUNK]), kk, D, bb)
            R = _each(lambda x: eye + x, X)
            for _ in range(5):
                X = _each(lambda x: _mm(x, x), X)
                R = _each(lambda r, x: r + _mm(r, x), R, X)
            u = _each(lambda r, b_, v_: _mm(r, b_ * v_), R, bb, v)
            w = _each(lambda r, b_, e_, k_: _mm(r, b_ * e_ * k_), R, bb, e, k)
            qk = _each(lambda q_, k_, D_: _mm16(q_, k_, "nt") * D_, q, k, D)
            for i, c in enumerate(cs):
                glast = gam[i][CHUNK - 1:CHUNK, :]
                u_ref[rows[i], :] = u[i]
                w_ref[rows[i], :] = w[i]
                qd_ref[rows[i], :] = e[i] * q[i]
                kd_ref[rows[i], :] = jnp.exp(glast - gam[i]) * k[i]
                qk_ref[0, c] = qk[i]
                ti_ref[0, c] = R[i]
                gl_ref[0, c] = jnp.exp(glast)

    full = jax.ShapeDtypeStruct((T, H * LANE), F32)
    sqs = jax.ShapeDtypeStruct((H, N, CHUNK, CHUNK), F32)
    return pl.pallas_call(
        body, name=name, grid=(H, N // cpb),
        in_specs=[col(0), col(H), col(2 * H), bc, bc],
        out_specs=[col(0)] * 4 + [sq, sq, vec],
        out_shape=[full] * 4 + [sqs, sqs, jax.ShapeDtypeStruct((H, N, 1, LANE), F32)],
        compiler_params=_params(("parallel", "parallel")),
    )(qkv, qkv, qkv, gamB, bB)


def _scan_specs(H, N, cpb, hb, rev):
    nbk = N // cpb
    blk = (lambda n: nbk - 1 - n) if rev else (lambda n: n)
    col = pl.BlockSpec((cpb * CHUNK, hb * LANE), lambda h, n: (blk(n), h))
    sq = pl.BlockSpec((hb, cpb, CHUNK, CHUNK), lambda h, n: (h, blk(n), 0, 0))
    vec = pl.BlockSpec((hb, cpb, 1, LANE), lambda h, n: (h, blk(n), 0, 0))
    st = pl.BlockSpec((hb, cpb, HEAD, HEAD), lambda h, n: (h, blk(n), 0, 0))
    return col, sq, vec, st


def _delta_scan_fwd(u, w, qd, kd, qk, gl, H, *, name):
    T = u.shape[0]
    N = T // CHUNK
    cpb = _tile(N, CPB_SCAN, 4)
    hb = min(GRP, H)
    col, sq, vec, st = _scan_specs(H, N, cpb, hb, False)
    lanes = [slice(j * LANE, (j + 1) * LANE) for j in range(hb)]
    heads = list(range(hb))

    def body(u_ref, w_ref, qd_ref, kd_ref, qk_ref, gl_ref, o_ref, vn_ref, ss_ref, s_scr):
        @pl.when(pl.program_id(1) == 0)
        def _():
            s_scr[...] = jnp.zeros_like(s_scr)

        def step(c, states):
            rows = pl.ds(pl.multiple_of(c * CHUNK, CHUNK), CHUNK)
            S = list(states)
            for j in heads:
                ss_ref[j, c] = S[j]
            wS = _each(lambda ln, s: _mm16(w_ref[rows, ln], s), lanes, S)
            qS = _each(lambda ln, s: _mm16(qd_ref[rows, ln], s), lanes, S)
            vn = _each(lambda ln, ws: u_ref[rows, ln] - ws, lanes, wS)
            o = _each(lambda j, qs, vn_: qs + _mm16(qk_ref[j, c], vn_), heads, qS, vn)
            new = _each(lambda j, ln, s, vn_: s * gl_ref[j, c] + _mm16(kd_ref[rows, ln], vn_, "tn"),
                        heads, lanes, S, vn)
            for j in heads:
                o_ref[rows, lanes[j]] = o[j]
                vn_ref[rows, lanes[j]] = vn[j]
            return tuple(new)
        out = lax.fori_loop(0, cpb, step, tuple(s_scr[j] for j in heads))
        for j in heads:
            s_scr[j] = out[j]

    full = jax.ShapeDtypeStruct((T, H * LANE), F32)
    return pl.pallas_call(
        body, name=name, grid=(H // hb, N // cpb),
        in_specs=[col] * 4 + [sq, vec],
        out_specs=[col, col, st],
        out_shape=[full, full, jax.ShapeDtypeStruct((H, N, HEAD, HEAD), F32)],
        scratch_shapes=[pltpu.VMEM((hb, HEAD, HEAD), F32)],
        compiler_params=_params(("parallel", "arbitrary")),
    )(u, w, qd, kd, qk, gl)


def _delta_scan_bwd(do, w, qd, kd, vn, qk, gl, ss, H, *, name):
    T = do.shape[0]
    N = T // CHUNK
    cpb = _tile(N, CPB_SCAN, 4)
    hb = min(GRP, H)
    col, sq, vec, st = _scan_specs(H, N, cpb, hb, True)
    lanes = [slice(j * LANE, (j + 1) * LANE) for j in range(hb)]
    heads = list(range(hb))

    def body(do_ref, w_ref, qd_ref, kd_ref, vn_ref, qk_ref, gl_ref, ss_ref,
             du_ref, dw_ref, dqd_ref, dkd_ref, dqk_ref, dgl_ref, ds_scr):
        @pl.when(pl.program_id(1) == 0)
        def _():
            ds_scr[...] = jnp.zeros_like(ds_scr)

        def step(i, dstates):
            c = cpb - 1 - i
            rows = pl.ds(pl.multiple_of(c * CHUNK, CHUNK), CHUNK)
            dS = list(dstates)
            S = [ss_ref[j, c] for j in heads]
            dov = [do_ref[rows, ln] for ln in lanes]
            vnv = [vn_ref[rows, ln] for ln in lanes]
            a1 = _each(lambda j, d_: _mm16(qk_ref[j, c], d_, "tn"), heads, dov)
            a2 = _each(lambda ln, ds: _mm16(kd_ref[rows, ln], ds), lanes, dS)
            dvn = _each(lambda x, y: x + y, a1, a2)
            dqd = _each(lambda d_, s: _mm16(d_, s, "nt"), dov, S)
            dkd = _each(lambda v_, ds: _mm16(v_, ds, "nt"), vnv, dS)
            dqk = _each(lambda d_, v_: _mm16(d_, v_, "nt"), dov, vnv)
            dw = _each(lambda dv_, s: -_mm16(dv_, s, "nt"), dvn, S)
            b1 = _each(lambda ln, d_: _mm16(qd_ref[rows, ln], d_, "tn"), lanes, dov)
            b2 = _each(lambda ln, dv_: _mm16(w_ref[rows, ln], dv_, "tn"), lanes, dvn)
            new = _each(lambda j, x, y, ds: x + ds * gl_ref[j, c] - y, heads, b1, b2, dS)
            for j in heads:
                du_ref[rows, lanes[j]] = dvn[j]
                dw_ref[rows, lanes[j]] = dw[j]
                dqd_ref[rows, lanes[j]] = dqd[j]
                dkd_ref[rows, lanes[j]] = dkd[j]
                dqk_ref[j, c] = dqk[j]
                dgl = jnp.sum(jnp.sum(dS[j] * S[j], axis=1, keepdims=True), axis=0, keepdims=True)
                dgl_ref[j, c] = jnp.broadcast_to(dgl, (1, LANE))
            return tuple(new)
        out = lax.fori_loop(0, cpb, step, tuple(ds_scr[j] for j in heads))
        for j in heads:
            ds_scr[j] = out[j]

    full = jax.ShapeDtypeStruct((T, H * LANE), F32)
    return pl.pallas_call(
        body, name=name, grid=(H // hb, N // cpb),
        in_specs=[col] * 5 + [sq, vec, st],
        out_specs=[col] * 4 + [sq, vec],
        out_shape=[full] * 4 + [jax.ShapeDtypeStruct((H, N, CHUNK, CHUNK), F32), jax.ShapeDtypeStruct((H, N, 1, LANE), F32)],
        scratch_shapes=[pltpu.VMEM((hb, HEAD, HEAD), F32)],
        compiler_params=_params(("parallel", "arbitrary")),
    )(do, w, qd, kd, vn, qk, gl, ss)


def _delta_prep_bwd(qkv, gamB, bB, ti, u, w, qk, du, dw, dqd, dkd, dqk, dgl, H, *, name):
    T = qkv.shape[0]
    N = T // CHUNK
    cpb = _tile(N, CPB, 8)
    grp = min(GRP, cpb)
    col, bc, sq, vec = _delta_specs(T, H, cpb)

    def body(q_ref, k_ref, v_ref, g_ref, b_ref, ti_ref, u_ref, w_ref, qk_ref,
             du_ref, dw_ref, dqd_ref, dkd_ref, dqk_ref, dgl_ref,
             dq_ref, dk_ref, dv_ref, dg_ref, db_ref):
        ones = jnp.ones((CHUNK, LANE), F32)
        strict = _tri(strict=True)
        last = lax.broadcasted_iota(jnp.int32, (CHUNK, LANE), 0) == CHUNK - 1
        lsum = lambda x: jnp.sum(x, axis=-1, keepdims=True)
        for c0 in range(0, cpb, grp):
            cs = list(range(c0, c0 + grp))
            rows = [slice(c * CHUNK, (c + 1) * CHUNK) for c in cs]
            ld = lambda r_: [r_[r, :] for r in rows]
            q, k, v, uv, wv, duv, dwv, dqd_v, dkd_v = (ld(r_) for r_ in (q_ref, k_ref, v_ref, u_ref, w_ref, du_ref, dw_ref, dqd_ref, dkd_ref))
            bb = [b_ref[0, r, :] for r in rows]
            gam = [g_ref[0, r, :] for r in rows]
            Ti = [ti_ref[0, c] for c in cs]
            QK = [qk_ref[0, c] for c in cs]
            dqk_v = [dqk_ref[0, c] for c in cs]
            D = _each(_decay, gam)
            e = _each(jnp.exp, gam)
            glast = [g_[CHUNK - 1:CHUNK, :] for g_ in gam]
            eL = _each(lambda gl_, g_: jnp.exp(gl_ - g_), glast, gam)
            kk = _each(lambda k_: _mm16(k_, k_, "nt"), k)
            KKD = _each(lambda kk_, D_: jnp.where(strict, kk_ * D_, 0.0), kk, D)
            dru = _each(lambda t, d_: _mm(t, d_, "tn"), Ti, duv)
            drw = _each(lambda t, d_: _mm(t, d_, "tn"), Ti, dwv)
            l1 = _each(lambda a, b: _mm(a, b, "nt"), dru, uv)
            l2 = _each(lambda a, b: _mm(a, b, "nt"), drw, wv)
            dL = _each(lambda a, b: jnp.where(strict, -(a + b), 0.0), l1, l2)
            Mm = _each(lambda dl, b_: dl * b_[:, :CHUNK], dL, bb)
            dKK = _each(lambda m_, D_: m_ * D_, Mm, D)
            dQK = _each(lambda a, D_: a * D_, dqk_v, D)
            P = _each(lambda m_, kkd, a, qk_: m_ * kkd + a * qk_, Mm, KKD, dqk_v, QK)
            q1 = _each(lambda a, k_: _mm16(a, k_), dQK, k)
            k1 = _each(lambda a, q_: _mm16(a, q_, "tn"), dQK, q)
            k2 = _each(lambda a, k_: _mm16(a, k_), dKK, k)
            k3 = _each(lambda a, k_: _mm16(a, k_, "tn"), dKK, k)
            s1 = _each(lambda dl, kkd: _mm(dl * kkd, ones), dL, KKD)
            p1 = _each(lambda p_: _mm(p_, ones), P)
            p2 = _each(lambda p_: _mm(p_, ones, "tn"), P)
            for i, c in enumerate(cs):
                r = rows[i]
                bek = bb[i] * e[i]
                kdv = eL[i] * k[i]
                dq_ref[r, :] = q1[i] + e[i] * dqd_v[i]
                dk_ref[r, :] = k1[i] + k2[i] + k3[i] + bek * drw[i] + eL[i] * dkd_v[i]
                dv_ref[r, :] = bb[i] * dru[i]
                db_ref[0, r, :] = s1[i] + lsum(dru[i] * v[i]) + lsum(drw[i] * e[i] * k[i])
                dgam = (p1[i] - p2[i] + lsum(drw[i] * bek * k[i]) + lsum(dqd_v[i] * e[i] * q[i])
                        - lsum(dkd_v[i] * kdv))
                xlast = jnp.sum(lsum(dkd_v[i] * kdv), axis=0, keepdims=True) + jnp.exp(glast[i]) * dgl_ref[0, c]
                dg_ref[0, r, :] = dgam + jnp.where(last, xlast, 0.0)

    full = jax.ShapeDtypeStruct((T, H * LANE), F32)
    bcs = jax.ShapeDtypeStruct((H, T, LANE), F32)
    return pl.pallas_call(
        body, name=name, grid=(H, N // cpb),
        in_specs=[col(0), col(H), col(2 * H), bc, bc, sq, col(0), col(0), sq, col(0), col(0), col(0), col(0), sq, vec],
        out_specs=[col(0), col(0), col(0), bc, bc],
        out_shape=[full, full, full, bcs, bcs],
        compiler_params=_params(("parallel", "parallel")),
    )(qkv, qkv, qkv, gamB, bB, ti, u, w, qk, du, dw, dqd, dkd, dqk, dgl)


def _adam(parts, w, m, v, *, name, own=None, me=None):
    P, R, C = parts.shape
    if R > 256 and R % 8:
        tr, tc = R, _tile(C, 256)
    else:
        tr, tc = _tile(R, 256, 8), C
    n_own = 0 if own is None else 2

    def body(*refs):
        p_ref, w_ref, m_ref, v_ref, g_ref, d_ref, nm_ref, nv_ref = refs[n_own:]
        g = None
        for i in range(P):
            t = p_ref[i].astype(F32)
            if n_own:
                t = jnp.where(refs[0][0] == i, refs[1][...].astype(F32), t)
            g = t if g is None else g + t
        mn = ADAM_B1 * m_ref[...] + (1.0 - ADAM_B1) * g
        vn = ADAM_B2 * v_ref[...] + (1.0 - ADAM_B2) * (g * g)
        m_hat = mn / (1.0 - ADAM_B1 ** ADAM_STEP)
        v_hat = vn / (1.0 - ADAM_B2 ** ADAM_STEP)
        g_ref[...] = g
        d_ref[...] = -ADAM_LR * (m_hat / (jnp.sqrt(v_hat) + ADAM_EPS) + ADAM_WD * w_ref[...])
        nm_ref[...] = mn
        nv_ref[...] = vn

    blk = pl.BlockSpec((tr, tc), lambda i, j: (i, j))
    return pl.pallas_call(
        body, name=name, grid=(R // tr, C // tc),
        in_specs=[pl.BlockSpec(memory_space=pltpu.SMEM), blk][:n_own] + [pl.BlockSpec((P, tr, tc), lambda i, j: (0, i, j)), blk, blk, blk],
        out_specs=[blk] * 4, out_shape=[jax.ShapeDtypeStruct((R, C), F32)] * 4,
        compiler_params=_params(("parallel", "parallel")),
    )(*([me, own] if n_own else []), parts, w, m, v)


def _mesh_pos():
    return lax.axis_index("x"), lax.axis_index("y"), lax.axis_index("c")


def _peer(k):
    x, y, c = _mesh_pos()
    px, py, pc = x ^ ((k >> 2) & 1), y ^ ((k >> 1) & 1), c ^ (k & 1)
    return (px, py, pc), 4 * px + 2 * py + pc


def _exchange(arrays, scatter, *, name, after=None):
    n = len(arrays)
    n_in = n if after is None else n + 1
    blocks = [a.shape[1:] if scatter else a.shape for a in arrays]

    def body(*refs):
        srcs, dsts = refs[:n], refs[n_in:n_in + n]
        send_sems, recv_sems, local_sems = refs[n_in + n:]
        x, y, c = _mesh_pos()
        me = 4 * x + 2 * y + c
        local, sends = [], []
        for a in range(n):
            cp = pltpu.make_async_copy(srcs[a].at[me] if scatter else srcs[a], dsts[a].at[me], local_sems.at[a])
            cp.start()
            local.append(cp)
            for k in range(1, N_DEV):
                dev, idx = _peer(k)
                cp = pltpu.make_async_remote_copy(
                    src_ref=srcs[a].at[idx] if scatter else srcs[a], dst_ref=dsts[a].at[me],
                    send_sem=send_sems.at[a * N_DEV + k], recv_sem=recv_sems.at[a * N_DEV + k],
                    device_id=dev, device_id_type=MESH)
                cp.start()
                sends.append(cp)
        for a in range(n):
            for k in range(1, N_DEV):
                dev, idx = _peer(k)
                pltpu.make_async_remote_copy(
                    src_ref=srcs[a].at[idx] if scatter else srcs[a], dst_ref=dsts[a].at[idx],
                    send_sem=send_sems.at[a * N_DEV + k], recv_sem=recv_sems.at[a * N_DEV + k],
                    device_id=dev, device_id_type=MESH).wait_recv()
        for cp in sends:
            cp.wait_send()
        for cp in local:
            cp.wait()

    anyspec = pl.BlockSpec(memory_space=pl.ANY)
    return pl.pallas_call(
        body, name=name, in_specs=[anyspec] * n_in, out_specs=[anyspec] * n,
        out_shape=[jax.ShapeDtypeStruct((N_DEV,) + tuple(b), a.dtype) for a, b in zip(arrays, blocks)],
        scratch_shapes=[pltpu.SemaphoreType.DMA((n * N_DEV,)), pltpu.SemaphoreType.DMA((n * N_DEV,)),
                        pltpu.SemaphoreType.DMA((n,))],
    )(*arrays, *([] if after is None else [after]))


_ANY = pl.BlockSpec(memory_space=pl.ANY)
_SEM = pl.BlockSpec(memory_space=pltpu.SEMAPHORE)
_EFFECT = pltpu.SideEffectType.DATAFLOW_SIDE_EFFECTING


def _in_hbm(a):
    return pltpu.with_memory_space_constraint(a, pltpu.HBM)


def _split_copy(src, land, send, recv, k, me, scatter, landed):
    dev, idx = _peer(k)
    return pltpu.make_async_remote_copy(
        src_ref=src.at[idx] if scatter else src, dst_ref=land.at[idx if landed else me],
        send_sem=send.at[k], recv_sem=recv.at[k], device_id=dev, device_id_type=MESH)


ALL_PEERS = tuple(range(1, N_DEV))
SIBLING = 1
SAME_CORE = (2, 4, 6)


def _split_start(srcs, lands, scatter, *, name, relations=None):
    n = len(srcs)
    relations = relations or [ALL_PEERS] * n

    def body(*refs):
        src, land, send, recv, token = refs[:n], refs[n:2 * n], refs[2 * n:3 * n], refs[3 * n:4 * n], refs[-1]
        x, y, c = _mesh_pos()
        me = 4 * x + 2 * y + c
        for a in range(n):
            for k in relations[a]:
                _split_copy(src[a], land[a], send[a], recv[a], k, me, scatter, False).start()
        token[...] = jnp.zeros_like(token)

    outs = pl.pallas_call(
        body, name=name,
        out_shape=[pltpu.SemaphoreType.DMA((N_DEV,))] * (2 * n) + [pltpu.HBM(t.shape, t.dtype) for t in list(srcs) + list(lands)]
        + [jax.ShapeDtypeStruct((8, LANE), F32)],
        in_specs=[_ANY] * (2 * n), out_specs=[_SEM] * (2 * n) + [_ANY] * (2 * n) + [pl.BlockSpec(memory_space=pltpu.VMEM)],
        input_output_aliases={i: 2 * n + i for i in range(2 * n)},
        compiler_params=pltpu.CompilerParams(has_side_effects=_EFFECT),
    )(*[_in_hbm(t) for t in list(srcs) + list(lands)])
    handles = [(outs[a], outs[n + a], outs[2 * n + a], outs[3 * n + a]) for a in range(n)]
    return handles, outs[-1]


def _split_wait(handle, after, scatter, *, name):
    send, recv, src_thru, land_thru = handle

    def body(src_ref, land_ref, send_ref, recv_ref, after_ref, src_out, land_out):
        x, y, c = _mesh_pos()
        me = 4 * x + 2 * y + c
        for k in range(1, N_DEV):
            cp = _split_copy(src_ref, land_ref, send_ref, recv_ref, k, me, scatter, True)
            cp.wait_send()
            cp.wait_recv()

    return pl.pallas_call(
        body, name=name,
        out_shape=(pltpu.HBM(src_thru.shape, src_thru.dtype), pltpu.HBM(land_thru.shape, land_thru.dtype)),
        in_specs=(_ANY, _ANY, _SEM, _SEM, _ANY), out_specs=(_ANY, _ANY), input_output_aliases={0: 0, 1: 1},
        compiler_params=pltpu.CompilerParams(has_side_effects=_EFFECT),
    )(src_thru, land_thru, send, recv, after)[1]


def _forward_copy(land, fsend, frecv, k, landed):
    x, y, c = _mesh_pos()
    _, idx = _peer(k | SIBLING if landed else k)
    return pltpu.make_async_remote_copy(src_ref=land.at[idx], dst_ref=land.at[idx], send_sem=fsend.at[k],
                                        recv_sem=frecv.at[k], device_id=(x, y, 1 - c), device_id_type=MESH)


def _gather_forward(handle, after, *, name):
    send, recv, src_thru, land_thru = handle

    def body(src_ref, land_ref, send_ref, recv_ref, after_ref, src_out, land_out, fsend, frecv):
        x, y, c = _mesh_pos()
        me = 4 * x + 2 * y + c
        for k in SAME_CORE:
            _split_copy(src_ref, land_ref, send_ref, recv_ref, k, me, False, True).wait_recv()
            _forward_copy(land_ref, fsend, frecv, k, False).start()

    src2, land2, fsend, frecv = pl.pallas_call(
        body, name=name,
        out_shape=(pltpu.HBM(src_thru.shape, src_thru.dtype), pltpu.HBM(land_thru.shape, land_thru.dtype),
                   pltpu.SemaphoreType.DMA((N_DEV,)), pltpu.SemaphoreType.DMA((N_DEV,))),
        in_specs=(_ANY, _ANY, _SEM, _SEM, _ANY), out_specs=(_ANY, _ANY, _SEM, _SEM), input_output_aliases={0: 0, 1: 1},
        compiler_params=pltpu.CompilerParams(has_side_effects=_EFFECT),
    )(src_thru, land_thru, send, recv, after)
    return (send, recv, src2, land2), (fsend, frecv)


def _gather_wait_two_level(handle, fwd, *, name):
    send, recv, src_thru, land_thru = handle
    fsend, frecv = fwd

    def body(src_ref, land_ref, send_ref, recv_ref, fsend_ref, frecv_ref, src_out, land_out):
        x, y, c = _mesh_pos()
        me = 4 * x + 2 * y + c
        for k in (SIBLING,) + SAME_CORE:
            _split_copy(src_ref, land_ref, send_ref, recv_ref, k, me, False, True).wait_send()
        _split_copy(src_ref, land_ref, send_ref, recv_ref, SIBLING, me, False, True).wait_recv()
        for k in SAME_CORE:
            _forward_copy(land_ref, fsend_ref, frecv_ref, k, False).wait_send()
            _forward_copy(land_ref, fsend_ref, frecv_ref, k, True).wait_recv()

    return pl.pallas_call(
        body, name=name,
        out_shape=(pltpu.HBM(src_thru.shape, src_thru.dtype), pltpu.HBM(land_thru.shape, land_thru.dtype)),
        in_specs=(_ANY, _ANY, _SEM, _SEM, _SEM, _SEM), out_specs=(_ANY, _ANY), input_output_aliases={0: 0, 1: 1},
        compiler_params=pltpu.CompilerParams(has_side_effects=_EFFECT),
    )(src_thru, land_thru, send, recv, fsend, frecv)[1]


def _local_step(x, p, tgt, S, wt, conv, emit):
    T, D = x.shape
    CW = DNW = D // 2
    H = DNW // HEAD
    nA, nD = CW // LANE, DNW // LANE
    qkv_off, z_off, ab_off = 3 * nA, 3 * nA + 3 * nD, 3 * nA + 4 * nD
    alog = jnp.pad(S["a_log"], ((0, 0), (0, LANE - H)))
    dtb = jnp.pad(S["dt_bias"], ((0, 0), (0, LANE - H)))

    h1 = _rms_fwd(x, S["g_mix"], name="rms1_fwd")
    pp = _matmul(p, wt("w_pp", h1), "nn", name="mm_pp", b_shards=True)
    w_in, cv = wt("w_in", pp), conv(pp)
    proj = _matmul(h1, w_in, "nt", name="mm_in")
    y_a = _group_a_fwd(proj, cv["conv_a"], CW, D, name="group_a_fwd")
    qkv = _qkv_fwd(proj, cv["conv_qkv"], qkv_off, H, name="qkv_fwd")
    gamB, bB = _gates_fwd(proj, alog, dtb, ab_off, H, name="gates_fwd")
    u, w, qd, kd, qk, ti, gl = _delta_prep_fwd(qkv, gamB, bB, H, name="delta_prep_fwd")
    o, vn, ss = _delta_scan_fwd(u, w, qd, kd, qk, gl, H, name="delta_scan_fwd")
    ycat = _gated_norm_fwd(o, proj, S["dn_g"], z_off, y_a, name="gated_norm_fwd")
    w_out = wt("w_out", ycat)
    rows = dict(tm=ROW_TILE, tn=D)
    x1, h2 = _matmul(ycat, w_out, "nn", name="mm_out", out_dtypes=(F32, BF16), epilogue=_epi_residual_rms,
                     extras=(x,), vec_extras=(S["g_ffn"],), **rows)
    w_up = wt("w_up", h2)
    up_pre = _matmul(h2, w_up, "nn", name="mm_up", b_shards=True, tn=SHARD_TILE, out_lanes=True)
    act = _ffn_act_fwd(up_pre, cv["conv_ffn"], name="ffn_act_fwd")
    w_down = wt("w_down", act)
    x2 = _matmul(act, w_down, "nn", name="mm_down", epilogue=lambda acc, r: (acc + r,), extras=(x1,), tk=LONG_K)
    h3 = _rms_fwd(x2, S["g_ple"], name="rms3_fwd")
    w_pg = wt("w_pg", h3)

    def ple_epi(acc, x2r, ppr):
        s = jax.nn.sigmoid(acc)
        return x2r + s * ppr, s

    x3, sg = _matmul(h3, w_pg, "nn", name="mm_pg", out_dtypes=(F32, F32), epilogue=ple_epi, extras=(x2, pp), tm=512)
    dx3, dg_final, loss, dpg, dpp = _final_loss(x3, S["g_final"], tgt, pp, sg, name="final_loss")

    G = {"g_final": dg_final}
    tok = emit({"w_pp": _matmul(p, dpp, "tn", name="mm_dwpp", out_dtypes=(BF16,), out_shards=True, tk=LONG_K),
                "w_pg": _matmul(h3, dpg, "tn", name="mm_dwpg", out_dtypes=(BF16,), tk=LONG_K)})
    bwd = dict(out_dtypes=(F32, BF16), epilogue=_epi_rms_bwd(2), n_vec=1, **rows)
    dx2, dx2b, G["g_ple"] = _matmul(dpg, w_pg, "nt", name="mm_dh3", after=tok, extras=(x2, dx3),
                                    vec_extras=(S["g_ple"],), **bwd)
    tok = emit({"w_down": _matmul(act, dx2b, "tn", name="mm_dwdown", out_dtypes=(BF16,), tk=LONG_K)})
    dact = _matmul(dx2b, w_down, "nt", name="mm_dact", after=tok, tn=SHARD_TILE)
    dup, dcf_g, dcf_v = _ffn_act_bwd(up_pre, cv["conv_ffn"], dact, name="ffn_act_bwd")
    G["conv_ffn"] = jnp.concatenate([dcf_g, dcf_v], axis=1)
    tok = emit({"w_up": _matmul(h2, dup, "tn", name="mm_dwup", out_dtypes=(BF16,), b_shards=True, out_shards=True,
                                tn=SHARD_TILE, tk=LONG_K)})
    dh2 = _matmul(dup, w_up, "nt", name="mm_dh2", after=tok, a_shards=True, b_shards=True, tk=2 * SHARD_TILE)
    dx1, dx1b, G["g_ffn"] = _rms_bwd(x1, S["g_ffn"], dh2, dx2, name="rms2_bwd")
    tok = emit({"w_out": _matmul(ycat, dx1b, "tn", name="mm_dwout", out_dtypes=(BF16,), tk=LONG_K)})
    dycat = _matmul(dx1b, w_out, "nt", name="mm_dycat", after=tok)
    do, dz, G["dn_g"] = _gated_norm_bwd(o, proj, S["dn_g"], dycat, z_off, nA, name="gated_norm_bwd")
    du, dw, dqd, dkd, dqk, dgl = _delta_scan_bwd(do, w, qd, kd, vn, qk, gl, ss, H, name="delta_scan_bwd")
    dq, dk, dv, dgB, dbB = _delta_prep_bwd(qkv, gamB, bB, ti, u, w, qk, du, dw, dqd, dkd, dqk, dgl, H,
                                           name="delta_prep_bwd")
    dab, dal, ddt = _gates_bwd(proj, alog, dtb, dgB, dbB, ab_off, H, name="gates_bwd")
    G["a_log"], G["dt_bias"] = dal[:, :H], ddt[:, :H]
    dqkv, G["conv_qkv"] = _qkv_bwd(proj, cv["conv_qkv"], dq, dk, dv, qkv_off, H, name="qkv_bwd")
    dax, dab_, dac, G["conv_a"] = _group_a_bwd(proj, cv["conv_a"], dycat, CW, name="group_a_bwd")
    in_p = w_in.shape[0]
    dproj = jnp.concatenate([dax, dab_, dac, dqkv, dz, dab, jnp.zeros((T, in_p - (ab_off + 1) * LANE), BF16)], axis=1)
    tok = emit({"w_in": _matmul(dproj, h1, "tn", name="mm_dwin", out_dtypes=(BF16,), tk=LONG_K)})
    dh1 = _matmul(dproj, w_in, "nn", name="mm_dh1", after=tok, tk=LONG_K)
    grad_x, _, G["g_mix"] = _rms_bwd(x, S["g_mix"], dh1, dx1, name="rms1_bwd")
    return loss, grad_x, G


def _col_sharded(landed):
    _, R, C = landed.shape
    return jnp.transpose(landed, (1, 0, 2)).reshape(R, N_DEV * C)


def kernel(x, p, norm_mix_g, w_in, conv_a_w, conv_qkv_w, a_log, dt_bias, dn_norm_g, w_out, norm_ffn_g, w_up, conv_ffn_w, w_down, norm_ple_g, w_ple_gate, w_ple_proj, final_norm_g, loss_target, m_norm_mix_g, m_w_in, m_conv_a_w, m_conv_qkv_w, m_a_log, m_dt_bias, m_dn_norm_g, m_w_out, m_norm_ffn_g, m_w_up, m_conv_ffn_w, m_w_down, m_norm_ple_g, m_w_ple_gate, m_w_ple_proj, m_final_norm_g, v_norm_mix_g, v_w_in, v_conv_a_w, v_conv_qkv_w, v_a_log, v_dt_bias, v_dn_norm_g, v_w_out, v_norm_ffn_g, v_w_up, v_conv_ffn_w, v_w_down, v_norm_ple_g, v_w_ple_gate, v_w_ple_proj, v_final_norm_g):
    T, D = x.shape[1], x.shape[2]
    xd, _, cd = _mesh_pos()
    me = 4 * xd + 2 * lax.axis_index("y") + cd

    conv_sh = [conv_a_w[0], conv_qkv_w[0], conv_ffn_w[0]]
    conv_n = [c.size for c in conv_sh]
    pack_rows = -(-sum(conv_n) // LANE)
    conv_pack = jnp.pad(jnp.concatenate([c.reshape(-1) for c in conv_sh]), (0, pack_rows * LANE - sum(conv_n))).reshape(pack_rows, LANE)
    names = ["w_pp", "w_in", "conv", "w_out", "w_up", "w_down", "w_pg"]
    tr_ = lambda t: jnp.swapaxes(t, 1, 2)
    shards = [w_ple_proj[0].astype(BF16), w_in[0].T.astype(BF16), conv_pack, w_out[0].astype(BF16), w_up[0].astype(BF16),
              w_down[0].astype(BF16), w_ple_gate[0].astype(BF16)]
    empty_slots = lambda blocks: [lax.empty((N_DEV,) + tuple(b.shape), b.dtype) for b in blocks]
    handles, tok0 = _split_start(shards, empty_slots(shards), False, name="gather_start",
                                 relations=[(SIBLING,) + SAME_CORE if nm == "w_in" else ALL_PEERS for nm in names])
    handle = dict(zip(names, handles))
    own = dict(zip(names, shards))
    in_cols = N_DEV * w_in.shape[2]
    in_p = (in_cols // LANE) * LANE + AB_PAD
    in_place = {"w_up", "w_pp"}

    def gathered(name, after):
        if name == "w_in":
            passed, fwd = _gather_forward(handle[name], after, name="gather_forward_w_in")
            landed = _gather_wait_two_level(passed, fwd, name="gather_wait_w_in")
        else:
            landed = _split_wait(handle[name], after, False, name="gather_wait_" + name)
        return lax.dynamic_update_index_in_dim(landed, own[name], me, 0)

    def wt(name, after):
        landed = gathered(name, after)
        if name in in_place:
            return landed
        full = landed.reshape(-1, D)
        return jnp.pad(full, ((0, in_p - in_cols), (0, 0))) if name == "w_in" else full

    def conv(after):
        flat = gathered("conv", after).reshape(N_DEV, pack_rows * LANE)
        out, o_ = {}, 0
        for nm, c, n_ in zip(("conv_a", "conv_qkv", "conv_ffn"), conv_sh, conv_n):
            out[nm] = _col_sharded(flat[:, o_:o_ + n_].reshape((N_DEV,) + c.shape))
            o_ += n_
        return out

    pending, mine = {}, {}

    def emit(grads):
        parts = [g if nm in in_place else (g[:in_cols] if nm == "w_in" else g).reshape(N_DEV, -1, D)
                 for nm, g in grads.items()]
        hs, tok = _split_start(parts, empty_slots([q[0] for q in parts]), True, name="scatter_start_" + "_".join(grads))
        pending.update(zip(grads, hs))
        mine.update({nm: lax.dynamic_index_in_dim(q, me, 0, keepdims=False) for nm, q in zip(grads, parts)})
        return tok

    S = {
        "g_mix": norm_mix_g + tok0[0, 0], "a_log": a_log, "dt_bias": dt_bias, "dn_g": dn_norm_g, "g_ffn": norm_ffn_g,
        "g_ple": norm_ple_g, "g_final": final_norm_g.reshape(1, D),
    }

    loss_v, grad_x, G = _local_step(x[0], p[0, 0], loss_target[0], S, wt, conv, emit)
    G["loss"] = loss_v[:, :1]

    small_names = ["g_mix", "g_ffn", "g_ple", "g_final", "dn_g", "a_log", "dt_bias", "conv_a", "conv_qkv", "conv_ffn", "loss"]
    small_rows, pieces = [], []
    for nm in small_names:
        g_ = G[nm].reshape(-1)
        r_ = -(-g_.size // (8 * LANE)) * 8
        small_rows.append(r_)
        pieces.append(jnp.pad(g_, (0, r_ * LANE - g_.size)).reshape(r_, LANE))
    landed = {nm: _split_wait(h_, grad_x, True, name="scatter_wait_" + nm) for nm, h_ in pending.items() if nm != "w_in"}

    def adam(parts, w_, m_, v_, nm, own_=None):
        shp = w_.shape
        w2, m2, v2 = (t.reshape(parts.shape[1:]) for t in (w_, m_, v_))
        kw = {} if own_ is None else {"own": own_, "me": me.astype(jnp.int32).reshape(1)}
        return tuple(t.reshape(shp) for t in _adam(parts, w2, m2, v2, name="adam_" + nm, **kw))

    big = {
        "w_up": adam(landed["w_up"], w_up, m_w_up, v_w_up, "w_up", mine["w_up"]),
        "w_down": adam(landed["w_down"], w_down, m_w_down, v_w_down, "w_down", mine["w_down"]),
        "w_out": adam(landed["w_out"], w_out, m_w_out, v_w_out, "w_out", mine["w_out"]),
        "w_pg": adam(landed["w_pg"], w_ple_gate, m_w_ple_gate, v_w_ple_gate, "w_ple_gate", mine["w_pg"]),
        "w_pp": adam(landed["w_pp"], w_ple_proj, m_w_ple_proj, v_w_ple_proj, "w_ple_proj", mine["w_pp"]),
    }
    first = lambda t: lax.slice(t, (0,) * t.ndim, (1,) * t.ndim).reshape(1)
    big_done = sum(first(r[1]) for r in big.values())
    (small_l,) = _exchange([jnp.concatenate(pieces, axis=0)], False, name="gather_small_grads", after=big_done)

    def small_parts(nm):
        i = small_names.index(nm)
        r0 = sum(small_rows[:i])
        shp = G[nm].shape
        return small_l[:, r0:r0 + small_rows[i], :].reshape(N_DEV, -1)[:, :G[nm].size].reshape((N_DEV,) + shp)

    def conv_parts(nm, shard):
        full = small_parts(nm)
        C = shard.shape[-1]
        return lax.dynamic_slice_in_dim(full, me * C, C, axis=2)

    res = [
        adam(small_parts("g_mix"), norm_mix_g, m_norm_mix_g, v_norm_mix_g, "norm_mix_g"),
        None,
        adam(conv_parts("conv_a", conv_a_w), conv_a_w, m_conv_a_w, v_conv_a_w, "conv_a_w"),
        adam(conv_parts("conv_qkv", conv_qkv_w), conv_qkv_w, m_conv_qkv_w, v_conv_qkv_w, "conv_qkv_w"),
        adam(small_parts("a_log"), a_log, m_a_log, v_a_log, "a_log"),
        adam(small_parts("dt_bias"), dt_bias, m_dt_bias, v_dt_bias, "dt_bias"),
        adam(small_parts("dn_g"), dn_norm_g, m_dn_norm_g, v_dn_norm_g, "dn_norm_g"),
        big["w_out"],
        adam(small_parts("g_ffn"), norm_ffn_g, m_norm_ffn_g, v_norm_ffn_g, "norm_ffn_g"),
        big["w_up"],
        adam(conv_parts("conv_ffn", conv_ffn_w), conv_ffn_w, m_conv_ffn_w, v_conv_ffn_w, "conv_ffn_w"),
        big["w_down"],
        adam(small_parts("g_ple"), norm_ple_g, m_norm_ple_g, v_norm_ple_g, "norm_ple_g"),
        big["w_pg"],
        big["w_pp"],
        adam(small_parts("g_final"), final_norm_g.reshape(1, D), m_final_norm_g.reshape(1, D),
             v_final_norm_g.reshape(1, D), "final_norm_g"),
    ]
    res[-1] = tuple(t.reshape(D) for t in res[-1])
    landed_in = _split_wait(pending["w_in"], res[10][1], True, name="scatter_wait_w_in")
    res[1] = tuple(tr_(t) for t in adam(landed_in, tr_(w_in), tr_(m_w_in), tr_(v_w_in), "w_in", mine["w_in"]))
    grads, deltas, new_m, new_v = zip(*res)
    loss = jnp.sum(small_parts("loss"))
    return (loss, grad_x[None], *grads, *deltas, *new_m, *new_v)
```

```python
import functools

import jax
import jax.numpy as jnp
from jax import lax
from jax.experimental import pallas as pl
from jax.experimental.pallas import tpu as pltpu

F32 = jnp.float32
BF16 = jnp.bfloat16

EPS = 1e-6
CHUNK = 64
HEAD = 128
LANE = 128
N_DEV = 8
AB_PAD = 512

ADAM_LR = 0.001
ADAM_B1 = 0.9
ADAM_B2 = 0.999
ADAM_EPS = 1e-08
ADAM_WD = 0.01
ADAM_STEP = 10

MESH = pl.DeviceIdType.MESH


def _tile(dim, target, align=LANE):
    if dim <= target:
        return dim
    t = (target // align) * align
    while t > align and dim % t:
        t -= align
    assert dim % t == 0, (dim, target)
    return t


def _params(sem, vmem_mb=48):
    return pltpu.CompilerParams(dimension_semantics=sem, vmem_limit_bytes=vmem_mb << 20)


_DN = {"nn": (((1,), (0,)), ((), ())), "nt": (((1,), (1,)), ((), ())), "tn": (((0,), (0,)), ((), ()))}
LONG_K = 4096
SHARD_TILE = 1408


def _matmul(a, b, mode, *, name, out_dtypes=(F32,), epilogue=None, extras=(), vec_extras=(), n_vec=0, after=None,
            a_shards=False, b_shards=False, out_shards=False, out_lanes=False, tm=1024, tn=1024, tk=2048):
    shard_w = b.shape[2] if b_shards else None
    if b_shards:
        b_rows, b_cols = b.shape[1], b.shape[0] * shard_w
    else:
        b_rows, b_cols = b.shape
    a_w = a.shape[2] if a_shards else None
    a_dims = (a.shape[1], a.shape[0] * a_w) if a_shards else a.shape
    if mode == "nn":
        (M, K), (K2, N) = a_dims, (b_rows, b_cols)
    elif mode == "nt":
        (M, K), (N, K2) = a_dims, (b_rows, b_cols)
    else:
        (K, M), (K2, N) = a_dims, (b_rows, b_cols)
    assert K == K2, (name, a.shape, b.shape)
    tm = _tile(M, tm)
    n_dims = [N] + ([shard_w] if (b_shards and mode != "nt") else []) + ([N // N_DEV] if out_shards else [])
    tn = _tile(min(n_dims), tn)
    assert all(d % tn == 0 for d in n_dims), (name, n_dims, tn)
    grp = 1
    if b_shards and mode == "nt":
        grp = max(g for g in (1, 2, 4, 8) if g <= max(1, tk // shard_w) and (a_w is None or a_w % (g * shard_w) == 0))
    k_dims = [K] + ([shard_w] if (b_shards and mode == "nt") else []) + ([a_w] if a_shards else [])
    tk = grp * shard_w if grp > 1 else _tile(min(k_dims), tk)
    assert K % tk == 0, (name, K, tk)
    nk = K // tk
    n_ex, n_out = len(extras) + len(vec_extras), len(out_dtypes)
    assert n_vec == 0 or tn == N, (name, tn, N)
    dn = _DN[mode]

    n_tok = 0 if after is None else 1

    def body(a_ref, b_ref, *rest):
        rest = rest[n_tok:]
        ex_refs, out_refs, vec_refs = rest[:n_ex], rest[n_ex:n_ex + n_out], rest[n_ex + n_out:n_ex + n_out + n_vec]
        if grp > 1:
            part = sum(lax.dot_general(a_ref[:, s * shard_w:(s + 1) * shard_w].astype(BF16), b_ref[s].astype(BF16), dn,
                                       preferred_element_type=F32) for s in range(grp))
        else:
            part = lax.dot_general(a_ref[...].astype(BF16), b_ref[...].astype(BF16), dn, preferred_element_type=F32)
        first_rows = pl.program_id(0) == 0

        def finish(res):
            outs = (res,) if epilogue is None else epilogue(res, *[e[...] for e in ex_refs])
            for o_ref, val in zip(out_refs, outs[:n_out]):
                if out_lanes:
                    for c in range(tn // LANE):
                        o_ref[c] = val[:, c * LANE:(c + 1) * LANE].astype(o_ref.dtype)
                else:
                    o_ref[...] = val.astype(o_ref.dtype)
            for v_ref, val in zip(vec_refs, outs[n_out:]):
                @pl.when(first_rows)
                def _(v_ref=v_ref, val=val):
                    v_ref[...] = val

                @pl.when(jnp.logical_not(first_rows))
                def _(v_ref=v_ref, val=val):
                    v_ref[...] += val

        if nk == 1:
            finish(part)
            return
        acc, k = rest[-1], pl.program_id(2)

        @pl.when(k == 0)
        def _():
            acc[...] = part

        @pl.when(k > 0)
        def _():
            acc[...] += part

        @pl.when(k == nk - 1)
        def _():
            finish(acc[...])

    if a_shards:
        assert mode == "nt" and a_w % tk == 0, (name, mode, a_w, tk)
        per_a = a_w // tk
        a_spec = pl.BlockSpec((None, tm, tk), lambda i, j, k: (lax.div(k, per_a), i, lax.rem(k, per_a)))
    else:
        a_spec = pl.BlockSpec((tk, tm), lambda i, j, k: (k, i)) if mode == "tn" else pl.BlockSpec((tm, tk), lambda i, j, k: (i, k))
    if b_shards and mode != "nt":
        per = shard_w // tn
        b_spec = pl.BlockSpec((None, tk, tn), lambda i, j, k: (lax.div(j, per), k, lax.rem(j, per)))
    elif b_shards and grp > 1:
        b_spec = pl.BlockSpec((grp, tn, shard_w), lambda i, j, k: (k, j, 0))
    elif b_shards:
        per = shard_w // tk
        b_spec = pl.BlockSpec((None, tn, tk), lambda i, j, k: (lax.div(k, per), j, lax.rem(k, per)))
    else:
        b_spec = pl.BlockSpec((tn, tk), lambda i, j, k: (j, k)) if mode == "nt" else pl.BlockSpec((tk, tn), lambda i, j, k: (k, j))
    mn_spec = pl.BlockSpec((tm, tn), lambda i, j, k: (i, j))
    vec_spec = pl.BlockSpec((1, tn), lambda i, j, k: (0, j))
    if out_shards:
        assert not extras
        per_o = (N // N_DEV) // tn
        out_spec = pl.BlockSpec((None, tm, tn), lambda i, j, k: (lax.div(j, per_o), i, lax.rem(j, per_o)))
        out_dims = (N_DEV, M, N // N_DEV)
    elif out_lanes:
        assert not extras
        out_spec = pl.BlockSpec((tn // LANE, tm, LANE), lambda i, j, k: (j, i, 0))
        out_dims = (N // LANE, M, LANE)
    else:
        out_spec, out_dims = mn_spec, (M, N)
    outs = pl.pallas_call(
        body, name=name, grid=(M // tm, N // tn, nk),
        in_specs=[a_spec, b_spec] + [pl.BlockSpec((8, LANE), lambda i, j, k: (0, 0))] * n_tok
        + [mn_spec] * len(extras) + [vec_spec] * len(vec_extras),
        out_specs=[out_spec] * n_out + [vec_spec] * n_vec,
        out_shape=[jax.ShapeDtypeStruct(out_dims, dt) for dt in out_dtypes] + [jax.ShapeDtypeStruct((1, N), F32)] * n_vec,
        scratch_shapes=[pltpu.VMEM((tm, tn), F32)] if nk > 1 else [],
        compiler_params=_params(("arbitrary" if n_vec else "parallel", "parallel", "arbitrary"), 56),
    )(a, b, *([] if after is None else [after]), *extras, *vec_extras)
    return outs[0] if n_out + n_vec == 1 else outs


def _rms_fwd(x, g, *, name):
    T, D = x.shape
    tr = _tile(T, 256, 8)

    def body(x_ref, g_ref, h_ref):
        xv = x_ref[...]
        r = lax.rsqrt(jnp.mean(xv * xv, axis=-1, keepdims=True) + EPS)
        h_ref[...] = (xv * r * g_ref[...]).astype(h_ref.dtype)

    return pl.pallas_call(
        body, name=name, grid=(T // tr,),
        in_specs=[pl.BlockSpec((tr, D), lambda i: (i, 0)), pl.BlockSpec((1, D), lambda i: (0, 0))],
        out_specs=pl.BlockSpec((tr, D), lambda i: (i, 0)),
        out_shape=jax.ShapeDtypeStruct((T, D), BF16),
        compiler_params=_params(("parallel",)),
    )(x, g)


def _rms_bwd(x, g, dh, dres, *, name):
    T, D = x.shape
    tr = _tile(T, 256, 8)
    epi = _epi_rms_bwd(2)

    def body(x_ref, g_ref, dh_ref, dres_ref, dx_ref, dxb_ref, dg_ref):
        dx, _, dgp = epi(dh_ref[...], x_ref[...], dres_ref[...], g_ref[...])

        @pl.when(pl.program_id(0) == 0)
        def _():
            dg_ref[...] = jnp.zeros_like(dg_ref)

        dg_ref[...] += dgp
        dx_ref[...] = dx
        dxb_ref[...] = dx.astype(dxb_ref.dtype)

    row = pl.BlockSpec((tr, D), lambda i: (i, 0))
    vec = pl.BlockSpec((1, D), lambda i: (0, 0))
    return pl.pallas_call(
        body, name=name, grid=(T // tr,),
        in_specs=[row, vec, row, row], out_specs=[row, row, vec],
        out_shape=[jax.ShapeDtypeStruct((T, D), F32), jax.ShapeDtypeStruct((T, D), BF16), jax.ShapeDtypeStruct((1, D), F32)],
        compiler_params=_params(("arbitrary",)),
    )(x, g, dh, dres)


ROW_TILE = 512


def _epi_residual_rms(acc, res, g):
    xn = acc + res
    r = lax.rsqrt(jnp.mean(xn * xn, axis=-1, keepdims=True) + EPS)
    return xn, xn * r * g


def _epi_rms_bwd(n_copies):
    def epi(dh, x, dres, g):
        r = lax.rsqrt(jnp.mean(x * x, axis=-1, keepdims=True) + EPS)
        xh = x * r
        dxh = dh * g
        dx = dres + r * (dxh - xh * jnp.mean(dxh * xh, axis=-1, keepdims=True))
        return (dx,) * n_copies + (jnp.sum(dh * xh, axis=0, keepdims=True),)
    return epi


def _final_loss(x, g, tgt, pp, sg, *, name):
    T, D = x.shape
    tr = _tile(T, 256, 8)

    def body(x_ref, g_ref, t_ref, pp_ref, sg_ref, dx_ref, dg_ref, loss_ref, dpg_ref, dpp_ref):
        xv = x_ref[...]
        r = lax.rsqrt(jnp.mean(xv * xv, axis=-1, keepdims=True) + EPS)
        xh = xv * r
        gv = g_ref[...]
        err = xh * gv - t_ref[...]

        @pl.when(pl.program_id(0) == 0)
        def _():
            dg_ref[...] = jnp.zeros_like(dg_ref)
            loss_ref[...] = jnp.zeros_like(loss_ref)

        part = 0.5 * jnp.sum(jnp.mean(err * err, axis=-1, keepdims=True), axis=0, keepdims=True)
        loss_ref[...] += jnp.broadcast_to(part, loss_ref.shape)
        dy = err * (1.0 / D)
        dg_ref[...] += jnp.sum(dy * xh, axis=0, keepdims=True)
        dxh = dy * gv
        dx = r * (dxh - xh * jnp.mean(dxh * xh, axis=-1, keepdims=True))
        dx_ref[...] = dx
        s = sg_ref[...]
        dpg_ref[...] = (dx * pp_ref[...] * s * (1.0 - s)).astype(dpg_ref.dtype)
        dpp_ref[...] = (dx * s).astype(dpp_ref.dtype)

    row = pl.BlockSpec((tr, D), lambda i: (i, 0))
    vec = pl.BlockSpec((1, D), lambda i: (0, 0))
    return pl.pallas_call(
        body, name=name, grid=(T // tr,),
        in_specs=[row, vec, row, row, row], out_specs=[row, vec, pl.BlockSpec((1, LANE), lambda i: (0, 0)), row, row],
        out_shape=[jax.ShapeDtypeStruct((T, D), F32), jax.ShapeDtypeStruct((1, D), F32),
                   jax.ShapeDtypeStruct((1, LANE), F32)] + [jax.ShapeDtypeStruct((T, D), BF16)] * 2,
        compiler_params=_params(("arbitrary",)),
    )(x, g, tgt, pp, sg)


ROWS_QKV_FWD, ROWS_QKV_BWD, ROWS_FFN_FWD, ROWS_FFN_BWD, ROWS_GROUP_A = 512, 256, 256, 128, 256


def _ext(ref, r0, T, before, after, RC):
    parts = []
    if before:
        p0 = pl.multiple_of(jnp.maximum(r0 - 8, 0), 8)
        parts.append(jnp.where(r0 > 0, ref[pl.ds(p0, 8), :], 0.0))
    parts.append(ref[pl.ds(r0, RC), :])
    if after:
        n0 = pl.multiple_of(jnp.minimum(r0 + RC, T - 8), 8)
        parts.append(jnp.where(r0 + RC < T, ref[pl.ds(n0, 8), :], 0.0))
    return parts[0] if len(parts) == 1 else jnp.concatenate(parts, axis=0)


def _fold8(x):
    return jnp.sum(x.reshape(x.shape[0] // 8, 8, x.shape[1]), axis=0)


def _win(ref, r0, lo, n, T, RC, edge):
    if not edge:
        return ref[pl.ds(r0 + lo, n), :]
    xx = _ext(ref, r0, T, True, True, RC)
    a = 8 + lo
    return (xx if a == 0 else pltpu.roll(xx, xx.shape[0] - a, 0))[:n, :]


def _taps(ref, w_ref, K, r0, n, T, RC, edge):
    wins = [_win(ref, r0, -(K - 1 - j), n, T, RC, edge) for j in range(K)]
    y = wins[0] * w_ref[0:1, :]
    for j in range(1, K):
        y = y + wins[j] * w_ref[j:j + 1, :]
    return wins, y


def _untaps(scr_ref, val, w_ref, K, RC):
    scr_ref[0:val.shape[0], :] = val
    y = scr_ref[K - 1:K - 1 + RC, :] * w_ref[0:1, :]
    for j in range(1, K):
        s = K - 1 - j
        y = y + scr_ref[s:s + RC, :] * w_ref[j:j + 1, :]
    return y


def _peeled(n_chunks, RC, step, init):
    carry = step(0, init, True)
    if n_chunks > 2:
        carry = lax.fori_loop(1, n_chunks - 1, lambda i, c: step(pl.multiple_of(i * RC, RC), c, False), carry)
    if n_chunks > 1:
        carry = step((n_chunks - 1) * RC, carry, True)
    return carry


def _silu(x):
    return x * jax.nn.sigmoid(x)


def _dsilu(x):
    s = jax.nn.sigmoid(x)
    return s * (1.0 + x * (1.0 - s))


def _col_specs(T, offs):
    return [pl.BlockSpec((T, LANE), functools.partial(lambda o, j: (0, o + j), o)) for o in offs]


def _group_a_fwd(proj, conv_w, CW, out_cols, *, name):
    T = proj.shape[0]
    RC = _tile(T, ROWS_GROUP_A, 8)
    nb = CW // LANE
    K = conv_w.shape[0]

    def body(ax_ref, ab_ref, ac_ref, w_ref, y_ref):
        def step(r0, carry, edge):
            c = None
            for j in range(K):
                lo = -(K - 1 - j)
                t = _win(ac_ref, r0, lo, RC, T, RC, edge) * _win(ax_ref, r0, lo, RC, T, RC, edge) * w_ref[j:j + 1, :]
                c = t if c is None else c + t
            y_ref[pl.ds(r0, RC), :] = (ab_ref[pl.ds(r0, RC), :] * c).astype(y_ref.dtype)
            return carry
        _peeled(T // RC, RC, step, 0)

    return pl.pallas_call(
        body, name=name, grid=(nb,),
        in_specs=_col_specs(T, (0, nb, 2 * nb)) + [pl.BlockSpec((K, LANE), lambda j: (0, j))],
        out_specs=pl.BlockSpec((T, LANE), lambda j: (0, j)),
        out_shape=jax.ShapeDtypeStruct((T, out_cols), BF16), compiler_params=_params(("parallel",)),
    )(proj, proj, proj, conv_w)


def _group_a_bwd(proj, conv_w, dycat, CW, *, name):
    T = proj.shape[0]
    RC = _tile(T, ROWS_GROUP_A, 8)
    nb = CW // LANE
    K = conv_w.shape[0]

    def body(ax_ref, ab_ref, ac_ref, w_ref, dy_ref, dax_ref, dab_ref, dac_ref, dw_ref, scr_ref):
        def step(r0, accs, edge):
            ms = [_win(ac_ref, r0, -(K - 1 - j), RC, T, RC, edge) * _win(ax_ref, r0, -(K - 1 - j), RC, T, RC, edge)
                  for j in range(K)]
            c = ms[0] * w_ref[0:1, :]
            for j in range(1, K):
                c = c + ms[j] * w_ref[j:j + 1, :]
            dy = dy_ref[pl.ds(r0, RC), :]
            dab_ref[pl.ds(r0, RC), :] = (dy * c).astype(dab_ref.dtype)
            dc2 = _win(dy_ref, r0, 0, RC + 8, T, RC, edge) * _win(ab_ref, r0, 0, RC + 8, T, RC, edge)
            dm = _untaps(scr_ref, dc2, w_ref, K, RC)
            dax_ref[pl.ds(r0, RC), :] = (dm * ac_ref[pl.ds(r0, RC), :]).astype(dax_ref.dtype)
            dac_ref[pl.ds(r0, RC), :] = (dm * ax_ref[pl.ds(r0, RC), :]).astype(dac_ref.dtype)
            return tuple(accs[j] + _fold8(dc2[:RC] * ms[j]) for j in range(K))

        accs = _peeled(T // RC, RC, step, tuple(jnp.zeros((8, LANE), F32) for _ in range(K)))
        for j in range(K):
            dw_ref[j:j + 1, :] = jnp.sum(accs[j], axis=0, keepdims=True)

    col = pl.BlockSpec((T, LANE), lambda j: (0, j))
    wsp = pl.BlockSpec((K, LANE), lambda j: (0, j))
    return pl.pallas_call(
        body, name=name, grid=(nb,),
        in_specs=_col_specs(T, (0, nb, 2 * nb)) + [wsp, col],
        out_specs=[col, col, col, wsp],
        out_shape=[jax.ShapeDtypeStruct((T, CW), BF16)] * 3 + [jax.ShapeDtypeStruct((K, CW), F32)],
        scratch_shapes=[pltpu.VMEM((RC + 8, LANE), F32)],
        compiler_params=_params(("parallel",)),
    )(proj, proj, proj, conv_w, dycat)


def _qkv_fwd(proj, conv_w, off, H, *, name):
    T = proj.shape[0]
    RC = _tile(T, ROWS_QKV_FWD, 8)
    nb = 3 * H
    K = conv_w.shape[0]

    def body(x_ref, w_ref, y_ref):
        j = pl.program_id(0)
        is_qk = j < 2 * H
        scale = jnp.where(j < H, HEAD ** -0.5, 1.0).astype(F32)

        def step(r0, carry, edge):
            s = _silu(_taps(x_ref, w_ref, K, r0, RC, T, RC, edge)[1])
            r = lax.rsqrt(jnp.sum(s * s, axis=-1, keepdims=True) + EPS) * scale
            y_ref[pl.ds(r0, RC), :] = s * jnp.where(is_qk, r, 1.0)
            return carry
        _peeled(T // RC, RC, step, 0)

    return pl.pallas_call(
        body, name=name, grid=(nb,),
        in_specs=_col_specs(T, (off,)) + [pl.BlockSpec((K, LANE), lambda j: (0, j))],
        out_specs=pl.BlockSpec((T, LANE), lambda j: (0, j)),
        out_shape=jax.ShapeDtypeStruct((T, nb * LANE), F32), compiler_params=_params(("parallel",)),
    )(proj, conv_w)


def _qkv_bwd(proj, conv_w, dq, dk, dv, off, H, *, name):
    T = proj.shape[0]
    RC = _tile(T, ROWS_QKV_BWD, 8)
    nb = 3 * H
    K = conv_w.shape[0]

    def body(x_ref, w_ref, dq_ref, dk_ref, dv_ref, dx_ref, dw_ref, scr_ref):
        j = pl.program_id(0)
        is_qk = j < 2 * H
        scale = jnp.where(j < H, HEAD ** -0.5, 1.0).astype(F32)

        def step(r0, accs, edge):
            xs, c2 = _taps(x_ref, w_ref, K, r0, RC + 8, T, RC, edge)
            s2 = _silu(c2)
            dn2 = jnp.where(j < H, _win(dq_ref, r0, 0, RC + 8, T, RC, edge),
                            jnp.where(is_qk, _win(dk_ref, r0, 0, RC + 8, T, RC, edge),
                                      _win(dv_ref, r0, 0, RC + 8, T, RC, edge)))
            r = lax.rsqrt(jnp.sum(s2 * s2, axis=-1, keepdims=True) + EPS)
            nh = s2 * r
            dnp = dn2 * scale
            ds_qk = r * (dnp - nh * jnp.sum(dnp * nh, axis=-1, keepdims=True))
            ds2 = jnp.where(is_qk, ds_qk, dn2)
            dc2 = ds2 * _dsilu(c2)
            dx_ref[pl.ds(r0, RC), :] = _untaps(scr_ref, dc2, w_ref, K, RC).astype(dx_ref.dtype)
            return tuple(accs[jj] + _fold8(dc2[:RC] * xs[jj][:RC]) for jj in range(K))

        accs = _peeled(T // RC, RC, step, tuple(jnp.zeros((8, LANE), F32) for _ in range(K)))
        for jj in range(K):
            dw_ref[jj:jj + 1, :] = jnp.sum(accs[jj], axis=0, keepdims=True)

    col = pl.BlockSpec((T, LANE), lambda j: (0, j))
    wsp = pl.BlockSpec((K, LANE), lambda j: (0, j))
    return pl.pallas_call(
        body, name=name, grid=(nb,),
        in_specs=_col_specs(T, (off,)) + [wsp] + [
            pl.BlockSpec((T, LANE), functools.partial(lambda o, j: (0, jnp.clip(j - o, 0, H - 1)), o)) for o in (0, H, 2 * H)],
        out_specs=[col, wsp],
        out_shape=[jax.ShapeDtypeStruct((T, nb * LANE), BF16), jax.ShapeDtypeStruct((K, nb * LANE), F32)],
        scratch_shapes=[pltpu.VMEM((RC + 8, LANE), F32)],
        compiler_params=_params(("parallel",)),
    )(proj, conv_w, dq, dk, dv)


def _softplus(x):
    return jnp.maximum(x, 0.0) + jnp.log(1.0 + jnp.exp(-jnp.abs(x)))


def _gates_fwd(proj, alog, dtb, off, H, *, name):
    T = proj.shape[0]
    tr = _tile(T, 512, CHUNK)

    def body(ab_ref, al_ref, dt_ref, gam_ref, beta_ref):
        ab = ab_ref[...]
        lane = lax.broadcasted_iota(jnp.int32, ab.shape, 1)
        g = -jnp.exp(al_ref[...]) * _softplus(ab + dt_ref[...])
        gb = jnp.where(lane < H, g, jnp.where(lane < 2 * H, jax.nn.sigmoid(ab), 0.0))
        tril = _tri().astype(F32)
        gam = jnp.concatenate([_mm(tril, gb[c * CHUNK:(c + 1) * CHUNK, :], precision=lax.Precision.HIGHEST)
                               for c in range(tr // CHUNK)], axis=0)
        for h in range(H):
            gam_ref[h] = jnp.broadcast_to(gam[:, h:h + 1], (tr, LANE))
            beta_ref[h] = jnp.broadcast_to(gb[:, H + h:H + h + 1], (tr, LANE))

    vec = pl.BlockSpec((1, LANE), lambda i: (0, 0))
    heads = pl.BlockSpec((H, tr, LANE), lambda i: (0, i, 0))
    return pl.pallas_call(
        body, name=name, grid=(T // tr,),
        in_specs=[pl.BlockSpec((tr, LANE), lambda i: (i, off)), vec, vec],
        out_specs=[heads, heads],
        out_shape=[jax.ShapeDtypeStruct((H, T, LANE), F32)] * 2, compiler_params=_params(("parallel",)),
    )(proj, alog, dtb)


def _gates_bwd(proj, alog, dtb, dgamB, dbB, off, H, *, name):
    T = proj.shape[0]
    tr = _tile(T, 512, CHUNK)

    def body(ab_ref, al_ref, dt_ref, dgam_ref, dbeta_ref, dab_ref, dal_ref, ddt_ref):
        ab = ab_ref[...]
        lane = lax.broadcasted_iota(jnp.int32, ab.shape, 1)
        is_g = lane < H
        d = jnp.zeros_like(ab)
        for h in range(H):
            d = jnp.where(lane == h, dgam_ref[h], jnp.where(lane == H + h, dbeta_ref[h], d))
        triu = _tri(upper=True).astype(F32)
        dg = jnp.concatenate([_mm(triu, d[c * CHUNK:(c + 1) * CHUNK, :], precision=lax.Precision.HIGHEST)
                              for c in range(tr // CHUNK)], axis=0)
        z = ab + dt_ref[...]
        A = -jnp.exp(al_ref[...])
        da = dg * A * jax.nn.sigmoid(z)
        beta = jax.nn.sigmoid(ab)
        db = d * beta * (1.0 - beta)
        dab_ref[...] = jnp.where(is_g, da, jnp.where(lane < 2 * H, db, 0.0)).astype(dab_ref.dtype)

        @pl.when(pl.program_id(0) == 0)
        def _():
            dal_ref[...] = jnp.zeros_like(dal_ref)
            ddt_ref[...] = jnp.zeros_like(ddt_ref)

        dal_ref[...] += jnp.sum(jnp.where(is_g, dg * A * _softplus(z), 0.0), axis=0, keepdims=True)
        ddt_ref[...] += jnp.sum(jnp.where(is_g, da, 0.0), axis=0, keepdims=True)

    vec = pl.BlockSpec((1, LANE), lambda i: (0, 0))
    row = pl.BlockSpec((tr, LANE), lambda i: (i, 0))
    heads = pl.BlockSpec((H, tr, LANE), lambda i: (0, i, 0))
    return pl.pallas_call(
        body, name=name, grid=(T // tr,),
        in_specs=[pl.BlockSpec((tr, LANE), lambda i: (i, off)), vec, vec, heads, heads],
        out_specs=[row, vec, vec],
        out_shape=[jax.ShapeDtypeStruct((T, LANE), BF16), jax.ShapeDtypeStruct((1, LANE), F32),
                   jax.ShapeDtypeStruct((1, LANE), F32)],
        compiler_params=_params(("arbitrary",)),
    )(proj, alog, dtb, dgamB, dbB)


def _gated_norm_fwd(o, proj, gn, zoff, ycat, *, name):
    T, W = o.shape
    tr = _tile(T, 256, 8)
    nh_, zblk = W // LANE, (zoff * LANE) // W
    assert zblk * W == zoff * LANE

    def body(o_ref, z_ref, g_ref, ycat_ref, y_ref):
        for h in range(nh_):
            ln = slice(h * LANE, (h + 1) * LANE)
            ov = o_ref[:, ln]
            r = lax.rsqrt(jnp.mean(ov * ov, axis=-1, keepdims=True) + EPS)
            y_ref[:, ln] = (ov * r * g_ref[...] * _silu(z_ref[:, ln])).astype(y_ref.dtype)

    assert ycat.shape == (T, 2 * W), ycat.shape
    blk = pl.BlockSpec((tr, W), lambda i: (i, 0))
    return pl.pallas_call(
        body, name=name, grid=(T // tr,),
        in_specs=[blk, pl.BlockSpec((tr, W), lambda i: (i, zblk)), pl.BlockSpec((1, LANE), lambda i: (0, 0)),
                  pl.BlockSpec(memory_space=pl.ANY)],
        out_specs=pl.BlockSpec((tr, W), lambda i: (i, 1)), out_shape=jax.ShapeDtypeStruct(ycat.shape, ycat.dtype),
        input_output_aliases={3: 0}, compiler_params=_params(("parallel",)),
    )(o, proj, gn, ycat)


def _gated_norm_bwd(o, proj, gn, dycat, zoff, yoff, *, name):
    T, W = o.shape
    tr = _tile(T, 256, 8)
    nh_, zblk, yblk = W // LANE, (zoff * LANE) // W, (yoff * LANE) // W
    assert zblk * W == zoff * LANE and yblk * W == yoff * LANE

    def body(o_ref, z_ref, g_ref, dy_ref, do_ref, dz_ref, dg_ref):
        @pl.when(pl.program_id(0) == 0)
        def _():
            dg_ref[...] = jnp.zeros_like(dg_ref)

        gv = g_ref[...]
        dg = jnp.zeros_like(gv)
        for h in range(nh_):
            ln = slice(h * LANE, (h + 1) * LANE)
            ov, zv, dy = o_ref[:, ln], z_ref[:, ln], dy_ref[:, ln]
            r = lax.rsqrt(jnp.mean(ov * ov, axis=-1, keepdims=True) + EPS)
            nh = ov * r
            s = _silu(zv)
            dg = dg + jnp.sum(dy * nh * s, axis=0, keepdims=True)
            dz_ref[:, ln] = (dy * nh * gv * _dsilu(zv)).astype(dz_ref.dtype)
            dn = dy * gv * s
            do_ref[:, ln] = r * (dn - nh * jnp.mean(dn * nh, axis=-1, keepdims=True))
        dg_ref[...] += dg

    blk = pl.BlockSpec((tr, W), lambda i: (i, 0))
    vec = pl.BlockSpec((1, LANE), lambda i: (0, 0))
    return pl.pallas_call(
        body, name=name, grid=(T // tr,),
        in_specs=[blk, pl.BlockSpec((tr, W), lambda i: (i, zblk)), vec, pl.BlockSpec((tr, W), lambda i: (i, yblk))],
        out_specs=[blk, blk, vec],
        out_shape=[jax.ShapeDtypeStruct((T, W), F32), jax.ShapeDtypeStruct((T, W), BF16),
                   jax.ShapeDtypeStruct((1, LANE), F32)],
        compiler_params=_params(("arbitrary",)),
    )(o, proj, gn, dycat)


def _ffn_act_fwd(up_pre, conv_w, *, name):
    T, F2 = up_pre.shape[1], up_pre.shape[0] * LANE
    RC = _tile(T, ROWS_FFN_FWD, 8)
    nb = F2 // 2 // LANE
    K = conv_w.shape[0]

    def body(g_ref, v_ref, wg_ref, wv_ref, y_ref):
        def step(r0, carry, edge):
            _, gate = _taps(g_ref, wg_ref, K, r0, RC, T, RC, edge)
            _, val = _taps(v_ref, wv_ref, K, r0, RC, T, RC, edge)
            y_ref[pl.ds(r0, RC), :] = (_silu(gate) * val).astype(y_ref.dtype)
            return carry
        _peeled(T // RC, RC, step, 0)

    return pl.pallas_call(
        body, name=name, grid=(nb,),
        in_specs=[pl.BlockSpec((None, T, LANE), lambda j: (j, 0, 0)), pl.BlockSpec((None, T, LANE), lambda j: (nb + j, 0, 0)),
                  pl.BlockSpec((K, LANE), lambda j: (0, j)), pl.BlockSpec((K, LANE), lambda j: (0, nb + j))],
        out_specs=pl.BlockSpec((T, LANE), lambda j: (0, j)),
        out_shape=jax.ShapeDtypeStruct((T, F2 // 2), BF16), compiler_params=_params(("parallel",)),
    )(up_pre, up_pre, conv_w, conv_w)


def _ffn_act_bwd(up_pre, conv_w, dact, *, name):
    T, F2 = up_pre.shape[1], up_pre.shape[0] * LANE
    RC = _tile(T, ROWS_FFN_BWD, 8)
    nb = F2 // 2 // LANE
    K = conv_w.shape[0]

    def body(g_ref, v_ref, wg_ref, wv_ref, da_ref, d_ref, dwg_ref, dwv_ref, sg_ref, sv_ref):
        def step(r0, accs, edge):
            gs, gate2 = _taps(g_ref, wg_ref, K, r0, RC + 8, T, RC, edge)
            vs, val2 = _taps(v_ref, wv_ref, K, r0, RC + 8, T, RC, edge)
            da2 = _win(da_ref, r0, 0, RC + 8, T, RC, edge)
            dgate2 = da2 * val2 * _dsilu(gate2)
            dval2 = da2 * _silu(gate2)
            d_ref[0, pl.ds(r0, RC), :] = _untaps(sg_ref, dgate2, wg_ref, K, RC).astype(d_ref.dtype)
            d_ref[1, pl.ds(r0, RC), :] = _untaps(sv_ref, dval2, wv_ref, K, RC).astype(d_ref.dtype)
            new = []
            for j in range(K):
                new.append(accs[2 * j] + _fold8(dgate2[:RC] * gs[j][:RC]))
                new.append(accs[2 * j + 1] + _fold8(dval2[:RC] * vs[j][:RC]))
            return tuple(new)

        accs = _peeled(T // RC, RC, step, tuple(jnp.zeros((8, LANE), F32) for _ in range(2 * K)))
        for j in range(K):
            dwg_ref[j:j + 1, :] = jnp.sum(accs[2 * j], axis=0, keepdims=True)
            dwv_ref[j:j + 1, :] = jnp.sum(accs[2 * j + 1], axis=0, keepdims=True)

    col = pl.BlockSpec((T, LANE), lambda j: (0, j))
    wsp = pl.BlockSpec((K, LANE), lambda j: (0, j))
    return pl.pallas_call(
        body, name=name, grid=(nb,),
        in_specs=[pl.BlockSpec((None, T, LANE), lambda j: (j, 0, 0)), pl.BlockSpec((None, T, LANE), lambda j: (nb + j, 0, 0)),
                  wsp, pl.BlockSpec((K, LANE), lambda j: (0, nb + j)), col],
        out_specs=[pl.BlockSpec((2, T, LANE), lambda j: (0, 0, j)), wsp, wsp],
        out_shape=[jax.ShapeDtypeStruct((2, T, F2 // 2), BF16)] + [jax.ShapeDtypeStruct((K, F2 // 2), F32)] * 2,
        scratch_shapes=[pltpu.VMEM((RC + 8, LANE), F32)] * 2,
        compiler_params=_params(("parallel",)),
    )(up_pre, up_pre, conv_w, conv_w, dact)


CPB = 8
CPB_SCAN = 4
GRP = 8
HP = lax.Precision.HIGH


def _tri(strict=False, upper=False):
    r = lax.broadcasted_iota(jnp.int32, (CHUNK, CHUNK), 0)
    c = lax.broadcasted_iota(jnp.int32, (CHUNK, CHUNK), 1)
    if upper:
        return c >= r
    return (r > c) if strict else (r >= c)


def _mm(a, b, dn="nn", precision=None):
    precision = HP if precision is None else precision
    return lax.dot_general(a, b, _DN[dn], precision=precision, preferred_element_type=F32)


def _mm16(a, b, dn="nn"):
    return lax.dot_general(a.astype(BF16), b.astype(BF16), _DN[dn], preferred_element_type=F32)


def _each(f, *cols):
    return [f(*xs) for xs in zip(*cols)]


def _decay(gam):
    return jnp.exp(jnp.where(_tri(), gam[:, :CHUNK] - gam.T[:CHUNK, :], -1e30))


def _delta_specs(T, H, cpb):
    rows = cpb * CHUNK
    col = lambda o: pl.BlockSpec((rows, LANE), functools.partial(lambda o, h, n: (n, o + h), o))
    bc = pl.BlockSpec((1, rows, LANE), lambda h, n: (h, n, 0))
    sq = pl.BlockSpec((1, cpb, CHUNK, CHUNK), lambda h, n: (h, n, 0, 0))
    vec = pl.BlockSpec((1, cpb, 1, LANE), lambda h, n: (h, n, 0, 0))
    return col, bc, sq, vec


def _delta_prep_fwd(qkv, gamB, bB, H, *, name):
    T = qkv.shape[0]
    N = T // CHUNK
    cpb = _tile(N, CPB, 8)
    grp = min(GRP, cpb)
    col, bc, sq, vec = _delta_specs(T, H, cpb)

    def body(q_ref, k_ref, v_ref, g_ref, b_ref, u_ref, w_ref, qd_ref, kd_ref, qk_ref, ti_ref, gl_ref):
        eye = (lax.broadcasted_iota(jnp.int32, (CHUNK, CHUNK), 0) == lax.broadcasted_iota(jnp.int32, (CHUNK, CHUNK), 1)).astype(F32)
        strict = _tri(strict=True)
        for c0 in range(0, cpb, grp):
            cs = list(range(c0, c0 + grp))
            rows = [slice(c * CHUNK, (c + 1) * CHUNK) for c in cs]
            q, k, v = ([r_[r, :] for r in rows] for r_ in (q_ref, k_ref, v_ref))
            bb = [b_ref[0, r, :] for r in rows]
            gam = [g_ref[0, r, :] for r in rows]
            D = _each(_decay, gam)
            e = _each(jnp.exp, gam)
            kk = _each(lambda k_: _mm16(k_, k_, "nt"), k)
            X = _each(lambda kk_, D_, b_: -(jnp.where(strict, kk_ * D_, 0.0) * b_[:, :CHUNK]), kk, D, bb)
            R = _each(lambda x: eye + x, X)
            for _ in range(5):
                X = _each(lambda x: _mm(x, x), X)
                R = _each(lambda r, x: r + _mm(r, x), R, X)
            u = _each(lambda r, b_, v_: _mm(r, b_ * v_), R, bb, v)
            w = _each(lambda r, b_, e_, k_: _mm(r, b_ * e_ * k_), R, bb, e, k)
            qk = _each(lambda q_, k_, D_: _mm16(q_, k_, "nt") * D_, q, k, D)
            for i, c in enumerate(cs):
                glast = gam[i][CHUNK - 1:CHUNK, :]
                u_ref[rows[i], :] = u[i]
                w_ref[rows[i], :] = w[i]
                qd_ref[rows[i], :] = e[i] * q[i]
                kd_ref[rows[i], :] = jnp.exp(glast - gam[i]) * k[i]
                qk_ref[0, c] = qk[i]
                ti_ref[0, c] = R[i]
                gl_ref[0, c] = jnp.exp(glast)

    full = jax.ShapeDtypeStruct((T, H * LANE), F32)
    sqs = jax.ShapeDtypeStruct((H, N, CHUNK, CHUNK), F32)
    return pl.pallas_call(
        body, name=name, grid=(H, N // cpb),
        in_specs=[col(0), col(H), col(2 * H), bc, bc],
        out_specs=[col(0)] * 4 + [sq, sq, vec],
        out_shape=[full] * 4 + [sqs, sqs, jax.ShapeDtypeStruct((H, N, 1, LANE), F32)],
        compiler_params=_params(("parallel", "parallel")),
    )(qkv, qkv, qkv, gamB, bB)


def _scan_specs(H, N, cpb, hb, rev):
    nbk = N // cpb
    blk = (lambda n: nbk - 1 - n) if rev else (lambda n: n)
    col = pl.BlockSpec((cpb * CHUNK, hb * LANE), lambda h, n: (blk(n), h))
    sq = pl.BlockSpec((hb, cpb, CHUNK, CHUNK), lambda h, n: (h, blk(n), 0, 0))
    vec = pl.BlockSpec((hb, cpb, 1, LANE), lambda h, n: (h, blk(n), 0, 0))
    st = pl.BlockSpec((hb, cpb, HEAD, HEAD), lambda h, n: (h, blk(n), 0, 0))
    return col, sq, vec, st


def _delta_scan_fwd(u, w, qd, kd, qk, gl, H, *, name):
    T = u.shape[0]
    N = T // CHUNK
    cpb = _tile(N, CPB_SCAN, 4)
    hb = min(GRP, H)
    col, sq, vec, st = _scan_specs(H, N, cpb, hb, False)
    lanes = [slice(j * LANE, (j + 1) * LANE) for j in range(hb)]
    heads = list(range(hb))

    def body(u_ref, w_ref, qd_ref, kd_ref, qk_ref, gl_ref, o_ref, vn_ref, ss_ref, s_scr):
        @pl.when(pl.program_id(1) == 0)
        def _():
            s_scr[...] = jnp.zeros_like(s_scr)

        def step(c, states):
            rows = pl.ds(pl.multiple_of(c * CHUNK, CHUNK), CHUNK)
            S = list(states)
            for j in heads:
                ss_ref[j, c] = S[j]
            wS = _each(lambda ln, s: _mm16(w_ref[rows, ln], s), lanes, S)
            qS = _each(lambda ln, s: _mm16(qd_ref[rows, ln], s), lanes, S)
            vn = _each(lambda ln, ws: u_ref[rows, ln] - ws, lanes, wS)
            o = _each(lambda j, qs, vn_: qs + _mm16(qk_ref[j, c], vn_), heads, qS, vn)
            new = _each(lambda j, ln, s, vn_: s * gl_ref[j, c] + _mm16(kd_ref[rows, ln], vn_, "tn"),
                        heads, lanes, S, vn)
            for j in heads:
                o_ref[rows, lanes[j]] = o[j]
                vn_ref[rows, lanes[j]] = vn[j]
            return tuple(new)
        out = lax.fori_loop(0, cpb, step, tuple(s_scr[j] for j in heads))
        for j in heads:
            s_scr[j] = out[j]

    full = jax.ShapeDtypeStruct((T, H * LANE), F32)
    return pl.pallas_call(
        body, name=name, grid=(H // hb, N // cpb),
        in_specs=[col] * 4 + [sq, vec],
        out_specs=[col, col, st],
        out_shape=[full, full, jax.ShapeDtypeStruct((H, N, HEAD, HEAD), F32)],
        scratch_shapes=[pltpu.VMEM((hb, HEAD, HEAD), F32)],
        compiler_params=_params(("parallel", "arbitrary")),
    )(u, w, qd, kd, qk, gl)


def _delta_scan_bwd(do, w, qd, kd, vn, qk, gl, ss, H, *, name):
    T = do.shape[0]
    N = T // CHUNK
    cpb = _tile(N, CPB_SCAN, 4)
    hb = min(GRP, H)
    col, sq, vec, st = _scan_specs(H, N, cpb, hb, True)
    lanes = [slice(j * LANE, (j + 1) * LANE) for j in range(hb)]
    heads = list(range(hb))

    def body(do_ref, w_ref, qd_ref, kd_ref, vn_ref, qk_ref, gl_ref, ss_ref,
             du_ref, dw_ref, dqd_ref, dkd_ref, dqk_ref, dgl_ref, ds_scr):
        @pl.when(pl.program_id(1) == 0)
        def _():
            ds_scr[...] = jnp.zeros_like(ds_scr)

        def step(i, dstates):
            c = cpb - 1 - i
            rows = pl.ds(pl.multiple_of(c * CHUNK, CHUNK), CHUNK)
            dS = list(dstates)
            S = [ss_ref[j, c] for j in heads]
            dov = [do_ref[rows, ln] for ln in lanes]
            vnv = [vn_ref[rows, ln] for ln in lanes]
            a1 = _each(lambda j, d_: _mm16(qk_ref[j, c], d_, "tn"), heads, dov)
            a2 = _each(lambda ln, ds: _mm16(kd_ref[rows, ln], ds), lanes, dS)
            dvn = _each(lambda x, y: x + y, a1, a2)
            dqd = _each(lambda d_, s: _mm16(d_, s, "nt"), dov, S)
            dkd = _each(lambda v_, ds: _mm16(v_, ds, "nt"), vnv, dS)
            dqk = _each(lambda d_, v_: _mm16(d_, v_, "nt"), dov, vnv)
            dw = _each(lambda dv_, s: -_mm16(dv_, s, "nt"), dvn, S)
            b1 = _each(lambda ln, d_: _mm16(qd_ref[rows, ln], d_, "tn"), lanes, dov)
            b2 = _each(lambda ln, dv_: _mm16(w_ref[rows, ln], dv_, "tn"), lanes, dvn)
            new = _each(lambda j, x, y, ds: x + ds * gl_ref[j, c] - y, heads, b1, b2, dS)
            for j in heads:
                du_ref[rows, lanes[j]] = dvn[j]
                dw_ref[rows, lanes[j]] = dw[j]
                dqd_ref[rows, lanes[j]] = dqd[j]
                dkd_ref[rows, lanes[j]] = dkd[j]
                dqk_ref[j, c] = dqk[j]
                dgl = jnp.sum(jnp.sum(dS[j] * S[j], axis=1, keepdims=True), axis=0, keepdims=True)
                dgl_ref[j, c] = jnp.broadcast_to(dgl, (1, LANE))
            return tuple(new)
        out = lax.fori_loop(0, cpb, step, tuple(ds_scr[j] for j in heads))
        for j in heads:
            ds_scr[j] = out[j]

    full = jax.ShapeDtypeStruct((T, H * LANE), F32)
    return pl.pallas_call(
        body, name=name, grid=(H // hb, N // cpb),
        in_specs=[col] * 5 + [sq, vec, st],
        out_specs=[col] * 4 + [sq, vec],
        out_shape=[full] * 4 + [jax.ShapeDtypeStruct((H, N, CHUNK, CHUNK), F32), jax.ShapeDtypeStruct((H, N, 1, LANE), F32)],
        scratch_shapes=[pltpu.VMEM((hb, HEAD, HEAD), F32)],
        compiler_params=_params(("parallel", "arbitrary")),
    )(do, w, qd, kd, vn, qk, gl, ss)


def _delta_prep_bwd(qkv, gamB, bB, ti, u, w, qk, du, dw, dqd, dkd, dqk, dgl, H, *, name):
    T = qkv.shape[0]
    N = T // CHUNK
    cpb = _tile(N, CPB, 8)
    grp = min(GRP, cpb)
    col, bc, sq, vec = _delta_specs(T, H, cpb)

    def body(q_ref, k_ref, v_ref, g_ref, b_ref, ti_ref, u_ref, w_ref, qk_ref,
             du_ref, dw_ref, dqd_ref, dkd_ref, dqk_ref, dgl_ref,
             dq_ref, dk_ref, dv_ref, dg_ref, db_ref):
        ones = jnp.ones((CHUNK, LANE), F32)
        strict = _tri(strict=True)
        last = lax.broadcasted_iota(jnp.int32, (CHUNK, LANE), 0) == CHUNK - 1
        lsum = lambda x: jnp.sum(x, axis=-1, keepdims=True)
        for c0 in range(0, cpb, grp):
            cs = list(range(c0, c0 + grp))
            rows = [slice(c * CHUNK, (c + 1) * CHUNK) for c in cs]
            ld = lambda r_: [r_[r, :] for r in rows]
            q, k, v, uv, wv, duv, dwv, dqd_v, dkd_v = (ld(r_) for r_ in (q_ref, k_ref, v_ref, u_ref, w_ref, du_ref, dw_ref, dqd_ref, dkd_ref))
            bb = [b_ref[0, r, :] for r in rows]
            gam = [g_ref[0, r, :] for r in rows]
            Ti = [ti_ref[0, c] for c in cs]
            QK = [qk_ref[0, c] for c in cs]
            dqk_v = [dqk_ref[0, c] for c in cs]
            D = _each(_decay, gam)
            e = _each(jnp.exp, gam)
            glast = [g_[CHUNK - 1:CHUNK, :] for g_ in gam]
            eL = _each(lambda gl_, g_: jnp.exp(gl_ - g_), glast, gam)
            kk = _each(lambda k_: _mm16(k_, k_, "nt"), k)
            KKD = _each(lambda kk_, D_: jnp.where(strict, kk_ * D_, 0.0), kk, D)
            dru = _each(lambda t, d_: _mm(t, d_, "tn"), Ti, duv)
            drw = _each(lambda t, d_: _mm(t, d_, "tn"), Ti, dwv)
            l1 = _each(lambda a, b: _mm(a, b, "nt"), dru, uv)
            l2 = _each(lambda a, b: _mm(a, b, "nt"), drw, wv)
            dL = _each(lambda a, b: jnp.where(strict, -(a + b), 0.0), l1, l2)
            Mm = _each(lambda dl, b_: dl * b_[:, :CHUNK], dL, bb)
            dKK = _each(lambda m_, D_: m_ * D_, Mm, D)
            dQK = _each(lambda a, D_: a * D_, dqk_v, D)
            P = _each(lambda m_, kkd, a, qk_: m_ * kkd + a * qk_, Mm, KKD, dqk_v, QK)
            q1 = _each(lambda a, k_: _mm16(a, k_), dQK, k)
            k1 = _each(lambda a, q_: _mm16(a, q_, "tn"), dQK, q)
            k2 = _each(lambda a, k_: _mm16(a, k_), dKK, k)
            k3 = _each(lambda a, k_: _mm16(a, k_, "tn"), dKK, k)
            s1 = _each(lambda dl, kkd: _mm(dl * kkd, ones), dL, KKD)
            p1 = _each(lambda p_: _mm(p_, ones), P)
            p2 = _each(lambda p_: _mm(p_, ones, "tn"), P)
            for i, c in enumerate(cs):
                r = rows[i]
                bek = bb[i] * e[i]
                kdv = eL[i] * k[i]
                dq_ref[r, :] = q1[i] + e[i] * dqd_v[i]
                dk_ref[r, :] = k1[i] + k2[i] + k3[i] + bek * drw[i] + eL[i] * dkd_v[i]
                dv_ref[r, :] = bb[i] * dru[i]
                db_ref[0, r, :] = s1[i] + lsum(dru[i] * v[i]) + lsum(drw[i] * e[i] * k[i])
                dgam = (p1[i] - p2[i] + lsum(drw[i] * bek * k[i]) + lsum(dqd_v[i] * e[i] * q[i])
                        - lsum(dkd_v[i] * kdv))
                xlast = jnp.sum(lsum(dkd_v[i] * kdv), axis=0, keepdims=True) + jnp.exp(glast[i]) * dgl_ref[0, c]
                dg_ref[0, r, :] = dgam + jnp.where(last, xlast, 0.0)

    full = jax.ShapeDtypeStruct((T, H * LANE), F32)
    bcs = jax.ShapeDtypeStruct((H, T, LANE), F32)
    return pl.pallas_call(
        body, name=name, grid=(H, N // cpb),
        in_specs=[col(0), col(H), col(2 * H), bc, bc, sq, col(0), col(0), sq, col(0), col(0), col(0), col(0), sq, vec],
        out_specs=[col(0), col(0), col(0), bc, bc],
        out_shape=[full, full, full, bcs, bcs],
        compiler_params=_params(("parallel", "parallel")),
    )(qkv, qkv, qkv, gamB, bB, ti, u, w, qk, du, dw, dqd, dkd, dqk, dgl)


def _adam(parts, w, m, v, *, name, own=None, me=None):
    P, R, C = parts.shape
    if R > 256 and R % 8:
        tr, tc = R, _tile(C, 256)
    else:
        tr, tc = _tile(R, 256, 8), C
    n_own = 0 if own is None else 2

    def body(*refs):
        p_ref, w_ref, m_ref, v_ref, g_ref, d_ref, nm_ref, nv_ref = refs[n_own:]
        g = None
        for i in range(P):
            t = p_ref[i].astype(F32)
            if n_own:
                t = jnp.where(refs[0][0] == i, refs[1][...].astype(F32), t)
            g = t if g is None else g + t
        mn = ADAM_B1 * m_ref[...] + (1.0 - ADAM_B1) * g
        vn = ADAM_B2 * v_ref[...] + (1.0 - ADAM_B2) * (g * g)
        m_hat = mn / (1.0 - ADAM_B1 ** ADAM_STEP)
        v_hat = vn / (1.0 - ADAM_B2 ** ADAM_STEP)
        g_ref[...] = g
        d_ref[...] = -ADAM_LR * (m_hat / (jnp.sqrt(v_hat) + ADAM_EPS) + ADAM_WD * w_ref[...])
        nm_ref[...] = mn
        nv_ref[...] = vn

    blk = pl.BlockSpec((tr, tc), lambda i, j: (i, j))
    return pl.pallas_call(
        body, name=name, grid=(R // tr, C // tc),
        in_specs=[pl.BlockSpec(memory_space=pltpu.SMEM), blk][:n_own] + [pl.BlockSpec((P, tr, tc), lambda i, j: (0, i, j)), blk, blk, blk],
        out_specs=[blk] * 4, out_shape=[jax.ShapeDtypeStruct((R, C), F32)] * 4,
        compiler_params=_params(("parallel", "parallel")),
    )(*([me, own] if n_own else []), parts, w, m, v)


def _mesh_pos():
    return lax.axis_index("x"), lax.axis_index("y"), lax.axis_index("c")


def _peer(k):
    x, y, c = _mesh_pos()
    px, py, pc = x ^ ((k >> 2) & 1), y ^ ((k >> 1) & 1), c ^ (k & 1)
    return (px, py, pc), 4 * px + 2 * py + pc


def _exchange(arrays, scatter, *, name, after=None):
    n = len(arrays)
    n_in = n if after is None else n + 1
    blocks = [a.shape[1:] if scatter else a.shape for a in arrays]

    def body(*refs):
        srcs, dsts = refs[:n], refs[n_in:n_in + n]
        send_sems, recv_sems, local_sems = refs[n_in + n:]
        x, y, c = _mesh_pos()
        me = 4 * x + 2 * y + c
        local, sends = [], []
        for a in range(n):
            cp = pltpu.make_async_copy(srcs[a].at[me] if scatter else srcs[a], dsts[a].at[me], local_sems.at[a])
            cp.start()
            local.append(cp)
            for k in range(1, N_DEV):
                dev, idx = _peer(k)
                cp = pltpu.make_async_remote_copy(
                    src_ref=srcs[a].at[idx] if scatter else srcs[a], dst_ref=dsts[a].at[me],
                    send_sem=send_sems.at[a * N_DEV + k], recv_sem=recv_sems.at[a * N_DEV + k],
                    device_id=dev, device_id_type=MESH)
                cp.start()
                sends.append(cp)
        for a in range(n):
            for k in range(1, N_DEV):
                dev, idx = _peer(k)
                pltpu.make_async_remote_copy(
                    src_ref=srcs[a].at[idx] if scatter else srcs[a], dst_ref=dsts[a].at[idx],
                    send_sem=send_sems.at[a * N_DEV + k], recv_sem=recv_sems.at[a * N_DEV + k],
                    device_id=dev, device_id_type=MESH).wait_recv()
        for cp in sends:
            cp.wait_send()
        for cp in local:
            cp.wait()

    anyspec = pl.BlockSpec(memory_space=pl.ANY)
    return pl.pallas_call(
        body, name=name, in_specs=[anyspec] * n_in, out_specs=[anyspec] * n,
        out_shape=[jax.ShapeDtypeStruct((N_DEV,) + tuple(b), a.dtype) for a, b in zip(arrays, blocks)],
        scratch_shapes=[pltpu.SemaphoreType.DMA((n * N_DEV,)), pltpu.SemaphoreType.DMA((n * N_DEV,)),
                        pltpu.SemaphoreType.DMA((n,))],
    )(*arrays, *([] if after is None else [after]))


_ANY = pl.BlockSpec(memory_space=pl.ANY)
_SEM = pl.BlockSpec(memory_space=pltpu.SEMAPHORE)
_EFFECT = pltpu.SideEffectType.DATAFLOW_SIDE_EFFECTING


def _in_hbm(a):
    return pltpu.with_memory_space_constraint(a, pltpu.HBM)


def _split_copy(src, land, send, recv, k, me, scatter, landed):
    dev, idx = _peer(k)
    return pltpu.make_async_remote_copy(
        src_ref=src.at[idx] if scatter else src, dst_ref=land.at[idx if landed else me],
        send_sem=send.at[k], recv_sem=recv.at[k], device_id=dev, device_id_type=MESH)


ALL_PEERS = tuple(range(1, N_DEV))
SIBLING = 1
SAME_CORE = (2, 4, 6)


def _split_start(srcs, lands, scatter, *, name, relations=None):
    n = len(srcs)
    relations = relations or [ALL_PEERS] * n

    def body(*refs):
        src, land, send, recv, token = refs[:n], refs[n:2 * n], refs[2 * n:3 * n], refs[3 * n:4 * n], refs[-1]
        x, y, c = _mesh_pos()
        me = 4 * x + 2 * y + c
        for a in range(n):
            for k in relations[a]:
                _split_copy(src[a], land[a], send[a], recv[a], k, me, scatter, False).start()
        token[...] = jnp.zeros_like(token)

    outs = pl.pallas_call(
        body, name=name,
        out_shape=[pltpu.SemaphoreType.DMA((N_DEV,))] * (2 * n) + [pltpu.HBM(t.shape, t.dtype) for t in list(srcs) + list(lands)]
        + [jax.ShapeDtypeStruct((8, LANE), F32)],
        in_specs=[_ANY] * (2 * n), out_specs=[_SEM] * (2 * n) + [_ANY] * (2 * n) + [pl.BlockSpec(memory_space=pltpu.VMEM)],
        input_output_aliases={i: 2 * n + i for i in range(2 * n)},
        compiler_params=pltpu.CompilerParams(has_side_effects=_EFFECT),
    )(*[_in_hbm(t) for t in list(srcs) + list(lands)])
    handles = [(outs[a], outs[n + a], outs[2 * n + a], outs[3 * n + a]) for a in range(n)]
    return handles, outs[-1]


def _split_wait(handle, after, scatter, *, name):
    send, recv, src_thru, land_thru = handle

    def body(src_ref, land_ref, send_ref, recv_ref, after_ref, src_out, land_out):
        x, y, c = _mesh_pos()
        me = 4 * x + 2 * y + c
        for k in range(1, N_DEV):
            cp = _split_copy(src_ref, land_ref, send_ref, recv_ref, k, me, scatter, True)
            cp.wait_send()
            cp.wait_recv()

    return pl.pallas_call(
        body, name=name,
        out_shape=(pltpu.HBM(src_thru.shape, src_thru.dtype), pltpu.HBM(land_thru.shape, land_thru.dtype)),
        in_specs=(_ANY, _ANY, _SEM, _SEM, _ANY), out_specs=(_ANY, _ANY), input_output_aliases={0: 0, 1: 1},
        compiler_params=pltpu.CompilerParams(has_side_effects=_EFFECT),
    )(src_thru, land_thru, send, recv, after)[1]


def _forward_copy(land, fsend, frecv, k, landed):
    x, y, c = _mesh_pos()
    _, idx = _peer(k | SIBLING if landed else k)
    return pltpu.make_async_remote_copy(src_ref=land.at[idx], dst_ref=land.at[idx], send_sem=fsend.at[k],
                                        recv_sem=frecv.at[k], device_id=(x, y, 1 - c), device_id_type=MESH)


def _gather_forward(handle, after, *, name):
    send, recv, src_thru, land_thru = handle

    def body(src_ref, land_ref, send_ref, recv_ref, after_ref, src_out, land_out, fsend, frecv):
        x, y, c = _mesh_pos()
        me = 4 * x + 2 * y + c
        for k in SAME_CORE:
            _split_copy(src_ref, land_ref, send_ref, recv_ref, k, me, False, True).wait_recv()
            _forward_copy(land_ref, fsend, frecv, k, False).start()

    src2, land2, fsend, frecv = pl.pallas_call(
        body, name=name,
        out_shape=(pltpu.HBM(src_thru.shape, src_thru.dtype), pltpu.HBM(land_thru.shape, land_thru.dtype),
                   pltpu.SemaphoreType.DMA((N_DEV,)), pltpu.SemaphoreType.DMA((N_DEV,))),
        in_specs=(_ANY, _ANY, _SEM, _SEM, _ANY), out_specs=(_ANY, _ANY, _SEM, _SEM), input_output_aliases={0: 0, 1: 1},
        compiler_params=pltpu.CompilerParams(has_side_effects=_EFFECT),
    )(src_thru, land_thru, send, recv, after)
    return (send, recv, src2, land2), (fsend, frecv)


def _gather_wait_two_level(handle, fwd, *, name):
    send, recv, src_thru, land_thru = handle
    fsend, frecv = fwd

    def body(src_ref, land_ref, send_ref, recv_ref, fsend_ref, frecv_ref, src_out, land_out):
        x, y, c = _mesh_pos()
        me = 4 * x + 2 * y + c
        for k in (SIBLING,) + SAME_CORE:
            _split_copy(src_ref, land_ref, send_ref, recv_ref, k, me, False, True).wait_send()
        _split_copy(src_ref, land_ref, send_ref, recv_ref, SIBLING, me, False, True).wait_recv()
        for k in SAME_CORE:
            _forward_copy(land_ref, fsend_ref, frecv_ref, k, False).wait_send()
            _forward_copy(land_ref, fsend_ref, frecv_ref, k, True).wait_recv()

    return pl.pallas_call(
        body, name=name,
        out_shape=(pltpu.HBM(src_thru.shape, src_thru.dtype), pltpu.HBM(land_thru.shape, land_thru.dtype)),
        in_specs=(_ANY, _ANY, _SEM, _SEM, _SEM, _SEM), out_specs=(_ANY, _ANY), input_output_aliases={0: 0, 1: 1},
        compiler_params=pltpu.CompilerParams(has_side_effects=_EFFECT),
    )(src_thru, land_thru, send, recv, fsend, frecv)[1]


def _local_step(x, p, tgt, S, wt, conv, emit):
    T, D = x.shape
    CW = DNW = D // 2
    H = DNW // HEAD
    nA, nD = CW // LANE, DNW // LANE
    qkv_off, z_off, ab_off = 3 * nA, 3 * nA + 3 * nD, 3 * nA + 4 * nD
    alog = jnp.pad(S["a_log"], ((0, 0), (0, LANE - H)))
    dtb = jnp.pad(S["dt_bias"], ((0, 0), (0, LANE - H)))

    h1 = _rms_fwd(x, S["g_mix"], name="rms1_fwd")
    pp = _matmul(p, wt("w_pp", h1), "nn", name="mm_pp", b_shards=True)
    w_in, cv = wt("w_in", pp), conv(pp)
    proj = _matmul(h1, w_in, "nt", name="mm_in")
    y_a = _group_a_fwd(proj, cv["conv_a"], CW, D, name="group_a_fwd")
    qkv = _qkv_fwd(proj, cv["conv_qkv"], qkv_off, H, name="qkv_fwd")
    gamB, bB = _gates_fwd(proj, alog, dtb, ab_off, H, name="gates_fwd")
    u, w, qd, kd, qk, ti, gl = _delta_prep_fwd(qkv, gamB, bB, H, name="delta_prep_fwd")
    o, vn, ss = _delta_scan_fwd(u, w, qd, kd, qk, gl, H, name="delta_scan_fwd")
    ycat = _gated_norm_fwd(o, proj, S["dn_g"], z_off, y_a, name="gated_norm_fwd")
    w_out = wt("w_out", ycat)
    rows = dict(tm=ROW_TILE, tn=D)
    x1, h2 = _matmul(ycat, w_out, "nn", name="mm_out", out_dtypes=(F32, BF16), epilogue=_epi_residual_rms,
                     extras=(x,), vec_extras=(S["g_ffn"],), **rows)
    w_up = wt("w_up", h2)
    up_pre = _matmul(h2, w_up, "nn", name="mm_up", b_shards=True, tn=SHARD_TILE, out_lanes=True)
    act = _ffn_act_fwd(up_pre, cv["conv_ffn"], name="ffn_act_fwd")
    w_down = wt("w_down", act)
    x2 = _matmul(act, w_down, "nn", name="mm_down", epilogue=lambda acc, r: (acc + r,), extras=(x1,), tk=LONG_K)
    h3 = _rms_fwd(x2, S["g_ple"], name="rms3_fwd")
    w_pg = wt("w_pg", h3)

    def ple_epi(acc, x2r, ppr):
        s = jax.nn.sigmoid(acc)
        return x2r + s * ppr, s

    x3, sg = _matmul(h3, w_pg, "nn", name="mm_pg", out_dtypes=(F32, F32), epilogue=ple_epi, extras=(x2, pp), tm=512)
    dx3, dg_final, loss, dpg, dpp = _final_loss(x3, S["g_final"], tgt, pp, sg, name="final_loss")

    G = {"g_final": dg_final}
    tok = emit({"w_pp": _matmul(p, dpp, "tn", name="mm_dwpp", out_dtypes=(BF16,), out_shards=True, tk=LONG_K),
                "w_pg": _matmul(h3, dpg, "tn", name="mm_dwpg", out_dtypes=(BF16,), tk=LONG_K)})
    bwd = dict(out_dtypes=(F32, BF16), epilogue=_epi_rms_bwd(2), n_vec=1, **rows)
    dx2, dx2b, G["g_ple"] = _matmul(dpg, w_pg, "nt", name="mm_dh3", after=tok, extras=(x2, dx3),
                                    vec_extras=(S["g_ple"],), **bwd)
    tok = emit({"w_down": _matmul(act, dx2b, "tn", name="mm_dwdown", out_dtypes=(BF16,), tk=LONG_K)})
    dact = _matmul(dx2b, w_down, "nt", name="mm_dact", after=tok, tn=SHARD_TILE)
    dup, dcf_g, dcf_v = _ffn_act_bwd(up_pre, cv["conv_ffn"], dact, name="ffn_act_bwd")
    G["conv_ffn"] = jnp.concatenate([dcf_g, dcf_v], axis=1)
    tok = emit({"w_up": _matmul(h2, dup, "tn", name="mm_dwup", out_dtypes=(BF16,), b_shards=True, out_shards=True,
                                tn=SHARD_TILE, tk=LONG_K)})
    dh2 = _matmul(dup, w_up, "nt", name="mm_dh2", after=tok, a_shards=True, b_shards=True, tk=2 * SHARD_TILE)
    dx1, dx1b, G["g_ffn"] = _rms_bwd(x1, S["g_ffn"], dh2, dx2, name="rms2_bwd")
    tok = emit({"w_out": _matmul(ycat, dx1b, "tn", name="mm_dwout", out_dtypes=(BF16,), tk=LONG_K)})
    dycat = _matmul(dx1b, w_out, "nt", name="mm_dycat", after=tok)
    do, dz, G["dn_g"] = _gated_norm_bwd(o, proj, S["dn_g"], dycat, z_off, nA, name="gated_norm_bwd")
    du, dw, dqd, dkd, dqk, dgl = _delta_scan_bwd(do, w, qd, kd, vn, qk, gl, ss, H, name="delta_scan_bwd")
    dq, dk, dv, dgB, dbB = _delta_prep_bwd(qkv, gamB, bB, ti, u, w, qk, du, dw, dqd, dkd, dqk, dgl, H,
                                           name="delta_prep_bwd")
    dab, dal, ddt = _gates_bwd(proj, alog, dtb, dgB, dbB, ab_off, H, name="gates_bwd")
    G["a_log"], G["dt_bias"] = dal[:, :H], ddt[:, :H]
    dqkv, G["conv_qkv"] = _qkv_bwd(proj, cv["conv_qkv"], dq, dk, dv, qkv_off, H, name="qkv_bwd")
    dax, dab_, dac, G["conv_a"] = _group_a_bwd(proj, cv["conv_a"], dycat, CW, name="group_a_bwd")
    in_p = w_in.shape[0]
    dproj = jnp.concatenate([dax, dab_, dac, dqkv, dz, dab, jnp.zeros((T, in_p - (ab_off + 1) * LANE), BF16)], axis=1)
    tok = emit({"w_in": _matmul(dproj, h1, "tn", name="mm_dwin", out_dtypes=(BF16,), tk=LONG_K)})
    dh1 = _matmul(dproj, w_in, "nn", name="mm_dh1", after=tok, tk=LONG_K)
    grad_x, _, G["g_mix"] = _rms_bwd(x, S["g_mix"], dh1, dx1, name="rms1_bwd")
    return loss, grad_x, G


def _col_sharded(landed):
    _, R, C = landed.shape
    return jnp.transpose(landed, (1, 0, 2)).reshape(R, N_DEV * C)


def kernel(x, p, norm_mix_g, w_in, conv_a_w, conv_qkv_w, a_log, dt_bias, dn_norm_g, w_out, norm_ffn_g, w_up, conv_ffn_w, w_down, norm_ple_g, w_ple_gate, w_ple_proj, final_norm_g, loss_target, m_norm_mix_g, m_w_in, m_conv_a_w, m_conv_qkv_w, m_a_log, m_dt_bias, m_dn_norm_g, m_w_out, m_norm_ffn_g, m_w_up, m_conv_ffn_w, m_w_down, m_norm_ple_g, m_w_ple_gate, m_w_ple_proj, m_final_norm_g, v_norm_mix_g, v_w_in, v_conv_a_w, v_conv_qkv_w, v_a_log, v_dt_bias, v_dn_norm_g, v_w_out, v_norm_ffn_g, v_w_up, v_conv_ffn_w, v_w_down, v_norm_ple_g, v_w_ple_gate, v_w_ple_proj, v_final_norm_g):
    T, D = x.shape[1], x.shape[2]
    xd, _, cd = _mesh_pos()
    me = 4 * xd + 2 * lax.axis_index("y") + cd

    conv_sh = [conv_a_w[0], conv_qkv_w[0], conv_ffn_w[0]]
    conv_n = [c.size for c in conv_sh]
    pack_rows = -(-sum(conv_n) // LANE)
    conv_pack = jnp.pad(jnp.concatenate([c.reshape(-1) for c in conv_sh]), (0, pack_rows * LANE - sum(conv_n))).reshape(pack_rows, LANE)
    names = ["w_pp", "w_in", "conv", "w_out", "w_up", "w_down", "w_pg"]
    tr_ = lambda t: jnp.swapaxes(t, 1, 2)
    shards = [w_ple_proj[0].astype(BF16), w_in[0].T.astype(BF16), conv_pack, w_out[0].astype(BF16), w_up[0].astype(BF16),
              w_down[0].astype(BF16), w_ple_gate[0].astype(BF16)]
    empty_slots = lambda blocks: [lax.empty((N_DEV,) + tuple(b.shape), b.dtype) for b in blocks]
    handles, tok0 = _split_start(shards, empty_slots(shards), False, name="gather_start",
                                 relations=[(SIBLING,) + SAME_CORE if nm == "w_in" else ALL_PEERS for nm in names])
    handle = dict(zip(names, handles))
    own = dict(zip(names, shards))
    in_cols = N_DEV * w_in.shape[2]
    in_p = (in_cols // LANE) * LANE + AB_PAD
    in_place = {"w_up", "w_pp"}

    def gathered(name, after):
        if name == "w_in":
            passed, fwd = _gather_forward(handle[name], after, name="gather_forward_w_in")
            landed = _gather_wait_two_level(passed, fwd, name="gather_wait_w_in")
        else:
            landed = _split_wait(handle[name], after, False, name="gather_wait_" + name)
        return lax.dynamic_update_index_in_dim(landed, own[name], me, 0)

    def wt(name, after):
        landed = gathered(name, after)
        if name in in_place:
            return landed
        full = landed.reshape(-1, D)
        return jnp.pad(full, ((0, in_p - in_cols), (0, 0))) if name == "w_in" else full

    def conv(after):
        flat = gathered("conv", after).reshape(N_DEV, pack_rows * LANE)
        out, o_ = {}, 0
        for nm, c, n_ in zip(("conv_a", "conv_qkv", "conv_ffn"), conv_sh, conv_n):
            out[nm] = _col_sharded(flat[:, o_:o_ + n_].reshape((N_DEV,) + c.shape))
            o_ += n_
        return out

    pending, mine = {}, {}

    def emit(grads):
        parts = [g if nm in in_place else (g[:in_cols] if nm == "w_in" else g).reshape(N_DEV, -1, D)
                 for nm, g in grads.items()]
        hs, tok = _split_start(parts, empty_slots([q[0] for q in parts]), True, name="scatter_start_" + "_".join(grads))
        pending.update(zip(grads, hs))
        mine.update({nm: lax.dynamic_index_in_dim(q, me, 0, keepdims=False) for nm, q in zip(grads, parts)})
        return tok

    S = {
        "g_mix": norm_mix_g + tok0[0, 0], "a_log": a_log, "dt_bias": dt_bias, "dn_g": dn_norm_g, "g_ffn": norm_ffn_g,
        "g_ple": norm_ple_g, "g_final": final_norm_g.reshape(1, D),
    }

    loss_v, grad_x, G = _local_step(x[0], p[0, 0], loss_target[0], S, wt, conv, emit)
    loss = lax.psum(loss_v[0, 0], ("x", "y", "c"))

    small_names = ["g_mix", "g_ffn", "g_ple", "g_final", "dn_g", "a_log", "dt_bias", "conv_a", "conv_qkv", "conv_ffn"]
    small_rows, pieces = [], []
    for nm in small_names:
        g_ = G[nm].reshape(-1)
        r_ = -(-g_.size // (8 * LANE)) * 8
        small_rows.append(r_)
        pieces.append(jnp.pad(g_, (0, r_ * LANE - g_.size)).reshape(r_, LANE))
    landed = {nm: _split_wait(h_, grad_x, True, name="scatter_wait_" + nm) for nm, h_ in pending.items() if nm != "w_in"}

    def adam(parts, w_, m_, v_, nm, own_=None):
        shp = w_.shape
        w2, m2, v2 = (t.reshape(parts.shape[1:]) for t in (w_, m_, v_))
        kw = {} if own_ is None else {"own": own_, "me": me.astype(jnp.int32).reshape(1)}
        return tuple(t.reshape(shp) for t in _adam(parts, w2, m2, v2, name="adam_" + nm, **kw))

    big = {
        "w_up": adam(landed["w_up"], w_up, m_w_up, v_w_up, "w_up", mine["w_up"]),
        "w_down": adam(landed["w_down"], w_down, m_w_down, v_w_down, "w_down", mine["w_down"]),
        "w_out": adam(landed["w_out"], w_out, m_w_out, v_w_out, "w_out", mine["w_out"]),
        "w_pg": adam(landed["w_pg"], w_ple_gate, m_w_ple_gate, v_w_ple_gate, "w_ple_gate", mine["w_pg"]),
        "w_pp": adam(landed["w_pp"], w_ple_proj, m_w_ple_proj, v_w_ple_proj, "w_ple_proj", mine["w_pp"]),
    }
    first = lambda t: lax.slice(t, (0,) * t.ndim, (1,) * t.ndim).reshape(1)
    big_done = sum(first(r[1]) for r in big.values())
    (small_l,) = _exchange([jnp.concatenate(pieces, axis=0)], False, name="gather_small_grads", after=big_done)

    def small_parts(nm):
        i = small_names.index(nm)
        r0 = sum(small_rows[:i])
        shp = G[nm].shape
        return small_l[:, r0:r0 + small_rows[i], :].reshape(N_DEV, -1)[:, :G[nm].size].reshape((N_DEV,) + shp)

    def conv_parts(nm, shard):
        full = small_parts(nm)
        C = shard.shape[-1]
        return lax.dynamic_slice_in_dim(full, me * C, C, axis=2)

    res = [
        adam(small_parts("g_mix"), norm_mix_g, m_norm_mix_g, v_norm_mix_g, "norm_mix_g"),
        None,
        adam(conv_parts("conv_a", conv_a_w), conv_a_w, m_conv_a_w, v_conv_a_w, "conv_a_w"),
        adam(conv_parts("conv_qkv", conv_qkv_w), conv_qkv_w, m_conv_qkv_w, v_conv_qkv_w, "conv_qkv_w"),
        adam(small_parts("a_log"), a_log, m_a_log, v_a_log, "a_log"),
        adam(small_parts("dt_bias"), dt_bias, m_dt_bias, v_dt_bias, "dt_bias"),
        adam(small_parts("dn_g"), dn_norm_g, m_dn_norm_g, v_dn_norm_g, "dn_norm_g"),
        big["w_out"],
        adam(small_parts("g_ffn"), norm_ffn_g, m_norm_ffn_g, v_norm_ffn_g, "norm_ffn_g"),
        big["w_up"],
        adam(conv_parts("conv_ffn", conv_ffn_w), conv_ffn_w, m_conv_ffn_w, v_conv_ffn_w, "conv_ffn_w"),
        big["w_down"],
        adam(small_parts("g_ple"), norm_ple_g, m_norm_ple_g, v_norm_ple_g, "norm_ple_g"),
        big["w_pg"],
        big["w_pp"],
        adam(small_parts("g_final"), final_norm_g.reshape(1, D), m_final_norm_g.reshape(1, D),
             v_final_norm_g.reshape(1, D), "final_norm_g"),
    ]
    res[-1] = tuple(t.reshape(D) for t in res[-1])
    landed_in = _split_wait(pending["w_in"], res[10][1], True, name="scatter_wait_w_in")
    res[1] = tuple(tr_(t) for t in adam(landed_in, tr_(w_in), tr_(m_w_in), tr_(v_w_in), "w_in", mine["w_in"]))
    grads, deltas, new_m, new_v = zip(*res)
    return (loss, grad_x[None], *grads, *deltas, *new_m, *new_v)
```

```python
import functools

import jax
import jax.numpy as jnp
from jax import lax
from jax.experimental import pallas as pl
from jax.experimental.pallas import tpu as pltpu

F32 = jnp.float32
BF16 = jnp.bfloat16

EPS = 1e-6
CHUNK = 64
HEAD = 128
LANE = 128
N_DEV = 8
AB_PAD = 512

ADAM_LR = 0.001
ADAM_B1 = 0.9
ADAM_B2 = 0.999
ADAM_EPS = 1e-08
ADAM_WD = 0.01
ADAM_STEP = 10

MESH = pl.DeviceIdType.MESH


def _tile(dim, target, align=LANE):
    if dim <= target:
        return dim
    t = (target // align) * align
    while t > align and dim % t:
        t -= align
    assert dim % t == 0, (dim, target)
    return t


def _params(sem, vmem_mb=48):
    return pltpu.CompilerParams(dimension_semantics=sem, vmem_limit_bytes=vmem_mb << 20)


_DN = {"nn": (((1,), (0,)), ((), ())), "nt": (((1,), (1,)), ((), ())), "tn": (((0,), (0,)), ((), ()))}
LONG_K = 4096
SHARD_TILE = 1408


def _matmul(a, b, mode, *, name, out_dtypes=(F32,), epilogue=None, extras=(), vec_extras=(), n_vec=0, after=None,
            a_shards=False, b_shards=False, out_shards=False, out_lanes=False, tm=1024, tn=1024, tk=2048):
    shard_w = b.shape[2] if b_shards else None
    if b_shards:
        b_rows, b_cols = b.shape[1], b.shape[0] * shard_w
    else:
        b_rows, b_cols = b.shape
    a_w = a.shape[2] if a_shards else None
    a_dims = (a.shape[1], a.shape[0] * a_w) if a_shards else a.shape
    if mode == "nn":
        (M, K), (K2, N) = a_dims, (b_rows, b_cols)
    elif mode == "nt":
        (M, K), (N, K2) = a_dims, (b_rows, b_cols)
    else:
        (K, M), (K2, N) = a_dims, (b_rows, b_cols)
    assert K == K2, (name, a.shape, b.shape)
    tm = _tile(M, tm)
    n_dims = [N] + ([shard_w] if (b_shards and mode != "nt") else []) + ([N // N_DEV] if out_shards else [])
    tn = _tile(min(n_dims), tn)
    assert all(d % tn == 0 for d in n_dims), (name, n_dims, tn)
    grp = 1
    if b_shards and mode == "nt":
        grp = max(g for g in (1, 2, 4, 8) if g <= max(1, tk // shard_w) and (a_w is None or a_w % (g * shard_w) == 0))
    k_dims = [K] + ([shard_w] if (b_shards and mode == "nt") else []) + ([a_w] if a_shards else [])
    tk = grp * shard_w if grp > 1 else _tile(min(k_dims), tk)
    assert K % tk == 0, (name, K, tk)
    nk = K // tk
    n_ex, n_out = len(extras) + len(vec_extras), len(out_dtypes)
    assert n_vec == 0 or tn == N, (name, tn, N)
    dn = _DN[mode]

    n_tok = 0 if after is None else 1

    def body(a_ref, b_ref, *rest):
        rest = rest[n_tok:]
        ex_refs, out_refs, vec_refs = rest[:n_ex], rest[n_ex:n_ex + n_out], rest[n_ex + n_out:n_ex + n_out + n_vec]
        if grp > 1:
            part = sum(lax.dot_general(a_ref[:, s * shard_w:(s + 1) * shard_w].astype(BF16), b_ref[s].astype(BF16), dn,
                                       preferred_element_type=F32) for s in range(grp))
        else:
            part = lax.dot_general(a_ref[...].astype(BF16), b_ref[...].astype(BF16), dn, preferred_element_type=F32)
        first_rows = pl.program_id(0) == 0

        def finish(res):
            outs = (res,) if epilogue is None else epilogue(res, *[e[...] for e in ex_refs])
            for o_ref, val in zip(out_refs, outs[:n_out]):
                if out_lanes:
                    for c in range(tn // LANE):
                        o_ref[c] = val[:, c * LANE:(c + 1) * LANE].astype(o_ref.dtype)
                else:
                    o_ref[...] = val.astype(o_ref.dtype)
            for v_ref, val in zip(vec_refs, outs[n_out:]):
                @pl.when(first_rows)
                def _(v_ref=v_ref, val=val):
                    v_ref[...] = val

                @pl.when(jnp.logical_not(first_rows))
                def _(v_ref=v_ref, val=val):
                    v_ref[...] += val

        if nk == 1:
            finish(part)
            return
        acc, k = rest[-1], pl.program_id(2)

        @pl.when(k == 0)
        def _():
            acc[...] = part

        @pl.when(k > 0)
        def _():
            acc[...] += part

        @pl.when(k == nk - 1)
        def _():
            finish(acc[...])

    if a_shards:
        assert mode == "nt" and a_w % tk == 0, (name, mode, a_w, tk)
        per_a = a_w // tk
        a_spec = pl.BlockSpec((None, tm, tk), lambda i, j, k: (lax.div(k, per_a), i, lax.rem(k, per_a)))
    else:
        a_spec = pl.BlockSpec((tk, tm), lambda i, j, k: (k, i)) if mode == "tn" else pl.BlockSpec((tm, tk), lambda i, j, k: (i, k))
    if b_shards and mode != "nt":
        per = shard_w // tn
        b_spec = pl.BlockSpec((None, tk, tn), lambda i, j, k: (lax.div(j, per), k, lax.rem(j, per)))
    elif b_shards and grp > 1:
        b_spec = pl.BlockSpec((grp, tn, shard_w), lambda i, j, k: (k, j, 0))
    elif b_shards:
        per = shard_w // tk
        b_spec = pl.BlockSpec((None, tn, tk), lambda i, j, k: (lax.div(k, per), j, lax.rem(k, per)))
    else:
        b_spec = pl.BlockSpec((tn, tk), lambda i, j, k: (j, k)) if mode == "nt" else pl.BlockSpec((tk, tn), lambda i, j, k: (k, j))
    mn_spec = pl.BlockSpec((tm, tn), lambda i, j, k: (i, j))
    vec_spec = pl.BlockSpec((1, tn), lambda i, j, k: (0, j))
    if out_shards:
        assert not extras
        per_o = (N // N_DEV) // tn
        out_spec = pl.BlockSpec((None, tm, tn), lambda i, j, k: (lax.div(j, per_o), i, lax.rem(j, per_o)))
        out_dims = (N_DEV, M, N // N_DEV)
    elif out_lanes:
        assert not extras
        out_spec = pl.BlockSpec((tn // LANE, tm, LANE), lambda i, j, k: (j, i, 0))
        out_dims = (N // LANE, M, LANE)
    else:
        out_spec, out_dims = mn_spec, (M, N)
    outs = pl.pallas_call(
        body, name=name, grid=(M // tm, N // tn, nk),
        in_specs=[a_spec, b_spec] + [pl.BlockSpec((8, LANE), lambda i, j, k: (0, 0))] * n_tok
        + [mn_spec] * len(extras) + [vec_spec] * len(vec_extras),
        out_specs=[out_spec] * n_out + [vec_spec] * n_vec,
        out_shape=[jax.ShapeDtypeStruct(out_dims, dt) for dt in out_dtypes] + [jax.ShapeDtypeStruct((1, N), F32)] * n_vec,
        scratch_shapes=[pltpu.VMEM((tm, tn), F32)] if nk > 1 else [],
        compiler_params=_params(("arbitrary" if n_vec else "parallel", "parallel", "arbitrary"), 56),
    )(a, b, *([] if after is None else [after]), *extras, *vec_extras)
    return outs[0] if n_out + n_vec == 1 else outs


def _rms_fwd(x, g, *, name):
    T, D = x.shape
    tr = _tile(T, 512, 8)

    def body(x_ref, g_ref, h_ref):
        xv = x_ref[...]
        r = lax.rsqrt(jnp.mean(xv * xv, axis=-1, keepdims=True) + EPS)
        h_ref[...] = (xv * r * g_ref[...]).astype(h_ref.dtype)

    return pl.pallas_call(
        body, name=name, grid=(T // tr,),
        in_specs=[pl.BlockSpec((tr, D), lambda i: (i, 0)), pl.BlockSpec((1, D), lambda i: (0, 0))],
        out_specs=pl.BlockSpec((tr, D), lambda i: (i, 0)),
        out_shape=jax.ShapeDtypeStruct((T, D), BF16),
        compiler_params=_params(("parallel",)),
    )(x, g)


def _rms_bwd(x, g, dh, dres, *, name):
    T, D = x.shape
    tr = _tile(T, 512, 8)
    epi = _epi_rms_bwd(2)

    def body(x_ref, g_ref, dh_ref, dres_ref, dx_ref, dxb_ref, dg_ref):
        dx, _, dgp = epi(dh_ref[...], x_ref[...], dres_ref[...], g_ref[...])

        @pl.when(pl.program_id(0) == 0)
        def _():
            dg_ref[...] = jnp.zeros_like(dg_ref)

        dg_ref[...] += dgp
        dx_ref[...] = dx
        dxb_ref[...] = dx.astype(dxb_ref.dtype)

    row = pl.BlockSpec((tr, D), lambda i: (i, 0))
    vec = pl.BlockSpec((1, D), lambda i: (0, 0))
    return pl.pallas_call(
        body, name=name, grid=(T // tr,),
        in_specs=[row, vec, row, row], out_specs=[row, row, vec],
        out_shape=[jax.ShapeDtypeStruct((T, D), F32), jax.ShapeDtypeStruct((T, D), BF16), jax.ShapeDtypeStruct((1, D), F32)],
        compiler_params=_params(("arbitrary",)),
    )(x, g, dh, dres)


ROW_TILE = 512


def _epi_residual_rms(acc, res, g):
    xn = acc + res
    r = lax.rsqrt(jnp.mean(xn * xn, axis=-1, keepdims=True) + EPS)
    return xn, xn * r * g


def _epi_rms_bwd(n_copies):
    def epi(dh, x, dres, g):
        r = lax.rsqrt(jnp.mean(x * x, axis=-1, keepdims=True) + EPS)
        xh = x * r
        dxh = dh * g
        dx = dres + r * (dxh - xh * jnp.mean(dxh * xh, axis=-1, keepdims=True))
        return (dx,) * n_copies + (jnp.sum(dh * xh, axis=0, keepdims=True),)
    return epi


def _final_loss(x, g, tgt, pp, sg, *, name):
    T, D = x.shape
    tr = _tile(T, 256, 8)

    def body(x_ref, g_ref, t_ref, pp_ref, sg_ref, dx_ref, dg_ref, loss_ref, dpg_ref, dpp_ref):
        xv = x_ref[...]
        r = lax.rsqrt(jnp.mean(xv * xv, axis=-1, keepdims=True) + EPS)
        xh = xv * r
        gv = g_ref[...]
        err = xh * gv - t_ref[...]

        @pl.when(pl.program_id(0) == 0)
        def _():
            dg_ref[...] = jnp.zeros_like(dg_ref)
            loss_ref[...] = jnp.zeros_like(loss_ref)

        part = 0.5 * jnp.sum(jnp.mean(err * err, axis=-1, keepdims=True), axis=0, keepdims=True)
        loss_ref[...] += jnp.broadcast_to(part, loss_ref.shape)
        dy = err * (1.0 / D)
        dg_ref[...] += jnp.sum(dy * xh, axis=0, keepdims=True)
        dxh = dy * gv
        dx = r * (dxh - xh * jnp.mean(dxh * xh, axis=-1, keepdims=True))
        dx_ref[...] = dx
        s = sg_ref[...]
        dpg_ref[...] = (dx * pp_ref[...] * s * (1.0 - s)).astype(dpg_ref.dtype)
        dpp_ref[...] = (dx * s).astype(dpp_ref.dtype)

    row = pl.BlockSpec((tr, D), lambda i: (i, 0))
    vec = pl.BlockSpec((1, D), lambda i: (0, 0))
    return pl.pallas_call(
        body, name=name, grid=(T // tr,),
        in_specs=[row, vec, row, row, row], out_specs=[row, vec, pl.BlockSpec((1, LANE), lambda i: (0, 0)), row, row],
        out_shape=[jax.ShapeDtypeStruct((T, D), F32), jax.ShapeDtypeStruct((1, D), F32),
                   jax.ShapeDtypeStruct((1, LANE), F32)] + [jax.ShapeDtypeStruct((T, D), BF16)] * 2,
        compiler_params=_params(("arbitrary",)),
    )(x, g, tgt, pp, sg)


ROWS_QKV_FWD, ROWS_QKV_BWD, ROWS_FFN_FWD, ROWS_FFN_BWD, ROWS_GROUP_A = 512, 256, 256, 128, 256


def _ext(ref, r0, T, before, after, RC):
    parts = []
    if before:
        p0 = pl.multiple_of(jnp.maximum(r0 - 8, 0), 8)
        parts.append(jnp.where(r0 > 0, ref[pl.ds(p0, 8), :], 0.0))
    parts.append(ref[pl.ds(r0, RC), :])
    if after:
        n0 = pl.multiple_of(jnp.minimum(r0 + RC, T - 8), 8)
        parts.append(jnp.where(r0 + RC < T, ref[pl.ds(n0, 8), :], 0.0))
    return parts[0] if len(parts) == 1 else jnp.concatenate(parts, axis=0)


def _fold8(x):
    return jnp.sum(x.reshape(x.shape[0] // 8, 8, x.shape[1]), axis=0)


def _win(ref, r0, lo, n, T, RC, edge):
    if not edge:
        return ref[pl.ds(r0 + lo, n), :]
    xx = _ext(ref, r0, T, True, True, RC)
    a = 8 + lo
    return (xx if a == 0 else pltpu.roll(xx, xx.shape[0] - a, 0))[:n, :]


def _taps(ref, w_ref, K, r0, n, T, RC, edge):
    wins = [_win(ref, r0, -(K - 1 - j), n, T, RC, edge) for j in range(K)]
    y = wins[0] * w_ref[0:1, :]
    for j in range(1, K):
        y = y + wins[j] * w_ref[j:j + 1, :]
    return wins, y


def _untaps(scr_ref, val, w_ref, K, RC):
    scr_ref[0:val.shape[0], :] = val
    y = scr_ref[K - 1:K - 1 + RC, :] * w_ref[0:1, :]
    for j in range(1, K):
        s = K - 1 - j
        y = y + scr_ref[s:s + RC, :] * w_ref[j:j + 1, :]
    return y


def _peeled(n_chunks, RC, step, init):
    carry = step(0, init, True)
    if n_chunks > 2:
        carry = lax.fori_loop(1, n_chunks - 1, lambda i, c: step(pl.multiple_of(i * RC, RC), c, False), carry)
    if n_chunks > 1:
        carry = step((n_chunks - 1) * RC, carry, True)
    return carry


def _silu(x):
    return x * jax.nn.sigmoid(x)


def _dsilu(x):
    s = jax.nn.sigmoid(x)
    return s * (1.0 + x * (1.0 - s))


def _col_specs(T, offs):
    return [pl.BlockSpec((T, LANE), functools.partial(lambda o, j: (0, o + j), o)) for o in offs]


def _group_a_fwd(proj, conv_w, CW, out_cols, *, name):
    T = proj.shape[0]
    RC = _tile(T, ROWS_GROUP_A, 8)
    nb = CW // LANE
    K = conv_w.shape[0]

    def body(ax_ref, ab_ref, ac_ref, w_ref, y_ref):
        def step(r0, carry, edge):
            c = None
            for j in range(K):
                lo = -(K - 1 - j)
                t = _win(ac_ref, r0, lo, RC, T, RC, edge) * _win(ax_ref, r0, lo, RC, T, RC, edge) * w_ref[j:j + 1, :]
                c = t if c is None else c + t
            y_ref[pl.ds(r0, RC), :] = (ab_ref[pl.ds(r0, RC), :] * c).astype(y_ref.dtype)
            return carry
        _peeled(T // RC, RC, step, 0)

    return pl.pallas_call(
        body, name=name, grid=(nb,),
        in_specs=_col_specs(T, (0, nb, 2 * nb)) + [pl.BlockSpec((K, LANE), lambda j: (0, j))],
        out_specs=pl.BlockSpec((T, LANE), lambda j: (0, j)),
        out_shape=jax.ShapeDtypeStruct((T, out_cols), BF16), compiler_params=_params(("parallel",)),
    )(proj, proj, proj, conv_w)


def _group_a_bwd(proj, conv_w, dycat, CW, *, name):
    T = proj.shape[0]
    RC = _tile(T, ROWS_GROUP_A, 8)
    nb = CW // LANE
    K = conv_w.shape[0]

    def body(ax_ref, ab_ref, ac_ref, w_ref, dy_ref, dax_ref, dab_ref, dac_ref, dw_ref, scr_ref):
        def step(r0, accs, edge):
            ms = [_win(ac_ref, r0, -(K - 1 - j), RC, T, RC, edge) * _win(ax_ref, r0, -(K - 1 - j), RC, T, RC, edge)
                  for j in range(K)]
            c = ms[0] * w_ref[0:1, :]
            for j in range(1, K):
                c = c + ms[j] * w_ref[j:j + 1, :]
            dy = dy_ref[pl.ds(r0, RC), :]
            dab_ref[pl.ds(r0, RC), :] = (dy * c).astype(dab_ref.dtype)
            dc2 = _win(dy_ref, r0, 0, RC + 8, T, RC, edge) * _win(ab_ref, r0, 0, RC + 8, T, RC, edge)
            dm = _untaps(scr_ref, dc2, w_ref, K, RC)
            dax_ref[pl.ds(r0, RC), :] = (dm * ac_ref[pl.ds(r0, RC), :]).astype(dax_ref.dtype)
            dac_ref[pl.ds(r0, RC), :] = (dm * ax_ref[pl.ds(r0, RC), :]).astype(dac_ref.dtype)
            return tuple(accs[j] + _fold8(dc2[:RC] * ms[j]) for j in range(K))

        accs = _peeled(T // RC, RC, step, tuple(jnp.zeros((8, LANE), F32) for _ in range(K)))
        for j in range(K):
            dw_ref[j:j + 1, :] = jnp.sum(accs[j], axis=0, keepdims=True)

    col = pl.BlockSpec((T, LANE), lambda j: (0, j))
    wsp = pl.BlockSpec((K, LANE), lambda j: (0, j))
    return pl.pallas_call(
        body, name=name, grid=(nb,),
        in_specs=_col_specs(T, (0, nb, 2 * nb)) + [wsp, col],
        out_specs=[col, col, col, wsp],
        out_shape=[jax.ShapeDtypeStruct((T, CW), BF16)] * 3 + [jax.ShapeDtypeStruct((K, CW), F32)],
        scratch_shapes=[pltpu.VMEM((RC + 8, LANE), F32)],
        compiler_params=_params(("parallel",)),
    )(proj, proj, proj, conv_w, dycat)


def _qkv_fwd(proj, conv_w, off, H, *, name):
    T = proj.shape[0]
    RC = _tile(T, ROWS_QKV_FWD, 8)
    nb = 3 * H
    K = conv_w.shape[0]

    def body(x_ref, w_ref, y_ref):
        j = pl.program_id(0)
        is_qk = j < 2 * H
        scale = jnp.where(j < H, HEAD ** -0.5, 1.0).astype(F32)

        def step(r0, carry, edge):
            s = _silu(_taps(x_ref, w_ref, K, r0, RC, T, RC, edge)[1])
            r = lax.rsqrt(jnp.sum(s * s, axis=-1, keepdims=True) + EPS) * scale
            y_ref[pl.ds(r0, RC), :] = s * jnp.where(is_qk, r, 1.0)
            return carry
        _peeled(T // RC, RC, step, 0)

    return pl.pallas_call(
        body, name=name, grid=(nb,),
        in_specs=_col_specs(T, (off,)) + [pl.BlockSpec((K, LANE), lambda j: (0, j))],
        out_specs=pl.BlockSpec((T, LANE), lambda j: (0, j)),
        out_shape=jax.ShapeDtypeStruct((T, nb * LANE), F32), compiler_params=_params(("parallel",)),
    )(proj, conv_w)


def _qkv_bwd(proj, conv_w, dq, dk, dv, off, H, *, name):
    T = proj.shape[0]
    RC = _tile(T, ROWS_QKV_BWD, 8)
    nb = 3 * H
    K = conv_w.shape[0]

    def body(x_ref, w_ref, dq_ref, dk_ref, dv_ref, dx_ref, dw_ref, scr_ref):
        j = pl.program_id(0)
        is_qk = j < 2 * H
        scale = jnp.where(j < H, HEAD ** -0.5, 1.0).astype(F32)

        def step(r0, accs, edge):
            xs, c2 = _taps(x_ref, w_ref, K, r0, RC + 8, T, RC, edge)
            s2 = _silu(c2)
            dn2 = jnp.where(j < H, _win(dq_ref, r0, 0, RC + 8, T, RC, edge),
                            jnp.where(is_qk, _win(dk_ref, r0, 0, RC + 8, T, RC, edge),
                                      _win(dv_ref, r0, 0, RC + 8, T, RC, edge)))
            r = lax.rsqrt(jnp.sum(s2 * s2, axis=-1, keepdims=True) + EPS)
            nh = s2 * r
            dnp = dn2 * scale
            ds_qk = r * (dnp - nh * jnp.sum(dnp * nh, axis=-1, keepdims=True))
            ds2 = jnp.where(is_qk, ds_qk, dn2)
            dc2 = ds2 * _dsilu(c2)
            dx_ref[pl.ds(r0, RC), :] = _untaps(scr_ref, dc2, w_ref, K, RC).astype(dx_ref.dtype)
            return tuple(accs[jj] + _fold8(dc2[:RC] * xs[jj][:RC]) for jj in range(K))

        accs = _peeled(T // RC, RC, step, tuple(jnp.zeros((8, LANE), F32) for _ in range(K)))
        for jj in range(K):
            dw_ref[jj:jj + 1, :] = jnp.sum(accs[jj], axis=0, keepdims=True)

    col = pl.BlockSpec((T, LANE), lambda j: (0, j))
    wsp = pl.BlockSpec((K, LANE), lambda j: (0, j))
    return pl.pallas_call(
        body, name=name, grid=(nb,),
        in_specs=_col_specs(T, (off,)) + [wsp] + [
            pl.BlockSpec((T, LANE), functools.partial(lambda o, j: (0, jnp.clip(j - o, 0, H - 1)), o)) for o in (0, H, 2 * H)],
        out_specs=[col, wsp],
        out_shape=[jax.ShapeDtypeStruct((T, nb * LANE), BF16), jax.ShapeDtypeStruct((K, nb * LANE), F32)],
        scratch_shapes=[pltpu.VMEM((RC + 8, LANE), F32)],
        compiler_params=_params(("parallel",)),
    )(proj, conv_w, dq, dk, dv)


def _softplus(x):
    return jnp.maximum(x, 0.0) + jnp.log(1.0 + jnp.exp(-jnp.abs(x)))


def _gates_fwd(proj, alog, dtb, off, H, *, name):
    T = proj.shape[0]
    tr = _tile(T, 512, CHUNK)

    def body(ab_ref, al_ref, dt_ref, gam_ref, beta_ref):
        ab = ab_ref[...]
        lane = lax.broadcasted_iota(jnp.int32, ab.shape, 1)
        g = -jnp.exp(al_ref[...]) * _softplus(ab + dt_ref[...])
        gb = jnp.where(lane < H, g, jnp.where(lane < 2 * H, jax.nn.sigmoid(ab), 0.0))
        tril = _tri().astype(F32)
        gam = jnp.concatenate([_mm(tril, gb[c * CHUNK:(c + 1) * CHUNK, :], precision=lax.Precision.HIGHEST)
                               for c in range(tr // CHUNK)], axis=0)
        for h in range(H):
            gam_ref[h] = jnp.broadcast_to(gam[:, h:h + 1], (tr, LANE))
            beta_ref[h] = jnp.broadcast_to(gb[:, H + h:H + h + 1], (tr, LANE))

    vec = pl.BlockSpec((1, LANE), lambda i: (0, 0))
    heads = pl.BlockSpec((H, tr, LANE), lambda i: (0, i, 0))
    return pl.pallas_call(
        body, name=name, grid=(T // tr,),
        in_specs=[pl.BlockSpec((tr, LANE), lambda i: (i, off)), vec, vec],
        out_specs=[heads, heads],
        out_shape=[jax.ShapeDtypeStruct((H, T, LANE), F32)] * 2, compiler_params=_params(("parallel",)),
    )(proj, alog, dtb)


def _gates_bwd(proj, alog, dtb, dgamB, dbB, off, H, *, name):
    T = proj.shape[0]
    tr = _tile(T, 512, CHUNK)

    def body(ab_ref, al_ref, dt_ref, dgam_ref, dbeta_ref, dab_ref, dal_ref, ddt_ref):
        ab = ab_ref[...]
        lane = lax.broadcasted_iota(jnp.int32, ab.shape, 1)
        is_g = lane < H
        d = jnp.zeros_like(ab)
        for h in range(H):
            d = jnp.where(lane == h, dgam_ref[h], jnp.where(lane == H + h, dbeta_ref[h], d))
        triu = _tri(upper=True).astype(F32)
        dg = jnp.concatenate([_mm(triu, d[c * CHUNK:(c + 1) * CHUNK, :], precision=lax.Precision.HIGHEST)
                              for c in range(tr // CHUNK)], axis=0)
        z = ab + dt_ref[...]
        A = -jnp.exp(al_ref[...])
        da = dg * A * jax.nn.sigmoid(z)
        beta = jax.nn.sigmoid(ab)
        db = d * beta * (1.0 - beta)
        dab_ref[...] = jnp.where(is_g, da, jnp.where(lane < 2 * H, db, 0.0)).astype(dab_ref.dtype)

        @pl.when(pl.program_id(0) == 0)
        def _():
            dal_ref[...] = jnp.zeros_like(dal_ref)
            ddt_ref[...] = jnp.zeros_like(ddt_ref)

        dal_ref[...] += jnp.sum(jnp.where(is_g, dg * A * _softplus(z), 0.0), axis=0, keepdims=True)
        ddt_ref[...] += jnp.sum(jnp.where(is_g, da, 0.0), axis=0, keepdims=True)

    vec = pl.BlockSpec((1, LANE), lambda i: (0, 0))
    row = pl.BlockSpec((tr, LANE), lambda i: (i, 0))
    heads = pl.BlockSpec((H, tr, LANE), lambda i: (0, i, 0))
    return pl.pallas_call(
        body, name=name, grid=(T // tr,),
        in_specs=[pl.BlockSpec((tr, LANE), lambda i: (i, off)), vec, vec, heads, heads],
        out_specs=[row, vec, vec],
        out_shape=[jax.ShapeDtypeStruct((T, LANE), BF16), jax.ShapeDtypeStruct((1, LANE), F32),
                   jax.ShapeDtypeStruct((1, LANE), F32)],
        compiler_params=_params(("arbitrary",)),
    )(proj, alog, dtb, dgamB, dbB)


def _gated_norm_fwd(o, proj, gn, zoff, ycat, *, name):
    T, W = o.shape
    tr = _tile(T, 512, 8)
    nh_, zblk = W // LANE, (zoff * LANE) // W
    assert zblk * W == zoff * LANE

    def body(o_ref, z_ref, g_ref, ycat_ref, y_ref):
        for h in range(nh_):
            ln = slice(h * LANE, (h + 1) * LANE)
            ov = o_ref[:, ln]
            r = lax.rsqrt(jnp.mean(ov * ov, axis=-1, keepdims=True) + EPS)
            y_ref[:, ln] = (ov * r * g_ref[...] * _silu(z_ref[:, ln])).astype(y_ref.dtype)

    assert ycat.shape == (T, 2 * W), ycat.shape
    blk = pl.BlockSpec((tr, W), lambda i: (i, 0))
    return pl.pallas_call(
        body, name=name, grid=(T // tr,),
        in_specs=[blk, pl.BlockSpec((tr, W), lambda i: (i, zblk)), pl.BlockSpec((1, LANE), lambda i: (0, 0)),
                  pl.BlockSpec(memory_space=pl.ANY)],
        out_specs=pl.BlockSpec((tr, W), lambda i: (i, 1)), out_shape=jax.ShapeDtypeStruct(ycat.shape, ycat.dtype),
        input_output_aliases={3: 0}, compiler_params=_params(("parallel",)),
    )(o, proj, gn, ycat)


def _gated_norm_bwd(o, proj, gn, dycat, zoff, yoff, *, name):
    T, W = o.shape
    tr = _tile(T, 512, 8)
    nh_, zblk, yblk = W // LANE, (zoff * LANE) // W, (yoff * LANE) // W
    assert zblk * W == zoff * LANE and yblk * W == yoff * LANE

    def body(o_ref, z_ref, g_ref, dy_ref, do_ref, dz_ref, dg_ref):
        @pl.when(pl.program_id(0) == 0)
        def _():
            dg_ref[...] = jnp.zeros_like(dg_ref)

        gv = g_ref[...]
        dg = jnp.zeros_like(gv)
        for h in range(nh_):
            ln = slice(h * LANE, (h + 1) * LANE)
            ov, zv, dy = o_ref[:, ln], z_ref[:, ln], dy_ref[:, ln]
            r = lax.rsqrt(jnp.mean(ov * ov, axis=-1, keepdims=True) + EPS)
            nh = ov * r
            s = _silu(zv)
            dg = dg + jnp.sum(dy * nh * s, axis=0, keepdims=True)
            dz_ref[:, ln] = (dy * nh * gv * _dsilu(zv)).astype(dz_ref.dtype)
            dn = dy * gv * s
            do_ref[:, ln] = r * (dn - nh * jnp.mean(dn * nh, axis=-1, keepdims=True))
        dg_ref[...] += dg

    blk = pl.BlockSpec((tr, W), lambda i: (i, 0))
    vec = pl.BlockSpec((1, LANE), lambda i: (0, 0))
    return pl.pallas_call(
        body, name=name, grid=(T // tr,),
        in_specs=[blk, pl.BlockSpec((tr, W), lambda i: (i, zblk)), vec, pl.BlockSpec((tr, W), lambda i: (i, yblk))],
        out_specs=[blk, blk, vec],
        out_shape=[jax.ShapeDtypeStruct((T, W), F32), jax.ShapeDtypeStruct((T, W), BF16),
                   jax.ShapeDtypeStruct((1, LANE), F32)],
        compiler_params=_params(("arbitrary",)),
    )(o, proj, gn, dycat)


def _ffn_act_fwd(up_pre, conv_w, *, name):
    T, F2 = up_pre.shape[1], up_pre.shape[0] * LANE
    RC = _tile(T, ROWS_FFN_FWD, 8)
    nb = F2 // 2 // LANE
    K = conv_w.shape[0]

    def body(g_ref, v_ref, wg_ref, wv_ref, y_ref):
        def step(r0, carry, edge):
            _, gate = _taps(g_ref, wg_ref, K, r0, RC, T, RC, edge)
            _, val = _taps(v_ref, wv_ref, K, r0, RC, T, RC, edge)
            y_ref[pl.ds(r0, RC), :] = (_silu(gate) * val).astype(y_ref.dtype)
            return carry
        _peeled(T // RC, RC, step, 0)

    return pl.pallas_call(
        body, name=name, grid=(nb,),
        in_specs=[pl.BlockSpec((None, T, LANE), lambda j: (j, 0, 0)), pl.BlockSpec((None, T, LANE), lambda j: (nb + j, 0, 0)),
                  pl.BlockSpec((K, LANE), lambda j: (0, j)), pl.BlockSpec((K, LANE), lambda j: (0, nb + j))],
        out_specs=pl.BlockSpec((T, LANE), lambda j: (0, j)),
        out_shape=jax.ShapeDtypeStruct((T, F2 // 2), BF16), compiler_params=_params(("parallel",)),
    )(up_pre, up_pre, conv_w, conv_w)


def _ffn_act_bwd(up_pre, conv_w, dact, *, name):
    T, F2 = up_pre.shape[1], up_pre.shape[0] * LANE
    RC = _tile(T, ROWS_FFN_BWD, 8)
    nb = F2 // 2 // LANE
    K = conv_w.shape[0]

    def body(g_ref, v_ref, wg_ref, wv_ref, da_ref, d_ref, dwg_ref, dwv_ref, sg_ref, sv_ref):
        def step(r0, accs, edge):
            gs, gate2 = _taps(g_ref, wg_ref, K, r0, RC + 8, T, RC, edge)
            vs, val2 = _taps(v_ref, wv_ref, K, r0, RC + 8, T, RC, edge)
            da2 = _win(da_ref, r0, 0, RC + 8, T, RC, edge)
            dgate2 = da2 * val2 * _dsilu(gate2)
            dval2 = da2 * _silu(gate2)
            d_ref[0, pl.ds(r0, RC), :] = _untaps(sg_ref, dgate2, wg_ref, K, RC).astype(d_ref.dtype)
            d_ref[1, pl.ds(r0, RC), :] = _untaps(sv_ref, dval2, wv_ref, K, RC).astype(d_ref.dtype)
            new = []
            for j in range(K):
                new.append(accs[2 * j] + _fold8(dgate2[:RC] * gs[j][:RC]))
                new.append(accs[2 * j + 1] + _fold8(dval2[:RC] * vs[j][:RC]))
            return tuple(new)

        accs = _peeled(T // RC, RC, step, tuple(jnp.zeros((8, LANE), F32) for _ in range(2 * K)))
        for j in range(K):
            dwg_ref[j:j + 1, :] = jnp.sum(accs[2 * j], axis=0, keepdims=True)
            dwv_ref[j:j + 1, :] = jnp.sum(accs[2 * j + 1], axis=0, keepdims=True)

    col = pl.BlockSpec((T, LANE), lambda j: (0, j))
    wsp = pl.BlockSpec((K, LANE), lambda j: (0, j))
    return pl.pallas_call(
        body, name=name, grid=(nb,),
        in_specs=[pl.BlockSpec((None, T, LANE), lambda j: (j, 0, 0)), pl.BlockSpec((None, T, LANE), lambda j: (nb + j, 0, 0)),
                  wsp, pl.BlockSpec((K, LANE), lambda j: (0, nb + j)), col],
        out_specs=[pl.BlockSpec((2, T, LANE), lambda j: (0, 0, j)), wsp, wsp],
        out_shape=[jax.ShapeDtypeStruct((2, T, F2 // 2), BF16)] + [jax.ShapeDtypeStruct((K, F2 // 2), F32)] * 2,
        scratch_shapes=[pltpu.VMEM((RC + 8, LANE), F32)] * 2,
        compiler_params=_params(("parallel",)),
    )(up_pre, up_pre, conv_w, conv_w, dact)


CPB = 8
CPB_SCAN = 4
GRP = 8
HP = lax.Precision.HIGH


def _tri(strict=False, upper=False):
    r = lax.broadcasted_iota(jnp.int32, (CHUNK, CHUNK), 0)
    c = lax.broadcasted_iota(jnp.int32, (CHUNK, CHUNK), 1)
    if upper:
        return c >= r
    return (r > c) if strict else (r >= c)


def _mm(a, b, dn="nn", precision=None):
    precision = HP if precision is None else precision
    return lax.dot_general(a, b, _DN[dn], precision=precision, preferred_element_type=F32)


def _mm16(a, b, dn="nn"):
    return lax.dot_general(a.astype(BF16), b.astype(BF16), _DN[dn], preferred_element_type=F32)


def _each(f, *cols):
    return [f(*xs) for xs in zip(*cols)]


def _decay(gam):
    return jnp.exp(jnp.where(_tri(), gam[:, :CHUNK] - gam.T[:CHUNK, :], -1e30))


def _delta_specs(T, H, cpb):
    rows = cpb * CHUNK
    col = lambda o: pl.BlockSpec((rows, LANE), functools.partial(lambda o, h, n: (n, o + h), o))
    bc = pl.BlockSpec((1, rows, LANE), lambda h, n: (h, n, 0))
    sq = pl.BlockSpec((1, cpb, CHUNK, CHUNK), lambda h, n: (h, n, 0, 0))
    vec = pl.BlockSpec((1, cpb, 1, LANE), lambda h, n: (h, n, 0, 0))
    return col, bc, sq, vec


def _delta_prep_fwd(qkv, gamB, bB, H, *, name):
    T = qkv.shape[0]
    N = T // CHUNK
    cpb = _tile(N, CPB, 8)
    grp = min(GRP, cpb)
    col, bc, sq, vec = _delta_specs(T, H, cpb)

    def body(q_ref, k_ref, v_ref, g_ref, b_ref, u_ref, w_ref, qd_ref, kd_ref, qk_ref, ti_ref, gl_ref):
        eye = (lax.broadcasted_iota(jnp.int32, (CHUNK, CHUNK), 0) == lax.broadcasted_iota(jnp.int32, (CHUNK, CHUNK), 1)).astype(F32)
        strict = _tri(strict=True)
        for c0 in range(0, cpb, grp):
            cs = list(range(c0, c0 + grp))
            rows = [slice(c * CHUNK, (c + 1) * CHUNK) for c in cs]
            q, k, v = ([r_[r, :] for r in rows] for r_ in (q_ref, k_ref, v_ref))
            bb = [b_ref[0, r, :] for r in rows]
            gam = [g_ref[0, r, :] for r in rows]
            D = _each(_decay, gam)
            e = _each(jnp.exp, gam)
            kk = _each(lambda k_: _mm16(k_, k_, "nt"), k)
            X = _each(lambda kk_, D_, b_: -(jnp.where(strict, kk_ * D_, 0.0) * b_[:, :CHUNK]), kk, D, bb)
            R = _each(lambda x: eye + x, X)
            for _ in range(5):
                X = _each(lambda x: _mm(x, x), X)
                R = _each(lambda r, x: r + _mm(r, x), R, X)
            u = _each(lambda r, b_, v_: _mm(r, b_ * v_), R, bb, v)
            w = _each(lambda r, b_, e_, k_: _mm(r, b_ * e_ * k_), R, bb, e, k)
            qk = _each(lambda q_, k_, D_: _mm16(q_, k_, "nt") * D_, q, k, D)
            for i, c in enumerate(cs):
                glast = gam[i][CHUNK - 1:CHUNK, :]
                u_ref[rows[i], :] = u[i]
                w_ref[rows[i], :] = w[i]
                qd_ref[rows[i], :] = e[i] * q[i]
                kd_ref[rows[i], :] = jnp.exp(glast - gam[i]) * k[i]
                qk_ref[0, c] = qk[i]
                ti_ref[0, c] = R[i]
                gl_ref[0, c] = jnp.exp(glast)

    full = jax.ShapeDtypeStruct((T, H * LANE), F32)
    sqs = jax.ShapeDtypeStruct((H, N, CHUNK, CHUNK), F32)
    return pl.pallas_call(
        body, name=name, grid=(H, N // cpb),
        in_specs=[col(0), col(H), col(2 * H), bc, bc],
        out_specs=[col(0)] * 4 + [sq, sq, vec],
        out_shape=[full] * 4 + [sqs, sqs, jax.ShapeDtypeStruct((H, N, 1, LANE), F32)],
        compiler_params=_params(("parallel", "parallel")),
    )(qkv, qkv, qkv, gamB, bB)


def _scan_specs(H, N, cpb, hb, rev):
    nbk = N // cpb
    blk = (lambda n: nbk - 1 - n) if rev else (lambda n: n)
    col = pl.BlockSpec((cpb * CHUNK, hb * LANE), lambda h, n: (blk(n), h))
    sq = pl.BlockSpec((hb, cpb, CHUNK, CHUNK), lambda h, n: (h, blk(n), 0, 0))
    vec = pl.BlockSpec((hb, cpb, 1, LANE), lambda h, n: (h, blk(n), 0, 0))
    st = pl.BlockSpec((hb, cpb, HEAD, HEAD), lambda h, n: (h, blk(n), 0, 0))
    return col, sq, vec, st


def _delta_scan_fwd(u, w, qd, kd, qk, gl, H, *, name):
    T = u.shape[0]
    N = T // CHUNK
    cpb = _tile(N, CPB_SCAN, 4)
    hb = min(GRP, H)
    col, sq, vec, st = _scan_specs(H, N, cpb, hb, False)
    lanes = [slice(j * LANE, (j + 1) * LANE) for j in range(hb)]
    heads = list(range(hb))

    def body(u_ref, w_ref, qd_ref, kd_ref, qk_ref, gl_ref, o_ref, vn_ref, ss_ref, s_scr):
        @pl.when(pl.program_id(1) == 0)
        def _():
            s_scr[...] = jnp.zeros_like(s_scr)

        def step(c, states):
            rows = pl.ds(pl.multiple_of(c * CHUNK, CHUNK), CHUNK)
            S = list(states)
            for j in heads:
                ss_ref[j, c] = S[j]
            wS = _each(lambda ln, s: _mm16(w_ref[rows, ln], s), lanes, S)
            qS = _each(lambda ln, s: _mm16(qd_ref[rows, ln], s), lanes, S)
            vn = _each(lambda ln, ws: u_ref[rows, ln] - ws, lanes, wS)
            o = _each(lambda j, qs, vn_: qs + _mm16(qk_ref[j, c], vn_), heads, qS, vn)
            new = _each(lambda j, ln, s, vn_: s * gl_ref[j, c] + _mm16(kd_ref[rows, ln], vn_, "tn"),
                        heads, lanes, S, vn)
            for j in heads:
                o_ref[rows, lanes[j]] = o[j]
                vn_ref[rows, lanes[j]] = vn[j]
            return tuple(new)
        out = lax.fori_loop(0, cpb, step, tuple(s_scr[j] for j in heads))
        for j in heads:
            s_scr[j] = out[j]

    full = jax.ShapeDtypeStruct((T, H * LANE), F32)
    return pl.pallas_call(
        body, name=name, grid=(H // hb, N // cpb),
        in_specs=[col] * 4 + [sq, vec],
        out_specs=[col, col, st],
        out_shape=[full, full, jax.ShapeDtypeStruct((H, N, HEAD, HEAD), F32)],
        scratch_shapes=[pltpu.VMEM((hb, HEAD, HEAD), F32)],
        compiler_params=_params(("parallel", "arbitrary")),
    )(u, w, qd, kd, qk, gl)


def _delta_scan_bwd(do, w, qd, kd, vn, qk, gl, ss, H, *, name):
    T = do.shape[0]
    N = T // CHUNK
    cpb = _tile(N, CPB_SCAN, 4)
    hb = min(GRP, H)
    col, sq, vec, st = _scan_specs(H, N, cpb, hb, True)
    lanes = [slice(j * LANE, (j + 1) * LANE) for j in range(hb)]
    heads = list(range(hb))

    def body(do_ref, w_ref, qd_ref, kd_ref, vn_ref, qk_ref, gl_ref, ss_ref,
             du_ref, dw_ref, dqd_ref, dkd_ref, dqk_ref, dgl_ref, ds_scr):
        @pl.when(pl.program_id(1) == 0)
        def _():
            ds_scr[...] = jnp.zeros_like(ds_scr)

        def step(i, dstates):
            c = cpb - 1 - i
            rows = pl.ds(pl.multiple_of(c * CHUNK, CHUNK), CHUNK)
            dS = list(dstates)
            S = [ss_ref[j, c] for j in heads]
            dov = [do_ref[rows, ln] for ln in lanes]
            vnv = [vn_ref[rows, ln] for ln in lanes]
            a1 = _each(lambda j, d_: _mm16(qk_ref[j, c], d_, "tn"), heads, dov)
            a2 = _each(lambda ln, ds: _mm16(kd_ref[rows, ln], ds), lanes, dS)
            dvn = _each(lambda x, y: x + y, a1, a2)
            dqd = _each(lambda d_, s: _mm16(d_, s, "nt"), dov, S)
            dkd = _each(lambda v_, ds: _mm16(v_, ds, "nt"), vnv, dS)
            dqk = _each(lambda d_, v_: _mm16(d_, v_, "nt"), dov, vnv)
            dw = _each(lambda dv_, s: -_mm16(dv_, s, "nt"), dvn, S)
            b1 = _each(lambda ln, d_: _mm16(qd_ref[rows, ln], d_, "tn"), lanes, dov)
            b2 = _each(lambda ln, dv_: _mm16(w_ref[rows, ln], dv_, "tn"), lanes, dvn)
            new = _each(lambda j, x, y, ds: x + ds * gl_ref[j, c] - y, heads, b1, b2, dS)
            for j in heads:
                du_ref[rows, lanes[j]] = dvn[j]
                dw_ref[rows, lanes[j]] = dw[j]
                dqd_ref[rows, lanes[j]] = dqd[j]
                dkd_ref[rows, lanes[j]] = dkd[j]
                dqk_ref[j, c] = dqk[j]
                dgl = jnp.sum(jnp.sum(dS[j] * S[j], axis=1, keepdims=True), axis=0, keepdims=True)
                dgl_ref[j, c] = jnp.broadcast_to(dgl, (1, LANE))
            return tuple(new)
        out = lax.fori_loop(0, cpb, step, tuple(ds_scr[j] for j in heads))
        for j in heads:
            ds_scr[j] = out[j]

    full = jax.ShapeDtypeStruct((T, H * LANE), F32)
    return pl.pallas_call(
        body, name=name, grid=(H // hb, N // cpb),
        in_specs=[col] * 5 + [sq, vec, st],
        out_specs=[col] * 4 + [sq, vec],
        out_shape=[full] * 4 + [jax.ShapeDtypeStruct((H, N, CHUNK, CHUNK), F32), jax.ShapeDtypeStruct((H, N, 1, LANE), F32)],
        scratch_shapes=[pltpu.VMEM((hb, HEAD, HEAD), F32)],
        compiler_params=_params(("parallel", "arbitrary")),
    )(do, w, qd, kd, vn, qk, gl, ss)


def _delta_prep_bwd(qkv, gamB, bB, ti, u, w, qk, du, dw, dqd, dkd, dqk, dgl, H, *, name):
    T = qkv.shape[0]
    N = T // CHUNK
    cpb = _tile(N, CPB, 8)
    grp = min(GRP, cpb)
    col, bc, sq, vec = _delta_specs(T, H, cpb)

    def body(q_ref, k_ref, v_ref, g_ref, b_ref, ti_ref, u_ref, w_ref, qk_ref,
             du_ref, dw_ref, dqd_ref, dkd_ref, dqk_ref, dgl_ref,
             dq_ref, dk_ref, dv_ref, dg_ref, db_ref):
        ones = jnp.ones((CHUNK, LANE), F32)
        strict = _tri(strict=True)
        last = lax.broadcasted_iota(jnp.int32, (CHUNK, LANE), 0) == CHUNK - 1
        lsum = lambda x: jnp.sum(x, axis=-1, keepdims=True)
        for c0 in range(0, cpb, grp):
            cs = list(range(c0, c0 + grp))
            rows = [slice(c * CHUNK, (c + 1) * CHUNK) for c in cs]
            ld = lambda r_: [r_[r, :] for r in rows]
            q, k, v, uv, wv, duv, dwv, dqd_v, dkd_v = (ld(r_) for r_ in (q_ref, k_ref, v_ref, u_ref, w_ref, du_ref, dw_ref, dqd_ref, dkd_ref))
            bb = [b_ref[0, r, :] for r in rows]
            gam = [g_ref[0, r, :] for r in rows]
            Ti = [ti_ref[0, c] for c in cs]
            QK = [qk_ref[0, c] for c in cs]
            dqk_v = [dqk_ref[0, c] for c in cs]
            D = _each(_decay, gam)
            e = _each(jnp.exp, gam)
            glast = [g_[CHUNK - 1:CHUNK, :] for g_ in gam]
            eL = _each(lambda gl_, g_: jnp.exp(gl_ - g_), glast, gam)
            kk = _each(lambda k_: _mm16(k_, k_, "nt"), k)
            KKD = _each(lambda kk_, D_: jnp.where(strict, kk_ * D_, 0.0), kk, D)
            dru = _each(lambda t, d_: _mm(t, d_, "tn"), Ti, duv)
            drw = _each(lambda t, d_: _mm(t, d_, "tn"), Ti, dwv)
            l1 = _each(lambda a, b: _mm(a, b, "nt"), dru, uv)
            l2 = _each(lambda a, b: _mm(a, b, "nt"), drw, wv)
            dL = _each(lambda a, b: jnp.where(strict, -(a + b), 0.0), l1, l2)
            Mm = _each(lambda dl, b_: dl * b_[:, :CHUNK], dL, bb)
            dKK = _each(lambda m_, D_: m_ * D_, Mm, D)
            dQK = _each(lambda a, D_: a * D_, dqk_v, D)
            P = _each(lambda m_, kkd, a, qk_: m_ * kkd + a * qk_, Mm, KKD, dqk_v, QK)
            q1 = _each(lambda a, k_: _mm16(a, k_), dQK, k)
            k1 = _each(lambda a, q_: _mm16(a, q_, "tn"), dQK, q)
            k2 = _each(lambda a, k_: _mm16(a, k_), dKK, k)
            k3 = _each(lambda a, k_: _mm16(a, k_, "tn"), dKK, k)
            s1 = _each(lambda dl, kkd: _mm(dl * kkd, ones), dL, KKD)
            p1 = _each(lambda p_: _mm(p_, ones), P)
            p2 = _each(lambda p_: _mm(p_, ones, "tn"), P)
            for i, c in enumerate(cs):
                r = rows[i]
                bek = bb[i] * e[i]
                kdv = eL[i] * k[i]
                dq_ref[r, :] = q1[i] + e[i] * dqd_v[i]
                dk_ref[r, :] = k1[i] + k2[i] + k3[i] + bek * drw[i] + eL[i] * dkd_v[i]
                dv_ref[r, :] = bb[i] * dru[i]
                db_ref[0, r, :] = s1[i] + lsum(dru[i] * v[i]) + lsum(drw[i] * e[i] * k[i])
                dgam = (p1[i] - p2[i] + lsum(drw[i] * bek * k[i]) + lsum(dqd_v[i] * e[i] * q[i])
                        - lsum(dkd_v[i] * kdv))
                xlast = jnp.sum(lsum(dkd_v[i] * kdv), axis=0, keepdims=True) + jnp.exp(glast[i]) * dgl_ref[0, c]
                dg_ref[0, r, :] = dgam + jnp.where(last, xlast, 0.0)

    full = jax.ShapeDtypeStruct((T, H * LANE), F32)
    bcs = jax.ShapeDtypeStruct((H, T, LANE), F32)
    return pl.pallas_call(
        body, name=name, grid=(H, N // cpb),
        in_specs=[col(0), col(H), col(2 * H), bc, bc, sq, col(0), col(0), sq, col(0), col(0), col(0), col(0), sq, vec],
        out_specs=[col(0), col(0), col(0), bc, bc],
        out_shape=[full, full, full, bcs, bcs],
        compiler_params=_params(("parallel", "parallel")),
    )(qkv, qkv, qkv, gamB, bB, ti, u, w, qk, du, dw, dqd, dkd, dqk, dgl)


def _adam(parts, w, m, v, *, name, own=None, me=None):
    P, R, C = parts.shape
    if R > 256 and R % 8:
        tr, tc = R, _tile(C, 256)
    else:
        tr, tc = _tile(R, 256, 8), C
    n_own = 0 if own is None else 2

    def body(*refs):
        p_ref, w_ref, m_ref, v_ref, g_ref, d_ref, nm_ref, nv_ref = refs[n_own:]
        g = None
        for i in range(P):
            t = p_ref[i].astype(F32)
            if n_own:
                t = jnp.where(refs[0][0] == i, refs[1][...].astype(F32), t)
            g = t if g is None else g + t
        mn = ADAM_B1 * m_ref[...] + (1.0 - ADAM_B1) * g
        vn = ADAM_B2 * v_ref[...] + (1.0 - ADAM_B2) * (g * g)
        m_hat = mn / (1.0 - ADAM_B1 ** ADAM_STEP)
        v_hat = vn / (1.0 - ADAM_B2 ** ADAM_STEP)
        g_ref[...] = g
        d_ref[...] = -ADAM_LR * (m_hat / (jnp.sqrt(v_hat) + ADAM_EPS) + ADAM_WD * w_ref[...])
        nm_ref[...] = mn
        nv_ref[...] = vn

    blk = pl.BlockSpec((tr, tc), lambda i, j: (i, j))
    return pl.pallas_call(
        body, name=name, grid=(R // tr, C // tc),
        in_specs=[pl.BlockSpec(memory_space=pltpu.SMEM), blk][:n_own] + [pl.BlockSpec((P, tr, tc), lambda i, j: (0, i, j)), blk, blk, blk],
        out_specs=[blk] * 4, out_shape=[jax.ShapeDtypeStruct((R, C), F32)] * 4,
        compiler_params=_params(("parallel", "parallel")),
    )(*([me, own] if n_own else []), parts, w, m, v)


def _mesh_pos():
    return lax.axis_index("x"), lax.axis_index("y"), lax.axis_index("c")


def _peer(k):
    x, y, c = _mesh_pos()
    px, py, pc = x ^ ((k >> 2) & 1), y ^ ((k >> 1) & 1), c ^ (k & 1)
    return (px, py, pc), 4 * px + 2 * py + pc


def _exchange(arrays, scatter, *, name, after=None):
    n = len(arrays)
    n_in = n if after is None else n + 1
    blocks = [a.shape[1:] if scatter else a.shape for a in arrays]

    def body(*refs):
        srcs, dsts = refs[:n], refs[n_in:n_in + n]
        send_sems, recv_sems, local_sems = refs[n_in + n:]
        x, y, c = _mesh_pos()
        me = 4 * x + 2 * y + c
        local, sends = [], []
        for a in range(n):
            cp = pltpu.make_async_copy(srcs[a].at[me] if scatter else srcs[a], dsts[a].at[me], local_sems.at[a])
            cp.start()
            local.append(cp)
            for k in range(1, N_DEV):
                dev, idx = _peer(k)
                cp = pltpu.make_async_remote_copy(
                    src_ref=srcs[a].at[idx] if scatter else srcs[a], dst_ref=dsts[a].at[me],
                    send_sem=send_sems.at[a * N_DEV + k], recv_sem=recv_sems.at[a * N_DEV + k],
                    device_id=dev, device_id_type=MESH)
                cp.start()
                sends.append(cp)
        for a in range(n):
            for k in range(1, N_DEV):
                dev, idx = _peer(k)
                pltpu.make_async_remote_copy(
                    src_ref=srcs[a].at[idx] if scatter else srcs[a], dst_ref=dsts[a].at[idx],
                    send_sem=send_sems.at[a * N_DEV + k], recv_sem=recv_sems.at[a * N_DEV + k],
                    device_id=dev, device_id_type=MESH).wait_recv()
        for cp in sends:
            cp.wait_send()
        for cp in local:
            cp.wait()

    anyspec = pl.BlockSpec(memory_space=pl.ANY)
    return pl.pallas_call(
        body, name=name, in_specs=[anyspec] * n_in, out_specs=[anyspec] * n,
        out_shape=[jax.ShapeDtypeStruct((N_DEV,) + tuple(b), a.dtype) for a, b in zip(arrays, blocks)],
        scratch_shapes=[pltpu.SemaphoreType.DMA((n * N_DEV,)), pltpu.SemaphoreType.DMA((n * N_DEV,)),
                        pltpu.SemaphoreType.DMA((n,))],
    )(*arrays, *([] if after is None else [after]))


_ANY = pl.BlockSpec(memory_space=pl.ANY)
_SEM = pl.BlockSpec(memory_space=pltpu.SEMAPHORE)
_EFFECT = pltpu.SideEffectType.DATAFLOW_SIDE_EFFECTING


def _in_hbm(a):
    return pltpu.with_memory_space_constraint(a, pltpu.HBM)


def _split_copy(src, land, send, recv, k, me, scatter, landed):
    dev, idx = _peer(k)
    return pltpu.make_async_remote_copy(
        src_ref=src.at[idx] if scatter else src, dst_ref=land.at[idx if landed else me],
        send_sem=send.at[k], recv_sem=recv.at[k], device_id=dev, device_id_type=MESH)


ALL_PEERS = tuple(range(1, N_DEV))
SIBLING = 1
SAME_CORE = (2, 4, 6)


def _split_start(srcs, lands, scatter, *, name, relations=None):
    n = len(srcs)
    relations = relations or [ALL_PEERS] * n

    def body(*refs):
        src, land, send, recv, token = refs[:n], refs[n:2 * n], refs[2 * n:3 * n], refs[3 * n:4 * n], refs[-1]
        x, y, c = _mesh_pos()
        me = 4 * x + 2 * y + c
        for a in range(n):
            for k in relations[a]:
                _split_copy(src[a], land[a], send[a], recv[a], k, me, scatter, False).start()
        token[...] = jnp.zeros_like(token)

    outs = pl.pallas_call(
        body, name=name,
        out_shape=[pltpu.SemaphoreType.DMA((N_DEV,))] * (2 * n) + [pltpu.HBM(t.shape, t.dtype) for t in list(srcs) + list(lands)]
        + [jax.ShapeDtypeStruct((8, LANE), F32)],
        in_specs=[_ANY] * (2 * n), out_specs=[_SEM] * (2 * n) + [_ANY] * (2 * n) + [pl.BlockSpec(memory_space=pltpu.VMEM)],
        input_output_aliases={i: 2 * n + i for i in range(2 * n)},
        compiler_params=pltpu.CompilerParams(has_side_effects=_EFFECT),
    )(*[_in_hbm(t) for t in list(srcs) + list(lands)])
    handles = [(outs[a], outs[n + a], outs[2 * n + a], outs[3 * n + a]) for a in range(n)]
    return handles, outs[-1]


def _split_wait(handle, after, scatter, *, name):
    send, recv, src_thru, land_thru = handle

    def body(src_ref, land_ref, send_ref, recv_ref, after_ref, src_out, land_out):
        x, y, c = _mesh_pos()
        me = 4 * x + 2 * y + c
        for k in range(1, N_DEV):
            cp = _split_copy(src_ref, land_ref, send_ref, recv_ref, k, me, scatter, True)
            cp.wait_send()
            cp.wait_recv()

    return pl.pallas_call(
        body, name=name,
        out_shape=(pltpu.HBM(src_thru.shape, src_thru.dtype), pltpu.HBM(land_thru.shape, land_thru.dtype)),
        in_specs=(_ANY, _ANY, _SEM, _SEM, _ANY), out_specs=(_ANY, _ANY), input_output_aliases={0: 0, 1: 1},
        compiler_params=pltpu.CompilerParams(has_side_effects=_EFFECT),
    )(src_thru, land_thru, send, recv, after)[1]


def _forward_copy(land, fsend, frecv, k, landed):
    x, y, c = _mesh_pos()
    _, idx = _peer(k | SIBLING if landed else k)
    return pltpu.make_async_remote_copy(src_ref=land.at[idx], dst_ref=land.at[idx], send_sem=fsend.at[k],
                                        recv_sem=frecv.at[k], device_id=(x, y, 1 - c), device_id_type=MESH)


def _gather_forward(handle, after, *, name):
    send, recv, src_thru, land_thru = handle

    def body(src_ref, land_ref, send_ref, recv_ref, after_ref, src_out, land_out, fsend, frecv):
        x, y, c = _mesh_pos()
        me = 4 * x + 2 * y + c
        for k in SAME_CORE:
            _split_copy(src_ref, land_ref, send_ref, recv_ref, k, me, False, True).wait_recv()
            _forward_copy(land_ref, fsend, frecv, k, False).start()

    src2, land2, fsend, frecv = pl.pallas_call(
        body, name=name,
        out_shape=(pltpu.HBM(src_thru.shape, src_thru.dtype), pltpu.HBM(land_thru.shape, land_thru.dtype),
                   pltpu.SemaphoreType.DMA((N_DEV,)), pltpu.SemaphoreType.DMA((N_DEV,))),
        in_specs=(_ANY, _ANY, _SEM, _SEM, _ANY), out_specs=(_ANY, _ANY, _SEM, _SEM), input_output_aliases={0: 0, 1: 1},
        compiler_params=pltpu.CompilerParams(has_side_effects=_EFFECT),
    )(src_thru, land_thru, send, recv, after)
    return (send, recv, src2, land2), (fsend, frecv)


def _gather_wait_two_level(handle, fwd, *, name):
    send, recv, src_thru, land_thru = handle
    fsend, frecv = fwd

    def body(src_ref, land_ref, send_ref, recv_ref, fsend_ref, frecv_ref, src_out, land_out):
        x, y, c = _mesh_pos()
        me = 4 * x + 2 * y + c
        for k in (SIBLING,) + SAME_CORE:
            _split_copy(src_ref, land_ref, send_ref, recv_ref, k, me, False, True).wait_send()
        _split_copy(src_ref, land_ref, send_ref, recv_ref, SIBLING, me, False, True).wait_recv()
        for k in SAME_CORE:
            _forward_copy(land_ref, fsend_ref, frecv_ref, k, False).wait_send()
            _forward_copy(land_ref, fsend_ref, frecv_ref, k, True).wait_recv()

    return pl.pallas_call(
        body, name=name,
        out_shape=(pltpu.HBM(src_thru.shape, src_thru.dtype), pltpu.HBM(land_thru.shape, land_thru.dtype)),
        in_specs=(_ANY, _ANY, _SEM, _SEM, _SEM, _SEM), out_specs=(_ANY, _ANY), input_output_aliases={0: 0, 1: 1},
        compiler_params=pltpu.CompilerParams(has_side_effects=_EFFECT),
    )(src_thru, land_thru, send, recv, fsend, frecv)[1]


def _local_step(x, p, tgt, S, wt, conv, emit):
    T, D = x.shape
    CW = DNW = D // 2
    H = DNW // HEAD
    nA, nD = CW // LANE, DNW // LANE
    qkv_off, z_off, ab_off = 3 * nA, 3 * nA + 3 * nD, 3 * nA + 4 * nD
    alog = jnp.pad(S["a_log"], ((0, 0), (0, LANE - H)))
    dtb = jnp.pad(S["dt_bias"], ((0, 0), (0, LANE - H)))

    h1 = _rms_fwd(x, S["g_mix"], name="rms1_fwd")
    pp = _matmul(p, wt("w_pp", h1), "nn", name="mm_pp", b_shards=True)
    w_in, cv = wt("w_in", pp), conv(pp)
    proj = _matmul(h1, w_in, "nt", name="mm_in", tn=1536)
    y_a = _group_a_fwd(proj, cv["conv_a"], CW, D, name="group_a_fwd")
    qkv = _qkv_fwd(proj, cv["conv_qkv"], qkv_off, H, name="qkv_fwd")
    gamB, bB = _gates_fwd(proj, alog, dtb, ab_off, H, name="gates_fwd")
    u, w, qd, kd, qk, ti, gl = _delta_prep_fwd(qkv, gamB, bB, H, name="delta_prep_fwd")
    o, vn, ss = _delta_scan_fwd(u, w, qd, kd, qk, gl, H, name="delta_scan_fwd")
    ycat = _gated_norm_fwd(o, proj, S["dn_g"], z_off, y_a, name="gated_norm_fwd")
    w_out = wt("w_out", ycat)
    rows = dict(tm=ROW_TILE, tn=D)
    x1, h2 = _matmul(ycat, w_out, "nn", name="mm_out", out_dtypes=(F32, BF16), epilogue=_epi_residual_rms,
                     extras=(x,), vec_extras=(S["g_ffn"],), **rows)
    w_up = wt("w_up", h2)
    up_pre = _matmul(h2, w_up, "nn", name="mm_up", b_shards=True, tn=SHARD_TILE, out_lanes=True)
    act = _ffn_act_fwd(up_pre, cv["conv_ffn"], name="ffn_act_fwd")
    w_down = wt("w_down", act)
    x2 = _matmul(act, w_down, "nn", name="mm_down", epilogue=lambda acc, r: (acc + r,), extras=(x1,), tk=LONG_K)
    h3 = _rms_fwd(x2, S["g_ple"], name="rms3_fwd")
    w_pg = wt("w_pg", h3)

    def ple_epi(acc, x2r, ppr):
        s = jax.nn.sigmoid(acc)
        return x2r + s * ppr, s

    x3, sg = _matmul(h3, w_pg, "nn", name="mm_pg", out_dtypes=(F32, F32), epilogue=ple_epi, extras=(x2, pp), tm=512)
    dx3, dg_final, loss, dpg, dpp = _final_loss(x3, S["g_final"], tgt, pp, sg, name="final_loss")

    G = {"g_final": dg_final}
    tok = emit({"w_pp": _matmul(p, dpp, "tn", name="mm_dwpp", out_dtypes=(BF16,), out_shards=True, tk=LONG_K),
                "w_pg": _matmul(h3, dpg, "tn", name="mm_dwpg", out_dtypes=(BF16,), tk=LONG_K)})
    bwd = dict(out_dtypes=(F32, BF16), epilogue=_epi_rms_bwd(2), n_vec=1, **rows)
    dx2, dx2b, G["g_ple"] = _matmul(dpg, w_pg, "nt", name="mm_dh3", after=tok, extras=(x2, dx3),
                                    vec_extras=(S["g_ple"],), **bwd)
    tok = emit({"w_down": _matmul(act, dx2b, "tn", name="mm_dwdown", out_dtypes=(BF16,), tk=LONG_K)})
    dact = _matmul(dx2b, w_down, "nt", name="mm_dact", after=tok, tn=SHARD_TILE)
    dup, dcf_g, dcf_v = _ffn_act_bwd(up_pre, cv["conv_ffn"], dact, name="ffn_act_bwd")
    G["conv_ffn"] = jnp.concatenate([dcf_g, dcf_v], axis=1)
    tok = emit({"w_up": _matmul(h2, dup, "tn", name="mm_dwup", out_dtypes=(BF16,), b_shards=True, out_shards=True,
                                tn=SHARD_TILE, tk=LONG_K)})
    dh2 = _matmul(dup, w_up, "nt", name="mm_dh2", after=tok, a_shards=True, b_shards=True, tk=2 * SHARD_TILE)
    dx1, dx1b, G["g_ffn"] = _rms_bwd(x1, S["g_ffn"], dh2, dx2, name="rms2_bwd")
    tok = emit({"w_out": _matmul(ycat, dx1b, "tn", name="mm_dwout", out_dtypes=(BF16,), tk=LONG_K)})
    dycat = _matmul(dx1b, w_out, "nt", name="mm_dycat", after=tok)
    do, dz, G["dn_g"] = _gated_norm_bwd(o, proj, S["dn_g"], dycat, z_off, nA, name="gated_norm_bwd")
    du, dw, dqd, dkd, dqk, dgl = _delta_scan_bwd(do, w, qd, kd, vn, qk, gl, ss, H, name="delta_scan_bwd")
    dq, dk, dv, dgB, dbB = _delta_prep_bwd(qkv, gamB, bB, ti, u, w, qk, du, dw, dqd, dkd, dqk, dgl, H,
                                           name="delta_prep_bwd")
    dab, dal, ddt = _gates_bwd(proj, alog, dtb, dgB, dbB, ab_off, H, name="gates_bwd")
    G["a_log"], G["dt_bias"] = dal[:, :H], ddt[:, :H]
    dqkv, G["conv_qkv"] = _qkv_bwd(proj, cv["conv_qkv"], dq, dk, dv, qkv_off, H, name="qkv_bwd")
    dax, dab_, dac, G["conv_a"] = _group_a_bwd(proj, cv["conv_a"], dycat, CW, name="group_a_bwd")
    in_p = w_in.shape[0]
    dproj = jnp.concatenate([dax, dab_, dac, dqkv, dz, dab, jnp.zeros((T, in_p - (ab_off + 1) * LANE), BF16)], axis=1)
    tok = emit({"w_in": _matmul(dproj, h1, "tn", name="mm_dwin", out_dtypes=(BF16,), tk=LONG_K)})
    dh1 = _matmul(dproj, w_in, "nn", name="mm_dh1", after=tok, tk=LONG_K)
    grad_x, _, G["g_mix"] = _rms_bwd(x, S["g_mix"], dh1, dx1, name="rms1_bwd")
    return loss, grad_x, G


def _col_sharded(landed):
    _, R, C = landed.shape
    return jnp.transpose(landed, (1, 0, 2)).reshape(R, N_DEV * C)


def kernel(x, p, norm_mix_g, w_in, conv_a_w, conv_qkv_w, a_log, dt_bias, dn_norm_g, w_out, norm_ffn_g, w_up, conv_ffn_w, w_down, norm_ple_g, w_ple_gate, w_ple_proj, final_norm_g, loss_target, m_norm_mix_g, m_w_in, m_conv_a_w, m_conv_qkv_w, m_a_log, m_dt_bias, m_dn_norm_g, m_w_out, m_norm_ffn_g, m_w_up, m_conv_ffn_w, m_w_down, m_norm_ple_g, m_w_ple_gate, m_w_ple_proj, m_final_norm_g, v_norm_mix_g, v_w_in, v_conv_a_w, v_conv_qkv_w, v_a_log, v_dt_bias, v_dn_norm_g, v_w_out, v_norm_ffn_g, v_w_up, v_conv_ffn_w, v_w_down, v_norm_ple_g, v_w_ple_gate, v_w_ple_proj, v_final_norm_g):
    T, D = x.shape[1], x.shape[2]
    xd, _, cd = _mesh_pos()
    me = 4 * xd + 2 * lax.axis_index("y") + cd

    conv_sh = [conv_a_w[0], conv_qkv_w[0], conv_ffn_w[0]]
    conv_n = [c.size for c in conv_sh]
    pack_rows = -(-sum(conv_n) // LANE)
    conv_pack = jnp.pad(jnp.concatenate([c.reshape(-1) for c in conv_sh]), (0, pack_rows * LANE - sum(conv_n))).reshape(pack_rows, LANE)
    names = ["w_pp", "w_in", "conv", "w_out", "w_up", "w_down", "w_pg"]
    tr_ = lambda t: jnp.swapaxes(t, 1, 2)
    shards = [w_ple_proj[0].astype(BF16), w_in[0].T.astype(BF16), conv_pack, w_out[0].astype(BF16), w_up[0].astype(BF16),
              w_down[0].astype(BF16), w_ple_gate[0].astype(BF16)]
    empty_slots = lambda blocks: [lax.empty((N_DEV,) + tuple(b.shape), b.dtype) for b in blocks]
    handles, tok0 = _split_start(shards, empty_slots(shards), False, name="gather_start",
                                 relations=[(SIBLING,) + SAME_CORE if nm == "w_in" else ALL_PEERS for nm in names])
    handle = dict(zip(names, handles))
    own = dict(zip(names, shards))
    in_cols = N_DEV * w_in.shape[2]
    in_p = (in_cols // LANE) * LANE + AB_PAD
    in_place = {"w_up", "w_pp"}

    def gathered(name, after):
        if name == "w_in":
            passed, fwd = _gather_forward(handle[name], after, name="gather_forward_w_in")
            landed = _gather_wait_two_level(passed, fwd, name="gather_wait_w_in")
        else:
            landed = _split_wait(handle[name], after, False, name="gather_wait_" + name)
        return lax.dynamic_update_index_in_dim(landed, own[name], me, 0)

    def wt(name, after):
        landed = gathered(name, after)
        if name in in_place:
            return landed
        full = landed.reshape(-1, D)
        return jnp.pad(full, ((0, in_p - in_cols), (0, 0))) if name == "w_in" else full

    def conv(after):
        flat = gathered("conv", after).reshape(N_DEV, pack_rows * LANE)
        out, o_ = {}, 0
        for nm, c, n_ in zip(("conv_a", "conv_qkv", "conv_ffn"), conv_sh, conv_n):
            out[nm] = _col_sharded(flat[:, o_:o_ + n_].reshape((N_DEV,) + c.shape))
            o_ += n_
        return out

    pending, mine = {}, {}

    def emit(grads):
        parts = [g if nm in in_place else (g[:in_cols] if nm == "w_in" else g).reshape(N_DEV, -1, D)
                 for nm, g in grads.items()]
        hs, tok = _split_start(parts, empty_slots([q[0] for q in parts]), True, name="scatter_start_" + "_".join(grads))
        pending.update(zip(grads, hs))
        mine.update({nm: lax.dynamic_index_in_dim(q, me, 0, keepdims=False) for nm, q in zip(grads, parts)})
        return tok

    S = {
        "g_mix": norm_mix_g + tok0[0, 0], "a_log": a_log, "dt_bias": dt_bias, "dn_g": dn_norm_g, "g_ffn": norm_ffn_g,
        "g_ple": norm_ple_g, "g_final": final_norm_g.reshape(1, D),
    }

    loss_v, grad_x, G = _local_step(x[0], p[0, 0], loss_target[0], S, wt, conv, emit)
    loss = lax.psum(loss_v[0, 0], ("x", "y", "c"))

    small_names = ["g_mix", "g_ffn", "g_ple", "g_final", "dn_g", "a_log", "dt_bias", "conv_a", "conv_qkv", "conv_ffn"]
    small_rows, pieces = [], []
    for nm in small_names:
        g_ = G[nm].reshape(-1)
        r_ = -(-g_.size // (8 * LANE)) * 8
        small_rows.append(r_)
        pieces.append(jnp.pad(g_, (0, r_ * LANE - g_.size)).reshape(r_, LANE))
    landed = {nm: _split_wait(h_, grad_x, True, name="scatter_wait_" + nm) for nm, h_ in pending.items() if nm != "w_in"}

    def adam(parts, w_, m_, v_, nm, own_=None):
        shp = w_.shape
        w2, m2, v2 = (t.reshape(parts.shape[1:]) for t in (w_, m_, v_))
        kw = {} if own_ is None else {"own": own_, "me": me.astype(jnp.int32).reshape(1)}
        return tuple(t.reshape(shp) for t in _adam(parts, w2, m2, v2, name="adam_" + nm, **kw))

    big = {
        "w_up": adam(landed["w_up"], w_up, m_w_up, v_w_up, "w_up", mine["w_up"]),
        "w_down": adam(landed["w_down"], w_down, m_w_down, v_w_down, "w_down", mine["w_down"]),
        "w_out": adam(landed["w_out"], w_out, m_w_out, v_w_out, "w_out", mine["w_out"]),
        "w_pg": adam(landed["w_pg"], w_ple_gate, m_w_ple_gate, v_w_ple_gate, "w_ple_gate", mine["w_pg"]),
        "w_pp": adam(landed["w_pp"], w_ple_proj, m_w_ple_proj, v_w_ple_proj, "w_ple_proj", mine["w_pp"]),
    }
    first = lambda t: lax.slice(t, (0,) * t.ndim, (1,) * t.ndim).reshape(1)
    big_done = sum(first(r[1]) for r in big.values())
    (small_l,) = _exchange([jnp.concatenate(pieces, axis=0)], False, name="gather_small_grads", after=big_done)

    def small_parts(nm):
        i = small_names.index(nm)
        r0 = sum(small_rows[:i])
        shp = G[nm].shape
        return small_l[:, r0:r0 + small_rows[i], :].reshape(N_DEV, -1)[:, :G[nm].size].reshape((N_DEV,) + shp)

    def conv_parts(nm, shard):
        full = small_parts(nm)
        C = shard.shape[-1]
        return lax.dynamic_slice_in_dim(full, me * C, C, axis=2)

    res = [
        adam(small_parts("g_mix"), norm_mix_g, m_norm_mix_g, v_norm_mix_g, "norm_mix_g"),
        None,
        adam(conv_parts("conv_a", conv_a_w), conv_a_w, m_conv_a_w, v_conv_a_w, "conv_a_w"),
        adam(conv_parts("conv_qkv", conv_qkv_w), conv_qkv_w, m_conv_qkv_w, v_conv_qkv_w, "conv_qkv_w"),
        adam(small_parts("a_log"), a_log, m_a_log, v_a_log, "a_log"),
        adam(small_parts("dt_bias"), dt_bias, m_dt_bias, v_dt_bias, "dt_bias"),
        adam(small_parts("dn_g"), dn_norm_g, m_dn_norm_g, v_dn_norm_g, "dn_norm_g"),
        big["w_out"],
        adam(small_parts("g_ffn"), norm_ffn_g, m_norm_ffn_g, v_norm_ffn_g, "norm_ffn_g"),
        big["w_up"],
        adam(conv_parts("conv_ffn", conv_ffn_w), conv_ffn_w, m_conv_ffn_w, v_conv_ffn_w, "conv_ffn_w"),
        big["w_down"],
        adam(small_parts("g_ple"), norm_ple_g, m_norm_ple_g, v_norm_ple_g, "norm_ple_g"),
        big["w_pg"],
        big["w_pp"],
        adam(small_parts("g_final"), final_norm_g.reshape(1, D), m_final_norm_g.reshape(1, D),
             v_final_norm_g.reshape(1, D), "final_norm_g"),
    ]
    res[-1] = tuple(t.reshape(D) for t in res[-1])
    landed_in = _split_wait(pending["w_in"], res[10][1], True, name="scatter_wait_w_in")
    res[1] = tuple(tr_(t) for t in adam(landed_in, tr_(w_in), tr_(m_w_in), tr_(v_w_in), "w_in", mine["w_in"]))
    grads, deltas, new_m, new_v = zip(*res)
    return (loss, grad_x[None], *grads, *deltas, *new_m, *new_v)
```

```python
import functools

import jax
import jax.numpy as jnp
from jax import lax
from jax.experimental import pallas as pl
from jax.experimental.pallas import tpu as pltpu

F32 = jnp.float32
BF16 = jnp.bfloat16

EPS = 1e-6
CHUNK = 64
HEAD = 128
LANE = 128
N_DEV = 8
AB_PAD = 512

ADAM_LR = 0.001
ADAM_B1 = 0.9
ADAM_B2 = 0.999
ADAM_EPS = 1e-08
ADAM_WD = 0.01
ADAM_STEP = 10

MESH = pl.DeviceIdType.MESH


def _tile(dim, target, align=LANE):
    if dim <= target:
        return dim
    t = (target // align) * align
    while t > align and dim % t:
        t -= align
    assert dim % t == 0, (dim, target)
    return t


def _params(sem, vmem_mb=48):
    return pltpu.CompilerParams(dimension_semantics=sem, vmem_limit_bytes=vmem_mb << 20)


_DN = {"nn": (((1,), (0,)), ((), ())), "nt": (((1,), (1,)), ((), ())), "tn": (((0,), (0,)), ((), ()))}
LONG_K = 4096
SHARD_TILE = 1408


def _matmul(a, b, mode, *, name, out_dtypes=(F32,), epilogue=None, extras=(), vec_extras=(), n_vec=0, after=None,
            a_shards=False, b_shards=False, out_shards=False, out_lanes=False, tm=1024, tn=1024, tk=2048):
    shard_w = b.shape[2] if b_shards else None
    if b_shards:
        b_rows, b_cols = b.shape[1], b.shape[0] * shard_w
    else:
        b_rows, b_cols = b.shape
    a_w = a.shape[2] if a_shards else None
    a_dims = (a.shape[1], a.shape[0] * a_w) if a_shards else a.shape
    if mode == "nn":
        (M, K), (K2, N) = a_dims, (b_rows, b_cols)
    elif mode == "nt":
        (M, K), (N, K2) = a_dims, (b_rows, b_cols)
    else:
        (K, M), (K2, N) = a_dims, (b_rows, b_cols)
    assert K == K2, (name, a.shape, b.shape)
    tm = _tile(M, tm)
    n_dims = [N] + ([shard_w] if (b_shards and mode != "nt") else []) + ([N // N_DEV] if out_shards else [])
    tn = _tile(min(n_dims), tn)
    assert all(d % tn == 0 for d in n_dims), (name, n_dims, tn)
    grp = 1
    if b_shards and mode == "nt":
        grp = max(g for g in (1, 2, 4, 8) if g <= max(1, tk // shard_w) and (a_w is None or a_w % (g * shard_w) == 0))
    k_dims = [K] + ([shard_w] if (b_shards and mode == "nt") else []) + ([a_w] if a_shards else [])
    tk = grp * shard_w if grp > 1 else _tile(min(k_dims), tk)
    assert K % tk == 0, (name, K, tk)
    nk = K // tk
    n_ex, n_out = len(extras) + len(vec_extras), len(out_dtypes)
    assert n_vec == 0 or tn == N, (name, tn, N)
    dn = _DN[mode]

    n_tok = 0 if after is None else 1

    def body(a_ref, b_ref, *rest):
        rest = rest[n_tok:]
        ex_refs, out_refs, vec_refs = rest[:n_ex], rest[n_ex:n_ex + n_out], rest[n_ex + n_out:n_ex + n_out + n_vec]
        if grp > 1:
            part = sum(lax.dot_general(a_ref[:, s * shard_w:(s + 1) * shard_w].astype(BF16), b_ref[s].astype(BF16), dn,
                                       preferred_element_type=F32) for s in range(grp))
        else:
            part = lax.dot_general(a_ref[...].astype(BF16), b_ref[...].astype(BF16), dn, preferred_element_type=F32)
        first_rows = pl.program_id(0) == 0

        def finish(res):
            outs = (res,) if epilogue is None else epilogue(res, *[e[...] for e in ex_refs])
            for o_ref, val in zip(out_refs, outs[:n_out]):
                if out_lanes:
                    for c in range(tn // LANE):
                        o_ref[c] = val[:, c * LANE:(c + 1) * LANE].astype(o_ref.dtype)
                else:
                    o_ref[...] = val.astype(o_ref.dtype)
            for v_ref, val in zip(vec_refs, outs[n_out:]):
                @pl.when(first_rows)
                def _(v_ref=v_ref, val=val):
                    v_ref[...] = val

                @pl.when(jnp.logical_not(first_rows))
                def _(v_ref=v_ref, val=val):
                    v_ref[...] += val

        if nk == 1:
            finish(part)
            return
        acc, k = rest[-1], pl.program_id(2)

        @pl.when(k == 0)
        def _():
            acc[...] = part

        @pl.when(k > 0)
        def _():
            acc[...] += part

        @pl.when(k == nk - 1)
        def _():
            finish(acc[...])

    if a_shards:
        assert mode == "nt" and a_w % tk == 0, (name, mode, a_w, tk)
        per_a = a_w // tk
        a_spec = pl.BlockSpec((None, tm, tk), lambda i, j, k: (lax.div(k, per_a), i, lax.rem(k, per_a)))
    else:
        a_spec = pl.BlockSpec((tk, tm), lambda i, j, k: (k, i)) if mode == "tn" else pl.BlockSpec((tm, tk), lambda i, j, k: (i, k))
    if b_shards and mode != "nt":
        per = shard_w // tn
        b_spec = pl.BlockSpec((None, tk, tn), lambda i, j, k: (lax.div(j, per), k, lax.rem(j, per)))
    elif b_shards and grp > 1:
        b_spec = pl.BlockSpec((grp, tn, shard_w), lambda i, j, k: (k, j, 0))
    elif b_shards:
        per = shard_w // tk
        b_spec = pl.BlockSpec((None, tn, tk), lambda i, j, k: (lax.div(k, per), j, lax.rem(k, per)))
    else:
        b_spec = pl.BlockSpec((tn, tk), lambda i, j, k: (j, k)) if mode == "nt" else pl.BlockSpec((tk, tn), lambda i, j, k: (k, j))
    mn_spec = pl.BlockSpec((tm, tn), lambda i, j, k: (i, j))
    vec_spec = pl.BlockSpec((1, tn), lambda i, j, k: (0, j))
    if out_shards:
        assert not extras
        per_o = (N // N_DEV) // tn
        out_spec = pl.BlockSpec((None, tm, tn), lambda i, j, k: (lax.div(j, per_o), i, lax.rem(j, per_o)))
        out_dims = (N_DEV, M, N // N_DEV)
    elif out_lanes:
        assert not extras
        out_spec = pl.BlockSpec((tn // LANE, tm, LANE), lambda i, j, k: (j, i, 0))
        out_dims = (N // LANE, M, LANE)
    else:
        out_spec, out_dims = mn_spec, (M, N)
    outs = pl.pallas_call(
        body, name=name, grid=(M // tm, N // tn, nk),
        in_specs=[a_spec, b_spec] + [pl.BlockSpec((8, LANE), lambda i, j, k: (0, 0))] * n_tok
        + [mn_spec] * len(extras) + [vec_spec] * len(vec_extras),
        out_specs=[out_spec] * n_out + [vec_spec] * n_vec,
        out_shape=[jax.ShapeDtypeStruct(out_dims, dt) for dt in out_dtypes] + [jax.ShapeDtypeStruct((1, N), F32)] * n_vec,
        scratch_shapes=[pltpu.VMEM((tm, tn), F32)] if nk > 1 else [],
        compiler_params=_params(("arbitrary" if n_vec else "parallel", "parallel", "arbitrary"), 56),
    )(a, b, *([] if after is None else [after]), *extras, *vec_extras)
    return outs[0] if n_out + n_vec == 1 else outs


def _rms_fwd(x, g, *, name):
    T, D = x.shape
    tr = _tile(T, 512, 8)

    def body(x_ref, g_ref, h_ref):
        xv = x_ref[...]
        r = lax.rsqrt(jnp.mean(xv * xv, axis=-1, keepdims=True) + EPS)
        h_ref[...] = (xv * r * g_ref[...]).astype(h_ref.dtype)

    return pl.pallas_call(
        body, name=name, grid=(T // tr,),
        in_specs=[pl.BlockSpec((tr, D), lambda i: (i, 0)), pl.BlockSpec((1, D), lambda i: (0, 0))],
        out_specs=pl.BlockSpec((tr, D), lambda i: (i, 0)),
        out_shape=jax.ShapeDtypeStruct((T, D), BF16),
        compiler_params=_params(("parallel",)),
    )(x, g)


def _rms_bwd(x, g, dh, dres, *, name):
    T, D = x.shape
    tr = _tile(T, 512, 8)
    epi = _epi_rms_bwd(2)

    def body(x_ref, g_ref, dh_ref, dres_ref, dx_ref, dxb_ref, dg_ref):
        dx, _, dgp = epi(dh_ref[...], x_ref[...], dres_ref[...], g_ref[...])

        @pl.when(pl.program_id(0) == 0)
        def _():
            dg_ref[...] = jnp.zeros_like(dg_ref)

        dg_ref[...] += dgp
        dx_ref[...] = dx
        dxb_ref[...] = dx.astype(dxb_ref.dtype)

    row = pl.BlockSpec((tr, D), lambda i: (i, 0))
    vec = pl.BlockSpec((1, D), lambda i: (0, 0))
    return pl.pallas_call(
        body, name=name, grid=(T // tr,),
        in_specs=[row, vec, row, row], out_specs=[row, row, vec],
        out_shape=[jax.ShapeDtypeStruct((T, D), F32), jax.ShapeDtypeStruct((T, D), BF16), jax.ShapeDtypeStruct((1, D), F32)],
        compiler_params=_params(("arbitrary",)),
    )(x, g, dh, dres)


ROW_TILE = 512


def _epi_residual_rms(acc, res, g):
    xn = acc + res
    r = lax.rsqrt(jnp.mean(xn * xn, axis=-1, keepdims=True) + EPS)
    return xn, xn * r * g


def _epi_rms_bwd(n_copies):
    def epi(dh, x, dres, g):
        r = lax.rsqrt(jnp.mean(x * x, axis=-1, keepdims=True) + EPS)
        xh = x * r
        dxh = dh * g
        dx = dres + r * (dxh - xh * jnp.mean(dxh * xh, axis=-1, keepdims=True))
        return (dx,) * n_copies + (jnp.sum(dh * xh, axis=0, keepdims=True),)
    return epi


def _final_loss(x, g, tgt, pp, sg, *, name):
    T, D = x.shape
    tr = _tile(T, 256, 8)

    def body(x_ref, g_ref, t_ref, pp_ref, sg_ref, dx_ref, dg_ref, loss_ref, dpg_ref, dpp_ref):
        xv = x_ref[...]
        r = lax.rsqrt(jnp.mean(xv * xv, axis=-1, keepdims=True) + EPS)
        xh = xv * r
        gv = g_ref[...]
        err = xh * gv - t_ref[...]

        @pl.when(pl.program_id(0) == 0)
        def _():
            dg_ref[...] = jnp.zeros_like(dg_ref)
            loss_ref[...] = jnp.zeros_like(loss_ref)

        part = 0.5 * jnp.sum(jnp.mean(err * err, axis=-1, keepdims=True), axis=0, keepdims=True)
        loss_ref[...] += jnp.broadcast_to(part, loss_ref.shape)
        dy = err * (1.0 / D)
        dg_ref[...] += jnp.sum(dy * xh, axis=0, keepdims=True)
        dxh = dy * gv
        dx = r * (dxh - xh * jnp.mean(dxh * xh, axis=-1, keepdims=True))
        dx_ref[...] = dx
        s = sg_ref[...]
        dpg_ref[...] = (dx * pp_ref[...] * s * (1.0 - s)).astype(dpg_ref.dtype)
        dpp_ref[...] = (dx * s).astype(dpp_ref.dtype)

    row = pl.BlockSpec((tr, D), lambda i: (i, 0))
    vec = pl.BlockSpec((1, D), lambda i: (0, 0))
    return pl.pallas_call(
        body, name=name, grid=(T // tr,),
        in_specs=[row, vec, row, row, row], out_specs=[row, vec, pl.BlockSpec((1, LANE), lambda i: (0, 0)), row, row],
        out_shape=[jax.ShapeDtypeStruct((T, D), F32), jax.ShapeDtypeStruct((1, D), F32),
                   jax.ShapeDtypeStruct((1, LANE), F32)] + [jax.ShapeDtypeStruct((T, D), BF16)] * 2,
        compiler_params=_params(("arbitrary",)),
    )(x, g, tgt, pp, sg)


ROWS_QKV_FWD, ROWS_QKV_BWD, ROWS_FFN_FWD, ROWS_FFN_BWD, ROWS_GROUP_A = 512, 256, 256, 128, 256


def _ext(ref, r0, T, before, after, RC):
    parts = []
    if before:
        p0 = pl.multiple_of(jnp.maximum(r0 - 8, 0), 8)
        parts.append(jnp.where(r0 > 0, ref[pl.ds(p0, 8), :], 0.0))
    parts.append(ref[pl.ds(r0, RC), :])
    if after:
        n0 = pl.multiple_of(jnp.minimum(r0 + RC, T - 8), 8)
        parts.append(jnp.where(r0 + RC < T, ref[pl.ds(n0, 8), :], 0.0))
    return parts[0] if len(parts) == 1 else jnp.concatenate(parts, axis=0)


def _fold8(x):
    return jnp.sum(x.reshape(x.shape[0] // 8, 8, x.shape[1]), axis=0)


def _win(ref, r0, lo, n, T, RC, edge):
    if not edge:
        return ref[pl.ds(r0 + lo, n), :]
    xx = _ext(ref, r0, T, True, True, RC)
    a = 8 + lo
    return (xx if a == 0 else pltpu.roll(xx, xx.shape[0] - a, 0))[:n, :]


def _taps(ref, w_ref, K, r0, n, T, RC, edge):
    wins = [_win(ref, r0, -(K - 1 - j), n, T, RC, edge) for j in range(K)]
    y = wins[0] * w_ref[0:1, :]
    for j in range(1, K):
        y = y + wins[j] * w_ref[j:j + 1, :]
    return wins, y


def _untaps(scr_ref, val, w_ref, K, RC):
    scr_ref[0:val.shape[0], :] = val
    y = scr_ref[K - 1:K - 1 + RC, :] * w_ref[0:1, :]
    for j in range(1, K):
        s = K - 1 - j
        y = y + scr_ref[s:s + RC, :] * w_ref[j:j + 1, :]
    return y


def _peeled(n_chunks, RC, step, init):
    carry = step(0, init, True)
    if n_chunks > 2:
        carry = lax.fori_loop(1, n_chunks - 1, lambda i, c: step(pl.multiple_of(i * RC, RC), c, False), carry)
    if n_chunks > 1:
        carry = step((n_chunks - 1) * RC, carry, True)
    return carry


def _silu(x):
    return x * jax.nn.sigmoid(x)


def _dsilu(x):
    s = jax.nn.sigmoid(x)
    return s * (1.0 + x * (1.0 - s))


def _col_specs(T, offs):
    return [pl.BlockSpec((T, LANE), functools.partial(lambda o, j: (0, o + j), o)) for o in offs]


def _group_a_fwd(proj, conv_w, CW, out_cols, *, name):
    T = proj.shape[0]
    RC = _tile(T, ROWS_GROUP_A, 8)
    nb = CW // LANE
    K = conv_w.shape[0]

    def body(ax_ref, ab_ref, ac_ref, w_ref, y_ref):
        def step(r0, carry, edge):
            c = None
            for j in range(K):
                lo = -(K - 1 - j)
                t = _win(ac_ref, r0, lo, RC, T, RC, edge) * _win(ax_ref, r0, lo, RC, T, RC, edge) * w_ref[j:j + 1, :]
                c = t if c is None else c + t
            y_ref[pl.ds(r0, RC), :] = (ab_ref[pl.ds(r0, RC), :] * c).astype(y_ref.dtype)
            return carry
        _peeled(T // RC, RC, step, 0)

    return pl.pallas_call(
        body, name=name, grid=(nb,),
        in_specs=_col_specs(T, (0, nb, 2 * nb)) + [pl.BlockSpec((K, LANE), lambda j: (0, j))],
        out_specs=pl.BlockSpec((T, LANE), lambda j: (0, j)),
        out_shape=jax.ShapeDtypeStruct((T, out_cols), BF16), compiler_params=_params(("parallel",)),
    )(proj, proj, proj, conv_w)


def _group_a_bwd(proj, conv_w, dycat, CW, *, name):
    T = proj.shape[0]
    RC = _tile(T, ROWS_GROUP_A, 8)
    nb = CW // LANE
    K = conv_w.shape[0]

    def body(ax_ref, ab_ref, ac_ref, w_ref, dy_ref, dax_ref, dab_ref, dac_ref, dw_ref, scr_ref):
        def step(r0, accs, edge):
            ms = [_win(ac_ref, r0, -(K - 1 - j), RC, T, RC, edge) * _win(ax_ref, r0, -(K - 1 - j), RC, T, RC, edge)
                  for j in range(K)]
            c = ms[0] * w_ref[0:1, :]
            for j in range(1, K):
                c = c + ms[j] * w_ref[j:j + 1, :]
            dy = dy_ref[pl.ds(r0, RC), :]
            dab_ref[pl.ds(r0, RC), :] = (dy * c).astype(dab_ref.dtype)
            dc2 = _win(dy_ref, r0, 0, RC + 8, T, RC, edge) * _win(ab_ref, r0, 0, RC + 8, T, RC, edge)
            dm = _untaps(scr_ref, dc2, w_ref, K, RC)
            dax_ref[pl.ds(r0, RC), :] = (dm * ac_ref[pl.ds(r0, RC), :]).astype(dax_ref.dtype)
            dac_ref[pl.ds(r0, RC), :] = (dm * ax_ref[pl.ds(r0, RC), :]).astype(dac_ref.dtype)
            return tuple(accs[j] + _fold8(dc2[:RC] * ms[j]) for j in range(K))

        accs = _peeled(T // RC, RC, step, tuple(jnp.zeros((8, LANE), F32) for _ in range(K)))
        for j in range(K):
            dw_ref[j:j + 1, :] = jnp.sum(accs[j], axis=0, keepdims=True)

    col = pl.BlockSpec((T, LANE), lambda j: (0, j))
    wsp = pl.BlockSpec((K, LANE), lambda j: (0, j))
    return pl.pallas_call(
        body, name=name, grid=(nb,),
        in_specs=_col_specs(T, (0, nb, 2 * nb)) + [wsp, col],
        out_specs=[col, col, col, wsp],
        out_shape=[jax.ShapeDtypeStruct((T, CW), BF16)] * 3 + [jax.ShapeDtypeStruct((K, CW), F32)],
        scratch_shapes=[pltpu.VMEM((RC + 8, LANE), F32)],
        compiler_params=_params(("parallel",)),
    )(proj, proj, proj, conv_w, dycat)


def _qkv_fwd(proj, conv_w, off, H, *, name):
    T = proj.shape[0]
    RC = _tile(T, ROWS_QKV_FWD, 8)
    nb = 3 * H
    K = conv_w.shape[0]

    def body(x_ref, w_ref, y_ref):
        j = pl.program_id(0)
        is_qk = j < 2 * H
        scale = jnp.where(j < H, HEAD ** -0.5, 1.0).astype(F32)

        def step(r0, carry, edge):
            s = _silu(_taps(x_ref, w_ref, K, r0, RC, T, RC, edge)[1])
            r = lax.rsqrt(jnp.sum(s * s, axis=-1, keepdims=True) + EPS) * scale
            y_ref[pl.ds(r0, RC), :] = s * jnp.where(is_qk, r, 1.0)
            return carry
        _peeled(T // RC, RC, step, 0)

    return pl.pallas_call(
        body, name=name, grid=(nb,),
        in_specs=_col_specs(T, (off,)) + [pl.BlockSpec((K, LANE), lambda j: (0, j))],
        out_specs=pl.BlockSpec((T, LANE), lambda j: (0, j)),
        out_shape=jax.ShapeDtypeStruct((T, nb * LANE), F32), compiler_params=_params(("parallel",)),
    )(proj, conv_w)


def _qkv_bwd(proj, conv_w, dq, dk, dv, off, H, *, name):
    T = proj.shape[0]
    RC = _tile(T, ROWS_QKV_BWD, 8)
    nb = 3 * H
    K = conv_w.shape[0]

    def body(x_ref, w_ref, dq_ref, dk_ref, dv_ref, dx_ref, dw_ref, scr_ref):
        j = pl.program_id(0)
        is_qk = j < 2 * H
        scale = jnp.where(j < H, HEAD ** -0.5, 1.0).astype(F32)

        def step(r0, accs, edge):
            xs, c2 = _taps(x_ref, w_ref, K, r0, RC + 8, T, RC, edge)
            s2 = _silu(c2)
            dn2 = jnp.where(j < H, _win(dq_ref, r0, 0, RC + 8, T, RC, edge),
                            jnp.where(is_qk, _win(dk_ref, r0, 0, RC + 8, T, RC, edge),
                                      _win(dv_ref, r0, 0, RC + 8, T, RC, edge)))
            r = lax.rsqrt(jnp.sum(s2 * s2, axis=-1, keepdims=True) + EPS)
            nh = s2 * r
            dnp = dn2 * scale
            ds_qk = r * (dnp - nh * jnp.sum(dnp * nh, axis=-1, keepdims=True))
            ds2 = jnp.where(is_qk, ds_qk, dn2)
            dc2 = ds2 * _dsilu(c2)
            dx_ref[pl.ds(r0, RC), :] = _untaps(scr_ref, dc2, w_ref, K, RC).astype(dx_ref.dtype)
            return tuple(accs[jj] + _fold8(dc2[:RC] * xs[jj][:RC]) for jj in range(K))

        accs = _peeled(T // RC, RC, step, tuple(jnp.zeros((8, LANE), F32) for _ in range(K)))
        for jj in range(K):
            dw_ref[jj:jj + 1, :] = jnp.sum(accs[jj], axis=0, keepdims=True)

    col = pl.BlockSpec((T, LANE), lambda j: (0, j))
    wsp = pl.BlockSpec((K, LANE), lambda j: (0, j))
    return pl.pallas_call(
        body, name=name, grid=(nb,),
        in_specs=_col_specs(T, (off,)) + [wsp] + [
            pl.BlockSpec((T, LANE), functools.partial(lambda o, j: (0, jnp.clip(j - o, 0, H - 1)), o)) for o in (0, H, 2 * H)],
        out_specs=[col, wsp],
        out_shape=[jax.ShapeDtypeStruct((T, nb * LANE), BF16), jax.ShapeDtypeStruct((K, nb * LANE), F32)],
        scratch_shapes=[pltpu.VMEM((RC + 8, LANE), F32)],
        compiler_params=_params(("parallel",)),
    )(proj, conv_w, dq, dk, dv)


def _softplus(x):
    return jnp.maximum(x, 0.0) + jnp.log(1.0 + jnp.exp(-jnp.abs(x)))


def _gates_fwd(proj, alog, dtb, off, H, *, name):
    T = proj.shape[0]
    tr = _tile(T, 512, CHUNK)

    def body(ab_ref, al_ref, dt_ref, gam_ref, beta_ref):
        ab = ab_ref[...]
        lane = lax.broadcasted_iota(jnp.int32, ab.shape, 1)
        g = -jnp.exp(al_ref[...]) * _softplus(ab + dt_ref[...])
        gb = jnp.where(lane < H, g, jnp.where(lane < 2 * H, jax.nn.sigmoid(ab), 0.0))
        tril = _tri().astype(F32)
        gam = jnp.concatenate([_mm(tril, gb[c * CHUNK:(c + 1) * CHUNK, :], precision=lax.Precision.HIGHEST)
                               for c in range(tr // CHUNK)], axis=0)
        for h in range(H):
            gam_ref[h] = jnp.broadcast_to(gam[:, h:h + 1], (tr, LANE))
            beta_ref[h] = jnp.broadcast_to(gb[:, H + h:H + h + 1], (tr, LANE))

    vec = pl.BlockSpec((1, LANE), lambda i: (0, 0))
    heads = pl.BlockSpec((H, tr, LANE), lambda i: (0, i, 0))
    return pl.pallas_call(
        body, name=name, grid=(T // tr,),
        in_specs=[pl.BlockSpec((tr, LANE), lambda i: (i, off)), vec, vec],
        out_specs=[heads, heads],
        out_shape=[jax.ShapeDtypeStruct((H, T, LANE), F32)] * 2, compiler_params=_params(("parallel",)),
    )(proj, alog, dtb)


def _gates_bwd(proj, alog, dtb, dgamB, dbB, off, H, *, name):
    T = proj.shape[0]
    tr = _tile(T, 512, CHUNK)

    def body(ab_ref, al_ref, dt_ref, dgam_ref, dbeta_ref, dab_ref, dal_ref, ddt_ref):
        ab = ab_ref[...]
        lane = lax.broadcasted_iota(jnp.int32, ab.shape, 1)
        is_g = lane < H
        d = jnp.zeros_like(ab)
        for h in range(H):
            d = jnp.where(lane == h, dgam_ref[h], jnp.where(lane == H + h, dbeta_ref[h], d))
        triu = _tri(upper=True).astype(F32)
        dg = jnp.concatenate([_mm(triu, d[c * CHUNK:(c + 1) * CHUNK, :], precision=lax.Precision.HIGHEST)
                              for c in range(tr // CHUNK)], axis=0)
        z = ab + dt_ref[...]
        A = -jnp.exp(al_ref[...])
        da = dg * A * jax.nn.sigmoid(z)
        beta = jax.nn.sigmoid(ab)
        db = d * beta * (1.0 - beta)
        dab_ref[...] = jnp.where(is_g, da, jnp.where(lane < 2 * H, db, 0.0)).astype(dab_ref.dtype)

        @pl.when(pl.program_id(0) == 0)
        def _():
            dal_ref[...] = jnp.zeros_like(dal_ref)
            ddt_ref[...] = jnp.zeros_like(ddt_ref)

        dal_ref[...] += jnp.sum(jnp.where(is_g, dg * A * _softplus(z), 0.0), axis=0, keepdims=True)
        ddt_ref[...] += jnp.sum(jnp.where(is_g, da, 0.0), axis=0, keepdims=True)

    vec = pl.BlockSpec((1, LANE), lambda i: (0, 0))
    row = pl.BlockSpec((tr, LANE), lambda i: (i, 0))
    heads = pl.BlockSpec((H, tr, LANE), lambda i: (0, i, 0))
    return pl.pallas_call(
        body, name=name, grid=(T // tr,),
        in_specs=[pl.BlockSpec((tr, LANE), lambda i: (i, off)), vec, vec, heads, heads],
        out_specs=[row, vec, vec],
        out_shape=[jax.ShapeDtypeStruct((T, LANE), BF16), jax.ShapeDtypeStruct((1, LANE), F32),
                   jax.ShapeDtypeStruct((1, LANE), F32)],
        compiler_params=_params(("arbitrary",)),
    )(proj, alog, dtb, dgamB, dbB)


def _gated_norm_fwd(o, proj, gn, zoff, ycat, *, name):
    T, W = o.shape
    tr = _tile(T, 512, 8)
    nh_, zblk = W // LANE, (zoff * LANE) // W
    assert zblk * W == zoff * LANE

    def body(o_ref, z_ref, g_ref, ycat_ref, y_ref):
        for h in range(nh_):
            ln = slice(h * LANE, (h + 1) * LANE)
            ov = o_ref[:, ln]
            r = lax.rsqrt(jnp.mean(ov * ov, axis=-1, keepdims=True) + EPS)
            y_ref[:, ln] = (ov * r * g_ref[...] * _silu(z_ref[:, ln])).astype(y_ref.dtype)

    assert ycat.shape == (T, 2 * W), ycat.shape
    blk = pl.BlockSpec((tr, W), lambda i: (i, 0))
    return pl.pallas_call(
        body, name=name, grid=(T // tr,),
        in_specs=[blk, pl.BlockSpec((tr, W), lambda i: (i, zblk)), pl.BlockSpec((1, LANE), lambda i: (0, 0)),
                  pl.BlockSpec(memory_space=pl.ANY)],
        out_specs=pl.BlockSpec((tr, W), lambda i: (i, 1)), out_shape=jax.ShapeDtypeStruct(ycat.shape, ycat.dtype),
        input_output_aliases={3: 0}, compiler_params=_params(("parallel",)),
    )(o, proj, gn, ycat)


def _gated_norm_bwd(o, proj, gn, dycat, zoff, yoff, *, name):
    T, W = o.shape
    tr = _tile(T, 512, 8)
    nh_, zblk, yblk = W // LANE, (zoff * LANE) // W, (yoff * LANE) // W
    assert zblk * W == zoff * LANE and yblk * W == yoff * LANE

    def body(o_ref, z_ref, g_ref, dy_ref, do_ref, dz_ref, dg_ref):
        @pl.when(pl.program_id(0) == 0)
        def _():
            dg_ref[...] = jnp.zeros_like(dg_ref)

        gv = g_ref[...]
        dg = jnp.zeros_like(gv)
        for h in range(nh_):
            ln = slice(h * LANE, (h + 1) * LANE)
            ov, zv, dy = o_ref[:, ln], z_ref[:, ln], dy_ref[:, ln]
            r = lax.rsqrt(jnp.mean(ov * ov, axis=-1, keepdims=True) + EPS)
            nh = ov * r
            s = _silu(zv)
            dg = dg + jnp.sum(dy * nh * s, axis=0, keepdims=True)
            dz_ref[:, ln] = (dy * nh * gv * _dsilu(zv)).astype(dz_ref.dtype)
            dn = dy * gv * s
            do_ref[:, ln] = r * (dn - nh * jnp.mean(dn * nh, axis=-1, keepdims=True))
        dg_ref[...] += dg

    blk = pl.BlockSpec((tr, W), lambda i: (i, 0))
    vec = pl.BlockSpec((1, LANE), lambda i: (0, 0))
    return pl.pallas_call(
        body, name=name, grid=(T // tr,),
        in_specs=[blk, pl.BlockSpec((tr, W), lambda i: (i, zblk)), vec, pl.BlockSpec((tr, W), lambda i: (i, yblk))],
        out_specs=[blk, blk, vec],
        out_shape=[jax.ShapeDtypeStruct((T, W), F32), jax.ShapeDtypeStruct((T, W), BF16),
                   jax.ShapeDtypeStruct((1, LANE), F32)],
        compiler_params=_params(("arbitrary",)),
    )(o, proj, gn, dycat)


def _ffn_act_fwd(up_pre, conv_w, *, name):
    T, F2 = up_pre.shape[1], up_pre.shape[0] * LANE
    RC = _tile(T, ROWS_FFN_FWD, 8)
    nb = F2 // 2 // LANE
    K = conv_w.shape[0]

    def body(g_ref, v_ref, wg_ref, wv_ref, y_ref):
        def step(r0, carry, edge):
            _, gate = _taps(g_ref, wg_ref, K, r0, RC, T, RC, edge)
            _, val = _taps(v_ref, wv_ref, K, r0, RC, T, RC, edge)
            y_ref[pl.ds(r0, RC), :] = (_silu(gate) * val).astype(y_ref.dtype)
            return carry
        _peeled(T // RC, RC, step, 0)

    return pl.pallas_call(
        body, name=name, grid=(nb,),
        in_specs=[pl.BlockSpec((None, T, LANE), lambda j: (j, 0, 0)), pl.BlockSpec((None, T, LANE), lambda j: (nb + j, 0, 0)),
                  pl.BlockSpec((K, LANE), lambda j: (0, j)), pl.BlockSpec((K, LANE), lambda j: (0, nb + j))],
        out_specs=pl.BlockSpec((T, LANE), lambda j: (0, j)),
        out_shape=jax.ShapeDtypeStruct((T, F2 // 2), BF16), compiler_params=_params(("parallel",)),
    )(up_pre, up_pre, conv_w, conv_w)


def _ffn_act_bwd(up_pre, conv_w, dact, *, name):
    T, F2 = up_pre.shape[1], up_pre.shape[0] * LANE
    RC = _tile(T, ROWS_FFN_BWD, 8)
    nb = F2 // 2 // LANE
    K = conv_w.shape[0]

    def body(g_ref, v_ref, wg_ref, wv_ref, da_ref, d_ref, dwg_ref, dwv_ref, sg_ref, sv_ref):
        def step(r0, accs, edge):
            gs, gate2 = _taps(g_ref, wg_ref, K, r0, RC + 8, T, RC, edge)
            vs, val2 = _taps(v_ref, wv_ref, K, r0, RC + 8, T, RC, edge)
            da2 = _win(da_ref, r0, 0, RC + 8, T, RC, edge)
            dgate2 = da2 * val2 * _dsilu(gate2)
            dval2 = da2 * _silu(gate2)
            d_ref[0, pl.ds(r0, RC), :] = _untaps(sg_ref, dgate2, wg_ref, K, RC).astype(d_ref.dtype)
            d_ref[1, pl.ds(r0, RC), :] = _untaps(sv_ref, dval2, wv_ref, K, RC).astype(d_ref.dtype)
            new = []
            for j in range(K):
                new.append(accs[2 * j] + _fold8(dgate2[:RC] * gs[j][:RC]))
                new.append(accs[2 * j + 1] + _fold8(dval2[:RC] * vs[j][:RC]))
            return tuple(new)

        accs = _peeled(T // RC, RC, step, tuple(jnp.zeros((8, LANE), F32) for _ in range(2 * K)))
        for j in range(K):
            dwg_ref[j:j + 1, :] = jnp.sum(accs[2 * j], axis=0, keepdims=True)
            dwv_ref[j:j + 1, :] = jnp.sum(accs[2 * j + 1], axis=0, keepdims=True)

    col = pl.BlockSpec((T, LANE), lambda j: (0, j))
    wsp = pl.BlockSpec((K, LANE), lambda j: (0, j))
    return pl.pallas_call(
        body, name=name, grid=(nb,),
        in_specs=[pl.BlockSpec((None, T, LANE), lambda j: (j, 0, 0)), pl.BlockSpec((None, T, LANE), lambda j: (nb + j, 0, 0)),
                  wsp, pl.BlockSpec((K, LANE), lambda j: (0, nb + j)), col],
        out_specs=[pl.BlockSpec((2, T, LANE), lambda j: (0, 0, j)), wsp, wsp],
        out_shape=[jax.ShapeDtypeStruct((2, T, F2 // 2), BF16)] + [jax.ShapeDtypeStruct((K, F2 // 2), F32)] * 2,
        scratch_shapes=[pltpu.VMEM((RC + 8, LANE), F32)] * 2,
        compiler_params=_params(("parallel",)),
    )(up_pre, up_pre, conv_w, conv_w, dact)


CPB = 8
CPB_SCAN = 4
GRP = 8
HP = lax.Precision.HIGH


def _tri(strict=False, upper=False):
    r = lax.broadcasted_iota(jnp.int32, (CHUNK, CHUNK), 0)
    c = lax.broadcasted_iota(jnp.int32, (CHUNK, CHUNK), 1)
    if upper:
        return c >= r
    return (r > c) if strict else (r >= c)


def _mm(a, b, dn="nn", precision=None):
    precision = HP if precision is None else precision
    return lax.dot_general(a, b, _DN[dn], precision=precision, preferred_element_type=F32)


def _mm16(a, b, dn="nn"):
    return lax.dot_general(a.astype(BF16), b.astype(BF16), _DN[dn], preferred_element_type=F32)


def _each(f, *cols):
    return [f(*xs) for xs in zip(*cols)]


def _decay(gam):
    return jnp.exp(jnp.where(_tri(), gam[:, :CHUNK] - gam.T[:CHUNK, :], -1e30))


def _delta_specs(T, H, cpb):
    rows = cpb * CHUNK
    col = lambda o: pl.BlockSpec((rows, LANE), functools.partial(lambda o, h, n: (n, o + h), o))
    bc = pl.BlockSpec((1, rows, LANE), lambda h, n: (h, n, 0))
    sq = pl.BlockSpec((1, cpb, CHUNK, CHUNK), lambda h, n: (h, n, 0, 0))
    vec = pl.BlockSpec((1, cpb, 1, LANE), lambda h, n: (h, n, 0, 0))
    return col, bc, sq, vec


def _delta_prep_fwd(qkv, gamB, bB, H, *, name):
    T = qkv.shape[0]
    N = T // CHUNK
    cpb = _tile(N, CPB, 8)
    grp = min(GRP, cpb)
    col, bc, sq, vec = _delta_specs(T, H, cpb)

    def body(q_ref, k_ref, v_ref, g_ref, b_ref, u_ref, w_ref, qd_ref, kd_ref, qk_ref, ti_ref, gl_ref):
        eye = (lax.broadcasted_iota(jnp.int32, (CHUNK, CHUNK), 0) == lax.broadcasted_iota(jnp.int32, (CHUNK, CHUNK), 1)).astype(F32)
        strict = _tri(strict=True)
        for c0 in range(0, cpb, grp):
            cs = list(range(c0, c0 + grp))
            rows = [slice(c * CHUNK, (c + 1) * CHUNK) for c in cs]
            q, k, v = ([r_[r, :] for r in rows] for r_ in (q_ref, k_ref, v_ref))
            bb = [b_ref[0, r, :] for r in rows]
            gam = [g_ref[0, r, :] for r in rows]
            D = _each(_decay, gam)
            e = _each(jnp.exp, gam)
            kk = _each(lambda k_: _mm16(k_, k_, "nt"), k)
            X = _each(lambda kk_, D_, b_: -(jnp.where(strict, kk_ * D_, 0.0) * b_[:, :CHUNK]), kk, D, bb)
            R = _each(lambda x: eye + x, X)
            for _ in range(5):
                X = _each(lambda x: _mm(x, x), X)
                R = _each(lambda r, x: r + _mm(r, x), R, X)
            u = _each(lambda r, b_, v_: _mm(r, b_ * v_), R, bb, v)
            w = _each(lambda r, b_, e_, k_: _mm(r, b_ * e_ * k_), R, bb, e, k)
            qk = _each(lambda q_, k_, D_: _mm16(q_, k_, "nt") * D_, q, k, D)
            for i, c in enumerate(cs):
                glast = gam[i][CHUNK - 1:CHUNK, :]
                u_ref[rows[i], :] = u[i]
                w_ref[rows[i], :] = w[i]
                qd_ref[rows[i], :] = e[i] * q[i]
                kd_ref[rows[i], :] = jnp.exp(glast - gam[i]) * k[i]
                qk_ref[0, c] = qk[i]
                ti_ref[0, c] = R[i]
                gl_ref[0, c] = jnp.exp(glast)

    full = jax.ShapeDtypeStruct((T, H * LANE), F32)
    sqs = jax.ShapeDtypeStruct((H, N, CHUNK, CHUNK), F32)
    return pl.pallas_call(
        body, name=name, grid=(H, N // cpb),
        in_specs=[col(0), col(H), col(2 * H), bc, bc],
        out_specs=[col(0)] * 4 + [sq, sq, vec],
        out_shape=[full] * 4 + [sqs, sqs, jax.ShapeDtypeStruct((H, N, 1, LANE), F32)],
        compiler_params=_params(("parallel", "parallel")),
    )(qkv, qkv, qkv, gamB, bB)


def _scan_specs(H, N, cpb, hb, rev):
    nbk = N // cpb
    blk = (lambda n: nbk - 1 - n) if rev else (lambda n: n)
    col = pl.BlockSpec((cpb * CHUNK, hb * LANE), lambda h, n: (blk(n), h))
    sq = pl.BlockSpec((hb, cpb, CHUNK, CHUNK), lambda h, n: (h, blk(n), 0, 0))
    vec = pl.BlockSpec((hb, cpb, 1, LANE), lambda h, n: (h, blk(n), 0, 0))
    st = pl.BlockSpec((hb, cpb, HEAD, HEAD), lambda h, n: (h, blk(n), 0, 0))
    return col, sq, vec, st


def _delta_scan_fwd(u, w, qd, kd, qk, gl, H, *, name):
    T = u.shape[0]
    N = T // CHUNK
    cpb = _tile(N, CPB_SCAN, 4)
    hb = min(GRP, H)
    col, sq, vec, st = _scan_specs(H, N, cpb, hb, False)
    lanes = [slice(j * LANE, (j + 1) * LANE) for j in range(hb)]
    heads = list(range(hb))

    def body(u_ref, w_ref, qd_ref, kd_ref, qk_ref, gl_ref, o_ref, vn_ref, ss_ref, s_scr):
        @pl.when(pl.program_id(1) == 0)
        def _():
            s_scr[...] = jnp.zeros_like(s_scr)

        def step(c, states):
            rows = pl.ds(pl.multiple_of(c * CHUNK, CHUNK), CHUNK)
            S = list(states)
            for j in heads:
                ss_ref[j, c] = S[j]
            wS = _each(lambda ln, s: _mm16(w_ref[rows, ln], s), lanes, S)
            qS = _each(lambda ln, s: _mm16(qd_ref[rows, ln], s), lanes, S)
            vn = _each(lambda ln, ws: u_ref[rows, ln] - ws, lanes, wS)
            o = _each(lambda j, qs, vn_: qs + _mm16(qk_ref[j, c], vn_), heads, qS, vn)
            new = _each(lambda j, ln, s, vn_: s * gl_ref[j, c] + _mm16(kd_ref[rows, ln], vn_, "tn"),
                        heads, lanes, S, vn)
            for j in heads:
                o_ref[rows, lanes[j]] = o[j]
                vn_ref[rows, lanes[j]] = vn[j]
            return tuple(new)
        out = lax.fori_loop(0, cpb, step, tuple(s_scr[j] for j in heads))
        for j in heads:
            s_scr[j] = out[j]

    full = jax.ShapeDtypeStruct((T, H * LANE), F32)
    return pl.pallas_call(
        body, name=name, grid=(H // hb, N // cpb),
        in_specs=[col] * 4 + [sq, vec],
        out_specs=[col, col, st],
        out_shape=[full, full, jax.ShapeDtypeStruct((H, N, HEAD, HEAD), F32)],
        scratch_shapes=[pltpu.VMEM((hb, HEAD, HEAD), F32)],
        compiler_params=_params(("parallel", "arbitrary")),
    )(u, w, qd, kd, qk, gl)


def _delta_scan_bwd(do, w, qd, kd, vn, qk, gl, ss, H, *, name):
    T = do.shape[0]
    N = T // CHUNK
    cpb = _tile(N, CPB_SCAN, 4)
    hb = min(GRP, H)
    col, sq, vec, st = _scan_specs(H, N, cpb, hb, True)
    lanes = [slice(j * LANE, (j + 1) * LANE) for j in range(hb)]
    heads = list(range(hb))

    def body(do_ref, w_ref, qd_ref, kd_ref, vn_ref, qk_ref, gl_ref, ss_ref,
             du_ref, dw_ref, dqd_ref, dkd_ref, dqk_ref, dgl_ref, ds_scr):
        @pl.when(pl.program_id(1) == 0)
        def _():
            ds_scr[...] = jnp.zeros_like(ds_scr)

        def step(i, dstates):
            c = cpb - 1 - i
            rows = pl.ds(pl.multiple_of(c * CHUNK, CHUNK), CHUNK)
            dS = list(dstates)
            S = [ss_ref[j, c] for j in heads]
            dov = [do_ref[rows, ln] for ln in lanes]
            vnv = [vn_ref[rows, ln] for ln in lanes]
            a1 = _each(lambda j, d_: _mm16(qk_ref[j, c], d_, "tn"), heads, dov)
            a2 = _each(lambda ln, ds: _mm16(kd_ref[rows, ln], ds), lanes, dS)
            dvn = _each(lambda x, y: x + y, a1, a2)
            dqd = _each(lambda d_, s: _mm16(d_, s, "nt"), dov, S)
            dkd = _each(lambda v_, ds: _mm16(v_, ds, "nt"), vnv, dS)
            dqk = _each(lambda d_, v_: _mm16(d_, v_, "nt"), dov, vnv)
            dw = _each(lambda dv_, s: -_mm16(dv_, s, "nt"), dvn, S)
            b1 = _each(lambda ln, d_: _mm16(qd_ref[rows, ln], d_, "tn"), lanes, dov)
            b2 = _each(lambda ln, dv_: _mm16(w_ref[rows, ln], dv_, "tn"), lanes, dvn)
            new = _each(lambda j, x, y, ds: x + ds * gl_ref[j, c] - y, heads, b1, b2, dS)
            for j in heads:
                du_ref[rows, lanes[j]] = dvn[j]
                dw_ref[rows, lanes[j]] = dw[j]
                dqd_ref[rows, lanes[j]] = dqd[j]
                dkd_ref[rows, lanes[j]] = dkd[j]
                dqk_ref[j, c] = dqk[j]
                dgl = jnp.sum(jnp.sum(dS[j] * S[j], axis=1, keepdims=True), axis=0, keepdims=True)
                dgl_ref[j, c] = jnp.broadcast_to(dgl, (1, LANE))
            return tuple(new)
        out = lax.fori_loop(0, cpb, step, tuple(ds_scr[j] for j in heads))
        for j in heads:
            ds_scr[j] = out[j]

    full = jax.ShapeDtypeStruct((T, H * LANE), F32)
    return pl.pallas_call(
        body, name=name, grid=(H // hb, N // cpb),
        in_specs=[col] * 5 + [sq, vec, st],
        out_specs=[col] * 4 + [sq, vec],
        out_shape=[full] * 4 + [jax.ShapeDtypeStruct((H, N, CHUNK, CHUNK), F32), jax.ShapeDtypeStruct((H, N, 1, LANE), F32)],
        scratch_shapes=[pltpu.VMEM((hb, HEAD, HEAD), F32)],
        compiler_params=_params(("parallel", "arbitrary")),
    )(do, w, qd, kd, vn, qk, gl, ss)


def _delta_prep_bwd(qkv, gamB, bB, ti, u, w, qk, du, dw, dqd, dkd, dqk, dgl, H, *, name):
    T = qkv.shape[0]
    N = T // CHUNK
    cpb = _tile(N, CPB, 8)
    grp = min(GRP, cpb)
    col, bc, sq, vec = _delta_specs(T, H, cpb)

    def body(q_ref, k_ref, v_ref, g_ref, b_ref, ti_ref, u_ref, w_ref, qk_ref,
             du_ref, dw_ref, dqd_ref, dkd_ref, dqk_ref, dgl_ref,
             dq_ref, dk_ref, dv_ref, dg_ref, db_ref):
        ones = jnp.ones((CHUNK, LANE), F32)
        strict = _tri(strict=True)
        last = lax.broadcasted_iota(jnp.int32, (CHUNK, LANE), 0) == CHUNK - 1
        lsum = lambda x: jnp.sum(x, axis=-1, keepdims=True)
        for c0 in range(0, cpb, grp):
            cs = list(range(c0, c0 + grp))
            rows = [slice(c * CHUNK, (c + 1) * CHUNK) for c in cs]
            ld = lambda r_: [r_[r, :] for r in rows]
            q, k, v, uv, wv, duv, dwv, dqd_v, dkd_v = (ld(r_) for r_ in (q_ref, k_ref, v_ref, u_ref, w_ref, du_ref, dw_ref, dqd_ref, dkd_ref))
            bb = [b_ref[0, r, :] for r in rows]
            gam = [g_ref[0, r, :] for r in rows]
            Ti = [ti_ref[0, c] for c in cs]
            QK = [qk_ref[0, c] for c in cs]
            dqk_v = [dqk_ref[0, c] for c in cs]
            D = _each(_decay, gam)
            e = _each(jnp.exp, gam)
            glast = [g_[CHUNK - 1:CHUNK, :] for g_ in gam]
            eL = _each(lambda gl_, g_: jnp.exp(gl_ - g_), glast, gam)
            kk = _each(lambda k_: _mm16(k_, k_, "nt"), k)
            KKD = _each(lambda kk_, D_: jnp.where(strict, kk_ * D_, 0.0), kk, D)
            dru = _each(lambda t, d_: _mm(t, d_, "tn"), Ti, duv)
            drw = _each(lambda t, d_: _mm(t, d_, "tn"), Ti, dwv)
            l1 = _each(lambda a, b: _mm(a, b, "nt"), dru, uv)
            l2 = _each(lambda a, b: _mm(a, b, "nt"), drw, wv)
            dL = _each(lambda a, b: jnp.where(strict, -(a + b), 0.0), l1, l2)
            Mm = _each(lambda dl, b_: dl * b_[:, :CHUNK], dL, bb)
            dKK = _each(lambda m_, D_: m_ * D_, Mm, D)
            dQK = _each(lambda a, D_: a * D_, dqk_v, D)
            P = _each(lambda m_, kkd, a, qk_: m_ * kkd + a * qk_, Mm, KKD, dqk_v, QK)
            q1 = _each(lambda a, k_: _mm16(a, k_), dQK, k)
            k1 = _each(lambda a, q_: _mm16(a, q_, "tn"), dQK, q)
            k2 = _each(lambda a, k_: _mm16(a, k_), dKK, k)
            k3 = _each(lambda a, k_: _mm16(a, k_, "tn"), dKK, k)
            s1 = _each(lambda dl, kkd: _mm(dl * kkd, ones), dL, KKD)
            p1 = _each(lambda p_: _mm(p_, ones), P)
            p2 = _each(lambda p_: _mm(p_, ones, "tn"), P)
            for i, c in enumerate(cs):
                r = rows[i]
                bek = bb[i] * e[i]
                kdv = eL[i] * k[i]
                dq_ref[r, :] = q1[i] + e[i] * dqd_v[i]
                dk_ref[r, :] = k1[i] + k2[i] + k3[i] + bek * drw[i] + eL[i] * dkd_v[i]
                dv_ref[r, :] = bb[i] * dru[i]
                db_ref[0, r, :] = s1[i] + lsum(dru[i] * v[i]) + lsum(drw[i] * e[i] * k[i])
                dgam = (p1[i] - p2[i] + lsum(drw[i] * bek * k[i]) + lsum(dqd_v[i] * e[i] * q[i])
                        - lsum(dkd_v[i] * kdv))
                xlast = jnp.sum(lsum(dkd_v[i] * kdv), axis=0, keepdims=True) + jnp.exp(glast[i]) * dgl_ref[0, c]
                dg_ref[0, r, :] = dgam + jnp.where(last, xlast, 0.0)

    full = jax.ShapeDtypeStruct((T, H * LANE), F32)
    bcs = jax.ShapeDtypeStruct((H, T, LANE), F32)
    return pl.pallas_call(
        body, name=name, grid=(H, N // cpb),
        in_specs=[col(0), col(H), col(2 * H), bc, bc, sq, col(0), col(0), sq, col(0), col(0), col(0), col(0), sq, vec],
        out_specs=[col(0), col(0), col(0), bc, bc],
        out_shape=[full, full, full, bcs, bcs],
        compiler_params=_params(("parallel", "parallel")),
    )(qkv, qkv, qkv, gamB, bB, ti, u, w, qk, du, dw, dqd, dkd, dqk, dgl)


def _adam(parts, w, m, v, *, name, own=None, me=None):
    P, R, C = parts.shape
    if R > 256 and R % 8:
        tr, tc = R, _tile(C, 256)
    else:
        tr, tc = _tile(R, 256, 8), C
    n_own = 0 if own is None else 2

    def body(*refs):
        p_ref, w_ref, m_ref, v_ref, g_ref, d_ref, nm_ref, nv_ref = refs[n_own:]
        g = None
        for i in range(P):
            t = p_ref[i].astype(F32)
            if n_own:
                t = jnp.where(refs[0][0] == i, refs[1][...].astype(F32), t)
            g = t if g is None else g + t
        mn = ADAM_B1 * m_ref[...] + (1.0 - ADAM_B1) * g
        vn = ADAM_B2 * v_ref[...] + (1.0 - ADAM_B2) * (g * g)
        m_hat = mn / (1.0 - ADAM_B1 ** ADAM_STEP)
        v_hat = vn / (1.0 - ADAM_B2 ** ADAM_STEP)
        g_ref[...] = g
        d_ref[...] = -ADAM_LR * (m_hat / (jnp.sqrt(v_hat) + ADAM_EPS) + ADAM_WD * w_ref[...])
        nm_ref[...] = mn
        nv_ref[...] = vn

    blk = pl.BlockSpec((tr, tc), lambda i, j: (i, j))
    return pl.pallas_call(
        body, name=name, grid=(R // tr, C // tc),
        in_specs=[pl.BlockSpec(memory_space=pltpu.SMEM), blk][:n_own] + [pl.BlockSpec((P, tr, tc), lambda i, j: (0, i, j)), blk, blk, blk],
        out_specs=[blk] * 4, out_shape=[jax.ShapeDtypeStruct((R, C), F32)] * 4,
        compiler_params=_params(("parallel", "parallel")),
    )(*([me, own] if n_own else []), parts, w, m, v)


def _mesh_pos():
    return lax.axis_index("x"), lax.axis_index("y"), lax.axis_index("c")


def _peer(k):
    x, y, c = _mesh_pos()
    px, py, pc = x ^ ((k >> 2) & 1), y ^ ((k >> 1) & 1), c ^ (k & 1)
    return (px, py, pc), 4 * px + 2 * py + pc


def _exchange(arrays, scatter, *, name, after=None):
    n = len(arrays)
    n_in = n if after is None else n + 1
    blocks = [a.shape[1:] if scatter else a.shape for a in arrays]

    def body(*refs):
        srcs, dsts = refs[:n], refs[n_in:n_in + n]
        send_sems, recv_sems, local_sems = refs[n_in + n:]
        x, y, c = _mesh_pos()
        me = 4 * x + 2 * y + c
        local, sends = [], []
        for a in range(n):
            cp = pltpu.make_async_copy(srcs[a].at[me] if scatter else srcs[a], dsts[a].at[me], local_sems.at[a])
            cp.start()
            local.append(cp)
            for k in range(1, N_DEV):
                dev, idx = _peer(k)
                cp = pltpu.make_async_remote_copy(
                    src_ref=srcs[a].at[idx] if scatter else srcs[a], dst_ref=dsts[a].at[me],
                    send_sem=send_sems.at[a * N_DEV + k], recv_sem=recv_sems.at[a * N_DEV + k],
                    device_id=dev, device_id_type=MESH)
                cp.start()
                sends.append(cp)
        for a in range(n):
            for k in range(1, N_DEV):
                dev, idx = _peer(k)
                pltpu.make_async_remote_copy(
                    src_ref=srcs[a].at[idx] if scatter else srcs[a], dst_ref=dsts[a].at[idx],
                    send_sem=send_sems.at[a * N_DEV + k], recv_sem=recv_sems.at[a * N_DEV + k],
                    device_id=dev, device_id_type=MESH).wait_recv()
        for cp in sends:
            cp.wait_send()
        for cp in local:
            cp.wait()

    anyspec = pl.BlockSpec(memory_space=pl.ANY)
    return pl.pallas_call(
        body, name=name, in_specs=[anyspec] * n_in, out_specs=[anyspec] * n,
        out_shape=[jax.ShapeDtypeStruct((N_DEV,) + tuple(b), a.dtype) for a, b in zip(arrays, blocks)],
        scratch_shapes=[pltpu.SemaphoreType.DMA((n * N_DEV,)), pltpu.SemaphoreType.DMA((n * N_DEV,)),
                        pltpu.SemaphoreType.DMA((n,))],
    )(*arrays, *([] if after is None else [after]))


_ANY = pl.BlockSpec(memory_space=pl.ANY)
_SEM = pl.BlockSpec(memory_space=pltpu.SEMAPHORE)
_EFFECT = pltpu.SideEffectType.DATAFLOW_SIDE_EFFECTING


def _in_hbm(a):
    return pltpu.with_memory_space_constraint(a, pltpu.HBM)


def _split_copy(src, land, send, recv, k, me, scatter, landed):
    dev, idx = _peer(k)
    return pltpu.make_async_remote_copy(
        src_ref=src.at[idx] if scatter else src, dst_ref=land.at[idx if landed else me],
        send_sem=send.at[k], recv_sem=recv.at[k], device_id=dev, device_id_type=MESH)


ALL_PEERS = tuple(range(1, N_DEV))
SIBLING = 1
SAME_CORE = (2, 4, 6)


def _split_start(srcs, lands, scatter, *, name, relations=None):
    n = len(srcs)
    relations = relations or [ALL_PEERS] * n

    def body(*refs):
        src, land, send, recv, token = refs[:n], refs[n:2 * n], refs[2 * n:3 * n], refs[3 * n:4 * n], refs[-1]
        x, y, c = _mesh_pos()
        me = 4 * x + 2 * y + c
        for a in range(n):
            for k in relations[a]:
                _split_copy(src[a], land[a], send[a], recv[a], k, me, scatter, False).start()
        token[...] = jnp.zeros_like(token)

    outs = pl.pallas_call(
        body, name=name,
        out_shape=[pltpu.SemaphoreType.DMA((N_DEV,))] * (2 * n) + [pltpu.HBM(t.shape, t.dtype) for t in list(srcs) + list(lands)]
        + [jax.ShapeDtypeStruct((8, LANE), F32)],
        in_specs=[_ANY] * (2 * n), out_specs=[_SEM] * (2 * n) + [_ANY] * (2 * n) + [pl.BlockSpec(memory_space=pltpu.VMEM)],
        input_output_aliases={i: 2 * n + i for i in range(2 * n)},
        compiler_params=pltpu.CompilerParams(has_side_effects=_EFFECT),
    )(*[_in_hbm(t) for t in list(srcs) + list(lands)])
    handles = [(outs[a], outs[n + a], outs[2 * n + a], outs[3 * n + a]) for a in range(n)]
    return handles, outs[-1]


def _split_wait(handle, after, scatter, *, name):
    send, recv, src_thru, land_thru = handle

    def body(src_ref, land_ref, send_ref, recv_ref, after_ref, src_out, land_out):
        x, y, c = _mesh_pos()
        me = 4 * x + 2 * y + c
        for k in range(1, N_DEV):
            cp = _split_copy(src_ref, land_ref, send_ref, recv_ref, k, me, scatter, True)
            cp.wait_send()
            cp.wait_recv()

    return pl.pallas_call(
        body, name=name,
        out_shape=(pltpu.HBM(src_thru.shape, src_thru.dtype), pltpu.HBM(land_thru.shape, land_thru.dtype)),
        in_specs=(_ANY, _ANY, _SEM, _SEM, _ANY), out_specs=(_ANY, _ANY), input_output_aliases={0: 0, 1: 1},
        compiler_params=pltpu.CompilerParams(has_side_effects=_EFFECT),
    )(src_thru, land_thru, send, recv, after)[1]


def _forward_copy(land, fsend, frecv, k, landed):
    x, y, c = _mesh_pos()
    _, idx = _peer(k | SIBLING if landed else k)
    return pltpu.make_async_remote_copy(src_ref=land.at[idx], dst_ref=land.at[idx], send_sem=fsend.at[k],
                                        recv_sem=frecv.at[k], device_id=(x, y, 1 - c), device_id_type=MESH)


def _gather_forward(handle, after, *, name):
    send, recv, src_thru, land_thru = handle

    def body(src_ref, land_ref, send_ref, recv_ref, after_ref, src_out, land_out, fsend, frecv):
        x, y, c = _mesh_pos()
        me = 4 * x + 2 * y + c
        for k in SAME_CORE:
            _split_copy(src_ref, land_ref, send_ref, recv_ref, k, me, False, True).wait_recv()
            _forward_copy(land_ref, fsend, frecv, k, False).start()

    src2, land2, fsend, frecv = pl.pallas_call(
        body, name=name,
        out_shape=(pltpu.HBM(src_thru.shape, src_thru.dtype), pltpu.HBM(land_thru.shape, land_thru.dtype),
                   pltpu.SemaphoreType.DMA((N_DEV,)), pltpu.SemaphoreType.DMA((N_DEV,))),
        in_specs=(_ANY, _ANY, _SEM, _SEM, _ANY), out_specs=(_ANY, _ANY, _SEM, _SEM), input_output_aliases={0: 0, 1: 1},
        compiler_params=pltpu.CompilerParams(has_side_effects=_EFFECT),
    )(src_thru, land_thru, send, recv, after)
    return (send, recv, src2, land2), (fsend, frecv)


def _gather_wait_two_level(handle, fwd, *, name):
    send, recv, src_thru, land_thru = handle
    fsend, frecv = fwd

    def body(src_ref, land_ref, send_ref, recv_ref, fsend_ref, frecv_ref, src_out, land_out):
        x, y, c = _mesh_pos()
        me = 4 * x + 2 * y + c
        for k in (SIBLING,) + SAME_CORE:
            _split_copy(src_ref, land_ref, send_ref, recv_ref, k, me, False, True).wait_send()
        _split_copy(src_ref, land_ref, send_ref, recv_ref, SIBLING, me, False, True).wait_recv()
        for k in SAME_CORE:
            _forward_copy(land_ref, fsend_ref, frecv_ref, k, False).wait_send()
            _forward_copy(land_ref, fsend_ref, frecv_ref, k, True).wait_recv()

    return pl.pallas_call(
        body, name=name,
        out_shape=(pltpu.HBM(src_thru.shape, src_thru.dtype), pltpu.HBM(land_thru.shape, land_thru.dtype)),
        in_specs=(_ANY, _ANY, _SEM, _SEM, _SEM, _SEM), out_specs=(_ANY, _ANY), input_output_aliases={0: 0, 1: 1},
        compiler_params=pltpu.CompilerParams(has_side_effects=_EFFECT),
    )(src_thru, land_thru, send, recv, fsend, frecv)[1]


def _local_step(x, p, tgt, S, wt, conv, emit):
    T, D = x.shape
    CW = DNW = D // 2
    H = DNW // HEAD
    nA, nD = CW // LANE, DNW // LANE
    qkv_off, z_off, ab_off = 3 * nA, 3 * nA + 3 * nD, 3 * nA + 4 * nD
    alog = jnp.pad(S["a_log"], ((0, 0), (0, LANE - H)))
    dtb = jnp.pad(S["dt_bias"], ((0, 0), (0, LANE - H)))

    h1 = _rms_fwd(x, S["g_mix"], name="rms1_fwd")
    pp = _matmul(p, wt("w_pp", h1), "nn", name="mm_pp", b_shards=True)
    w_in, cv = wt("w_in", pp), conv(pp)
    proj = _matmul(h1, w_in, "nt", name="mm_in", tn=1536)
    y_a = _group_a_fwd(proj, cv["conv_a"], CW, D, name="group_a_fwd")
    qkv = _qkv_fwd(proj, cv["conv_qkv"], qkv_off, H, name="qkv_fwd")
    gamB, bB = _gates_fwd(proj, alog, dtb, ab_off, H, name="gates_fwd")
    u, w, qd, kd, qk, ti, gl = _delta_prep_fwd(qkv, gamB, bB, H, name="delta_prep_fwd")
    o, vn, ss = _delta_scan_fwd(u, w, qd, kd, qk, gl, H, name="delta_scan_fwd")
    ycat = _gated_norm_fwd(o, proj, S["dn_g"], z_off, y_a, name="gated_norm_fwd")
    w_out = wt("w_out", ycat)
    rows = dict(tm=ROW_TILE, tn=D)
    x1, h2 = _matmul(ycat, w_out, "nn", name="mm_out", out_dtypes=(F32, BF16), epilogue=_epi_residual_rms,
                     extras=(x,), vec_extras=(S["g_ffn"],), **rows)
    w_up = wt("w_up", h2)
    up_pre = _matmul(h2, w_up, "nn", name="mm_up", b_shards=True, tn=SHARD_TILE, out_lanes=True)
    act = _ffn_act_fwd(up_pre, cv["conv_ffn"], name="ffn_act_fwd")
    w_down = wt("w_down", act)
    x2 = _matmul(act, w_down, "nn", name="mm_down", epilogue=lambda acc, r: (acc + r,), extras=(x1,), tk=LONG_K)
    h3 = _rms_fwd(x2, S["g_ple"], name="rms3_fwd")
    w_pg = wt("w_pg", h3)

    def ple_epi(acc, x2r, ppr):
        s = jax.nn.sigmoid(acc)
        return x2r + s * ppr, s

    x3, sg = _matmul(h3, w_pg, "nn", name="mm_pg", out_dtypes=(F32, F32), epilogue=ple_epi, extras=(x2, pp), tm=256, tn=D)
    dx3, dg_final, loss, dpg, dpp = _final_loss(x3, S["g_final"], tgt, pp, sg, name="final_loss")

    G = {"g_final": dg_final}
    tok = emit({"w_pp": _matmul(p, dpp, "tn", name="mm_dwpp", out_dtypes=(BF16,), out_shards=True, tk=LONG_K),
                "w_pg": _matmul(h3, dpg, "tn", name="mm_dwpg", out_dtypes=(BF16,), tk=LONG_K)})
    bwd = dict(out_dtypes=(F32, BF16), epilogue=_epi_rms_bwd(2), n_vec=1, **rows)
    dx2, dx2b, G["g_ple"] = _matmul(dpg, w_pg, "nt", name="mm_dh3", after=tok, extras=(x2, dx3),
                                    vec_extras=(S["g_ple"],), **bwd)
    tok = emit({"w_down": _matmul(act, dx2b, "tn", name="mm_dwdown", out_dtypes=(BF16,), tk=LONG_K)})
    dact = _matmul(dx2b, w_down, "nt", name="mm_dact", after=tok, tn=SHARD_TILE)
    dup, dcf_g, dcf_v = _ffn_act_bwd(up_pre, cv["conv_ffn"], dact, name="ffn_act_bwd")
    G["conv_ffn"] = jnp.concatenate([dcf_g, dcf_v], axis=1)
    tok = emit({"w_up": _matmul(h2, dup, "tn", name="mm_dwup", out_dtypes=(BF16,), b_shards=True, out_shards=True,
                                tn=SHARD_TILE, tk=LONG_K)})
    dh2 = _matmul(dup, w_up, "nt", name="mm_dh2", after=tok, a_shards=True, b_shards=True, tk=2 * SHARD_TILE)
    dx1, dx1b, G["g_ffn"] = _rms_bwd(x1, S["g_ffn"], dh2, dx2, name="rms2_bwd")
    tok = emit({"w_out": _matmul(ycat, dx1b, "tn", name="mm_dwout", out_dtypes=(BF16,), tk=LONG_K)})
    dycat = _matmul(dx1b, w_out, "nt", name="mm_dycat", after=tok)
    do, dz, G["dn_g"] = _gated_norm_bwd(o, proj, S["dn_g"], dycat, z_off, nA, name="gated_norm_bwd")
    du, dw, dqd, dkd, dqk, dgl = _delta_scan_bwd(do, w, qd, kd, vn, qk, gl, ss, H, name="delta_scan_bwd")
    dq, dk, dv, dgB, dbB = _delta_prep_bwd(qkv, gamB, bB, ti, u, w, qk, du, dw, dqd, dkd, dqk, dgl, H,
                                           name="delta_prep_bwd")
    dab, dal, ddt = _gates_bwd(proj, alog, dtb, dgB, dbB, ab_off, H, name="gates_bwd")
    G["a_log"], G["dt_bias"] = dal[:, :H], ddt[:, :H]
    dqkv, G["conv_qkv"] = _qkv_bwd(proj, cv["conv_qkv"], dq, dk, dv, qkv_off, H, name="qkv_bwd")
    dax, dab_, dac, G["conv_a"] = _group_a_bwd(proj, cv["conv_a"], dycat, CW, name="group_a_bwd")
    in_p = w_in.shape[0]
    dproj = jnp.concatenate([dax, dab_, dac, dqkv, dz, dab, jnp.zeros((T, in_p - (ab_off + 1) * LANE), BF16)], axis=1)
    tok = emit({"w_in": _matmul(dproj, h1, "tn", name="mm_dwin", out_dtypes=(BF16,), tk=LONG_K)})
    dh1 = _matmul(dproj, w_in, "nn", name="mm_dh1", after=tok, tk=LONG_K)
    grad_x, _, G["g_mix"] = _rms_bwd(x, S["g_mix"], dh1, dx1, name="rms1_bwd")
    return loss, grad_x, G


def _col_sharded(landed):
    _, R, C = landed.shape
    return jnp.transpose(landed, (1, 0, 2)).reshape(R, N_DEV * C)


def kernel(x, p, norm_mix_g, w_in, conv_a_w, conv_qkv_w, a_log, dt_bias, dn_norm_g, w_out, norm_ffn_g, w_up, conv_ffn_w, w_down, norm_ple_g, w_ple_gate, w_ple_proj, final_norm_g, loss_target, m_norm_mix_g, m_w_in, m_conv_a_w, m_conv_qkv_w, m_a_log, m_dt_bias, m_dn_norm_g, m_w_out, m_norm_ffn_g, m_w_up, m_conv_ffn_w, m_w_down, m_norm_ple_g, m_w_ple_gate, m_w_ple_proj, m_final_norm_g, v_norm_mix_g, v_w_in, v_conv_a_w, v_conv_qkv_w, v_a_log, v_dt_bias, v_dn_norm_g, v_w_out, v_norm_ffn_g, v_w_up, v_conv_ffn_w, v_w_down, v_norm_ple_g, v_w_ple_gate, v_w_ple_proj, v_final_norm_g):
    T, D = x.shape[1], x.shape[2]
    xd, _, cd = _mesh_pos()
    me = 4 * xd + 2 * lax.axis_index("y") + cd

    conv_sh = [conv_a_w[0], conv_qkv_w[0], conv_ffn_w[0]]
    conv_n = [c.size for c in conv_sh]
    pack_rows = -(-sum(conv_n) // LANE)
    conv_pack = jnp.pad(jnp.concatenate([c.reshape(-1) for c in conv_sh]), (0, pack_rows * LANE - sum(conv_n))).reshape(pack_rows, LANE)
    names = ["w_pp", "w_in", "conv", "w_out", "w_up", "w_down", "w_pg"]
    tr_ = lambda t: jnp.swapaxes(t, 1, 2)
    shards = [w_ple_proj[0].astype(BF16), w_in[0].T.astype(BF16), conv_pack, w_out[0].astype(BF16), w_up[0].astype(BF16),
              w_down[0].astype(BF16), w_ple_gate[0].astype(BF16)]
    empty_slots = lambda blocks: [lax.empty((N_DEV,) + tuple(b.shape), b.dtype) for b in blocks]
    handles, tok0 = _split_start(shards, empty_slots(shards), False, name="gather_start",
                                 relations=[(SIBLING,) + SAME_CORE if nm == "w_in" else ALL_PEERS for nm in names])
    handle = dict(zip(names, handles))
    own = dict(zip(names, shards))
    in_cols = N_DEV * w_in.shape[2]
    in_p = (in_cols // LANE) * LANE + AB_PAD
    in_place = {"w_up", "w_pp"}

    def gathered(name, after):
        if name == "w_in":
            passed, fwd = _gather_forward(handle[name], after, name="gather_forward_w_in")
            landed = _gather_wait_two_level(passed, fwd, name="gather_wait_w_in")
        else:
            landed = _split_wait(handle[name], after, False, name="gather_wait_" + name)
        return lax.dynamic_update_index_in_dim(landed, own[name], me, 0)

    def wt(name, after):
        landed = gathered(name, after)
        if name in in_place:
            return landed
        full = landed.reshape(-1, D)
        return jnp.pad(full, ((0, in_p - in_cols), (0, 0))) if name == "w_in" else full

    def conv(after):
        flat = gathered("conv", after).reshape(N_DEV, pack_rows * LANE)
        out, o_ = {}, 0
        for nm, c, n_ in zip(("conv_a", "conv_qkv", "conv_ffn"), conv_sh, conv_n):
            out[nm] = _col_sharded(flat[:, o_:o_ + n_].reshape((N_DEV,) + c.shape))
            o_ += n_
        return out

    pending, mine = {}, {}

    def emit(grads):
        parts = [g if nm in in_place else (g[:in_cols] if nm == "w_in" else g).reshape(N_DEV, -1, D)
                 for nm, g in grads.items()]
        hs, tok = _split_start(parts, empty_slots([q[0] for q in parts]), True, name="scatter_start_" + "_".join(grads))
        pending.update(zip(grads, hs))
        mine.update({nm: lax.dynamic_index_in_dim(q, me, 0, keepdims=False) for nm, q in zip(grads, parts)})
        return tok

    S = {
        "g_mix": norm_mix_g + tok0[0, 0], "a_log": a_log, "dt_bias": dt_bias, "dn_g": dn_norm_g, "g_ffn": norm_ffn_g,
        "g_ple": norm_ple_g, "g_final": final_norm_g.reshape(1, D),
    }

    loss_v, grad_x, G = _local_step(x[0], p[0, 0], loss_target[0], S, wt, conv, emit)
    loss = lax.psum(loss_v[0, 0], ("x", "y", "c"))

    small_names = ["g_mix", "g_ffn", "g_ple", "g_final", "dn_g", "a_log", "dt_bias", "conv_a", "conv_qkv", "conv_ffn"]
    small_rows, pieces = [], []
    for nm in small_names:
        g_ = G[nm].reshape(-1)
        r_ = -(-g_.size // (8 * LANE)) * 8
        small_rows.append(r_)
        pieces.append(jnp.pad(g_, (0, r_ * LANE - g_.size)).reshape(r_, LANE))
    landed = {nm: _split_wait(h_, grad_x, True, name="scatter_wait_" + nm) for nm, h_ in pending.items() if nm != "w_in"}

    def adam(parts, w_, m_, v_, nm, own_=None):
        shp = w_.shape
        w2, m2, v2 = (t.reshape(parts.shape[1:]) for t in (w_, m_, v_))
        kw = {} if own_ is None else {"own": own_, "me": me.astype(jnp.int32).reshape(1)}
        return tuple(t.reshape(shp) for t in _adam(parts, w2, m2, v2, name="adam_" + nm, **kw))

    big = {
        "w_up": adam(landed["w_up"], w_up, m_w_up, v_w_up, "w_up", mine["w_up"]),
        "w_down": adam(landed["w_down"], w_down, m_w_down, v_w_down, "w_down", mine["w_down"]),
        "w_out": adam(landed["w_out"], w_out, m_w_out, v_w_out, "w_out", mine["w_out"]),
        "w_pg": adam(landed["w_pg"], w_ple_gate, m_w_ple_gate, v_w_ple_gate, "w_ple_gate", mine["w_pg"]),
        "w_pp": adam(landed["w_pp"], w_ple_proj, m_w_ple_proj, v_w_ple_proj, "w_ple_proj", mine["w_pp"]),
    }
    first = lambda t: lax.slice(t, (0,) * t.ndim, (1,) * t.ndim).reshape(1)
    big_done = sum(first(r[1]) for r in big.values())
    (small_l,) = _exchange([jnp.concatenate(pieces, axis=0)], False, name="gather_small_grads", after=big_done)

    def small_parts(nm):
        i = small_names.index(nm)
        r0 = sum(small_rows[:i])
        shp = G[nm].shape
        return small_l[:, r0:r0 + small_rows[i], :].reshape(N_DEV, -1)[:, :G[nm].size].reshape((N_DEV,) + shp)

    def conv_parts(nm, shard):
        full = small_parts(nm)
        C = shard.shape[-1]
        return lax.dynamic_slice_in_dim(full, me * C, C, axis=2)

    res = [
        adam(small_parts("g_mix"), norm_mix_g, m_norm_mix_g, v_norm_mix_g, "norm_mix_g"),
        None,
        adam(conv_parts("conv_a", conv_a_w), conv_a_w, m_conv_a_w, v_conv_a_w, "conv_a_w"),
        adam(conv_parts("conv_qkv", conv_qkv_w), conv_qkv_w, m_conv_qkv_w, v_conv_qkv_w, "conv_qkv_w"),
        adam(small_parts("a_log"), a_log, m_a_log, v_a_log, "a_log"),
        adam(small_parts("dt_bias"), dt_bias, m_dt_bias, v_dt_bias, "dt_bias"),
        adam(small_parts("dn_g"), dn_norm_g, m_dn_norm_g, v_dn_norm_g, "dn_norm_g"),
        big["w_out"],
        adam(small_parts("g_ffn"), norm_ffn_g, m_norm_ffn_g, v_norm_ffn_g, "norm_ffn_g"),
        big["w_up"],
        adam(conv_parts("conv_ffn", conv_ffn_w), conv_ffn_w, m_conv_ffn_w, v_conv_ffn_w, "conv_ffn_w"),
        big["w_down"],
        adam(small_parts("g_ple"), norm_ple_g, m_norm_ple_g, v_norm_ple_g, "norm_ple_g"),
        big["w_pg"],
        big["w_pp"],
        adam(small_parts("g_final"), final_norm_g.reshape(1, D), m_final_norm_g.reshape(1, D),
             v_final_norm_g.reshape(1, D), "final_norm_g"),
    ]
    res[-1] = tuple(t.reshape(D) for t in res[-1])
    landed_in = _split_wait(pending["w_in"], res[10][1], True, name="scatter_wait_w_in")
    res[1] = tuple(tr_(t) for t in adam(landed_in, tr_(w_in), tr_(m_w_in), tr_(v_w_in), "w_in", mine["w_in"]))
    grads, deltas, new_m, new_v = zip(*res)
    return (loss, grad_x[None], *grads, *deltas, *new_m, *new_v)
```

```python
import functools

import jax
import jax.numpy as jnp
from jax import lax
from jax.experimental import pallas as pl
from jax.experimental.pallas import tpu as pltpu

F32 = jnp.float32
BF16 = jnp.bfloat16

EPS = 1e-6
CHUNK = 64
HEAD = 128
LANE = 128
N_DEV = 8
AB_PAD = 512

ADAM_LR = 0.001
ADAM_B1 = 0.9
ADAM_B2 = 0.999
ADAM_EPS = 1e-08
ADAM_WD = 0.01
ADAM_STEP = 10

MESH = pl.DeviceIdType.MESH


def _tile(dim, target, align=LANE):
    if dim <= target:
        return dim
    t = (target // align) * align
    while t > align and dim % t:
        t -= align
    assert dim % t == 0, (dim, target)
    return t


def _params(sem, vmem_mb=48):
    return pltpu.CompilerParams(dimension_semantics=sem, vmem_limit_bytes=vmem_mb << 20)


_DN = {"nn": (((1,), (0,)), ((), ())), "nt": (((1,), (1,)), ((), ())), "tn": (((0,), (0,)), ((), ()))}
LONG_K = 4096
SHARD_TILE = 1408


def _matmul(a, b, mode, *, name, out_dtypes=(F32,), epilogue=None, extras=(), vec_extras=(), n_vec=0, after=None,
            a_shards=False, b_shards=False, out_shards=False, out_lanes=False, tm=1024, tn=1024, tk=2048):
    shard_w = b.shape[2] if b_shards else None
    if b_shards:
        b_rows, b_cols = b.shape[1], b.shape[0] * shard_w
    else:
        b_rows, b_cols = b.shape
    a_w = a.shape[2] if a_shards else None
    a_dims = (a.shape[1], a.shape[0] * a_w) if a_shards else a.shape
    if mode == "nn":
        (M, K), (K2, N) = a_dims, (b_rows, b_cols)
    elif mode == "nt":
        (M, K), (N, K2) = a_dims, (b_rows, b_cols)
    else:
        (K, M), (K2, N) = a_dims, (b_rows, b_cols)
    assert K == K2, (name, a.shape, b.shape)
    tm = _tile(M, tm)
    n_dims = [N] + ([shard_w] if (b_shards and mode != "nt") else []) + ([N // N_DEV] if out_shards else [])
    tn = _tile(min(n_dims), tn)
    assert all(d % tn == 0 for d in n_dims), (name, n_dims, tn)
    grp = 1
    if b_shards and mode == "nt":
        grp = max(g for g in (1, 2, 4, 8) if g <= max(1, tk // shard_w) and (a_w is None or a_w % (g * shard_w) == 0))
    k_dims = [K] + ([shard_w] if (b_shards and mode == "nt") else []) + ([a_w] if a_shards else [])
    tk = grp * shard_w if grp > 1 else _tile(min(k_dims), tk)
    assert K % tk == 0, (name, K, tk)
    nk = K // tk
    n_ex, n_out = len(extras) + len(vec_extras), len(out_dtypes)
    assert n_vec == 0 or tn == N, (name, tn, N)
    dn = _DN[mode]

    n_tok = 0 if after is None else 1

    def body(a_ref, b_ref, *rest):
        rest = rest[n_tok:]
        ex_refs, out_refs, vec_refs = rest[:n_ex], rest[n_ex:n_ex + n_out], rest[n_ex + n_out:n_ex + n_out + n_vec]
        if grp > 1:
            part = sum(lax.dot_general(a_ref[:, s * shard_w:(s + 1) * shard_w].astype(BF16), b_ref[s].astype(BF16), dn,
                                       preferred_element_type=F32) for s in range(grp))
        else:
            part = lax.dot_general(a_ref[...].astype(BF16), b_ref[...].astype(BF16), dn, preferred_element_type=F32)
        first_rows = pl.program_id(0) == 0

        def finish(res):
            outs = (res,) if epilogue is None else epilogue(res, *[e[...] for e in ex_refs])
            for o_ref, val in zip(out_refs, outs[:n_out]):
                if out_lanes:
                    for c in range(tn // LANE):
                        o_ref[c] = val[:, c * LANE:(c + 1) * LANE].astype(o_ref.dtype)
                else:
                    o_ref[...] = val.astype(o_ref.dtype)
            for v_ref, val in zip(vec_refs, outs[n_out:]):
                @pl.when(first_rows)
                def _(v_ref=v_ref, val=val):
                    v_ref[...] = val

                @pl.when(jnp.logical_not(first_rows))
                def _(v_ref=v_ref, val=val):
                    v_ref[...] += val

        if nk == 1:
            finish(part)
            return
        acc, k = rest[-1], pl.program_id(2)

        @pl.when(k == 0)
        def _():
            acc[...] = part

        @pl.when(k > 0)
        def _():
            acc[...] += part

        @pl.when(k == nk - 1)
        def _():
            finish(acc[...])

    if a_shards:
        assert mode == "nt" and a_w % tk == 0, (name, mode, a_w, tk)
        per_a = a_w // tk
        a_spec = pl.BlockSpec((None, tm, tk), lambda i, j, k: (lax.div(k, per_a), i, lax.rem(k, per_a)))
    else:
        a_spec = pl.BlockSpec((tk, tm), lambda i, j, k: (k, i)) if mode == "tn" else pl.BlockSpec((tm, tk), lambda i, j, k: (i, k))
    if b_shards and mode != "nt":
        per = shard_w // tn
        b_spec = pl.BlockSpec((None, tk, tn), lambda i, j, k: (lax.div(j, per), k, lax.rem(j, per)))
    elif b_shards and grp > 1:
        b_spec = pl.BlockSpec((grp, tn, shard_w), lambda i, j, k: (k, j, 0))
    elif b_shards:
        per = shard_w // tk
        b_spec = pl.BlockSpec((None, tn, tk), lambda i, j, k: (lax.div(k, per), j, lax.rem(k, per)))
    else:
        b_spec = pl.BlockSpec((tn, tk), lambda i, j, k: (j, k)) if mode == "nt" else pl.BlockSpec((tk, tn), lambda i, j, k: (k, j))
    mn_spec = pl.BlockSpec((tm, tn), lambda i, j, k: (i, j))
    vec_spec = pl.BlockSpec((1, tn), lambda i, j, k: (0, j))
    if out_shards:
        assert not extras
        per_o = (N // N_DEV) // tn
        out_spec = pl.BlockSpec((None, tm, tn), lambda i, j, k: (lax.div(j, per_o), i, lax.rem(j, per_o)))
        out_dims = (N_DEV, M, N // N_DEV)
    elif out_lanes:
        assert not extras
        out_spec = pl.BlockSpec((tn // LANE, tm, LANE), lambda i, j, k: (j, i, 0))
        out_dims = (N // LANE, M, LANE)
    else:
        out_spec, out_dims = mn_spec, (M, N)
    outs = pl.pallas_call(
        body, name=name, grid=(M // tm, N // tn, nk),
        in_specs=[a_spec, b_spec] + [pl.BlockSpec((8, LANE), lambda i, j, k: (0, 0))] * n_tok
        + [mn_spec] * len(extras) + [vec_spec] * len(vec_extras),
        out_specs=[out_spec] * n_out + [vec_spec] * n_vec,
        out_shape=[jax.ShapeDtypeStruct(out_dims, dt) for dt in out_dtypes] + [jax.ShapeDtypeStruct((1, N), F32)] * n_vec,
        scratch_shapes=[pltpu.VMEM((tm, tn), F32)] if nk > 1 else [],
        compiler_params=_params(("arbitrary" if n_vec else "parallel", "parallel", "arbitrary"), 56),
    )(a, b, *([] if after is None else [after]), *extras, *vec_extras)
    return outs[0] if n_out + n_vec == 1 else outs


def _rms_fwd(x, g, *, name):
    T, D = x.shape
    tr = _tile(T, 512, 8)

    def body(x_ref, g_ref, h_ref):
        xv = x_ref[...]
        r = lax.rsqrt(jnp.mean(xv * xv, axis=-1, keepdims=True) + EPS)
        h_ref[...] = (xv * r * g_ref[...]).astype(h_ref.dtype)

    return pl.pallas_call(
        body, name=name, grid=(T // tr,),
        in_specs=[pl.BlockSpec((tr, D), lambda i: (i, 0)), pl.BlockSpec((1, D), lambda i: (0, 0))],
        out_specs=pl.BlockSpec((tr, D), lambda i: (i, 0)),
        out_shape=jax.ShapeDtypeStruct((T, D), BF16),
        compiler_params=_params(("parallel",)),
    )(x, g)


def _rms_bwd(x, g, dh, dres, *, name):
    T, D = x.shape
    tr = _tile(T, 512, 8)
    epi = _epi_rms_bwd(2)

    def body(x_ref, g_ref, dh_ref, dres_ref, dx_ref, dxb_ref, dg_ref):
        dx, _, dgp = epi(dh_ref[...], x_ref[...], dres_ref[...], g_ref[...])

        @pl.when(pl.program_id(0) == 0)
        def _():
            dg_ref[...] = jnp.zeros_like(dg_ref)

        dg_ref[...] += dgp
        dx_ref[...] = dx
        dxb_ref[...] = dx.astype(dxb_ref.dtype)

    row = pl.BlockSpec((tr, D), lambda i: (i, 0))
    vec = pl.BlockSpec((1, D), lambda i: (0, 0))
    return pl.pallas_call(
        body, name=name, grid=(T // tr,),
        in_specs=[row, vec, row, row], out_specs=[row, row, vec],
        out_shape=[jax.ShapeDtypeStruct((T, D), F32), jax.ShapeDtypeStruct((T, D), BF16), jax.ShapeDtypeStruct((1, D), F32)],
        compiler_params=_params(("arbitrary",)),
    )(x, g, dh, dres)


ROW_TILE = 512


def _epi_residual_rms(acc, res, g):
    xn = acc + res
    r = lax.rsqrt(jnp.mean(xn * xn, axis=-1, keepdims=True) + EPS)
    return xn, xn * r * g


def _epi_rms_bwd(n_copies):
    def epi(dh, x, dres, g):
        r = lax.rsqrt(jnp.mean(x * x, axis=-1, keepdims=True) + EPS)
        xh = x * r
        dxh = dh * g
        dx = dres + r * (dxh - xh * jnp.mean(dxh * xh, axis=-1, keepdims=True))
        return (dx,) * n_copies + (jnp.sum(dh * xh, axis=0, keepdims=True),)
    return epi


def _final_loss(x, g, tgt, pp, sg, *, name):
    T, D = x.shape
    tr = _tile(T, 256, 8)

    def body(x_ref, g_ref, t_ref, pp_ref, sg_ref, dx_ref, dg_ref, loss_ref, dpg_ref, dpp_ref):
        xv = x_ref[...]
        r = lax.rsqrt(jnp.mean(xv * xv, axis=-1, keepdims=True) + EPS)
        xh = xv * r
        gv = g_ref[...]
        err = xh * gv - t_ref[...]

        @pl.when(pl.program_id(0) == 0)
        def _():
            dg_ref[...] = jnp.zeros_like(dg_ref)
            loss_ref[...] = jnp.zeros_like(loss_ref)

        part = 0.5 * jnp.sum(jnp.mean(err * err, axis=-1, keepdims=True), axis=0, keepdims=True)
        loss_ref[...] += jnp.broadcast_to(part, loss_ref.shape)
        dy = err * (1.0 / D)
        dg_ref[...] += jnp.sum(dy * xh, axis=0, keepdims=True)
        dxh = dy * gv
        dx = r * (dxh - xh * jnp.mean(dxh * xh, axis=-1, keepdims=True))
        dx_ref[...] = dx
        s = sg_ref[...]
        dpg_ref[...] = (dx * pp_ref[...] * s * (1.0 - s)).astype(dpg_ref.dtype)
        dpp_ref[...] = (dx * s).astype(dpp_ref.dtype)

    row = pl.BlockSpec((tr, D), lambda i: (i, 0))
    vec = pl.BlockSpec((1, D), lambda i: (0, 0))
    return pl.pallas_call(
        body, name=name, grid=(T // tr,),
        in_specs=[row, vec, row, row, row], out_specs=[row, vec, pl.BlockSpec((1, LANE), lambda i: (0, 0)), row, row],
        out_shape=[jax.ShapeDtypeStruct((T, D), F32), jax.ShapeDtypeStruct((1, D), F32),
                   jax.ShapeDtypeStruct((1, LANE), F32)] + [jax.ShapeDtypeStruct((T, D), BF16)] * 2,
        compiler_params=_params(("arbitrary",)),
    )(x, g, tgt, pp, sg)


ROWS_QKV_FWD, ROWS_QKV_BWD, ROWS_FFN_FWD, ROWS_FFN_BWD, ROWS_GROUP_A = 512, 256, 256, 128, 256


def _ext(ref, r0, T, before, after, RC):
    parts = []
    if before:
        p0 = pl.multiple_of(jnp.maximum(r0 - 8, 0), 8)
        parts.append(jnp.where(r0 > 0, ref[pl.ds(p0, 8), :], 0.0))
    parts.append(ref[pl.ds(r0, RC), :])
    if after:
        n0 = pl.multiple_of(jnp.minimum(r0 + RC, T - 8), 8)
        parts.append(jnp.where(r0 + RC < T, ref[pl.ds(n0, 8), :], 0.0))
    return parts[0] if len(parts) == 1 else jnp.concatenate(parts, axis=0)


def _fold8(x):
    return jnp.sum(x.reshape(x.shape[0] // 8, 8, x.shape[1]), axis=0)


def _win(ref, r0, lo, n, T, RC, edge):
    if not edge:
        return ref[pl.ds(r0 + lo, n), :]
    xx = _ext(ref, r0, T, True, True, RC)
    a = 8 + lo
    return (xx if a == 0 else pltpu.roll(xx, xx.shape[0] - a, 0))[:n, :]


def _taps(ref, w_ref, K, r0, n, T, RC, edge):
    wins = [_win(ref, r0, -(K - 1 - j), n, T, RC, edge) for j in range(K)]
    y = wins[0] * w_ref[0:1, :]
    for j in range(1, K):
        y = y + wins[j] * w_ref[j:j + 1, :]
    return wins, y


def _untaps(scr_ref, val, w_ref, K, RC):
    scr_ref[0:val.shape[0], :] = val
    y = scr_ref[K - 1:K - 1 + RC, :] * w_ref[0:1, :]
    for j in range(1, K):
        s = K - 1 - j
        y = y + scr_ref[s:s + RC, :] * w_ref[j:j + 1, :]
    return y


def _peeled(n_chunks, RC, step, init):
    carry = step(0, init, True)
    if n_chunks > 2:
        carry = lax.fori_loop(1, n_chunks - 1, lambda i, c: step(pl.multiple_of(i * RC, RC), c, False), carry)
    if n_chunks > 1:
        carry = step((n_chunks - 1) * RC, carry, True)
    return carry


def _silu(x):
    return x * jax.nn.sigmoid(x)


def _dsilu(x):
    s = jax.nn.sigmoid(x)
    return s * (1.0 + x * (1.0 - s))


def _col_specs(T, offs):
    return [pl.BlockSpec((T, LANE), functools.partial(lambda o, j: (0, o + j), o)) for o in offs]


def _group_a_fwd(proj, conv_w, CW, out_cols, *, name):
    T = proj.shape[0]
    RC = _tile(T, ROWS_GROUP_A, 8)
    nb = CW // LANE
    K = conv_w.shape[0]

    def body(ax_ref, ab_ref, ac_ref, w_ref, y_ref):
        def step(r0, carry, edge):
            c = None
            for j in range(K):
                lo = -(K - 1 - j)
                t = _win(ac_ref, r0, lo, RC, T, RC, edge) * _win(ax_ref, r0, lo, RC, T, RC, edge) * w_ref[j:j + 1, :]
                c = t if c is None else c + t
            y_ref[pl.ds(r0, RC), :] = (ab_ref[pl.ds(r0, RC), :] * c).astype(y_ref.dtype)
            return carry
        _peeled(T // RC, RC, step, 0)

    return pl.pallas_call(
        body, name=name, grid=(nb,),
        in_specs=_col_specs(T, (0, nb, 2 * nb)) + [pl.BlockSpec((K, LANE), lambda j: (0, j))],
        out_specs=pl.BlockSpec((T, LANE), lambda j: (0, j)),
        out_shape=jax.ShapeDtypeStruct((T, out_cols), BF16), compiler_params=_params(("parallel",)),
    )(proj, proj, proj, conv_w)


def _group_a_bwd(proj, conv_w, dycat, CW, *, name):
    T = proj.shape[0]
    RC = _tile(T, ROWS_GROUP_A, 8)
    nb = CW // LANE
    K = conv_w.shape[0]

    def body(ax_ref, ab_ref, ac_ref, w_ref, dy_ref, dax_ref, dab_ref, dac_ref, dw_ref, scr_ref):
        def step(r0, accs, edge):
            ms = [_win(ac_ref, r0, -(K - 1 - j), RC, T, RC, edge) * _win(ax_ref, r0, -(K - 1 - j), RC, T, RC, edge)
                  for j in range(K)]
            c = ms[0] * w_ref[0:1, :]
            for j in range(1, K):
                c = c + ms[j] * w_ref[j:j + 1, :]
            dy = dy_ref[pl.ds(r0, RC), :]
            dab_ref[pl.ds(r0, RC), :] = (dy * c).astype(dab_ref.dtype)
            dc2 = _win(dy_ref, r0, 0, RC + 8, T, RC, edge) * _win(ab_ref, r0, 0, RC + 8, T, RC, edge)
            dm = _untaps(scr_ref, dc2, w_ref, K, RC)
            dax_ref[pl.ds(r0, RC), :] = (dm * ac_ref[pl.ds(r0, RC), :]).astype(dax_ref.dtype)
            dac_ref[pl.ds(r0, RC), :] = (dm * ax_ref[pl.ds(r0, RC), :]).astype(dac_ref.dtype)
            return tuple(accs[j] + _fold8(dc2[:RC] * ms[j]) for j in range(K))

        accs = _peeled(T // RC, RC, step, tuple(jnp.zeros((8, LANE), F32) for _ in range(K)))
        for j in range(K):
            dw_ref[j:j + 1, :] = jnp.sum(accs[j], axis=0, keepdims=True)

    col = pl.BlockSpec((T, LANE), lambda j: (0, j))
    wsp = pl.BlockSpec((K, LANE), lambda j: (0, j))
    return pl.pallas_call(
        body, name=name, grid=(nb,),
        in_specs=_col_specs(T, (0, nb, 2 * nb)) + [wsp, col],
        out_specs=[col, col, col, wsp],
        out_shape=[jax.ShapeDtypeStruct((T, CW), BF16)] * 3 + [jax.ShapeDtypeStruct((K, CW), F32)],
        scratch_shapes=[pltpu.VMEM((RC + 8, LANE), F32)],
        compiler_params=_params(("parallel",)),
    )(proj, proj, proj, conv_w, dycat)


def _qkv_fwd(proj, conv_w, off, H, *, name):
    T = proj.shape[0]
    RC = _tile(T, ROWS_QKV_FWD, 8)
    nb = 3 * H
    K = conv_w.shape[0]

    def body(x_ref, w_ref, y_ref):
        j = pl.program_id(0)
        is_qk = j < 2 * H
        scale = jnp.where(j < H, HEAD ** -0.5, 1.0).astype(F32)

        def step(r0, carry, edge):
            s = _silu(_taps(x_ref, w_ref, K, r0, RC, T, RC, edge)[1])
            r = lax.rsqrt(jnp.sum(s * s, axis=-1, keepdims=True) + EPS) * scale
            y_ref[pl.ds(r0, RC), :] = s * jnp.where(is_qk, r, 1.0)
            return carry
        _peeled(T // RC, RC, step, 0)

    return pl.pallas_call(
        body, name=name, grid=(nb,),
        in_specs=_col_specs(T, (off,)) + [pl.BlockSpec((K, LANE), lambda j: (0, j))],
        out_specs=pl.BlockSpec((T, LANE), lambda j: (0, j)),
        out_shape=jax.ShapeDtypeStruct((T, nb * LANE), F32), compiler_params=_params(("parallel",)),
    )(proj, conv_w)


def _qkv_bwd(proj, conv_w, dq, dk, dv, off, H, into, *, name):
    T = proj.shape[0]
    RC = _tile(T, ROWS_QKV_BWD, 8)
    nb = 3 * H
    K = conv_w.shape[0]

    def body(x_ref, w_ref, dq_ref, dk_ref, dv_ref, dx_ref, dw_ref, scr_ref):
        j = pl.program_id(0)
        is_qk = j < 2 * H
        scale = jnp.where(j < H, HEAD ** -0.5, 1.0).astype(F32)

        def step(r0, accs, edge):
            xs, c2 = _taps(x_ref, w_ref, K, r0, RC + 8, T, RC, edge)
            s2 = _silu(c2)
            dn2 = jnp.where(j < H, _win(dq_ref, r0, 0, RC + 8, T, RC, edge),
                            jnp.where(is_qk, _win(dk_ref, r0, 0, RC + 8, T, RC, edge),
                                      _win(dv_ref, r0, 0, RC + 8, T, RC, edge)))
            r = lax.rsqrt(jnp.sum(s2 * s2, axis=-1, keepdims=True) + EPS)
            nh = s2 * r
            dnp = dn2 * scale
            ds_qk = r * (dnp - nh * jnp.sum(dnp * nh, axis=-1, keepdims=True))
            ds2 = jnp.where(is_qk, ds_qk, dn2)
            dc2 = ds2 * _dsilu(c2)
            dx_ref[pl.ds(r0, RC), :] = _untaps(scr_ref, dc2, w_ref, K, RC).astype(dx_ref.dtype)
            return tuple(accs[jj] + _fold8(dc2[:RC] * xs[jj][:RC]) for jj in range(K))

        accs = _peeled(T // RC, RC, step, tuple(jnp.zeros((8, LANE), F32) for _ in range(K)))
        for jj in range(K):
            dw_ref[jj:jj + 1, :] = jnp.sum(accs[jj], axis=0, keepdims=True)

    wsp = pl.BlockSpec((K, LANE), lambda j: (0, j))
    return pl.pallas_call(
        lambda x_ref, w_ref, dq_ref, dk_ref, dv_ref, into_ref, dx_ref, dw_ref, scr_ref: body(
            x_ref, w_ref, dq_ref, dk_ref, dv_ref, dx_ref, dw_ref, scr_ref),
        name=name, grid=(nb,),
        in_specs=_col_specs(T, (off,)) + [wsp] + [
            pl.BlockSpec((T, LANE), functools.partial(lambda o, j: (0, jnp.clip(j - o, 0, H - 1)), o)) for o in (0, H, 2 * H)
        ] + [pl.BlockSpec(memory_space=pl.ANY)],
        out_specs=[pl.BlockSpec((T, LANE), lambda j: (0, off + j)), wsp],
        out_shape=[jax.ShapeDtypeStruct(into.shape, into.dtype), jax.ShapeDtypeStruct((K, nb * LANE), F32)],
        scratch_shapes=[pltpu.VMEM((RC + 8, LANE), F32)],
        input_output_aliases={5: 0}, compiler_params=_params(("parallel",)),
    )(proj, conv_w, dq, dk, dv, into)


def _softplus(x):
    return jnp.maximum(x, 0.0) + jnp.log(1.0 + jnp.exp(-jnp.abs(x)))


def _gates_fwd(proj, alog, dtb, off, H, *, name):
    T = proj.shape[0]
    tr = _tile(T, 512, CHUNK)

    def body(ab_ref, al_ref, dt_ref, gam_ref, beta_ref):
        ab = ab_ref[...]
        lane = lax.broadcasted_iota(jnp.int32, ab.shape, 1)
        g = -jnp.exp(al_ref[...]) * _softplus(ab + dt_ref[...])
        gb = jnp.where(lane < H, g, jnp.where(lane < 2 * H, jax.nn.sigmoid(ab), 0.0))
        tril = _tri().astype(F32)
        gam = jnp.concatenate([_mm(tril, gb[c * CHUNK:(c + 1) * CHUNK, :], precision=lax.Precision.HIGHEST)
                               for c in range(tr // CHUNK)], axis=0)
        for h in range(H):
            gam_ref[h] = jnp.broadcast_to(gam[:, h:h + 1], (tr, LANE))
            beta_ref[h] = jnp.broadcast_to(gb[:, H + h:H + h + 1], (tr, LANE))

    vec = pl.BlockSpec((1, LANE), lambda i: (0, 0))
    heads = pl.BlockSpec((H, tr, LANE), lambda i: (0, i, 0))
    return pl.pallas_call(
        body, name=name, grid=(T // tr,),
        in_specs=[pl.BlockSpec((tr, LANE), lambda i: (i, off)), vec, vec],
        out_specs=[heads, heads],
        out_shape=[jax.ShapeDtypeStruct((H, T, LANE), F32)] * 2, compiler_params=_params(("parallel",)),
    )(proj, alog, dtb)


def _gates_bwd(proj, alog, dtb, dgamB, dbB, off, H, *, name):
    T = proj.shape[0]
    tr = _tile(T, 512, CHUNK)

    def body(ab_ref, al_ref, dt_ref, dgam_ref, dbeta_ref, dab_ref, dal_ref, ddt_ref):
        ab = ab_ref[...]
        lane = lax.broadcasted_iota(jnp.int32, ab.shape, 1)
        is_g = lane < H
        d = jnp.zeros_like(ab)
        for h in range(H):
            d = jnp.where(lane == h, dgam_ref[h], jnp.where(lane == H + h, dbeta_ref[h], d))
        triu = _tri(upper=True).astype(F32)
        dg = jnp.concatenate([_mm(triu, d[c * CHUNK:(c + 1) * CHUNK, :], precision=lax.Precision.HIGHEST)
                              for c in range(tr // CHUNK)], axis=0)
        z = ab + dt_ref[...]
        A = -jnp.exp(al_ref[...])
        da = dg * A * jax.nn.sigmoid(z)
        beta = jax.nn.sigmoid(ab)
        db = d * beta * (1.0 - beta)
        dab_ref[...] = jnp.where(is_g, da, jnp.where(lane < 2 * H, db, 0.0)).astype(dab_ref.dtype)

        @pl.when(pl.program_id(0) == 0)
        def _():
            dal_ref[...] = jnp.zeros_like(dal_ref)
            ddt_ref[...] = jnp.zeros_like(ddt_ref)

        dal_ref[...] += jnp.sum(jnp.where(is_g, dg * A * _softplus(z), 0.0), axis=0, keepdims=True)
        ddt_ref[...] += jnp.sum(jnp.where(is_g, da, 0.0), axis=0, keepdims=True)

    vec = pl.BlockSpec((1, LANE), lambda i: (0, 0))
    row = pl.BlockSpec((tr, LANE), lambda i: (i, 0))
    heads = pl.BlockSpec((H, tr, LANE), lambda i: (0, i, 0))
    return pl.pallas_call(
        body, name=name, grid=(T // tr,),
        in_specs=[pl.BlockSpec((tr, LANE), lambda i: (i, off)), vec, vec, heads, heads],
        out_specs=[row, vec, vec],
        out_shape=[jax.ShapeDtypeStruct((T, LANE), BF16), jax.ShapeDtypeStruct((1, LANE), F32),
                   jax.ShapeDtypeStruct((1, LANE), F32)],
        compiler_params=_params(("arbitrary",)),
    )(proj, alog, dtb, dgamB, dbB)


def _gated_norm_fwd(o, proj, gn, zoff, ycat, *, name):
    T, W = o.shape
    tr = _tile(T, 512, 8)
    nh_, zblk = W // LANE, (zoff * LANE) // W
    assert zblk * W == zoff * LANE

    def body(o_ref, z_ref, g_ref, ycat_ref, y_ref):
        for h in range(nh_):
            ln = slice(h * LANE, (h + 1) * LANE)
            ov = o_ref[:, ln]
            r = lax.rsqrt(jnp.mean(ov * ov, axis=-1, keepdims=True) + EPS)
            y_ref[:, ln] = (ov * r * g_ref[...] * _silu(z_ref[:, ln])).astype(y_ref.dtype)

    assert ycat.shape == (T, 2 * W), ycat.shape
    blk = pl.BlockSpec((tr, W), lambda i: (i, 0))
    return pl.pallas_call(
        body, name=name, grid=(T // tr,),
        in_specs=[blk, pl.BlockSpec((tr, W), lambda i: (i, zblk)), pl.BlockSpec((1, LANE), lambda i: (0, 0)),
                  pl.BlockSpec(memory_space=pl.ANY)],
        out_specs=pl.BlockSpec((tr, W), lambda i: (i, 1)), out_shape=jax.ShapeDtypeStruct(ycat.shape, ycat.dtype),
        input_output_aliases={3: 0}, compiler_params=_params(("parallel",)),
    )(o, proj, gn, ycat)


def _gated_norm_bwd(o, proj, gn, dycat, zoff, yoff, *, name):
    T, W = o.shape
    tr = _tile(T, 512, 8)
    nh_, zblk, yblk = W // LANE, (zoff * LANE) // W, (yoff * LANE) // W
    assert zblk * W == zoff * LANE and yblk * W == yoff * LANE

    def body(o_ref, z_ref, g_ref, dy_ref, do_ref, dz_ref, dg_ref):
        @pl.when(pl.program_id(0) == 0)
        def _():
            dg_ref[...] = jnp.zeros_like(dg_ref)

        gv = g_ref[...]
        dg = jnp.zeros_like(gv)
        for h in range(nh_):
            ln = slice(h * LANE, (h + 1) * LANE)
            ov, zv, dy = o_ref[:, ln], z_ref[:, ln], dy_ref[:, ln]
            r = lax.rsqrt(jnp.mean(ov * ov, axis=-1, keepdims=True) + EPS)
            nh = ov * r
            s = _silu(zv)
            dg = dg + jnp.sum(dy * nh * s, axis=0, keepdims=True)
            dz_ref[:, ln] = (dy * nh * gv * _dsilu(zv)).astype(dz_ref.dtype)
            dn = dy * gv * s
            do_ref[:, ln] = r * (dn - nh * jnp.mean(dn * nh, axis=-1, keepdims=True))
        dg_ref[...] += dg

    blk = pl.BlockSpec((tr, W), lambda i: (i, 0))
    vec = pl.BlockSpec((1, LANE), lambda i: (0, 0))
    return pl.pallas_call(
        body, name=name, grid=(T // tr,),
        in_specs=[blk, pl.BlockSpec((tr, W), lambda i: (i, zblk)), vec, pl.BlockSpec((tr, W), lambda i: (i, yblk))],
        out_specs=[blk, blk, vec],
        out_shape=[jax.ShapeDtypeStruct((T, W), F32), jax.ShapeDtypeStruct((T, W), BF16),
                   jax.ShapeDtypeStruct((1, LANE), F32)],
        compiler_params=_params(("arbitrary",)),
    )(o, proj, gn, dycat)


def _ffn_act_fwd(up_pre, conv_w, *, name):
    T, F2 = up_pre.shape[1], up_pre.shape[0] * LANE
    RC = _tile(T, ROWS_FFN_FWD, 8)
    nb = F2 // 2 // LANE
    K = conv_w.shape[0]

    def body(g_ref, v_ref, wg_ref, wv_ref, y_ref):
        def step(r0, carry, edge):
            _, gate = _taps(g_ref, wg_ref, K, r0, RC, T, RC, edge)
            _, val = _taps(v_ref, wv_ref, K, r0, RC, T, RC, edge)
            y_ref[pl.ds(r0, RC), :] = (_silu(gate) * val).astype(y_ref.dtype)
            return carry
        _peeled(T // RC, RC, step, 0)

    return pl.pallas_call(
        body, name=name, grid=(nb,),
        in_specs=[pl.BlockSpec((None, T, LANE), lambda j: (j, 0, 0)), pl.BlockSpec((None, T, LANE), lambda j: (nb + j, 0, 0)),
                  pl.BlockSpec((K, LANE), lambda j: (0, j)), pl.BlockSpec((K, LANE), lambda j: (0, nb + j))],
        out_specs=pl.BlockSpec((T, LANE), lambda j: (0, j)),
        out_shape=jax.ShapeDtypeStruct((T, F2 // 2), BF16), compiler_params=_params(("parallel",)),
    )(up_pre, up_pre, conv_w, conv_w)


def _ffn_act_bwd(up_pre, conv_w, dact, *, name):
    T, F2 = up_pre.shape[1], up_pre.shape[0] * LANE
    RC = _tile(T, ROWS_FFN_BWD, 8)
    nb = F2 // 2 // LANE
    K = conv_w.shape[0]

    def body(g_ref, v_ref, wg_ref, wv_ref, da_ref, d_ref, dwg_ref, dwv_ref, sg_ref, sv_ref):
        def step(r0, accs, edge):
            gs, gate2 = _taps(g_ref, wg_ref, K, r0, RC + 8, T, RC, edge)
            vs, val2 = _taps(v_ref, wv_ref, K, r0, RC + 8, T, RC, edge)
            da2 = _win(da_ref, r0, 0, RC + 8, T, RC, edge)
            dgate2 = da2 * val2 * _dsilu(gate2)
            dval2 = da2 * _silu(gate2)
            d_ref[0, pl.ds(r0, RC), :] = _untaps(sg_ref, dgate2, wg_ref, K, RC).astype(d_ref.dtype)
            d_ref[1, pl.ds(r0, RC), :] = _untaps(sv_ref, dval2, wv_ref, K, RC).astype(d_ref.dtype)
            new = []
            for j in range(K):
                new.append(accs[2 * j] + _fold8(dgate2[:RC] * gs[j][:RC]))
                new.append(accs[2 * j + 1] + _fold8(dval2[:RC] * vs[j][:RC]))
            return tuple(new)

        accs = _peeled(T // RC, RC, step, tuple(jnp.zeros((8, LANE), F32) for _ in range(2 * K)))
        for j in range(K):
            dwg_ref[j:j + 1, :] = jnp.sum(accs[2 * j], axis=0, keepdims=True)
            dwv_ref[j:j + 1, :] = jnp.sum(accs[2 * j + 1], axis=0, keepdims=True)

    col = pl.BlockSpec((T, LANE), lambda j: (0, j))
    wsp = pl.BlockSpec((K, LANE), lambda j: (0, j))
    return pl.pallas_call(
        body, name=name, grid=(nb,),
        in_specs=[pl.BlockSpec((None, T, LANE), lambda j: (j, 0, 0)), pl.BlockSpec((None, T, LANE), lambda j: (nb + j, 0, 0)),
                  wsp, pl.BlockSpec((K, LANE), lambda j: (0, nb + j)), col],
        out_specs=[pl.BlockSpec((2, T, LANE), lambda j: (0, 0, j)), wsp, wsp],
        out_shape=[jax.ShapeDtypeStruct((2, T, F2 // 2), BF16)] + [jax.ShapeDtypeStruct((K, F2 // 2), F32)] * 2,
        scratch_shapes=[pltpu.VMEM((RC + 8, LANE), F32)] * 2,
        compiler_params=_params(("parallel",)),
    )(up_pre, up_pre, conv_w, conv_w, dact)


CPB = 8
CPB_SCAN = 4
GRP = 8
HP = lax.Precision.HIGH


def _tri(strict=False, upper=False):
    r = lax.broadcasted_iota(jnp.int32, (CHUNK, CHUNK), 0)
    c = lax.broadcasted_iota(jnp.int32, (CHUNK, CHUNK), 1)
    if upper:
        return c >= r
    return (r > c) if strict else (r >= c)


def _mm(a, b, dn="nn", precision=None):
    precision = HP if precision is None else precision
    return lax.dot_general(a, b, _DN[dn], precision=precision, preferred_element_type=F32)


def _mm16(a, b, dn="nn"):
    return lax.dot_general(a.astype(BF16), b.astype(BF16), _DN[dn], preferred_element_type=F32)


def _each(f, *cols):
    return [f(*xs) for xs in zip(*cols)]


def _decay(gam):
    return jnp.exp(jnp.where(_tri(), gam[:, :CHUNK] - gam.T[:CHUNK, :], -1e30))


def _delta_specs(T, H, cpb):
    rows = cpb * CHUNK
    col = lambda o: pl.BlockSpec((rows, LANE), functools.partial(lambda o, h, n: (n, o + h), o))
    bc = pl.BlockSpec((1, rows, LANE), lambda h, n: (h, n, 0))
    sq = pl.BlockSpec((1, cpb, CHUNK, CHUNK), lambda h, n: (h, n, 0, 0))
    vec = pl.BlockSpec((1, cpb, 1, LANE), lambda h, n: (h, n, 0, 0))
    return col, bc, sq, vec


def _delta_prep_fwd(qkv, gamB, bB, H, *, name):
    T = qkv.shape[0]
    N = T // CHUNK
    cpb = _tile(N, CPB, 8)
    grp = min(GRP, cpb)
    col, bc, sq, vec = _delta_specs(T, H, cpb)

    def body(q_ref, k_ref, v_ref, g_ref, b_ref, u_ref, w_ref, qd_ref, kd_ref, qk_ref, ti_ref, gl_ref):
        eye = (lax.broadcasted_iota(jnp.int32, (CHUNK, CHUNK), 0) == lax.broadcasted_iota(jnp.int32, (CHUNK, CHUNK), 1)).astype(F32)
        strict = _tri(strict=True)
        for c0 in range(0, cpb, grp):
            cs = list(range(c0, c0 + grp))
            rows = [slice(c * CHUNK, (c + 1) * CHUNK) for c in cs]
            q, k, v = ([r_[r, :] for r in rows] for r_ in (q_ref, k_ref, v_ref))
            bb = [b_ref[0, r, :] for r in rows]
            gam = [g_ref[0, r, :] for r in rows]
            D = _each(_decay, gam)
            e = _each(jnp.exp, gam)
            kk = _each(lambda k_: _mm16(k_, k_, "nt"), k)
            X = _each(lambda kk_, D_, b_: -(jnp.where(strict, kk_ * D_, 0.0) * b_[:, :CHUNK]), kk, D, bb)
            R = _each(lambda x: eye + x, X)
            for _ in range(5):
                X = _each(lambda x: _mm(x, x), X)
                R = _each(lambda r, x: r + _mm(r, x), R, X)
            u = _each(lambda r, b_, v_: _mm(r, b_ * v_), R, bb, v)
            w = _each(lambda r, b_, e_, k_: _mm(r, b_ * e_ * k_), R, bb, e, k)
            qk = _each(lambda q_, k_, D_: _mm16(q_, k_, "nt") * D_, q, k, D)
            for i, c in enumerate(cs):
                glast = gam[i][CHUNK - 1:CHUNK, :]
                u_ref[rows[i], :] = u[i]
                w_ref[rows[i], :] = w[i]
                qd_ref[rows[i], :] = e[i] * q[i]
                kd_ref[rows[i], :] = jnp.exp(glast - gam[i]) * k[i]
                qk_ref[0, c] = qk[i]
                ti_ref[0, c] = R[i]
                gl_ref[0, c] = jnp.exp(glast)

    full = jax.ShapeDtypeStruct((T, H * LANE), F32)
    sqs = jax.ShapeDtypeStruct((H, N, CHUNK, CHUNK), F32)
    return pl.pallas_call(
        body, name=name, grid=(H, N // cpb),
        in_specs=[col(0), col(H), col(2 * H), bc, bc],
        out_specs=[col(0)] * 4 + [sq, sq, vec],
        out_shape=[full] * 4 + [sqs, sqs, jax.ShapeDtypeStruct((H, N, 1, LANE), F32)],
        compiler_params=_params(("parallel", "parallel")),
    )(qkv, qkv, qkv, gamB, bB)


def _scan_specs(H, N, cpb, hb, rev):
    nbk = N // cpb
    blk = (lambda n: nbk - 1 - n) if rev else (lambda n: n)
    col = pl.BlockSpec((cpb * CHUNK, hb * LANE), lambda h, n: (blk(n), h))
    sq = pl.BlockSpec((hb, cpb, CHUNK, CHUNK), lambda h, n: (h, blk(n), 0, 0))
    vec = pl.BlockSpec((hb, cpb, 1, LANE), lambda h, n: (h, blk(n), 0, 0))
    st = pl.BlockSpec((hb, cpb, HEAD, HEAD), lambda h, n: (h, blk(n), 0, 0))
    return col, sq, vec, st


def _delta_scan_fwd(u, w, qd, kd, qk, gl, H, *, name):
    T = u.shape[0]
    N = T // CHUNK
    cpb = _tile(N, CPB_SCAN, 4)
    hb = min(GRP, H)
    col, sq, vec, st = _scan_specs(H, N, cpb, hb, False)
    lanes = [slice(j * LANE, (j + 1) * LANE) for j in range(hb)]
    heads = list(range(hb))

    def body(u_ref, w_ref, qd_ref, kd_ref, qk_ref, gl_ref, o_ref, vn_ref, ss_ref, s_scr):
        @pl.when(pl.program_id(1) == 0)
        def _():
            s_scr[...] = jnp.zeros_like(s_scr)

        def step(c, states):
            rows = pl.ds(pl.multiple_of(c * CHUNK, CHUNK), CHUNK)
            S = list(states)
            for j in heads:
                ss_ref[j, c] = S[j]
            wS = _each(lambda ln, s: _mm16(w_ref[rows, ln], s), lanes, S)
            qS = _each(lambda ln, s: _mm16(qd_ref[rows, ln], s), lanes, S)
            vn = _each(lambda ln, ws: u_ref[rows, ln] - ws, lanes, wS)
            o = _each(lambda j, qs, vn_: qs + _mm16(qk_ref[j, c], vn_), heads, qS, vn)
            new = _each(lambda j, ln, s, vn_: s * gl_ref[j, c] + _mm16(kd_ref[rows, ln], vn_, "tn"),
                        heads, lanes, S, vn)
            for j in heads:
                o_ref[rows, lanes[j]] = o[j]
                vn_ref[rows, lanes[j]] = vn[j]
            return tuple(new)
        out = lax.fori_loop(0, cpb, step, tuple(s_scr[j] for j in heads))
        for j in heads:
            s_scr[j] = out[j]

    full = jax.ShapeDtypeStruct((T, H * LANE), F32)
    return pl.pallas_call(
        body, name=name, grid=(H // hb, N // cpb),
        in_specs=[col] * 4 + [sq, vec],
        out_specs=[col, col, st],
        out_shape=[full, full, jax.ShapeDtypeStruct((H, N, HEAD, HEAD), F32)],
        scratch_shapes=[pltpu.VMEM((hb, HEAD, HEAD), F32)],
        compiler_params=_params(("parallel", "arbitrary")),
    )(u, w, qd, kd, qk, gl)


def _delta_scan_bwd(do, w, qd, kd, vn, qk, gl, ss, H, *, name):
    T = do.shape[0]
    N = T // CHUNK
    cpb = _tile(N, CPB_SCAN, 4)
    hb = min(GRP, H)
    col, sq, vec, st = _scan_specs(H, N, cpb, hb, True)
    lanes = [slice(j * LANE, (j + 1) * LANE) for j in range(hb)]
    heads = list(range(hb))

    def body(do_ref, w_ref, qd_ref, kd_ref, vn_ref, qk_ref, gl_ref, ss_ref,
             du_ref, dw_ref, dqd_ref, dkd_ref, dqk_ref, dgl_ref, ds_scr):
        @pl.when(pl.program_id(1) == 0)
        def _():
            ds_scr[...] = jnp.zeros_like(ds_scr)

        def step(i, dstates):
            c = cpb - 1 - i
            rows = pl.ds(pl.multiple_of(c * CHUNK, CHUNK), CHUNK)
            dS = list(dstates)
            S = [ss_ref[j, c] for j in heads]
            dov = [do_ref[rows, ln] for ln in lanes]
            vnv = [vn_ref[rows, ln] for ln in lanes]
            a1 = _each(lambda j, d_: _mm16(qk_ref[j, c], d_, "tn"), heads, dov)
            a2 = _each(lambda ln, ds: _mm16(kd_ref[rows, ln], ds), lanes, dS)
            dvn = _each(lambda x, y: x + y, a1, a2)
            dqd = _each(lambda d_, s: _mm16(d_, s, "nt"), dov, S)
            dkd = _each(lambda v_, ds: _mm16(v_, ds, "nt"), vnv, dS)
            dqk = _each(lambda d_, v_: _mm16(d_, v_, "nt"), dov, vnv)
            dw = _each(lambda dv_, s: -_mm16(dv_, s, "nt"), dvn, S)
            b1 = _each(lambda ln, d_: _mm16(qd_ref[rows, ln], d_, "tn"), lanes, dov)
            b2 = _each(lambda ln, dv_: _mm16(w_ref[rows, ln], dv_, "tn"), lanes, dvn)
            new = _each(lambda j, x, y, ds: x + ds * gl_ref[j, c] - y, heads, b1, b2, dS)
            for j in heads:
                du_ref[rows, lanes[j]] = dvn[j]
                dw_ref[rows, lanes[j]] = dw[j]
                dqd_ref[rows, lanes[j]] = dqd[j]
                dkd_ref[rows, lanes[j]] = dkd[j]
                dqk_ref[j, c] = dqk[j]
                dgl = jnp.sum(jnp.sum(dS[j] * S[j], axis=1, keepdims=True), axis=0, keepdims=True)
                dgl_ref[j, c] = jnp.broadcast_to(dgl, (1, LANE))
            return tuple(new)
        out = lax.fori_loop(0, cpb, step, tuple(ds_scr[j] for j in heads))
        for j in heads:
            ds_scr[j] = out[j]

    full = jax.ShapeDtypeStruct((T, H * LANE), F32)
    return pl.pallas_call(
        body, name=name, grid=(H // hb, N // cpb),
        in_specs=[col] * 5 + [sq, vec, st],
        out_specs=[col] * 4 + [sq, vec],
        out_shape=[full] * 4 + [jax.ShapeDtypeStruct((H, N, CHUNK, CHUNK), F32), jax.ShapeDtypeStruct((H, N, 1, LANE), F32)],
        scratch_shapes=[pltpu.VMEM((hb, HEAD, HEAD), F32)],
        compiler_params=_params(("parallel", "arbitrary")),
    )(do, w, qd, kd, vn, qk, gl, ss)


def _delta_prep_bwd(qkv, gamB, bB, ti, u, w, qk, du, dw, dqd, dkd, dqk, dgl, H, *, name):
    T = qkv.shape[0]
    N = T // CHUNK
    cpb = _tile(N, CPB, 8)
    grp = min(GRP, cpb)
    col, bc, sq, vec = _delta_specs(T, H, cpb)

    def body(q_ref, k_ref, v_ref, g_ref, b_ref, ti_ref, u_ref, w_ref, qk_ref,
             du_ref, dw_ref, dqd_ref, dkd_ref, dqk_ref, dgl_ref,
             dq_ref, dk_ref, dv_ref, dg_ref, db_ref):
        ones = jnp.ones((CHUNK, LANE), F32)
        strict = _tri(strict=True)
        last = lax.broadcasted_iota(jnp.int32, (CHUNK, LANE), 0) == CHUNK - 1
        lsum = lambda x: jnp.sum(x, axis=-1, keepdims=True)
        for c0 in range(0, cpb, grp):
            cs = list(range(c0, c0 + grp))
            rows = [slice(c * CHUNK, (c + 1) * CHUNK) for c in cs]
            ld = lambda r_: [r_[r, :] for r in rows]
            q, k, v, uv, wv, duv, dwv, dqd_v, dkd_v = (ld(r_) for r_ in (q_ref, k_ref, v_ref, u_ref, w_ref, du_ref, dw_ref, dqd_ref, dkd_ref))
            bb = [b_ref[0, r, :] for r in rows]
            gam = [g_ref[0, r, :] for r in rows]
            Ti = [ti_ref[0, c] for c in cs]
            QK = [qk_ref[0, c] for c in cs]
            dqk_v = [dqk_ref[0, c] for c in cs]
            D = _each(_decay, gam)
            e = _each(jnp.exp, gam)
            glast = [g_[CHUNK - 1:CHUNK, :] for g_ in gam]
            eL = _each(lambda gl_, g_: jnp.exp(gl_ - g_), glast, gam)
            kk = _each(lambda k_: _mm16(k_, k_, "nt"), k)
            KKD = _each(lambda kk_, D_: jnp.where(strict, kk_ * D_, 0.0), kk, D)
            dru = _each(lambda t, d_: _mm(t, d_, "tn"), Ti, duv)
            drw = _each(lambda t, d_: _mm(t, d_, "tn"), Ti, dwv)
            l1 = _each(lambda a, b: _mm(a, b, "nt"), dru, uv)
            l2 = _each(lambda a, b: _mm(a, b, "nt"), drw, wv)
            dL = _each(lambda a, b: jnp.where(strict, -(a + b), 0.0), l1, l2)
            Mm = _each(lambda dl, b_: dl * b_[:, :CHUNK], dL, bb)
            dKK = _each(lambda m_, D_: m_ * D_, Mm, D)
            dQK = _each(lambda a, D_: a * D_, dqk_v, D)
            P = _each(lambda m_, kkd, a, qk_: m_ * kkd + a * qk_, Mm, KKD, dqk_v, QK)
            q1 = _each(lambda a, k_: _mm16(a, k_), dQK, k)
            k1 = _each(lambda a, q_: _mm16(a, q_, "tn"), dQK, q)
            k2 = _each(lambda a, k_: _mm16(a, k_), dKK, k)
            k3 = _each(lambda a, k_: _mm16(a, k_, "tn"), dKK, k)
            s1 = _each(lambda dl, kkd: _mm(dl * kkd, ones), dL, KKD)
            p1 = _each(lambda p_: _mm(p_, ones), P)
            p2 = _each(lambda p_: _mm(p_, ones, "tn"), P)
            for i, c in enumerate(cs):
                r = rows[i]
                bek = bb[i] * e[i]
                kdv = eL[i] * k[i]
                dq_ref[r, :] = q1[i] + e[i] * dqd_v[i]
                dk_ref[r, :] = k1[i] + k2[i] + k3[i] + bek * drw[i] + eL[i] * dkd_v[i]
                dv_ref[r, :] = bb[i] * dru[i]
                db_ref[0, r, :] = s1[i] + lsum(dru[i] * v[i]) + lsum(drw[i] * e[i] * k[i])
                dgam = (p1[i] - p2[i] + lsum(drw[i] * bek * k[i]) + lsum(dqd_v[i] * e[i] * q[i])
                        - lsum(dkd_v[i] * kdv))
                xlast = jnp.sum(lsum(dkd_v[i] * kdv), axis=0, keepdims=True) + jnp.exp(glast[i]) * dgl_ref[0, c]
                dg_ref[0, r, :] = dgam + jnp.where(last, xlast, 0.0)

    full = jax.ShapeDtypeStruct((T, H * LANE), F32)
    bcs = jax.ShapeDtypeStruct((H, T, LANE), F32)
    return pl.pallas_call(
        body, name=name, grid=(H, N // cpb),
        in_specs=[col(0), col(H), col(2 * H), bc, bc, sq, col(0), col(0), sq, col(0), col(0), col(0), col(0), sq, vec],
        out_specs=[col(0), col(0), col(0), bc, bc],
        out_shape=[full, full, full, bcs, bcs],
        compiler_params=_params(("parallel", "parallel")),
    )(qkv, qkv, qkv, gamB, bB, ti, u, w, qk, du, dw, dqd, dkd, dqk, dgl)


def _adam(parts, w, m, v, *, name, own=None, me=None):
    P, R, C = parts.shape
    if R > 256 and R % 8:
        tr, tc = R, _tile(C, 256)
    else:
        tr, tc = _tile(R, 256, 8), C
    n_own = 0 if own is None else 2

    def body(*refs):
        p_ref, w_ref, m_ref, v_ref, g_ref, d_ref, nm_ref, nv_ref = refs[n_own:]
        g = None
        for i in range(P):
            t = p_ref[i].astype(F32)
            if n_own:
                t = jnp.where(refs[0][0] == i, refs[1][...].astype(F32), t)
            g = t if g is None else g + t
        mn = ADAM_B1 * m_ref[...] + (1.0 - ADAM_B1) * g
        vn = ADAM_B2 * v_ref[...] + (1.0 - ADAM_B2) * (g * g)
        m_hat = mn / (1.0 - ADAM_B1 ** ADAM_STEP)
        v_hat = vn / (1.0 - ADAM_B2 ** ADAM_STEP)
        g_ref[...] = g
        d_ref[...] = -ADAM_LR * (m_hat / (jnp.sqrt(v_hat) + ADAM_EPS) + ADAM_WD * w_ref[...])
        nm_ref[...] = mn
        nv_ref[...] = vn

    blk = pl.BlockSpec((tr, tc), lambda i, j: (i, j))
    return pl.pallas_call(
        body, name=name, grid=(R // tr, C // tc),
        in_specs=[pl.BlockSpec(memory_space=pltpu.SMEM), blk][:n_own] + [pl.BlockSpec((P, tr, tc), lambda i, j: (0, i, j)), blk, blk, blk],
        out_specs=[blk] * 4, out_shape=[jax.ShapeDtypeStruct((R, C), F32)] * 4,
        compiler_params=_params(("parallel", "parallel")),
    )(*([me, own] if n_own else []), parts, w, m, v)


def _mesh_pos():
    return lax.axis_index("x"), lax.axis_index("y"), lax.axis_index("c")


def _peer(k):
    x, y, c = _mesh_pos()
    px, py, pc = x ^ ((k >> 2) & 1), y ^ ((k >> 1) & 1), c ^ (k & 1)
    return (px, py, pc), 4 * px + 2 * py + pc


def _exchange(arrays, scatter, *, name, after=None):
    n = len(arrays)
    n_in = n if after is None else n + 1
    blocks = [a.shape[1:] if scatter else a.shape for a in arrays]

    def body(*refs):
        srcs, dsts = refs[:n], refs[n_in:n_in + n]
        send_sems, recv_sems, local_sems = refs[n_in + n:]
        x, y, c = _mesh_pos()
        me = 4 * x + 2 * y + c
        local, sends = [], []
        for a in range(n):
            cp = pltpu.make_async_copy(srcs[a].at[me] if scatter else srcs[a], dsts[a].at[me], local_sems.at[a])
            cp.start()
            local.append(cp)
            for k in range(1, N_DEV):
                dev, idx = _peer(k)
                cp = pltpu.make_async_remote_copy(
                    src_ref=srcs[a].at[idx] if scatter else srcs[a], dst_ref=dsts[a].at[me],
                    send_sem=send_sems.at[a * N_DEV + k], recv_sem=recv_sems.at[a * N_DEV + k],
                    device_id=dev, device_id_type=MESH)
                cp.start()
                sends.append(cp)
        for a in range(n):
            for k in range(1, N_DEV):
                dev, idx = _peer(k)
                pltpu.make_async_remote_copy(
                    src_ref=srcs[a].at[idx] if scatter else srcs[a], dst_ref=dsts[a].at[idx],
                    send_sem=send_sems.at[a * N_DEV + k], recv_sem=recv_sems.at[a * N_DEV + k],
                    device_id=dev, device_id_type=MESH).wait_recv()
        for cp in sends:
            cp.wait_send()
        for cp in local:
            cp.wait()

    anyspec = pl.BlockSpec(memory_space=pl.ANY)
    return pl.pallas_call(
        body, name=name, in_specs=[anyspec] * n_in, out_specs=[anyspec] * n,
        out_shape=[jax.ShapeDtypeStruct((N_DEV,) + tuple(b), a.dtype) for a, b in zip(arrays, blocks)],
        scratch_shapes=[pltpu.SemaphoreType.DMA((n * N_DEV,)), pltpu.SemaphoreType.DMA((n * N_DEV,)),
                        pltpu.SemaphoreType.DMA((n,))],
    )(*arrays, *([] if after is None else [after]))


_ANY = pl.BlockSpec(memory_space=pl.ANY)
_SEM = pl.BlockSpec(memory_space=pltpu.SEMAPHORE)
_EFFECT = pltpu.SideEffectType.DATAFLOW_SIDE_EFFECTING


def _in_hbm(a):
    return pltpu.with_memory_space_constraint(a, pltpu.HBM)


def _split_copy(src, land, send, recv, k, me, scatter, landed):
    dev, idx = _peer(k)
    return pltpu.make_async_remote_copy(
        src_ref=src.at[idx] if scatter else src, dst_ref=land.at[idx if landed else me],
        send_sem=send.at[k], recv_sem=recv.at[k], device_id=dev, device_id_type=MESH)


ALL_PEERS = tuple(range(1, N_DEV))
SIBLING = 1
SAME_CORE = (2, 4, 6)


def _split_start(srcs, lands, scatter, *, name, relations=None):
    n = len(srcs)
    relations = relations or [ALL_PEERS] * n

    def body(*refs):
        src, land, send, recv, token = refs[:n], refs[n:2 * n], refs[2 * n:3 * n], refs[3 * n:4 * n], refs[-1]
        x, y, c = _mesh_pos()
        me = 4 * x + 2 * y + c
        for a in range(n):
            for k in relations[a]:
                _split_copy(src[a], land[a], send[a], recv[a], k, me, scatter, False).start()
        token[...] = jnp.zeros_like(token)

    outs = pl.pallas_call(
        body, name=name,
        out_shape=[pltpu.SemaphoreType.DMA((N_DEV,))] * (2 * n) + [pltpu.HBM(t.shape, t.dtype) for t in list(srcs) + list(lands)]
        + [jax.ShapeDtypeStruct((8, LANE), F32)],
        in_specs=[_ANY] * (2 * n), out_specs=[_SEM] * (2 * n) + [_ANY] * (2 * n) + [pl.BlockSpec(memory_space=pltpu.VMEM)],
        input_output_aliases={i: 2 * n + i for i in range(2 * n)},
        compiler_params=pltpu.CompilerParams(has_side_effects=_EFFECT),
    )(*[_in_hbm(t) for t in list(srcs) + list(lands)])
    handles = [(outs[a], outs[n + a], outs[2 * n + a], outs[3 * n + a]) for a in range(n)]
    return handles, outs[-1]


def _split_wait(handle, after, scatter, *, name):
    send, recv, src_thru, land_thru = handle

    def body(src_ref, land_ref, send_ref, recv_ref, after_ref, src_out, land_out):
        x, y, c = _mesh_pos()
        me = 4 * x + 2 * y + c
        for k in range(1, N_DEV):
            cp = _split_copy(src_ref, land_ref, send_ref, recv_ref, k, me, scatter, True)
            cp.wait_send()
            cp.wait_recv()

    return pl.pallas_call(
        body, name=name,
        out_shape=(pltpu.HBM(src_thru.shape, src_thru.dtype), pltpu.HBM(land_thru.shape, land_thru.dtype)),
        in_specs=(_ANY, _ANY, _SEM, _SEM, _ANY), out_specs=(_ANY, _ANY), input_output_aliases={0: 0, 1: 1},
        compiler_params=pltpu.CompilerParams(has_side_effects=_EFFECT),
    )(src_thru, land_thru, send, recv, after)[1]


def _forward_copy(land, fsend, frecv, k, landed):
    x, y, c = _mesh_pos()
    _, idx = _peer(k | SIBLING if landed else k)
    return pltpu.make_async_remote_copy(src_ref=land.at[idx], dst_ref=land.at[idx], send_sem=fsend.at[k],
                                        recv_sem=frecv.at[k], device_id=(x, y, 1 - c), device_id_type=MESH)


def _gather_forward(handle, after, *, name):
    send, recv, src_thru, land_thru = handle

    def body(src_ref, land_ref, send_ref, recv_ref, after_ref, src_out, land_out, fsend, frecv):
        x, y, c = _mesh_pos()
        me = 4 * x + 2 * y + c
        for k in SAME_CORE:
            _split_copy(src_ref, land_ref, send_ref, recv_ref, k, me, False, True).wait_recv()
            _forward_copy(land_ref, fsend, frecv, k, False).start()

    src2, land2, fsend, frecv = pl.pallas_call(
        body, name=name,
        out_shape=(pltpu.HBM(src_thru.shape, src_thru.dtype), pltpu.HBM(land_thru.shape, land_thru.dtype),
                   pltpu.SemaphoreType.DMA((N_DEV,)), pltpu.SemaphoreType.DMA((N_DEV,))),
        in_specs=(_ANY, _ANY, _SEM, _SEM, _ANY), out_specs=(_ANY, _ANY, _SEM, _SEM), input_output_aliases={0: 0, 1: 1},
        compiler_params=pltpu.CompilerParams(has_side_effects=_EFFECT),
    )(src_thru, land_thru, send, recv, after)
    return (send, recv, src2, land2), (fsend, frecv)


def _gather_wait_two_level(handle, fwd, *, name):
    send, recv, src_thru, land_thru = handle
    fsend, frecv = fwd

    def body(src_ref, land_ref, send_ref, recv_ref, fsend_ref, frecv_ref, src_out, land_out):
        x, y, c = _mesh_pos()
        me = 4 * x + 2 * y + c
        for k in (SIBLING,) + SAME_CORE:
            _split_copy(src_ref, land_ref, send_ref, recv_ref, k, me, False, True).wait_send()
        _split_copy(src_ref, land_ref, send_ref, recv_ref, SIBLING, me, False, True).wait_recv()
        for k in SAME_CORE:
            _forward_copy(land_ref, fsend_ref, frecv_ref, k, False).wait_send()
            _forward_copy(land_ref, fsend_ref, frecv_ref, k, True).wait_recv()

    return pl.pallas_call(
        body, name=name,
        out_shape=(pltpu.HBM(src_thru.shape, src_thru.dtype), pltpu.HBM(land_thru.shape, land_thru.dtype)),
        in_specs=(_ANY, _ANY, _SEM, _SEM, _SEM, _SEM), out_specs=(_ANY, _ANY), input_output_aliases={0: 0, 1: 1},
        compiler_params=pltpu.CompilerParams(has_side_effects=_EFFECT),
    )(src_thru, land_thru, send, recv, fsend, frecv)[1]


def _local_step(x, p, tgt, S, wt, conv, emit):
    T, D = x.shape
    CW = DNW = D // 2
    H = DNW // HEAD
    nA, nD = CW // LANE, DNW // LANE
    qkv_off, z_off, ab_off = 3 * nA, 3 * nA + 3 * nD, 3 * nA + 4 * nD
    alog = jnp.pad(S["a_log"], ((0, 0), (0, LANE - H)))
    dtb = jnp.pad(S["dt_bias"], ((0, 0), (0, LANE - H)))

    h1 = _rms_fwd(x, S["g_mix"], name="rms1_fwd")
    pp = _matmul(p, wt("w_pp", h1), "nn", name="mm_pp", b_shards=True)
    w_in, cv = wt("w_in", pp), conv(pp)
    proj = _matmul(h1, w_in, "nt", name="mm_in", tn=1536)
    y_a = _group_a_fwd(proj, cv["conv_a"], CW, D, name="group_a_fwd")
    qkv = _qkv_fwd(proj, cv["conv_qkv"], qkv_off, H, name="qkv_fwd")
    gamB, bB = _gates_fwd(proj, alog, dtb, ab_off, H, name="gates_fwd")
    u, w, qd, kd, qk, ti, gl = _delta_prep_fwd(qkv, gamB, bB, H, name="delta_prep_fwd")
    o, vn, ss = _delta_scan_fwd(u, w, qd, kd, qk, gl, H, name="delta_scan_fwd")
    ycat = _gated_norm_fwd(o, proj, S["dn_g"], z_off, y_a, name="gated_norm_fwd")
    w_out = wt("w_out", ycat)
    rows = dict(tm=ROW_TILE, tn=D)
    x1, h2 = _matmul(ycat, w_out, "nn", name="mm_out", out_dtypes=(F32, BF16), epilogue=_epi_residual_rms,
                     extras=(x,), vec_extras=(S["g_ffn"],), **rows)
    w_up = wt("w_up", h2)
    up_pre = _matmul(h2, w_up, "nn", name="mm_up", b_shards=True, tn=SHARD_TILE, out_lanes=True)
    act = _ffn_act_fwd(up_pre, cv["conv_ffn"], name="ffn_act_fwd")
    w_down = wt("w_down", act)
    x2 = _matmul(act, w_down, "nn", name="mm_down", epilogue=lambda acc, r: (acc + r,), extras=(x1,), tk=LONG_K)
    h3 = _rms_fwd(x2, S["g_ple"], name="rms3_fwd")
    w_pg = wt("w_pg", h3)

    def ple_epi(acc, x2r, ppr):
        s = jax.nn.sigmoid(acc)
        return x2r + s * ppr, s

    x3, sg = _matmul(h3, w_pg, "nn", name="mm_pg", out_dtypes=(F32, F32), epilogue=ple_epi, extras=(x2, pp), tm=256, tn=D)
    dx3, dg_final, loss, dpg, dpp = _final_loss(x3, S["g_final"], tgt, pp, sg, name="final_loss")

    G = {"g_final": dg_final}
    tok = emit({"w_pp": _matmul(p, dpp, "tn", name="mm_dwpp", out_dtypes=(BF16,), out_shards=True, tk=LONG_K),
                "w_pg": _matmul(h3, dpg, "tn", name="mm_dwpg", out_dtypes=(BF16,), tk=LONG_K)})
    bwd = dict(out_dtypes=(F32, BF16), epilogue=_epi_rms_bwd(2), n_vec=1, **rows)
    dx2, dx2b, G["g_ple"] = _matmul(dpg, w_pg, "nt", name="mm_dh3", after=tok, extras=(x2, dx3),
                                    vec_extras=(S["g_ple"],), **bwd)
    tok = emit({"w_down": _matmul(act, dx2b, "tn", name="mm_dwdown", out_dtypes=(BF16,), tk=LONG_K)})
    dact = _matmul(dx2b, w_down, "nt", name="mm_dact", after=tok, tn=SHARD_TILE)
    dup, dcf_g, dcf_v = _ffn_act_bwd(up_pre, cv["conv_ffn"], dact, name="ffn_act_bwd")
    G["conv_ffn"] = jnp.concatenate([dcf_g, dcf_v], axis=1)
    tok = emit({"w_up": _matmul(h2, dup, "tn", name="mm_dwup", out_dtypes=(BF16,), b_shards=True, out_shards=True,
                                tn=SHARD_TILE, tk=LONG_K)})
    dh2 = _matmul(dup, w_up, "nt", name="mm_dh2", after=tok, a_shards=True, b_shards=True, tk=2 * SHARD_TILE)
    dx1, dx1b, G["g_ffn"] = _rms_bwd(x1, S["g_ffn"], dh2, dx2, name="rms2_bwd")
    tok = emit({"w_out": _matmul(ycat, dx1b, "tn", name="mm_dwout", out_dtypes=(BF16,), tk=LONG_K)})
    dycat = _matmul(dx1b, w_out, "nt", name="mm_dycat", after=tok)
    do, dz, G["dn_g"] = _gated_norm_bwd(o, proj, S["dn_g"], dycat, z_off, nA, name="gated_norm_bwd")
    du, dw, dqd, dkd, dqk, dgl = _delta_scan_bwd(do, w, qd, kd, vn, qk, gl, ss, H, name="delta_scan_bwd")
    dq, dk, dv, dgB, dbB = _delta_prep_bwd(qkv, gamB, bB, ti, u, w, qk, du, dw, dqd, dkd, dqk, dgl, H,
                                           name="delta_prep_bwd")
    dab, dal, ddt = _gates_bwd(proj, alog, dtb, dgB, dbB, ab_off, H, name="gates_bwd")
    G["a_log"], G["dt_bias"] = dal[:, :H], ddt[:, :H]
    dax, dab_, dac, G["conv_a"] = _group_a_bwd(proj, cv["conv_a"], dycat, CW, name="group_a_bwd")
    in_p = w_in.shape[0]
    dproj = lax.empty((T, in_p), BF16)
    pieces_at = [(dax, 0), (dab_, CW), (dac, 2 * CW), (dz, z_off * LANE), (dab, ab_off * LANE)]
    if in_p > (ab_off + 1) * LANE:
        pieces_at.append((jnp.zeros((T, in_p - (ab_off + 1) * LANE), BF16), (ab_off + 1) * LANE))
    for piece, c0 in pieces_at:
        dproj = lax.dynamic_update_slice(dproj, piece, (0, c0))
    dproj, G["conv_qkv"] = _qkv_bwd(proj, cv["conv_qkv"], dq, dk, dv, qkv_off, H, dproj, name="qkv_bwd")
    tok = emit({"w_in": _matmul(dproj, h1, "tn", name="mm_dwin", out_dtypes=(BF16,), tk=LONG_K)})
    dh1 = _matmul(dproj, w_in, "nn", name="mm_dh1", after=tok, tk=LONG_K)
    grad_x, _, G["g_mix"] = _rms_bwd(x, S["g_mix"], dh1, dx1, name="rms1_bwd")
    return loss, grad_x, G


def _col_sharded(landed):
    _, R, C = landed.shape
    return jnp.transpose(landed, (1, 0, 2)).reshape(R, N_DEV * C)


def kernel(x, p, norm_mix_g, w_in, conv_a_w, conv_qkv_w, a_log, dt_bias, dn_norm_g, w_out, norm_ffn_g, w_up, conv_ffn_w, w_down, norm_ple_g, w_ple_gate, w_ple_proj, final_norm_g, loss_target, m_norm_mix_g, m_w_in, m_conv_a_w, m_conv_qkv_w, m_a_log, m_dt_bias, m_dn_norm_g, m_w_out, m_norm_ffn_g, m_w_up, m_conv_ffn_w, m_w_down, m_norm_ple_g, m_w_ple_gate, m_w_ple_proj, m_final_norm_g, v_norm_mix_g, v_w_in, v_conv_a_w, v_conv_qkv_w, v_a_log, v_dt_bias, v_dn_norm_g, v_w_out, v_norm_ffn_g, v_w_up, v_conv_ffn_w, v_w_down, v_norm_ple_g, v_w_ple_gate, v_w_ple_proj, v_final_norm_g):
    T, D = x.shape[1], x.shape[2]
    xd, _, cd = _mesh_pos()
    me = 4 * xd + 2 * lax.axis_index("y") + cd

    conv_sh = [conv_a_w[0], conv_qkv_w[0], conv_ffn_w[0]]
    conv_n = [c.size for c in conv_sh]
    pack_rows = -(-sum(conv_n) // LANE)
    conv_pack = jnp.pad(jnp.concatenate([c.reshape(-1) for c in conv_sh]), (0, pack_rows * LANE - sum(conv_n))).reshape(pack_rows, LANE)
    names = ["w_pp", "w_in", "conv", "w_out", "w_up", "w_down", "w_pg"]
    tr_ = lambda t: jnp.swapaxes(t, 1, 2)
    shards = [w_ple_proj[0].astype(BF16), w_in[0].T.astype(BF16), conv_pack, w_out[0].astype(BF16), w_up[0].astype(BF16),
              w_down[0].astype(BF16), w_ple_gate[0].astype(BF16)]
    empty_slots = lambda blocks: [lax.empty((N_DEV,) + tuple(b.shape), b.dtype) for b in blocks]
    handles, tok0 = _split_start(shards, empty_slots(shards), False, name="gather_start",
                                 relations=[(SIBLING,) + SAME_CORE if nm == "w_in" else ALL_PEERS for nm in names])
    handle = dict(zip(names, handles))
    own = dict(zip(names, shards))
    in_cols = N_DEV * w_in.shape[2]
    in_p = (in_cols // LANE) * LANE + AB_PAD
    in_place = {"w_up", "w_pp"}

    def gathered(name, after):
        if name == "w_in":
            passed, fwd = _gather_forward(handle[name], after, name="gather_forward_w_in")
            landed = _gather_wait_two_level(passed, fwd, name="gather_wait_w_in")
        else:
            landed = _split_wait(handle[name], after, False, name="gather_wait_" + name)
        return lax.dynamic_update_index_in_dim(landed, own[name], me, 0)

    def wt(name, after):
        landed = gathered(name, after)
        if name in in_place:
            return landed
        full = landed.reshape(-1, D)
        return jnp.pad(full, ((0, in_p - in_cols), (0, 0))) if name == "w_in" else full

    def conv(after):
        flat = gathered("conv", after).reshape(N_DEV, pack_rows * LANE)
        out, o_ = {}, 0
        for nm, c, n_ in zip(("conv_a", "conv_qkv", "conv_ffn"), conv_sh, conv_n):
            out[nm] = _col_sharded(flat[:, o_:o_ + n_].reshape((N_DEV,) + c.shape))
            o_ += n_
        return out

    pending, mine = {}, {}

    def emit(grads):
        parts = [g if nm in in_place else (g[:in_cols] if nm == "w_in" else g).reshape(N_DEV, -1, D)
                 for nm, g in grads.items()]
        hs, tok = _split_start(parts, empty_slots([q[0] for q in parts]), True, name="scatter_start_" + "_".join(grads))
        pending.update(zip(grads, hs))
        mine.update({nm: lax.dynamic_index_in_dim(q, me, 0, keepdims=False) for nm, q in zip(grads, parts)})
        return tok

    S = {
        "g_mix": norm_mix_g + tok0[0, 0], "a_log": a_log, "dt_bias": dt_bias, "dn_g": dn_norm_g, "g_ffn": norm_ffn_g,
        "g_ple": norm_ple_g, "g_final": final_norm_g.reshape(1, D),
    }

    loss_v, grad_x, G = _local_step(x[0], p[0, 0], loss_target[0], S, wt, conv, emit)
    loss = lax.psum(loss_v[0, 0], ("x", "y", "c"))

    small_names = ["g_mix", "g_ffn", "g_ple", "g_final", "dn_g", "a_log", "dt_bias", "conv_a", "conv_qkv", "conv_ffn"]
    small_rows, pieces = [], []
    for nm in small_names:
        g_ = G[nm].reshape(-1)
        r_ = -(-g_.size // (8 * LANE)) * 8
        small_rows.append(r_)
        pieces.append(jnp.pad(g_, (0, r_ * LANE - g_.size)).reshape(r_, LANE))
    landed = {nm: _split_wait(h_, grad_x, True, name="scatter_wait_" + nm) for nm, h_ in pending.items() if nm != "w_in"}

    def adam(parts, w_, m_, v_, nm, own_=None):
        shp = w_.shape
        w2, m2, v2 = (t.reshape(parts.shape[1:]) for t in (w_, m_, v_))
        kw = {} if own_ is None else {"own": own_, "me": me.astype(jnp.int32).reshape(1)}
        return tuple(t.reshape(shp) for t in _adam(parts, w2, m2, v2, name="adam_" + nm, **kw))

    big = {
        "w_up": adam(landed["w_up"], w_up, m_w_up, v_w_up, "w_up", mine["w_up"]),
        "w_down": adam(landed["w_down"], w_down, m_w_down, v_w_down, "w_down", mine["w_down"]),
        "w_out": adam(landed["w_out"], w_out, m_w_out, v_w_out, "w_out", mine["w_out"]),
        "w_pg": adam(landed["w_pg"], w_ple_gate, m_w_ple_gate, v_w_ple_gate, "w_ple_gate", mine["w_pg"]),
        "w_pp": adam(landed["w_pp"], w_ple_proj, m_w_ple_proj, v_w_ple_proj, "w_ple_proj", mine["w_pp"]),
    }
    first = lambda t: lax.slice(t, (0,) * t.ndim, (1,) * t.ndim).reshape(1)
    big_done = sum(first(r[1]) for r in big.values())
    (small_l,) = _exchange([jnp.concatenate(pieces, axis=0)], False, name="gather_small_grads", after=big_done)

    def small_parts(nm):
        i = small_names.index(nm)
        r0 = sum(small_rows[:i])
        shp = G[nm].shape
        return small_l[:, r0:r0 + small_rows[i], :].reshape(N_DEV, -1)[:, :G[nm].size].reshape((N_DEV,) + shp)

    def conv_parts(nm, shard):
        full = small_parts(nm)
        C = shard.shape[-1]
        return lax.dynamic_slice_in_dim(full, me * C, C, axis=2)

    res = [
        adam(small_parts("g_mix"), norm_mix_g, m_norm_mix_g, v_norm_mix_g, "norm_mix_g"),
        None,
        adam(conv_parts("conv_a", conv_a_w), conv_a_w, m_conv_a_w, v_conv_a_w, "conv_a_w"),
        adam(conv_parts("conv_qkv", conv_qkv_w), conv_qkv_w, m_conv_qkv_w, v_conv_qkv_w, "conv_qkv_w"),
        adam(small_parts("a_log"), a_log, m_a_log, v_a_log, "a_log"),
        adam(small_parts("dt_bias"), dt_bias, m_dt_bias, v_dt_bias, "dt_bias"),
        adam(small_parts("dn_g"), dn_norm_g, m_dn_norm_g, v_dn_norm_g, "dn_norm_g"),
        big["w_out"],
        adam(small_parts("g_ffn"), norm_ffn_g, m_norm_ffn_g, v_norm_ffn_g, "norm_ffn_g"),
        big["w_up"],
        adam(conv_parts("conv_ffn", conv_ffn_w), conv_ffn_w, m_conv_ffn_w, v_conv_ffn_w, "conv_ffn_w"),
        big["w_down"],
        adam(small_parts("g_ple"), norm_ple_g, m_norm_ple_g, v_norm_ple_g, "norm_ple_g"),
        big["w_pg"],
        big["w_pp"],
        adam(small_parts("g_final"), final_norm_g.reshape(1, D), m_final_norm_g.reshape(1, D),
             v_final_norm_g.reshape(1, D), "final_norm_g"),
    ]
    res[-1] = tuple(t.reshape(D) for t in res[-1])
    landed_in = _split_wait(pending["w_in"], res[10][1], True, name="scatter_wait_w_in")
    res[1] = tuple(tr_(t) for t in adam(landed_in, tr_(w_in), tr_(m_w_in), tr_(v_w_in), "w_in", mine["w_in"]))
    grads, deltas, new_m, new_v = zip(*res)
    return (loss, grad_x[None], *grads, *deltas, *new_m, *new_v)
```

```python
import functools

import jax
import jax.numpy as jnp
from jax import lax
from jax.experimental import pallas as pl
from jax.experimental.pallas import tpu as pltpu

F32 = jnp.float32
BF16 = jnp.bfloat16

EPS = 1e-6
CHUNK = 64
HEAD = 128
LANE = 128
N_DEV = 8
AB_PAD = 512

ADAM_LR = 0.001
ADAM_B1 = 0.9
ADAM_B2 = 0.999
ADAM_EPS = 1e-08
ADAM_WD = 0.01
ADAM_STEP = 10

MESH = pl.DeviceIdType.MESH


def _tile(dim, target, align=LANE):
    if dim <= target:
        return dim
    t = (target // align) * align
    while t > align and dim % t:
        t -= align
    assert dim % t == 0, (dim, target)
    return t


def _params(sem, vmem_mb=48):
    return pltpu.CompilerParams(dimension_semantics=sem, vmem_limit_bytes=vmem_mb << 20)


_DN = {"nn": (((1,), (0,)), ((), ())), "nt": (((1,), (1,)), ((), ())), "tn": (((0,), (0,)), ((), ()))}
LONG_K = 4096
SHARD_TILE = 1408


def _matmul(a, b, mode, *, name, out_dtypes=(F32,), epilogue=None, extras=(), vec_extras=(), n_vec=0, after=None,
            a_shards=False, b_shards=False, out_shards=False, out_lanes=False, tm=1024, tn=1024, tk=2048):
    shard_w = b.shape[2] if b_shards else None
    if b_shards:
        b_rows, b_cols = b.shape[1], b.shape[0] * shard_w
    else:
        b_rows, b_cols = b.shape
    a_w = a.shape[2] if a_shards else None
    a_dims = (a.shape[1], a.shape[0] * a_w) if a_shards else a.shape
    if mode == "nn":
        (M, K), (K2, N) = a_dims, (b_rows, b_cols)
    elif mode == "nt":
        (M, K), (N, K2) = a_dims, (b_rows, b_cols)
    else:
        (K, M), (K2, N) = a_dims, (b_rows, b_cols)
    assert K == K2, (name, a.shape, b.shape)
    tm = _tile(M, tm)
    n_dims = [N] + ([shard_w] if (b_shards and mode != "nt") else []) + ([N // N_DEV] if out_shards else [])
    tn = _tile(min(n_dims), tn)
    assert all(d % tn == 0 for d in n_dims), (name, n_dims, tn)
    grp = 1
    if b_shards and mode == "nt":
        grp = max(g for g in (1, 2, 4, 8) if g <= max(1, tk // shard_w) and (a_w is None or a_w % (g * shard_w) == 0))
    k_dims = [K] + ([shard_w] if (b_shards and mode == "nt") else []) + ([a_w] if a_shards else [])
    tk = grp * shard_w if grp > 1 else _tile(min(k_dims), tk)
    assert K % tk == 0, (name, K, tk)
    nk = K // tk
    n_ex, n_out = len(extras) + len(vec_extras), len(out_dtypes)
    assert n_vec == 0 or tn == N, (name, tn, N)
    dn = _DN[mode]

    n_tok = 0 if after is None else 1

    def body(a_ref, b_ref, *rest):
        rest = rest[n_tok:]
        ex_refs, out_refs, vec_refs = rest[:n_ex], rest[n_ex:n_ex + n_out], rest[n_ex + n_out:n_ex + n_out + n_vec]
        if grp > 1:
            part = sum(lax.dot_general(a_ref[:, s * shard_w:(s + 1) * shard_w].astype(BF16), b_ref[s].astype(BF16), dn,
                                       preferred_element_type=F32) for s in range(grp))
        else:
            part = lax.dot_general(a_ref[...].astype(BF16), b_ref[...].astype(BF16), dn, preferred_element_type=F32)
        first_rows = pl.program_id(0) == 0

        def finish(res):
            outs = (res,) if epilogue is None else epilogue(res, *[e[...] for e in ex_refs])
            for o_ref, val in zip(out_refs, outs[:n_out]):
                if out_lanes:
                    for c in range(tn // LANE):
                        o_ref[c] = val[:, c * LANE:(c + 1) * LANE].astype(o_ref.dtype)
                else:
                    o_ref[...] = val.astype(o_ref.dtype)
            for v_ref, val in zip(vec_refs, outs[n_out:]):
                @pl.when(first_rows)
                def _(v_ref=v_ref, val=val):
                    v_ref[...] = val

                @pl.when(jnp.logical_not(first_rows))
                def _(v_ref=v_ref, val=val):
                    v_ref[...] += val

        if nk == 1:
            finish(part)
            return
        acc, k = rest[-1], pl.program_id(2)

        @pl.when(k == 0)
        def _():
            acc[...] = part

        @pl.when(k > 0)
        def _():
            acc[...] += part

        @pl.when(k == nk - 1)
        def _():
            finish(acc[...])

    if a_shards:
        assert mode == "nt" and a_w % tk == 0, (name, mode, a_w, tk)
        per_a = a_w // tk
        a_spec = pl.BlockSpec((None, tm, tk), lambda i, j, k: (lax.div(k, per_a), i, lax.rem(k, per_a)))
    else:
        a_spec = pl.BlockSpec((tk, tm), lambda i, j, k: (k, i)) if mode == "tn" else pl.BlockSpec((tm, tk), lambda i, j, k: (i, k))
    if b_shards and mode != "nt":
        per = shard_w // tn
        b_spec = pl.BlockSpec((None, tk, tn), lambda i, j, k: (lax.div(j, per), k, lax.rem(j, per)))
    elif b_shards and grp > 1:
        b_spec = pl.BlockSpec((grp, tn, shard_w), lambda i, j, k: (k, j, 0))
    elif b_shards:
        per = shard_w // tk
        b_spec = pl.BlockSpec((None, tn, tk), lambda i, j, k: (lax.div(k, per), j, lax.rem(k, per)))
    else:
        b_spec = pl.BlockSpec((tn, tk), lambda i, j, k: (j, k)) if mode == "nt" else pl.BlockSpec((tk, tn), lambda i, j, k: (k, j))
    mn_spec = pl.BlockSpec((tm, tn), lambda i, j, k: (i, j))
    vec_spec = pl.BlockSpec((1, tn), lambda i, j, k: (0, j))
    if out_shards:
        assert not extras
        per_o = (N // N_DEV) // tn
        out_spec = pl.BlockSpec((None, tm, tn), lambda i, j, k: (lax.div(j, per_o), i, lax.rem(j, per_o)))
        out_dims = (N_DEV, M, N // N_DEV)
    elif out_lanes:
        assert not extras
        out_spec = pl.BlockSpec((tn // LANE, tm, LANE), lambda i, j, k: (j, i, 0))
        out_dims = (N // LANE, M, LANE)
    else:
        out_spec, out_dims = mn_spec, (M, N)
    outs = pl.pallas_call(
        body, name=name, grid=(M // tm, N // tn, nk),
        in_specs=[a_spec, b_spec] + [pl.BlockSpec((8, LANE), lambda i, j, k: (0, 0))] * n_tok
        + [mn_spec] * len(extras) + [vec_spec] * len(vec_extras),
        out_specs=[out_spec] * n_out + [vec_spec] * n_vec,
        out_shape=[jax.ShapeDtypeStruct(out_dims, dt) for dt in out_dtypes] + [jax.ShapeDtypeStruct((1, N), F32)] * n_vec,
        scratch_shapes=[pltpu.VMEM((tm, tn), F32)] if nk > 1 else [],
        compiler_params=_params(("arbitrary" if n_vec else "parallel", "parallel", "arbitrary"), 56),
    )(a, b, *([] if after is None else [after]), *extras, *vec_extras)
    return outs[0] if n_out + n_vec == 1 else outs


def _rms_fwd(x, g, *, name):
    T, D = x.shape
    tr = _tile(T, 512, 8)

    def body(x_ref, g_ref, h_ref):
        xv = x_ref[...]
        r = lax.rsqrt(jnp.mean(xv * xv, axis=-1, keepdims=True) + EPS)
        h_ref[...] = (xv * r * g_ref[...]).astype(h_ref.dtype)

    return pl.pallas_call(
        body, name=name, grid=(T // tr,),
        in_specs=[pl.BlockSpec((tr, D), lambda i: (i, 0)), pl.BlockSpec((1, D), lambda i: (0, 0))],
        out_specs=pl.BlockSpec((tr, D), lambda i: (i, 0)),
        out_shape=jax.ShapeDtypeStruct((T, D), BF16),
        compiler_params=_params(("parallel",)),
    )(x, g)


def _rms_bwd(x, g, dh, dres, *, name):
    T, D = x.shape
    tr = _tile(T, 512, 8)
    epi = _epi_rms_bwd(2)

    def body(x_ref, g_ref, dh_ref, dres_ref, dx_ref, dxb_ref, dg_ref):
        dx, _, dgp = epi(dh_ref[...], x_ref[...], dres_ref[...], g_ref[...])

        @pl.when(pl.program_id(0) == 0)
        def _():
            dg_ref[...] = jnp.zeros_like(dg_ref)

        dg_ref[...] += dgp
        dx_ref[...] = dx
        dxb_ref[...] = dx.astype(dxb_ref.dtype)

    row = pl.BlockSpec((tr, D), lambda i: (i, 0))
    vec = pl.BlockSpec((1, D), lambda i: (0, 0))
    return pl.pallas_call(
        body, name=name, grid=(T // tr,),
        in_specs=[row, vec, row, row], out_specs=[row, row, vec],
        out_shape=[jax.ShapeDtypeStruct((T, D), F32), jax.ShapeDtypeStruct((T, D), BF16), jax.ShapeDtypeStruct((1, D), F32)],
        compiler_params=_params(("arbitrary",)),
    )(x, g, dh, dres)


ROW_TILE = 512


def _epi_residual_rms(acc, res, g):
    xn = acc + res
    r = lax.rsqrt(jnp.mean(xn * xn, axis=-1, keepdims=True) + EPS)
    return xn, xn * r * g


def _epi_rms_bwd(n_copies):
    def epi(dh, x, dres, g):
        r = lax.rsqrt(jnp.mean(x * x, axis=-1, keepdims=True) + EPS)
        xh = x * r
        dxh = dh * g
        dx = dres + r * (dxh - xh * jnp.mean(dxh * xh, axis=-1, keepdims=True))
        return (dx,) * n_copies + (jnp.sum(dh * xh, axis=0, keepdims=True),)
    return epi


def _final_loss(x, g, tgt, pp, sg, *, name):
    T, D = x.shape
    tr = _tile(T, 256, 8)

    def body(x_ref, g_ref, t_ref, pp_ref, sg_ref, dx_ref, dg_ref, loss_ref, dpg_ref, dpp_ref):
        xv = x_ref[...]
        r = lax.rsqrt(jnp.mean(xv * xv, axis=-1, keepdims=True) + EPS)
        xh = xv * r
        gv = g_ref[...]
        err = xh * gv - t_ref[...]

        @pl.when(pl.program_id(0) == 0)
        def _():
            dg_ref[...] = jnp.zeros_like(dg_ref)
            loss_ref[...] = jnp.zeros_like(loss_ref)

        part = 0.5 * jnp.sum(jnp.mean(err * err, axis=-1, keepdims=True), axis=0, keepdims=True)
        loss_ref[...] += jnp.broadcast_to(part, loss_ref.shape)
        dy = err * (1.0 / D)
        dg_ref[...] += jnp.sum(dy * xh, axis=0, keepdims=True)
        dxh = dy * gv
        dx = r * (dxh - xh * jnp.mean(dxh * xh, axis=-1, keepdims=True))
        dx_ref[...] = dx
        s = sg_ref[...]
        dpg_ref[...] = (dx * pp_ref[...] * s * (1.0 - s)).astype(dpg_ref.dtype)
        dpp_ref[...] = (dx * s).astype(dpp_ref.dtype)

    row = pl.BlockSpec((tr, D), lambda i: (i, 0))
    vec = pl.BlockSpec((1, D), lambda i: (0, 0))
    return pl.pallas_call(
        body, name=name, grid=(T // tr,),
        in_specs=[row, vec, row, row, row], out_specs=[row, vec, pl.BlockSpec((1, LANE), lambda i: (0, 0)), row, row],
        out_shape=[jax.ShapeDtypeStruct((T, D), F32), jax.ShapeDtypeStruct((1, D), F32),
                   jax.ShapeDtypeStruct((1, LANE), F32)] + [jax.ShapeDtypeStruct((T, D), BF16)] * 2,
        compiler_params=_params(("arbitrary",)),
    )(x, g, tgt, pp, sg)


ROWS_QKV_FWD, ROWS_QKV_BWD, ROWS_FFN_FWD, ROWS_FFN_BWD, ROWS_GROUP_A = 512, 256, 256, 128, 256


def _ext(ref, r0, T, before, after, RC):
    parts = []
    if before:
        p0 = pl.multiple_of(jnp.maximum(r0 - 8, 0), 8)
        parts.append(jnp.where(r0 > 0, ref[pl.ds(p0, 8), :], 0.0))
    parts.append(ref[pl.ds(r0, RC), :])
    if after:
        n0 = pl.multiple_of(jnp.minimum(r0 + RC, T - 8), 8)
        parts.append(jnp.where(r0 + RC < T, ref[pl.ds(n0, 8), :], 0.0))
    return parts[0] if len(parts) == 1 else jnp.concatenate(parts, axis=0)


def _fold8(x):
    return jnp.sum(x.reshape(x.shape[0] // 8, 8, x.shape[1]), axis=0)


def _win(ref, r0, lo, n, T, RC, edge):
    if not edge:
        return ref[pl.ds(r0 + lo, n), :]
    xx = _ext(ref, r0, T, True, True, RC)
    a = 8 + lo
    return (xx if a == 0 else pltpu.roll(xx, xx.shape[0] - a, 0))[:n, :]


def _taps(ref, w_ref, K, r0, n, T, RC, edge):
    wins = [_win(ref, r0, -(K - 1 - j), n, T, RC, edge) for j in range(K)]
    y = wins[0] * w_ref[0:1, :]
    for j in range(1, K):
        y = y + wins[j] * w_ref[j:j + 1, :]
    return wins, y


def _untaps(scr_ref, val, w_ref, K, RC):
    scr_ref[0:val.shape[0], :] = val
    y = scr_ref[K - 1:K - 1 + RC, :] * w_ref[0:1, :]
    for j in range(1, K):
        s = K - 1 - j
        y = y + scr_ref[s:s + RC, :] * w_ref[j:j + 1, :]
    return y


def _peeled(n_chunks, RC, step, init):
    carry = step(0, init, True)
    if n_chunks > 2:
        carry = lax.fori_loop(1, n_chunks - 1, lambda i, c: step(pl.multiple_of(i * RC, RC), c, False), carry)
    if n_chunks > 1:
        carry = step((n_chunks - 1) * RC, carry, True)
    return carry


def _silu(x):
    return x * jax.nn.sigmoid(x)


def _dsilu(x):
    s = jax.nn.sigmoid(x)
    return s * (1.0 + x * (1.0 - s))


def _col_specs(T, offs):
    return [pl.BlockSpec((T, LANE), functools.partial(lambda o, j: (0, o + j), o)) for o in offs]


def _group_a_fwd(proj, conv_w, CW, out_cols, *, name):
    T = proj.shape[0]
    RC = _tile(T, ROWS_GROUP_A, 8)
    nb = CW // LANE
    K = conv_w.shape[0]

    def body(ax_ref, ab_ref, ac_ref, w_ref, y_ref):
        def step(r0, carry, edge):
            c = None
            for j in range(K):
                lo = -(K - 1 - j)
                t = _win(ac_ref, r0, lo, RC, T, RC, edge) * _win(ax_ref, r0, lo, RC, T, RC, edge) * w_ref[j:j + 1, :]
                c = t if c is None else c + t
            y_ref[pl.ds(r0, RC), :] = (ab_ref[pl.ds(r0, RC), :] * c).astype(y_ref.dtype)
            return carry
        _peeled(T // RC, RC, step, 0)

    return pl.pallas_call(
        body, name=name, grid=(nb,),
        in_specs=_col_specs(T, (0, nb, 2 * nb)) + [pl.BlockSpec((K, LANE), lambda j: (0, j))],
        out_specs=pl.BlockSpec((T, LANE), lambda j: (0, j)),
        out_shape=jax.ShapeDtypeStruct((T, out_cols), BF16), compiler_params=_params(("parallel",)),
    )(proj, proj, proj, conv_w)


def _group_a_bwd(proj, conv_w, dycat, CW, *, name):
    T = proj.shape[0]
    RC = _tile(T, ROWS_GROUP_A, 8)
    nb = CW // LANE
    K = conv_w.shape[0]

    def body(ax_ref, ab_ref, ac_ref, w_ref, dy_ref, dax_ref, dab_ref, dac_ref, dw_ref, scr_ref):
        def step(r0, accs, edge):
            ms = [_win(ac_ref, r0, -(K - 1 - j), RC, T, RC, edge) * _win(ax_ref, r0, -(K - 1 - j), RC, T, RC, edge)
                  for j in range(K)]
            c = ms[0] * w_ref[0:1, :]
            for j in range(1, K):
                c = c + ms[j] * w_ref[j:j + 1, :]
            dy = dy_ref[pl.ds(r0, RC), :]
            dab_ref[pl.ds(r0, RC), :] = (dy * c).astype(dab_ref.dtype)
            dc2 = _win(dy_ref, r0, 0, RC + 8, T, RC, edge) * _win(ab_ref, r0, 0, RC + 8, T, RC, edge)
            dm = _untaps(scr_ref, dc2, w_ref, K, RC)
            dax_ref[pl.ds(r0, RC), :] = (dm * ac_ref[pl.ds(r0, RC), :]).astype(dax_ref.dtype)
            dac_ref[pl.ds(r0, RC), :] = (dm * ax_ref[pl.ds(r0, RC), :]).astype(dac_ref.dtype)
            return tuple(accs[j] + _fold8(dc2[:RC] * ms[j]) for j in range(K))

        accs = _peeled(T // RC, RC, step, tuple(jnp.zeros((8, LANE), F32) for _ in range(K)))
        for j in range(K):
            dw_ref[j:j + 1, :] = jnp.sum(accs[j], axis=0, keepdims=True)

    col = pl.BlockSpec((T, LANE), lambda j: (0, j))
    wsp = pl.BlockSpec((K, LANE), lambda j: (0, j))
    return pl.pallas_call(
        body, name=name, grid=(nb,),
        in_specs=_col_specs(T, (0, nb, 2 * nb)) + [wsp, col],
        out_specs=[col, col, col, wsp],
        out_shape=[jax.ShapeDtypeStruct((T, CW), BF16)] * 3 + [jax.ShapeDtypeStruct((K, CW), F32)],
        scratch_shapes=[pltpu.VMEM((RC + 8, LANE), F32)],
        compiler_params=_params(("parallel",)),
    )(proj, proj, proj, conv_w, dycat)


def _qkv_fwd(proj, conv_w, off, H, *, name):
    T = proj.shape[0]
    RC = _tile(T, ROWS_QKV_FWD, 8)
    nb = 3 * H
    K = conv_w.shape[0]

    def body(x_ref, w_ref, y_ref):
        j = pl.program_id(0)
        is_qk = j < 2 * H
        scale = jnp.where(j < H, HEAD ** -0.5, 1.0).astype(F32)

        def step(r0, carry, edge):
            s = _silu(_taps(x_ref, w_ref, K, r0, RC, T, RC, edge)[1])
            r = lax.rsqrt(jnp.sum(s * s, axis=-1, keepdims=True) + EPS) * scale
            y_ref[pl.ds(r0, RC), :] = s * jnp.where(is_qk, r, 1.0)
            return carry
        _peeled(T // RC, RC, step, 0)

    return pl.pallas_call(
        body, name=name, grid=(nb,),
        in_specs=_col_specs(T, (off,)) + [pl.BlockSpec((K, LANE), lambda j: (0, j))],
        out_specs=pl.BlockSpec((T, LANE), lambda j: (0, j)),
        out_shape=jax.ShapeDtypeStruct((T, nb * LANE), F32), compiler_params=_params(("parallel",)),
    )(proj, conv_w)


def _qkv_bwd(proj, conv_w, dq, dk, dv, off, H, into, *, name):
    T = proj.shape[0]
    RC = _tile(T, ROWS_QKV_BWD, 8)
    nb = 3 * H
    K = conv_w.shape[0]

    def body(x_ref, w_ref, dq_ref, dk_ref, dv_ref, dx_ref, dw_ref, scr_ref):
        j = pl.program_id(0)
        is_qk = j < 2 * H
        scale = jnp.where(j < H, HEAD ** -0.5, 1.0).astype(F32)

        def step(r0, accs, edge):
            xs, c2 = _taps(x_ref, w_ref, K, r0, RC + 8, T, RC, edge)
            s2 = _silu(c2)
            dn2 = jnp.where(j < H, _win(dq_ref, r0, 0, RC + 8, T, RC, edge),
                            jnp.where(is_qk, _win(dk_ref, r0, 0, RC + 8, T, RC, edge),
                                      _win(dv_ref, r0, 0, RC + 8, T, RC, edge)))
            r = lax.rsqrt(jnp.sum(s2 * s2, axis=-1, keepdims=True) + EPS)
            nh = s2 * r
            dnp = dn2 * scale
            ds_qk = r * (dnp - nh * jnp.sum(dnp * nh, axis=-1, keepdims=True))
            ds2 = jnp.where(is_qk, ds_qk, dn2)
            dc2 = ds2 * _dsilu(c2)
            dx_ref[pl.ds(r0, RC), :] = _untaps(scr_ref, dc2, w_ref, K, RC).astype(dx_ref.dtype)
            return tuple(accs[jj] + _fold8(dc2[:RC] * xs[jj][:RC]) for jj in range(K))

        accs = _peeled(T // RC, RC, step, tuple(jnp.zeros((8, LANE), F32) for _ in range(K)))
        for jj in range(K):
            dw_ref[jj:jj + 1, :] = jnp.sum(accs[jj], axis=0, keepdims=True)

    wsp = pl.BlockSpec((K, LANE), lambda j: (0, j))
    return pl.pallas_call(
        lambda x_ref, w_ref, dq_ref, dk_ref, dv_ref, into_ref, dx_ref, dw_ref, scr_ref: body(
            x_ref, w_ref, dq_ref, dk_ref, dv_ref, dx_ref, dw_ref, scr_ref),
        name=name, grid=(nb,),
        in_specs=_col_specs(T, (off,)) + [wsp] + [
            pl.BlockSpec((T, LANE), functools.partial(lambda o, j: (0, jnp.clip(j - o, 0, H - 1)), o)) for o in (0, H, 2 * H)
        ] + [pl.BlockSpec(memory_space=pl.ANY)],
        out_specs=[pl.BlockSpec((T, LANE), lambda j: (0, off + j)), wsp],
        out_shape=[jax.ShapeDtypeStruct(into.shape, into.dtype), jax.ShapeDtypeStruct((K, nb * LANE), F32)],
        scratch_shapes=[pltpu.VMEM((RC + 8, LANE), F32)],
        input_output_aliases={5: 0}, compiler_params=_params(("parallel",)),
    )(proj, conv_w, dq, dk, dv, into)


def _softplus(x):
    return jnp.maximum(x, 0.0) + jnp.log(1.0 + jnp.exp(-jnp.abs(x)))


def _gates_fwd(proj, alog, dtb, off, H, *, name):
    T = proj.shape[0]
    tr = _tile(T, 512, CHUNK)

    def body(ab_ref, al_ref, dt_ref, gam_ref, beta_ref):
        ab = ab_ref[...]
        lane = lax.broadcasted_iota(jnp.int32, ab.shape, 1)
        g = -jnp.exp(al_ref[...]) * _softplus(ab + dt_ref[...])
        gb = jnp.where(lane < H, g, jnp.where(lane < 2 * H, jax.nn.sigmoid(ab), 0.0))
        tril = _tri().astype(F32)
        gam = jnp.concatenate([_mm(tril, gb[c * CHUNK:(c + 1) * CHUNK, :], precision=lax.Precision.HIGHEST)
                               for c in range(tr // CHUNK)], axis=0)
        for h in range(H):
            gam_ref[h] = jnp.broadcast_to(gam[:, h:h + 1], (tr, LANE))
            beta_ref[h] = jnp.broadcast_to(gb[:, H + h:H + h + 1], (tr, LANE))

    vec = pl.BlockSpec((1, LANE), lambda i: (0, 0))
    heads = pl.BlockSpec((H, tr, LANE), lambda i: (0, i, 0))
    return pl.pallas_call(
        body, name=name, grid=(T // tr,),
        in_specs=[pl.BlockSpec((tr, LANE), lambda i: (i, off)), vec, vec],
        out_specs=[heads, heads],
        out_shape=[jax.ShapeDtypeStruct((H, T, LANE), F32)] * 2, compiler_params=_params(("parallel",)),
    )(proj, alog, dtb)


def _gates_bwd(proj, alog, dtb, dgamB, dbB, off, H, *, name):
    T = proj.shape[0]
    tr = _tile(T, 512, CHUNK)

    def body(ab_ref, al_ref, dt_ref, dgam_ref, dbeta_ref, dab_ref, dal_ref, ddt_ref):
        ab = ab_ref[...]
        lane = lax.broadcasted_iota(jnp.int32, ab.shape, 1)
        is_g = lane < H
        d = jnp.zeros_like(ab)
        for h in range(H):
            d = jnp.where(lane == h, dgam_ref[h], jnp.where(lane == H + h, dbeta_ref[h], d))
        triu = _tri(upper=True).astype(F32)
        dg = jnp.concatenate([_mm(triu, d[c * CHUNK:(c + 1) * CHUNK, :], precision=lax.Precision.HIGHEST)
                              for c in range(tr // CHUNK)], axis=0)
        z = ab + dt_ref[...]
        A = -jnp.exp(al_ref[...])
        da = dg * A * jax.nn.sigmoid(z)
        beta = jax.nn.sigmoid(ab)
        db = d * beta * (1.0 - beta)
        dab_ref[...] = jnp.where(is_g, da, jnp.where(lane < 2 * H, db, 0.0)).astype(dab_ref.dtype)

        @pl.when(pl.program_id(0) == 0)
        def _():
            dal_ref[...] = jnp.zeros_like(dal_ref)
            ddt_ref[...] = jnp.zeros_like(ddt_ref)

        dal_ref[...] += jnp.sum(jnp.where(is_g, dg * A * _softplus(z), 0.0), axis=0, keepdims=True)
        ddt_ref[...] += jnp.sum(jnp.where(is_g, da, 0.0), axis=0, keepdims=True)

    vec = pl.BlockSpec((1, LANE), lambda i: (0, 0))
    row = pl.BlockSpec((tr, LANE), lambda i: (i, 0))
    heads = pl.BlockSpec((H, tr, LANE), lambda i: (0, i, 0))
    return pl.pallas_call(
        body, name=name, grid=(T // tr,),
        in_specs=[pl.BlockSpec((tr, LANE), lambda i: (i, off)), vec, vec, heads, heads],
        out_specs=[row, vec, vec],
        out_shape=[jax.ShapeDtypeStruct((T, LANE), BF16), jax.ShapeDtypeStruct((1, LANE), F32),
                   jax.ShapeDtypeStruct((1, LANE), F32)],
        compiler_params=_params(("arbitrary",)),
    )(proj, alog, dtb, dgamB, dbB)


def _gated_norm_fwd(o, proj, gn, zoff, ycat, *, name):
    T, W = o.shape
    tr = _tile(T, 512, 8)
    nh_, zblk = W // LANE, (zoff * LANE) // W
    assert zblk * W == zoff * LANE

    def body(o_ref, z_ref, g_ref, ycat_ref, y_ref):
        for h in range(nh_):
            ln = slice(h * LANE, (h + 1) * LANE)
            ov = o_ref[:, ln]
            r = lax.rsqrt(jnp.mean(ov * ov, axis=-1, keepdims=True) + EPS)
            y_ref[:, ln] = (ov * r * g_ref[...] * _silu(z_ref[:, ln])).astype(y_ref.dtype)

    assert ycat.shape == (T, 2 * W), ycat.shape
    blk = pl.BlockSpec((tr, W), lambda i: (i, 0))
    return pl.pallas_call(
        body, name=name, grid=(T // tr,),
        in_specs=[blk, pl.BlockSpec((tr, W), lambda i: (i, zblk)), pl.BlockSpec((1, LANE), lambda i: (0, 0)),
                  pl.BlockSpec(memory_space=pl.ANY)],
        out_specs=pl.BlockSpec((tr, W), lambda i: (i, 1)), out_shape=jax.ShapeDtypeStruct(ycat.shape, ycat.dtype),
        input_output_aliases={3: 0}, compiler_params=_params(("parallel",)),
    )(o, proj, gn, ycat)


def _gated_norm_bwd(o, proj, gn, dycat, zoff, yoff, *, name):
    T, W = o.shape
    tr = _tile(T, 512, 8)
    nh_, zblk, yblk = W // LANE, (zoff * LANE) // W, (yoff * LANE) // W
    assert zblk * W == zoff * LANE and yblk * W == yoff * LANE

    def body(o_ref, z_ref, g_ref, dy_ref, do_ref, dz_ref, dg_ref):
        @pl.when(pl.program_id(0) == 0)
        def _():
            dg_ref[...] = jnp.zeros_like(dg_ref)

        gv = g_ref[...]
        dg = jnp.zeros_like(gv)
        for h in range(nh_):
            ln = slice(h * LANE, (h + 1) * LANE)
            ov, zv, dy = o_ref[:, ln], z_ref[:, ln], dy_ref[:, ln]
            r = lax.rsqrt(jnp.mean(ov * ov, axis=-1, keepdims=True) + EPS)
            nh = ov * r
            s = _silu(zv)
            dg = dg + jnp.sum(dy * nh * s, axis=0, keepdims=True)
            dz_ref[:, ln] = (dy * nh * gv * _dsilu(zv)).astype(dz_ref.dtype)
            dn = dy * gv * s
            do_ref[:, ln] = r * (dn - nh * jnp.mean(dn * nh, axis=-1, keepdims=True))
        dg_ref[...] += dg

    blk = pl.BlockSpec((tr, W), lambda i: (i, 0))
    vec = pl.BlockSpec((1, LANE), lambda i: (0, 0))
    return pl.pallas_call(
        body, name=name, grid=(T // tr,),
        in_specs=[blk, pl.BlockSpec((tr, W), lambda i: (i, zblk)), vec, pl.BlockSpec((tr, W), lambda i: (i, yblk))],
        out_specs=[blk, pl.BlockSpec((tr, W), lambda i: (i, zblk)), vec],
        out_shape=[jax.ShapeDtypeStruct((T, W), F32), jax.ShapeDtypeStruct(proj.shape, BF16),
                   jax.ShapeDtypeStruct((1, LANE), F32)],
        compiler_params=_params(("arbitrary",)),
    )(o, proj, gn, dycat)


def _ffn_act_fwd(up_pre, conv_w, *, name):
    T, F2 = up_pre.shape[1], up_pre.shape[0] * LANE
    RC = _tile(T, ROWS_FFN_FWD, 8)
    nb = F2 // 2 // LANE
    K = conv_w.shape[0]

    def body(g_ref, v_ref, wg_ref, wv_ref, y_ref):
        def step(r0, carry, edge):
            _, gate = _taps(g_ref, wg_ref, K, r0, RC, T, RC, edge)
            _, val = _taps(v_ref, wv_ref, K, r0, RC, T, RC, edge)
            y_ref[pl.ds(r0, RC), :] = (_silu(gate) * val).astype(y_ref.dtype)
            return carry
        _peeled(T // RC, RC, step, 0)

    return pl.pallas_call(
        body, name=name, grid=(nb,),
        in_specs=[pl.BlockSpec((None, T, LANE), lambda j: (j, 0, 0)), pl.BlockSpec((None, T, LANE), lambda j: (nb + j, 0, 0)),
                  pl.BlockSpec((K, LANE), lambda j: (0, j)), pl.BlockSpec((K, LANE), lambda j: (0, nb + j))],
        out_specs=pl.BlockSpec((T, LANE), lambda j: (0, j)),
        out_shape=jax.ShapeDtypeStruct((T, F2 // 2), BF16), compiler_params=_params(("parallel",)),
    )(up_pre, up_pre, conv_w, conv_w)


def _ffn_act_bwd(up_pre, conv_w, dact, *, name):
    T, F2 = up_pre.shape[1], up_pre.shape[0] * LANE
    RC = _tile(T, ROWS_FFN_BWD, 8)
    nb = F2 // 2 // LANE
    K = conv_w.shape[0]

    def body(g_ref, v_ref, wg_ref, wv_ref, da_ref, d_ref, dwg_ref, dwv_ref, sg_ref, sv_ref):
        def step(r0, accs, edge):
            gs, gate2 = _taps(g_ref, wg_ref, K, r0, RC + 8, T, RC, edge)
            vs, val2 = _taps(v_ref, wv_ref, K, r0, RC + 8, T, RC, edge)
            da2 = _win(da_ref, r0, 0, RC + 8, T, RC, edge)
            dgate2 = da2 * val2 * _dsilu(gate2)
            dval2 = da2 * _silu(gate2)
            d_ref[0, pl.ds(r0, RC), :] = _untaps(sg_ref, dgate2, wg_ref, K, RC).astype(d_ref.dtype)
            d_ref[1, pl.ds(r0, RC), :] = _untaps(sv_ref, dval2, wv_ref, K, RC).astype(d_ref.dtype)
            new = []
            for j in range(K):
                new.append(accs[2 * j] + _fold8(dgate2[:RC] * gs[j][:RC]))
                new.append(accs[2 * j + 1] + _fold8(dval2[:RC] * vs[j][:RC]))
            return tuple(new)

        accs = _peeled(T // RC, RC, step, tuple(jnp.zeros((8, LANE), F32) for _ in range(2 * K)))
        for j in range(K):
            dwg_ref[j:j + 1, :] = jnp.sum(accs[2 * j], axis=0, keepdims=True)
            dwv_ref[j:j + 1, :] = jnp.sum(accs[2 * j + 1], axis=0, keepdims=True)

    col = pl.BlockSpec((T, LANE), lambda j: (0, j))
    wsp = pl.BlockSpec((K, LANE), lambda j: (0, j))
    return pl.pallas_call(
        body, name=name, grid=(nb,),
        in_specs=[pl.BlockSpec((None, T, LANE), lambda j: (j, 0, 0)), pl.BlockSpec((None, T, LANE), lambda j: (nb + j, 0, 0)),
                  wsp, pl.BlockSpec((K, LANE), lambda j: (0, nb + j)), col],
        out_specs=[pl.BlockSpec((2, T, LANE), lambda j: (0, 0, j)), wsp, wsp],
        out_shape=[jax.ShapeDtypeStruct((2, T, F2 // 2), BF16)] + [jax.ShapeDtypeStruct((K, F2 // 2), F32)] * 2,
        scratch_shapes=[pltpu.VMEM((RC + 8, LANE), F32)] * 2,
        compiler_params=_params(("parallel",)),
    )(up_pre, up_pre, conv_w, conv_w, dact)


CPB = 8
CPB_SCAN = 4
GRP = 8
HP = lax.Precision.HIGH


def _tri(strict=False, upper=False):
    r = lax.broadcasted_iota(jnp.int32, (CHUNK, CHUNK), 0)
    c = lax.broadcasted_iota(jnp.int32, (CHUNK, CHUNK), 1)
    if upper:
        return c >= r
    return (r > c) if strict else (r >= c)


def _mm(a, b, dn="nn", precision=None):
    precision = HP if precision is None else precision
    return lax.dot_general(a, b, _DN[dn], precision=precision, preferred_element_type=F32)


def _mm16(a, b, dn="nn"):
    return lax.dot_general(a.astype(BF16), b.astype(BF16), _DN[dn], preferred_element_type=F32)


def _each(f, *cols):
    return [f(*xs) for xs in zip(*cols)]


def _decay(gam):
    return jnp.exp(jnp.where(_tri(), gam[:, :CHUNK] - gam.T[:CHUNK, :], -1e30))


def _delta_specs(T, H, cpb):
    rows = cpb * CHUNK
    col = lambda o: pl.BlockSpec((rows, LANE), functools.partial(lambda o, h, n: (n, o + h), o))
    bc = pl.BlockSpec((1, rows, LANE), lambda h, n: (h, n, 0))
    sq = pl.BlockSpec((1, cpb, CHUNK, CHUNK), lambda h, n: (h, n, 0, 0))
    vec = pl.BlockSpec((1, cpb, 1, LANE), lambda h, n: (h, n, 0, 0))
    return col, bc, sq, vec


def _delta_prep_fwd(qkv, gamB, bB, H, *, name):
    T = qkv.shape[0]
    N = T // CHUNK
    cpb = _tile(N, CPB, 8)
    grp = min(GRP, cpb)
    col, bc, sq, vec = _delta_specs(T, H, cpb)

    def body(q_ref, k_ref, v_ref, g_ref, b_ref, u_ref, w_ref, qd_ref, kd_ref, qk_ref, ti_ref, gl_ref):
        eye = (lax.broadcasted_iota(jnp.int32, (CHUNK, CHUNK), 0) == lax.broadcasted_iota(jnp.int32, (CHUNK, CHUNK), 1)).astype(F32)
        strict = _tri(strict=True)
        for c0 in range(0, cpb, grp):
            cs = list(range(c0, c0 + grp))
            rows = [slice(c * CHUNK, (c + 1) * CHUNK) for c in cs]
            q, k, v = ([r_[r, :] for r in rows] for r_ in (q_ref, k_ref, v_ref))
            bb = [b_ref[0, r, :] for r in rows]
            gam = [g_ref[0, r, :] for r in rows]
            D = _each(_decay, gam)
            e = _each(jnp.exp, gam)
            kk = _each(lambda k_: _mm16(k_, k_, "nt"), k)
            X = _each(lambda kk_, D_, b_: -(jnp.where(strict, kk_ * D_, 0.0) * b_[:, :CHUNK]), kk, D, bb)
            R = _each(lambda x: eye + x, X)
            for _ in range(5):
                X = _each(lambda x: _mm(x, x), X)
                R = _each(lambda r, x: r + _mm(r, x), R, X)
            u = _each(lambda r, b_, v_: _mm(r, b_ * v_), R, bb, v)
            w = _each(lambda r, b_, e_, k_: _mm(r, b_ * e_ * k_), R, bb, e, k)
            qk = _each(lambda q_, k_, D_: _mm16(q_, k_, "nt") * D_, q, k, D)
            for i, c in enumerate(cs):
                glast = gam[i][CHUNK - 1:CHUNK, :]
                u_ref[rows[i], :] = u[i]
                w_ref[rows[i], :] = w[i]
                qd_ref[rows[i], :] = e[i] * q[i]
                kd_ref[rows[i], :] = jnp.exp(glast - gam[i]) * k[i]
                qk_ref[0, c] = qk[i]
                ti_ref[0, c] = R[i]
                gl_ref[0, c] = jnp.exp(glast)

    full = jax.ShapeDtypeStruct((T, H * LANE), F32)
    sqs = jax.ShapeDtypeStruct((H, N, CHUNK, CHUNK), F32)
    return pl.pallas_call(
        body, name=name, grid=(H, N // cpb),
        in_specs=[col(0), col(H), col(2 * H), bc, bc],
        out_specs=[col(0)] * 4 + [sq, sq, vec],
        out_shape=[full] * 4 + [sqs, sqs, jax.ShapeDtypeStruct((H, N, 1, LANE), F32)],
        compiler_params=_params(("parallel", "parallel")),
    )(qkv, qkv, qkv, gamB, bB)


def _scan_specs(H, N, cpb, hb, rev):
    nbk = N // cpb
    blk = (lambda n: nbk - 1 - n) if rev else (lambda n: n)
    col = pl.BlockSpec((cpb * CHUNK, hb * LANE), lambda h, n: (blk(n), h))
    sq = pl.BlockSpec((hb, cpb, CHUNK, CHUNK), lambda h, n: (h, blk(n), 0, 0))
    vec = pl.BlockSpec((hb, cpb, 1, LANE), lambda h, n: (h, blk(n), 0, 0))
    st = pl.BlockSpec((hb, cpb, HEAD, HEAD), lambda h, n: (h, blk(n), 0, 0))
    return col, sq, vec, st


def _delta_scan_fwd(u, w, qd, kd, qk, gl, H, *, name):
    T = u.shape[0]
    N = T // CHUNK
    cpb = _tile(N, CPB_SCAN, 4)
    hb = min(GRP, H)
    col, sq, vec, st = _scan_specs(H, N, cpb, hb, False)
    lanes = [slice(j * LANE, (j + 1) * LANE) for j in range(hb)]
    heads = list(range(hb))

    def body(u_ref, w_ref, qd_ref, kd_ref, qk_ref, gl_ref, o_ref, vn_ref, ss_ref, s_scr):
        @pl.when(pl.program_id(1) == 0)
        def _():
            s_scr[...] = jnp.zeros_like(s_scr)

        def step(c, states):
            rows = pl.ds(pl.multiple_of(c * CHUNK, CHUNK), CHUNK)
            S = list(states)
            for j in heads:
                ss_ref[j, c] = S[j]
            wS = _each(lambda ln, s: _mm16(w_ref[rows, ln], s), lanes, S)
            qS = _each(lambda ln, s: _mm16(qd_ref[rows, ln], s), lanes, S)
            vn = _each(lambda ln, ws: u_ref[rows, ln] - ws, lanes, wS)
            o = _each(lambda j, qs, vn_: qs + _mm16(qk_ref[j, c], vn_), heads, qS, vn)
            new = _each(lambda j, ln, s, vn_: s * gl_ref[j, c] + _mm16(kd_ref[rows, ln], vn_, "tn"),
                        heads, lanes, S, vn)
            for j in heads:
                o_ref[rows, lanes[j]] = o[j]
                vn_ref[rows, lanes[j]] = vn[j]
            return tuple(new)
        out = lax.fori_loop(0, cpb, step, tuple(s_scr[j] for j in heads))
        for j in heads:
            s_scr[j] = out[j]

    full = jax.ShapeDtypeStruct((T, H * LANE), F32)
    return pl.pallas_call(
        body, name=name, grid=(H // hb, N // cpb),
        in_specs=[col] * 4 + [sq, vec],
        out_specs=[col, col, st],
        out_shape=[full, full, jax.ShapeDtypeStruct((H, N, HEAD, HEAD), F32)],
        scratch_shapes=[pltpu.VMEM((hb, HEAD, HEAD), F32)],
        compiler_params=_params(("parallel", "arbitrary")),
    )(u, w, qd, kd, qk, gl)


def _delta_scan_bwd(do, w, qd, kd, vn, qk, gl, ss, H, *, name):
    T = do.shape[0]
    N = T // CHUNK
    cpb = _tile(N, CPB_SCAN, 4)
    hb = min(GRP, H)
    col, sq, vec, st = _scan_specs(H, N, cpb, hb, True)
    lanes = [slice(j * LANE, (j + 1) * LANE) for j in range(hb)]
    heads = list(range(hb))

    def body(do_ref, w_ref, qd_ref, kd_ref, vn_ref, qk_ref, gl_ref, ss_ref,
             du_ref, dw_ref, dqd_ref, dkd_ref, dqk_ref, dgl_ref, ds_scr):
        @pl.when(pl.program_id(1) == 0)
        def _():
            ds_scr[...] = jnp.zeros_like(ds_scr)

        def step(i, dstates):
            c = cpb - 1 - i
            rows = pl.ds(pl.multiple_of(c * CHUNK, CHUNK), CHUNK)
            dS = list(dstates)
            S = [ss_ref[j, c] for j in heads]
            dov = [do_ref[rows, ln] for ln in lanes]
            vnv = [vn_ref[rows, ln] for ln in lanes]
            a1 = _each(lambda j, d_: _mm16(qk_ref[j, c], d_, "tn"), heads, dov)
            a2 = _each(lambda ln, ds: _mm16(kd_ref[rows, ln], ds), lanes, dS)
            dvn = _each(lambda x, y: x + y, a1, a2)
            dqd = _each(lambda d_, s: _mm16(d_, s, "nt"), dov, S)
            dkd = _each(lambda v_, ds: _mm16(v_, ds, "nt"), vnv, dS)
            dqk = _each(lambda d_, v_: _mm16(d_, v_, "nt"), dov, vnv)
            dw = _each(lambda dv_, s: -_mm16(dv_, s, "nt"), dvn, S)
            b1 = _each(lambda ln, d_: _mm16(qd_ref[rows, ln], d_, "tn"), lanes, dov)
            b2 = _each(lambda ln, dv_: _mm16(w_ref[rows, ln], dv_, "tn"), lanes, dvn)
            new = _each(lambda j, x, y, ds: x + ds * gl_ref[j, c] - y, heads, b1, b2, dS)
            for j in heads:
                du_ref[rows, lanes[j]] = dvn[j]
                dw_ref[rows, lanes[j]] = dw[j]
                dqd_ref[rows, lanes[j]] = dqd[j]
                dkd_ref[rows, lanes[j]] = dkd[j]
                dqk_ref[j, c] = dqk[j]
                dgl = jnp.sum(jnp.sum(dS[j] * S[j], axis=1, keepdims=True), axis=0, keepdims=True)
                dgl_ref[j, c] = jnp.broadcast_to(dgl, (1, LANE))
            return tuple(new)
        out = lax.fori_loop(0, cpb, step, tuple(ds_scr[j] for j in heads))
        for j in heads:
            ds_scr[j] = out[j]

    full = jax.ShapeDtypeStruct((T, H * LANE), F32)
    return pl.pallas_call(
        body, name=name, grid=(H // hb, N // cpb),
        in_specs=[col] * 5 + [sq, vec, st],
        out_specs=[col] * 4 + [sq, vec],
        out_shape=[full] * 4 + [jax.ShapeDtypeStruct((H, N, CHUNK, CHUNK), F32), jax.ShapeDtypeStruct((H, N, 1, LANE), F32)],
        scratch_shapes=[pltpu.VMEM((hb, HEAD, HEAD), F32)],
        compiler_params=_params(("parallel", "arbitrary")),
    )(do, w, qd, kd, vn, qk, gl, ss)


def _delta_prep_bwd(qkv, gamB, bB, ti, u, w, qk, du, dw, dqd, dkd, dqk, dgl, H, *, name):
    T = qkv.shape[0]
    N = T // CHUNK
    cpb = _tile(N, CPB, 8)
    grp = min(GRP, cpb)
    col, bc, sq, vec = _delta_specs(T, H, cpb)

    def body(q_ref, k_ref, v_ref, g_ref, b_ref, ti_ref, u_ref, w_ref, qk_ref,
             du_ref, dw_ref, dqd_ref, dkd_ref, dqk_ref, dgl_ref,
             dq_ref, dk_ref, dv_ref, dg_ref, db_ref):
        ones = jnp.ones((CHUNK, LANE), F32)
        strict = _tri(strict=True)
        last = lax.broadcasted_iota(jnp.int32, (CHUNK, LANE), 0) == CHUNK - 1
        lsum = lambda x: jnp.sum(x, axis=-1, keepdims=True)
        for c0 in range(0, cpb, grp):
            cs = list(range(c0, c0 + grp))
            rows = [slice(c * CHUNK, (c + 1) * CHUNK) for c in cs]
            ld = lambda r_: [r_[r, :] for r in rows]
            q, k, v, uv, wv, duv, dwv, dqd_v, dkd_v = (ld(r_) for r_ in (q_ref, k_ref, v_ref, u_ref, w_ref, du_ref, dw_ref, dqd_ref, dkd_ref))
            bb = [b_ref[0, r, :] for r in rows]
            gam = [g_ref[0, r, :] for r in rows]
            Ti = [ti_ref[0, c] for c in cs]
            QK = [qk_ref[0, c] for c in cs]
            dqk_v = [dqk_ref[0, c] for c in cs]
            D = _each(_decay, gam)
            e = _each(jnp.exp, gam)
            glast = [g_[CHUNK - 1:CHUNK, :] for g_ in gam]
            eL = _each(lambda gl_, g_: jnp.exp(gl_ - g_), glast, gam)
            kk = _each(lambda k_: _mm16(k_, k_, "nt"), k)
            KKD = _each(lambda kk_, D_: jnp.where(strict, kk_ * D_, 0.0), kk, D)
            dru = _each(lambda t, d_: _mm(t, d_, "tn"), Ti, duv)
            drw = _each(lambda t, d_: _mm(t, d_, "tn"), Ti, dwv)
            l1 = _each(lambda a, b: _mm(a, b, "nt"), dru, uv)
            l2 = _each(lambda a, b: _mm(a, b, "nt"), drw, wv)
            dL = _each(lambda a, b: jnp.where(strict, -(a + b), 0.0), l1, l2)
            Mm = _each(lambda dl, b_: dl * b_[:, :CHUNK], dL, bb)
            dKK = _each(lambda m_, D_: m_ * D_, Mm, D)
            dQK = _each(lambda a, D_: a * D_, dqk_v, D)
            P = _each(lambda m_, kkd, a, qk_: m_ * kkd + a * qk_, Mm, KKD, dqk_v, QK)
            q1 = _each(lambda a, k_: _mm16(a, k_), dQK, k)
            k1 = _each(lambda a, q_: _mm16(a, q_, "tn"), dQK, q)
            k2 = _each(lambda a, k_: _mm16(a, k_), dKK, k)
            k3 = _each(lambda a, k_: _mm16(a, k_, "tn"), dKK, k)
            s1 = _each(lambda dl, kkd: _mm(dl * kkd, ones), dL, KKD)
            p1 = _each(lambda p_: _mm(p_, ones), P)
            p2 = _each(lambda p_: _mm(p_, ones, "tn"), P)
            for i, c in enumerate(cs):
                r = rows[i]
                bek = bb[i] * e[i]
                kdv = eL[i] * k[i]
                dq_ref[r, :] = q1[i] + e[i] * dqd_v[i]
                dk_ref[r, :] = k1[i] + k2[i] + k3[i] + bek * drw[i] + eL[i] * dkd_v[i]
                dv_ref[r, :] = bb[i] * dru[i]
                db_ref[0, r, :] = s1[i] + lsum(dru[i] * v[i]) + lsum(drw[i] * e[i] * k[i])
                dgam = (p1[i] - p2[i] + lsum(drw[i] * bek * k[i]) + lsum(dqd_v[i] * e[i] * q[i])
                        - lsum(dkd_v[i] * kdv))
                xlast = jnp.sum(lsum(dkd_v[i] * kdv), axis=0, keepdims=True) + jnp.exp(glast[i]) * dgl_ref[0, c]
                dg_ref[0, r, :] = dgam + jnp.where(last, xlast, 0.0)

    full = jax.ShapeDtypeStruct((T, H * LANE), F32)
    bcs = jax.ShapeDtypeStruct((H, T, LANE), F32)
    return pl.pallas_call(
        body, name=name, grid=(H, N // cpb),
        in_specs=[col(0), col(H), col(2 * H), bc, bc, sq, col(0), col(0), sq, col(0), col(0), col(0), col(0), sq, vec],
        out_specs=[col(0), col(0), col(0), bc, bc],
        out_shape=[full, full, full, bcs, bcs],
        compiler_params=_params(("parallel", "parallel")),
    )(qkv, qkv, qkv, gamB, bB, ti, u, w, qk, du, dw, dqd, dkd, dqk, dgl)


def _adam(parts, w, m, v, *, name, own=None, me=None):
    P, R, C = parts.shape
    if R > 256 and R % 8:
        tr, tc = R, _tile(C, 256)
    else:
        tr, tc = _tile(R, 256, 8), C
    n_own = 0 if own is None else 2

    def body(*refs):
        p_ref, w_ref, m_ref, v_ref, g_ref, d_ref, nm_ref, nv_ref = refs[n_own:]
        g = None
        for i in range(P):
            t = p_ref[i].astype(F32)
            if n_own:
                t = jnp.where(refs[0][0] == i, refs[1][...].astype(F32), t)
            g = t if g is None else g + t
        mn = ADAM_B1 * m_ref[...] + (1.0 - ADAM_B1) * g
        vn = ADAM_B2 * v_ref[...] + (1.0 - ADAM_B2) * (g * g)
        m_hat = mn / (1.0 - ADAM_B1 ** ADAM_STEP)
        v_hat = vn / (1.0 - ADAM_B2 ** ADAM_STEP)
        g_ref[...] = g
        d_ref[...] = -ADAM_LR * (m_hat / (jnp.sqrt(v_hat) + ADAM_EPS) + ADAM_WD * w_ref[...])
        nm_ref[...] = mn
        nv_ref[...] = vn

    blk = pl.BlockSpec((tr, tc), lambda i, j: (i, j))
    return pl.pallas_call(
        body, name=name, grid=(R // tr, C // tc),
        in_specs=[pl.BlockSpec(memory_space=pltpu.SMEM), blk][:n_own] + [pl.BlockSpec((P, tr, tc), lambda i, j: (0, i, j)), blk, blk, blk],
        out_specs=[blk] * 4, out_shape=[jax.ShapeDtypeStruct((R, C), F32)] * 4,
        compiler_params=_params(("parallel", "parallel")),
    )(*([me, own] if n_own else []), parts, w, m, v)


def _mesh_pos():
    return lax.axis_index("x"), lax.axis_index("y"), lax.axis_index("c")


def _peer(k):
    x, y, c = _mesh_pos()
    px, py, pc = x ^ ((k >> 2) & 1), y ^ ((k >> 1) & 1), c ^ (k & 1)
    return (px, py, pc), 4 * px + 2 * py + pc


def _exchange(arrays, scatter, *, name, after=None):
    n = len(arrays)
    n_in = n if after is None else n + 1
    blocks = [a.shape[1:] if scatter else a.shape for a in arrays]

    def body(*refs):
        srcs, dsts = refs[:n], refs[n_in:n_in + n]
        send_sems, recv_sems, local_sems = refs[n_in + n:]
        x, y, c = _mesh_pos()
        me = 4 * x + 2 * y + c
        local, sends = [], []
        for a in range(n):
            cp = pltpu.make_async_copy(srcs[a].at[me] if scatter else srcs[a], dsts[a].at[me], local_sems.at[a])
            cp.start()
            local.append(cp)
            for k in range(1, N_DEV):
                dev, idx = _peer(k)
                cp = pltpu.make_async_remote_copy(
                    src_ref=srcs[a].at[idx] if scatter else srcs[a], dst_ref=dsts[a].at[me],
                    send_sem=send_sems.at[a * N_DEV + k], recv_sem=recv_sems.at[a * N_DEV + k],
                    device_id=dev, device_id_type=MESH)
                cp.start()
                sends.append(cp)
        for a in range(n):
            for k in range(1, N_DEV):
                dev, idx = _peer(k)
                pltpu.make_async_remote_copy(
                    src_ref=srcs[a].at[idx] if scatter else srcs[a], dst_ref=dsts[a].at[idx],
                    send_sem=send_sems.at[a * N_DEV + k], recv_sem=recv_sems.at[a * N_DEV + k],
                    device_id=dev, device_id_type=MESH).wait_recv()
        for cp in sends:
            cp.wait_send()
        for cp in local:
            cp.wait()

    anyspec = pl.BlockSpec(memory_space=pl.ANY)
    return pl.pallas_call(
        body, name=name, in_specs=[anyspec] * n_in, out_specs=[anyspec] * n,
        out_shape=[jax.ShapeDtypeStruct((N_DEV,) + tuple(b), a.dtype) for a, b in zip(arrays, blocks)],
        scratch_shapes=[pltpu.SemaphoreType.DMA((n * N_DEV,)), pltpu.SemaphoreType.DMA((n * N_DEV,)),
                        pltpu.SemaphoreType.DMA((n,))],
    )(*arrays, *([] if after is None else [after]))


_ANY = pl.BlockSpec(memory_space=pl.ANY)
_SEM = pl.BlockSpec(memory_space=pltpu.SEMAPHORE)
_EFFECT = pltpu.SideEffectType.DATAFLOW_SIDE_EFFECTING


def _in_hbm(a):
    return pltpu.with_memory_space_constraint(a, pltpu.HBM)


def _split_copy(src, land, send, recv, k, me, scatter, landed):
    dev, idx = _peer(k)
    return pltpu.make_async_remote_copy(
        src_ref=src.at[idx] if scatter else src, dst_ref=land.at[idx if landed else me],
        send_sem=send.at[k], recv_sem=recv.at[k], device_id=dev, device_id_type=MESH)


ALL_PEERS = tuple(range(1, N_DEV))
SIBLING = 1
SAME_CORE = (2, 4, 6)


def _split_start(srcs, lands, scatter, *, name, relations=None):
    n = len(srcs)
    relations = relations or [ALL_PEERS] * n

    def body(*refs):
        src, land, send, recv, token = refs[:n], refs[n:2 * n], refs[2 * n:3 * n], refs[3 * n:4 * n], refs[-1]
        x, y, c = _mesh_pos()
        me = 4 * x + 2 * y + c
        for a in range(n):
            for k in relations[a]:
                _split_copy(src[a], land[a], send[a], recv[a], k, me, scatter, False).start()
        token[...] = jnp.zeros_like(token)

    outs = pl.pallas_call(
        body, name=name,
        out_shape=[pltpu.SemaphoreType.DMA((N_DEV,))] * (2 * n) + [pltpu.HBM(t.shape, t.dtype) for t in list(srcs) + list(lands)]
        + [jax.ShapeDtypeStruct((8, LANE), F32)],
        in_specs=[_ANY] * (2 * n), out_specs=[_SEM] * (2 * n) + [_ANY] * (2 * n) + [pl.BlockSpec(memory_space=pltpu.VMEM)],
        input_output_aliases={i: 2 * n + i for i in range(2 * n)},
        compiler_params=pltpu.CompilerParams(has_side_effects=_EFFECT),
    )(*[_in_hbm(t) for t in list(srcs) + list(lands)])
    handles = [(outs[a], outs[n + a], outs[2 * n + a], outs[3 * n + a]) for a in range(n)]
    return handles, outs[-1]


def _split_wait(handle, after, scatter, *, name):
    send, recv, src_thru, land_thru = handle

    def body(src_ref, land_ref, send_ref, recv_ref, after_ref, src_out, land_out):
        x, y, c = _mesh_pos()
        me = 4 * x + 2 * y + c
        for k in range(1, N_DEV):
            cp = _split_copy(src_ref, land_ref, send_ref, recv_ref, k, me, scatter, True)
            cp.wait_send()
            cp.wait_recv()

    return pl.pallas_call(
        body, name=name,
        out_shape=(pltpu.HBM(src_thru.shape, src_thru.dtype), pltpu.HBM(land_thru.shape, land_thru.dtype)),
        in_specs=(_ANY, _ANY, _SEM, _SEM, _ANY), out_specs=(_ANY, _ANY), input_output_aliases={0: 0, 1: 1},
        compiler_params=pltpu.CompilerParams(has_side_effects=_EFFECT),
    )(src_thru, land_thru, send, recv, after)[1]


def _forward_copy(land, fsend, frecv, k, landed):
    x, y, c = _mesh_pos()
    _, idx = _peer(k | SIBLING if landed else k)
    return pltpu.make_async_remote_copy(src_ref=land.at[idx], dst_ref=land.at[idx], send_sem=fsend.at[k],
                                        recv_sem=frecv.at[k], device_id=(x, y, 1 - c), device_id_type=MESH)


def _gather_forward(handle, after, *, name):
    send, recv, src_thru, land_thru = handle

    def body(src_ref, land_ref, send_ref, recv_ref, after_ref, src_out, land_out, fsend, frecv):
        x, y, c = _mesh_pos()
        me = 4 * x + 2 * y + c
        for k in SAME_CORE:
            _split_copy(src_ref, land_ref, send_ref, recv_ref, k, me, False, True).wait_recv()
            _forward_copy(land_ref, fsend, frecv, k, False).start()

    src2, land2, fsend, frecv = pl.pallas_call(
        body, name=name,
        out_shape=(pltpu.HBM(src_thru.shape, src_thru.dtype), pltpu.HBM(land_thru.shape, land_thru.dtype),
                   pltpu.SemaphoreType.DMA((N_DEV,)), pltpu.SemaphoreType.DMA((N_DEV,))),
        in_specs=(_ANY, _ANY, _SEM, _SEM, _ANY), out_specs=(_ANY, _ANY, _SEM, _SEM), input_output_aliases={0: 0, 1: 1},
        compiler_params=pltpu.CompilerParams(has_side_effects=_EFFECT),
    )(src_thru, land_thru, send, recv, after)
    return (send, recv, src2, land2), (fsend, frecv)


def _gather_wait_two_level(handle, fwd, *, name):
    send, recv, src_thru, land_thru = handle
    fsend, frecv = fwd

    def body(src_ref, land_ref, send_ref, recv_ref, fsend_ref, frecv_ref, src_out, land_out):
        x, y, c = _mesh_pos()
        me = 4 * x + 2 * y + c
        for k in (SIBLING,) + SAME_CORE:
            _split_copy(src_ref, land_ref, send_ref, recv_ref, k, me, False, True).wait_send()
        _split_copy(src_ref, land_ref, send_ref, recv_ref, SIBLING, me, False, True).wait_recv()
        for k in SAME_CORE:
            _forward_copy(land_ref, fsend_ref, frecv_ref, k, False).wait_send()
            _forward_copy(land_ref, fsend_ref, frecv_ref, k, True).wait_recv()

    return pl.pallas_call(
        body, name=name,
        out_shape=(pltpu.HBM(src_thru.shape, src_thru.dtype), pltpu.HBM(land_thru.shape, land_thru.dtype)),
        in_specs=(_ANY, _ANY, _SEM, _SEM, _SEM, _SEM), out_specs=(_ANY, _ANY), input_output_aliases={0: 0, 1: 1},
        compiler_params=pltpu.CompilerParams(has_side_effects=_EFFECT),
    )(src_thru, land_thru, send, recv, fsend, frecv)[1]


def _local_step(x, p, tgt, S, wt, conv, emit):
    T, D = x.shape
    CW = DNW = D // 2
    H = DNW // HEAD
    nA, nD = CW // LANE, DNW // LANE
    qkv_off, z_off, ab_off = 3 * nA, 3 * nA + 3 * nD, 3 * nA + 4 * nD
    alog = jnp.pad(S["a_log"], ((0, 0), (0, LANE - H)))
    dtb = jnp.pad(S["dt_bias"], ((0, 0), (0, LANE - H)))

    h1 = _rms_fwd(x, S["g_mix"], name="rms1_fwd")
    pp = _matmul(p, wt("w_pp", h1), "nn", name="mm_pp", b_shards=True)
    w_in, cv = wt("w_in", pp), conv(pp)
    proj = _matmul(h1, w_in, "nt", name="mm_in", tn=1536)
    y_a = _group_a_fwd(proj, cv["conv_a"], CW, D, name="group_a_fwd")
    qkv = _qkv_fwd(proj, cv["conv_qkv"], qkv_off, H, name="qkv_fwd")
    gamB, bB = _gates_fwd(proj, alog, dtb, ab_off, H, name="gates_fwd")
    u, w, qd, kd, qk, ti, gl = _delta_prep_fwd(qkv, gamB, bB, H, name="delta_prep_fwd")
    o, vn, ss = _delta_scan_fwd(u, w, qd, kd, qk, gl, H, name="delta_scan_fwd")
    ycat = _gated_norm_fwd(o, proj, S["dn_g"], z_off, y_a, name="gated_norm_fwd")
    w_out = wt("w_out", ycat)
    rows = dict(tm=ROW_TILE, tn=D)
    x1, h2 = _matmul(ycat, w_out, "nn", name="mm_out", out_dtypes=(F32, BF16), epilogue=_epi_residual_rms,
                     extras=(x,), vec_extras=(S["g_ffn"],), **rows)
    w_up = wt("w_up", h2)
    up_pre = _matmul(h2, w_up, "nn", name="mm_up", b_shards=True, tn=SHARD_TILE, out_lanes=True)
    act = _ffn_act_fwd(up_pre, cv["conv_ffn"], name="ffn_act_fwd")
    w_down = wt("w_down", act)
    x2 = _matmul(act, w_down, "nn", name="mm_down", epilogue=lambda acc, r: (acc + r,), extras=(x1,), tk=LONG_K)
    h3 = _rms_fwd(x2, S["g_ple"], name="rms3_fwd")
    w_pg = wt("w_pg", h3)

    def ple_epi(acc, x2r, ppr):
        s = jax.nn.sigmoid(acc)
        return x2r + s * ppr, s

    x3, sg = _matmul(h3, w_pg, "nn", name="mm_pg", out_dtypes=(F32, F32), epilogue=ple_epi, extras=(x2, pp), tm=256, tn=D)
    dx3, dg_final, loss, dpg, dpp = _final_loss(x3, S["g_final"], tgt, pp, sg, name="final_loss")

    G = {"g_final": dg_final}
    tok = emit({"w_pp": _matmul(p, dpp, "tn", name="mm_dwpp", out_dtypes=(BF16,), out_shards=True, tk=LONG_K),
                "w_pg": _matmul(h3, dpg, "tn", name="mm_dwpg", out_dtypes=(BF16,), tk=LONG_K)})
    bwd = dict(out_dtypes=(F32, BF16), epilogue=_epi_rms_bwd(2), n_vec=1, **rows)
    dx2, dx2b, G["g_ple"] = _matmul(dpg, w_pg, "nt", name="mm_dh3", after=tok, extras=(x2, dx3),
                                    vec_extras=(S["g_ple"],), **bwd)
    tok = emit({"w_down": _matmul(act, dx2b, "tn", name="mm_dwdown", out_dtypes=(BF16,), tk=LONG_K)})
    dact = _matmul(dx2b, w_down, "nt", name="mm_dact", after=tok, tn=SHARD_TILE)
    dup, dcf_g, dcf_v = _ffn_act_bwd(up_pre, cv["conv_ffn"], dact, name="ffn_act_bwd")
    G["conv_ffn"] = jnp.concatenate([dcf_g, dcf_v], axis=1)
    tok = emit({"w_up": _matmul(h2, dup, "tn", name="mm_dwup", out_dtypes=(BF16,), b_shards=True, out_shards=True,
                                tn=SHARD_TILE, tk=LONG_K)})
    dh2 = _matmul(dup, w_up, "nt", name="mm_dh2", after=tok, a_shards=True, b_shards=True, tk=2 * SHARD_TILE)
    dx1, dx1b, G["g_ffn"] = _rms_bwd(x1, S["g_ffn"], dh2, dx2, name="rms2_bwd")
    tok = emit({"w_out": _matmul(ycat, dx1b, "tn", name="mm_dwout", out_dtypes=(BF16,), tk=LONG_K)})
    dycat = _matmul(dx1b, w_out, "nt", name="mm_dycat", after=tok)
    do, dz, G["dn_g"] = _gated_norm_bwd(o, proj, S["dn_g"], dycat, z_off, nA, name="gated_norm_bwd")
    du, dw, dqd, dkd, dqk, dgl = _delta_scan_bwd(do, w, qd, kd, vn, qk, gl, ss, H, name="delta_scan_bwd")
    dq, dk, dv, dgB, dbB = _delta_prep_bwd(qkv, gamB, bB, ti, u, w, qk, du, dw, dqd, dkd, dqk, dgl, H,
                                           name="delta_prep_bwd")
    dab, dal, ddt = _gates_bwd(proj, alog, dtb, dgB, dbB, ab_off, H, name="gates_bwd")
    G["a_log"], G["dt_bias"] = dal[:, :H], ddt[:, :H]
    dax, dab_, dac, G["conv_a"] = _group_a_bwd(proj, cv["conv_a"], dycat, CW, name="group_a_bwd")
    in_p = w_in.shape[0]
    assert dz.shape == (T, in_p), dz.shape
    dproj = dz
    pieces_at = [(dax, 0), (dab_, CW), (dac, 2 * CW), (dab, ab_off * LANE)]
    if in_p > (ab_off + 1) * LANE:
        pieces_at.append((jnp.zeros((T, in_p - (ab_off + 1) * LANE), BF16), (ab_off + 1) * LANE))
    for piece, c0 in pieces_at:
        dproj = lax.dynamic_update_slice(dproj, piece, (0, c0))
    dproj, G["conv_qkv"] = _qkv_bwd(proj, cv["conv_qkv"], dq, dk, dv, qkv_off, H, dproj, name="qkv_bwd")
    tok = emit({"w_in": _matmul(dproj, h1, "tn", name="mm_dwin", out_dtypes=(BF16,), tk=LONG_K)})
    dh1 = _matmul(dproj, w_in, "nn", name="mm_dh1", after=tok, tk=LONG_K)
    grad_x, _, G["g_mix"] = _rms_bwd(x, S["g_mix"], dh1, dx1, name="rms1_bwd")
    return loss, grad_x, G


def _col_sharded(landed):
    _, R, C = landed.shape
    return jnp.transpose(landed, (1, 0, 2)).reshape(R, N_DEV * C)


def kernel(x, p, norm_mix_g, w_in, conv_a_w, conv_qkv_w, a_log, dt_bias, dn_norm_g, w_out, norm_ffn_g, w_up, conv_ffn_w, w_down, norm_ple_g, w_ple_gate, w_ple_proj, final_norm_g, loss_target, m_norm_mix_g, m_w_in, m_conv_a_w, m_conv_qkv_w, m_a_log, m_dt_bias, m_dn_norm_g, m_w_out, m_norm_ffn_g, m_w_up, m_conv_ffn_w, m_w_down, m_norm_ple_g, m_w_ple_gate, m_w_ple_proj, m_final_norm_g, v_norm_mix_g, v_w_in, v_conv_a_w, v_conv_qkv_w, v_a_log, v_dt_bias, v_dn_norm_g, v_w_out, v_norm_ffn_g, v_w_up, v_conv_ffn_w, v_w_down, v_norm_ple_g, v_w_ple_gate, v_w_ple_proj, v_final_norm_g):
    T, D = x.shape[1], x.shape[2]
    xd, _, cd = _mesh_pos()
    me = 4 * xd + 2 * lax.axis_index("y") + cd

    conv_sh = [conv_a_w[0], conv_qkv_w[0], conv_ffn_w[0]]
    conv_n = [c.size for c in conv_sh]
    pack_rows = -(-sum(conv_n) // LANE)
    conv_pack = jnp.pad(jnp.concatenate([c.reshape(-1) for c in conv_sh]), (0, pack_rows * LANE - sum(conv_n))).reshape(pack_rows, LANE)
    names = ["w_pp", "w_in", "conv", "w_out", "w_up", "w_down", "w_pg"]
    tr_ = lambda t: jnp.swapaxes(t, 1, 2)
    shards = [w_ple_proj[0].astype(BF16), w_in[0].T.astype(BF16), conv_pack, w_out[0].astype(BF16), w_up[0].astype(BF16),
              w_down[0].astype(BF16), w_ple_gate[0].astype(BF16)]
    empty_slots = lambda blocks: [lax.empty((N_DEV,) + tuple(b.shape), b.dtype) for b in blocks]
    handles, tok0 = _split_start(shards, empty_slots(shards), False, name="gather_start",
                                 relations=[(SIBLING,) + SAME_CORE if nm == "w_in" else ALL_PEERS for nm in names])
    handle = dict(zip(names, handles))
    own = dict(zip(names, shards))
    in_cols = N_DEV * w_in.shape[2]
    in_p = (in_cols // LANE) * LANE + AB_PAD
    in_place = {"w_up", "w_pp"}

    def gathered(name, after):
        if name == "w_in":
            passed, fwd = _gather_forward(handle[name], after, name="gather_forward_w_in")
            landed = _gather_wait_two_level(passed, fwd, name="gather_wait_w_in")
        else:
            landed = _split_wait(handle[name], after, False, name="gather_wait_" + name)
        return lax.dynamic_update_index_in_dim(landed, own[name], me, 0)

    def wt(name, after):
        landed = gathered(name, after)
        if name in in_place:
            return landed
        full = landed.reshape(-1, D)
        return jnp.pad(full, ((0, in_p - in_cols), (0, 0))) if name == "w_in" else full

    def conv(after):
        flat = gathered("conv", after).reshape(N_DEV, pack_rows * LANE)
        out, o_ = {}, 0
        for nm, c, n_ in zip(("conv_a", "conv_qkv", "conv_ffn"), conv_sh, conv_n):
            out[nm] = _col_sharded(flat[:, o_:o_ + n_].reshape((N_DEV,) + c.shape))
            o_ += n_
        return out

    pending, mine = {}, {}

    def emit(grads):
        parts = [g if nm in in_place else (g[:in_cols] if nm == "w_in" else g).reshape(N_DEV, -1, D)
                 for nm, g in grads.items()]
        hs, tok = _split_start(parts, empty_slots([q[0] for q in parts]), True, name="scatter_start_" + "_".join(grads))
        pending.update(zip(grads, hs))
        mine.update({nm: lax.dynamic_index_in_dim(q, me, 0, keepdims=False) for nm, q in zip(grads, parts)})
        return tok

    S = {
        "g_mix": norm_mix_g + tok0[0, 0], "a_log": a_log, "dt_bias": dt_bias, "dn_g": dn_norm_g, "g_ffn": norm_ffn_g,
        "g_ple": norm_ple_g, "g_final": final_norm_g.reshape(1, D),
    }

    loss_v, grad_x, G = _local_step(x[0], p[0, 0], loss_target[0], S, wt, conv, emit)
    loss = lax.psum(loss_v[0, 0], ("x", "y", "c"))

    small_names = ["g_mix", "g_ffn", "g_ple", "g_final", "dn_g", "a_log", "dt_bias", "conv_a", "conv_qkv", "conv_ffn"]
    small_rows, pieces = [], []
    for nm in small_names:
        g_ = G[nm].reshape(-1)
        r_ = -(-g_.size // (8 * LANE)) * 8
        small_rows.append(r_)
        pieces.append(jnp.pad(g_, (0, r_ * LANE - g_.size)).reshape(r_, LANE))
    landed = {nm: _split_wait(h_, grad_x, True, name="scatter_wait_" + nm) for nm, h_ in pending.items() if nm != "w_in"}

    def adam(parts, w_, m_, v_, nm, own_=None):
        shp = w_.shape
        w2, m2, v2 = (t.reshape(parts.shape[1:]) for t in (w_, m_, v_))
        kw = {} if own_ is None else {"own": own_, "me": me.astype(jnp.int32).reshape(1)}
        return tuple(t.reshape(shp) for t in _adam(parts, w2, m2, v2, name="adam_" + nm, **kw))

    big = {
        "w_up": adam(landed["w_up"], w_up, m_w_up, v_w_up, "w_up", mine["w_up"]),
        "w_down": adam(landed["w_down"], w_down, m_w_down, v_w_down, "w_down", mine["w_down"]),
        "w_out": adam(landed["w_out"], w_out, m_w_out, v_w_out, "w_out", mine["w_out"]),
        "w_pg": adam(landed["w_pg"], w_ple_gate, m_w_ple_gate, v_w_ple_gate, "w_ple_gate", mine["w_pg"]),
        "w_pp": adam(landed["w_pp"], w_ple_proj, m_w_ple_proj, v_w_ple_proj, "w_ple_proj", mine["w_pp"]),
    }
    first = lambda t: lax.slice(t, (0,) * t.ndim, (1,) * t.ndim).reshape(1)
    big_done = sum(first(r[1]) for r in big.values())
    (small_l,) = _exchange([jnp.concatenate(pieces, axis=0)], False, name="gather_small_grads", after=big_done)

    def small_parts(nm):
        i = small_names.index(nm)
        r0 = sum(small_rows[:i])
        shp = G[nm].shape
        return small_l[:, r0:r0 + small_rows[i], :].reshape(N_DEV, -1)[:, :G[nm].size].reshape((N_DEV,) + shp)

    def conv_parts(nm, shard):
        full = small_parts(nm)
        C = shard.shape[-1]
        return lax.dynamic_slice_in_dim(full, me * C, C, axis=2)

    res = [
        adam(small_parts("g_mix"), norm_mix_g, m_norm_mix_g, v_norm_mix_g, "norm_mix_g"),
        None,
        adam(conv_parts("conv_a", conv_a_w), conv_a_w, m_conv_a_w, v_conv_a_w, "conv_a_w"),
        adam(conv_parts("conv_qkv", conv_qkv_w), conv_qkv_w, m_conv_qkv_w, v_conv_qkv_w, "conv_qkv_w"),
        adam(small_parts("a_log"), a_log, m_a_log, v_a_log, "a_log"),
        adam(small_parts("dt_bias"), dt_bias, m_dt_bias, v_dt_bias, "dt_bias"),
        adam(small_parts("dn_g"), dn_norm_g, m_dn_norm_g, v_dn_norm_g, "dn_norm_g"),
        big["w_out"],
        adam(small_parts("g_ffn"), norm_ffn_g, m_norm_ffn_g, v_norm_ffn_g, "norm_ffn_g"),
        big["w_up"],
        adam(conv_parts("conv_ffn", conv_ffn_w), conv_ffn_w, m_conv_ffn_w, v_conv_ffn_w, "conv_ffn_w"),
        big["w_down"],
        adam(small_parts("g_ple"), norm_ple_g, m_norm_ple_g, v_norm_ple_g, "norm_ple_g"),
        big["w_pg"],
        big["w_pp"],
        adam(small_parts("g_final"), final_norm_g.reshape(1, D), m_final_norm_g.reshape(1, D),
             v_final_norm_g.reshape(1, D), "final_norm_g"),
    ]
    res[-1] = tuple(t.reshape(D) for t in res[-1])
    landed_in = _split_wait(pending["w_in"], res[10][1], True, name="scatter_wait_w_in")
    res[1] = tuple(tr_(t) for t in adam(landed_in, tr_(w_in), tr_(m_w_in), tr_(v_w_in), "w_in", mine["w_in"]))
    grads, deltas, new_m, new_v = zip(*res)
    return (loss, grad_x[None], *grads, *deltas, *new_m, *new_v)
```
